```python
import jax, jax.numpy as jnp
from jax import lax
import numpy as np

D_MODEL = 1024
BATCH = 8
SEQ = 8192
DEPTH = 1

D_CONV = D_MODEL
CONV_GROUPS = 8
CONV_A_WIDTH = 3
D_RNN = D_MODEL
RNN_HEADS = 4
RNN_BLOCK = D_RNN // RNN_HEADS
CONV_B_WIDTH = 4
LRU_C = 8.0
D_FF = ((8 * D_MODEL + 3 * 256 - 1) // (3 * 256)) * 256
N_MOD = 6
EPS = 1e-6
IN_WIDTHS = (D_CONV, D_CONV, D_CONV, D_RNN, D_RNN, D_MODEL, D_MODEL)
IN_TOTAL = sum(IN_WIDTHS)
IN_SPLITS = tuple(int(v) for v in np.cumsum(IN_WIDTHS)[:-1])

kernel_name = "hybrid_conv_rglru_gated_merge_adaln"


def rmsnorm(x, g):
    xf = x.astype(jnp.float32)
    y = xf * lax.rsqrt(jnp.mean(xf * xf, axis=-1, keepdims=True) + EPS) * g.astype(jnp.float32)
    return y.astype(x.dtype)


def modulate(h, shift, scale):
    return h * (1.0 + scale[:, None, :]) + shift[:, None, :]


def causal_depthwise_conv(u, w):
    k, ch = w.shape
    return lax.conv_general_dilated(
        u, w[:, None, :].astype(u.dtype), window_strides=(1,), padding=[(k - 1, 0)],
        dimension_numbers=("NWC", "WIO", "NWC"), feature_group_count=ch)


def block_diag_linear(u, w, b):
    bs, s, d = u.shape
    uh = u.reshape(bs, s, RNN_HEADS, RNN_BLOCK)
    return jnp.einsum("bshi,hij->bshj", uh, w).reshape(bs, s, d) + b


def rg_lru(u, w_a, b_a, w_x, b_x, lam):
    r = jax.nn.sigmoid(block_diag_linear(u, w_a, b_a)).astype(jnp.float32)
    i = jax.nn.sigmoid(block_diag_linear(u, w_x, b_x))
    log_a = LRU_C * r * jax.nn.log_sigmoid(lam.astype(jnp.float32))
    a = jnp.exp(log_a)
    mult = jnp.sqrt(jnp.maximum(-jnp.expm1(2.0 * log_a), 0.0))
    mult = mult.at[:, 0].set(1.0)
    bx = mult * (i * u).astype(jnp.float32)

    def combine(left, right):
        a1, b1 = left
        a2, b2 = right
        return a1 * a2, a2 * b1 + b2

    _, h = lax.associative_scan(combine, (a, bx), axis=1)
    return h.astype(u.dtype)


def _fwd_setup_inputs(seed: int = 0) -> dict:
    key = jax.random.key(seed)
    ks = jax.random.split(key, 20)
    f32 = jnp.float32
    nrm = lambda k, shape, s: jax.random.normal(k, shape, f32) * s
    a8 = jax.random.uniform(ks[11], (DEPTH, D_RNN), f32, 0.9, 0.999)
    base = a8 ** (1.0 / LRU_C)
    lru_lambda = jnp.log(base) - jnp.log1p(-base)
    return {
        "x": nrm(ks[0], (BATCH, SEQ, D_MODEL), 1.0),
        "c": nrm(ks[1], (BATCH, D_MODEL), 1.0),
        "w_ada": nrm(ks[2], (DEPTH, D_MODEL, N_MOD * D_MODEL), 0.5 * D_MODEL ** -0.5),
        "b_ada": nrm(ks[3], (DEPTH, N_MOD * D_MODEL), 0.01),
        "g_norm_mix": 1.0 + nrm(ks[4], (DEPTH, D_MODEL), 0.05),
        "w_in": nrm(ks[5], (DEPTH, D_MODEL, IN_TOTAL), D_MODEL ** -0.5),
        "conv_a_w": nrm(ks[6], (DEPTH, CONV_A_WIDTH, D_CONV), CONV_A_WIDTH ** -0.5),
        "conv_b_w": nrm(ks[7], (DEPTH, CONV_B_WIDTH, D_RNN), CONV_B_WIDTH ** -0.5),
        "conv_b_bias": nrm(ks[8], (DEPTH, D_RNN), 0.01),
        "w_rg_a": nrm(ks[9], (DEPTH, RNN_HEADS, RNN_BLOCK, RNN_BLOCK), RNN_BLOCK ** -0.5),
        "b_rg_a": nrm(ks[10], (DEPTH, D_RNN), 0.01),
        "w_rg_x": nrm(ks[12], (DEPTH, RNN_HEADS, RNN_BLOCK, RNN_BLOCK), RNN_BLOCK ** -0.5),
        "b_rg_x": nrm(ks[13], (DEPTH, D_RNN), 0.01),
        "lru_lambda": lru_lambda,
        "w_out": nrm(ks[14], (DEPTH, D_MODEL, D_MODEL), D_MODEL ** -0.5),
        "g_norm_ffn": 1.0 + nrm(ks[15], (DEPTH, D_MODEL), 0.05),
        "w_gate_up": nrm(ks[16], (DEPTH, D_MODEL, 2 * D_FF), D_MODEL ** -0.5),
        "w_down": nrm(ks[17], (DEPTH, D_FF, D_MODEL), D_FF ** -0.5),
        "g_norm_final": 1.0 + nrm(ks[18], (D_MODEL,), 0.05),
    }


def _fwd_reference(x, c, w_ada, b_ada, g_norm_mix, w_in, conv_a_w, conv_b_w, conv_b_bias,
              w_rg_a, b_rg_a, w_rg_x, b_rg_x, lru_lambda, w_out, g_norm_ffn,
              w_gate_up, w_down, g_norm_final):
    c_act = jax.nn.silu(c)
    for l in range(DEPTH):
        mod = c_act @ w_ada[l] + b_ada[l]
        sh1, sc1, gt1, sh2, sc2, gt2 = jnp.split(mod, N_MOD, axis=-1)

        h = modulate(rmsnorm(x, g_norm_mix[l]), sh1, sc1)
        proj = h @ w_in[l]
        cb, cc, cx, rx, rg, ga, gb = jnp.split(proj, IN_SPLITS, axis=-1)
        y_a = cb * causal_depthwise_conv(cc * cx, conv_a_w[l])
        u = causal_depthwise_conv(rx, conv_b_w[l]) + conv_b_bias[l]
        y_b = rg_lru(u, w_rg_a[l], b_rg_a[l], w_rg_x[l], b_rg_x[l], lru_lambda[l]) * jax.nn.gelu(rg)
        merged = jax.nn.sigmoid(ga) * y_a + jax.nn.sigmoid(gb) * y_b
        x = x + gt1[:, None, :] * (merged @ w_out[l])

        h = modulate(rmsnorm(x, g_norm_ffn[l]), sh2, sc2)
        g_ff, u_ff = jnp.split(h @ w_gate_up[l], 2, axis=-1)
        x = x + gt2[:, None, :] * ((jax.nn.silu(g_ff) * u_ff) @ w_down[l])
    return rmsnorm(x, g_norm_final)


import jax as _jax
import jax.numpy as _jnp

TWIN_FORMAT = 'train_step'
FWD_PARAMS = ['x', 'c', 'w_ada', 'b_ada', 'g_norm_mix', 'w_in', 'conv_a_w', 'conv_b_w', 'conv_b_bias', 'w_rg_a', 'b_rg_a', 'w_rg_x', 'b_rg_x', 'lru_lambda', 'w_out', 'g_norm_ffn', 'w_gate_up', 'w_down', 'g_norm_final']
TWIN_WEIGHTS = ['w_ada', 'b_ada', 'g_norm_mix', 'w_in', 'conv_a_w', 'conv_b_w', 'conv_b_bias', 'w_rg_a', 'b_rg_a', 'w_rg_x', 'b_rg_x', 'lru_lambda', 'w_out', 'g_norm_ffn', 'w_gate_up', 'w_down', 'g_norm_final']
TWIN_DIFF_INPUT = 'x'
TWIN_INPUTS = ['x', 'c', 'w_ada', 'b_ada', 'g_norm_mix', 'w_in', 'conv_a_w', 'conv_b_w', 'conv_b_bias', 'w_rg_a', 'b_rg_a', 'w_rg_x', 'b_rg_x', 'lru_lambda', 'w_out', 'g_norm_ffn', 'w_gate_up', 'w_down', 'g_norm_final', 'loss_target', 'm_w_ada', 'm_b_ada', 'm_g_norm_mix', 'm_w_in', 'm_conv_a_w', 'm_conv_b_w', 'm_conv_b_bias', 'm_w_rg_a', 'm_b_rg_a', 'm_w_rg_x', 'm_b_rg_x', 'm_lru_lambda', 'm_w_out', 'm_g_norm_ffn', 'm_w_gate_up', 'm_w_down', 'm_g_norm_final', 'v_w_ada', 'v_b_ada', 'v_g_norm_mix', 'v_w_in', 'v_conv_a_w', 'v_conv_b_w', 'v_conv_b_bias', 'v_w_rg_a', 'v_b_rg_a', 'v_w_rg_x', 'v_b_rg_x', 'v_lru_lambda', 'v_w_out', 'v_g_norm_ffn', 'v_w_gate_up', 'v_w_down', 'v_g_norm_final']
TWIN_OUTPUTS = ['loss', 'grad_x', 'grad_w_ada', 'grad_b_ada', 'grad_g_norm_mix', 'grad_w_in', 'grad_conv_a_w', 'grad_conv_b_w', 'grad_conv_b_bias', 'grad_w_rg_a', 'grad_b_rg_a', 'grad_w_rg_x', 'grad_b_rg_x', 'grad_lru_lambda', 'grad_w_out', 'grad_g_norm_ffn', 'grad_w_gate_up', 'grad_w_down', 'grad_g_norm_final', 'delta_w_ada', 'delta_b_ada', 'delta_g_norm_mix', 'delta_w_in', 'delta_conv_a_w', 'delta_conv_b_w', 'delta_conv_b_bias', 'delta_w_rg_a', 'delta_b_rg_a', 'delta_w_rg_x', 'delta_b_rg_x', 'delta_lru_lambda', 'delta_w_out', 'delta_g_norm_ffn', 'delta_w_gate_up', 'delta_w_down', 'delta_g_norm_final', 'new_m_w_ada', 'new_m_b_ada', 'new_m_g_norm_mix', 'new_m_w_in', 'new_m_conv_a_w', 'new_m_conv_b_w', 'new_m_conv_b_bias', 'new_m_w_rg_a', 'new_m_b_rg_a', 'new_m_w_rg_x', 'new_m_b_rg_x', 'new_m_lru_lambda', 'new_m_w_out', 'new_m_g_norm_ffn', 'new_m_w_gate_up', 'new_m_w_down', 'new_m_g_norm_final', 'new_v_w_ada', 'new_v_b_ada', 'new_v_g_norm_mix', 'new_v_w_in', 'new_v_conv_a_w', 'new_v_conv_b_w', 'new_v_conv_b_bias', 'new_v_w_rg_a', 'new_v_b_rg_a', 'new_v_w_rg_x', 'new_v_b_rg_x', 'new_v_lru_lambda', 'new_v_w_out', 'new_v_g_norm_ffn', 'new_v_w_gate_up', 'new_v_w_down', 'new_v_g_norm_final']
TWIN_LEAF_KINDS = {'loss': 'loss', 'grad_x': 'grad_x', 'grad_w_ada': 'grad_w', 'grad_b_ada': 'grad_w', 'grad_g_norm_mix': 'grad_w', 'grad_w_in': 'grad_w', 'grad_conv_a_w': 'grad_w', 'grad_conv_b_w': 'grad_w', 'grad_conv_b_bias': 'grad_w', 'grad_w_rg_a': 'grad_w', 'grad_b_rg_a': 'grad_w', 'grad_w_rg_x': 'grad_w', 'grad_b_rg_x': 'grad_w', 'grad_lru_lambda': 'grad_w', 'grad_w_out': 'grad_w', 'grad_g_norm_ffn': 'grad_w', 'grad_w_gate_up': 'grad_w', 'grad_w_down': 'grad_w', 'grad_g_norm_final': 'grad_w', 'delta_w_ada': 'delta_w', 'delta_b_ada': 'delta_w', 'delta_g_norm_mix': 'delta_w', 'delta_w_in': 'delta_w', 'delta_conv_a_w': 'delta_w', 'delta_conv_b_w': 'delta_w', 'delta_conv_b_bias': 'delta_w', 'delta_w_rg_a': 'delta_w', 'delta_b_rg_a': 'delta_w', 'delta_w_rg_x': 'delta_w', 'delta_b_rg_x': 'delta_w', 'delta_lru_lambda': 'delta_w', 'delta_w_out': 'delta_w', 'delta_g_norm_ffn': 'delta_w', 'delta_w_gate_up': 'delta_w', 'delta_w_down': 'delta_w', 'delta_g_norm_final': 'delta_w', 'new_m_w_ada': 'new_m', 'new_m_b_ada': 'new_m', 'new_m_g_norm_mix': 'new_m', 'new_m_w_in': 'new_m', 'new_m_conv_a_w': 'new_m', 'new_m_conv_b_w': 'new_m', 'new_m_conv_b_bias': 'new_m', 'new_m_w_rg_a': 'new_m', 'new_m_b_rg_a': 'new_m', 'new_m_w_rg_x': 'new_m', 'new_m_b_rg_x': 'new_m', 'new_m_lru_lambda': 'new_m', 'new_m_w_out': 'new_m', 'new_m_g_norm_ffn': 'new_m', 'new_m_w_gate_up': 'new_m', 'new_m_w_down': 'new_m', 'new_m_g_norm_final': 'new_m', 'new_v_w_ada': 'new_v', 'new_v_b_ada': 'new_v', 'new_v_g_norm_mix': 'new_v', 'new_v_w_in': 'new_v', 'new_v_conv_a_w': 'new_v', 'new_v_conv_b_w': 'new_v', 'new_v_conv_b_bias': 'new_v', 'new_v_w_rg_a': 'new_v', 'new_v_b_rg_a': 'new_v', 'new_v_w_rg_x': 'new_v', 'new_v_b_rg_x': 'new_v', 'new_v_lru_lambda': 'new_v', 'new_v_w_out': 'new_v', 'new_v_g_norm_ffn': 'new_v', 'new_v_w_gate_up': 'new_v', 'new_v_w_down': 'new_v', 'new_v_g_norm_final': 'new_v'}


def _forward(args):
    return _fwd_reference(*[args[k] for k in FWD_PARAMS])


def _output_shape():
    def fwd():
        inp = _fwd_setup_inputs(0)
        return _fwd_reference(*[inp[k] for k in FWD_PARAMS])
    out = _jax.eval_shape(fwd)
    return out.shape, out.dtype

N_MICROBATCH = 1
ADAM_LR = 0.001
ADAM_B1 = 0.9
ADAM_B2 = 0.999
ADAM_EPS = 1e-08
ADAM_WD = 0.01
ADAM_STEP = 10
PER_EXAMPLE_BATCH_AXIS = {'x': 0, 'c': 0, 'loss_target': 0}
SHARED_INPUTS = []
_WEIGHT_DTYPES = {'w_ada': _jnp.float32, 'b_ada': _jnp.float32, 'g_norm_mix': _jnp.float32, 'w_in': _jnp.float32, 'conv_a_w': _jnp.float32, 'conv_b_w': _jnp.float32, 'conv_b_bias': _jnp.float32, 'w_rg_a': _jnp.float32, 'b_rg_a': _jnp.float32, 'w_rg_x': _jnp.float32, 'b_rg_x': _jnp.float32, 'lru_lambda': _jnp.float32, 'w_out': _jnp.float32, 'g_norm_ffn': _jnp.float32, 'w_gate_up': _jnp.float32, 'w_down': _jnp.float32, 'g_norm_final': _jnp.float32}
MOMENT_SCALE = {'w_ada': 1.521874e-01, 'b_ada': 2.919573e-01, 'g_norm_mix': 1.030275e-01, 'w_in': 5.213399e-02, 'conv_a_w': 5.269821e-02, 'conv_b_w': 8.170647e-02, 'conv_b_bias': 2.530248e-01, 'w_rg_a': 7.704519e-03, 'b_rg_a': 1.341547e-02, 'w_rg_x': 1.536206e-02, 'b_rg_x': 2.965143e-02, 'lru_lambda': 3.451030e-02, 'w_out': 8.737643e-02, 'g_norm_ffn': 7.335436e-02, 'w_gate_up': 3.222005e-02, 'w_down': 5.244605e-02, 'g_norm_final': 6.417596e+01}


def _to_microbatches(a, axis):
    t = _jnp.moveaxis(a, axis, 0)
    t = t.reshape((N_MICROBATCH, t.shape[0] // N_MICROBATCH) + t.shape[1:])
    return _jnp.moveaxis(t, 1, axis + 1)


def setup_inputs(seed: int = 0) -> dict:
    inp = _fwd_setup_inputs(seed)
    key = _jax.random.fold_in(_jax.random.key(seed), 7919)
    shape, _ = _output_shape()
    out = dict(inp)
    out["loss_target"] = _jax.random.normal(_jax.random.fold_in(key, 0), shape, _jnp.float32)
    for i, name in enumerate(TWIN_WEIGHTS):
        w = inp[name].astype(_jnp.float32)
        if MOMENT_SCALE is None:
            s = _jnp.sqrt(_jnp.mean(_jnp.square(w)) + 1e-30)
        else:
            s = MOMENT_SCALE[name]
        km, kv = _jax.random.split(_jax.random.fold_in(key, i + 1))
        out[name] = w
        out["m_" + name] = s * _jax.random.normal(km, w.shape, _jnp.float32)
        out["v_" + name] = (s * s) * _jax.random.uniform(kv, w.shape, _jnp.float32, 0.5, 1.5)
    if N_MICROBATCH > 1:
        for name, axis in PER_EXAMPLE_BATCH_AXIS.items():
            out[name] = _to_microbatches(out[name], axis)
    return {'x': out['x'], 'c': out['c'], 'w_ada': out['w_ada'], 'b_ada': out['b_ada'], 'g_norm_mix': out['g_norm_mix'], 'w_in': out['w_in'], 'conv_a_w': out['conv_a_w'], 'conv_b_w': out['conv_b_w'], 'conv_b_bias': out['conv_b_bias'], 'w_rg_a': out['w_rg_a'], 'b_rg_a': out['b_rg_a'], 'w_rg_x': out['w_rg_x'], 'b_rg_x': out['b_rg_x'], 'lru_lambda': out['lru_lambda'], 'w_out': out['w_out'], 'g_norm_ffn': out['g_norm_ffn'], 'w_gate_up': out['w_gate_up'], 'w_down': out['w_down'], 'g_norm_final': out['g_norm_final'], 'loss_target': out['loss_target'], 'm_w_ada': out['m_w_ada'], 'm_b_ada': out['m_b_ada'], 'm_g_norm_mix': out['m_g_norm_mix'], 'm_w_in': out['m_w_in'], 'm_conv_a_w': out['m_conv_a_w'], 'm_conv_b_w': out['m_conv_b_w'], 'm_conv_b_bias': out['m_conv_b_bias'], 'm_w_rg_a': out['m_w_rg_a'], 'm_b_rg_a': out['m_b_rg_a'], 'm_w_rg_x': out['m_w_rg_x'], 'm_b_rg_x': out['m_b_rg_x'], 'm_lru_lambda': out['m_lru_lambda'], 'm_w_out': out['m_w_out'], 'm_g_norm_ffn': out['m_g_norm_ffn'], 'm_w_gate_up': out['m_w_gate_up'], 'm_w_down': out['m_w_down'], 'm_g_norm_final': out['m_g_norm_final'], 'v_w_ada': out['v_w_ada'], 'v_b_ada': out['v_b_ada'], 'v_g_norm_mix': out['v_g_norm_mix'], 'v_w_in': out['v_w_in'], 'v_conv_a_w': out['v_conv_a_w'], 'v_conv_b_w': out['v_conv_b_w'], 'v_conv_b_bias': out['v_conv_b_bias'], 'v_w_rg_a': out['v_w_rg_a'], 'v_b_rg_a': out['v_b_rg_a'], 'v_w_rg_x': out['v_w_rg_x'], 'v_b_rg_x': out['v_b_rg_x'], 'v_lru_lambda': out['v_lru_lambda'], 'v_w_out': out['v_w_out'], 'v_g_norm_ffn': out['v_g_norm_ffn'], 'v_w_gate_up': out['v_w_gate_up'], 'v_w_down': out['v_w_down'], 'v_g_norm_final': out['v_g_norm_final']}


def _loss(weights, diff, rest, loss_target):
    with _jax.named_scope("forward"):
        args = {**rest, TWIN_DIFF_INPUT: diff, **{k: w.astype(_WEIGHT_DTYPES[k]) for k, w in weights.items()}}
        y = _forward(args)
    with _jax.named_scope("loss_head"):
        err = _jnp.square(y.astype(_jnp.float32) - loss_target)
        return 0.5 * _jnp.sum(_jnp.mean(err, axis=-1)) if err.ndim else 0.5 * err


def _adamw(w, g, m, v):
    m = ADAM_B1 * m + (1.0 - ADAM_B1) * g
    v = ADAM_B2 * v + (1.0 - ADAM_B2) * _jnp.square(g)
    m_hat = m / (1.0 - ADAM_B1 ** ADAM_STEP)
    v_hat = v / (1.0 - ADAM_B2 ** ADAM_STEP)
    delta = -ADAM_LR * (m_hat / (_jnp.sqrt(v_hat) + ADAM_EPS) + ADAM_WD * w)
    return delta, m, v


def reference(x, c, w_ada, b_ada, g_norm_mix, w_in, conv_a_w, conv_b_w, conv_b_bias, w_rg_a, b_rg_a, w_rg_x, b_rg_x, lru_lambda, w_out, g_norm_ffn, w_gate_up, w_down, g_norm_final, loss_target, m_w_ada, m_b_ada, m_g_norm_mix, m_w_in, m_conv_a_w, m_conv_b_w, m_conv_b_bias, m_w_rg_a, m_b_rg_a, m_w_rg_x, m_b_rg_x, m_lru_lambda, m_w_out, m_g_norm_ffn, m_w_gate_up, m_w_down, m_g_norm_final, v_w_ada, v_b_ada, v_g_norm_mix, v_w_in, v_conv_a_w, v_conv_b_w, v_conv_b_bias, v_w_rg_a, v_b_rg_a, v_w_rg_x, v_b_rg_x, v_lru_lambda, v_w_out, v_g_norm_ffn, v_w_gate_up, v_w_down, v_g_norm_final):
    given = dict(x=x, c=c, w_ada=w_ada, b_ada=b_ada, g_norm_mix=g_norm_mix, w_in=w_in, conv_a_w=conv_a_w, conv_b_w=conv_b_w, conv_b_bias=conv_b_bias, w_rg_a=w_rg_a, b_rg_a=b_rg_a, w_rg_x=w_rg_x, b_rg_x=b_rg_x, lru_lambda=lru_lambda, w_out=w_out, g_norm_ffn=g_norm_ffn, w_gate_up=w_gate_up, w_down=w_down, g_norm_final=g_norm_final, loss_target=loss_target, m_w_ada=m_w_ada, m_b_ada=m_b_ada, m_g_norm_mix=m_g_norm_mix, m_w_in=m_w_in, m_conv_a_w=m_conv_a_w, m_conv_b_w=m_conv_b_w, m_conv_b_bias=m_conv_b_bias, m_w_rg_a=m_w_rg_a, m_b_rg_a=m_b_rg_a, m_w_rg_x=m_w_rg_x, m_b_rg_x=m_b_rg_x, m_lru_lambda=m_lru_lambda, m_w_out=m_w_out, m_g_norm_ffn=m_g_norm_ffn, m_w_gate_up=m_w_gate_up, m_w_down=m_w_down, m_g_norm_final=m_g_norm_final, v_w_ada=v_w_ada, v_b_ada=v_b_ada, v_g_norm_mix=v_g_norm_mix, v_w_in=v_w_in, v_conv_a_w=v_conv_a_w, v_conv_b_w=v_conv_b_w, v_conv_b_bias=v_conv_b_bias, v_w_rg_a=v_w_rg_a, v_b_rg_a=v_b_rg_a, v_w_rg_x=v_w_rg_x, v_b_rg_x=v_b_rg_x, v_lru_lambda=v_lru_lambda, v_w_out=v_w_out, v_g_norm_ffn=v_g_norm_ffn, v_w_gate_up=v_w_gate_up, v_w_down=v_w_down, v_g_norm_final=v_g_norm_final)
    weights = {n: given[n] for n in TWIN_WEIGHTS}
    shared = {n: given[n] for n in SHARED_INPUTS}
    per_example = {n: given[n] for n in ['x', 'c']}
    grad_fn = _jax.value_and_grad(_loss, argnums=(0, 1))

    def one_microbatch(ex, loss_target):
        ex = dict(ex)
        diff = ex.pop(TWIN_DIFF_INPUT)
        return grad_fn(weights, diff, {**shared, **ex}, loss_target)

    if N_MICROBATCH == 1:
        loss, (grad_w, grad_x) = one_microbatch(per_example, given["loss_target"])
    else:
        def body(carry, xs):
            loss_sum, grad_sum = carry
            l_k, (gw_k, gx_k) = one_microbatch(xs[0], xs[1])
            with _jax.named_scope("update"):
                return (loss_sum + l_k, _jax.tree.map(_jnp.add, grad_sum, gw_k)), gx_k

        init = (_jnp.zeros((), _jnp.float32), _jax.tree.map(_jnp.zeros_like, weights))
        (loss, grad_w), grad_x = _jax.lax.scan(body, init, (per_example, given["loss_target"]))
    with _jax.named_scope("update"):
        delta_w, new_m, new_v = {}, {}, {}
        for n in TWIN_WEIGHTS:
            delta_w[n], new_m[n], new_v[n] = _adamw(weights[n], grad_w[n], given["m_" + n], given["v_" + n])
    return (loss, grad_x, *[grad_w[n] for n in TWIN_WEIGHTS], *[delta_w[n] for n in TWIN_WEIGHTS],
            *[new_m[n] for n in TWIN_WEIGHTS], *[new_v[n] for n in TWIN_WEIGHTS])
```

```python
import functools
import math

import jax
import jax.numpy as jnp
from jax import lax
from jax.experimental import pallas as pl
from jax.experimental.pallas import tpu as pltpu

F32 = jnp.float32
BF16 = jnp.bfloat16

D = 1024
DFF = 2816
NDEV = 8
HEADS = 4
HB = D // HEADS
FB = DFF // 4
EPS = 1e-6
LRU_C = 8.0
ADAM_LR, ADAM_B1, ADAM_B2, ADAM_EPS, ADAM_WD, ADAM_STEP = 0.001, 0.9, 0.999, 1e-08, 0.01, 10

VMEM_LIMIT = 56 * 1024 * 1024
TM = 512
TK = 512
TT = 256
CG = 256
MESH = pl.DeviceIdType.MESH
AXES = ("x", "y", "c")


def _cp(*sem):
    return pltpu.CompilerParams(dimension_semantics=sem, vmem_limit_bytes=VMEM_LIMIT)


def _sig(x):
    return 1.0 / (1.0 + jnp.exp(-x))


def _log_sigmoid(x):
    z = jnp.exp(-jnp.abs(x))
    u = 1.0 + z
    d = u - 1.0
    l1p = jnp.where(d == 0.0, z, jnp.log(u) * (z / jnp.where(d == 0.0, 1.0, d)))
    return -(jnp.maximum(-x, 0.0) + l1p)


def _neg_expm1(x):
    p = x * (1.0 + x * 0.5 * (1.0 + x * (1.0 / 3.0) * (1.0 + x * 0.25 * (1.0 + x * 0.2 * (1.0 + x * (1.0 / 6.0))))))
    return jnp.where(x > -0.25, -p, 1.0 - jnp.exp(x))


_GC = math.sqrt(2.0 / math.pi)


def _gelu(x):
    t = jnp.tanh(_GC * (x + 0.044715 * x * x * x))
    return 0.5 * x * (1.0 + t), t


def _dot(a, b):
    return jnp.dot(a, b, preferred_element_type=F32)


def _dot_nt(a, b):
    return lax.dot_general(a, b, (((1,), (1,)), ((), ())), preferred_element_type=F32)


def _dot_tn(a, b):
    return lax.dot_general(a, b, (((0,), (0,)), ((), ())), preferred_element_type=F32)


def _in_proj(x, mod, g_mix, w_in):
    t_len = x.shape[0]
    tm = min(TM, t_len)

    def body(x_ref, mod_ref, g_ref, w_ref, proj_ref, h_ref, h_scr):
        @pl.when(pl.program_id(1) == 0)
        def _():
            xv = x_ref[...]
            r = lax.rsqrt(jnp.mean(xv * xv, axis=-1, keepdims=True) + EPS)
            h = (xv * r * g_ref[...] * (1.0 + mod_ref[1:2, :]) + mod_ref[0:1, :]).astype(BF16)
            h_scr[...] = h
            h_ref[...] = h

        proj_ref[0] = _dot(h_scr[...], w_ref[...]).astype(BF16)

    return pl.pallas_call(
        body, name="in_proj", grid=(t_len // tm, 7),
        in_specs=[pl.BlockSpec((tm, D), lambda i, s: (i, 0)),
                  pl.BlockSpec((8, D), lambda i, s: (0, 0)),
                  pl.BlockSpec((1, D), lambda i, s: (0, 0)),
                  pl.BlockSpec((D, D), lambda i, s: (0, s))],
        out_specs=[pl.BlockSpec((1, tm, D), lambda i, s: (s, i, 0)),
                   pl.BlockSpec((tm, D), lambda i, s: (i, 0))],
        out_shape=[jax.ShapeDtypeStruct((7, t_len, D), BF16), jax.ShapeDtypeStruct((t_len, D), BF16)],
        scratch_shapes=[pltpu.VMEM((tm, D), BF16)],
        compiler_params=_cp("parallel", "arbitrary"),
    )(x, mod, g_mix, w_in)


P_WA, P_WB, P_CBIAS, P_BA, P_BX, P_LAM = 0, 3, 7, 8, 9, 10


def _lru_gates(rp, ip, ls, first_row):
    r = _sig(rp)
    ig = _sig(ip)
    la = LRU_C * r * ls
    a = jnp.exp(la)
    m2 = _neg_expm1(2.0 * la)
    mult = jnp.where(first_row, 1.0, jnp.sqrt(jnp.maximum(m2, 0.0)))
    return r, ig, la, a, m2, mult


def _mixer_fwd(proj, prm, wa, wx):
    t_len = proj.shape[1]
    tt = min(TT, t_len)

    def body(proj_ref, prm_ref, wa_ref, wx_ref, mg_ref, hl_ref, xe, ve, hc, u_s, ya_s, rp_s, ip_s):
        t = pl.program_id(0)

        @pl.when(t == 0)
        def _():
            xe[0:8, :] = jnp.zeros((8, D), F32)
            ve[0:8, :] = jnp.zeros((8, D), F32)
            hc[...] = jnp.zeros((8, D), F32)

        xe[8:8 + tt, :] = proj_ref[3].astype(F32)
        ve[8:8 + tt, :] = proj_ref[1].astype(F32) * proj_ref[2].astype(F32)
        u = prm_ref[P_CBIAS:P_CBIAS + 1, :] + prm_ref[P_WB:P_WB + 1, :] * xe[5:5 + tt, :]
        for k in range(1, 4):
            u = u + prm_ref[P_WB + k:P_WB + k + 1, :] * xe[5 + k:5 + k + tt, :]
        u_s[...] = u
        ya = prm_ref[P_WA:P_WA + 1, :] * ve[6:6 + tt, :]
        for k in range(1, 3):
            ya = ya + prm_ref[P_WA + k:P_WA + k + 1, :] * ve[6 + k:6 + k + tt, :]
        ya_s[...] = ya
        xe[0:8, :] = xe[tt:tt + 8, :]
        ve[0:8, :] = ve[tt:tt + 8, :]

        ub = u.astype(BF16)
        for h in range(HEADS):
            cs = slice(h * HB, (h + 1) * HB)
            rp_s[:, cs] = _dot(ub[:, cs], wa_ref[h]) + prm_ref[P_BA:P_BA + 1, cs]
            ip_s[:, cs] = _dot(ub[:, cs], wx_ref[h]) + prm_ref[P_BX:P_BX + 1, cs]

        ls_all = _log_sigmoid(prm_ref[P_LAM:P_LAM + 1, :])
        row = lax.broadcasted_iota(jnp.int32, (8, CG), 0)

        def blk(i, carry):
            r0 = pl.multiple_of(i * 16, 16)
            for g in range(D // CG):
                cs = slice(g * CG, (g + 1) * CG)
                ls = ls_all[:, cs]
                hprev = hc[:, cs]
                hs = []
                for sb in range(2):
                    rr = r0 + 8 * sb
                    first = (row + (t * tt + rr)) == 0
                    _, ig, _, a, _, mult = _lru_gates(rp_s[pl.ds(rr, 8), cs], ip_s[pl.ds(rr, 8), cs], ls, first)
                    b = mult * (ig * u_s[pl.ds(rr, 8), cs])
                    for s in (1, 2, 4):
                        a_sh = jnp.where(row >= s, pltpu.roll(a, s, 0), 1.0)
                        b_sh = jnp.where(row >= s, pltpu.roll(b, s, 0), 0.0)
                        b = a * b_sh + b
                        a = a * a_sh
                    hv = a * hprev + b
                    hprev = jnp.broadcast_to(hv[7:8, :], hv.shape)
                    hs.append(hv)
                hc[:, cs] = hprev
                h16 = jnp.concatenate(hs, axis=0)
                rows = pl.ds(r0, 16)
                gl, _ = _gelu(proj_ref[4, rows, cs].astype(F32))
                y_b = h16 * gl
                y_a = proj_ref[0, rows, cs].astype(F32) * ya_s[rows, cs]
                mg = _sig(proj_ref[5, rows, cs].astype(F32)) * y_a + _sig(proj_ref[6, rows, cs].astype(F32)) * y_b
                mg_ref[rows, cs] = mg.astype(BF16)
                hl_ref[rows, cs] = h16.astype(BF16)
            return carry

        lax.fori_loop(0, tt // 16, blk, 0)

    return pl.pallas_call(
        body, name="mixer_fwd", grid=(t_len // tt,),
        in_specs=[pl.BlockSpec((7, tt, D), lambda t: (0, t, 0)),
                  pl.BlockSpec((16, D), lambda t: (0, 0)),
                  pl.BlockSpec((HEADS, HB, HB), lambda t: (0, 0, 0)),
                  pl.BlockSpec((HEADS, HB, HB), lambda t: (0, 0, 0))],
        out_specs=[pl.BlockSpec((tt, D), lambda t: (t, 0)), pl.BlockSpec((tt, D), lambda t: (t, 0))],
        out_shape=[jax.ShapeDtypeStruct((t_len, D), BF16), jax.ShapeDtypeStruct((t_len, D), BF16)],
        scratch_shapes=[pltpu.VMEM((tt + 8, D), F32), pltpu.VMEM((tt + 8, D), F32), pltpu.VMEM((8, D), F32),
                        pltpu.VMEM((tt, D), F32), pltpu.VMEM((tt, D), F32), pltpu.VMEM((tt, D), F32),
                        pltpu.VMEM((tt, D), F32)],
        compiler_params=_cp("arbitrary"),
    )(proj, prm, wa, wx)


def _out_proj(merged, x, mod, g_ffn, w_out):
    t_len = x.shape[0]
    tm = min(TM, t_len)

    def body(mg_ref, x_ref, mod_ref, g_ref, w_ref, x1_ref, h2_ref):
        x1 = x_ref[...] + mod_ref[2:3, :] * _dot(mg_ref[...], w_ref[...])
        x1_ref[...] = x1
        r = lax.rsqrt(jnp.mean(x1 * x1, axis=-1, keepdims=True) + EPS)
        h2_ref[...] = (x1 * r * g_ref[...] * (1.0 + mod_ref[4:5, :]) + mod_ref[3:4, :]).astype(BF16)

    return pl.pallas_call(
        body, name="out_proj", grid=(t_len // tm,),
        in_specs=[pl.BlockSpec((tm, D), lambda i: (i, 0)), pl.BlockSpec((tm, D), lambda i: (i, 0)),
                  pl.BlockSpec((8, D), lambda i: (0, 0)), pl.BlockSpec((1, D), lambda i: (0, 0)),
                  pl.BlockSpec((D, D), lambda i: (0, 0))],
        out_specs=[pl.BlockSpec((tm, D), lambda i: (i, 0)), pl.BlockSpec((tm, D), lambda i: (i, 0))],
        out_shape=[jax.ShapeDtypeStruct((t_len, D), F32), jax.ShapeDtypeStruct((t_len, D), BF16)],
        compiler_params=_cp("parallel"),
    )(merged, x, mod, g_ffn, w_out)


def _ffn_fwd(h2, x1, target, mod, g_fin, w_gu, w_down):
    t_len = x1.shape[0]
    tm = min(TM, t_len)

    def body(h2_ref, x1_ref, tg_ref, mod_ref, g_ref, wgu_ref, wd_ref, gu_ref, dx2_ref, loss_ref, dg_ref, acc):
        i, j = pl.program_id(0), pl.program_id(1)

        @pl.when((i == 0) & (j == 0))
        def _():
            loss_ref[...] = jnp.zeros_like(loss_ref)
            dg_ref[...] = jnp.zeros_like(dg_ref)

        hb = h2_ref[...]
        gate = _dot(hb, wgu_ref[0, 0])
        up = _dot(hb, wgu_ref[1, 0])
        gu_ref[0, 0] = gate.astype(BF16)
        gu_ref[1, 0] = up.astype(BF16)
        act = (gate * _sig(gate) * up).astype(BF16)
        part = _dot(act, wd_ref[...])

        @pl.when(j == 0)
        def _():
            acc[...] = part

        @pl.when(j > 0)
        def _():
            acc[...] += part

        @pl.when(j == 3)
        def _():
            x2 = x1_ref[...] + mod_ref[5:6, :] * acc[...]
            r = lax.rsqrt(jnp.mean(x2 * x2, axis=-1, keepdims=True) + EPS)
            xn = x2 * r
            gf = g_ref[...]
            diff = xn * gf - tg_ref[...]
            loss_ref[...] += jnp.sum(diff * diff) * (0.5 / D)
            dy = diff * (1.0 / D)
            dg_ref[...] += jnp.sum(dy * xn, axis=0, keepdims=True)
            dxn = dy * gf
            dx2_ref[...] = r * (dxn - xn * jnp.mean(dxn * xn, axis=-1, keepdims=True))

    return pl.pallas_call(
        body, name="ffn_fwd", grid=(t_len // tm, 4),
        in_specs=[pl.BlockSpec((tm, D), lambda i, j: (i, 0)), pl.BlockSpec((tm, D), lambda i, j: (i, 0)),
                  pl.BlockSpec((tm, D), lambda i, j: (i, 0)), pl.BlockSpec((8, D), lambda i, j: (0, 0)),
                  pl.BlockSpec((1, D), lambda i, j: (0, 0)),
                  pl.BlockSpec((2, 1, D, FB), lambda i, j: (0, j, 0, 0)),
                  pl.BlockSpec((FB, D), lambda i, j: (j, 0))],
        out_specs=[pl.BlockSpec((2, 1, tm, FB), lambda i, j: (0, j, i, 0)),
                   pl.BlockSpec((tm, D), lambda i, j: (i, 0)),
                   pl.BlockSpec((1, 128), lambda i, j: (0, 0)),
                   pl.BlockSpec((1, D), lambda i, j: (0, 0))],
        out_shape=[jax.ShapeDtypeStruct((2, 4, t_len, FB), BF16), jax.ShapeDtypeStruct((t_len, D), F32),
                   jax.ShapeDtypeStruct((1, 128), F32), jax.ShapeDtypeStruct((1, D), F32)],
        scratch_shapes=[pltpu.VMEM((tm, D), F32)],
        compiler_params=_cp("arbitrary", "arbitrary"),
    )(h2, x1, target, mod, g_fin, w_gu, w_down)


S_SH, S_SC, S_G = 0, 1, 2


def _ffn_bwd(dx2, gu, x1, mod, g_ffn, w_gu, w_down, w_out):
    t_len = x1.shape[0]
    tm = min(TM, t_len)

    def body(dx2_ref, gu_ref, x1_ref, mod_ref, g_ref, wgu_ref, wd_ref, wo_ref,
             dgu_ref, act_ref, dx1_ref, dmg_ref, sums_ref, acc, dffn):
        i, j = pl.program_id(0), pl.program_id(1)

        @pl.when((i == 0) & (j == 0))
        def _():
            sums_ref[...] = jnp.zeros_like(sums_ref)

        @pl.when(j == 0)
        def _():
            dffn[...] = (dx2_ref[...] * mod_ref[5:6, :]).astype(BF16)

        dact = _dot_nt(dffn[...], wd_ref[...])
        gate = gu_ref[0, 0].astype(F32)
        up = gu_ref[1, 0].astype(F32)
        sg = _sig(gate)
        silu = gate * sg
        act_ref[0] = (silu * up).astype(BF16)
        dgate = (dact * up * (sg * (1.0 + gate * (1.0 - sg)))).astype(BF16)
        dup = (dact * silu).astype(BF16)
        dgu_ref[0, 0] = dgate
        dgu_ref[1, 0] = dup
        part = _dot_nt(dgate, wgu_ref[0, 0]) + _dot_nt(dup, wgu_ref[1, 0])

        @pl.when(j == 0)
        def _():
            acc[...] = part

        @pl.when(j > 0)
        def _():
            acc[...] += part

        @pl.when(j == 3)
        def _():
            dh2 = acc[...]
            x1 = x1_ref[...]
            r = lax.rsqrt(jnp.mean(x1 * x1, axis=-1, keepdims=True) + EPS)
            xn = x1 * r
            gf = g_ref[...]
            sums_ref[S_SH:S_SH + 1, :] += jnp.sum(dh2, axis=0, keepdims=True)
            sums_ref[S_SC:S_SC + 1, :] += jnp.sum(dh2 * (xn * gf), axis=0, keepdims=True)
            dhn = dh2 * (1.0 + mod_ref[4:5, :])
            sums_ref[S_G:S_G + 1, :] += jnp.sum(dhn * xn, axis=0, keepdims=True)
            dxn = dhn * gf
            dx1 = dx2_ref[...] + r * (dxn - xn * jnp.mean(dxn * xn, axis=-1, keepdims=True))
            dx1_ref[...] = dx1
            dmg_ref[...] = _dot_nt((dx1 * mod_ref[2:3, :]).astype(BF16), wo_ref[...]).astype(BF16)

    return pl.pallas_call(
        body, name="ffn_bwd", grid=(t_len // tm, 4),
        in_specs=[pl.BlockSpec((tm, D), lambda i, j: (i, 0)),
                  pl.BlockSpec((2, 1, tm, FB), lambda i, j: (0, j, i, 0)),
                  pl.BlockSpec((tm, D), lambda i, j: (i, 0)),
                  pl.BlockSpec((8, D), lambda i, j: (0, 0)), pl.BlockSpec((1, D), lambda i, j: (0, 0)),
                  pl.BlockSpec((2, 1, D, FB), lambda i, j: (0, j, 0, 0)),
                  pl.BlockSpec((FB, D), lambda i, j: (j, 0)),
                  pl.BlockSpec((D, D), lambda i, j: (0, 0))],
        out_specs=[pl.BlockSpec((2, 1, tm, FB), lambda i, j: (0, j, i, 0)),
                   pl.BlockSpec((1, tm, FB), lambda i, j: (j, i, 0)),
                   pl.BlockSpec((tm, D), lambda i, j: (i, 0)),
                   pl.BlockSpec((tm, D), lambda i, j: (i, 0)),
                   pl.BlockSpec((8, D), lambda i, j: (0, 0))],
        out_shape=[jax.ShapeDtypeStruct((2, 4, t_len, FB), BF16), jax.ShapeDtypeStruct((4, t_len, FB), BF16),
                   jax.ShapeDtypeStruct((t_len, D), F32), jax.ShapeDtypeStruct((t_len, D), BF16),
                   jax.ShapeDtypeStruct((8, D), F32)],
        scratch_shapes=[pltpu.VMEM((tm, D), F32), pltpu.VMEM((tm, D), BF16)],
        compiler_params=_cp("arbitrary", "arbitrary"),
    )(dx2, gu, x1, mod, g_ffn, w_gu, w_down, w_out)


def _gu_wgrad(h2, dgu):
    t_len = h2.shape[0]
    tk = min(TK, t_len)
    nk = t_len // tk

    def body(h_ref, d_ref, o_ref, acc):
        k = pl.program_id(2)
        part = _dot_tn(h_ref[...], d_ref[0, 0])

        @pl.when(k == 0)
        def _():
            acc[...] = part

        @pl.when(k > 0)
        def _():
            acc[...] += part

        @pl.when(k == nk - 1)
        def _():
            o_ref[0, 0] = acc[...].astype(BF16)

    out = pl.pallas_call(
        body, name="gu_wgrad", grid=(2, 4, nk),
        in_specs=[pl.BlockSpec((tk, D), lambda a, j, k: (k, 0)),
                  pl.BlockSpec((1, 1, tk, FB), lambda a, j, k: (a, j, k, 0))],
        out_specs=pl.BlockSpec((1, 1, D, FB), lambda a, j, k: (a, j, 0, 0)),
        out_shape=jax.ShapeDtypeStruct((2, 4, D, FB), BF16),
        scratch_shapes=[pltpu.VMEM((D, FB), F32)],
        compiler_params=_cp("parallel", "parallel", "arbitrary"),
    )(h2, dgu)
    return out.reshape(NDEV, D, FB)


def _scaled_wgrad(name, a, dx, w, gate_row, mod):
    nb, t_len, kb = a.shape
    tk = min(TK, t_len)
    nk = t_len // tk

    def body(a_ref, dx_ref, w_ref, mod_ref, o_ref, dg_ref, acc):
        j, k = pl.program_id(0), pl.program_id(1)

        @pl.when((j == 0) & (k == 0))
        def _():
            dg_ref[...] = jnp.zeros_like(dg_ref)

        part = _dot_tn(a_ref[0], dx_ref[...].astype(BF16))

        @pl.when(k == 0)
        def _():
            acc[...] = part

        @pl.when(k > 0)
        def _():
            acc[...] += part

        @pl.when(k == nk - 1)
        def _():
            z = acc[...]
            o_ref[...] = (z * mod_ref[gate_row:gate_row + 1, :]).astype(BF16)
            dg_ref[...] += jnp.sum(z * w_ref[...].astype(F32), axis=0, keepdims=True)

    return pl.pallas_call(
        body, name=name, grid=(nb, nk),
        in_specs=[pl.BlockSpec((1, tk, kb), lambda j, k: (j, k, 0)),
                  pl.BlockSpec((tk, D), lambda j, k: (k, 0)),
                  pl.BlockSpec((kb, D), lambda j, k: (j, 0)),
                  pl.BlockSpec((8, D), lambda j, k: (0, 0))],
        out_specs=[pl.BlockSpec((kb, D), lambda j, k: (j, 0)), pl.BlockSpec((1, D), lambda j, k: (0, 0))],
        out_shape=[jax.ShapeDtypeStruct((nb * kb, D), BF16), jax.ShapeDtypeStruct((1, D), F32)],
        scratch_shapes=[pltpu.VMEM((kb, D), F32)],
        compiler_params=_cp("arbitrary", "arbitrary"),
    )(a, dx, w, mod)


M_WA, M_WB, M_CBIAS, M_BA, M_BX, M_LS = 0, 3, 7, 8, 9, 10


def _mixer_bwd(proj, hl, dmg, prm, wa, wx):
    t_len = proj.shape[1]
    tt = min(TT, t_len)
    nt = t_len // tt
    hb8 = tt // 8

    def rev(i):
        return nt - 1 - i

    def halo(i):
        return jnp.maximum(rev(i) * hb8 - 1, 0)

    def body(proj_ref, ph_ref, hl_ref, hh_ref, dmg_ref, prm_ref, wa_ref, wx_ref,
             dp_ref, sums_ref, gwa_ref, gwx_ref,
             xe, ve, he, u_s, ya_s, rp_s, ip_s, due, dye, drp_s, dip_s, an, gn):
        i = pl.program_id(0)
        t = rev(i)

        @pl.when(i == 0)
        def _():
            sums_ref[...] = jnp.zeros_like(sums_ref)
            gwa_ref[...] = jnp.zeros_like(gwa_ref)
            gwx_ref[...] = jnp.zeros_like(gwx_ref)
            due[tt:tt + 8, :] = jnp.zeros((8, D), F32)
            dye[tt:tt + 8, :] = jnp.zeros((8, D), F32)
            an[...] = jnp.zeros((8, D), F32)
            gn[...] = jnp.zeros((8, D), F32)

        live = (t > 0).astype(F32)
        xe[0:8, :] = ph_ref[3].astype(F32) * live
        ve[0:8, :] = ph_ref[1].astype(F32) * ph_ref[2].astype(F32) * live
        he[0:8, :] = hh_ref[...].astype(F32) * live
        xe[8:8 + tt, :] = proj_ref[3].astype(F32)
        ve[8:8 + tt, :] = proj_ref[1].astype(F32) * proj_ref[2].astype(F32)
        he[8:8 + tt, :] = hl_ref[...].astype(F32)
        u = prm_ref[P_CBIAS:P_CBIAS + 1, :] + prm_ref[P_WB:P_WB + 1, :] * xe[5:5 + tt, :]
        for k in range(1, 4):
            u = u + prm_ref[P_WB + k:P_WB + k + 1, :] * xe[5 + k:5 + k + tt, :]
        u_s[...] = u
        ya = prm_ref[P_WA:P_WA + 1, :] * ve[6:6 + tt, :]
        for k in range(1, 3):
            ya = ya + prm_ref[P_WA + k:P_WA + k + 1, :] * ve[6 + k:6 + k + tt, :]
        ya_s[...] = ya
        ub = u.astype(BF16)
        for h in range(HEADS):
            cs = slice(h * HB, (h + 1) * HB)
            rp_s[:, cs] = _dot(ub[:, cs], wa_ref[h]) + prm_ref[P_BA:P_BA + 1, cs]
            ip_s[:, cs] = _dot(ub[:, cs], wx_ref[h]) + prm_ref[P_BX:P_BX + 1, cs]

        ls_all = _log_sigmoid(prm_ref[P_LAM:P_LAM + 1, :])
        row = lax.broadcasted_iota(jnp.int32, (8, CG), 0)
        nblk = tt // 16

        def blk(ib, carry):
            r0 = pl.multiple_of((nblk - 1 - ib) * 16, 16)
            rows = pl.ds(r0, 16)
            for g in range(D // CG):
                cs = slice(g * CG, (g + 1) * CG)
                ls = ls_all[:, cs]
                dm = dmg_ref[rows, cs].astype(F32)
                cb = proj_ref[0, rows, cs].astype(F32)
                rg = proj_ref[4, rows, cs].astype(F32)
                sga = _sig(proj_ref[5, rows, cs].astype(F32))
                sgb = _sig(proj_ref[6, rows, cs].astype(F32))
                ya0 = ya_s[rows, cs]
                h16 = he[pl.ds(r0 + 8, 16), cs]
                gl, th = _gelu(rg)
                dgl = 0.5 * (1.0 + th) + 0.5 * rg * (1.0 - th * th) * (_GC * (1.0 + 3.0 * 0.044715 * rg * rg))
                y_a = cb * ya0
                y_b = h16 * gl
                dy_a = dm * sga
                dy_b = dm * sgb
                dp_ref[5, rows, cs] = (dm * y_a * sga * (1.0 - sga)).astype(BF16)
                dp_ref[6, rows, cs] = (dm * y_b * sgb * (1.0 - sgb)).astype(BF16)
                dp_ref[4, rows, cs] = (dy_b * h16 * dgl).astype(BF16)
                dp_ref[0, rows, cs] = (dy_a * ya0).astype(BF16)
                dye[rows, cs] = dy_a * cb
                dh16 = dy_b * gl

                a_next = an[:, cs]
                g_next = gn[:, cs]
                s_ba = jnp.zeros((8, CG), F32)
                s_bx = jnp.zeros((8, CG), F32)
                s_ls = jnp.zeros((8, CG), F32)
                for sb in (1, 0):
                    rr = r0 + 8 * sb
                    first = (row + (t * tt + rr)) == 0
                    uu = u_s[pl.ds(rr, 8), cs]
                    r, ig, la, a, m2, mult = _lru_gates(rp_s[pl.ds(rr, 8), cs], ip_s[pl.ds(rr, 8), cs], ls, first)
                    ca = jnp.where(row < 7, pltpu.roll(a, 7, 0), a_next)
                    cb_ = dh16[8 * sb:8 * sb + 8, :]
                    for s in (1, 2, 4):
                        a_sh = jnp.where(row < 8 - s, pltpu.roll(ca, 8 - s, 0), 1.0)
                        b_sh = jnp.where(row < 8 - s, pltpu.roll(cb_, 8 - s, 0), 0.0)
                        cb_ = ca * b_sh + cb_
                        ca = ca * a_sh
                    gv = ca * g_next + cb_
                    g_next = jnp.broadcast_to(gv[0:1, :], gv.shape)
                    a_next = jnp.broadcast_to(a[0:1, :], a.shape)
                    hprev = jnp.where(row >= 1, pltpu.roll(he[pl.ds(rr + 8, 8), cs], 1, 0),
                                      pltpu.roll(he[pl.ds(rr, 8), cs], 1, 0))
                    da = gv * hprev
                    dmult = jnp.where(first, 0.0, gv * ig * uu)
                    dla = da * a + jnp.where(m2 > 0.0, dmult * (-(a * a) / mult), 0.0)
                    drp = dla * (LRU_C * ls) * r * (1.0 - r)
                    dip = gv * mult * uu * ig * (1.0 - ig)
                    s_ls = s_ls + dla * (LRU_C * r)
                    s_ba = s_ba + drp
                    s_bx = s_bx + dip
                    drp_s[pl.ds(rr, 8), cs] = drp
                    dip_s[pl.ds(rr, 8), cs] = dip
                    due[pl.ds(rr, 8), cs] = gv * mult * ig
                an[:, cs] = a_next
                gn[:, cs] = g_next
                sums_ref[M_BA:M_BA + 1, cs] += jnp.sum(s_ba, axis=0, keepdims=True)
                sums_ref[M_BX:M_BX + 1, cs] += jnp.sum(s_bx, axis=0, keepdims=True)
                sums_ref[M_LS:M_LS + 1, cs] += jnp.sum(s_ls, axis=0, keepdims=True)
            return carry

        lax.fori_loop(0, nblk, blk, 0)

        drp_b = drp_s[...].astype(BF16)
        dip_b = dip_s[...].astype(BF16)
        for h in range(HEADS):
            cs = slice(h * HB, (h + 1) * HB)
            due[0:tt, cs] += _dot_nt(drp_b[:, cs], wa_ref[h]) + _dot_nt(dip_b[:, cs], wx_ref[h])
            gwa_ref[h] += _dot_tn(ub[:, cs], drp_b[:, cs])
            gwx_ref[h] += _dot_tn(ub[:, cs], dip_b[:, cs])

        du = due[0:tt, :]
        sums_ref[M_CBIAS:M_CBIAS + 1, :] += jnp.sum(du, axis=0, keepdims=True)
        drx = prm_ref[P_WB:P_WB + 1, :] * due[3:3 + tt, :]
        sums_ref[M_WB:M_WB + 1, :] += jnp.sum(du * xe[5:5 + tt, :], axis=0, keepdims=True)
        for k in range(1, 4):
            drx = drx + prm_ref[P_WB + k:P_WB + k + 1, :] * due[3 - k:3 - k + tt, :]
            sums_ref[M_WB + k:M_WB + k + 1, :] += jnp.sum(du * xe[5 + k:5 + k + tt, :], axis=0, keepdims=True)
        dp_ref[3] = drx.astype(BF16)
        dya = dye[0:tt, :]
        dv = prm_ref[P_WA:P_WA + 1, :] * dye[2:2 + tt, :]
        sums_ref[M_WA:M_WA + 1, :] += jnp.sum(dya * ve[6:6 + tt, :], axis=0, keepdims=True)
        for k in range(1, 3):
            dv = dv + prm_ref[P_WA + k:P_WA + k + 1, :] * dye[2 - k:2 - k + tt, :]
            sums_ref[M_WA + k:M_WA + k + 1, :] += jnp.sum(dya * ve[6 + k:6 + k + tt, :], axis=0, keepdims=True)
        dp_ref[1] = (dv * proj_ref[2].astype(F32)).astype(BF16)
        dp_ref[2] = (dv * proj_ref[1].astype(F32)).astype(BF16)
        due[tt:tt + 8, :] = due[0:8, :]
        dye[tt:tt + 8, :] = dye[0:8, :]

        @pl.when(i == nt - 1)
        def _():
            sums_ref[M_LS:M_LS + 1, :] = sums_ref[M_LS:M_LS + 1, :] * _sig(-prm_ref[P_LAM:P_LAM + 1, :])

    big = lambda: pltpu.VMEM((tt + 8, D), F32)
    tile = lambda: pltpu.VMEM((tt, D), F32)
    return pl.pallas_call(
        body, name="mixer_bwd", grid=(nt,),
        in_specs=[pl.BlockSpec((7, tt, D), lambda i: (0, rev(i), 0)),
                  pl.BlockSpec((7, 8, D), lambda i: (0, halo(i), 0)),
                  pl.BlockSpec((tt, D), lambda i: (rev(i), 0)),
                  pl.BlockSpec((8, D), lambda i: (halo(i), 0)),
                  pl.BlockSpec((tt, D), lambda i: (rev(i), 0)),
                  pl.BlockSpec((16, D), lambda i: (0, 0)),
                  pl.BlockSpec((HEADS, HB, HB), lambda i: (0, 0, 0)),
                  pl.BlockSpec((HEADS, HB, HB), lambda i: (0, 0, 0))],
        out_specs=[pl.BlockSpec((7, tt, D), lambda i: (0, rev(i), 0)),
                   pl.BlockSpec((16, D), lambda i: (0, 0)),
                   pl.BlockSpec((HEADS, HB, HB), lambda i: (0, 0, 0)),
                   pl.BlockSpec((HEADS, HB, HB), lambda i: (0, 0, 0))],
        out_shape=[jax.ShapeDtypeStruct((7, t_len, D), BF16), jax.ShapeDtypeStruct((16, D), F32),
                   jax.ShapeDtypeStruct((HEADS, HB, HB), F32), jax.ShapeDtypeStruct((HEADS, HB, HB), F32)],
        scratch_shapes=[big(), big(), big(), tile(), tile(), tile(), tile(), big(), big(), tile(), tile(),
                        pltpu.VMEM((8, D), F32), pltpu.VMEM((8, D), F32)],
        compiler_params=_cp("arbitrary"),
    )(proj, proj, hl, hl, dmg, prm, wa, wx)


def _in_proj_bwd(dproj, w_in, x, dx1, mod, g_mix):
    t_len = x.shape[0]
    tm = min(TM, t_len)

    def body(dp_ref, w_ref, x_ref, dx1_ref, mod_ref, g_ref, gx_ref, sums_ref, acc):
        i, s = pl.program_id(0), pl.program_id(1)

        @pl.when((i == 0) & (s == 0))
        def _():
            sums_ref[...] = jnp.zeros_like(sums_ref)

        part = _dot_nt(dp_ref[0], w_ref[...])

        @pl.when(s == 0)
        def _():
            acc[...] = part

        @pl.when(s > 0)
        def _():
            acc[...] += part

        @pl.when(s == 6)
        def _():
            dh = acc[...]
            xv = x_ref[...]
            r = lax.rsqrt(jnp.mean(xv * xv, axis=-1, keepdims=True) + EPS)
            xn = xv * r
            gf = g_ref[...]
            sums_ref[S_SH:S_SH + 1, :] += jnp.sum(dh, axis=0, keepdims=True)
            sums_ref[S_SC:S_SC + 1, :] += jnp.sum(dh * (xn * gf), axis=0, keepdims=True)
            dhn = dh * (1.0 + mod_ref[1:2, :])
            sums_ref[S_G:S_G + 1, :] += jnp.sum(dhn * xn, axis=0, keepdims=True)
            dxn = dhn * gf
            gx_ref[...] = dx1_ref[...] + r * (dxn - xn * jnp.mean(dxn * xn, axis=-1, keepdims=True))

    return pl.pallas_call(
        body, name="in_proj_bwd", grid=(t_len // tm, 7),
        in_specs=[pl.BlockSpec((1, tm, D), lambda i, s: (s, i, 0)),
                  pl.BlockSpec((D, D), lambda i, s: (0, s)),
                  pl.BlockSpec((tm, D), lambda i, s: (i, 0)), pl.BlockSpec((tm, D), lambda i, s: (i, 0)),
                  pl.BlockSpec((8, D), lambda i, s: (0, 0)), pl.BlockSpec((1, D), lambda i, s: (0, 0))],
        out_specs=[pl.BlockSpec((tm, D), lambda i, s: (i, 0)), pl.BlockSpec((8, D), lambda i, s: (0, 0))],
        out_shape=[jax.ShapeDtypeStruct((t_len, D), F32), jax.ShapeDtypeStruct((8, D), F32)],
        scratch_shapes=[pltpu.VMEM((tm, D), F32)],
        compiler_params=_cp("arbitrary", "arbitrary"),
    )(dproj, w_in, x, dx1, mod, g_mix)


def _in_wgrad(h, dproj):
    t_len = h.shape[0]
    tk = min(TK, t_len)
    nk = t_len // tk

    def body(h_ref, d_ref, o_ref, acc):
        k = pl.program_id(1)
        part = _dot_tn(h_ref[...], d_ref[0])

        @pl.when(k == 0)
        def _():
            acc[...] = part

        @pl.when(k > 0)
        def _():
            acc[...] += part

        @pl.when(k == nk - 1)
        def _():
            o_ref[...] = acc[...].astype(BF16)

    return pl.pallas_call(
        body, name="in_wgrad", grid=(7, nk),
        in_specs=[pl.BlockSpec((tk, D), lambda s, k: (k, 0)), pl.BlockSpec((1, tk, D), lambda s, k: (s, k, 0))],
        out_specs=pl.BlockSpec((D, D), lambda s, k: (0, s)),
        out_shape=jax.ShapeDtypeStruct((D, 7 * D), BF16),
        scratch_shapes=[pltpu.VMEM((D, D), F32)],
        compiler_params=_cp("parallel", "arbitrary"),
    )(h, dproj)


def _ada_fwd(c_all, w_ada, b_cols):
    def body(c_ref, w_ref, b_ref, o_ref):
        cv = c_ref[...]
        o_ref[...] = _dot((cv * _sig(cv)).astype(BF16), w_ref[...].astype(BF16)) + b_ref[...]

    return pl.pallas_call(body, name="ada_fwd", out_shape=jax.ShapeDtypeStruct((16, w_ada.shape[1]), F32),
                          compiler_params=_cp())(c_all, w_ada, b_cols)


def _adam_math(w, g, m, v):
    m = ADAM_B1 * m + (1.0 - ADAM_B1) * g
    v = ADAM_B2 * v + (1.0 - ADAM_B2) * (g * g)
    m_hat = m / (1.0 - ADAM_B1 ** ADAM_STEP)
    v_hat = v / (1.0 - ADAM_B2 ** ADAM_STEP)
    delta = -ADAM_LR * (m_hat / (jnp.sqrt(v_hat) + ADAM_EPS) + ADAM_WD * w)
    return delta, m, v


def _ada_bwd(c_all, dmod_cols, w, m, v):
    rb = 256
    n = w.shape[1]
    nrow = c_all.shape[0]

    def body(c_ref, d_ref, w_ref, m_ref, v_ref, g_ref, dl_ref, nm_ref, nv_ref):
        cv = c_ref[...]
        g = _dot_tn((cv * _sig(cv)).astype(BF16), d_ref[...].astype(BF16))
        g_ref[...] = g
        dl_ref[...], nm_ref[...], nv_ref[...] = _adam_math(w_ref[...], g, m_ref[...], v_ref[...])

    blk = pl.BlockSpec((rb, n), lambda i: (i, 0))
    sds = jax.ShapeDtypeStruct(w.shape, F32)
    return pl.pallas_call(
        body, name="ada_bwd", grid=(D // rb,),
        in_specs=[pl.BlockSpec((nrow, rb), lambda i: (0, i)), pl.BlockSpec((nrow, n), lambda i: (0, 0)), blk, blk, blk],
        out_specs=[blk, blk, blk, blk], out_shape=[sds, sds, sds, sds],
        compiler_params=_cp("parallel"),
    )(c_all, dmod_cols, w, m, v)


def _adam(name, parts, w, m, v):
    p, r, c = parts.shape
    rb = r
    for cand in (256, 128, 64, 32, 16, 8):
        if r % cand == 0 and r >= cand:
            rb = cand
            break

    def body(p_ref, w_ref, m_ref, v_ref, g_ref, dl_ref, nm_ref, nv_ref):
        g = p_ref[0].astype(F32)
        for q in range(1, p):
            g = g + p_ref[q].astype(F32)
        g_ref[...] = g
        dl_ref[...], nm_ref[...], nv_ref[...] = _adam_math(w_ref[...], g, m_ref[...], v_ref[...])

    blk = pl.BlockSpec((rb, c), lambda i: (i, 0))
    sds = jax.ShapeDtypeStruct((r, c), F32)
    return pl.pallas_call(
        body, name=name, grid=(r // rb,),
        in_specs=[pl.BlockSpec((p, rb, c), lambda i: (0, i, 0)), blk, blk, blk],
        out_specs=[blk, blk, blk, blk], out_shape=[sds, sds, sds, sds],
        compiler_params=_cp("parallel"),
    )(parts, w, m, v)


def _my_pos():
    return lax.axis_index("x"), lax.axis_index("y"), lax.axis_index("c")


def _all_gather_small(name, v):
    m_per, n = v.shape

    def body(x_ref, out_ref, send_sems, recv_sems, local_sem):
        x, y, c = _my_pos()
        me, sibling = (x, y, c), (x, y, 1 - c)
        chips = [(1 - x, y), (x, 1 - y), (1 - x, 1 - y)]

        def rows(px, py, pc):
            return out_ref.at[pl.ds((4 * px + 2 * py + pc) * m_per, m_per), :]

        def copy(k, block, to, src=None):
            return pltpu.make_async_remote_copy(
                src_ref=rows(*block) if src is None else src, dst_ref=rows(*block),
                send_sem=send_sems.at[k], recv_sem=recv_sems.at[k], device_id=to, device_id_type=MESH)

        mine = pltpu.make_async_copy(x_ref, rows(*me), local_sem)
        mine.start()
        first = [copy(0, me, sibling, src=x_ref)]
        first += [copy(1 + j, me, (*chip, c), src=x_ref) for j, chip in enumerate(chips)]
        for cp in first:
            cp.start()
        passed = [copy(4 + j, (*chip, c), sibling) for j, chip in enumerate(chips)]
        for j, chip in enumerate(chips):
            copy(1 + j, (*chip, c), me).wait_recv()
            passed[j].start()
        copy(0, sibling, me).wait_recv()
        for j, chip in enumerate(chips):
            copy(4 + j, (*chip, 1 - c), me).wait_recv()
        for cp in first + passed:
            cp.wait_send()
        mine.wait()

    return pl.pallas_call(
        body, name=name, out_shape=jax.ShapeDtypeStruct((NDEV * m_per, n), v.dtype),
        in_specs=[pl.BlockSpec(memory_space=pltpu.VMEM)], out_specs=pl.BlockSpec(memory_space=pltpu.VMEM),
        scratch_shapes=[pltpu.SemaphoreType.DMA((7,)), pltpu.SemaphoreType.DMA((7,)), pltpu.SemaphoreType.DMA],
    )(v)


def _blk_cols(n):
    return lambda ref, b: ref.at[:, pl.ds(pl.multiple_of(b * n, 128), n)]


def _blk_rows(n):
    return lambda ref, b: ref.at[pl.ds(pl.multiple_of(b * n, 8), n), :]


def _blk_lead(ref, b):
    return ref.at[b]


def _blk_heads(ref, b):
    return ref.at[:, pl.ds(pl.multiple_of(b * (HB // NDEV), 8), HB // NDEV), :]


def _all_gather_weights(shards, fulls, slicers):
    na = len(shards)

    def body(*refs):
        ins, outs = refs[:na], refs[na:2 * na]
        send_sems, recv_sems, local_sems = refs[2 * na:]
        x, y, c = _my_pos()
        sibling = (x, y, 1 - c)
        chips = [(1 - x, y), (x, 1 - y), (1 - x, 1 - y)]

        def idx(px, py, pc):
            return 4 * px + 2 * py + pc

        def copy(a, k, block, to, from_shard=False):
            dst = slicers[a](outs[a], idx(*block))
            return pltpu.make_async_remote_copy(
                src_ref=ins[a] if from_shard else dst, dst_ref=dst,
                send_sem=send_sems.at[a * 7 + k], recv_sem=recv_sems.at[a * 7 + k], device_id=to, device_id_type=MESH)

        mine, first, passed = [], [], []
        for a in range(na):
            cp = pltpu.make_async_copy(ins[a], slicers[a](outs[a], idx(x, y, c)), local_sems.at[a])
            cp.start()
            mine.append(cp)
        for a in range(na):
            cps = [copy(a, 0, (x, y, c), sibling, True)]
            cps += [copy(a, 1 + j, (x, y, c), (*chip, c), True) for j, chip in enumerate(chips)]
            for cp in cps:
                cp.start()
            first += cps
        for a in range(na):
            for j, chip in enumerate(chips):
                copy(a, 1 + j, (*chip, c), (x, y, c)).wait_recv()
                cp = copy(a, 4 + j, (*chip, c), sibling)
                cp.start()
                passed.append(cp)
        for a in range(na):
            copy(a, 0, sibling, (x, y, c)).wait_recv()
            for j, chip in enumerate(chips):
                copy(a, 4 + j, (*chip, 1 - c), (x, y, c)).wait_recv()
        for cp in first + passed:
            cp.wait_send()
        for cp in mine:
            cp.wait()

    any_spec = pl.BlockSpec(memory_space=pl.ANY)
    return pl.pallas_call(
        body, name="gather_weights",
        out_shape=[jax.ShapeDtypeStruct(s, sh.dtype) for s, sh in zip(fulls, shards)],
        in_specs=[any_spec] * na, out_specs=[any_spec] * na,
        scratch_shapes=[pltpu.SemaphoreType.DMA((7 * na,)), pltpu.SemaphoreType.DMA((7 * na,)),
                        pltpu.SemaphoreType.DMA((na,))],
    )(*shards)


def _scatter_grads(grads, shard_shapes, slicers):
    na = len(grads)

    def body(*refs):
        ins, outs = refs[:na], refs[na:2 * na]
        send_sems, recv_sems, local_sems = refs[2 * na:]
        x, y, c = _my_pos()
        me = 4 * x + 2 * y + c
        mine, sent = [], []
        for a in range(na):
            cp = pltpu.make_async_copy(slicers[a](ins[a], me), outs[a].at[me], local_sems.at[a])
            cp.start()
            mine.append(cp)
        rel = [(k >> 2 & 1, k >> 1 & 1, k & 1) for k in range(1, NDEV)]
        for a in range(na):
            for k, (fx, fy, fc) in enumerate(rel):
                px, py, pc = x ^ fx, y ^ fy, c ^ fc
                cp = pltpu.make_async_remote_copy(
                    src_ref=slicers[a](ins[a], 4 * px + 2 * py + pc), dst_ref=outs[a].at[me],
                    send_sem=send_sems.at[a * 7 + k], recv_sem=recv_sems.at[a * 7 + k],
                    device_id=(px, py, pc), device_id_type=MESH)
                cp.start()
                sent.append(cp)
        for a in range(na):
            for k, (fx, fy, fc) in enumerate(rel):
                px, py, pc = x ^ fx, y ^ fy, c ^ fc
                src = 4 * px + 2 * py + pc
                pltpu.make_async_remote_copy(
                    src_ref=slicers[a](ins[a], me), dst_ref=outs[a].at[src],
                    send_sem=send_sems.at[a * 7 + k], recv_sem=recv_sems.at[a * 7 + k],
                    device_id=(px, py, pc), device_id_type=MESH).wait_recv()
        for cp in sent:
            cp.wait_send()
        for cp in mine:
            cp.wait()

    any_spec = pl.BlockSpec(memory_space=pl.ANY)
    return pl.pallas_call(
        body, name="scatter_grads",
        out_shape=[jax.ShapeDtypeStruct((NDEV,) + tuple(s), g.dtype) for s, g in zip(shard_shapes, grads)],
        in_specs=[any_spec] * na, out_specs=[any_spec] * na,
        scratch_shapes=[pltpu.SemaphoreType.DMA((7 * na,)), pltpu.SemaphoreType.DMA((7 * na,)),
                        pltpu.SemaphoreType.DMA((na,))],
    )(*grads)


def _local_step(x, target, mod, g_mix, g_ffn, g_fin, prm, w_in, wa, wx, w_out, w_gu, w_down):
    proj, h = _in_proj(x, mod, g_mix, w_in)
    merged, hl = _mixer_fwd(proj, prm, wa, wx)
    x1, h2 = _out_proj(merged, x, mod, g_ffn, w_out)
    gu, dx2, loss, d_gfin = _ffn_fwd(h2, x1, target, mod, g_fin, w_gu, w_down)
    dgu, act, dx1, dmg, sums2 = _ffn_bwd(dx2, gu, x1, mod, g_ffn, w_gu, w_down, w_out)
    g_wgu = _gu_wgrad(h2, dgu)
    g_wdown, d_gt2 = _scaled_wgrad("down_wgrad", act, dx2, w_down, 5, mod)
    g_wout, d_gt1 = _scaled_wgrad("out_wgrad", merged.reshape(1, *merged.shape), dx1, w_out, 2, mod)
    dproj, msums, g_wa, g_wx = _mixer_bwd(proj, hl, dmg, prm, wa, wx)
    grad_x, sums1 = _in_proj_bwd(dproj, w_in, x, dx1, mod, g_mix)
    g_win = _in_wgrad(h, dproj)
    return dict(loss=loss, grad_x=grad_x, d_gfin=d_gfin, sums1=sums1, sums2=sums2, msums=msums,
                d_gt1=d_gt1, d_gt2=d_gt2, g_win=g_win, g_wa=g_wa, g_wx=g_wx, g_wout=g_wout, g_wgu=g_wgu,
                g_wdown=g_wdown)


def kernel(x, c, w_ada, b_ada, g_norm_mix, w_in, conv_a_w, conv_b_w, conv_b_bias, w_rg_a, b_rg_a, w_rg_x, b_rg_x, lru_lambda, w_out, g_norm_ffn, w_gate_up, w_down, g_norm_final, loss_target, m_w_ada, m_b_ada, m_g_norm_mix, m_w_in, m_conv_a_w, m_conv_b_w, m_conv_b_bias, m_w_rg_a, m_b_rg_a, m_w_rg_x, m_b_rg_x, m_lru_lambda, m_w_out, m_g_norm_ffn, m_w_gate_up, m_w_down, m_g_norm_final, v_w_ada, v_b_ada, v_g_norm_mix, v_w_in, v_conv_a_w, v_conv_b_w, v_conv_b_bias, v_w_rg_a, v_b_rg_a, v_w_rg_x, v_b_rg_x, v_lru_lambda, v_w_out, v_g_norm_ffn, v_w_gate_up, v_w_down, v_g_norm_final):
    me = 4 * lax.axis_index("x") + 2 * lax.axis_index("y") + lax.axis_index("c")
    ncol = w_ada.shape[2]
    cw = conv_a_w.shape[2]

    pack0 = jnp.concatenate([c, conv_a_w.reshape(1, 3 * cw), conv_b_w.reshape(1, 4 * cw)], axis=1)
    got0 = _all_gather_small("gather_c", jnp.broadcast_to(pack0, (8, pack0.shape[1])))
    got0 = got0.reshape(NDEV, 8, -1)[:, 0, :]
    c_all = got0[:, :D]
    conv_a = got0[:, D:D + 3 * cw].reshape(NDEV, 3, cw).transpose(1, 0, 2).reshape(3, D)
    conv_b = got0[:, D + 3 * cw:].reshape(NDEV, 4, cw).transpose(1, 0, 2).reshape(4, D)

    b_cols = lax.dynamic_slice_in_dim(b_ada, me * ncol, ncol, axis=1)
    c16 = jnp.concatenate([c_all, jnp.zeros((8, D), F32)], axis=0)
    mod_cols = _ada_fwd(c16, w_ada[0], b_cols)[:NDEV]
    got1 = _all_gather_small("gather_mod", mod_cols).reshape(NDEV, NDEV, ncol)
    mod6 = lax.dynamic_index_in_dim(got1, me, axis=1, keepdims=False).reshape(6, D)
    mod = jnp.concatenate([mod6, jnp.zeros((2, D), F32)], axis=0)

    shards = [w_in[0].astype(BF16), w_rg_a[0].astype(BF16), w_rg_x[0].astype(BF16), w_out[0].astype(BF16),
              w_gate_up[0].astype(BF16), w_down[0].astype(BF16)]
    fulls = [(D, 7 * D), (HEADS, HB, HB), (HEADS, HB, HB), (D, D), (NDEV, D, FB), (DFF, D)]
    slicers = [_blk_cols(7 * D // NDEV), _blk_heads, _blk_heads, _blk_rows(D // NDEV), _blk_lead,
               _blk_rows(DFF // NDEV)]
    w_in_f, wa_f, wx_f, w_out_f, w_gu_f, w_down_f = _all_gather_weights(shards, fulls, slicers)

    prm = jnp.concatenate([conv_a, conv_b, conv_b_bias, b_rg_a, b_rg_x, lru_lambda, jnp.zeros((5, D), F32)], axis=0)
    r = _local_step(x[0], loss_target[0], mod, g_norm_mix, g_norm_ffn, g_norm_final.reshape(1, D), prm,
                    w_in_f, wa_f, wx_f, w_out_f, w_gu_f.reshape(2, 4, D, FB), w_down_f)

    grads = [r["g_win"], r["g_wa"], r["g_wx"], r["g_wout"], r["g_wgu"], r["g_wdown"]]
    shard_shapes = [s.shape for s in shards]
    parts = _scatter_grads(grads, shard_shapes, slicers)
    big = {}
    for nm, p, w, m, v in (("w_in", parts[0], w_in, m_w_in, v_w_in), ("w_rg_a", parts[1], w_rg_a, m_w_rg_a, v_w_rg_a),
                           ("w_rg_x", parts[2], w_rg_x, m_w_rg_x, v_w_rg_x), ("w_out", parts[3], w_out, m_w_out, v_w_out),
                           ("w_gate_up", parts[4], w_gate_up, m_w_gate_up, v_w_gate_up),
                           ("w_down", parts[5], w_down, m_w_down, v_w_down)):
        two_d = (-1, w.shape[-1])
        outs = _adam("adam_" + nm, p.reshape((NDEV,) + w.reshape(two_d).shape), w.reshape(two_d), m.reshape(two_d),
                     v.reshape(two_d))
        big[nm] = [o.reshape(w.shape) for o in outs]

    small = jnp.concatenate([
        r["sums1"][S_SH:S_SH + 1], r["sums1"][S_SC:S_SC + 1], r["d_gt1"],
        r["sums2"][S_SH:S_SH + 1], r["sums2"][S_SC:S_SC + 1], r["d_gt2"],
        r["sums1"][S_G:S_G + 1],
        r["msums"][M_CBIAS:M_CBIAS + 1], r["msums"][M_BA:M_BA + 1], r["msums"][M_BX:M_BX + 1],
        r["msums"][M_LS:M_LS + 1],
        r["sums2"][S_G:S_G + 1], r["d_gfin"],
        r["msums"][M_WA:M_WA + 3], r["msums"][M_WB:M_WB + 4],
        jnp.zeros((4, D), F32)], axis=0)
    got2 = _all_gather_small("gather_small", small).reshape(NDEV, 24, D)

    rep_w = jnp.concatenate([b_ada.reshape(6, D), g_norm_mix, conv_b_bias, b_rg_a, b_rg_x, lru_lambda, g_norm_ffn,
                             g_norm_final.reshape(1, D), jnp.zeros((3, D), F32)], axis=0)
    rep_m = jnp.concatenate([m_b_ada.reshape(6, D), m_g_norm_mix, m_conv_b_bias, m_b_rg_a, m_b_rg_x, m_lru_lambda,
                             m_g_norm_ffn, m_g_norm_final.reshape(1, D), jnp.zeros((3, D), F32)], axis=0)
    rep_v = jnp.concatenate([v_b_ada.reshape(6, D), v_g_norm_mix, v_conv_b_bias, v_b_rg_a, v_b_rg_x, v_lru_lambda,
                             v_g_norm_ffn, v_g_norm_final.reshape(1, D), jnp.ones((3, D), F32)], axis=0)
    rep = _adam("adam_rep", got2[:, :16, :], rep_w, rep_m, rep_v)

    conv_parts = lax.dynamic_slice_in_dim(got2[:, 13:21, :], me * cw, cw, axis=2)
    cv_w = jnp.concatenate([conv_a_w[0], conv_b_w[0], jnp.zeros((1, cw), F32)], axis=0)
    cv_m = jnp.concatenate([m_conv_a_w[0], m_conv_b_w[0], jnp.zeros((1, cw), F32)], axis=0)
    cv_v = jnp.concatenate([v_conv_a_w[0], v_conv_b_w[0], jnp.ones((1, cw), F32)], axis=0)
    cvo = _adam("adam_conv", conv_parts, cv_w, cv_m, cv_v)

    dmod_cols = lax.dynamic_slice_in_dim(got2[:, :6, :].reshape(NDEV, 6 * D), me * ncol, ncol, axis=1)
    dmod16 = jnp.concatenate([dmod_cols, jnp.zeros((8, ncol), F32)], axis=0)
    ada = _ada_bwd(c16, dmod16, w_ada[0], m_w_ada[0], v_w_ada[0])

    loss = lax.psum(r["loss"][0, 0], AXES)

    def pick(q):
        one = lambda i: rep[q][i:i + 1]
        return [ada[q].reshape(w_ada.shape), rep[q][0:6].reshape(b_ada.shape), one(6), big["w_in"][q],
                cvo[q][0:3].reshape(conv_a_w.shape), cvo[q][3:7].reshape(conv_b_w.shape), one(7),
                big["w_rg_a"][q], one(8), big["w_rg_x"][q], one(9), one(10), big["w_out"][q], one(11),
                big["w_gate_up"][q], big["w_down"][q], rep[q][12]]

    return (loss, r["grad_x"].reshape(x.shape), *pick(0), *pick(1), *pick(2), *pick(3))
```

```python
import functools
import math

import jax
import jax.numpy as jnp
from jax import lax
from jax.experimental import pallas as pl
from jax.experimental.pallas import tpu as pltpu

F32 = jnp.float32
BF16 = jnp.bfloat16

D = 1024
DFF = 2816
NDEV = 8
HEADS = 4
HB = D // HEADS
FB = DFF // 4
EPS = 1e-6
LRU_C = 8.0
ADAM_LR, ADAM_B1, ADAM_B2, ADAM_EPS, ADAM_WD, ADAM_STEP = 0.001, 0.9, 0.999, 1e-08, 0.01, 10

VMEM_LIMIT = 56 * 1024 * 1024
TM = 512
TMI = 1024
TK = 2048
TKI = 1024
TT = 256
CG = 256
MESH = pl.DeviceIdType.MESH
AXES = ("x", "y", "c")


def _cp(*sem):
    return pltpu.CompilerParams(dimension_semantics=sem, vmem_limit_bytes=VMEM_LIMIT)


def _sig(x):
    return 1.0 / (1.0 + jnp.exp(-x))


def _log_sigmoid(x):
    z = jnp.exp(-jnp.abs(x))
    u = 1.0 + z
    d = u - 1.0
    l1p = jnp.where(d == 0.0, z, jnp.log(u) * (z / jnp.where(d == 0.0, 1.0, d)))
    return -(jnp.maximum(-x, 0.0) + l1p)


def _neg_expm1(x):
    p = x * (1.0 + x * 0.5 * (1.0 + x * (1.0 / 3.0) * (1.0 + x * 0.25 * (1.0 + x * 0.2 * (1.0 + x * (1.0 / 6.0))))))
    return jnp.where(x > -0.25, -p, 1.0 - jnp.exp(x))


_GC = math.sqrt(2.0 / math.pi)


def _gelu(x):
    t = jnp.tanh(_GC * (x + 0.044715 * x * x * x))
    return 0.5 * x * (1.0 + t), t


def _dot(a, b):
    return jnp.dot(a, b, preferred_element_type=F32)


def _dot_nt(a, b):
    return lax.dot_general(a, b, (((1,), (1,)), ((), ())), preferred_element_type=F32)


def _dot_tn(a, b):
    return lax.dot_general(a, b, (((0,), (0,)), ((), ())), preferred_element_type=F32)


def _in_proj(x, mod, g_mix, w_in, shards, fulls, slicers):
    t_len = x.shape[0]
    tm = min(TMI, t_len)
    ni = t_len // tm
    na = len(shards)
    rc = 32

    def body(x_ref, mod_ref, g_ref, w_ref, *rest):
        ins, (proj_ref, h_ref), outs = rest[:na], rest[na:na + 2], rest[na + 2:2 * na + 2]
        h_scr = rest[2 * na + 2]
        start, forward, finish = _ag_phases(ins, outs, slicers, *rest[2 * na + 3:])
        i, s = pl.program_id(0), pl.program_id(1)

        @pl.when((i == 0) & (s == 0))
        def _():
            start()

        @pl.when((i == ni // 2) & (s == 0))
        def _():
            forward()

        @pl.when(s == 0)
        def _():
            gs = g_ref[...] * (1.0 + mod_ref[1:2, :])
            sh = mod_ref[0:1, :]

            def chunk(i, carry):
                rows = pl.ds(pl.multiple_of(i * rc, rc), rc)
                xv = x_ref[rows, :]
                r = lax.rsqrt(jnp.mean(xv * xv, axis=-1, keepdims=True) + EPS)
                h = (xv * r * gs + sh).astype(BF16)
                h_scr[rows, :] = h
                h_ref[rows, :] = h
                return carry

            lax.fori_loop(0, tm // rc, chunk, 0)

        proj_ref[0] = _dot(h_scr[...], w_ref[...]).astype(BF16)

        @pl.when((i == ni - 1) & (s == 6))
        def _():
            finish()

    res = pl.pallas_call(
        body, name="in_proj", grid=(ni, 7),
        in_specs=[pl.BlockSpec((tm, D), lambda i, s: (i, 0)),
                  pl.BlockSpec((8, D), lambda i, s: (0, 0)),
                  pl.BlockSpec((1, D), lambda i, s: (0, 0)),
                  pl.BlockSpec((D, D), lambda i, s: (0, s))] + [_ANY] * na,
        out_specs=[pl.BlockSpec((1, tm, D), lambda i, s: (s, i, 0)),
                   pl.BlockSpec((tm, D), lambda i, s: (i, 0))] + [_ANY] * na,
        out_shape=[jax.ShapeDtypeStruct((7, t_len, D), BF16), jax.ShapeDtypeStruct((t_len, D), BF16)]
        + [jax.ShapeDtypeStruct(f, sh.dtype) for f, sh in zip(fulls, shards)],
        scratch_shapes=[pltpu.VMEM((tm, D), BF16)] + _ag_sems(na),
        compiler_params=_cp("arbitrary", "arbitrary"),
    )(x, mod, g_mix, w_in, *shards)
    return res[0], res[1], res[2:]


P_WA, P_WB, P_CBIAS, P_BA, P_BX, P_LAM = 0, 3, 7, 8, 9, 10


def _lru_gates(rp, ip, ls, first_row):
    r = _sig(rp)
    ig = _sig(ip)
    la = LRU_C * r * ls
    a = jnp.exp(la)
    m2 = _neg_expm1(2.0 * la)
    mult = jnp.where(first_row, 1.0, jnp.sqrt(jnp.maximum(m2, 0.0)))
    return r, ig, la, a, m2, mult


def _mixer_fwd(proj, prm, wa, wx):
    t_len = proj.shape[1]
    tt = min(TT, t_len)

    def body(proj_ref, prm_ref, wa_ref, wx_ref, mg_ref, hl_ref, xe, ve, hc, u_s, ya_s, rp_s, ip_s):
        t = pl.program_id(0)

        @pl.when(t == 0)
        def _():
            xe[0:8, :] = jnp.zeros((8, D), F32)
            ve[0:8, :] = jnp.zeros((8, D), F32)
            hc[...] = jnp.zeros((8, D), F32)

        xe[8:8 + tt, :] = proj_ref[3].astype(F32)
        ve[8:8 + tt, :] = proj_ref[1].astype(F32) * proj_ref[2].astype(F32)
        u = prm_ref[P_CBIAS:P_CBIAS + 1, :] + prm_ref[P_WB:P_WB + 1, :] * xe[5:5 + tt, :]
        for k in range(1, 4):
            u = u + prm_ref[P_WB + k:P_WB + k + 1, :] * xe[5 + k:5 + k + tt, :]
        u_s[...] = u
        ya = prm_ref[P_WA:P_WA + 1, :] * ve[6:6 + tt, :]
        for k in range(1, 3):
            ya = ya + prm_ref[P_WA + k:P_WA + k + 1, :] * ve[6 + k:6 + k + tt, :]
        ya_s[...] = ya
        xe[0:8, :] = xe[tt:tt + 8, :]
        ve[0:8, :] = ve[tt:tt + 8, :]

        ub = u.astype(BF16)
        for h in range(HEADS):
            cs = slice(h * HB, (h + 1) * HB)
            rp_s[:, cs] = _dot(ub[:, cs], wa_ref[h]) + prm_ref[P_BA:P_BA + 1, cs]
            ip_s[:, cs] = _dot(ub[:, cs], wx_ref[h]) + prm_ref[P_BX:P_BX + 1, cs]

        ls_all = _log_sigmoid(prm_ref[P_LAM:P_LAM + 1, :])
        row = lax.broadcasted_iota(jnp.int32, (8, CG), 0)

        def blk(i, carry):
            r0 = pl.multiple_of(i * 16, 16)
            for g in range(D // CG):
                cs = slice(g * CG, (g + 1) * CG)
                ls = ls_all[:, cs]
                hprev = hc[:, cs]
                hs = []
                for sb in range(2):
                    rr = r0 + 8 * sb
                    first = (row + (t * tt + rr)) == 0
                    _, ig, _, a, _, mult = _lru_gates(rp_s[pl.ds(rr, 8), cs], ip_s[pl.ds(rr, 8), cs], ls, first)
                    b = mult * (ig * u_s[pl.ds(rr, 8), cs])
                    for s in (1, 2, 4):
                        a_sh = jnp.where(row >= s, pltpu.roll(a, s, 0), 1.0)
                        b_sh = jnp.where(row >= s, pltpu.roll(b, s, 0), 0.0)
                        b = a * b_sh + b
                        a = a * a_sh
                    hv = a * hprev + b
                    hprev = jnp.broadcast_to(hv[7:8, :], hv.shape)
                    hs.append(hv)
                hc[:, cs] = hprev
                h16 = jnp.concatenate(hs, axis=0)
                rows = pl.ds(r0, 16)
                gl, _ = _gelu(proj_ref[4, rows, cs].astype(F32))
                y_b = h16 * gl
                y_a = proj_ref[0, rows, cs].astype(F32) * ya_s[rows, cs]
                mg = _sig(proj_ref[5, rows, cs].astype(F32)) * y_a + _sig(proj_ref[6, rows, cs].astype(F32)) * y_b
                mg_ref[rows, cs] = mg.astype(BF16)
                hl_ref[rows, cs] = h16.astype(BF16)
            return carry

        lax.fori_loop(0, tt // 16, blk, 0)

    return pl.pallas_call(
        body, name="mixer_fwd", grid=(t_len // tt,),
        in_specs=[pl.BlockSpec((7, tt, D), lambda t: (0, t, 0)),
                  pl.BlockSpec((16, D), lambda t: (0, 0)),
                  pl.BlockSpec((HEADS, HB, HB), lambda t: (0, 0, 0)),
                  pl.BlockSpec((HEADS, HB, HB), lambda t: (0, 0, 0))],
        out_specs=[pl.BlockSpec((tt, D), lambda t: (t, 0)), pl.BlockSpec((tt, D), lambda t: (t, 0))],
        out_shape=[jax.ShapeDtypeStruct((t_len, D), BF16), jax.ShapeDtypeStruct((t_len, D), BF16)],
        scratch_shapes=[pltpu.VMEM((tt + 8, D), F32), pltpu.VMEM((tt + 8, D), F32), pltpu.VMEM((8, D), F32),
                        pltpu.VMEM((tt, D), F32), pltpu.VMEM((tt, D), F32), pltpu.VMEM((tt, D), F32),
                        pltpu.VMEM((tt, D), F32)],
        compiler_params=_cp("arbitrary"),
    )(proj, prm, wa, wx)


def _out_proj(merged, x, mod, g_ffn, w_out):
    t_len = x.shape[0]
    tm = min(TM, t_len)

    def body(mg_ref, x_ref, mod_ref, g_ref, w_ref, x1_ref, h2_ref):
        x1 = x_ref[...] + mod_ref[2:3, :] * _dot(mg_ref[...], w_ref[...])
        x1_ref[...] = x1
        r = lax.rsqrt(jnp.mean(x1 * x1, axis=-1, keepdims=True) + EPS)
        h2_ref[...] = (x1 * r * g_ref[...] * (1.0 + mod_ref[4:5, :]) + mod_ref[3:4, :]).astype(BF16)

    return pl.pallas_call(
        body, name="out_proj", grid=(t_len // tm,),
        in_specs=[pl.BlockSpec((tm, D), lambda i: (i, 0)), pl.BlockSpec((tm, D), lambda i: (i, 0)),
                  pl.BlockSpec((8, D), lambda i: (0, 0)), pl.BlockSpec((1, D), lambda i: (0, 0)),
                  pl.BlockSpec((D, D), lambda i: (0, 0))],
        out_specs=[pl.BlockSpec((tm, D), lambda i: (i, 0)), pl.BlockSpec((tm, D), lambda i: (i, 0))],
        out_shape=[jax.ShapeDtypeStruct((t_len, D), F32), jax.ShapeDtypeStruct((t_len, D), BF16)],
        compiler_params=_cp("parallel"),
    )(merged, x, mod, g_ffn, w_out)


def _ffn_fwd(h2, x1, target, mod, g_fin, w_gu, w_down):
    t_len = x1.shape[0]
    tm = min(TM, t_len)

    def body(h2_ref, x1_ref, tg_ref, mod_ref, g_ref, wgu_ref, wd_ref, gu_ref, dx2_ref, dx2b_ref, loss_ref, dg_ref, acc):
        i, j = pl.program_id(0), pl.program_id(1)

        @pl.when((i == 0) & (j == 0))
        def _():
            loss_ref[...] = jnp.zeros_like(loss_ref)
            dg_ref[...] = jnp.zeros_like(dg_ref)

        hb = h2_ref[...]
        gate = _dot(hb, wgu_ref[0, 0])
        up = _dot(hb, wgu_ref[1, 0])
        gu_ref[0, 0] = gate.astype(BF16)
        gu_ref[1, 0] = up.astype(BF16)
        act = (gate * _sig(gate) * up).astype(BF16)
        part = _dot(act, wd_ref[...])

        @pl.when(j == 0)
        def _():
            acc[...] = part

        @pl.when(j > 0)
        def _():
            acc[...] += part

        @pl.when(j == 3)
        def _():
            x2 = x1_ref[...] + mod_ref[5:6, :] * acc[...]
            r = lax.rsqrt(jnp.mean(x2 * x2, axis=-1, keepdims=True) + EPS)
            xn = x2 * r
            gf = g_ref[...]
            diff = xn * gf - tg_ref[...]
            loss_ref[...] += jnp.sum(diff * diff) * (0.5 / D)
            dy = diff * (1.0 / D)
            dg_ref[...] += jnp.sum(dy * xn, axis=0, keepdims=True)
            dxn = dy * gf
            dx2 = r * (dxn - xn * jnp.mean(dxn * xn, axis=-1, keepdims=True))
            dx2_ref[...] = dx2
            dx2b_ref[...] = dx2.astype(BF16)

    return pl.pallas_call(
        body, name="ffn_fwd", grid=(t_len // tm, 4),
        in_specs=[pl.BlockSpec((tm, D), lambda i, j: (i, 0)), pl.BlockSpec((tm, D), lambda i, j: (i, 0)),
                  pl.BlockSpec((tm, D), lambda i, j: (i, 0)), pl.BlockSpec((8, D), lambda i, j: (0, 0)),
                  pl.BlockSpec((1, D), lambda i, j: (0, 0)),
                  pl.BlockSpec((2, 1, D, FB), lambda i, j: (0, j, 0, 0)),
                  pl.BlockSpec((FB, D), lambda i, j: (j, 0))],
        out_specs=[pl.BlockSpec((2, 1, tm, FB), lambda i, j: (0, j, i, 0)),
                   pl.BlockSpec((tm, D), lambda i, j: (i, 0)),
                   pl.BlockSpec((tm, D), lambda i, j: (i, 0)),
                   pl.BlockSpec((1, 128), lambda i, j: (0, 0)),
                   pl.BlockSpec((1, D), lambda i, j: (0, 0))],
        out_shape=[jax.ShapeDtypeStruct((2, 4, t_len, FB), BF16), jax.ShapeDtypeStruct((t_len, D), F32),
                   jax.ShapeDtypeStruct((t_len, D), BF16),
                   jax.ShapeDtypeStruct((1, 128), F32), jax.ShapeDtypeStruct((1, D), F32)],
        scratch_shapes=[pltpu.VMEM((tm, D), F32)],
        compiler_params=_cp("arbitrary", "arbitrary"),
    )(h2, x1, target, mod, g_fin, w_gu, w_down)


S_SH, S_SC, S_G = 0, 1, 2


def _ffn_bwd(dx2, gu, x1, mod, g_ffn, w_gu, w_down, w_out):
    t_len = x1.shape[0]
    tm = min(TM, t_len)

    def body(dx2_ref, gu_ref, x1_ref, mod_ref, g_ref, wgu_ref, wd_ref, wo_ref,
             dgu_ref, act_ref, dx1_ref, dx1b_ref, dmg_ref, sums_ref, acc, dffn):
        i, j = pl.program_id(0), pl.program_id(1)

        @pl.when((i == 0) & (j == 0))
        def _():
            sums_ref[...] = jnp.zeros_like(sums_ref)

        @pl.when(j == 0)
        def _():
            dffn[...] = (dx2_ref[...] * mod_ref[5:6, :]).astype(BF16)

        dact = _dot_nt(dffn[...], wd_ref[...])
        gate = gu_ref[0, 0].astype(F32)
        up = gu_ref[1, 0].astype(F32)
        sg = _sig(gate)
        silu = gate * sg
        act_ref[0] = (silu * up).astype(BF16)
        dgate = (dact * up * (sg * (1.0 + gate * (1.0 - sg)))).astype(BF16)
        dup = (dact * silu).astype(BF16)
        dgu_ref[0, 0] = dgate
        dgu_ref[1, 0] = dup
        part = _dot_nt(dgate, wgu_ref[0, 0]) + _dot_nt(dup, wgu_ref[1, 0])

        @pl.when(j == 0)
        def _():
            acc[...] = part

        @pl.when(j > 0)
        def _():
            acc[...] += part

        @pl.when(j == 3)
        def _():
            dh2 = acc[...]
            x1 = x1_ref[...]
            r = lax.rsqrt(jnp.mean(x1 * x1, axis=-1, keepdims=True) + EPS)
            xn = x1 * r
            gf = g_ref[...]
            sums_ref[S_SH:S_SH + 1, :] += jnp.sum(dh2, axis=0, keepdims=True)
            sums_ref[S_SC:S_SC + 1, :] += jnp.sum(dh2 * (xn * gf), axis=0, keepdims=True)
            dhn = dh2 * (1.0 + mod_ref[4:5, :])
            sums_ref[S_G:S_G + 1, :] += jnp.sum(dhn * xn, axis=0, keepdims=True)
            dxn = dhn * gf
            dx1 = dx2_ref[...] + r * (dxn - xn * jnp.mean(dxn * xn, axis=-1, keepdims=True))
            dx1_ref[...] = dx1
            dx1b_ref[...] = dx1.astype(BF16)
            dmg_ref[...] =_dot_nt((dx1 * mod_ref[2:3, :]).astype(BF16), wo_ref[...]).astype(BF16)

    return pl.pallas_call(
        body, name="ffn_bwd", grid=(t_len // tm, 4),
        in_specs=[pl.BlockSpec((tm, D), lambda i, j: (i, 0)),
                  pl.BlockSpec((2, 1, tm, FB), lambda i, j: (0, j, i, 0)),
                  pl.BlockSpec((tm, D), lambda i, j: (i, 0)),
                  pl.BlockSpec((8, D), lambda i, j: (0, 0)), pl.BlockSpec((1, D), lambda i, j: (0, 0)),
                  pl.BlockSpec((2, 1, D, FB), lambda i, j: (0, j, 0, 0)),
                  pl.BlockSpec((FB, D), lambda i, j: (j, 0)),
                  pl.BlockSpec((D, D), lambda i, j: (0, 0))],
        out_specs=[pl.BlockSpec((2, 1, tm, FB), lambda i, j: (0, j, i, 0)),
                   pl.BlockSpec((1, tm, FB), lambda i, j: (j, i, 0)),
                   pl.BlockSpec((tm, D), lambda i, j: (i, 0)),
                   pl.BlockSpec((tm, D), lambda i, j: (i, 0)),
                   pl.BlockSpec((tm, D), lambda i, j: (i, 0)),
                   pl.BlockSpec((8, D), lambda i, j: (0, 0))],
        out_shape=[jax.ShapeDtypeStruct((2, 4, t_len, FB), BF16), jax.ShapeDtypeStruct((4, t_len, FB), BF16),
                   jax.ShapeDtypeStruct((t_len, D), F32), jax.ShapeDtypeStruct((t_len, D), BF16),
                   jax.ShapeDtypeStruct((t_len, D), BF16), jax.ShapeDtypeStruct((8, D), F32)],
        scratch_shapes=[pltpu.VMEM((tm, D), F32), pltpu.VMEM((tm, D), BF16)],
        compiler_params=_cp("arbitrary", "arbitrary"),
    )(dx2, gu, x1, mod, g_ffn, w_gu, w_down, w_out)


def _my_pos():
    return lax.axis_index("x"), lax.axis_index("y"), lax.axis_index("c")


def _my_index():
    x, y, c = _my_pos()
    return 4 * x + 2 * y + c


def _device_of(b):
    return (b >> 2) & 1, (b >> 1) & 1, b & 1


def _rs_send(src, parts_ref, b, send_sems, recv_sems, local_sem):
    me = _my_index()
    dst = parts_ref.at[me]

    @pl.when(b == me)
    def _():
        pltpu.make_async_copy(src, dst, local_sem).start()

    @pl.when(b != me)
    def _():
        pltpu.make_async_remote_copy(src_ref=src, dst_ref=dst, send_sem=send_sems.at[b], recv_sem=recv_sems.at[me],
                                     device_id=_device_of(b), device_id_type=MESH).start()


def _rs_finish(src_of, parts_ref, send_sems, recv_sems, local_sem):
    me = _my_index()
    for s in range(NDEV):
        @pl.when(s != me)
        def _():
            cp = pltpu.make_async_remote_copy(src_ref=src_of(s), dst_ref=parts_ref.at[s], send_sem=send_sems.at[s],
                                              recv_sem=recv_sems.at[s], device_id=_device_of(s), device_id_type=MESH)
            cp.wait_send()
            cp.wait_recv()

        @pl.when(s == me)
        def _():
            pltpu.make_async_copy(src_of(s), parts_ref.at[s], local_sem).wait()


_RS_SEMS = [pltpu.SemaphoreType.DMA((NDEV,)), pltpu.SemaphoreType.DMA((NDEV,)), pltpu.SemaphoreType.DMA]
_ANY = pl.BlockSpec(memory_space=pl.ANY)


def _gu_wgrad(h2, dgu):
    t_len = h2.shape[0]
    tk = min(TK, t_len)
    nk = t_len // tk

    def body(h_ref, d_ref, parts_ref, acc, stage, send_sems, recv_sems, local_sem):
        a, j, k = pl.program_id(0), pl.program_id(1), pl.program_id(2)
        part = _dot_tn(h_ref[...], d_ref[0, 0])

        @pl.when(k == 0)
        def _():
            acc[...] = part

        @pl.when(k > 0)
        def _():
            acc[...] += part

        @pl.when(k == nk - 1)
        def _():
            b = a * 4 + j
            stage[b] = acc[...].astype(BF16)
            _rs_send(stage.at[b], parts_ref, b, send_sems, recv_sems, local_sem)

        @pl.when((a == 1) & (j == 3) & (k == nk - 1))
        def _():
            _rs_finish(lambda s: stage.at[s], parts_ref, send_sems, recv_sems, local_sem)

    return pl.pallas_call(
        body, name="gu_wgrad", grid=(2, 4, nk),
        in_specs=[pl.BlockSpec((tk, D), lambda a, j, k: (k, 0)),
                  pl.BlockSpec((1, 1, tk, FB), lambda a, j, k: (a, j, k, 0))],
        out_specs=_ANY,
        out_shape=jax.ShapeDtypeStruct((NDEV, D, FB), BF16),
        scratch_shapes=[pltpu.VMEM((D, FB), F32), pltpu.VMEM((NDEV, D, FB), BF16)] + _RS_SEMS,
        compiler_params=_cp("arbitrary", "arbitrary", "arbitrary"),
    )(h2, dgu)


def _scaled_wgrad(name, a, dx, w, gate_row, mod):
    nb, t_len, kb = a.shape
    tk = min(TK, t_len)
    nk = t_len // tk
    per = NDEV // nb
    rows = kb // per

    def body(a_ref, dx_ref, w_ref, mod_ref, parts_ref, dg_ref, acc, stage, send_sems, recv_sems, local_sem):
        j, k = pl.program_id(0), pl.program_id(1)

        @pl.when((j == 0) & (k == 0))
        def _():
            dg_ref[...] = jnp.zeros_like(dg_ref)

        part = _dot_tn(a_ref[0], dx_ref[...])

        @pl.when(k == 0)
        def _():
            acc[...] = part

        @pl.when(k > 0)
        def _():
            acc[...] += part

        def src_of(blk, q):
            return stage.at[blk, pl.ds(q * rows, rows), :]

        @pl.when(k == nk - 1)
        def _():
            z = acc[...]
            stage[j] = (z * mod_ref[gate_row:gate_row + 1, :]).astype(BF16)
            dg_ref[0:1, :] += jnp.sum(z * w_ref[...].astype(F32), axis=0, keepdims=True)
            for q in range(per):
                _rs_send(src_of(j, q), parts_ref, j * per + q, send_sems, recv_sems, local_sem)

        @pl.when((j == nb - 1) & (k == nk - 1))
        def _():
            _rs_finish(lambda s: src_of(s // per, s % per), parts_ref, send_sems, recv_sems, local_sem)

    return pl.pallas_call(
        body, name=name, grid=(nb, nk),
        in_specs=[pl.BlockSpec((1, tk, kb), lambda j, k: (j, k, 0)),
                  pl.BlockSpec((tk, D), lambda j, k: (k, 0)),
                  pl.BlockSpec((kb, D), lambda j, k: (j, 0)),
                  pl.BlockSpec((8, D), lambda j, k: (0, 0))],
        out_specs=[_ANY, pl.BlockSpec((8, D), lambda j, k: (0, 0))],
        out_shape=[jax.ShapeDtypeStruct((NDEV, rows, D), BF16), jax.ShapeDtypeStruct((8, D), F32)],
        scratch_shapes=[pltpu.VMEM((kb, D), F32), pltpu.VMEM((nb, kb, D), BF16)] + _RS_SEMS,
        compiler_params=_cp("arbitrary", "arbitrary"),
    )(a, dx, w, mod)


M_WA, M_WB, M_CBIAS, M_BA, M_BX, M_LS = 0, 3, 7, 8, 9, 10


def _mixer_bwd(proj, hl, dmg, prm, wa, wx):
    t_len = proj.shape[1]
    tt = min(TT, t_len)
    nt = t_len // tt
    hb8 = tt // 8

    def rev(i):
        return nt - 1 - i

    def halo(i):
        return jnp.maximum(rev(i) * hb8 - 1, 0)

    def body(proj_ref, ph_ref, hl_ref, hh_ref, dmg_ref, prm_ref, wa_ref, wx_ref,
             dp_ref, sums_ref, gwa_ref, gwx_ref,
             xe, ve, he, u_s, ya_s, rp_s, ip_s, due, dye, drp_s, dip_s, an, gn):
        i = pl.program_id(0)
        t = rev(i)

        @pl.when(i == 0)
        def _():
            sums_ref[...] = jnp.zeros_like(sums_ref)
            gwa_ref[...] = jnp.zeros_like(gwa_ref)
            gwx_ref[...] = jnp.zeros_like(gwx_ref)
            due[tt:tt + 8, :] = jnp.zeros((8, D), F32)
            dye[tt:tt + 8, :] = jnp.zeros((8, D), F32)
            an[...] = jnp.zeros((8, D), F32)
            gn[...] = jnp.zeros((8, D), F32)

        live = (t > 0).astype(F32)
        xe[0:8, :] = ph_ref[3].astype(F32) * live
        ve[0:8, :] = ph_ref[1].astype(F32) * ph_ref[2].astype(F32) * live
        he[0:8, :] = hh_ref[...].astype(F32) * live
        xe[8:8 + tt, :] = proj_ref[3].astype(F32)
        ve[8:8 + tt, :] = proj_ref[1].astype(F32) * proj_ref[2].astype(F32)
        he[8:8 + tt, :] = hl_ref[...].astype(F32)
        u = prm_ref[P_CBIAS:P_CBIAS + 1, :] + prm_ref[P_WB:P_WB + 1, :] * xe[5:5 + tt, :]
        for k in range(1, 4):
            u = u + prm_ref[P_WB + k:P_WB + k + 1, :] * xe[5 + k:5 + k + tt, :]
        u_s[...] = u
        ya = prm_ref[P_WA:P_WA + 1, :] * ve[6:6 + tt, :]
        for k in range(1, 3):
            ya = ya + prm_ref[P_WA + k:P_WA + k + 1, :] * ve[6 + k:6 + k + tt, :]
        ya_s[...] = ya
        ub = u.astype(BF16)
        for h in range(HEADS):
            cs = slice(h * HB, (h + 1) * HB)
            rp_s[:, cs] = _dot(ub[:, cs], wa_ref[h]) + prm_ref[P_BA:P_BA + 1, cs]
            ip_s[:, cs] = _dot(ub[:, cs], wx_ref[h]) + prm_ref[P_BX:P_BX + 1, cs]

        ls_all = _log_sigmoid(prm_ref[P_LAM:P_LAM + 1, :])
        row = lax.broadcasted_iota(jnp.int32, (8, CG), 0)
        nblk = tt // 16

        def blk(ib, carry):
            r0 = pl.multiple_of((nblk - 1 - ib) * 16, 16)
            rows = pl.ds(r0, 16)
            for g in range(D // CG):
                cs = slice(g * CG, (g + 1) * CG)
                ls = ls_all[:, cs]
                dm = dmg_ref[rows, cs].astype(F32)
                cb = proj_ref[0, rows, cs].astype(F32)
                rg = proj_ref[4, rows, cs].astype(F32)
                sga = _sig(proj_ref[5, rows, cs].astype(F32))
                sgb = _sig(proj_ref[6, rows, cs].astype(F32))
                ya0 = ya_s[rows, cs]
                h16 = he[pl.ds(r0 + 8, 16), cs]
                gl, th = _gelu(rg)
                dgl = 0.5 * (1.0 + th) + 0.5 * rg * (1.0 - th * th) * (_GC * (1.0 + 3.0 * 0.044715 * rg * rg))
                y_a = cb * ya0
                y_b = h16 * gl
                dy_a = dm * sga
                dy_b = dm * sgb
                col = lambda s: slice(s * D + g * CG, s * D + (g + 1) * CG)
                dp_ref[rows, col(5)] = (dm * y_a * sga * (1.0 - sga)).astype(BF16)
                dp_ref[rows, col(6)] = (dm * y_b * sgb * (1.0 - sgb)).astype(BF16)
                dp_ref[rows, col(4)] = (dy_b * h16 * dgl).astype(BF16)
                dp_ref[rows, col(0)] = (dy_a * ya0).astype(BF16)
                dye[rows, cs] = dy_a * cb
                dh16 = dy_b * gl

                a_next = an[:, cs]
                g_next = gn[:, cs]
                s_ba = jnp.zeros((8, CG), F32)
                s_bx = jnp.zeros((8, CG), F32)
                s_ls = jnp.zeros((8, CG), F32)
                for sb in (1, 0):
                    rr = r0 + 8 * sb
                    first = (row + (t * tt + rr)) == 0
                    uu = u_s[pl.ds(rr, 8), cs]
                    r, ig, la, a, m2, mult = _lru_gates(rp_s[pl.ds(rr, 8), cs], ip_s[pl.ds(rr, 8), cs], ls, first)
                    ca = jnp.where(row < 7, pltpu.roll(a, 7, 0), a_next)
                    cb_ = dh16[8 * sb:8 * sb + 8, :]
                    for s in (1, 2, 4):
                        a_sh = jnp.where(row < 8 - s, pltpu.roll(ca, 8 - s, 0), 1.0)
                        b_sh = jnp.where(row < 8 - s, pltpu.roll(cb_, 8 - s, 0), 0.0)
                        cb_ = ca * b_sh + cb_
                        ca = ca * a_sh
                    gv = ca * g_next + cb_
                    g_next = jnp.broadcast_to(gv[0:1, :], gv.shape)
                    a_next = jnp.broadcast_to(a[0:1, :], a.shape)
                    hprev = jnp.where(row >= 1, pltpu.roll(he[pl.ds(rr + 8, 8), cs], 1, 0),
                                      pltpu.roll(he[pl.ds(rr, 8), cs], 1, 0))
                    da = gv * hprev
                    dmult = jnp.where(first, 0.0, gv * ig * uu)
                    dla = da * a + jnp.where(m2 > 0.0, dmult * (-(a * a) / mult), 0.0)
                    drp = dla * (LRU_C * ls) * r * (1.0 - r)
                    dip = gv * mult * uu * ig * (1.0 - ig)
                    s_ls = s_ls + dla * (LRU_C * r)
                    s_ba = s_ba + drp
                    s_bx = s_bx + dip
                    drp_s[pl.ds(rr, 8), cs] = drp
                    dip_s[pl.ds(rr, 8), cs] = dip
                    due[pl.ds(rr, 8), cs] = gv * mult * ig
                an[:, cs] = a_next
                gn[:, cs] = g_next
                sums_ref[M_BA:M_BA + 1, cs] += jnp.sum(s_ba, axis=0, keepdims=True)
                sums_ref[M_BX:M_BX + 1, cs] += jnp.sum(s_bx, axis=0, keepdims=True)
                sums_ref[M_LS:M_LS + 1, cs] += jnp.sum(s_ls, axis=0, keepdims=True)
            return carry

        lax.fori_loop(0, nblk, blk, 0)

        drp_b = drp_s[...].astype(BF16)
        dip_b = dip_s[...].astype(BF16)
        for h in range(HEADS):
            cs = slice(h * HB, (h + 1) * HB)
            due[0:tt, cs] += _dot_nt(drp_b[:, cs], wa_ref[h]) + _dot_nt(dip_b[:, cs], wx_ref[h])
            gwa_ref[h] += _dot_tn(ub[:, cs], drp_b[:, cs])
            gwx_ref[h] += _dot_tn(ub[:, cs], dip_b[:, cs])

        du = due[0:tt, :]
        sums_ref[M_CBIAS:M_CBIAS + 1, :] += jnp.sum(du, axis=0, keepdims=True)
        drx = prm_ref[P_WB:P_WB + 1, :] * due[3:3 + tt, :]
        sums_ref[M_WB:M_WB + 1, :] += jnp.sum(du * xe[5:5 + tt, :], axis=0, keepdims=True)
        for k in range(1, 4):
            drx = drx + prm_ref[P_WB + k:P_WB + k + 1, :] * due[3 - k:3 - k + tt, :]
            sums_ref[M_WB + k:M_WB + k + 1, :] += jnp.sum(du * xe[5 + k:5 + k + tt, :], axis=0, keepdims=True)
        dp_ref[:, 3 * D:4 * D] = drx.astype(BF16)
        dya = dye[0:tt, :]
        dv = prm_ref[P_WA:P_WA + 1, :] * dye[2:2 + tt, :]
        sums_ref[M_WA:M_WA + 1, :] += jnp.sum(dya * ve[6:6 + tt, :], axis=0, keepdims=True)
        for k in range(1, 3):
            dv = dv + prm_ref[P_WA + k:P_WA + k + 1, :] * dye[2 - k:2 - k + tt, :]
            sums_ref[M_WA + k:M_WA + k + 1, :] += jnp.sum(dya * ve[6 + k:6 + k + tt, :], axis=0, keepdims=True)
        dp_ref[:, D:2 * D] = (dv * proj_ref[2].astype(F32)).astype(BF16)
        dp_ref[:, 2 * D:3 * D] = (dv * proj_ref[1].astype(F32)).astype(BF16)
        due[tt:tt + 8, :] = due[0:8, :]
        dye[tt:tt + 8, :] = dye[0:8, :]

        @pl.when(i == nt - 1)
        def _():
            sums_ref[M_LS:M_LS + 1, :] = sums_ref[M_LS:M_LS + 1, :] * _sig(-prm_ref[P_LAM:P_LAM + 1, :])

    big = lambda: pltpu.VMEM((tt + 8, D), F32)
    tile = lambda: pltpu.VMEM((tt, D), F32)
    return pl.pallas_call(
        body, name="mixer_bwd", grid=(nt,),
        in_specs=[pl.BlockSpec((7, tt, D), lambda i: (0, rev(i), 0)),
                  pl.BlockSpec((7, 8, D), lambda i: (0, halo(i), 0)),
                  pl.BlockSpec((tt, D), lambda i: (rev(i), 0)),
                  pl.BlockSpec((8, D), lambda i: (halo(i), 0)),
                  pl.BlockSpec((tt, D), lambda i: (rev(i), 0)),
                  pl.BlockSpec((16, D), lambda i: (0, 0)),
                  pl.BlockSpec((HEADS, HB, HB), lambda i: (0, 0, 0)),
                  pl.BlockSpec((HEADS, HB, HB), lambda i: (0, 0, 0))],
        out_specs=[pl.BlockSpec((tt, 7 * D), lambda i: (rev(i), 0)),
                   pl.BlockSpec((16, D), lambda i: (0, 0)),
                   pl.BlockSpec((HEADS, HB, HB), lambda i: (0, 0, 0)),
                   pl.BlockSpec((HEADS, HB, HB), lambda i: (0, 0, 0))],
        out_shape=[jax.ShapeDtypeStruct((t_len, 7 * D), BF16), jax.ShapeDtypeStruct((16, D), F32),
                   jax.ShapeDtypeStruct((HEADS, HB, HB), F32), jax.ShapeDtypeStruct((HEADS, HB, HB), F32)],
        scratch_shapes=[big(), big(), big(), tile(), tile(), tile(), tile(), big(), big(), tile(), tile(),
                        pltpu.VMEM((8, D), F32), pltpu.VMEM((8, D), F32)],
        compiler_params=_cp("arbitrary"),
    )(proj, proj, hl, hl, dmg, prm, wa, wx)


def _in_proj_bwd(dproj, w_in, x, dx1, mod, g_mix):
    t_len = x.shape[0]
    tm = min(TM, t_len)

    def body(dp_ref, w_ref, x_ref, dx1_ref, mod_ref, g_ref, gx_ref, sums_ref, acc):
        i, s = pl.program_id(0), pl.program_id(1)

        @pl.when((i == 0) & (s == 0))
        def _():
            sums_ref[...] = jnp.zeros_like(sums_ref)

        part = _dot_nt(dp_ref[...], w_ref[...])

        @pl.when(s == 0)
        def _():
            acc[...] = part

        @pl.when(s > 0)
        def _():
            acc[...] += part

        @pl.when(s == 6)
        def _():
            dh = acc[...]
            xv = x_ref[...]
            r = lax.rsqrt(jnp.mean(xv * xv, axis=-1, keepdims=True) + EPS)
            xn = xv * r
            gf = g_ref[...]
            sums_ref[S_SH:S_SH + 1, :] += jnp.sum(dh, axis=0, keepdims=True)
            sums_ref[S_SC:S_SC + 1, :] += jnp.sum(dh * (xn * gf), axis=0, keepdims=True)
            dhn = dh * (1.0 + mod_ref[1:2, :])
            sums_ref[S_G:S_G + 1, :] += jnp.sum(dhn * xn, axis=0, keepdims=True)
            dxn = dhn * gf
            gx_ref[...] = dx1_ref[...] + r * (dxn - xn * jnp.mean(dxn * xn, axis=-1, keepdims=True))

    return pl.pallas_call(
        body, name="in_proj_bwd", grid=(t_len // tm, 7),
        in_specs=[pl.BlockSpec((tm, D), lambda i, s: (i, s)),
                  pl.BlockSpec((D, D), lambda i, s: (0, s)),
                  pl.BlockSpec((tm, D), lambda i, s: (i, 0)), pl.BlockSpec((tm, D), lambda i, s: (i, 0)),
                  pl.BlockSpec((8, D), lambda i, s: (0, 0)), pl.BlockSpec((1, D), lambda i, s: (0, 0))],
        out_specs=[pl.BlockSpec((tm, D), lambda i, s: (i, 0)), pl.BlockSpec((8, D), lambda i, s: (0, 0))],
        out_shape=[jax.ShapeDtypeStruct((t_len, D), F32), jax.ShapeDtypeStruct((8, D), F32)],
        scratch_shapes=[pltpu.VMEM((tm, D), F32)],
        compiler_params=_cp("arbitrary", "arbitrary"),
    )(dproj, w_in, x, dx1, mod, g_mix)


def _in_wgrad(h, dproj, g_wa, g_wx):
    t_len = h.shape[0]
    tk = min(TKI, t_len)
    nk = t_len // tk
    nq = 4
    cw = 7 * D // NDEV
    hr = HB // NDEV

    def body(h_ref, d_ref, ga_ref, gx_ref, parts_ref, pa_ref, px_ref, acc, stage, *sems):
        q, k = pl.program_id(0), pl.program_id(1)

        def head_rows(ref):
            return lambda s: ref.at[:, pl.ds(s * hr, hr), :]

        @pl.when((q == 0) & (k == 0))
        def _():
            for s in range(NDEV):
                _rs_send(head_rows(ga_ref)(s), pa_ref, s, *sems[3:6])
                _rs_send(head_rows(gx_ref)(s), px_ref, s, *sems[6:9])

        part = _dot_tn(h_ref[...], d_ref[...])

        @pl.when(k == 0)
        def _():
            acc[...] = part

        @pl.when(k > 0)
        def _():
            acc[...] += part

        def src_of(blk, r):
            return stage.at[blk, :, pl.ds(r * cw, cw)]

        @pl.when(k == nk - 1)
        def _():
            stage[q] = acc[...].astype(BF16)
            for r in range(2):
                _rs_send(src_of(q, r), parts_ref, q * 2 + r, *sems[0:3])

        @pl.when((q == nq - 1) & (k == nk - 1))
        def _():
            _rs_finish(lambda s: src_of(s // 2, s % 2), parts_ref, *sems[0:3])
            _rs_finish(head_rows(ga_ref), pa_ref, *sems[3:6])
            _rs_finish(head_rows(gx_ref), px_ref, *sems[6:9])

    return pl.pallas_call(
        body, name="in_wgrad", grid=(nq, nk),
        in_specs=[pl.BlockSpec((tk, D), lambda q, k: (k, 0)), pl.BlockSpec((tk, 2 * cw), lambda q, k: (k, q)),
                  _ANY, _ANY],
        out_specs=[_ANY, _ANY, _ANY],
        out_shape=[jax.ShapeDtypeStruct((NDEV, D, cw), BF16), jax.ShapeDtypeStruct((NDEV, HEADS, hr, HB), F32),
                   jax.ShapeDtypeStruct((NDEV, HEADS, hr, HB), F32)],
        scratch_shapes=[pltpu.VMEM((D, 2 * cw), F32), pltpu.VMEM((nq, D, 2 * cw), BF16)] + _RS_SEMS * 3,
        compiler_params=_cp("arbitrary", "arbitrary"),
    )(h, dproj, g_wa, g_wx)


def _ada_fwd(c_all, w_ada, b_cols):
    def body(c_ref, w_ref, b_ref, o_ref):
        cv = c_ref[...]
        o_ref[...] = _dot((cv * _sig(cv)).astype(BF16), w_ref[...].astype(BF16)) + b_ref[...]

    return pl.pallas_call(body, name="ada_fwd", out_shape=jax.ShapeDtypeStruct((16, w_ada.shape[1]), F32),
                          compiler_params=_cp())(c_all, w_ada, b_cols)


def _adam_math(w, g, m, v):
    m = ADAM_B1 * m + (1.0 - ADAM_B1) * g
    v = ADAM_B2 * v + (1.0 - ADAM_B2) * (g * g)
    m_hat = m / (1.0 - ADAM_B1 ** ADAM_STEP)
    v_hat = v / (1.0 - ADAM_B2 ** ADAM_STEP)
    delta = -ADAM_LR * (m_hat / (jnp.sqrt(v_hat) + ADAM_EPS) + ADAM_WD * w)
    return delta, m, v


def _ada_bwd(c_all, dmod_cols, w, m, v):
    rb = 256
    n = w.shape[1]
    nrow = c_all.shape[0]

    def body(c_ref, d_ref, w_ref, m_ref, v_ref, g_ref, dl_ref, nm_ref, nv_ref):
        cv = c_ref[...]
        g = _dot_tn((cv * _sig(cv)).astype(BF16), d_ref[...].astype(BF16))
        g_ref[...] = g
        dl_ref[...], nm_ref[...], nv_ref[...] = _adam_math(w_ref[...], g, m_ref[...], v_ref[...])

    blk = pl.BlockSpec((rb, n), lambda i: (i, 0))
    sds = jax.ShapeDtypeStruct(w.shape, F32)
    return pl.pallas_call(
        body, name="ada_bwd", grid=(D // rb,),
        in_specs=[pl.BlockSpec((nrow, rb), lambda i: (0, i)), pl.BlockSpec((nrow, n), lambda i: (0, 0)), blk, blk, blk],
        out_specs=[blk, blk, blk, blk], out_shape=[sds, sds, sds, sds],
        compiler_params=_cp("parallel"),
    )(c_all, dmod_cols, w, m, v)


def _adam(name, parts, w, m, v):
    p, r, c = parts.shape
    rb = r
    for cand in (256, 128, 64, 32, 16, 8):
        if r % cand == 0 and r >= cand:
            rb = cand
            break

    def body(p_ref, w_ref, m_ref, v_ref, g_ref, dl_ref, nm_ref, nv_ref):
        g = p_ref[0].astype(F32)
        for q in range(1, p):
            g = g + p_ref[q].astype(F32)
        g_ref[...] = g
        dl_ref[...], nm_ref[...], nv_ref[...] = _adam_math(w_ref[...], g, m_ref[...], v_ref[...])

    blk = pl.BlockSpec((rb, c), lambda i: (i, 0))
    sds = jax.ShapeDtypeStruct((r, c), F32)
    return pl.pallas_call(
        body, name=name, grid=(r // rb,),
        in_specs=[pl.BlockSpec((p, rb, c), lambda i: (0, i, 0)), blk, blk, blk],
        out_specs=[blk, blk, blk, blk], out_shape=[sds, sds, sds, sds],
        compiler_params=_cp("parallel"),
    )(parts, w, m, v)


def _my_pos():
    return lax.axis_index("x"), lax.axis_index("y"), lax.axis_index("c")


def _all_gather_small(name, v):
    m_per, n = v.shape

    def body(x_ref, out_ref, send_sems, recv_sems, local_sem):
        x, y, c = _my_pos()
        me, sibling = (x, y, c), (x, y, 1 - c)
        chips = [(1 - x, y), (x, 1 - y), (1 - x, 1 - y)]

        def rows(px, py, pc):
            return out_ref.at[pl.ds((4 * px + 2 * py + pc) * m_per, m_per), :]

        def copy(k, block, to, src=None):
            return pltpu.make_async_remote_copy(
                src_ref=rows(*block) if src is None else src, dst_ref=rows(*block),
                send_sem=send_sems.at[k], recv_sem=recv_sems.at[k], device_id=to, device_id_type=MESH)

        mine = pltpu.make_async_copy(x_ref, rows(*me), local_sem)
        mine.start()
        first = [copy(0, me, sibling, src=x_ref)]
        first += [copy(1 + j, me, (*chip, c), src=x_ref) for j, chip in enumerate(chips)]
        for cp in first:
            cp.start()
        passed = [copy(4 + j, (*chip, c), sibling) for j, chip in enumerate(chips)]
        for j, chip in enumerate(chips):
            copy(1 + j, (*chip, c), me).wait_recv()
            passed[j].start()
        copy(0, sibling, me).wait_recv()
        for j, chip in enumerate(chips):
            copy(4 + j, (*chip, 1 - c), me).wait_recv()
        for cp in first + passed:
            cp.wait_send()
        mine.wait()

    return pl.pallas_call(
        body, name=name, out_shape=jax.ShapeDtypeStruct((NDEV * m_per, n), v.dtype),
        in_specs=[pl.BlockSpec(memory_space=pltpu.VMEM)], out_specs=pl.BlockSpec(memory_space=pltpu.VMEM),
        scratch_shapes=[pltpu.SemaphoreType.DMA((7,)), pltpu.SemaphoreType.DMA((7,)), pltpu.SemaphoreType.DMA],
    )(v)


def _blk_cols(n):
    return lambda ref, b: ref.at[:, pl.ds(pl.multiple_of(b * n, 128), n)]


def _blk_rows(n):
    return lambda ref, b: ref.at[pl.ds(pl.multiple_of(b * n, 8), n), :]


def _blk_lead(ref, b):
    return ref.at[b]


def _blk_heads(ref, b):
    return ref.at[:, pl.ds(pl.multiple_of(b * (HB // NDEV), 8), HB // NDEV), :]


def _ag_phases(ins, outs, slicers, send_sems, recv_sems, local_sems):
    na = len(ins)
    x, y, c = _my_pos()
    me, sibling = (x, y, c), (x, y, 1 - c)
    chips = [(1 - x, y), (x, 1 - y), (1 - x, 1 - y)]

    def copy(a, k, block, to, from_shard=False):
        px, py, pc = block
        dst = slicers[a](outs[a], 4 * px + 2 * py + pc)
        return pltpu.make_async_remote_copy(
            src_ref=ins[a] if from_shard else dst, dst_ref=dst,
            send_sem=send_sems.at[a * 7 + k], recv_sem=recv_sems.at[a * 7 + k], device_id=to, device_id_type=MESH)

    def local(a):
        return pltpu.make_async_copy(ins[a], slicers[a](outs[a], 4 * x + 2 * y + c), local_sems.at[a])

    def firsts(a):
        return [copy(a, 0, me, sibling, True)] + [copy(a, 1 + j, me, (*chip, c), True) for j, chip in enumerate(chips)]

    def start():
        for a in range(na):
            local(a).start()
            for cp in firsts(a):
                cp.start()

    def forward():
        for a in range(na):
            for j, chip in enumerate(chips):
                copy(a, 1 + j, (*chip, c), me).wait_recv()
                copy(a, 4 + j, (*chip, c), sibling).start()

    def finish():
        for a in range(na):
            copy(a, 0, sibling, me).wait_recv()
            for j, chip in enumerate(chips):
                copy(a, 4 + j, (*chip, 1 - c), me).wait_recv()
        for a in range(na):
            for cp in firsts(a) + [copy(a, 4 + j, (*chip, c), sibling) for j, chip in enumerate(chips)]:
                cp.wait_send()
            local(a).wait()

    return start, forward, finish


def _ag_sems(na):
    return [pltpu.SemaphoreType.DMA((7 * na,)), pltpu.SemaphoreType.DMA((7 * na,)), pltpu.SemaphoreType.DMA((na,))]


def _all_gather_weights(shards, fulls, slicers):
    na = len(shards)

    def body(*refs):
        start, forward, finish = _ag_phases(refs[:na], refs[na:2 * na], slicers, *refs[2 * na:])
        start()
        forward()
        finish()

    return pl.pallas_call(
        body, name="gather_weights",
        out_shape=[jax.ShapeDtypeStruct(s, sh.dtype) for s, sh in zip(fulls, shards)],
        in_specs=[_ANY] * na, out_specs=[_ANY] * na, scratch_shapes=_ag_sems(na),
    )(*shards)


def _scatter_grads(grads, shard_shapes, slicers):
    na = len(grads)

    def body(*refs):
        ins, outs = refs[:na], refs[na:2 * na]
        send_sems, recv_sems, local_sems = refs[2 * na:]
        x, y, c = _my_pos()
        me = 4 * x + 2 * y + c
        mine, sent = [], []
        for a in range(na):
            cp = pltpu.make_async_copy(slicers[a](ins[a], me), outs[a].at[me], local_sems.at[a])
            cp.start()
            mine.append(cp)
        rel = [(k >> 2 & 1, k >> 1 & 1, k & 1) for k in range(1, NDEV)]
        for a in range(na):
            for k, (fx, fy, fc) in enumerate(rel):
                px, py, pc = x ^ fx, y ^ fy, c ^ fc
                cp = pltpu.make_async_remote_copy(
                    src_ref=slicers[a](ins[a], 4 * px + 2 * py + pc), dst_ref=outs[a].at[me],
                    send_sem=send_sems.at[a * 7 + k], recv_sem=recv_sems.at[a * 7 + k],
                    device_id=(px, py, pc), device_id_type=MESH)
                cp.start()
                sent.append(cp)
        for a in range(na):
            for k, (fx, fy, fc) in enumerate(rel):
                px, py, pc = x ^ fx, y ^ fy, c ^ fc
                src = 4 * px + 2 * py + pc
                pltpu.make_async_remote_copy(
                    src_ref=slicers[a](ins[a], me), dst_ref=outs[a].at[src],
                    send_sem=send_sems.at[a * 7 + k], recv_sem=recv_sems.at[a * 7 + k],
                    device_id=(px, py, pc), device_id_type=MESH).wait_recv()
        for cp in sent:
            cp.wait_send()
        for cp in mine:
            cp.wait()

    any_spec = pl.BlockSpec(memory_space=pl.ANY)
    return pl.pallas_call(
        body, name="scatter_grads",
        out_shape=[jax.ShapeDtypeStruct((NDEV,) + tuple(s), g.dtype) for s, g in zip(shard_shapes, grads)],
        in_specs=[any_spec] * na, out_specs=[any_spec] * na,
        scratch_shapes=[pltpu.SemaphoreType.DMA((7 * na,)), pltpu.SemaphoreType.DMA((7 * na,)),
                        pltpu.SemaphoreType.DMA((na,))],
    )(*grads)


def _local_step(x, target, mod, g_mix, g_ffn, g_fin, prm, w_in, shards):
    fulls = [(HEADS, HB, HB), (HEADS, HB, HB), (D, D), (NDEV, D, FB), (DFF, D)]
    slicers = [_blk_heads, _blk_heads, _blk_rows(D // NDEV), _blk_lead, _blk_rows(DFF // NDEV)]
    proj, h, (wa, wx, w_out, w_gu, w_down) = _in_proj(x, mod, g_mix, w_in, shards, fulls, slicers)
    w_gu = w_gu.reshape(2, 4, D, FB)
    merged, hl = _mixer_fwd(proj, prm, wa, wx)
    x1, h2 = _out_proj(merged, x, mod, g_ffn, w_out)
    gu, dx2, dx2b, loss, d_gfin = _ffn_fwd(h2, x1, target, mod, g_fin, w_gu, w_down)
    dgu, act, dx1, dx1b, dmg, sums2 = _ffn_bwd(dx2, gu, x1, mod, g_ffn, w_gu, w_down, w_out)
    p_wgu = _gu_wgrad(h2, dgu)
    p_wdown, d_gt2 = _scaled_wgrad("down_wgrad", act, dx2b, w_down, 5, mod)
    p_wout, d_gt1 = _scaled_wgrad("out_wgrad", merged.reshape(1, *merged.shape), dx1b, w_out, 2, mod)
    dproj, msums, g_wa, g_wx = _mixer_bwd(proj, hl, dmg, prm, wa, wx)
    p_win, p_wa, p_wx = _in_wgrad(h, dproj, g_wa, g_wx)
    grad_x, sums1 = _in_proj_bwd(dproj, w_in, x, dx1, mod, g_mix)
    return dict(loss=loss, grad_x=grad_x, d_gfin=d_gfin, sums1=sums1, sums2=sums2, msums=msums,
                d_gt1=d_gt1[0:1], d_gt2=d_gt2[0:1], p_win=p_win, p_wa=p_wa, p_wx=p_wx, p_wout=p_wout, p_wgu=p_wgu,
                p_wdown=p_wdown)


def kernel(x, c, w_ada, b_ada, g_norm_mix, w_in, conv_a_w, conv_b_w, conv_b_bias, w_rg_a, b_rg_a, w_rg_x, b_rg_x, lru_lambda, w_out, g_norm_ffn, w_gate_up, w_down, g_norm_final, loss_target, m_w_ada, m_b_ada, m_g_norm_mix, m_w_in, m_conv_a_w, m_conv_b_w, m_conv_b_bias, m_w_rg_a, m_b_rg_a, m_w_rg_x, m_b_rg_x, m_lru_lambda, m_w_out, m_g_norm_ffn, m_w_gate_up, m_w_down, m_g_norm_final, v_w_ada, v_b_ada, v_g_norm_mix, v_w_in, v_conv_a_w, v_conv_b_w, v_conv_b_bias, v_w_rg_a, v_b_rg_a, v_w_rg_x, v_b_rg_x, v_lru_lambda, v_w_out, v_g_norm_ffn, v_w_gate_up, v_w_down, v_g_norm_final):
    me = 4 * lax.axis_index("x") + 2 * lax.axis_index("y") + lax.axis_index("c")
    ncol = w_ada.shape[2]
    cw = conv_a_w.shape[2]

    pack0 = jnp.concatenate([c, conv_a_w.reshape(1, 3 * cw), conv_b_w.reshape(1, 4 * cw)], axis=1)
    got0 = _all_gather_small("gather_c", jnp.broadcast_to(pack0, (8, pack0.shape[1])))
    got0 = got0.reshape(NDEV, 8, -1)[:, 0, :]
    c_all = got0[:, :D]
    conv_a = got0[:, D:D + 3 * cw].reshape(NDEV, 3, cw).transpose(1, 0, 2).reshape(3, D)
    conv_b = got0[:, D + 3 * cw:].reshape(NDEV, 4, cw).transpose(1, 0, 2).reshape(4, D)

    b_cols = lax.dynamic_slice_in_dim(b_ada, me * ncol, ncol, axis=1)
    c16 = jnp.concatenate([c_all, jnp.zeros((8, D), F32)], axis=0)
    mod_cols = _ada_fwd(c16, w_ada[0], b_cols)[:NDEV]
    got1 = _all_gather_small("gather_mod", mod_cols).reshape(NDEV, NDEV, ncol)
    mod6 = lax.dynamic_index_in_dim(got1, me, axis=1, keepdims=False).reshape(6, D)
    mod = jnp.concatenate([mod6, jnp.zeros((2, D), F32)], axis=0)

    (w_in_f,) = _all_gather_weights([w_in[0].astype(BF16)], [(D, 7 * D)], [_blk_cols(7 * D // NDEV)])
    shards = [w_rg_a[0].astype(BF16), w_rg_x[0].astype(BF16), w_out[0].astype(BF16), w_gate_up[0].astype(BF16),
              w_down[0].astype(BF16)]

    prm = jnp.concatenate([conv_a, conv_b, conv_b_bias, b_rg_a, b_rg_x, lru_lambda, jnp.zeros((5, D), F32)], axis=0)
    r = _local_step(x[0], loss_target[0], mod, g_norm_mix, g_norm_ffn, g_norm_final.reshape(1, D), prm,
                    w_in_f, shards)

    parts = [r["p_win"], r["p_wa"], r["p_wx"], r["p_wout"], r["p_wgu"], r["p_wdown"]]
    big = {}
    for nm, p, w, m, v in (("w_in", parts[0], w_in, m_w_in, v_w_in), ("w_rg_a", parts[1], w_rg_a, m_w_rg_a, v_w_rg_a),
                           ("w_rg_x", parts[2], w_rg_x, m_w_rg_x, v_w_rg_x), ("w_out", parts[3], w_out, m_w_out, v_w_out),
                           ("w_gate_up", parts[4], w_gate_up, m_w_gate_up, v_w_gate_up),
                           ("w_down", parts[5], w_down, m_w_down, v_w_down)):
        two_d = (-1, w.shape[-1])
        outs = _adam("adam_" + nm, p.reshape((NDEV,) + w.reshape(two_d).shape), w.reshape(two_d), m.reshape(two_d),
                     v.reshape(two_d))
        big[nm] = [o.reshape(w.shape) for o in outs]

    small = jnp.concatenate([
        r["sums1"][S_SH:S_SH + 1], r["sums1"][S_SC:S_SC + 1], r["d_gt1"],
        r["sums2"][S_SH:S_SH + 1], r["sums2"][S_SC:S_SC + 1], r["d_gt2"],
        r["sums1"][S_G:S_G + 1],
        r["msums"][M_CBIAS:M_CBIAS + 1], r["msums"][M_BA:M_BA + 1], r["msums"][M_BX:M_BX + 1],
        r["msums"][M_LS:M_LS + 1],
        r["sums2"][S_G:S_G + 1], r["d_gfin"],
        r["msums"][M_WA:M_WA + 3], r["msums"][M_WB:M_WB + 4],
        jnp.zeros((4, D), F32)], axis=0)
    got2 = _all_gather_small("gather_small", small).reshape(NDEV, 24, D)

    rep_w = jnp.concatenate([b_ada.reshape(6, D), g_norm_mix, conv_b_bias, b_rg_a, b_rg_x, lru_lambda, g_norm_ffn,
                             g_norm_final.reshape(1, D), jnp.zeros((3, D), F32)], axis=0)
    rep_m = jnp.concatenate([m_b_ada.reshape(6, D), m_g_norm_mix, m_conv_b_bias, m_b_rg_a, m_b_rg_x, m_lru_lambda,
                             m_g_norm_ffn, m_g_norm_final.reshape(1, D), jnp.zeros((3, D), F32)], axis=0)
    rep_v = jnp.concatenate([v_b_ada.reshape(6, D), v_g_norm_mix, v_conv_b_bias, v_b_rg_a, v_b_rg_x, v_lru_lambda,
                             v_g_norm_ffn, v_g_norm_final.reshape(1, D), jnp.ones((3, D), F32)], axis=0)
    rep = _adam("adam_rep", got2[:, :16, :], rep_w, rep_m, rep_v)

    conv_parts = lax.dynamic_slice_in_dim(got2[:, 13:21, :], me * cw, cw, axis=2)
    cv_w = jnp.concatenate([conv_a_w[0], conv_b_w[0], jnp.zeros((1, cw), F32)], axis=0)
    cv_m = jnp.concatenate([m_conv_a_w[0], m_conv_b_w[0], jnp.zeros((1, cw), F32)], axis=0)
    cv_v = jnp.concatenate([v_conv_a_w[0], v_conv_b_w[0], jnp.ones((1, cw), F32)], axis=0)
    cvo = _adam("adam_conv", conv_parts, cv_w, cv_m, cv_v)

    dmod_cols = lax.dynamic_slice_in_dim(got2[:, :6, :].reshape(NDEV, 6 * D), me * ncol, ncol, axis=1)
    dmod16 = jnp.concatenate([dmod_cols, jnp.zeros((8, ncol), F32)], axis=0)
    ada = _ada_bwd(c16, dmod16, w_ada[0], m_w_ada[0], v_w_ada[0])

    loss = lax.psum(r["loss"][0, 0], AXES)

    def pick(q):
        one = lambda i: rep[q][i:i + 1]
        return [ada[q].reshape(w_ada.shape), rep[q][0:6].reshape(b_ada.shape), one(6), big["w_in"][q],
                cvo[q][0:3].reshape(conv_a_w.shape), cvo[q][3:7].reshape(conv_b_w.shape), one(7),
                big["w_rg_a"][q], one(8), big["w_rg_x"][q], one(9), one(10), big["w_out"][q], one(11),
                big["w_gate_up"][q], big["w_down"][q], rep[q][12]]

    return (loss, r["grad_x"].reshape(x.shape), *pick(0), *pick(1), *pick(2), *pick(3))
```

```python
import functools
import math

import jax
import jax.numpy as jnp
from jax import lax
from jax.experimental import pallas as pl
from jax.experimental.pallas import tpu as pltpu

F32 = jnp.float32
BF16 = jnp.bfloat16

D = 1024
DFF = 2816
NDEV = 8
HEADS = 4
HB = D // HEADS
FB = DFF // 4
EPS = 1e-6
LRU_C = 8.0
ADAM_LR, ADAM_B1, ADAM_B2, ADAM_EPS, ADAM_WD, ADAM_STEP = 0.001, 0.9, 0.999, 1e-08, 0.01, 10

VMEM_LIMIT = 56 * 1024 * 1024
TM = 512
TMI = 1024
TK = 2048
TKI = 1024
TT = 256
CG = 256
MESH = pl.DeviceIdType.MESH
AXES = ("x", "y", "c")


def _cp(*sem):
    return pltpu.CompilerParams(dimension_semantics=sem, vmem_limit_bytes=VMEM_LIMIT)


def _sig(x):
    return 1.0 / (1.0 + jnp.exp(-x))


def _log_sigmoid(x):
    z = jnp.exp(-jnp.abs(x))
    u = 1.0 + z
    d = u - 1.0
    l1p = jnp.where(d == 0.0, z, jnp.log(u) * (z / jnp.where(d == 0.0, 1.0, d)))
    return -(jnp.maximum(-x, 0.0) + l1p)


def _neg_expm1(x):
    p = x * (1.0 + x * 0.5 * (1.0 + x * (1.0 / 3.0) * (1.0 + x * 0.25 * (1.0 + x * 0.2 * (1.0 + x * (1.0 / 6.0))))))
    return jnp.where(x > -0.25, -p, 1.0 - jnp.exp(x))


_GC = math.sqrt(2.0 / math.pi)


def _gelu(x):
    t = jnp.tanh(_GC * (x + 0.044715 * x * x * x))
    return 0.5 * x * (1.0 + t), t


def _dot(a, b):
    return jnp.dot(a, b, preferred_element_type=F32)


def _dot_nt(a, b):
    return lax.dot_general(a, b, (((1,), (1,)), ((), ())), preferred_element_type=F32)


def _dot_tn(a, b):
    return lax.dot_general(a, b, (((0,), (0,)), ((), ())), preferred_element_type=F32)


def _in_proj(x, mod, g_mix, w_in, shards, fulls, slicers):
    t_len = x.shape[0]
    tm = min(TMI, t_len)
    ni = t_len // tm
    na = len(shards)
    rc = 32

    def body(x_ref, mod_ref, g_ref, w_ref, *rest):
        ins, (proj_ref, h_ref), outs = rest[:na], rest[na:na + 2], rest[na + 2:2 * na + 2]
        h_scr = rest[2 * na + 2]
        start, forward, finish = _ag_phases(ins, outs, slicers, *rest[2 * na + 3:])
        i, s = pl.program_id(0), pl.program_id(1)

        @pl.when((i == 0) & (s == 0))
        def _():
            start()

        @pl.when((i == ni // 2) & (s == 0))
        def _():
            forward()

        @pl.when(s == 0)
        def _():
            gs = g_ref[...] * (1.0 + mod_ref[1:2, :])
            sh = mod_ref[0:1, :]

            def chunk(i, carry):
                rows = pl.ds(pl.multiple_of(i * rc, rc), rc)
                xv = x_ref[rows, :]
                r = lax.rsqrt(jnp.mean(xv * xv, axis=-1, keepdims=True) + EPS)
                h = (xv * r * gs + sh).astype(BF16)
                h_scr[rows, :] = h
                h_ref[rows, :] = h
                return carry

            lax.fori_loop(0, tm // rc, chunk, 0)

        proj_ref[0] = _dot(h_scr[...], w_ref[...]).astype(BF16)

        @pl.when((i == ni - 1) & (s == 6))
        def _():
            finish()

    res = pl.pallas_call(
        body, name="in_proj", grid=(ni, 7),
        in_specs=[pl.BlockSpec((tm, D), lambda i, s: (i, 0)),
                  pl.BlockSpec((8, D), lambda i, s: (0, 0)),
                  pl.BlockSpec((1, D), lambda i, s: (0, 0)),
                  pl.BlockSpec((D, D), lambda i, s: (0, s))] + [_ANY] * na,
        out_specs=[pl.BlockSpec((1, tm, D), lambda i, s: (s, i, 0)),
                   pl.BlockSpec((tm, D), lambda i, s: (i, 0))] + [_ANY] * na,
        out_shape=[jax.ShapeDtypeStruct((7, t_len, D), BF16), jax.ShapeDtypeStruct((t_len, D), BF16)]
        + [jax.ShapeDtypeStruct(f, sh.dtype) for f, sh in zip(fulls, shards)],
        scratch_shapes=[pltpu.VMEM((tm, D), BF16)] + _ag_sems(na),
        compiler_params=_cp("arbitrary", "arbitrary"),
    )(x, mod, g_mix, w_in, *shards)
    return res[0], res[1], res[2:]


P_WA, P_WB, P_CBIAS, P_BA, P_BX, P_LAM = 0, 3, 7, 8, 9, 10


def _lru_gates(rp, ip, ls, first_row):
    r = _sig(rp)
    ig = _sig(ip)
    la = LRU_C * r * ls
    a = jnp.exp(la)
    m2 = _neg_expm1(2.0 * la)
    mult = jnp.where(first_row, 1.0, jnp.sqrt(jnp.maximum(m2, 0.0)))
    return r, ig, la, a, m2, mult


def _mixer_fwd(proj, prm, wa, wx):
    t_len = proj.shape[1]
    tt = min(TT, t_len)

    def body(proj_ref, prm_ref, wa_ref, wx_ref, mg_ref, hl_ref, xe, ve, hc, u_s, ya_s, rp_s, ip_s):
        t = pl.program_id(0)

        @pl.when(t == 0)
        def _():
            xe[0:8, :] = jnp.zeros((8, D), F32)
            ve[0:8, :] = jnp.zeros((8, D), F32)
            hc[...] = jnp.zeros((8, D), F32)

        xe[8:8 + tt, :] = proj_ref[3].astype(F32)
        ve[8:8 + tt, :] = proj_ref[1].astype(F32) * proj_ref[2].astype(F32)
        u = prm_ref[P_CBIAS:P_CBIAS + 1, :] + prm_ref[P_WB:P_WB + 1, :] * xe[5:5 + tt, :]
        for k in range(1, 4):
            u = u + prm_ref[P_WB + k:P_WB + k + 1, :] * xe[5 + k:5 + k + tt, :]
        u_s[...] = u
        ya = prm_ref[P_WA:P_WA + 1, :] * ve[6:6 + tt, :]
        for k in range(1, 3):
            ya = ya + prm_ref[P_WA + k:P_WA + k + 1, :] * ve[6 + k:6 + k + tt, :]
        ya_s[...] = ya
        xe[0:8, :] = xe[tt:tt + 8, :]
        ve[0:8, :] = ve[tt:tt + 8, :]

        ub = u.astype(BF16)
        for h in range(HEADS):
            cs = slice(h * HB, (h + 1) * HB)
            rp_s[:, cs] = _dot(ub[:, cs], wa_ref[h]) + prm_ref[P_BA:P_BA + 1, cs]
            ip_s[:, cs] = _dot(ub[:, cs], wx_ref[h]) + prm_ref[P_BX:P_BX + 1, cs]

        ls_all = _log_sigmoid(prm_ref[P_LAM:P_LAM + 1, :])
        row = lax.broadcasted_iota(jnp.int32, (8, CG), 0)

        def blk(i, carry):
            r0 = pl.multiple_of(i * 16, 16)
            for g in range(D // CG):
                cs = slice(g * CG, (g + 1) * CG)
                ls = ls_all[:, cs]
                hprev = hc[:, cs]
                hs = []
                for sb in range(2):
                    rr = r0 + 8 * sb
                    first = (row + (t * tt + rr)) == 0
                    _, ig, _, a, _, mult = _lru_gates(rp_s[pl.ds(rr, 8), cs], ip_s[pl.ds(rr, 8), cs], ls, first)
                    b = mult * (ig * u_s[pl.ds(rr, 8), cs])
                    for s in (1, 2, 4):
                        a_sh = jnp.where(row >= s, pltpu.roll(a, s, 0), 1.0)
                        b_sh = jnp.where(row >= s, pltpu.roll(b, s, 0), 0.0)
                        b = a * b_sh + b
                        a = a * a_sh
                    hv = a * hprev + b
                    hprev = jnp.broadcast_to(hv[7:8, :], hv.shape)
                    hs.append(hv)
                hc[:, cs] = hprev
                h16 = jnp.concatenate(hs, axis=0)
                rows = pl.ds(r0, 16)
                gl, _ = _gelu(proj_ref[4, rows, cs].astype(F32))
                y_b = h16 * gl
                y_a = proj_ref[0, rows, cs].astype(F32) * ya_s[rows, cs]
                mg = _sig(proj_ref[5, rows, cs].astype(F32)) * y_a + _sig(proj_ref[6, rows, cs].astype(F32)) * y_b
                mg_ref[rows, cs] = mg.astype(BF16)
                hl_ref[rows, cs] = h16.astype(BF16)
            return carry

        lax.fori_loop(0, tt // 16, blk, 0)

    return pl.pallas_call(
        body, name="mixer_fwd", grid=(t_len // tt,),
        in_specs=[pl.BlockSpec((7, tt, D), lambda t: (0, t, 0)),
                  pl.BlockSpec((16, D), lambda t: (0, 0)),
                  pl.BlockSpec((HEADS, HB, HB), lambda t: (0, 0, 0)),
                  pl.BlockSpec((HEADS, HB, HB), lambda t: (0, 0, 0))],
        out_specs=[pl.BlockSpec((tt, D), lambda t: (t, 0)), pl.BlockSpec((tt, D), lambda t: (t, 0))],
        out_shape=[jax.ShapeDtypeStruct((t_len, D), BF16), jax.ShapeDtypeStruct((t_len, D), BF16)],
        scratch_shapes=[pltpu.VMEM((tt + 8, D), F32), pltpu.VMEM((tt + 8, D), F32), pltpu.VMEM((8, D), F32),
                        pltpu.VMEM((tt, D), F32), pltpu.VMEM((tt, D), F32), pltpu.VMEM((tt, D), F32),
                        pltpu.VMEM((tt, D), F32)],
        compiler_params=_cp("arbitrary"),
    )(proj, prm, wa, wx)


def _out_proj(merged, x, mod, g_ffn, w_out):
    t_len = x.shape[0]
    tm = min(TM, t_len)

    def body(mg_ref, x_ref, mod_ref, g_ref, w_ref, x1_ref, h2_ref):
        x1 = x_ref[...] + mod_ref[2:3, :] * _dot(mg_ref[...], w_ref[...])
        x1_ref[...] = x1
        r = lax.rsqrt(jnp.mean(x1 * x1, axis=-1, keepdims=True) + EPS)
        h2_ref[...] = (x1 * r * g_ref[...] * (1.0 + mod_ref[4:5, :]) + mod_ref[3:4, :]).astype(BF16)

    return pl.pallas_call(
        body, name="out_proj", grid=(t_len // tm,),
        in_specs=[pl.BlockSpec((tm, D), lambda i: (i, 0)), pl.BlockSpec((tm, D), lambda i: (i, 0)),
                  pl.BlockSpec((8, D), lambda i: (0, 0)), pl.BlockSpec((1, D), lambda i: (0, 0)),
                  pl.BlockSpec((D, D), lambda i: (0, 0))],
        out_specs=[pl.BlockSpec((tm, D), lambda i: (i, 0)), pl.BlockSpec((tm, D), lambda i: (i, 0))],
        out_shape=[jax.ShapeDtypeStruct((t_len, D), F32), jax.ShapeDtypeStruct((t_len, D), BF16)],
        compiler_params=_cp("parallel"),
    )(merged, x, mod, g_ffn, w_out)


def _ffn_fwd(h2, x1, target, mod, g_fin, w_gu, w_down):
    t_len = x1.shape[0]
    tm = min(TM, t_len)

    def body(h2_ref, x1_ref, tg_ref, mod_ref, g_ref, wgu_ref, wd_ref, gu_ref, dx2_ref, dx2b_ref, loss_ref, dg_ref, acc):
        i, j = pl.program_id(0), pl.program_id(1)

        @pl.when((i == 0) & (j == 0))
        def _():
            loss_ref[...] = jnp.zeros_like(loss_ref)
            dg_ref[...] = jnp.zeros_like(dg_ref)

        hb = h2_ref[...]
        gate = _dot(hb, wgu_ref[0, 0])
        up = _dot(hb, wgu_ref[1, 0])
        gu_ref[0, 0] = gate.astype(BF16)
        gu_ref[1, 0] = up.astype(BF16)
        act = (gate * _sig(gate) * up).astype(BF16)
        part = _dot(act, wd_ref[...])

        @pl.when(j == 0)
        def _():
            acc[...] = part

        @pl.when(j > 0)
        def _():
            acc[...] += part

        @pl.when(j == 3)
        def _():
            x2 = x1_ref[...] + mod_ref[5:6, :] * acc[...]
            r = lax.rsqrt(jnp.mean(x2 * x2, axis=-1, keepdims=True) + EPS)
            xn = x2 * r
            gf = g_ref[...]
            diff = xn * gf - tg_ref[...]
            loss_ref[...] += jnp.sum(diff * diff) * (0.5 / D)
            dy = diff * (1.0 / D)
            dg_ref[...] += jnp.sum(dy * xn, axis=0, keepdims=True)
            dxn = dy * gf
            dx2 = r * (dxn - xn * jnp.mean(dxn * xn, axis=-1, keepdims=True))
            dx2_ref[...] = dx2
            dx2b_ref[...] = dx2.astype(BF16)

    return pl.pallas_call(
        body, name="ffn_fwd", grid=(t_len // tm, 4),
        in_specs=[pl.BlockSpec((tm, D), lambda i, j: (i, 0)), pl.BlockSpec((tm, D), lambda i, j: (i, 0)),
                  pl.BlockSpec((tm, D), lambda i, j: (i, 0)), pl.BlockSpec((8, D), lambda i, j: (0, 0)),
                  pl.BlockSpec((1, D), lambda i, j: (0, 0)),
                  pl.BlockSpec((2, 1, D, FB), lambda i, j: (0, j, 0, 0)),
                  pl.BlockSpec((FB, D), lambda i, j: (j, 0))],
        out_specs=[pl.BlockSpec((2, 1, tm, FB), lambda i, j: (0, j, i, 0)),
                   pl.BlockSpec((tm, D), lambda i, j: (i, 0)),
                   pl.BlockSpec((tm, D), lambda i, j: (i, 0)),
                   pl.BlockSpec((1, 128), lambda i, j: (0, 0)),
                   pl.BlockSpec((1, D), lambda i, j: (0, 0))],
        out_shape=[jax.ShapeDtypeStruct((2, 4, t_len, FB), BF16), jax.ShapeDtypeStruct((t_len, D), F32),
                   jax.ShapeDtypeStruct((t_len, D), BF16),
                   jax.ShapeDtypeStruct((1, 128), F32), jax.ShapeDtypeStruct((1, D), F32)],
        scratch_shapes=[pltpu.VMEM((tm, D), F32)],
        compiler_params=_cp("arbitrary", "arbitrary"),
    )(h2, x1, target, mod, g_fin, w_gu, w_down)


S_SH, S_SC, S_G = 0, 1, 2


def _ffn_bwd(dx2, gu, x1, mod, g_ffn, w_gu, w_down, w_out):
    t_len = x1.shape[0]
    tm = min(TM, t_len)

    def body(dx2_ref, gu_ref, x1_ref, mod_ref, g_ref, wgu_ref, wd_ref, wo_ref,
             dgu_ref, act_ref, dx1_ref, dx1b_ref, dmg_ref, sums_ref, acc, dffn):
        i, j = pl.program_id(0), pl.program_id(1)

        @pl.when((i == 0) & (j == 0))
        def _():
            sums_ref[...] = jnp.zeros_like(sums_ref)

        @pl.when(j == 0)
        def _():
            dffn[...] = (dx2_ref[...] * mod_ref[5:6, :]).astype(BF16)

        dact = _dot_nt(dffn[...], wd_ref[...])
        gate = gu_ref[0, 0].astype(F32)
        up = gu_ref[1, 0].astype(F32)
        sg = _sig(gate)
        silu = gate * sg
        act_ref[0] = (silu * up).astype(BF16)
        dgate = (dact * up * (sg * (1.0 + gate * (1.0 - sg)))).astype(BF16)
        dup = (dact * silu).astype(BF16)
        dgu_ref[0, 0] = dgate
        dgu_ref[1, 0] = dup
        part = _dot_nt(dgate, wgu_ref[0, 0]) + _dot_nt(dup, wgu_ref[1, 0])

        @pl.when(j == 0)
        def _():
            acc[...] = part

        @pl.when(j > 0)
        def _():
            acc[...] += part

        @pl.when(j == 3)
        def _():
            dh2 = acc[...]
            x1 = x1_ref[...]
            r = lax.rsqrt(jnp.mean(x1 * x1, axis=-1, keepdims=True) + EPS)
            xn = x1 * r
            gf = g_ref[...]
            sums_ref[S_SH:S_SH + 1, :] += jnp.sum(dh2, axis=0, keepdims=True)
            sums_ref[S_SC:S_SC + 1, :] += jnp.sum(dh2 * (xn * gf), axis=0, keepdims=True)
            dhn = dh2 * (1.0 + mod_ref[4:5, :])
            sums_ref[S_G:S_G + 1, :] += jnp.sum(dhn * xn, axis=0, keepdims=True)
            dxn = dhn * gf
            dx1 = dx2_ref[...] + r * (dxn - xn * jnp.mean(dxn * xn, axis=-1, keepdims=True))
            dx1_ref[...] = dx1
            dx1b_ref[...] = dx1.astype(BF16)
            dmg_ref[...] =_dot_nt((dx1 * mod_ref[2:3, :]).astype(BF16), wo_ref[...]).astype(BF16)

    return pl.pallas_call(
        body, name="ffn_bwd", grid=(t_len // tm, 4),
        in_specs=[pl.BlockSpec((tm, D), lambda i, j: (i, 0)),
                  pl.BlockSpec((2, 1, tm, FB), lambda i, j: (0, j, i, 0)),
                  pl.BlockSpec((tm, D), lambda i, j: (i, 0)),
                  pl.BlockSpec((8, D), lambda i, j: (0, 0)), pl.BlockSpec((1, D), lambda i, j: (0, 0)),
                  pl.BlockSpec((2, 1, D, FB), lambda i, j: (0, j, 0, 0)),
                  pl.BlockSpec((FB, D), lambda i, j: (j, 0)),
                  pl.BlockSpec((D, D), lambda i, j: (0, 0))],
        out_specs=[pl.BlockSpec((2, 1, tm, FB), lambda i, j: (0, j, i, 0)),
                   pl.BlockSpec((1, tm, FB), lambda i, j: (j, i, 0)),
                   pl.BlockSpec((tm, D), lambda i, j: (i, 0)),
                   pl.BlockSpec((tm, D), lambda i, j: (i, 0)),
                   pl.BlockSpec((tm, D), lambda i, j: (i, 0)),
                   pl.BlockSpec((8, D), lambda i, j: (0, 0))],
        out_shape=[jax.ShapeDtypeStruct((2, 4, t_len, FB), BF16), jax.ShapeDtypeStruct((4, t_len, FB), BF16),
                   jax.ShapeDtypeStruct((t_len, D), F32), jax.ShapeDtypeStruct((t_len, D), BF16),
                   jax.ShapeDtypeStruct((t_len, D), BF16), jax.ShapeDtypeStruct((8, D), F32)],
        scratch_shapes=[pltpu.VMEM((tm, D), F32), pltpu.VMEM((tm, D), BF16)],
        compiler_params=_cp("arbitrary", "arbitrary"),
    )(dx2, gu, x1, mod, g_ffn, w_gu, w_down, w_out)


def _my_pos():
    return lax.axis_index("x"), lax.axis_index("y"), lax.axis_index("c")


def _my_index():
    x, y, c = _my_pos()
    return 4 * x + 2 * y + c


def _device_of(b):
    return (b >> 2) & 1, (b >> 1) & 1, b & 1


def _rs_send(src, parts_ref, b, send_sems, recv_sems, local_sem):
    me = _my_index()
    dst = parts_ref.at[me]

    @pl.when(b == me)
    def _():
        pltpu.make_async_copy(src, dst, local_sem).start()

    @pl.when(b != me)
    def _():
        pltpu.make_async_remote_copy(src_ref=src, dst_ref=dst, send_sem=send_sems.at[b], recv_sem=recv_sems.at[me],
                                     device_id=_device_of(b), device_id_type=MESH).start()


def _rs_finish(src_of, parts_ref, send_sems, recv_sems, local_sem):
    me = _my_index()
    for s in range(NDEV):
        @pl.when(s != me)
        def _():
            cp = pltpu.make_async_remote_copy(src_ref=src_of(s), dst_ref=parts_ref.at[s], send_sem=send_sems.at[s],
                                              recv_sem=recv_sems.at[s], device_id=_device_of(s), device_id_type=MESH)
            cp.wait_send()
            cp.wait_recv()

        @pl.when(s == me)
        def _():
            pltpu.make_async_copy(src_of(s), parts_ref.at[s], local_sem).wait()


_RS_SEMS = [pltpu.SemaphoreType.DMA((NDEV,)), pltpu.SemaphoreType.DMA((NDEV,)), pltpu.SemaphoreType.DMA]
_ANY = pl.BlockSpec(memory_space=pl.ANY)


def _xor_order(me, n):
    return (me ^ (n - 1 - jnp.arange(n, dtype=jnp.int32))).astype(jnp.int32)


def _gu_wgrad(h2, dgu, order):
    t_len = h2.shape[0]
    tk = min(TK, t_len)
    nk = t_len // tk

    def body(ord_ref, h_ref, d_ref, parts_ref, acc, stage, send_sems, recv_sems, local_sem):
        p, k = pl.program_id(0), pl.program_id(1)
        part = _dot_tn(h_ref[...], d_ref[0])

        @pl.when(k == 0)
        def _():
            acc[...] = part

        @pl.when(k > 0)
        def _():
            acc[...] += part

        @pl.when(k == nk - 1)
        def _():
            b = ord_ref[p]
            stage[b] = acc[...].astype(BF16)
            _rs_send(stage.at[b], parts_ref, b, send_sems, recv_sems, local_sem)

        @pl.when((p == NDEV - 1) & (k == nk - 1))
        def _():
            _rs_finish(lambda s: stage.at[s], parts_ref, send_sems, recv_sems, local_sem)

    return pl.pallas_call(
        body, name="gu_wgrad",
        grid_spec=pltpu.PrefetchScalarGridSpec(
            num_scalar_prefetch=1, grid=(NDEV, nk),
            in_specs=[pl.BlockSpec((tk, D), lambda p, k, o: (k, 0)),
                      pl.BlockSpec((1, tk, FB), lambda p, k, o: (o[p], k, 0))],
            out_specs=_ANY,
            scratch_shapes=[pltpu.VMEM((D, FB), F32), pltpu.VMEM((NDEV, D, FB), BF16)] + _RS_SEMS),
        out_shape=jax.ShapeDtypeStruct((NDEV, D, FB), BF16),
        compiler_params=_cp("arbitrary", "arbitrary"),
    )(order, h2, dgu.reshape(NDEV, t_len, FB))


def _scaled_wgrad(name, a, dx, w, gate_row, mod, order):
    nb, t_len, kb = a.shape
    tk = min(TK, t_len)
    nk = t_len // tk
    per = NDEV // nb
    rows = kb // per

    def body(ord_ref, a_ref, dx_ref, w_ref, mod_ref, parts_ref, dg_ref, acc, stage, send_sems, recv_sems, local_sem):
        p, k = pl.program_id(0), pl.program_id(1)
        j = ord_ref[p]

        @pl.when((p == 0) & (k == 0))
        def _():
            dg_ref[...] = jnp.zeros_like(dg_ref)

        part = _dot_tn(a_ref[0], dx_ref[...])

        @pl.when(k == 0)
        def _():
            acc[...] = part

        @pl.when(k > 0)
        def _():
            acc[...] += part

        def src_of(blk, q):
            return stage.at[blk, pl.ds(q * rows, rows), :]

        @pl.when(k == nk - 1)
        def _():
            z = acc[...]
            stage[j] = (z * mod_ref[gate_row:gate_row + 1, :]).astype(BF16)
            dg_ref[0:1, :] += jnp.sum(z * w_ref[...].astype(F32), axis=0, keepdims=True)
            for q in range(per):
                _rs_send(src_of(j, q), parts_ref, j * per + q, send_sems, recv_sems, local_sem)

        @pl.when((p == nb - 1) & (k == nk - 1))
        def _():
            _rs_finish(lambda s: src_of(s // per, s % per), parts_ref, send_sems, recv_sems, local_sem)

    return pl.pallas_call(
        body, name=name,
        grid_spec=pltpu.PrefetchScalarGridSpec(
            num_scalar_prefetch=1, grid=(nb, nk),
            in_specs=[pl.BlockSpec((1, tk, kb), lambda p, k, o: (o[p], k, 0)),
                      pl.BlockSpec((tk, D), lambda p, k, o: (k, 0)),
                      pl.BlockSpec((kb, D), lambda p, k, o: (o[p], 0)),
                      pl.BlockSpec((8, D), lambda p, k, o: (0, 0))],
            out_specs=[_ANY, pl.BlockSpec((8, D), lambda p, k, o: (0, 0))],
            scratch_shapes=[pltpu.VMEM((kb, D), F32), pltpu.VMEM((nb, kb, D), BF16)] + _RS_SEMS),
        out_shape=[jax.ShapeDtypeStruct((NDEV, rows, D), BF16), jax.ShapeDtypeStruct((8, D), F32)],
        compiler_params=_cp("arbitrary", "arbitrary"),
    )(order, a, dx, w, mod)


M_WA, M_WB, M_CBIAS, M_BA, M_BX, M_LS = 0, 3, 7, 8, 9, 10


def _mixer_bwd(proj, hl, dmg, prm, wa, wx):
    t_len = proj.shape[1]
    tt = min(TT, t_len)
    nt = t_len // tt
    hb8 = tt // 8

    def rev(i):
        return nt - 1 - i

    def halo(i):
        return jnp.maximum(rev(i) * hb8 - 1, 0)

    def body(proj_ref, ph_ref, hl_ref, hh_ref, dmg_ref, prm_ref, wa_ref, wx_ref,
             dp_ref, sums_ref, gwa_ref, gwx_ref,
             xe, ve, he, u_s, ya_s, rp_s, ip_s, due, dye, drp_s, dip_s, an, gn):
        i = pl.program_id(0)
        t = rev(i)

        @pl.when(i == 0)
        def _():
            sums_ref[...] = jnp.zeros_like(sums_ref)
            gwa_ref[...] = jnp.zeros_like(gwa_ref)
            gwx_ref[...] = jnp.zeros_like(gwx_ref)
            due[tt:tt + 8, :] = jnp.zeros((8, D), F32)
            dye[tt:tt + 8, :] = jnp.zeros((8, D), F32)
            an[...] = jnp.zeros((8, D), F32)
            gn[...] = jnp.zeros((8, D), F32)

        live = (t > 0).astype(F32)
        xe[0:8, :] = ph_ref[3].astype(F32) * live
        ve[0:8, :] = ph_ref[1].astype(F32) * ph_ref[2].astype(F32) * live
        he[0:8, :] = hh_ref[...].astype(F32) * live
        xe[8:8 + tt, :] = proj_ref[3].astype(F32)
        ve[8:8 + tt, :] = proj_ref[1].astype(F32) * proj_ref[2].astype(F32)
        he[8:8 + tt, :] = hl_ref[...].astype(F32)
        u = prm_ref[P_CBIAS:P_CBIAS + 1, :] + prm_ref[P_WB:P_WB + 1, :] * xe[5:5 + tt, :]
        for k in range(1, 4):
            u = u + prm_ref[P_WB + k:P_WB + k + 1, :] * xe[5 + k:5 + k + tt, :]
        u_s[...] = u
        ya = prm_ref[P_WA:P_WA + 1, :] * ve[6:6 + tt, :]
        for k in range(1, 3):
            ya = ya + prm_ref[P_WA + k:P_WA + k + 1, :] * ve[6 + k:6 + k + tt, :]
        ya_s[...] = ya
        ub = u.astype(BF16)
        for h in range(HEADS):
            cs = slice(h * HB, (h + 1) * HB)
            rp_s[:, cs] = _dot(ub[:, cs], wa_ref[h]) + prm_ref[P_BA:P_BA + 1, cs]
            ip_s[:, cs] = _dot(ub[:, cs], wx_ref[h]) + prm_ref[P_BX:P_BX + 1, cs]

        ls_all = _log_sigmoid(prm_ref[P_LAM:P_LAM + 1, :])
        row = lax.broadcasted_iota(jnp.int32, (8, CG), 0)
        nblk = tt // 16

        def blk(ib, carry):
            r0 = pl.multiple_of((nblk - 1 - ib) * 16, 16)
            rows = pl.ds(r0, 16)
            for g in range(D // CG):
                cs = slice(g * CG, (g + 1) * CG)
                ls = ls_all[:, cs]
                dm = dmg_ref[rows, cs].astype(F32)
                cb = proj_ref[0, rows, cs].astype(F32)
                rg = proj_ref[4, rows, cs].astype(F32)
                sga = _sig(proj_ref[5, rows, cs].astype(F32))
                sgb = _sig(proj_ref[6, rows, cs].astype(F32))
                ya0 = ya_s[rows, cs]
                h16 = he[pl.ds(r0 + 8, 16), cs]
                gl, th = _gelu(rg)
                dgl = 0.5 * (1.0 + th) + 0.5 * rg * (1.0 - th * th) * (_GC * (1.0 + 3.0 * 0.044715 * rg * rg))
                y_a = cb * ya0
                y_b = h16 * gl
                dy_a = dm * sga
                dy_b = dm * sgb
                col = lambda s: slice(s * D + g * CG, s * D + (g + 1) * CG)
                dp_ref[rows, col(5)] = (dm * y_a * sga * (1.0 - sga)).astype(BF16)
                dp_ref[rows, col(6)] = (dm * y_b * sgb * (1.0 - sgb)).astype(BF16)
                dp_ref[rows, col(4)] = (dy_b * h16 * dgl).astype(BF16)
                dp_ref[rows, col(0)] = (dy_a * ya0).astype(BF16)
                dye[rows, cs] = dy_a * cb
                dh16 = dy_b * gl

                a_next = an[:, cs]
                g_next = gn[:, cs]
                s_ba = jnp.zeros((8, CG), F32)
                s_bx = jnp.zeros((8, CG), F32)
                s_ls = jnp.zeros((8, CG), F32)
                for sb in (1, 0):
                    rr = r0 + 8 * sb
                    first = (row + (t * tt + rr)) == 0
                    uu = u_s[pl.ds(rr, 8), cs]
                    r, ig, la, a, m2, mult = _lru_gates(rp_s[pl.ds(rr, 8), cs], ip_s[pl.ds(rr, 8), cs], ls, first)
                    ca = jnp.where(row < 7, pltpu.roll(a, 7, 0), a_next)
                    cb_ = dh16[8 * sb:8 * sb + 8, :]
                    for s in (1, 2, 4):
                        a_sh = jnp.where(row < 8 - s, pltpu.roll(ca, 8 - s, 0), 1.0)
                        b_sh = jnp.where(row < 8 - s, pltpu.roll(cb_, 8 - s, 0), 0.0)
                        cb_ = ca * b_sh + cb_
                        ca = ca * a_sh
                    gv = ca * g_next + cb_
                    g_next = jnp.broadcast_to(gv[0:1, :], gv.shape)
                    a_next = jnp.broadcast_to(a[0:1, :], a.shape)
                    hprev = jnp.where(row >= 1, pltpu.roll(he[pl.ds(rr + 8, 8), cs], 1, 0),
                                      pltpu.roll(he[pl.ds(rr, 8), cs], 1, 0))
                    da = gv * hprev
                    dmult = jnp.where(first, 0.0, gv * ig * uu)
                    dla = da * a + jnp.where(m2 > 0.0, dmult * (-(a * a) / mult), 0.0)
                    drp = dla * (LRU_C * ls) * r * (1.0 - r)
                    dip = gv * mult * uu * ig * (1.0 - ig)
                    s_ls = s_ls + dla * (LRU_C * r)
                    s_ba = s_ba + drp
                    s_bx = s_bx + dip
                    drp_s[pl.ds(rr, 8), cs] = drp
                    dip_s[pl.ds(rr, 8), cs] = dip
                    due[pl.ds(rr, 8), cs] = gv * mult * ig
                an[:, cs] = a_next
                gn[:, cs] = g_next
                sums_ref[M_BA:M_BA + 1, cs] += jnp.sum(s_ba, axis=0, keepdims=True)
                sums_ref[M_BX:M_BX + 1, cs] += jnp.sum(s_bx, axis=0, keepdims=True)
                sums_ref[M_LS:M_LS + 1, cs] += jnp.sum(s_ls, axis=0, keepdims=True)
            return carry

        lax.fori_loop(0, nblk, blk, 0)

        drp_b = drp_s[...].astype(BF16)
        dip_b = dip_s[...].astype(BF16)
        for h in range(HEADS):
            cs = slice(h * HB, (h + 1) * HB)
            due[0:tt, cs] += _dot_nt(drp_b[:, cs], wa_ref[h]) + _dot_nt(dip_b[:, cs], wx_ref[h])
            gwa_ref[h] += _dot_tn(ub[:, cs], drp_b[:, cs])
            gwx_ref[h] += _dot_tn(ub[:, cs], dip_b[:, cs])

        du = due[0:tt, :]
        sums_ref[M_CBIAS:M_CBIAS + 1, :] += jnp.sum(du, axis=0, keepdims=True)
        drx = prm_ref[P_WB:P_WB + 1, :] * due[3:3 + tt, :]
        sums_ref[M_WB:M_WB + 1, :] += jnp.sum(du * xe[5:5 + tt, :], axis=0, keepdims=True)
        for k in range(1, 4):
            drx = drx + prm_ref[P_WB + k:P_WB + k + 1, :] * due[3 - k:3 - k + tt, :]
            sums_ref[M_WB + k:M_WB + k + 1, :] += jnp.sum(du * xe[5 + k:5 + k + tt, :], axis=0, keepdims=True)
        dp_ref[:, 3 * D:4 * D] = drx.astype(BF16)
        dya = dye[0:tt, :]
        dv = prm_ref[P_WA:P_WA + 1, :] * dye[2:2 + tt, :]
        sums_ref[M_WA:M_WA + 1, :] += jnp.sum(dya * ve[6:6 + tt, :], axis=0, keepdims=True)
        for k in range(1, 3):
            dv = dv + prm_ref[P_WA + k:P_WA + k + 1, :] * dye[2 - k:2 - k + tt, :]
            sums_ref[M_WA + k:M_WA + k + 1, :] += jnp.sum(dya * ve[6 + k:6 + k + tt, :], axis=0, keepdims=True)
        dp_ref[:, D:2 * D] = (dv * proj_ref[2].astype(F32)).astype(BF16)
        dp_ref[:, 2 * D:3 * D] = (dv * proj_ref[1].astype(F32)).astype(BF16)
        due[tt:tt + 8, :] = due[0:8, :]
        dye[tt:tt + 8, :] = dye[0:8, :]

        @pl.when(i == nt - 1)
        def _():
            sums_ref[M_LS:M_LS + 1, :] = sums_ref[M_LS:M_LS + 1, :] * _sig(-prm_ref[P_LAM:P_LAM + 1, :])

    big = lambda: pltpu.VMEM((tt + 8, D), F32)
    tile = lambda: pltpu.VMEM((tt, D), F32)
    return pl.pallas_call(
        body, name="mixer_bwd", grid=(nt,),
        in_specs=[pl.BlockSpec((7, tt, D), lambda i: (0, rev(i), 0)),
                  pl.BlockSpec((7, 8, D), lambda i: (0, halo(i), 0)),
                  pl.BlockSpec((tt, D), lambda i: (rev(i), 0)),
                  pl.BlockSpec((8, D), lambda i: (halo(i), 0)),
                  pl.BlockSpec((tt, D), lambda i: (rev(i), 0)),
                  pl.BlockSpec((16, D), lambda i: (0, 0)),
                  pl.BlockSpec((HEADS, HB, HB), lambda i: (0, 0, 0)),
                  pl.BlockSpec((HEADS, HB, HB), lambda i: (0, 0, 0))],
        out_specs=[pl.BlockSpec((tt, 7 * D), lambda i: (rev(i), 0)),
                   pl.BlockSpec((16, D), lambda i: (0, 0)),
                   pl.BlockSpec((HEADS, HB, HB), lambda i: (0, 0, 0)),
                   pl.BlockSpec((HEADS, HB, HB), lambda i: (0, 0, 0))],
        out_shape=[jax.ShapeDtypeStruct((t_len, 7 * D), BF16), jax.ShapeDtypeStruct((16, D), F32),
                   jax.ShapeDtypeStruct((HEADS, HB, HB), F32), jax.ShapeDtypeStruct((HEADS, HB, HB), F32)],
        scratch_shapes=[big(), big(), big(), tile(), tile(), tile(), tile(), big(), big(), tile(), tile(),
                        pltpu.VMEM((8, D), F32), pltpu.VMEM((8, D), F32)],
        compiler_params=_cp("arbitrary"),
    )(proj, proj, hl, hl, dmg, prm, wa, wx)


def _in_proj_bwd(dproj, w_in, x, dx1, mod, g_mix):
    t_len = x.shape[0]
    tm = min(TM, t_len)

    def body(dp_ref, w_ref, x_ref, dx1_ref, mod_ref, g_ref, gx_ref, sums_ref, acc):
        i, s = pl.program_id(0), pl.program_id(1)

        @pl.when((i == 0) & (s == 0))
        def _():
            sums_ref[...] = jnp.zeros_like(sums_ref)

        part = _dot_nt(dp_ref[...], w_ref[...])

        @pl.when(s == 0)
        def _():
            acc[...] = part

        @pl.when(s > 0)
        def _():
            acc[...] += part

        @pl.when(s == 6)
        def _():
            dh = acc[...]
            xv = x_ref[...]
            r = lax.rsqrt(jnp.mean(xv * xv, axis=-1, keepdims=True) + EPS)
            xn = xv * r
            gf = g_ref[...]
            sums_ref[S_SH:S_SH + 1, :] += jnp.sum(dh, axis=0, keepdims=True)
            sums_ref[S_SC:S_SC + 1, :] += jnp.sum(dh * (xn * gf), axis=0, keepdims=True)
            dhn = dh * (1.0 + mod_ref[1:2, :])
            sums_ref[S_G:S_G + 1, :] += jnp.sum(dhn * xn, axis=0, keepdims=True)
            dxn = dhn * gf
            gx_ref[...] = dx1_ref[...] + r * (dxn - xn * jnp.mean(dxn * xn, axis=-1, keepdims=True))

    return pl.pallas_call(
        body, name="in_proj_bwd", grid=(t_len // tm, 7),
        in_specs=[pl.BlockSpec((tm, D), lambda i, s: (i, s)),
                  pl.BlockSpec((D, D), lambda i, s: (0, s)),
                  pl.BlockSpec((tm, D), lambda i, s: (i, 0)), pl.BlockSpec((tm, D), lambda i, s: (i, 0)),
                  pl.BlockSpec((8, D), lambda i, s: (0, 0)), pl.BlockSpec((1, D), lambda i, s: (0, 0))],
        out_specs=[pl.BlockSpec((tm, D), lambda i, s: (i, 0)), pl.BlockSpec((8, D), lambda i, s: (0, 0))],
        out_shape=[jax.ShapeDtypeStruct((t_len, D), F32), jax.ShapeDtypeStruct((8, D), F32)],
        scratch_shapes=[pltpu.VMEM((tm, D), F32)],
        compiler_params=_cp("arbitrary", "arbitrary"),
    )(dproj, w_in, x, dx1, mod, g_mix)


def _in_wgrad(h, dproj, g_wa, g_wx, order):
    t_len = h.shape[0]
    tk = min(TKI, t_len)
    nk = t_len // tk
    nq = 4
    cw = 7 * D // NDEV
    hr = HB // NDEV

    def body(ord_ref, h_ref, d_ref, ga_ref, gx_ref, parts_ref, pa_ref, px_ref, acc, stage, *sems):
        p, k = pl.program_id(0), pl.program_id(1)
        q = ord_ref[p]

        def head_rows(ref):
            return lambda s: ref.at[:, pl.ds(s * hr, hr), :]

        @pl.when((p == 0) & (k == 0))
        def _():
            for s in range(NDEV):
                _rs_send(head_rows(ga_ref)(s), pa_ref, s, *sems[3:6])
                _rs_send(head_rows(gx_ref)(s), px_ref, s, *sems[6:9])

        part = _dot_tn(h_ref[...], d_ref[...])

        @pl.when(k == 0)
        def _():
            acc[...] = part

        @pl.when(k > 0)
        def _():
            acc[...] += part

        def src_of(blk, r):
            return stage.at[blk, :, pl.ds(r * cw, cw)]

        @pl.when(k == nk - 1)
        def _():
            stage[q] = acc[...].astype(BF16)
            for r in range(2):
                _rs_send(src_of(q, r), parts_ref, q * 2 + r, *sems[0:3])

        @pl.when((p == nq - 1) & (k == nk - 1))
        def _():
            _rs_finish(lambda s: src_of(s // 2, s % 2), parts_ref, *sems[0:3])
            _rs_finish(head_rows(ga_ref), pa_ref, *sems[3:6])
            _rs_finish(head_rows(gx_ref), px_ref, *sems[6:9])

    return pl.pallas_call(
        body, name="in_wgrad",
        grid_spec=pltpu.PrefetchScalarGridSpec(
            num_scalar_prefetch=1, grid=(nq, nk),
            in_specs=[pl.BlockSpec((tk, D), lambda p, k, o: (k, 0)),
                      pl.BlockSpec((tk, 2 * cw), lambda p, k, o: (k, o[p])), _ANY, _ANY],
            out_specs=[_ANY, _ANY, _ANY],
            scratch_shapes=[pltpu.VMEM((D, 2 * cw), F32), pltpu.VMEM((nq, D, 2 * cw), BF16)] + _RS_SEMS * 3),
        out_shape=[jax.ShapeDtypeStruct((NDEV, D, cw), BF16), jax.ShapeDtypeStruct((NDEV, HEADS, hr, HB), F32),
                   jax.ShapeDtypeStruct((NDEV, HEADS, hr, HB), F32)],
        compiler_params=_cp("arbitrary", "arbitrary"),
    )(order, h, dproj, g_wa, g_wx)


def _ada_fwd(c_all, w_ada, b_cols):
    def body(c_ref, w_ref, b_ref, o_ref):
        cv = c_ref[...]
        o_ref[...] = _dot((cv * _sig(cv)).astype(BF16), w_ref[...].astype(BF16)) + b_ref[...]

    return pl.pallas_call(body, name="ada_fwd", out_shape=jax.ShapeDtypeStruct((16, w_ada.shape[1]), F32),
                          compiler_params=_cp())(c_all, w_ada, b_cols)


def _adam_math(w, g, m, v):
    m = ADAM_B1 * m + (1.0 - ADAM_B1) * g
    v = ADAM_B2 * v + (1.0 - ADAM_B2) * (g * g)
    m_hat = m / (1.0 - ADAM_B1 ** ADAM_STEP)
    v_hat = v / (1.0 - ADAM_B2 ** ADAM_STEP)
    delta = -ADAM_LR * (m_hat / (jnp.sqrt(v_hat) + ADAM_EPS) + ADAM_WD * w)
    return delta, m, v


def _ada_bwd(c_all, dmod_cols, w, m, v):
    rb = 256
    n = w.shape[1]
    nrow = c_all.shape[0]

    def body(c_ref, d_ref, w_ref, m_ref, v_ref, g_ref, dl_ref, nm_ref, nv_ref):
        cv = c_ref[...]
        g = _dot_tn((cv * _sig(cv)).astype(BF16), d_ref[...].astype(BF16))
        g_ref[...] = g
        dl_ref[...], nm_ref[...], nv_ref[...] = _adam_math(w_ref[...], g, m_ref[...], v_ref[...])

    blk = pl.BlockSpec((rb, n), lambda i: (i, 0))
    sds = jax.ShapeDtypeStruct(w.shape, F32)
    return pl.pallas_call(
        body, name="ada_bwd", grid=(D // rb,),
        in_specs=[pl.BlockSpec((nrow, rb), lambda i: (0, i)), pl.BlockSpec((nrow, n), lambda i: (0, 0)), blk, blk, blk],
        out_specs=[blk, blk, blk, blk], out_shape=[sds, sds, sds, sds],
        compiler_params=_cp("parallel"),
    )(c_all, dmod_cols, w, m, v)


def _adam(name, parts, w, m, v):
    p, r, c = parts.shape
    rb = r
    for cand in (256, 128, 64, 32, 16, 8):
        if r % cand == 0 and r >= cand:
            rb = cand
            break

    def body(p_ref, w_ref, m_ref, v_ref, g_ref, dl_ref, nm_ref, nv_ref):
        g = p_ref[0].astype(F32)
        for q in range(1, p):
            g = g + p_ref[q].astype(F32)
        g_ref[...] = g
        dl_ref[...], nm_ref[...], nv_ref[...] = _adam_math(w_ref[...], g, m_ref[...], v_ref[...])

    blk = pl.BlockSpec((rb, c), lambda i: (i, 0))
    sds = jax.ShapeDtypeStruct((r, c), F32)
    return pl.pallas_call(
        body, name=name, grid=(r // rb,),
        in_specs=[pl.BlockSpec((p, rb, c), lambda i: (0, i, 0)), blk, blk, blk],
        out_specs=[blk, blk, blk, blk], out_shape=[sds, sds, sds, sds],
        compiler_params=_cp("parallel"),
    )(parts, w, m, v)


def _my_pos():
    return lax.axis_index("x"), lax.axis_index("y"), lax.axis_index("c")


def _all_gather_small(name, v):
    m_per, n = v.shape

    def body(x_ref, out_ref, send_sems, recv_sems, local_sem):
        x, y, c = _my_pos()
        me, sibling = (x, y, c), (x, y, 1 - c)
        chips = [(1 - x, y), (x, 1 - y), (1 - x, 1 - y)]

        def rows(px, py, pc):
            return out_ref.at[pl.ds((4 * px + 2 * py + pc) * m_per, m_per), :]

        def copy(k, block, to, src=None):
            return pltpu.make_async_remote_copy(
                src_ref=rows(*block) if src is None else src, dst_ref=rows(*block),
                send_sem=send_sems.at[k], recv_sem=recv_sems.at[k], device_id=to, device_id_type=MESH)

        mine = pltpu.make_async_copy(x_ref, rows(*me), local_sem)
        mine.start()
        first = [copy(0, me, sibling, src=x_ref)]
        first += [copy(1 + j, me, (*chip, c), src=x_ref) for j, chip in enumerate(chips)]
        for cp in first:
            cp.start()
        passed = [copy(4 + j, (*chip, c), sibling) for j, chip in enumerate(chips)]
        for j, chip in enumerate(chips):
            copy(1 + j, (*chip, c), me).wait_recv()
            passed[j].start()
        copy(0, sibling, me).wait_recv()
        for j, chip in enumerate(chips):
            copy(4 + j, (*chip, 1 - c), me).wait_recv()
        for cp in first + passed:
            cp.wait_send()
        mine.wait()

    return pl.pallas_call(
        body, name=name, out_shape=jax.ShapeDtypeStruct((NDEV * m_per, n), v.dtype),
        in_specs=[pl.BlockSpec(memory_space=pltpu.VMEM)], out_specs=pl.BlockSpec(memory_space=pltpu.VMEM),
        scratch_shapes=[pltpu.SemaphoreType.DMA((7,)), pltpu.SemaphoreType.DMA((7,)), pltpu.SemaphoreType.DMA],
    )(v)


def _blk_cols(n):
    return lambda ref, b: ref.at[:, pl.ds(pl.multiple_of(b * n, 128), n)]


def _blk_rows(n):
    return lambda ref, b: ref.at[pl.ds(pl.multiple_of(b * n, 8), n), :]


def _blk_lead(ref, b):
    return ref.at[b]


def _blk_heads(ref, b):
    return ref.at[:, pl.ds(pl.multiple_of(b * (HB // NDEV), 8), HB // NDEV), :]


def _ag_phases(ins, outs, slicers, send_sems, recv_sems, local_sems):
    na = len(ins)
    x, y, c = _my_pos()
    me, sibling = (x, y, c), (x, y, 1 - c)
    chips = [(1 - x, y), (x, 1 - y), (1 - x, 1 - y)]

    def copy(a, k, block, to, from_shard=False):
        px, py, pc = block
        dst = slicers[a](outs[a], 4 * px + 2 * py + pc)
        return pltpu.make_async_remote_copy(
            src_ref=ins[a] if from_shard else dst, dst_ref=dst,
            send_sem=send_sems.at[a * 7 + k], recv_sem=recv_sems.at[a * 7 + k], device_id=to, device_id_type=MESH)

    def local(a):
        return pltpu.make_async_copy(ins[a], slicers[a](outs[a], 4 * x + 2 * y + c), local_sems.at[a])

    def firsts(a):
        return [copy(a, 0, me, sibling, True)] + [copy(a, 1 + j, me, (*chip, c), True) for j, chip in enumerate(chips)]

    def start():
        for a in range(na):
            local(a).start()
            for cp in firsts(a):
                cp.start()

    def forward():
        for a in range(na):
            for j, chip in enumerate(chips):
                copy(a, 1 + j, (*chip, c), me).wait_recv()
                copy(a, 4 + j, (*chip, c), sibling).start()

    def finish():
        for a in range(na):
            copy(a, 0, sibling, me).wait_recv()
            for j, chip in enumerate(chips):
                copy(a, 4 + j, (*chip, 1 - c), me).wait_recv()
        for a in range(na):
            for cp in firsts(a) + [copy(a, 4 + j, (*chip, c), sibling) for j, chip in enumerate(chips)]:
                cp.wait_send()
            local(a).wait()

    return start, forward, finish


def _ag_sems(na):
    return [pltpu.SemaphoreType.DMA((7 * na,)), pltpu.SemaphoreType.DMA((7 * na,)), pltpu.SemaphoreType.DMA((na,))]


def _all_gather_weights(shards, fulls, slicers):
    na = len(shards)

    def body(*refs):
        start, forward, finish = _ag_phases(refs[:na], refs[na:2 * na], slicers, *refs[2 * na:])
        start()
        forward()
        finish()

    return pl.pallas_call(
        body, name="gather_weights",
        out_shape=[jax.ShapeDtypeStruct(s, sh.dtype) for s, sh in zip(fulls, shards)],
        in_specs=[_ANY] * na, out_specs=[_ANY] * na, scratch_shapes=_ag_sems(na),
    )(*shards)


def _scatter_grads(grads, shard_shapes, slicers):
    na = len(grads)

    def body(*refs):
        ins, outs = refs[:na], refs[na:2 * na]
        send_sems, recv_sems, local_sems = refs[2 * na:]
        x, y, c = _my_pos()
        me = 4 * x + 2 * y + c
        mine, sent = [], []
        for a in range(na):
            cp = pltpu.make_async_copy(slicers[a](ins[a], me), outs[a].at[me], local_sems.at[a])
            cp.start()
            mine.append(cp)
        rel = [(k >> 2 & 1, k >> 1 & 1, k & 1) for k in range(1, NDEV)]
        for a in range(na):
            for k, (fx, fy, fc) in enumerate(rel):
                px, py, pc = x ^ fx, y ^ fy, c ^ fc
                cp = pltpu.make_async_remote_copy(
                    src_ref=slicers[a](ins[a], 4 * px + 2 * py + pc), dst_ref=outs[a].at[me],
                    send_sem=send_sems.at[a * 7 + k], recv_sem=recv_sems.at[a * 7 + k],
                    device_id=(px, py, pc), device_id_type=MESH)
                cp.start()
                sent.append(cp)
        for a in range(na):
            for k, (fx, fy, fc) in enumerate(rel):
                px, py, pc = x ^ fx, y ^ fy, c ^ fc
                src = 4 * px + 2 * py + pc
                pltpu.make_async_remote_copy(
                    src_ref=slicers[a](ins[a], me), dst_ref=outs[a].at[src],
                    send_sem=send_sems.at[a * 7 + k], recv_sem=recv_sems.at[a * 7 + k],
                    device_id=(px, py, pc), device_id_type=MESH).wait_recv()
        for cp in sent:
            cp.wait_send()
        for cp in mine:
            cp.wait()

    any_spec = pl.BlockSpec(memory_space=pl.ANY)
    return pl.pallas_call(
        body, name="scatter_grads",
        out_shape=[jax.ShapeDtypeStruct((NDEV,) + tuple(s), g.dtype) for s, g in zip(shard_shapes, grads)],
        in_specs=[any_spec] * na, out_specs=[any_spec] * na,
        scratch_shapes=[pltpu.SemaphoreType.DMA((7 * na,)), pltpu.SemaphoreType.DMA((7 * na,)),
                        pltpu.SemaphoreType.DMA((na,))],
    )(*grads)


def _local_step(x, target, mod, g_mix, g_ffn, g_fin, prm, w_in, shards):
    fulls = [(HEADS, HB, HB), (HEADS, HB, HB), (D, D), (NDEV, D, FB), (DFF, D)]
    slicers = [_blk_heads, _blk_heads, _blk_rows(D // NDEV), _blk_lead, _blk_rows(DFF // NDEV)]
    proj, h, (wa, wx, w_out, w_gu, w_down) = _in_proj(x, mod, g_mix, w_in, shards, fulls, slicers)
    w_gu = w_gu.reshape(2, 4, D, FB)
    merged, hl = _mixer_fwd(proj, prm, wa, wx)
    x1, h2 = _out_proj(merged, x, mod, g_ffn, w_out)
    gu, dx2, dx2b, loss, d_gfin = _ffn_fwd(h2, x1, target, mod, g_fin, w_gu, w_down)
    dgu, act, dx1, dx1b, dmg, sums2 = _ffn_bwd(dx2, gu, x1, mod, g_ffn, w_gu, w_down, w_out)
    dev_order = _xor_order(_my_index(), NDEV)
    chip_order = _xor_order(_my_index() >> 1, NDEV // 2)
    p_wgu = _gu_wgrad(h2, dgu, dev_order)
    p_wdown, d_gt2 = _scaled_wgrad("down_wgrad", act, dx2b, w_down, 5, mod, chip_order)
    p_wout, d_gt1 = _scaled_wgrad("out_wgrad", merged.reshape(1, *merged.shape), dx1b, w_out, 2, mod,
                                  jnp.zeros((1,), jnp.int32))
    dproj, msums, g_wa, g_wx = _mixer_bwd(proj, hl, dmg, prm, wa, wx)
    p_win, p_wa, p_wx = _in_wgrad(h, dproj, g_wa, g_wx, chip_order)
    grad_x, sums1 = _in_proj_bwd(dproj, w_in, x, dx1, mod, g_mix)
    return dict(loss=loss, grad_x=grad_x, d_gfin=d_gfin, sums1=sums1, sums2=sums2, msums=msums,
                d_gt1=d_gt1[0:1], d_gt2=d_gt2[0:1], p_win=p_win, p_wa=p_wa, p_wx=p_wx, p_wout=p_wout, p_wgu=p_wgu,
                p_wdown=p_wdown)


def kernel(x, c, w_ada, b_ada, g_norm_mix, w_in, conv_a_w, conv_b_w, conv_b_bias, w_rg_a, b_rg_a, w_rg_x, b_rg_x, lru_lambda, w_out, g_norm_ffn, w_gate_up, w_down, g_norm_final, loss_target, m_w_ada, m_b_ada, m_g_norm_mix, m_w_in, m_conv_a_w, m_conv_b_w, m_conv_b_bias, m_w_rg_a, m_b_rg_a, m_w_rg_x, m_b_rg_x, m_lru_lambda, m_w_out, m_g_norm_ffn, m_w_gate_up, m_w_down, m_g_norm_final, v_w_ada, v_b_ada, v_g_norm_mix, v_w_in, v_conv_a_w, v_conv_b_w, v_conv_b_bias, v_w_rg_a, v_b_rg_a, v_w_rg_x, v_b_rg_x, v_lru_lambda, v_w_out, v_g_norm_ffn, v_w_gate_up, v_w_down, v_g_norm_final):
    me = 4 * lax.axis_index("x") + 2 * lax.axis_index("y") + lax.axis_index("c")
    ncol = w_ada.shape[2]
    cw = conv_a_w.shape[2]

    pack0 = jnp.concatenate([c, conv_a_w.reshape(1, 3 * cw), conv_b_w.reshape(1, 4 * cw)], axis=1)
    got0 = _all_gather_small("gather_c", jnp.broadcast_to(pack0, (8, pack0.shape[1])))
    got0 = got0.reshape(NDEV, 8, -1)[:, 0, :]
    c_all = got0[:, :D]
    conv_a = got0[:, D:D + 3 * cw].reshape(NDEV, 3, cw).transpose(1, 0, 2).reshape(3, D)
    conv_b = got0[:, D + 3 * cw:].reshape(NDEV, 4, cw).transpose(1, 0, 2).reshape(4, D)

    b_cols = lax.dynamic_slice_in_dim(b_ada, me * ncol, ncol, axis=1)
    c16 = jnp.concatenate([c_all, jnp.zeros((8, D), F32)], axis=0)
    mod_cols = _ada_fwd(c16, w_ada[0], b_cols)[:NDEV]
    got1 = _all_gather_small("gather_mod", mod_cols).reshape(NDEV, NDEV, ncol)
    mod6 = lax.dynamic_index_in_dim(got1, me, axis=1, keepdims=False).reshape(6, D)
    mod = jnp.concatenate([mod6, jnp.zeros((2, D), F32)], axis=0)

    (w_in_f,) = _all_gather_weights([w_in[0].astype(BF16)], [(D, 7 * D)], [_blk_cols(7 * D // NDEV)])
    shards = [w_rg_a[0].astype(BF16), w_rg_x[0].astype(BF16), w_out[0].astype(BF16), w_gate_up[0].astype(BF16),
              w_down[0].astype(BF16)]

    prm = jnp.concatenate([conv_a, conv_b, conv_b_bias, b_rg_a, b_rg_x, lru_lambda, jnp.zeros((5, D), F32)], axis=0)
    r = _local_step(x[0], loss_target[0], mod, g_norm_mix, g_norm_ffn, g_norm_final.reshape(1, D), prm,
                    w_in_f, shards)

    parts = [r["p_win"], r["p_wa"], r["p_wx"], r["p_wout"], r["p_wgu"], r["p_wdown"]]
    big = {}
    for nm, p, w, m, v in (("w_in", parts[0], w_in, m_w_in, v_w_in), ("w_rg_a", parts[1], w_rg_a, m_w_rg_a, v_w_rg_a),
                           ("w_rg_x", parts[2], w_rg_x, m_w_rg_x, v_w_rg_x), ("w_out", parts[3], w_out, m_w_out, v_w_out),
                           ("w_gate_up", parts[4], w_gate_up, m_w_gate_up, v_w_gate_up),
                           ("w_down", parts[5], w_down, m_w_down, v_w_down)):
        two_d = (-1, w.shape[-1])
        outs = _adam("adam_" + nm, p.reshape((NDEV,) + w.reshape(two_d).shape), w.reshape(two_d), m.reshape(two_d),
                     v.reshape(two_d))
        big[nm] = [o.reshape(w.shape) for o in outs]

    small = jnp.concatenate([
        r["sums1"][S_SH:S_SH + 1], r["sums1"][S_SC:S_SC + 1], r["d_gt1"],
        r["sums2"][S_SH:S_SH + 1], r["sums2"][S_SC:S_SC + 1], r["d_gt2"],
        r["sums1"][S_G:S_G + 1],
        r["msums"][M_CBIAS:M_CBIAS + 1], r["msums"][M_BA:M_BA + 1], r["msums"][M_BX:M_BX + 1],
        r["msums"][M_LS:M_LS + 1],
        r["sums2"][S_G:S_G + 1], r["d_gfin"],
        r["msums"][M_WA:M_WA + 3], r["msums"][M_WB:M_WB + 4],
        jnp.zeros((4, D), F32)], axis=0)
    got2 = _all_gather_small("gather_small", small).reshape(NDEV, 24, D)

    rep_w = jnp.concatenate([b_ada.reshape(6, D), g_norm_mix, conv_b_bias, b_rg_a, b_rg_x, lru_lambda, g_norm_ffn,
                             g_norm_final.reshape(1, D), jnp.zeros((3, D), F32)], axis=0)
    rep_m = jnp.concatenate([m_b_ada.reshape(6, D), m_g_norm_mix, m_conv_b_bias, m_b_rg_a, m_b_rg_x, m_lru_lambda,
                             m_g_norm_ffn, m_g_norm_final.reshape(1, D), jnp.zeros((3, D), F32)], axis=0)
    rep_v = jnp.concatenate([v_b_ada.reshape(6, D), v_g_norm_mix, v_conv_b_bias, v_b_rg_a, v_b_rg_x, v_lru_lambda,
                             v_g_norm_ffn, v_g_norm_final.reshape(1, D), jnp.ones((3, D), F32)], axis=0)
    rep = _adam("adam_rep", got2[:, :16, :], rep_w, rep_m, rep_v)

    conv_parts = lax.dynamic_slice_in_dim(got2[:, 13:21, :], me * cw, cw, axis=2)
    cv_w = jnp.concatenate([conv_a_w[0], conv_b_w[0], jnp.zeros((1, cw), F32)], axis=0)
    cv_m = jnp.concatenate([m_conv_a_w[0], m_conv_b_w[0], jnp.zeros((1, cw), F32)], axis=0)
    cv_v = jnp.concatenate([v_conv_a_w[0], v_conv_b_w[0], jnp.ones((1, cw), F32)], axis=0)
    cvo = _adam("adam_conv", conv_parts, cv_w, cv_m, cv_v)

    dmod_cols = lax.dynamic_slice_in_dim(got2[:, :6, :].reshape(NDEV, 6 * D), me * ncol, ncol, axis=1)
    dmod16 = jnp.concatenate([dmod_cols, jnp.zeros((8, ncol), F32)], axis=0)
    ada = _ada_bwd(c16, dmod16, w_ada[0], m_w_ada[0], v_w_ada[0])

    loss = lax.psum(r["loss"][0, 0], AXES)

    def pick(q):
        one = lambda i: rep[q][i:i + 1]
        return [ada[q].reshape(w_ada.shape), rep[q][0:6].reshape(b_ada.shape), one(6), big["w_in"][q],
                cvo[q][0:3].reshape(conv_a_w.shape), cvo[q][3:7].reshape(conv_b_w.shape), one(7),
                big["w_rg_a"][q], one(8), big["w_rg_x"][q], one(9), one(10), big["w_out"][q], one(11),
                big["w_gate_up"][q], big["w_down"][q], rep[q][12]]

    return (loss, r["grad_x"].reshape(x.shape), *pick(0), *pick(1), *pick(2), *pick(3))
```

```python
import functools
import math

import jax
import jax.numpy as jnp
from jax import lax
from jax.experimental import pallas as pl
from jax.experimental.pallas import tpu as pltpu

F32 = jnp.float32
BF16 = jnp.bfloat16

D = 1024
DFF = 2816
NDEV = 8
HEADS = 4
HB = D // HEADS
FB = DFF // 4
EPS = 1e-6
LRU_C = 8.0
ADAM_LR, ADAM_B1, ADAM_B2, ADAM_EPS, ADAM_WD, ADAM_STEP = 0.001, 0.9, 0.999, 1e-08, 0.01, 10

VMEM_LIMIT = 56 * 1024 * 1024
TM = 512
TMI = 1024
TK = 2048
TKI = 1024
SUB = 256
UNROLL = 4
TT = 256
CG = 256
MESH = pl.DeviceIdType.MESH
AXES = ("x", "y", "c")


def _cp(*sem):
    return pltpu.CompilerParams(dimension_semantics=sem, vmem_limit_bytes=VMEM_LIMIT)


def _sig(x):
    return 1.0 / (1.0 + jnp.exp(-x))


def _log_sigmoid(x):
    z = jnp.exp(-jnp.abs(x))
    u = 1.0 + z
    d = u - 1.0
    l1p = jnp.where(d == 0.0, z, jnp.log(u) * (z / jnp.where(d == 0.0, 1.0, d)))
    return -(jnp.maximum(-x, 0.0) + l1p)


def _neg_expm1(x):
    p = x * (1.0 + x * 0.5 * (1.0 + x * (1.0 / 3.0) * (1.0 + x * 0.25 * (1.0 + x * 0.2 * (1.0 + x * (1.0 / 6.0))))))
    return jnp.where(x > -0.25, -p, 1.0 - jnp.exp(x))


_GC = math.sqrt(2.0 / math.pi)


def _gelu(x):
    t = jnp.tanh(_GC * (x + 0.044715 * x * x * x))
    return 0.5 * x * (1.0 + t), t


def _dot(a, b):
    return jnp.dot(a, b, preferred_element_type=F32)


def _dot_nt(a, b):
    return lax.dot_general(a, b, (((1,), (1,)), ((), ())), preferred_element_type=F32)


def _dot_tn(a, b):
    return lax.dot_general(a, b, (((0,), (0,)), ((), ())), preferred_element_type=F32)


def _sub_blocks(n_rows):
    step = min(SUB, n_rows)
    return [slice(r, r + step) for r in range(0, n_rows, step)]


def _fold8(v):
    return v[0:8] + v[8:16]


def _in_proj(x, mod, g_mix, w_in, shards, fulls, slicers):
    t_len = x.shape[0]
    tm = min(TMI, t_len)
    ni = t_len // tm
    na = len(shards)
    rc = 32

    def body(x_ref, mod_ref, g_ref, w_ref, *rest):
        ins, (proj_ref, h_ref), outs = rest[:na], rest[na:na + 2], rest[na + 2:2 * na + 2]
        h_scr = rest[2 * na + 2]
        start, forward, finish = _ag_phases(ins, outs, slicers, *rest[2 * na + 3:])
        i, s = pl.program_id(0), pl.program_id(1)

        @pl.when((i == 0) & (s == 0))
        def _():
            start()

        @pl.when((i == ni // 2) & (s == 0))
        def _():
            forward()

        @pl.when(s == 0)
        def _():
            gs = g_ref[...] * (1.0 + mod_ref[1:2, :])
            sh = mod_ref[0:1, :]

            def chunk(i, carry):
                rows = pl.ds(pl.multiple_of(i * rc, rc), rc)
                xv = x_ref[rows, :]
                r = lax.rsqrt(jnp.mean(xv * xv, axis=-1, keepdims=True) + EPS)
                h = (xv * r * gs + sh).astype(BF16)
                h_scr[rows, :] = h
                h_ref[rows, :] = h
                return carry

            lax.fori_loop(0, tm // rc, chunk, 0, unroll=UNROLL)

        proj_ref[0] = _dot(h_scr[...], w_ref[...]).astype(BF16)

        @pl.when((i == ni - 1) & (s == 6))
        def _():
            finish()

    res = pl.pallas_call(
        body, name="in_proj", grid=(ni, 7),
        in_specs=[pl.BlockSpec((tm, D), lambda i, s: (i, 0)),
                  pl.BlockSpec((8, D), lambda i, s: (0, 0)),
                  pl.BlockSpec((1, D), lambda i, s: (0, 0)),
                  pl.BlockSpec((D, D), lambda i, s: (0, s))] + [_ANY] * na,
        out_specs=[pl.BlockSpec((1, tm, D), lambda i, s: (s, i, 0)),
                   pl.BlockSpec((tm, D), lambda i, s: (i, 0))] + [_ANY] * na,
        out_shape=[jax.ShapeDtypeStruct((7, t_len, D), BF16), jax.ShapeDtypeStruct((t_len, D), BF16)]
        + [jax.ShapeDtypeStruct(f, sh.dtype) for f, sh in zip(fulls, shards)],
        scratch_shapes=[pltpu.VMEM((tm, D), BF16)] + _ag_sems(na),
        compiler_params=_cp("arbitrary", "arbitrary"),
    )(x, mod, g_mix, w_in, *shards)
    return res[0], res[1], res[2:]


P_WA, P_WB, P_CBIAS, P_BA, P_BX, P_LAM = 0, 3, 7, 8, 9, 10


def _lru_gates(rp, ip, ls, first_row):
    r = _sig(rp)
    ig = _sig(ip)
    la = LRU_C * r * ls
    a = jnp.exp(la)
    m2 = _neg_expm1(2.0 * la)
    mult = jnp.where(first_row, 1.0, jnp.sqrt(jnp.maximum(m2, 0.0)))
    return r, ig, la, a, m2, mult


def _mixer_fwd(proj, prm, wa, wx):
    t_len = proj.shape[1]
    tt = min(TT, t_len)

    def body(proj_ref, prm_ref, wa_ref, wx_ref, mg_ref, hl_ref, xe, ve, hc, u_s, ya_s, rp_s, ip_s):
        t = pl.program_id(0)

        @pl.when(t == 0)
        def _():
            xe[0:8, :] = jnp.zeros((8, D), F32)
            ve[0:8, :] = jnp.zeros((8, D), F32)
            hc[...] = jnp.zeros((8, D), F32)

        xe[8:8 + tt, :] = proj_ref[3].astype(F32)
        ve[8:8 + tt, :] = proj_ref[1].astype(F32) * proj_ref[2].astype(F32)
        u = prm_ref[P_CBIAS:P_CBIAS + 1, :] + prm_ref[P_WB:P_WB + 1, :] * xe[5:5 + tt, :]
        for k in range(1, 4):
            u = u + prm_ref[P_WB + k:P_WB + k + 1, :] * xe[5 + k:5 + k + tt, :]
        u_s[...] = u
        ya = prm_ref[P_WA:P_WA + 1, :] * ve[6:6 + tt, :]
        for k in range(1, 3):
            ya = ya + prm_ref[P_WA + k:P_WA + k + 1, :] * ve[6 + k:6 + k + tt, :]
        ya_s[...] = ya
        xe[0:8, :] = xe[tt:tt + 8, :]
        ve[0:8, :] = ve[tt:tt + 8, :]

        ub = u.astype(BF16)
        for h in range(HEADS):
            cs = slice(h * HB, (h + 1) * HB)
            rp_s[:, cs] = _dot(ub[:, cs], wa_ref[h]) + prm_ref[P_BA:P_BA + 1, cs]
            ip_s[:, cs] = _dot(ub[:, cs], wx_ref[h]) + prm_ref[P_BX:P_BX + 1, cs]

        ls_all = _log_sigmoid(prm_ref[P_LAM:P_LAM + 1, :])
        row = lax.broadcasted_iota(jnp.int32, (8, CG), 0)

        def blk(i, carry):
            r0 = pl.multiple_of(i * 16, 16)
            for g in range(D // CG):
                cs = slice(g * CG, (g + 1) * CG)
                ls = ls_all[:, cs]
                hprev = hc[:, cs]
                hs = []
                for sb in range(2):
                    rr = r0 + 8 * sb
                    first = (row + (t * tt + rr)) == 0
                    _, ig, _, a, _, mult = _lru_gates(rp_s[pl.ds(rr, 8), cs], ip_s[pl.ds(rr, 8), cs], ls, first)
                    b = mult * (ig * u_s[pl.ds(rr, 8), cs])
                    for s in (1, 2, 4):
                        a_sh = jnp.where(row >= s, pltpu.roll(a, s, 0), 1.0)
                        b_sh = jnp.where(row >= s, pltpu.roll(b, s, 0), 0.0)
                        b = a * b_sh + b
                        a = a * a_sh
                    hv = a * hprev + b
                    hprev = jnp.broadcast_to(hv[7:8, :], hv.shape)
                    hs.append(hv)
                hc[:, cs] = hprev
                h16 = jnp.concatenate(hs, axis=0)
                rows = pl.ds(r0, 16)
                gl, _ = _gelu(proj_ref[4, rows, cs].astype(F32))
                y_b = h16 * gl
                y_a = proj_ref[0, rows, cs].astype(F32) * ya_s[rows, cs]
                mg = _sig(proj_ref[5, rows, cs].astype(F32)) * y_a + _sig(proj_ref[6, rows, cs].astype(F32)) * y_b
                mg_ref[rows, cs] = mg.astype(BF16)
                hl_ref[rows, cs] = h16.astype(BF16)
            return carry

        lax.fori_loop(0, tt // 16, blk, 0)

    return pl.pallas_call(
        body, name="mixer_fwd", grid=(t_len // tt,),
        in_specs=[pl.BlockSpec((7, tt, D), lambda t: (0, t, 0)),
                  pl.BlockSpec((16, D), lambda t: (0, 0)),
                  pl.BlockSpec((HEADS, HB, HB), lambda t: (0, 0, 0)),
                  pl.BlockSpec((HEADS, HB, HB), lambda t: (0, 0, 0))],
        out_specs=[pl.BlockSpec((tt, D), lambda t: (t, 0)), pl.BlockSpec((tt, D), lambda t: (t, 0))],
        out_shape=[jax.ShapeDtypeStruct((t_len, D), BF16), jax.ShapeDtypeStruct((t_len, D), BF16)],
        scratch_shapes=[pltpu.VMEM((tt + 8, D), F32), pltpu.VMEM((tt + 8, D), F32), pltpu.VMEM((8, D), F32),
                        pltpu.VMEM((tt, D), F32), pltpu.VMEM((tt, D), F32), pltpu.VMEM((tt, D), F32),
                        pltpu.VMEM((tt, D), F32)],
        compiler_params=_cp("arbitrary"),
    )(proj, prm, wa, wx)


def _out_proj(merged, x, mod, g_ffn, w_out):
    t_len = x.shape[0]
    tm = min(TM, t_len)

    def body(mg_ref, x_ref, mod_ref, g_ref, w_ref, x1_ref, h2_ref):
        gt1 = mod_ref[2:3, :]
        for rows in _sub_blocks(tm):
            x1_ref[rows, :] = x_ref[rows, :] + gt1 * _dot(mg_ref[rows, :], w_ref[...])
        gs = g_ref[...] * (1.0 + mod_ref[4:5, :])
        sh = mod_ref[3:4, :]

        def chunk(c, carry):
            rows = pl.ds(pl.multiple_of(c * 16, 16), 16)
            x1 = x1_ref[rows, :]
            r = lax.rsqrt(jnp.mean(x1 * x1, axis=-1, keepdims=True) + EPS)
            h2_ref[rows, :] = (x1 * r * gs + sh).astype(BF16)
            return carry

        lax.fori_loop(0, tm // 16, chunk, 0, unroll=UNROLL)

    return pl.pallas_call(
        body, name="out_proj", grid=(t_len // tm,),
        in_specs=[pl.BlockSpec((tm, D), lambda i: (i, 0)), pl.BlockSpec((tm, D), lambda i: (i, 0)),
                  pl.BlockSpec((8, D), lambda i: (0, 0)), pl.BlockSpec((1, D), lambda i: (0, 0)),
                  pl.BlockSpec((D, D), lambda i: (0, 0))],
        out_specs=[pl.BlockSpec((tm, D), lambda i: (i, 0)), pl.BlockSpec((tm, D), lambda i: (i, 0))],
        out_shape=[jax.ShapeDtypeStruct((t_len, D), F32), jax.ShapeDtypeStruct((t_len, D), BF16)],
        compiler_params=_cp("parallel"),
    )(merged, x, mod, g_ffn, w_out)


def _ffn_fwd(h2, x1, target, mod, g_fin, w_gu, w_down):
    t_len = x1.shape[0]
    tm = min(TM, t_len)
    assert tm % (16 * UNROLL) == 0

    def body(h2_ref, x1_ref, tg_ref, mod_ref, g_ref, wgu_ref, wd_ref, gu_ref, dx2_ref, dx2b_ref, loss_ref, dg_ref, acc):
        i, j = pl.program_id(0), pl.program_id(1)

        @pl.when((i == 0) & (j == 0))
        def _():
            loss_ref[...] = jnp.zeros_like(loss_ref)
            dg_ref[...] = jnp.zeros_like(dg_ref)

        @pl.when(j == 0)
        def _():
            acc[...] = jnp.zeros_like(acc)

        for rows in _sub_blocks(tm):
            hb = h2_ref[rows, :]
            gate = _dot(hb, wgu_ref[0, 0])
            up = _dot(hb, wgu_ref[1, 0])
            gu_ref[0, 0, rows, :] = gate.astype(BF16)
            gu_ref[1, 0, rows, :] = up.astype(BF16)
            act = (gate * _sig(gate) * up).astype(BF16)
            acc[rows, :] += _dot(act, wd_ref[...])

        @pl.when(j == 3)
        def _():
            gt2 = mod_ref[5:6, :]
            gf = g_ref[...]

            def chunk(c, carry):
                s_loss, s_dg = carry
                for u in range(UNROLL):
                    rows = pl.ds(pl.multiple_of(c * (16 * UNROLL), 16) + 16 * u, 16)
                    x2 = x1_ref[rows, :] + gt2 * acc[rows, :]
                    r = lax.rsqrt(jnp.mean(x2 * x2, axis=-1, keepdims=True) + EPS)
                    xn = x2 * r
                    diff = xn * gf - tg_ref[rows, :]
                    dy = diff * (1.0 / D)
                    dxn = dy * gf
                    dx2 = r * (dxn - xn * jnp.mean(dxn * xn, axis=-1, keepdims=True))
                    dx2_ref[rows, :] = dx2
                    dx2b_ref[rows, :] = dx2.astype(BF16)
                    s_loss, s_dg = s_loss + _fold8(diff * diff), s_dg + _fold8(dy * xn)
                return s_loss, s_dg

            zero = jnp.zeros((8, D), F32)
            s_loss, s_dg = lax.fori_loop(0, tm // (16 * UNROLL), chunk, (zero, zero))
            loss_ref[...] += jnp.sum(s_loss) * (0.5 / D)
            dg_ref[...] += jnp.sum(s_dg, axis=0, keepdims=True)

    return pl.pallas_call(
        body, name="ffn_fwd", grid=(t_len // tm, 4),
        in_specs=[pl.BlockSpec((tm, D), lambda i, j: (i, 0)), pl.BlockSpec((tm, D), lambda i, j: (i, 0)),
                  pl.BlockSpec((tm, D), lambda i, j: (i, 0)), pl.BlockSpec((8, D), lambda i, j: (0, 0)),
                  pl.BlockSpec((1, D), lambda i, j: (0, 0)),
                  pl.BlockSpec((2, 1, D, FB), lambda i, j: (0, j, 0, 0)),
                  pl.BlockSpec((FB, D), lambda i, j: (j, 0))],
        out_specs=[pl.BlockSpec((2, 1, tm, FB), lambda i, j: (0, j, i, 0)),
                   pl.BlockSpec((tm, D), lambda i, j: (i, 0)),
                   pl.BlockSpec((tm, D), lambda i, j: (i, 0)),
                   pl.BlockSpec((1, 128), lambda i, j: (0, 0)),
                   pl.BlockSpec((1, D), lambda i, j: (0, 0))],
        out_shape=[jax.ShapeDtypeStruct((2, 4, t_len, FB), BF16), jax.ShapeDtypeStruct((t_len, D), F32),
                   jax.ShapeDtypeStruct((t_len, D), BF16),
                   jax.ShapeDtypeStruct((1, 128), F32), jax.ShapeDtypeStruct((1, D), F32)],
        scratch_shapes=[pltpu.VMEM((tm, D), F32)],
        compiler_params=_cp("arbitrary", "arbitrary"),
    )(h2, x1, target, mod, g_fin, w_gu, w_down)


S_SH, S_SC, S_G = 0, 1, 2


def _norm_bwd_rows(n_rows, rc, dh_ref, x_ref, dres_ref, scale, gain, sums_ref, write):
    assert n_rows % (rc * UNROLL) == 0
    gs = 1.0 + scale
    fold = _fold8 if rc == 16 else (lambda v: v)

    def chunk(c, carry):
        s_sh, s_sc, s_g = carry
        for u in range(UNROLL):
            rows = pl.ds(pl.multiple_of(c * (rc * UNROLL), rc) + rc * u, rc)
            dh = dh_ref[rows, :]
            xv = x_ref[rows, :]
            r = lax.rsqrt(jnp.mean(xv * xv, axis=-1, keepdims=True) + EPS)
            xn = xv * r
            dhn = dh * gs
            dxn = dhn * gain
            write(rows, dres_ref[rows, :] + r * (dxn - xn * jnp.mean(dxn * xn, axis=-1, keepdims=True)))
            s_sh, s_sc, s_g = s_sh + fold(dh), s_sc + fold(dh * (xn * gain)), s_g + fold(dhn * xn)
        return s_sh, s_sc, s_g

    zero = jnp.zeros((8, D), F32)
    s_sh, s_sc, s_g = lax.fori_loop(0, n_rows // (rc * UNROLL), chunk, (zero, zero, zero))
    sums_ref[S_SH:S_SH + 1, :] += jnp.sum(s_sh, axis=0, keepdims=True)
    sums_ref[S_SC:S_SC + 1, :] += jnp.sum(s_sc, axis=0, keepdims=True)
    sums_ref[S_G:S_G + 1, :] += jnp.sum(s_g, axis=0, keepdims=True)


def _ffn_bwd(dx2, gu, x1, mod, g_ffn, w_gu, w_down, w_out):
    t_len = x1.shape[0]
    tm = min(TM, t_len)

    def body(dx2_ref, gu_ref, x1_ref, mod_ref, g_ref, wgu_ref, wd_ref, wo_ref,
             dgu_ref, act_ref, dx1_ref, dx1b_ref, dmg_ref, sums_ref, acc, dffn):
        i, j = pl.program_id(0), pl.program_id(1)

        @pl.when((i == 0) & (j == 0))
        def _():
            sums_ref[...] = jnp.zeros_like(sums_ref)

        @pl.when(j == 0)
        def _():
            acc[...] = jnp.zeros_like(acc)
            dffn[...] = (dx2_ref[...] * mod_ref[5:6, :]).astype(BF16)

        for rows in _sub_blocks(tm):
            dact = _dot_nt(dffn[rows, :], wd_ref[...])
            gate = gu_ref[0, 0, rows, :].astype(F32)
            up = gu_ref[1, 0, rows, :].astype(F32)
            sg = _sig(gate)
            silu = gate * sg
            act_ref[0, rows, :] = (silu * up).astype(BF16)
            dgate = (dact * up * (sg * (1.0 + gate * (1.0 - sg)))).astype(BF16)
            dup = (dact * silu).astype(BF16)
            dgu_ref[0, 0, rows, :] = dgate
            dgu_ref[1, 0, rows, :] = dup
            acc[rows, :] += _dot_nt(dgate, wgu_ref[0, 0]) + _dot_nt(dup, wgu_ref[1, 0])

        @pl.when(j == 3)
        def _():
            gt1 = mod_ref[2:3, :]

            def write(rows, dx1):
                dx1_ref[rows, :] = dx1
                dx1b_ref[rows, :] = dx1.astype(BF16)
                dffn[rows, :] = (dx1 * gt1).astype(BF16)

            _norm_bwd_rows(tm, 16, acc, x1_ref, dx2_ref, mod_ref[4:5, :], g_ref[...], sums_ref, write)
            dmg_ref[...] = _dot_nt(dffn[...], wo_ref[...]).astype(BF16)

    return pl.pallas_call(
        body, name="ffn_bwd", grid=(t_len // tm, 4),
        in_specs=[pl.BlockSpec((tm, D), lambda i, j: (i, 0)),
                  pl.BlockSpec((2, 1, tm, FB), lambda i, j: (0, j, i, 0)),
                  pl.BlockSpec((tm, D), lambda i, j: (i, 0)),
                  pl.BlockSpec((8, D), lambda i, j: (0, 0)), pl.BlockSpec((1, D), lambda i, j: (0, 0)),
                  pl.BlockSpec((2, 1, D, FB), lambda i, j: (0, j, 0, 0)),
                  pl.BlockSpec((FB, D), lambda i, j: (j, 0)),
                  pl.BlockSpec((D, D), lambda i, j: (0, 0))],
        out_specs=[pl.BlockSpec((2, 1, tm, FB), lambda i, j: (0, j, i, 0)),
                   pl.BlockSpec((1, tm, FB), lambda i, j: (j, i, 0)),
                   pl.BlockSpec((tm, D), lambda i, j: (i, 0)),
                   pl.BlockSpec((tm, D), lambda i, j: (i, 0)),
                   pl.BlockSpec((tm, D), lambda i, j: (i, 0)),
                   pl.BlockSpec((8, D), lambda i, j: (0, 0))],
        out_shape=[jax.ShapeDtypeStruct((2, 4, t_len, FB), BF16), jax.ShapeDtypeStruct((4, t_len, FB), BF16),
                   jax.ShapeDtypeStruct((t_len, D), F32), jax.ShapeDtypeStruct((t_len, D), BF16),
                   jax.ShapeDtypeStruct((t_len, D), BF16), jax.ShapeDtypeStruct((8, D), F32)],
        scratch_shapes=[pltpu.VMEM((tm, D), F32), pltpu.VMEM((tm, D), BF16)],
        compiler_params=_cp("arbitrary", "arbitrary"),
    )(dx2, gu, x1, mod, g_ffn, w_gu, w_down, w_out)


def _my_pos():
    return lax.axis_index("x"), lax.axis_index("y"), lax.axis_index("c")


def _my_index():
    x, y, c = _my_pos()
    return 4 * x + 2 * y + c


def _device_of(b):
    return (b >> 2) & 1, (b >> 1) & 1, b & 1


def _rs_send(src, parts_ref, b, send_sems, recv_sems, local_sem):
    me = _my_index()
    dst = parts_ref.at[me]

    @pl.when(b == me)
    def _():
        pltpu.make_async_copy(src, dst, local_sem).start()

    @pl.when(b != me)
    def _():
        pltpu.make_async_remote_copy(src_ref=src, dst_ref=dst, send_sem=send_sems.at[b], recv_sem=recv_sems.at[me],
                                     device_id=_device_of(b), device_id_type=MESH).start()


def _rs_finish(src_of, parts_ref, send_sems, recv_sems, local_sem):
    me = _my_index()
    for s in range(NDEV):
        @pl.when(s != me)
        def _():
            cp = pltpu.make_async_remote_copy(src_ref=src_of(s), dst_ref=parts_ref.at[s], send_sem=send_sems.at[s],
                                              recv_sem=recv_sems.at[s], device_id=_device_of(s), device_id_type=MESH)
            cp.wait_send()
            cp.wait_recv()

        @pl.when(s == me)
        def _():
            pltpu.make_async_copy(src_of(s), parts_ref.at[s], local_sem).wait()


_RS_SEMS = [pltpu.SemaphoreType.DMA((NDEV,)), pltpu.SemaphoreType.DMA((NDEV,)), pltpu.SemaphoreType.DMA]
_ANY = pl.BlockSpec(memory_space=pl.ANY)


def _xor_order(me, n):
    return (me ^ (n - 1 - jnp.arange(n, dtype=jnp.int32))).astype(jnp.int32)


def _gu_wgrad(h2, dgu, order):
    t_len = h2.shape[0]
    tk = min(TK, t_len)
    nk = t_len // tk

    def body(ord_ref, h_ref, d_ref, parts_ref, acc, stage, send_sems, recv_sems, local_sem):
        p, k = pl.program_id(0), pl.program_id(1)
        part = _dot_tn(h_ref[...], d_ref[0])

        @pl.when(k == 0)
        def _():
            acc[...] = part

        @pl.when(k > 0)
        def _():
            acc[...] += part

        @pl.when(k == nk - 1)
        def _():
            b = ord_ref[p]
            stage[b] = acc[...].astype(BF16)
            _rs_send(stage.at[b], parts_ref, b, send_sems, recv_sems, local_sem)

        @pl.when((p == NDEV - 1) & (k == nk - 1))
        def _():
            _rs_finish(lambda s: stage.at[s], parts_ref, send_sems, recv_sems, local_sem)

    return pl.pallas_call(
        body, name="gu_wgrad",
        grid_spec=pltpu.PrefetchScalarGridSpec(
            num_scalar_prefetch=1, grid=(NDEV, nk),
            in_specs=[pl.BlockSpec((tk, D), lambda p, k, o: (k, 0)),
                      pl.BlockSpec((1, tk, FB), lambda p, k, o: (o[p], k, 0))],
            out_specs=_ANY,
            scratch_shapes=[pltpu.VMEM((D, FB), F32), pltpu.VMEM((NDEV, D, FB), BF16)] + _RS_SEMS),
        out_shape=jax.ShapeDtypeStruct((NDEV, D, FB), BF16),
        compiler_params=_cp("arbitrary", "arbitrary"),
    )(order, h2, dgu.reshape(NDEV, t_len, FB))


def _scaled_wgrad(name, a, dx, w, gate_row, mod, order):
    nb, t_len, kb = a.shape
    tk = min(TK, t_len)
    nk = t_len // tk
    per = NDEV // nb
    rows = kb // per

    def body(ord_ref, a_ref, dx_ref, w_ref, mod_ref, parts_ref, dg_ref, acc, stage, send_sems, recv_sems, local_sem):
        p, k = pl.program_id(0), pl.program_id(1)
        j = ord_ref[p]

        @pl.when((p == 0) & (k == 0))
        def _():
            dg_ref[...] = jnp.zeros_like(dg_ref)

        part = _dot_tn(a_ref[0], dx_ref[...])

        @pl.when(k == 0)
        def _():
            acc[...] = part

        @pl.when(k > 0)
        def _():
            acc[...] += part

        def src_of(blk, q):
            return stage.at[blk, pl.ds(q * rows, rows), :]

        @pl.when(k == nk - 1)
        def _():
            z = acc[...]
            stage[j] = (z * mod_ref[gate_row:gate_row + 1, :]).astype(BF16)
            dg_ref[0:1, :] += jnp.sum(z * w_ref[...].astype(F32), axis=0, keepdims=True)
            for q in range(per):
                _rs_send(src_of(j, q), parts_ref, j * per + q, send_sems, recv_sems, local_sem)

        @pl.when((p == nb - 1) & (k == nk - 1))
        def _():
            _rs_finish(lambda s: src_of(s // per, s % per), parts_ref, send_sems, recv_sems, local_sem)

    return pl.pallas_call(
        body, name=name,
        grid_spec=pltpu.PrefetchScalarGridSpec(
            num_scalar_prefetch=1, grid=(nb, nk),
            in_specs=[pl.BlockSpec((1, tk, kb), lambda p, k, o: (o[p], k, 0)),
                      pl.BlockSpec((tk, D), lambda p, k, o: (k, 0)),
                      pl.BlockSpec((kb, D), lambda p, k, o: (o[p], 0)),
                      pl.BlockSpec((8, D), lambda p, k, o: (0, 0))],
            out_specs=[_ANY, pl.BlockSpec((8, D), lambda p, k, o: (0, 0))],
            scratch_shapes=[pltpu.VMEM((kb, D), F32), pltpu.VMEM((nb, kb, D), BF16)] + _RS_SEMS),
        out_shape=[jax.ShapeDtypeStruct((NDEV, rows, D), BF16), jax.ShapeDtypeStruct((8, D), F32)],
        compiler_params=_cp("arbitrary", "arbitrary"),
    )(order, a, dx, w, mod)


M_WA, M_WB, M_CBIAS, M_BA, M_BX, M_LS = 0, 3, 7, 8, 9, 10


def _mixer_bwd(proj, hl, dmg, prm, wa, wx):
    t_len = proj.shape[1]
    tt = min(TT, t_len)
    nt = t_len // tt
    hb8 = tt // 8

    def rev(i):
        return nt - 1 - i

    def halo(i):
        return jnp.maximum(rev(i) * hb8 - 1, 0)

    def body(proj_ref, ph_ref, hl_ref, hh_ref, dmg_ref, prm_ref, wa_ref, wx_ref,
             dp_ref, sums_ref, gwa_ref, gwx_ref,
             xe, ve, he, u_s, ya_s, rp_s, ip_s, due, dye, drp_s, dip_s, an, gn):
        i = pl.program_id(0)
        t = rev(i)

        @pl.when(i == 0)
        def _():
            sums_ref[...] = jnp.zeros_like(sums_ref)
            gwa_ref[...] = jnp.zeros_like(gwa_ref)
            gwx_ref[...] = jnp.zeros_like(gwx_ref)
            due[tt:tt + 8, :] = jnp.zeros((8, D), F32)
            dye[tt:tt + 8, :] = jnp.zeros((8, D), F32)
            an[...] = jnp.zeros((8, D), F32)
            gn[...] = jnp.zeros((8, D), F32)

        live = (t > 0).astype(F32)
        xe[0:8, :] = ph_ref[3].astype(F32) * live
        ve[0:8, :] = ph_ref[1].astype(F32) * ph_ref[2].astype(F32) * live
        he[0:8, :] = hh_ref[...].astype(F32) * live
        xe[8:8 + tt, :] = proj_ref[3].astype(F32)
        ve[8:8 + tt, :] = proj_ref[1].astype(F32) * proj_ref[2].astype(F32)
        he[8:8 + tt, :] = hl_ref[...].astype(F32)
        u = prm_ref[P_CBIAS:P_CBIAS + 1, :] + prm_ref[P_WB:P_WB + 1, :] * xe[5:5 + tt, :]
        for k in range(1, 4):
            u = u + prm_ref[P_WB + k:P_WB + k + 1, :] * xe[5 + k:5 + k + tt, :]
        u_s[...] = u
        ya = prm_ref[P_WA:P_WA + 1, :] * ve[6:6 + tt, :]
        for k in range(1, 3):
            ya = ya + prm_ref[P_WA + k:P_WA + k + 1, :] * ve[6 + k:6 + k + tt, :]
        ya_s[...] = ya
        ub = u.astype(BF16)
        for h in range(HEADS):
            cs = slice(h * HB, (h + 1) * HB)
            rp_s[:, cs] = _dot(ub[:, cs], wa_ref[h]) + prm_ref[P_BA:P_BA + 1, cs]
            ip_s[:, cs] = _dot(ub[:, cs], wx_ref[h]) + prm_ref[P_BX:P_BX + 1, cs]

        ls_all = _log_sigmoid(prm_ref[P_LAM:P_LAM + 1, :])
        row = lax.broadcasted_iota(jnp.int32, (8, CG), 0)
        nblk = tt // 16

        def blk(ib, carry):
            r0 = pl.multiple_of((nblk - 1 - ib) * 16, 16)
            rows = pl.ds(r0, 16)
            for g in range(D // CG):
                cs = slice(g * CG, (g + 1) * CG)
                ls = ls_all[:, cs]
                dm = dmg_ref[rows, cs].astype(F32)
                cb = proj_ref[0, rows, cs].astype(F32)
                rg = proj_ref[4, rows, cs].astype(F32)
                sga = _sig(proj_ref[5, rows, cs].astype(F32))
                sgb = _sig(proj_ref[6, rows, cs].astype(F32))
                ya0 = ya_s[rows, cs]
                h16 = he[pl.ds(r0 + 8, 16), cs]
                gl, th = _gelu(rg)
                dgl = 0.5 * (1.0 + th) + 0.5 * rg * (1.0 - th * th) * (_GC * (1.0 + 3.0 * 0.044715 * rg * rg))
                y_a = cb * ya0
                y_b = h16 * gl
                dy_a = dm * sga
                dy_b = dm * sgb
                col = lambda s: slice(s * D + g * CG, s * D + (g + 1) * CG)
                dp_ref[rows, col(5)] = (dm * y_a * sga * (1.0 - sga)).astype(BF16)
                dp_ref[rows, col(6)] = (dm * y_b * sgb * (1.0 - sgb)).astype(BF16)
                dp_ref[rows, col(4)] = (dy_b * h16 * dgl).astype(BF16)
                dp_ref[rows, col(0)] = (dy_a * ya0).astype(BF16)
                dye[rows, cs] = dy_a * cb
                dh16 = dy_b * gl

                a_next = an[:, cs]
                g_next = gn[:, cs]
                s_ba = jnp.zeros((8, CG), F32)
                s_bx = jnp.zeros((8, CG), F32)
                s_ls = jnp.zeros((8, CG), F32)
                for sb in (1, 0):
                    rr = r0 + 8 * sb
                    first = (row + (t * tt + rr)) == 0
                    uu = u_s[pl.ds(rr, 8), cs]
                    r, ig, la, a, m2, mult = _lru_gates(rp_s[pl.ds(rr, 8), cs], ip_s[pl.ds(rr, 8), cs], ls, first)
                    ca = jnp.where(row < 7, pltpu.roll(a, 7, 0), a_next)
                    cb_ = dh16[8 * sb:8 * sb + 8, :]
                    for s in (1, 2, 4):
                        a_sh = jnp.where(row < 8 - s, pltpu.roll(ca, 8 - s, 0), 1.0)
                        b_sh = jnp.where(row < 8 - s, pltpu.roll(cb_, 8 - s, 0), 0.0)
                        cb_ = ca * b_sh + cb_
                        ca = ca * a_sh
                    gv = ca * g_next + cb_
                    g_next = jnp.broadcast_to(gv[0:1, :], gv.shape)
                    a_next = jnp.broadcast_to(a[0:1, :], a.shape)
                    hprev = jnp.where(row >= 1, pltpu.roll(he[pl.ds(rr + 8, 8), cs], 1, 0),
                                      pltpu.roll(he[pl.ds(rr, 8), cs], 1, 0))
                    da = gv * hprev
                    dmult = jnp.where(first, 0.0, gv * ig * uu)
                    dla = da * a + jnp.where(m2 > 0.0, dmult * (-(a * a) / mult), 0.0)
                    drp = dla * (LRU_C * ls) * r * (1.0 - r)
                    dip = gv * mult * uu * ig * (1.0 - ig)
                    s_ls = s_ls + dla * (LRU_C * r)
                    s_ba = s_ba + drp
                    s_bx = s_bx + dip
                    drp_s[pl.ds(rr, 8), cs] = drp
                    dip_s[pl.ds(rr, 8), cs] = dip
                    due[pl.ds(rr, 8), cs] = gv * mult * ig
                an[:, cs] = a_next
                gn[:, cs] = g_next
                sums_ref[M_BA:M_BA + 1, cs] += jnp.sum(s_ba, axis=0, keepdims=True)
                sums_ref[M_BX:M_BX + 1, cs] += jnp.sum(s_bx, axis=0, keepdims=True)
                sums_ref[M_LS:M_LS + 1, cs] += jnp.sum(s_ls, axis=0, keepdims=True)
            return carry

        lax.fori_loop(0, nblk, blk, 0)

        drp_b = drp_s[...].astype(BF16)
        dip_b = dip_s[...].astype(BF16)
        for h in range(HEADS):
            cs = slice(h * HB, (h + 1) * HB)
            due[0:tt, cs] += _dot_nt(drp_b[:, cs], wa_ref[h]) + _dot_nt(dip_b[:, cs], wx_ref[h])
            gwa_ref[h] += _dot_tn(ub[:, cs], drp_b[:, cs])
            gwx_ref[h] += _dot_tn(ub[:, cs], dip_b[:, cs])

        du = due[0:tt, :]
        sums_ref[M_CBIAS:M_CBIAS + 1, :] += jnp.sum(du, axis=0, keepdims=True)
        drx = prm_ref[P_WB:P_WB + 1, :] * due[3:3 + tt, :]
        sums_ref[M_WB:M_WB + 1, :] += jnp.sum(du * xe[5:5 + tt, :], axis=0, keepdims=True)
        for k in range(1, 4):
            drx = drx + prm_ref[P_WB + k:P_WB + k + 1, :] * due[3 - k:3 - k + tt, :]
            sums_ref[M_WB + k:M_WB + k + 1, :] += jnp.sum(du * xe[5 + k:5 + k + tt, :], axis=0, keepdims=True)
        dp_ref[:, 3 * D:4 * D] = drx.astype(BF16)
        dya = dye[0:tt, :]
        dv = prm_ref[P_WA:P_WA + 1, :] * dye[2:2 + tt, :]
        sums_ref[M_WA:M_WA + 1, :] += jnp.sum(dya * ve[6:6 + tt, :], axis=0, keepdims=True)
        for k in range(1, 3):
            dv = dv + prm_ref[P_WA + k:P_WA + k + 1, :] * dye[2 - k:2 - k + tt, :]
            sums_ref[M_WA + k:M_WA + k + 1, :] += jnp.sum(dya * ve[6 + k:6 + k + tt, :], axis=0, keepdims=True)
        dp_ref[:, D:2 * D] = (dv * proj_ref[2].astype(F32)).astype(BF16)
        dp_ref[:, 2 * D:3 * D] = (dv * proj_ref[1].astype(F32)).astype(BF16)
        due[tt:tt + 8, :] = due[0:8, :]
        dye[tt:tt + 8, :] = dye[0:8, :]

        @pl.when(i == nt - 1)
        def _():
            sums_ref[M_LS:M_LS + 1, :] = sums_ref[M_LS:M_LS + 1, :] * _sig(-prm_ref[P_LAM:P_LAM + 1, :])

    big = lambda: pltpu.VMEM((tt + 8, D), F32)
    tile = lambda: pltpu.VMEM((tt, D), F32)
    return pl.pallas_call(
        body, name="mixer_bwd", grid=(nt,),
        in_specs=[pl.BlockSpec((7, tt, D), lambda i: (0, rev(i), 0)),
                  pl.BlockSpec((7, 8, D), lambda i: (0, halo(i), 0)),
                  pl.BlockSpec((tt, D), lambda i: (rev(i), 0)),
                  pl.BlockSpec((8, D), lambda i: (halo(i), 0)),
                  pl.BlockSpec((tt, D), lambda i: (rev(i), 0)),
                  pl.BlockSpec((16, D), lambda i: (0, 0)),
                  pl.BlockSpec((HEADS, HB, HB), lambda i: (0, 0, 0)),
                  pl.BlockSpec((HEADS, HB, HB), lambda i: (0, 0, 0))],
        out_specs=[pl.BlockSpec((tt, 7 * D), lambda i: (rev(i), 0)),
                   pl.BlockSpec((16, D), lambda i: (0, 0)),
                   pl.BlockSpec((HEADS, HB, HB), lambda i: (0, 0, 0)),
                   pl.BlockSpec((HEADS, HB, HB), lambda i: (0, 0, 0))],
        out_shape=[jax.ShapeDtypeStruct((t_len, 7 * D), BF16), jax.ShapeDtypeStruct((16, D), F32),
                   jax.ShapeDtypeStruct((HEADS, HB, HB), F32), jax.ShapeDtypeStruct((HEADS, HB, HB), F32)],
        scratch_shapes=[big(), big(), big(), tile(), tile(), tile(), tile(), big(), big(), tile(), tile(),
                        pltpu.VMEM((8, D), F32), pltpu.VMEM((8, D), F32)],
        compiler_params=_cp("arbitrary"),
    )(proj, proj, hl, hl, dmg, prm, wa, wx)


def _in_proj_bwd(dproj, w_in, x, dx1, mod, g_mix):
    t_len = x.shape[0]
    tm = min(TMI, t_len)

    def body(dp_ref, w_ref, x_ref, dx1_ref, mod_ref, g_ref, gx_ref, sums_ref, acc):
        i, s = pl.program_id(0), pl.program_id(1)

        @pl.when((i == 0) & (s == 0))
        def _():
            sums_ref[...] = jnp.zeros_like(sums_ref)

        @pl.when(s == 0)
        def _():
            acc[...] = jnp.zeros_like(acc)

        for rows in _sub_blocks(tm):
            acc[rows, :] += _dot_nt(dp_ref[rows, :], w_ref[...])

        @pl.when(s == 6)
        def _():
            def write(rows, dx):
                gx_ref[rows, :] = dx

            _norm_bwd_rows(tm, 16, acc, x_ref, dx1_ref, mod_ref[1:2, :], g_ref[...], sums_ref, write)

    return pl.pallas_call(
        body, name="in_proj_bwd", grid=(t_len // tm, 7),
        in_specs=[pl.BlockSpec((tm, D), lambda i, s: (i, s)),
                  pl.BlockSpec((D, D), lambda i, s: (0, s)),
                  pl.BlockSpec((tm, D), lambda i, s: (i, 0)), pl.BlockSpec((tm, D), lambda i, s: (i, 0)),
                  pl.BlockSpec((8, D), lambda i, s: (0, 0)), pl.BlockSpec((1, D), lambda i, s: (0, 0))],
        out_specs=[pl.BlockSpec((tm, D), lambda i, s: (i, 0)), pl.BlockSpec((8, D), lambda i, s: (0, 0))],
        out_shape=[jax.ShapeDtypeStruct((t_len, D), F32), jax.ShapeDtypeStruct((8, D), F32)],
        scratch_shapes=[pltpu.VMEM((tm, D), F32)],
        compiler_params=_cp("arbitrary", "arbitrary"),
    )(dproj, w_in, x, dx1, mod, g_mix)


def _in_wgrad(h, dproj, g_wa, g_wx, order):
    t_len = h.shape[0]
    tk = min(TKI, t_len)
    nk = t_len // tk
    nq = 4
    cw = 7 * D // NDEV
    hr = HB // NDEV

    def body(ord_ref, h_ref, d_ref, ga_ref, gx_ref, parts_ref, pa_ref, px_ref, acc, stage, *sems):
        p, k = pl.program_id(0), pl.program_id(1)
        q = ord_ref[p]

        def head_rows(ref):
            return lambda s: ref.at[:, pl.ds(s * hr, hr), :]

        @pl.when((p == 0) & (k == 0))
        def _():
            for s in range(NDEV):
                _rs_send(head_rows(ga_ref)(s), pa_ref, s, *sems[3:6])
                _rs_send(head_rows(gx_ref)(s), px_ref, s, *sems[6:9])

        part = _dot_tn(h_ref[...], d_ref[...])

        @pl.when(k == 0)
        def _():
            acc[...] = part

        @pl.when(k > 0)
        def _():
            acc[...] += part

        def src_of(blk, r):
            return stage.at[blk, :, pl.ds(r * cw, cw)]

        @pl.when(k == nk - 1)
        def _():
            stage[q] = acc[...].astype(BF16)
            for r in range(2):
                _rs_send(src_of(q, r), parts_ref, q * 2 + r, *sems[0:3])

        @pl.when((p == nq - 1) & (k == nk - 1))
        def _():
            _rs_finish(lambda s: src_of(s // 2, s % 2), parts_ref, *sems[0:3])
            _rs_finish(head_rows(ga_ref), pa_ref, *sems[3:6])
            _rs_finish(head_rows(gx_ref), px_ref, *sems[6:9])

    return pl.pallas_call(
        body, name="in_wgrad",
        grid_spec=pltpu.PrefetchScalarGridSpec(
            num_scalar_prefetch=1, grid=(nq, nk),
            in_specs=[pl.BlockSpec((tk, D), lambda p, k, o: (k, 0)),
                      pl.BlockSpec((tk, 2 * cw), lambda p, k, o: (k, o[p])), _ANY, _ANY],
            out_specs=[_ANY, _ANY, _ANY],
            scratch_shapes=[pltpu.VMEM((D, 2 * cw), F32), pltpu.VMEM((nq, D, 2 * cw), BF16)] + _RS_SEMS * 3),
        out_shape=[jax.ShapeDtypeStruct((NDEV, D, cw), BF16), jax.ShapeDtypeStruct((NDEV, HEADS, hr, HB), F32),
                   jax.ShapeDtypeStruct((NDEV, HEADS, hr, HB), F32)],
        compiler_params=_cp("arbitrary", "arbitrary"),
    )(order, h, dproj, g_wa, g_wx)


def _ada_fwd(c_all, w_ada, b_cols):
    def body(c_ref, w_ref, b_ref, o_ref):
        cv = c_ref[...]
        o_ref[...] = _dot((cv * _sig(cv)).astype(BF16), w_ref[...].astype(BF16)) + b_ref[...]

    return pl.pallas_call(body, name="ada_fwd", out_shape=jax.ShapeDtypeStruct((16, w_ada.shape[1]), F32),
                          compiler_params=_cp())(c_all, w_ada, b_cols)


def _adam_math(w, g, m, v):
    m = ADAM_B1 * m + (1.0 - ADAM_B1) * g
    v = ADAM_B2 * v + (1.0 - ADAM_B2) * (g * g)
    m_hat = m / (1.0 - ADAM_B1 ** ADAM_STEP)
    v_hat = v / (1.0 - ADAM_B2 ** ADAM_STEP)
    delta = -ADAM_LR * (m_hat / (jnp.sqrt(v_hat) + ADAM_EPS) + ADAM_WD * w)
    return delta, m, v


def _ada_bwd(c_all, dmod_cols, w, m, v):
    rb = 256
    n = w.shape[1]
    nrow = c_all.shape[0]

    def body(c_ref, d_ref, w_ref, m_ref, v_ref, g_ref, dl_ref, nm_ref, nv_ref):
        cv = c_ref[...]
        g = _dot_tn((cv * _sig(cv)).astype(BF16), d_ref[...].astype(BF16))
        g_ref[...] = g
        dl_ref[...], nm_ref[...], nv_ref[...] = _adam_math(w_ref[...], g, m_ref[...], v_ref[...])

    blk = pl.BlockSpec((rb, n), lambda i: (i, 0))
    sds = jax.ShapeDtypeStruct(w.shape, F32)
    return pl.pallas_call(
        body, name="ada_bwd", grid=(D // rb,),
        in_specs=[pl.BlockSpec((nrow, rb), lambda i: (0, i)), pl.BlockSpec((nrow, n), lambda i: (0, 0)), blk, blk, blk],
        out_specs=[blk, blk, blk, blk], out_shape=[sds, sds, sds, sds],
        compiler_params=_cp("parallel"),
    )(c_all, dmod_cols, w, m, v)


def _adam(name, parts, w, m, v):
    p, r, c = parts.shape
    rb = r
    for cand in (256, 128, 64, 32, 16, 8):
        if r % cand == 0 and r >= cand:
            rb = cand
            break

    def body(p_ref, w_ref, m_ref, v_ref, g_ref, dl_ref, nm_ref, nv_ref):
        g = p_ref[0].astype(F32)
        for q in range(1, p):
            g = g + p_ref[q].astype(F32)
        g_ref[...] = g
        dl_ref[...], nm_ref[...], nv_ref[...] = _adam_math(w_ref[...], g, m_ref[...], v_ref[...])

    blk = pl.BlockSpec((rb, c), lambda i: (i, 0))
    sds = jax.ShapeDtypeStruct((r, c), F32)
    return pl.pallas_call(
        body, name=name, grid=(r // rb,),
        in_specs=[pl.BlockSpec((p, rb, c), lambda i: (0, i, 0)), blk, blk, blk],
        out_specs=[blk, blk, blk, blk], out_shape=[sds, sds, sds, sds],
        compiler_params=_cp("parallel"),
    )(parts, w, m, v)


def _my_pos():
    return lax.axis_index("x"), lax.axis_index("y"), lax.axis_index("c")


def _all_gather_small(name, v):
    m_per, n = v.shape

    def body(x_ref, out_ref, send_sems, recv_sems, local_sem):
        x, y, c = _my_pos()
        me, sibling = (x, y, c), (x, y, 1 - c)
        chips = [(1 - x, y), (x, 1 - y), (1 - x, 1 - y)]

        def rows(px, py, pc):
            return out_ref.at[pl.ds((4 * px + 2 * py + pc) * m_per, m_per), :]

        def copy(k, block, to, src=None):
            return pltpu.make_async_remote_copy(
                src_ref=rows(*block) if src is None else src, dst_ref=rows(*block),
                send_sem=send_sems.at[k], recv_sem=recv_sems.at[k], device_id=to, device_id_type=MESH)

        mine = pltpu.make_async_copy(x_ref, rows(*me), local_sem)
        mine.start()
        first = [copy(0, me, sibling, src=x_ref)]
        first += [copy(1 + j, me, (*chip, c), src=x_ref) for j, chip in enumerate(chips)]
        for cp in first:
            cp.start()
        passed = [copy(4 + j, (*chip, c), sibling) for j, chip in enumerate(chips)]
        for j, chip in enumerate(chips):
            copy(1 + j, (*chip, c), me).wait_recv()
            passed[j].start()
        copy(0, sibling, me).wait_recv()
        for j, chip in enumerate(chips):
            copy(4 + j, (*chip, 1 - c), me).wait_recv()
        for cp in first + passed:
            cp.wait_send()
        mine.wait()

    return pl.pallas_call(
        body, name=name, out_shape=jax.ShapeDtypeStruct((NDEV * m_per, n), v.dtype),
        in_specs=[pl.BlockSpec(memory_space=pltpu.VMEM)], out_specs=pl.BlockSpec(memory_space=pltpu.VMEM),
        scratch_shapes=[pltpu.SemaphoreType.DMA((7,)), pltpu.SemaphoreType.DMA((7,)), pltpu.SemaphoreType.DMA],
    )(v)


def _blk_cols(n):
    return lambda ref, b: ref.at[:, pl.ds(pl.multiple_of(b * n, 128), n)]


def _blk_rows(n):
    return lambda ref, b: ref.at[pl.ds(pl.multiple_of(b * n, 8), n), :]


def _blk_lead(ref, b):
    return ref.at[b]


def _blk_heads(ref, b):
    return ref.at[:, pl.ds(pl.multiple_of(b * (HB // NDEV), 8), HB // NDEV), :]


def _ag_phases(ins, outs, slicers, send_sems, recv_sems, local_sems):
    na = len(ins)
    x, y, c = _my_pos()
    me, sibling = (x, y, c), (x, y, 1 - c)
    chips = [(1 - x, y), (x, 1 - y), (1 - x, 1 - y)]

    def copy(a, k, block, to, from_shard=False):
        px, py, pc = block
        dst = slicers[a](outs[a], 4 * px + 2 * py + pc)
        return pltpu.make_async_remote_copy(
            src_ref=ins[a] if from_shard else dst, dst_ref=dst,
            send_sem=send_sems.at[a * 7 + k], recv_sem=recv_sems.at[a * 7 + k], device_id=to, device_id_type=MESH)

    def local(a):
        return pltpu.make_async_copy(ins[a], slicers[a](outs[a], 4 * x + 2 * y + c), local_sems.at[a])

    def firsts(a):
        return [copy(a, 0, me, sibling, True)] + [copy(a, 1 + j, me, (*chip, c), True) for j, chip in enumerate(chips)]

    def start():
        for a in range(na):
            local(a).start()
            for cp in firsts(a):
                cp.start()

    def forward():
        for a in range(na):
            for j, chip in enumerate(chips):
                copy(a, 1 + j, (*chip, c), me).wait_recv()
                copy(a, 4 + j, (*chip, c), sibling).start()

    def finish():
        for a in range(na):
            copy(a, 0, sibling, me).wait_recv()
            for j, chip in enumerate(chips):
                copy(a, 4 + j, (*chip, 1 - c), me).wait_recv()
        for a in range(na):
            for cp in firsts(a) + [copy(a, 4 + j, (*chip, c), sibling) for j, chip in enumerate(chips)]:
                cp.wait_send()
            local(a).wait()

    return start, forward, finish


def _ag_sems(na):
    return [pltpu.SemaphoreType.DMA((7 * na,)), pltpu.SemaphoreType.DMA((7 * na,)), pltpu.SemaphoreType.DMA((na,))]


def _all_gather_weights(shards, fulls, slicers):
    na = len(shards)

    def body(*refs):
        start, forward, finish = _ag_phases(refs[:na], refs[na:2 * na], slicers, *refs[2 * na:])
        start()
        forward()
        finish()

    return pl.pallas_call(
        body, name="gather_weights",
        out_shape=[jax.ShapeDtypeStruct(s, sh.dtype) for s, sh in zip(fulls, shards)],
        in_specs=[_ANY] * na, out_specs=[_ANY] * na, scratch_shapes=_ag_sems(na),
    )(*shards)


def _scatter_grads(grads, shard_shapes, slicers):
    na = len(grads)

    def body(*refs):
        ins, outs = refs[:na], refs[na:2 * na]
        send_sems, recv_sems, local_sems = refs[2 * na:]
        x, y, c = _my_pos()
        me = 4 * x + 2 * y + c
        mine, sent = [], []
        for a in range(na):
            cp = pltpu.make_async_copy(slicers[a](ins[a], me), outs[a].at[me], local_sems.at[a])
            cp.start()
            mine.append(cp)
        rel = [(k >> 2 & 1, k >> 1 & 1, k & 1) for k in range(1, NDEV)]
        for a in range(na):
            for k, (fx, fy, fc) in enumerate(rel):
                px, py, pc = x ^ fx, y ^ fy, c ^ fc
                cp = pltpu.make_async_remote_copy(
                    src_ref=slicers[a](ins[a], 4 * px + 2 * py + pc), dst_ref=outs[a].at[me],
                    send_sem=send_sems.at[a * 7 + k], recv_sem=recv_sems.at[a * 7 + k],
                    device_id=(px, py, pc), device_id_type=MESH)
                cp.start()
                sent.append(cp)
        for a in range(na):
            for k, (fx, fy, fc) in enumerate(rel):
                px, py, pc = x ^ fx, y ^ fy, c ^ fc
                src = 4 * px + 2 * py + pc
                pltpu.make_async_remote_copy(
                    src_ref=slicers[a](ins[a], me), dst_ref=outs[a].at[src],
                    send_sem=send_sems.at[a * 7 + k], recv_sem=recv_sems.at[a * 7 + k],
                    device_id=(px, py, pc), device_id_type=MESH).wait_recv()
        for cp in sent:
            cp.wait_send()
        for cp in mine:
            cp.wait()

    any_spec = pl.BlockSpec(memory_space=pl.ANY)
    return pl.pallas_call(
        body, name="scatter_grads",
        out_shape=[jax.ShapeDtypeStruct((NDEV,) + tuple(s), g.dtype) for s, g in zip(shard_shapes, grads)],
        in_specs=[any_spec] * na, out_specs=[any_spec] * na,
        scratch_shapes=[pltpu.SemaphoreType.DMA((7 * na,)), pltpu.SemaphoreType.DMA((7 * na,)),
                        pltpu.SemaphoreType.DMA((na,))],
    )(*grads)


def _local_step(x, target, mod, g_mix, g_ffn, g_fin, prm, w_in, shards):
    fulls = [(HEADS, HB, HB), (HEADS, HB, HB), (D, D), (NDEV, D, FB), (DFF, D)]
    slicers = [_blk_heads, _blk_heads, _blk_rows(D // NDEV), _blk_lead, _blk_rows(DFF // NDEV)]
    proj, h, (wa, wx, w_out, w_gu, w_down) = _in_proj(x, mod, g_mix, w_in, shards, fulls, slicers)
    w_gu = w_gu.reshape(2, 4, D, FB)
    merged, hl = _mixer_fwd(proj, prm, wa, wx)
    x1, h2 = _out_proj(merged, x, mod, g_ffn, w_out)
    gu, dx2, dx2b, loss, d_gfin = _ffn_fwd(h2, x1, target, mod, g_fin, w_gu, w_down)
    dgu, act, dx1, dx1b, dmg, sums2 = _ffn_bwd(dx2, gu, x1, mod, g_ffn, w_gu, w_down, w_out)
    dev_order = _xor_order(_my_index(), NDEV)
    chip_order = _xor_order(_my_index() >> 1, NDEV // 2)
    p_wgu = _gu_wgrad(h2, dgu, dev_order)
    p_wdown, d_gt2 = _scaled_wgrad("down_wgrad", act, dx2b, w_down, 5, mod, chip_order)
    p_wout, d_gt1 = _scaled_wgrad("out_wgrad", merged.reshape(1, *merged.shape), dx1b, w_out, 2, mod,
                                  jnp.zeros((1,), jnp.int32))
    dproj, msums, g_wa, g_wx = _mixer_bwd(proj, hl, dmg, prm, wa, wx)
    p_win, p_wa, p_wx = _in_wgrad(h, dproj, g_wa, g_wx, chip_order)
    grad_x, sums1 = _in_proj_bwd(dproj, w_in, x, dx1, mod, g_mix)
    return dict(loss=loss, grad_x=grad_x, d_gfin=d_gfin, sums1=sums1, sums2=sums2, msums=msums,
                d_gt1=d_gt1[0:1], d_gt2=d_gt2[0:1], p_win=p_win, p_wa=p_wa, p_wx=p_wx, p_wout=p_wout, p_wgu=p_wgu,
                p_wdown=p_wdown)


def kernel(x, c, w_ada, b_ada, g_norm_mix, w_in, conv_a_w, conv_b_w, conv_b_bias, w_rg_a, b_rg_a, w_rg_x, b_rg_x, lru_lambda, w_out, g_norm_ffn, w_gate_up, w_down, g_norm_final, loss_target, m_w_ada, m_b_ada, m_g_norm_mix, m_w_in, m_conv_a_w, m_conv_b_w, m_conv_b_bias, m_w_rg_a, m_b_rg_a, m_w_rg_x, m_b_rg_x, m_lru_lambda, m_w_out, m_g_norm_ffn, m_w_gate_up, m_w_down, m_g_norm_final, v_w_ada, v_b_ada, v_g_norm_mix, v_w_in, v_conv_a_w, v_conv_b_w, v_conv_b_bias, v_w_rg_a, v_b_rg_a, v_w_rg_x, v_b_rg_x, v_lru_lambda, v_w_out, v_g_norm_ffn, v_w_gate_up, v_w_down, v_g_norm_final):
    me = 4 * lax.axis_index("x") + 2 * lax.axis_index("y") + lax.axis_index("c")
    ncol = w_ada.shape[2]
    cw = conv_a_w.shape[2]

    pack0 = jnp.concatenate([c, conv_a_w.reshape(1, 3 * cw), conv_b_w.reshape(1, 4 * cw)], axis=1)
    got0 = _all_gather_small("gather_c", jnp.broadcast_to(pack0, (8, pack0.shape[1])))
    got0 = got0.reshape(NDEV, 8, -1)[:, 0, :]
    c_all = got0[:, :D]
    conv_a = got0[:, D:D + 3 * cw].reshape(NDEV, 3, cw).transpose(1, 0, 2).reshape(3, D)
    conv_b = got0[:, D + 3 * cw:].reshape(NDEV, 4, cw).transpose(1, 0, 2).reshape(4, D)

    b_cols = lax.dynamic_slice_in_dim(b_ada, me * ncol, ncol, axis=1)
    c16 = jnp.concatenate([c_all, jnp.zeros((8, D), F32)], axis=0)
    mod_cols = _ada_fwd(c16, w_ada[0], b_cols)[:NDEV]
    got1 = _all_gather_small("gather_mod", mod_cols).reshape(NDEV, NDEV, ncol)
    mod6 = lax.dynamic_index_in_dim(got1, me, axis=1, keepdims=False).reshape(6, D)
    mod = jnp.concatenate([mod6, jnp.zeros((2, D), F32)], axis=0)

    (w_in_f,) = _all_gather_weights([w_in[0].astype(BF16)], [(D, 7 * D)], [_blk_cols(7 * D // NDEV)])
    shards = [w_rg_a[0].astype(BF16), w_rg_x[0].astype(BF16), w_out[0].astype(BF16), w_gate_up[0].astype(BF16),
              w_down[0].astype(BF16)]

    prm = jnp.concatenate([conv_a, conv_b, conv_b_bias, b_rg_a, b_rg_x, lru_lambda, jnp.zeros((5, D), F32)], axis=0)
    r = _local_step(x[0], loss_target[0], mod, g_norm_mix, g_norm_ffn, g_norm_final.reshape(1, D), prm,
                    w_in_f, shards)

    parts = [r["p_win"], r["p_wa"], r["p_wx"], r["p_wout"], r["p_wgu"], r["p_wdown"]]
    big = {}
    for nm, p, w, m, v in (("w_in", parts[0], w_in, m_w_in, v_w_in), ("w_rg_a", parts[1], w_rg_a, m_w_rg_a, v_w_rg_a),
                           ("w_rg_x", parts[2], w_rg_x, m_w_rg_x, v_w_rg_x), ("w_out", parts[3], w_out, m_w_out, v_w_out),
                           ("w_gate_up", parts[4], w_gate_up, m_w_gate_up, v_w_gate_up),
                           ("w_down", parts[5], w_down, m_w_down, v_w_down)):
        two_d = (-1, w.shape[-1])
        outs = _adam("adam_" + nm, p.reshape((NDEV,) + w.reshape(two_d).shape), w.reshape(two_d), m.reshape(two_d),
                     v.reshape(two_d))
        big[nm] = [o.reshape(w.shape) for o in outs]

    small = jnp.concatenate([
        r["sums1"][S_SH:S_SH + 1], r["sums1"][S_SC:S_SC + 1], r["d_gt1"],
        r["sums2"][S_SH:S_SH + 1], r["sums2"][S_SC:S_SC + 1], r["d_gt2"],
        r["sums1"][S_G:S_G + 1],
        r["msums"][M_CBIAS:M_CBIAS + 1], r["msums"][M_BA:M_BA + 1], r["msums"][M_BX:M_BX + 1],
        r["msums"][M_LS:M_LS + 1],
        r["sums2"][S_G:S_G + 1], r["d_gfin"],
        r["msums"][M_WA:M_WA + 3], r["msums"][M_WB:M_WB + 4],
        jnp.zeros((4, D), F32)], axis=0)
    got2 = _all_gather_small("gather_small", small).reshape(NDEV, 24, D)

    rep_w = jnp.concatenate([b_ada.reshape(6, D), g_norm_mix, conv_b_bias, b_rg_a, b_rg_x, lru_lambda, g_norm_ffn,
                             g_norm_final.reshape(1, D), jnp.zeros((3, D), F32)], axis=0)
    rep_m = jnp.concatenate([m_b_ada.reshape(6, D), m_g_norm_mix, m_conv_b_bias, m_b_rg_a, m_b_rg_x, m_lru_lambda,
                             m_g_norm_ffn, m_g_norm_final.reshape(1, D), jnp.zeros((3, D), F32)], axis=0)
    rep_v = jnp.concatenate([v_b_ada.reshape(6, D), v_g_norm_mix, v_conv_b_bias, v_b_rg_a, v_b_rg_x, v_lru_lambda,
                             v_g_norm_ffn, v_g_norm_final.reshape(1, D), jnp.ones((3, D), F32)], axis=0)
    rep = _adam("adam_rep", got2[:, :16, :], rep_w, rep_m, rep_v)

    conv_parts = lax.dynamic_slice_in_dim(got2[:, 13:21, :], me * cw, cw, axis=2)
    cv_w = jnp.concatenate([conv_a_w[0], conv_b_w[0], jnp.zeros((1, cw), F32)], axis=0)
    cv_m = jnp.concatenate([m_conv_a_w[0], m_conv_b_w[0], jnp.zeros((1, cw), F32)], axis=0)
    cv_v = jnp.concatenate([v_conv_a_w[0], v_conv_b_w[0], jnp.ones((1, cw), F32)], axis=0)
    cvo = _adam("adam_conv", conv_parts, cv_w, cv_m, cv_v)

    dmod_cols = lax.dynamic_slice_in_dim(got2[:, :6, :].reshape(NDEV, 6 * D), me * ncol, ncol, axis=1)
    dmod16 = jnp.concatenate([dmod_cols, jnp.zeros((8, ncol), F32)], axis=0)
    ada = _ada_bwd(c16, dmod16, w_ada[0], m_w_ada[0], v_w_ada[0])

    loss = lax.psum(r["loss"][0, 0], AXES)

    def pick(q):
        one = lambda i: rep[q][i:i + 1]
        return [ada[q].reshape(w_ada.shape), rep[q][0:6].reshape(b_ada.shape), one(6), big["w_in"][q],
                cvo[q][0:3].reshape(conv_a_w.shape), cvo[q][3:7].reshape(conv_b_w.shape), one(7),
                big["w_rg_a"][q], one(8), big["w_rg_x"][q], one(9), one(10), big["w_out"][q], one(11),
                big["w_gate_up"][q], big["w_down"][q], rep[q][12]]

    return (loss, r["grad_x"].reshape(x.shape), *pick(0), *pick(1), *pick(2), *pick(3))
```

```python
import functools
import math

import jax
import jax.numpy as jnp
from jax import lax
from jax.experimental import pallas as pl
from jax.experimental.pallas import tpu as pltpu

F32 = jnp.float32
BF16 = jnp.bfloat16

D = 1024
DFF = 2816
NDEV = 8
HEADS = 4
HB = D // HEADS
FB = DFF // 4
EPS = 1e-6
LRU_C = 8.0
ADAM_LR, ADAM_B1, ADAM_B2, ADAM_EPS, ADAM_WD, ADAM_STEP = 0.001, 0.9, 0.999, 1e-08, 0.01, 10

VMEM_LIMIT = 56 * 1024 * 1024
TM = 512
TMI = 1024
TK = 2048
TKI = 1024
SUB = 256
UNROLL = 4
TT = 256
CG = 256
MESH = pl.DeviceIdType.MESH
AXES = ("x", "y", "c")


def _cp(*sem):
    return pltpu.CompilerParams(dimension_semantics=sem, vmem_limit_bytes=VMEM_LIMIT)


def _sig(x):
    return 1.0 / (1.0 + jnp.exp(-x))


def _log_sigmoid(x):
    z = jnp.exp(-jnp.abs(x))
    u = 1.0 + z
    d = u - 1.0
    l1p = jnp.where(d == 0.0, z, jnp.log(u) * (z / jnp.where(d == 0.0, 1.0, d)))
    return -(jnp.maximum(-x, 0.0) + l1p)


def _neg_expm1(x):
    p = x * (1.0 + x * 0.5 * (1.0 + x * (1.0 / 3.0) * (1.0 + x * 0.25 * (1.0 + x * 0.2 * (1.0 + x * (1.0 / 6.0))))))
    return jnp.where(x > -0.25, -p, 1.0 - jnp.exp(x))


_GC = math.sqrt(2.0 / math.pi)


def _gelu(x):
    t = jnp.tanh(_GC * (x + 0.044715 * x * x * x))
    return 0.5 * x * (1.0 + t), t


def _dot(a, b):
    return jnp.dot(a, b, preferred_element_type=F32)


def _dot_nt(a, b):
    return lax.dot_general(a, b, (((1,), (1,)), ((), ())), preferred_element_type=F32)


def _dot_tn(a, b):
    return lax.dot_general(a, b, (((0,), (0,)), ((), ())), preferred_element_type=F32)


def _resident(shape):
    return pl.BlockSpec(shape, lambda *_: (0,) * len(shape), pipeline_mode=pl.Buffered(1))


def _sub_blocks(n_rows):
    step = min(SUB, n_rows)
    return [slice(r, r + step) for r in range(0, n_rows, step)]


def _fold8(v):
    return v[0:8] + v[8:16]


def _in_proj(x, mod, g_mix, w_in, shards, fulls, slicers):
    t_len = x.shape[0]
    tm = min(TMI, t_len)
    ni = t_len // tm
    na = len(shards)
    rc = 32

    def body(x_ref, mod_ref, g_ref, w_ref, *rest):
        ins, (proj_ref, h_ref), outs = rest[:na], rest[na:na + 2], rest[na + 2:2 * na + 2]
        h_scr = rest[2 * na + 2]
        start, forward, finish = _ag_phases(ins, outs, slicers, *rest[2 * na + 3:])
        i, s = pl.program_id(0), pl.program_id(1)

        @pl.when((i == 0) & (s == 0))
        def _():
            start()

        @pl.when((i == ni // 2) & (s == 0))
        def _():
            forward()

        @pl.when(s == 0)
        def _():
            gs = g_ref[...] * (1.0 + mod_ref[1:2, :])
            sh = mod_ref[0:1, :]

            def chunk(i, carry):
                rows = pl.ds(pl.multiple_of(i * rc, rc), rc)
                xv = x_ref[rows, :]
                r = lax.rsqrt(jnp.mean(xv * xv, axis=-1, keepdims=True) + EPS)
                h = (xv * r * gs + sh).astype(BF16)
                h_scr[rows, :] = h
                h_ref[rows, :] = h
                return carry

            lax.fori_loop(0, tm // rc, chunk, 0, unroll=UNROLL)

        proj_ref[0] = _dot(h_scr[...], w_ref[:, pl.ds(pl.multiple_of(s * D, D), D)]).astype(BF16)

        @pl.when((i == ni - 1) & (s == 6))
        def _():
            finish()

    res = pl.pallas_call(
        body, name="in_proj", grid=(ni, 7),
        in_specs=[pl.BlockSpec((tm, D), lambda i, s: (i, 0)),
                  pl.BlockSpec((8, D), lambda i, s: (0, 0)),
                  pl.BlockSpec((1, D), lambda i, s: (0, 0)),
                  _resident((D, 7 * D))] + [_ANY] * na,
        out_specs=[pl.BlockSpec((1, tm, D), lambda i, s: (s, i, 0)),
                   pl.BlockSpec((tm, D), lambda i, s: (i, 0))] + [_ANY] * na,
        out_shape=[jax.ShapeDtypeStruct((7, t_len, D), BF16), jax.ShapeDtypeStruct((t_len, D), BF16)]
        + [jax.ShapeDtypeStruct(f, sh.dtype) for f, sh in zip(fulls, shards)],
        scratch_shapes=[pltpu.VMEM((tm, D), BF16)] + _ag_sems(na),
        compiler_params=_cp("arbitrary", "arbitrary"),
    )(x, mod, g_mix, w_in, *shards)
    return res[0], res[1], res[2:]


P_WA, P_WB, P_CBIAS, P_BA, P_BX, P_LAM = 0, 3, 7, 8, 9, 10


def _lru_gates(rp, ip, ls, first_row):
    r = _sig(rp)
    ig = _sig(ip)
    la = LRU_C * r * ls
    a = jnp.exp(la)
    m2 = _neg_expm1(2.0 * la)
    mult = jnp.where(first_row, 1.0, jnp.sqrt(jnp.maximum(m2, 0.0)))
    return r, ig, la, a, m2, mult


def _mixer_fwd(proj, prm, wa, wx):
    t_len = proj.shape[1]
    tt = min(TT, t_len)

    def body(proj_ref, prm_ref, wa_ref, wx_ref, mg_ref, hl_ref, xe, ve, hc, u_s, ya_s, rp_s, ip_s):
        t = pl.program_id(0)

        @pl.when(t == 0)
        def _():
            xe[0:8, :] = jnp.zeros((8, D), F32)
            ve[0:8, :] = jnp.zeros((8, D), F32)
            hc[...] = jnp.zeros((8, D), F32)

        xe[8:8 + tt, :] = proj_ref[3].astype(F32)
        ve[8:8 + tt, :] = proj_ref[1].astype(F32) * proj_ref[2].astype(F32)
        u = prm_ref[P_CBIAS:P_CBIAS + 1, :] + prm_ref[P_WB:P_WB + 1, :] * xe[5:5 + tt, :]
        for k in range(1, 4):
            u = u + prm_ref[P_WB + k:P_WB + k + 1, :] * xe[5 + k:5 + k + tt, :]
        u_s[...] = u
        ya = prm_ref[P_WA:P_WA + 1, :] * ve[6:6 + tt, :]
        for k in range(1, 3):
            ya = ya + prm_ref[P_WA + k:P_WA + k + 1, :] * ve[6 + k:6 + k + tt, :]
        ya_s[...] = ya
        xe[0:8, :] = xe[tt:tt + 8, :]
        ve[0:8, :] = ve[tt:tt + 8, :]

        ub = u.astype(BF16)
        for h in range(HEADS):
            cs = slice(h * HB, (h + 1) * HB)
            rp_s[:, cs] = _dot(ub[:, cs], wa_ref[h]) + prm_ref[P_BA:P_BA + 1, cs]
            ip_s[:, cs] = _dot(ub[:, cs], wx_ref[h]) + prm_ref[P_BX:P_BX + 1, cs]

        ls_all = _log_sigmoid(prm_ref[P_LAM:P_LAM + 1, :])
        row = lax.broadcasted_iota(jnp.int32, (8, CG), 0)

        def blk(i, carry):
            r0 = pl.multiple_of(i * 16, 16)
            for g in range(D // CG):
                cs = slice(g * CG, (g + 1) * CG)
                ls = ls_all[:, cs]
                hprev = hc[:, cs]
                hs = []
                for sb in range(2):
                    rr = r0 + 8 * sb
                    first = (row + (t * tt + rr)) == 0
                    _, ig, _, a, _, mult = _lru_gates(rp_s[pl.ds(rr, 8), cs], ip_s[pl.ds(rr, 8), cs], ls, first)
                    b = mult * (ig * u_s[pl.ds(rr, 8), cs])
                    for s in (1, 2, 4):
                        a_sh = jnp.where(row >= s, pltpu.roll(a, s, 0), 1.0)
                        b_sh = jnp.where(row >= s, pltpu.roll(b, s, 0), 0.0)
                        b = a * b_sh + b
                        a = a * a_sh
                    hv = a * hprev + b
                    hprev = jnp.broadcast_to(hv[7:8, :], hv.shape)
                    hs.append(hv)
                hc[:, cs] = hprev
                h16 = jnp.concatenate(hs, axis=0)
                rows = pl.ds(r0, 16)
                gl, _ = _gelu(proj_ref[4, rows, cs].astype(F32))
                y_b = h16 * gl
                y_a = proj_ref[0, rows, cs].astype(F32) * ya_s[rows, cs]
                mg = _sig(proj_ref[5, rows, cs].astype(F32)) * y_a + _sig(proj_ref[6, rows, cs].astype(F32)) * y_b
                mg_ref[rows, cs] = mg.astype(BF16)
                hl_ref[rows, cs] = h16.astype(BF16)
            return carry

        lax.fori_loop(0, tt // 16, blk, 0)

    return pl.pallas_call(
        body, name="mixer_fwd", grid=(t_len // tt,),
        in_specs=[pl.BlockSpec((7, tt, D), lambda t: (0, t, 0)),
                  pl.BlockSpec((16, D), lambda t: (0, 0)),
                  pl.BlockSpec((HEADS, HB, HB), lambda t: (0, 0, 0)),
                  pl.BlockSpec((HEADS, HB, HB), lambda t: (0, 0, 0))],
        out_specs=[pl.BlockSpec((tt, D), lambda t: (t, 0)), pl.BlockSpec((tt, D), lambda t: (t, 0))],
        out_shape=[jax.ShapeDtypeStruct((t_len, D), BF16), jax.ShapeDtypeStruct((t_len, D), BF16)],
        scratch_shapes=[pltpu.VMEM((tt + 8, D), F32), pltpu.VMEM((tt + 8, D), F32), pltpu.VMEM((8, D), F32),
                        pltpu.VMEM((tt, D), F32), pltpu.VMEM((tt, D), F32), pltpu.VMEM((tt, D), F32),
                        pltpu.VMEM((tt, D), F32)],
        compiler_params=_cp("arbitrary"),
    )(proj, prm, wa, wx)


def _out_proj(merged, x, mod, g_ffn, w_out):
    t_len = x.shape[0]
    tm = min(TM, t_len)

    def body(mg_ref, x_ref, mod_ref, g_ref, w_ref, x1_ref, h2_ref):
        gt1 = mod_ref[2:3, :]
        for rows in _sub_blocks(tm):
            x1_ref[rows, :] = x_ref[rows, :] + gt1 * _dot(mg_ref[rows, :], w_ref[...])
        gs = g_ref[...] * (1.0 + mod_ref[4:5, :])
        sh = mod_ref[3:4, :]

        def chunk(c, carry):
            rows = pl.ds(pl.multiple_of(c * 16, 16), 16)
            x1 = x1_ref[rows, :]
            r = lax.rsqrt(jnp.mean(x1 * x1, axis=-1, keepdims=True) + EPS)
            h2_ref[rows, :] = (x1 * r * gs + sh).astype(BF16)
            return carry

        lax.fori_loop(0, tm // 16, chunk, 0, unroll=UNROLL)

    return pl.pallas_call(
        body, name="out_proj", grid=(t_len // tm,),
        in_specs=[pl.BlockSpec((tm, D), lambda i: (i, 0)), pl.BlockSpec((tm, D), lambda i: (i, 0)),
                  pl.BlockSpec((8, D), lambda i: (0, 0)), pl.BlockSpec((1, D), lambda i: (0, 0)),
                  pl.BlockSpec((D, D), lambda i: (0, 0))],
        out_specs=[pl.BlockSpec((tm, D), lambda i: (i, 0)), pl.BlockSpec((tm, D), lambda i: (i, 0))],
        out_shape=[jax.ShapeDtypeStruct((t_len, D), F32), jax.ShapeDtypeStruct((t_len, D), BF16)],
        compiler_params=_cp("parallel"),
    )(merged, x, mod, g_ffn, w_out)


def _ffn_fwd(h2, x1, target, mod, g_fin, w_gu, w_down):
    t_len = x1.shape[0]
    tm = min(TM, t_len)
    assert tm % (16 * UNROLL) == 0

    def body(h2_ref, x1_ref, tg_ref, mod_ref, g_ref, wgu_ref, wd_ref, gu_ref, dx2_ref, dx2b_ref, loss_ref, dg_ref, acc):
        i, j = pl.program_id(0), pl.program_id(1)

        @pl.when((i == 0) & (j == 0))
        def _():
            loss_ref[...] = jnp.zeros_like(loss_ref)
            dg_ref[...] = jnp.zeros_like(dg_ref)

        @pl.when(j == 0)
        def _():
            acc[...] = jnp.zeros_like(acc)

        for rows in _sub_blocks(tm):
            hb = h2_ref[rows, :]
            gate = _dot(hb, wgu_ref[0, j])
            up = _dot(hb, wgu_ref[1, j])
            gu_ref[0, 0, rows, :] = gate.astype(BF16)
            gu_ref[1, 0, rows, :] = up.astype(BF16)
            act = (gate * _sig(gate) * up).astype(BF16)
            acc[rows, :] += _dot(act, wd_ref[pl.ds(pl.multiple_of(j * FB, 16), FB), :])

        @pl.when(j == 3)
        def _():
            gt2 = mod_ref[5:6, :]
            gf = g_ref[...]

            def chunk(c, carry):
                s_loss, s_dg = carry
                for u in range(UNROLL):
                    rows = pl.ds(pl.multiple_of(c * (16 * UNROLL), 16) + 16 * u, 16)
                    x2 = x1_ref[rows, :] + gt2 * acc[rows, :]
                    r = lax.rsqrt(jnp.mean(x2 * x2, axis=-1, keepdims=True) + EPS)
                    xn = x2 * r
                    diff = xn * gf - tg_ref[rows, :]
                    dy = diff * (1.0 / D)
                    dxn = dy * gf
                    dx2 = r * (dxn - xn * jnp.mean(dxn * xn, axis=-1, keepdims=True))
                    dx2_ref[rows, :] = dx2
                    dx2b_ref[rows, :] = dx2.astype(BF16)
                    s_loss, s_dg = s_loss + _fold8(diff * diff), s_dg + _fold8(dy * xn)
                return s_loss, s_dg

            zero = jnp.zeros((8, D), F32)
            s_loss, s_dg = lax.fori_loop(0, tm // (16 * UNROLL), chunk, (zero, zero))
            loss_ref[...] += jnp.sum(s_loss) * (0.5 / D)
            dg_ref[...] += jnp.sum(s_dg, axis=0, keepdims=True)

    return pl.pallas_call(
        body, name="ffn_fwd", grid=(t_len // tm, 4),
        in_specs=[pl.BlockSpec((tm, D), lambda i, j: (i, 0)), pl.BlockSpec((tm, D), lambda i, j: (i, 0)),
                  pl.BlockSpec((tm, D), lambda i, j: (i, 0)), pl.BlockSpec((8, D), lambda i, j: (0, 0)),
                  pl.BlockSpec((1, D), lambda i, j: (0, 0)),
                  _resident((2, 4, D, FB)), _resident((DFF, D))],
        out_specs=[pl.BlockSpec((2, 1, tm, FB), lambda i, j: (0, j, i, 0)),
                   pl.BlockSpec((tm, D), lambda i, j: (i, 0)),
                   pl.BlockSpec((tm, D), lambda i, j: (i, 0)),
                   pl.BlockSpec((1, 128), lambda i, j: (0, 0)),
                   pl.BlockSpec((1, D), lambda i, j: (0, 0))],
        out_shape=[jax.ShapeDtypeStruct((2, 4, t_len, FB), BF16), jax.ShapeDtypeStruct((t_len, D), F32),
                   jax.ShapeDtypeStruct((t_len, D), BF16),
                   jax.ShapeDtypeStruct((1, 128), F32), jax.ShapeDtypeStruct((1, D), F32)],
        scratch_shapes=[pltpu.VMEM((tm, D), F32)],
        compiler_params=_cp("arbitrary", "arbitrary"),
    )(h2, x1, target, mod, g_fin, w_gu, w_down)


S_SH, S_SC, S_G = 0, 1, 2


def _norm_bwd_rows(n_rows, rc, dh_ref, x_ref, dres_ref, scale, gain, sums_ref, write):
    assert n_rows % (rc * UNROLL) == 0
    gs = 1.0 + scale
    fold = _fold8 if rc == 16 else (lambda v: v)

    def chunk(c, carry):
        s_sh, s_sc, s_g = carry
        for u in range(UNROLL):
            rows = pl.ds(pl.multiple_of(c * (rc * UNROLL), rc) + rc * u, rc)
            dh = dh_ref[rows, :]
            xv = x_ref[rows, :]
            r = lax.rsqrt(jnp.mean(xv * xv, axis=-1, keepdims=True) + EPS)
            xn = xv * r
            dhn = dh * gs
            dxn = dhn * gain
            write(rows, dres_ref[rows, :] + r * (dxn - xn * jnp.mean(dxn * xn, axis=-1, keepdims=True)))
            s_sh, s_sc, s_g = s_sh + fold(dh), s_sc + fold(dh * (xn * gain)), s_g + fold(dhn * xn)
        return s_sh, s_sc, s_g

    zero = jnp.zeros((8, D), F32)
    s_sh, s_sc, s_g = lax.fori_loop(0, n_rows // (rc * UNROLL), chunk, (zero, zero, zero))
    sums_ref[S_SH:S_SH + 1, :] += jnp.sum(s_sh, axis=0, keepdims=True)
    sums_ref[S_SC:S_SC + 1, :] += jnp.sum(s_sc, axis=0, keepdims=True)
    sums_ref[S_G:S_G + 1, :] += jnp.sum(s_g, axis=0, keepdims=True)


def _ffn_bwd(dx2, gu, x1, mod, g_ffn, w_gu, w_down, w_out):
    t_len = x1.shape[0]
    tm = min(TM, t_len)

    def body(dx2_ref, gu_ref, x1_ref, mod_ref, g_ref, wgu_ref, wd_ref, wo_ref,
             dgu_ref, act_ref, dx1_ref, dx1b_ref, dmg_ref, sums_ref, acc, dffn):
        i, j = pl.program_id(0), pl.program_id(1)

        @pl.when((i == 0) & (j == 0))
        def _():
            sums_ref[...] = jnp.zeros_like(sums_ref)

        @pl.when(j == 0)
        def _():
            acc[...] = jnp.zeros_like(acc)
            dffn[...] = (dx2_ref[...] * mod_ref[5:6, :]).astype(BF16)

        for rows in _sub_blocks(tm):
            dact = _dot_nt(dffn[rows, :], wd_ref[pl.ds(pl.multiple_of(j * FB, 16), FB), :])
            gate = gu_ref[0, 0, rows, :].astype(F32)
            up = gu_ref[1, 0, rows, :].astype(F32)
            sg = _sig(gate)
            silu = gate * sg
            act_ref[0, rows, :] = (silu * up).astype(BF16)
            dgate = (dact * up * (sg * (1.0 + gate * (1.0 - sg)))).astype(BF16)
            dup = (dact * silu).astype(BF16)
            dgu_ref[0, 0, rows, :] = dgate
            dgu_ref[1, 0, rows, :] = dup
            acc[rows, :] += _dot_nt(dgate, wgu_ref[0, j]) + _dot_nt(dup, wgu_ref[1, j])

        @pl.when(j == 3)
        def _():
            gt1 = mod_ref[2:3, :]

            def write(rows, dx1):
                dx1_ref[rows, :] = dx1
                dx1b_ref[rows, :] = dx1.astype(BF16)
                dffn[rows, :] = (dx1 * gt1).astype(BF16)

            _norm_bwd_rows(tm, 16, acc, x1_ref, dx2_ref, mod_ref[4:5, :], g_ref[...], sums_ref, write)
            dmg_ref[...] = _dot_nt(dffn[...], wo_ref[...]).astype(BF16)

    return pl.pallas_call(
        body, name="ffn_bwd", grid=(t_len // tm, 4),
        in_specs=[pl.BlockSpec((tm, D), lambda i, j: (i, 0)),
                  pl.BlockSpec((2, 1, tm, FB), lambda i, j: (0, j, i, 0)),
                  pl.BlockSpec((tm, D), lambda i, j: (i, 0)),
                  pl.BlockSpec((8, D), lambda i, j: (0, 0)), pl.BlockSpec((1, D), lambda i, j: (0, 0)),
                  _resident((2, 4, D, FB)), _resident((DFF, D)), _resident((D, D))],
        out_specs=[pl.BlockSpec((2, 1, tm, FB), lambda i, j: (0, j, i, 0)),
                   pl.BlockSpec((1, tm, FB), lambda i, j: (j, i, 0)),
                   pl.BlockSpec((tm, D), lambda i, j: (i, 0)),
                   pl.BlockSpec((tm, D), lambda i, j: (i, 0)),
                   pl.BlockSpec((tm, D), lambda i, j: (i, 0)),
                   pl.BlockSpec((8, D), lambda i, j: (0, 0))],
        out_shape=[jax.ShapeDtypeStruct((2, 4, t_len, FB), BF16), jax.ShapeDtypeStruct((4, t_len, FB), BF16),
                   jax.ShapeDtypeStruct((t_len, D), F32), jax.ShapeDtypeStruct((t_len, D), BF16),
                   jax.ShapeDtypeStruct((t_len, D), BF16), jax.ShapeDtypeStruct((8, D), F32)],
        scratch_shapes=[pltpu.VMEM((tm, D), F32), pltpu.VMEM((tm, D), BF16)],
        compiler_params=_cp("arbitrary", "arbitrary"),
    )(dx2, gu, x1, mod, g_ffn, w_gu, w_down, w_out)


def _my_pos():
    return lax.axis_index("x"), lax.axis_index("y"), lax.axis_index("c")


def _my_index():
    x, y, c = _my_pos()
    return 4 * x + 2 * y + c


def _device_of(b):
    return (b >> 2) & 1, (b >> 1) & 1, b & 1


def _rs_send(src, parts_ref, b, send_sems, recv_sems, local_sem):
    me = _my_index()
    dst = parts_ref.at[me]

    @pl.when(b == me)
    def _():
        pltpu.make_async_copy(src, dst, local_sem).start()

    @pl.when(b != me)
    def _():
        pltpu.make_async_remote_copy(src_ref=src, dst_ref=dst, send_sem=send_sems.at[b], recv_sem=recv_sems.at[me],
                                     device_id=_device_of(b), device_id_type=MESH).start()


def _rs_finish(src_of, parts_ref, send_sems, recv_sems, local_sem):
    me = _my_index()
    for s in range(NDEV):
        @pl.when(s != me)
        def _():
            cp = pltpu.make_async_remote_copy(src_ref=src_of(s), dst_ref=parts_ref.at[s], send_sem=send_sems.at[s],
                                              recv_sem=recv_sems.at[s], device_id=_device_of(s), device_id_type=MESH)
            cp.wait_send()
            cp.wait_recv()

        @pl.when(s == me)
        def _():
            pltpu.make_async_copy(src_of(s), parts_ref.at[s], local_sem).wait()


_RS_SEMS = [pltpu.SemaphoreType.DMA((NDEV,)), pltpu.SemaphoreType.DMA((NDEV,)), pltpu.SemaphoreType.DMA]
_ANY = pl.BlockSpec(memory_space=pl.ANY)


def _xor_order(me, n):
    return (me ^ (n - 1 - jnp.arange(n, dtype=jnp.int32))).astype(jnp.int32)


def _gu_wgrad(h2, dgu, order):
    t_len = h2.shape[0]
    tk = min(TK, t_len)
    nk = t_len // tk

    def body(ord_ref, h_ref, d_ref, parts_ref, acc, stage, send_sems, recv_sems, local_sem):
        p, k = pl.program_id(0), pl.program_id(1)
        part = _dot_tn(h_ref[...], d_ref[0])

        @pl.when(k == 0)
        def _():
            acc[...] = part

        @pl.when(k > 0)
        def _():
            acc[...] += part

        @pl.when(k == nk - 1)
        def _():
            b = ord_ref[p]
            stage[b] = acc[...].astype(BF16)
            _rs_send(stage.at[b], parts_ref, b, send_sems, recv_sems, local_sem)

        @pl.when((p == NDEV - 1) & (k == nk - 1))
        def _():
            _rs_finish(lambda s: stage.at[s], parts_ref, send_sems, recv_sems, local_sem)

    return pl.pallas_call(
        body, name="gu_wgrad",
        grid_spec=pltpu.PrefetchScalarGridSpec(
            num_scalar_prefetch=1, grid=(NDEV, nk),
            in_specs=[pl.BlockSpec((tk, D), lambda p, k, o: (k, 0)),
                      pl.BlockSpec((1, tk, FB), lambda p, k, o: (o[p], k, 0))],
            out_specs=_ANY,
            scratch_shapes=[pltpu.VMEM((D, FB), F32), pltpu.VMEM((NDEV, D, FB), BF16)] + _RS_SEMS),
        out_shape=jax.ShapeDtypeStruct((NDEV, D, FB), BF16),
        compiler_params=_cp("arbitrary", "arbitrary"),
    )(order, h2, dgu.reshape(NDEV, t_len, FB))


def _scaled_wgrad(name, a, dx, w, gate_row, mod, order):
    nb, t_len, kb = a.shape
    tk = min(TK, t_len)
    nk = t_len // tk
    per = NDEV // nb
    rows = kb // per

    def body(ord_ref, a_ref, dx_ref, w_ref, mod_ref, parts_ref, dg_ref, acc, stage, send_sems, recv_sems, local_sem):
        p, k = pl.program_id(0), pl.program_id(1)
        j = ord_ref[p]

        @pl.when((p == 0) & (k == 0))
        def _():
            dg_ref[...] = jnp.zeros_like(dg_ref)

        part = _dot_tn(a_ref[0], dx_ref[...])

        @pl.when(k == 0)
        def _():
            acc[...] = part

        @pl.when(k > 0)
        def _():
            acc[...] += part

        def src_of(blk, q):
            return stage.at[blk, pl.ds(q * rows, rows), :]

        @pl.when(k == nk - 1)
        def _():
            z = acc[...]
            stage[j] = (z * mod_ref[gate_row:gate_row + 1, :]).astype(BF16)
            dg_ref[0:1, :] += jnp.sum(z * w_ref[...].astype(F32), axis=0, keepdims=True)
            for q in range(per):
                _rs_send(src_of(j, q), parts_ref, j * per + q, send_sems, recv_sems, local_sem)

        @pl.when((p == nb - 1) & (k == nk - 1))
        def _():
            _rs_finish(lambda s: src_of(s // per, s % per), parts_ref, send_sems, recv_sems, local_sem)

    return pl.pallas_call(
        body, name=name,
        grid_spec=pltpu.PrefetchScalarGridSpec(
            num_scalar_prefetch=1, grid=(nb, nk),
            in_specs=[pl.BlockSpec((1, tk, kb), lambda p, k, o: (o[p], k, 0)),
                      pl.BlockSpec((tk, D), lambda p, k, o: (k, 0)),
                      pl.BlockSpec((kb, D), lambda p, k, o: (o[p], 0)),
                      pl.BlockSpec((8, D), lambda p, k, o: (0, 0))],
            out_specs=[_ANY, pl.BlockSpec((8, D), lambda p, k, o: (0, 0))],
            scratch_shapes=[pltpu.VMEM((kb, D), F32), pltpu.VMEM((nb, kb, D), BF16)] + _RS_SEMS),
        out_shape=[jax.ShapeDtypeStruct((NDEV, rows, D), BF16), jax.ShapeDtypeStruct((8, D), F32)],
        compiler_params=_cp("arbitrary", "arbitrary"),
    )(order, a, dx, w, mod)


M_WA, M_WB, M_CBIAS, M_BA, M_BX, M_LS = 0, 3, 7, 8, 9, 10


def _mixer_bwd(proj, hl, dmg, prm, wa, wx):
    t_len = proj.shape[1]
    tt = min(TT, t_len)
    nt = t_len // tt
    hb8 = tt // 8

    def rev(i):
        return nt - 1 - i

    def halo(i):
        return jnp.maximum(rev(i) * hb8 - 1, 0)

    def body(proj_ref, ph_ref, hl_ref, hh_ref, dmg_ref, prm_ref, wa_ref, wx_ref,
             dp_ref, sums_ref, gwa_ref, gwx_ref,
             xe, ve, he, u_s, ya_s, rp_s, ip_s, due, dye, drp_s, dip_s, an, gn):
        i = pl.program_id(0)
        t = rev(i)

        @pl.when(i == 0)
        def _():
            sums_ref[...] = jnp.zeros_like(sums_ref)
            gwa_ref[...] = jnp.zeros_like(gwa_ref)
            gwx_ref[...] = jnp.zeros_like(gwx_ref)
            due[tt:tt + 8, :] = jnp.zeros((8, D), F32)
            dye[tt:tt + 8, :] = jnp.zeros((8, D), F32)
            an[...] = jnp.zeros((8, D), F32)
            gn[...] = jnp.zeros((8, D), F32)

        live = (t > 0).astype(F32)
        xe[0:8, :] = ph_ref[3].astype(F32) * live
        ve[0:8, :] = ph_ref[1].astype(F32) * ph_ref[2].astype(F32) * live
        he[0:8, :] = hh_ref[...].astype(F32) * live
        xe[8:8 + tt, :] = proj_ref[3].astype(F32)
        ve[8:8 + tt, :] = proj_ref[1].astype(F32) * proj_ref[2].astype(F32)
        he[8:8 + tt, :] = hl_ref[...].astype(F32)
        u = prm_ref[P_CBIAS:P_CBIAS + 1, :] + prm_ref[P_WB:P_WB + 1, :] * xe[5:5 + tt, :]
        for k in range(1, 4):
            u = u + prm_ref[P_WB + k:P_WB + k + 1, :] * xe[5 + k:5 + k + tt, :]
        u_s[...] = u
        ya = prm_ref[P_WA:P_WA + 1, :] * ve[6:6 + tt, :]
        for k in range(1, 3):
            ya = ya + prm_ref[P_WA + k:P_WA + k + 1, :] * ve[6 + k:6 + k + tt, :]
        ya_s[...] = ya
        ub = u.astype(BF16)
        for h in range(HEADS):
            cs = slice(h * HB, (h + 1) * HB)
            rp_s[:, cs] = _dot(ub[:, cs], wa_ref[h]) + prm_ref[P_BA:P_BA + 1, cs]
            ip_s[:, cs] = _dot(ub[:, cs], wx_ref[h]) + prm_ref[P_BX:P_BX + 1, cs]

        ls_all = _log_sigmoid(prm_ref[P_LAM:P_LAM + 1, :])
        row = lax.broadcasted_iota(jnp.int32, (8, CG), 0)
        nblk = tt // 16

        def blk(ib, carry):
            r0 = pl.multiple_of((nblk - 1 - ib) * 16, 16)
            rows = pl.ds(r0, 16)
            for g in range(D // CG):
                cs = slice(g * CG, (g + 1) * CG)
                ls = ls_all[:, cs]
                dm = dmg_ref[rows, cs].astype(F32)
                cb = proj_ref[0, rows, cs].astype(F32)
                rg = proj_ref[4, rows, cs].astype(F32)
                sga = _sig(proj_ref[5, rows, cs].astype(F32))
                sgb = _sig(proj_ref[6, rows, cs].astype(F32))
                ya0 = ya_s[rows, cs]
                h16 = he[pl.ds(r0 + 8, 16), cs]
                gl, th = _gelu(rg)
                dgl = 0.5 * (1.0 + th) + 0.5 * rg * (1.0 - th * th) * (_GC * (1.0 + 3.0 * 0.044715 * rg * rg))
                y_a = cb * ya0
                y_b = h16 * gl
                dy_a = dm * sga
                dy_b = dm * sgb
                col = lambda s: slice(s * D + g * CG, s * D + (g + 1) * CG)
                dp_ref[rows, col(5)] = (dm * y_a * sga * (1.0 - sga)).astype(BF16)
                dp_ref[rows, col(6)] = (dm * y_b * sgb * (1.0 - sgb)).astype(BF16)
                dp_ref[rows, col(4)] = (dy_b * h16 * dgl).astype(BF16)
                dp_ref[rows, col(0)] = (dy_a * ya0).astype(BF16)
                dye[rows, cs] = dy_a * cb
                dh16 = dy_b * gl

                a_next = an[:, cs]
                g_next = gn[:, cs]
                s_ba = jnp.zeros((8, CG), F32)
                s_bx = jnp.zeros((8, CG), F32)
                s_ls = jnp.zeros((8, CG), F32)
                for sb in (1, 0):
                    rr = r0 + 8 * sb
                    first = (row + (t * tt + rr)) == 0
                    uu = u_s[pl.ds(rr, 8), cs]
                    r, ig, la, a, m2, mult = _lru_gates(rp_s[pl.ds(rr, 8), cs], ip_s[pl.ds(rr, 8), cs], ls, first)
                    ca = jnp.where(row < 7, pltpu.roll(a, 7, 0), a_next)
                    cb_ = dh16[8 * sb:8 * sb + 8, :]
                    for s in (1, 2, 4):
                        a_sh = jnp.where(row < 8 - s, pltpu.roll(ca, 8 - s, 0), 1.0)
                        b_sh = jnp.where(row < 8 - s, pltpu.roll(cb_, 8 - s, 0), 0.0)
                        cb_ = ca * b_sh + cb_
                        ca = ca * a_sh
                    gv = ca * g_next + cb_
                    g_next = jnp.broadcast_to(gv[0:1, :], gv.shape)
                    a_next = jnp.broadcast_to(a[0:1, :], a.shape)
                    hprev = jnp.where(row >= 1, pltpu.roll(he[pl.ds(rr + 8, 8), cs], 1, 0),
                                      pltpu.roll(he[pl.ds(rr, 8), cs], 1, 0))
                    da = gv * hprev
                    dmult = jnp.where(first, 0.0, gv * ig * uu)
                    dla = da * a + jnp.where(m2 > 0.0, dmult * (-(a * a) / mult), 0.0)
                    drp = dla * (LRU_C * ls) * r * (1.0 - r)
                    dip = gv * mult * uu * ig * (1.0 - ig)
                    s_ls = s_ls + dla * (LRU_C * r)
                    s_ba = s_ba + drp
                    s_bx = s_bx + dip
                    drp_s[pl.ds(rr, 8), cs] = drp
                    dip_s[pl.ds(rr, 8), cs] = dip
                    due[pl.ds(rr, 8), cs] = gv * mult * ig
                an[:, cs] = a_next
                gn[:, cs] = g_next
                sums_ref[M_BA:M_BA + 1, cs] += jnp.sum(s_ba, axis=0, keepdims=True)
                sums_ref[M_BX:M_BX + 1, cs] += jnp.sum(s_bx, axis=0, keepdims=True)
                sums_ref[M_LS:M_LS + 1, cs] += jnp.sum(s_ls, axis=0, keepdims=True)
            return carry

        lax.fori_loop(0, nblk, blk, 0)

        drp_b = drp_s[...].astype(BF16)
        dip_b = dip_s[...].astype(BF16)
        for h in range(HEADS):
            cs = slice(h * HB, (h + 1) * HB)
            due[0:tt, cs] += _dot_nt(drp_b[:, cs], wa_ref[h]) + _dot_nt(dip_b[:, cs], wx_ref[h])
            gwa_ref[h] += _dot_tn(ub[:, cs], drp_b[:, cs])
            gwx_ref[h] += _dot_tn(ub[:, cs], dip_b[:, cs])

        du = due[0:tt, :]
        sums_ref[M_CBIAS:M_CBIAS + 1, :] += jnp.sum(du, axis=0, keepdims=True)
        drx = prm_ref[P_WB:P_WB + 1, :] * due[3:3 + tt, :]
        sums_ref[M_WB:M_WB + 1, :] += jnp.sum(du * xe[5:5 + tt, :], axis=0, keepdims=True)
        for k in range(1, 4):
            drx = drx + prm_ref[P_WB + k:P_WB + k + 1, :] * due[3 - k:3 - k + tt, :]
            sums_ref[M_WB + k:M_WB + k + 1, :] += jnp.sum(du * xe[5 + k:5 + k + tt, :], axis=0, keepdims=True)
        dp_ref[:, 3 * D:4 * D] = drx.astype(BF16)
        dya = dye[0:tt, :]
        dv = prm_ref[P_WA:P_WA + 1, :] * dye[2:2 + tt, :]
        sums_ref[M_WA:M_WA + 1, :] += jnp.sum(dya * ve[6:6 + tt, :], axis=0, keepdims=True)
        for k in range(1, 3):
            dv = dv + prm_ref[P_WA + k:P_WA + k + 1, :] * dye[2 - k:2 - k + tt, :]
            sums_ref[M_WA + k:M_WA + k + 1, :] += jnp.sum(dya * ve[6 + k:6 + k + tt, :], axis=0, keepdims=True)
        dp_ref[:, D:2 * D] = (dv * proj_ref[2].astype(F32)).astype(BF16)
        dp_ref[:, 2 * D:3 * D] = (dv * proj_ref[1].astype(F32)).astype(BF16)
        due[tt:tt + 8, :] = due[0:8, :]
        dye[tt:tt + 8, :] = dye[0:8, :]

        @pl.when(i == nt - 1)
        def _():
            sums_ref[M_LS:M_LS + 1, :] = sums_ref[M_LS:M_LS + 1, :] * _sig(-prm_ref[P_LAM:P_LAM + 1, :])

    big = lambda: pltpu.VMEM((tt + 8, D), F32)
    tile = lambda: pltpu.VMEM((tt, D), F32)
    return pl.pallas_call(
        body, name="mixer_bwd", grid=(nt,),
        in_specs=[pl.BlockSpec((7, tt, D), lambda i: (0, rev(i), 0)),
                  pl.BlockSpec((7, 8, D), lambda i: (0, halo(i), 0)),
                  pl.BlockSpec((tt, D), lambda i: (rev(i), 0)),
                  pl.BlockSpec((8, D), lambda i: (halo(i), 0)),
                  pl.BlockSpec((tt, D), lambda i: (rev(i), 0)),
                  pl.BlockSpec((16, D), lambda i: (0, 0)),
                  pl.BlockSpec((HEADS, HB, HB), lambda i: (0, 0, 0)),
                  pl.BlockSpec((HEADS, HB, HB), lambda i: (0, 0, 0))],
        out_specs=[pl.BlockSpec((tt, 7 * D), lambda i: (rev(i), 0)),
                   pl.BlockSpec((16, D), lambda i: (0, 0)),
                   pl.BlockSpec((HEADS, HB, HB), lambda i: (0, 0, 0)),
                   pl.BlockSpec((HEADS, HB, HB), lambda i: (0, 0, 0))],
        out_shape=[jax.ShapeDtypeStruct((t_len, 7 * D), BF16), jax.ShapeDtypeStruct((16, D), F32),
                   jax.ShapeDtypeStruct((HEADS, HB, HB), F32), jax.ShapeDtypeStruct((HEADS, HB, HB), F32)],
        scratch_shapes=[big(), big(), big(), tile(), tile(), tile(), tile(), big(), big(), tile(), tile(),
                        pltpu.VMEM((8, D), F32), pltpu.VMEM((8, D), F32)],
        compiler_params=_cp("arbitrary"),
    )(proj, proj, hl, hl, dmg, prm, wa, wx)


def _in_proj_bwd(dproj, w_in, x, dx1, mod, g_mix):
    t_len = x.shape[0]
    tm = min(TMI, t_len)

    def body(dp_ref, w_ref, x_ref, dx1_ref, mod_ref, g_ref, gx_ref, sums_ref, acc):
        i, s = pl.program_id(0), pl.program_id(1)

        @pl.when((i == 0) & (s == 0))
        def _():
            sums_ref[...] = jnp.zeros_like(sums_ref)

        @pl.when(s == 0)
        def _():
            acc[...] = jnp.zeros_like(acc)

        for rows in _sub_blocks(tm):
            acc[rows, :] += _dot_nt(dp_ref[rows, :], w_ref[:, pl.ds(pl.multiple_of(s * D, D), D)])

        @pl.when(s == 6)
        def _():
            def write(rows, dx):
                gx_ref[rows, :] = dx

            _norm_bwd_rows(tm, 16, acc, x_ref, dx1_ref, mod_ref[1:2, :], g_ref[...], sums_ref, write)

    return pl.pallas_call(
        body, name="in_proj_bwd", grid=(t_len // tm, 7),
        in_specs=[pl.BlockSpec((tm, D), lambda i, s: (i, s)),
                  _resident((D, 7 * D)),
                  pl.BlockSpec((tm, D), lambda i, s: (i, 0)), pl.BlockSpec((tm, D), lambda i, s: (i, 0)),
                  pl.BlockSpec((8, D), lambda i, s: (0, 0)), pl.BlockSpec((1, D), lambda i, s: (0, 0))],
        out_specs=[pl.BlockSpec((tm, D), lambda i, s: (i, 0)), pl.BlockSpec((8, D), lambda i, s: (0, 0))],
        out_shape=[jax.ShapeDtypeStruct((t_len, D), F32), jax.ShapeDtypeStruct((8, D), F32)],
        scratch_shapes=[pltpu.VMEM((tm, D), F32)],
        compiler_params=_cp("arbitrary", "arbitrary"),
    )(dproj, w_in, x, dx1, mod, g_mix)


def _in_wgrad(h, dproj, g_wa, g_wx, order):
    t_len = h.shape[0]
    tk = min(TKI, t_len)
    nk = t_len // tk
    nq = 4
    cw = 7 * D // NDEV
    hr = HB // NDEV

    def body(ord_ref, h_ref, d_ref, ga_ref, gx_ref, parts_ref, pa_ref, px_ref, acc, stage, *sems):
        p, k = pl.program_id(0), pl.program_id(1)
        q = ord_ref[p]

        def head_rows(ref):
            return lambda s: ref.at[:, pl.ds(s * hr, hr), :]

        @pl.when((p == 0) & (k == 0))
        def _():
            for s in range(NDEV):
                _rs_send(head_rows(ga_ref)(s), pa_ref, s, *sems[3:6])
                _rs_send(head_rows(gx_ref)(s), px_ref, s, *sems[6:9])

        part = _dot_tn(h_ref[...], d_ref[...])

        @pl.when(k == 0)
        def _():
            acc[...] = part

        @pl.when(k > 0)
        def _():
            acc[...] += part

        def src_of(blk, r):
            return stage.at[blk, :, pl.ds(r * cw, cw)]

        @pl.when(k == nk - 1)
        def _():
            stage[q] = acc[...].astype(BF16)
            for r in range(2):
                _rs_send(src_of(q, r), parts_ref, q * 2 + r, *sems[0:3])

        @pl.when((p == nq - 1) & (k == nk - 1))
        def _():
            _rs_finish(lambda s: src_of(s // 2, s % 2), parts_ref, *sems[0:3])
            _rs_finish(head_rows(ga_ref), pa_ref, *sems[3:6])
            _rs_finish(head_rows(gx_ref), px_ref, *sems[6:9])

    return pl.pallas_call(
        body, name="in_wgrad",
        grid_spec=pltpu.PrefetchScalarGridSpec(
            num_scalar_prefetch=1, grid=(nq, nk),
            in_specs=[pl.BlockSpec((tk, D), lambda p, k, o: (k, 0)),
                      pl.BlockSpec((tk, 2 * cw), lambda p, k, o: (k, o[p])), _ANY, _ANY],
            out_specs=[_ANY, _ANY, _ANY],
            scratch_shapes=[pltpu.VMEM((D, 2 * cw), F32), pltpu.VMEM((nq, D, 2 * cw), BF16)] + _RS_SEMS * 3),
        out_shape=[jax.ShapeDtypeStruct((NDEV, D, cw), BF16), jax.ShapeDtypeStruct((NDEV, HEADS, hr, HB), F32),
                   jax.ShapeDtypeStruct((NDEV, HEADS, hr, HB), F32)],
        compiler_params=_cp("arbitrary", "arbitrary"),
    )(order, h, dproj, g_wa, g_wx)


def _ada_fwd(c_all, w_ada, b_cols):
    def body(c_ref, w_ref, b_ref, o_ref):
        cv = c_ref[...]
        o_ref[...] = _dot((cv * _sig(cv)).astype(BF16), w_ref[...].astype(BF16)) + b_ref[...]

    return pl.pallas_call(body, name="ada_fwd", out_shape=jax.ShapeDtypeStruct((16, w_ada.shape[1]), F32),
                          compiler_params=_cp())(c_all, w_ada, b_cols)


def _adam_math(w, g, m, v):
    m = ADAM_B1 * m + (1.0 - ADAM_B1) * g
    v = ADAM_B2 * v + (1.0 - ADAM_B2) * (g * g)
    m_hat = m / (1.0 - ADAM_B1 ** ADAM_STEP)
    v_hat = v / (1.0 - ADAM_B2 ** ADAM_STEP)
    delta = -ADAM_LR * (m_hat / (jnp.sqrt(v_hat) + ADAM_EPS) + ADAM_WD * w)
    return delta, m, v


def _ada_bwd(c_all, dmod_cols, w, m, v):
    rb = 256
    n = w.shape[1]
    nrow = c_all.shape[0]

    def body(c_ref, d_ref, w_ref, m_ref, v_ref, g_ref, dl_ref, nm_ref, nv_ref):
        cv = c_ref[...]
        g = _dot_tn((cv * _sig(cv)).astype(BF16), d_ref[...].astype(BF16))
        g_ref[...] = g
        dl_ref[...], nm_ref[...], nv_ref[...] = _adam_math(w_ref[...], g, m_ref[...], v_ref[...])

    blk = pl.BlockSpec((rb, n), lambda i: (i, 0))
    sds = jax.ShapeDtypeStruct(w.shape, F32)
    return pl.pallas_call(
        body, name="ada_bwd", grid=(D // rb,),
        in_specs=[pl.BlockSpec((nrow, rb), lambda i: (0, i)), pl.BlockSpec((nrow, n), lambda i: (0, 0)), blk, blk, blk],
        out_specs=[blk, blk, blk, blk], out_shape=[sds, sds, sds, sds],
        compiler_params=_cp("parallel"),
    )(c_all, dmod_cols, w, m, v)


def _adam(name, parts, w, m, v):
    p, r, c = parts.shape
    rb = r
    for cand in (256, 128, 64, 32, 16, 8):
        if r % cand == 0 and r >= cand:
            rb = cand
            break

    def body(p_ref, w_ref, m_ref, v_ref, g_ref, dl_ref, nm_ref, nv_ref):
        g = p_ref[0].astype(F32)
        for q in range(1, p):
            g = g + p_ref[q].astype(F32)
        g_ref[...] = g
        dl_ref[...], nm_ref[...], nv_ref[...] = _adam_math(w_ref[...], g, m_ref[...], v_ref[...])

    blk = pl.BlockSpec((rb, c), lambda i: (i, 0))
    sds = jax.ShapeDtypeStruct((r, c), F32)
    return pl.pallas_call(
        body, name=name, grid=(r // rb,),
        in_specs=[pl.BlockSpec((p, rb, c), lambda i: (0, i, 0)), blk, blk, blk],
        out_specs=[blk, blk, blk, blk], out_shape=[sds, sds, sds, sds],
        compiler_params=_cp("parallel"),
    )(parts, w, m, v)


def _my_pos():
    return lax.axis_index("x"), lax.axis_index("y"), lax.axis_index("c")


def _all_gather_small(name, v):
    m_per, n = v.shape

    def body(x_ref, out_ref, send_sems, recv_sems, local_sem):
        x, y, c = _my_pos()
        me, sibling = (x, y, c), (x, y, 1 - c)
        chips = [(1 - x, y), (x, 1 - y), (1 - x, 1 - y)]

        def rows(px, py, pc):
            return out_ref.at[pl.ds((4 * px + 2 * py + pc) * m_per, m_per), :]

        def copy(k, block, to, src=None):
            return pltpu.make_async_remote_copy(
                src_ref=rows(*block) if src is None else src, dst_ref=rows(*block),
                send_sem=send_sems.at[k], recv_sem=recv_sems.at[k], device_id=to, device_id_type=MESH)

        mine = pltpu.make_async_copy(x_ref, rows(*me), local_sem)
        mine.start()
        first = [copy(0, me, sibling, src=x_ref)]
        first += [copy(1 + j, me, (*chip, c), src=x_ref) for j, chip in enumerate(chips)]
        for cp in first:
            cp.start()
        passed = [copy(4 + j, (*chip, c), sibling) for j, chip in enumerate(chips)]
        for j, chip in enumerate(chips):
            copy(1 + j, (*chip, c), me).wait_recv()
            passed[j].start()
        copy(0, sibling, me).wait_recv()
        for j, chip in enumerate(chips):
            copy(4 + j, (*chip, 1 - c), me).wait_recv()
        for cp in first + passed:
            cp.wait_send()
        mine.wait()

    return pl.pallas_call(
        body, name=name, out_shape=jax.ShapeDtypeStruct((NDEV * m_per, n), v.dtype),
        in_specs=[pl.BlockSpec(memory_space=pltpu.VMEM)], out_specs=pl.BlockSpec(memory_space=pltpu.VMEM),
        scratch_shapes=[pltpu.SemaphoreType.DMA((7,)), pltpu.SemaphoreType.DMA((7,)), pltpu.SemaphoreType.DMA],
    )(v)


def _blk_cols(n):
    return lambda ref, b: ref.at[:, pl.ds(pl.multiple_of(b * n, 128), n)]


def _blk_rows(n):
    return lambda ref, b: ref.at[pl.ds(pl.multiple_of(b * n, 8), n), :]


def _blk_lead(ref, b):
    return ref.at[b]


def _blk_heads(ref, b):
    return ref.at[:, pl.ds(pl.multiple_of(b * (HB // NDEV), 8), HB // NDEV), :]


def _ag_phases(ins, outs, slicers, send_sems, recv_sems, local_sems):
    na = len(ins)
    x, y, c = _my_pos()
    me, sibling = (x, y, c), (x, y, 1 - c)
    chips = [(1 - x, y), (x, 1 - y), (1 - x, 1 - y)]

    def copy(a, k, block, to, from_shard=False):
        px, py, pc = block
        dst = slicers[a](outs[a], 4 * px + 2 * py + pc)
        return pltpu.make_async_remote_copy(
            src_ref=ins[a] if from_shard else dst, dst_ref=dst,
            send_sem=send_sems.at[a * 7 + k], recv_sem=recv_sems.at[a * 7 + k], device_id=to, device_id_type=MESH)

    def local(a):
        return pltpu.make_async_copy(ins[a], slicers[a](outs[a], 4 * x + 2 * y + c), local_sems.at[a])

    def firsts(a):
        return [copy(a, 0, me, sibling, True)] + [copy(a, 1 + j, me, (*chip, c), True) for j, chip in enumerate(chips)]

    def start():
        for a in range(na):
            local(a).start()
            for cp in firsts(a):
                cp.start()

    def forward():
        for a in range(na):
            for j, chip in enumerate(chips):
                copy(a, 1 + j, (*chip, c), me).wait_recv()
                copy(a, 4 + j, (*chip, c), sibling).start()

    def finish():
        for a in range(na):
            copy(a, 0, sibling, me).wait_recv()
            for j, chip in enumerate(chips):
                copy(a, 4 + j, (*chip, 1 - c), me).wait_recv()
        for a in range(na):
            for cp in firsts(a) + [copy(a, 4 + j, (*chip, c), sibling) for j, chip in enumerate(chips)]:
                cp.wait_send()
            local(a).wait()

    return start, forward, finish


def _ag_sems(na):
    return [pltpu.SemaphoreType.DMA((7 * na,)), pltpu.SemaphoreType.DMA((7 * na,)), pltpu.SemaphoreType.DMA((na,))]


def _all_gather_weights(shards, fulls, slicers):
    na = len(shards)

    def body(*refs):
        start, forward, finish = _ag_phases(refs[:na], refs[na:2 * na], slicers, *refs[2 * na:])
        start()
        forward()
        finish()

    return pl.pallas_call(
        body, name="gather_weights",
        out_shape=[jax.ShapeDtypeStruct(s, sh.dtype) for s, sh in zip(fulls, shards)],
        in_specs=[_ANY] * na, out_specs=[_ANY] * na, scratch_shapes=_ag_sems(na),
    )(*shards)


def _scatter_grads(grads, shard_shapes, slicers):
    na = len(grads)

    def body(*refs):
        ins, outs = refs[:na], refs[na:2 * na]
        send_sems, recv_sems, local_sems = refs[2 * na:]
        x, y, c = _my_pos()
        me = 4 * x + 2 * y + c
        mine, sent = [], []
        for a in range(na):
            cp = pltpu.make_async_copy(slicers[a](ins[a], me), outs[a].at[me], local_sems.at[a])
            cp.start()
            mine.append(cp)
        rel = [(k >> 2 & 1, k >> 1 & 1, k & 1) for k in range(1, NDEV)]
        for a in range(na):
            for k, (fx, fy, fc) in enumerate(rel):
                px, py, pc = x ^ fx, y ^ fy, c ^ fc
                cp = pltpu.make_async_remote_copy(
                    src_ref=slicers[a](ins[a], 4 * px + 2 * py + pc), dst_ref=outs[a].at[me],
                    send_sem=send_sems.at[a * 7 + k], recv_sem=recv_sems.at[a * 7 + k],
                    device_id=(px, py, pc), device_id_type=MESH)
                cp.start()
                sent.append(cp)
        for a in range(na):
            for k, (fx, fy, fc) in enumerate(rel):
                px, py, pc = x ^ fx, y ^ fy, c ^ fc
                src = 4 * px + 2 * py + pc
                pltpu.make_async_remote_copy(
                    src_ref=slicers[a](ins[a], me), dst_ref=outs[a].at[src],
                    send_sem=send_sems.at[a * 7 + k], recv_sem=recv_sems.at[a * 7 + k],
                    device_id=(px, py, pc), device_id_type=MESH).wait_recv()
        for cp in sent:
            cp.wait_send()
        for cp in mine:
            cp.wait()

    any_spec = pl.BlockSpec(memory_space=pl.ANY)
    return pl.pallas_call(
        body, name="scatter_grads",
        out_shape=[jax.ShapeDtypeStruct((NDEV,) + tuple(s), g.dtype) for s, g in zip(shard_shapes, grads)],
        in_specs=[any_spec] * na, out_specs=[any_spec] * na,
        scratch_shapes=[pltpu.SemaphoreType.DMA((7 * na,)), pltpu.SemaphoreType.DMA((7 * na,)),
                        pltpu.SemaphoreType.DMA((na,))],
    )(*grads)


def _local_step(x, target, mod, g_mix, g_ffn, g_fin, prm, w_in, shards):
    fulls = [(HEADS, HB, HB), (HEADS, HB, HB), (D, D), (NDEV, D, FB), (DFF, D)]
    slicers = [_blk_heads, _blk_heads, _blk_rows(D // NDEV), _blk_lead, _blk_rows(DFF // NDEV)]
    proj, h, (wa, wx, w_out, w_gu, w_down) = _in_proj(x, mod, g_mix, w_in, shards, fulls, slicers)
    w_gu = w_gu.reshape(2, 4, D, FB)
    merged, hl = _mixer_fwd(proj, prm, wa, wx)
    x1, h2 = _out_proj(merged, x, mod, g_ffn, w_out)
    gu, dx2, dx2b, loss, d_gfin = _ffn_fwd(h2, x1, target, mod, g_fin, w_gu, w_down)
    dgu, act, dx1, dx1b, dmg, sums2 = _ffn_bwd(dx2, gu, x1, mod, g_ffn, w_gu, w_down, w_out)
    dev_order = _xor_order(_my_index(), NDEV)
    chip_order = _xor_order(_my_index() >> 1, NDEV // 2)
    p_wgu = _gu_wgrad(h2, dgu, dev_order)
    p_wdown, d_gt2 = _scaled_wgrad("down_wgrad", act, dx2b, w_down, 5, mod, chip_order)
    p_wout, d_gt1 = _scaled_wgrad("out_wgrad", merged.reshape(1, *merged.shape), dx1b, w_out, 2, mod,
                                  jnp.zeros((1,), jnp.int32))
    dproj, msums, g_wa, g_wx = _mixer_bwd(proj, hl, dmg, prm, wa, wx)
    p_win, p_wa, p_wx = _in_wgrad(h, dproj, g_wa, g_wx, chip_order)
    grad_x, sums1 = _in_proj_bwd(dproj, w_in, x, dx1, mod, g_mix)
    return dict(loss=loss, grad_x=grad_x, d_gfin=d_gfin, sums1=sums1, sums2=sums2, msums=msums,
                d_gt1=d_gt1[0:1], d_gt2=d_gt2[0:1], p_win=p_win, p_wa=p_wa, p_wx=p_wx, p_wout=p_wout, p_wgu=p_wgu,
                p_wdown=p_wdown)


def kernel(x, c, w_ada, b_ada, g_norm_mix, w_in, conv_a_w, conv_b_w, conv_b_bias, w_rg_a, b_rg_a, w_rg_x, b_rg_x, lru_lambda, w_out, g_norm_ffn, w_gate_up, w_down, g_norm_final, loss_target, m_w_ada, m_b_ada, m_g_norm_mix, m_w_in, m_conv_a_w, m_conv_b_w, m_conv_b_bias, m_w_rg_a, m_b_rg_a, m_w_rg_x, m_b_rg_x, m_lru_lambda, m_w_out, m_g_norm_ffn, m_w_gate_up, m_w_down, m_g_norm_final, v_w_ada, v_b_ada, v_g_norm_mix, v_w_in, v_conv_a_w, v_conv_b_w, v_conv_b_bias, v_w_rg_a, v_b_rg_a, v_w_rg_x, v_b_rg_x, v_lru_lambda, v_w_out, v_g_norm_ffn, v_w_gate_up, v_w_down, v_g_norm_final):
    me = 4 * lax.axis_index("x") + 2 * lax.axis_index("y") + lax.axis_index("c")
    ncol = w_ada.shape[2]
    cw = conv_a_w.shape[2]

    pack0 = jnp.concatenate([c, conv_a_w.reshape(1, 3 * cw), conv_b_w.reshape(1, 4 * cw)], axis=1)
    got0 = _all_gather_small("gather_c", jnp.broadcast_to(pack0, (8, pack0.shape[1])))
    got0 = got0.reshape(NDEV, 8, -1)[:, 0, :]
    c_all = got0[:, :D]
    conv_a = got0[:, D:D + 3 * cw].reshape(NDEV, 3, cw).transpose(1, 0, 2).reshape(3, D)
    conv_b = got0[:, D + 3 * cw:].reshape(NDEV, 4, cw).transpose(1, 0, 2).reshape(4, D)

    b_cols = lax.dynamic_slice_in_dim(b_ada, me * ncol, ncol, axis=1)
    c16 = jnp.concatenate([c_all, jnp.zeros((8, D), F32)], axis=0)
    mod_cols = _ada_fwd(c16, w_ada[0], b_cols)[:NDEV]
    got1 = _all_gather_small("gather_mod", mod_cols).reshape(NDEV, NDEV, ncol)
    mod6 = lax.dynamic_index_in_dim(got1, me, axis=1, keepdims=False).reshape(6, D)
    mod = jnp.concatenate([mod6, jnp.zeros((2, D), F32)], axis=0)

    (w_in_f,) = _all_gather_weights([w_in[0].astype(BF16)], [(D, 7 * D)], [_blk_cols(7 * D // NDEV)])
    shards = [w_rg_a[0].astype(BF16), w_rg_x[0].astype(BF16), w_out[0].astype(BF16), w_gate_up[0].astype(BF16),
              w_down[0].astype(BF16)]

    prm = jnp.concatenate([conv_a, conv_b, conv_b_bias, b_rg_a, b_rg_x, lru_lambda, jnp.zeros((5, D), F32)], axis=0)
    r = _local_step(x[0], loss_target[0], mod, g_norm_mix, g_norm_ffn, g_norm_final.reshape(1, D), prm,
                    w_in_f, shards)

    parts = [r["p_win"], r["p_wa"], r["p_wx"], r["p_wout"], r["p_wgu"], r["p_wdown"]]
    big = {}
    for nm, p, w, m, v in (("w_in", parts[0], w_in, m_w_in, v_w_in), ("w_rg_a", parts[1], w_rg_a, m_w_rg_a, v_w_rg_a),
                           ("w_rg_x", parts[2], w_rg_x, m_w_rg_x, v_w_rg_x), ("w_out", parts[3], w_out, m_w_out, v_w_out),
                           ("w_gate_up", parts[4], w_gate_up, m_w_gate_up, v_w_gate_up),
                           ("w_down", parts[5], w_down, m_w_down, v_w_down)):
        two_d = (-1, w.shape[-1])
        outs = _adam("adam_" + nm, p.reshape((NDEV,) + w.reshape(two_d).shape), w.reshape(two_d), m.reshape(two_d),
                     v.reshape(two_d))
        big[nm] = [o.reshape(w.shape) for o in outs]

    small = jnp.concatenate([
        r["sums1"][S_SH:S_SH + 1], r["sums1"][S_SC:S_SC + 1], r["d_gt1"],
        r["sums2"][S_SH:S_SH + 1], r["sums2"][S_SC:S_SC + 1], r["d_gt2"],
        r["sums1"][S_G:S_G + 1],
        r["msums"][M_CBIAS:M_CBIAS + 1], r["msums"][M_BA:M_BA + 1], r["msums"][M_BX:M_BX + 1],
        r["msums"][M_LS:M_LS + 1],
        r["sums2"][S_G:S_G + 1], r["d_gfin"],
        r["msums"][M_WA:M_WA + 3], r["msums"][M_WB:M_WB + 4],
        jnp.zeros((4, D), F32)], axis=0)
    got2 = _all_gather_small("gather_small", small).reshape(NDEV, 24, D)

    rep_w = jnp.concatenate([b_ada.reshape(6, D), g_norm_mix, conv_b_bias, b_rg_a, b_rg_x, lru_lambda, g_norm_ffn,
                             g_norm_final.reshape(1, D), jnp.zeros((3, D), F32)], axis=0)
    rep_m = jnp.concatenate([m_b_ada.reshape(6, D), m_g_norm_mix, m_conv_b_bias, m_b_rg_a, m_b_rg_x, m_lru_lambda,
                             m_g_norm_ffn, m_g_norm_final.reshape(1, D), jnp.zeros((3, D), F32)], axis=0)
    rep_v = jnp.concatenate([v_b_ada.reshape(6, D), v_g_norm_mix, v_conv_b_bias, v_b_rg_a, v_b_rg_x, v_lru_lambda,
                             v_g_norm_ffn, v_g_norm_final.reshape(1, D), jnp.ones((3, D), F32)], axis=0)
    rep = _adam("adam_rep", got2[:, :16, :], rep_w, rep_m, rep_v)

    conv_parts = lax.dynamic_slice_in_dim(got2[:, 13:21, :], me * cw, cw, axis=2)
    cv_w = jnp.concatenate([conv_a_w[0], conv_b_w[0], jnp.zeros((1, cw), F32)], axis=0)
    cv_m = jnp.concatenate([m_conv_a_w[0], m_conv_b_w[0], jnp.zeros((1, cw), F32)], axis=0)
    cv_v = jnp.concatenate([v_conv_a_w[0], v_conv_b_w[0], jnp.ones((1, cw), F32)], axis=0)
    cvo = _adam("adam_conv", conv_parts, cv_w, cv_m, cv_v)

    dmod_cols = lax.dynamic_slice_in_dim(got2[:, :6, :].reshape(NDEV, 6 * D), me * ncol, ncol, axis=1)
    dmod16 = jnp.concatenate([dmod_cols, jnp.zeros((8, ncol), F32)], axis=0)
    ada = _ada_bwd(c16, dmod16, w_ada[0], m_w_ada[0], v_w_ada[0])

    loss = lax.psum(r["loss"][0, 0], AXES)

    def pick(q):
        one = lambda i: rep[q][i:i + 1]
        return [ada[q].reshape(w_ada.shape), rep[q][0:6].reshape(b_ada.shape), one(6), big["w_in"][q],
                cvo[q][0:3].reshape(conv_a_w.shape), cvo[q][3:7].reshape(conv_b_w.shape), one(7),
                big["w_rg_a"][q], one(8), big["w_rg_x"][q], one(9), one(10), big["w_out"][q], one(11),
                big["w_gate_up"][q], big["w_down"][q], rep[q][12]]

    return (loss, r["grad_x"].reshape(x.shape), *pick(0), *pick(1), *pick(2), *pick(3))
```

```python
import functools
import math

import jax
import jax.numpy as jnp
from jax import lax
from jax.experimental import pallas as pl
from jax.experimental.pallas import tpu as pltpu

F32 = jnp.float32
BF16 = jnp.bfloat16

D = 1024
DFF = 2816
NDEV = 8
HEADS = 4
HB = D // HEADS
FB = DFF // 4
EPS = 1e-6
LRU_C = 8.0
ADAM_LR, ADAM_B1, ADAM_B2, ADAM_EPS, ADAM_WD, ADAM_STEP = 0.001, 0.9, 0.999, 1e-08, 0.01, 10

VMEM_LIMIT = 56 * 1024 * 1024
TM = 512
TMI = 1024
TK = 2048
TKI = 1024
SUB = 256
UNROLL = 4
TT = 256
CG = 256
MESH = pl.DeviceIdType.MESH
AXES = ("x", "y", "c")


def _cp(*sem):
    return pltpu.CompilerParams(dimension_semantics=sem, vmem_limit_bytes=VMEM_LIMIT)


def _sig(x):
    return 1.0 / (1.0 + jnp.exp(-x))


def _log_sigmoid(x):
    z = jnp.exp(-jnp.abs(x))
    u = 1.0 + z
    d = u - 1.0
    l1p = jnp.where(d == 0.0, z, jnp.log(u) * (z / jnp.where(d == 0.0, 1.0, d)))
    return -(jnp.maximum(-x, 0.0) + l1p)


def _neg_expm1(x):
    p = x * (1.0 + x * 0.5 * (1.0 + x * (1.0 / 3.0) * (1.0 + x * 0.25 * (1.0 + x * 0.2 * (1.0 + x * (1.0 / 6.0))))))
    return jnp.where(x > -0.25, -p, 1.0 - jnp.exp(x))


_GC = math.sqrt(2.0 / math.pi)


def _gelu(x):
    t = jnp.tanh(_GC * (x + 0.044715 * x * x * x))
    return 0.5 * x * (1.0 + t), t


def _dot(a, b):
    return jnp.dot(a, b, preferred_element_type=F32)


def _dot_nt(a, b):
    return lax.dot_general(a, b, (((1,), (1,)), ((), ())), preferred_element_type=F32)


def _dot_tn(a, b):
    return lax.dot_general(a, b, (((0,), (0,)), ((), ())), preferred_element_type=F32)


def _resident(shape):
    return pl.BlockSpec(shape, lambda *_: (0,) * len(shape), pipeline_mode=pl.Buffered(1))


def _sub_blocks(n_rows):
    step = min(SUB, n_rows)
    return [slice(r, r + step) for r in range(0, n_rows, step)]


def _fold8(v):
    return v[0:8] + v[8:16]


def _in_proj(x, mod, g_mix, w_in, shards, fulls, slicers):
    t_len = x.shape[0]
    tm = min(TMI, t_len)
    ni = t_len // tm
    na = len(shards)
    rc = 32

    def body(x_ref, mod_ref, g_ref, w_ref, *rest):
        ins, (proj_ref, h_ref), outs = rest[:na], rest[na:na + 2], rest[na + 2:2 * na + 2]
        h_scr = rest[2 * na + 2]
        start, forward, finish = _ag_phases(ins, outs, slicers, *rest[2 * na + 3:])
        i, s = pl.program_id(0), pl.program_id(1)

        @pl.when((i == 0) & (s == 0))
        def _():
            start()

        @pl.when((i == ni // 2) & (s == 0))
        def _():
            forward()

        @pl.when(s == 0)
        def _():
            gs = g_ref[...] * (1.0 + mod_ref[1:2, :])
            sh = mod_ref[0:1, :]

            def chunk(i, carry):
                rows = pl.ds(pl.multiple_of(i * rc, rc), rc)
                xv = x_ref[rows, :]
                r = lax.rsqrt(jnp.mean(xv * xv, axis=-1, keepdims=True) + EPS)
                h = (xv * r * gs + sh).astype(BF16)
                h_scr[rows, :] = h
                h_ref[rows, :] = h
                return carry

            lax.fori_loop(0, tm // rc, chunk, 0, unroll=UNROLL)

        proj_ref[0] = _dot(h_scr[...], w_ref[:, pl.ds(pl.multiple_of(s * D, D), D)]).astype(BF16)

        @pl.when((i == ni - 1) & (s == 6))
        def _():
            finish()

    res = pl.pallas_call(
        body, name="in_proj", grid=(ni, 7),
        in_specs=[pl.BlockSpec((tm, D), lambda i, s: (i, 0)),
                  pl.BlockSpec((8, D), lambda i, s: (0, 0)),
                  pl.BlockSpec((1, D), lambda i, s: (0, 0)),
                  _resident((D, 7 * D))] + [_ANY] * na,
        out_specs=[pl.BlockSpec((1, tm, D), lambda i, s: (s, i, 0)),
                   pl.BlockSpec((tm, D), lambda i, s: (i, 0))] + [_ANY] * na,
        out_shape=[jax.ShapeDtypeStruct((7, t_len, D), BF16), jax.ShapeDtypeStruct((t_len, D), BF16)]
        + [jax.ShapeDtypeStruct(f, sh.dtype) for f, sh in zip(fulls, shards)],
        scratch_shapes=[pltpu.VMEM((tm, D), BF16)] + _ag_sems(na),
        compiler_params=_cp("arbitrary", "arbitrary"),
    )(x, mod, g_mix, w_in, *shards)
    return res[0], res[1], res[2:]


P_WA, P_WB, P_CBIAS, P_BA, P_BX, P_LAM = 0, 3, 7, 8, 9, 10


def _lru_gates(rp, ip, ls, first_row):
    r = _sig(rp)
    ig = _sig(ip)
    la = LRU_C * r * ls
    a = jnp.exp(la)
    m2 = _neg_expm1(2.0 * la)
    mult = jnp.where(first_row, 1.0, jnp.sqrt(jnp.maximum(m2, 0.0)))
    return r, ig, la, a, m2, mult


def _mixer_fwd(proj, prm, wa, wx):
    t_len = proj.shape[1]
    tt = min(TT, t_len)

    def body(proj_ref, prm_ref, wa_ref, wx_ref, mg_ref, hl_ref, xe, ve, hc, u_s, ya_s, rp_s, ip_s):
        t = pl.program_id(0)

        @pl.when(t == 0)
        def _():
            xe[0:8, :] = jnp.zeros((8, D), F32)
            ve[0:8, :] = jnp.zeros((8, D), F32)
            hc[...] = jnp.zeros((8, D), F32)

        xe[8:8 + tt, :] = proj_ref[3].astype(F32)
        ve[8:8 + tt, :] = proj_ref[1].astype(F32) * proj_ref[2].astype(F32)
        u = prm_ref[P_CBIAS:P_CBIAS + 1, :] + prm_ref[P_WB:P_WB + 1, :] * xe[5:5 + tt, :]
        for k in range(1, 4):
            u = u + prm_ref[P_WB + k:P_WB + k + 1, :] * xe[5 + k:5 + k + tt, :]
        u_s[...] = u
        ya = prm_ref[P_WA:P_WA + 1, :] * ve[6:6 + tt, :]
        for k in range(1, 3):
            ya = ya + prm_ref[P_WA + k:P_WA + k + 1, :] * ve[6 + k:6 + k + tt, :]
        ya_s[...] = ya
        xe[0:8, :] = xe[tt:tt + 8, :]
        ve[0:8, :] = ve[tt:tt + 8, :]

        ub = u.astype(BF16)
        for h in range(HEADS):
            cs = slice(h * HB, (h + 1) * HB)
            rp_s[:, cs] = _dot(ub[:, cs], wa_ref[h]) + prm_ref[P_BA:P_BA + 1, cs]
            ip_s[:, cs] = _dot(ub[:, cs], wx_ref[h]) + prm_ref[P_BX:P_BX + 1, cs]

        ls_all = _log_sigmoid(prm_ref[P_LAM:P_LAM + 1, :])
        row = lax.broadcasted_iota(jnp.int32, (8, CG), 0)

        def blk(i, carry):
            r0 = pl.multiple_of(i * 16, 16)
            for g in range(D // CG):
                cs = slice(g * CG, (g + 1) * CG)
                ls = ls_all[:, cs]
                hprev = hc[:, cs]
                hs = []
                for sb in range(2):
                    rr = r0 + 8 * sb
                    first = (row + (t * tt + rr)) == 0
                    _, ig, _, a, _, mult = _lru_gates(rp_s[pl.ds(rr, 8), cs], ip_s[pl.ds(rr, 8), cs], ls, first)
                    b = mult * (ig * u_s[pl.ds(rr, 8), cs])
                    for s in (1, 2, 4):
                        a_sh = jnp.where(row >= s, pltpu.roll(a, s, 0), 1.0)
                        b_sh = jnp.where(row >= s, pltpu.roll(b, s, 0), 0.0)
                        b = a * b_sh + b
                        a = a * a_sh
                    hv = a * hprev + b
                    hprev = jnp.broadcast_to(hv[7:8, :], hv.shape)
                    hs.append(hv)
                hc[:, cs] = hprev
                h16 = jnp.concatenate(hs, axis=0)
                rows = pl.ds(r0, 16)
                gl, _ = _gelu(proj_ref[4, rows, cs].astype(F32))
                y_b = h16 * gl
                y_a = proj_ref[0, rows, cs].astype(F32) * ya_s[rows, cs]
                mg = _sig(proj_ref[5, rows, cs].astype(F32)) * y_a + _sig(proj_ref[6, rows, cs].astype(F32)) * y_b
                mg_ref[rows, cs] = mg.astype(BF16)
                hl_ref[rows, cs] = h16.astype(BF16)
            return carry

        lax.fori_loop(0, tt // 16, blk, 0)

    return pl.pallas_call(
        body, name="mixer_fwd", grid=(t_len // tt,),
        in_specs=[pl.BlockSpec((7, tt, D), lambda t: (0, t, 0)),
                  pl.BlockSpec((16, D), lambda t: (0, 0)),
                  pl.BlockSpec((HEADS, HB, HB), lambda t: (0, 0, 0)),
                  pl.BlockSpec((HEADS, HB, HB), lambda t: (0, 0, 0))],
        out_specs=[pl.BlockSpec((tt, D), lambda t: (t, 0)), pl.BlockSpec((tt, D), lambda t: (t, 0))],
        out_shape=[jax.ShapeDtypeStruct((t_len, D), BF16), jax.ShapeDtypeStruct((t_len, D), BF16)],
        scratch_shapes=[pltpu.VMEM((tt + 8, D), F32), pltpu.VMEM((tt + 8, D), F32), pltpu.VMEM((8, D), F32),
                        pltpu.VMEM((tt, D), F32), pltpu.VMEM((tt, D), F32), pltpu.VMEM((tt, D), F32),
                        pltpu.VMEM((tt, D), F32)],
        compiler_params=_cp("arbitrary"),
    )(proj, prm, wa, wx)


def _out_proj(merged, x, mod, g_ffn, w_out):
    t_len = x.shape[0]
    tm = min(TM, t_len)

    def body(mg_ref, x_ref, mod_ref, g_ref, w_ref, x1_ref, h2_ref):
        gt1 = mod_ref[2:3, :]
        for rows in _sub_blocks(tm):
            x1_ref[rows, :] = x_ref[rows, :] + gt1 * _dot(mg_ref[rows, :], w_ref[...])
        gs = g_ref[...] * (1.0 + mod_ref[4:5, :])
        sh = mod_ref[3:4, :]

        def chunk(c, carry):
            rows = pl.ds(pl.multiple_of(c * 16, 16), 16)
            x1 = x1_ref[rows, :]
            r = lax.rsqrt(jnp.mean(x1 * x1, axis=-1, keepdims=True) + EPS)
            h2_ref[rows, :] = (x1 * r * gs + sh).astype(BF16)
            return carry

        lax.fori_loop(0, tm // 16, chunk, 0, unroll=UNROLL)

    return pl.pallas_call(
        body, name="out_proj", grid=(t_len // tm,),
        in_specs=[pl.BlockSpec((tm, D), lambda i: (i, 0)), pl.BlockSpec((tm, D), lambda i: (i, 0)),
                  pl.BlockSpec((8, D), lambda i: (0, 0)), pl.BlockSpec((1, D), lambda i: (0, 0)),
                  pl.BlockSpec((D, D), lambda i: (0, 0))],
        out_specs=[pl.BlockSpec((tm, D), lambda i: (i, 0)), pl.BlockSpec((tm, D), lambda i: (i, 0))],
        out_shape=[jax.ShapeDtypeStruct((t_len, D), F32), jax.ShapeDtypeStruct((t_len, D), BF16)],
        compiler_params=_cp("parallel"),
    )(merged, x, mod, g_ffn, w_out)


def _ffn_fwd(h2, x1, target, mod, g_fin, w_gu, w_down):
    t_len = x1.shape[0]
    tm = min(TM, t_len)
    assert tm % (16 * UNROLL) == 0

    def body(h2_ref, x1_ref, tg_ref, mod_ref, g_ref, wgu_ref, wd_ref, gu_ref, dx2_ref, dx2b_ref, loss_ref, dg_ref, acc):
        i, j = pl.program_id(0), pl.program_id(1)

        @pl.when((i == 0) & (j == 0))
        def _():
            loss_ref[...] = jnp.zeros_like(loss_ref)
            dg_ref[...] = jnp.zeros_like(dg_ref)

        @pl.when(j == 0)
        def _():
            acc[...] = jnp.zeros_like(acc)

        for rows in _sub_blocks(tm):
            hb = h2_ref[rows, :]
            gate = _dot(hb, wgu_ref[0, j])
            up = _dot(hb, wgu_ref[1, j])
            gu_ref[0, 0, rows, :] = gate.astype(BF16)
            gu_ref[1, 0, rows, :] = up.astype(BF16)
            act = (gate * _sig(gate) * up).astype(BF16)
            acc[rows, :] += _dot(act, wd_ref[pl.ds(pl.multiple_of(j * FB, 16), FB), :])

        @pl.when(j == 3)
        def _():
            gt2 = mod_ref[5:6, :]
            gf = g_ref[...]

            def chunk(c, carry):
                s_loss, s_dg = carry
                for u in range(UNROLL):
                    rows = pl.ds(pl.multiple_of(c * (16 * UNROLL), 16) + 16 * u, 16)
                    x2 = x1_ref[rows, :] + gt2 * acc[rows, :]
                    r = lax.rsqrt(jnp.mean(x2 * x2, axis=-1, keepdims=True) + EPS)
                    xn = x2 * r
                    diff = xn * gf - tg_ref[rows, :]
                    dy = diff * (1.0 / D)
                    dxn = dy * gf
                    dx2 = r * (dxn - xn * jnp.mean(dxn * xn, axis=-1, keepdims=True))
                    dx2_ref[rows, :] = dx2
                    dx2b_ref[rows, :] = dx2.astype(BF16)
                    s_loss, s_dg = s_loss + _fold8(diff * diff), s_dg + _fold8(dy * xn)
                return s_loss, s_dg

            zero = jnp.zeros((8, D), F32)
            s_loss, s_dg = lax.fori_loop(0, tm // (16 * UNROLL), chunk, (zero, zero))
            loss_ref[...] += jnp.sum(s_loss) * (0.5 / D)
            dg_ref[...] += jnp.sum(s_dg, axis=0, keepdims=True)

    return pl.pallas_call(
        body, name="ffn_fwd", grid=(t_len // tm, 4),
        in_specs=[pl.BlockSpec((tm, D), lambda i, j: (i, 0)), pl.BlockSpec((tm, D), lambda i, j: (i, 0)),
                  pl.BlockSpec((tm, D), lambda i, j: (i, 0)), pl.BlockSpec((8, D), lambda i, j: (0, 0)),
                  pl.BlockSpec((1, D), lambda i, j: (0, 0)),
                  _resident((2, 4, D, FB)), _resident((DFF, D))],
        out_specs=[pl.BlockSpec((2, 1, tm, FB), lambda i, j: (0, j, i, 0)),
                   pl.BlockSpec((tm, D), lambda i, j: (i, 0)),
                   pl.BlockSpec((tm, D), lambda i, j: (i, 0)),
                   pl.BlockSpec((1, 128), lambda i, j: (0, 0)),
                   pl.BlockSpec((1, D), lambda i, j: (0, 0))],
        out_shape=[jax.ShapeDtypeStruct((2, 4, t_len, FB), BF16), jax.ShapeDtypeStruct((t_len, D), F32),
                   jax.ShapeDtypeStruct((t_len, D), BF16),
                   jax.ShapeDtypeStruct((1, 128), F32), jax.ShapeDtypeStruct((1, D), F32)],
        scratch_shapes=[pltpu.VMEM((tm, D), F32)],
        compiler_params=_cp("arbitrary", "arbitrary"),
    )(h2, x1, target, mod, g_fin, w_gu, w_down)


S_SH, S_SC, S_G = 0, 1, 2


def _norm_bwd_rows(n_rows, rc, dh_ref, x_ref, dres_ref, scale, gain, sums_ref, write):
    assert n_rows % (rc * UNROLL) == 0
    gs = 1.0 + scale
    fold = _fold8 if rc == 16 else (lambda v: v)

    def chunk(c, carry):
        s_sh, s_sc, s_g = carry
        for u in range(UNROLL):
            rows = pl.ds(pl.multiple_of(c * (rc * UNROLL), rc) + rc * u, rc)
            dh = dh_ref[rows, :]
            xv = x_ref[rows, :]
            r = lax.rsqrt(jnp.mean(xv * xv, axis=-1, keepdims=True) + EPS)
            xn = xv * r
            dhn = dh * gs
            dxn = dhn * gain
            write(rows, dres_ref[rows, :] + r * (dxn - xn * jnp.mean(dxn * xn, axis=-1, keepdims=True)))
            s_sh, s_sc, s_g = s_sh + fold(dh), s_sc + fold(dh * (xn * gain)), s_g + fold(dhn * xn)
        return s_sh, s_sc, s_g

    zero = jnp.zeros((8, D), F32)
    s_sh, s_sc, s_g = lax.fori_loop(0, n_rows // (rc * UNROLL), chunk, (zero, zero, zero))
    sums_ref[S_SH:S_SH + 1, :] += jnp.sum(s_sh, axis=0, keepdims=True)
    sums_ref[S_SC:S_SC + 1, :] += jnp.sum(s_sc, axis=0, keepdims=True)
    sums_ref[S_G:S_G + 1, :] += jnp.sum(s_g, axis=0, keepdims=True)


def _ffn_bwd(dx2, gu, x1, mod, g_ffn, w_gu, w_down, w_out):
    t_len = x1.shape[0]
    tm = min(TM, t_len)

    def body(dx2_ref, gu_ref, x1_ref, mod_ref, g_ref, wgu_ref, wd_ref, wo_ref,
             dgu_ref, act_ref, dx1_ref, dx1b_ref, dmg_ref, sums_ref, acc, dffn):
        i, j = pl.program_id(0), pl.program_id(1)

        @pl.when((i == 0) & (j == 0))
        def _():
            sums_ref[...] = jnp.zeros_like(sums_ref)

        @pl.when(j == 0)
        def _():
            acc[...] = jnp.zeros_like(acc)
            dffn[...] = (dx2_ref[...] * mod_ref[5:6, :]).astype(BF16)

        for rows in _sub_blocks(tm):
            dact = _dot_nt(dffn[rows, :], wd_ref[pl.ds(pl.multiple_of(j * FB, 16), FB), :])
            gate = gu_ref[0, 0, rows, :].astype(F32)
            up = gu_ref[1, 0, rows, :].astype(F32)
            sg = _sig(gate)
            silu = gate * sg
            act_ref[0, rows, :] = (silu * up).astype(BF16)
            dgate = (dact * up * (sg * (1.0 + gate * (1.0 - sg)))).astype(BF16)
            dup = (dact * silu).astype(BF16)
            dgu_ref[0, 0, rows, :] = dgate
            dgu_ref[1, 0, rows, :] = dup
            acc[rows, :] += _dot_nt(dgate, wgu_ref[0, j]) + _dot_nt(dup, wgu_ref[1, j])

        @pl.when(j == 3)
        def _():
            gt1 = mod_ref[2:3, :]

            def write(rows, dx1):
                dx1_ref[rows, :] = dx1
                dx1b_ref[rows, :] = dx1.astype(BF16)
                dffn[rows, :] = (dx1 * gt1).astype(BF16)

            _norm_bwd_rows(tm, 16, acc, x1_ref, dx2_ref, mod_ref[4:5, :], g_ref[...], sums_ref, write)
            dmg_ref[...] = _dot_nt(dffn[...], wo_ref[...]).astype(BF16)

    return pl.pallas_call(
        body, name="ffn_bwd", grid=(t_len // tm, 4),
        in_specs=[pl.BlockSpec((tm, D), lambda i, j: (i, 0)),
                  pl.BlockSpec((2, 1, tm, FB), lambda i, j: (0, j, i, 0)),
                  pl.BlockSpec((tm, D), lambda i, j: (i, 0)),
                  pl.BlockSpec((8, D), lambda i, j: (0, 0)), pl.BlockSpec((1, D), lambda i, j: (0, 0)),
                  _resident((2, 4, D, FB)), _resident((DFF, D)), _resident((D, D))],
        out_specs=[pl.BlockSpec((2, 1, tm, FB), lambda i, j: (0, j, i, 0)),
                   pl.BlockSpec((1, tm, FB), lambda i, j: (j, i, 0)),
                   pl.BlockSpec((tm, D), lambda i, j: (i, 0)),
                   pl.BlockSpec((tm, D), lambda i, j: (i, 0)),
                   pl.BlockSpec((tm, D), lambda i, j: (i, 0)),
                   pl.BlockSpec((8, D), lambda i, j: (0, 0))],
        out_shape=[jax.ShapeDtypeStruct((2, 4, t_len, FB), BF16), jax.ShapeDtypeStruct((4, t_len, FB), BF16),
                   jax.ShapeDtypeStruct((t_len, D), F32), jax.ShapeDtypeStruct((t_len, D), BF16),
                   jax.ShapeDtypeStruct((t_len, D), BF16), jax.ShapeDtypeStruct((8, D), F32)],
        scratch_shapes=[pltpu.VMEM((tm, D), F32), pltpu.VMEM((tm, D), BF16)],
        compiler_params=_cp("arbitrary", "arbitrary"),
    )(dx2, gu, x1, mod, g_ffn, w_gu, w_down, w_out)


def _my_pos():
    return lax.axis_index("x"), lax.axis_index("y"), lax.axis_index("c")


def _my_index():
    x, y, c = _my_pos()
    return 4 * x + 2 * y + c


def _device_of(b):
    return (b >> 2) & 1, (b >> 1) & 1, b & 1


def _rs_send(src, parts_ref, b, send_sems, recv_sems, local_sem):
    me = _my_index()
    dst = parts_ref.at[me]

    @pl.when(b == me)
    def _():
        pltpu.make_async_copy(src, dst, local_sem).start()

    @pl.when(b != me)
    def _():
        pltpu.make_async_remote_copy(src_ref=src, dst_ref=dst, send_sem=send_sems.at[b], recv_sem=recv_sems.at[me],
                                     device_id=_device_of(b), device_id_type=MESH).start()


def _rs_finish(src_of, parts_ref, send_sems, recv_sems, local_sem):
    me = _my_index()
    for s in range(NDEV):
        @pl.when(s != me)
        def _():
            cp = pltpu.make_async_remote_copy(src_ref=src_of(s), dst_ref=parts_ref.at[s], send_sem=send_sems.at[s],
                                              recv_sem=recv_sems.at[s], device_id=_device_of(s), device_id_type=MESH)
            cp.wait_send()
            cp.wait_recv()

        @pl.when(s == me)
        def _():
            pltpu.make_async_copy(src_of(s), parts_ref.at[s], local_sem).wait()


_RS_SEMS = [pltpu.SemaphoreType.DMA((NDEV,)), pltpu.SemaphoreType.DMA((NDEV,)), pltpu.SemaphoreType.DMA]
_ANY = pl.BlockSpec(memory_space=pl.ANY)


def _xor_order(me, n):
    return (me ^ (n - 1 - jnp.arange(n, dtype=jnp.int32))).astype(jnp.int32)


NCHIP = NDEV // 2


def _rs2_scratch(half_shape):
    blocks = lambda *lead: pltpu.VMEM(lead + tuple(half_shape), BF16)
    return [blocks(NCHIP, 2), blocks(NCHIP)] + [pltpu.SemaphoreType.DMA((NCHIP,))] * 4 + [pltpu.SemaphoreType.DMA]


def _rs2_to_sibling(q, rs):
    stage, from_sib, d_send, d_recv = rs[:4]
    x, y, c = _my_pos()
    pltpu.make_async_remote_copy(src_ref=stage.at[q, 1 - c], dst_ref=from_sib.at[q], send_sem=d_send.at[q],
                                 recv_sem=d_recv.at[q], device_id=(x, y, 1 - c), device_id_type=MESH).start()


def _rs2_forward(q, parts_ref, rs):
    stage, chip_sum, d_send, d_recv, i_send, i_recv, local_sem = rs
    x, y, c = _my_pos()
    my_chip = 2 * x + y
    pltpu.make_async_remote_copy(src_ref=stage.at[q, c], dst_ref=chip_sum.at[q], send_sem=d_send.at[q],
                                 recv_sem=d_recv.at[q], device_id=(x, y, 1 - c), device_id_type=MESH).wait_recv()
    chip_sum[q] = (stage[q, c].astype(F32) + chip_sum[q].astype(F32)).astype(BF16)

    @pl.when(q == my_chip)
    def _():
        pltpu.make_async_copy(chip_sum.at[q], parts_ref.at[my_chip], local_sem).start()

    @pl.when(q != my_chip)
    def _():
        pltpu.make_async_remote_copy(src_ref=chip_sum.at[q], dst_ref=parts_ref.at[my_chip], send_sem=i_send.at[q],
                                     recv_sem=i_recv.at[my_chip], device_id=((q >> 1) & 1, q & 1, c),
                                     device_id_type=MESH).start()


def _rs2_finish(parts_ref, rs):
    stage, chip_sum, d_send, d_recv, i_send, i_recv, local_sem = rs
    x, y, c = _my_pos()
    my_chip = 2 * x + y
    for q in range(NCHIP):
        pltpu.make_async_remote_copy(src_ref=stage.at[q, 1 - c], dst_ref=chip_sum.at[q], send_sem=d_send.at[q],
                                     recv_sem=d_recv.at[q], device_id=(x, y, 1 - c), device_id_type=MESH).wait_send()

        @pl.when(q != my_chip)
        def _():
            cp = pltpu.make_async_remote_copy(src_ref=chip_sum.at[q], dst_ref=parts_ref.at[q], send_sem=i_send.at[q],
                                              recv_sem=i_recv.at[q], device_id=((q >> 1) & 1, q & 1, c),
                                              device_id_type=MESH)
            cp.wait_send()
            cp.wait_recv()

        @pl.when(q == my_chip)
        def _():
            pltpu.make_async_copy(chip_sum.at[q], parts_ref.at[q], local_sem).wait()


def _gu_wgrad(h2, dgu, order):
    t_len = h2.shape[0]
    tk = min(TK, t_len)
    nk = t_len // tk

    def body(ord_ref, h_ref, d_ref, parts_ref, acc, *rs):
        p, k = pl.program_id(0), pl.program_id(1)

        @pl.when(k == 0)
        def _():
            acc[...] = jnp.zeros_like(acc)

        hb = h_ref[...]
        for half in range(2):
            acc[half] += _dot_tn(hb, d_ref[0, half])

        @pl.when(k == nk - 1)
        def _():
            q = ord_ref[p]
            rs[0][q] = acc[...].astype(BF16)
            _rs2_to_sibling(q, rs)

        @pl.when((k == nk - 1) & (p > 0))
        def _():
            _rs2_forward(ord_ref[p - 1], parts_ref, rs)

        @pl.when((p == NCHIP - 1) & (k == nk - 1))
        def _():
            _rs2_forward(ord_ref[p], parts_ref, rs)
            _rs2_finish(parts_ref, rs)

    return pl.pallas_call(
        body, name="gu_wgrad",
        grid_spec=pltpu.PrefetchScalarGridSpec(
            num_scalar_prefetch=1, grid=(NCHIP, nk),
            in_specs=[pl.BlockSpec((tk, D), lambda p, k, o: (k, 0)),
                      pl.BlockSpec((1, 2, tk, FB), lambda p, k, o: (o[p], 0, k, 0))],
            out_specs=_ANY,
            scratch_shapes=[pltpu.VMEM((2, D, FB), F32)] + _rs2_scratch((D, FB))),
        out_shape=jax.ShapeDtypeStruct((NCHIP, D, FB), BF16),
        compiler_params=_cp("arbitrary", "arbitrary"),
    )(order, h2, dgu.reshape(NCHIP, 2, t_len, FB))


def _scaled_wgrad(name, a, dx, w, gate_row, mod, order):
    nb, t_len, kb = a.shape
    tk = min(TK, t_len)
    nk = t_len // tk
    cpb = NCHIP // nb
    rows = kb // (2 * cpb)

    def body(ord_ref, a_ref, dx_ref, w_ref, mod_ref, parts_ref, dg_ref, acc, *rs):
        p, k = pl.program_id(0), pl.program_id(1)
        j = ord_ref[p]

        @pl.when((p == 0) & (k == 0))
        def _():
            dg_ref[...] = jnp.zeros_like(dg_ref)

        @pl.when(k == 0)
        def _():
            acc[...] = jnp.zeros_like(acc)

        acc[...] += _dot_tn(a_ref[0], dx_ref[...])

        @pl.when(k == nk - 1)
        def _():
            z = acc[...]
            zg = (z * mod_ref[gate_row:gate_row + 1, :]).astype(BF16)
            dg_ref[0:1, :] += jnp.sum(z * w_ref[...].astype(F32), axis=0, keepdims=True)
            for i in range(cpb):
                q = j * cpb + i
                for half in range(2):
                    rs[0][q, half] = zg[(2 * i + half) * rows:(2 * i + half + 1) * rows]
                _rs2_to_sibling(q, rs)

        if cpb == 1:
            @pl.when((k == nk - 1) & (p > 0))
            def _():
                _rs2_forward(ord_ref[p - 1], parts_ref, rs)

        @pl.when((p == nb - 1) & (k == nk - 1))
        def _():
            for i in range(cpb):
                _rs2_forward(j * cpb + i, parts_ref, rs)
            _rs2_finish(parts_ref, rs)

    return pl.pallas_call(
        body, name=name,
        grid_spec=pltpu.PrefetchScalarGridSpec(
            num_scalar_prefetch=1, grid=(nb, nk),
            in_specs=[pl.BlockSpec((1, tk, kb), lambda p, k, o: (o[p], k, 0)),
                      pl.BlockSpec((tk, D), lambda p, k, o: (k, 0)),
                      pl.BlockSpec((kb, D), lambda p, k, o: (o[p], 0)),
                      pl.BlockSpec((8, D), lambda p, k, o: (0, 0))],
            out_specs=[_ANY, pl.BlockSpec((8, D), lambda p, k, o: (0, 0))],
            scratch_shapes=[pltpu.VMEM((kb, D), F32)] + _rs2_scratch((rows, D))),
        out_shape=[jax.ShapeDtypeStruct((NCHIP, rows, D), BF16), jax.ShapeDtypeStruct((8, D), F32)],
        compiler_params=_cp("arbitrary", "arbitrary"),
    )(order, a, dx, w, mod)


M_WA, M_WB, M_CBIAS, M_BA, M_BX, M_LS = 0, 3, 7, 8, 9, 10


def _mixer_bwd(proj, hl, dmg, prm, wa, wx):
    t_len = proj.shape[1]
    tt = min(TT, t_len)
    nt = t_len // tt
    hb8 = tt // 8

    def rev(i):
        return nt - 1 - i

    def halo(i):
        return jnp.maximum(rev(i) * hb8 - 1, 0)

    def body(proj_ref, ph_ref, hl_ref, hh_ref, dmg_ref, prm_ref, wa_ref, wx_ref,
             dp_ref, sums_ref, gwa_ref, gwx_ref,
             xe, ve, he, u_s, ya_s, rp_s, ip_s, due, dye, drp_s, dip_s, an, gn):
        i = pl.program_id(0)
        t = rev(i)

        @pl.when(i == 0)
        def _():
            sums_ref[...] = jnp.zeros_like(sums_ref)
            gwa_ref[...] = jnp.zeros_like(gwa_ref)
            gwx_ref[...] = jnp.zeros_like(gwx_ref)
            due[tt:tt + 8, :] = jnp.zeros((8, D), F32)
            dye[tt:tt + 8, :] = jnp.zeros((8, D), F32)
            an[...] = jnp.zeros((8, D), F32)
            gn[...] = jnp.zeros((8, D), F32)

        live = (t > 0).astype(F32)
        xe[0:8, :] = ph_ref[3].astype(F32) * live
        ve[0:8, :] = ph_ref[1].astype(F32) * ph_ref[2].astype(F32) * live
        he[0:8, :] = hh_ref[...].astype(F32) * live
        xe[8:8 + tt, :] = proj_ref[3].astype(F32)
        ve[8:8 + tt, :] = proj_ref[1].astype(F32) * proj_ref[2].astype(F32)
        he[8:8 + tt, :] = hl_ref[...].astype(F32)
        u = prm_ref[P_CBIAS:P_CBIAS + 1, :] + prm_ref[P_WB:P_WB + 1, :] * xe[5:5 + tt, :]
        for k in range(1, 4):
            u = u + prm_ref[P_WB + k:P_WB + k + 1, :] * xe[5 + k:5 + k + tt, :]
        u_s[...] = u
        ya = prm_ref[P_WA:P_WA + 1, :] * ve[6:6 + tt, :]
        for k in range(1, 3):
            ya = ya + prm_ref[P_WA + k:P_WA + k + 1, :] * ve[6 + k:6 + k + tt, :]
        ya_s[...] = ya
        ub = u.astype(BF16)
        for h in range(HEADS):
            cs = slice(h * HB, (h + 1) * HB)
            rp_s[:, cs] = _dot(ub[:, cs], wa_ref[h]) + prm_ref[P_BA:P_BA + 1, cs]
            ip_s[:, cs] = _dot(ub[:, cs], wx_ref[h]) + prm_ref[P_BX:P_BX + 1, cs]

        ls_all = _log_sigmoid(prm_ref[P_LAM:P_LAM + 1, :])
        row = lax.broadcasted_iota(jnp.int32, (8, CG), 0)
        nblk = tt // 16

        def blk(ib, carry):
            r0 = pl.multiple_of((nblk - 1 - ib) * 16, 16)
            rows = pl.ds(r0, 16)
            for g in range(D // CG):
                cs = slice(g * CG, (g + 1) * CG)
                ls = ls_all[:, cs]
                dm = dmg_ref[rows, cs].astype(F32)
                cb = proj_ref[0, rows, cs].astype(F32)
                rg = proj_ref[4, rows, cs].astype(F32)
                sga = _sig(proj_ref[5, rows, cs].astype(F32))
                sgb = _sig(proj_ref[6, rows, cs].astype(F32))
                ya0 = ya_s[rows, cs]
                h16 = he[pl.ds(r0 + 8, 16), cs]
                gl, th = _gelu(rg)
                dgl = 0.5 * (1.0 + th) + 0.5 * rg * (1.0 - th * th) * (_GC * (1.0 + 3.0 * 0.044715 * rg * rg))
                y_a = cb * ya0
                y_b = h16 * gl
                dy_a = dm * sga
                dy_b = dm * sgb
                col = lambda s: slice(s * D + g * CG, s * D + (g + 1) * CG)
                dp_ref[rows, col(5)] = (dm * y_a * sga * (1.0 - sga)).astype(BF16)
                dp_ref[rows, col(6)] = (dm * y_b * sgb * (1.0 - sgb)).astype(BF16)
                dp_ref[rows, col(4)] = (dy_b * h16 * dgl).astype(BF16)
                dp_ref[rows, col(0)] = (dy_a * ya0).astype(BF16)
                dye[rows, cs] = dy_a * cb
                dh16 = dy_b * gl

                a_next = an[:, cs]
                g_next = gn[:, cs]
                s_ba = jnp.zeros((8, CG), F32)
                s_bx = jnp.zeros((8, CG), F32)
                s_ls = jnp.zeros((8, CG), F32)
                for sb in (1, 0):
                    rr = r0 + 8 * sb
                    first = (row + (t * tt + rr)) == 0
                    uu = u_s[pl.ds(rr, 8), cs]
                    r, ig, la, a, m2, mult = _lru_gates(rp_s[pl.ds(rr, 8), cs], ip_s[pl.ds(rr, 8), cs], ls, first)
                    ca = jnp.where(row < 7, pltpu.roll(a, 7, 0), a_next)
                    cb_ = dh16[8 * sb:8 * sb + 8, :]
                    for s in (1, 2, 4):
                        a_sh = jnp.where(row < 8 - s, pltpu.roll(ca, 8 - s, 0), 1.0)
                        b_sh = jnp.where(row < 8 - s, pltpu.roll(cb_, 8 - s, 0), 0.0)
                        cb_ = ca * b_sh + cb_
                        ca = ca * a_sh
                    gv = ca * g_next + cb_
                    g_next = jnp.broadcast_to(gv[0:1, :], gv.shape)
                    a_next = jnp.broadcast_to(a[0:1, :], a.shape)
                    hprev = jnp.where(row >= 1, pltpu.roll(he[pl.ds(rr + 8, 8), cs], 1, 0),
                                      pltpu.roll(he[pl.ds(rr, 8), cs], 1, 0))
                    da = gv * hprev
                    dmult = jnp.where(first, 0.0, gv * ig * uu)
                    dla = da * a + jnp.where(m2 > 0.0, dmult * (-(a * a) / mult), 0.0)
                    drp = dla * (LRU_C * ls) * r * (1.0 - r)
                    dip = gv * mult * uu * ig * (1.0 - ig)
                    s_ls = s_ls + dla * (LRU_C * r)
                    s_ba = s_ba + drp
                    s_bx = s_bx + dip
                    drp_s[pl.ds(rr, 8), cs] = drp
                    dip_s[pl.ds(rr, 8), cs] = dip
                    due[pl.ds(rr, 8), cs] = gv * mult * ig
                an[:, cs] = a_next
                gn[:, cs] = g_next
                sums_ref[M_BA:M_BA + 1, cs] += jnp.sum(s_ba, axis=0, keepdims=True)
                sums_ref[M_BX:M_BX + 1, cs] += jnp.sum(s_bx, axis=0, keepdims=True)
                sums_ref[M_LS:M_LS + 1, cs] += jnp.sum(s_ls, axis=0, keepdims=True)
            return carry

        lax.fori_loop(0, nblk, blk, 0)

        drp_b = drp_s[...].astype(BF16)
        dip_b = dip_s[...].astype(BF16)
        for h in range(HEADS):
            cs = slice(h * HB, (h + 1) * HB)
            due[0:tt, cs] += _dot_nt(drp_b[:, cs], wa_ref[h]) + _dot_nt(dip_b[:, cs], wx_ref[h])
            gwa_ref[h] += _dot_tn(ub[:, cs], drp_b[:, cs])
            gwx_ref[h] += _dot_tn(ub[:, cs], dip_b[:, cs])

        du = due[0:tt, :]
        sums_ref[M_CBIAS:M_CBIAS + 1, :] += jnp.sum(du, axis=0, keepdims=True)
        drx = prm_ref[P_WB:P_WB + 1, :] * due[3:3 + tt, :]
        sums_ref[M_WB:M_WB + 1, :] += jnp.sum(du * xe[5:5 + tt, :], axis=0, keepdims=True)
        for k in range(1, 4):
            drx = drx + prm_ref[P_WB + k:P_WB + k + 1, :] * due[3 - k:3 - k + tt, :]
            sums_ref[M_WB + k:M_WB + k + 1, :] += jnp.sum(du * xe[5 + k:5 + k + tt, :], axis=0, keepdims=True)
        dp_ref[:, 3 * D:4 * D] = drx.astype(BF16)
        dya = dye[0:tt, :]
        dv = prm_ref[P_WA:P_WA + 1, :] * dye[2:2 + tt, :]
        sums_ref[M_WA:M_WA + 1, :] += jnp.sum(dya * ve[6:6 + tt, :], axis=0, keepdims=True)
        for k in range(1, 3):
            dv = dv + prm_ref[P_WA + k:P_WA + k + 1, :] * dye[2 - k:2 - k + tt, :]
            sums_ref[M_WA + k:M_WA + k + 1, :] += jnp.sum(dya * ve[6 + k:6 + k + tt, :], axis=0, keepdims=True)
        dp_ref[:, D:2 * D] = (dv * proj_ref[2].astype(F32)).astype(BF16)
        dp_ref[:, 2 * D:3 * D] = (dv * proj_ref[1].astype(F32)).astype(BF16)
        due[tt:tt + 8, :] = due[0:8, :]
        dye[tt:tt + 8, :] = dye[0:8, :]

        @pl.when(i == nt - 1)
        def _():
            sums_ref[M_LS:M_LS + 1, :] = sums_ref[M_LS:M_LS + 1, :] * _sig(-prm_ref[P_LAM:P_LAM + 1, :])

    big = lambda: pltpu.VMEM((tt + 8, D), F32)
    tile = lambda: pltpu.VMEM((tt, D), F32)
    return pl.pallas_call(
        body, name="mixer_bwd", grid=(nt,),
        in_specs=[pl.BlockSpec((7, tt, D), lambda i: (0, rev(i), 0)),
                  pl.BlockSpec((7, 8, D), lambda i: (0, halo(i), 0)),
                  pl.BlockSpec((tt, D), lambda i: (rev(i), 0)),
                  pl.BlockSpec((8, D), lambda i: (halo(i), 0)),
                  pl.BlockSpec((tt, D), lambda i: (rev(i), 0)),
                  pl.BlockSpec((16, D), lambda i: (0, 0)),
                  pl.BlockSpec((HEADS, HB, HB), lambda i: (0, 0, 0)),
                  pl.BlockSpec((HEADS, HB, HB), lambda i: (0, 0, 0))],
        out_specs=[pl.BlockSpec((tt, 7 * D), lambda i: (rev(i), 0)),
                   pl.BlockSpec((16, D), lambda i: (0, 0)),
                   pl.BlockSpec((HEADS, HB, HB), lambda i: (0, 0, 0)),
                   pl.BlockSpec((HEADS, HB, HB), lambda i: (0, 0, 0))],
        out_shape=[jax.ShapeDtypeStruct((t_len, 7 * D), BF16), jax.ShapeDtypeStruct((16, D), F32),
                   jax.ShapeDtypeStruct((HEADS, HB, HB), F32), jax.ShapeDtypeStruct((HEADS, HB, HB), F32)],
        scratch_shapes=[big(), big(), big(), tile(), tile(), tile(), tile(), big(), big(), tile(), tile(),
                        pltpu.VMEM((8, D), F32), pltpu.VMEM((8, D), F32)],
        compiler_params=_cp("arbitrary"),
    )(proj, proj, hl, hl, dmg, prm, wa, wx)


def _in_proj_bwd(dproj, w_in, x, dx1, mod, g_mix):
    t_len = x.shape[0]
    tm = min(TMI, t_len)

    def body(dp_ref, w_ref, x_ref, dx1_ref, mod_ref, g_ref, gx_ref, sums_ref, acc):
        i, s = pl.program_id(0), pl.program_id(1)

        @pl.when((i == 0) & (s == 0))
        def _():
            sums_ref[...] = jnp.zeros_like(sums_ref)

        @pl.when(s == 0)
        def _():
            acc[...] = jnp.zeros_like(acc)

        for rows in _sub_blocks(tm):
            acc[rows, :] += _dot_nt(dp_ref[rows, :], w_ref[:, pl.ds(pl.multiple_of(s * D, D), D)])

        @pl.when(s == 6)
        def _():
            def write(rows, dx):
                gx_ref[rows, :] = dx

            _norm_bwd_rows(tm, 16, acc, x_ref, dx1_ref, mod_ref[1:2, :], g_ref[...], sums_ref, write)

    return pl.pallas_call(
        body, name="in_proj_bwd", grid=(t_len // tm, 7),
        in_specs=[pl.BlockSpec((tm, D), lambda i, s: (i, s)),
                  _resident((D, 7 * D)),
                  pl.BlockSpec((tm, D), lambda i, s: (i, 0)), pl.BlockSpec((tm, D), lambda i, s: (i, 0)),
                  pl.BlockSpec((8, D), lambda i, s: (0, 0)), pl.BlockSpec((1, D), lambda i, s: (0, 0))],
        out_specs=[pl.BlockSpec((tm, D), lambda i, s: (i, 0)), pl.BlockSpec((8, D), lambda i, s: (0, 0))],
        out_shape=[jax.ShapeDtypeStruct((t_len, D), F32), jax.ShapeDtypeStruct((8, D), F32)],
        scratch_shapes=[pltpu.VMEM((tm, D), F32)],
        compiler_params=_cp("arbitrary", "arbitrary"),
    )(dproj, w_in, x, dx1, mod, g_mix)


def _in_wgrad(h, dproj, g_wa, g_wx, order):
    t_len = h.shape[0]
    tk = min(TKI, t_len)
    nk = t_len // tk
    cw = 7 * D // NDEV
    hr = HB // NDEV

    def body(ord_ref, h_ref, d_ref, ga_ref, gx_ref, parts_ref, pa_ref, px_ref, acc, *scr):
        rs, sems = scr[:-6], scr[-6:]
        p, k = pl.program_id(0), pl.program_id(1)

        def head_rows(ref):
            return lambda s: ref.at[:, pl.ds(s * hr, hr), :]

        @pl.when((p == 0) & (k == 0))
        def _():
            for s in range(NDEV):
                _rs_send(head_rows(ga_ref)(s), pa_ref, s, *sems[0:3])
                _rs_send(head_rows(gx_ref)(s), px_ref, s, *sems[3:6])

        @pl.when(k == 0)
        def _():
            acc[...] = jnp.zeros_like(acc)

        acc[...] += _dot_tn(h_ref[...], d_ref[...])

        @pl.when(k == nk - 1)
        def _():
            q = ord_ref[p]
            for half in range(2):
                rs[0][q, half] = acc[:, half * cw:(half + 1) * cw].astype(BF16)
            _rs2_to_sibling(q, rs)

        @pl.when((k == nk - 1) & (p > 0))
        def _():
            _rs2_forward(ord_ref[p - 1], parts_ref, rs)

        @pl.when((p == NCHIP - 1) & (k == nk - 1))
        def _():
            _rs2_forward(ord_ref[p], parts_ref, rs)
            _rs2_finish(parts_ref, rs)
            _rs_finish(head_rows(ga_ref), pa_ref, *sems[0:3])
            _rs_finish(head_rows(gx_ref), px_ref, *sems[3:6])

    return pl.pallas_call(
        body, name="in_wgrad",
        grid_spec=pltpu.PrefetchScalarGridSpec(
            num_scalar_prefetch=1, grid=(NCHIP, nk),
            in_specs=[pl.BlockSpec((tk, D), lambda p, k, o: (k, 0)),
                      pl.BlockSpec((tk, 2 * cw), lambda p, k, o: (k, o[p])), _ANY, _ANY],
            out_specs=[_ANY, _ANY, _ANY],
            scratch_shapes=[pltpu.VMEM((D, 2 * cw), F32)] + _rs2_scratch((D, cw)) + _RS_SEMS * 2),
        out_shape=[jax.ShapeDtypeStruct((NCHIP, D, cw), BF16), jax.ShapeDtypeStruct((NDEV, HEADS, hr, HB), F32),
                   jax.ShapeDtypeStruct((NDEV, HEADS, hr, HB), F32)],
        compiler_params=_cp("arbitrary", "arbitrary"),
    )(order, h, dproj, g_wa, g_wx)


def _ada_fwd(c_all, w_ada, b_cols):
    def body(c_ref, w_ref, b_ref, o_ref):
        cv = c_ref[...]
        o_ref[...] = _dot((cv * _sig(cv)).astype(BF16), w_ref[...].astype(BF16)) + b_ref[...]

    return pl.pallas_call(body, name="ada_fwd", out_shape=jax.ShapeDtypeStruct((16, w_ada.shape[1]), F32),
                          compiler_params=_cp())(c_all, w_ada, b_cols)


def _adam_math(w, g, m, v):
    m = ADAM_B1 * m + (1.0 - ADAM_B1) * g
    v = ADAM_B2 * v + (1.0 - ADAM_B2) * (g * g)
    m_hat = m / (1.0 - ADAM_B1 ** ADAM_STEP)
    v_hat = v / (1.0 - ADAM_B2 ** ADAM_STEP)
    delta = -ADAM_LR * (m_hat / (jnp.sqrt(v_hat) + ADAM_EPS) + ADAM_WD * w)
    return delta, m, v


def _ada_bwd(c_all, dmod_cols, w, m, v):
    rb = 256
    n = w.shape[1]
    nrow = c_all.shape[0]

    def body(c_ref, d_ref, w_ref, m_ref, v_ref, g_ref, dl_ref, nm_ref, nv_ref):
        cv = c_ref[...]
        g = _dot_tn((cv * _sig(cv)).astype(BF16), d_ref[...].astype(BF16))
        g_ref[...] = g
        dl_ref[...], nm_ref[...], nv_ref[...] = _adam_math(w_ref[...], g, m_ref[...], v_ref[...])

    blk = pl.BlockSpec((rb, n), lambda i: (i, 0))
    sds = jax.ShapeDtypeStruct(w.shape, F32)
    return pl.pallas_call(
        body, name="ada_bwd", grid=(D // rb,),
        in_specs=[pl.BlockSpec((nrow, rb), lambda i: (0, i)), pl.BlockSpec((nrow, n), lambda i: (0, 0)), blk, blk, blk],
        out_specs=[blk, blk, blk, blk], out_shape=[sds, sds, sds, sds],
        compiler_params=_cp("parallel"),
    )(c_all, dmod_cols, w, m, v)


def _adam(name, parts, w, m, v):
    p, r, c = parts.shape
    rb = r
    for cand in (256, 128, 64, 32, 16, 8):
        if r % cand == 0 and r >= cand:
            rb = cand
            break

    def body(p_ref, w_ref, m_ref, v_ref, g_ref, dl_ref, nm_ref, nv_ref):
        g = p_ref[0].astype(F32)
        for q in range(1, p):
            g = g + p_ref[q].astype(F32)
        g_ref[...] = g
        dl_ref[...], nm_ref[...], nv_ref[...] = _adam_math(w_ref[...], g, m_ref[...], v_ref[...])

    blk = pl.BlockSpec((rb, c), lambda i: (i, 0))
    sds = jax.ShapeDtypeStruct((r, c), F32)
    return pl.pallas_call(
        body, name=name, grid=(r // rb,),
        in_specs=[pl.BlockSpec((p, rb, c), lambda i: (0, i, 0)), blk, blk, blk],
        out_specs=[blk, blk, blk, blk], out_shape=[sds, sds, sds, sds],
        compiler_params=_cp("parallel"),
    )(parts, w, m, v)


def _my_pos():
    return lax.axis_index("x"), lax.axis_index("y"), lax.axis_index("c")


def _all_gather_small(name, v):
    m_per, n = v.shape

    def body(x_ref, out_ref, send_sems, recv_sems, local_sem):
        x, y, c = _my_pos()
        me, sibling = (x, y, c), (x, y, 1 - c)
        chips = [(1 - x, y), (x, 1 - y), (1 - x, 1 - y)]

        def rows(px, py, pc):
            return out_ref.at[pl.ds((4 * px + 2 * py + pc) * m_per, m_per), :]

        def copy(k, block, to, src=None):
            return pltpu.make_async_remote_copy(
                src_ref=rows(*block) if src is None else src, dst_ref=rows(*block),
                send_sem=send_sems.at[k], recv_sem=recv_sems.at[k], device_id=to, device_id_type=MESH)

        mine = pltpu.make_async_copy(x_ref, rows(*me), local_sem)
        mine.start()
        first = [copy(0, me, sibling, src=x_ref)]
        first += [copy(1 + j, me, (*chip, c), src=x_ref) for j, chip in enumerate(chips)]
        for cp in first:
            cp.start()
        passed = [copy(4 + j, (*chip, c), sibling) for j, chip in enumerate(chips)]
        for j, chip in enumerate(chips):
            copy(1 + j, (*chip, c), me).wait_recv()
            passed[j].start()
        copy(0, sibling, me).wait_recv()
        for j, chip in enumerate(chips):
            copy(4 + j, (*chip, 1 - c), me).wait_recv()
        for cp in first + passed:
            cp.wait_send()
        mine.wait()

    return pl.pallas_call(
        body, name=name, out_shape=jax.ShapeDtypeStruct((NDEV * m_per, n), v.dtype),
        in_specs=[pl.BlockSpec(memory_space=pltpu.VMEM)], out_specs=pl.BlockSpec(memory_space=pltpu.VMEM),
        scratch_shapes=[pltpu.SemaphoreType.DMA((7,)), pltpu.SemaphoreType.DMA((7,)), pltpu.SemaphoreType.DMA],
    )(v)


def _blk_cols(n):
    return lambda ref, b: ref.at[:, pl.ds(pl.multiple_of(b * n, 128), n)]


def _blk_rows(n):
    return lambda ref, b: ref.at[pl.ds(pl.multiple_of(b * n, 8), n), :]


def _blk_lead(ref, b):
    return ref.at[b]


def _blk_heads(ref, b):
    return ref.at[:, pl.ds(pl.multiple_of(b * (HB // NDEV), 8), HB // NDEV), :]


def _ag_phases(ins, outs, slicers, send_sems, recv_sems, local_sems):
    na = len(ins)
    x, y, c = _my_pos()
    me, sibling = (x, y, c), (x, y, 1 - c)
    chips = [(1 - x, y), (x, 1 - y), (1 - x, 1 - y)]

    def copy(a, k, block, to, from_shard=False):
        px, py, pc = block
        dst = slicers[a](outs[a], 4 * px + 2 * py + pc)
        return pltpu.make_async_remote_copy(
            src_ref=ins[a] if from_shard else dst, dst_ref=dst,
            send_sem=send_sems.at[a * 7 + k], recv_sem=recv_sems.at[a * 7 + k], device_id=to, device_id_type=MESH)

    def local(a):
        return pltpu.make_async_copy(ins[a], slicers[a](outs[a], 4 * x + 2 * y + c), local_sems.at[a])

    def firsts(a):
        return [copy(a, 0, me, sibling, True)] + [copy(a, 1 + j, me, (*chip, c), True) for j, chip in enumerate(chips)]

    def start():
        for a in range(na):
            local(a).start()
            for cp in firsts(a):
                cp.start()

    def forward():
        for a in range(na):
            for j, chip in enumerate(chips):
                copy(a, 1 + j, (*chip, c), me).wait_recv()
                copy(a, 4 + j, (*chip, c), sibling).start()

    def finish():
        for a in range(na):
            copy(a, 0, sibling, me).wait_recv()
            for j, chip in enumerate(chips):
                copy(a, 4 + j, (*chip, 1 - c), me).wait_recv()
        for a in range(na):
            for cp in firsts(a) + [copy(a, 4 + j, (*chip, c), sibling) for j, chip in enumerate(chips)]:
                cp.wait_send()
            local(a).wait()

    return start, forward, finish


def _ag_sems(na):
    return [pltpu.SemaphoreType.DMA((7 * na,)), pltpu.SemaphoreType.DMA((7 * na,)), pltpu.SemaphoreType.DMA((na,))]


def _all_gather_weights(shards, fulls, slicers):
    na = len(shards)

    def body(*refs):
        start, forward, finish = _ag_phases(refs[:na], refs[na:2 * na], slicers, *refs[2 * na:])
        start()
        forward()
        finish()

    return pl.pallas_call(
        body, name="gather_weights",
        out_shape=[jax.ShapeDtypeStruct(s, sh.dtype) for s, sh in zip(fulls, shards)],
        in_specs=[_ANY] * na, out_specs=[_ANY] * na, scratch_shapes=_ag_sems(na),
    )(*shards)


def _scatter_grads(grads, shard_shapes, slicers):
    na = len(grads)

    def body(*refs):
        ins, outs = refs[:na], refs[na:2 * na]
        send_sems, recv_sems, local_sems = refs[2 * na:]
        x, y, c = _my_pos()
        me = 4 * x + 2 * y + c
        mine, sent = [], []
        for a in range(na):
            cp = pltpu.make_async_copy(slicers[a](ins[a], me), outs[a].at[me], local_sems.at[a])
            cp.start()
            mine.append(cp)
        rel = [(k >> 2 & 1, k >> 1 & 1, k & 1) for k in range(1, NDEV)]
        for a in range(na):
            for k, (fx, fy, fc) in enumerate(rel):
                px, py, pc = x ^ fx, y ^ fy, c ^ fc
                cp = pltpu.make_async_remote_copy(
                    src_ref=slicers[a](ins[a], 4 * px + 2 * py + pc), dst_ref=outs[a].at[me],
                    send_sem=send_sems.at[a * 7 + k], recv_sem=recv_sems.at[a * 7 + k],
                    device_id=(px, py, pc), device_id_type=MESH)
                cp.start()
                sent.append(cp)
        for a in range(na):
            for k, (fx, fy, fc) in enumerate(rel):
                px, py, pc = x ^ fx, y ^ fy, c ^ fc
                src = 4 * px + 2 * py + pc
                pltpu.make_async_remote_copy(
                    src_ref=slicers[a](ins[a], me), dst_ref=outs[a].at[src],
                    send_sem=send_sems.at[a * 7 + k], recv_sem=recv_sems.at[a * 7 + k],
                    device_id=(px, py, pc), device_id_type=MESH).wait_recv()
        for cp in sent:
            cp.wait_send()
        for cp in mine:
            cp.wait()

    any_spec = pl.BlockSpec(memory_space=pl.ANY)
    return pl.pallas_call(
        body, name="scatter_grads",
        out_shape=[jax.ShapeDtypeStruct((NDEV,) + tuple(s), g.dtype) for s, g in zip(shard_shapes, grads)],
        in_specs=[any_spec] * na, out_specs=[any_spec] * na,
        scratch_shapes=[pltpu.SemaphoreType.DMA((7 * na,)), pltpu.SemaphoreType.DMA((7 * na,)),
                        pltpu.SemaphoreType.DMA((na,))],
    )(*grads)


def _local_step(x, target, mod, g_mix, g_ffn, g_fin, prm, w_in, shards):
    fulls = [(HEADS, HB, HB), (HEADS, HB, HB), (D, D), (NDEV, D, FB), (DFF, D)]
    slicers = [_blk_heads, _blk_heads, _blk_rows(D // NDEV), _blk_lead, _blk_rows(DFF // NDEV)]
    proj, h, (wa, wx, w_out, w_gu, w_down) = _in_proj(x, mod, g_mix, w_in, shards, fulls, slicers)
    w_gu = w_gu.reshape(2, 4, D, FB)
    merged, hl = _mixer_fwd(proj, prm, wa, wx)
    x1, h2 = _out_proj(merged, x, mod, g_ffn, w_out)
    gu, dx2, dx2b, loss, d_gfin = _ffn_fwd(h2, x1, target, mod, g_fin, w_gu, w_down)
    dgu, act, dx1, dx1b, dmg, sums2 = _ffn_bwd(dx2, gu, x1, mod, g_ffn, w_gu, w_down, w_out)
    chip_order = _xor_order(_my_index() >> 1, NCHIP)
    p_wgu = _gu_wgrad(h2, dgu, chip_order)
    p_wdown, d_gt2 = _scaled_wgrad("down_wgrad", act, dx2b, w_down, 5, mod, chip_order)
    p_wout, d_gt1 = _scaled_wgrad("out_wgrad", merged.reshape(1, *merged.shape), dx1b, w_out, 2, mod,
                                  jnp.zeros((1,), jnp.int32))
    dproj, msums, g_wa, g_wx = _mixer_bwd(proj, hl, dmg, prm, wa, wx)
    p_win, p_wa, p_wx = _in_wgrad(h, dproj, g_wa, g_wx, chip_order)
    grad_x, sums1 = _in_proj_bwd(dproj, w_in, x, dx1, mod, g_mix)
    return dict(loss=loss, grad_x=grad_x, d_gfin=d_gfin, sums1=sums1, sums2=sums2, msums=msums,
                d_gt1=d_gt1[0:1], d_gt2=d_gt2[0:1], p_win=p_win, p_wa=p_wa, p_wx=p_wx, p_wout=p_wout, p_wgu=p_wgu,
                p_wdown=p_wdown)


def kernel(x, c, w_ada, b_ada, g_norm_mix, w_in, conv_a_w, conv_b_w, conv_b_bias, w_rg_a, b_rg_a, w_rg_x, b_rg_x, lru_lambda, w_out, g_norm_ffn, w_gate_up, w_down, g_norm_final, loss_target, m_w_ada, m_b_ada, m_g_norm_mix, m_w_in, m_conv_a_w, m_conv_b_w, m_conv_b_bias, m_w_rg_a, m_b_rg_a, m_w_rg_x, m_b_rg_x, m_lru_lambda, m_w_out, m_g_norm_ffn, m_w_gate_up, m_w_down, m_g_norm_final, v_w_ada, v_b_ada, v_g_norm_mix, v_w_in, v_conv_a_w, v_conv_b_w, v_conv_b_bias, v_w_rg_a, v_b_rg_a, v_w_rg_x, v_b_rg_x, v_lru_lambda, v_w_out, v_g_norm_ffn, v_w_gate_up, v_w_down, v_g_norm_final):
    me = 4 * lax.axis_index("x") + 2 * lax.axis_index("y") + lax.axis_index("c")
    ncol = w_ada.shape[2]
    cw = conv_a_w.shape[2]

    pack0 = jnp.concatenate([c, conv_a_w.reshape(1, 3 * cw), conv_b_w.reshape(1, 4 * cw)], axis=1)
    got0 = _all_gather_small("gather_c", jnp.broadcast_to(pack0, (8, pack0.shape[1])))
    got0 = got0.reshape(NDEV, 8, -1)[:, 0, :]
    c_all = got0[:, :D]
    conv_a = got0[:, D:D + 3 * cw].reshape(NDEV, 3, cw).transpose(1, 0, 2).reshape(3, D)
    conv_b = got0[:, D + 3 * cw:].reshape(NDEV, 4, cw).transpose(1, 0, 2).reshape(4, D)

    b_cols = lax.dynamic_slice_in_dim(b_ada, me * ncol, ncol, axis=1)
    c16 = jnp.concatenate([c_all, jnp.zeros((8, D), F32)], axis=0)
    mod_cols = _ada_fwd(c16, w_ada[0], b_cols)[:NDEV]
    got1 = _all_gather_small("gather_mod", mod_cols).reshape(NDEV, NDEV, ncol)
    mod6 = lax.dynamic_index_in_dim(got1, me, axis=1, keepdims=False).reshape(6, D)
    mod = jnp.concatenate([mod6, jnp.zeros((2, D), F32)], axis=0)

    (w_in_f,) = _all_gather_weights([w_in[0].astype(BF16)], [(D, 7 * D)], [_blk_cols(7 * D // NDEV)])
    shards = [w_rg_a[0].astype(BF16), w_rg_x[0].astype(BF16), w_out[0].astype(BF16), w_gate_up[0].astype(BF16),
              w_down[0].astype(BF16)]

    prm = jnp.concatenate([conv_a, conv_b, conv_b_bias, b_rg_a, b_rg_x, lru_lambda, jnp.zeros((5, D), F32)], axis=0)
    r = _local_step(x[0], loss_target[0], mod, g_norm_mix, g_norm_ffn, g_norm_final.reshape(1, D), prm,
                    w_in_f, shards)

    parts = [r["p_win"], r["p_wa"], r["p_wx"], r["p_wout"], r["p_wgu"], r["p_wdown"]]
    big = {}
    for nm, p, w, m, v in (("w_in", parts[0], w_in, m_w_in, v_w_in), ("w_rg_a", parts[1], w_rg_a, m_w_rg_a, v_w_rg_a),
                           ("w_rg_x", parts[2], w_rg_x, m_w_rg_x, v_w_rg_x), ("w_out", parts[3], w_out, m_w_out, v_w_out),
                           ("w_gate_up", parts[4], w_gate_up, m_w_gate_up, v_w_gate_up),
                           ("w_down", parts[5], w_down, m_w_down, v_w_down)):
        two_d = (-1, w.shape[-1])
        outs = _adam("adam_" + nm, p.reshape((p.shape[0],) + w.reshape(two_d).shape), w.reshape(two_d), m.reshape(two_d),
                     v.reshape(two_d))
        big[nm] = [o.reshape(w.shape) for o in outs]

    small = jnp.concatenate([
        r["sums1"][S_SH:S_SH + 1], r["sums1"][S_SC:S_SC + 1], r["d_gt1"],
        r["sums2"][S_SH:S_SH + 1], r["sums2"][S_SC:S_SC + 1], r["d_gt2"],
        r["sums1"][S_G:S_G + 1],
        r["msums"][M_CBIAS:M_CBIAS + 1], r["msums"][M_BA:M_BA + 1], r["msums"][M_BX:M_BX + 1],
        r["msums"][M_LS:M_LS + 1],
        r["sums2"][S_G:S_G + 1], r["d_gfin"],
        r["msums"][M_WA:M_WA + 3], r["msums"][M_WB:M_WB + 4],
        jnp.zeros((4, D), F32)], axis=0)
    got2 = _all_gather_small("gather_small", small).reshape(NDEV, 24, D)

    rep_w = jnp.concatenate([b_ada.reshape(6, D), g_norm_mix, conv_b_bias, b_rg_a, b_rg_x, lru_lambda, g_norm_ffn,
                             g_norm_final.reshape(1, D), jnp.zeros((3, D), F32)], axis=0)
    rep_m = jnp.concatenate([m_b_ada.reshape(6, D), m_g_norm_mix, m_conv_b_bias, m_b_rg_a, m_b_rg_x, m_lru_lambda,
                             m_g_norm_ffn, m_g_norm_final.reshape(1, D), jnp.zeros((3, D), F32)], axis=0)
    rep_v = jnp.concatenate([v_b_ada.reshape(6, D), v_g_norm_mix, v_conv_b_bias, v_b_rg_a, v_b_rg_x, v_lru_lambda,
                             v_g_norm_ffn, v_g_norm_final.reshape(1, D), jnp.ones((3, D), F32)], axis=0)
    rep = _adam("adam_rep", got2[:, :16, :], rep_w, rep_m, rep_v)

    conv_parts = lax.dynamic_slice_in_dim(got2[:, 13:21, :], me * cw, cw, axis=2)
    cv_w = jnp.concatenate([conv_a_w[0], conv_b_w[0], jnp.zeros((1, cw), F32)], axis=0)
    cv_m = jnp.concatenate([m_conv_a_w[0], m_conv_b_w[0], jnp.zeros((1, cw), F32)], axis=0)
    cv_v = jnp.concatenate([v_conv_a_w[0], v_conv_b_w[0], jnp.ones((1, cw), F32)], axis=0)
    cvo = _adam("adam_conv", conv_parts, cv_w, cv_m, cv_v)

    dmod_cols = lax.dynamic_slice_in_dim(got2[:, :6, :].reshape(NDEV, 6 * D), me * ncol, ncol, axis=1)
    dmod16 = jnp.concatenate([dmod_cols, jnp.zeros((8, ncol), F32)], axis=0)
    ada = _ada_bwd(c16, dmod16, w_ada[0], m_w_ada[0], v_w_ada[0])

    loss = lax.psum(r["loss"][0, 0], AXES)

    def pick(q):
        one = lambda i: rep[q][i:i + 1]
        return [ada[q].reshape(w_ada.shape), rep[q][0:6].reshape(b_ada.shape), one(6), big["w_in"][q],
                cvo[q][0:3].reshape(conv_a_w.shape), cvo[q][3:7].reshape(conv_b_w.shape), one(7),
                big["w_rg_a"][q], one(8), big["w_rg_x"][q], one(9), one(10), big["w_out"][q], one(11),
                big["w_gate_up"][q], big["w_down"][q], rep[q][12]]

    return (loss, r["grad_x"].reshape(x.shape), *pick(0), *pick(1), *pick(2), *pick(3))
```

```python
import functools
import math

import jax
import jax.numpy as jnp
from jax import lax
from jax.experimental import pallas as pl
from jax.experimental.pallas import tpu as pltpu

F32 = jnp.float32
BF16 = jnp.bfloat16

D = 1024
DFF = 2816
NDEV = 8
HEADS = 4
HB = D // HEADS
FB = DFF // 4
EPS = 1e-6
LRU_C = 8.0
ADAM_LR, ADAM_B1, ADAM_B2, ADAM_EPS, ADAM_WD, ADAM_STEP = 0.001, 0.9, 0.999, 1e-08, 0.01, 10

VMEM_LIMIT = 56 * 1024 * 1024
TM = 512
TMI = 1024
TMF = 256
TK = 2048
TKI = 1024
SUB = 256
UNROLL = 4
TT = 256
CG = 256
MESH = pl.DeviceIdType.MESH
AXES = ("x", "y", "c")


def _cp(*sem):
    return pltpu.CompilerParams(dimension_semantics=sem, vmem_limit_bytes=VMEM_LIMIT)


def _sig(x):
    return 1.0 / (1.0 + jnp.exp(-x))


def _log_sigmoid(x):
    z = jnp.exp(-jnp.abs(x))
    u = 1.0 + z
    d = u - 1.0
    l1p = jnp.where(d == 0.0, z, jnp.log(u) * (z / jnp.where(d == 0.0, 1.0, d)))
    return -(jnp.maximum(-x, 0.0) + l1p)


def _neg_expm1(x):
    p = x * (1.0 + x * 0.5 * (1.0 + x * (1.0 / 3.0) * (1.0 + x * 0.25 * (1.0 + x * 0.2 * (1.0 + x * (1.0 / 6.0))))))
    return jnp.where(x > -0.25, -p, 1.0 - jnp.exp(x))


_GC = math.sqrt(2.0 / math.pi)


def _gelu(x):
    t = jnp.tanh(_GC * (x + 0.044715 * x * x * x))
    return 0.5 * x * (1.0 + t), t


def _dot(a, b):
    return jnp.dot(a, b, preferred_element_type=F32)


def _dot_nt(a, b):
    return lax.dot_general(a, b, (((1,), (1,)), ((), ())), preferred_element_type=F32)


def _dot_tn(a, b):
    return lax.dot_general(a, b, (((0,), (0,)), ((), ())), preferred_element_type=F32)


def _resident(shape):
    return pl.BlockSpec(shape, lambda *_: (0,) * len(shape), pipeline_mode=pl.Buffered(1))


def _sub_blocks(n_rows):
    step = min(SUB, n_rows)
    return [slice(r, r + step) for r in range(0, n_rows, step)]


def _fold8(v):
    return v[0:8] + v[8:16]


def _in_proj(x, mod, g_mix, w_in, shards, fulls, slicers):
    t_len = x.shape[0]
    tm = min(TMI, t_len)
    ni = t_len // tm
    na = len(shards)
    rc = 32

    def body(x_ref, mod_ref, g_ref, w_ref, *rest):
        ins, (proj_ref, h_ref), outs = rest[:na], rest[na:na + 2], rest[na + 2:2 * na + 2]
        h_scr = rest[2 * na + 2]
        start, forward, finish = _ag_phases(ins, outs, slicers, *rest[2 * na + 3:])
        i, s = pl.program_id(0), pl.program_id(1)

        @pl.when((i == 0) & (s == 0))
        def _():
            start()

        @pl.when((i == ni // 2) & (s == 0))
        def _():
            forward()

        @pl.when(s == 0)
        def _():
            gs = g_ref[...] * (1.0 + mod_ref[1:2, :])
            sh = mod_ref[0:1, :]

            def chunk(i, carry):
                rows = pl.ds(pl.multiple_of(i * rc, rc), rc)
                xv = x_ref[rows, :]
                r = lax.rsqrt(jnp.mean(xv * xv, axis=-1, keepdims=True) + EPS)
                h = (xv * r * gs + sh).astype(BF16)
                h_scr[rows, :] = h
                h_ref[rows, :] = h
                return carry

            lax.fori_loop(0, tm // rc, chunk, 0, unroll=UNROLL)

        proj_ref[0] = _dot(h_scr[...], w_ref[:, pl.ds(pl.multiple_of(s * D, D), D)]).astype(BF16)

        @pl.when((i == ni - 1) & (s == 6))
        def _():
            finish()

    res = pl.pallas_call(
        body, name="in_proj", grid=(ni, 7),
        in_specs=[pl.BlockSpec((tm, D), lambda i, s: (i, 0)),
                  pl.BlockSpec((8, D), lambda i, s: (0, 0)),
                  pl.BlockSpec((1, D), lambda i, s: (0, 0)),
                  _resident((D, 7 * D))] + [_ANY] * na,
        out_specs=[pl.BlockSpec((1, tm, D), lambda i, s: (s, i, 0)),
                   pl.BlockSpec((tm, D), lambda i, s: (i, 0))] + [_ANY] * na,
        out_shape=[jax.ShapeDtypeStruct((7, t_len, D), BF16), jax.ShapeDtypeStruct((t_len, D), BF16)]
        + [jax.ShapeDtypeStruct(f, sh.dtype) for f, sh in zip(fulls, shards)],
        scratch_shapes=[pltpu.VMEM((tm, D), BF16)] + _ag_sems(na),
        compiler_params=_cp("arbitrary", "arbitrary"),
    )(x, mod, g_mix, w_in, *shards)
    return res[0], res[1], res[2:]


P_WA, P_WB, P_CBIAS, P_BA, P_BX, P_LAM = 0, 3, 7, 8, 9, 10


def _lru_gates(rp, ip, ls, first_row):
    r = _sig(rp)
    ig = _sig(ip)
    la = LRU_C * r * ls
    a = jnp.exp(la)
    m2 = _neg_expm1(2.0 * la)
    mult = jnp.where(first_row, 1.0, jnp.sqrt(jnp.maximum(m2, 0.0)))
    return r, ig, la, a, m2, mult


def _mixer_fwd(proj, prm, wa, wx):
    t_len = proj.shape[1]
    tt = min(TT, t_len)

    def body(proj_ref, prm_ref, wa_ref, wx_ref, mg_ref, hl_ref, xe, ve, hc, u_s, ya_s, rp_s, ip_s):
        t = pl.program_id(0)

        @pl.when(t == 0)
        def _():
            xe[0:8, :] = jnp.zeros((8, D), F32)
            ve[0:8, :] = jnp.zeros((8, D), F32)
            hc[...] = jnp.zeros((8, D), F32)

        xe[8:8 + tt, :] = proj_ref[3].astype(F32)
        ve[8:8 + tt, :] = proj_ref[1].astype(F32) * proj_ref[2].astype(F32)
        u = prm_ref[P_CBIAS:P_CBIAS + 1, :] + prm_ref[P_WB:P_WB + 1, :] * xe[5:5 + tt, :]
        for k in range(1, 4):
            u = u + prm_ref[P_WB + k:P_WB + k + 1, :] * xe[5 + k:5 + k + tt, :]
        u_s[...] = u
        ya = prm_ref[P_WA:P_WA + 1, :] * ve[6:6 + tt, :]
        for k in range(1, 3):
            ya = ya + prm_ref[P_WA + k:P_WA + k + 1, :] * ve[6 + k:6 + k + tt, :]
        ya_s[...] = ya
        xe[0:8, :] = xe[tt:tt + 8, :]
        ve[0:8, :] = ve[tt:tt + 8, :]

        ub = u.astype(BF16)
        for h in range(HEADS):
            cs = slice(h * HB, (h + 1) * HB)
            rp_s[:, cs] = _dot(ub[:, cs], wa_ref[h]) + prm_ref[P_BA:P_BA + 1, cs]
            ip_s[:, cs] = _dot(ub[:, cs], wx_ref[h]) + prm_ref[P_BX:P_BX + 1, cs]

        ls_all = _log_sigmoid(prm_ref[P_LAM:P_LAM + 1, :])
        row = lax.broadcasted_iota(jnp.int32, (8, CG), 0)

        def blk(i, carry):
            r0 = pl.multiple_of(i * 16, 16)
            for g in range(D // CG):
                cs = slice(g * CG, (g + 1) * CG)
                ls = ls_all[:, cs]
                hprev = hc[:, cs]
                hs = []
                for sb in range(2):
                    rr = r0 + 8 * sb
                    first = (row + (t * tt + rr)) == 0
                    _, ig, _, a, _, mult = _lru_gates(rp_s[pl.ds(rr, 8), cs], ip_s[pl.ds(rr, 8), cs], ls, first)
                    b = mult * (ig * u_s[pl.ds(rr, 8), cs])
                    for s in (1, 2, 4):
                        a_sh = jnp.where(row >= s, pltpu.roll(a, s, 0), 1.0)
                        b_sh = jnp.where(row >= s, pltpu.roll(b, s, 0), 0.0)
                        b = a * b_sh + b
                        a = a * a_sh
                    hv = a * hprev + b
                    hprev = jnp.broadcast_to(hv[7:8, :], hv.shape)
                    hs.append(hv)
                hc[:, cs] = hprev
                h16 = jnp.concatenate(hs, axis=0)
                rows = pl.ds(r0, 16)
                gl, _ = _gelu(proj_ref[4, rows, cs].astype(F32))
                y_b = h16 * gl
                y_a = proj_ref[0, rows, cs].astype(F32) * ya_s[rows, cs]
                mg = _sig(proj_ref[5, rows, cs].astype(F32)) * y_a + _sig(proj_ref[6, rows, cs].astype(F32)) * y_b
                mg_ref[rows, cs] = mg.astype(BF16)
                hl_ref[rows, cs] = h16.astype(BF16)
            return carry

        lax.fori_loop(0, tt // 16, blk, 0)

    return pl.pallas_call(
        body, name="mixer_fwd", grid=(t_len // tt,),
        in_specs=[pl.BlockSpec((7, tt, D), lambda t: (0, t, 0)),
                  pl.BlockSpec((16, D), lambda t: (0, 0)),
                  pl.BlockSpec((HEADS, HB, HB), lambda t: (0, 0, 0)),
                  pl.BlockSpec((HEADS, HB, HB), lambda t: (0, 0, 0))],
        out_specs=[pl.BlockSpec((tt, D), lambda t: (t, 0)), pl.BlockSpec((tt, D), lambda t: (t, 0))],
        out_shape=[jax.ShapeDtypeStruct((t_len, D), BF16), jax.ShapeDtypeStruct((t_len, D), BF16)],
        scratch_shapes=[pltpu.VMEM((tt + 8, D), F32), pltpu.VMEM((tt + 8, D), F32), pltpu.VMEM((8, D), F32),
                        pltpu.VMEM((tt, D), F32), pltpu.VMEM((tt, D), F32), pltpu.VMEM((tt, D), F32),
                        pltpu.VMEM((tt, D), F32)],
        compiler_params=_cp("arbitrary"),
    )(proj, prm, wa, wx)


def _out_proj(merged, x, mod, g_ffn, w_out):
    t_len = x.shape[0]
    tm = min(TM, t_len)

    def body(mg_ref, x_ref, mod_ref, g_ref, w_ref, x1_ref, h2_ref):
        gt1 = mod_ref[2:3, :]
        for rows in _sub_blocks(tm):
            x1_ref[rows, :] = x_ref[rows, :] + gt1 * _dot(mg_ref[rows, :], w_ref[...])
        gs = g_ref[...] * (1.0 + mod_ref[4:5, :])
        sh = mod_ref[3:4, :]

        def chunk(c, carry):
            rows = pl.ds(pl.multiple_of(c * 16, 16), 16)
            x1 = x1_ref[rows, :]
            r = lax.rsqrt(jnp.mean(x1 * x1, axis=-1, keepdims=True) + EPS)
            h2_ref[rows, :] = (x1 * r * gs + sh).astype(BF16)
            return carry

        lax.fori_loop(0, tm // 16, chunk, 0, unroll=UNROLL)

    return pl.pallas_call(
        body, name="out_proj", grid=(t_len // tm,),
        in_specs=[pl.BlockSpec((tm, D), lambda i: (i, 0)), pl.BlockSpec((tm, D), lambda i: (i, 0)),
                  pl.BlockSpec((8, D), lambda i: (0, 0)), pl.BlockSpec((1, D), lambda i: (0, 0)),
                  pl.BlockSpec((D, D), lambda i: (0, 0))],
        out_specs=[pl.BlockSpec((tm, D), lambda i: (i, 0)), pl.BlockSpec((tm, D), lambda i: (i, 0))],
        out_shape=[jax.ShapeDtypeStruct((t_len, D), F32), jax.ShapeDtypeStruct((t_len, D), BF16)],
        compiler_params=_cp("parallel"),
    )(merged, x, mod, g_ffn, w_out)


def _ffn_fwd(h2, x1, target, mod, g_fin, w_gu, w_down):
    t_len = x1.shape[0]
    tm = min(TMF, t_len)
    assert tm % (16 * UNROLL) == 0

    def body(h2_ref, x1_ref, tg_ref, mod_ref, g_ref, wgu_ref, wd_ref, gu_ref, dx2_ref, dx2b_ref, loss_ref, dg_ref, acc):
        @pl.when(pl.program_id(0) == 0)
        def _():
            loss_ref[...] = jnp.zeros_like(loss_ref)
            dg_ref[...] = jnp.zeros_like(dg_ref)

        hb = h2_ref[...]
        ffn = None
        for j in range(4):
            gate = _dot(hb, wgu_ref[0, j])
            up = _dot(hb, wgu_ref[1, j])
            gu_ref[0, j] = gate.astype(BF16)
            gu_ref[1, j] = up.astype(BF16)
            act = (gate * _sig(gate) * up).astype(BF16)
            part = _dot(act, wd_ref[j * FB:(j + 1) * FB, :])
            ffn = part if ffn is None else ffn + part
        acc[...] = ffn

        gt2 = mod_ref[5:6, :]
        gf = g_ref[...]

        def chunk(c, carry):
            s_loss, s_dg = carry
            for u in range(UNROLL):
                rows = pl.ds(pl.multiple_of(c * (16 * UNROLL), 16) + 16 * u, 16)
                x2 = x1_ref[rows, :] + gt2 * acc[rows, :]
                r = lax.rsqrt(jnp.mean(x2 * x2, axis=-1, keepdims=True) + EPS)
                xn = x2 * r
                diff = xn * gf - tg_ref[rows, :]
                dy = diff * (1.0 / D)
                dxn = dy * gf
                dx2 = r * (dxn - xn * jnp.mean(dxn * xn, axis=-1, keepdims=True))
                dx2_ref[rows, :] = dx2
                dx2b_ref[rows, :] = dx2.astype(BF16)
                s_loss, s_dg = s_loss + _fold8(diff * diff), s_dg + _fold8(dy * xn)
            return s_loss, s_dg

        zero = jnp.zeros((8, D), F32)
        s_loss, s_dg = lax.fori_loop(0, tm // (16 * UNROLL), chunk, (zero, zero))
        loss_ref[...] += jnp.sum(s_loss) * (0.5 / D)
        dg_ref[...] += jnp.sum(s_dg, axis=0, keepdims=True)

    row = pl.BlockSpec((tm, D), lambda i: (i, 0))
    return pl.pallas_call(
        body, name="ffn_fwd", grid=(t_len // tm,),
        in_specs=[row, row, row, pl.BlockSpec((8, D), lambda i: (0, 0)), pl.BlockSpec((1, D), lambda i: (0, 0)),
                  _resident((2, 4, D, FB)), _resident((DFF, D))],
        out_specs=[pl.BlockSpec((2, 4, tm, FB), lambda i: (0, 0, i, 0)), row, row,
                   pl.BlockSpec((1, 128), lambda i: (0, 0)), pl.BlockSpec((1, D), lambda i: (0, 0))],
        out_shape=[jax.ShapeDtypeStruct((2, 4, t_len, FB), BF16), jax.ShapeDtypeStruct((t_len, D), F32),
                   jax.ShapeDtypeStruct((t_len, D), BF16),
                   jax.ShapeDtypeStruct((1, 128), F32), jax.ShapeDtypeStruct((1, D), F32)],
        scratch_shapes=[pltpu.VMEM((tm, D), F32)],
        compiler_params=_cp("arbitrary"),
    )(h2, x1, target, mod, g_fin, w_gu, w_down)


S_SH, S_SC, S_G = 0, 1, 2


def _norm_bwd_rows(n_rows, rc, dh_ref, x_ref, dres_ref, scale, gain, sums_ref, write):
    assert n_rows % (rc * UNROLL) == 0
    gs = 1.0 + scale
    fold = _fold8 if rc == 16 else (lambda v: v)

    def chunk(c, carry):
        s_sh, s_sc, s_g = carry
        for u in range(UNROLL):
            rows = pl.ds(pl.multiple_of(c * (rc * UNROLL), rc) + rc * u, rc)
            dh = dh_ref[rows, :]
            xv = x_ref[rows, :]
            r = lax.rsqrt(jnp.mean(xv * xv, axis=-1, keepdims=True) + EPS)
            xn = xv * r
            dhn = dh * gs
            dxn = dhn * gain
            write(rows, dres_ref[rows, :] + r * (dxn - xn * jnp.mean(dxn * xn, axis=-1, keepdims=True)))
            s_sh, s_sc, s_g = s_sh + fold(dh), s_sc + fold(dh * (xn * gain)), s_g + fold(dhn * xn)
        return s_sh, s_sc, s_g

    zero = jnp.zeros((8, D), F32)
    s_sh, s_sc, s_g = lax.fori_loop(0, n_rows // (rc * UNROLL), chunk, (zero, zero, zero))
    sums_ref[S_SH:S_SH + 1, :] += jnp.sum(s_sh, axis=0, keepdims=True)
    sums_ref[S_SC:S_SC + 1, :] += jnp.sum(s_sc, axis=0, keepdims=True)
    sums_ref[S_G:S_G + 1, :] += jnp.sum(s_g, axis=0, keepdims=True)


def _ffn_bwd(dx2, gu, x1, mod, g_ffn, w_gu, w_down, w_out):
    t_len = x1.shape[0]
    tm = min(TMF, t_len)

    def body(dx2_ref, gu_ref, x1_ref, mod_ref, g_ref, wgu_ref, wd_ref, wo_ref,
             dgu_ref, act_ref, dx1_ref, dx1b_ref, dmg_ref, sums_ref, acc, dmo):
        @pl.when(pl.program_id(0) == 0)
        def _():
            sums_ref[...] = jnp.zeros_like(sums_ref)

        dffn = (dx2_ref[...] * mod_ref[5:6, :]).astype(BF16)
        dh2 = None
        for j in range(4):
            dact = _dot_nt(dffn, wd_ref[j * FB:(j + 1) * FB, :])
            gate = gu_ref[0, j].astype(F32)
            up = gu_ref[1, j].astype(F32)
            sg = _sig(gate)
            silu = gate * sg
            act_ref[j] = (silu * up).astype(BF16)
            dgate = (dact * up * (sg * (1.0 + gate * (1.0 - sg)))).astype(BF16)
            dup = (dact * silu).astype(BF16)
            dgu_ref[0, j] = dgate
            dgu_ref[1, j] = dup
            part = _dot_nt(dgate, wgu_ref[0, j]) + _dot_nt(dup, wgu_ref[1, j])
            dh2 = part if dh2 is None else dh2 + part
        acc[...] = dh2

        gt1 = mod_ref[2:3, :]

        def write(rows, dx1):
            dx1_ref[rows, :] = dx1
            dx1b_ref[rows, :] = dx1.astype(BF16)
            dmo[rows, :] = (dx1 * gt1).astype(BF16)

        _norm_bwd_rows(tm, 16, acc, x1_ref, dx2_ref, mod_ref[4:5, :], g_ref[...], sums_ref, write)
        dmg_ref[...] = _dot_nt(dmo[...], wo_ref[...]).astype(BF16)

    row = pl.BlockSpec((tm, D), lambda i: (i, 0))
    return pl.pallas_call(
        body, name="ffn_bwd", grid=(t_len // tm,),
        in_specs=[row, pl.BlockSpec((2, 4, tm, FB), lambda i: (0, 0, i, 0)), row,
                  pl.BlockSpec((8, D), lambda i: (0, 0)), pl.BlockSpec((1, D), lambda i: (0, 0)),
                  _resident((2, 4, D, FB)), _resident((DFF, D)), _resident((D, D))],
        out_specs=[pl.BlockSpec((2, 4, tm, FB), lambda i: (0, 0, i, 0)),
                   pl.BlockSpec((4, tm, FB), lambda i: (0, i, 0)), row, row, row,
                   pl.BlockSpec((8, D), lambda i: (0, 0))],
        out_shape=[jax.ShapeDtypeStruct((2, 4, t_len, FB), BF16), jax.ShapeDtypeStruct((4, t_len, FB), BF16),
                   jax.ShapeDtypeStruct((t_len, D), F32), jax.ShapeDtypeStruct((t_len, D), BF16),
                   jax.ShapeDtypeStruct((t_len, D), BF16), jax.ShapeDtypeStruct((8, D), F32)],
        scratch_shapes=[pltpu.VMEM((tm, D), F32), pltpu.VMEM((tm, D), BF16)],
        compiler_params=_cp("arbitrary"),
    )(dx2, gu, x1, mod, g_ffn, w_gu, w_down, w_out)


def _my_pos():
    return lax.axis_index("x"), lax.axis_index("y"), lax.axis_index("c")


def _my_index():
    x, y, c = _my_pos()
    return 4 * x + 2 * y + c


def _device_of(b):
    return (b >> 2) & 1, (b >> 1) & 1, b & 1


def _rs_send(src, parts_ref, b, send_sems, recv_sems, local_sem):
    me = _my_index()
    dst = parts_ref.at[me]

    @pl.when(b == me)
    def _():
        pltpu.make_async_copy(src, dst, local_sem).start()

    @pl.when(b != me)
    def _():
        pltpu.make_async_remote_copy(src_ref=src, dst_ref=dst, send_sem=send_sems.at[b], recv_sem=recv_sems.at[me],
                                     device_id=_device_of(b), device_id_type=MESH).start()


def _rs_finish(src_of, parts_ref, send_sems, recv_sems, local_sem):
    me = _my_index()
    for s in range(NDEV):
        @pl.when(s != me)
        def _():
            cp = pltpu.make_async_remote_copy(src_ref=src_of(s), dst_ref=parts_ref.at[s], send_sem=send_sems.at[s],
                                              recv_sem=recv_sems.at[s], device_id=_device_of(s), device_id_type=MESH)
            cp.wait_send()
            cp.wait_recv()

        @pl.when(s == me)
        def _():
            pltpu.make_async_copy(src_of(s), parts_ref.at[s], local_sem).wait()


_RS_SEMS = [pltpu.SemaphoreType.DMA((NDEV,)), pltpu.SemaphoreType.DMA((NDEV,)), pltpu.SemaphoreType.DMA]
_ANY = pl.BlockSpec(memory_space=pl.ANY)


def _xor_order(me, n):
    return (me ^ (n - 1 - jnp.arange(n, dtype=jnp.int32))).astype(jnp.int32)


NCHIP = NDEV // 2


def _rs2_scratch(half_shape):
    blocks = lambda *lead: pltpu.VMEM(lead + tuple(half_shape), BF16)
    return [blocks(NCHIP, 2), blocks(NCHIP)] + [pltpu.SemaphoreType.DMA((NCHIP,))] * 4 + [pltpu.SemaphoreType.DMA]


def _rs2_to_sibling(q, rs):
    stage, from_sib, d_send, d_recv = rs[:4]
    x, y, c = _my_pos()
    pltpu.make_async_remote_copy(src_ref=stage.at[q, 1 - c], dst_ref=from_sib.at[q], send_sem=d_send.at[q],
                                 recv_sem=d_recv.at[q], device_id=(x, y, 1 - c), device_id_type=MESH).start()


def _rs2_forward(q, parts_ref, rs):
    stage, chip_sum, d_send, d_recv, i_send, i_recv, local_sem = rs
    x, y, c = _my_pos()
    my_chip = 2 * x + y
    pltpu.make_async_remote_copy(src_ref=stage.at[q, c], dst_ref=chip_sum.at[q], send_sem=d_send.at[q],
                                 recv_sem=d_recv.at[q], device_id=(x, y, 1 - c), device_id_type=MESH).wait_recv()
    chip_sum[q] = (stage[q, c].astype(F32) + chip_sum[q].astype(F32)).astype(BF16)

    @pl.when(q == my_chip)
    def _():
        pltpu.make_async_copy(chip_sum.at[q], parts_ref.at[my_chip], local_sem).start()

    @pl.when(q != my_chip)
    def _():
        pltpu.make_async_remote_copy(src_ref=chip_sum.at[q], dst_ref=parts_ref.at[my_chip], send_sem=i_send.at[q],
                                     recv_sem=i_recv.at[my_chip], device_id=((q >> 1) & 1, q & 1, c),
                                     device_id_type=MESH).start()


def _rs2_finish(parts_ref, rs):
    stage, chip_sum, d_send, d_recv, i_send, i_recv, local_sem = rs
    x, y, c = _my_pos()
    my_chip = 2 * x + y
    for q in range(NCHIP):
        pltpu.make_async_remote_copy(src_ref=stage.at[q, 1 - c], dst_ref=chip_sum.at[q], send_sem=d_send.at[q],
                                     recv_sem=d_recv.at[q], device_id=(x, y, 1 - c), device_id_type=MESH).wait_send()

        @pl.when(q != my_chip)
        def _():
            cp = pltpu.make_async_remote_copy(src_ref=chip_sum.at[q], dst_ref=parts_ref.at[q], send_sem=i_send.at[q],
                                              recv_sem=i_recv.at[q], device_id=((q >> 1) & 1, q & 1, c),
                                              device_id_type=MESH)
            cp.wait_send()
            cp.wait_recv()

        @pl.when(q == my_chip)
        def _():
            pltpu.make_async_copy(chip_sum.at[q], parts_ref.at[q], local_sem).wait()


def _gu_wgrad(h2, dgu, order):
    t_len = h2.shape[0]
    tk = min(TK, t_len)
    nk = t_len // tk

    def body(ord_ref, h_ref, d_ref, parts_ref, acc, *rs):
        p, k = pl.program_id(0), pl.program_id(1)

        @pl.when(k == 0)
        def _():
            acc[...] = jnp.zeros_like(acc)

        hb = h_ref[...]
        for half in range(2):
            acc[half] += _dot_tn(hb, d_ref[0, half])

        @pl.when(k == nk - 1)
        def _():
            q = ord_ref[p]
            rs[0][q] = acc[...].astype(BF16)
            _rs2_to_sibling(q, rs)

        @pl.when((k == nk - 1) & (p > 0))
        def _():
            _rs2_forward(ord_ref[p - 1], parts_ref, rs)

        @pl.when((p == NCHIP - 1) & (k == nk - 1))
        def _():
            _rs2_forward(ord_ref[p], parts_ref, rs)
            _rs2_finish(parts_ref, rs)

    return pl.pallas_call(
        body, name="gu_wgrad",
        grid_spec=pltpu.PrefetchScalarGridSpec(
            num_scalar_prefetch=1, grid=(NCHIP, nk),
            in_specs=[pl.BlockSpec((tk, D), lambda p, k, o: (k, 0)),
                      pl.BlockSpec((1, 2, tk, FB), lambda p, k, o: (o[p], 0, k, 0))],
            out_specs=_ANY,
            scratch_shapes=[pltpu.VMEM((2, D, FB), F32)] + _rs2_scratch((D, FB))),
        out_shape=jax.ShapeDtypeStruct((NCHIP, D, FB), BF16),
        compiler_params=_cp("arbitrary", "arbitrary"),
    )(order, h2, dgu.reshape(NCHIP, 2, t_len, FB))


def _scaled_wgrad(name, a, dx, w, gate_row, mod, order):
    nb, t_len, kb = a.shape
    tk = min(TK, t_len)
    nk = t_len // tk
    cpb = NCHIP // nb
    rows = kb // (2 * cpb)

    def body(ord_ref, a_ref, dx_ref, w_ref, mod_ref, parts_ref, dg_ref, acc, *rs):
        p, k = pl.program_id(0), pl.program_id(1)
        j = ord_ref[p]

        @pl.when((p == 0) & (k == 0))
        def _():
            dg_ref[...] = jnp.zeros_like(dg_ref)

        @pl.when(k == 0)
        def _():
            acc[...] = jnp.zeros_like(acc)

        acc[...] += _dot_tn(a_ref[0], dx_ref[...])

        @pl.when(k == nk - 1)
        def _():
            z = acc[...]
            zg = (z * mod_ref[gate_row:gate_row + 1, :]).astype(BF16)
            dg_ref[0:1, :] += jnp.sum(z * w_ref[...].astype(F32), axis=0, keepdims=True)
            for i in range(cpb):
                q = j * cpb + i
                for half in range(2):
                    rs[0][q, half] = zg[(2 * i + half) * rows:(2 * i + half + 1) * rows]
                _rs2_to_sibling(q, rs)

        if cpb == 1:
            @pl.when((k == nk - 1) & (p > 0))
            def _():
                _rs2_forward(ord_ref[p - 1], parts_ref, rs)

        @pl.when((p == nb - 1) & (k == nk - 1))
        def _():
            for i in range(cpb):
                _rs2_forward(j * cpb + i, parts_ref, rs)
            _rs2_finish(parts_ref, rs)

    return pl.pallas_call(
        body, name=name,
        grid_spec=pltpu.PrefetchScalarGridSpec(
            num_scalar_prefetch=1, grid=(nb, nk),
            in_specs=[pl.BlockSpec((1, tk, kb), lambda p, k, o: (o[p], k, 0)),
                      pl.BlockSpec((tk, D), lambda p, k, o: (k, 0)),
                      pl.BlockSpec((kb, D), lambda p, k, o: (o[p], 0)),
                      pl.BlockSpec((8, D), lambda p, k, o: (0, 0))],
            out_specs=[_ANY, pl.BlockSpec((8, D), lambda p, k, o: (0, 0))],
            scratch_shapes=[pltpu.VMEM((kb, D), F32)] + _rs2_scratch((rows, D))),
        out_shape=[jax.ShapeDtypeStruct((NCHIP, rows, D), BF16), jax.ShapeDtypeStruct((8, D), F32)],
        compiler_params=_cp("arbitrary", "arbitrary"),
    )(order, a, dx, w, mod)


M_WA, M_WB, M_CBIAS, M_BA, M_BX, M_LS = 0, 3, 7, 8, 9, 10


def _mixer_bwd(proj, hl, dmg, prm, wa, wx):
    t_len = proj.shape[1]
    tt = min(TT, t_len)
    nt = t_len // tt
    hb8 = tt // 8

    def rev(i):
        return nt - 1 - i

    def halo(i):
        return jnp.maximum(rev(i) * hb8 - 1, 0)

    def body(proj_ref, ph_ref, hl_ref, hh_ref, dmg_ref, prm_ref, wa_ref, wx_ref,
             dp_ref, sums_ref, gwa_ref, gwx_ref,
             xe, ve, he, u_s, ya_s, rp_s, ip_s, due, dye, drp_s, dip_s, an, gn):
        i = pl.program_id(0)
        t = rev(i)

        @pl.when(i == 0)
        def _():
            sums_ref[...] = jnp.zeros_like(sums_ref)
            gwa_ref[...] = jnp.zeros_like(gwa_ref)
            gwx_ref[...] = jnp.zeros_like(gwx_ref)
            due[tt:tt + 8, :] = jnp.zeros((8, D), F32)
            dye[tt:tt + 8, :] = jnp.zeros((8, D), F32)
            an[...] = jnp.zeros((8, D), F32)
            gn[...] = jnp.zeros((8, D), F32)

        live = (t > 0).astype(F32)
        xe[0:8, :] = ph_ref[3].astype(F32) * live
        ve[0:8, :] = ph_ref[1].astype(F32) * ph_ref[2].astype(F32) * live
        he[0:8, :] = hh_ref[...].astype(F32) * live
        xe[8:8 + tt, :] = proj_ref[3].astype(F32)
        ve[8:8 + tt, :] = proj_ref[1].astype(F32) * proj_ref[2].astype(F32)
        he[8:8 + tt, :] = hl_ref[...].astype(F32)
        u = prm_ref[P_CBIAS:P_CBIAS + 1, :] + prm_ref[P_WB:P_WB + 1, :] * xe[5:5 + tt, :]
        for k in range(1, 4):
            u = u + prm_ref[P_WB + k:P_WB + k + 1, :] * xe[5 + k:5 + k + tt, :]
        u_s[...] = u
        ya = prm_ref[P_WA:P_WA + 1, :] * ve[6:6 + tt, :]
        for k in range(1, 3):
            ya = ya + prm_ref[P_WA + k:P_WA + k + 1, :] * ve[6 + k:6 + k + tt, :]
        ya_s[...] = ya
        ub = u.astype(BF16)
        for h in range(HEADS):
            cs = slice(h * HB, (h + 1) * HB)
            rp_s[:, cs] = _dot(ub[:, cs], wa_ref[h]) + prm_ref[P_BA:P_BA + 1, cs]
            ip_s[:, cs] = _dot(ub[:, cs], wx_ref[h]) + prm_ref[P_BX:P_BX + 1, cs]

        ls_all = _log_sigmoid(prm_ref[P_LAM:P_LAM + 1, :])
        row = lax.broadcasted_iota(jnp.int32, (8, CG), 0)
        nblk = tt // 16

        def blk(ib, carry):
            r0 = pl.multiple_of((nblk - 1 - ib) * 16, 16)
            rows = pl.ds(r0, 16)
            for g in range(D // CG):
                cs = slice(g * CG, (g + 1) * CG)
                ls = ls_all[:, cs]
                dm = dmg_ref[rows, cs].astype(F32)
                cb = proj_ref[0, rows, cs].astype(F32)
                rg = proj_ref[4, rows, cs].astype(F32)
                sga = _sig(proj_ref[5, rows, cs].astype(F32))
                sgb = _sig(proj_ref[6, rows, cs].astype(F32))
                ya0 = ya_s[rows, cs]
                h16 = he[pl.ds(r0 + 8, 16), cs]
                gl, th = _gelu(rg)
                dgl = 0.5 * (1.0 + th) + 0.5 * rg * (1.0 - th * th) * (_GC * (1.0 + 3.0 * 0.044715 * rg * rg))
                y_a = cb * ya0
                y_b = h16 * gl
                dy_a = dm * sga
                dy_b = dm * sgb
                col = lambda s: slice(s * D + g * CG, s * D + (g + 1) * CG)
                dp_ref[rows, col(5)] = (dm * y_a * sga * (1.0 - sga)).astype(BF16)
                dp_ref[rows, col(6)] = (dm * y_b * sgb * (1.0 - sgb)).astype(BF16)
                dp_ref[rows, col(4)] = (dy_b * h16 * dgl).astype(BF16)
                dp_ref[rows, col(0)] = (dy_a * ya0).astype(BF16)
                dye[rows, cs] = dy_a * cb
                dh16 = dy_b * gl

                a_next = an[:, cs]
                g_next = gn[:, cs]
                s_ba = jnp.zeros((8, CG), F32)
                s_bx = jnp.zeros((8, CG), F32)
                s_ls = jnp.zeros((8, CG), F32)
                for sb in (1, 0):
                    rr = r0 + 8 * sb
                    first = (row + (t * tt + rr)) == 0
                    uu = u_s[pl.ds(rr, 8), cs]
                    r, ig, la, a, m2, mult = _lru_gates(rp_s[pl.ds(rr, 8), cs], ip_s[pl.ds(rr, 8), cs], ls, first)
                    ca = jnp.where(row < 7, pltpu.roll(a, 7, 0), a_next)
                    cb_ = dh16[8 * sb:8 * sb + 8, :]
                    for s in (1, 2, 4):
                        a_sh = jnp.where(row < 8 - s, pltpu.roll(ca, 8 - s, 0), 1.0)
                        b_sh = jnp.where(row < 8 - s, pltpu.roll(cb_, 8 - s, 0), 0.0)
                        cb_ = ca * b_sh + cb_
                        ca = ca * a_sh
                    gv = ca * g_next + cb_
                    g_next = jnp.broadcast_to(gv[0:1, :], gv.shape)
                    a_next = jnp.broadcast_to(a[0:1, :], a.shape)
                    hprev = jnp.where(row >= 1, pltpu.roll(he[pl.ds(rr + 8, 8), cs], 1, 0),
                                      pltpu.roll(he[pl.ds(rr, 8), cs], 1, 0))
                    da = gv * hprev
                    dmult = jnp.where(first, 0.0, gv * ig * uu)
                    dla = da * a + jnp.where(m2 > 0.0, dmult * (-(a * a) / mult), 0.0)
                    drp = dla * (LRU_C * ls) * r * (1.0 - r)
                    dip = gv * mult * uu * ig * (1.0 - ig)
                    s_ls = s_ls + dla * (LRU_C * r)
                    s_ba = s_ba + drp
                    s_bx = s_bx + dip
                    drp_s[pl.ds(rr, 8), cs] = drp
                    dip_s[pl.ds(rr, 8), cs] = dip
                    due[pl.ds(rr, 8), cs] = gv * mult * ig
                an[:, cs] = a_next
                gn[:, cs] = g_next
                sums_ref[M_BA:M_BA + 1, cs] += jnp.sum(s_ba, axis=0, keepdims=True)
                sums_ref[M_BX:M_BX + 1, cs] += jnp.sum(s_bx, axis=0, keepdims=True)
                sums_ref[M_LS:M_LS + 1, cs] += jnp.sum(s_ls, axis=0, keepdims=True)
            return carry

        lax.fori_loop(0, nblk, blk, 0)

        drp_b = drp_s[...].astype(BF16)
        dip_b = dip_s[...].astype(BF16)
        for h in range(HEADS):
            cs = slice(h * HB, (h + 1) * HB)
            due[0:tt, cs] += _dot_nt(drp_b[:, cs], wa_ref[h]) + _dot_nt(dip_b[:, cs], wx_ref[h])
            gwa_ref[h] += _dot_tn(ub[:, cs], drp_b[:, cs])
            gwx_ref[h] += _dot_tn(ub[:, cs], dip_b[:, cs])

        du = due[0:tt, :]
        sums_ref[M_CBIAS:M_CBIAS + 1, :] += jnp.sum(du, axis=0, keepdims=True)
        drx = prm_ref[P_WB:P_WB + 1, :] * due[3:3 + tt, :]
        sums_ref[M_WB:M_WB + 1, :] += jnp.sum(du * xe[5:5 + tt, :], axis=0, keepdims=True)
        for k in range(1, 4):
            drx = drx + prm_ref[P_WB + k:P_WB + k + 1, :] * due[3 - k:3 - k + tt, :]
            sums_ref[M_WB + k:M_WB + k + 1, :] += jnp.sum(du * xe[5 + k:5 + k + tt, :], axis=0, keepdims=True)
        dp_ref[:, 3 * D:4 * D] = drx.astype(BF16)
        dya = dye[0:tt, :]
        dv = prm_ref[P_WA:P_WA + 1, :] * dye[2:2 + tt, :]
        sums_ref[M_WA:M_WA + 1, :] += jnp.sum(dya * ve[6:6 + tt, :], axis=0, keepdims=True)
        for k in range(1, 3):
            dv = dv + prm_ref[P_WA + k:P_WA + k + 1, :] * dye[2 - k:2 - k + tt, :]
            sums_ref[M_WA + k:M_WA + k + 1, :] += jnp.sum(dya * ve[6 + k:6 + k + tt, :], axis=0, keepdims=True)
        dp_ref[:, D:2 * D] = (dv * proj_ref[2].astype(F32)).astype(BF16)
        dp_ref[:, 2 * D:3 * D] = (dv * proj_ref[1].astype(F32)).astype(BF16)
        due[tt:tt + 8, :] = due[0:8, :]
        dye[tt:tt + 8, :] = dye[0:8, :]

        @pl.when(i == nt - 1)
        def _():
            sums_ref[M_LS:M_LS + 1, :] = sums_ref[M_LS:M_LS + 1, :] * _sig(-prm_ref[P_LAM:P_LAM + 1, :])

    big = lambda: pltpu.VMEM((tt + 8, D), F32)
    tile = lambda: pltpu.VMEM((tt, D), F32)
    return pl.pallas_call(
        body, name="mixer_bwd", grid=(nt,),
        in_specs=[pl.BlockSpec((7, tt, D), lambda i: (0, rev(i), 0)),
                  pl.BlockSpec((7, 8, D), lambda i: (0, halo(i), 0)),
                  pl.BlockSpec((tt, D), lambda i: (rev(i), 0)),
                  pl.BlockSpec((8, D), lambda i: (halo(i), 0)),
                  pl.BlockSpec((tt, D), lambda i: (rev(i), 0)),
                  pl.BlockSpec((16, D), lambda i: (0, 0)),
                  pl.BlockSpec((HEADS, HB, HB), lambda i: (0, 0, 0)),
                  pl.BlockSpec((HEADS, HB, HB), lambda i: (0, 0, 0))],
        out_specs=[pl.BlockSpec((tt, 7 * D), lambda i: (rev(i), 0)),
                   pl.BlockSpec((16, D), lambda i: (0, 0)),
                   pl.BlockSpec((HEADS, HB, HB), lambda i: (0, 0, 0)),
                   pl.BlockSpec((HEADS, HB, HB), lambda i: (0, 0, 0))],
        out_shape=[jax.ShapeDtypeStruct((t_len, 7 * D), BF16), jax.ShapeDtypeStruct((16, D), F32),
                   jax.ShapeDtypeStruct((HEADS, HB, HB), F32), jax.ShapeDtypeStruct((HEADS, HB, HB), F32)],
        scratch_shapes=[big(), big(), big(), tile(), tile(), tile(), tile(), big(), big(), tile(), tile(),
                        pltpu.VMEM((8, D), F32), pltpu.VMEM((8, D), F32)],
        compiler_params=_cp("arbitrary"),
    )(proj, proj, hl, hl, dmg, prm, wa, wx)


def _in_proj_bwd(dproj, w_in, x, dx1, mod, g_mix):
    t_len = x.shape[0]
    tm = min(TM, t_len)

    def body(dp_ref, w_ref, x_ref, dx1_ref, mod_ref, g_ref, gx_ref, sums_ref, acc):
        @pl.when(pl.program_id(0) == 0)
        def _():
            sums_ref[...] = jnp.zeros_like(sums_ref)

        for rows in _sub_blocks(tm):
            acc[rows, :] = _dot_nt(dp_ref[rows, :], w_ref[...])

        def write(rows, dx):
            gx_ref[rows, :] = dx

        _norm_bwd_rows(tm, 16, acc, x_ref, dx1_ref, mod_ref[1:2, :], g_ref[...], sums_ref, write)

    return pl.pallas_call(
        body, name="in_proj_bwd", grid=(t_len // tm,),
        in_specs=[pl.BlockSpec((tm, 7 * D), lambda i: (i, 0)),
                  _resident((D, 7 * D)),
                  pl.BlockSpec((tm, D), lambda i: (i, 0)), pl.BlockSpec((tm, D), lambda i: (i, 0)),
                  pl.BlockSpec((8, D), lambda i: (0, 0)), pl.BlockSpec((1, D), lambda i: (0, 0))],
        out_specs=[pl.BlockSpec((tm, D), lambda i: (i, 0)), pl.BlockSpec((8, D), lambda i: (0, 0))],
        out_shape=[jax.ShapeDtypeStruct((t_len, D), F32), jax.ShapeDtypeStruct((8, D), F32)],
        scratch_shapes=[pltpu.VMEM((tm, D), F32)],
        compiler_params=_cp("arbitrary"),
    )(dproj, w_in, x, dx1, mod, g_mix)


def _in_wgrad(h, dproj, g_wa, g_wx, order):
    t_len = h.shape[0]
    tk = min(TKI, t_len)
    nk = t_len // tk
    cw = 7 * D // NDEV
    hr = HB // NDEV

    def body(ord_ref, h_ref, d_ref, ga_ref, gx_ref, parts_ref, pa_ref, px_ref, acc, *scr):
        rs, sems = scr[:-6], scr[-6:]
        p, k = pl.program_id(0), pl.program_id(1)

        def head_rows(ref):
            return lambda s: ref.at[:, pl.ds(s * hr, hr), :]

        @pl.when((p == 0) & (k == 0))
        def _():
            for s in range(NDEV):
                _rs_send(head_rows(ga_ref)(s), pa_ref, s, *sems[0:3])
                _rs_send(head_rows(gx_ref)(s), px_ref, s, *sems[3:6])

        @pl.when(k == 0)
        def _():
            acc[...] = jnp.zeros_like(acc)

        acc[...] += _dot_tn(h_ref[...], d_ref[...])

        @pl.when(k == nk - 1)
        def _():
            q = ord_ref[p]
            for half in range(2):
                rs[0][q, half] = acc[:, half * cw:(half + 1) * cw].astype(BF16)
            _rs2_to_sibling(q, rs)

        @pl.when((k == nk - 1) & (p > 0))
        def _():
            _rs2_forward(ord_ref[p - 1], parts_ref, rs)

        @pl.when((p == NCHIP - 1) & (k == nk - 1))
        def _():
            _rs2_forward(ord_ref[p], parts_ref, rs)
            _rs2_finish(parts_ref, rs)
            _rs_finish(head_rows(ga_ref), pa_ref, *sems[0:3])
            _rs_finish(head_rows(gx_ref), px_ref, *sems[3:6])

    return pl.pallas_call(
        body, name="in_wgrad",
        grid_spec=pltpu.PrefetchScalarGridSpec(
            num_scalar_prefetch=1, grid=(NCHIP, nk),
            in_specs=[pl.BlockSpec((tk, D), lambda p, k, o: (k, 0)),
                      pl.BlockSpec((tk, 2 * cw), lambda p, k, o: (k, o[p])), _ANY, _ANY],
            out_specs=[_ANY, _ANY, _ANY],
            scratch_shapes=[pltpu.VMEM((D, 2 * cw), F32)] + _rs2_scratch((D, cw)) + _RS_SEMS * 2),
        out_shape=[jax.ShapeDtypeStruct((NCHIP, D, cw), BF16), jax.ShapeDtypeStruct((NDEV, HEADS, hr, HB), F32),
                   jax.ShapeDtypeStruct((NDEV, HEADS, hr, HB), F32)],
        compiler_params=_cp("arbitrary", "arbitrary"),
    )(order, h, dproj, g_wa, g_wx)


def _ada_fwd(c_all, w_ada, b_cols):
    def body(c_ref, w_ref, b_ref, o_ref):
        cv = c_ref[...]
        o_ref[...] = _dot((cv * _sig(cv)).astype(BF16), w_ref[...].astype(BF16)) + b_ref[...]

    return pl.pallas_call(body, name="ada_fwd", out_shape=jax.ShapeDtypeStruct((16, w_ada.shape[1]), F32),
                          compiler_params=_cp())(c_all, w_ada, b_cols)


def _adam_math(w, g, m, v):
    m = ADAM_B1 * m + (1.0 - ADAM_B1) * g
    v = ADAM_B2 * v + (1.0 - ADAM_B2) * (g * g)
    m_hat = m / (1.0 - ADAM_B1 ** ADAM_STEP)
    v_hat = v / (1.0 - ADAM_B2 ** ADAM_STEP)
    delta = -ADAM_LR * (m_hat / (jnp.sqrt(v_hat) + ADAM_EPS) + ADAM_WD * w)
    return delta, m, v


def _ada_bwd(c_all, dmod_cols, w, m, v):
    rb = 256
    n = w.shape[1]
    nrow = c_all.shape[0]

    def body(c_ref, d_ref, w_ref, m_ref, v_ref, g_ref, dl_ref, nm_ref, nv_ref):
        cv = c_ref[...]
        g = _dot_tn((cv * _sig(cv)).astype(BF16), d_ref[...].astype(BF16))
        g_ref[...] = g
        dl_ref[...], nm_ref[...], nv_ref[...] = _adam_math(w_ref[...], g, m_ref[...], v_ref[...])

    blk = pl.BlockSpec((rb, n), lambda i: (i, 0))
    sds = jax.ShapeDtypeStruct(w.shape, F32)
    return pl.pallas_call(
        body, name="ada_bwd", grid=(D // rb,),
        in_specs=[pl.BlockSpec((nrow, rb), lambda i: (0, i)), pl.BlockSpec((nrow, n), lambda i: (0, 0)), blk, blk, blk],
        out_specs=[blk, blk, blk, blk], out_shape=[sds, sds, sds, sds],
        compiler_params=_cp("parallel"),
    )(c_all, dmod_cols, w, m, v)


def _adam(name, parts, w, m, v):
    p, r, c = parts.shape
    rb = r
    for cand in (256, 128, 64, 32, 16, 8):
        if r % cand == 0 and r >= cand:
            rb = cand
            break

    def body(p_ref, w_ref, m_ref, v_ref, g_ref, dl_ref, nm_ref, nv_ref):
        g = p_ref[0].astype(F32)
        for q in range(1, p):
            g = g + p_ref[q].astype(F32)
        g_ref[...] = g
        dl_ref[...], nm_ref[...], nv_ref[...] = _adam_math(w_ref[...], g, m_ref[...], v_ref[...])

    blk = pl.BlockSpec((rb, c), lambda i: (i, 0))
    sds = jax.ShapeDtypeStruct((r, c), F32)
    return pl.pallas_call(
        body, name=name, grid=(r // rb,),
        in_specs=[pl.BlockSpec((p, rb, c), lambda i: (0, i, 0)), blk, blk, blk],
        out_specs=[blk, blk, blk, blk], out_shape=[sds, sds, sds, sds],
        compiler_params=_cp("parallel"),
    )(parts, w, m, v)


def _my_pos():
    return lax.axis_index("x"), lax.axis_index("y"), lax.axis_index("c")


def _all_gather_small(name, v):
    m_per, n = v.shape

    def body(x_ref, out_ref, send_sems, recv_sems, local_sem):
        x, y, c = _my_pos()
        me, sibling = (x, y, c), (x, y, 1 - c)
        chips = [(1 - x, y), (x, 1 - y), (1 - x, 1 - y)]

        def rows(px, py, pc):
            return out_ref.at[pl.ds((4 * px + 2 * py + pc) * m_per, m_per), :]

        def copy(k, block, to, src=None):
            return pltpu.make_async_remote_copy(
                src_ref=rows(*block) if src is None else src, dst_ref=rows(*block),
                send_sem=send_sems.at[k], recv_sem=recv_sems.at[k], device_id=to, device_id_type=MESH)

        mine = pltpu.make_async_copy(x_ref, rows(*me), local_sem)
        mine.start()
        first = [copy(0, me, sibling, src=x_ref)]
        first += [copy(1 + j, me, (*chip, c), src=x_ref) for j, chip in enumerate(chips)]
        for cp in first:
            cp.start()
        passed = [copy(4 + j, (*chip, c), sibling) for j, chip in enumerate(chips)]
        for j, chip in enumerate(chips):
            copy(1 + j, (*chip, c), me).wait_recv()
            passed[j].start()
        copy(0, sibling, me).wait_recv()
        for j, chip in enumerate(chips):
            copy(4 + j, (*chip, 1 - c), me).wait_recv()
        for cp in first + passed:
            cp.wait_send()
        mine.wait()

    return pl.pallas_call(
        body, name=name, out_shape=jax.ShapeDtypeStruct((NDEV * m_per, n), v.dtype),
        in_specs=[pl.BlockSpec(memory_space=pltpu.VMEM)], out_specs=pl.BlockSpec(memory_space=pltpu.VMEM),
        scratch_shapes=[pltpu.SemaphoreType.DMA((7,)), pltpu.SemaphoreType.DMA((7,)), pltpu.SemaphoreType.DMA],
    )(v)


def _blk_cols(n):
    return lambda ref, b: ref.at[:, pl.ds(pl.multiple_of(b * n, 128), n)]


def _blk_rows(n):
    return lambda ref, b: ref.at[pl.ds(pl.multiple_of(b * n, 8), n), :]


def _blk_lead(ref, b):
    return ref.at[b]


def _blk_heads(ref, b):
    return ref.at[:, pl.ds(pl.multiple_of(b * (HB // NDEV), 8), HB // NDEV), :]


def _ag_phases(ins, outs, slicers, send_sems, recv_sems, local_sems):
    na = len(ins)
    x, y, c = _my_pos()
    me, sibling = (x, y, c), (x, y, 1 - c)
    chips = [(1 - x, y), (x, 1 - y), (1 - x, 1 - y)]

    def copy(a, k, block, to, from_shard=False):
        px, py, pc = block
        dst = slicers[a](outs[a], 4 * px + 2 * py + pc)
        return pltpu.make_async_remote_copy(
            src_ref=ins[a] if from_shard else dst, dst_ref=dst,
            send_sem=send_sems.at[a * 7 + k], recv_sem=recv_sems.at[a * 7 + k], device_id=to, device_id_type=MESH)

    def local(a):
        return pltpu.make_async_copy(ins[a], slicers[a](outs[a], 4 * x + 2 * y + c), local_sems.at[a])

    def firsts(a):
        return [copy(a, 0, me, sibling, True)] + [copy(a, 1 + j, me, (*chip, c), True) for j, chip in enumerate(chips)]

    def start():
        for a in range(na):
            local(a).start()
            for cp in firsts(a):
                cp.start()

    def forward():
        for a in range(na):
            for j, chip in enumerate(chips):
                copy(a, 1 + j, (*chip, c), me).wait_recv()
                copy(a, 4 + j, (*chip, c), sibling).start()

    def finish():
        for a in range(na):
            copy(a, 0, sibling, me).wait_recv()
            for j, chip in enumerate(chips):
                copy(a, 4 + j, (*chip, 1 - c), me).wait_recv()
        for a in range(na):
            for cp in firsts(a) + [copy(a, 4 + j, (*chip, c), sibling) for j, chip in enumerate(chips)]:
                cp.wait_send()
            local(a).wait()

    return start, forward, finish


def _ag_sems(na):
    return [pltpu.SemaphoreType.DMA((7 * na,)), pltpu.SemaphoreType.DMA((7 * na,)), pltpu.SemaphoreType.DMA((na,))]


def _all_gather_weights(shards, fulls, slicers):
    na = len(shards)

    def body(*refs):
        start, forward, finish = _ag_phases(refs[:na], refs[na:2 * na], slicers, *refs[2 * na:])
        start()
        forward()
        finish()

    return pl.pallas_call(
        body, name="gather_weights",
        out_shape=[jax.ShapeDtypeStruct(s, sh.dtype) for s, sh in zip(fulls, shards)],
        in_specs=[_ANY] * na, out_specs=[_ANY] * na, scratch_shapes=_ag_sems(na),
    )(*shards)


def _scatter_grads(grads, shard_shapes, slicers):
    na = len(grads)

    def body(*refs):
        ins, outs = refs[:na], refs[na:2 * na]
        send_sems, recv_sems, local_sems = refs[2 * na:]
        x, y, c = _my_pos()
        me = 4 * x + 2 * y + c
        mine, sent = [], []
        for a in range(na):
            cp = pltpu.make_async_copy(slicers[a](ins[a], me), outs[a].at[me], local_sems.at[a])
            cp.start()
            mine.append(cp)
        rel = [(k >> 2 & 1, k >> 1 & 1, k & 1) for k in range(1, NDEV)]
        for a in range(na):
            for k, (fx, fy, fc) in enumerate(rel):
                px, py, pc = x ^ fx, y ^ fy, c ^ fc
                cp = pltpu.make_async_remote_copy(
                    src_ref=slicers[a](ins[a], 4 * px + 2 * py + pc), dst_ref=outs[a].at[me],
                    send_sem=send_sems.at[a * 7 + k], recv_sem=recv_sems.at[a * 7 + k],
                    device_id=(px, py, pc), device_id_type=MESH)
                cp.start()
                sent.append(cp)
        for a in range(na):
            for k, (fx, fy, fc) in enumerate(rel):
                px, py, pc = x ^ fx, y ^ fy, c ^ fc
                src = 4 * px + 2 * py + pc
                pltpu.make_async_remote_copy(
                    src_ref=slicers[a](ins[a], me), dst_ref=outs[a].at[src],
                    send_sem=send_sems.at[a * 7 + k], recv_sem=recv_sems.at[a * 7 + k],
                    device_id=(px, py, pc), device_id_type=MESH).wait_recv()
        for cp in sent:
            cp.wait_send()
        for cp in mine:
            cp.wait()

    any_spec = pl.BlockSpec(memory_space=pl.ANY)
    return pl.pallas_call(
        body, name="scatter_grads",
        out_shape=[jax.ShapeDtypeStruct((NDEV,) + tuple(s), g.dtype) for s, g in zip(shard_shapes, grads)],
        in_specs=[any_spec] * na, out_specs=[any_spec] * na,
        scratch_shapes=[pltpu.SemaphoreType.DMA((7 * na,)), pltpu.SemaphoreType.DMA((7 * na,)),
                        pltpu.SemaphoreType.DMA((na,))],
    )(*grads)


def _local_step(x, target, mod, g_mix, g_ffn, g_fin, prm, w_in, shards):
    fulls = [(HEADS, HB, HB), (HEADS, HB, HB), (D, D), (NDEV, D, FB), (DFF, D)]
    slicers = [_blk_heads, _blk_heads, _blk_rows(D // NDEV), _blk_lead, _blk_rows(DFF // NDEV)]
    proj, h, (wa, wx, w_out, w_gu, w_down) = _in_proj(x, mod, g_mix, w_in, shards, fulls, slicers)
    w_gu = w_gu.reshape(2, 4, D, FB)
    merged, hl = _mixer_fwd(proj, prm, wa, wx)
    x1, h2 = _out_proj(merged, x, mod, g_ffn, w_out)
    gu, dx2, dx2b, loss, d_gfin = _ffn_fwd(h2, x1, target, mod, g_fin, w_gu, w_down)
    dgu, act, dx1, dx1b, dmg, sums2 = _ffn_bwd(dx2, gu, x1, mod, g_ffn, w_gu, w_down, w_out)
    chip_order = _xor_order(_my_index() >> 1, NCHIP)
    p_wgu = _gu_wgrad(h2, dgu, chip_order)
    p_wdown, d_gt2 = _scaled_wgrad("down_wgrad", act, dx2b, w_down, 5, mod, chip_order)
    p_wout, d_gt1 = _scaled_wgrad("out_wgrad", merged.reshape(1, *merged.shape), dx1b, w_out, 2, mod,
                                  jnp.zeros((1,), jnp.int32))
    dproj, msums, g_wa, g_wx = _mixer_bwd(proj, hl, dmg, prm, wa, wx)
    p_win, p_wa, p_wx = _in_wgrad(h, dproj, g_wa, g_wx, chip_order)
    grad_x, sums1 = _in_proj_bwd(dproj, w_in, x, dx1, mod, g_mix)
    return dict(loss=loss, grad_x=grad_x, d_gfin=d_gfin, sums1=sums1, sums2=sums2, msums=msums,
                d_gt1=d_gt1[0:1], d_gt2=d_gt2[0:1], p_win=p_win, p_wa=p_wa, p_wx=p_wx, p_wout=p_wout, p_wgu=p_wgu,
                p_wdown=p_wdown)


def kernel(x, c, w_ada, b_ada, g_norm_mix, w_in, conv_a_w, conv_b_w, conv_b_bias, w_rg_a, b_rg_a, w_rg_x, b_rg_x, lru_lambda, w_out, g_norm_ffn, w_gate_up, w_down, g_norm_final, loss_target, m_w_ada, m_b_ada, m_g_norm_mix, m_w_in, m_conv_a_w, m_conv_b_w, m_conv_b_bias, m_w_rg_a, m_b_rg_a, m_w_rg_x, m_b_rg_x, m_lru_lambda, m_w_out, m_g_norm_ffn, m_w_gate_up, m_w_down, m_g_norm_final, v_w_ada, v_b_ada, v_g_norm_mix, v_w_in, v_conv_a_w, v_conv_b_w, v_conv_b_bias, v_w_rg_a, v_b_rg_a, v_w_rg_x, v_b_rg_x, v_lru_lambda, v_w_out, v_g_norm_ffn, v_w_gate_up, v_w_down, v_g_norm_final):
    me = 4 * lax.axis_index("x") + 2 * lax.axis_index("y") + lax.axis_index("c")
    ncol = w_ada.shape[2]
    cw = conv_a_w.shape[2]

    pack0 = jnp.concatenate([c, conv_a_w.reshape(1, 3 * cw), conv_b_w.reshape(1, 4 * cw)], axis=1)
    got0 = _all_gather_small("gather_c", jnp.broadcast_to(pack0, (8, pack0.shape[1])))
    got0 = got0.reshape(NDEV, 8, -1)[:, 0, :]
    c_all = got0[:, :D]
    conv_a = got0[:, D:D + 3 * cw].reshape(NDEV, 3, cw).transpose(1, 0, 2).reshape(3, D)
    conv_b = got0[:, D + 3 * cw:].reshape(NDEV, 4, cw).transpose(1, 0, 2).reshape(4, D)

    b_cols = lax.dynamic_slice_in_dim(b_ada, me * ncol, ncol, axis=1)
    c16 = jnp.concatenate([c_all, jnp.zeros((8, D), F32)], axis=0)
    mod_cols = _ada_fwd(c16, w_ada[0], b_cols)[:NDEV]
    got1 = _all_gather_small("gather_mod", mod_cols).reshape(NDEV, NDEV, ncol)
    mod6 = lax.dynamic_index_in_dim(got1, me, axis=1, keepdims=False).reshape(6, D)
    mod = jnp.concatenate([mod6, jnp.zeros((2, D), F32)], axis=0)

    (w_in_f,) = _all_gather_weights([w_in[0].astype(BF16)], [(D, 7 * D)], [_blk_cols(7 * D // NDEV)])
    shards = [w_rg_a[0].astype(BF16), w_rg_x[0].astype(BF16), w_out[0].astype(BF16), w_gate_up[0].astype(BF16),
              w_down[0].astype(BF16)]

    prm = jnp.concatenate([conv_a, conv_b, conv_b_bias, b_rg_a, b_rg_x, lru_lambda, jnp.zeros((5, D), F32)], axis=0)
    r = _local_step(x[0], loss_target[0], mod, g_norm_mix, g_norm_ffn, g_norm_final.reshape(1, D), prm,
                    w_in_f, shards)

    parts = [r["p_win"], r["p_wa"], r["p_wx"], r["p_wout"], r["p_wgu"], r["p_wdown"]]
    big = {}
    for nm, p, w, m, v in (("w_in", parts[0], w_in, m_w_in, v_w_in), ("w_rg_a", parts[1], w_rg_a, m_w_rg_a, v_w_rg_a),
                           ("w_rg_x", parts[2], w_rg_x, m_w_rg_x, v_w_rg_x), ("w_out", parts[3], w_out, m_w_out, v_w_out),
                           ("w_gate_up", parts[4], w_gate_up, m_w_gate_up, v_w_gate_up),
                           ("w_down", parts[5], w_down, m_w_down, v_w_down)):
        two_d = (-1, w.shape[-1])
        outs = _adam("adam_" + nm, p.reshape((p.shape[0],) + w.reshape(two_d).shape), w.reshape(two_d), m.reshape(two_d),
                     v.reshape(two_d))
        big[nm] = [o.reshape(w.shape) for o in outs]

    small = jnp.concatenate([
        r["sums1"][S_SH:S_SH + 1], r["sums1"][S_SC:S_SC + 1], r["d_gt1"],
        r["sums2"][S_SH:S_SH + 1], r["sums2"][S_SC:S_SC + 1], r["d_gt2"],
        r["sums1"][S_G:S_G + 1],
        r["msums"][M_CBIAS:M_CBIAS + 1], r["msums"][M_BA:M_BA + 1], r["msums"][M_BX:M_BX + 1],
        r["msums"][M_LS:M_LS + 1],
        r["sums2"][S_G:S_G + 1], r["d_gfin"],
        r["msums"][M_WA:M_WA + 3], r["msums"][M_WB:M_WB + 4],
        jnp.zeros((4, D), F32)], axis=0)
    got2 = _all_gather_small("gather_small", small).reshape(NDEV, 24, D)

    rep_w = jnp.concatenate([b_ada.reshape(6, D), g_norm_mix, conv_b_bias, b_rg_a, b_rg_x, lru_lambda, g_norm_ffn,
                             g_norm_final.reshape(1, D), jnp.zeros((3, D), F32)], axis=0)
    rep_m = jnp.concatenate([m_b_ada.reshape(6, D), m_g_norm_mix, m_conv_b_bias, m_b_rg_a, m_b_rg_x, m_lru_lambda,
                             m_g_norm_ffn, m_g_norm_final.reshape(1, D), jnp.zeros((3, D), F32)], axis=0)
    rep_v = jnp.concatenate([v_b_ada.reshape(6, D), v_g_norm_mix, v_conv_b_bias, v_b_rg_a, v_b_rg_x, v_lru_lambda,
                             v_g_norm_ffn, v_g_norm_final.reshape(1, D), jnp.ones((3, D), F32)], axis=0)
    rep = _adam("adam_rep", got2[:, :16, :], rep_w, rep_m, rep_v)

    conv_parts = lax.dynamic_slice_in_dim(got2[:, 13:21, :], me * cw, cw, axis=2)
    cv_w = jnp.concatenate([conv_a_w[0], conv_b_w[0], jnp.zeros((1, cw), F32)], axis=0)
    cv_m = jnp.concatenate([m_conv_a_w[0], m_conv_b_w[0], jnp.zeros((1, cw), F32)], axis=0)
    cv_v = jnp.concatenate([v_conv_a_w[0], v_conv_b_w[0], jnp.ones((1, cw), F32)], axis=0)
    cvo = _adam("adam_conv", conv_parts, cv_w, cv_m, cv_v)

    dmod_cols = lax.dynamic_slice_in_dim(got2[:, :6, :].reshape(NDEV, 6 * D), me * ncol, ncol, axis=1)
    dmod16 = jnp.concatenate([dmod_cols, jnp.zeros((8, ncol), F32)], axis=0)
    ada = _ada_bwd(c16, dmod16, w_ada[0], m_w_ada[0], v_w_ada[0])

    loss = lax.psum(r["loss"][0, 0], AXES)

    def pick(q):
        one = lambda i: rep[q][i:i + 1]
        return [ada[q].reshape(w_ada.shape), rep[q][0:6].reshape(b_ada.shape), one(6), big["w_in"][q],
                cvo[q][0:3].reshape(conv_a_w.shape), cvo[q][3:7].reshape(conv_b_w.shape), one(7),
                big["w_rg_a"][q], one(8), big["w_rg_x"][q], one(9), one(10), big["w_out"][q], one(11),
                big["w_gate_up"][q], big["w_down"][q], rep[q][12]]

    return (loss, r["grad_x"].reshape(x.shape), *pick(0), *pick(1), *pick(2), *pick(3))
```

```python
import functools
import math

import jax
import jax.numpy as jnp
from jax import lax
from jax.experimental import pallas as pl
from jax.experimental.pallas import tpu as pltpu

F32 = jnp.float32
BF16 = jnp.bfloat16

D = 1024
DFF = 2816
NDEV = 8
HEADS = 4
HB = D // HEADS
FB = DFF // 4
EPS = 1e-6
LRU_C = 8.0
ADAM_LR, ADAM_B1, ADAM_B2, ADAM_EPS, ADAM_WD, ADAM_STEP = 0.001, 0.9, 0.999, 1e-08, 0.01, 10

VMEM_LIMIT = 56 * 1024 * 1024
TM = 512
TMI = 1024
TMF = 256
TK = 2048
TKI = 1024
SUB = 256
UNROLL = 4
TT = 256
CG = 256
MESH = pl.DeviceIdType.MESH
AXES = ("x", "y", "c")


def _cp(*sem):
    return pltpu.CompilerParams(dimension_semantics=sem, vmem_limit_bytes=VMEM_LIMIT)


def _sig(x):
    return 1.0 / (1.0 + jnp.exp(-x))


def _log_sigmoid(x):
    z = jnp.exp(-jnp.abs(x))
    u = 1.0 + z
    d = u - 1.0
    l1p = jnp.where(d == 0.0, z, jnp.log(u) * (z / jnp.where(d == 0.0, 1.0, d)))
    return -(jnp.maximum(-x, 0.0) + l1p)


def _neg_expm1(x):
    p = x * (1.0 + x * 0.5 * (1.0 + x * (1.0 / 3.0) * (1.0 + x * 0.25 * (1.0 + x * 0.2 * (1.0 + x * (1.0 / 6.0))))))
    return jnp.where(x > -0.25, -p, 1.0 - jnp.exp(x))


_GC = math.sqrt(2.0 / math.pi)


def _gelu(x):
    t = jnp.tanh(_GC * (x + 0.044715 * x * x * x))
    return 0.5 * x * (1.0 + t), t


def _dot(a, b):
    return jnp.dot(a, b, preferred_element_type=F32)


def _dot_nt(a, b):
    return lax.dot_general(a, b, (((1,), (1,)), ((), ())), preferred_element_type=F32)


def _dot_tn(a, b):
    return lax.dot_general(a, b, (((0,), (0,)), ((), ())), preferred_element_type=F32)


def _resident(shape):
    return pl.BlockSpec(shape, lambda *_: (0,) * len(shape), pipeline_mode=pl.Buffered(1))


def _sub_blocks(n_rows):
    step = min(SUB, n_rows)
    return [slice(r, r + step) for r in range(0, n_rows, step)]


def _fold8(v):
    return v[0:8] + v[8:16]


def _pj(ref, s, rows=slice(None), cols=slice(0, D)):
    return ref[rows, s * D + cols.start:s * D + cols.stop]


def _in_proj(x, mod, g_mix, w_shard, order, shards, fulls, slicers):
    t_len = x.shape[0]
    tm = min(TMI, t_len)
    ni = t_len // tm
    na = len(shards)
    cw = 7 * D // NDEV
    rc = 32

    def body(ord_ref, x_ref, mod_ref, g_ref, wsh_ref, *rest):
        ins, (proj_ref, h_ref, wfull_ref), outs = rest[:na], rest[na:na + 3], rest[na + 3:2 * na + 3]
        h_scr, w_scr, wsend, wrecv, wlocal, wout = rest[2 * na + 3:2 * na + 9]
        start, forward, finish = _ag_phases(ins, outs, slicers, *rest[2 * na + 9:])
        p, i = pl.program_id(0), pl.program_id(1)
        x_, y_, c = _my_pos()
        me, sibling = (x_, y_, c), (x_, y_, 1 - c)
        chip_at = [None, (x_, 1 - y_), (1 - x_, y_), (1 - x_, 1 - y_)]

        def cols(px, py, pc):
            return w_scr.at[:, pl.ds(pl.multiple_of((4 * px + 2 * py + pc) * cw, 128), cw)]

        def wcopy(k, block, to, from_shard=False):
            dst = cols(*block)
            return pltpu.make_async_remote_copy(src_ref=wsh_ref if from_shard else dst, dst_ref=dst,
                                                send_sem=wsend.at[k], recv_sem=wrecv.at[k], device_id=to,
                                                device_id_type=MESH)

        own_local = pltpu.make_async_copy(wsh_ref, cols(*me), wlocal)
        to_hbm = pltpu.make_async_copy(w_scr, wfull_ref, wout)

        @pl.when((p == 0) & (i == 0))
        def _():
            own_local.start()
            wcopy(0, me, sibling, True).start()
            for q in (1, 2, 3):
                wcopy(q, me, (*chip_at[q], c), True).start()
            start()
            own_local.wait()
            wcopy(0, sibling, me).wait_recv()

        for q in (1, 2, 3):
            @pl.when((p == q - 1) & (i == ni // 2))
            def _():
                wcopy(q, (*chip_at[q], c), me).wait_recv()
                wcopy(3 + q, (*chip_at[q], c), sibling).start()

            @pl.when((p == q) & (i == 0))
            def _():
                wcopy(3 + q, (*chip_at[q], 1 - c), me).wait_recv()

        @pl.when((p == 1) & (i == 0))
        def _():
            forward()

        @pl.when((p == NCHIP - 1) & (i == 0))
        def _():
            to_hbm.start()

        gs = g_ref[...] * (1.0 + mod_ref[1:2, :])
        sh = mod_ref[0:1, :]

        def chunk(j, carry):
            rows = pl.ds(pl.multiple_of(j * rc, rc), rc)
            xv = x_ref[rows, :]
            r = lax.rsqrt(jnp.mean(xv * xv, axis=-1, keepdims=True) + EPS)
            h_scr[rows, :] = (xv * r * gs + sh).astype(BF16)
            return carry

        lax.fori_loop(0, tm // rc, chunk, 0, unroll=UNROLL)

        @pl.when(p == 0)
        def _():
            h_ref[...] = h_scr[...]

        wcols = pl.ds(pl.multiple_of(ord_ref[p] * (2 * cw), 128), 2 * cw)
        proj_ref[...] = _dot(h_scr[...], w_scr[:, wcols]).astype(BF16)

        @pl.when((p == NCHIP - 1) & (i == ni - 1))
        def _():
            wcopy(0, me, sibling, True).wait_send()
            for q in (1, 2, 3):
                wcopy(q, me, (*chip_at[q], c), True).wait_send()
                wcopy(3 + q, (*chip_at[q], c), sibling).wait_send()
            finish()
            to_hbm.wait()

    res = pl.pallas_call(
        body, name="in_proj",
        grid_spec=pltpu.PrefetchScalarGridSpec(
            num_scalar_prefetch=1, grid=(NCHIP, ni),
            in_specs=[pl.BlockSpec((tm, D), lambda p, i, o: (i, 0)),
                      pl.BlockSpec((8, D), lambda p, i, o: (0, 0)),
                      pl.BlockSpec((1, D), lambda p, i, o: (0, 0))] + [_ANY] * (1 + na),
            out_specs=[pl.BlockSpec((tm, 2 * cw), lambda p, i, o: (i, o[p])),
                       pl.BlockSpec((tm, D), lambda p, i, o: (jnp.where(p == 0, i, ni - 1), 0))]
            + [_ANY] * (1 + na),
            scratch_shapes=[pltpu.VMEM((tm, D), BF16), pltpu.VMEM((D, 7 * D), BF16),
                            pltpu.SemaphoreType.DMA((7,)), pltpu.SemaphoreType.DMA((7,)),
                            pltpu.SemaphoreType.DMA, pltpu.SemaphoreType.DMA] + _ag_sems(na)),
        out_shape=[jax.ShapeDtypeStruct((t_len, 7 * D), BF16), jax.ShapeDtypeStruct((t_len, D), BF16),
                   jax.ShapeDtypeStruct((D, 7 * D), BF16)]
        + [jax.ShapeDtypeStruct(f, sh.dtype) for f, sh in zip(fulls, shards)],
        compiler_params=_cp("arbitrary", "arbitrary"),
    )(order, x, mod, g_mix, w_shard, *shards)
    return res[0], res[1], res[2], res[3:]


P_WA, P_WB, P_CBIAS, P_BA, P_BX, P_LAM = 0, 3, 7, 8, 9, 10


def _lru_gates(rp, ip, ls, first_row):
    r = _sig(rp)
    ig = _sig(ip)
    la = LRU_C * r * ls
    a = jnp.exp(la)
    m2 = _neg_expm1(2.0 * la)
    mult = jnp.where(first_row, 1.0, jnp.sqrt(jnp.maximum(m2, 0.0)))
    return r, ig, la, a, m2, mult


def _mixer_fwd(proj, prm, wa, wx):
    t_len = proj.shape[0]
    tt = min(TT, t_len)

    def body(proj_ref, prm_ref, wa_ref, wx_ref, mg_ref, hl_ref, xe, ve, hc, u_s, ya_s, rp_s, ip_s):
        t = pl.program_id(0)

        @pl.when(t == 0)
        def _():
            xe[0:8, :] = jnp.zeros((8, D), F32)
            ve[0:8, :] = jnp.zeros((8, D), F32)
            hc[...] = jnp.zeros((8, D), F32)

        xe[8:8 + tt, :] = _pj(proj_ref, 3).astype(F32)
        ve[8:8 + tt, :] = _pj(proj_ref, 1).astype(F32) * _pj(proj_ref, 2).astype(F32)
        u = prm_ref[P_CBIAS:P_CBIAS + 1, :] + prm_ref[P_WB:P_WB + 1, :] * xe[5:5 + tt, :]
        for k in range(1, 4):
            u = u + prm_ref[P_WB + k:P_WB + k + 1, :] * xe[5 + k:5 + k + tt, :]
        u_s[...] = u
        ya = prm_ref[P_WA:P_WA + 1, :] * ve[6:6 + tt, :]
        for k in range(1, 3):
            ya = ya + prm_ref[P_WA + k:P_WA + k + 1, :] * ve[6 + k:6 + k + tt, :]
        ya_s[...] = ya
        xe[0:8, :] = xe[tt:tt + 8, :]
        ve[0:8, :] = ve[tt:tt + 8, :]

        ub = u.astype(BF16)
        for h in range(HEADS):
            cs = slice(h * HB, (h + 1) * HB)
            rp_s[:, cs] = _dot(ub[:, cs], wa_ref[h]) + prm_ref[P_BA:P_BA + 1, cs]
            ip_s[:, cs] = _dot(ub[:, cs], wx_ref[h]) + prm_ref[P_BX:P_BX + 1, cs]

        ls_all = _log_sigmoid(prm_ref[P_LAM:P_LAM + 1, :])
        row = lax.broadcasted_iota(jnp.int32, (8, CG), 0)

        def blk(i, carry):
            r0 = pl.multiple_of(i * 16, 16)
            for g in range(D // CG):
                cs = slice(g * CG, (g + 1) * CG)
                ls = ls_all[:, cs]
                hprev = hc[:, cs]
                hs = []
                for sb in range(2):
                    rr = r0 + 8 * sb
                    first = (row + (t * tt + rr)) == 0
                    _, ig, _, a, _, mult = _lru_gates(rp_s[pl.ds(rr, 8), cs], ip_s[pl.ds(rr, 8), cs], ls, first)
                    b = mult * (ig * u_s[pl.ds(rr, 8), cs])
                    for s in (1, 2, 4):
                        a_sh = jnp.where(row >= s, pltpu.roll(a, s, 0), 1.0)
                        b_sh = jnp.where(row >= s, pltpu.roll(b, s, 0), 0.0)
                        b = a * b_sh + b
                        a = a * a_sh
                    hv = a * hprev + b
                    hprev = jnp.broadcast_to(hv[7:8, :], hv.shape)
                    hs.append(hv)
                hc[:, cs] = hprev
                h16 = jnp.concatenate(hs, axis=0)
                rows = pl.ds(r0, 16)
                gl, _ = _gelu(_pj(proj_ref, 4, rows, cs).astype(F32))
                y_b = h16 * gl
                y_a = _pj(proj_ref, 0, rows, cs).astype(F32) * ya_s[rows, cs]
                mg = (_sig(_pj(proj_ref, 5, rows, cs).astype(F32)) * y_a
                      + _sig(_pj(proj_ref, 6, rows, cs).astype(F32)) * y_b)
                mg_ref[rows, cs] = mg.astype(BF16)
                hl_ref[rows, cs] = h16.astype(BF16)
            return carry

        lax.fori_loop(0, tt // 16, blk, 0)

    return pl.pallas_call(
        body, name="mixer_fwd", grid=(t_len // tt,),
        in_specs=[pl.BlockSpec((tt, 7 * D), lambda t: (t, 0)),
                  pl.BlockSpec((16, D), lambda t: (0, 0)),
                  pl.BlockSpec((HEADS, HB, HB), lambda t: (0, 0, 0)),
                  pl.BlockSpec((HEADS, HB, HB), lambda t: (0, 0, 0))],
        out_specs=[pl.BlockSpec((tt, D), lambda t: (t, 0)), pl.BlockSpec((tt, D), lambda t: (t, 0))],
        out_shape=[jax.ShapeDtypeStruct((t_len, D), BF16), jax.ShapeDtypeStruct((t_len, D), BF16)],
        scratch_shapes=[pltpu.VMEM((tt + 8, D), F32), pltpu.VMEM((tt + 8, D), F32), pltpu.VMEM((8, D), F32),
                        pltpu.VMEM((tt, D), F32), pltpu.VMEM((tt, D), F32), pltpu.VMEM((tt, D), F32),
                        pltpu.VMEM((tt, D), F32)],
        compiler_params=_cp("arbitrary"),
    )(proj, prm, wa, wx)


def _out_proj(merged, x, mod, g_ffn, w_out):
    t_len = x.shape[0]
    tm = min(TM, t_len)

    def body(mg_ref, x_ref, mod_ref, g_ref, w_ref, x1_ref, h2_ref):
        gt1 = mod_ref[2:3, :]
        for rows in _sub_blocks(tm):
            x1_ref[rows, :] = x_ref[rows, :] + gt1 * _dot(mg_ref[rows, :], w_ref[...])
        gs = g_ref[...] * (1.0 + mod_ref[4:5, :])
        sh = mod_ref[3:4, :]

        def chunk(c, carry):
            rows = pl.ds(pl.multiple_of(c * 16, 16), 16)
            x1 = x1_ref[rows, :]
            r = lax.rsqrt(jnp.mean(x1 * x1, axis=-1, keepdims=True) + EPS)
            h2_ref[rows, :] = (x1 * r * gs + sh).astype(BF16)
            return carry

        lax.fori_loop(0, tm // 16, chunk, 0, unroll=UNROLL)

    return pl.pallas_call(
        body, name="out_proj", grid=(t_len // tm,),
        in_specs=[pl.BlockSpec((tm, D), lambda i: (i, 0)), pl.BlockSpec((tm, D), lambda i: (i, 0)),
                  pl.BlockSpec((8, D), lambda i: (0, 0)), pl.BlockSpec((1, D), lambda i: (0, 0)),
                  pl.BlockSpec((D, D), lambda i: (0, 0))],
        out_specs=[pl.BlockSpec((tm, D), lambda i: (i, 0)), pl.BlockSpec((tm, D), lambda i: (i, 0))],
        out_shape=[jax.ShapeDtypeStruct((t_len, D), F32), jax.ShapeDtypeStruct((t_len, D), BF16)],
        compiler_params=_cp("parallel"),
    )(merged, x, mod, g_ffn, w_out)


def _ffn_fwd(h2, x1, target, mod, g_fin, w_gu, w_down):
    t_len = x1.shape[0]
    tm = min(TMF, t_len)
    assert tm % (16 * UNROLL) == 0

    def body(h2_ref, x1_ref, tg_ref, mod_ref, g_ref, wgu_ref, wd_ref, gu_ref, dx2_ref, dx2b_ref, loss_ref, dg_ref, acc):
        @pl.when(pl.program_id(0) == 0)
        def _():
            loss_ref[...] = jnp.zeros_like(loss_ref)
            dg_ref[...] = jnp.zeros_like(dg_ref)

        hb = h2_ref[...]
        ffn = None
        for j in range(4):
            gate = _dot_nt(hb, wgu_ref[0, j])
            up = _dot_nt(hb, wgu_ref[1, j])
            gu_ref[0, j] = gate.astype(BF16)
            gu_ref[1, j] = up.astype(BF16)
            act = (gate * _sig(gate) * up).astype(BF16)
            part = _dot(act, wd_ref[j * FB:(j + 1) * FB, :])
            ffn = part if ffn is None else ffn + part
        acc[...] = ffn

        gt2 = mod_ref[5:6, :]
        gf = g_ref[...]

        def chunk(c, carry):
            s_loss, s_dg = carry
            for u in range(UNROLL):
                rows = pl.ds(pl.multiple_of(c * (16 * UNROLL), 16) + 16 * u, 16)
                x2 = x1_ref[rows, :] + gt2 * acc[rows, :]
                r = lax.rsqrt(jnp.mean(x2 * x2, axis=-1, keepdims=True) + EPS)
                xn = x2 * r
                diff = xn * gf - tg_ref[rows, :]
                dy = diff * (1.0 / D)
                dxn = dy * gf
                dx2 = r * (dxn - xn * jnp.mean(dxn * xn, axis=-1, keepdims=True))
                dx2_ref[rows, :] = dx2
                dx2b_ref[rows, :] = dx2.astype(BF16)
                s_loss, s_dg = s_loss + _fold8(diff * diff), s_dg + _fold8(dy * xn)
            return s_loss, s_dg

        zero = jnp.zeros((8, D), F32)
        s_loss, s_dg = lax.fori_loop(0, tm // (16 * UNROLL), chunk, (zero, zero))
        loss_ref[...] += jnp.sum(s_loss) * (0.5 / D)
        dg_ref[...] += jnp.sum(s_dg, axis=0, keepdims=True)

    row = pl.BlockSpec((tm, D), lambda i: (i, 0))
    return pl.pallas_call(
        body, name="ffn_fwd", grid=(t_len // tm,),
        in_specs=[row, row, row, pl.BlockSpec((8, D), lambda i: (0, 0)), pl.BlockSpec((1, D), lambda i: (0, 0)),
                  _resident((2, 4, FB, D)), _resident((DFF, D))],
        out_specs=[pl.BlockSpec((2, 4, tm, FB), lambda i: (0, 0, i, 0)), row, row,
                   pl.BlockSpec((1, 128), lambda i: (0, 0)), pl.BlockSpec((1, D), lambda i: (0, 0))],
        out_shape=[jax.ShapeDtypeStruct((2, 4, t_len, FB), BF16), jax.ShapeDtypeStruct((t_len, D), F32),
                   jax.ShapeDtypeStruct((t_len, D), BF16),
                   jax.ShapeDtypeStruct((1, 128), F32), jax.ShapeDtypeStruct((1, D), F32)],
        scratch_shapes=[pltpu.VMEM((tm, D), F32)],
        compiler_params=_cp("arbitrary"),
    )(h2, x1, target, mod, g_fin, w_gu, w_down)


S_SH, S_SC, S_G = 0, 1, 2


def _norm_bwd_rows(n_rows, rc, dh_ref, x_ref, dres_ref, scale, gain, sums_ref, write):
    assert n_rows % (rc * UNROLL) == 0
    gs = 1.0 + scale
    fold = _fold8 if rc == 16 else (lambda v: v)

    def chunk(c, carry):
        s_sh, s_sc, s_g = carry
        for u in range(UNROLL):
            rows = pl.ds(pl.multiple_of(c * (rc * UNROLL), rc) + rc * u, rc)
            dh = dh_ref[rows, :]
            xv = x_ref[rows, :]
            r = lax.rsqrt(jnp.mean(xv * xv, axis=-1, keepdims=True) + EPS)
            xn = xv * r
            dhn = dh * gs
            dxn = dhn * gain
            write(rows, dres_ref[rows, :] + r * (dxn - xn * jnp.mean(dxn * xn, axis=-1, keepdims=True)))
            s_sh, s_sc, s_g = s_sh + fold(dh), s_sc + fold(dh * (xn * gain)), s_g + fold(dhn * xn)
        return s_sh, s_sc, s_g

    zero = jnp.zeros((8, D), F32)
    s_sh, s_sc, s_g = lax.fori_loop(0, n_rows // (rc * UNROLL), chunk, (zero, zero, zero))
    sums_ref[S_SH:S_SH + 1, :] += jnp.sum(s_sh, axis=0, keepdims=True)
    sums_ref[S_SC:S_SC + 1, :] += jnp.sum(s_sc, axis=0, keepdims=True)
    sums_ref[S_G:S_G + 1, :] += jnp.sum(s_g, axis=0, keepdims=True)


def _ffn_bwd(dx2, gu, x1, mod, g_ffn, w_gu, w_down, w_out):
    t_len = x1.shape[0]
    tm = min(TMF, t_len)

    def body(dx2_ref, gu_ref, x1_ref, mod_ref, g_ref, wgu_ref, wd_ref, wo_ref,
             dgu_ref, act_ref, dx1_ref, dx1b_ref, dmg_ref, sums_ref, acc, dmo):
        @pl.when(pl.program_id(0) == 0)
        def _():
            sums_ref[...] = jnp.zeros_like(sums_ref)

        dffn = (dx2_ref[...] * mod_ref[5:6, :]).astype(BF16)
        dh2 = None
        for j in range(4):
            dact = _dot_nt(dffn, wd_ref[j * FB:(j + 1) * FB, :])
            gate = gu_ref[0, j].astype(F32)
            up = gu_ref[1, j].astype(F32)
            sg = _sig(gate)
            silu = gate * sg
            act_ref[j] = (silu * up).astype(BF16)
            dgate = (dact * up * (sg * (1.0 + gate * (1.0 - sg)))).astype(BF16)
            dup = (dact * silu).astype(BF16)
            dgu_ref[0, j] = dgate
            dgu_ref[1, j] = dup
            part = _dot(dgate, wgu_ref[0, j]) + _dot(dup, wgu_ref[1, j])
            dh2 = part if dh2 is None else dh2 + part
        acc[...] = dh2

        gt1 = mod_ref[2:3, :]

        def write(rows, dx1):
            dx1_ref[rows, :] = dx1
            dx1b_ref[rows, :] = dx1.astype(BF16)
            dmo[rows, :] = (dx1 * gt1).astype(BF16)

        _norm_bwd_rows(tm, 16, acc, x1_ref, dx2_ref, mod_ref[4:5, :], g_ref[...], sums_ref, write)
        dmg_ref[...] = _dot_nt(dmo[...], wo_ref[...]).astype(BF16)

    row = pl.BlockSpec((tm, D), lambda i: (i, 0))
    return pl.pallas_call(
        body, name="ffn_bwd", grid=(t_len // tm,),
        in_specs=[row, pl.BlockSpec((2, 4, tm, FB), lambda i: (0, 0, i, 0)), row,
                  pl.BlockSpec((8, D), lambda i: (0, 0)), pl.BlockSpec((1, D), lambda i: (0, 0)),
                  _resident((2, 4, FB, D)), _resident((DFF, D)), _resident((D, D))],
        out_specs=[pl.BlockSpec((2, 4, tm, FB), lambda i: (0, 0, i, 0)),
                   pl.BlockSpec((4, tm, FB), lambda i: (0, i, 0)), row, row, row,
                   pl.BlockSpec((8, D), lambda i: (0, 0))],
        out_shape=[jax.ShapeDtypeStruct((2, 4, t_len, FB), BF16), jax.ShapeDtypeStruct((4, t_len, FB), BF16),
                   jax.ShapeDtypeStruct((t_len, D), F32), jax.ShapeDtypeStruct((t_len, D), BF16),
                   jax.ShapeDtypeStruct((t_len, D), BF16), jax.ShapeDtypeStruct((8, D), F32)],
        scratch_shapes=[pltpu.VMEM((tm, D), F32), pltpu.VMEM((tm, D), BF16)],
        compiler_params=_cp("arbitrary"),
    )(dx2, gu, x1, mod, g_ffn, w_gu, w_down, w_out)


def _my_pos():
    return lax.axis_index("x"), lax.axis_index("y"), lax.axis_index("c")


def _my_index():
    x, y, c = _my_pos()
    return 4 * x + 2 * y + c


def _device_of(b):
    return (b >> 2) & 1, (b >> 1) & 1, b & 1


def _rs_send(src, parts_ref, b, send_sems, recv_sems, local_sem):
    me = _my_index()
    dst = parts_ref.at[me]

    @pl.when(b == me)
    def _():
        pltpu.make_async_copy(src, dst, local_sem).start()

    @pl.when(b != me)
    def _():
        pltpu.make_async_remote_copy(src_ref=src, dst_ref=dst, send_sem=send_sems.at[b], recv_sem=recv_sems.at[me],
                                     device_id=_device_of(b), device_id_type=MESH).start()


def _rs_finish(src_of, parts_ref, send_sems, recv_sems, local_sem):
    me = _my_index()
    for s in range(NDEV):
        @pl.when(s != me)
        def _():
            cp = pltpu.make_async_remote_copy(src_ref=src_of(s), dst_ref=parts_ref.at[s], send_sem=send_sems.at[s],
                                              recv_sem=recv_sems.at[s], device_id=_device_of(s), device_id_type=MESH)
            cp.wait_send()
            cp.wait_recv()

        @pl.when(s == me)
        def _():
            pltpu.make_async_copy(src_of(s), parts_ref.at[s], local_sem).wait()


_RS_SEMS = [pltpu.SemaphoreType.DMA((NDEV,)), pltpu.SemaphoreType.DMA((NDEV,)), pltpu.SemaphoreType.DMA]
_ANY = pl.BlockSpec(memory_space=pl.ANY)


def _xor_order(me, n):
    return (me ^ (n - 1 - jnp.arange(n, dtype=jnp.int32))).astype(jnp.int32)


NCHIP = NDEV // 2


def _rs2_scratch(half_shape):
    blocks = lambda *lead: pltpu.VMEM(lead + tuple(half_shape), BF16)
    return [blocks(NCHIP, 2), blocks(NCHIP)] + [pltpu.SemaphoreType.DMA((NCHIP,))] * 4 + [pltpu.SemaphoreType.DMA]


def _rs2_to_sibling(q, rs):
    stage, from_sib, d_send, d_recv = rs[:4]
    x, y, c = _my_pos()
    pltpu.make_async_remote_copy(src_ref=stage.at[q, 1 - c], dst_ref=from_sib.at[q], send_sem=d_send.at[q],
                                 recv_sem=d_recv.at[q], device_id=(x, y, 1 - c), device_id_type=MESH).start()


def _rs2_forward(q, parts_ref, rs):
    stage, chip_sum, d_send, d_recv, i_send, i_recv, local_sem = rs
    x, y, c = _my_pos()
    my_chip = 2 * x + y
    pltpu.make_async_remote_copy(src_ref=stage.at[q, c], dst_ref=chip_sum.at[q], send_sem=d_send.at[q],
                                 recv_sem=d_recv.at[q], device_id=(x, y, 1 - c), device_id_type=MESH).wait_recv()
    chip_sum[q] = (stage[q, c].astype(F32) + chip_sum[q].astype(F32)).astype(BF16)

    @pl.when(q == my_chip)
    def _():
        pltpu.make_async_copy(chip_sum.at[q], parts_ref.at[my_chip], local_sem).start()

    @pl.when(q != my_chip)
    def _():
        pltpu.make_async_remote_copy(src_ref=chip_sum.at[q], dst_ref=parts_ref.at[my_chip], send_sem=i_send.at[q],
                                     recv_sem=i_recv.at[my_chip], device_id=((q >> 1) & 1, q & 1, c),
                                     device_id_type=MESH).start()


def _rs2_finish(parts_ref, rs):
    stage, chip_sum, d_send, d_recv, i_send, i_recv, local_sem = rs
    x, y, c = _my_pos()
    my_chip = 2 * x + y
    for q in range(NCHIP):
        pltpu.make_async_remote_copy(src_ref=stage.at[q, 1 - c], dst_ref=chip_sum.at[q], send_sem=d_send.at[q],
                                     recv_sem=d_recv.at[q], device_id=(x, y, 1 - c), device_id_type=MESH).wait_send()

        @pl.when(q != my_chip)
        def _():
            cp = pltpu.make_async_remote_copy(src_ref=chip_sum.at[q], dst_ref=parts_ref.at[q], send_sem=i_send.at[q],
                                              recv_sem=i_recv.at[q], device_id=((q >> 1) & 1, q & 1, c),
                                              device_id_type=MESH)
            cp.wait_send()
            cp.wait_recv()

        @pl.when(q == my_chip)
        def _():
            pltpu.make_async_copy(chip_sum.at[q], parts_ref.at[q], local_sem).wait()


def _gu_wgrad(h2, dgu, order):
    t_len = h2.shape[0]
    tk = min(TK, t_len)
    nk = t_len // tk

    def body(ord_ref, h_ref, d_ref, parts_ref, acc, *rs):
        p, k = pl.program_id(0), pl.program_id(1)

        @pl.when(k == 0)
        def _():
            acc[...] = jnp.zeros_like(acc)

        hb = h_ref[...]
        for half in range(2):
            acc[half] += _dot_tn(d_ref[0, half], hb)

        @pl.when(k == nk - 1)
        def _():
            q = ord_ref[p]
            rs[0][q] = acc[...].astype(BF16)
            _rs2_to_sibling(q, rs)

        @pl.when((k == nk - 1) & (p > 0))
        def _():
            _rs2_forward(ord_ref[p - 1], parts_ref, rs)

        @pl.when((p == NCHIP - 1) & (k == nk - 1))
        def _():
            _rs2_forward(ord_ref[p], parts_ref, rs)
            _rs2_finish(parts_ref, rs)

    return pl.pallas_call(
        body, name="gu_wgrad",
        grid_spec=pltpu.PrefetchScalarGridSpec(
            num_scalar_prefetch=1, grid=(NCHIP, nk),
            in_specs=[pl.BlockSpec((tk, D), lambda p, k, o: (k, 0)),
                      pl.BlockSpec((1, 2, tk, FB), lambda p, k, o: (o[p], 0, k, 0))],
            out_specs=_ANY,
            scratch_shapes=[pltpu.VMEM((2, FB, D), F32)] + _rs2_scratch((FB, D))),
        out_shape=jax.ShapeDtypeStruct((NCHIP, FB, D), BF16),
        compiler_params=_cp("arbitrary", "arbitrary"),
    )(order, h2, dgu.reshape(NCHIP, 2, t_len, FB))


def _scaled_wgrad(name, a, dx, w, gate_row, mod, order):
    nb, t_len, kb = a.shape
    tk = min(TK, t_len)
    nk = t_len // tk
    cpb = NCHIP // nb
    rows = kb // (2 * cpb)

    def body(ord_ref, a_ref, dx_ref, w_ref, mod_ref, parts_ref, dg_ref, acc, *rs):
        p, k = pl.program_id(0), pl.program_id(1)
        j = ord_ref[p]

        @pl.when((p == 0) & (k == 0))
        def _():
            dg_ref[...] = jnp.zeros_like(dg_ref)

        @pl.when(k == 0)
        def _():
            acc[...] = jnp.zeros_like(acc)

        acc[...] += _dot_tn(a_ref[0], dx_ref[...])

        @pl.when(k == nk - 1)
        def _():
            z = acc[...]
            zg = (z * mod_ref[gate_row:gate_row + 1, :]).astype(BF16)
            dg_ref[0:1, :] += jnp.sum(z * w_ref[...].astype(F32), axis=0, keepdims=True)
            for i in range(cpb):
                q = j * cpb + i
                for half in range(2):
                    rs[0][q, half] = zg[(2 * i + half) * rows:(2 * i + half + 1) * rows]
                _rs2_to_sibling(q, rs)

        if cpb == 1:
            @pl.when((k == nk - 1) & (p > 0))
            def _():
                _rs2_forward(ord_ref[p - 1], parts_ref, rs)

        @pl.when((p == nb - 1) & (k == nk - 1))
        def _():
            for i in range(cpb):
                _rs2_forward(j * cpb + i, parts_ref, rs)
            _rs2_finish(parts_ref, rs)

    return pl.pallas_call(
        body, name=name,
        grid_spec=pltpu.PrefetchScalarGridSpec(
            num_scalar_prefetch=1, grid=(nb, nk),
            in_specs=[pl.BlockSpec((1, tk, kb), lambda p, k, o: (o[p], k, 0)),
                      pl.BlockSpec((tk, D), lambda p, k, o: (k, 0)),
                      pl.BlockSpec((kb, D), lambda p, k, o: (o[p], 0)),
                      pl.BlockSpec((8, D), lambda p, k, o: (0, 0))],
            out_specs=[_ANY, pl.BlockSpec((8, D), lambda p, k, o: (0, 0))],
            scratch_shapes=[pltpu.VMEM((kb, D), F32)] + _rs2_scratch((rows, D))),
        out_shape=[jax.ShapeDtypeStruct((NCHIP, rows, D), BF16), jax.ShapeDtypeStruct((8, D), F32)],
        compiler_params=_cp("arbitrary", "arbitrary"),
    )(order, a, dx, w, mod)


M_WA, M_WB, M_CBIAS, M_BA, M_BX, M_LS = 0, 3, 7, 8, 9, 10


def _mixer_bwd(proj, hl, dmg, prm, wa, wx):
    t_len = proj.shape[0]
    tt = min(TT, t_len)
    nt = t_len // tt
    hb8 = tt // 8

    def rev(i):
        return nt - 1 - i

    def halo(i):
        return jnp.maximum(rev(i) * hb8 - 1, 0)

    def body(proj_ref, ph_ref, hl_ref, hh_ref, dmg_ref, prm_ref, wa_ref, wx_ref,
             dp_ref, sums_ref, gwa_ref, gwx_ref,
             xe, ve, he, u_s, ya_s, rp_s, ip_s, due, dye, drp_s, dip_s, an, gn):
        i = pl.program_id(0)
        t = rev(i)

        @pl.when(i == 0)
        def _():
            sums_ref[...] = jnp.zeros_like(sums_ref)
            gwa_ref[...] = jnp.zeros_like(gwa_ref)
            gwx_ref[...] = jnp.zeros_like(gwx_ref)
            due[tt:tt + 8, :] = jnp.zeros((8, D), F32)
            dye[tt:tt + 8, :] = jnp.zeros((8, D), F32)
            an[...] = jnp.zeros((8, D), F32)
            gn[...] = jnp.zeros((8, D), F32)

        live = (t > 0).astype(F32)
        xe[0:8, :] = _pj(ph_ref, 3).astype(F32) * live
        ve[0:8, :] = _pj(ph_ref, 1).astype(F32) * _pj(ph_ref, 2).astype(F32) * live
        he[0:8, :] = hh_ref[...].astype(F32) * live
        xe[8:8 + tt, :] = _pj(proj_ref, 3).astype(F32)
        ve[8:8 + tt, :] = _pj(proj_ref, 1).astype(F32) * _pj(proj_ref, 2).astype(F32)
        he[8:8 + tt, :] = hl_ref[...].astype(F32)
        u = prm_ref[P_CBIAS:P_CBIAS + 1, :] + prm_ref[P_WB:P_WB + 1, :] * xe[5:5 + tt, :]
        for k in range(1, 4):
            u = u + prm_ref[P_WB + k:P_WB + k + 1, :] * xe[5 + k:5 + k + tt, :]
        u_s[...] = u
        ya = prm_ref[P_WA:P_WA + 1, :] * ve[6:6 + tt, :]
        for k in range(1, 3):
            ya = ya + prm_ref[P_WA + k:P_WA + k + 1, :] * ve[6 + k:6 + k + tt, :]
        ya_s[...] = ya
        ub = u.astype(BF16)
        for h in range(HEADS):
            cs = slice(h * HB, (h + 1) * HB)
            rp_s[:, cs] = _dot(ub[:, cs], wa_ref[h]) + prm_ref[P_BA:P_BA + 1, cs]
            ip_s[:, cs] = _dot(ub[:, cs], wx_ref[h]) + prm_ref[P_BX:P_BX + 1, cs]

        ls_all = _log_sigmoid(prm_ref[P_LAM:P_LAM + 1, :])
        row = lax.broadcasted_iota(jnp.int32, (8, CG), 0)
        nblk = tt // 16

        def blk(ib, carry):
            r0 = pl.multiple_of((nblk - 1 - ib) * 16, 16)
            rows = pl.ds(r0, 16)
            for g in range(D // CG):
                cs = slice(g * CG, (g + 1) * CG)
                ls = ls_all[:, cs]
                dm = dmg_ref[rows, cs].astype(F32)
                cb = _pj(proj_ref, 0, rows, cs).astype(F32)
                rg = _pj(proj_ref, 4, rows, cs).astype(F32)
                sga = _sig(_pj(proj_ref, 5, rows, cs).astype(F32))
                sgb = _sig(_pj(proj_ref, 6, rows, cs).astype(F32))
                ya0 = ya_s[rows, cs]
                h16 = he[pl.ds(r0 + 8, 16), cs]
                gl, th = _gelu(rg)
                dgl = 0.5 * (1.0 + th) + 0.5 * rg * (1.0 - th * th) * (_GC * (1.0 + 3.0 * 0.044715 * rg * rg))
                y_a = cb * ya0
                y_b = h16 * gl
                dy_a = dm * sga
                dy_b = dm * sgb
                col = lambda s: slice(s * D + g * CG, s * D + (g + 1) * CG)
                dp_ref[rows, col(5)] = (dm * y_a * sga * (1.0 - sga)).astype(BF16)
                dp_ref[rows, col(6)] = (dm * y_b * sgb * (1.0 - sgb)).astype(BF16)
                dp_ref[rows, col(4)] = (dy_b * h16 * dgl).astype(BF16)
                dp_ref[rows, col(0)] = (dy_a * ya0).astype(BF16)
                dye[rows, cs] = dy_a * cb
                dh16 = dy_b * gl

                a_next = an[:, cs]
                g_next = gn[:, cs]
                s_ba = jnp.zeros((8, CG), F32)
                s_bx = jnp.zeros((8, CG), F32)
                s_ls = jnp.zeros((8, CG), F32)
                for sb in (1, 0):
                    rr = r0 + 8 * sb
                    first = (row + (t * tt + rr)) == 0
                    uu = u_s[pl.ds(rr, 8), cs]
                    r, ig, la, a, m2, mult = _lru_gates(rp_s[pl.ds(rr, 8), cs], ip_s[pl.ds(rr, 8), cs], ls, first)
                    ca = jnp.where(row < 7, pltpu.roll(a, 7, 0), a_next)
                    cb_ = dh16[8 * sb:8 * sb + 8, :]
                    for s in (1, 2, 4):
                        a_sh = jnp.where(row < 8 - s, pltpu.roll(ca, 8 - s, 0), 1.0)
                        b_sh = jnp.where(row < 8 - s, pltpu.roll(cb_, 8 - s, 0), 0.0)
                        cb_ = ca * b_sh + cb_
                        ca = ca * a_sh
                    gv = ca * g_next + cb_
                    g_next = jnp.broadcast_to(gv[0:1, :], gv.shape)
                    a_next = jnp.broadcast_to(a[0:1, :], a.shape)
                    hprev = jnp.where(row >= 1, pltpu.roll(he[pl.ds(rr + 8, 8), cs], 1, 0),
                                      pltpu.roll(he[pl.ds(rr, 8), cs], 1, 0))
                    da = gv * hprev
                    dmult = jnp.where(first, 0.0, gv * ig * uu)
                    dla = da * a + jnp.where(m2 > 0.0, dmult * (-(a * a) / mult), 0.0)
                    drp = dla * (LRU_C * ls) * r * (1.0 - r)
                    dip = gv * mult * uu * ig * (1.0 - ig)
                    s_ls = s_ls + dla * (LRU_C * r)
                    s_ba = s_ba + drp
                    s_bx = s_bx + dip
                    drp_s[pl.ds(rr, 8), cs] = drp
                    dip_s[pl.ds(rr, 8), cs] = dip
                    due[pl.ds(rr, 8), cs] = gv * mult * ig
                an[:, cs] = a_next
                gn[:, cs] = g_next
                sums_ref[M_BA:M_BA + 1, cs] += jnp.sum(s_ba, axis=0, keepdims=True)
                sums_ref[M_BX:M_BX + 1, cs] += jnp.sum(s_bx, axis=0, keepdims=True)
                sums_ref[M_LS:M_LS + 1, cs] += jnp.sum(s_ls, axis=0, keepdims=True)
            return carry

        lax.fori_loop(0, nblk, blk, 0)

        drp_b = drp_s[...].astype(BF16)
        dip_b = dip_s[...].astype(BF16)
        for h in range(HEADS):
            cs = slice(h * HB, (h + 1) * HB)
            due[0:tt, cs] += _dot_nt(drp_b[:, cs], wa_ref[h]) + _dot_nt(dip_b[:, cs], wx_ref[h])
            gwa_ref[h] += _dot_tn(ub[:, cs], drp_b[:, cs])
            gwx_ref[h] += _dot_tn(ub[:, cs], dip_b[:, cs])

        du = due[0:tt, :]
        sums_ref[M_CBIAS:M_CBIAS + 1, :] += jnp.sum(du, axis=0, keepdims=True)
        drx = prm_ref[P_WB:P_WB + 1, :] * due[3:3 + tt, :]
        sums_ref[M_WB:M_WB + 1, :] += jnp.sum(du * xe[5:5 + tt, :], axis=0, keepdims=True)
        for k in range(1, 4):
            drx = drx + prm_ref[P_WB + k:P_WB + k + 1, :] * due[3 - k:3 - k + tt, :]
            sums_ref[M_WB + k:M_WB + k + 1, :] += jnp.sum(du * xe[5 + k:5 + k + tt, :], axis=0, keepdims=True)
        dp_ref[:, 3 * D:4 * D] = drx.astype(BF16)
        dya = dye[0:tt, :]
        dv = prm_ref[P_WA:P_WA + 1, :] * dye[2:2 + tt, :]
        sums_ref[M_WA:M_WA + 1, :] += jnp.sum(dya * ve[6:6 + tt, :], axis=0, keepdims=True)
        for k in range(1, 3):
            dv = dv + prm_ref[P_WA + k:P_WA + k + 1, :] * dye[2 - k:2 - k + tt, :]
            sums_ref[M_WA + k:M_WA + k + 1, :] += jnp.sum(dya * ve[6 + k:6 + k + tt, :], axis=0, keepdims=True)
        dp_ref[:, D:2 * D] = (dv * _pj(proj_ref, 2).astype(F32)).astype(BF16)
        dp_ref[:, 2 * D:3 * D] = (dv * _pj(proj_ref, 1).astype(F32)).astype(BF16)
        due[tt:tt + 8, :] = due[0:8, :]
        dye[tt:tt + 8, :] = dye[0:8, :]

        @pl.when(i == nt - 1)
        def _():
            sums_ref[M_LS:M_LS + 1, :] = sums_ref[M_LS:M_LS + 1, :] * _sig(-prm_ref[P_LAM:P_LAM + 1, :])

    big = lambda: pltpu.VMEM((tt + 8, D), F32)
    tile = lambda: pltpu.VMEM((tt, D), F32)
    return pl.pallas_call(
        body, name="mixer_bwd", grid=(nt,),
        in_specs=[pl.BlockSpec((tt, 7 * D), lambda i: (rev(i), 0)),
                  pl.BlockSpec((8, 7 * D), lambda i: (halo(i), 0)),
                  pl.BlockSpec((tt, D), lambda i: (rev(i), 0)),
                  pl.BlockSpec((8, D), lambda i: (halo(i), 0)),
                  pl.BlockSpec((tt, D), lambda i: (rev(i), 0)),
                  pl.BlockSpec((16, D), lambda i: (0, 0)),
                  pl.BlockSpec((HEADS, HB, HB), lambda i: (0, 0, 0)),
                  pl.BlockSpec((HEADS, HB, HB), lambda i: (0, 0, 0))],
        out_specs=[pl.BlockSpec((tt, 7 * D), lambda i: (rev(i), 0)),
                   pl.BlockSpec((16, D), lambda i: (0, 0)),
                   pl.BlockSpec((HEADS, HB, HB), lambda i: (0, 0, 0)),
                   pl.BlockSpec((HEADS, HB, HB), lambda i: (0, 0, 0))],
        out_shape=[jax.ShapeDtypeStruct((t_len, 7 * D), BF16), jax.ShapeDtypeStruct((16, D), F32),
                   jax.ShapeDtypeStruct((HEADS, HB, HB), F32), jax.ShapeDtypeStruct((HEADS, HB, HB), F32)],
        scratch_shapes=[big(), big(), big(), tile(), tile(), tile(), tile(), big(), big(), tile(), tile(),
                        pltpu.VMEM((8, D), F32), pltpu.VMEM((8, D), F32)],
        compiler_params=_cp("arbitrary"),
    )(proj, proj, hl, hl, dmg, prm, wa, wx)


def _in_proj_bwd(dproj, w_in, x, dx1, mod, g_mix):
    t_len = x.shape[0]
    tm = min(TM, t_len)

    def body(dp_ref, w_ref, x_ref, dx1_ref, mod_ref, g_ref, gx_ref, sums_ref, acc):
        @pl.when(pl.program_id(0) == 0)
        def _():
            sums_ref[...] = jnp.zeros_like(sums_ref)

        for rows in _sub_blocks(tm):
            acc[rows, :] = _dot_nt(dp_ref[rows, :], w_ref[...])

        def write(rows, dx):
            gx_ref[rows, :] = dx

        _norm_bwd_rows(tm, 16, acc, x_ref, dx1_ref, mod_ref[1:2, :], g_ref[...], sums_ref, write)

    return pl.pallas_call(
        body, name="in_proj_bwd", grid=(t_len // tm,),
        in_specs=[pl.BlockSpec((tm, 7 * D), lambda i: (i, 0)),
                  _resident((D, 7 * D)),
                  pl.BlockSpec((tm, D), lambda i: (i, 0)), pl.BlockSpec((tm, D), lambda i: (i, 0)),
                  pl.BlockSpec((8, D), lambda i: (0, 0)), pl.BlockSpec((1, D), lambda i: (0, 0))],
        out_specs=[pl.BlockSpec((tm, D), lambda i: (i, 0)), pl.BlockSpec((8, D), lambda i: (0, 0))],
        out_shape=[jax.ShapeDtypeStruct((t_len, D), F32), jax.ShapeDtypeStruct((8, D), F32)],
        scratch_shapes=[pltpu.VMEM((tm, D), F32)],
        compiler_params=_cp("arbitrary"),
    )(dproj, w_in, x, dx1, mod, g_mix)


def _in_wgrad(h, dproj, g_wa, g_wx, order):
    t_len = h.shape[0]
    tk = min(TKI, t_len)
    nk = t_len // tk
    cw = 7 * D // NDEV
    hr = HB // NDEV

    def body(ord_ref, h_ref, d_ref, ga_ref, gx_ref, parts_ref, pa_ref, px_ref, acc, *scr):
        rs, sems = scr[:-6], scr[-6:]
        p, k = pl.program_id(0), pl.program_id(1)

        def head_rows(ref):
            return lambda s: ref.at[:, pl.ds(s * hr, hr), :]

        @pl.when((p == 0) & (k == 0))
        def _():
            for s in range(NDEV):
                _rs_send(head_rows(ga_ref)(s), pa_ref, s, *sems[0:3])
                _rs_send(head_rows(gx_ref)(s), px_ref, s, *sems[3:6])

        @pl.when(k == 0)
        def _():
            acc[...] = jnp.zeros_like(acc)

        acc[...] += _dot_tn(h_ref[...], d_ref[...])

        @pl.when(k == nk - 1)
        def _():
            q = ord_ref[p]
            for half in range(2):
                rs[0][q, half] = acc[:, half * cw:(half + 1) * cw].astype(BF16)
            _rs2_to_sibling(q, rs)

        @pl.when((k == nk - 1) & (p > 0))
        def _():
            _rs2_forward(ord_ref[p - 1], parts_ref, rs)

        @pl.when((p == NCHIP - 1) & (k == nk - 1))
        def _():
            _rs2_forward(ord_ref[p], parts_ref, rs)
            _rs2_finish(parts_ref, rs)
            _rs_finish(head_rows(ga_ref), pa_ref, *sems[0:3])
            _rs_finish(head_rows(gx_ref), px_ref, *sems[3:6])

    return pl.pallas_call(
        body, name="in_wgrad",
        grid_spec=pltpu.PrefetchScalarGridSpec(
            num_scalar_prefetch=1, grid=(NCHIP, nk),
            in_specs=[pl.BlockSpec((tk, D), lambda p, k, o: (k, 0)),
                      pl.BlockSpec((tk, 2 * cw), lambda p, k, o: (k, o[p])), _ANY, _ANY],
            out_specs=[_ANY, _ANY, _ANY],
            scratch_shapes=[pltpu.VMEM((D, 2 * cw), F32)] + _rs2_scratch((D, cw)) + _RS_SEMS * 2),
        out_shape=[jax.ShapeDtypeStruct((NCHIP, D, cw), BF16), jax.ShapeDtypeStruct((NDEV, HEADS, hr, HB), F32),
                   jax.ShapeDtypeStruct((NDEV, HEADS, hr, HB), F32)],
        compiler_params=_cp("arbitrary", "arbitrary"),
    )(order, h, dproj, g_wa, g_wx)


def _ada_fwd(c_all, w_ada, b_cols):
    def body(c_ref, w_ref, b_ref, o_ref):
        cv = c_ref[...]
        o_ref[...] = _dot((cv * _sig(cv)).astype(BF16), w_ref[...].astype(BF16)) + b_ref[...]

    return pl.pallas_call(body, name="ada_fwd", out_shape=jax.ShapeDtypeStruct((16, w_ada.shape[1]), F32),
                          compiler_params=_cp())(c_all, w_ada, b_cols)


def _adam_math(w, g, m, v):
    m = ADAM_B1 * m + (1.0 - ADAM_B1) * g
    v = ADAM_B2 * v + (1.0 - ADAM_B2) * (g * g)
    m_hat = m / (1.0 - ADAM_B1 ** ADAM_STEP)
    v_hat = v / (1.0 - ADAM_B2 ** ADAM_STEP)
    delta = -ADAM_LR * (m_hat / (jnp.sqrt(v_hat) + ADAM_EPS) + ADAM_WD * w)
    return delta, m, v


def _ada_bwd(c_all, dmod_cols, w, m, v):
    rb = 256
    n = w.shape[1]
    nrow = c_all.shape[0]

    def body(c_ref, d_ref, w_ref, m_ref, v_ref, g_ref, dl_ref, nm_ref, nv_ref):
        cv = c_ref[...]
        g = _dot_tn((cv * _sig(cv)).astype(BF16), d_ref[...].astype(BF16))
        g_ref[...] = g
        dl_ref[...], nm_ref[...], nv_ref[...] = _adam_math(w_ref[...], g, m_ref[...], v_ref[...])

    blk = pl.BlockSpec((rb, n), lambda i: (i, 0))
    sds = jax.ShapeDtypeStruct(w.shape, F32)
    return pl.pallas_call(
        body, name="ada_bwd", grid=(D // rb,),
        in_specs=[pl.BlockSpec((nrow, rb), lambda i: (0, i)), pl.BlockSpec((nrow, n), lambda i: (0, 0)), blk, blk, blk],
        out_specs=[blk, blk, blk, blk], out_shape=[sds, sds, sds, sds],
        compiler_params=_cp("parallel"),
    )(c_all, dmod_cols, w, m, v)


def _adam(name, parts, w, m, v):
    p, r, c = parts.shape
    rb = r
    for cand in (256, 128, 64, 32, 16, 8):
        if r % cand == 0 and r >= cand:
            rb = cand
            break

    def body(p_ref, w_ref, m_ref, v_ref, g_ref, dl_ref, nm_ref, nv_ref):
        g = p_ref[0].astype(F32)
        for q in range(1, p):
            g = g + p_ref[q].astype(F32)
        g_ref[...] = g
        dl_ref[...], nm_ref[...], nv_ref[...] = _adam_math(w_ref[...], g, m_ref[...], v_ref[...])

    blk = pl.BlockSpec((rb, c), lambda i: (i, 0))
    sds = jax.ShapeDtypeStruct((r, c), F32)
    return pl.pallas_call(
        body, name=name, grid=(r // rb,),
        in_specs=[pl.BlockSpec((p, rb, c), lambda i: (0, i, 0)), blk, blk, blk],
        out_specs=[blk, blk, blk, blk], out_shape=[sds, sds, sds, sds],
        compiler_params=_cp("parallel"),
    )(parts, w, m, v)


def _my_pos():
    return lax.axis_index("x"), lax.axis_index("y"), lax.axis_index("c")


def _all_gather_small(name, v):
    m_per, n = v.shape

    def body(x_ref, out_ref, send_sems, recv_sems, local_sem):
        x, y, c = _my_pos()
        me, sibling = (x, y, c), (x, y, 1 - c)
        chips = [(1 - x, y), (x, 1 - y), (1 - x, 1 - y)]

        def rows(px, py, pc):
            return out_ref.at[pl.ds((4 * px + 2 * py + pc) * m_per, m_per), :]

        def copy(k, block, to, src=None):
            return pltpu.make_async_remote_copy(
                src_ref=rows(*block) if src is None else src, dst_ref=rows(*block),
                send_sem=send_sems.at[k], recv_sem=recv_sems.at[k], device_id=to, device_id_type=MESH)

        mine = pltpu.make_async_copy(x_ref, rows(*me), local_sem)
        mine.start()
        first = [copy(0, me, sibling, src=x_ref)]
        first += [copy(1 + j, me, (*chip, c), src=x_ref) for j, chip in enumerate(chips)]
        for cp in first:
            cp.start()
        passed = [copy(4 + j, (*chip, c), sibling) for j, chip in enumerate(chips)]
        for j, chip in enumerate(chips):
            copy(1 + j, (*chip, c), me).wait_recv()
            passed[j].start()
        copy(0, sibling, me).wait_recv()
        for j, chip in enumerate(chips):
            copy(4 + j, (*chip, 1 - c), me).wait_recv()
        for cp in first + passed:
            cp.wait_send()
        mine.wait()

    return pl.pallas_call(
        body, name=name, out_shape=jax.ShapeDtypeStruct((NDEV * m_per, n), v.dtype),
        in_specs=[pl.BlockSpec(memory_space=pltpu.VMEM)], out_specs=pl.BlockSpec(memory_space=pltpu.VMEM),
        scratch_shapes=[pltpu.SemaphoreType.DMA((7,)), pltpu.SemaphoreType.DMA((7,)), pltpu.SemaphoreType.DMA],
    )(v)


def _blk_cols(n):
    return lambda ref, b: ref.at[:, pl.ds(pl.multiple_of(b * n, 128), n)]


def _blk_rows(n):
    return lambda ref, b: ref.at[pl.ds(pl.multiple_of(b * n, 8), n), :]


def _blk_lead(ref, b):
    return ref.at[b]


def _blk_heads(ref, b):
    return ref.at[:, pl.ds(pl.multiple_of(b * (HB // NDEV), 8), HB // NDEV), :]


def _ag_phases(ins, outs, slicers, send_sems, recv_sems, local_sems):
    na = len(ins)
    x, y, c = _my_pos()
    me, sibling = (x, y, c), (x, y, 1 - c)
    chips = [(1 - x, y), (x, 1 - y), (1 - x, 1 - y)]

    def copy(a, k, block, to, from_shard=False):
        px, py, pc = block
        dst = slicers[a](outs[a], 4 * px + 2 * py + pc)
        return pltpu.make_async_remote_copy(
            src_ref=ins[a] if from_shard else dst, dst_ref=dst,
            send_sem=send_sems.at[a * 7 + k], recv_sem=recv_sems.at[a * 7 + k], device_id=to, device_id_type=MESH)

    def local(a):
        return pltpu.make_async_copy(ins[a], slicers[a](outs[a], 4 * x + 2 * y + c), local_sems.at[a])

    def firsts(a):
        return [copy(a, 0, me, sibling, True)] + [copy(a, 1 + j, me, (*chip, c), True) for j, chip in enumerate(chips)]

    def start():
        for a in range(na):
            local(a).start()
            for cp in firsts(a):
                cp.start()

    def forward():
        for a in range(na):
            for j, chip in enumerate(chips):
                copy(a, 1 + j, (*chip, c), me).wait_recv()
                copy(a, 4 + j, (*chip, c), sibling).start()

    def finish():
        for a in range(na):
            copy(a, 0, sibling, me).wait_recv()
            for j, chip in enumerate(chips):
                copy(a, 4 + j, (*chip, 1 - c), me).wait_recv()
        for a in range(na):
            for cp in firsts(a) + [copy(a, 4 + j, (*chip, c), sibling) for j, chip in enumerate(chips)]:
                cp.wait_send()
            local(a).wait()

    return start, forward, finish


def _ag_sems(na):
    return [pltpu.SemaphoreType.DMA((7 * na,)), pltpu.SemaphoreType.DMA((7 * na,)), pltpu.SemaphoreType.DMA((na,))]


def _all_gather_weights(shards, fulls, slicers):
    na = len(shards)

    def body(*refs):
        start, forward, finish = _ag_phases(refs[:na], refs[na:2 * na], slicers, *refs[2 * na:])
        start()
        forward()
        finish()

    return pl.pallas_call(
        body, name="gather_weights",
        out_shape=[jax.ShapeDtypeStruct(s, sh.dtype) for s, sh in zip(fulls, shards)],
        in_specs=[_ANY] * na, out_specs=[_ANY] * na, scratch_shapes=_ag_sems(na),
    )(*shards)


def _scatter_grads(grads, shard_shapes, slicers):
    na = len(grads)

    def body(*refs):
        ins, outs = refs[:na], refs[na:2 * na]
        send_sems, recv_sems, local_sems = refs[2 * na:]
        x, y, c = _my_pos()
        me = 4 * x + 2 * y + c
        mine, sent = [], []
        for a in range(na):
            cp = pltpu.make_async_copy(slicers[a](ins[a], me), outs[a].at[me], local_sems.at[a])
            cp.start()
            mine.append(cp)
        rel = [(k >> 2 & 1, k >> 1 & 1, k & 1) for k in range(1, NDEV)]
        for a in range(na):
            for k, (fx, fy, fc) in enumerate(rel):
                px, py, pc = x ^ fx, y ^ fy, c ^ fc
                cp = pltpu.make_async_remote_copy(
                    src_ref=slicers[a](ins[a], 4 * px + 2 * py + pc), dst_ref=outs[a].at[me],
                    send_sem=send_sems.at[a * 7 + k], recv_sem=recv_sems.at[a * 7 + k],
                    device_id=(px, py, pc), device_id_type=MESH)
                cp.start()
                sent.append(cp)
        for a in range(na):
            for k, (fx, fy, fc) in enumerate(rel):
                px, py, pc = x ^ fx, y ^ fy, c ^ fc
                src = 4 * px + 2 * py + pc
                pltpu.make_async_remote_copy(
                    src_ref=slicers[a](ins[a], me), dst_ref=outs[a].at[src],
                    send_sem=send_sems.at[a * 7 + k], recv_sem=recv_sems.at[a * 7 + k],
                    device_id=(px, py, pc), device_id_type=MESH).wait_recv()
        for cp in sent:
            cp.wait_send()
        for cp in mine:
            cp.wait()

    any_spec = pl.BlockSpec(memory_space=pl.ANY)
    return pl.pallas_call(
        body, name="scatter_grads",
        out_shape=[jax.ShapeDtypeStruct((NDEV,) + tuple(s), g.dtype) for s, g in zip(shard_shapes, grads)],
        in_specs=[any_spec] * na, out_specs=[any_spec] * na,
        scratch_shapes=[pltpu.SemaphoreType.DMA((7 * na,)), pltpu.SemaphoreType.DMA((7 * na,)),
                        pltpu.SemaphoreType.DMA((na,))],
    )(*grads)


def _local_step(x, target, mod, g_mix, g_ffn, g_fin, prm, w_in_shard, shards):
    fulls = [(HEADS, HB, HB), (HEADS, HB, HB), (D, D), (NDEV, FB, D), (DFF, D)]
    slicers = [_blk_heads, _blk_heads, _blk_rows(D // NDEV), _blk_lead, _blk_rows(DFF // NDEV)]
    my_chip = _my_index() >> 1
    own_first = (my_chip ^ jnp.arange(NCHIP, dtype=jnp.int32)).astype(jnp.int32)
    proj, h, w_in, (wa, wx, w_out, w_gu, w_down) = _in_proj(x, mod, g_mix, w_in_shard, own_first, shards, fulls,
                                                            slicers)
    w_gu = w_gu.reshape(2, 4, FB, D)
    merged, hl = _mixer_fwd(proj, prm, wa, wx)
    x1, h2 = _out_proj(merged, x, mod, g_ffn, w_out)
    gu, dx2, dx2b, loss, d_gfin = _ffn_fwd(h2, x1, target, mod, g_fin, w_gu, w_down)
    dgu, act, dx1, dx1b, dmg, sums2 = _ffn_bwd(dx2, gu, x1, mod, g_ffn, w_gu, w_down, w_out)
    chip_order = _xor_order(_my_index() >> 1, NCHIP)
    p_wgu = _gu_wgrad(h2, dgu, chip_order)
    p_wdown, d_gt2 = _scaled_wgrad("down_wgrad", act, dx2b, w_down, 5, mod, chip_order)
    p_wout, d_gt1 = _scaled_wgrad("out_wgrad", merged.reshape(1, *merged.shape), dx1b, w_out, 2, mod,
                                  jnp.zeros((1,), jnp.int32))
    dproj, msums, g_wa, g_wx = _mixer_bwd(proj, hl, dmg, prm, wa, wx)
    p_win, p_wa, p_wx = _in_wgrad(h, dproj, g_wa, g_wx, chip_order)
    grad_x, sums1 = _in_proj_bwd(dproj, w_in, x, dx1, mod, g_mix)
    return dict(loss=loss, grad_x=grad_x, d_gfin=d_gfin, sums1=sums1, sums2=sums2, msums=msums,
                d_gt1=d_gt1[0:1], d_gt2=d_gt2[0:1], p_win=p_win, p_wa=p_wa, p_wx=p_wx, p_wout=p_wout, p_wgu=p_wgu,
                p_wdown=p_wdown)


def kernel(x, c, w_ada, b_ada, g_norm_mix, w_in, conv_a_w, conv_b_w, conv_b_bias, w_rg_a, b_rg_a, w_rg_x, b_rg_x, lru_lambda, w_out, g_norm_ffn, w_gate_up, w_down, g_norm_final, loss_target, m_w_ada, m_b_ada, m_g_norm_mix, m_w_in, m_conv_a_w, m_conv_b_w, m_conv_b_bias, m_w_rg_a, m_b_rg_a, m_w_rg_x, m_b_rg_x, m_lru_lambda, m_w_out, m_g_norm_ffn, m_w_gate_up, m_w_down, m_g_norm_final, v_w_ada, v_b_ada, v_g_norm_mix, v_w_in, v_conv_a_w, v_conv_b_w, v_conv_b_bias, v_w_rg_a, v_b_rg_a, v_w_rg_x, v_b_rg_x, v_lru_lambda, v_w_out, v_g_norm_ffn, v_w_gate_up, v_w_down, v_g_norm_final):
    me = 4 * lax.axis_index("x") + 2 * lax.axis_index("y") + lax.axis_index("c")
    ncol = w_ada.shape[2]
    cw = conv_a_w.shape[2]

    pack0 = jnp.concatenate([c, conv_a_w.reshape(1, 3 * cw), conv_b_w.reshape(1, 4 * cw)], axis=1)
    got0 = _all_gather_small("gather_c", jnp.broadcast_to(pack0, (8, pack0.shape[1])))
    got0 = got0.reshape(NDEV, 8, -1)[:, 0, :]
    c_all = got0[:, :D]
    conv_a = got0[:, D:D + 3 * cw].reshape(NDEV, 3, cw).transpose(1, 0, 2).reshape(3, D)
    conv_b = got0[:, D + 3 * cw:].reshape(NDEV, 4, cw).transpose(1, 0, 2).reshape(4, D)

    b_cols = lax.dynamic_slice_in_dim(b_ada, me * ncol, ncol, axis=1)
    c16 = jnp.concatenate([c_all, jnp.zeros((8, D), F32)], axis=0)
    mod_cols = _ada_fwd(c16, w_ada[0], b_cols)[:NDEV]
    got1 = _all_gather_small("gather_mod", mod_cols).reshape(NDEV, NDEV, ncol)
    mod6 = lax.dynamic_index_in_dim(got1, me, axis=1, keepdims=False).reshape(6, D)
    mod = jnp.concatenate([mod6, jnp.zeros((2, D), F32)], axis=0)

    tr = lambda a: jnp.swapaxes(a, 1, 2)
    shards = [w_rg_a[0].astype(BF16), w_rg_x[0].astype(BF16), w_out[0].astype(BF16), tr(w_gate_up)[0].astype(BF16),
              w_down[0].astype(BF16)]

    prm = jnp.concatenate([conv_a, conv_b, conv_b_bias, b_rg_a, b_rg_x, lru_lambda, jnp.zeros((5, D), F32)], axis=0)
    r = _local_step(x[0], loss_target[0], mod, g_norm_mix, g_norm_ffn, g_norm_final.reshape(1, D), prm,
                    w_in[0].astype(BF16), shards)

    parts = [r["p_win"], r["p_wa"], r["p_wx"], r["p_wout"], r["p_wgu"], r["p_wdown"]]
    big = {}
    for nm, p, w, m, v in (("w_in", parts[0], w_in, m_w_in, v_w_in), ("w_rg_a", parts[1], w_rg_a, m_w_rg_a, v_w_rg_a),
                           ("w_rg_x", parts[2], w_rg_x, m_w_rg_x, v_w_rg_x), ("w_out", parts[3], w_out, m_w_out, v_w_out),
                           ("w_gate_up", parts[4], tr(w_gate_up), tr(m_w_gate_up), tr(v_w_gate_up)),
                           ("w_down", parts[5], w_down, m_w_down, v_w_down)):
        two_d = (-1, w.shape[-1])
        outs = _adam("adam_" + nm, p.reshape((p.shape[0],) + w.reshape(two_d).shape), w.reshape(two_d), m.reshape(two_d),
                     v.reshape(two_d))
        big[nm] = [o.reshape(w.shape) for o in outs]
    big["w_gate_up"] = [tr(o) for o in big["w_gate_up"]]

    small = jnp.concatenate([
        r["sums1"][S_SH:S_SH + 1], r["sums1"][S_SC:S_SC + 1], r["d_gt1"],
        r["sums2"][S_SH:S_SH + 1], r["sums2"][S_SC:S_SC + 1], r["d_gt2"],
        r["sums1"][S_G:S_G + 1],
        r["msums"][M_CBIAS:M_CBIAS + 1], r["msums"][M_BA:M_BA + 1], r["msums"][M_BX:M_BX + 1],
        r["msums"][M_LS:M_LS + 1],
        r["sums2"][S_G:S_G + 1], r["d_gfin"],
        r["msums"][M_WA:M_WA + 3], r["msums"][M_WB:M_WB + 4],
        jnp.zeros((4, D), F32)], axis=0)
    got2 = _all_gather_small("gather_small", small).reshape(NDEV, 24, D)

    rep_w = jnp.concatenate([b_ada.reshape(6, D), g_norm_mix, conv_b_bias, b_rg_a, b_rg_x, lru_lambda, g_norm_ffn,
                             g_norm_final.reshape(1, D), jnp.zeros((3, D), F32)], axis=0)
    rep_m = jnp.concatenate([m_b_ada.reshape(6, D), m_g_norm_mix, m_conv_b_bias, m_b_rg_a, m_b_rg_x, m_lru_lambda,
                             m_g_norm_ffn, m_g_norm_final.reshape(1, D), jnp.zeros((3, D), F32)], axis=0)
    rep_v = jnp.concatenate([v_b_ada.reshape(6, D), v_g_norm_mix, v_conv_b_bias, v_b_rg_a, v_b_rg_x, v_lru_lambda,
                             v_g_norm_ffn, v_g_norm_final.reshape(1, D), jnp.ones((3, D), F32)], axis=0)
    rep = _adam("adam_rep", got2[:, :16, :], rep_w, rep_m, rep_v)

    conv_parts = lax.dynamic_slice_in_dim(got2[:, 13:21, :], me * cw, cw, axis=2)
    cv_w = jnp.concatenate([conv_a_w[0], conv_b_w[0], jnp.zeros((1, cw), F32)], axis=0)
    cv_m = jnp.concatenate([m_conv_a_w[0], m_conv_b_w[0], jnp.zeros((1, cw), F32)], axis=0)
    cv_v = jnp.concatenate([v_conv_a_w[0], v_conv_b_w[0], jnp.ones((1, cw), F32)], axis=0)
    cvo = _adam("adam_conv", conv_parts, cv_w, cv_m, cv_v)

    dmod_cols = lax.dynamic_slice_in_dim(got2[:, :6, :].reshape(NDEV, 6 * D), me * ncol, ncol, axis=1)
    dmod16 = jnp.concatenate([dmod_cols, jnp.zeros((8, ncol), F32)], axis=0)
    ada = _ada_bwd(c16, dmod16, w_ada[0], m_w_ada[0], v_w_ada[0])

    loss = lax.psum(r["loss"][0, 0], AXES)

    def pick(q):
        one = lambda i: rep[q][i:i + 1]
        return [ada[q].reshape(w_ada.shape), rep[q][0:6].reshape(b_ada.shape), one(6), big["w_in"][q],
                cvo[q][0:3].reshape(conv_a_w.shape), cvo[q][3:7].reshape(conv_b_w.shape), one(7),
                big["w_rg_a"][q], one(8), big["w_rg_x"][q], one(9), one(10), big["w_out"][q], one(11),
                big["w_gate_up"][q], big["w_down"][q], rep[q][12]]

    return (loss, r["grad_x"].reshape(x.shape), *pick(0), *pick(1), *pick(2), *pick(3))
```

```python
import functools
import math

import jax
import jax.numpy as jnp
from jax import lax
from jax.experimental import pallas as pl
from jax.experimental.pallas import tpu as pltpu

F32 = jnp.float32
BF16 = jnp.bfloat16

D = 1024
DFF = 2816
NDEV = 8
HEADS = 4
HB = D // HEADS
FB = DFF // 4
EPS = 1e-6
LRU_C = 8.0
ADAM_LR, ADAM_B1, ADAM_B2, ADAM_EPS, ADAM_WD, ADAM_STEP = 0.001, 0.9, 0.999, 1e-08, 0.01, 10

VMEM_LIMIT = 56 * 1024 * 1024
TM = 512
TMI = 1024
TMF = 256
TK = 2048
TKI = 1024
SUB = 256
UNROLL = 4
TT = 256
CG = 256
MESH = pl.DeviceIdType.MESH
AXES = ("x", "y", "c")


def _cp(*sem):
    return pltpu.CompilerParams(dimension_semantics=sem, vmem_limit_bytes=VMEM_LIMIT)


def _sig(x):
    return 1.0 / (1.0 + jnp.exp(-x))


def _log_sigmoid(x):
    z = jnp.exp(-jnp.abs(x))
    u = 1.0 + z
    d = u - 1.0
    l1p = jnp.where(d == 0.0, z, jnp.log(u) * (z / jnp.where(d == 0.0, 1.0, d)))
    return -(jnp.maximum(-x, 0.0) + l1p)


def _neg_expm1(x):
    p = x * (1.0 + x * 0.5 * (1.0 + x * (1.0 / 3.0) * (1.0 + x * 0.25 * (1.0 + x * 0.2 * (1.0 + x * (1.0 / 6.0))))))
    return jnp.where(x > -0.25, -p, 1.0 - jnp.exp(x))


_GC = math.sqrt(2.0 / math.pi)


def _gelu(x):
    t = jnp.tanh(_GC * (x + 0.044715 * x * x * x))
    return 0.5 * x * (1.0 + t), t


def _dot(a, b):
    return jnp.dot(a, b, preferred_element_type=F32)


def _dot_nt(a, b):
    return lax.dot_general(a, b, (((1,), (1,)), ((), ())), preferred_element_type=F32)


def _dot_tn(a, b):
    return lax.dot_general(a, b, (((0,), (0,)), ((), ())), preferred_element_type=F32)


def _resident(shape):
    return pl.BlockSpec(shape, lambda *_: (0,) * len(shape), pipeline_mode=pl.Buffered(1))


def _sub_blocks(n_rows):
    step = min(SUB, n_rows)
    return [slice(r, r + step) for r in range(0, n_rows, step)]


def _fold8(v):
    return v[0:8] + v[8:16]


def _pj(ref, s, rows=slice(None), cols=slice(0, D)):
    return ref[rows, s * D + cols.start:s * D + cols.stop]


def _in_proj(x, mod, g_mix, w_shard, order, shards, fulls, slicers):
    t_len = x.shape[0]
    tm = min(TMI, t_len)
    ni = t_len // tm
    na = len(shards)
    cw = 7 * D // NDEV
    rc = 32

    def body(ord_ref, x_ref, mod_ref, g_ref, wsh_ref, *rest):
        ins, (proj_ref, h_ref, wfull_ref), outs = rest[:na], rest[na:na + 3], rest[na + 3:2 * na + 3]
        h_scr, w_scr, wsend, wrecv, wlocal, wout = rest[2 * na + 3:2 * na + 9]
        start, forward, finish = _ag_phases(ins, outs, slicers, *rest[2 * na + 9:])
        p, i = pl.program_id(0), pl.program_id(1)
        x_, y_, c = _my_pos()
        me, sibling = (x_, y_, c), (x_, y_, 1 - c)
        chip_at = [None, (x_, 1 - y_), (1 - x_, y_), (1 - x_, 1 - y_)]

        def cols(px, py, pc):
            return w_scr.at[:, pl.ds(pl.multiple_of((4 * px + 2 * py + pc) * cw, 128), cw)]

        def wcopy(k, block, to, from_shard=False):
            dst = cols(*block)
            return pltpu.make_async_remote_copy(src_ref=wsh_ref if from_shard else dst, dst_ref=dst,
                                                send_sem=wsend.at[k], recv_sem=wrecv.at[k], device_id=to,
                                                device_id_type=MESH)

        own_local = pltpu.make_async_copy(wsh_ref, cols(*me), wlocal)
        to_hbm = pltpu.make_async_copy(w_scr, wfull_ref, wout)

        @pl.when((p == 0) & (i == 0))
        def _():
            own_local.start()
            wcopy(0, me, sibling, True).start()
            for q in (1, 2, 3):
                wcopy(q, me, (*chip_at[q], c), True).start()
            start()
            own_local.wait()
            wcopy(0, sibling, me).wait_recv()

        for q in (1, 2, 3):
            @pl.when((p == q - 1) & (i == ni // 2))
            def _():
                wcopy(q, (*chip_at[q], c), me).wait_recv()
                wcopy(3 + q, (*chip_at[q], c), sibling).start()

            @pl.when((p == q) & (i == 0))
            def _():
                wcopy(3 + q, (*chip_at[q], 1 - c), me).wait_recv()

        @pl.when((p == 1) & (i == 0))
        def _():
            forward()

        @pl.when((p == NCHIP - 1) & (i == 0))
        def _():
            to_hbm.start()

        gs = g_ref[...] * (1.0 + mod_ref[1:2, :])
        sh = mod_ref[0:1, :]

        def chunk(j, carry):
            rows = pl.ds(pl.multiple_of(j * rc, rc), rc)
            xv = x_ref[rows, :]
            r = lax.rsqrt(jnp.mean(xv * xv, axis=-1, keepdims=True) + EPS)
            h_scr[rows, :] = (xv * r * gs + sh).astype(BF16)
            return carry

        lax.fori_loop(0, tm // rc, chunk, 0, unroll=UNROLL)

        @pl.when(p == 0)
        def _():
            h_ref[...] = h_scr[...]

        wcols = pl.ds(pl.multiple_of(ord_ref[p] * (2 * cw), 128), 2 * cw)
        proj_ref[...] = _dot(h_scr[...], w_scr[:, wcols]).astype(BF16)

        @pl.when((p == NCHIP - 1) & (i == ni - 1))
        def _():
            wcopy(0, me, sibling, True).wait_send()
            for q in (1, 2, 3):
                wcopy(q, me, (*chip_at[q], c), True).wait_send()
                wcopy(3 + q, (*chip_at[q], c), sibling).wait_send()
            finish()
            to_hbm.wait()

    res = pl.pallas_call(
        body, name="in_proj",
        grid_spec=pltpu.PrefetchScalarGridSpec(
            num_scalar_prefetch=1, grid=(NCHIP, ni),
            in_specs=[pl.BlockSpec((tm, D), lambda p, i, o: (i, 0)),
                      pl.BlockSpec((8, D), lambda p, i, o: (0, 0)),
                      pl.BlockSpec((1, D), lambda p, i, o: (0, 0))] + [_ANY] * (1 + na),
            out_specs=[pl.BlockSpec((tm, 2 * cw), lambda p, i, o: (i, o[p])),
                       pl.BlockSpec((tm, D), lambda p, i, o: (jnp.where(p == 0, i, ni - 1), 0))]
            + [_ANY] * (1 + na),
            scratch_shapes=[pltpu.VMEM((tm, D), BF16), pltpu.VMEM((D, 7 * D), BF16),
                            pltpu.SemaphoreType.DMA((7,)), pltpu.SemaphoreType.DMA((7,)),
                            pltpu.SemaphoreType.DMA, pltpu.SemaphoreType.DMA] + _ag_sems(na)),
        out_shape=[jax.ShapeDtypeStruct((t_len, 7 * D), BF16), jax.ShapeDtypeStruct((t_len, D), BF16),
                   jax.ShapeDtypeStruct((D, 7 * D), BF16)]
        + [jax.ShapeDtypeStruct(f, sh.dtype) for f, sh in zip(fulls, shards)],
        compiler_params=_cp("arbitrary", "arbitrary"),
    )(order, x, mod, g_mix, w_shard, *shards)
    return res[0], res[1], res[2], res[3:]


P_WA, P_WB, P_CBIAS, P_BA, P_BX, P_LAM = 0, 3, 7, 8, 9, 10


def _lru_gates(rp, ip, ls, first_row):
    r = _sig(rp)
    ig = _sig(ip)
    la = LRU_C * r * ls
    a = jnp.exp(la)
    m2 = _neg_expm1(2.0 * la)
    mult = jnp.where(first_row, 1.0, jnp.sqrt(jnp.maximum(m2, 0.0)))
    return r, ig, la, a, m2, mult


def _mixer_fwd(proj, prm, wa, wx, shards, fulls, slicers):
    t_len = proj.shape[0]
    tt = min(TT, t_len)
    nt = t_len // tt
    na = len(shards)

    def body(proj_ref, prm_ref, wa_ref, wx_ref, *rest):
        ins, (mg_ref, hl_ref), outs = rest[:na], rest[na:na + 2], rest[na + 2:2 * na + 2]
        xe, ve, hc, u_s, ya_s, rp_s, ip_s = rest[2 * na + 2:2 * na + 9]
        start, forward, finish = _ag_phases(ins, outs, slicers, *rest[2 * na + 9:])
        t = pl.program_id(0)

        @pl.when(t == 0)
        def _():
            start()
            xe[0:8, :] = jnp.zeros((8, D), F32)
            ve[0:8, :] = jnp.zeros((8, D), F32)
            hc[...] = jnp.zeros((8, D), F32)

        @pl.when(t == nt // 2)
        def _():
            forward()

        xe[8:8 + tt, :] = _pj(proj_ref, 3).astype(F32)
        ve[8:8 + tt, :] = _pj(proj_ref, 1).astype(F32) * _pj(proj_ref, 2).astype(F32)
        u = prm_ref[P_CBIAS:P_CBIAS + 1, :] + prm_ref[P_WB:P_WB + 1, :] * xe[5:5 + tt, :]
        for k in range(1, 4):
            u = u + prm_ref[P_WB + k:P_WB + k + 1, :] * xe[5 + k:5 + k + tt, :]
        u_s[...] = u
        ya = prm_ref[P_WA:P_WA + 1, :] * ve[6:6 + tt, :]
        for k in range(1, 3):
            ya = ya + prm_ref[P_WA + k:P_WA + k + 1, :] * ve[6 + k:6 + k + tt, :]
        ya_s[...] = ya
        xe[0:8, :] = xe[tt:tt + 8, :]
        ve[0:8, :] = ve[tt:tt + 8, :]

        ub = u.astype(BF16)
        for h in range(HEADS):
            cs = slice(h * HB, (h + 1) * HB)
            rp_s[:, cs] = _dot(ub[:, cs], wa_ref[h]) + prm_ref[P_BA:P_BA + 1, cs]
            ip_s[:, cs] = _dot(ub[:, cs], wx_ref[h]) + prm_ref[P_BX:P_BX + 1, cs]

        ls_all = _log_sigmoid(prm_ref[P_LAM:P_LAM + 1, :])
        row = lax.broadcasted_iota(jnp.int32, (8, CG), 0)

        def blk(i, carry):
            r0 = pl.multiple_of(i * 16, 16)
            for g in range(D // CG):
                cs = slice(g * CG, (g + 1) * CG)
                ls = ls_all[:, cs]
                hprev = hc[:, cs]
                hs = []
                for sb in range(2):
                    rr = r0 + 8 * sb
                    first = (row + (t * tt + rr)) == 0
                    _, ig, _, a, _, mult = _lru_gates(rp_s[pl.ds(rr, 8), cs], ip_s[pl.ds(rr, 8), cs], ls, first)
                    b = mult * (ig * u_s[pl.ds(rr, 8), cs])
                    for s in (1, 2, 4):
                        a_sh = jnp.where(row >= s, pltpu.roll(a, s, 0), 1.0)
                        b_sh = jnp.where(row >= s, pltpu.roll(b, s, 0), 0.0)
                        b = a * b_sh + b
                        a = a * a_sh
                    hv = a * hprev + b
                    hprev = jnp.broadcast_to(hv[7:8, :], hv.shape)
                    hs.append(hv)
                hc[:, cs] = hprev
                h16 = jnp.concatenate(hs, axis=0)
                rows = pl.ds(r0, 16)
                gl, _ = _gelu(_pj(proj_ref, 4, rows, cs).astype(F32))
                y_b = h16 * gl
                y_a = _pj(proj_ref, 0, rows, cs).astype(F32) * ya_s[rows, cs]
                mg = (_sig(_pj(proj_ref, 5, rows, cs).astype(F32)) * y_a
                      + _sig(_pj(proj_ref, 6, rows, cs).astype(F32)) * y_b)
                mg_ref[rows, cs] = mg.astype(BF16)
                hl_ref[rows, cs] = h16.astype(BF16)
            return carry

        lax.fori_loop(0, tt // 16, blk, 0)

        @pl.when(t == nt - 1)
        def _():
            finish()

    res = pl.pallas_call(
        body, name="mixer_fwd", grid=(nt,),
        in_specs=[pl.BlockSpec((tt, 7 * D), lambda t: (t, 0)),
                  pl.BlockSpec((16, D), lambda t: (0, 0)),
                  pl.BlockSpec((HEADS, HB, HB), lambda t: (0, 0, 0)),
                  pl.BlockSpec((HEADS, HB, HB), lambda t: (0, 0, 0))] + [_ANY] * na,
        out_specs=[pl.BlockSpec((tt, D), lambda t: (t, 0)), pl.BlockSpec((tt, D), lambda t: (t, 0))] + [_ANY] * na,
        out_shape=[jax.ShapeDtypeStruct((t_len, D), BF16), jax.ShapeDtypeStruct((t_len, D), BF16)]
        + [jax.ShapeDtypeStruct(f, sh.dtype) for f, sh in zip(fulls, shards)],
        scratch_shapes=[pltpu.VMEM((tt + 8, D), F32), pltpu.VMEM((tt + 8, D), F32), pltpu.VMEM((8, D), F32),
                        pltpu.VMEM((tt, D), F32), pltpu.VMEM((tt, D), F32), pltpu.VMEM((tt, D), F32),
                        pltpu.VMEM((tt, D), F32)] + _ag_sems(na),
        compiler_params=_cp("arbitrary"),
    )(proj, prm, wa, wx, *shards)
    return res[0], res[1], res[2:]


def _out_proj(merged, x, mod, g_ffn, w_out):
    t_len = x.shape[0]
    tm = min(TM, t_len)

    def body(mg_ref, x_ref, mod_ref, g_ref, w_ref, x1_ref, h2_ref):
        gt1 = mod_ref[2:3, :]
        for rows in _sub_blocks(tm):
            x1_ref[rows, :] = x_ref[rows, :] + gt1 * _dot(mg_ref[rows, :], w_ref[...])
        gs = g_ref[...] * (1.0 + mod_ref[4:5, :])
        sh = mod_ref[3:4, :]

        def chunk(c, carry):
            rows = pl.ds(pl.multiple_of(c * 16, 16), 16)
            x1 = x1_ref[rows, :]
            r = lax.rsqrt(jnp.mean(x1 * x1, axis=-1, keepdims=True) + EPS)
            h2_ref[rows, :] = (x1 * r * gs + sh).astype(BF16)
            return carry

        lax.fori_loop(0, tm // 16, chunk, 0, unroll=UNROLL)

    return pl.pallas_call(
        body, name="out_proj", grid=(t_len // tm,),
        in_specs=[pl.BlockSpec((tm, D), lambda i: (i, 0)), pl.BlockSpec((tm, D), lambda i: (i, 0)),
                  pl.BlockSpec((8, D), lambda i: (0, 0)), pl.BlockSpec((1, D), lambda i: (0, 0)),
                  pl.BlockSpec((D, D), lambda i: (0, 0))],
        out_specs=[pl.BlockSpec((tm, D), lambda i: (i, 0)), pl.BlockSpec((tm, D), lambda i: (i, 0))],
        out_shape=[jax.ShapeDtypeStruct((t_len, D), F32), jax.ShapeDtypeStruct((t_len, D), BF16)],
        compiler_params=_cp("parallel"),
    )(merged, x, mod, g_ffn, w_out)


def _ffn_fwd(h2, x1, target, mod, g_fin, w_gu, w_down):
    t_len = x1.shape[0]
    tm = min(TMF, t_len)
    assert tm % (16 * UNROLL) == 0

    def body(h2_ref, x1_ref, tg_ref, mod_ref, g_ref, wgu_ref, wd_ref, gu_ref, dx2_ref, dx2b_ref, loss_ref, dg_ref, acc):
        @pl.when(pl.program_id(0) == 0)
        def _():
            loss_ref[...] = jnp.zeros_like(loss_ref)
            dg_ref[...] = jnp.zeros_like(dg_ref)

        hb = h2_ref[...]
        ffn = None
        for j in range(4):
            gate = _dot_nt(hb, wgu_ref[0, j])
            up = _dot_nt(hb, wgu_ref[1, j])
            gu_ref[0, j] = gate.astype(BF16)
            gu_ref[1, j] = up.astype(BF16)
            act = (gate * _sig(gate) * up).astype(BF16)
            part = _dot(act, wd_ref[j * FB:(j + 1) * FB, :])
            ffn = part if ffn is None else ffn + part
        acc[...] = ffn

        gt2 = mod_ref[5:6, :]
        gf = g_ref[...]

        def chunk(c, carry):
            s_loss, s_dg = carry
            for u in range(UNROLL):
                rows = pl.ds(pl.multiple_of(c * (16 * UNROLL), 16) + 16 * u, 16)
                x2 = x1_ref[rows, :] + gt2 * acc[rows, :]
                r = lax.rsqrt(jnp.mean(x2 * x2, axis=-1, keepdims=True) + EPS)
                xn = x2 * r
                diff = xn * gf - tg_ref[rows, :]
                dy = diff * (1.0 / D)
                dxn = dy * gf
                dx2 = r * (dxn - xn * jnp.mean(dxn * xn, axis=-1, keepdims=True))
                dx2_ref[rows, :] = dx2
                dx2b_ref[rows, :] = dx2.astype(BF16)
                s_loss, s_dg = s_loss + _fold8(diff * diff), s_dg + _fold8(dy * xn)
            return s_loss, s_dg

        zero = jnp.zeros((8, D), F32)
        s_loss, s_dg = lax.fori_loop(0, tm // (16 * UNROLL), chunk, (zero, zero))
        loss_ref[...] += jnp.sum(s_loss) * (0.5 / D)
        dg_ref[...] += jnp.sum(s_dg, axis=0, keepdims=True)

    row = pl.BlockSpec((tm, D), lambda i: (i, 0))
    return pl.pallas_call(
        body, name="ffn_fwd", grid=(t_len // tm,),
        in_specs=[row, row, row, pl.BlockSpec((8, D), lambda i: (0, 0)), pl.BlockSpec((1, D), lambda i: (0, 0)),
                  _resident((2, 4, FB, D)), _resident((DFF, D))],
        out_specs=[pl.BlockSpec((2, 4, tm, FB), lambda i: (0, 0, i, 0)), row, row,
                   pl.BlockSpec((1, 128), lambda i: (0, 0)), pl.BlockSpec((1, D), lambda i: (0, 0))],
        out_shape=[jax.ShapeDtypeStruct((2, 4, t_len, FB), BF16), jax.ShapeDtypeStruct((t_len, D), F32),
                   jax.ShapeDtypeStruct((t_len, D), BF16),
                   jax.ShapeDtypeStruct((1, 128), F32), jax.ShapeDtypeStruct((1, D), F32)],
        scratch_shapes=[pltpu.VMEM((tm, D), F32)],
        compiler_params=_cp("arbitrary"),
    )(h2, x1, target, mod, g_fin, w_gu, w_down)


S_SH, S_SC, S_G = 0, 1, 2


def _norm_bwd_rows(n_rows, rc, dh_ref, x_ref, dres_ref, scale, gain, sums_ref, write):
    assert n_rows % (rc * UNROLL) == 0
    gs = 1.0 + scale
    fold = _fold8 if rc == 16 else (lambda v: v)

    def chunk(c, carry):
        s_sh, s_sc, s_g = carry
        for u in range(UNROLL):
            rows = pl.ds(pl.multiple_of(c * (rc * UNROLL), rc) + rc * u, rc)
            dh = dh_ref[rows, :]
            xv = x_ref[rows, :]
            r = lax.rsqrt(jnp.mean(xv * xv, axis=-1, keepdims=True) + EPS)
            xn = xv * r
            dhn = dh * gs
            dxn = dhn * gain
            write(rows, dres_ref[rows, :] + r * (dxn - xn * jnp.mean(dxn * xn, axis=-1, keepdims=True)))
            s_sh, s_sc, s_g = s_sh + fold(dh), s_sc + fold(dh * (xn * gain)), s_g + fold(dhn * xn)
        return s_sh, s_sc, s_g

    zero = jnp.zeros((8, D), F32)
    s_sh, s_sc, s_g = lax.fori_loop(0, n_rows // (rc * UNROLL), chunk, (zero, zero, zero))
    sums_ref[S_SH:S_SH + 1, :] += jnp.sum(s_sh, axis=0, keepdims=True)
    sums_ref[S_SC:S_SC + 1, :] += jnp.sum(s_sc, axis=0, keepdims=True)
    sums_ref[S_G:S_G + 1, :] += jnp.sum(s_g, axis=0, keepdims=True)


def _ffn_bwd(dx2, gu, x1, mod, g_ffn, w_gu, w_down, w_out):
    t_len = x1.shape[0]
    tm = min(TMF, t_len)

    def body(dx2_ref, gu_ref, x1_ref, mod_ref, g_ref, wgu_ref, wd_ref, wo_ref,
             dgu_ref, act_ref, dx1_ref, dx1b_ref, dmg_ref, sums_ref, acc, dmo):
        @pl.when(pl.program_id(0) == 0)
        def _():
            sums_ref[...] = jnp.zeros_like(sums_ref)

        dffn = (dx2_ref[...] * mod_ref[5:6, :]).astype(BF16)
        dh2 = None
        for j in range(4):
            dact = _dot_nt(dffn, wd_ref[j * FB:(j + 1) * FB, :])
            gate = gu_ref[0, j].astype(F32)
            up = gu_ref[1, j].astype(F32)
            sg = _sig(gate)
            silu = gate * sg
            act_ref[j] = (silu * up).astype(BF16)
            dgate = (dact * up * (sg * (1.0 + gate * (1.0 - sg)))).astype(BF16)
            dup = (dact * silu).astype(BF16)
            dgu_ref[0, j] = dgate
            dgu_ref[1, j] = dup
            part = _dot(dgate, wgu_ref[0, j]) + _dot(dup, wgu_ref[1, j])
            dh2 = part if dh2 is None else dh2 + part
        acc[...] = dh2

        gt1 = mod_ref[2:3, :]

        def write(rows, dx1):
            dx1_ref[rows, :] = dx1
            dx1b_ref[rows, :] = dx1.astype(BF16)
            dmo[rows, :] = (dx1 * gt1).astype(BF16)

        _norm_bwd_rows(tm, 16, acc, x1_ref, dx2_ref, mod_ref[4:5, :], g_ref[...], sums_ref, write)
        dmg_ref[...] = _dot_nt(dmo[...], wo_ref[...]).astype(BF16)

    row = pl.BlockSpec((tm, D), lambda i: (i, 0))
    return pl.pallas_call(
        body, name="ffn_bwd", grid=(t_len // tm,),
        in_specs=[row, pl.BlockSpec((2, 4, tm, FB), lambda i: (0, 0, i, 0)), row,
                  pl.BlockSpec((8, D), lambda i: (0, 0)), pl.BlockSpec((1, D), lambda i: (0, 0)),
                  _resident((2, 4, FB, D)), _resident((DFF, D)), _resident((D, D))],
        out_specs=[pl.BlockSpec((2, 4, tm, FB), lambda i: (0, 0, i, 0)),
                   pl.BlockSpec((4, tm, FB), lambda i: (0, i, 0)), row, row, row,
                   pl.BlockSpec((8, D), lambda i: (0, 0))],
        out_shape=[jax.ShapeDtypeStruct((2, 4, t_len, FB), BF16), jax.ShapeDtypeStruct((4, t_len, FB), BF16),
                   jax.ShapeDtypeStruct((t_len, D), F32), jax.ShapeDtypeStruct((t_len, D), BF16),
                   jax.ShapeDtypeStruct((t_len, D), BF16), jax.ShapeDtypeStruct((8, D), F32)],
        scratch_shapes=[pltpu.VMEM((tm, D), F32), pltpu.VMEM((tm, D), BF16)],
        compiler_params=_cp("arbitrary"),
    )(dx2, gu, x1, mod, g_ffn, w_gu, w_down, w_out)


def _my_pos():
    return lax.axis_index("x"), lax.axis_index("y"), lax.axis_index("c")


def _my_index():
    x, y, c = _my_pos()
    return 4 * x + 2 * y + c


def _device_of(b):
    return (b >> 2) & 1, (b >> 1) & 1, b & 1


def _rs_send(src, parts_ref, b, send_sems, recv_sems, local_sem):
    me = _my_index()
    dst = parts_ref.at[me]

    @pl.when(b == me)
    def _():
        pltpu.make_async_copy(src, dst, local_sem).start()

    @pl.when(b != me)
    def _():
        pltpu.make_async_remote_copy(src_ref=src, dst_ref=dst, send_sem=send_sems.at[b], recv_sem=recv_sems.at[me],
                                     device_id=_device_of(b), device_id_type=MESH).start()


def _rs_finish(src_of, parts_ref, send_sems, recv_sems, local_sem):
    me = _my_index()
    for s in range(NDEV):
        @pl.when(s != me)
        def _():
            cp = pltpu.make_async_remote_copy(src_ref=src_of(s), dst_ref=parts_ref.at[s], send_sem=send_sems.at[s],
                                              recv_sem=recv_sems.at[s], device_id=_device_of(s), device_id_type=MESH)
            cp.wait_send()
            cp.wait_recv()

        @pl.when(s == me)
        def _():
            pltpu.make_async_copy(src_of(s), parts_ref.at[s], local_sem).wait()


_RS_SEMS = [pltpu.SemaphoreType.DMA((NDEV,)), pltpu.SemaphoreType.DMA((NDEV,)), pltpu.SemaphoreType.DMA]
_ANY = pl.BlockSpec(memory_space=pl.ANY)


def _xor_order(me, n):
    return (me ^ (n - 1 - jnp.arange(n, dtype=jnp.int32))).astype(jnp.int32)


NCHIP = NDEV // 2


def _rs2_scratch(half_shape):
    blocks = lambda *lead: pltpu.VMEM(lead + tuple(half_shape), BF16)
    return [blocks(NCHIP, 2), blocks(NCHIP)] + [pltpu.SemaphoreType.DMA((NCHIP,))] * 4 + [pltpu.SemaphoreType.DMA]


def _rs2_to_sibling(q, rs):
    stage, from_sib, d_send, d_recv = rs[:4]
    x, y, c = _my_pos()
    pltpu.make_async_remote_copy(src_ref=stage.at[q, 1 - c], dst_ref=from_sib.at[q], send_sem=d_send.at[q],
                                 recv_sem=d_recv.at[q], device_id=(x, y, 1 - c), device_id_type=MESH).start()


def _rs2_forward(q, parts_ref, rs):
    stage, chip_sum, d_send, d_recv, i_send, i_recv, local_sem = rs
    x, y, c = _my_pos()
    my_chip = 2 * x + y
    pltpu.make_async_remote_copy(src_ref=stage.at[q, c], dst_ref=chip_sum.at[q], send_sem=d_send.at[q],
                                 recv_sem=d_recv.at[q], device_id=(x, y, 1 - c), device_id_type=MESH).wait_recv()
    chip_sum[q] = (stage[q, c].astype(F32) + chip_sum[q].astype(F32)).astype(BF16)

    @pl.when(q == my_chip)
    def _():
        pltpu.make_async_copy(chip_sum.at[q], parts_ref.at[my_chip], local_sem).start()

    @pl.when(q != my_chip)
    def _():
        pltpu.make_async_remote_copy(src_ref=chip_sum.at[q], dst_ref=parts_ref.at[my_chip], send_sem=i_send.at[q],
                                     recv_sem=i_recv.at[my_chip], device_id=((q >> 1) & 1, q & 1, c),
                                     device_id_type=MESH).start()


def _rs2_finish(parts_ref, rs):
    stage, chip_sum, d_send, d_recv, i_send, i_recv, local_sem = rs
    x, y, c = _my_pos()
    my_chip = 2 * x + y
    for q in range(NCHIP):
        pltpu.make_async_remote_copy(src_ref=stage.at[q, 1 - c], dst_ref=chip_sum.at[q], send_sem=d_send.at[q],
                                     recv_sem=d_recv.at[q], device_id=(x, y, 1 - c), device_id_type=MESH).wait_send()

        @pl.when(q != my_chip)
        def _():
            cp = pltpu.make_async_remote_copy(src_ref=chip_sum.at[q], dst_ref=parts_ref.at[q], send_sem=i_send.at[q],
                                              recv_sem=i_recv.at[q], device_id=((q >> 1) & 1, q & 1, c),
                                              device_id_type=MESH)
            cp.wait_send()
            cp.wait_recv()

        @pl.when(q == my_chip)
        def _():
            pltpu.make_async_copy(chip_sum.at[q], parts_ref.at[q], local_sem).wait()


def _gu_wgrad(h2, dgu, order):
    t_len = h2.shape[0]
    tk = min(TK, t_len)
    nk = t_len // tk

    def body(ord_ref, h_ref, d_ref, parts_ref, acc, *rs):
        p, k = pl.program_id(0), pl.program_id(1)

        @pl.when(k == 0)
        def _():
            acc[...] = jnp.zeros_like(acc)

        hb = h_ref[...]
        for half in range(2):
            acc[half] += _dot_tn(d_ref[0, half], hb)

        @pl.when(k == nk - 1)
        def _():
            q = ord_ref[p]
            rs[0][q] = acc[...].astype(BF16)
            _rs2_to_sibling(q, rs)

        @pl.when((k == nk - 1) & (p > 0))
        def _():
            _rs2_forward(ord_ref[p - 1], parts_ref, rs)

        @pl.when((p == NCHIP - 1) & (k == nk - 1))
        def _():
            _rs2_forward(ord_ref[p], parts_ref, rs)
            _rs2_finish(parts_ref, rs)

    return pl.pallas_call(
        body, name="gu_wgrad",
        grid_spec=pltpu.PrefetchScalarGridSpec(
            num_scalar_prefetch=1, grid=(NCHIP, nk),
            in_specs=[pl.BlockSpec((tk, D), lambda p, k, o: (k, 0)),
                      pl.BlockSpec((1, 2, tk, FB), lambda p, k, o: (o[p], 0, k, 0))],
            out_specs=_ANY,
            scratch_shapes=[pltpu.VMEM((2, FB, D), F32)] + _rs2_scratch((FB, D))),
        out_shape=jax.ShapeDtypeStruct((NCHIP, FB, D), BF16),
        compiler_params=_cp("arbitrary", "arbitrary"),
    )(order, h2, dgu.reshape(NCHIP, 2, t_len, FB))


def _scaled_wgrad(name, a, dx, w, gate_row, mod, order):
    nb, t_len, kb = a.shape
    tk = min(TK, t_len)
    nk = t_len // tk
    cpb = NCHIP // nb
    rows = kb // (2 * cpb)

    def body(ord_ref, a_ref, dx_ref, w_ref, mod_ref, parts_ref, dg_ref, acc, *rs):
        p, k = pl.program_id(0), pl.program_id(1)
        j = ord_ref[p]

        @pl.when((p == 0) & (k == 0))
        def _():
            dg_ref[...] = jnp.zeros_like(dg_ref)

        @pl.when(k == 0)
        def _():
            acc[...] = jnp.zeros_like(acc)

        acc[...] += _dot_tn(a_ref[0], dx_ref[...])

        @pl.when(k == nk - 1)
        def _():
            z = acc[...]
            zg = (z * mod_ref[gate_row:gate_row + 1, :]).astype(BF16)
            dg_ref[0:1, :] += jnp.sum(z * w_ref[...].astype(F32), axis=0, keepdims=True)
            for i in range(cpb):
                q = j * cpb + i
                for half in range(2):
                    rs[0][q, half] = zg[(2 * i + half) * rows:(2 * i + half + 1) * rows]
                _rs2_to_sibling(q, rs)

        if cpb == 1:
            @pl.when((k == nk - 1) & (p > 0))
            def _():
                _rs2_forward(ord_ref[p - 1], parts_ref, rs)

        @pl.when((p == nb - 1) & (k == nk - 1))
        def _():
            for i in range(cpb):
                _rs2_forward(j * cpb + i, parts_ref, rs)
            _rs2_finish(parts_ref, rs)

    return pl.pallas_call(
        body, name=name,
        grid_spec=pltpu.PrefetchScalarGridSpec(
            num_scalar_prefetch=1, grid=(nb, nk),
            in_specs=[pl.BlockSpec((1, tk, kb), lambda p, k, o: (o[p], k, 0)),
                      pl.BlockSpec((tk, D), lambda p, k, o: (k, 0)),
                      pl.BlockSpec((kb, D), lambda p, k, o: (o[p], 0)),
                      pl.BlockSpec((8, D), lambda p, k, o: (0, 0))],
            out_specs=[_ANY, pl.BlockSpec((8, D), lambda p, k, o: (0, 0))],
            scratch_shapes=[pltpu.VMEM((kb, D), F32)] + _rs2_scratch((rows, D))),
        out_shape=[jax.ShapeDtypeStruct((NCHIP, rows, D), BF16), jax.ShapeDtypeStruct((8, D), F32)],
        compiler_params=_cp("arbitrary", "arbitrary"),
    )(order, a, dx, w, mod)


M_WA, M_WB, M_CBIAS, M_BA, M_BX, M_LS = 0, 3, 7, 8, 9, 10


def _mixer_bwd(proj, hl, dmg, prm, wa, wx):
    t_len = proj.shape[0]
    tt = min(TT, t_len)
    nt = t_len // tt
    hb8 = tt // 8

    def rev(i):
        return nt - 1 - i

    def halo(i):
        return jnp.maximum(rev(i) * hb8 - 1, 0)

    def body(proj_ref, ph_ref, hl_ref, hh_ref, dmg_ref, prm_ref, wa_ref, wx_ref,
             dp_ref, sums_ref, gwa_ref, gwx_ref,
             xe, ve, he, u_s, ya_s, rp_s, ip_s, due, dye, drp_s, dip_s, an, gn):
        i = pl.program_id(0)
        t = rev(i)

        @pl.when(i == 0)
        def _():
            sums_ref[...] = jnp.zeros_like(sums_ref)
            gwa_ref[...] = jnp.zeros_like(gwa_ref)
            gwx_ref[...] = jnp.zeros_like(gwx_ref)
            due[tt:tt + 8, :] = jnp.zeros((8, D), F32)
            dye[tt:tt + 8, :] = jnp.zeros((8, D), F32)
            an[...] = jnp.zeros((8, D), F32)
            gn[...] = jnp.zeros((8, D), F32)

        live = (t > 0).astype(F32)
        xe[0:8, :] = _pj(ph_ref, 3).astype(F32) * live
        ve[0:8, :] = _pj(ph_ref, 1).astype(F32) * _pj(ph_ref, 2).astype(F32) * live
        he[0:8, :] = hh_ref[...].astype(F32) * live
        xe[8:8 + tt, :] = _pj(proj_ref, 3).astype(F32)
        ve[8:8 + tt, :] = _pj(proj_ref, 1).astype(F32) * _pj(proj_ref, 2).astype(F32)
        he[8:8 + tt, :] = hl_ref[...].astype(F32)
        u = prm_ref[P_CBIAS:P_CBIAS + 1, :] + prm_ref[P_WB:P_WB + 1, :] * xe[5:5 + tt, :]
        for k in range(1, 4):
            u = u + prm_ref[P_WB + k:P_WB + k + 1, :] * xe[5 + k:5 + k + tt, :]
        u_s[...] = u
        ya = prm_ref[P_WA:P_WA + 1, :] * ve[6:6 + tt, :]
        for k in range(1, 3):
            ya = ya + prm_ref[P_WA + k:P_WA + k + 1, :] * ve[6 + k:6 + k + tt, :]
        ya_s[...] = ya
        ub = u.astype(BF16)
        for h in range(HEADS):
            cs = slice(h * HB, (h + 1) * HB)
            rp_s[:, cs] = _dot(ub[:, cs], wa_ref[h]) + prm_ref[P_BA:P_BA + 1, cs]
            ip_s[:, cs] = _dot(ub[:, cs], wx_ref[h]) + prm_ref[P_BX:P_BX + 1, cs]

        ls_all = _log_sigmoid(prm_ref[P_LAM:P_LAM + 1, :])
        row = lax.broadcasted_iota(jnp.int32, (8, CG), 0)
        nblk = tt // 16

        def blk(ib, carry):
            r0 = pl.multiple_of((nblk - 1 - ib) * 16, 16)
            rows = pl.ds(r0, 16)
            for g in range(D // CG):
                cs = slice(g * CG, (g + 1) * CG)
                ls = ls_all[:, cs]
                dm = dmg_ref[rows, cs].astype(F32)
                cb = _pj(proj_ref, 0, rows, cs).astype(F32)
                rg = _pj(proj_ref, 4, rows, cs).astype(F32)
                sga = _sig(_pj(proj_ref, 5, rows, cs).astype(F32))
                sgb = _sig(_pj(proj_ref, 6, rows, cs).astype(F32))
                ya0 = ya_s[rows, cs]
                h16 = he[pl.ds(r0 + 8, 16), cs]
                gl, th = _gelu(rg)
                dgl = 0.5 * (1.0 + th) + 0.5 * rg * (1.0 - th * th) * (_GC * (1.0 + 3.0 * 0.044715 * rg * rg))
                y_a = cb * ya0
                y_b = h16 * gl
                dy_a = dm * sga
                dy_b = dm * sgb
                col = lambda s: slice(s * D + g * CG, s * D + (g + 1) * CG)
                dp_ref[rows, col(5)] = (dm * y_a * sga * (1.0 - sga)).astype(BF16)
                dp_ref[rows, col(6)] = (dm * y_b * sgb * (1.0 - sgb)).astype(BF16)
                dp_ref[rows, col(4)] = (dy_b * h16 * dgl).astype(BF16)
                dp_ref[rows, col(0)] = (dy_a * ya0).astype(BF16)
                dye[rows, cs] = dy_a * cb
                dh16 = dy_b * gl

                a_next = an[:, cs]
                g_next = gn[:, cs]
                s_ba = jnp.zeros((8, CG), F32)
                s_bx = jnp.zeros((8, CG), F32)
                s_ls = jnp.zeros((8, CG), F32)
                for sb in (1, 0):
                    rr = r0 + 8 * sb
                    first = (row + (t * tt + rr)) == 0
                    uu = u_s[pl.ds(rr, 8), cs]
                    r, ig, la, a, m2, mult = _lru_gates(rp_s[pl.ds(rr, 8), cs], ip_s[pl.ds(rr, 8), cs], ls, first)
                    ca = jnp.where(row < 7, pltpu.roll(a, 7, 0), a_next)
                    cb_ = dh16[8 * sb:8 * sb + 8, :]
                    for s in (1, 2, 4):
                        a_sh = jnp.where(row < 8 - s, pltpu.roll(ca, 8 - s, 0), 1.0)
                        b_sh = jnp.where(row < 8 - s, pltpu.roll(cb_, 8 - s, 0), 0.0)
                        cb_ = ca * b_sh + cb_
                        ca = ca * a_sh
                    gv = ca * g_next + cb_
                    g_next = jnp.broadcast_to(gv[0:1, :], gv.shape)
                    a_next = jnp.broadcast_to(a[0:1, :], a.shape)
                    hprev = jnp.where(row >= 1, pltpu.roll(he[pl.ds(rr + 8, 8), cs], 1, 0),
                                      pltpu.roll(he[pl.ds(rr, 8), cs], 1, 0))
                    da = gv * hprev
                    dmult = jnp.where(first, 0.0, gv * ig * uu)
                    dla = da * a + jnp.where(m2 > 0.0, dmult * (-(a * a) / mult), 0.0)
                    drp = dla * (LRU_C * ls) * r * (1.0 - r)
                    dip = gv * mult * uu * ig * (1.0 - ig)
                    s_ls = s_ls + dla * (LRU_C * r)
                    s_ba = s_ba + drp
                    s_bx = s_bx + dip
                    drp_s[pl.ds(rr, 8), cs] = drp
                    dip_s[pl.ds(rr, 8), cs] = dip
                    due[pl.ds(rr, 8), cs] = gv * mult * ig
                an[:, cs] = a_next
                gn[:, cs] = g_next
                sums_ref[M_BA:M_BA + 1, cs] += jnp.sum(s_ba, axis=0, keepdims=True)
                sums_ref[M_BX:M_BX + 1, cs] += jnp.sum(s_bx, axis=0, keepdims=True)
                sums_ref[M_LS:M_LS + 1, cs] += jnp.sum(s_ls, axis=0, keepdims=True)
            return carry

        lax.fori_loop(0, nblk, blk, 0)

        drp_b = drp_s[...].astype(BF16)
        dip_b = dip_s[...].astype(BF16)
        for h in range(HEADS):
            cs = slice(h * HB, (h + 1) * HB)
            due[0:tt, cs] += _dot_nt(drp_b[:, cs], wa_ref[h]) + _dot_nt(dip_b[:, cs], wx_ref[h])
            gwa_ref[h] += _dot_tn(ub[:, cs], drp_b[:, cs])
            gwx_ref[h] += _dot_tn(ub[:, cs], dip_b[:, cs])

        du = due[0:tt, :]
        sums_ref[M_CBIAS:M_CBIAS + 1, :] += jnp.sum(du, axis=0, keepdims=True)
        drx = prm_ref[P_WB:P_WB + 1, :] * due[3:3 + tt, :]
        sums_ref[M_WB:M_WB + 1, :] += jnp.sum(du * xe[5:5 + tt, :], axis=0, keepdims=True)
        for k in range(1, 4):
            drx = drx + prm_ref[P_WB + k:P_WB + k + 1, :] * due[3 - k:3 - k + tt, :]
            sums_ref[M_WB + k:M_WB + k + 1, :] += jnp.sum(du * xe[5 + k:5 + k + tt, :], axis=0, keepdims=True)
        dp_ref[:, 3 * D:4 * D] = drx.astype(BF16)
        dya = dye[0:tt, :]
        dv = prm_ref[P_WA:P_WA + 1, :] * dye[2:2 + tt, :]
        sums_ref[M_WA:M_WA + 1, :] += jnp.sum(dya * ve[6:6 + tt, :], axis=0, keepdims=True)
        for k in range(1, 3):
            dv = dv + prm_ref[P_WA + k:P_WA + k + 1, :] * dye[2 - k:2 - k + tt, :]
            sums_ref[M_WA + k:M_WA + k + 1, :] += jnp.sum(dya * ve[6 + k:6 + k + tt, :], axis=0, keepdims=True)
        dp_ref[:, D:2 * D] = (dv * _pj(proj_ref, 2).astype(F32)).astype(BF16)
        dp_ref[:, 2 * D:3 * D] = (dv * _pj(proj_ref, 1).astype(F32)).astype(BF16)
        due[tt:tt + 8, :] = due[0:8, :]
        dye[tt:tt + 8, :] = dye[0:8, :]

        @pl.when(i == nt - 1)
        def _():
            sums_ref[M_LS:M_LS + 1, :] = sums_ref[M_LS:M_LS + 1, :] * _sig(-prm_ref[P_LAM:P_LAM + 1, :])

    big = lambda: pltpu.VMEM((tt + 8, D), F32)
    tile = lambda: pltpu.VMEM((tt, D), F32)
    return pl.pallas_call(
        body, name="mixer_bwd", grid=(nt,),
        in_specs=[pl.BlockSpec((tt, 7 * D), lambda i: (rev(i), 0)),
                  pl.BlockSpec((8, 7 * D), lambda i: (halo(i), 0)),
                  pl.BlockSpec((tt, D), lambda i: (rev(i), 0)),
                  pl.BlockSpec((8, D), lambda i: (halo(i), 0)),
                  pl.BlockSpec((tt, D), lambda i: (rev(i), 0)),
                  pl.BlockSpec((16, D), lambda i: (0, 0)),
                  pl.BlockSpec((HEADS, HB, HB), lambda i: (0, 0, 0)),
                  pl.BlockSpec((HEADS, HB, HB), lambda i: (0, 0, 0))],
        out_specs=[pl.BlockSpec((tt, 7 * D), lambda i: (rev(i), 0)),
                   pl.BlockSpec((16, D), lambda i: (0, 0)),
                   pl.BlockSpec((HEADS, HB, HB), lambda i: (0, 0, 0)),
                   pl.BlockSpec((HEADS, HB, HB), lambda i: (0, 0, 0))],
        out_shape=[jax.ShapeDtypeStruct((t_len, 7 * D), BF16), jax.ShapeDtypeStruct((16, D), F32),
                   jax.ShapeDtypeStruct((HEADS, HB, HB), F32), jax.ShapeDtypeStruct((HEADS, HB, HB), F32)],
        scratch_shapes=[big(), big(), big(), tile(), tile(), tile(), tile(), big(), big(), tile(), tile(),
                        pltpu.VMEM((8, D), F32), pltpu.VMEM((8, D), F32)],
        compiler_params=_cp("arbitrary"),
    )(proj, proj, hl, hl, dmg, prm, wa, wx)


def _in_proj_bwd(dproj, w_in, x, dx1, mod, g_mix):
    t_len = x.shape[0]
    tm = min(TM, t_len)

    def body(dp_ref, w_ref, x_ref, dx1_ref, mod_ref, g_ref, gx_ref, sums_ref, acc):
        @pl.when(pl.program_id(0) == 0)
        def _():
            sums_ref[...] = jnp.zeros_like(sums_ref)

        for rows in _sub_blocks(tm):
            acc[rows, :] = _dot_nt(dp_ref[rows, :], w_ref[...])

        def write(rows, dx):
            gx_ref[rows, :] = dx

        _norm_bwd_rows(tm, 16, acc, x_ref, dx1_ref, mod_ref[1:2, :], g_ref[...], sums_ref, write)

    return pl.pallas_call(
        body, name="in_proj_bwd", grid=(t_len // tm,),
        in_specs=[pl.BlockSpec((tm, 7 * D), lambda i: (i, 0)),
                  _resident((D, 7 * D)),
                  pl.BlockSpec((tm, D), lambda i: (i, 0)), pl.BlockSpec((tm, D), lambda i: (i, 0)),
                  pl.BlockSpec((8, D), lambda i: (0, 0)), pl.BlockSpec((1, D), lambda i: (0, 0))],
        out_specs=[pl.BlockSpec((tm, D), lambda i: (i, 0)), pl.BlockSpec((8, D), lambda i: (0, 0))],
        out_shape=[jax.ShapeDtypeStruct((t_len, D), F32), jax.ShapeDtypeStruct((8, D), F32)],
        scratch_shapes=[pltpu.VMEM((tm, D), F32)],
        compiler_params=_cp("arbitrary"),
    )(dproj, w_in, x, dx1, mod, g_mix)


def _in_wgrad(h, dproj, g_wa, g_wx, order):
    t_len = h.shape[0]
    tk = min(TKI, t_len)
    nk = t_len // tk
    cw = 7 * D // NDEV
    hr = HB // NDEV

    def body(ord_ref, h_ref, d_ref, ga_ref, gx_ref, parts_ref, pa_ref, px_ref, acc, *scr):
        rs, sems = scr[:-6], scr[-6:]
        p, k = pl.program_id(0), pl.program_id(1)

        def head_rows(ref):
            return lambda s: ref.at[:, pl.ds(s * hr, hr), :]

        @pl.when((p == 0) & (k == 0))
        def _():
            for s in range(NDEV):
                _rs_send(head_rows(ga_ref)(s), pa_ref, s, *sems[0:3])
                _rs_send(head_rows(gx_ref)(s), px_ref, s, *sems[3:6])

        @pl.when(k == 0)
        def _():
            acc[...] = jnp.zeros_like(acc)

        acc[...] += _dot_tn(h_ref[...], d_ref[...])

        @pl.when(k == nk - 1)
        def _():
            q = ord_ref[p]
            for half in range(2):
                rs[0][q, half] = acc[:, half * cw:(half + 1) * cw].astype(BF16)
            _rs2_to_sibling(q, rs)

        @pl.when((k == nk - 1) & (p > 0))
        def _():
            _rs2_forward(ord_ref[p - 1], parts_ref, rs)

        @pl.when((p == NCHIP - 1) & (k == nk - 1))
        def _():
            _rs2_forward(ord_ref[p], parts_ref, rs)
            _rs2_finish(parts_ref, rs)
            _rs_finish(head_rows(ga_ref), pa_ref, *sems[0:3])
            _rs_finish(head_rows(gx_ref), px_ref, *sems[3:6])

    return pl.pallas_call(
        body, name="in_wgrad",
        grid_spec=pltpu.PrefetchScalarGridSpec(
            num_scalar_prefetch=1, grid=(NCHIP, nk),
            in_specs=[pl.BlockSpec((tk, D), lambda p, k, o: (k, 0)),
                      pl.BlockSpec((tk, 2 * cw), lambda p, k, o: (k, o[p])), _ANY, _ANY],
            out_specs=[_ANY, _ANY, _ANY],
            scratch_shapes=[pltpu.VMEM((D, 2 * cw), F32)] + _rs2_scratch((D, cw)) + _RS_SEMS * 2),
        out_shape=[jax.ShapeDtypeStruct((NCHIP, D, cw), BF16), jax.ShapeDtypeStruct((NDEV, HEADS, hr, HB), F32),
                   jax.ShapeDtypeStruct((NDEV, HEADS, hr, HB), F32)],
        compiler_params=_cp("arbitrary", "arbitrary"),
    )(order, h, dproj, g_wa, g_wx)


def _ada_fwd(c_all, w_ada, b_cols):
    def body(c_ref, w_ref, b_ref, o_ref):
        cv = c_ref[...]
        o_ref[...] = _dot((cv * _sig(cv)).astype(BF16), w_ref[...].astype(BF16)) + b_ref[...]

    return pl.pallas_call(body, name="ada_fwd", out_shape=jax.ShapeDtypeStruct((16, w_ada.shape[1]), F32),
                          compiler_params=_cp())(c_all, w_ada, b_cols)


def _adam_math(w, g, m, v):
    m = ADAM_B1 * m + (1.0 - ADAM_B1) * g
    v = ADAM_B2 * v + (1.0 - ADAM_B2) * (g * g)
    m_hat = m / (1.0 - ADAM_B1 ** ADAM_STEP)
    v_hat = v / (1.0 - ADAM_B2 ** ADAM_STEP)
    delta = -ADAM_LR * (m_hat / (jnp.sqrt(v_hat) + ADAM_EPS) + ADAM_WD * w)
    return delta, m, v


def _ada_bwd(c_all, dmod_cols, w, m, v):
    rb = 256
    n = w.shape[1]
    nrow = c_all.shape[0]

    def body(c_ref, d_ref, w_ref, m_ref, v_ref, g_ref, dl_ref, nm_ref, nv_ref):
        cv = c_ref[...]
        g = _dot_tn((cv * _sig(cv)).astype(BF16), d_ref[...].astype(BF16))
        g_ref[...] = g
        dl_ref[...], nm_ref[...], nv_ref[...] = _adam_math(w_ref[...], g, m_ref[...], v_ref[...])

    blk = pl.BlockSpec((rb, n), lambda i: (i, 0))
    sds = jax.ShapeDtypeStruct(w.shape, F32)
    return pl.pallas_call(
        body, name="ada_bwd", grid=(D // rb,),
        in_specs=[pl.BlockSpec((nrow, rb), lambda i: (0, i)), pl.BlockSpec((nrow, n), lambda i: (0, 0)), blk, blk, blk],
        out_specs=[blk, blk, blk, blk], out_shape=[sds, sds, sds, sds],
        compiler_params=_cp("parallel"),
    )(c_all, dmod_cols, w, m, v)


def _adam(name, parts, w, m, v):
    p, r, c = parts.shape
    rb = r
    for cand in (256, 128, 64, 32, 16, 8):
        if r % cand == 0 and r >= cand:
            rb = cand
            break

    def body(p_ref, w_ref, m_ref, v_ref, g_ref, dl_ref, nm_ref, nv_ref):
        g = p_ref[0].astype(F32)
        for q in range(1, p):
            g = g + p_ref[q].astype(F32)
        g_ref[...] = g
        dl_ref[...], nm_ref[...], nv_ref[...] = _adam_math(w_ref[...], g, m_ref[...], v_ref[...])

    blk = pl.BlockSpec((rb, c), lambda i: (i, 0))
    sds = jax.ShapeDtypeStruct((r, c), F32)
    return pl.pallas_call(
        body, name=name, grid=(r // rb,),
        in_specs=[pl.BlockSpec((p, rb, c), lambda i: (0, i, 0)), blk, blk, blk],
        out_specs=[blk, blk, blk, blk], out_shape=[sds, sds, sds, sds],
        compiler_params=_cp("parallel"),
    )(parts, w, m, v)


def _my_pos():
    return lax.axis_index("x"), lax.axis_index("y"), lax.axis_index("c")


def _all_gather_small(name, v):
    m_per, n = v.shape

    def body(x_ref, out_ref, send_sems, recv_sems, local_sem):
        x, y, c = _my_pos()
        me, sibling = (x, y, c), (x, y, 1 - c)
        chips = [(1 - x, y), (x, 1 - y), (1 - x, 1 - y)]

        def rows(px, py, pc):
            return out_ref.at[pl.ds((4 * px + 2 * py + pc) * m_per, m_per), :]

        def copy(k, block, to, src=None):
            return pltpu.make_async_remote_copy(
                src_ref=rows(*block) if src is None else src, dst_ref=rows(*block),
                send_sem=send_sems.at[k], recv_sem=recv_sems.at[k], device_id=to, device_id_type=MESH)

        mine = pltpu.make_async_copy(x_ref, rows(*me), local_sem)
        mine.start()
        first = [copy(0, me, sibling, src=x_ref)]
        first += [copy(1 + j, me, (*chip, c), src=x_ref) for j, chip in enumerate(chips)]
        for cp in first:
            cp.start()
        passed = [copy(4 + j, (*chip, c), sibling) for j, chip in enumerate(chips)]
        for j, chip in enumerate(chips):
            copy(1 + j, (*chip, c), me).wait_recv()
            passed[j].start()
        copy(0, sibling, me).wait_recv()
        for j, chip in enumerate(chips):
            copy(4 + j, (*chip, 1 - c), me).wait_recv()
        for cp in first + passed:
            cp.wait_send()
        mine.wait()

    return pl.pallas_call(
        body, name=name, out_shape=jax.ShapeDtypeStruct((NDEV * m_per, n), v.dtype),
        in_specs=[pl.BlockSpec(memory_space=pltpu.VMEM)], out_specs=pl.BlockSpec(memory_space=pltpu.VMEM),
        scratch_shapes=[pltpu.SemaphoreType.DMA((7,)), pltpu.SemaphoreType.DMA((7,)), pltpu.SemaphoreType.DMA],
    )(v)


def _blk_cols(n):
    return lambda ref, b: ref.at[:, pl.ds(pl.multiple_of(b * n, 128), n)]


def _blk_rows(n):
    return lambda ref, b: ref.at[pl.ds(pl.multiple_of(b * n, 8), n), :]


def _blk_lead(ref, b):
    return ref.at[b]


def _blk_heads(ref, b):
    return ref.at[:, pl.ds(pl.multiple_of(b * (HB // NDEV), 8), HB // NDEV), :]


def _ag_phases(ins, outs, slicers, send_sems, recv_sems, local_sems):
    na = len(ins)
    x, y, c = _my_pos()
    me, sibling = (x, y, c), (x, y, 1 - c)
    chips = [(1 - x, y), (x, 1 - y), (1 - x, 1 - y)]

    def copy(a, k, block, to, from_shard=False):
        px, py, pc = block
        dst = slicers[a](outs[a], 4 * px + 2 * py + pc)
        return pltpu.make_async_remote_copy(
            src_ref=ins[a] if from_shard else dst, dst_ref=dst,
            send_sem=send_sems.at[a * 7 + k], recv_sem=recv_sems.at[a * 7 + k], device_id=to, device_id_type=MESH)

    def local(a):
        return pltpu.make_async_copy(ins[a], slicers[a](outs[a], 4 * x + 2 * y + c), local_sems.at[a])

    def firsts(a):
        return [copy(a, 0, me, sibling, True)] + [copy(a, 1 + j, me, (*chip, c), True) for j, chip in enumerate(chips)]

    def start():
        for a in range(na):
            local(a).start()
            for cp in firsts(a):
                cp.start()

    def forward():
        for a in range(na):
            for j, chip in enumerate(chips):
                copy(a, 1 + j, (*chip, c), me).wait_recv()
                copy(a, 4 + j, (*chip, c), sibling).start()

    def finish():
        for a in range(na):
            copy(a, 0, sibling, me).wait_recv()
            for j, chip in enumerate(chips):
                copy(a, 4 + j, (*chip, 1 - c), me).wait_recv()
        for a in range(na):
            for cp in firsts(a) + [copy(a, 4 + j, (*chip, c), sibling) for j, chip in enumerate(chips)]:
                cp.wait_send()
            local(a).wait()

    return start, forward, finish


def _ag_sems(na):
    return [pltpu.SemaphoreType.DMA((7 * na,)), pltpu.SemaphoreType.DMA((7 * na,)), pltpu.SemaphoreType.DMA((na,))]


def _all_gather_weights(shards, fulls, slicers):
    na = len(shards)

    def body(*refs):
        start, forward, finish = _ag_phases(refs[:na], refs[na:2 * na], slicers, *refs[2 * na:])
        start()
        forward()
        finish()

    return pl.pallas_call(
        body, name="gather_weights",
        out_shape=[jax.ShapeDtypeStruct(s, sh.dtype) for s, sh in zip(fulls, shards)],
        in_specs=[_ANY] * na, out_specs=[_ANY] * na, scratch_shapes=_ag_sems(na),
    )(*shards)


def _scatter_grads(grads, shard_shapes, slicers):
    na = len(grads)

    def body(*refs):
        ins, outs = refs[:na], refs[na:2 * na]
        send_sems, recv_sems, local_sems = refs[2 * na:]
        x, y, c = _my_pos()
        me = 4 * x + 2 * y + c
        mine, sent = [], []
        for a in range(na):
            cp = pltpu.make_async_copy(slicers[a](ins[a], me), outs[a].at[me], local_sems.at[a])
            cp.start()
            mine.append(cp)
        rel = [(k >> 2 & 1, k >> 1 & 1, k & 1) for k in range(1, NDEV)]
        for a in range(na):
            for k, (fx, fy, fc) in enumerate(rel):
                px, py, pc = x ^ fx, y ^ fy, c ^ fc
                cp = pltpu.make_async_remote_copy(
                    src_ref=slicers[a](ins[a], 4 * px + 2 * py + pc), dst_ref=outs[a].at[me],
                    send_sem=send_sems.at[a * 7 + k], recv_sem=recv_sems.at[a * 7 + k],
                    device_id=(px, py, pc), device_id_type=MESH)
                cp.start()
                sent.append(cp)
        for a in range(na):
            for k, (fx, fy, fc) in enumerate(rel):
                px, py, pc = x ^ fx, y ^ fy, c ^ fc
                src = 4 * px + 2 * py + pc
                pltpu.make_async_remote_copy(
                    src_ref=slicers[a](ins[a], me), dst_ref=outs[a].at[src],
                    send_sem=send_sems.at[a * 7 + k], recv_sem=recv_sems.at[a * 7 + k],
                    device_id=(px, py, pc), device_id_type=MESH).wait_recv()
        for cp in sent:
            cp.wait_send()
        for cp in mine:
            cp.wait()

    any_spec = pl.BlockSpec(memory_space=pl.ANY)
    return pl.pallas_call(
        body, name="scatter_grads",
        out_shape=[jax.ShapeDtypeStruct((NDEV,) + tuple(s), g.dtype) for s, g in zip(shard_shapes, grads)],
        in_specs=[any_spec] * na, out_specs=[any_spec] * na,
        scratch_shapes=[pltpu.SemaphoreType.DMA((7 * na,)), pltpu.SemaphoreType.DMA((7 * na,)),
                        pltpu.SemaphoreType.DMA((na,))],
    )(*grads)


def _local_step(x, target, mod, g_mix, g_ffn, g_fin, prm, w_in_shard, shards):
    fulls = [(HEADS, HB, HB), (HEADS, HB, HB), (D, D), (NDEV, FB, D), (DFF, D)]
    slicers = [_blk_heads, _blk_heads, _blk_rows(D // NDEV), _blk_lead, _blk_rows(DFF // NDEV)]
    my_chip = _my_index() >> 1
    own_first = (my_chip ^ jnp.arange(NCHIP, dtype=jnp.int32)).astype(jnp.int32)
    proj, h, w_in, (wa, wx) = _in_proj(x, mod, g_mix, w_in_shard, own_first, shards[:2], fulls[:2], slicers[:2])
    merged, hl, (w_out, w_gu, w_down) = _mixer_fwd(proj, prm, wa, wx, shards[2:], fulls[2:], slicers[2:])
    w_gu = w_gu.reshape(2, 4, FB, D)
    x1, h2 = _out_proj(merged, x, mod, g_ffn, w_out)
    gu, dx2, dx2b, loss, d_gfin = _ffn_fwd(h2, x1, target, mod, g_fin, w_gu, w_down)
    dgu, act, dx1, dx1b, dmg, sums2 = _ffn_bwd(dx2, gu, x1, mod, g_ffn, w_gu, w_down, w_out)
    chip_order = _xor_order(_my_index() >> 1, NCHIP)
    p_wgu = _gu_wgrad(h2, dgu, chip_order)
    p_wdown, d_gt2 = _scaled_wgrad("down_wgrad", act, dx2b, w_down, 5, mod, chip_order)
    p_wout, d_gt1 = _scaled_wgrad("out_wgrad", merged.reshape(1, *merged.shape), dx1b, w_out, 2, mod,
                                  jnp.zeros((1,), jnp.int32))
    dproj, msums, g_wa, g_wx = _mixer_bwd(proj, hl, dmg, prm, wa, wx)
    p_win, p_wa, p_wx = _in_wgrad(h, dproj, g_wa, g_wx, chip_order)
    grad_x, sums1 = _in_proj_bwd(dproj, w_in, x, dx1, mod, g_mix)
    return dict(loss=loss, grad_x=grad_x, d_gfin=d_gfin, sums1=sums1, sums2=sums2, msums=msums,
                d_gt1=d_gt1[0:1], d_gt2=d_gt2[0:1], p_win=p_win, p_wa=p_wa, p_wx=p_wx, p_wout=p_wout, p_wgu=p_wgu,
                p_wdown=p_wdown)


def kernel(x, c, w_ada, b_ada, g_norm_mix, w_in, conv_a_w, conv_b_w, conv_b_bias, w_rg_a, b_rg_a, w_rg_x, b_rg_x, lru_lambda, w_out, g_norm_ffn, w_gate_up, w_down, g_norm_final, loss_target, m_w_ada, m_b_ada, m_g_norm_mix, m_w_in, m_conv_a_w, m_conv_b_w, m_conv_b_bias, m_w_rg_a, m_b_rg_a, m_w_rg_x, m_b_rg_x, m_lru_lambda, m_w_out, m_g_norm_ffn, m_w_gate_up, m_w_down, m_g_norm_final, v_w_ada, v_b_ada, v_g_norm_mix, v_w_in, v_conv_a_w, v_conv_b_w, v_conv_b_bias, v_w_rg_a, v_b_rg_a, v_w_rg_x, v_b_rg_x, v_lru_lambda, v_w_out, v_g_norm_ffn, v_w_gate_up, v_w_down, v_g_norm_final):
    me = 4 * lax.axis_index("x") + 2 * lax.axis_index("y") + lax.axis_index("c")
    ncol = w_ada.shape[2]
    cw = conv_a_w.shape[2]

    pack0 = jnp.concatenate([c, conv_a_w.reshape(1, 3 * cw), conv_b_w.reshape(1, 4 * cw)], axis=1)
    got0 = _all_gather_small("gather_c", jnp.broadcast_to(pack0, (8, pack0.shape[1])))
    got0 = got0.reshape(NDEV, 8, -1)[:, 0, :]
    c_all = got0[:, :D]
    conv_a = got0[:, D:D + 3 * cw].reshape(NDEV, 3, cw).transpose(1, 0, 2).reshape(3, D)
    conv_b = got0[:, D + 3 * cw:].reshape(NDEV, 4, cw).transpose(1, 0, 2).reshape(4, D)

    b_cols = lax.dynamic_slice_in_dim(b_ada, me * ncol, ncol, axis=1)
    c16 = jnp.concatenate([c_all, jnp.zeros((8, D), F32)], axis=0)
    mod_cols = _ada_fwd(c16, w_ada[0], b_cols)[:NDEV]
    got1 = _all_gather_small("gather_mod", mod_cols).reshape(NDEV, NDEV, ncol)
    mod6 = lax.dynamic_index_in_dim(got1, me, axis=1, keepdims=False).reshape(6, D)
    mod = jnp.concatenate([mod6, jnp.zeros((2, D), F32)], axis=0)

    tr = lambda a: jnp.swapaxes(a, 1, 2)
    shards = [w_rg_a[0].astype(BF16), w_rg_x[0].astype(BF16), w_out[0].astype(BF16), tr(w_gate_up)[0].astype(BF16),
              w_down[0].astype(BF16)]

    prm = jnp.concatenate([conv_a, conv_b, conv_b_bias, b_rg_a, b_rg_x, lru_lambda, jnp.zeros((5, D), F32)], axis=0)
    r = _local_step(x[0], loss_target[0], mod, g_norm_mix, g_norm_ffn, g_norm_final.reshape(1, D), prm,
                    w_in[0].astype(BF16), shards)

    parts = [r["p_win"], r["p_wa"], r["p_wx"], r["p_wout"], r["p_wgu"], r["p_wdown"]]
    big = {}
    for nm, p, w, m, v in (("w_in", parts[0], w_in, m_w_in, v_w_in), ("w_rg_a", parts[1], w_rg_a, m_w_rg_a, v_w_rg_a),
                           ("w_rg_x", parts[2], w_rg_x, m_w_rg_x, v_w_rg_x), ("w_out", parts[3], w_out, m_w_out, v_w_out),
                           ("w_gate_up", parts[4], tr(w_gate_up), tr(m_w_gate_up), tr(v_w_gate_up)),
                           ("w_down", parts[5], w_down, m_w_down, v_w_down)):
        two_d = (-1, w.shape[-1])
        outs = _adam("adam_" + nm, p.reshape((p.shape[0],) + w.reshape(two_d).shape), w.reshape(two_d), m.reshape(two_d),
                     v.reshape(two_d))
        big[nm] = [o.reshape(w.shape) for o in outs]
    big["w_gate_up"] = [tr(o) for o in big["w_gate_up"]]

    small = jnp.concatenate([
        r["sums1"][S_SH:S_SH + 1], r["sums1"][S_SC:S_SC + 1], r["d_gt1"],
        r["sums2"][S_SH:S_SH + 1], r["sums2"][S_SC:S_SC + 1], r["d_gt2"],
        r["sums1"][S_G:S_G + 1],
        r["msums"][M_CBIAS:M_CBIAS + 1], r["msums"][M_BA:M_BA + 1], r["msums"][M_BX:M_BX + 1],
        r["msums"][M_LS:M_LS + 1],
        r["sums2"][S_G:S_G + 1], r["d_gfin"],
        r["msums"][M_WA:M_WA + 3], r["msums"][M_WB:M_WB + 4],
        jnp.zeros((4, D), F32)], axis=0)
    got2 = _all_gather_small("gather_small", small).reshape(NDEV, 24, D)

    rep_w = jnp.concatenate([b_ada.reshape(6, D), g_norm_mix, conv_b_bias, b_rg_a, b_rg_x, lru_lambda, g_norm_ffn,
                             g_norm_final.reshape(1, D), jnp.zeros((3, D), F32)], axis=0)
    rep_m = jnp.concatenate([m_b_ada.reshape(6, D), m_g_norm_mix, m_conv_b_bias, m_b_rg_a, m_b_rg_x, m_lru_lambda,
                             m_g_norm_ffn, m_g_norm_final.reshape(1, D), jnp.zeros((3, D), F32)], axis=0)
    rep_v = jnp.concatenate([v_b_ada.reshape(6, D), v_g_norm_mix, v_conv_b_bias, v_b_rg_a, v_b_rg_x, v_lru_lambda,
                             v_g_norm_ffn, v_g_norm_final.reshape(1, D), jnp.ones((3, D), F32)], axis=0)
    rep = _adam("adam_rep", got2[:, :16, :], rep_w, rep_m, rep_v)

    conv_parts = lax.dynamic_slice_in_dim(got2[:, 13:21, :], me * cw, cw, axis=2)
    cv_w = jnp.concatenate([conv_a_w[0], conv_b_w[0], jnp.zeros((1, cw), F32)], axis=0)
    cv_m = jnp.concatenate([m_conv_a_w[0], m_conv_b_w[0], jnp.zeros((1, cw), F32)], axis=0)
    cv_v = jnp.concatenate([v_conv_a_w[0], v_conv_b_w[0], jnp.ones((1, cw), F32)], axis=0)
    cvo = _adam("adam_conv", conv_parts, cv_w, cv_m, cv_v)

    dmod_cols = lax.dynamic_slice_in_dim(got2[:, :6, :].reshape(NDEV, 6 * D), me * ncol, ncol, axis=1)
    dmod16 = jnp.concatenate([dmod_cols, jnp.zeros((8, ncol), F32)], axis=0)
    ada = _ada_bwd(c16, dmod16, w_ada[0], m_w_ada[0], v_w_ada[0])

    loss = lax.psum(r["loss"][0, 0], AXES)

    def pick(q):
        one = lambda i: rep[q][i:i + 1]
        return [ada[q].reshape(w_ada.shape), rep[q][0:6].reshape(b_ada.shape), one(6), big["w_in"][q],
                cvo[q][0:3].reshape(conv_a_w.shape), cvo[q][3:7].reshape(conv_b_w.shape), one(7),
                big["w_rg_a"][q], one(8), big["w_rg_x"][q], one(9), one(10), big["w_out"][q], one(11),
                big["w_gate_up"][q], big["w_down"][q], rep[q][12]]

    return (loss, r["grad_x"].reshape(x.shape), *pick(0), *pick(1), *pick(2), *pick(3))
```

```python
import functools
import math

import jax
import jax.numpy as jnp
from jax import lax
from jax.experimental import pallas as pl
from jax.experimental.pallas import tpu as pltpu

F32 = jnp.float32
BF16 = jnp.bfloat16

D = 1024
DFF = 2816
NDEV = 8
HEADS = 4
HB = D // HEADS
FB = DFF // 4
EPS = 1e-6
LRU_C = 8.0
ADAM_LR, ADAM_B1, ADAM_B2, ADAM_EPS, ADAM_WD, ADAM_STEP = 0.001, 0.9, 0.999, 1e-08, 0.01, 10

VMEM_LIMIT = 56 * 1024 * 1024
TM = 512
TMI = 1024
TMF = 256
TK = 2048
TKI = 1024
SUB = 256
UNROLL = 4
TT = 256
CG = 256
MESH = pl.DeviceIdType.MESH
AXES = ("x", "y", "c")


def _cp(*sem):
    return pltpu.CompilerParams(dimension_semantics=sem, vmem_limit_bytes=VMEM_LIMIT)


def _sig(x):
    return 1.0 / (1.0 + jnp.exp(-x))


def _log_sigmoid(x):
    z = jnp.exp(-jnp.abs(x))
    u = 1.0 + z
    d = u - 1.0
    l1p = jnp.where(d == 0.0, z, jnp.log(u) * (z / jnp.where(d == 0.0, 1.0, d)))
    return -(jnp.maximum(-x, 0.0) + l1p)


def _neg_expm1(x):
    p = x * (1.0 + x * 0.5 * (1.0 + x * (1.0 / 3.0) * (1.0 + x * 0.25 * (1.0 + x * 0.2 * (1.0 + x * (1.0 / 6.0))))))
    return jnp.where(x > -0.25, -p, 1.0 - jnp.exp(x))


_GC = math.sqrt(2.0 / math.pi)


def _gelu(x):
    t = jnp.tanh(_GC * (x + 0.044715 * x * x * x))
    return 0.5 * x * (1.0 + t), t


def _dot(a, b):
    return jnp.dot(a, b, preferred_element_type=F32)


def _dot_nt(a, b):
    return lax.dot_general(a, b, (((1,), (1,)), ((), ())), preferred_element_type=F32)


def _dot_tn(a, b):
    return lax.dot_general(a, b, (((0,), (0,)), ((), ())), preferred_element_type=F32)


def _resident(shape):
    return pl.BlockSpec(shape, lambda *_: (0,) * len(shape), pipeline_mode=pl.Buffered(1))


def _sub_blocks(n_rows):
    step = min(SUB, n_rows)
    return [slice(r, r + step) for r in range(0, n_rows, step)]


def _fold8(v):
    return v[0:8] + v[8:16]


def _pj(ref, s, rows=slice(None), cols=slice(0, D)):
    return ref[rows, s * D + cols.start:s * D + cols.stop]


def _in_proj(x, mod, g_mix, w_shard, order, shards, fulls, slicers):
    t_len = x.shape[0]
    tm = min(TMI, t_len)
    ni = t_len // tm
    na = len(shards)
    cw = 7 * D // NDEV
    rc = 32

    def body(ord_ref, x_ref, mod_ref, g_ref, wsh_ref, *rest):
        ins, (proj_ref, h_ref, wfull_ref), outs = rest[:na], rest[na:na + 3], rest[na + 3:2 * na + 3]
        h_scr, w_scr, wsend, wrecv, wlocal, wout = rest[2 * na + 3:2 * na + 9]
        start, forward, finish = _ag_phases(ins, outs, slicers, *rest[2 * na + 9:])
        p, i = pl.program_id(0), pl.program_id(1)
        x_, y_, c = _my_pos()
        me, sibling = (x_, y_, c), (x_, y_, 1 - c)
        chip_at = [None, (x_, 1 - y_), (1 - x_, y_), (1 - x_, 1 - y_)]

        def cols(px, py, pc):
            return w_scr.at[:, pl.ds(pl.multiple_of((4 * px + 2 * py + pc) * cw, 128), cw)]

        def wcopy(k, block, to, from_shard=False):
            dst = cols(*block)
            return pltpu.make_async_remote_copy(src_ref=wsh_ref if from_shard else dst, dst_ref=dst,
                                                send_sem=wsend.at[k], recv_sem=wrecv.at[k], device_id=to,
                                                device_id_type=MESH)

        own_local = pltpu.make_async_copy(wsh_ref, cols(*me), wlocal)
        to_hbm = pltpu.make_async_copy(w_scr, wfull_ref, wout)

        @pl.when((p == 0) & (i == 0))
        def _():
            own_local.start()
            wcopy(0, me, sibling, True).start()
            for q in (1, 2):
                wcopy(q, me, (*chip_at[q], c), True).start()
            start()
            own_local.wait()
            wcopy(0, sibling, me).wait_recv()

        @pl.when((p == 0) & (i == ni // 2))
        def _():
            wcopy(3, me, (*chip_at[3], c), True).start()

        for q in (1, 2, 3):
            @pl.when((p == q - 1) & (i == ni - 1))
            def _():
                wcopy(q, (*chip_at[q], c), me).wait_recv()
                wcopy(3 + q, (*chip_at[q], c), sibling).start()

            @pl.when((p == q) & (i == 0))
            def _():
                wcopy(3 + q, (*chip_at[q], 1 - c), me).wait_recv()

        @pl.when((p == 1) & (i == 0))
        def _():
            forward()

        @pl.when((p == NCHIP - 1) & (i == 0))
        def _():
            to_hbm.start()

        gs = g_ref[...] * (1.0 + mod_ref[1:2, :])
        sh = mod_ref[0:1, :]

        def chunk(j, carry):
            rows = pl.ds(pl.multiple_of(j * rc, rc), rc)
            xv = x_ref[rows, :]
            r = lax.rsqrt(jnp.mean(xv * xv, axis=-1, keepdims=True) + EPS)
            h_scr[rows, :] = (xv * r * gs + sh).astype(BF16)
            return carry

        lax.fori_loop(0, tm // rc, chunk, 0, unroll=UNROLL)

        @pl.when(p == 0)
        def _():
            h_ref[...] = h_scr[...]

        wcols = pl.ds(pl.multiple_of(ord_ref[p] * (2 * cw), 128), 2 * cw)
        proj_ref[...] = _dot(h_scr[...], w_scr[:, wcols]).astype(BF16)

        @pl.when((p == NCHIP - 1) & (i == ni - 1))
        def _():
            wcopy(0, me, sibling, True).wait_send()
            for q in (1, 2, 3):
                wcopy(q, me, (*chip_at[q], c), True).wait_send()
                wcopy(3 + q, (*chip_at[q], c), sibling).wait_send()
            finish()
            to_hbm.wait()

    res = pl.pallas_call(
        body, name="in_proj",
        grid_spec=pltpu.PrefetchScalarGridSpec(
            num_scalar_prefetch=1, grid=(NCHIP, ni),
            in_specs=[pl.BlockSpec((tm, D), lambda p, i, o: (i, 0)),
                      pl.BlockSpec((8, D), lambda p, i, o: (0, 0)),
                      pl.BlockSpec((1, D), lambda p, i, o: (0, 0))] + [_ANY] * (1 + na),
            out_specs=[pl.BlockSpec((tm, 2 * cw), lambda p, i, o: (i, o[p])),
                       pl.BlockSpec((tm, D), lambda p, i, o: (jnp.where(p == 0, i, ni - 1), 0))]
            + [_ANY] * (1 + na),
            scratch_shapes=[pltpu.VMEM((tm, D), BF16), pltpu.VMEM((D, 7 * D), BF16),
                            pltpu.SemaphoreType.DMA((7,)), pltpu.SemaphoreType.DMA((7,)),
                            pltpu.SemaphoreType.DMA, pltpu.SemaphoreType.DMA] + _ag_sems(na)),
        out_shape=[jax.ShapeDtypeStruct((t_len, 7 * D), BF16), jax.ShapeDtypeStruct((t_len, D), BF16),
                   jax.ShapeDtypeStruct((D, 7 * D), BF16)]
        + [jax.ShapeDtypeStruct(f, sh.dtype) for f, sh in zip(fulls, shards)],
        compiler_params=_cp("arbitrary", "arbitrary"),
    )(order, x, mod, g_mix, w_shard, *shards)
    return res[0], res[1], res[2], res[3:]


P_WA, P_WB, P_CBIAS, P_BA, P_BX, P_LAM = 0, 3, 7, 8, 9, 10


def _lru_gates(rp, ip, ls, first_row):
    r = _sig(rp)
    ig = _sig(ip)
    la = LRU_C * r * ls
    a = jnp.exp(la)
    m2 = _neg_expm1(2.0 * la)
    mult = jnp.where(first_row, 1.0, jnp.sqrt(jnp.maximum(m2, 0.0)))
    return r, ig, la, a, m2, mult


def _shift_down(cur, prev, s, row):
    return jnp.where(row >= s, pltpu.roll(cur, s, 0), pltpu.roll(prev, s, 0))


def _shift_up(cur, nxt, s, row):
    return jnp.where(row < 8 - s, pltpu.roll(cur, 8 - s, 0), pltpu.roll(nxt, 8 - s, 0))


def _conv_fwd_rows(tt, proj_ref, prm_ref, xe, ve, u_s, ub_s, ya_s):
    row = lax.broadcasted_iota(jnp.int32, (8, CG), 0)
    w_b = [prm_ref[P_WB + k:P_WB + k + 1, :] for k in range(4)]
    w_a = [prm_ref[P_WA + k:P_WA + k + 1, :] for k in range(3)]
    bias = prm_ref[P_CBIAS:P_CBIAS + 1, :]

    def blk(ib, carry):
        r0 = pl.multiple_of(ib * 16, 16)
        rows = pl.ds(r0, 16)
        for g in range(D // CG):
            cs = slice(g * CG, (g + 1) * CG)
            x16 = _pj(proj_ref, 3, rows, cs).astype(F32)
            v16 = _pj(proj_ref, 1, rows, cs).astype(F32) * _pj(proj_ref, 2, rows, cs).astype(F32)
            xp = xe[pl.ds(r0, 8), cs]
            vp = ve[pl.ds(r0, 8), cs]
            xe[pl.ds(r0 + 8, 16), cs] = x16
            ve[pl.ds(r0 + 8, 16), cs] = v16
            us, yas = [], []
            for sb in range(2):
                xc, vc = x16[8 * sb:8 * sb + 8], v16[8 * sb:8 * sb + 8]
                u8 = bias[:, cs] + w_b[3][:, cs] * xc
                for s in (1, 2, 3):
                    u8 = u8 + w_b[3 - s][:, cs] * _shift_down(xc, xp, s, row)
                y8 = w_a[2][:, cs] * vc
                for s in (1, 2):
                    y8 = y8 + w_a[2 - s][:, cs] * _shift_down(vc, vp, s, row)
                us.append(u8)
                yas.append(y8)
                xp, vp = xc, vc
            u16 = jnp.concatenate(us, axis=0)
            u_s[rows, cs] = u16
            ub_s[rows, cs] = u16.astype(BF16)
            ya_s[rows, cs] = jnp.concatenate(yas, axis=0)
        return carry

    lax.fori_loop(0, tt // 16, blk, 0)


def _mixer_fwd(proj, prm, wa, wx, shards, fulls, slicers):
    t_len = proj.shape[0]
    tt = min(TT, t_len)
    nt = t_len // tt
    na = len(shards)

    def body(proj_ref, prm_ref, wa_ref, wx_ref, *rest):
        ins, (mg_ref, hl_ref), outs = rest[:na], rest[na:na + 2], rest[na + 2:2 * na + 2]
        xe, ve, hc, u_s, ya_s, rp_s, ip_s, ub_s = rest[2 * na + 2:2 * na + 10]
        start, forward, finish = _ag_phases(ins, outs, slicers, *rest[2 * na + 10:])
        t = pl.program_id(0)

        @pl.when(t == 0)
        def _():
            start()
            xe[0:8, :] = jnp.zeros((8, D), F32)
            ve[0:8, :] = jnp.zeros((8, D), F32)
            hc[...] = jnp.zeros((8, D), F32)

        @pl.when(t == (3 * nt) // 4)
        def _():
            forward()

        _conv_fwd_rows(tt, proj_ref, prm_ref, xe, ve, u_s, ub_s, ya_s)
        xe[0:8, :] = xe[tt:tt + 8, :]
        ve[0:8, :] = ve[tt:tt + 8, :]

        ub = ub_s[...]
        for h in range(HEADS):
            cs = slice(h * HB, (h + 1) * HB)
            rp_s[:, cs] = _dot(ub[:, cs], wa_ref[h]) + prm_ref[P_BA:P_BA + 1, cs]
            ip_s[:, cs] = _dot(ub[:, cs], wx_ref[h]) + prm_ref[P_BX:P_BX + 1, cs]

        ls_all = _log_sigmoid(prm_ref[P_LAM:P_LAM + 1, :])
        row = lax.broadcasted_iota(jnp.int32, (8, CG), 0)

        def blk(i, carry):
            r0 = pl.multiple_of(i * 16, 16)
            for g in range(D // CG):
                cs = slice(g * CG, (g + 1) * CG)
                ls = ls_all[:, cs]
                hprev = hc[:, cs]
                hs = []
                for sb in range(2):
                    rr = r0 + 8 * sb
                    first = (row + (t * tt + rr)) == 0
                    _, ig, _, a, _, mult = _lru_gates(rp_s[pl.ds(rr, 8), cs], ip_s[pl.ds(rr, 8), cs], ls, first)
                    b = mult * (ig * u_s[pl.ds(rr, 8), cs])
                    for s in (1, 2, 4):
                        a_sh = jnp.where(row >= s, pltpu.roll(a, s, 0), 1.0)
                        b_sh = jnp.where(row >= s, pltpu.roll(b, s, 0), 0.0)
                        b = a * b_sh + b
                        a = a * a_sh
                    hv = a * hprev + b
                    hprev = jnp.broadcast_to(hv[7:8, :], hv.shape)
                    hs.append(hv)
                hc[:, cs] = hprev
                h16 = jnp.concatenate(hs, axis=0)
                rows = pl.ds(r0, 16)
                gl, _ = _gelu(_pj(proj_ref, 4, rows, cs).astype(F32))
                y_b = h16 * gl
                y_a = _pj(proj_ref, 0, rows, cs).astype(F32) * ya_s[rows, cs]
                mg = (_sig(_pj(proj_ref, 5, rows, cs).astype(F32)) * y_a
                      + _sig(_pj(proj_ref, 6, rows, cs).astype(F32)) * y_b)
                mg_ref[rows, cs] = mg.astype(BF16)
                hl_ref[rows, cs] = h16.astype(BF16)
            return carry

        lax.fori_loop(0, tt // 16, blk, 0)

        @pl.when(t == nt - 1)
        def _():
            finish()

    res = pl.pallas_call(
        body, name="mixer_fwd", grid=(nt,),
        in_specs=[pl.BlockSpec((tt, 7 * D), lambda t: (t, 0)),
                  pl.BlockSpec((16, D), lambda t: (0, 0)),
                  pl.BlockSpec((HEADS, HB, HB), lambda t: (0, 0, 0)),
                  pl.BlockSpec((HEADS, HB, HB), lambda t: (0, 0, 0))] + [_ANY] * na,
        out_specs=[pl.BlockSpec((tt, D), lambda t: (t, 0)), pl.BlockSpec((tt, D), lambda t: (t, 0))] + [_ANY] * na,
        out_shape=[jax.ShapeDtypeStruct((t_len, D), BF16), jax.ShapeDtypeStruct((t_len, D), BF16)]
        + [jax.ShapeDtypeStruct(f, sh.dtype) for f, sh in zip(fulls, shards)],
        scratch_shapes=[pltpu.VMEM((tt + 8, D), F32), pltpu.VMEM((tt + 8, D), F32), pltpu.VMEM((8, D), F32),
                        pltpu.VMEM((tt, D), F32), pltpu.VMEM((tt, D), F32), pltpu.VMEM((tt, D), F32),
                        pltpu.VMEM((tt, D), F32), pltpu.VMEM((tt, D), BF16)] + _ag_sems(na),
        compiler_params=_cp("arbitrary"),
    )(proj, prm, wa, wx, *shards)
    return res[0], res[1], res[2:]


def _out_proj(merged, x, mod, g_ffn, w_out):
    t_len = x.shape[0]
    tm = min(TM, t_len)

    def body(mg_ref, x_ref, mod_ref, g_ref, w_ref, x1_ref, h2_ref):
        gt1 = mod_ref[2:3, :]
        for rows in _sub_blocks(tm):
            x1_ref[rows, :] = x_ref[rows, :] + gt1 * _dot(mg_ref[rows, :], w_ref[...])
        gs = g_ref[...] * (1.0 + mod_ref[4:5, :])
        sh = mod_ref[3:4, :]

        def chunk(c, carry):
            rows = pl.ds(pl.multiple_of(c * 16, 16), 16)
            x1 = x1_ref[rows, :]
            r = lax.rsqrt(jnp.mean(x1 * x1, axis=-1, keepdims=True) + EPS)
            h2_ref[rows, :] = (x1 * r * gs + sh).astype(BF16)
            return carry

        lax.fori_loop(0, tm // 16, chunk, 0, unroll=UNROLL)

    return pl.pallas_call(
        body, name="out_proj", grid=(t_len // tm,),
        in_specs=[pl.BlockSpec((tm, D), lambda i: (i, 0)), pl.BlockSpec((tm, D), lambda i: (i, 0)),
                  pl.BlockSpec((8, D), lambda i: (0, 0)), pl.BlockSpec((1, D), lambda i: (0, 0)),
                  pl.BlockSpec((D, D), lambda i: (0, 0))],
        out_specs=[pl.BlockSpec((tm, D), lambda i: (i, 0)), pl.BlockSpec((tm, D), lambda i: (i, 0))],
        out_shape=[jax.ShapeDtypeStruct((t_len, D), F32), jax.ShapeDtypeStruct((t_len, D), BF16)],
        compiler_params=_cp("parallel"),
    )(merged, x, mod, g_ffn, w_out)


def _ffn_fwd(h2, x1, target, mod, g_fin, w_gu, w_down):
    t_len = x1.shape[0]
    tm = min(TMF, t_len)
    assert tm % (16 * UNROLL) == 0

    def body(h2_ref, x1_ref, tg_ref, mod_ref, g_ref, wgu_ref, wd_ref, gu_ref, dx2_ref, dx2b_ref, loss_ref, dg_ref, acc):
        @pl.when(pl.program_id(0) == 0)
        def _():
            loss_ref[...] = jnp.zeros_like(loss_ref)
            dg_ref[...] = jnp.zeros_like(dg_ref)

        hb = h2_ref[...]
        ffn = None
        for j in range(4):
            gate = _dot_nt(hb, wgu_ref[0, j])
            up = _dot_nt(hb, wgu_ref[1, j])
            gu_ref[0, j] = gate.astype(BF16)
            gu_ref[1, j] = up.astype(BF16)
            act = (gate * _sig(gate) * up).astype(BF16)
            part = _dot(act, wd_ref[j * FB:(j + 1) * FB, :])
            ffn = part if ffn is None else ffn + part
        acc[...] = ffn

        gt2 = mod_ref[5:6, :]
        gf = g_ref[...]

        def chunk(c, carry):
            s_loss, s_dg = carry
            for u in range(UNROLL):
                rows = pl.ds(pl.multiple_of(c * (16 * UNROLL), 16) + 16 * u, 16)
                x2 = x1_ref[rows, :] + gt2 * acc[rows, :]
                r = lax.rsqrt(jnp.mean(x2 * x2, axis=-1, keepdims=True) + EPS)
                xn = x2 * r
                diff = xn * gf - tg_ref[rows, :]
                dy = diff * (1.0 / D)
                dxn = dy * gf
                dx2 = r * (dxn - xn * jnp.mean(dxn * xn, axis=-1, keepdims=True))
                dx2_ref[rows, :] = dx2
                dx2b_ref[rows, :] = dx2.astype(BF16)
                s_loss, s_dg = s_loss + _fold8(diff * diff), s_dg + _fold8(dy * xn)
            return s_loss, s_dg

        zero = jnp.zeros((8, D), F32)
        s_loss, s_dg = lax.fori_loop(0, tm // (16 * UNROLL), chunk, (zero, zero))
        loss_ref[...] += jnp.sum(s_loss) * (0.5 / D)
        dg_ref[...] += jnp.sum(s_dg, axis=0, keepdims=True)

    row = pl.BlockSpec((tm, D), lambda i: (i, 0))
    return pl.pallas_call(
        body, name="ffn_fwd", grid=(t_len // tm,),
        in_specs=[row, row, row, pl.BlockSpec((8, D), lambda i: (0, 0)), pl.BlockSpec((1, D), lambda i: (0, 0)),
                  _resident((2, 4, FB, D)), _resident((DFF, D))],
        out_specs=[pl.BlockSpec((2, 4, tm, FB), lambda i: (0, 0, i, 0)), row, row,
                   pl.BlockSpec((1, 128), lambda i: (0, 0)), pl.BlockSpec((1, D), lambda i: (0, 0))],
        out_shape=[jax.ShapeDtypeStruct((2, 4, t_len, FB), BF16), jax.ShapeDtypeStruct((t_len, D), F32),
                   jax.ShapeDtypeStruct((t_len, D), BF16),
                   jax.ShapeDtypeStruct((1, 128), F32), jax.ShapeDtypeStruct((1, D), F32)],
        scratch_shapes=[pltpu.VMEM((tm, D), F32)],
        compiler_params=_cp("arbitrary"),
    )(h2, x1, target, mod, g_fin, w_gu, w_down)


S_SH, S_SC, S_G = 0, 1, 2


def _norm_bwd_rows(n_rows, rc, dh_ref, x_ref, dres_ref, scale, gain, sums_ref, write):
    assert n_rows % (rc * UNROLL) == 0
    gs = 1.0 + scale
    fold = _fold8 if rc == 16 else (lambda v: v)

    def chunk(c, carry):
        s_sh, s_sc, s_g = carry
        for u in range(UNROLL):
            rows = pl.ds(pl.multiple_of(c * (rc * UNROLL), rc) + rc * u, rc)
            dh = dh_ref[rows, :]
            xv = x_ref[rows, :]
            r = lax.rsqrt(jnp.mean(xv * xv, axis=-1, keepdims=True) + EPS)
            xn = xv * r
            dhn = dh * gs
            dxn = dhn * gain
            write(rows, dres_ref[rows, :] + r * (dxn - xn * jnp.mean(dxn * xn, axis=-1, keepdims=True)))
            s_sh, s_sc, s_g = s_sh + fold(dh), s_sc + fold(dh * (xn * gain)), s_g + fold(dhn * xn)
        return s_sh, s_sc, s_g

    zero = jnp.zeros((8, D), F32)
    s_sh, s_sc, s_g = lax.fori_loop(0, n_rows // (rc * UNROLL), chunk, (zero, zero, zero))
    sums_ref[S_SH:S_SH + 1, :] += jnp.sum(s_sh, axis=0, keepdims=True)
    sums_ref[S_SC:S_SC + 1, :] += jnp.sum(s_sc, axis=0, keepdims=True)
    sums_ref[S_G:S_G + 1, :] += jnp.sum(s_g, axis=0, keepdims=True)


def _ffn_bwd(dx2, gu, x1, mod, g_ffn, w_gu, w_down, w_out):
    t_len = x1.shape[0]
    tm = min(TMF, t_len)

    def body(dx2_ref, gu_ref, x1_ref, mod_ref, g_ref, wgu_ref, wd_ref, wo_ref,
             dgu_ref, act_ref, dx1_ref, dx1b_ref, dmg_ref, sums_ref, acc, dmo):
        @pl.when(pl.program_id(0) == 0)
        def _():
            sums_ref[...] = jnp.zeros_like(sums_ref)

        dffn = (dx2_ref[...] * mod_ref[5:6, :]).astype(BF16)
        dh2 = None
        for j in range(4):
            dact = _dot_nt(dffn, wd_ref[j * FB:(j + 1) * FB, :])
            gate = gu_ref[0, j].astype(F32)
            up = gu_ref[1, j].astype(F32)
            sg = _sig(gate)
            silu = gate * sg
            act_ref[j] = (silu * up).astype(BF16)
            dgate = (dact * up * (sg * (1.0 + gate * (1.0 - sg)))).astype(BF16)
            dup = (dact * silu).astype(BF16)
            dgu_ref[0, j] = dgate
            dgu_ref[1, j] = dup
            part = _dot(dgate, wgu_ref[0, j]) + _dot(dup, wgu_ref[1, j])
            dh2 = part if dh2 is None else dh2 + part
        acc[...] = dh2

        gt1 = mod_ref[2:3, :]

        def write(rows, dx1):
            dx1_ref[rows, :] = dx1
            dx1b_ref[rows, :] = dx1.astype(BF16)
            dmo[rows, :] = (dx1 * gt1).astype(BF16)

        _norm_bwd_rows(tm, 16, acc, x1_ref, dx2_ref, mod_ref[4:5, :], g_ref[...], sums_ref, write)
        dmg_ref[...] = _dot_nt(dmo[...], wo_ref[...]).astype(BF16)

    row = pl.BlockSpec((tm, D), lambda i: (i, 0))
    return pl.pallas_call(
        body, name="ffn_bwd", grid=(t_len // tm,),
        in_specs=[row, pl.BlockSpec((2, 4, tm, FB), lambda i: (0, 0, i, 0)), row,
                  pl.BlockSpec((8, D), lambda i: (0, 0)), pl.BlockSpec((1, D), lambda i: (0, 0)),
                  _resident((2, 4, FB, D)), _resident((DFF, D)), _resident((D, D))],
        out_specs=[pl.BlockSpec((2, 4, tm, FB), lambda i: (0, 0, i, 0)),
                   pl.BlockSpec((4, tm, FB), lambda i: (0, i, 0)), row, row, row,
                   pl.BlockSpec((8, D), lambda i: (0, 0))],
        out_shape=[jax.ShapeDtypeStruct((2, 4, t_len, FB), BF16), jax.ShapeDtypeStruct((4, t_len, FB), BF16),
                   jax.ShapeDtypeStruct((t_len, D), F32), jax.ShapeDtypeStruct((t_len, D), BF16),
                   jax.ShapeDtypeStruct((t_len, D), BF16), jax.ShapeDtypeStruct((8, D), F32)],
        scratch_shapes=[pltpu.VMEM((tm, D), F32), pltpu.VMEM((tm, D), BF16)],
        compiler_params=_cp("arbitrary"),
    )(dx2, gu, x1, mod, g_ffn, w_gu, w_down, w_out)


def _my_pos():
    return lax.axis_index("x"), lax.axis_index("y"), lax.axis_index("c")


def _my_index():
    x, y, c = _my_pos()
    return 4 * x + 2 * y + c


def _device_of(b):
    return (b >> 2) & 1, (b >> 1) & 1, b & 1


def _rs_send(src, parts_ref, b, send_sems, recv_sems, local_sem):
    me = _my_index()
    dst = parts_ref.at[me]

    @pl.when(b == me)
    def _():
        pltpu.make_async_copy(src, dst, local_sem).start()

    @pl.when(b != me)
    def _():
        pltpu.make_async_remote_copy(src_ref=src, dst_ref=dst, send_sem=send_sems.at[b], recv_sem=recv_sems.at[me],
                                     device_id=_device_of(b), device_id_type=MESH).start()


def _rs_finish(src_of, parts_ref, send_sems, recv_sems, local_sem):
    me = _my_index()
    for s in range(NDEV):
        @pl.when(s != me)
        def _():
            cp = pltpu.make_async_remote_copy(src_ref=src_of(s), dst_ref=parts_ref.at[s], send_sem=send_sems.at[s],
                                              recv_sem=recv_sems.at[s], device_id=_device_of(s), device_id_type=MESH)
            cp.wait_send()
            cp.wait_recv()

        @pl.when(s == me)
        def _():
            pltpu.make_async_copy(src_of(s), parts_ref.at[s], local_sem).wait()


_RS_SEMS = [pltpu.SemaphoreType.DMA((NDEV,)), pltpu.SemaphoreType.DMA((NDEV,)), pltpu.SemaphoreType.DMA]
_ANY = pl.BlockSpec(memory_space=pl.ANY)


def _xor_order(me, n):
    return (me ^ (n - 1 - jnp.arange(n, dtype=jnp.int32))).astype(jnp.int32)


NCHIP = NDEV // 2


def _rs2_scratch(half_shape):
    blocks = lambda *lead: pltpu.VMEM(lead + tuple(half_shape), BF16)
    return [blocks(NCHIP, 2), blocks(NCHIP)] + [pltpu.SemaphoreType.DMA((NCHIP,))] * 4 + [pltpu.SemaphoreType.DMA]


def _rs2_to_sibling(q, rs):
    stage, from_sib, d_send, d_recv = rs[:4]
    x, y, c = _my_pos()
    pltpu.make_async_remote_copy(src_ref=stage.at[q, 1 - c], dst_ref=from_sib.at[q], send_sem=d_send.at[q],
                                 recv_sem=d_recv.at[q], device_id=(x, y, 1 - c), device_id_type=MESH).start()


def _rs2_forward(q, parts_ref, rs):
    stage, chip_sum, d_send, d_recv, i_send, i_recv, local_sem = rs
    x, y, c = _my_pos()
    my_chip = 2 * x + y
    pltpu.make_async_remote_copy(src_ref=stage.at[q, c], dst_ref=chip_sum.at[q], send_sem=d_send.at[q],
                                 recv_sem=d_recv.at[q], device_id=(x, y, 1 - c), device_id_type=MESH).wait_recv()
    chip_sum[q] = (stage[q, c].astype(F32) + chip_sum[q].astype(F32)).astype(BF16)

    @pl.when(q == my_chip)
    def _():
        pltpu.make_async_copy(chip_sum.at[q], parts_ref.at[my_chip], local_sem).start()

    @pl.when(q != my_chip)
    def _():
        pltpu.make_async_remote_copy(src_ref=chip_sum.at[q], dst_ref=parts_ref.at[my_chip], send_sem=i_send.at[q],
                                     recv_sem=i_recv.at[my_chip], device_id=((q >> 1) & 1, q & 1, c),
                                     device_id_type=MESH).start()


def _rs2_finish(parts_ref, rs):
    stage, chip_sum, d_send, d_recv, i_send, i_recv, local_sem = rs
    x, y, c = _my_pos()
    my_chip = 2 * x + y
    for q in range(NCHIP):
        pltpu.make_async_remote_copy(src_ref=stage.at[q, 1 - c], dst_ref=chip_sum.at[q], send_sem=d_send.at[q],
                                     recv_sem=d_recv.at[q], device_id=(x, y, 1 - c), device_id_type=MESH).wait_send()

        @pl.when(q != my_chip)
        def _():
            cp = pltpu.make_async_remote_copy(src_ref=chip_sum.at[q], dst_ref=parts_ref.at[q], send_sem=i_send.at[q],
                                              recv_sem=i_recv.at[q], device_id=((q >> 1) & 1, q & 1, c),
                                              device_id_type=MESH)
            cp.wait_send()
            cp.wait_recv()

        @pl.when(q == my_chip)
        def _():
            pltpu.make_async_copy(chip_sum.at[q], parts_ref.at[q], local_sem).wait()


def _gu_wgrad(h2, dgu, order):
    t_len = h2.shape[0]
    tk = min(TK, t_len)
    nk = t_len // tk

    def body(ord_ref, h_ref, d_ref, parts_ref, acc, *rs):
        p, k = pl.program_id(0), pl.program_id(1)

        @pl.when(k == 0)
        def _():
            acc[...] = jnp.zeros_like(acc)

        hb = h_ref[...]
        for half in range(2):
            acc[half] += _dot_tn(d_ref[0, half], hb)

        @pl.when(k == nk - 1)
        def _():
            q = ord_ref[p]
            rs[0][q] = acc[...].astype(BF16)
            _rs2_to_sibling(q, rs)

        @pl.when((k == nk - 1) & (p > 0))
        def _():
            _rs2_forward(ord_ref[p - 1], parts_ref, rs)

        @pl.when((p == NCHIP - 1) & (k == nk - 1))
        def _():
            _rs2_forward(ord_ref[p], parts_ref, rs)
            _rs2_finish(parts_ref, rs)

    return pl.pallas_call(
        body, name="gu_wgrad",
        grid_spec=pltpu.PrefetchScalarGridSpec(
            num_scalar_prefetch=1, grid=(NCHIP, nk),
            in_specs=[pl.BlockSpec((tk, D), lambda p, k, o: (k, 0)),
                      pl.BlockSpec((1, 2, tk, FB), lambda p, k, o: (o[p], 0, k, 0))],
            out_specs=_ANY,
            scratch_shapes=[pltpu.VMEM((2, FB, D), F32)] + _rs2_scratch((FB, D))),
        out_shape=jax.ShapeDtypeStruct((NCHIP, FB, D), BF16),
        compiler_params=_cp("arbitrary", "arbitrary"),
    )(order, h2, dgu.reshape(NCHIP, 2, t_len, FB))


def _scaled_wgrad(name, a, dx, w, gate_row, mod, order):
    nb, t_len, kb = a.shape
    tk = min(TK, t_len)
    nk = t_len // tk
    cpb = NCHIP // nb
    rows = kb // (2 * cpb)

    def body(ord_ref, a_ref, dx_ref, w_ref, mod_ref, parts_ref, dg_ref, acc, *rs):
        p, k = pl.program_id(0), pl.program_id(1)
        j = ord_ref[p]

        @pl.when((p == 0) & (k == 0))
        def _():
            dg_ref[...] = jnp.zeros_like(dg_ref)

        @pl.when(k == 0)
        def _():
            acc[...] = jnp.zeros_like(acc)

        acc[...] += _dot_tn(a_ref[0], dx_ref[...])

        @pl.when(k == nk - 1)
        def _():
            z = acc[...]
            zg = (z * mod_ref[gate_row:gate_row + 1, :]).astype(BF16)
            dg_ref[0:1, :] += jnp.sum(z * w_ref[...].astype(F32), axis=0, keepdims=True)
            for i in range(cpb):
                q = j * cpb + i
                for half in range(2):
                    rs[0][q, half] = zg[(2 * i + half) * rows:(2 * i + half + 1) * rows]
                _rs2_to_sibling(q, rs)

        if cpb == 1:
            @pl.when((k == nk - 1) & (p > 0))
            def _():
                _rs2_forward(ord_ref[p - 1], parts_ref, rs)

        @pl.when((p == nb - 1) & (k == nk - 1))
        def _():
            for i in range(cpb):
                _rs2_forward(j * cpb + i, parts_ref, rs)
            _rs2_finish(parts_ref, rs)

    return pl.pallas_call(
        body, name=name,
        grid_spec=pltpu.PrefetchScalarGridSpec(
            num_scalar_prefetch=1, grid=(nb, nk),
            in_specs=[pl.BlockSpec((1, tk, kb), lambda p, k, o: (o[p], k, 0)),
                      pl.BlockSpec((tk, D), lambda p, k, o: (k, 0)),
                      pl.BlockSpec((kb, D), lambda p, k, o: (o[p], 0)),
                      pl.BlockSpec((8, D), lambda p, k, o: (0, 0))],
            out_specs=[_ANY, pl.BlockSpec((8, D), lambda p, k, o: (0, 0))],
            scratch_shapes=[pltpu.VMEM((kb, D), F32)] + _rs2_scratch((rows, D))),
        out_shape=[jax.ShapeDtypeStruct((NCHIP, rows, D), BF16), jax.ShapeDtypeStruct((8, D), F32)],
        compiler_params=_cp("arbitrary", "arbitrary"),
    )(order, a, dx, w, mod)


M_WA, M_WB, M_CBIAS, M_BA, M_BX, M_LS = 0, 3, 7, 8, 9, 10


def _conv_bwd_rows(tt, proj_ref, prm_ref, xe, ve, due, dye, dp_ref, acc8):
    row = lax.broadcasted_iota(jnp.int32, (8, CG), 0)
    w_b = [prm_ref[P_WB + k:P_WB + k + 1, :] for k in range(4)]
    w_a = [prm_ref[P_WA + k:P_WA + k + 1, :] for k in range(3)]

    def blk(ib, carry):
        r0 = pl.multiple_of(ib * 16, 16)
        rows = pl.ds(r0, 16)
        for g in range(D // CG):
            cs = slice(g * CG, (g + 1) * CG)
            du16, du_after = due[rows, cs], due[pl.ds(r0 + 16, 8), cs]
            dy16, dy_after = dye[rows, cs], dye[pl.ds(r0 + 16, 8), cs]
            x16, xp = xe[pl.ds(r0 + 8, 16), cs], xe[pl.ds(r0, 8), cs]
            v16, vp = ve[pl.ds(r0 + 8, 16), cs], ve[pl.ds(r0, 8), cs]
            acc = [acc8[8 * k:8 * k + 8, cs] for k in range(8)]
            drx, dv = [], []
            for sb in range(2):
                lo = slice(8 * sb, 8 * sb + 8)
                duc, dyc, xc, vc = du16[lo], dy16[lo], x16[lo], v16[lo]
                du_n = du16[8:16] if sb == 0 else du_after
                dy_n = dy16[8:16] if sb == 0 else dy_after
                acc[0] = acc[0] + duc
                acc[4] = acc[4] + duc * xc
                d8 = w_b[3][:, cs] * duc
                for s in (1, 2, 3):
                    acc[4 - s] = acc[4 - s] + duc * _shift_down(xc, xp, s, row)
                    d8 = d8 + w_b[3 - s][:, cs] * _shift_up(duc, du_n, s, row)
                acc[7] = acc[7] + dyc * vc
                e8 = w_a[2][:, cs] * dyc
                for s in (1, 2):
                    acc[7 - s] = acc[7 - s] + dyc * _shift_down(vc, vp, s, row)
                    e8 = e8 + w_a[2 - s][:, cs] * _shift_up(dyc, dy_n, s, row)
                drx.append(d8)
                dv.append(e8)
                xp, vp = xc, vc
            for k in range(8):
                acc8[8 * k:8 * k + 8, cs] = acc[k]
            dv16 = jnp.concatenate(dv, axis=0)
            col = lambda s: slice(s * D + g * CG, s * D + (g + 1) * CG)
            dp_ref[rows, col(3)] = jnp.concatenate(drx, axis=0).astype(BF16)
            dp_ref[rows, col(1)] = (dv16 * _pj(proj_ref, 2, rows, cs).astype(F32)).astype(BF16)
            dp_ref[rows, col(2)] = (dv16 * _pj(proj_ref, 1, rows, cs).astype(F32)).astype(BF16)
        return carry

    lax.fori_loop(0, tt // 16, blk, 0)


def _mixer_bwd(proj, hl, dmg, prm, wa, wx):
    t_len = proj.shape[0]
    tt = min(TT, t_len)
    nt = t_len // tt
    hb8 = tt // 8

    def rev(i):
        return nt - 1 - i

    def halo(i):
        return jnp.maximum(rev(i) * hb8 - 1, 0)

    def body(proj_ref, ph_ref, hl_ref, hh_ref, dmg_ref, prm_ref, wa_ref, wx_ref,
             dp_ref, sums_ref, gwa_ref, gwx_ref,
             xe, ve, he, u_s, ya_s, rp_s, ip_s, due, dye, drp_s, dip_s, an, gn, ub_s, acc8):
        i = pl.program_id(0)
        t = rev(i)

        @pl.when(i == 0)
        def _():
            sums_ref[...] = jnp.zeros_like(sums_ref)
            gwa_ref[...] = jnp.zeros_like(gwa_ref)
            gwx_ref[...] = jnp.zeros_like(gwx_ref)
            due[tt:tt + 8, :] = jnp.zeros((8, D), F32)
            dye[tt:tt + 8, :] = jnp.zeros((8, D), F32)
            an[...] = jnp.zeros((8, D), F32)
            gn[...] = jnp.zeros((8, D), F32)

        live = (t > 0).astype(F32)
        xe[0:8, :] = _pj(ph_ref, 3).astype(F32) * live
        ve[0:8, :] = _pj(ph_ref, 1).astype(F32) * _pj(ph_ref, 2).astype(F32) * live
        he[0:8, :] = hh_ref[...].astype(F32) * live
        he[8:8 + tt, :] = hl_ref[...].astype(F32)
        _conv_fwd_rows(tt, proj_ref, prm_ref, xe, ve, u_s, ub_s, ya_s)
        ub = ub_s[...]
        for h in range(HEADS):
            cs = slice(h * HB, (h + 1) * HB)
            rp_s[:, cs] = _dot(ub[:, cs], wa_ref[h]) + prm_ref[P_BA:P_BA + 1, cs]
            ip_s[:, cs] = _dot(ub[:, cs], wx_ref[h]) + prm_ref[P_BX:P_BX + 1, cs]

        ls_all = _log_sigmoid(prm_ref[P_LAM:P_LAM + 1, :])
        row = lax.broadcasted_iota(jnp.int32, (8, CG), 0)
        nblk = tt // 16

        def blk(ib, carry):
            r0 = pl.multiple_of((nblk - 1 - ib) * 16, 16)
            rows = pl.ds(r0, 16)
            for g in range(D // CG):
                cs = slice(g * CG, (g + 1) * CG)
                ls = ls_all[:, cs]
                dm = dmg_ref[rows, cs].astype(F32)
                cb = _pj(proj_ref, 0, rows, cs).astype(F32)
                rg = _pj(proj_ref, 4, rows, cs).astype(F32)
                sga = _sig(_pj(proj_ref, 5, rows, cs).astype(F32))
                sgb = _sig(_pj(proj_ref, 6, rows, cs).astype(F32))
                ya0 = ya_s[rows, cs]
                h16 = he[pl.ds(r0 + 8, 16), cs]
                gl, th = _gelu(rg)
                dgl = 0.5 * (1.0 + th) + 0.5 * rg * (1.0 - th * th) * (_GC * (1.0 + 3.0 * 0.044715 * rg * rg))
                y_a = cb * ya0
                y_b = h16 * gl
                dy_a = dm * sga
                dy_b = dm * sgb
                col = lambda s: slice(s * D + g * CG, s * D + (g + 1) * CG)
                dp_ref[rows, col(5)] = (dm * y_a * sga * (1.0 - sga)).astype(BF16)
                dp_ref[rows, col(6)] = (dm * y_b * sgb * (1.0 - sgb)).astype(BF16)
                dp_ref[rows, col(4)] = (dy_b * h16 * dgl).astype(BF16)
                dp_ref[rows, col(0)] = (dy_a * ya0).astype(BF16)
                dye[rows, cs] = dy_a * cb
                dh16 = dy_b * gl

                a_next = an[:, cs]
                g_next = gn[:, cs]
                s_ba = jnp.zeros((8, CG), F32)
                s_bx = jnp.zeros((8, CG), F32)
                s_ls = jnp.zeros((8, CG), F32)
                for sb in (1, 0):
                    rr = r0 + 8 * sb
                    first = (row + (t * tt + rr)) == 0
                    uu = u_s[pl.ds(rr, 8), cs]
                    r, ig, la, a, m2, mult = _lru_gates(rp_s[pl.ds(rr, 8), cs], ip_s[pl.ds(rr, 8), cs], ls, first)
                    ca = jnp.where(row < 7, pltpu.roll(a, 7, 0), a_next)
                    cb_ = dh16[8 * sb:8 * sb + 8, :]
                    for s in (1, 2, 4):
                        a_sh = jnp.where(row < 8 - s, pltpu.roll(ca, 8 - s, 0), 1.0)
                        b_sh = jnp.where(row < 8 - s, pltpu.roll(cb_, 8 - s, 0), 0.0)
                        cb_ = ca * b_sh + cb_
                        ca = ca * a_sh
                    gv = ca * g_next + cb_
                    g_next = jnp.broadcast_to(gv[0:1, :], gv.shape)
                    a_next = jnp.broadcast_to(a[0:1, :], a.shape)
                    hprev = jnp.where(row >= 1, pltpu.roll(he[pl.ds(rr + 8, 8), cs], 1, 0),
                                      pltpu.roll(he[pl.ds(rr, 8), cs], 1, 0))
                    da = gv * hprev
                    dmult = jnp.where(first, 0.0, gv * ig * uu)
                    dla = da * a + jnp.where(m2 > 0.0, dmult * (-(a * a) / mult), 0.0)
                    drp = dla * (LRU_C * ls) * r * (1.0 - r)
                    dip = gv * mult * uu * ig * (1.0 - ig)
                    s_ls = s_ls + dla * (LRU_C * r)
                    s_ba = s_ba + drp
                    s_bx = s_bx + dip
                    drp_s[pl.ds(rr, 8), cs] = drp
                    dip_s[pl.ds(rr, 8), cs] = dip
                    due[pl.ds(rr, 8), cs] = gv * mult * ig
                an[:, cs] = a_next
                gn[:, cs] = g_next
                sums_ref[M_BA:M_BA + 1, cs] += jnp.sum(s_ba, axis=0, keepdims=True)
                sums_ref[M_BX:M_BX + 1, cs] += jnp.sum(s_bx, axis=0, keepdims=True)
                sums_ref[M_LS:M_LS + 1, cs] += jnp.sum(s_ls, axis=0, keepdims=True)
            return carry

        lax.fori_loop(0, nblk, blk, 0)

        drp_b = drp_s[...].astype(BF16)
        dip_b = dip_s[...].astype(BF16)
        for h in range(HEADS):
            cs = slice(h * HB, (h + 1) * HB)
            due[0:tt, cs] += _dot_nt(drp_b[:, cs], wa_ref[h]) + _dot_nt(dip_b[:, cs], wx_ref[h])
            gwa_ref[h] += _dot_tn(ub[:, cs], drp_b[:, cs])
            gwx_ref[h] += _dot_tn(ub[:, cs], dip_b[:, cs])

        acc8[...] = jnp.zeros_like(acc8)
        _conv_bwd_rows(tt, proj_ref, prm_ref, xe, ve, due, dye, dp_ref, acc8)
        for k, dst in enumerate([M_CBIAS] + [M_WB + k for k in range(4)] + [M_WA + k for k in range(3)]):
            sums_ref[dst:dst + 1, :] += jnp.sum(acc8[8 * k:8 * k + 8, :], axis=0, keepdims=True)
        due[tt:tt + 8, :] = due[0:8, :]
        dye[tt:tt + 8, :] = dye[0:8, :]

        @pl.when(i == nt - 1)
        def _():
            sums_ref[M_LS:M_LS + 1, :] = sums_ref[M_LS:M_LS + 1, :] * _sig(-prm_ref[P_LAM:P_LAM + 1, :])

    big = lambda: pltpu.VMEM((tt + 8, D), F32)
    tile = lambda: pltpu.VMEM((tt, D), F32)
    return pl.pallas_call(
        body, name="mixer_bwd", grid=(nt,),
        in_specs=[pl.BlockSpec((tt, 7 * D), lambda i: (rev(i), 0)),
                  pl.BlockSpec((8, 7 * D), lambda i: (halo(i), 0)),
                  pl.BlockSpec((tt, D), lambda i: (rev(i), 0)),
                  pl.BlockSpec((8, D), lambda i: (halo(i), 0)),
                  pl.BlockSpec((tt, D), lambda i: (rev(i), 0)),
                  pl.BlockSpec((16, D), lambda i: (0, 0)),
                  pl.BlockSpec((HEADS, HB, HB), lambda i: (0, 0, 0)),
                  pl.BlockSpec((HEADS, HB, HB), lambda i: (0, 0, 0))],
        out_specs=[pl.BlockSpec((tt, 7 * D), lambda i: (rev(i), 0)),
                   pl.BlockSpec((16, D), lambda i: (0, 0)),
                   pl.BlockSpec((HEADS, HB, HB), lambda i: (0, 0, 0)),
                   pl.BlockSpec((HEADS, HB, HB), lambda i: (0, 0, 0))],
        out_shape=[jax.ShapeDtypeStruct((t_len, 7 * D), BF16), jax.ShapeDtypeStruct((16, D), F32),
                   jax.ShapeDtypeStruct((HEADS, HB, HB), F32), jax.ShapeDtypeStruct((HEADS, HB, HB), F32)],
        scratch_shapes=[big(), big(), big(), tile(), tile(), tile(), tile(), big(), big(), tile(), tile(),
                        pltpu.VMEM((8, D), F32), pltpu.VMEM((8, D), F32), pltpu.VMEM((tt, D), BF16),
                        pltpu.VMEM((64, D), F32)],
        compiler_params=_cp("arbitrary"),
    )(proj, proj, hl, hl, dmg, prm, wa, wx)


def _in_proj_bwd(dproj, w_in, x, dx1, mod, g_mix):
    t_len = x.shape[0]
    tm = min(TM, t_len)

    def body(dp_ref, w_ref, x_ref, dx1_ref, mod_ref, g_ref, gx_ref, sums_ref, acc):
        @pl.when(pl.program_id(0) == 0)
        def _():
            sums_ref[...] = jnp.zeros_like(sums_ref)

        for rows in _sub_blocks(tm):
            acc[rows, :] = _dot_nt(dp_ref[rows, :], w_ref[...])

        def write(rows, dx):
            gx_ref[rows, :] = dx

        _norm_bwd_rows(tm, 16, acc, x_ref, dx1_ref, mod_ref[1:2, :], g_ref[...], sums_ref, write)

    return pl.pallas_call(
        body, name="in_proj_bwd", grid=(t_len // tm,),
        in_specs=[pl.BlockSpec((tm, 7 * D), lambda i: (i, 0)),
                  _resident((D, 7 * D)),
                  pl.BlockSpec((tm, D), lambda i: (i, 0)), pl.BlockSpec((tm, D), lambda i: (i, 0)),
                  pl.BlockSpec((8, D), lambda i: (0, 0)), pl.BlockSpec((1, D), lambda i: (0, 0))],
        out_specs=[pl.BlockSpec((tm, D), lambda i: (i, 0)), pl.BlockSpec((8, D), lambda i: (0, 0))],
        out_shape=[jax.ShapeDtypeStruct((t_len, D), F32), jax.ShapeDtypeStruct((8, D), F32)],
        scratch_shapes=[pltpu.VMEM((tm, D), F32)],
        compiler_params=_cp("arbitrary"),
    )(dproj, w_in, x, dx1, mod, g_mix)


def _in_wgrad(h, dproj, g_wa, g_wx, order):
    t_len = h.shape[0]
    tk = min(TKI, t_len)
    nk = t_len // tk
    cw = 7 * D // NDEV
    hr = HB // NDEV

    def body(ord_ref, h_ref, d_ref, ga_ref, gx_ref, parts_ref, pa_ref, px_ref, acc, *scr):
        rs, sems = scr[:-6], scr[-6:]
        p, k = pl.program_id(0), pl.program_id(1)

        def head_rows(ref):
            return lambda s: ref.at[:, pl.ds(s * hr, hr), :]

        @pl.when((p == 0) & (k == 0))
        def _():
            for s in range(NDEV):
                _rs_send(head_rows(ga_ref)(s), pa_ref, s, *sems[0:3])
                _rs_send(head_rows(gx_ref)(s), px_ref, s, *sems[3:6])

        @pl.when(k == 0)
        def _():
            acc[...] = jnp.zeros_like(acc)

        acc[...] += _dot_tn(h_ref[...], d_ref[...])

        @pl.when(k == nk - 1)
        def _():
            q = ord_ref[p]
            for half in range(2):
                rs[0][q, half] = acc[:, half * cw:(half + 1) * cw].astype(BF16)
            _rs2_to_sibling(q, rs)

        @pl.when((k == nk - 1) & (p > 0))
        def _():
            _rs2_forward(ord_ref[p - 1], parts_ref, rs)

        @pl.when((p == NCHIP - 1) & (k == nk - 1))
        def _():
            _rs2_forward(ord_ref[p], parts_ref, rs)
            _rs2_finish(parts_ref, rs)
            _rs_finish(head_rows(ga_ref), pa_ref, *sems[0:3])
            _rs_finish(head_rows(gx_ref), px_ref, *sems[3:6])

    return pl.pallas_call(
        body, name="in_wgrad",
        grid_spec=pltpu.PrefetchScalarGridSpec(
            num_scalar_prefetch=1, grid=(NCHIP, nk),
            in_specs=[pl.BlockSpec((tk, D), lambda p, k, o: (k, 0)),
                      pl.BlockSpec((tk, 2 * cw), lambda p, k, o: (k, o[p])), _ANY, _ANY],
            out_specs=[_ANY, _ANY, _ANY],
            scratch_shapes=[pltpu.VMEM((D, 2 * cw), F32)] + _rs2_scratch((D, cw)) + _RS_SEMS * 2),
        out_shape=[jax.ShapeDtypeStruct((NCHIP, D, cw), BF16), jax.ShapeDtypeStruct((NDEV, HEADS, hr, HB), F32),
                   jax.ShapeDtypeStruct((NDEV, HEADS, hr, HB), F32)],
        compiler_params=_cp("arbitrary", "arbitrary"),
    )(order, h, dproj, g_wa, g_wx)


def _ada_fwd(c_all, w_ada, b_cols):
    def body(c_ref, w_ref, b_ref, o_ref):
        cv = c_ref[...]
        o_ref[...] = _dot((cv * _sig(cv)).astype(BF16), w_ref[...].astype(BF16)) + b_ref[...]

    return pl.pallas_call(body, name="ada_fwd", out_shape=jax.ShapeDtypeStruct((16, w_ada.shape[1]), F32),
                          compiler_params=_cp())(c_all, w_ada, b_cols)


def _adam_math(w, g, m, v):
    m = ADAM_B1 * m + (1.0 - ADAM_B1) * g
    v = ADAM_B2 * v + (1.0 - ADAM_B2) * (g * g)
    m_hat = m / (1.0 - ADAM_B1 ** ADAM_STEP)
    v_hat = v / (1.0 - ADAM_B2 ** ADAM_STEP)
    delta = -ADAM_LR * (m_hat / (jnp.sqrt(v_hat) + ADAM_EPS) + ADAM_WD * w)
    return delta, m, v


def _ada_bwd(c_all, dmod_cols, w, m, v):
    rb = 256
    n = w.shape[1]
    nrow = c_all.shape[0]

    def body(c_ref, d_ref, w_ref, m_ref, v_ref, g_ref, dl_ref, nm_ref, nv_ref):
        cv = c_ref[...]
        g = _dot_tn((cv * _sig(cv)).astype(BF16), d_ref[...].astype(BF16))
        g_ref[...] = g
        dl_ref[...], nm_ref[...], nv_ref[...] = _adam_math(w_ref[...], g, m_ref[...], v_ref[...])

    blk = pl.BlockSpec((rb, n), lambda i: (i, 0))
    sds = jax.ShapeDtypeStruct(w.shape, F32)
    return pl.pallas_call(
        body, name="ada_bwd", grid=(D // rb,),
        in_specs=[pl.BlockSpec((nrow, rb), lambda i: (0, i)), pl.BlockSpec((nrow, n), lambda i: (0, 0)), blk, blk, blk],
        out_specs=[blk, blk, blk, blk], out_shape=[sds, sds, sds, sds],
        compiler_params=_cp("parallel"),
    )(c_all, dmod_cols, w, m, v)


def _adam(name, parts, w, m, v):
    p, r, c = parts.shape
    rb = r
    for cand in (256, 128, 64, 32, 16, 8):
        if r % cand == 0 and r >= cand:
            rb = cand
            break

    def body(p_ref, w_ref, m_ref, v_ref, g_ref, dl_ref, nm_ref, nv_ref):
        g = p_ref[0].astype(F32)
        for q in range(1, p):
            g = g + p_ref[q].astype(F32)
        g_ref[...] = g
        dl_ref[...], nm_ref[...], nv_ref[...] = _adam_math(w_ref[...], g, m_ref[...], v_ref[...])

    blk = pl.BlockSpec((rb, c), lambda i: (i, 0))
    sds = jax.ShapeDtypeStruct((r, c), F32)
    return pl.pallas_call(
        body, name=name, grid=(r // rb,),
        in_specs=[pl.BlockSpec((p, rb, c), lambda i: (0, i, 0)), blk, blk, blk],
        out_specs=[blk, blk, blk, blk], out_shape=[sds, sds, sds, sds],
        compiler_params=_cp("parallel"),
    )(parts, w, m, v)


def _my_pos():
    return lax.axis_index("x"), lax.axis_index("y"), lax.axis_index("c")


def _all_gather_small(name, v):
    m_per, n = v.shape

    def body(x_ref, out_ref, send_sems, recv_sems, local_sem):
        x, y, c = _my_pos()
        me, sibling = (x, y, c), (x, y, 1 - c)
        chips = [(1 - x, y), (x, 1 - y), (1 - x, 1 - y)]

        def rows(px, py, pc):
            return out_ref.at[pl.ds((4 * px + 2 * py + pc) * m_per, m_per), :]

        def copy(k, block, to, src=None):
            return pltpu.make_async_remote_copy(
                src_ref=rows(*block) if src is None else src, dst_ref=rows(*block),
                send_sem=send_sems.at[k], recv_sem=recv_sems.at[k], device_id=to, device_id_type=MESH)

        mine = pltpu.make_async_copy(x_ref, rows(*me), local_sem)
        mine.start()
        first = [copy(0, me, sibling, src=x_ref)]
        first += [copy(1 + j, me, (*chip, c), src=x_ref) for j, chip in enumerate(chips)]
        for cp in first:
            cp.start()
        passed = [copy(4 + j, (*chip, c), sibling) for j, chip in enumerate(chips)]
        for j, chip in enumerate(chips):
            copy(1 + j, (*chip, c), me).wait_recv()
            passed[j].start()
        copy(0, sibling, me).wait_recv()
        for j, chip in enumerate(chips):
            copy(4 + j, (*chip, 1 - c), me).wait_recv()
        for cp in first + passed:
            cp.wait_send()
        mine.wait()

    return pl.pallas_call(
        body, name=name, out_shape=jax.ShapeDtypeStruct((NDEV * m_per, n), v.dtype),
        in_specs=[pl.BlockSpec(memory_space=pltpu.VMEM)], out_specs=pl.BlockSpec(memory_space=pltpu.VMEM),
        scratch_shapes=[pltpu.SemaphoreType.DMA((7,)), pltpu.SemaphoreType.DMA((7,)), pltpu.SemaphoreType.DMA],
    )(v)


def _blk_cols(n):
    return lambda ref, b: ref.at[:, pl.ds(pl.multiple_of(b * n, 128), n)]


def _blk_rows(n):
    return lambda ref, b: ref.at[pl.ds(pl.multiple_of(b * n, 8), n), :]


def _blk_lead(ref, b):
    return ref.at[b]


def _blk_heads(ref, b):
    return ref.at[:, pl.ds(pl.multiple_of(b * (HB // NDEV), 8), HB // NDEV), :]


def _ag_phases(ins, outs, slicers, send_sems, recv_sems, local_sems):
    na = len(ins)
    x, y, c = _my_pos()
    me, sibling = (x, y, c), (x, y, 1 - c)
    chips = [(1 - x, y), (x, 1 - y), (1 - x, 1 - y)]

    def copy(a, k, block, to, from_shard=False):
        px, py, pc = block
        dst = slicers[a](outs[a], 4 * px + 2 * py + pc)
        return pltpu.make_async_remote_copy(
            src_ref=ins[a] if from_shard else dst, dst_ref=dst,
            send_sem=send_sems.at[a * 7 + k], recv_sem=recv_sems.at[a * 7 + k], device_id=to, device_id_type=MESH)

    def local(a):
        return pltpu.make_async_copy(ins[a], slicers[a](outs[a], 4 * x + 2 * y + c), local_sems.at[a])

    def firsts(a):
        return [copy(a, 0, me, sibling, True)] + [copy(a, 1 + j, me, (*chip, c), True) for j, chip in enumerate(chips)]

    def start():
        for a in range(na):
            local(a).start()
            for cp in firsts(a):
                cp.start()

    def forward():
        for a in range(na):
            for j, chip in enumerate(chips):
                copy(a, 1 + j, (*chip, c), me).wait_recv()
                copy(a, 4 + j, (*chip, c), sibling).start()

    def finish():
        for a in range(na):
            copy(a, 0, sibling, me).wait_recv()
            for j, chip in enumerate(chips):
                copy(a, 4 + j, (*chip, 1 - c), me).wait_recv()
        for a in range(na):
            for cp in firsts(a) + [copy(a, 4 + j, (*chip, c), sibling) for j, chip in enumerate(chips)]:
                cp.wait_send()
            local(a).wait()

    return start, forward, finish


def _ag_sems(na):
    return [pltpu.SemaphoreType.DMA((7 * na,)), pltpu.SemaphoreType.DMA((7 * na,)), pltpu.SemaphoreType.DMA((na,))]


def _all_gather_weights(shards, fulls, slicers):
    na = len(shards)

    def body(*refs):
        start, forward, finish = _ag_phases(refs[:na], refs[na:2 * na], slicers, *refs[2 * na:])
        start()
        forward()
        finish()

    return pl.pallas_call(
        body, name="gather_weights",
        out_shape=[jax.ShapeDtypeStruct(s, sh.dtype) for s, sh in zip(fulls, shards)],
        in_specs=[_ANY] * na, out_specs=[_ANY] * na, scratch_shapes=_ag_sems(na),
    )(*shards)


def _scatter_grads(grads, shard_shapes, slicers):
    na = len(grads)

    def body(*refs):
        ins, outs = refs[:na], refs[na:2 * na]
        send_sems, recv_sems, local_sems = refs[2 * na:]
        x, y, c = _my_pos()
        me = 4 * x + 2 * y + c
        mine, sent = [], []
        for a in range(na):
            cp = pltpu.make_async_copy(slicers[a](ins[a], me), outs[a].at[me], local_sems.at[a])
            cp.start()
            mine.append(cp)
        rel = [(k >> 2 & 1, k >> 1 & 1, k & 1) for k in range(1, NDEV)]
        for a in range(na):
            for k, (fx, fy, fc) in enumerate(rel):
                px, py, pc = x ^ fx, y ^ fy, c ^ fc
                cp = pltpu.make_async_remote_copy(
                    src_ref=slicers[a](ins[a], 4 * px + 2 * py + pc), dst_ref=outs[a].at[me],
                    send_sem=send_sems.at[a * 7 + k], recv_sem=recv_sems.at[a * 7 + k],
                    device_id=(px, py, pc), device_id_type=MESH)
                cp.start()
                sent.append(cp)
        for a in range(na):
            for k, (fx, fy, fc) in enumerate(rel):
                px, py, pc = x ^ fx, y ^ fy, c ^ fc
                src = 4 * px + 2 * py + pc
                pltpu.make_async_remote_copy(
                    src_ref=slicers[a](ins[a], me), dst_ref=outs[a].at[src],
                    send_sem=send_sems.at[a * 7 + k], recv_sem=recv_sems.at[a * 7 + k],
                    device_id=(px, py, pc), device_id_type=MESH).wait_recv()
        for cp in sent:
            cp.wait_send()
        for cp in mine:
            cp.wait()

    any_spec = pl.BlockSpec(memory_space=pl.ANY)
    return pl.pallas_call(
        body, name="scatter_grads",
        out_shape=[jax.ShapeDtypeStruct((NDEV,) + tuple(s), g.dtype) for s, g in zip(shard_shapes, grads)],
        in_specs=[any_spec] * na, out_specs=[any_spec] * na,
        scratch_shapes=[pltpu.SemaphoreType.DMA((7 * na,)), pltpu.SemaphoreType.DMA((7 * na,)),
                        pltpu.SemaphoreType.DMA((na,))],
    )(*grads)


def _local_step(x, target, mod, g_mix, g_ffn, g_fin, prm, w_in_shard, shards):
    fulls = [(HEADS, HB, HB), (HEADS, HB, HB), (D, D), (NDEV, FB, D), (DFF, D)]
    slicers = [_blk_heads, _blk_heads, _blk_rows(D // NDEV), _blk_lead, _blk_rows(DFF // NDEV)]
    my_chip = _my_index() >> 1
    own_first = (my_chip ^ jnp.arange(NCHIP, dtype=jnp.int32)).astype(jnp.int32)
    proj, h, w_in, (wa, wx) = _in_proj(x, mod, g_mix, w_in_shard, own_first, shards[:2], fulls[:2], slicers[:2])
    merged, hl, (w_out, w_gu, w_down) = _mixer_fwd(proj, prm, wa, wx, shards[2:], fulls[2:], slicers[2:])
    w_gu = w_gu.reshape(2, 4, FB, D)
    x1, h2 = _out_proj(merged, x, mod, g_ffn, w_out)
    gu, dx2, dx2b, loss, d_gfin = _ffn_fwd(h2, x1, target, mod, g_fin, w_gu, w_down)
    dgu, act, dx1, dx1b, dmg, sums2 = _ffn_bwd(dx2, gu, x1, mod, g_ffn, w_gu, w_down, w_out)
    chip_order = _xor_order(_my_index() >> 1, NCHIP)
    p_wgu = _gu_wgrad(h2, dgu, chip_order)
    p_wdown, d_gt2 = _scaled_wgrad("down_wgrad", act, dx2b, w_down, 5, mod, chip_order)
    p_wout, d_gt1 = _scaled_wgrad("out_wgrad", merged.reshape(1, *merged.shape), dx1b, w_out, 2, mod,
                                  jnp.zeros((1,), jnp.int32))
    dproj, msums, g_wa, g_wx = _mixer_bwd(proj, hl, dmg, prm, wa, wx)
    p_win, p_wa, p_wx = _in_wgrad(h, dproj, g_wa, g_wx, chip_order)
    grad_x, sums1 = _in_proj_bwd(dproj, w_in, x, dx1, mod, g_mix)
    return dict(loss=loss, grad_x=grad_x, d_gfin=d_gfin, sums1=sums1, sums2=sums2, msums=msums,
                d_gt1=d_gt1[0:1], d_gt2=d_gt2[0:1], p_win=p_win, p_wa=p_wa, p_wx=p_wx, p_wout=p_wout, p_wgu=p_wgu,
                p_wdown=p_wdown)


def kernel(x, c, w_ada, b_ada, g_norm_mix, w_in, conv_a_w, conv_b_w, conv_b_bias, w_rg_a, b_rg_a, w_rg_x, b_rg_x, lru_lambda, w_out, g_norm_ffn, w_gate_up, w_down, g_norm_final, loss_target, m_w_ada, m_b_ada, m_g_norm_mix, m_w_in, m_conv_a_w, m_conv_b_w, m_conv_b_bias, m_w_rg_a, m_b_rg_a, m_w_rg_x, m_b_rg_x, m_lru_lambda, m_w_out, m_g_norm_ffn, m_w_gate_up, m_w_down, m_g_norm_final, v_w_ada, v_b_ada, v_g_norm_mix, v_w_in, v_conv_a_w, v_conv_b_w, v_conv_b_bias, v_w_rg_a, v_b_rg_a, v_w_rg_x, v_b_rg_x, v_lru_lambda, v_w_out, v_g_norm_ffn, v_w_gate_up, v_w_down, v_g_norm_final):
    me = 4 * lax.axis_index("x") + 2 * lax.axis_index("y") + lax.axis_index("c")
    ncol = w_ada.shape[2]
    cw = conv_a_w.shape[2]

    pack0 = jnp.concatenate([c, conv_a_w.reshape(1, 3 * cw), conv_b_w.reshape(1, 4 * cw)], axis=1)
    got0 = _all_gather_small("gather_c", jnp.broadcast_to(pack0, (8, pack0.shape[1])))
    got0 = got0.reshape(NDEV, 8, -1)[:, 0, :]
    c_all = got0[:, :D]
    conv_a = got0[:, D:D + 3 * cw].reshape(NDEV, 3, cw).transpose(1, 0, 2).reshape(3, D)
    conv_b = got0[:, D + 3 * cw:].reshape(NDEV, 4, cw).transpose(1, 0, 2).reshape(4, D)

    b_cols = lax.dynamic_slice_in_dim(b_ada, me * ncol, ncol, axis=1)
    c16 = jnp.concatenate([c_all, jnp.zeros((8, D), F32)], axis=0)
    mod_cols = _ada_fwd(c16, w_ada[0], b_cols)[:NDEV]
    got1 = _all_gather_small("gather_mod", mod_cols).reshape(NDEV, NDEV, ncol)
    mod6 = lax.dynamic_index_in_dim(got1, me, axis=1, keepdims=False).reshape(6, D)
    mod = jnp.concatenate([mod6, jnp.zeros((2, D), F32)], axis=0)

    tr = lambda a: jnp.swapaxes(a, 1, 2)
    shards = [w_rg_a[0].astype(BF16), w_rg_x[0].astype(BF16), w_out[0].astype(BF16), tr(w_gate_up)[0].astype(BF16),
              w_down[0].astype(BF16)]

    prm = jnp.concatenate([conv_a, conv_b, conv_b_bias, b_rg_a, b_rg_x, lru_lambda, jnp.zeros((5, D), F32)], axis=0)
    r = _local_step(x[0], loss_target[0], mod, g_norm_mix, g_norm_ffn, g_norm_final.reshape(1, D), prm,
                    w_in[0].astype(BF16), shards)

    parts = [r["p_win"], r["p_wa"], r["p_wx"], r["p_wout"], r["p_wgu"], r["p_wdown"]]
    big = {}
    for nm, p, w, m, v in (("w_in", parts[0], w_in, m_w_in, v_w_in), ("w_rg_a", parts[1], w_rg_a, m_w_rg_a, v_w_rg_a),
                           ("w_rg_x", parts[2], w_rg_x, m_w_rg_x, v_w_rg_x), ("w_out", parts[3], w_out, m_w_out, v_w_out),
                           ("w_gate_up", parts[4], tr(w_gate_up), tr(m_w_gate_up), tr(v_w_gate_up)),
                           ("w_down", parts[5], w_down, m_w_down, v_w_down)):
        two_d = (-1, w.shape[-1])
        outs = _adam("adam_" + nm, p.reshape((p.shape[0],) + w.reshape(two_d).shape), w.reshape(two_d), m.reshape(two_d),
                     v.reshape(two_d))
        big[nm] = [o.reshape(w.shape) for o in outs]
    big["w_gate_up"] = [tr(o) for o in big["w_gate_up"]]

    small = jnp.concatenate([
        r["sums1"][S_SH:S_SH + 1], r["sums1"][S_SC:S_SC + 1], r["d_gt1"],
        r["sums2"][S_SH:S_SH + 1], r["sums2"][S_SC:S_SC + 1], r["d_gt2"],
        r["sums1"][S_G:S_G + 1],
        r["msums"][M_CBIAS:M_CBIAS + 1], r["msums"][M_BA:M_BA + 1], r["msums"][M_BX:M_BX + 1],
        r["msums"][M_LS:M_LS + 1],
        r["sums2"][S_G:S_G + 1], r["d_gfin"],
        r["msums"][M_WA:M_WA + 3], r["msums"][M_WB:M_WB + 4],
        jnp.zeros((4, D), F32)], axis=0)
    got2 = _all_gather_small("gather_small", small).reshape(NDEV, 24, D)

    rep_w = jnp.concatenate([b_ada.reshape(6, D), g_norm_mix, conv_b_bias, b_rg_a, b_rg_x, lru_lambda, g_norm_ffn,
                             g_norm_final.reshape(1, D), jnp.zeros((3, D), F32)], axis=0)
    rep_m = jnp.concatenate([m_b_ada.reshape(6, D), m_g_norm_mix, m_conv_b_bias, m_b_rg_a, m_b_rg_x, m_lru_lambda,
                             m_g_norm_ffn, m_g_norm_final.reshape(1, D), jnp.zeros((3, D), F32)], axis=0)
    rep_v = jnp.concatenate([v_b_ada.reshape(6, D), v_g_norm_mix, v_conv_b_bias, v_b_rg_a, v_b_rg_x, v_lru_lambda,
                             v_g_norm_ffn, v_g_norm_final.reshape(1, D), jnp.ones((3, D), F32)], axis=0)
    rep = _adam("adam_rep", got2[:, :16, :], rep_w, rep_m, rep_v)

    conv_parts = lax.dynamic_slice_in_dim(got2[:, 13:21, :], me * cw, cw, axis=2)
    cv_w = jnp.concatenate([conv_a_w[0], conv_b_w[0], jnp.zeros((1, cw), F32)], axis=0)
    cv_m = jnp.concatenate([m_conv_a_w[0], m_conv_b_w[0], jnp.zeros((1, cw), F32)], axis=0)
    cv_v = jnp.concatenate([v_conv_a_w[0], v_conv_b_w[0], jnp.ones((1, cw), F32)], axis=0)
    cvo = _adam("adam_conv", conv_parts, cv_w, cv_m, cv_v)

    dmod_cols = lax.dynamic_slice_in_dim(got2[:, :6, :].reshape(NDEV, 6 * D), me * ncol, ncol, axis=1)
    dmod16 = jnp.concatenate([dmod_cols, jnp.zeros((8, ncol), F32)], axis=0)
    ada = _ada_bwd(c16, dmod16, w_ada[0], m_w_ada[0], v_w_ada[0])

    loss = lax.psum(r["loss"][0, 0], AXES)

    def pick(q):
        one = lambda i: rep[q][i:i + 1]
        return [ada[q].reshape(w_ada.shape), rep[q][0:6].reshape(b_ada.shape), one(6), big["w_in"][q],
                cvo[q][0:3].reshape(conv_a_w.shape), cvo[q][3:7].reshape(conv_b_w.shape), one(7),
                big["w_rg_a"][q], one(8), big["w_rg_x"][q], one(9), one(10), big["w_out"][q], one(11),
                big["w_gate_up"][q], big["w_down"][q], rep[q][12]]

    return (loss, r["grad_x"].reshape(x.shape), *pick(0), *pick(1), *pick(2), *pick(3))
```

```python
import functools
import math

import jax
import jax.numpy as jnp
from jax import lax
from jax.experimental import pallas as pl
from jax.experimental.pallas import tpu as pltpu

F32 = jnp.float32
BF16 = jnp.bfloat16

D = 1024
DFF = 2816
NDEV = 8
HEADS = 4
HB = D // HEADS
FB = DFF // 4
EPS = 1e-6
LRU_C = 8.0
ADAM_LR, ADAM_B1, ADAM_B2, ADAM_EPS, ADAM_WD, ADAM_STEP = 0.001, 0.9, 0.999, 1e-08, 0.01, 10

VMEM_LIMIT = 56 * 1024 * 1024
TM = 512
TMI = 1024
TMF = 256
TK = 2048
TKI = 1024
SUB = 256
TT = 256
CG = 256
MESH = pl.DeviceIdType.MESH
AXES = ("x", "y", "c")


def _cp(*sem):
    return pltpu.CompilerParams(dimension_semantics=sem, vmem_limit_bytes=VMEM_LIMIT)


def _sig(x):
    return 1.0 / (1.0 + jnp.exp(-x))


def _log_sigmoid(x):
    z = jnp.exp(-jnp.abs(x))
    u = 1.0 + z
    d = u - 1.0
    l1p = jnp.where(d == 0.0, z, jnp.log(u) * (z / jnp.where(d == 0.0, 1.0, d)))
    return -(jnp.maximum(-x, 0.0) + l1p)


def _neg_expm1(x):
    p = x * (1.0 + x * 0.5 * (1.0 + x * (1.0 / 3.0) * (1.0 + x * 0.25 * (1.0 + x * 0.2 * (1.0 + x * (1.0 / 6.0))))))
    return jnp.where(x > -0.25, -p, 1.0 - jnp.exp(x))


_GC = math.sqrt(2.0 / math.pi)


def _gelu(x):
    t = jnp.tanh(_GC * (x + 0.044715 * x * x * x))
    return 0.5 * x * (1.0 + t), t


def _dot(a, b):
    return jnp.dot(a, b, preferred_element_type=F32)


def _dot_nt(a, b):
    return lax.dot_general(a, b, (((1,), (1,)), ((), ())), preferred_element_type=F32)


def _dot_tn(a, b):
    return lax.dot_general(a, b, (((0,), (0,)), ((), ())), preferred_element_type=F32)


def _resident(shape):
    return pl.BlockSpec(shape, lambda *_: (0,) * len(shape), pipeline_mode=pl.Buffered(1))


def _sub_blocks(n_rows):
    step = min(SUB, n_rows)
    return [slice(r, r + step) for r in range(0, n_rows, step)]


def _fold8(v):
    return v[0:8] + v[8:16]


def _pj(ref, s, rows=slice(None), cols=slice(0, D)):
    return ref[rows, s * D + cols.start:s * D + cols.stop]


def _in_proj(x, mod, g_mix, w_shard, order, shards, fulls, slicers):
    t_len = x.shape[0]
    tm = min(TMI, t_len)
    ni = t_len // tm
    na = len(shards)
    cw = 7 * D // NDEV
    rc = 32

    def body(ord_ref, x_ref, mod_ref, g_ref, wsh_ref, *rest):
        ins, (proj_ref, h_ref, wfull_ref), outs = rest[:na], rest[na:na + 3], rest[na + 3:2 * na + 3]
        h_scr, w_scr, wsend, wrecv, wlocal, wout = rest[2 * na + 3:2 * na + 9]
        start, forward, finish = _ag_phases(ins, outs, slicers, *rest[2 * na + 9:])
        p, i = pl.program_id(0), pl.program_id(1)
        x_, y_, c = _my_pos()
        me, sibling = (x_, y_, c), (x_, y_, 1 - c)
        chip_at = [None, (x_, 1 - y_), (1 - x_, y_), (1 - x_, 1 - y_)]

        def cols(px, py, pc):
            return w_scr.at[:, pl.ds(pl.multiple_of((4 * px + 2 * py + pc) * cw, 128), cw)]

        def wcopy(k, block, to, from_shard=False):
            dst = cols(*block)
            return pltpu.make_async_remote_copy(src_ref=wsh_ref if from_shard else dst, dst_ref=dst,
                                                send_sem=wsend.at[k], recv_sem=wrecv.at[k], device_id=to,
                                                device_id_type=MESH)

        own_local = pltpu.make_async_copy(wsh_ref, cols(*me), wlocal)
        to_hbm = pltpu.make_async_copy(w_scr, wfull_ref, wout)

        @pl.when((p == 0) & (i == 0))
        def _():
            own_local.start()
            wcopy(0, me, sibling, True).start()
            for q in (1, 2):
                wcopy(q, me, (*chip_at[q], c), True).start()
            own_local.wait()
            wcopy(0, sibling, me).wait_recv()

        @pl.when((p == 0) & (i == ni // 2))
        def _():
            wcopy(3, me, (*chip_at[3], c), True).start()

        for q in (1, 2, 3):
            @pl.when((p == q - 1) & (i == ni - 1))
            def _():
                wcopy(q, (*chip_at[q], c), me).wait_recv()
                wcopy(3 + q, (*chip_at[q], c), sibling).start()

            @pl.when((p == q) & (i == 0))
            def _():
                wcopy(3 + q, (*chip_at[q], 1 - c), me).wait_recv()

        @pl.when((p == NCHIP // 2) & (i == 0))
        def _():
            start()

        @pl.when((p == NCHIP - 1) & (i == (3 * ni) // 4))
        def _():
            forward()

        @pl.when((p == NCHIP - 1) & (i == 0))
        def _():
            to_hbm.start()

        gs = g_ref[...] * (1.0 + mod_ref[1:2, :])
        sh = mod_ref[0:1, :]

        wcols = pl.ds(pl.multiple_of(ord_ref[p] * (2 * cw), 128), 2 * cw)
        for sub in _sub_blocks(tm):
            for r0 in range(sub.start, sub.stop, rc):
                xv = x_ref[r0:r0 + rc, :]
                r = lax.rsqrt(jnp.mean(xv * xv, axis=-1, keepdims=True) + EPS)
                h_scr[r0:r0 + rc, :] = (xv * r * gs + sh).astype(BF16)
            proj_ref[sub, :] = _dot(h_scr[sub, :], w_scr[:, wcols]).astype(BF16)

        @pl.when(p == 0)
        def _():
            h_ref[...] = h_scr[...]

        @pl.when((p == NCHIP - 1) & (i == ni - 1))
        def _():
            wcopy(0, me, sibling, True).wait_send()
            for q in (1, 2, 3):
                wcopy(q, me, (*chip_at[q], c), True).wait_send()
                wcopy(3 + q, (*chip_at[q], c), sibling).wait_send()
            finish()
            to_hbm.wait()

    res = pl.pallas_call(
        body, name="in_proj",
        grid_spec=pltpu.PrefetchScalarGridSpec(
            num_scalar_prefetch=1, grid=(NCHIP, ni),
            in_specs=[pl.BlockSpec((tm, D), lambda p, i, o: (i, 0)),
                      pl.BlockSpec((8, D), lambda p, i, o: (0, 0)),
                      pl.BlockSpec((1, D), lambda p, i, o: (0, 0))] + [_ANY] * (1 + na),
            out_specs=[pl.BlockSpec((tm, 2 * cw), lambda p, i, o: (i, o[p])),
                       pl.BlockSpec((tm, D), lambda p, i, o: (jnp.where(p == 0, i, ni - 1), 0))]
            + [_ANY] * (1 + na),
            scratch_shapes=[pltpu.VMEM((tm, D), BF16), pltpu.VMEM((D, 7 * D), BF16),
                            pltpu.SemaphoreType.DMA((7,)), pltpu.SemaphoreType.DMA((7,)),
                            pltpu.SemaphoreType.DMA, pltpu.SemaphoreType.DMA] + _ag_sems(na)),
        out_shape=[jax.ShapeDtypeStruct((t_len, 7 * D), BF16), jax.ShapeDtypeStruct((t_len, D), BF16),
                   jax.ShapeDtypeStruct((D, 7 * D), BF16)]
        + [jax.ShapeDtypeStruct(f, sh.dtype) for f, sh in zip(fulls, shards)],
        compiler_params=_cp("arbitrary", "arbitrary"),
    )(order, x, mod, g_mix, w_shard, *shards)
    return res[0], res[1], res[2], res[3:]


P_WA, P_WB, P_CBIAS, P_BA, P_BX, P_LAM = 0, 3, 7, 8, 9, 10


def _lru_gates(rp, ip, ls, first_row):
    r = _sig(rp)
    ig = _sig(ip)
    la = LRU_C * r * ls
    a = jnp.exp(la)
    m2 = _neg_expm1(2.0 * la)
    mult = jnp.where(first_row, 1.0, jnp.sqrt(jnp.maximum(m2, 0.0)))
    return r, ig, la, a, m2, mult


def _shift_down(cur, prev, s, row):
    return jnp.where(row >= s, pltpu.roll(cur, s, 0), pltpu.roll(prev, s, 0))


def _shift_up(cur, nxt, s, row):
    return jnp.where(row < 8 - s, pltpu.roll(cur, 8 - s, 0), pltpu.roll(nxt, 8 - s, 0))


def _conv_fwd_rows(tt, proj_ref, prm_ref, xe, ve, u_s, ub_s, ya_s):
    row = lax.broadcasted_iota(jnp.int32, (8, CG), 0)
    w_b = [prm_ref[P_WB + k:P_WB + k + 1, :] for k in range(4)]
    w_a = [prm_ref[P_WA + k:P_WA + k + 1, :] for k in range(3)]
    bias = prm_ref[P_CBIAS:P_CBIAS + 1, :]

    def blk(ib, carry):
        r0 = pl.multiple_of(ib * 16, 16)
        rows = pl.ds(r0, 16)
        for g in range(D // CG):
            cs = slice(g * CG, (g + 1) * CG)
            x16 = _pj(proj_ref, 3, rows, cs).astype(F32)
            v16 = _pj(proj_ref, 1, rows, cs).astype(F32) * _pj(proj_ref, 2, rows, cs).astype(F32)
            xp = xe[pl.ds(r0, 8), cs]
            vp = ve[pl.ds(r0, 8), cs]
            xe[pl.ds(r0 + 8, 16), cs] = x16
            ve[pl.ds(r0 + 8, 16), cs] = v16
            us, yas = [], []
            for sb in range(2):
                xc, vc = x16[8 * sb:8 * sb + 8], v16[8 * sb:8 * sb + 8]
                u8 = bias[:, cs] + w_b[3][:, cs] * xc
                for s in (1, 2, 3):
                    u8 = u8 + w_b[3 - s][:, cs] * _shift_down(xc, xp, s, row)
                y8 = w_a[2][:, cs] * vc
                for s in (1, 2):
                    y8 = y8 + w_a[2 - s][:, cs] * _shift_down(vc, vp, s, row)
                us.append(u8)
                yas.append(y8)
                xp, vp = xc, vc
            u16 = jnp.concatenate(us, axis=0)
            u_s[rows, cs] = u16
            ub_s[rows, cs] = u16.astype(BF16)
            ya_s[rows, cs] = jnp.concatenate(yas, axis=0)
        return carry

    lax.fori_loop(0, tt // 16, blk, 0)


def _mixer_fwd(proj, prm, wa, wx, shards, fulls, slicers):
    t_len = proj.shape[0]
    tt = min(TT, t_len)
    nt = t_len // tt
    na = len(shards)

    def body(proj_ref, prm_ref, wa_ref, wx_ref, *rest):
        ins, (mg_ref, hl_ref), outs = rest[:na], rest[na:na + 2], rest[na + 2:2 * na + 2]
        xe, ve, hc, u_s, ya_s, rp_s, ip_s, ub_s = rest[2 * na + 2:2 * na + 10]
        start, forward, finish = _ag_phases(ins, outs, slicers, *rest[2 * na + 10:])
        t = pl.program_id(0)

        @pl.when(t == 0)
        def _():
            start()
            xe[0:8, :] = jnp.zeros((8, D), F32)
            ve[0:8, :] = jnp.zeros((8, D), F32)
            hc[...] = jnp.zeros((8, D), F32)

        @pl.when(t == (3 * nt) // 4)
        def _():
            forward()

        _conv_fwd_rows(tt, proj_ref, prm_ref, xe, ve, u_s, ub_s, ya_s)
        xe[0:8, :] = xe[tt:tt + 8, :]
        ve[0:8, :] = ve[tt:tt + 8, :]

        ub = ub_s[...]
        for h in range(HEADS):
            cs = slice(h * HB, (h + 1) * HB)
            rp_s[:, cs] = _dot(ub[:, cs], wa_ref[h]) + prm_ref[P_BA:P_BA + 1, cs]
            ip_s[:, cs] = _dot(ub[:, cs], wx_ref[h]) + prm_ref[P_BX:P_BX + 1, cs]

        ls_all = _log_sigmoid(prm_ref[P_LAM:P_LAM + 1, :])
        row = lax.broadcasted_iota(jnp.int32, (8, CG), 0)

        def blk(i, carry):
            r0 = pl.multiple_of(i * 16, 16)
            for g in range(D // CG):
                cs = slice(g * CG, (g + 1) * CG)
                ls = ls_all[:, cs]
                hprev = hc[:, cs]
                hs = []
                for sb in range(2):
                    rr = r0 + 8 * sb
                    first = (row + (t * tt + rr)) == 0
                    _, ig, _, a, _, mult = _lru_gates(rp_s[pl.ds(rr, 8), cs], ip_s[pl.ds(rr, 8), cs], ls, first)
                    b = mult * (ig * u_s[pl.ds(rr, 8), cs])
                    for s in (1, 2, 4):
                        a_sh = jnp.where(row >= s, pltpu.roll(a, s, 0), 1.0)
                        b_sh = jnp.where(row >= s, pltpu.roll(b, s, 0), 0.0)
                        b = a * b_sh + b
                        a = a * a_sh
                    hv = a * hprev + b
                    hprev = jnp.broadcast_to(hv[7:8, :], hv.shape)
                    hs.append(hv)
                hc[:, cs] = hprev
                h16 = jnp.concatenate(hs, axis=0)
                rows = pl.ds(r0, 16)
                gl, _ = _gelu(_pj(proj_ref, 4, rows, cs).astype(F32))
                y_b = h16 * gl
                y_a = _pj(proj_ref, 0, rows, cs).astype(F32) * ya_s[rows, cs]
                mg = (_sig(_pj(proj_ref, 5, rows, cs).astype(F32)) * y_a
                      + _sig(_pj(proj_ref, 6, rows, cs).astype(F32)) * y_b)
                mg_ref[rows, cs] = mg.astype(BF16)
                hl_ref[rows, cs] = h16.astype(BF16)
            return carry

        lax.fori_loop(0, tt // 16, blk, 0)

        @pl.when(t == nt - 1)
        def _():
            finish()

    res = pl.pallas_call(
        body, name="mixer_fwd", grid=(nt,),
        in_specs=[pl.BlockSpec((tt, 7 * D), lambda t: (t, 0)),
                  pl.BlockSpec((16, D), lambda t: (0, 0)),
                  pl.BlockSpec((HEADS, HB, HB), lambda t: (0, 0, 0)),
                  pl.BlockSpec((HEADS, HB, HB), lambda t: (0, 0, 0))] + [_ANY] * na,
        out_specs=[pl.BlockSpec((tt, D), lambda t: (t, 0)), pl.BlockSpec((tt, D), lambda t: (t, 0))] + [_ANY] * na,
        out_shape=[jax.ShapeDtypeStruct((t_len, D), BF16), jax.ShapeDtypeStruct((t_len, D), BF16)]
        + [jax.ShapeDtypeStruct(f, sh.dtype) for f, sh in zip(fulls, shards)],
        scratch_shapes=[pltpu.VMEM((tt + 8, D), F32), pltpu.VMEM((tt + 8, D), F32), pltpu.VMEM((8, D), F32),
                        pltpu.VMEM((tt, D), F32), pltpu.VMEM((tt, D), F32), pltpu.VMEM((tt, D), F32),
                        pltpu.VMEM((tt, D), F32), pltpu.VMEM((tt, D), BF16)] + _ag_sems(na),
        compiler_params=_cp("arbitrary"),
    )(proj, prm, wa, wx, *shards)
    return res[0], res[1], res[2:]


def _out_proj(merged, x, mod, g_ffn, w_out):
    t_len = x.shape[0]
    tm = min(TM, t_len)

    def body(mg_ref, x_ref, mod_ref, g_ref, w_ref, x1_ref, h2_ref):
        gt1 = mod_ref[2:3, :]
        gs = g_ref[...] * (1.0 + mod_ref[4:5, :])
        sh = mod_ref[3:4, :]
        for sub in _sub_blocks(tm):
            x1_ref[sub, :] = x_ref[sub, :] + gt1 * _dot(mg_ref[sub, :], w_ref[...])
            for r0 in range(sub.start, sub.stop, 16):
                x1 = x1_ref[r0:r0 + 16, :]
                r = lax.rsqrt(jnp.mean(x1 * x1, axis=-1, keepdims=True) + EPS)
                h2_ref[r0:r0 + 16, :] = (x1 * r * gs + sh).astype(BF16)

    return pl.pallas_call(
        body, name="out_proj", grid=(t_len // tm,),
        in_specs=[pl.BlockSpec((tm, D), lambda i: (i, 0)), pl.BlockSpec((tm, D), lambda i: (i, 0)),
                  pl.BlockSpec((8, D), lambda i: (0, 0)), pl.BlockSpec((1, D), lambda i: (0, 0)),
                  pl.BlockSpec((D, D), lambda i: (0, 0))],
        out_specs=[pl.BlockSpec((tm, D), lambda i: (i, 0)), pl.BlockSpec((tm, D), lambda i: (i, 0))],
        out_shape=[jax.ShapeDtypeStruct((t_len, D), F32), jax.ShapeDtypeStruct((t_len, D), BF16)],
        compiler_params=_cp("parallel"),
    )(merged, x, mod, g_ffn, w_out)


def _ffn_fwd(h2, x1, target, mod, g_fin, w_gu, w_down):
    t_len = x1.shape[0]
    tm = min(TMF, t_len)

    def body(h2_ref, x1_ref, tg_ref, mod_ref, g_ref, wgu_ref, wd_ref, gu_ref, dx2_ref, dx2b_ref, loss_ref, dg_ref, acc):
        @pl.when(pl.program_id(0) == 0)
        def _():
            loss_ref[...] = jnp.zeros_like(loss_ref)
            dg_ref[...] = jnp.zeros_like(dg_ref)

        hb = h2_ref[...]
        ffn = None
        for j in range(4):
            gate = _dot_nt(hb, wgu_ref[0, j])
            up = _dot_nt(hb, wgu_ref[1, j])
            gu_ref[0, j] = gate.astype(BF16)
            gu_ref[1, j] = up.astype(BF16)
            act = (gate * _sig(gate) * up).astype(BF16)
            part = _dot(act, wd_ref[j * FB:(j + 1) * FB, :])
            ffn = part if ffn is None else ffn + part
        acc[...] = ffn

        gt2 = mod_ref[5:6, :]
        gf = g_ref[...]

        s_loss = s_dg = jnp.zeros((8, D), F32)
        for r0 in range(0, tm, 16):
            rows = slice(r0, r0 + 16)
            x2 = x1_ref[rows, :] + gt2 * acc[rows, :]
            r = lax.rsqrt(jnp.mean(x2 * x2, axis=-1, keepdims=True) + EPS)
            xn = x2 * r
            diff = xn * gf - tg_ref[rows, :]
            dy = diff * (1.0 / D)
            dxn = dy * gf
            dx2 = r * (dxn - xn * jnp.mean(dxn * xn, axis=-1, keepdims=True))
            dx2_ref[rows, :] = dx2
            dx2b_ref[rows, :] = dx2.astype(BF16)
            s_loss, s_dg = s_loss + _fold8(diff * diff), s_dg + _fold8(dy * xn)
        loss_ref[...] += jnp.sum(s_loss) * (0.5 / D)
        dg_ref[...] += jnp.sum(s_dg, axis=0, keepdims=True)

    row = pl.BlockSpec((tm, D), lambda i: (i, 0))
    return pl.pallas_call(
        body, name="ffn_fwd", grid=(t_len // tm,),
        in_specs=[row, row, row, pl.BlockSpec((8, D), lambda i: (0, 0)), pl.BlockSpec((1, D), lambda i: (0, 0)),
                  _resident((2, 4, FB, D)), _resident((DFF, D))],
        out_specs=[pl.BlockSpec((2, 4, tm, FB), lambda i: (0, 0, i, 0)), row, row,
                   pl.BlockSpec((1, 128), lambda i: (0, 0)), pl.BlockSpec((1, D), lambda i: (0, 0))],
        out_shape=[jax.ShapeDtypeStruct((2, 4, t_len, FB), BF16), jax.ShapeDtypeStruct((t_len, D), F32),
                   jax.ShapeDtypeStruct((t_len, D), BF16),
                   jax.ShapeDtypeStruct((1, 128), F32), jax.ShapeDtypeStruct((1, D), F32)],
        scratch_shapes=[pltpu.VMEM((tm, D), F32)],
        compiler_params=_cp("arbitrary"),
    )(h2, x1, target, mod, g_fin, w_gu, w_down)


S_SH, S_SC, S_G = 0, 1, 2


def _norm_bwd_rows(span, sums, dh_ref, x_ref, dres_ref, scale, gain, write):
    gs = 1.0 + scale
    s_sh, s_sc, s_g = sums
    for r0 in range(span.start, span.stop, 16):
        rows = slice(r0, r0 + 16)
        dh = dh_ref[rows, :]
        xv = x_ref[rows, :]
        r = lax.rsqrt(jnp.mean(xv * xv, axis=-1, keepdims=True) + EPS)
        xn = xv * r
        dhn = dh * gs
        dxn = dhn * gain
        write(rows, dres_ref[rows, :] + r * (dxn - xn * jnp.mean(dxn * xn, axis=-1, keepdims=True)))
        s_sh, s_sc, s_g = s_sh + _fold8(dh), s_sc + _fold8(dh * (xn * gain)), s_g + _fold8(dhn * xn)
    return s_sh, s_sc, s_g


def _add_norm_sums(sums_ref, sums):
    for dst, s in zip((S_SH, S_SC, S_G), sums):
        sums_ref[dst:dst + 1, :] += jnp.sum(s, axis=0, keepdims=True)


def _ffn_bwd(dx2, gu, x1, mod, g_ffn, w_gu, w_down, w_out):
    t_len = x1.shape[0]
    tm = min(TMF, t_len)

    def body(dx2_ref, gu_ref, x1_ref, mod_ref, g_ref, wgu_ref, wd_ref, wo_ref,
             dgu_ref, act_ref, dx1_ref, dx1b_ref, dmg_ref, sums_ref, acc, dmo):
        @pl.when(pl.program_id(0) == 0)
        def _():
            sums_ref[...] = jnp.zeros_like(sums_ref)

        dffn = (dx2_ref[...] * mod_ref[5:6, :]).astype(BF16)
        dh2 = None
        for j in range(4):
            dact = _dot_nt(dffn, wd_ref[j * FB:(j + 1) * FB, :])
            gate = gu_ref[0, j].astype(F32)
            up = gu_ref[1, j].astype(F32)
            sg = _sig(gate)
            silu = gate * sg
            act_ref[j] = (silu * up).astype(BF16)
            dgate = (dact * up * (sg * (1.0 + gate * (1.0 - sg)))).astype(BF16)
            dup = (dact * silu).astype(BF16)
            dgu_ref[0, j] = dgate
            dgu_ref[1, j] = dup
            part = _dot(dgate, wgu_ref[0, j]) + _dot(dup, wgu_ref[1, j])
            dh2 = part if dh2 is None else dh2 + part
        acc[...] = dh2

        gt1 = mod_ref[2:3, :]

        def write(rows, dx1):
            dx1_ref[rows, :] = dx1
            dx1b_ref[rows, :] = dx1.astype(BF16)
            dmo[rows, :] = (dx1 * gt1).astype(BF16)

        zero = jnp.zeros((8, D), F32)
        sums = (zero, zero, zero)
        for sub in (slice(0, tm // 2), slice(tm // 2, tm)):
            sums = _norm_bwd_rows(sub, sums, acc, x1_ref, dx2_ref, mod_ref[4:5, :], g_ref[...], write)
            dmg_ref[sub, :] = _dot_nt(dmo[sub, :], wo_ref[...]).astype(BF16)
        _add_norm_sums(sums_ref, sums)

    row = pl.BlockSpec((tm, D), lambda i: (i, 0))
    return pl.pallas_call(
        body, name="ffn_bwd", grid=(t_len // tm,),
        in_specs=[row, pl.BlockSpec((2, 4, tm, FB), lambda i: (0, 0, i, 0)), row,
                  pl.BlockSpec((8, D), lambda i: (0, 0)), pl.BlockSpec((1, D), lambda i: (0, 0)),
                  _resident((2, 4, FB, D)), _resident((DFF, D)), _resident((D, D))],
        out_specs=[pl.BlockSpec((2, 4, tm, FB), lambda i: (0, 0, i, 0)),
                   pl.BlockSpec((4, tm, FB), lambda i: (0, i, 0)), row, row, row,
                   pl.BlockSpec((8, D), lambda i: (0, 0))],
        out_shape=[jax.ShapeDtypeStruct((2, 4, t_len, FB), BF16), jax.ShapeDtypeStruct((4, t_len, FB), BF16),
                   jax.ShapeDtypeStruct((t_len, D), F32), jax.ShapeDtypeStruct((t_len, D), BF16),
                   jax.ShapeDtypeStruct((t_len, D), BF16), jax.ShapeDtypeStruct((8, D), F32)],
        scratch_shapes=[pltpu.VMEM((tm, D), F32), pltpu.VMEM((tm, D), BF16)],
        compiler_params=_cp("arbitrary"),
    )(dx2, gu, x1, mod, g_ffn, w_gu, w_down, w_out)


def _my_pos():
    return lax.axis_index("x"), lax.axis_index("y"), lax.axis_index("c")


def _my_index():
    x, y, c = _my_pos()
    return 4 * x + 2 * y + c


def _device_of(b):
    return (b >> 2) & 1, (b >> 1) & 1, b & 1


def _rs_send(src, parts_ref, b, send_sems, recv_sems, local_sem):
    me = _my_index()
    dst = parts_ref.at[me]

    @pl.when(b == me)
    def _():
        pltpu.make_async_copy(src, dst, local_sem).start()

    @pl.when(b != me)
    def _():
        pltpu.make_async_remote_copy(src_ref=src, dst_ref=dst, send_sem=send_sems.at[b], recv_sem=recv_sems.at[me],
                                     device_id=_device_of(b), device_id_type=MESH).start()


def _rs_finish(src_of, parts_ref, send_sems, recv_sems, local_sem):
    me = _my_index()
    for s in range(NDEV):
        @pl.when(s != me)
        def _():
            cp = pltpu.make_async_remote_copy(src_ref=src_of(s), dst_ref=parts_ref.at[s], send_sem=send_sems.at[s],
                                              recv_sem=recv_sems.at[s], device_id=_device_of(s), device_id_type=MESH)
            cp.wait_send()
            cp.wait_recv()

        @pl.when(s == me)
        def _():
            pltpu.make_async_copy(src_of(s), parts_ref.at[s], local_sem).wait()


_RS_SEMS = [pltpu.SemaphoreType.DMA((NDEV,)), pltpu.SemaphoreType.DMA((NDEV,)), pltpu.SemaphoreType.DMA]
_ANY = pl.BlockSpec(memory_space=pl.ANY)


def _xor_order(me, n):
    return (me ^ (n - 1 - jnp.arange(n, dtype=jnp.int32))).astype(jnp.int32)


NCHIP = NDEV // 2


def _rs2_scratch(half_shape):
    blocks = lambda *lead: pltpu.VMEM(lead + tuple(half_shape), BF16)
    return [blocks(NCHIP, 2), blocks(NCHIP)] + [pltpu.SemaphoreType.DMA((NCHIP,))] * 4 + [pltpu.SemaphoreType.DMA]


def _rs2_to_sibling(q, rs):
    stage, from_sib, d_send, d_recv = rs[:4]
    x, y, c = _my_pos()
    pltpu.make_async_remote_copy(src_ref=stage.at[q, 1 - c], dst_ref=from_sib.at[q], send_sem=d_send.at[q],
                                 recv_sem=d_recv.at[q], device_id=(x, y, 1 - c), device_id_type=MESH).start()


def _rs2_forward(q, parts_ref, rs):
    stage, chip_sum, d_send, d_recv, i_send, i_recv, local_sem = rs
    x, y, c = _my_pos()
    my_chip = 2 * x + y
    pltpu.make_async_remote_copy(src_ref=stage.at[q, c], dst_ref=chip_sum.at[q], send_sem=d_send.at[q],
                                 recv_sem=d_recv.at[q], device_id=(x, y, 1 - c), device_id_type=MESH).wait_recv()
    chip_sum[q] = (stage[q, c].astype(F32) + chip_sum[q].astype(F32)).astype(BF16)

    @pl.when(q == my_chip)
    def _():
        pltpu.make_async_copy(chip_sum.at[q], parts_ref.at[my_chip], local_sem).start()

    @pl.when(q != my_chip)
    def _():
        pltpu.make_async_remote_copy(src_ref=chip_sum.at[q], dst_ref=parts_ref.at[my_chip], send_sem=i_send.at[q],
                                     recv_sem=i_recv.at[my_chip], device_id=((q >> 1) & 1, q & 1, c),
                                     device_id_type=MESH).start()


def _rs2_finish(parts_ref, rs):
    stage, chip_sum, d_send, d_recv, i_send, i_recv, local_sem = rs
    x, y, c = _my_pos()
    my_chip = 2 * x + y
    for q in range(NCHIP):
        pltpu.make_async_remote_copy(src_ref=stage.at[q, 1 - c], dst_ref=chip_sum.at[q], send_sem=d_send.at[q],
                                     recv_sem=d_recv.at[q], device_id=(x, y, 1 - c), device_id_type=MESH).wait_send()

        @pl.when(q != my_chip)
        def _():
            cp = pltpu.make_async_remote_copy(src_ref=chip_sum.at[q], dst_ref=parts_ref.at[q], send_sem=i_send.at[q],
                                              recv_sem=i_recv.at[q], device_id=((q >> 1) & 1, q & 1, c),
                                              device_id_type=MESH)
            cp.wait_send()
            cp.wait_recv()

        @pl.when(q == my_chip)
        def _():
            pltpu.make_async_copy(chip_sum.at[q], parts_ref.at[q], local_sem).wait()


def _gu_wgrad(h2, dgu, order):
    t_len = h2.shape[0]
    tk = min(TK, t_len)
    nk = t_len // tk

    def body(ord_ref, h_ref, d_ref, parts_ref, acc, *rs):
        p, k = pl.program_id(0), pl.program_id(1)

        @pl.when(k == 0)
        def _():
            acc[...] = jnp.zeros_like(acc)

        hb = h_ref[...]
        for half in range(2):
            acc[half] += _dot_tn(d_ref[0, half], hb)

        @pl.when(k == nk - 1)
        def _():
            q = ord_ref[p]
            rs[0][q] = acc[...].astype(BF16)
            _rs2_to_sibling(q, rs)

        @pl.when((k == nk - 1) & (p > 0))
        def _():
            _rs2_forward(ord_ref[p - 1], parts_ref, rs)

        @pl.when((p == NCHIP - 1) & (k == nk - 1))
        def _():
            _rs2_forward(ord_ref[p], parts_ref, rs)
            _rs2_finish(parts_ref, rs)

    return pl.pallas_call(
        body, name="gu_wgrad",
        grid_spec=pltpu.PrefetchScalarGridSpec(
            num_scalar_prefetch=1, grid=(NCHIP, nk),
            in_specs=[pl.BlockSpec((tk, D), lambda p, k, o: (k, 0)),
                      pl.BlockSpec((1, 2, tk, FB), lambda p, k, o: (o[p], 0, k, 0))],
            out_specs=_ANY,
            scratch_shapes=[pltpu.VMEM((2, FB, D), F32)] + _rs2_scratch((FB, D))),
        out_shape=jax.ShapeDtypeStruct((NCHIP, FB, D), BF16),
        compiler_params=_cp("arbitrary", "arbitrary"),
    )(order, h2, dgu.reshape(NCHIP, 2, t_len, FB))


def _scaled_wgrad(name, a, dx, w, gate_row, mod, order):
    nb, t_len, kb = a.shape
    tk = min(TK, t_len)
    nk = t_len // tk
    cpb = NCHIP // nb
    rows = kb // (2 * cpb)

    def body(ord_ref, a_ref, dx_ref, w_ref, mod_ref, parts_ref, dg_ref, acc, *rs):
        p, k = pl.program_id(0), pl.program_id(1)
        j = ord_ref[p]

        @pl.when((p == 0) & (k == 0))
        def _():
            dg_ref[...] = jnp.zeros_like(dg_ref)

        @pl.when(k == 0)
        def _():
            acc[...] = jnp.zeros_like(acc)

        acc[...] += _dot_tn(a_ref[0], dx_ref[...])

        @pl.when(k == nk - 1)
        def _():
            z = acc[...]
            zg = (z * mod_ref[gate_row:gate_row + 1, :]).astype(BF16)
            dg_ref[0:1, :] += jnp.sum(z * w_ref[...].astype(F32), axis=0, keepdims=True)
            for i in range(cpb):
                q = j * cpb + i
                for half in range(2):
                    rs[0][q, half] = zg[(2 * i + half) * rows:(2 * i + half + 1) * rows]
                _rs2_to_sibling(q, rs)

        if cpb == 1:
            @pl.when((k == nk - 1) & (p > 0))
            def _():
                _rs2_forward(ord_ref[p - 1], parts_ref, rs)

        @pl.when((p == nb - 1) & (k == nk - 1))
        def _():
            for i in range(cpb):
                _rs2_forward(j * cpb + i, parts_ref, rs)
            _rs2_finish(parts_ref, rs)

    return pl.pallas_call(
        body, name=name,
        grid_spec=pltpu.PrefetchScalarGridSpec(
            num_scalar_prefetch=1, grid=(nb, nk),
            in_specs=[pl.BlockSpec((1, tk, kb), lambda p, k, o: (o[p], k, 0)),
                      pl.BlockSpec((tk, D), lambda p, k, o: (k, 0)),
                      pl.BlockSpec((kb, D), lambda p, k, o: (o[p], 0)),
                      pl.BlockSpec((8, D), lambda p, k, o: (0, 0))],
            out_specs=[_ANY, pl.BlockSpec((8, D), lambda p, k, o: (0, 0))],
            scratch_shapes=[pltpu.VMEM((kb, D), F32)] + _rs2_scratch((rows, D))),
        out_shape=[jax.ShapeDtypeStruct((NCHIP, rows, D), BF16), jax.ShapeDtypeStruct((8, D), F32)],
        compiler_params=_cp("arbitrary", "arbitrary"),
    )(order, a, dx, w, mod)


M_WA, M_WB, M_CBIAS, M_BA, M_BX, M_LS = 0, 3, 7, 8, 9, 10


def _conv_bwd_rows(tt, proj_ref, prm_ref, xe, ve, due, dye, dp_ref, acc8):
    row = lax.broadcasted_iota(jnp.int32, (8, CG), 0)
    w_b = [prm_ref[P_WB + k:P_WB + k + 1, :] for k in range(4)]
    w_a = [prm_ref[P_WA + k:P_WA + k + 1, :] for k in range(3)]

    def blk(ib, carry):
        r0 = pl.multiple_of(ib * 16, 16)
        rows = pl.ds(r0, 16)
        for g in range(D // CG):
            cs = slice(g * CG, (g + 1) * CG)
            du16, du_after = due[rows, cs], due[pl.ds(r0 + 16, 8), cs]
            dy16, dy_after = dye[rows, cs], dye[pl.ds(r0 + 16, 8), cs]
            x16, xp = xe[pl.ds(r0 + 8, 16), cs], xe[pl.ds(r0, 8), cs]
            v16, vp = ve[pl.ds(r0 + 8, 16), cs], ve[pl.ds(r0, 8), cs]
            acc = [acc8[8 * k:8 * k + 8, cs] for k in range(8)]
            drx, dv = [], []
            for sb in range(2):
                lo = slice(8 * sb, 8 * sb + 8)
                duc, dyc, xc, vc = du16[lo], dy16[lo], x16[lo], v16[lo]
                du_n = du16[8:16] if sb == 0 else du_after
                dy_n = dy16[8:16] if sb == 0 else dy_after
                acc[0] = acc[0] + duc
                acc[4] = acc[4] + duc * xc
                d8 = w_b[3][:, cs] * duc
                for s in (1, 2, 3):
                    acc[4 - s] = acc[4 - s] + duc * _shift_down(xc, xp, s, row)
                    d8 = d8 + w_b[3 - s][:, cs] * _shift_up(duc, du_n, s, row)
                acc[7] = acc[7] + dyc * vc
                e8 = w_a[2][:, cs] * dyc
                for s in (1, 2):
                    acc[7 - s] = acc[7 - s] + dyc * _shift_down(vc, vp, s, row)
                    e8 = e8 + w_a[2 - s][:, cs] * _shift_up(dyc, dy_n, s, row)
                drx.append(d8)
                dv.append(e8)
                xp, vp = xc, vc
            for k in range(8):
                acc8[8 * k:8 * k + 8, cs] = acc[k]
            dv16 = jnp.concatenate(dv, axis=0)
            col = lambda s: slice(s * D + g * CG, s * D + (g + 1) * CG)
            dp_ref[rows, col(3)] = jnp.concatenate(drx, axis=0).astype(BF16)
            dp_ref[rows, col(1)] = (dv16 * _pj(proj_ref, 2, rows, cs).astype(F32)).astype(BF16)
            dp_ref[rows, col(2)] = (dv16 * _pj(proj_ref, 1, rows, cs).astype(F32)).astype(BF16)
        return carry

    lax.fori_loop(0, tt // 16, blk, 0)


def _mixer_bwd(proj, hl, dmg, prm, wa, wx):
    t_len = proj.shape[0]
    tt = min(TT, t_len)
    nt = t_len // tt
    hb8 = tt // 8

    def rev(i):
        return nt - 1 - i

    def halo(i):
        return jnp.maximum(rev(i) * hb8 - 1, 0)

    def body(proj_ref, ph_ref, hl_ref, hh_ref, dmg_ref, prm_ref, wa_ref, wx_ref,
             dp_ref, sums_ref, gwa_ref, gwx_ref,
             xe, ve, he, u_s, ya_s, rp_s, ip_s, due, dye, drp_s, dip_s, an, gn, ub_s, acc8):
        i = pl.program_id(0)
        t = rev(i)

        @pl.when(i == 0)
        def _():
            sums_ref[...] = jnp.zeros_like(sums_ref)
            gwa_ref[...] = jnp.zeros_like(gwa_ref)
            gwx_ref[...] = jnp.zeros_like(gwx_ref)
            due[tt:tt + 8, :] = jnp.zeros((8, D), F32)
            dye[tt:tt + 8, :] = jnp.zeros((8, D), F32)
            an[...] = jnp.zeros((8, D), F32)
            gn[...] = jnp.zeros((8, D), F32)

        live = (t > 0).astype(F32)
        xe[0:8, :] = _pj(ph_ref, 3).astype(F32) * live
        ve[0:8, :] = _pj(ph_ref, 1).astype(F32) * _pj(ph_ref, 2).astype(F32) * live
        he[0:8, :] = hh_ref[...].astype(F32) * live
        he[8:8 + tt, :] = hl_ref[...].astype(F32)
        _conv_fwd_rows(tt, proj_ref, prm_ref, xe, ve, u_s, ub_s, ya_s)
        ub = ub_s[...]
        for h in range(HEADS):
            cs = slice(h * HB, (h + 1) * HB)
            rp_s[:, cs] = _dot(ub[:, cs], wa_ref[h]) + prm_ref[P_BA:P_BA + 1, cs]
            ip_s[:, cs] = _dot(ub[:, cs], wx_ref[h]) + prm_ref[P_BX:P_BX + 1, cs]

        ls_all = _log_sigmoid(prm_ref[P_LAM:P_LAM + 1, :])
        row = lax.broadcasted_iota(jnp.int32, (8, CG), 0)
        nblk = tt // 16

        def blk(ib, carry):
            r0 = pl.multiple_of((nblk - 1 - ib) * 16, 16)
            rows = pl.ds(r0, 16)
            for g in range(D // CG):
                cs = slice(g * CG, (g + 1) * CG)
                ls = ls_all[:, cs]
                dm = dmg_ref[rows, cs].astype(F32)
                cb = _pj(proj_ref, 0, rows, cs).astype(F32)
                rg = _pj(proj_ref, 4, rows, cs).astype(F32)
                sga = _sig(_pj(proj_ref, 5, rows, cs).astype(F32))
                sgb = _sig(_pj(proj_ref, 6, rows, cs).astype(F32))
                ya0 = ya_s[rows, cs]
                h16 = he[pl.ds(r0 + 8, 16), cs]
                gl, th = _gelu(rg)
                dgl = 0.5 * (1.0 + th) + 0.5 * rg * (1.0 - th * th) * (_GC * (1.0 + 3.0 * 0.044715 * rg * rg))
                y_a = cb * ya0
                y_b = h16 * gl
                dy_a = dm * sga
                dy_b = dm * sgb
                col = lambda s: slice(s * D + g * CG, s * D + (g + 1) * CG)
                dp_ref[rows, col(5)] = (dm * y_a * sga * (1.0 - sga)).astype(BF16)
                dp_ref[rows, col(6)] = (dm * y_b * sgb * (1.0 - sgb)).astype(BF16)
                dp_ref[rows, col(4)] = (dy_b * h16 * dgl).astype(BF16)
                dp_ref[rows, col(0)] = (dy_a * ya0).astype(BF16)
                dye[rows, cs] = dy_a * cb
                dh16 = dy_b * gl

                a_next = an[:, cs]
                g_next = gn[:, cs]
                s_ba = jnp.zeros((8, CG), F32)
                s_bx = jnp.zeros((8, CG), F32)
                s_ls = jnp.zeros((8, CG), F32)
                for sb in (1, 0):
                    rr = r0 + 8 * sb
                    first = (row + (t * tt + rr)) == 0
                    uu = u_s[pl.ds(rr, 8), cs]
                    r, ig, la, a, m2, mult = _lru_gates(rp_s[pl.ds(rr, 8), cs], ip_s[pl.ds(rr, 8), cs], ls, first)
                    ca = jnp.where(row < 7, pltpu.roll(a, 7, 0), a_next)
                    cb_ = dh16[8 * sb:8 * sb + 8, :]
                    for s in (1, 2, 4):
                        a_sh = jnp.where(row < 8 - s, pltpu.roll(ca, 8 - s, 0), 1.0)
                        b_sh = jnp.where(row < 8 - s, pltpu.roll(cb_, 8 - s, 0), 0.0)
                        cb_ = ca * b_sh + cb_
                        ca = ca * a_sh
                    gv = ca * g_next + cb_
                    g_next = jnp.broadcast_to(gv[0:1, :], gv.shape)
                    a_next = jnp.broadcast_to(a[0:1, :], a.shape)
                    hprev = jnp.where(row >= 1, pltpu.roll(he[pl.ds(rr + 8, 8), cs], 1, 0),
                                      pltpu.roll(he[pl.ds(rr, 8), cs], 1, 0))
                    da = gv * hprev
                    dmult = jnp.where(first, 0.0, gv * ig * uu)
                    dla = da * a + jnp.where(m2 > 0.0, dmult * (-(a * a) / mult), 0.0)
                    drp = dla * (LRU_C * ls) * r * (1.0 - r)
                    dip = gv * mult * uu * ig * (1.0 - ig)
                    s_ls = s_ls + dla * (LRU_C * r)
                    s_ba = s_ba + drp
                    s_bx = s_bx + dip
                    drp_s[pl.ds(rr, 8), cs] = drp
                    dip_s[pl.ds(rr, 8), cs] = dip
                    due[pl.ds(rr, 8), cs] = gv * mult * ig
                an[:, cs] = a_next
                gn[:, cs] = g_next
                sums_ref[M_BA:M_BA + 1, cs] += jnp.sum(s_ba, axis=0, keepdims=True)
                sums_ref[M_BX:M_BX + 1, cs] += jnp.sum(s_bx, axis=0, keepdims=True)
                sums_ref[M_LS:M_LS + 1, cs] += jnp.sum(s_ls, axis=0, keepdims=True)
            return carry

        lax.fori_loop(0, nblk, blk, 0)

        drp_b = drp_s[...].astype(BF16)
        dip_b = dip_s[...].astype(BF16)
        for h in range(HEADS):
            cs = slice(h * HB, (h + 1) * HB)
            due[0:tt, cs] += _dot_nt(drp_b[:, cs], wa_ref[h]) + _dot_nt(dip_b[:, cs], wx_ref[h])
            gwa_ref[h] += _dot_tn(ub[:, cs], drp_b[:, cs])
            gwx_ref[h] += _dot_tn(ub[:, cs], dip_b[:, cs])

        acc8[...] = jnp.zeros_like(acc8)
        _conv_bwd_rows(tt, proj_ref, prm_ref, xe, ve, due, dye, dp_ref, acc8)
        for k, dst in enumerate([M_CBIAS] + [M_WB + k for k in range(4)] + [M_WA + k for k in range(3)]):
            sums_ref[dst:dst + 1, :] += jnp.sum(acc8[8 * k:8 * k + 8, :], axis=0, keepdims=True)
        due[tt:tt + 8, :] = due[0:8, :]
        dye[tt:tt + 8, :] = dye[0:8, :]

        @pl.when(i == nt - 1)
        def _():
            sums_ref[M_LS:M_LS + 1, :] = sums_ref[M_LS:M_LS + 1, :] * _sig(-prm_ref[P_LAM:P_LAM + 1, :])

    big = lambda: pltpu.VMEM((tt + 8, D), F32)
    tile = lambda: pltpu.VMEM((tt, D), F32)
    return pl.pallas_call(
        body, name="mixer_bwd", grid=(nt,),
        in_specs=[pl.BlockSpec((tt, 7 * D), lambda i: (rev(i), 0)),
                  pl.BlockSpec((8, 7 * D), lambda i: (halo(i), 0)),
                  pl.BlockSpec((tt, D), lambda i: (rev(i), 0)),
                  pl.BlockSpec((8, D), lambda i: (halo(i), 0)),
                  pl.BlockSpec((tt, D), lambda i: (rev(i), 0)),
                  pl.BlockSpec((16, D), lambda i: (0, 0)),
                  pl.BlockSpec((HEADS, HB, HB), lambda i: (0, 0, 0)),
                  pl.BlockSpec((HEADS, HB, HB), lambda i: (0, 0, 0))],
        out_specs=[pl.BlockSpec((tt, 7 * D), lambda i: (rev(i), 0)),
                   pl.BlockSpec((16, D), lambda i: (0, 0)),
                   pl.BlockSpec((HEADS, HB, HB), lambda i: (0, 0, 0)),
                   pl.BlockSpec((HEADS, HB, HB), lambda i: (0, 0, 0))],
        out_shape=[jax.ShapeDtypeStruct((t_len, 7 * D), BF16), jax.ShapeDtypeStruct((16, D), F32),
                   jax.ShapeDtypeStruct((HEADS, HB, HB), F32), jax.ShapeDtypeStruct((HEADS, HB, HB), F32)],
        scratch_shapes=[big(), big(), big(), tile(), tile(), tile(), tile(), big(), big(), tile(), tile(),
                        pltpu.VMEM((8, D), F32), pltpu.VMEM((8, D), F32), pltpu.VMEM((tt, D), BF16),
                        pltpu.VMEM((64, D), F32)],
        compiler_params=_cp("arbitrary"),
    )(proj, proj, hl, hl, dmg, prm, wa, wx)


def _in_proj_bwd(dproj, w_in, x, dx1, mod, g_mix):
    t_len = x.shape[0]
    tm = min(TM, t_len)

    def body(dp_ref, w_ref, x_ref, dx1_ref, mod_ref, g_ref, gx_ref, sums_ref, acc):
        @pl.when(pl.program_id(0) == 0)
        def _():
            sums_ref[...] = jnp.zeros_like(sums_ref)

        def write(rows, dx):
            gx_ref[rows, :] = dx

        zero = jnp.zeros((8, D), F32)
        sums = (zero, zero, zero)
        for sub in _sub_blocks(tm):
            acc[sub, :] = _dot_nt(dp_ref[sub, :], w_ref[...])
            sums = _norm_bwd_rows(sub, sums, acc, x_ref, dx1_ref, mod_ref[1:2, :], g_ref[...], write)
        _add_norm_sums(sums_ref, sums)

    return pl.pallas_call(
        body, name="in_proj_bwd", grid=(t_len // tm,),
        in_specs=[pl.BlockSpec((tm, 7 * D), lambda i: (i, 0)),
                  _resident((D, 7 * D)),
                  pl.BlockSpec((tm, D), lambda i: (i, 0)), pl.BlockSpec((tm, D), lambda i: (i, 0)),
                  pl.BlockSpec((8, D), lambda i: (0, 0)), pl.BlockSpec((1, D), lambda i: (0, 0))],
        out_specs=[pl.BlockSpec((tm, D), lambda i: (i, 0)), pl.BlockSpec((8, D), lambda i: (0, 0))],
        out_shape=[jax.ShapeDtypeStruct((t_len, D), F32), jax.ShapeDtypeStruct((8, D), F32)],
        scratch_shapes=[pltpu.VMEM((tm, D), F32)],
        compiler_params=_cp("arbitrary"),
    )(dproj, w_in, x, dx1, mod, g_mix)


def _in_wgrad(h, dproj, g_wa, g_wx, order):
    t_len = h.shape[0]
    tk = min(TKI, t_len)
    nk = t_len // tk
    cw = 7 * D // NDEV
    hr = HB // NDEV

    def body(ord_ref, h_ref, d_ref, ga_ref, gx_ref, parts_ref, pa_ref, px_ref, acc, *scr):
        rs, sems = scr[:-6], scr[-6:]
        p, k = pl.program_id(0), pl.program_id(1)

        def head_rows(ref):
            return lambda s: ref.at[:, pl.ds(s * hr, hr), :]

        @pl.when((p == 0) & (k == 0))
        def _():
            for s in range(NDEV):
                _rs_send(head_rows(ga_ref)(s), pa_ref, s, *sems[0:3])
                _rs_send(head_rows(gx_ref)(s), px_ref, s, *sems[3:6])

        @pl.when(k == 0)
        def _():
            acc[...] = jnp.zeros_like(acc)

        acc[...] += _dot_tn(h_ref[...], d_ref[...])

        @pl.when(k == nk - 1)
        def _():
            q = ord_ref[p]
            for half in range(2):
                rs[0][q, half] = acc[:, half * cw:(half + 1) * cw].astype(BF16)
            _rs2_to_sibling(q, rs)

        @pl.when((k == nk - 1) & (p > 0))
        def _():
            _rs2_forward(ord_ref[p - 1], parts_ref, rs)

        @pl.when((p == NCHIP - 1) & (k == nk - 1))
        def _():
            _rs2_forward(ord_ref[p], parts_ref, rs)
            _rs2_finish(parts_ref, rs)
            _rs_finish(head_rows(ga_ref), pa_ref, *sems[0:3])
            _rs_finish(head_rows(gx_ref), px_ref, *sems[3:6])

    return pl.pallas_call(
        body, name="in_wgrad",
        grid_spec=pltpu.PrefetchScalarGridSpec(
            num_scalar_prefetch=1, grid=(NCHIP, nk),
            in_specs=[pl.BlockSpec((tk, D), lambda p, k, o: (k, 0)),
                      pl.BlockSpec((tk, 2 * cw), lambda p, k, o: (k, o[p])), _ANY, _ANY],
            out_specs=[_ANY, _ANY, _ANY],
            scratch_shapes=[pltpu.VMEM((D, 2 * cw), F32)] + _rs2_scratch((D, cw)) + _RS_SEMS * 2),
        out_shape=[jax.ShapeDtypeStruct((NCHIP, D, cw), BF16), jax.ShapeDtypeStruct((NDEV, HEADS, hr, HB), F32),
                   jax.ShapeDtypeStruct((NDEV, HEADS, hr, HB), F32)],
        compiler_params=_cp("arbitrary", "arbitrary"),
    )(order, h, dproj, g_wa, g_wx)


def _ada_fwd(c_all, w_ada, b_cols):
    def body(c_ref, w_ref, b_ref, o_ref):
        cv = c_ref[...]
        o_ref[...] = _dot((cv * _sig(cv)).astype(BF16), w_ref[...].astype(BF16)) + b_ref[...]

    return pl.pallas_call(body, name="ada_fwd", out_shape=jax.ShapeDtypeStruct((16, w_ada.shape[1]), F32),
                          compiler_params=_cp())(c_all, w_ada, b_cols)


def _adam_math(w, g, m, v):
    m = ADAM_B1 * m + (1.0 - ADAM_B1) * g
    v = ADAM_B2 * v + (1.0 - ADAM_B2) * (g * g)
    m_hat = m / (1.0 - ADAM_B1 ** ADAM_STEP)
    v_hat = v / (1.0 - ADAM_B2 ** ADAM_STEP)
    delta = -ADAM_LR * (m_hat / (jnp.sqrt(v_hat) + ADAM_EPS) + ADAM_WD * w)
    return delta, m, v


def _ada_bwd(c_all, dmod_cols, w, m, v):
    rb = 256
    n = w.shape[1]
    nrow = c_all.shape[0]

    def body(c_ref, d_ref, w_ref, m_ref, v_ref, g_ref, dl_ref, nm_ref, nv_ref):
        cv = c_ref[...]
        g = _dot_tn((cv * _sig(cv)).astype(BF16), d_ref[...].astype(BF16))
        g_ref[...] = g
        dl_ref[...], nm_ref[...], nv_ref[...] = _adam_math(w_ref[...], g, m_ref[...], v_ref[...])

    blk = pl.BlockSpec((rb, n), lambda i: (i, 0))
    sds = jax.ShapeDtypeStruct(w.shape, F32)
    return pl.pallas_call(
        body, name="ada_bwd", grid=(D // rb,),
        in_specs=[pl.BlockSpec((nrow, rb), lambda i: (0, i)), pl.BlockSpec((nrow, n), lambda i: (0, 0)), blk, blk, blk],
        out_specs=[blk, blk, blk, blk], out_shape=[sds, sds, sds, sds],
        compiler_params=_cp("parallel"),
    )(c_all, dmod_cols, w, m, v)


def _adam(name, parts, w, m, v):
    p, r, c = parts.shape
    rb = r
    for cand in (256, 128, 64, 32, 16, 8):
        if r % cand == 0 and r >= cand:
            rb = cand
            break

    def body(p_ref, w_ref, m_ref, v_ref, g_ref, dl_ref, nm_ref, nv_ref):
        g = p_ref[0].astype(F32)
        for q in range(1, p):
            g = g + p_ref[q].astype(F32)
        g_ref[...] = g
        dl_ref[...], nm_ref[...], nv_ref[...] = _adam_math(w_ref[...], g, m_ref[...], v_ref[...])

    blk = pl.BlockSpec((rb, c), lambda i: (i, 0))
    sds = jax.ShapeDtypeStruct((r, c), F32)
    return pl.pallas_call(
        body, name=name, grid=(r // rb,),
        in_specs=[pl.BlockSpec((p, rb, c), lambda i: (0, i, 0)), blk, blk, blk],
        out_specs=[blk, blk, blk, blk], out_shape=[sds, sds, sds, sds],
        compiler_params=_cp("parallel"),
    )(parts, w, m, v)


def _my_pos():
    return lax.axis_index("x"), lax.axis_index("y"), lax.axis_index("c")


def _all_gather_small(name, v):
    m_per, n = v.shape

    def body(x_ref, out_ref, send_sems, recv_sems, local_sem):
        x, y, c = _my_pos()
        me, sibling = (x, y, c), (x, y, 1 - c)
        chips = [(1 - x, y), (x, 1 - y), (1 - x, 1 - y)]

        def rows(px, py, pc):
            return out_ref.at[pl.ds((4 * px + 2 * py + pc) * m_per, m_per), :]

        def copy(k, block, to, src=None):
            return pltpu.make_async_remote_copy(
                src_ref=rows(*block) if src is None else src, dst_ref=rows(*block),
                send_sem=send_sems.at[k], recv_sem=recv_sems.at[k], device_id=to, device_id_type=MESH)

        mine = pltpu.make_async_copy(x_ref, rows(*me), local_sem)
        mine.start()
        first = [copy(0, me, sibling, src=x_ref)]
        first += [copy(1 + j, me, (*chip, c), src=x_ref) for j, chip in enumerate(chips)]
        for cp in first:
            cp.start()
        passed = [copy(4 + j, (*chip, c), sibling) for j, chip in enumerate(chips)]
        for j, chip in enumerate(chips):
            copy(1 + j, (*chip, c), me).wait_recv()
            passed[j].start()
        copy(0, sibling, me).wait_recv()
        for j, chip in enumerate(chips):
            copy(4 + j, (*chip, 1 - c), me).wait_recv()
        for cp in first + passed:
            cp.wait_send()
        mine.wait()

    return pl.pallas_call(
        body, name=name, out_shape=jax.ShapeDtypeStruct((NDEV * m_per, n), v.dtype),
        in_specs=[pl.BlockSpec(memory_space=pltpu.VMEM)], out_specs=pl.BlockSpec(memory_space=pltpu.VMEM),
        scratch_shapes=[pltpu.SemaphoreType.DMA((7,)), pltpu.SemaphoreType.DMA((7,)), pltpu.SemaphoreType.DMA],
    )(v)


def _blk_cols(n):
    return lambda ref, b: ref.at[:, pl.ds(pl.multiple_of(b * n, 128), n)]


def _blk_rows(n):
    return lambda ref, b: ref.at[pl.ds(pl.multiple_of(b * n, 8), n), :]


def _blk_lead(ref, b):
    return ref.at[b]


def _blk_heads(ref, b):
    return ref.at[:, pl.ds(pl.multiple_of(b * (HB // NDEV), 8), HB // NDEV), :]


def _ag_phases(ins, outs, slicers, send_sems, recv_sems, local_sems):
    na = len(ins)
    x, y, c = _my_pos()
    me, sibling = (x, y, c), (x, y, 1 - c)
    chips = [(1 - x, y), (x, 1 - y), (1 - x, 1 - y)]

    def copy(a, k, block, to, from_shard=False):
        px, py, pc = block
        dst = slicers[a](outs[a], 4 * px + 2 * py + pc)
        return pltpu.make_async_remote_copy(
            src_ref=ins[a] if from_shard else dst, dst_ref=dst,
            send_sem=send_sems.at[a * 7 + k], recv_sem=recv_sems.at[a * 7 + k], device_id=to, device_id_type=MESH)

    def local(a):
        return pltpu.make_async_copy(ins[a], slicers[a](outs[a], 4 * x + 2 * y + c), local_sems.at[a])

    def firsts(a):
        return [copy(a, 0, me, sibling, True)] + [copy(a, 1 + j, me, (*chip, c), True) for j, chip in enumerate(chips)]

    def start():
        for a in range(na):
            local(a).start()
            for cp in firsts(a):
                cp.start()

    def forward():
        for a in range(na):
            for j, chip in enumerate(chips):
                copy(a, 1 + j, (*chip, c), me).wait_recv()
                copy(a, 4 + j, (*chip, c), sibling).start()

    def finish():
        for a in range(na):
            copy(a, 0, sibling, me).wait_recv()
            for j, chip in enumerate(chips):
                copy(a, 4 + j, (*chip, 1 - c), me).wait_recv()
        for a in range(na):
            for cp in firsts(a) + [copy(a, 4 + j, (*chip, c), sibling) for j, chip in enumerate(chips)]:
                cp.wait_send()
            local(a).wait()

    return start, forward, finish


def _ag_sems(na):
    return [pltpu.SemaphoreType.DMA((7 * na,)), pltpu.SemaphoreType.DMA((7 * na,)), pltpu.SemaphoreType.DMA((na,))]


def _all_gather_weights(shards, fulls, slicers):
    na = len(shards)

    def body(*refs):
        start, forward, finish = _ag_phases(refs[:na], refs[na:2 * na], slicers, *refs[2 * na:])
        start()
        forward()
        finish()

    return pl.pallas_call(
        body, name="gather_weights",
        out_shape=[jax.ShapeDtypeStruct(s, sh.dtype) for s, sh in zip(fulls, shards)],
        in_specs=[_ANY] * na, out_specs=[_ANY] * na, scratch_shapes=_ag_sems(na),
    )(*shards)


def _scatter_grads(grads, shard_shapes, slicers):
    na = len(grads)

    def body(*refs):
        ins, outs = refs[:na], refs[na:2 * na]
        send_sems, recv_sems, local_sems = refs[2 * na:]
        x, y, c = _my_pos()
        me = 4 * x + 2 * y + c
        mine, sent = [], []
        for a in range(na):
            cp = pltpu.make_async_copy(slicers[a](ins[a], me), outs[a].at[me], local_sems.at[a])
            cp.start()
            mine.append(cp)
        rel = [(k >> 2 & 1, k >> 1 & 1, k & 1) for k in range(1, NDEV)]
        for a in range(na):
            for k, (fx, fy, fc) in enumerate(rel):
                px, py, pc = x ^ fx, y ^ fy, c ^ fc
                cp = pltpu.make_async_remote_copy(
                    src_ref=slicers[a](ins[a], 4 * px + 2 * py + pc), dst_ref=outs[a].at[me],
                    send_sem=send_sems.at[a * 7 + k], recv_sem=recv_sems.at[a * 7 + k],
                    device_id=(px, py, pc), device_id_type=MESH)
                cp.start()
                sent.append(cp)
        for a in range(na):
            for k, (fx, fy, fc) in enumerate(rel):
                px, py, pc = x ^ fx, y ^ fy, c ^ fc
                src = 4 * px + 2 * py + pc
                pltpu.make_async_remote_copy(
                    src_ref=slicers[a](ins[a], me), dst_ref=outs[a].at[src],
                    send_sem=send_sems.at[a * 7 + k], recv_sem=recv_sems.at[a * 7 + k],
                    device_id=(px, py, pc), device_id_type=MESH).wait_recv()
        for cp in sent:
            cp.wait_send()
        for cp in mine:
            cp.wait()

    any_spec = pl.BlockSpec(memory_space=pl.ANY)
    return pl.pallas_call(
        body, name="scatter_grads",
        out_shape=[jax.ShapeDtypeStruct((NDEV,) + tuple(s), g.dtype) for s, g in zip(shard_shapes, grads)],
        in_specs=[any_spec] * na, out_specs=[any_spec] * na,
        scratch_shapes=[pltpu.SemaphoreType.DMA((7 * na,)), pltpu.SemaphoreType.DMA((7 * na,)),
                        pltpu.SemaphoreType.DMA((na,))],
    )(*grads)


def _local_step(x, target, mod, g_mix, g_ffn, g_fin, prm, w_in_shard, shards):
    fulls = [(HEADS, HB, HB), (HEADS, HB, HB), (D, D), (NDEV, FB, D), (DFF, D)]
    slicers = [_blk_heads, _blk_heads, _blk_rows(D // NDEV), _blk_lead, _blk_rows(DFF // NDEV)]
    my_chip = _my_index() >> 1
    own_first = (my_chip ^ jnp.arange(NCHIP, dtype=jnp.int32)).astype(jnp.int32)
    early, late = [0, 1, 2, 4], [3]
    pick = lambda lst, idx: [lst[i] for i in idx]
    proj, h, w_in, (wa, wx, w_out, w_down) = _in_proj(x, mod, g_mix, w_in_shard, own_first, pick(shards, early),
                                                      pick(fulls, early), pick(slicers, early))
    merged, hl, (w_gu,) = _mixer_fwd(proj, prm, wa, wx, pick(shards, late), pick(fulls, late), pick(slicers, late))
    w_gu = w_gu.reshape(2, 4, FB, D)
    x1, h2 = _out_proj(merged, x, mod, g_ffn, w_out)
    gu, dx2, dx2b, loss, d_gfin = _ffn_fwd(h2, x1, target, mod, g_fin, w_gu, w_down)
    dgu, act, dx1, dx1b, dmg, sums2 = _ffn_bwd(dx2, gu, x1, mod, g_ffn, w_gu, w_down, w_out)
    chip_order = _xor_order(_my_index() >> 1, NCHIP)
    p_wgu = _gu_wgrad(h2, dgu, chip_order)
    p_wdown, d_gt2 = _scaled_wgrad("down_wgrad", act, dx2b, w_down, 5, mod, chip_order)
    p_wout, d_gt1 = _scaled_wgrad("out_wgrad", merged.reshape(1, *merged.shape), dx1b, w_out, 2, mod,
                                  jnp.zeros((1,), jnp.int32))
    dproj, msums, g_wa, g_wx = _mixer_bwd(proj, hl, dmg, prm, wa, wx)
    p_win, p_wa, p_wx = _in_wgrad(h, dproj, g_wa, g_wx, chip_order)
    grad_x, sums1 = _in_proj_bwd(dproj, w_in, x, dx1, mod, g_mix)
    return dict(loss=loss, grad_x=grad_x, d_gfin=d_gfin, sums1=sums1, sums2=sums2, msums=msums,
                d_gt1=d_gt1[0:1], d_gt2=d_gt2[0:1], p_win=p_win, p_wa=p_wa, p_wx=p_wx, p_wout=p_wout, p_wgu=p_wgu,
                p_wdown=p_wdown)


def kernel(x, c, w_ada, b_ada, g_norm_mix, w_in, conv_a_w, conv_b_w, conv_b_bias, w_rg_a, b_rg_a, w_rg_x, b_rg_x, lru_lambda, w_out, g_norm_ffn, w_gate_up, w_down, g_norm_final, loss_target, m_w_ada, m_b_ada, m_g_norm_mix, m_w_in, m_conv_a_w, m_conv_b_w, m_conv_b_bias, m_w_rg_a, m_b_rg_a, m_w_rg_x, m_b_rg_x, m_lru_lambda, m_w_out, m_g_norm_ffn, m_w_gate_up, m_w_down, m_g_norm_final, v_w_ada, v_b_ada, v_g_norm_mix, v_w_in, v_conv_a_w, v_conv_b_w, v_conv_b_bias, v_w_rg_a, v_b_rg_a, v_w_rg_x, v_b_rg_x, v_lru_lambda, v_w_out, v_g_norm_ffn, v_w_gate_up, v_w_down, v_g_norm_final):
    me = 4 * lax.axis_index("x") + 2 * lax.axis_index("y") + lax.axis_index("c")
    ncol = w_ada.shape[2]
    cw = conv_a_w.shape[2]

    pack0 = jnp.concatenate([c, conv_a_w.reshape(1, 3 * cw), conv_b_w.reshape(1, 4 * cw)], axis=1)
    got0 = _all_gather_small("gather_c", jnp.broadcast_to(pack0, (8, pack0.shape[1])))
    got0 = got0.reshape(NDEV, 8, -1)[:, 0, :]
    c_all = got0[:, :D]
    conv_a = got0[:, D:D + 3 * cw].reshape(NDEV, 3, cw).transpose(1, 0, 2).reshape(3, D)
    conv_b = got0[:, D + 3 * cw:].reshape(NDEV, 4, cw).transpose(1, 0, 2).reshape(4, D)

    b_cols = lax.dynamic_slice_in_dim(b_ada, me * ncol, ncol, axis=1)
    c16 = jnp.concatenate([c_all, jnp.zeros((8, D), F32)], axis=0)
    mod_cols = _ada_fwd(c16, w_ada[0], b_cols)[:NDEV]
    got1 = _all_gather_small("gather_mod", mod_cols).reshape(NDEV, NDEV, ncol)
    mod6 = lax.dynamic_index_in_dim(got1, me, axis=1, keepdims=False).reshape(6, D)
    mod = jnp.concatenate([mod6, jnp.zeros((2, D), F32)], axis=0)

    tr = lambda a: jnp.swapaxes(a, 1, 2)
    shards = [w_rg_a[0].astype(BF16), w_rg_x[0].astype(BF16), w_out[0].astype(BF16), tr(w_gate_up)[0].astype(BF16),
              w_down[0].astype(BF16)]

    prm = jnp.concatenate([conv_a, conv_b, conv_b_bias, b_rg_a, b_rg_x, lru_lambda, jnp.zeros((5, D), F32)], axis=0)
    r = _local_step(x[0], loss_target[0], mod, g_norm_mix, g_norm_ffn, g_norm_final.reshape(1, D), prm,
                    w_in[0].astype(BF16), shards)

    parts = [r["p_win"], r["p_wa"], r["p_wx"], r["p_wout"], r["p_wgu"], r["p_wdown"]]
    big = {}
    for nm, p, w, m, v in (("w_in", parts[0], w_in, m_w_in, v_w_in), ("w_rg_a", parts[1], w_rg_a, m_w_rg_a, v_w_rg_a),
                           ("w_rg_x", parts[2], w_rg_x, m_w_rg_x, v_w_rg_x), ("w_out", parts[3], w_out, m_w_out, v_w_out),
                           ("w_gate_up", parts[4], tr(w_gate_up), tr(m_w_gate_up), tr(v_w_gate_up)),
                           ("w_down", parts[5], w_down, m_w_down, v_w_down)):
        two_d = (-1, w.shape[-1])
        outs = _adam("adam_" + nm, p.reshape((p.shape[0],) + w.reshape(two_d).shape), w.reshape(two_d), m.reshape(two_d),
                     v.reshape(two_d))
        big[nm] = [o.reshape(w.shape) for o in outs]
    big["w_gate_up"] = [tr(o) for o in big["w_gate_up"]]

    small = jnp.concatenate([
        r["sums1"][S_SH:S_SH + 1], r["sums1"][S_SC:S_SC + 1], r["d_gt1"],
        r["sums2"][S_SH:S_SH + 1], r["sums2"][S_SC:S_SC + 1], r["d_gt2"],
        r["sums1"][S_G:S_G + 1],
        r["msums"][M_CBIAS:M_CBIAS + 1], r["msums"][M_BA:M_BA + 1], r["msums"][M_BX:M_BX + 1],
        r["msums"][M_LS:M_LS + 1],
        r["sums2"][S_G:S_G + 1], r["d_gfin"],
        r["msums"][M_WA:M_WA + 3], r["msums"][M_WB:M_WB + 4],
        jnp.broadcast_to(r["loss"][0:1, 0:1], (1, D)),
        jnp.zeros((3, D), F32)], axis=0)
    got2 = _all_gather_small("gather_small", small).reshape(NDEV, 24, D)

    rep_w = jnp.concatenate([b_ada.reshape(6, D), g_norm_mix, conv_b_bias, b_rg_a, b_rg_x, lru_lambda, g_norm_ffn,
                             g_norm_final.reshape(1, D), jnp.zeros((3, D), F32)], axis=0)
    rep_m = jnp.concatenate([m_b_ada.reshape(6, D), m_g_norm_mix, m_conv_b_bias, m_b_rg_a, m_b_rg_x, m_lru_lambda,
                             m_g_norm_ffn, m_g_norm_final.reshape(1, D), jnp.zeros((3, D), F32)], axis=0)
    rep_v = jnp.concatenate([v_b_ada.reshape(6, D), v_g_norm_mix, v_conv_b_bias, v_b_rg_a, v_b_rg_x, v_lru_lambda,
                             v_g_norm_ffn, v_g_norm_final.reshape(1, D), jnp.ones((3, D), F32)], axis=0)
    rep = _adam("adam_rep", got2[:, :16, :], rep_w, rep_m, rep_v)

    conv_parts = lax.dynamic_slice_in_dim(got2[:, 13:21, :], me * cw, cw, axis=2)
    cv_w = jnp.concatenate([conv_a_w[0], conv_b_w[0], jnp.zeros((1, cw), F32)], axis=0)
    cv_m = jnp.concatenate([m_conv_a_w[0], m_conv_b_w[0], jnp.zeros((1, cw), F32)], axis=0)
    cv_v = jnp.concatenate([v_conv_a_w[0], v_conv_b_w[0], jnp.ones((1, cw), F32)], axis=0)
    cvo = _adam("adam_conv", conv_parts, cv_w, cv_m, cv_v)

    dmod_cols = lax.dynamic_slice_in_dim(got2[:, :6, :].reshape(NDEV, 6 * D), me * ncol, ncol, axis=1)
    dmod16 = jnp.concatenate([dmod_cols, jnp.zeros((8, ncol), F32)], axis=0)
    ada = _ada_bwd(c16, dmod16, w_ada[0], m_w_ada[0], v_w_ada[0])

    loss = jnp.sum(got2[:, 20, 0])

    def pick(q):
        one = lambda i: rep[q][i:i + 1]
        return [ada[q].reshape(w_ada.shape), rep[q][0:6].reshape(b_ada.shape), one(6), big["w_in"][q],
                cvo[q][0:3].reshape(conv_a_w.shape), cvo[q][3:7].reshape(conv_b_w.shape), one(7),
                big["w_rg_a"][q], one(8), big["w_rg_x"][q], one(9), one(10), big["w_out"][q], one(11),
                big["w_gate_up"][q], big["w_down"][q], rep[q][12]]

    return (loss, r["grad_x"].reshape(x.shape), *pick(0), *pick(1), *pick(2), *pick(3))
```

```python
import functools
import math

import jax
import jax.numpy as jnp
from jax import lax
from jax.experimental import pallas as pl
from jax.experimental.pallas import tpu as pltpu

F32 = jnp.float32
BF16 = jnp.bfloat16

D = 1024
DFF = 2816
NDEV = 8
HEADS = 4
HB = D // HEADS
FB = DFF // 4
EPS = 1e-6
LRU_C = 8.0
ADAM_LR, ADAM_B1, ADAM_B2, ADAM_EPS, ADAM_WD, ADAM_STEP = 0.001, 0.9, 0.999, 1e-08, 0.01, 10

VMEM_LIMIT = 56 * 1024 * 1024
TM = 512
TMI = 1024
TMF = 256
TK = 2048
TKI = 2048
SUB = 256
TT = 256
CG = 256
MESH = pl.DeviceIdType.MESH
AXES = ("x", "y", "c")


def _cp(*sem):
    return pltpu.CompilerParams(dimension_semantics=sem, vmem_limit_bytes=VMEM_LIMIT)


def _sig(x):
    return 1.0 / (1.0 + jnp.exp(-x))


def _log_sigmoid(x):
    z = jnp.exp(-jnp.abs(x))
    u = 1.0 + z
    d = u - 1.0
    l1p = jnp.where(d == 0.0, z, jnp.log(u) * (z / jnp.where(d == 0.0, 1.0, d)))
    return -(jnp.maximum(-x, 0.0) + l1p)


def _neg_expm1(x):
    p = x * (1.0 + x * 0.5 * (1.0 + x * (1.0 / 3.0) * (1.0 + x * 0.25 * (1.0 + x * 0.2 * (1.0 + x * (1.0 / 6.0))))))
    return jnp.where(x > -0.25, -p, 1.0 - jnp.exp(x))


_GC = math.sqrt(2.0 / math.pi)


def _gelu(x):
    t = jnp.tanh(_GC * (x + 0.044715 * x * x * x))
    return 0.5 * x * (1.0 + t), t


def _dot(a, b):
    return jnp.dot(a, b, preferred_element_type=F32)


def _dot_nt(a, b):
    return lax.dot_general(a, b, (((1,), (1,)), ((), ())), preferred_element_type=F32)


def _dot_tn(a, b):
    return lax.dot_general(a, b, (((0,), (0,)), ((), ())), preferred_element_type=F32)


def _resident(shape):
    return pl.BlockSpec(shape, lambda *_: (0,) * len(shape), pipeline_mode=pl.Buffered(1))


def _sub_blocks(n_rows):
    step = min(SUB, n_rows)
    return [slice(r, r + step) for r in range(0, n_rows, step)]


def _fold8(v):
    return v[0:8] + v[8:16]


def _pj(ref, s, rows=slice(None), cols=slice(0, D)):
    return ref[rows, s * D + cols.start:s * D + cols.stop]


def _in_proj(x, mod, g_mix, w_shard, order, shards, fulls, slicers):
    t_len = x.shape[0]
    tm = min(TMI, t_len)
    ni = t_len // tm
    na = len(shards)
    cw = 7 * D // NDEV
    rc = 32

    def body(ord_ref, x_ref, mod_ref, g_ref, wsh_ref, *rest):
        ins, (proj_ref, h_ref, wfull_ref), outs = rest[:na], rest[na:na + 3], rest[na + 3:2 * na + 3]
        h_scr, w_scr, wsend, wrecv, wlocal, wout = rest[2 * na + 3:2 * na + 9]
        start, forward, finish = _ag_phases(ins, outs, slicers, *rest[2 * na + 9:])
        p, i = pl.program_id(0), pl.program_id(1)
        x_, y_, c = _my_pos()
        me, sibling = (x_, y_, c), (x_, y_, 1 - c)
        chip_at = [None, (x_, 1 - y_), (1 - x_, y_), (1 - x_, 1 - y_)]

        def cols(px, py, pc):
            return w_scr.at[:, pl.ds(pl.multiple_of((4 * px + 2 * py + pc) * cw, 128), cw)]

        def wcopy(k, block, to, from_shard=False):
            dst = cols(*block)
            return pltpu.make_async_remote_copy(src_ref=wsh_ref if from_shard else dst, dst_ref=dst,
                                                send_sem=wsend.at[k], recv_sem=wrecv.at[k], device_id=to,
                                                device_id_type=MESH)

        own_local = pltpu.make_async_copy(wsh_ref, cols(*me), wlocal)
        to_hbm = pltpu.make_async_copy(w_scr, wfull_ref, wout)

        @pl.when((p == 0) & (i == 0))
        def _():
            own_local.start()
            wcopy(0, me, sibling, True).start()
            for q in (1, 2):
                wcopy(q, me, (*chip_at[q], c), True).start()
            own_local.wait()
            wcopy(0, sibling, me).wait_recv()

        @pl.when((p == 0) & (i == ni // 2))
        def _():
            wcopy(3, me, (*chip_at[3], c), True).start()

        for q in (1, 2, 3):
            @pl.when((p == q - 1) & (i == ni - 1))
            def _():
                wcopy(q, (*chip_at[q], c), me).wait_recv()
                wcopy(3 + q, (*chip_at[q], c), sibling).start()

            @pl.when((p == q) & (i == 0))
            def _():
                wcopy(3 + q, (*chip_at[q], 1 - c), me).wait_recv()

        @pl.when((p == 1) & (i == 0))
        def _():
            start()

        @pl.when((p == NCHIP - 1) & (i == ni // 2))
        def _():
            forward()

        @pl.when((p == NCHIP - 1) & (i == 0))
        def _():
            to_hbm.start()

        gs = g_ref[...] * (1.0 + mod_ref[1:2, :])
        sh = mod_ref[0:1, :]

        wcols = pl.ds(pl.multiple_of(ord_ref[p] * (2 * cw), 128), 2 * cw)
        for sub in _sub_blocks(tm):
            for r0 in range(sub.start, sub.stop, rc):
                xv = x_ref[r0:r0 + rc, :]
                r = lax.rsqrt(jnp.mean(xv * xv, axis=-1, keepdims=True) + EPS)
                h_scr[r0:r0 + rc, :] = (xv * r * gs + sh).astype(BF16)
            proj_ref[sub, :] = _dot(h_scr[sub, :], w_scr[:, wcols]).astype(BF16)

        @pl.when(p == 0)
        def _():
            h_ref[...] = h_scr[...]

        @pl.when((p == NCHIP - 1) & (i == ni - 1))
        def _():
            wcopy(0, me, sibling, True).wait_send()
            for q in (1, 2, 3):
                wcopy(q, me, (*chip_at[q], c), True).wait_send()
                wcopy(3 + q, (*chip_at[q], c), sibling).wait_send()
            finish()
            to_hbm.wait()

    res = pl.pallas_call(
        body, name="in_proj",
        grid_spec=pltpu.PrefetchScalarGridSpec(
            num_scalar_prefetch=1, grid=(NCHIP, ni),
            in_specs=[pl.BlockSpec((tm, D), lambda p, i, o: (i, 0)),
                      pl.BlockSpec((8, D), lambda p, i, o: (0, 0)),
                      pl.BlockSpec((1, D), lambda p, i, o: (0, 0))] + [_ANY] * (1 + na),
            out_specs=[pl.BlockSpec((tm, 2 * cw), lambda p, i, o: (i, o[p])),
                       pl.BlockSpec((tm, D), lambda p, i, o: (jnp.where(p == 0, i, ni - 1), 0))]
            + [_ANY] * (1 + na),
            scratch_shapes=[pltpu.VMEM((tm, D), BF16), pltpu.VMEM((D, 7 * D), BF16),
                            pltpu.SemaphoreType.DMA((7,)), pltpu.SemaphoreType.DMA((7,)),
                            pltpu.SemaphoreType.DMA, pltpu.SemaphoreType.DMA] + _ag_sems(na)),
        out_shape=[jax.ShapeDtypeStruct((t_len, 7 * D), BF16), jax.ShapeDtypeStruct((t_len, D), BF16),
                   jax.ShapeDtypeStruct((D, 7 * D), BF16)]
        + [jax.ShapeDtypeStruct(f, sh.dtype) for f, sh in zip(fulls, shards)],
        compiler_params=_cp("arbitrary", "arbitrary"),
    )(order, x, mod, g_mix, w_shard, *shards)
    return res[0], res[1], res[2], res[3:]


P_WA, P_WB, P_CBIAS, P_BA, P_BX, P_LAM = 0, 3, 7, 8, 9, 10
SV_PLANES = SV_U, SV_YA, SV_R, SV_I, SV_A, SV_MULT = range(6)


def _lru_gates(rp, ip, ls, first_row):
    r = _sig(rp)
    ig = _sig(ip)
    la = LRU_C * r * ls
    a = jnp.exp(la)
    m2 = _neg_expm1(2.0 * la)
    mult = jnp.where(first_row, 1.0, jnp.sqrt(jnp.maximum(m2, 0.0)))
    return r, ig, la, a, m2, mult


def _shift_down(cur, prev, s, row):
    return jnp.where(row >= s, pltpu.roll(cur, s, 0), pltpu.roll(prev, s, 0))


def _shift_up(cur, nxt, s, row):
    return jnp.where(row < 8 - s, pltpu.roll(cur, 8 - s, 0), pltpu.roll(nxt, 8 - s, 0))


def _conv_fwd_rows(tt, proj_ref, prm_ref, xe, ve, u_s, ub_s, ya_s):
    row = lax.broadcasted_iota(jnp.int32, (8, CG), 0)
    w_b = [prm_ref[P_WB + k:P_WB + k + 1, :] for k in range(4)]
    w_a = [prm_ref[P_WA + k:P_WA + k + 1, :] for k in range(3)]
    bias = prm_ref[P_CBIAS:P_CBIAS + 1, :]

    def blk(ib, carry):
        r0 = pl.multiple_of(ib * 16, 16)
        rows = pl.ds(r0, 16)
        for g in range(D // CG):
            cs = slice(g * CG, (g + 1) * CG)
            x16 = _pj(proj_ref, 3, rows, cs).astype(F32)
            v16 = _pj(proj_ref, 1, rows, cs).astype(F32) * _pj(proj_ref, 2, rows, cs).astype(F32)
            xp = xe[pl.ds(r0, 8), cs]
            vp = ve[pl.ds(r0, 8), cs]
            xe[pl.ds(r0 + 8, 16), cs] = x16
            ve[pl.ds(r0 + 8, 16), cs] = v16
            us, yas = [], []
            for sb in range(2):
                xc, vc = x16[8 * sb:8 * sb + 8], v16[8 * sb:8 * sb + 8]
                u8 = bias[:, cs] + w_b[3][:, cs] * xc
                for s in (1, 2, 3):
                    u8 = u8 + w_b[3 - s][:, cs] * _shift_down(xc, xp, s, row)
                y8 = w_a[2][:, cs] * vc
                for s in (1, 2):
                    y8 = y8 + w_a[2 - s][:, cs] * _shift_down(vc, vp, s, row)
                us.append(u8)
                yas.append(y8)
                xp, vp = xc, vc
            u16 = jnp.concatenate(us, axis=0)
            u_s[rows, cs] = u16
            ub_s[rows, cs] = u16.astype(BF16)
            ya_s[rows, cs] = jnp.concatenate(yas, axis=0)
        return carry

    lax.fori_loop(0, tt // 16, blk, 0)


def _mixer_fwd(proj, prm, wa, wx, shards, fulls, slicers):
    t_len = proj.shape[0]
    tt = min(TT, t_len)
    nt = t_len // tt
    na = len(shards)

    def body(proj_ref, prm_ref, wa_ref, wx_ref, *rest):
        ins, (mg_ref, hl_ref, sv_ref), outs = rest[:na], rest[na:na + 3], rest[na + 3:2 * na + 3]
        xe, ve, hc, rp_s, ip_s, ub_s = rest[2 * na + 3:2 * na + 9]
        start, forward, finish = _ag_phases(ins, outs, slicers, *rest[2 * na + 9:])
        t = pl.program_id(0)

        @pl.when(t == 0)
        def _():
            start()
            xe[0:8, :] = jnp.zeros((8, D), F32)
            ve[0:8, :] = jnp.zeros((8, D), F32)
            hc[...] = jnp.zeros((8, D), F32)

        @pl.when(t == (3 * nt) // 4)
        def _():
            forward()

        _conv_fwd_rows(tt, proj_ref, prm_ref, xe, ve, sv_ref.at[SV_U], ub_s, sv_ref.at[SV_YA])
        xe[0:8, :] = xe[tt:tt + 8, :]
        ve[0:8, :] = ve[tt:tt + 8, :]

        ub = ub_s[...]
        for h in range(HEADS):
            cs = slice(h * HB, (h + 1) * HB)
            rp_s[:, cs] = _dot(ub[:, cs], wa_ref[h]) + prm_ref[P_BA:P_BA + 1, cs]
            ip_s[:, cs] = _dot(ub[:, cs], wx_ref[h]) + prm_ref[P_BX:P_BX + 1, cs]

        ls_all = _log_sigmoid(prm_ref[P_LAM:P_LAM + 1, :])
        row = lax.broadcasted_iota(jnp.int32, (8, CG), 0)

        def blk(i, carry):
            r0 = pl.multiple_of(i * 16, 16)
            for g in range(D // CG):
                cs = slice(g * CG, (g + 1) * CG)
                ls = ls_all[:, cs]
                hprev = hc[:, cs]
                hs = []
                for sb in range(2):
                    rr = r0 + 8 * sb
                    first = (row + (t * tt + rr)) == 0
                    r8 = pl.ds(rr, 8)
                    r, ig, _, a, _, mult = _lru_gates(rp_s[r8, cs], ip_s[r8, cs], ls, first)
                    for plane, val in ((SV_R, r), (SV_I, ig), (SV_A, a), (SV_MULT, mult)):
                        sv_ref[plane, r8, cs] = val
                    b = mult * (ig * sv_ref[SV_U, r8, cs])
                    for s in (1, 2, 4):
                        a_sh = jnp.where(row >= s, pltpu.roll(a, s, 0), 1.0)
                        b_sh = jnp.where(row >= s, pltpu.roll(b, s, 0), 0.0)
                        b = a * b_sh + b
                        a = a * a_sh
                    hv = a * hprev + b
                    hprev = jnp.broadcast_to(hv[7:8, :], hv.shape)
                    hs.append(hv)
                hc[:, cs] = hprev
                h16 = jnp.concatenate(hs, axis=0)
                rows = pl.ds(r0, 16)
                gl, _ = _gelu(_pj(proj_ref, 4, rows, cs).astype(F32))
                y_b = h16 * gl
                y_a = _pj(proj_ref, 0, rows, cs).astype(F32) * sv_ref[SV_YA, rows, cs]
                mg = (_sig(_pj(proj_ref, 5, rows, cs).astype(F32)) * y_a
                      + _sig(_pj(proj_ref, 6, rows, cs).astype(F32)) * y_b)
                mg_ref[rows, cs] = mg.astype(BF16)
                hl_ref[rows, cs] = h16.astype(BF16)
            return carry

        lax.fori_loop(0, tt // 16, blk, 0)

        @pl.when(t == nt - 1)
        def _():
            finish()

    res = pl.pallas_call(
        body, name="mixer_fwd", grid=(nt,),
        in_specs=[pl.BlockSpec((tt, 7 * D), lambda t: (t, 0)),
                  pl.BlockSpec((16, D), lambda t: (0, 0)),
                  pl.BlockSpec((HEADS, HB, HB), lambda t: (0, 0, 0)),
                  pl.BlockSpec((HEADS, HB, HB), lambda t: (0, 0, 0))] + [_ANY] * na,
        out_specs=[pl.BlockSpec((tt, D), lambda t: (t, 0)), pl.BlockSpec((tt, D), lambda t: (t, 0)),
                   pl.BlockSpec((len(SV_PLANES), tt, D), lambda t: (0, t, 0))] + [_ANY] * na,
        out_shape=[jax.ShapeDtypeStruct((t_len, D), BF16), jax.ShapeDtypeStruct((t_len, D), BF16),
                   jax.ShapeDtypeStruct((len(SV_PLANES), t_len, D), F32)]
        + [jax.ShapeDtypeStruct(f, sh.dtype) for f, sh in zip(fulls, shards)],
        scratch_shapes=[pltpu.VMEM((tt + 8, D), F32), pltpu.VMEM((tt + 8, D), F32), pltpu.VMEM((8, D), F32),
                        pltpu.VMEM((tt, D), F32), pltpu.VMEM((tt, D), F32), pltpu.VMEM((tt, D), BF16)]
        + _ag_sems(na),
        compiler_params=_cp("arbitrary"),
    )(proj, prm, wa, wx, *shards)
    return res[0], res[1], res[2], res[3:]


def _out_proj(merged, x, mod, g_ffn, w_out):
    t_len = x.shape[0]
    tm = min(TM, t_len)

    def body(mg_ref, x_ref, mod_ref, g_ref, w_ref, x1_ref, h2_ref):
        gt1 = mod_ref[2:3, :]
        gs = g_ref[...] * (1.0 + mod_ref[4:5, :])
        sh = mod_ref[3:4, :]
        for sub in _sub_blocks(tm):
            x1_ref[sub, :] = x_ref[sub, :] + gt1 * _dot(mg_ref[sub, :], w_ref[...])
            for r0 in range(sub.start, sub.stop, 16):
                x1 = x1_ref[r0:r0 + 16, :]
                r = lax.rsqrt(jnp.mean(x1 * x1, axis=-1, keepdims=True) + EPS)
                h2_ref[r0:r0 + 16, :] = (x1 * r * gs + sh).astype(BF16)

    return pl.pallas_call(
        body, name="out_proj", grid=(t_len // tm,),
        in_specs=[pl.BlockSpec((tm, D), lambda i: (i, 0)), pl.BlockSpec((tm, D), lambda i: (i, 0)),
                  pl.BlockSpec((8, D), lambda i: (0, 0)), pl.BlockSpec((1, D), lambda i: (0, 0)),
                  pl.BlockSpec((D, D), lambda i: (0, 0))],
        out_specs=[pl.BlockSpec((tm, D), lambda i: (i, 0)), pl.BlockSpec((tm, D), lambda i: (i, 0))],
        out_shape=[jax.ShapeDtypeStruct((t_len, D), F32), jax.ShapeDtypeStruct((t_len, D), BF16)],
        compiler_params=_cp("parallel"),
    )(merged, x, mod, g_ffn, w_out)


def _ffn_fwd(h2, x1, target, mod, g_fin, w_gu, w_down):
    t_len = x1.shape[0]
    tm = min(TMF, t_len)

    def body(h2_ref, x1_ref, tg_ref, mod_ref, g_ref, wgu_ref, wd_ref, gu_ref, dx2_ref, dx2b_ref, loss_ref, dg_ref, acc):
        @pl.when(pl.program_id(0) == 0)
        def _():
            loss_ref[...] = jnp.zeros_like(loss_ref)
            dg_ref[...] = jnp.zeros_like(dg_ref)

        hb = h2_ref[...]
        ffn = None
        for j in range(4):
            gate = _dot_nt(hb, wgu_ref[0, j])
            up = _dot_nt(hb, wgu_ref[1, j])
            gu_ref[0, j] = gate.astype(BF16)
            gu_ref[1, j] = up.astype(BF16)
            act = (gate * _sig(gate) * up).astype(BF16)
            part = _dot(act, wd_ref[j * FB:(j + 1) * FB, :])
            ffn = part if ffn is None else ffn + part
        acc[...] = ffn

        gt2 = mod_ref[5:6, :]
        gf = g_ref[...]

        s_loss = s_dg = jnp.zeros((8, D), F32)
        for r0 in range(0, tm, 16):
            rows = slice(r0, r0 + 16)
            x2 = x1_ref[rows, :] + gt2 * acc[rows, :]
            r = lax.rsqrt(jnp.mean(x2 * x2, axis=-1, keepdims=True) + EPS)
            xn = x2 * r
            diff = xn * gf - tg_ref[rows, :]
            dy = diff * (1.0 / D)
            dxn = dy * gf
            dx2 = r * (dxn - xn * jnp.mean(dxn * xn, axis=-1, keepdims=True))
            dx2_ref[rows, :] = dx2
            dx2b_ref[rows, :] = dx2.astype(BF16)
            s_loss, s_dg = s_loss + _fold8(diff * diff), s_dg + _fold8(dy * xn)
        loss_ref[...] += jnp.sum(s_loss) * (0.5 / D)
        dg_ref[...] += jnp.sum(s_dg, axis=0, keepdims=True)

    row = pl.BlockSpec((tm, D), lambda i: (i, 0))
    return pl.pallas_call(
        body, name="ffn_fwd", grid=(t_len // tm,),
        in_specs=[row, row, row, pl.BlockSpec((8, D), lambda i: (0, 0)), pl.BlockSpec((1, D), lambda i: (0, 0)),
                  _resident((2, 4, FB, D)), _resident((DFF, D))],
        out_specs=[pl.BlockSpec((2, 4, tm, FB), lambda i: (0, 0, i, 0)), row, row,
                   pl.BlockSpec((1, 128), lambda i: (0, 0)), pl.BlockSpec((1, D), lambda i: (0, 0))],
        out_shape=[jax.ShapeDtypeStruct((2, 4, t_len, FB), BF16), jax.ShapeDtypeStruct((t_len, D), F32),
                   jax.ShapeDtypeStruct((t_len, D), BF16),
                   jax.ShapeDtypeStruct((1, 128), F32), jax.ShapeDtypeStruct((1, D), F32)],
        scratch_shapes=[pltpu.VMEM((tm, D), F32)],
        compiler_params=_cp("arbitrary"),
    )(h2, x1, target, mod, g_fin, w_gu, w_down)


S_SH, S_SC, S_G = 0, 1, 2


def _norm_bwd_rows(span, sums, dh_ref, x_ref, dres_ref, scale, gain, write):
    gs = 1.0 + scale
    s_sh, s_sc, s_g = sums
    for r0 in range(span.start, span.stop, 16):
        rows = slice(r0, r0 + 16)
        dh = dh_ref[rows, :]
        xv = x_ref[rows, :]
        r = lax.rsqrt(jnp.mean(xv * xv, axis=-1, keepdims=True) + EPS)
        xn = xv * r
        dhn = dh * gs
        dxn = dhn * gain
        write(rows, dres_ref[rows, :] + r * (dxn - xn * jnp.mean(dxn * xn, axis=-1, keepdims=True)))
        s_sh, s_sc, s_g = s_sh + _fold8(dh), s_sc + _fold8(dh * (xn * gain)), s_g + _fold8(dhn * xn)
    return s_sh, s_sc, s_g


def _add_norm_sums(sums_ref, sums):
    for dst, s in zip((S_SH, S_SC, S_G), sums):
        sums_ref[dst:dst + 1, :] += jnp.sum(s, axis=0, keepdims=True)


def _ffn_bwd(dx2, gu, x1, mod, g_ffn, w_gu, w_down, w_out):
    t_len = x1.shape[0]
    tm = min(TMF, t_len)

    def body(dx2_ref, gu_ref, x1_ref, mod_ref, g_ref, wgu_ref, wd_ref, wo_ref,
             dgu_ref, act_ref, dx1_ref, dx1b_ref, dmg_ref, sums_ref, acc, dmo):
        @pl.when(pl.program_id(0) == 0)
        def _():
            sums_ref[...] = jnp.zeros_like(sums_ref)

        dffn = (dx2_ref[...] * mod_ref[5:6, :]).astype(BF16)
        dh2 = None
        for j in range(4):
            dact = _dot_nt(dffn, wd_ref[j * FB:(j + 1) * FB, :])
            gate = gu_ref[0, j].astype(F32)
            up = gu_ref[1, j].astype(F32)
            sg = _sig(gate)
            silu = gate * sg
            act_ref[j] = (silu * up).astype(BF16)
            dgate = (dact * up * (sg * (1.0 + gate * (1.0 - sg)))).astype(BF16)
            dup = (dact * silu).astype(BF16)
            dgu_ref[0, j] = dgate
            dgu_ref[1, j] = dup
            part = _dot(dgate, wgu_ref[0, j]) + _dot(dup, wgu_ref[1, j])
            dh2 = part if dh2 is None else dh2 + part
        acc[...] = dh2

        gt1 = mod_ref[2:3, :]

        def write(rows, dx1):
            dx1_ref[rows, :] = dx1
            dx1b_ref[rows, :] = dx1.astype(BF16)
            dmo[rows, :] = (dx1 * gt1).astype(BF16)

        zero = jnp.zeros((8, D), F32)
        sums = (zero, zero, zero)
        for sub in (slice(0, tm // 2), slice(tm // 2, tm)):
            sums = _norm_bwd_rows(sub, sums, acc, x1_ref, dx2_ref, mod_ref[4:5, :], g_ref[...], write)
            dmg_ref[sub, :] = _dot_nt(dmo[sub, :], wo_ref[...]).astype(BF16)
        _add_norm_sums(sums_ref, sums)

    row = pl.BlockSpec((tm, D), lambda i: (i, 0))
    return pl.pallas_call(
        body, name="ffn_bwd", grid=(t_len // tm,),
        in_specs=[row, pl.BlockSpec((2, 4, tm, FB), lambda i: (0, 0, i, 0)), row,
                  pl.BlockSpec((8, D), lambda i: (0, 0)), pl.BlockSpec((1, D), lambda i: (0, 0)),
                  _resident((2, 4, FB, D)), _resident((DFF, D)), _resident((D, D))],
        out_specs=[pl.BlockSpec((2, 4, tm, FB), lambda i: (0, 0, i, 0)),
                   pl.BlockSpec((4, tm, FB), lambda i: (0, i, 0)), row, row, row,
                   pl.BlockSpec((8, D), lambda i: (0, 0))],
        out_shape=[jax.ShapeDtypeStruct((2, 4, t_len, FB), BF16), jax.ShapeDtypeStruct((4, t_len, FB), BF16),
                   jax.ShapeDtypeStruct((t_len, D), F32), jax.ShapeDtypeStruct((t_len, D), BF16),
                   jax.ShapeDtypeStruct((t_len, D), BF16), jax.ShapeDtypeStruct((8, D), F32)],
        scratch_shapes=[pltpu.VMEM((tm, D), F32), pltpu.VMEM((tm, D), BF16)],
        compiler_params=_cp("arbitrary"),
    )(dx2, gu, x1, mod, g_ffn, w_gu, w_down, w_out)


def _my_pos():
    return lax.axis_index("x"), lax.axis_index("y"), lax.axis_index("c")


def _my_index():
    x, y, c = _my_pos()
    return 4 * x + 2 * y + c


def _device_of(b):
    return (b >> 2) & 1, (b >> 1) & 1, b & 1


def _rs_send(src, parts_ref, b, send_sems, recv_sems, local_sem):
    me = _my_index()
    dst = parts_ref.at[me]

    @pl.when(b == me)
    def _():
        pltpu.make_async_copy(src, dst, local_sem).start()

    @pl.when(b != me)
    def _():
        pltpu.make_async_remote_copy(src_ref=src, dst_ref=dst, send_sem=send_sems.at[b], recv_sem=recv_sems.at[me],
                                     device_id=_device_of(b), device_id_type=MESH).start()


def _rs_finish(src_of, parts_ref, send_sems, recv_sems, local_sem):
    me = _my_index()
    for s in range(NDEV):
        @pl.when(s != me)
        def _():
            cp = pltpu.make_async_remote_copy(src_ref=src_of(s), dst_ref=parts_ref.at[s], send_sem=send_sems.at[s],
                                              recv_sem=recv_sems.at[s], device_id=_device_of(s), device_id_type=MESH)
            cp.wait_send()
            cp.wait_recv()

        @pl.when(s == me)
        def _():
            pltpu.make_async_copy(src_of(s), parts_ref.at[s], local_sem).wait()


_RS_SEMS = [pltpu.SemaphoreType.DMA((NDEV,)), pltpu.SemaphoreType.DMA((NDEV,)), pltpu.SemaphoreType.DMA]
_ANY = pl.BlockSpec(memory_space=pl.ANY)


def _xor_order(me, n):
    return (me ^ (n - 1 - jnp.arange(n, dtype=jnp.int32))).astype(jnp.int32)


NCHIP = NDEV // 2


def _rs2_scratch(half_shape):
    blocks = lambda *lead: pltpu.VMEM(lead + tuple(half_shape), BF16)
    return [blocks(NCHIP, 2), blocks(NCHIP)] + [pltpu.SemaphoreType.DMA((NCHIP,))] * 4 + [pltpu.SemaphoreType.DMA]


def _rs2_to_sibling(q, rs):
    stage, from_sib, d_send, d_recv = rs[:4]
    x, y, c = _my_pos()
    pltpu.make_async_remote_copy(src_ref=stage.at[q, 1 - c], dst_ref=from_sib.at[q], send_sem=d_send.at[q],
                                 recv_sem=d_recv.at[q], device_id=(x, y, 1 - c), device_id_type=MESH).start()


def _rs2_forward(q, parts_ref, rs):
    stage, chip_sum, d_send, d_recv, i_send, i_recv, local_sem = rs
    x, y, c = _my_pos()
    my_chip = 2 * x + y
    pltpu.make_async_remote_copy(src_ref=stage.at[q, c], dst_ref=chip_sum.at[q], send_sem=d_send.at[q],
                                 recv_sem=d_recv.at[q], device_id=(x, y, 1 - c), device_id_type=MESH).wait_recv()
    chip_sum[q] = (stage[q, c].astype(F32) + chip_sum[q].astype(F32)).astype(BF16)

    @pl.when(q == my_chip)
    def _():
        pltpu.make_async_copy(chip_sum.at[q], parts_ref.at[my_chip], local_sem).start()

    @pl.when(q != my_chip)
    def _():
        pltpu.make_async_remote_copy(src_ref=chip_sum.at[q], dst_ref=parts_ref.at[my_chip], send_sem=i_send.at[q],
                                     recv_sem=i_recv.at[my_chip], device_id=((q >> 1) & 1, q & 1, c),
                                     device_id_type=MESH).start()


def _rs2_finish(parts_ref, rs):
    stage, chip_sum, d_send, d_recv, i_send, i_recv, local_sem = rs
    x, y, c = _my_pos()
    my_chip = 2 * x + y
    for q in range(NCHIP):
        pltpu.make_async_remote_copy(src_ref=stage.at[q, 1 - c], dst_ref=chip_sum.at[q], send_sem=d_send.at[q],
                                     recv_sem=d_recv.at[q], device_id=(x, y, 1 - c), device_id_type=MESH).wait_send()

        @pl.when(q != my_chip)
        def _():
            cp = pltpu.make_async_remote_copy(src_ref=chip_sum.at[q], dst_ref=parts_ref.at[q], send_sem=i_send.at[q],
                                              recv_sem=i_recv.at[q], device_id=((q >> 1) & 1, q & 1, c),
                                              device_id_type=MESH)
            cp.wait_send()
            cp.wait_recv()

        @pl.when(q == my_chip)
        def _():
            pltpu.make_async_copy(chip_sum.at[q], parts_ref.at[q], local_sem).wait()


def _gu_wgrad(h2, dgu, order):
    t_len = h2.shape[0]
    tk = min(TK, t_len)
    nk = t_len // tk

    def body(ord_ref, h_ref, d_ref, parts_ref, acc, *rs):
        p, k = pl.program_id(0), pl.program_id(1)

        @pl.when(k == 0)
        def _():
            acc[...] = jnp.zeros_like(acc)

        hb = h_ref[...]
        for half in range(2):
            acc[half] += _dot_tn(d_ref[0, half], hb)

        @pl.when(k == nk - 1)
        def _():
            q = ord_ref[p]
            rs[0][q] = acc[...].astype(BF16)
            _rs2_to_sibling(q, rs)

        @pl.when((k == nk - 1) & (p > 0))
        def _():
            _rs2_forward(ord_ref[p - 1], parts_ref, rs)

        @pl.when((p == NCHIP - 1) & (k == nk - 1))
        def _():
            _rs2_forward(ord_ref[p], parts_ref, rs)
            _rs2_finish(parts_ref, rs)

    return pl.pallas_call(
        body, name="gu_wgrad",
        grid_spec=pltpu.PrefetchScalarGridSpec(
            num_scalar_prefetch=1, grid=(NCHIP, nk),
            in_specs=[pl.BlockSpec((tk, D), lambda p, k, o: (k, 0)),
                      pl.BlockSpec((1, 2, tk, FB), lambda p, k, o: (o[p], 0, k, 0))],
            out_specs=_ANY,
            scratch_shapes=[pltpu.VMEM((2, FB, D), F32)] + _rs2_scratch((FB, D))),
        out_shape=jax.ShapeDtypeStruct((NCHIP, FB, D), BF16),
        compiler_params=_cp("arbitrary", "arbitrary"),
    )(order, h2, dgu.reshape(NCHIP, 2, t_len, FB))


def _scaled_wgrad(name, a, dx, w, gate_row, mod, order):
    nb, t_len, kb = a.shape
    tk = min(TK, t_len)
    nk = t_len // tk
    cpb = NCHIP // nb
    rows = kb // (2 * cpb)

    def body(ord_ref, a_ref, dx_ref, w_ref, mod_ref, parts_ref, dg_ref, acc, *rs):
        p, k = pl.program_id(0), pl.program_id(1)
        j = ord_ref[p]

        @pl.when((p == 0) & (k == 0))
        def _():
            dg_ref[...] = jnp.zeros_like(dg_ref)

        @pl.when(k == 0)
        def _():
            acc[...] = jnp.zeros_like(acc)

        acc[...] += _dot_tn(a_ref[0], dx_ref[...])

        @pl.when(k == nk - 1)
        def _():
            z = acc[...]
            zg = (z * mod_ref[gate_row:gate_row + 1, :]).astype(BF16)
            dg_ref[0:1, :] += jnp.sum(z * w_ref[...].astype(F32), axis=0, keepdims=True)
            for i in range(cpb):
                q = j * cpb + i
                for half in range(2):
                    rs[0][q, half] = zg[(2 * i + half) * rows:(2 * i + half + 1) * rows]
                _rs2_to_sibling(q, rs)

        if cpb == 1:
            @pl.when((k == nk - 1) & (p > 0))
            def _():
                _rs2_forward(ord_ref[p - 1], parts_ref, rs)

        @pl.when((p == nb - 1) & (k == nk - 1))
        def _():
            for i in range(cpb):
                _rs2_forward(j * cpb + i, parts_ref, rs)
            _rs2_finish(parts_ref, rs)

    return pl.pallas_call(
        body, name=name,
        grid_spec=pltpu.PrefetchScalarGridSpec(
            num_scalar_prefetch=1, grid=(nb, nk),
            in_specs=[pl.BlockSpec((1, tk, kb), lambda p, k, o: (o[p], k, 0)),
                      pl.BlockSpec((tk, D), lambda p, k, o: (k, 0)),
                      pl.BlockSpec((kb, D), lambda p, k, o: (o[p], 0)),
                      pl.BlockSpec((8, D), lambda p, k, o: (0, 0))],
            out_specs=[_ANY, pl.BlockSpec((8, D), lambda p, k, o: (0, 0))],
            scratch_shapes=[pltpu.VMEM((kb, D), F32)] + _rs2_scratch((rows, D))),
        out_shape=[jax.ShapeDtypeStruct((NCHIP, rows, D), BF16), jax.ShapeDtypeStruct((8, D), F32)],
        compiler_params=_cp("arbitrary", "arbitrary"),
    )(order, a, dx, w, mod)


M_WA, M_WB, M_CBIAS, M_BA, M_BX, M_LS = 0, 3, 7, 8, 9, 10


def _conv_bwd_rows(tt, proj_ref, prm_ref, xe, ve, due, dye, dp_ref, acc8):
    row = lax.broadcasted_iota(jnp.int32, (8, CG), 0)
    w_b = [prm_ref[P_WB + k:P_WB + k + 1, :] for k in range(4)]
    w_a = [prm_ref[P_WA + k:P_WA + k + 1, :] for k in range(3)]

    def blk(ib, carry):
        r0 = pl.multiple_of(ib * 16, 16)
        rows = pl.ds(r0, 16)
        for g in range(D // CG):
            cs = slice(g * CG, (g + 1) * CG)
            du16, du_after = due[rows, cs], due[pl.ds(r0 + 16, 8), cs]
            dy16, dy_after = dye[rows, cs], dye[pl.ds(r0 + 16, 8), cs]
            cc16 = _pj(proj_ref, 1, rows, cs).astype(F32)
            cx16 = _pj(proj_ref, 2, rows, cs).astype(F32)
            x16 = _pj(proj_ref, 3, rows, cs).astype(F32)
            v16 = cc16 * cx16
            xp, vp = xe[pl.ds(r0, 8), cs], ve[pl.ds(r0, 8), cs]
            xe[pl.ds(r0 + 16, 8), cs] = x16[8:16]
            ve[pl.ds(r0 + 16, 8), cs] = v16[8:16]
            acc = [acc8[8 * k:8 * k + 8, cs] for k in range(8)]
            drx, dv = [], []
            for sb in range(2):
                lo = slice(8 * sb, 8 * sb + 8)
                duc, dyc, xc, vc = du16[lo], dy16[lo], x16[lo], v16[lo]
                du_n = du16[8:16] if sb == 0 else du_after
                dy_n = dy16[8:16] if sb == 0 else dy_after
                acc[0] = acc[0] + duc
                acc[4] = acc[4] + duc * xc
                d8 = w_b[3][:, cs] * duc
                for s in (1, 2, 3):
                    acc[4 - s] = acc[4 - s] + duc * _shift_down(xc, xp, s, row)
                    d8 = d8 + w_b[3 - s][:, cs] * _shift_up(duc, du_n, s, row)
                acc[7] = acc[7] + dyc * vc
                e8 = w_a[2][:, cs] * dyc
                for s in (1, 2):
                    acc[7 - s] = acc[7 - s] + dyc * _shift_down(vc, vp, s, row)
                    e8 = e8 + w_a[2 - s][:, cs] * _shift_up(dyc, dy_n, s, row)
                drx.append(d8)
                dv.append(e8)
                xp, vp = xc, vc
            for k in range(8):
                acc8[8 * k:8 * k + 8, cs] = acc[k]
            dv16 = jnp.concatenate(dv, axis=0)
            col = lambda s: slice(s * D + g * CG, s * D + (g + 1) * CG)
            dp_ref[rows, col(3)] = jnp.concatenate(drx, axis=0).astype(BF16)
            dp_ref[rows, col(1)] = (dv16 * cx16).astype(BF16)
            dp_ref[rows, col(2)] = (dv16 * cc16).astype(BF16)
        return carry

    lax.fori_loop(0, tt // 16, blk, 0)


def _mixer_bwd(proj, hl, sv, dmg, prm, wa, wx):
    t_len = proj.shape[0]
    tt = min(TT, t_len)
    nt = t_len // tt
    hb8 = tt // 8

    def rev(i):
        return nt - 1 - i

    def halo(i):
        return jnp.maximum(rev(i) * hb8 - 1, 0)

    def body(proj_ref, ph_ref, hl_ref, hh_ref, sv_ref, dmg_ref, prm_ref, wa_ref, wx_ref,
             dp_ref, sums_ref, gwa_ref, gwx_ref,
             xe, ve, he, due, dye, drp_s, dip_s, an, gn, acc8):
        i = pl.program_id(0)
        t = rev(i)

        @pl.when(i == 0)
        def _():
            sums_ref[...] = jnp.zeros_like(sums_ref)
            gwa_ref[...] = jnp.zeros_like(gwa_ref)
            gwx_ref[...] = jnp.zeros_like(gwx_ref)
            due[tt:tt + 8, :] = jnp.zeros((8, D), F32)
            dye[tt:tt + 8, :] = jnp.zeros((8, D), F32)
            an[...] = jnp.zeros((8, D), F32)
            gn[...] = jnp.zeros((8, D), F32)

        live = (t > 0).astype(F32)
        xe[0:8, :] = _pj(ph_ref, 3).astype(F32) * live
        ve[0:8, :] = _pj(ph_ref, 1).astype(F32) * _pj(ph_ref, 2).astype(F32) * live
        he[0:8, :] = hh_ref[...].astype(F32) * live
        he[8:8 + tt, :] = hl_ref[...].astype(F32)

        ls_all = _log_sigmoid(prm_ref[P_LAM:P_LAM + 1, :])
        row = lax.broadcasted_iota(jnp.int32, (8, CG), 0)
        nblk = tt // 16

        def blk(ib, carry):
            r0 = pl.multiple_of((nblk - 1 - ib) * 16, 16)
            rows = pl.ds(r0, 16)
            for g in range(D // CG):
                cs = slice(g * CG, (g + 1) * CG)
                ls = ls_all[:, cs]
                dm = dmg_ref[rows, cs].astype(F32)
                cb = _pj(proj_ref, 0, rows, cs).astype(F32)
                rg = _pj(proj_ref, 4, rows, cs).astype(F32)
                sga = _sig(_pj(proj_ref, 5, rows, cs).astype(F32))
                sgb = _sig(_pj(proj_ref, 6, rows, cs).astype(F32))
                ya0 = sv_ref[SV_YA, rows, cs]
                h16 = he[pl.ds(r0 + 8, 16), cs]
                gl, th = _gelu(rg)
                dgl = 0.5 * (1.0 + th) + 0.5 * rg * (1.0 - th * th) * (_GC * (1.0 + 3.0 * 0.044715 * rg * rg))
                y_a = cb * ya0
                y_b = h16 * gl
                dy_a = dm * sga
                dy_b = dm * sgb
                col = lambda s: slice(s * D + g * CG, s * D + (g + 1) * CG)
                dp_ref[rows, col(5)] = (dm * y_a * sga * (1.0 - sga)).astype(BF16)
                dp_ref[rows, col(6)] = (dm * y_b * sgb * (1.0 - sgb)).astype(BF16)
                dp_ref[rows, col(4)] = (dy_b * h16 * dgl).astype(BF16)
                dp_ref[rows, col(0)] = (dy_a * ya0).astype(BF16)
                dye[rows, cs] = dy_a * cb
                dh16 = dy_b * gl

                a_next = an[:, cs]
                g_next = gn[:, cs]
                s_ba = jnp.zeros((8, CG), F32)
                s_bx = jnp.zeros((8, CG), F32)
                s_ls = jnp.zeros((8, CG), F32)
                for sb in (1, 0):
                    rr = r0 + 8 * sb
                    first = (row + (t * tt + rr)) == 0
                    r8 = pl.ds(rr, 8)
                    uu, r, ig, a, mult = (sv_ref[pln, r8, cs] for pln in (SV_U, SV_R, SV_I, SV_A, SV_MULT))
                    ca = jnp.where(row < 7, pltpu.roll(a, 7, 0), a_next)
                    cb_ = dh16[8 * sb:8 * sb + 8, :]
                    for s in (1, 2, 4):
                        a_sh = jnp.where(row < 8 - s, pltpu.roll(ca, 8 - s, 0), 1.0)
                        b_sh = jnp.where(row < 8 - s, pltpu.roll(cb_, 8 - s, 0), 0.0)
                        cb_ = ca * b_sh + cb_
                        ca = ca * a_sh
                    gv = ca * g_next + cb_
                    g_next = jnp.broadcast_to(gv[0:1, :], gv.shape)
                    a_next = jnp.broadcast_to(a[0:1, :], a.shape)
                    hprev = jnp.where(row >= 1, pltpu.roll(he[pl.ds(rr + 8, 8), cs], 1, 0),
                                      pltpu.roll(he[pl.ds(rr, 8), cs], 1, 0))
                    da = gv * hprev
                    dmult = jnp.where(first, 0.0, gv * ig * uu)
                    dla = da * a + jnp.where(mult > 0.0, dmult * (-(a * a) / mult), 0.0)
                    drp = dla * (LRU_C * ls) * r * (1.0 - r)
                    dip = gv * mult * uu * ig * (1.0 - ig)
                    s_ls = s_ls + dla * (LRU_C * r)
                    s_ba = s_ba + drp
                    s_bx = s_bx + dip
                    drp_s[pl.ds(rr, 8), cs] = drp
                    dip_s[pl.ds(rr, 8), cs] = dip
                    due[pl.ds(rr, 8), cs] = gv * mult * ig
                an[:, cs] = a_next
                gn[:, cs] = g_next
                sums_ref[M_BA:M_BA + 1, cs] += jnp.sum(s_ba, axis=0, keepdims=True)
                sums_ref[M_BX:M_BX + 1, cs] += jnp.sum(s_bx, axis=0, keepdims=True)
                sums_ref[M_LS:M_LS + 1, cs] += jnp.sum(s_ls, axis=0, keepdims=True)
            return carry

        lax.fori_loop(0, nblk, blk, 0)

        drp_b = drp_s[...].astype(BF16)
        dip_b = dip_s[...].astype(BF16)
        ub = sv_ref[SV_U].astype(BF16)
        for h in range(HEADS):
            cs = slice(h * HB, (h + 1) * HB)
            due[0:tt, cs] += _dot_nt(drp_b[:, cs], wa_ref[h]) + _dot_nt(dip_b[:, cs], wx_ref[h])
            gwa_ref[h] += _dot_tn(ub[:, cs], drp_b[:, cs])
            gwx_ref[h] += _dot_tn(ub[:, cs], dip_b[:, cs])

        acc8[...] = jnp.zeros_like(acc8)
        _conv_bwd_rows(tt, proj_ref, prm_ref, xe, ve, due, dye, dp_ref, acc8)
        for k, dst in enumerate([M_CBIAS] + [M_WB + k for k in range(4)] + [M_WA + k for k in range(3)]):
            sums_ref[dst:dst + 1, :] += jnp.sum(acc8[8 * k:8 * k + 8, :], axis=0, keepdims=True)
        due[tt:tt + 8, :] = due[0:8, :]
        dye[tt:tt + 8, :] = dye[0:8, :]

        @pl.when(i == nt - 1)
        def _():
            sums_ref[M_LS:M_LS + 1, :] = sums_ref[M_LS:M_LS + 1, :] * _sig(-prm_ref[P_LAM:P_LAM + 1, :])

    big = lambda: pltpu.VMEM((tt + 8, D), F32)
    tile = lambda: pltpu.VMEM((tt, D), F32)
    return pl.pallas_call(
        body, name="mixer_bwd", grid=(nt,),
        in_specs=[pl.BlockSpec((tt, 7 * D), lambda i: (rev(i), 0)),
                  pl.BlockSpec((8, 7 * D), lambda i: (halo(i), 0)),
                  pl.BlockSpec((tt, D), lambda i: (rev(i), 0)),
                  pl.BlockSpec((8, D), lambda i: (halo(i), 0)),
                  pl.BlockSpec((len(SV_PLANES), tt, D), lambda i: (0, rev(i), 0)),
                  pl.BlockSpec((tt, D), lambda i: (rev(i), 0)),
                  pl.BlockSpec((16, D), lambda i: (0, 0)),
                  pl.BlockSpec((HEADS, HB, HB), lambda i: (0, 0, 0)),
                  pl.BlockSpec((HEADS, HB, HB), lambda i: (0, 0, 0))],
        out_specs=[pl.BlockSpec((tt, 7 * D), lambda i: (rev(i), 0)),
                   pl.BlockSpec((16, D), lambda i: (0, 0)),
                   pl.BlockSpec((HEADS, HB, HB), lambda i: (0, 0, 0)),
                   pl.BlockSpec((HEADS, HB, HB), lambda i: (0, 0, 0))],
        out_shape=[jax.ShapeDtypeStruct((t_len, 7 * D), BF16), jax.ShapeDtypeStruct((16, D), F32),
                   jax.ShapeDtypeStruct((HEADS, HB, HB), F32), jax.ShapeDtypeStruct((HEADS, HB, HB), F32)],
        scratch_shapes=[big(), big(), big(), big(), big(), tile(), tile(),
                        pltpu.VMEM((8, D), F32), pltpu.VMEM((8, D), F32), pltpu.VMEM((64, D), F32)],
        compiler_params=_cp("arbitrary"),
    )(proj, proj, hl, hl, sv, dmg, prm, wa, wx)


def _in_proj_bwd(dproj, w_in, x, dx1, mod, g_mix):
    t_len = x.shape[0]
    tm = min(TM, t_len)

    def body(dp_ref, w_ref, x_ref, dx1_ref, mod_ref, g_ref, gx_ref, sums_ref, acc):
        @pl.when(pl.program_id(0) == 0)
        def _():
            sums_ref[...] = jnp.zeros_like(sums_ref)

        def write(rows, dx):
            gx_ref[rows, :] = dx

        zero = jnp.zeros((8, D), F32)
        sums = (zero, zero, zero)
        for sub in _sub_blocks(tm):
            acc[sub, :] = _dot_nt(dp_ref[sub, :], w_ref[...])
            sums = _norm_bwd_rows(sub, sums, acc, x_ref, dx1_ref, mod_ref[1:2, :], g_ref[...], write)
        _add_norm_sums(sums_ref, sums)

    return pl.pallas_call(
        body, name="in_proj_bwd", grid=(t_len // tm,),
        in_specs=[pl.BlockSpec((tm, 7 * D), lambda i: (i, 0)),
                  _resident((D, 7 * D)),
                  pl.BlockSpec((tm, D), lambda i: (i, 0)), pl.BlockSpec((tm, D), lambda i: (i, 0)),
                  pl.BlockSpec((8, D), lambda i: (0, 0)), pl.BlockSpec((1, D), lambda i: (0, 0))],
        out_specs=[pl.BlockSpec((tm, D), lambda i: (i, 0)), pl.BlockSpec((8, D), lambda i: (0, 0))],
        out_shape=[jax.ShapeDtypeStruct((t_len, D), F32), jax.ShapeDtypeStruct((8, D), F32)],
        scratch_shapes=[pltpu.VMEM((tm, D), F32)],
        compiler_params=_cp("arbitrary"),
    )(dproj, w_in, x, dx1, mod, g_mix)


def _in_wgrad(h, dproj, g_wa, g_wx, order):
    t_len = h.shape[0]
    tk = min(TKI, t_len)
    nk = t_len // tk
    cw = 7 * D // NDEV
    hr = HB // NDEV

    def body(ord_ref, h_ref, d_ref, ga_ref, gx_ref, parts_ref, pa_ref, px_ref, acc, *scr):
        rs, sems = scr[:-6], scr[-6:]
        p, k = pl.program_id(0), pl.program_id(1)

        def head_rows(ref):
            return lambda s: ref.at[:, pl.ds(s * hr, hr), :]

        @pl.when((p == 0) & (k == 0))
        def _():
            for s in range(NDEV):
                _rs_send(head_rows(ga_ref)(s), pa_ref, s, *sems[0:3])
                _rs_send(head_rows(gx_ref)(s), px_ref, s, *sems[3:6])

        @pl.when(k == 0)
        def _():
            acc[...] = jnp.zeros_like(acc)

        acc[...] += _dot_tn(h_ref[...], d_ref[...])

        @pl.when(k == nk - 1)
        def _():
            q = ord_ref[p]
            for half in range(2):
                rs[0][q, half] = acc[:, half * cw:(half + 1) * cw].astype(BF16)
            _rs2_to_sibling(q, rs)

        @pl.when((k == nk - 1) & (p > 0))
        def _():
            _rs2_forward(ord_ref[p - 1], parts_ref, rs)

        @pl.when((p == NCHIP - 1) & (k == nk - 1))
        def _():
            _rs2_forward(ord_ref[p], parts_ref, rs)
            _rs2_finish(parts_ref, rs)
            _rs_finish(head_rows(ga_ref), pa_ref, *sems[0:3])
            _rs_finish(head_rows(gx_ref), px_ref, *sems[3:6])

    return pl.pallas_call(
        body, name="in_wgrad",
        grid_spec=pltpu.PrefetchScalarGridSpec(
            num_scalar_prefetch=1, grid=(NCHIP, nk),
            in_specs=[pl.BlockSpec((tk, D), lambda p, k, o: (k, 0)),
                      pl.BlockSpec((tk, 2 * cw), lambda p, k, o: (k, o[p])), _ANY, _ANY],
            out_specs=[_ANY, _ANY, _ANY],
            scratch_shapes=[pltpu.VMEM((D, 2 * cw), F32)] + _rs2_scratch((D, cw)) + _RS_SEMS * 2),
        out_shape=[jax.ShapeDtypeStruct((NCHIP, D, cw), BF16), jax.ShapeDtypeStruct((NDEV, HEADS, hr, HB), F32),
                   jax.ShapeDtypeStruct((NDEV, HEADS, hr, HB), F32)],
        compiler_params=_cp("arbitrary", "arbitrary"),
    )(order, h, dproj, g_wa, g_wx)


def _ada_fwd(c_all, w_ada, b_cols):
    def body(c_ref, w_ref, b_ref, o_ref):
        cv = c_ref[...]
        o_ref[...] = _dot((cv * _sig(cv)).astype(BF16), w_ref[...].astype(BF16)) + b_ref[...]

    return pl.pallas_call(body, name="ada_fwd", out_shape=jax.ShapeDtypeStruct((16, w_ada.shape[1]), F32),
                          compiler_params=_cp())(c_all, w_ada, b_cols)


def _adam_math(w, g, m, v):
    m = ADAM_B1 * m + (1.0 - ADAM_B1) * g
    v = ADAM_B2 * v + (1.0 - ADAM_B2) * (g * g)
    m_hat = m / (1.0 - ADAM_B1 ** ADAM_STEP)
    v_hat = v / (1.0 - ADAM_B2 ** ADAM_STEP)
    delta = -ADAM_LR * (m_hat / (jnp.sqrt(v_hat) + ADAM_EPS) + ADAM_WD * w)
    return delta, m, v


def _ada_bwd(c_all, dmod_cols, w, m, v):
    rb = 256
    n = w.shape[1]
    nrow = c_all.shape[0]

    def body(c_ref, d_ref, w_ref, m_ref, v_ref, g_ref, dl_ref, nm_ref, nv_ref):
        cv = c_ref[...]
        g = _dot_tn((cv * _sig(cv)).astype(BF16), d_ref[...].astype(BF16))
        g_ref[...] = g
        dl_ref[...], nm_ref[...], nv_ref[...] = _adam_math(w_ref[...], g, m_ref[...], v_ref[...])

    blk = pl.BlockSpec((rb, n), lambda i: (i, 0))
    sds = jax.ShapeDtypeStruct(w.shape, F32)
    return pl.pallas_call(
        body, name="ada_bwd", grid=(D // rb,),
        in_specs=[pl.BlockSpec((nrow, rb), lambda i: (0, i)), pl.BlockSpec((nrow, n), lambda i: (0, 0)), blk, blk, blk],
        out_specs=[blk, blk, blk, blk], out_shape=[sds, sds, sds, sds],
        compiler_params=_cp("parallel"),
    )(c_all, dmod_cols, w, m, v)


def _adam(name, parts, w, m, v):
    p, r, c = parts.shape
    rb = r
    for cand in (256, 128, 64, 32, 16, 8):
        if r % cand == 0 and r >= cand:
            rb = cand
            break

    def body(p_ref, w_ref, m_ref, v_ref, g_ref, dl_ref, nm_ref, nv_ref):
        g = p_ref[0].astype(F32)
        for q in range(1, p):
            g = g + p_ref[q].astype(F32)
        g_ref[...] = g
        dl_ref[...], nm_ref[...], nv_ref[...] = _adam_math(w_ref[...], g, m_ref[...], v_ref[...])

    blk = pl.BlockSpec((rb, c), lambda i: (i, 0))
    sds = jax.ShapeDtypeStruct((r, c), F32)
    return pl.pallas_call(
        body, name=name, grid=(r // rb,),
        in_specs=[pl.BlockSpec((p, rb, c), lambda i: (0, i, 0)), blk, blk, blk],
        out_specs=[blk, blk, blk, blk], out_shape=[sds, sds, sds, sds],
        compiler_params=_cp("parallel"),
    )(parts, w, m, v)


def _my_pos():
    return lax.axis_index("x"), lax.axis_index("y"), lax.axis_index("c")


def _all_gather_small(name, v):
    m_per, n = v.shape

    def body(x_ref, out_ref, send_sems, recv_sems, local_sem):
        x, y, c = _my_pos()
        me, sibling = (x, y, c), (x, y, 1 - c)
        chips = [(1 - x, y), (x, 1 - y), (1 - x, 1 - y)]

        def rows(px, py, pc):
            return out_ref.at[pl.ds((4 * px + 2 * py + pc) * m_per, m_per), :]

        def copy(k, block, to, src=None):
            return pltpu.make_async_remote_copy(
                src_ref=rows(*block) if src is None else src, dst_ref=rows(*block),
                send_sem=send_sems.at[k], recv_sem=recv_sems.at[k], device_id=to, device_id_type=MESH)

        mine = pltpu.make_async_copy(x_ref, rows(*me), local_sem)
        mine.start()
        first = [copy(0, me, sibling, src=x_ref)]
        first += [copy(1 + j, me, (*chip, c), src=x_ref) for j, chip in enumerate(chips)]
        for cp in first:
            cp.start()
        passed = [copy(4 + j, (*chip, c), sibling) for j, chip in enumerate(chips)]
        for j, chip in enumerate(chips):
            copy(1 + j, (*chip, c), me).wait_recv()
            passed[j].start()
        copy(0, sibling, me).wait_recv()
        for j, chip in enumerate(chips):
            copy(4 + j, (*chip, 1 - c), me).wait_recv()
        for cp in first + passed:
            cp.wait_send()
        mine.wait()

    return pl.pallas_call(
        body, name=name, out_shape=jax.ShapeDtypeStruct((NDEV * m_per, n), v.dtype),
        in_specs=[pl.BlockSpec(memory_space=pltpu.VMEM)], out_specs=pl.BlockSpec(memory_space=pltpu.VMEM),
        scratch_shapes=[pltpu.SemaphoreType.DMA((7,)), pltpu.SemaphoreType.DMA((7,)), pltpu.SemaphoreType.DMA],
    )(v)


def _blk_cols(n):
    return lambda ref, b: ref.at[:, pl.ds(pl.multiple_of(b * n, 128), n)]


def _blk_rows(n):
    return lambda ref, b: ref.at[pl.ds(pl.multiple_of(b * n, 8), n), :]


def _blk_lead(ref, b):
    return ref.at[b]


def _blk_heads(ref, b):
    return ref.at[:, pl.ds(pl.multiple_of(b * (HB // NDEV), 8), HB // NDEV), :]


def _ag_phases(ins, outs, slicers, send_sems, recv_sems, local_sems):
    na = len(ins)
    x, y, c = _my_pos()
    me, sibling = (x, y, c), (x, y, 1 - c)
    chips = [(1 - x, y), (x, 1 - y), (1 - x, 1 - y)]

    def copy(a, k, block, to, from_shard=False):
        px, py, pc = block
        dst = slicers[a](outs[a], 4 * px + 2 * py + pc)
        return pltpu.make_async_remote_copy(
            src_ref=ins[a] if from_shard else dst, dst_ref=dst,
            send_sem=send_sems.at[a * 7 + k], recv_sem=recv_sems.at[a * 7 + k], device_id=to, device_id_type=MESH)

    def local(a):
        return pltpu.make_async_copy(ins[a], slicers[a](outs[a], 4 * x + 2 * y + c), local_sems.at[a])

    def firsts(a):
        return [copy(a, 0, me, sibling, True)] + [copy(a, 1 + j, me, (*chip, c), True) for j, chip in enumerate(chips)]

    def start():
        for a in range(na):
            local(a).start()
            for cp in firsts(a):
                cp.start()

    def forward():
        for a in range(na):
            for j, chip in enumerate(chips):
                copy(a, 1 + j, (*chip, c), me).wait_recv()
                copy(a, 4 + j, (*chip, c), sibling).start()

    def finish():
        for a in range(na):
            copy(a, 0, sibling, me).wait_recv()
            for j, chip in enumerate(chips):
                copy(a, 4 + j, (*chip, 1 - c), me).wait_recv()
        for a in range(na):
            for cp in firsts(a) + [copy(a, 4 + j, (*chip, c), sibling) for j, chip in enumerate(chips)]:
                cp.wait_send()
            local(a).wait()

    return start, forward, finish


def _ag_sems(na):
    return [pltpu.SemaphoreType.DMA((7 * na,)), pltpu.SemaphoreType.DMA((7 * na,)), pltpu.SemaphoreType.DMA((na,))]


def _all_gather_weights(shards, fulls, slicers):
    na = len(shards)

    def body(*refs):
        start, forward, finish = _ag_phases(refs[:na], refs[na:2 * na], slicers, *refs[2 * na:])
        start()
        forward()
        finish()

    return pl.pallas_call(
        body, name="gather_weights",
        out_shape=[jax.ShapeDtypeStruct(s, sh.dtype) for s, sh in zip(fulls, shards)],
        in_specs=[_ANY] * na, out_specs=[_ANY] * na, scratch_shapes=_ag_sems(na),
    )(*shards)


def _scatter_grads(grads, shard_shapes, slicers):
    na = len(grads)

    def body(*refs):
        ins, outs = refs[:na], refs[na:2 * na]
        send_sems, recv_sems, local_sems = refs[2 * na:]
        x, y, c = _my_pos()
        me = 4 * x + 2 * y + c
        mine, sent = [], []
        for a in range(na):
            cp = pltpu.make_async_copy(slicers[a](ins[a], me), outs[a].at[me], local_sems.at[a])
            cp.start()
            mine.append(cp)
        rel = [(k >> 2 & 1, k >> 1 & 1, k & 1) for k in range(1, NDEV)]
        for a in range(na):
            for k, (fx, fy, fc) in enumerate(rel):
                px, py, pc = x ^ fx, y ^ fy, c ^ fc
                cp = pltpu.make_async_remote_copy(
                    src_ref=slicers[a](ins[a], 4 * px + 2 * py + pc), dst_ref=outs[a].at[me],
                    send_sem=send_sems.at[a * 7 + k], recv_sem=recv_sems.at[a * 7 + k],
                    device_id=(px, py, pc), device_id_type=MESH)
                cp.start()
                sent.append(cp)
        for a in range(na):
            for k, (fx, fy, fc) in enumerate(rel):
                px, py, pc = x ^ fx, y ^ fy, c ^ fc
                src = 4 * px + 2 * py + pc
                pltpu.make_async_remote_copy(
                    src_ref=slicers[a](ins[a], me), dst_ref=outs[a].at[src],
                    send_sem=send_sems.at[a * 7 + k], recv_sem=recv_sems.at[a * 7 + k],
                    device_id=(px, py, pc), device_id_type=MESH).wait_recv()
        for cp in sent:
            cp.wait_send()
        for cp in mine:
            cp.wait()

    any_spec = pl.BlockSpec(memory_space=pl.ANY)
    return pl.pallas_call(
        body, name="scatter_grads",
        out_shape=[jax.ShapeDtypeStruct((NDEV,) + tuple(s), g.dtype) for s, g in zip(shard_shapes, grads)],
        in_specs=[any_spec] * na, out_specs=[any_spec] * na,
        scratch_shapes=[pltpu.SemaphoreType.DMA((7 * na,)), pltpu.SemaphoreType.DMA((7 * na,)),
                        pltpu.SemaphoreType.DMA((na,))],
    )(*grads)


def _local_step(x, target, mod, g_mix, g_ffn, g_fin, prm, w_in_shard, shards):
    fulls = [(HEADS, HB, HB), (HEADS, HB, HB), (D, D), (NDEV, FB, D), (DFF, D)]
    slicers = [_blk_heads, _blk_heads, _blk_rows(D // NDEV), _blk_lead, _blk_rows(DFF // NDEV)]
    my_chip = _my_index() >> 1
    own_first = (my_chip ^ jnp.arange(NCHIP, dtype=jnp.int32)).astype(jnp.int32)
    early, late = [0, 1, 2, 4], [3]
    pick = lambda lst, idx: [lst[i] for i in idx]
    proj, h, w_in, (wa, wx, w_out, w_down) = _in_proj(x, mod, g_mix, w_in_shard, own_first, pick(shards, early),
                                                      pick(fulls, early), pick(slicers, early))
    merged, hl, sv, (w_gu,) = _mixer_fwd(proj, prm, wa, wx, pick(shards, late), pick(fulls, late),
                                         pick(slicers, late))
    w_gu = w_gu.reshape(2, 4, FB, D)
    x1, h2 = _out_proj(merged, x, mod, g_ffn, w_out)
    gu, dx2, dx2b, loss, d_gfin = _ffn_fwd(h2, x1, target, mod, g_fin, w_gu, w_down)
    dgu, act, dx1, dx1b, dmg, sums2 = _ffn_bwd(dx2, gu, x1, mod, g_ffn, w_gu, w_down, w_out)
    chip_order = _xor_order(_my_index() >> 1, NCHIP)
    p_wgu = _gu_wgrad(h2, dgu, chip_order)
    p_wdown, d_gt2 = _scaled_wgrad("down_wgrad", act, dx2b, w_down, 5, mod, chip_order)
    p_wout, d_gt1 = _scaled_wgrad("out_wgrad", merged.reshape(1, *merged.shape), dx1b, w_out, 2, mod,
                                  jnp.zeros((1,), jnp.int32))
    dproj, msums, g_wa, g_wx = _mixer_bwd(proj, hl, sv, dmg, prm, wa, wx)
    p_win, p_wa, p_wx = _in_wgrad(h, dproj, g_wa, g_wx, chip_order)
    grad_x, sums1 = _in_proj_bwd(dproj, w_in, x, dx1, mod, g_mix)
    return dict(loss=loss, grad_x=grad_x, d_gfin=d_gfin, sums1=sums1, sums2=sums2, msums=msums,
                d_gt1=d_gt1[0:1], d_gt2=d_gt2[0:1], p_win=p_win, p_wa=p_wa, p_wx=p_wx, p_wout=p_wout, p_wgu=p_wgu,
                p_wdown=p_wdown)


def kernel(x, c, w_ada, b_ada, g_norm_mix, w_in, conv_a_w, conv_b_w, conv_b_bias, w_rg_a, b_rg_a, w_rg_x, b_rg_x, lru_lambda, w_out, g_norm_ffn, w_gate_up, w_down, g_norm_final, loss_target, m_w_ada, m_b_ada, m_g_norm_mix, m_w_in, m_conv_a_w, m_conv_b_w, m_conv_b_bias, m_w_rg_a, m_b_rg_a, m_w_rg_x, m_b_rg_x, m_lru_lambda, m_w_out, m_g_norm_ffn, m_w_gate_up, m_w_down, m_g_norm_final, v_w_ada, v_b_ada, v_g_norm_mix, v_w_in, v_conv_a_w, v_conv_b_w, v_conv_b_bias, v_w_rg_a, v_b_rg_a, v_w_rg_x, v_b_rg_x, v_lru_lambda, v_w_out, v_g_norm_ffn, v_w_gate_up, v_w_down, v_g_norm_final):
    me = 4 * lax.axis_index("x") + 2 * lax.axis_index("y") + lax.axis_index("c")
    ncol = w_ada.shape[2]
    cw = conv_a_w.shape[2]

    pack0 = jnp.concatenate([c, conv_a_w.reshape(1, 3 * cw), conv_b_w.reshape(1, 4 * cw)], axis=1)
    got0 = _all_gather_small("gather_c", jnp.broadcast_to(pack0, (8, pack0.shape[1])))
    got0 = got0.reshape(NDEV, 8, -1)[:, 0, :]
    c_all = got0[:, :D]
    conv_a = got0[:, D:D + 3 * cw].reshape(NDEV, 3, cw).transpose(1, 0, 2).reshape(3, D)
    conv_b = got0[:, D + 3 * cw:].reshape(NDEV, 4, cw).transpose(1, 0, 2).reshape(4, D)

    b_cols = lax.dynamic_slice_in_dim(b_ada, me * ncol, ncol, axis=1)
    c16 = jnp.concatenate([c_all, jnp.zeros((8, D), F32)], axis=0)
    mod_cols = _ada_fwd(c16, w_ada[0], b_cols)[:NDEV]
    got1 = _all_gather_small("gather_mod", mod_cols).reshape(NDEV, NDEV, ncol)
    mod6 = lax.dynamic_index_in_dim(got1, me, axis=1, keepdims=False).reshape(6, D)
    mod = jnp.concatenate([mod6, jnp.zeros((2, D), F32)], axis=0)

    tr = lambda a: jnp.swapaxes(a, 1, 2)
    shards = [w_rg_a[0].astype(BF16), w_rg_x[0].astype(BF16), w_out[0].astype(BF16), tr(w_gate_up)[0].astype(BF16),
              w_down[0].astype(BF16)]

    prm = jnp.concatenate([conv_a, conv_b, conv_b_bias, b_rg_a, b_rg_x, lru_lambda, jnp.zeros((5, D), F32)], axis=0)
    r = _local_step(x[0], loss_target[0], mod, g_norm_mix, g_norm_ffn, g_norm_final.reshape(1, D), prm,
                    w_in[0].astype(BF16), shards)

    parts = [r["p_win"], r["p_wa"], r["p_wx"], r["p_wout"], r["p_wgu"], r["p_wdown"]]
    big = {}
    for nm, p, w, m, v in (("w_in", parts[0], w_in, m_w_in, v_w_in), ("w_rg_a", parts[1], w_rg_a, m_w_rg_a, v_w_rg_a),
                           ("w_rg_x", parts[2], w_rg_x, m_w_rg_x, v_w_rg_x), ("w_out", parts[3], w_out, m_w_out, v_w_out),
                           ("w_gate_up", parts[4], tr(w_gate_up), tr(m_w_gate_up), tr(v_w_gate_up)),
                           ("w_down", parts[5], w_down, m_w_down, v_w_down)):
        two_d = (-1, w.shape[-1])
        outs = _adam("adam_" + nm, p.reshape((p.shape[0],) + w.reshape(two_d).shape), w.reshape(two_d), m.reshape(two_d),
                     v.reshape(two_d))
        big[nm] = [o.reshape(w.shape) for o in outs]
    big["w_gate_up"] = [tr(o) for o in big["w_gate_up"]]

    small = jnp.concatenate([
        r["sums1"][S_SH:S_SH + 1], r["sums1"][S_SC:S_SC + 1], r["d_gt1"],
        r["sums2"][S_SH:S_SH + 1], r["sums2"][S_SC:S_SC + 1], r["d_gt2"],
        r["sums1"][S_G:S_G + 1],
        r["msums"][M_CBIAS:M_CBIAS + 1], r["msums"][M_BA:M_BA + 1], r["msums"][M_BX:M_BX + 1],
        r["msums"][M_LS:M_LS + 1],
        r["sums2"][S_G:S_G + 1], r["d_gfin"],
        r["msums"][M_WA:M_WA + 3], r["msums"][M_WB:M_WB + 4],
        jnp.broadcast_to(r["loss"][0:1, 0:1], (1, D)),
        jnp.zeros((3, D), F32)], axis=0)
    got2 = _all_gather_small("gather_small", small).reshape(NDEV, 24, D)

    rep_w = jnp.concatenate([b_ada.reshape(6, D), g_norm_mix, conv_b_bias, b_rg_a, b_rg_x, lru_lambda, g_norm_ffn,
                             g_norm_final.reshape(1, D), jnp.zeros((3, D), F32)], axis=0)
    rep_m = jnp.concatenate([m_b_ada.reshape(6, D), m_g_norm_mix, m_conv_b_bias, m_b_rg_a, m_b_rg_x, m_lru_lambda,
                             m_g_norm_ffn, m_g_norm_final.reshape(1, D), jnp.zeros((3, D), F32)], axis=0)
    rep_v = jnp.concatenate([v_b_ada.reshape(6, D), v_g_norm_mix, v_conv_b_bias, v_b_rg_a, v_b_rg_x, v_lru_lambda,
                             v_g_norm_ffn, v_g_norm_final.reshape(1, D), jnp.ones((3, D), F32)], axis=0)
    rep = _adam("adam_rep", got2[:, :16, :], rep_w, rep_m, rep_v)

    conv_parts = lax.dynamic_slice_in_dim(got2[:, 13:21, :], me * cw, cw, axis=2)
    cv_w = jnp.concatenate([conv_a_w[0], conv_b_w[0], jnp.zeros((1, cw), F32)], axis=0)
    cv_m = jnp.concatenate([m_conv_a_w[0], m_conv_b_w[0], jnp.zeros((1, cw), F32)], axis=0)
    cv_v = jnp.concatenate([v_conv_a_w[0], v_conv_b_w[0], jnp.ones((1, cw), F32)], axis=0)
    cvo = _adam("adam_conv", conv_parts, cv_w, cv_m, cv_v)

    dmod_cols = lax.dynamic_slice_in_dim(got2[:, :6, :].reshape(NDEV, 6 * D), me * ncol, ncol, axis=1)
    dmod16 = jnp.concatenate([dmod_cols, jnp.zeros((8, ncol), F32)], axis=0)
    ada = _ada_bwd(c16, dmod16, w_ada[0], m_w_ada[0], v_w_ada[0])

    loss = jnp.sum(got2[:, 20, 0])

    def pick(q):
        one = lambda i: rep[q][i:i + 1]
        return [ada[q].reshape(w_ada.shape), rep[q][0:6].reshape(b_ada.shape), one(6), big["w_in"][q],
                cvo[q][0:3].reshape(conv_a_w.shape), cvo[q][3:7].reshape(conv_b_w.shape), one(7),
                big["w_rg_a"][q], one(8), big["w_rg_x"][q], one(9), one(10), big["w_out"][q], one(11),
                big["w_gate_up"][q], big["w_down"][q], rep[q][12]]

    return (loss, r["grad_x"].reshape(x.shape), *pick(0), *pick(1), *pick(2), *pick(3))
```

```python
import functools
import math

import jax
import jax.numpy as jnp
from jax import lax
from jax.experimental import pallas as pl
from jax.experimental.pallas import tpu as pltpu

F32 = jnp.float32
BF16 = jnp.bfloat16

D = 1024
DFF = 2816
NDEV = 8
HEADS = 4
HB = D // HEADS
FB = DFF // 4
EPS = 1e-6
LRU_C = 8.0
ADAM_LR, ADAM_B1, ADAM_B2, ADAM_EPS, ADAM_WD, ADAM_STEP = 0.001, 0.9, 0.999, 1e-08, 0.01, 10

VMEM_LIMIT = 56 * 1024 * 1024
TM = 512
TMI = 1024
TMF = 256
TK = 2048
TKI = 2048
SUB = 256
TT = 256
CG = 256
MESH = pl.DeviceIdType.MESH
AXES = ("x", "y", "c")


def _cp(*sem):
    return pltpu.CompilerParams(dimension_semantics=sem, vmem_limit_bytes=VMEM_LIMIT)


def _sig(x):
    return 1.0 / (1.0 + jnp.exp(-x))


def _log_sigmoid(x):
    z = jnp.exp(-jnp.abs(x))
    u = 1.0 + z
    d = u - 1.0
    l1p = jnp.where(d == 0.0, z, jnp.log(u) * (z / jnp.where(d == 0.0, 1.0, d)))
    return -(jnp.maximum(-x, 0.0) + l1p)


def _neg_expm1(x):
    p = x * (1.0 + x * 0.5 * (1.0 + x * (1.0 / 3.0) * (1.0 + x * 0.25 * (1.0 + x * 0.2 * (1.0 + x * (1.0 / 6.0))))))
    return jnp.where(x > -0.25, -p, 1.0 - jnp.exp(x))


_GC = math.sqrt(2.0 / math.pi)


def _gelu(x):
    t = jnp.tanh(_GC * (x + 0.044715 * x * x * x))
    return 0.5 * x * (1.0 + t), t


def _dot(a, b):
    return jnp.dot(a, b, preferred_element_type=F32)


def _dot_nt(a, b):
    return lax.dot_general(a, b, (((1,), (1,)), ((), ())), preferred_element_type=F32)


def _dot_tn(a, b):
    return lax.dot_general(a, b, (((0,), (0,)), ((), ())), preferred_element_type=F32)


def _resident(shape):
    return pl.BlockSpec(shape, lambda *_: (0,) * len(shape), pipeline_mode=pl.Buffered(1))


def _sub_blocks(n_rows):
    step = min(SUB, n_rows)
    return [slice(r, r + step) for r in range(0, n_rows, step)]


def _fold8(v):
    return v[0:8] + v[8:16]


def _pj(ref, s, rows=slice(None), cols=slice(0, D)):
    return ref[rows, s * D + cols.start:s * D + cols.stop]


def _in_proj(x, mod, g_mix, w_shard, order, shards, fulls, slicers):
    t_len = x.shape[0]
    tm = min(TMI, t_len)
    ni = t_len // tm
    na = len(shards)
    cw = 7 * D // NDEV
    rc = 32

    def body(ord_ref, x_ref, mod_ref, g_ref, wsh_ref, *rest):
        ins, (proj_ref, h_ref, wfull_ref), outs = rest[:na], rest[na:na + 3], rest[na + 3:2 * na + 3]
        h_scr, w_scr, wsend, wrecv, wlocal, wout = rest[2 * na + 3:2 * na + 9]
        start, forward, finish = _ag_phases(ins, outs, slicers, *rest[2 * na + 9:])
        p, i = pl.program_id(0), pl.program_id(1)
        x_, y_, c = _my_pos()
        me, sibling = (x_, y_, c), (x_, y_, 1 - c)
        chip_at = [None, (x_, 1 - y_), (1 - x_, y_), (1 - x_, 1 - y_)]

        def cols(px, py, pc):
            return w_scr.at[:, pl.ds(pl.multiple_of((4 * px + 2 * py + pc) * cw, 128), cw)]

        def wcopy(k, block, to, from_shard=False):
            dst = cols(*block)
            return pltpu.make_async_remote_copy(src_ref=wsh_ref if from_shard else dst, dst_ref=dst,
                                                send_sem=wsend.at[k], recv_sem=wrecv.at[k], device_id=to,
                                                device_id_type=MESH)

        own_local = pltpu.make_async_copy(wsh_ref, cols(*me), wlocal)
        to_hbm = pltpu.make_async_copy(w_scr, wfull_ref, wout)

        @pl.when((p == 0) & (i == 0))
        def _():
            own_local.start()
            wcopy(0, me, sibling, True).start()
            for q in (1, 2):
                wcopy(q, me, (*chip_at[q], c), True).start()
            own_local.wait()
            wcopy(0, sibling, me).wait_recv()

        @pl.when((p == 0) & (i == ni // 2))
        def _():
            wcopy(3, me, (*chip_at[3], c), True).start()

        for q in (1, 2, 3):
            @pl.when((p == q - 1) & (i == ni - 1))
            def _():
                wcopy(q, (*chip_at[q], c), me).wait_recv()
                wcopy(3 + q, (*chip_at[q], c), sibling).start()

            @pl.when((p == q) & (i == 0))
            def _():
                wcopy(3 + q, (*chip_at[q], 1 - c), me).wait_recv()

        @pl.when((p == 1) & (i == 0))
        def _():
            start()

        @pl.when((p == NCHIP - 1) & (i == ni // 2))
        def _():
            forward()

        @pl.when((p == NCHIP - 1) & (i == 0))
        def _():
            to_hbm.start()

        gs = g_ref[...] * (1.0 + mod_ref[1:2, :])
        sh = mod_ref[0:1, :]

        wcols = pl.ds(pl.multiple_of(ord_ref[p] * (2 * cw), 128), 2 * cw)
        for sub in _sub_blocks(tm):
            for r0 in range(sub.start, sub.stop, rc):
                xv = x_ref[r0:r0 + rc, :]
                r = lax.rsqrt(jnp.mean(xv * xv, axis=-1, keepdims=True) + EPS)
                h_scr[r0:r0 + rc, :] = (xv * r * gs + sh).astype(BF16)
            proj_ref[sub, :] = _dot(h_scr[sub, :], w_scr[:, wcols]).astype(BF16)

        @pl.when(p == 0)
        def _():
            h_ref[...] = h_scr[...]

        @pl.when((p == NCHIP - 1) & (i == ni - 1))
        def _():
            wcopy(0, me, sibling, True).wait_send()
            for q in (1, 2, 3):
                wcopy(q, me, (*chip_at[q], c), True).wait_send()
                wcopy(3 + q, (*chip_at[q], c), sibling).wait_send()
            finish()
            to_hbm.wait()

    res = pl.pallas_call(
        body, name="in_proj",
        grid_spec=pltpu.PrefetchScalarGridSpec(
            num_scalar_prefetch=1, grid=(NCHIP, ni),
            in_specs=[pl.BlockSpec((tm, D), lambda p, i, o: (i, 0)),
                      pl.BlockSpec((8, D), lambda p, i, o: (0, 0)),
                      pl.BlockSpec((1, D), lambda p, i, o: (0, 0))] + [_ANY] * (1 + na),
            out_specs=[pl.BlockSpec((tm, 2 * cw), lambda p, i, o: (i, o[p])),
                       pl.BlockSpec((tm, D), lambda p, i, o: (jnp.where(p == 0, i, ni - 1), 0))]
            + [_ANY] * (1 + na),
            scratch_shapes=[pltpu.VMEM((tm, D), BF16), pltpu.VMEM((D, 7 * D), BF16),
                            pltpu.SemaphoreType.DMA((7,)), pltpu.SemaphoreType.DMA((7,)),
                            pltpu.SemaphoreType.DMA, pltpu.SemaphoreType.DMA] + _ag_sems(na)),
        out_shape=[jax.ShapeDtypeStruct((t_len, 7 * D), BF16), jax.ShapeDtypeStruct((t_len, D), BF16),
                   jax.ShapeDtypeStruct((D, 7 * D), BF16)]
        + [jax.ShapeDtypeStruct(f, sh.dtype) for f, sh in zip(fulls, shards)],
        compiler_params=_cp("arbitrary", "arbitrary"),
    )(order, x, mod, g_mix, w_shard, *shards)
    return res[0], res[1], res[2], res[3:]


P_WA, P_WB, P_CBIAS, P_BA, P_BX, P_LAM = 0, 3, 7, 8, 9, 10
SV_PLANES = SV_U, SV_YA, SV_R, SV_I, SV_A, SV_MULT = range(6)


def _lru_gates(rp, ip, ls, first_row):
    r = _sig(rp)
    ig = _sig(ip)
    la = LRU_C * r * ls
    a = jnp.exp(la)
    m2 = _neg_expm1(2.0 * la)
    mult = jnp.where(first_row, 1.0, jnp.sqrt(jnp.maximum(m2, 0.0)))
    return r, ig, la, a, m2, mult


def _shift_down(cur, prev, s, row):
    return jnp.where(row >= s, pltpu.roll(cur, s, 0), pltpu.roll(prev, s, 0))


def _shift_up(cur, nxt, s, row):
    return jnp.where(row < 8 - s, pltpu.roll(cur, 8 - s, 0), pltpu.roll(nxt, 8 - s, 0))


def _conv_fwd_rows(tt, proj_ref, prm_ref, xe, ve, u_s, ub_s, ya_s):
    row = lax.broadcasted_iota(jnp.int32, (8, CG), 0)
    w_b = [prm_ref[P_WB + k:P_WB + k + 1, :] for k in range(4)]
    w_a = [prm_ref[P_WA + k:P_WA + k + 1, :] for k in range(3)]
    bias = prm_ref[P_CBIAS:P_CBIAS + 1, :]

    def blk(ib, carry):
        r0 = pl.multiple_of(ib * 16, 16)
        rows = pl.ds(r0, 16)
        for g in range(D // CG):
            cs = slice(g * CG, (g + 1) * CG)
            x16 = _pj(proj_ref, 3, rows, cs).astype(F32)
            v16 = _pj(proj_ref, 1, rows, cs).astype(F32) * _pj(proj_ref, 2, rows, cs).astype(F32)
            xp = xe[pl.ds(r0, 8), cs]
            vp = ve[pl.ds(r0, 8), cs]
            xe[pl.ds(r0 + 8, 16), cs] = x16
            ve[pl.ds(r0 + 8, 16), cs] = v16
            us, yas = [], []
            for sb in range(2):
                xc, vc = x16[8 * sb:8 * sb + 8], v16[8 * sb:8 * sb + 8]
                u8 = bias[:, cs] + w_b[3][:, cs] * xc
                for s in (1, 2, 3):
                    u8 = u8 + w_b[3 - s][:, cs] * _shift_down(xc, xp, s, row)
                y8 = w_a[2][:, cs] * vc
                for s in (1, 2):
                    y8 = y8 + w_a[2 - s][:, cs] * _shift_down(vc, vp, s, row)
                us.append(u8)
                yas.append(y8)
                xp, vp = xc, vc
            u16 = jnp.concatenate(us, axis=0)
            u_s[rows, cs] = u16
            ub_s[rows, cs] = u16.astype(BF16)
            ya_s[rows, cs] = jnp.concatenate(yas, axis=0)
        return carry

    lax.fori_loop(0, tt // 16, blk, 0)


def _mixer_fwd(proj, prm, wa, wx, shards, fulls, slicers):
    t_len = proj.shape[0]
    tt = min(TT, t_len)
    nt = t_len // tt
    na = len(shards)

    def body(proj_ref, prm_ref, wa_ref, wx_ref, *rest):
        ins, (mg_ref, hl_ref, sv_ref), outs = rest[:na], rest[na:na + 3], rest[na + 3:2 * na + 3]
        xe, ve, hc, rp_s, ip_s, ub_s = rest[2 * na + 3:2 * na + 9]
        start, forward, finish = _ag_phases(ins, outs, slicers, *rest[2 * na + 9:])
        t = pl.program_id(0)

        @pl.when(t == 0)
        def _():
            start()
            xe[0:8, :] = jnp.zeros((8, D), F32)
            ve[0:8, :] = jnp.zeros((8, D), F32)
            hc[...] = jnp.zeros((8, D), F32)

        @pl.when(t == (3 * nt) // 4)
        def _():
            forward()

        _conv_fwd_rows(tt, proj_ref, prm_ref, xe, ve, sv_ref.at[SV_U], ub_s, sv_ref.at[SV_YA])
        xe[0:8, :] = xe[tt:tt + 8, :]
        ve[0:8, :] = ve[tt:tt + 8, :]

        ub = ub_s[...]
        for h in range(HEADS):
            cs = slice(h * HB, (h + 1) * HB)
            rp_s[:, cs] = _dot(ub[:, cs], wa_ref[h]) + prm_ref[P_BA:P_BA + 1, cs]
            ip_s[:, cs] = _dot(ub[:, cs], wx_ref[h]) + prm_ref[P_BX:P_BX + 1, cs]

        ls_all = _log_sigmoid(prm_ref[P_LAM:P_LAM + 1, :])
        row = lax.broadcasted_iota(jnp.int32, (8, CG), 0)

        def blk(i, carry):
            r0 = pl.multiple_of(i * 16, 16)
            for g in range(D // CG):
                cs = slice(g * CG, (g + 1) * CG)
                ls = ls_all[:, cs]
                hprev = hc[:, cs]
                hs = []
                for sb in range(2):
                    rr = r0 + 8 * sb
                    first = (row + (t * tt + rr)) == 0
                    r8 = pl.ds(rr, 8)
                    r, ig, _, a, _, mult = _lru_gates(rp_s[r8, cs], ip_s[r8, cs], ls, first)
                    for plane, val in ((SV_R, r), (SV_I, ig), (SV_A, a), (SV_MULT, mult)):
                        sv_ref[plane, r8, cs] = val
                    b = mult * (ig * sv_ref[SV_U, r8, cs])
                    for s in (1, 2, 4):
                        a_sh = jnp.where(row >= s, pltpu.roll(a, s, 0), 1.0)
                        b_sh = jnp.where(row >= s, pltpu.roll(b, s, 0), 0.0)
                        b = a * b_sh + b
                        a = a * a_sh
                    hv = a * hprev + b
                    hprev = jnp.broadcast_to(hv[7:8, :], hv.shape)
                    hs.append(hv)
                hc[:, cs] = hprev
                h16 = jnp.concatenate(hs, axis=0)
                rows = pl.ds(r0, 16)
                gl, _ = _gelu(_pj(proj_ref, 4, rows, cs).astype(F32))
                y_b = h16 * gl
                y_a = _pj(proj_ref, 0, rows, cs).astype(F32) * sv_ref[SV_YA, rows, cs]
                mg = (_sig(_pj(proj_ref, 5, rows, cs).astype(F32)) * y_a
                      + _sig(_pj(proj_ref, 6, rows, cs).astype(F32)) * y_b)
                mg_ref[rows, cs] = mg.astype(BF16)
                hl_ref[rows, cs] = h16.astype(BF16)
            return carry

        lax.fori_loop(0, tt // 16, blk, 0)

        @pl.when(t == nt - 1)
        def _():
            finish()

    res = pl.pallas_call(
        body, name="mixer_fwd", grid=(nt,),
        in_specs=[pl.BlockSpec((tt, 7 * D), lambda t: (t, 0)),
                  pl.BlockSpec((16, D), lambda t: (0, 0)),
                  pl.BlockSpec((HEADS, HB, HB), lambda t: (0, 0, 0)),
                  pl.BlockSpec((HEADS, HB, HB), lambda t: (0, 0, 0))] + [_ANY] * na,
        out_specs=[pl.BlockSpec((tt, D), lambda t: (t, 0)), pl.BlockSpec((tt, D), lambda t: (t, 0)),
                   pl.BlockSpec((len(SV_PLANES), tt, D), lambda t: (0, t, 0))] + [_ANY] * na,
        out_shape=[jax.ShapeDtypeStruct((t_len, D), BF16), jax.ShapeDtypeStruct((t_len, D), BF16),
                   jax.ShapeDtypeStruct((len(SV_PLANES), t_len, D), F32)]
        + [jax.ShapeDtypeStruct(f, sh.dtype) for f, sh in zip(fulls, shards)],
        scratch_shapes=[pltpu.VMEM((tt + 8, D), F32), pltpu.VMEM((tt + 8, D), F32), pltpu.VMEM((8, D), F32),
                        pltpu.VMEM((tt, D), F32), pltpu.VMEM((tt, D), F32), pltpu.VMEM((tt, D), BF16)]
        + _ag_sems(na),
        compiler_params=_cp("arbitrary"),
    )(proj, prm, wa, wx, *shards)
    return res[0], res[1], res[2], res[3:]


def _out_proj(merged, x, mod, g_ffn, w_out):
    t_len = x.shape[0]
    tm = min(TM, t_len)

    def body(mg_ref, x_ref, mod_ref, g_ref, w_ref, x1_ref, h2_ref):
        gt1 = mod_ref[2:3, :]
        gs = g_ref[...] * (1.0 + mod_ref[4:5, :])
        sh = mod_ref[3:4, :]
        for sub in _sub_blocks(tm):
            x1_ref[sub, :] = x_ref[sub, :] + gt1 * _dot(mg_ref[sub, :], w_ref[...])
            for r0 in range(sub.start, sub.stop, 16):
                x1 = x1_ref[r0:r0 + 16, :]
                r = lax.rsqrt(jnp.mean(x1 * x1, axis=-1, keepdims=True) + EPS)
                h2_ref[r0:r0 + 16, :] = (x1 * r * gs + sh).astype(BF16)

    return pl.pallas_call(
        body, name="out_proj", grid=(t_len // tm,),
        in_specs=[pl.BlockSpec((tm, D), lambda i: (i, 0)), pl.BlockSpec((tm, D), lambda i: (i, 0)),
                  pl.BlockSpec((8, D), lambda i: (0, 0)), pl.BlockSpec((1, D), lambda i: (0, 0)),
                  pl.BlockSpec((D, D), lambda i: (0, 0))],
        out_specs=[pl.BlockSpec((tm, D), lambda i: (i, 0)), pl.BlockSpec((tm, D), lambda i: (i, 0))],
        out_shape=[jax.ShapeDtypeStruct((t_len, D), F32), jax.ShapeDtypeStruct((t_len, D), BF16)],
        compiler_params=_cp("parallel"),
    )(merged, x, mod, g_ffn, w_out)


def _ffn_fwd(h2, x1, target, mod, g_fin, w_gu, w_down):
    t_len = x1.shape[0]
    tm = min(TMF, t_len)

    def body(h2_ref, x1_ref, tg_ref, mod_ref, g_ref, wgu_ref, wd_ref, gu_ref, dx2_ref, dx2b_ref, loss_ref, dg_ref,
             acc, gate_s, up_s, act_s):
        @pl.when(pl.program_id(0) == 0)
        def _():
            loss_ref[...] = jnp.zeros_like(loss_ref)
            dg_ref[...] = jnp.zeros_like(dg_ref)

        hb = h2_ref[...]
        gate_s[0] = _dot_nt(hb, wgu_ref[0, 0])
        up_s[0] = _dot_nt(hb, wgu_ref[1, 0])
        for j in range(4):
            if j < 3:
                gate_s[(j + 1) % 2] = _dot_nt(hb, wgu_ref[0, j + 1])
                up_s[(j + 1) % 2] = _dot_nt(hb, wgu_ref[1, j + 1])
            for r0 in range(0, tm, 16):
                rows = slice(r0, r0 + 16)
                gate = gate_s[j % 2, rows, :]
                up = up_s[j % 2, rows, :]
                gu_ref[0, j, rows, :] = gate.astype(BF16)
                gu_ref[1, j, rows, :] = up.astype(BF16)
                act_s[j % 2, rows, :] = (gate * _sig(gate) * up).astype(BF16)
            part = _dot(act_s[j % 2], wd_ref[j * FB:(j + 1) * FB, :])
            if j == 0:
                acc[...] = part
            else:
                acc[...] += part

        gt2 = mod_ref[5:6, :]
        gf = g_ref[...]

        s_loss = s_dg = jnp.zeros((8, D), F32)
        for r0 in range(0, tm, 16):
            rows = slice(r0, r0 + 16)
            x2 = x1_ref[rows, :] + gt2 * acc[rows, :]
            r = lax.rsqrt(jnp.mean(x2 * x2, axis=-1, keepdims=True) + EPS)
            xn = x2 * r
            diff = xn * gf - tg_ref[rows, :]
            dy = diff * (1.0 / D)
            dxn = dy * gf
            dx2 = r * (dxn - xn * jnp.mean(dxn * xn, axis=-1, keepdims=True))
            dx2_ref[rows, :] = dx2
            dx2b_ref[rows, :] = dx2.astype(BF16)
            s_loss, s_dg = s_loss + _fold8(diff * diff), s_dg + _fold8(dy * xn)
        loss_ref[...] += jnp.sum(s_loss) * (0.5 / D)
        dg_ref[...] += jnp.sum(s_dg, axis=0, keepdims=True)

    row = pl.BlockSpec((tm, D), lambda i: (i, 0))
    return pl.pallas_call(
        body, name="ffn_fwd", grid=(t_len // tm,),
        in_specs=[row, row, row, pl.BlockSpec((8, D), lambda i: (0, 0)), pl.BlockSpec((1, D), lambda i: (0, 0)),
                  _resident((2, 4, FB, D)), _resident((DFF, D))],
        out_specs=[pl.BlockSpec((2, 4, tm, FB), lambda i: (0, 0, i, 0)), row, row,
                   pl.BlockSpec((1, 128), lambda i: (0, 0)), pl.BlockSpec((1, D), lambda i: (0, 0))],
        out_shape=[jax.ShapeDtypeStruct((2, 4, t_len, FB), BF16), jax.ShapeDtypeStruct((t_len, D), F32),
                   jax.ShapeDtypeStruct((t_len, D), BF16),
                   jax.ShapeDtypeStruct((1, 128), F32), jax.ShapeDtypeStruct((1, D), F32)],
        scratch_shapes=[pltpu.VMEM((tm, D), F32), pltpu.VMEM((2, tm, FB), F32), pltpu.VMEM((2, tm, FB), F32),
                        pltpu.VMEM((2, tm, FB), BF16)],
        compiler_params=_cp("arbitrary"),
    )(h2, x1, target, mod, g_fin, w_gu, w_down)


S_SH, S_SC, S_G = 0, 1, 2


def _norm_bwd_rows(span, sums, dh_ref, x_ref, dres_ref, scale, gain, write):
    gs = 1.0 + scale
    s_sh, s_sc, s_g = sums
    for r0 in range(span.start, span.stop, 16):
        rows = slice(r0, r0 + 16)
        dh = dh_ref[rows, :]
        xv = x_ref[rows, :]
        r = lax.rsqrt(jnp.mean(xv * xv, axis=-1, keepdims=True) + EPS)
        xn = xv * r
        dhn = dh * gs
        dxn = dhn * gain
        write(rows, dres_ref[rows, :] + r * (dxn - xn * jnp.mean(dxn * xn, axis=-1, keepdims=True)))
        s_sh, s_sc, s_g = s_sh + _fold8(dh), s_sc + _fold8(dh * (xn * gain)), s_g + _fold8(dhn * xn)
    return s_sh, s_sc, s_g


def _add_norm_sums(sums_ref, sums):
    for dst, s in zip((S_SH, S_SC, S_G), sums):
        sums_ref[dst:dst + 1, :] += jnp.sum(s, axis=0, keepdims=True)


def _ffn_bwd(dx2, gu, x1, mod, g_ffn, w_gu, w_down, w_out):
    t_len = x1.shape[0]
    tm = min(TMF, t_len)

    def body(dx2_ref, gu_ref, x1_ref, mod_ref, g_ref, wgu_ref, wd_ref, wo_ref,
             dgu_ref, act_ref, dx1_ref, dx1b_ref, dmg_ref, sums_ref, acc, dmo, dact_s):
        @pl.when(pl.program_id(0) == 0)
        def _():
            sums_ref[...] = jnp.zeros_like(sums_ref)

        dffn = (dx2_ref[...] * mod_ref[5:6, :]).astype(BF16)
        dact_s[0] = _dot_nt(dffn, wd_ref[0:FB, :])
        for j in range(4):
            if j < 3:
                dact_s[(j + 1) % 2] = _dot_nt(dffn, wd_ref[(j + 1) * FB:(j + 2) * FB, :])
            for r0 in range(0, tm, 16):
                rows = slice(r0, r0 + 16)
                dact = dact_s[j % 2, rows, :]
                gate = gu_ref[0, j, rows, :].astype(F32)
                up = gu_ref[1, j, rows, :].astype(F32)
                sg = _sig(gate)
                silu = gate * sg
                act_ref[j, rows, :] = (silu * up).astype(BF16)
                dgu_ref[0, j, rows, :] = (dact * up * (sg * (1.0 + gate * (1.0 - sg)))).astype(BF16)
                dgu_ref[1, j, rows, :] = (dact * silu).astype(BF16)
            part = _dot(dgu_ref[0, j], wgu_ref[0, j]) + _dot(dgu_ref[1, j], wgu_ref[1, j])
            if j == 0:
                acc[...] = part
            else:
                acc[...] += part

        gt1 = mod_ref[2:3, :]

        def write(rows, dx1):
            dx1_ref[rows, :] = dx1
            dx1b_ref[rows, :] = dx1.astype(BF16)
            dmo[rows, :] = (dx1 * gt1).astype(BF16)

        zero = jnp.zeros((8, D), F32)
        sums = (zero, zero, zero)
        for sub in (slice(0, tm // 2), slice(tm // 2, tm)):
            sums = _norm_bwd_rows(sub, sums, acc, x1_ref, dx2_ref, mod_ref[4:5, :], g_ref[...], write)
            dmg_ref[sub, :] = _dot_nt(dmo[sub, :], wo_ref[...]).astype(BF16)
        _add_norm_sums(sums_ref, sums)

    row = pl.BlockSpec((tm, D), lambda i: (i, 0))
    return pl.pallas_call(
        body, name="ffn_bwd", grid=(t_len // tm,),
        in_specs=[row, pl.BlockSpec((2, 4, tm, FB), lambda i: (0, 0, i, 0)), row,
                  pl.BlockSpec((8, D), lambda i: (0, 0)), pl.BlockSpec((1, D), lambda i: (0, 0)),
                  _resident((2, 4, FB, D)), _resident((DFF, D)), _resident((D, D))],
        out_specs=[pl.BlockSpec((2, 4, tm, FB), lambda i: (0, 0, i, 0)),
                   pl.BlockSpec((4, tm, FB), lambda i: (0, i, 0)), row, row, row,
                   pl.BlockSpec((8, D), lambda i: (0, 0))],
        out_shape=[jax.ShapeDtypeStruct((2, 4, t_len, FB), BF16), jax.ShapeDtypeStruct((4, t_len, FB), BF16),
                   jax.ShapeDtypeStruct((t_len, D), F32), jax.ShapeDtypeStruct((t_len, D), BF16),
                   jax.ShapeDtypeStruct((t_len, D), BF16), jax.ShapeDtypeStruct((8, D), F32)],
        scratch_shapes=[pltpu.VMEM((tm, D), F32), pltpu.VMEM((tm, D), BF16), pltpu.VMEM((2, tm, FB), F32)],
        compiler_params=_cp("arbitrary"),
    )(dx2, gu, x1, mod, g_ffn, w_gu, w_down, w_out)


def _my_pos():
    return lax.axis_index("x"), lax.axis_index("y"), lax.axis_index("c")


def _my_index():
    x, y, c = _my_pos()
    return 4 * x + 2 * y + c


def _device_of(b):
    return (b >> 2) & 1, (b >> 1) & 1, b & 1


def _rs_send(src, parts_ref, b, send_sems, recv_sems, local_sem):
    me = _my_index()
    dst = parts_ref.at[me]

    @pl.when(b == me)
    def _():
        pltpu.make_async_copy(src, dst, local_sem).start()

    @pl.when(b != me)
    def _():
        pltpu.make_async_remote_copy(src_ref=src, dst_ref=dst, send_sem=send_sems.at[b], recv_sem=recv_sems.at[me],
                                     device_id=_device_of(b), device_id_type=MESH).start()


def _rs_finish(src_of, parts_ref, send_sems, recv_sems, local_sem):
    me = _my_index()
    for s in range(NDEV):
        @pl.when(s != me)
        def _():
            cp = pltpu.make_async_remote_copy(src_ref=src_of(s), dst_ref=parts_ref.at[s], send_sem=send_sems.at[s],
                                              recv_sem=recv_sems.at[s], device_id=_device_of(s), device_id_type=MESH)
            cp.wait_send()
            cp.wait_recv()

        @pl.when(s == me)
        def _():
            pltpu.make_async_copy(src_of(s), parts_ref.at[s], local_sem).wait()


_RS_SEMS = [pltpu.SemaphoreType.DMA((NDEV,)), pltpu.SemaphoreType.DMA((NDEV,)), pltpu.SemaphoreType.DMA]
_ANY = pl.BlockSpec(memory_space=pl.ANY)


def _xor_order(me, n):
    return (me ^ (n - 1 - jnp.arange(n, dtype=jnp.int32))).astype(jnp.int32)


NCHIP = NDEV // 2


def _rs2_scratch(half_shape):
    blocks = lambda *lead: pltpu.VMEM(lead + tuple(half_shape), BF16)
    return [blocks(NCHIP, 2), blocks(NCHIP)] + [pltpu.SemaphoreType.DMA((NCHIP,))] * 4 + [pltpu.SemaphoreType.DMA]


def _rs2_to_sibling(q, rs):
    stage, from_sib, d_send, d_recv = rs[:4]
    x, y, c = _my_pos()
    pltpu.make_async_remote_copy(src_ref=stage.at[q, 1 - c], dst_ref=from_sib.at[q], send_sem=d_send.at[q],
                                 recv_sem=d_recv.at[q], device_id=(x, y, 1 - c), device_id_type=MESH).start()


def _rs2_forward(q, parts_ref, rs):
    stage, chip_sum, d_send, d_recv, i_send, i_recv, local_sem = rs
    x, y, c = _my_pos()
    my_chip = 2 * x + y
    pltpu.make_async_remote_copy(src_ref=stage.at[q, c], dst_ref=chip_sum.at[q], send_sem=d_send.at[q],
                                 recv_sem=d_recv.at[q], device_id=(x, y, 1 - c), device_id_type=MESH).wait_recv()
    chip_sum[q] = (stage[q, c].astype(F32) + chip_sum[q].astype(F32)).astype(BF16)

    @pl.when(q == my_chip)
    def _():
        pltpu.make_async_copy(chip_sum.at[q], parts_ref.at[my_chip], local_sem).start()

    @pl.when(q != my_chip)
    def _():
        pltpu.make_async_remote_copy(src_ref=chip_sum.at[q], dst_ref=parts_ref.at[my_chip], send_sem=i_send.at[q],
                                     recv_sem=i_recv.at[my_chip], device_id=((q >> 1) & 1, q & 1, c),
                                     device_id_type=MESH).start()


def _rs2_finish(parts_ref, rs):
    stage, chip_sum, d_send, d_recv, i_send, i_recv, local_sem = rs
    x, y, c = _my_pos()
    my_chip = 2 * x + y
    for q in range(NCHIP):
        pltpu.make_async_remote_copy(src_ref=stage.at[q, 1 - c], dst_ref=chip_sum.at[q], send_sem=d_send.at[q],
                                     recv_sem=d_recv.at[q], device_id=(x, y, 1 - c), device_id_type=MESH).wait_send()

        @pl.when(q != my_chip)
        def _():
            cp = pltpu.make_async_remote_copy(src_ref=chip_sum.at[q], dst_ref=parts_ref.at[q], send_sem=i_send.at[q],
                                              recv_sem=i_recv.at[q], device_id=((q >> 1) & 1, q & 1, c),
                                              device_id_type=MESH)
            cp.wait_send()
            cp.wait_recv()

        @pl.when(q == my_chip)
        def _():
            pltpu.make_async_copy(chip_sum.at[q], parts_ref.at[q], local_sem).wait()


def _gu_wgrad(h2, dgu, order):
    t_len = h2.shape[0]
    tk = min(TK, t_len)
    nk = t_len // tk

    def body(ord_ref, h_ref, d_ref, parts_ref, acc, *rs):
        p, k = pl.program_id(0), pl.program_id(1)

        @pl.when(k == 0)
        def _():
            acc[...] = jnp.zeros_like(acc)

        hb = h_ref[...]
        for half in range(2):
            acc[half] += _dot_tn(d_ref[0, half], hb)

        @pl.when(k == nk - 1)
        def _():
            q = ord_ref[p]
            rs[0][q] = acc[...].astype(BF16)
            _rs2_to_sibling(q, rs)

        @pl.when((k == nk - 1) & (p > 0))
        def _():
            _rs2_forward(ord_ref[p - 1], parts_ref, rs)

        @pl.when((p == NCHIP - 1) & (k == nk - 1))
        def _():
            _rs2_forward(ord_ref[p], parts_ref, rs)
            _rs2_finish(parts_ref, rs)

    return pl.pallas_call(
        body, name="gu_wgrad",
        grid_spec=pltpu.PrefetchScalarGridSpec(
            num_scalar_prefetch=1, grid=(NCHIP, nk),
            in_specs=[pl.BlockSpec((tk, D), lambda p, k, o: (k, 0)),
                      pl.BlockSpec((1, 2, tk, FB), lambda p, k, o: (o[p], 0, k, 0))],
            out_specs=_ANY,
            scratch_shapes=[pltpu.VMEM((2, FB, D), F32)] + _rs2_scratch((FB, D))),
        out_shape=jax.ShapeDtypeStruct((NCHIP, FB, D), BF16),
        compiler_params=_cp("arbitrary", "arbitrary"),
    )(order, h2, dgu.reshape(NCHIP, 2, t_len, FB))


def _scaled_wgrad(name, a, dx, w, gate_row, mod, order):
    nb, t_len, kb = a.shape
    tk = min(TK, t_len)
    nk = t_len // tk
    cpb = NCHIP // nb
    rows = kb // (2 * cpb)

    def body(ord_ref, a_ref, dx_ref, w_ref, mod_ref, parts_ref, dg_ref, acc, *rs):
        p, k = pl.program_id(0), pl.program_id(1)
        j = ord_ref[p]

        @pl.when((p == 0) & (k == 0))
        def _():
            dg_ref[...] = jnp.zeros_like(dg_ref)

        @pl.when(k == 0)
        def _():
            acc[...] = jnp.zeros_like(acc)

        acc[...] += _dot_tn(a_ref[0], dx_ref[...])

        @pl.when(k == nk - 1)
        def _():
            z = acc[...]
            zg = (z * mod_ref[gate_row:gate_row + 1, :]).astype(BF16)
            dg_ref[0:1, :] += jnp.sum(z * w_ref[...].astype(F32), axis=0, keepdims=True)
            for i in range(cpb):
                q = j * cpb + i
                for half in range(2):
                    rs[0][q, half] = zg[(2 * i + half) * rows:(2 * i + half + 1) * rows]
                _rs2_to_sibling(q, rs)

        if cpb == 1:
            @pl.when((k == nk - 1) & (p > 0))
            def _():
                _rs2_forward(ord_ref[p - 1], parts_ref, rs)

        @pl.when((p == nb - 1) & (k == nk - 1))
        def _():
            for i in range(cpb):
                _rs2_forward(j * cpb + i, parts_ref, rs)
            _rs2_finish(parts_ref, rs)

    return pl.pallas_call(
        body, name=name,
        grid_spec=pltpu.PrefetchScalarGridSpec(
            num_scalar_prefetch=1, grid=(nb, nk),
            in_specs=[pl.BlockSpec((1, tk, kb), lambda p, k, o: (o[p], k, 0)),
                      pl.BlockSpec((tk, D), lambda p, k, o: (k, 0)),
                      pl.BlockSpec((kb, D), lambda p, k, o: (o[p], 0)),
                      pl.BlockSpec((8, D), lambda p, k, o: (0, 0))],
            out_specs=[_ANY, pl.BlockSpec((8, D), lambda p, k, o: (0, 0))],
            scratch_shapes=[pltpu.VMEM((kb, D), F32)] + _rs2_scratch((rows, D))),
        out_shape=[jax.ShapeDtypeStruct((NCHIP, rows, D), BF16), jax.ShapeDtypeStruct((8, D), F32)],
        compiler_params=_cp("arbitrary", "arbitrary"),
    )(order, a, dx, w, mod)


M_WA, M_WB, M_CBIAS, M_BA, M_BX, M_LS = 0, 3, 7, 8, 9, 10


def _conv_bwd_rows(tt, proj_ref, prm_ref, xe, ve, due, dye, dp_ref, acc8):
    row = lax.broadcasted_iota(jnp.int32, (8, CG), 0)
    w_b = [prm_ref[P_WB + k:P_WB + k + 1, :] for k in range(4)]
    w_a = [prm_ref[P_WA + k:P_WA + k + 1, :] for k in range(3)]

    def blk(ib, carry):
        r0 = pl.multiple_of(ib * 16, 16)
        rows = pl.ds(r0, 16)
        for g in range(D // CG):
            cs = slice(g * CG, (g + 1) * CG)
            du16, du_after = due[rows, cs], due[pl.ds(r0 + 16, 8), cs]
            dy16, dy_after = dye[rows, cs], dye[pl.ds(r0 + 16, 8), cs]
            cc16 = _pj(proj_ref, 1, rows, cs).astype(F32)
            cx16 = _pj(proj_ref, 2, rows, cs).astype(F32)
            x16 = _pj(proj_ref, 3, rows, cs).astype(F32)
            v16 = cc16 * cx16
            xp, vp = xe[pl.ds(r0, 8), cs], ve[pl.ds(r0, 8), cs]
            xe[pl.ds(r0 + 16, 8), cs] = x16[8:16]
            ve[pl.ds(r0 + 16, 8), cs] = v16[8:16]
            acc = [acc8[8 * k:8 * k + 8, cs] for k in range(8)]
            drx, dv = [], []
            for sb in range(2):
                lo = slice(8 * sb, 8 * sb + 8)
                duc, dyc, xc, vc = du16[lo], dy16[lo], x16[lo], v16[lo]
                du_n = du16[8:16] if sb == 0 else du_after
                dy_n = dy16[8:16] if sb == 0 else dy_after
                acc[0] = acc[0] + duc
                acc[4] = acc[4] + duc * xc
                d8 = w_b[3][:, cs] * duc
                for s in (1, 2, 3):
                    acc[4 - s] = acc[4 - s] + duc * _shift_down(xc, xp, s, row)
                    d8 = d8 + w_b[3 - s][:, cs] * _shift_up(duc, du_n, s, row)
                acc[7] = acc[7] + dyc * vc
                e8 = w_a[2][:, cs] * dyc
                for s in (1, 2):
                    acc[7 - s] = acc[7 - s] + dyc * _shift_down(vc, vp, s, row)
                    e8 = e8 + w_a[2 - s][:, cs] * _shift_up(dyc, dy_n, s, row)
                drx.append(d8)
                dv.append(e8)
                xp, vp = xc, vc
            for k in range(8):
                acc8[8 * k:8 * k + 8, cs] = acc[k]
            dv16 = jnp.concatenate(dv, axis=0)
            col = lambda s: slice(s * D + g * CG, s * D + (g + 1) * CG)
            dp_ref[rows, col(3)] = jnp.concatenate(drx, axis=0).astype(BF16)
            dp_ref[rows, col(1)] = (dv16 * cx16).astype(BF16)
            dp_ref[rows, col(2)] = (dv16 * cc16).astype(BF16)
        return carry

    lax.fori_loop(0, tt // 16, blk, 0)


def _mixer_bwd(proj, hl, sv, dmg, prm, wa, wx):
    t_len = proj.shape[0]
    tt = min(TT, t_len)
    nt = t_len // tt
    hb8 = tt // 8

    def rev(i):
        return nt - 1 - i

    def halo(i):
        return jnp.maximum(rev(i) * hb8 - 1, 0)

    def body(proj_ref, ph_ref, hl_ref, hh_ref, sv_ref, dmg_ref, prm_ref, wa_ref, wx_ref,
             dp_ref, sums_ref, gwa_ref, gwx_ref,
             xe, ve, he, due, dye, drp_s, dip_s, an, gn, acc8):
        i = pl.program_id(0)
        t = rev(i)

        @pl.when(i == 0)
        def _():
            sums_ref[...] = jnp.zeros_like(sums_ref)
            gwa_ref[...] = jnp.zeros_like(gwa_ref)
            gwx_ref[...] = jnp.zeros_like(gwx_ref)
            due[tt:tt + 8, :] = jnp.zeros((8, D), F32)
            dye[tt:tt + 8, :] = jnp.zeros((8, D), F32)
            an[...] = jnp.zeros((8, D), F32)
            gn[...] = jnp.zeros((8, D), F32)

        live = (t > 0).astype(F32)
        xe[0:8, :] = _pj(ph_ref, 3).astype(F32) * live
        ve[0:8, :] = _pj(ph_ref, 1).astype(F32) * _pj(ph_ref, 2).astype(F32) * live
        he[0:8, :] = hh_ref[...].astype(F32) * live
        he[8:8 + tt, :] = hl_ref[...].astype(F32)

        ls_all = _log_sigmoid(prm_ref[P_LAM:P_LAM + 1, :])
        row = lax.broadcasted_iota(jnp.int32, (8, CG), 0)
        nblk = tt // 16

        def blk(ib, carry):
            r0 = pl.multiple_of((nblk - 1 - ib) * 16, 16)
            rows = pl.ds(r0, 16)
            for g in range(D // CG):
                cs = slice(g * CG, (g + 1) * CG)
                ls = ls_all[:, cs]
                dm = dmg_ref[rows, cs].astype(F32)
                cb = _pj(proj_ref, 0, rows, cs).astype(F32)
                rg = _pj(proj_ref, 4, rows, cs).astype(F32)
                sga = _sig(_pj(proj_ref, 5, rows, cs).astype(F32))
                sgb = _sig(_pj(proj_ref, 6, rows, cs).astype(F32))
                ya0 = sv_ref[SV_YA, rows, cs]
                h16 = he[pl.ds(r0 + 8, 16), cs]
                gl, th = _gelu(rg)
                dgl = 0.5 * (1.0 + th) + 0.5 * rg * (1.0 - th * th) * (_GC * (1.0 + 3.0 * 0.044715 * rg * rg))
                y_a = cb * ya0
                y_b = h16 * gl
                dy_a = dm * sga
                dy_b = dm * sgb
                col = lambda s: slice(s * D + g * CG, s * D + (g + 1) * CG)
                dp_ref[rows, col(5)] = (dm * y_a * sga * (1.0 - sga)).astype(BF16)
                dp_ref[rows, col(6)] = (dm * y_b * sgb * (1.0 - sgb)).astype(BF16)
                dp_ref[rows, col(4)] = (dy_b * h16 * dgl).astype(BF16)
                dp_ref[rows, col(0)] = (dy_a * ya0).astype(BF16)
                dye[rows, cs] = dy_a * cb
                dh16 = dy_b * gl

                a_next = an[:, cs]
                g_next = gn[:, cs]
                s_ba = jnp.zeros((8, CG), F32)
                s_bx = jnp.zeros((8, CG), F32)
                s_ls = jnp.zeros((8, CG), F32)
                for sb in (1, 0):
                    rr = r0 + 8 * sb
                    first = (row + (t * tt + rr)) == 0
                    r8 = pl.ds(rr, 8)
                    uu, r, ig, a, mult = (sv_ref[pln, r8, cs] for pln in (SV_U, SV_R, SV_I, SV_A, SV_MULT))
                    ca = jnp.where(row < 7, pltpu.roll(a, 7, 0), a_next)
                    cb_ = dh16[8 * sb:8 * sb + 8, :]
                    for s in (1, 2, 4):
                        a_sh = jnp.where(row < 8 - s, pltpu.roll(ca, 8 - s, 0), 1.0)
                        b_sh = jnp.where(row < 8 - s, pltpu.roll(cb_, 8 - s, 0), 0.0)
                        cb_ = ca * b_sh + cb_
                        ca = ca * a_sh
                    gv = ca * g_next + cb_
                    g_next = jnp.broadcast_to(gv[0:1, :], gv.shape)
                    a_next = jnp.broadcast_to(a[0:1, :], a.shape)
                    hprev = jnp.where(row >= 1, pltpu.roll(he[pl.ds(rr + 8, 8), cs], 1, 0),
                                      pltpu.roll(he[pl.ds(rr, 8), cs], 1, 0))
                    da = gv * hprev
                    dmult = jnp.where(first, 0.0, gv * ig * uu)
                    dla = da * a + jnp.where(mult > 0.0, dmult * (-(a * a) / mult), 0.0)
                    drp = dla * (LRU_C * ls) * r * (1.0 - r)
                    dip = gv * mult * uu * ig * (1.0 - ig)
                    s_ls = s_ls + dla * (LRU_C * r)
                    s_ba = s_ba + drp
                    s_bx = s_bx + dip
                    drp_s[pl.ds(rr, 8), cs] = drp
                    dip_s[pl.ds(rr, 8), cs] = dip
                    due[pl.ds(rr, 8), cs] = gv * mult * ig
                an[:, cs] = a_next
                gn[:, cs] = g_next
                sums_ref[M_BA:M_BA + 1, cs] += jnp.sum(s_ba, axis=0, keepdims=True)
                sums_ref[M_BX:M_BX + 1, cs] += jnp.sum(s_bx, axis=0, keepdims=True)
                sums_ref[M_LS:M_LS + 1, cs] += jnp.sum(s_ls, axis=0, keepdims=True)
            return carry

        lax.fori_loop(0, nblk, blk, 0)

        drp_b = drp_s[...].astype(BF16)
        dip_b = dip_s[...].astype(BF16)
        ub = sv_ref[SV_U].astype(BF16)
        for h in range(HEADS):
            cs = slice(h * HB, (h + 1) * HB)
            due[0:tt, cs] += _dot_nt(drp_b[:, cs], wa_ref[h]) + _dot_nt(dip_b[:, cs], wx_ref[h])
            gwa_ref[h] += _dot_tn(ub[:, cs], drp_b[:, cs])
            gwx_ref[h] += _dot_tn(ub[:, cs], dip_b[:, cs])

        acc8[...] = jnp.zeros_like(acc8)
        _conv_bwd_rows(tt, proj_ref, prm_ref, xe, ve, due, dye, dp_ref, acc8)
        for k, dst in enumerate([M_CBIAS] + [M_WB + k for k in range(4)] + [M_WA + k for k in range(3)]):
            sums_ref[dst:dst + 1, :] += jnp.sum(acc8[8 * k:8 * k + 8, :], axis=0, keepdims=True)
        due[tt:tt + 8, :] = due[0:8, :]
        dye[tt:tt + 8, :] = dye[0:8, :]

        @pl.when(i == nt - 1)
        def _():
            sums_ref[M_LS:M_LS + 1, :] = sums_ref[M_LS:M_LS + 1, :] * _sig(-prm_ref[P_LAM:P_LAM + 1, :])

    big = lambda: pltpu.VMEM((tt + 8, D), F32)
    tile = lambda: pltpu.VMEM((tt, D), F32)
    return pl.pallas_call(
        body, name="mixer_bwd", grid=(nt,),
        in_specs=[pl.BlockSpec((tt, 7 * D), lambda i: (rev(i), 0)),
                  pl.BlockSpec((8, 7 * D), lambda i: (halo(i), 0)),
                  pl.BlockSpec((tt, D), lambda i: (rev(i), 0)),
                  pl.BlockSpec((8, D), lambda i: (halo(i), 0)),
                  pl.BlockSpec((len(SV_PLANES), tt, D), lambda i: (0, rev(i), 0)),
                  pl.BlockSpec((tt, D), lambda i: (rev(i), 0)),
                  pl.BlockSpec((16, D), lambda i: (0, 0)),
                  pl.BlockSpec((HEADS, HB, HB), lambda i: (0, 0, 0)),
                  pl.BlockSpec((HEADS, HB, HB), lambda i: (0, 0, 0))],
        out_specs=[pl.BlockSpec((tt, 7 * D), lambda i: (rev(i), 0)),
                   pl.BlockSpec((16, D), lambda i: (0, 0)),
                   pl.BlockSpec((HEADS, HB, HB), lambda i: (0, 0, 0)),
                   pl.BlockSpec((HEADS, HB, HB), lambda i: (0, 0, 0))],
        out_shape=[jax.ShapeDtypeStruct((t_len, 7 * D), BF16), jax.ShapeDtypeStruct((16, D), F32),
                   jax.ShapeDtypeStruct((HEADS, HB, HB), F32), jax.ShapeDtypeStruct((HEADS, HB, HB), F32)],
        scratch_shapes=[big(), big(), big(), big(), big(), tile(), tile(),
                        pltpu.VMEM((8, D), F32), pltpu.VMEM((8, D), F32), pltpu.VMEM((64, D), F32)],
        compiler_params=_cp("arbitrary"),
    )(proj, proj, hl, hl, sv, dmg, prm, wa, wx)


def _in_proj_bwd(dproj, w_in, x, dx1, mod, g_mix):
    t_len = x.shape[0]
    tm = min(TM, t_len)

    def body(dp_ref, w_ref, x_ref, dx1_ref, mod_ref, g_ref, gx_ref, sums_ref, acc):
        @pl.when(pl.program_id(0) == 0)
        def _():
            sums_ref[...] = jnp.zeros_like(sums_ref)

        def write(rows, dx):
            gx_ref[rows, :] = dx

        zero = jnp.zeros((8, D), F32)
        sums = (zero, zero, zero)
        for sub in _sub_blocks(tm):
            acc[sub, :] = _dot_nt(dp_ref[sub, :], w_ref[...])
            sums = _norm_bwd_rows(sub, sums, acc, x_ref, dx1_ref, mod_ref[1:2, :], g_ref[...], write)
        _add_norm_sums(sums_ref, sums)

    return pl.pallas_call(
        body, name="in_proj_bwd", grid=(t_len // tm,),
        in_specs=[pl.BlockSpec((tm, 7 * D), lambda i: (i, 0)),
                  _resident((D, 7 * D)),
                  pl.BlockSpec((tm, D), lambda i: (i, 0)), pl.BlockSpec((tm, D), lambda i: (i, 0)),
                  pl.BlockSpec((8, D), lambda i: (0, 0)), pl.BlockSpec((1, D), lambda i: (0, 0))],
        out_specs=[pl.BlockSpec((tm, D), lambda i: (i, 0)), pl.BlockSpec((8, D), lambda i: (0, 0))],
        out_shape=[jax.ShapeDtypeStruct((t_len, D), F32), jax.ShapeDtypeStruct((8, D), F32)],
        scratch_shapes=[pltpu.VMEM((tm, D), F32)],
        compiler_params=_cp("arbitrary"),
    )(dproj, w_in, x, dx1, mod, g_mix)


def _in_wgrad(h, dproj, g_wa, g_wx, order):
    t_len = h.shape[0]
    tk = min(TKI, t_len)
    nk = t_len // tk
    cw = 7 * D // NDEV
    hr = HB // NDEV

    def body(ord_ref, h_ref, d_ref, ga_ref, gx_ref, parts_ref, pa_ref, px_ref, acc, *scr):
        rs, sems = scr[:-6], scr[-6:]
        p, k = pl.program_id(0), pl.program_id(1)

        def head_rows(ref):
            return lambda s: ref.at[:, pl.ds(s * hr, hr), :]

        @pl.when((p == 0) & (k == 0))
        def _():
            for s in range(NDEV):
                _rs_send(head_rows(ga_ref)(s), pa_ref, s, *sems[0:3])
                _rs_send(head_rows(gx_ref)(s), px_ref, s, *sems[3:6])

        @pl.when(k == 0)
        def _():
            acc[...] = jnp.zeros_like(acc)

        acc[...] += _dot_tn(h_ref[...], d_ref[...])

        @pl.when(k == nk - 1)
        def _():
            q = ord_ref[p]
            for half in range(2):
                rs[0][q, half] = acc[:, half * cw:(half + 1) * cw].astype(BF16)
            _rs2_to_sibling(q, rs)

        @pl.when((k == nk - 1) & (p > 0))
        def _():
            _rs2_forward(ord_ref[p - 1], parts_ref, rs)

        @pl.when((p == NCHIP - 1) & (k == nk - 1))
        def _():
            _rs2_forward(ord_ref[p], parts_ref, rs)
            _rs2_finish(parts_ref, rs)
            _rs_finish(head_rows(ga_ref), pa_ref, *sems[0:3])
            _rs_finish(head_rows(gx_ref), px_ref, *sems[3:6])

    return pl.pallas_call(
        body, name="in_wgrad",
        grid_spec=pltpu.PrefetchScalarGridSpec(
            num_scalar_prefetch=1, grid=(NCHIP, nk),
            in_specs=[pl.BlockSpec((tk, D), lambda p, k, o: (k, 0)),
                      pl.BlockSpec((tk, 2 * cw), lambda p, k, o: (k, o[p])), _ANY, _ANY],
            out_specs=[_ANY, _ANY, _ANY],
            scratch_shapes=[pltpu.VMEM((D, 2 * cw), F32)] + _rs2_scratch((D, cw)) + _RS_SEMS * 2),
        out_shape=[jax.ShapeDtypeStruct((NCHIP, D, cw), BF16), jax.ShapeDtypeStruct((NDEV, HEADS, hr, HB), F32),
                   jax.ShapeDtypeStruct((NDEV, HEADS, hr, HB), F32)],
        compiler_params=_cp("arbitrary", "arbitrary"),
    )(order, h, dproj, g_wa, g_wx)


def _ada_fwd(c_all, w_ada, b_cols):
    def body(c_ref, w_ref, b_ref, o_ref):
        cv = c_ref[...]
        o_ref[...] = _dot((cv * _sig(cv)).astype(BF16), w_ref[...].astype(BF16)) + b_ref[...]

    return pl.pallas_call(body, name="ada_fwd", out_shape=jax.ShapeDtypeStruct((16, w_ada.shape[1]), F32),
                          compiler_params=_cp())(c_all, w_ada, b_cols)


def _adam_math(w, g, m, v):
    m = ADAM_B1 * m + (1.0 - ADAM_B1) * g
    v = ADAM_B2 * v + (1.0 - ADAM_B2) * (g * g)
    m_hat = m / (1.0 - ADAM_B1 ** ADAM_STEP)
    v_hat = v / (1.0 - ADAM_B2 ** ADAM_STEP)
    delta = -ADAM_LR * (m_hat / (jnp.sqrt(v_hat) + ADAM_EPS) + ADAM_WD * w)
    return delta, m, v


def _ada_bwd(c_all, dmod_cols, w, m, v):
    rb = 256
    n = w.shape[1]
    nrow = c_all.shape[0]

    def body(c_ref, d_ref, w_ref, m_ref, v_ref, g_ref, dl_ref, nm_ref, nv_ref):
        cv = c_ref[...]
        g = _dot_tn((cv * _sig(cv)).astype(BF16), d_ref[...].astype(BF16))
        g_ref[...] = g
        dl_ref[...], nm_ref[...], nv_ref[...] = _adam_math(w_ref[...], g, m_ref[...], v_ref[...])

    blk = pl.BlockSpec((rb, n), lambda i: (i, 0))
    sds = jax.ShapeDtypeStruct(w.shape, F32)
    return pl.pallas_call(
        body, name="ada_bwd", grid=(D // rb,),
        in_specs=[pl.BlockSpec((nrow, rb), lambda i: (0, i)), pl.BlockSpec((nrow, n), lambda i: (0, 0)), blk, blk, blk],
        out_specs=[blk, blk, blk, blk], out_shape=[sds, sds, sds, sds],
        compiler_params=_cp("parallel"),
    )(c_all, dmod_cols, w, m, v)


def _adam(name, parts, w, m, v):
    p, r, c = parts.shape
    rb = r
    for cand in (256, 128, 64, 32, 16, 8):
        if r % cand == 0 and r >= cand:
            rb = cand
            break

    def body(p_ref, w_ref, m_ref, v_ref, g_ref, dl_ref, nm_ref, nv_ref):
        g = p_ref[0].astype(F32)
        for q in range(1, p):
            g = g + p_ref[q].astype(F32)
        g_ref[...] = g
        dl_ref[...], nm_ref[...], nv_ref[...] = _adam_math(w_ref[...], g, m_ref[...], v_ref[...])

    blk = pl.BlockSpec((rb, c), lambda i: (i, 0))
    sds = jax.ShapeDtypeStruct((r, c), F32)
    return pl.pallas_call(
        body, name=name, grid=(r // rb,),
        in_specs=[pl.BlockSpec((p, rb, c), lambda i: (0, i, 0)), blk, blk, blk],
        out_specs=[blk, blk, blk, blk], out_shape=[sds, sds, sds, sds],
        compiler_params=_cp("parallel"),
    )(parts, w, m, v)


def _my_pos():
    return lax.axis_index("x"), lax.axis_index("y"), lax.axis_index("c")


def _all_gather_small(name, v):
    m_per, n = v.shape

    def body(x_ref, out_ref, send_sems, recv_sems, local_sem):
        x, y, c = _my_pos()
        me, sibling = (x, y, c), (x, y, 1 - c)
        chips = [(1 - x, y), (x, 1 - y), (1 - x, 1 - y)]

        def rows(px, py, pc):
            return out_ref.at[pl.ds((4 * px + 2 * py + pc) * m_per, m_per), :]

        def copy(k, block, to, src=None):
            return pltpu.make_async_remote_copy(
                src_ref=rows(*block) if src is None else src, dst_ref=rows(*block),
                send_sem=send_sems.at[k], recv_sem=recv_sems.at[k], device_id=to, device_id_type=MESH)

        mine = pltpu.make_async_copy(x_ref, rows(*me), local_sem)
        mine.start()
        first = [copy(0, me, sibling, src=x_ref)]
        first += [copy(1 + j, me, (*chip, c), src=x_ref) for j, chip in enumerate(chips)]
        for cp in first:
            cp.start()
        passed = [copy(4 + j, (*chip, c), sibling) for j, chip in enumerate(chips)]
        for j, chip in enumerate(chips):
            copy(1 + j, (*chip, c), me).wait_recv()
            passed[j].start()
        copy(0, sibling, me).wait_recv()
        for j, chip in enumerate(chips):
            copy(4 + j, (*chip, 1 - c), me).wait_recv()
        for cp in first + passed:
            cp.wait_send()
        mine.wait()

    return pl.pallas_call(
        body, name=name, out_shape=jax.ShapeDtypeStruct((NDEV * m_per, n), v.dtype),
        in_specs=[pl.BlockSpec(memory_space=pltpu.VMEM)], out_specs=pl.BlockSpec(memory_space=pltpu.VMEM),
        scratch_shapes=[pltpu.SemaphoreType.DMA((7,)), pltpu.SemaphoreType.DMA((7,)), pltpu.SemaphoreType.DMA],
    )(v)


def _blk_cols(n):
    return lambda ref, b: ref.at[:, pl.ds(pl.multiple_of(b * n, 128), n)]


def _blk_rows(n):
    return lambda ref, b: ref.at[pl.ds(pl.multiple_of(b * n, 8), n), :]


def _blk_lead(ref, b):
    return ref.at[b]


def _blk_heads(ref, b):
    return ref.at[:, pl.ds(pl.multiple_of(b * (HB // NDEV), 8), HB // NDEV), :]


def _ag_phases(ins, outs, slicers, send_sems, recv_sems, local_sems):
    na = len(ins)
    x, y, c = _my_pos()
    me, sibling = (x, y, c), (x, y, 1 - c)
    chips = [(1 - x, y), (x, 1 - y), (1 - x, 1 - y)]

    def copy(a, k, block, to, from_shard=False):
        px, py, pc = block
        dst = slicers[a](outs[a], 4 * px + 2 * py + pc)
        return pltpu.make_async_remote_copy(
            src_ref=ins[a] if from_shard else dst, dst_ref=dst,
            send_sem=send_sems.at[a * 7 + k], recv_sem=recv_sems.at[a * 7 + k], device_id=to, device_id_type=MESH)

    def local(a):
        return pltpu.make_async_copy(ins[a], slicers[a](outs[a], 4 * x + 2 * y + c), local_sems.at[a])

    def firsts(a):
        return [copy(a, 0, me, sibling, True)] + [copy(a, 1 + j, me, (*chip, c), True) for j, chip in enumerate(chips)]

    def start():
        for a in range(na):
            local(a).start()
            for cp in firsts(a):
                cp.start()

    def forward():
        for a in range(na):
            for j, chip in enumerate(chips):
                copy(a, 1 + j, (*chip, c), me).wait_recv()
                copy(a, 4 + j, (*chip, c), sibling).start()

    def finish():
        for a in range(na):
            copy(a, 0, sibling, me).wait_recv()
            for j, chip in enumerate(chips):
                copy(a, 4 + j, (*chip, 1 - c), me).wait_recv()
        for a in range(na):
            for cp in firsts(a) + [copy(a, 4 + j, (*chip, c), sibling) for j, chip in enumerate(chips)]:
                cp.wait_send()
            local(a).wait()

    return start, forward, finish


def _ag_sems(na):
    return [pltpu.SemaphoreType.DMA((7 * na,)), pltpu.SemaphoreType.DMA((7 * na,)), pltpu.SemaphoreType.DMA((na,))]


def _all_gather_weights(shards, fulls, slicers):
    na = len(shards)

    def body(*refs):
        start, forward, finish = _ag_phases(refs[:na], refs[na:2 * na], slicers, *refs[2 * na:])
        start()
        forward()
        finish()

    return pl.pallas_call(
        body, name="gather_weights",
        out_shape=[jax.ShapeDtypeStruct(s, sh.dtype) for s, sh in zip(fulls, shards)],
        in_specs=[_ANY] * na, out_specs=[_ANY] * na, scratch_shapes=_ag_sems(na),
    )(*shards)


def _scatter_grads(grads, shard_shapes, slicers):
    na = len(grads)

    def body(*refs):
        ins, outs = refs[:na], refs[na:2 * na]
        send_sems, recv_sems, local_sems = refs[2 * na:]
        x, y, c = _my_pos()
        me = 4 * x + 2 * y + c
        mine, sent = [], []
        for a in range(na):
            cp = pltpu.make_async_copy(slicers[a](ins[a], me), outs[a].at[me], local_sems.at[a])
            cp.start()
            mine.append(cp)
        rel = [(k >> 2 & 1, k >> 1 & 1, k & 1) for k in range(1, NDEV)]
        for a in range(na):
            for k, (fx, fy, fc) in enumerate(rel):
                px, py, pc = x ^ fx, y ^ fy, c ^ fc
                cp = pltpu.make_async_remote_copy(
                    src_ref=slicers[a](ins[a], 4 * px + 2 * py + pc), dst_ref=outs[a].at[me],
                    send_sem=send_sems.at[a * 7 + k], recv_sem=recv_sems.at[a * 7 + k],
                    device_id=(px, py, pc), device_id_type=MESH)
                cp.start()
                sent.append(cp)
        for a in range(na):
            for k, (fx, fy, fc) in enumerate(rel):
                px, py, pc = x ^ fx, y ^ fy, c ^ fc
                src = 4 * px + 2 * py + pc
                pltpu.make_async_remote_copy(
                    src_ref=slicers[a](ins[a], me), dst_ref=outs[a].at[src],
                    send_sem=send_sems.at[a * 7 + k], recv_sem=recv_sems.at[a * 7 + k],
                    device_id=(px, py, pc), device_id_type=MESH).wait_recv()
        for cp in sent:
            cp.wait_send()
        for cp in mine:
            cp.wait()

    any_spec = pl.BlockSpec(memory_space=pl.ANY)
    return pl.pallas_call(
        body, name="scatter_grads",
        out_shape=[jax.ShapeDtypeStruct((NDEV,) + tuple(s), g.dtype) for s, g in zip(shard_shapes, grads)],
        in_specs=[any_spec] * na, out_specs=[any_spec] * na,
        scratch_shapes=[pltpu.SemaphoreType.DMA((7 * na,)), pltpu.SemaphoreType.DMA((7 * na,)),
                        pltpu.SemaphoreType.DMA((na,))],
    )(*grads)


def _local_step(x, target, mod, g_mix, g_ffn, g_fin, prm, w_in_shard, shards):
    fulls = [(HEADS, HB, HB), (HEADS, HB, HB), (D, D), (NDEV, FB, D), (DFF, D)]
    slicers = [_blk_heads, _blk_heads, _blk_rows(D // NDEV), _blk_lead, _blk_rows(DFF // NDEV)]
    my_chip = _my_index() >> 1
    own_first = (my_chip ^ jnp.arange(NCHIP, dtype=jnp.int32)).astype(jnp.int32)
    early, late = [0, 1, 2, 4], [3]
    pick = lambda lst, idx: [lst[i] for i in idx]
    proj, h, w_in, (wa, wx, w_out, w_down) = _in_proj(x, mod, g_mix, w_in_shard, own_first, pick(shards, early),
                                                      pick(fulls, early), pick(slicers, early))
    merged, hl, sv, (w_gu,) = _mixer_fwd(proj, prm, wa, wx, pick(shards, late), pick(fulls, late),
                                         pick(slicers, late))
    w_gu = w_gu.reshape(2, 4, FB, D)
    x1, h2 = _out_proj(merged, x, mod, g_ffn, w_out)
    gu, dx2, dx2b, loss, d_gfin = _ffn_fwd(h2, x1, target, mod, g_fin, w_gu, w_down)
    dgu, act, dx1, dx1b, dmg, sums2 = _ffn_bwd(dx2, gu, x1, mod, g_ffn, w_gu, w_down, w_out)
    chip_order = _xor_order(_my_index() >> 1, NCHIP)
    p_wgu = _gu_wgrad(h2, dgu, chip_order)
    p_wdown, d_gt2 = _scaled_wgrad("down_wgrad", act, dx2b, w_down, 5, mod, chip_order)
    p_wout, d_gt1 = _scaled_wgrad("out_wgrad", merged.reshape(1, *merged.shape), dx1b, w_out, 2, mod,
                                  jnp.zeros((1,), jnp.int32))
    dproj, msums, g_wa, g_wx = _mixer_bwd(proj, hl, sv, dmg, prm, wa, wx)
    p_win, p_wa, p_wx = _in_wgrad(h, dproj, g_wa, g_wx, chip_order)
    grad_x, sums1 = _in_proj_bwd(dproj, w_in, x, dx1, mod, g_mix)
    return dict(loss=loss, grad_x=grad_x, d_gfin=d_gfin, sums1=sums1, sums2=sums2, msums=msums,
                d_gt1=d_gt1[0:1], d_gt2=d_gt2[0:1], p_win=p_win, p_wa=p_wa, p_wx=p_wx, p_wout=p_wout, p_wgu=p_wgu,
                p_wdown=p_wdown)


def kernel(x, c, w_ada, b_ada, g_norm_mix, w_in, conv_a_w, conv_b_w, conv_b_bias, w_rg_a, b_rg_a, w_rg_x, b_rg_x, lru_lambda, w_out, g_norm_ffn, w_gate_up, w_down, g_norm_final, loss_target, m_w_ada, m_b_ada, m_g_norm_mix, m_w_in, m_conv_a_w, m_conv_b_w, m_conv_b_bias, m_w_rg_a, m_b_rg_a, m_w_rg_x, m_b_rg_x, m_lru_lambda, m_w_out, m_g_norm_ffn, m_w_gate_up, m_w_down, m_g_norm_final, v_w_ada, v_b_ada, v_g_norm_mix, v_w_in, v_conv_a_w, v_conv_b_w, v_conv_b_bias, v_w_rg_a, v_b_rg_a, v_w_rg_x, v_b_rg_x, v_lru_lambda, v_w_out, v_g_norm_ffn, v_w_gate_up, v_w_down, v_g_norm_final):
    me = 4 * lax.axis_index("x") + 2 * lax.axis_index("y") + lax.axis_index("c")
    ncol = w_ada.shape[2]
    cw = conv_a_w.shape[2]

    pack0 = jnp.concatenate([c, conv_a_w.reshape(1, 3 * cw), conv_b_w.reshape(1, 4 * cw)], axis=1)
    got0 = _all_gather_small("gather_c", jnp.broadcast_to(pack0, (8, pack0.shape[1])))
    got0 = got0.reshape(NDEV, 8, -1)[:, 0, :]
    c_all = got0[:, :D]
    conv_a = got0[:, D:D + 3 * cw].reshape(NDEV, 3, cw).transpose(1, 0, 2).reshape(3, D)
    conv_b = got0[:, D + 3 * cw:].reshape(NDEV, 4, cw).transpose(1, 0, 2).reshape(4, D)

    b_cols = lax.dynamic_slice_in_dim(b_ada, me * ncol, ncol, axis=1)
    c16 = jnp.concatenate([c_all, jnp.zeros((8, D), F32)], axis=0)
    mod_cols = _ada_fwd(c16, w_ada[0], b_cols)[:NDEV]
    got1 = _all_gather_small("gather_mod", mod_cols).reshape(NDEV, NDEV, ncol)
    mod6 = lax.dynamic_index_in_dim(got1, me, axis=1, keepdims=False).reshape(6, D)
    mod = jnp.concatenate([mod6, jnp.zeros((2, D), F32)], axis=0)

    tr = lambda a: jnp.swapaxes(a, 1, 2)
    shards = [w_rg_a[0].astype(BF16), w_rg_x[0].astype(BF16), w_out[0].astype(BF16), tr(w_gate_up)[0].astype(BF16),
              w_down[0].astype(BF16)]

    prm = jnp.concatenate([conv_a, conv_b, conv_b_bias, b_rg_a, b_rg_x, lru_lambda, jnp.zeros((5, D), F32)], axis=0)
    r = _local_step(x[0], loss_target[0], mod, g_norm_mix, g_norm_ffn, g_norm_final.reshape(1, D), prm,
                    w_in[0].astype(BF16), shards)

    parts = [r["p_win"], r["p_wa"], r["p_wx"], r["p_wout"], r["p_wgu"], r["p_wdown"]]
    big = {}
    for nm, p, w, m, v in (("w_in", parts[0], w_in, m_w_in, v_w_in), ("w_rg_a", parts[1], w_rg_a, m_w_rg_a, v_w_rg_a),
                           ("w_rg_x", parts[2], w_rg_x, m_w_rg_x, v_w_rg_x), ("w_out", parts[3], w_out, m_w_out, v_w_out),
                           ("w_gate_up", parts[4], tr(w_gate_up), tr(m_w_gate_up), tr(v_w_gate_up)),
                           ("w_down", parts[5], w_down, m_w_down, v_w_down)):
        two_d = (-1, w.shape[-1])
        outs = _adam("adam_" + nm, p.reshape((p.shape[0],) + w.reshape(two_d).shape), w.reshape(two_d), m.reshape(two_d),
                     v.reshape(two_d))
        big[nm] = [o.reshape(w.shape) for o in outs]
    big["w_gate_up"] = [tr(o) for o in big["w_gate_up"]]

    small = jnp.concatenate([
        r["sums1"][S_SH:S_SH + 1], r["sums1"][S_SC:S_SC + 1], r["d_gt1"],
        r["sums2"][S_SH:S_SH + 1], r["sums2"][S_SC:S_SC + 1], r["d_gt2"],
        r["sums1"][S_G:S_G + 1],
        r["msums"][M_CBIAS:M_CBIAS + 1], r["msums"][M_BA:M_BA + 1], r["msums"][M_BX:M_BX + 1],
        r["msums"][M_LS:M_LS + 1],
        r["sums2"][S_G:S_G + 1], r["d_gfin"],
        r["msums"][M_WA:M_WA + 3], r["msums"][M_WB:M_WB + 4],
        jnp.broadcast_to(r["loss"][0:1, 0:1], (1, D)),
        jnp.zeros((3, D), F32)], axis=0)
    got2 = _all_gather_small("gather_small", small).reshape(NDEV, 24, D)

    rep_w = jnp.concatenate([b_ada.reshape(6, D), g_norm_mix, conv_b_bias, b_rg_a, b_rg_x, lru_lambda, g_norm_ffn,
                             g_norm_final.reshape(1, D), jnp.zeros((3, D), F32)], axis=0)
    rep_m = jnp.concatenate([m_b_ada.reshape(6, D), m_g_norm_mix, m_conv_b_bias, m_b_rg_a, m_b_rg_x, m_lru_lambda,
                             m_g_norm_ffn, m_g_norm_final.reshape(1, D), jnp.zeros((3, D), F32)], axis=0)
    rep_v = jnp.concatenate([v_b_ada.reshape(6, D), v_g_norm_mix, v_conv_b_bias, v_b_rg_a, v_b_rg_x, v_lru_lambda,
                             v_g_norm_ffn, v_g_norm_final.reshape(1, D), jnp.ones((3, D), F32)], axis=0)
    rep = _adam("adam_rep", got2[:, :16, :], rep_w, rep_m, rep_v)

    conv_parts = lax.dynamic_slice_in_dim(got2[:, 13:21, :], me * cw, cw, axis=2)
    cv_w = jnp.concatenate([conv_a_w[0], conv_b_w[0], jnp.zeros((1, cw), F32)], axis=0)
    cv_m = jnp.concatenate([m_conv_a_w[0], m_conv_b_w[0], jnp.zeros((1, cw), F32)], axis=0)
    cv_v = jnp.concatenate([v_conv_a_w[0], v_conv_b_w[0], jnp.ones((1, cw), F32)], axis=0)
    cvo = _adam("adam_conv", conv_parts, cv_w, cv_m, cv_v)

    dmod_cols = lax.dynamic_slice_in_dim(got2[:, :6, :].reshape(NDEV, 6 * D), me * ncol, ncol, axis=1)
    dmod16 = jnp.concatenate([dmod_cols, jnp.zeros((8, ncol), F32)], axis=0)
    ada = _ada_bwd(c16, dmod16, w_ada[0], m_w_ada[0], v_w_ada[0])

    loss = jnp.sum(got2[:, 20, 0])

    def pick(q):
        one = lambda i: rep[q][i:i + 1]
        return [ada[q].reshape(w_ada.shape), rep[q][0:6].reshape(b_ada.shape), one(6), big["w_in"][q],
                cvo[q][0:3].reshape(conv_a_w.shape), cvo[q][3:7].reshape(conv_b_w.shape), one(7),
                big["w_rg_a"][q], one(8), big["w_rg_x"][q], one(9), one(10), big["w_out"][q], one(11),
                big["w_gate_up"][q], big["w_down"][q], rep[q][12]]

    return (loss, r["grad_x"].reshape(x.shape), *pick(0), *pick(1), *pick(2), *pick(3))
```

```python
import functools
import math

import jax
import jax.numpy as jnp
from jax import lax
from jax.experimental import pallas as pl
from jax.experimental.pallas import tpu as pltpu

F32 = jnp.float32
BF16 = jnp.bfloat16

D = 1024
DFF = 2816
NDEV = 8
HEADS = 4
HB = D // HEADS
FB = DFF // 4
EPS = 1e-6
LRU_C = 8.0
ADAM_LR, ADAM_B1, ADAM_B2, ADAM_EPS, ADAM_WD, ADAM_STEP = 0.001, 0.9, 0.999, 1e-08, 0.01, 10

VMEM_LIMIT = 56 * 1024 * 1024
TM = 512
TMI = 1024
TMF = 256
TK = 2048
TKI = 2048
SUB = 256
TT = 256
CG = 256
MESH = pl.DeviceIdType.MESH
AXES = ("x", "y", "c")


def _cp(*sem):
    return pltpu.CompilerParams(dimension_semantics=sem, vmem_limit_bytes=VMEM_LIMIT)


def _sig(x):
    return 1.0 / (1.0 + jnp.exp(-x))


def _log_sigmoid(x):
    z = jnp.exp(-jnp.abs(x))
    u = 1.0 + z
    d = u - 1.0
    l1p = jnp.where(d == 0.0, z, jnp.log(u) * (z / jnp.where(d == 0.0, 1.0, d)))
    return -(jnp.maximum(-x, 0.0) + l1p)


def _neg_expm1(x):
    p = x * (1.0 + x * 0.5 * (1.0 + x * (1.0 / 3.0) * (1.0 + x * 0.25 * (1.0 + x * 0.2 * (1.0 + x * (1.0 / 6.0))))))
    return jnp.where(x > -0.25, -p, 1.0 - jnp.exp(x))


_GC = math.sqrt(2.0 / math.pi)


def _gelu(x):
    t = jnp.tanh(_GC * (x + 0.044715 * x * x * x))
    return 0.5 * x * (1.0 + t), t


def _dot(a, b):
    return jnp.dot(a, b, preferred_element_type=F32)


def _dot_nt(a, b):
    return lax.dot_general(a, b, (((1,), (1,)), ((), ())), preferred_element_type=F32)


def _dot_tn(a, b):
    return lax.dot_general(a, b, (((0,), (0,)), ((), ())), preferred_element_type=F32)


def _resident(shape):
    return pl.BlockSpec(shape, lambda *_: (0,) * len(shape), pipeline_mode=pl.Buffered(1))


def _sub_blocks(n_rows):
    step = min(SUB, n_rows)
    return [slice(r, r + step) for r in range(0, n_rows, step)]


def _fold8(v):
    return v[0:8] + v[8:16]


def _pj(ref, s, rows=slice(None), cols=slice(0, D)):
    return ref[rows, s * D + cols.start:s * D + cols.stop]


def _in_proj(x, mod, g_mix, w_shard, order, shards, fulls, slicers):
    t_len = x.shape[0]
    tm = min(TMI, t_len)
    ni = t_len // tm
    na = len(shards)
    cw = 7 * D // NDEV
    rc = 32

    def body(ord_ref, x_ref, mod_ref, g_ref, wsh_ref, *rest):
        ins, (proj_ref, h_ref, wfull_ref), outs = rest[:na], rest[na:na + 3], rest[na + 3:2 * na + 3]
        h_scr, w_scr, wsend, wrecv, wlocal, wout = rest[2 * na + 3:2 * na + 9]
        start, forward, finish = _ag_phases(ins, outs, slicers, *rest[2 * na + 9:])
        p, i = pl.program_id(0), pl.program_id(1)
        x_, y_, c = _my_pos()
        me, sibling = (x_, y_, c), (x_, y_, 1 - c)
        chip_at = [None, (x_, 1 - y_), (1 - x_, y_), (1 - x_, 1 - y_)]

        def cols(px, py, pc):
            return w_scr.at[:, pl.ds(pl.multiple_of((4 * px + 2 * py + pc) * cw, 128), cw)]

        def wcopy(k, block, to, from_shard=False):
            dst = cols(*block)
            return pltpu.make_async_remote_copy(src_ref=wsh_ref if from_shard else dst, dst_ref=dst,
                                                send_sem=wsend.at[k], recv_sem=wrecv.at[k], device_id=to,
                                                device_id_type=MESH)

        own_local = pltpu.make_async_copy(wsh_ref, cols(*me), wlocal)
        to_hbm = pltpu.make_async_copy(w_scr, wfull_ref, wout)

        @pl.when((p == 0) & (i == 0))
        def _():
            own_local.start()
            wcopy(0, me, sibling, True).start()
            for q in (1, 2):
                wcopy(q, me, (*chip_at[q], c), True).start()
            own_local.wait()
            wcopy(0, sibling, me).wait_recv()

        @pl.when((p == 0) & (i == ni // 2))
        def _():
            wcopy(3, me, (*chip_at[3], c), True).start()

        for q in (1, 2, 3):
            @pl.when((p == q - 1) & (i == ni - 1))
            def _():
                wcopy(q, (*chip_at[q], c), me).wait_recv()
                wcopy(3 + q, (*chip_at[q], c), sibling).start()

            @pl.when((p == q) & (i == 0))
            def _():
                wcopy(3 + q, (*chip_at[q], 1 - c), me).wait_recv()

        @pl.when((p == 1) & (i == 0))
        def _():
            start()

        @pl.when((p == NCHIP - 1) & (i == ni // 2))
        def _():
            forward()

        @pl.when((p == NCHIP - 1) & (i == 0))
        def _():
            to_hbm.start()

        gs = g_ref[...] * (1.0 + mod_ref[1:2, :])
        sh = mod_ref[0:1, :]

        wcols = pl.ds(pl.multiple_of(ord_ref[p] * (2 * cw), 128), 2 * cw)
        for sub in _sub_blocks(tm):
            for r0 in range(sub.start, sub.stop, rc):
                xv = x_ref[r0:r0 + rc, :]
                r = lax.rsqrt(jnp.mean(xv * xv, axis=-1, keepdims=True) + EPS)
                h_scr[r0:r0 + rc, :] = (xv * r * gs + sh).astype(BF16)
            proj_ref[sub, :] = _dot(h_scr[sub, :], w_scr[:, wcols]).astype(BF16)

        @pl.when(p == 0)
        def _():
            h_ref[...] = h_scr[...]

        @pl.when((p == NCHIP - 1) & (i == ni - 1))
        def _():
            wcopy(0, me, sibling, True).wait_send()
            for q in (1, 2, 3):
                wcopy(q, me, (*chip_at[q], c), True).wait_send()
                wcopy(3 + q, (*chip_at[q], c), sibling).wait_send()
            finish()
            to_hbm.wait()

    res = pl.pallas_call(
        body, name="in_proj",
        grid_spec=pltpu.PrefetchScalarGridSpec(
            num_scalar_prefetch=1, grid=(NCHIP, ni),
            in_specs=[pl.BlockSpec((tm, D), lambda p, i, o: (i, 0)),
                      pl.BlockSpec((8, D), lambda p, i, o: (0, 0)),
                      pl.BlockSpec((1, D), lambda p, i, o: (0, 0))] + [_ANY] * (1 + na),
            out_specs=[pl.BlockSpec((tm, 2 * cw), lambda p, i, o: (i, o[p])),
                       pl.BlockSpec((tm, D), lambda p, i, o: (jnp.where(p == 0, i, ni - 1), 0))]
            + [_ANY] * (1 + na),
            scratch_shapes=[pltpu.VMEM((tm, D), BF16), pltpu.VMEM((D, 7 * D), BF16),
                            pltpu.SemaphoreType.DMA((7,)), pltpu.SemaphoreType.DMA((7,)),
                            pltpu.SemaphoreType.DMA, pltpu.SemaphoreType.DMA] + _ag_sems(na)),
        out_shape=[jax.ShapeDtypeStruct((t_len, 7 * D), BF16), jax.ShapeDtypeStruct((t_len, D), BF16),
                   jax.ShapeDtypeStruct((D, 7 * D), BF16)]
        + [jax.ShapeDtypeStruct(f, sh.dtype) for f, sh in zip(fulls, shards)],
        compiler_params=_cp("arbitrary", "arbitrary"),
    )(order, x, mod, g_mix, w_shard, *shards)
    return res[0], res[1], res[2], res[3:]


P_WA, P_WB, P_CBIAS, P_BA, P_BX, P_LAM = 0, 3, 7, 8, 9, 10
SV_PLANES = SV_U, SV_YA, SV_R, SV_I, SV_A, SV_MULT = range(6)


def _lru_gates(rp, ip, ls, first_row):
    r = _sig(rp)
    ig = _sig(ip)
    la = LRU_C * r * ls
    a = jnp.exp(la)
    m2 = _neg_expm1(2.0 * la)
    mult = jnp.where(first_row, 1.0, jnp.sqrt(jnp.maximum(m2, 0.0)))
    return r, ig, la, a, m2, mult


def _shift_down(cur, prev, s, row):
    return jnp.where(row >= s, pltpu.roll(cur, s, 0), pltpu.roll(prev, s, 0))


def _shift_up(cur, nxt, s, row):
    return jnp.where(row < 8 - s, pltpu.roll(cur, 8 - s, 0), pltpu.roll(nxt, 8 - s, 0))


def _conv_fwd_rows(tt, proj_ref, prm_ref, xe, ve, u_s, ub_s, ya_s):
    row = lax.broadcasted_iota(jnp.int32, (8, CG), 0)
    w_b = [prm_ref[P_WB + k:P_WB + k + 1, :] for k in range(4)]
    w_a = [prm_ref[P_WA + k:P_WA + k + 1, :] for k in range(3)]
    bias = prm_ref[P_CBIAS:P_CBIAS + 1, :]

    def blk(ib, carry):
        r0 = pl.multiple_of(ib * 16, 16)
        rows = pl.ds(r0, 16)
        for g in range(D // CG):
            cs = slice(g * CG, (g + 1) * CG)
            x16 = _pj(proj_ref, 3, rows, cs).astype(F32)
            v16 = _pj(proj_ref, 1, rows, cs).astype(F32) * _pj(proj_ref, 2, rows, cs).astype(F32)
            xp = xe[pl.ds(r0, 8), cs]
            vp = ve[pl.ds(r0, 8), cs]
            xe[pl.ds(r0 + 8, 16), cs] = x16
            ve[pl.ds(r0 + 8, 16), cs] = v16
            us, yas = [], []
            for sb in range(2):
                xc, vc = x16[8 * sb:8 * sb + 8], v16[8 * sb:8 * sb + 8]
                u8 = bias[:, cs] + w_b[3][:, cs] * xc
                for s in (1, 2, 3):
                    u8 = u8 + w_b[3 - s][:, cs] * _shift_down(xc, xp, s, row)
                y8 = w_a[2][:, cs] * vc
                for s in (1, 2):
                    y8 = y8 + w_a[2 - s][:, cs] * _shift_down(vc, vp, s, row)
                us.append(u8)
                yas.append(y8)
                xp, vp = xc, vc
            u16 = jnp.concatenate(us, axis=0)
            u_s[rows, cs] = u16
            ub_s[rows, cs] = u16.astype(BF16)
            ya_s[rows, cs] = jnp.concatenate(yas, axis=0)
        return carry

    lax.fori_loop(0, tt // 16, blk, 0)


def _mixer_fwd(proj, prm, wa, wx, shards, fulls, slicers):
    t_len = proj.shape[0]
    tt = min(TT, t_len)
    nt = t_len // tt
    na = len(shards)

    def body(proj_ref, prm_ref, wa_ref, wx_ref, *rest):
        ins, (mg_ref, hl_ref, sv_hbm), outs = rest[:na], rest[na:na + 3], rest[na + 3:2 * na + 3]
        xe, ve, hc, rp_s, ip_s, ub_s, sv_st, sv_sems = rest[2 * na + 3:2 * na + 11]
        start, forward, finish = _ag_phases(ins, outs, slicers, *rest[2 * na + 11:])
        t = pl.program_id(0)

        slot = t % 2
        sv_ref = sv_st.at[slot]

        def sv_out(tile, sl):
            rows = pl.ds(pl.multiple_of(tile * tt, tt), tt)
            return pltpu.make_async_copy(sv_st.at[sl], sv_hbm.at[:, rows, :], sv_sems.at[sl])

        @pl.when(t >= 2)
        def _():
            sv_out(t - 2, slot).wait()

        @pl.when(t == 0)
        def _():
            start()
            xe[0:8, :] = jnp.zeros((8, D), F32)
            ve[0:8, :] = jnp.zeros((8, D), F32)
            hc[...] = jnp.zeros((8, D), F32)

        @pl.when(t == (3 * nt) // 4)
        def _():
            forward()

        _conv_fwd_rows(tt, proj_ref, prm_ref, xe, ve, sv_ref.at[SV_U], ub_s, sv_ref.at[SV_YA])
        xe[0:8, :] = xe[tt:tt + 8, :]
        ve[0:8, :] = ve[tt:tt + 8, :]

        ub = ub_s[...]
        for h in range(HEADS):
            cs = slice(h * HB, (h + 1) * HB)
            rp_s[:, cs] = _dot(ub[:, cs], wa_ref[h]) + prm_ref[P_BA:P_BA + 1, cs]
            ip_s[:, cs] = _dot(ub[:, cs], wx_ref[h]) + prm_ref[P_BX:P_BX + 1, cs]

        ls_all = _log_sigmoid(prm_ref[P_LAM:P_LAM + 1, :])
        row = lax.broadcasted_iota(jnp.int32, (8, CG), 0)

        def blk(i, carry):
            r0 = pl.multiple_of(i * 16, 16)
            for g in range(D // CG):
                cs = slice(g * CG, (g + 1) * CG)
                ls = ls_all[:, cs]
                hprev = hc[:, cs]
                hs = []
                for sb in range(2):
                    rr = r0 + 8 * sb
                    first = (row + (t * tt + rr)) == 0
                    r8 = pl.ds(rr, 8)
                    r, ig, _, a, _, mult = _lru_gates(rp_s[r8, cs], ip_s[r8, cs], ls, first)
                    for plane, val in ((SV_R, r), (SV_I, ig), (SV_A, a), (SV_MULT, mult)):
                        sv_ref[plane, r8, cs] = val
                    b = mult * (ig * sv_ref[SV_U, r8, cs])
                    for s in (1, 2, 4):
                        a_sh = jnp.where(row >= s, pltpu.roll(a, s, 0), 1.0)
                        b_sh = jnp.where(row >= s, pltpu.roll(b, s, 0), 0.0)
                        b = a * b_sh + b
                        a = a * a_sh
                    hv = a * hprev + b
                    hprev = jnp.broadcast_to(hv[7:8, :], hv.shape)
                    hs.append(hv)
                hc[:, cs] = hprev
                h16 = jnp.concatenate(hs, axis=0)
                rows = pl.ds(r0, 16)
                gl, _ = _gelu(_pj(proj_ref, 4, rows, cs).astype(F32))
                y_b = h16 * gl
                y_a = _pj(proj_ref, 0, rows, cs).astype(F32) * sv_ref[SV_YA, rows, cs]
                mg = (_sig(_pj(proj_ref, 5, rows, cs).astype(F32)) * y_a
                      + _sig(_pj(proj_ref, 6, rows, cs).astype(F32)) * y_b)
                mg_ref[rows, cs] = mg.astype(BF16)
                hl_ref[rows, cs] = h16.astype(BF16)
            return carry

        lax.fori_loop(0, tt // 16, blk, 0)
        sv_out(t, slot).start()

        @pl.when(t == nt - 1)
        def _():
            finish()
            if nt >= 2:
                sv_out(t - 1, 1 - slot).wait()
            sv_out(t, slot).wait()

    res = pl.pallas_call(
        body, name="mixer_fwd", grid=(nt,),
        in_specs=[pl.BlockSpec((tt, 7 * D), lambda t: (t, 0)),
                  pl.BlockSpec((16, D), lambda t: (0, 0)),
                  pl.BlockSpec((HEADS, HB, HB), lambda t: (0, 0, 0)),
                  pl.BlockSpec((HEADS, HB, HB), lambda t: (0, 0, 0))] + [_ANY] * na,
        out_specs=[pl.BlockSpec((tt, D), lambda t: (t, 0)), pl.BlockSpec((tt, D), lambda t: (t, 0)), _ANY]
        + [_ANY] * na,
        out_shape=[jax.ShapeDtypeStruct((t_len, D), BF16), jax.ShapeDtypeStruct((t_len, D), BF16),
                   jax.ShapeDtypeStruct((len(SV_PLANES), t_len, D), F32)]
        + [jax.ShapeDtypeStruct(f, sh.dtype) for f, sh in zip(fulls, shards)],
        scratch_shapes=[pltpu.VMEM((tt + 8, D), F32), pltpu.VMEM((tt + 8, D), F32), pltpu.VMEM((8, D), F32),
                        pltpu.VMEM((tt, D), F32), pltpu.VMEM((tt, D), F32), pltpu.VMEM((tt, D), BF16),
                        pltpu.VMEM((2, len(SV_PLANES), tt, D), F32), pltpu.SemaphoreType.DMA((2,))]
        + _ag_sems(na),
        compiler_params=_cp("arbitrary"),
    )(proj, prm, wa, wx, *shards)
    return res[0], res[1], res[2], res[3:]


def _out_proj(merged, x, mod, g_ffn, w_out):
    t_len = x.shape[0]
    tm = min(TM, t_len)

    def body(mg_ref, x_ref, mod_ref, g_ref, w_ref, x1_ref, h2_ref):
        gt1 = mod_ref[2:3, :]
        gs = g_ref[...] * (1.0 + mod_ref[4:5, :])
        sh = mod_ref[3:4, :]
        for sub in _sub_blocks(tm):
            x1_ref[sub, :] = x_ref[sub, :] + gt1 * _dot(mg_ref[sub, :], w_ref[...])
            for r0 in range(sub.start, sub.stop, 16):
                x1 = x1_ref[r0:r0 + 16, :]
                r = lax.rsqrt(jnp.mean(x1 * x1, axis=-1, keepdims=True) + EPS)
                h2_ref[r0:r0 + 16, :] = (x1 * r * gs + sh).astype(BF16)

    return pl.pallas_call(
        body, name="out_proj", grid=(t_len // tm,),
        in_specs=[pl.BlockSpec((tm, D), lambda i: (i, 0)), pl.BlockSpec((tm, D), lambda i: (i, 0)),
                  pl.BlockSpec((8, D), lambda i: (0, 0)), pl.BlockSpec((1, D), lambda i: (0, 0)),
                  pl.BlockSpec((D, D), lambda i: (0, 0))],
        out_specs=[pl.BlockSpec((tm, D), lambda i: (i, 0)), pl.BlockSpec((tm, D), lambda i: (i, 0))],
        out_shape=[jax.ShapeDtypeStruct((t_len, D), F32), jax.ShapeDtypeStruct((t_len, D), BF16)],
        compiler_params=_cp("parallel"),
    )(merged, x, mod, g_ffn, w_out)


def _ffn_fwd(h2, x1, target, mod, g_fin, w_gu, w_down):
    t_len = x1.shape[0]
    tm = min(TMF, t_len)

    def body(h2_ref, x1_ref, tg_ref, mod_ref, g_ref, wgu_ref, wd_ref, gu_ref, dx2_ref, dx2b_ref, loss_ref, dg_ref, acc):
        @pl.when(pl.program_id(0) == 0)
        def _():
            loss_ref[...] = jnp.zeros_like(loss_ref)
            dg_ref[...] = jnp.zeros_like(dg_ref)

        hb = h2_ref[...]
        ffn = None
        nxt = (_dot_nt(hb, wgu_ref[0, 0]), _dot_nt(hb, wgu_ref[1, 0]))
        for j in range(4):
            gate, up = nxt
            if j < 3:
                nxt = (_dot_nt(hb, wgu_ref[0, j + 1]), _dot_nt(hb, wgu_ref[1, j + 1]))
            gu_ref[0, j] = gate.astype(BF16)
            gu_ref[1, j] = up.astype(BF16)
            act = (gate * _sig(gate) * up).astype(BF16)
            part = _dot(act, wd_ref[j * FB:(j + 1) * FB, :])
            ffn = part if ffn is None else ffn + part
        acc[...] = ffn

        gt2 = mod_ref[5:6, :]
        gf = g_ref[...]

        s_loss = s_dg = jnp.zeros((8, D), F32)
        for r0 in range(0, tm, 16):
            rows = slice(r0, r0 + 16)
            x2 = x1_ref[rows, :] + gt2 * acc[rows, :]
            r = lax.rsqrt(jnp.mean(x2 * x2, axis=-1, keepdims=True) + EPS)
            xn = x2 * r
            diff = xn * gf - tg_ref[rows, :]
            dy = diff * (1.0 / D)
            dxn = dy * gf
            dx2 = r * (dxn - xn * jnp.mean(dxn * xn, axis=-1, keepdims=True))
            dx2_ref[rows, :] = dx2
            dx2b_ref[rows, :] = dx2.astype(BF16)
            s_loss, s_dg = s_loss + _fold8(diff * diff), s_dg + _fold8(dy * xn)
        loss_ref[...] += jnp.sum(s_loss) * (0.5 / D)
        dg_ref[...] += jnp.sum(s_dg, axis=0, keepdims=True)

    row = pl.BlockSpec((tm, D), lambda i: (i, 0))
    return pl.pallas_call(
        body, name="ffn_fwd", grid=(t_len // tm,),
        in_specs=[row, row, row, pl.BlockSpec((8, D), lambda i: (0, 0)), pl.BlockSpec((1, D), lambda i: (0, 0)),
                  _resident((2, 4, FB, D)), _resident((DFF, D))],
        out_specs=[pl.BlockSpec((2, 4, tm, FB), lambda i: (0, 0, i, 0)), row, row,
                   pl.BlockSpec((1, 128), lambda i: (0, 0)), pl.BlockSpec((1, D), lambda i: (0, 0))],
        out_shape=[jax.ShapeDtypeStruct((2, 4, t_len, FB), BF16), jax.ShapeDtypeStruct((t_len, D), F32),
                   jax.ShapeDtypeStruct((t_len, D), BF16),
                   jax.ShapeDtypeStruct((1, 128), F32), jax.ShapeDtypeStruct((1, D), F32)],
        scratch_shapes=[pltpu.VMEM((tm, D), F32)],
        compiler_params=_cp("arbitrary"),
    )(h2, x1, target, mod, g_fin, w_gu, w_down)


S_SH, S_SC, S_G = 0, 1, 2


def _norm_bwd_rows(span, sums, dh_ref, x_ref, dres_ref, scale, gain, write):
    gs = 1.0 + scale
    s_sh, s_sc, s_g = sums
    for r0 in range(span.start, span.stop, 16):
        rows = slice(r0, r0 + 16)
        dh = dh_ref[rows, :]
        xv = x_ref[rows, :]
        r = lax.rsqrt(jnp.mean(xv * xv, axis=-1, keepdims=True) + EPS)
        xn = xv * r
        dhn = dh * gs
        dxn = dhn * gain
        write(rows, dres_ref[rows, :] + r * (dxn - xn * jnp.mean(dxn * xn, axis=-1, keepdims=True)))
        s_sh, s_sc, s_g = s_sh + _fold8(dh), s_sc + _fold8(dh * (xn * gain)), s_g + _fold8(dhn * xn)
    return s_sh, s_sc, s_g


def _add_norm_sums(sums_ref, sums):
    for dst, s in zip((S_SH, S_SC, S_G), sums):
        sums_ref[dst:dst + 1, :] += jnp.sum(s, axis=0, keepdims=True)


def _ffn_bwd(dx2, gu, x1, mod, g_ffn, w_gu, w_down, w_out):
    t_len = x1.shape[0]
    tm = min(TMF, t_len)

    def body(dx2_ref, gu_ref, x1_ref, mod_ref, g_ref, wgu_ref, wd_ref, wo_ref,
             dgu_ref, act_ref, dx1_ref, dx1b_ref, dmg_ref, sums_ref, acc, dmo, dact_s):
        @pl.when(pl.program_id(0) == 0)
        def _():
            sums_ref[...] = jnp.zeros_like(sums_ref)

        dffn = (dx2_ref[...] * mod_ref[5:6, :]).astype(BF16)
        dact_s[0] = _dot_nt(dffn, wd_ref[0:FB, :])
        for j in range(4):
            if j < 3:
                dact_s[(j + 1) % 2] = _dot_nt(dffn, wd_ref[(j + 1) * FB:(j + 2) * FB, :])
            for r0 in range(0, tm, 16):
                rows = slice(r0, r0 + 16)
                dact = dact_s[j % 2, rows, :]
                gate = gu_ref[0, j, rows, :].astype(F32)
                up = gu_ref[1, j, rows, :].astype(F32)
                sg = _sig(gate)
                silu = gate * sg
                act_ref[j, rows, :] = (silu * up).astype(BF16)
                dgu_ref[0, j, rows, :] = (dact * up * (sg * (1.0 + gate * (1.0 - sg)))).astype(BF16)
                dgu_ref[1, j, rows, :] = (dact * silu).astype(BF16)
            part = _dot(dgu_ref[0, j], wgu_ref[0, j]) + _dot(dgu_ref[1, j], wgu_ref[1, j])
            if j == 0:
                acc[...] = part
            else:
                acc[...] += part

        gt1 = mod_ref[2:3, :]

        def write(rows, dx1):
            dx1_ref[rows, :] = dx1
            dx1b_ref[rows, :] = dx1.astype(BF16)
            dmo[rows, :] = (dx1 * gt1).astype(BF16)

        zero = jnp.zeros((8, D), F32)
        sums = (zero, zero, zero)
        for sub in (slice(0, tm // 2), slice(tm // 2, tm)):
            sums = _norm_bwd_rows(sub, sums, acc, x1_ref, dx2_ref, mod_ref[4:5, :], g_ref[...], write)
            dmg_ref[sub, :] = _dot_nt(dmo[sub, :], wo_ref[...]).astype(BF16)
        _add_norm_sums(sums_ref, sums)

    row = pl.BlockSpec((tm, D), lambda i: (i, 0))
    return pl.pallas_call(
        body, name="ffn_bwd", grid=(t_len // tm,),
        in_specs=[row, pl.BlockSpec((2, 4, tm, FB), lambda i: (0, 0, i, 0)), row,
                  pl.BlockSpec((8, D), lambda i: (0, 0)), pl.BlockSpec((1, D), lambda i: (0, 0)),
                  _resident((2, 4, FB, D)), _resident((DFF, D)), _resident((D, D))],
        out_specs=[pl.BlockSpec((2, 4, tm, FB), lambda i: (0, 0, i, 0)),
                   pl.BlockSpec((4, tm, FB), lambda i: (0, i, 0)), row, row, row,
                   pl.BlockSpec((8, D), lambda i: (0, 0))],
        out_shape=[jax.ShapeDtypeStruct((2, 4, t_len, FB), BF16), jax.ShapeDtypeStruct((4, t_len, FB), BF16),
                   jax.ShapeDtypeStruct((t_len, D), F32), jax.ShapeDtypeStruct((t_len, D), BF16),
                   jax.ShapeDtypeStruct((t_len, D), BF16), jax.ShapeDtypeStruct((8, D), F32)],
        scratch_shapes=[pltpu.VMEM((tm, D), F32), pltpu.VMEM((tm, D), BF16), pltpu.VMEM((2, tm, FB), F32)],
        compiler_params=_cp("arbitrary"),
    )(dx2, gu, x1, mod, g_ffn, w_gu, w_down, w_out)


def _my_pos():
    return lax.axis_index("x"), lax.axis_index("y"), lax.axis_index("c")


def _my_index():
    x, y, c = _my_pos()
    return 4 * x + 2 * y + c


def _device_of(b):
    return (b >> 2) & 1, (b >> 1) & 1, b & 1


def _rs_send(src, parts_ref, b, send_sems, recv_sems, local_sem):
    me = _my_index()
    dst = parts_ref.at[me]

    @pl.when(b == me)
    def _():
        pltpu.make_async_copy(src, dst, local_sem).start()

    @pl.when(b != me)
    def _():
        pltpu.make_async_remote_copy(src_ref=src, dst_ref=dst, send_sem=send_sems.at[b], recv_sem=recv_sems.at[me],
                                     device_id=_device_of(b), device_id_type=MESH).start()


def _rs_finish(src_of, parts_ref, send_sems, recv_sems, local_sem):
    me = _my_index()
    for s in range(NDEV):
        @pl.when(s != me)
        def _():
            cp = pltpu.make_async_remote_copy(src_ref=src_of(s), dst_ref=parts_ref.at[s], send_sem=send_sems.at[s],
                                              recv_sem=recv_sems.at[s], device_id=_device_of(s), device_id_type=MESH)
            cp.wait_send()
            cp.wait_recv()

        @pl.when(s == me)
        def _():
            pltpu.make_async_copy(src_of(s), parts_ref.at[s], local_sem).wait()


_RS_SEMS = [pltpu.SemaphoreType.DMA((NDEV,)), pltpu.SemaphoreType.DMA((NDEV,)), pltpu.SemaphoreType.DMA]
_ANY = pl.BlockSpec(memory_space=pl.ANY)


def _xor_order(me, n):
    return (me ^ (n - 1 - jnp.arange(n, dtype=jnp.int32))).astype(jnp.int32)


NCHIP = NDEV // 2


def _rs2_scratch(half_shape):
    blocks = lambda *lead: pltpu.VMEM(lead + tuple(half_shape), BF16)
    return [blocks(NCHIP, 2), blocks(NCHIP)] + [pltpu.SemaphoreType.DMA((NCHIP,))] * 4 + [pltpu.SemaphoreType.DMA]


def _rs2_to_sibling(q, rs):
    stage, from_sib, d_send, d_recv = rs[:4]
    x, y, c = _my_pos()
    pltpu.make_async_remote_copy(src_ref=stage.at[q, 1 - c], dst_ref=from_sib.at[q], send_sem=d_send.at[q],
                                 recv_sem=d_recv.at[q], device_id=(x, y, 1 - c), device_id_type=MESH).start()


def _rs2_forward(q, parts_ref, rs):
    stage, chip_sum, d_send, d_recv, i_send, i_recv, local_sem = rs
    x, y, c = _my_pos()
    my_chip = 2 * x + y
    pltpu.make_async_remote_copy(src_ref=stage.at[q, c], dst_ref=chip_sum.at[q], send_sem=d_send.at[q],
                                 recv_sem=d_recv.at[q], device_id=(x, y, 1 - c), device_id_type=MESH).wait_recv()
    chip_sum[q] = (stage[q, c].astype(F32) + chip_sum[q].astype(F32)).astype(BF16)

    @pl.when(q == my_chip)
    def _():
        pltpu.make_async_copy(chip_sum.at[q], parts_ref.at[my_chip], local_sem).start()

    @pl.when(q != my_chip)
    def _():
        pltpu.make_async_remote_copy(src_ref=chip_sum.at[q], dst_ref=parts_ref.at[my_chip], send_sem=i_send.at[q],
                                     recv_sem=i_recv.at[my_chip], device_id=((q >> 1) & 1, q & 1, c),
                                     device_id_type=MESH).start()


def _rs2_finish(parts_ref, rs):
    stage, chip_sum, d_send, d_recv, i_send, i_recv, local_sem = rs
    x, y, c = _my_pos()
    my_chip = 2 * x + y
    for q in range(NCHIP):
        pltpu.make_async_remote_copy(src_ref=stage.at[q, 1 - c], dst_ref=chip_sum.at[q], send_sem=d_send.at[q],
                                     recv_sem=d_recv.at[q], device_id=(x, y, 1 - c), device_id_type=MESH).wait_send()

        @pl.when(q != my_chip)
        def _():
            cp = pltpu.make_async_remote_copy(src_ref=chip_sum.at[q], dst_ref=parts_ref.at[q], send_sem=i_send.at[q],
                                              recv_sem=i_recv.at[q], device_id=((q >> 1) & 1, q & 1, c),
                                              device_id_type=MESH)
            cp.wait_send()
            cp.wait_recv()

        @pl.when(q == my_chip)
        def _():
            pltpu.make_async_copy(chip_sum.at[q], parts_ref.at[q], local_sem).wait()


def _gu_wgrad(h2, dgu, order):
    t_len = h2.shape[0]
    tk = min(TK, t_len)
    nk = t_len // tk

    def body(ord_ref, h_ref, d_ref, parts_ref, acc, *rs):
        p, k = pl.program_id(0), pl.program_id(1)

        @pl.when(k == 0)
        def _():
            acc[...] = jnp.zeros_like(acc)

        hb = h_ref[...]
        for half in range(2):
            acc[half] += _dot_tn(d_ref[0, half], hb)

        @pl.when(k == nk - 1)
        def _():
            q = ord_ref[p]
            rs[0][q] = acc[...].astype(BF16)
            _rs2_to_sibling(q, rs)

        @pl.when((k == nk - 1) & (p > 0))
        def _():
            _rs2_forward(ord_ref[p - 1], parts_ref, rs)

        @pl.when((p == NCHIP - 1) & (k == nk - 1))
        def _():
            _rs2_forward(ord_ref[p], parts_ref, rs)
            _rs2_finish(parts_ref, rs)

    return pl.pallas_call(
        body, name="gu_wgrad",
        grid_spec=pltpu.PrefetchScalarGridSpec(
            num_scalar_prefetch=1, grid=(NCHIP, nk),
            in_specs=[pl.BlockSpec((tk, D), lambda p, k, o: (k, 0)),
                      pl.BlockSpec((1, 2, tk, FB), lambda p, k, o: (o[p], 0, k, 0))],
            out_specs=_ANY,
            scratch_shapes=[pltpu.VMEM((2, FB, D), F32)] + _rs2_scratch((FB, D))),
        out_shape=jax.ShapeDtypeStruct((NCHIP, FB, D), BF16),
        compiler_params=_cp("arbitrary", "arbitrary"),
    )(order, h2, dgu.reshape(NCHIP, 2, t_len, FB))


def _scaled_wgrad(name, a, dx, w, gate_row, mod, order):
    nb, t_len, kb = a.shape
    tk = min(TK, t_len)
    nk = t_len // tk
    cpb = NCHIP // nb
    rows = kb // (2 * cpb)

    def body(ord_ref, a_ref, dx_ref, w_ref, mod_ref, parts_ref, dg_ref, acc, *rs):
        p, k = pl.program_id(0), pl.program_id(1)
        j = ord_ref[p]

        @pl.when((p == 0) & (k == 0))
        def _():
            dg_ref[...] = jnp.zeros_like(dg_ref)

        @pl.when(k == 0)
        def _():
            acc[...] = jnp.zeros_like(acc)

        acc[...] += _dot_tn(a_ref[0], dx_ref[...])

        @pl.when(k == nk - 1)
        def _():
            z = acc[...]
            zg = (z * mod_ref[gate_row:gate_row + 1, :]).astype(BF16)
            dg_ref[0:1, :] += jnp.sum(z * w_ref[...].astype(F32), axis=0, keepdims=True)
            for i in range(cpb):
                q = j * cpb + i
                for half in range(2):
                    rs[0][q, half] = zg[(2 * i + half) * rows:(2 * i + half + 1) * rows]
                _rs2_to_sibling(q, rs)

        if cpb == 1:
            @pl.when((k == nk - 1) & (p > 0))
            def _():
                _rs2_forward(ord_ref[p - 1], parts_ref, rs)

        @pl.when((p == nb - 1) & (k == nk - 1))
        def _():
            for i in range(cpb):
                _rs2_forward(j * cpb + i, parts_ref, rs)
            _rs2_finish(parts_ref, rs)

    return pl.pallas_call(
        body, name=name,
        grid_spec=pltpu.PrefetchScalarGridSpec(
            num_scalar_prefetch=1, grid=(nb, nk),
            in_specs=[pl.BlockSpec((1, tk, kb), lambda p, k, o: (o[p], k, 0)),
                      pl.BlockSpec((tk, D), lambda p, k, o: (k, 0)),
                      pl.BlockSpec((kb, D), lambda p, k, o: (o[p], 0)),
                      pl.BlockSpec((8, D), lambda p, k, o: (0, 0))],
            out_specs=[_ANY, pl.BlockSpec((8, D), lambda p, k, o: (0, 0))],
            scratch_shapes=[pltpu.VMEM((kb, D), F32)] + _rs2_scratch((rows, D))),
        out_shape=[jax.ShapeDtypeStruct((NCHIP, rows, D), BF16), jax.ShapeDtypeStruct((8, D), F32)],
        compiler_params=_cp("arbitrary", "arbitrary"),
    )(order, a, dx, w, mod)


M_WA, M_WB, M_CBIAS, M_BA, M_BX, M_LS = 0, 3, 7, 8, 9, 10


def _conv_bwd_rows(tt, proj_ref, prm_ref, xe, ve, due, dye, dp_ref, acc8):
    row = lax.broadcasted_iota(jnp.int32, (8, CG), 0)
    w_b = [prm_ref[P_WB + k:P_WB + k + 1, :] for k in range(4)]
    w_a = [prm_ref[P_WA + k:P_WA + k + 1, :] for k in range(3)]

    def blk(ib, carry):
        r0 = pl.multiple_of(ib * 16, 16)
        rows = pl.ds(r0, 16)
        for g in range(D // CG):
            cs = slice(g * CG, (g + 1) * CG)
            du16, du_after = due[rows, cs], due[pl.ds(r0 + 16, 8), cs]
            dy16, dy_after = dye[rows, cs], dye[pl.ds(r0 + 16, 8), cs]
            cc16 = _pj(proj_ref, 1, rows, cs).astype(F32)
            cx16 = _pj(proj_ref, 2, rows, cs).astype(F32)
            x16 = _pj(proj_ref, 3, rows, cs).astype(F32)
            v16 = cc16 * cx16
            xp, vp = xe[pl.ds(r0, 8), cs], ve[pl.ds(r0, 8), cs]
            xe[pl.ds(r0 + 16, 8), cs] = x16[8:16]
            ve[pl.ds(r0 + 16, 8), cs] = v16[8:16]
            acc = [acc8[8 * k:8 * k + 8, cs] for k in range(8)]
            drx, dv = [], []
            for sb in range(2):
                lo = slice(8 * sb, 8 * sb + 8)
                duc, dyc, xc, vc = du16[lo], dy16[lo], x16[lo], v16[lo]
                du_n = du16[8:16] if sb == 0 else du_after
                dy_n = dy16[8:16] if sb == 0 else dy_after
                acc[0] = acc[0] + duc
                acc[4] = acc[4] + duc * xc
                d8 = w_b[3][:, cs] * duc
                for s in (1, 2, 3):
                    acc[4 - s] = acc[4 - s] + duc * _shift_down(xc, xp, s, row)
                    d8 = d8 + w_b[3 - s][:, cs] * _shift_up(duc, du_n, s, row)
                acc[7] = acc[7] + dyc * vc
                e8 = w_a[2][:, cs] * dyc
                for s in (1, 2):
                    acc[7 - s] = acc[7 - s] + dyc * _shift_down(vc, vp, s, row)
                    e8 = e8 + w_a[2 - s][:, cs] * _shift_up(dyc, dy_n, s, row)
                drx.append(d8)
                dv.append(e8)
                xp, vp = xc, vc
            for k in range(8):
                acc8[8 * k:8 * k + 8, cs] = acc[k]
            dv16 = jnp.concatenate(dv, axis=0)
            col = lambda s: slice(s * D + g * CG, s * D + (g + 1) * CG)
            dp_ref[rows, col(3)] = jnp.concatenate(drx, axis=0).astype(BF16)
            dp_ref[rows, col(1)] = (dv16 * cx16).astype(BF16)
            dp_ref[rows, col(2)] = (dv16 * cc16).astype(BF16)
        return carry

    lax.fori_loop(0, tt // 16, blk, 0)


def _mixer_bwd(proj, hl, sv, dmg, prm, wa, wx):
    t_len = proj.shape[0]
    tt = min(TT, t_len)
    nt = t_len // tt
    hb8 = tt // 8

    def rev(i):
        return nt - 1 - i

    def halo(i):
        return jnp.maximum(rev(i) * hb8 - 1, 0)

    def body(proj_ref, ph_ref, hl_ref, hh_ref, sv_ref, dmg_ref, prm_ref, wa_ref, wx_ref,
             dp_ref, sums_ref, gwa_ref, gwx_ref,
             xe, ve, he, due, dye, drp_s, dip_s, an, gn, acc8):
        i = pl.program_id(0)
        t = rev(i)

        @pl.when(i == 0)
        def _():
            sums_ref[...] = jnp.zeros_like(sums_ref)
            gwa_ref[...] = jnp.zeros_like(gwa_ref)
            gwx_ref[...] = jnp.zeros_like(gwx_ref)
            due[tt:tt + 8, :] = jnp.zeros((8, D), F32)
            dye[tt:tt + 8, :] = jnp.zeros((8, D), F32)
            an[...] = jnp.zeros((8, D), F32)
            gn[...] = jnp.zeros((8, D), F32)

        live = (t > 0).astype(F32)
        xe[0:8, :] = _pj(ph_ref, 3).astype(F32) * live
        ve[0:8, :] = _pj(ph_ref, 1).astype(F32) * _pj(ph_ref, 2).astype(F32) * live
        he[0:8, :] = hh_ref[...].astype(F32) * live
        he[8:8 + tt, :] = hl_ref[...].astype(F32)

        ls_all = _log_sigmoid(prm_ref[P_LAM:P_LAM + 1, :])
        row = lax.broadcasted_iota(jnp.int32, (8, CG), 0)
        nblk = tt // 16

        def blk(ib, carry):
            r0 = pl.multiple_of((nblk - 1 - ib) * 16, 16)
            rows = pl.ds(r0, 16)
            for g in range(D // CG):
                cs = slice(g * CG, (g + 1) * CG)
                ls = ls_all[:, cs]
                dm = dmg_ref[rows, cs].astype(F32)
                cb = _pj(proj_ref, 0, rows, cs).astype(F32)
                rg = _pj(proj_ref, 4, rows, cs).astype(F32)
                sga = _sig(_pj(proj_ref, 5, rows, cs).astype(F32))
                sgb = _sig(_pj(proj_ref, 6, rows, cs).astype(F32))
                ya0 = sv_ref[SV_YA, rows, cs]
                h16 = he[pl.ds(r0 + 8, 16), cs]
                gl, th = _gelu(rg)
                dgl = 0.5 * (1.0 + th) + 0.5 * rg * (1.0 - th * th) * (_GC * (1.0 + 3.0 * 0.044715 * rg * rg))
                y_a = cb * ya0
                y_b = h16 * gl
                dy_a = dm * sga
                dy_b = dm * sgb
                col = lambda s: slice(s * D + g * CG, s * D + (g + 1) * CG)
                dp_ref[rows, col(5)] = (dm * y_a * sga * (1.0 - sga)).astype(BF16)
                dp_ref[rows, col(6)] = (dm * y_b * sgb * (1.0 - sgb)).astype(BF16)
                dp_ref[rows, col(4)] = (dy_b * h16 * dgl).astype(BF16)
                dp_ref[rows, col(0)] = (dy_a * ya0).astype(BF16)
                dye[rows, cs] = dy_a * cb
                dh16 = dy_b * gl

                a_next = an[:, cs]
                g_next = gn[:, cs]
                s_ba = jnp.zeros((8, CG), F32)
                s_bx = jnp.zeros((8, CG), F32)
                s_ls = jnp.zeros((8, CG), F32)
                for sb in (1, 0):
                    rr = r0 + 8 * sb
                    first = (row + (t * tt + rr)) == 0
                    r8 = pl.ds(rr, 8)
                    uu, r, ig, a, mult = (sv_ref[pln, r8, cs] for pln in (SV_U, SV_R, SV_I, SV_A, SV_MULT))
                    ca = jnp.where(row < 7, pltpu.roll(a, 7, 0), a_next)
                    cb_ = dh16[8 * sb:8 * sb + 8, :]
                    for s in (1, 2, 4):
                        a_sh = jnp.where(row < 8 - s, pltpu.roll(ca, 8 - s, 0), 1.0)
                        b_sh = jnp.where(row < 8 - s, pltpu.roll(cb_, 8 - s, 0), 0.0)
                        cb_ = ca * b_sh + cb_
                        ca = ca * a_sh
                    gv = ca * g_next + cb_
                    g_next = jnp.broadcast_to(gv[0:1, :], gv.shape)
                    a_next = jnp.broadcast_to(a[0:1, :], a.shape)
                    hprev = jnp.where(row >= 1, pltpu.roll(he[pl.ds(rr + 8, 8), cs], 1, 0),
                                      pltpu.roll(he[pl.ds(rr, 8), cs], 1, 0))
                    da = gv * hprev
                    dmult = jnp.where(first, 0.0, gv * ig * uu)
                    dla = da * a + jnp.where(mult > 0.0, dmult * (-(a * a) / mult), 0.0)
                    drp = dla * (LRU_C * ls) * r * (1.0 - r)
                    dip = gv * mult * uu * ig * (1.0 - ig)
                    s_ls = s_ls + dla * (LRU_C * r)
                    s_ba = s_ba + drp
                    s_bx = s_bx + dip
                    drp_s[pl.ds(rr, 8), cs] = drp
                    dip_s[pl.ds(rr, 8), cs] = dip
                    due[pl.ds(rr, 8), cs] = gv * mult * ig
                an[:, cs] = a_next
                gn[:, cs] = g_next
                sums_ref[M_BA:M_BA + 1, cs] += jnp.sum(s_ba, axis=0, keepdims=True)
                sums_ref[M_BX:M_BX + 1, cs] += jnp.sum(s_bx, axis=0, keepdims=True)
                sums_ref[M_LS:M_LS + 1, cs] += jnp.sum(s_ls, axis=0, keepdims=True)
            return carry

        lax.fori_loop(0, nblk, blk, 0)

        drp_b = drp_s[...].astype(BF16)
        dip_b = dip_s[...].astype(BF16)
        ub = sv_ref[SV_U].astype(BF16)
        for h in range(HEADS):
            cs = slice(h * HB, (h + 1) * HB)
            due[0:tt, cs] += _dot_nt(drp_b[:, cs], wa_ref[h]) + _dot_nt(dip_b[:, cs], wx_ref[h])
            gwa_ref[h] += _dot_tn(ub[:, cs], drp_b[:, cs])
            gwx_ref[h] += _dot_tn(ub[:, cs], dip_b[:, cs])

        acc8[...] = jnp.zeros_like(acc8)
        _conv_bwd_rows(tt, proj_ref, prm_ref, xe, ve, due, dye, dp_ref, acc8)
        for k, dst in enumerate([M_CBIAS] + [M_WB + k for k in range(4)] + [M_WA + k for k in range(3)]):
            sums_ref[dst:dst + 1, :] += jnp.sum(acc8[8 * k:8 * k + 8, :], axis=0, keepdims=True)
        due[tt:tt + 8, :] = due[0:8, :]
        dye[tt:tt + 8, :] = dye[0:8, :]

        @pl.when(i == nt - 1)
        def _():
            sums_ref[M_LS:M_LS + 1, :] = sums_ref[M_LS:M_LS + 1, :] * _sig(-prm_ref[P_LAM:P_LAM + 1, :])

    big = lambda: pltpu.VMEM((tt + 8, D), F32)
    tile = lambda: pltpu.VMEM((tt, D), F32)
    return pl.pallas_call(
        body, name="mixer_bwd", grid=(nt,),
        in_specs=[pl.BlockSpec((tt, 7 * D), lambda i: (rev(i), 0)),
                  pl.BlockSpec((8, 7 * D), lambda i: (halo(i), 0)),
                  pl.BlockSpec((tt, D), lambda i: (rev(i), 0)),
                  pl.BlockSpec((8, D), lambda i: (halo(i), 0)),
                  pl.BlockSpec((len(SV_PLANES), tt, D), lambda i: (0, rev(i), 0)),
                  pl.BlockSpec((tt, D), lambda i: (rev(i), 0)),
                  pl.BlockSpec((16, D), lambda i: (0, 0)),
                  pl.BlockSpec((HEADS, HB, HB), lambda i: (0, 0, 0)),
                  pl.BlockSpec((HEADS, HB, HB), lambda i: (0, 0, 0))],
        out_specs=[pl.BlockSpec((tt, 7 * D), lambda i: (rev(i), 0)),
                   pl.BlockSpec((16, D), lambda i: (0, 0)),
                   pl.BlockSpec((HEADS, HB, HB), lambda i: (0, 0, 0)),
                   pl.BlockSpec((HEADS, HB, HB), lambda i: (0, 0, 0))],
        out_shape=[jax.ShapeDtypeStruct((t_len, 7 * D), BF16), jax.ShapeDtypeStruct((16, D), F32),
                   jax.ShapeDtypeStruct((HEADS, HB, HB), F32), jax.ShapeDtypeStruct((HEADS, HB, HB), F32)],
        scratch_shapes=[big(), big(), big(), big(), big(), tile(), tile(),
                        pltpu.VMEM((8, D), F32), pltpu.VMEM((8, D), F32), pltpu.VMEM((64, D), F32)],
        compiler_params=_cp("arbitrary"),
    )(proj, proj, hl, hl, sv, dmg, prm, wa, wx)


def _in_proj_bwd(dproj, w_in, x, dx1, mod, g_mix):
    t_len = x.shape[0]
    tm = min(TM, t_len)

    def body(dp_ref, w_ref, x_ref, dx1_ref, mod_ref, g_ref, gx_ref, sums_ref, acc):
        @pl.when(pl.program_id(0) == 0)
        def _():
            sums_ref[...] = jnp.zeros_like(sums_ref)

        def write(rows, dx):
            gx_ref[rows, :] = dx

        zero = jnp.zeros((8, D), F32)
        sums = (zero, zero, zero)
        for sub in _sub_blocks(tm):
            acc[sub, :] = _dot_nt(dp_ref[sub, :], w_ref[...])
            sums = _norm_bwd_rows(sub, sums, acc, x_ref, dx1_ref, mod_ref[1:2, :], g_ref[...], write)
        _add_norm_sums(sums_ref, sums)

    return pl.pallas_call(
        body, name="in_proj_bwd", grid=(t_len // tm,),
        in_specs=[pl.BlockSpec((tm, 7 * D), lambda i: (i, 0)),
                  _resident((D, 7 * D)),
                  pl.BlockSpec((tm, D), lambda i: (i, 0)), pl.BlockSpec((tm, D), lambda i: (i, 0)),
                  pl.BlockSpec((8, D), lambda i: (0, 0)), pl.BlockSpec((1, D), lambda i: (0, 0))],
        out_specs=[pl.BlockSpec((tm, D), lambda i: (i, 0)), pl.BlockSpec((8, D), lambda i: (0, 0))],
        out_shape=[jax.ShapeDtypeStruct((t_len, D), F32), jax.ShapeDtypeStruct((8, D), F32)],
        scratch_shapes=[pltpu.VMEM((tm, D), F32)],
        compiler_params=_cp("arbitrary"),
    )(dproj, w_in, x, dx1, mod, g_mix)


def _in_wgrad(h, dproj, g_wa, g_wx, order):
    t_len = h.shape[0]
    tk = min(TKI, t_len)
    nk = t_len // tk
    cw = 7 * D // NDEV
    hr = HB // NDEV

    def body(ord_ref, h_ref, d_ref, ga_ref, gx_ref, parts_ref, pa_ref, px_ref, acc, *scr):
        rs, sems = scr[:-6], scr[-6:]
        p, k = pl.program_id(0), pl.program_id(1)

        def head_rows(ref):
            return lambda s: ref.at[:, pl.ds(s * hr, hr), :]

        @pl.when((p == 0) & (k == 0))
        def _():
            for s in range(NDEV):
                _rs_send(head_rows(ga_ref)(s), pa_ref, s, *sems[0:3])
                _rs_send(head_rows(gx_ref)(s), px_ref, s, *sems[3:6])

        @pl.when(k == 0)
        def _():
            acc[...] = jnp.zeros_like(acc)

        acc[...] += _dot_tn(h_ref[...], d_ref[...])

        @pl.when(k == nk - 1)
        def _():
            q = ord_ref[p]
            for half in range(2):
                rs[0][q, half] = acc[:, half * cw:(half + 1) * cw].astype(BF16)
            _rs2_to_sibling(q, rs)

        @pl.when((k == nk - 1) & (p > 0))
        def _():
            _rs2_forward(ord_ref[p - 1], parts_ref, rs)

        @pl.when((p == NCHIP - 1) & (k == nk - 1))
        def _():
            _rs2_forward(ord_ref[p], parts_ref, rs)
            _rs2_finish(parts_ref, rs)
            _rs_finish(head_rows(ga_ref), pa_ref, *sems[0:3])
            _rs_finish(head_rows(gx_ref), px_ref, *sems[3:6])

    return pl.pallas_call(
        body, name="in_wgrad",
        grid_spec=pltpu.PrefetchScalarGridSpec(
            num_scalar_prefetch=1, grid=(NCHIP, nk),
            in_specs=[pl.BlockSpec((tk, D), lambda p, k, o: (k, 0)),
                      pl.BlockSpec((tk, 2 * cw), lambda p, k, o: (k, o[p])), _ANY, _ANY],
            out_specs=[_ANY, _ANY, _ANY],
            scratch_shapes=[pltpu.VMEM((D, 2 * cw), F32)] + _rs2_scratch((D, cw)) + _RS_SEMS * 2),
        out_shape=[jax.ShapeDtypeStruct((NCHIP, D, cw), BF16), jax.ShapeDtypeStruct((NDEV, HEADS, hr, HB), F32),
                   jax.ShapeDtypeStruct((NDEV, HEADS, hr, HB), F32)],
        compiler_params=_cp("arbitrary", "arbitrary"),
    )(order, h, dproj, g_wa, g_wx)


def _ada_fwd(c_all, w_ada, b_cols):
    def body(c_ref, w_ref, b_ref, o_ref):
        cv = c_ref[...]
        o_ref[...] = _dot((cv * _sig(cv)).astype(BF16), w_ref[...].astype(BF16)) + b_ref[...]

    return pl.pallas_call(body, name="ada_fwd", out_shape=jax.ShapeDtypeStruct((16, w_ada.shape[1]), F32),
                          compiler_params=_cp())(c_all, w_ada, b_cols)


def _adam_math(w, g, m, v):
    m = ADAM_B1 * m + (1.0 - ADAM_B1) * g
    v = ADAM_B2 * v + (1.0 - ADAM_B2) * (g * g)
    m_hat = m / (1.0 - ADAM_B1 ** ADAM_STEP)
    v_hat = v / (1.0 - ADAM_B2 ** ADAM_STEP)
    delta = -ADAM_LR * (m_hat / (jnp.sqrt(v_hat) + ADAM_EPS) + ADAM_WD * w)
    return delta, m, v


def _ada_bwd(c_all, dmod_cols, w, m, v):
    rb = 256
    n = w.shape[1]
    nrow = c_all.shape[0]

    def body(c_ref, d_ref, w_ref, m_ref, v_ref, g_ref, dl_ref, nm_ref, nv_ref):
        cv = c_ref[...]
        g = _dot_tn((cv * _sig(cv)).astype(BF16), d_ref[...].astype(BF16))
        g_ref[...] = g
        dl_ref[...], nm_ref[...], nv_ref[...] = _adam_math(w_ref[...], g, m_ref[...], v_ref[...])

    blk = pl.BlockSpec((rb, n), lambda i: (i, 0))
    sds = jax.ShapeDtypeStruct(w.shape, F32)
    return pl.pallas_call(
        body, name="ada_bwd", grid=(D // rb,),
        in_specs=[pl.BlockSpec((nrow, rb), lambda i: (0, i)), pl.BlockSpec((nrow, n), lambda i: (0, 0)), blk, blk, blk],
        out_specs=[blk, blk, blk, blk], out_shape=[sds, sds, sds, sds],
        compiler_params=_cp("parallel"),
    )(c_all, dmod_cols, w, m, v)


def _adam(name, parts, w, m, v):
    p, r, c = parts.shape
    rb = r
    for cand in (256, 128, 64, 32, 16, 8):
        if r % cand == 0 and r >= cand:
            rb = cand
            break

    def body(p_ref, w_ref, m_ref, v_ref, g_ref, dl_ref, nm_ref, nv_ref):
        g = p_ref[0].astype(F32)
        for q in range(1, p):
            g = g + p_ref[q].astype(F32)
        g_ref[...] = g
        dl_ref[...], nm_ref[...], nv_ref[...] = _adam_math(w_ref[...], g, m_ref[...], v_ref[...])

    blk = pl.BlockSpec((rb, c), lambda i: (i, 0))
    sds = jax.ShapeDtypeStruct((r, c), F32)
    return pl.pallas_call(
        body, name=name, grid=(r // rb,),
        in_specs=[pl.BlockSpec((p, rb, c), lambda i: (0, i, 0)), blk, blk, blk],
        out_specs=[blk, blk, blk, blk], out_shape=[sds, sds, sds, sds],
        compiler_params=_cp("parallel"),
    )(parts, w, m, v)


def _my_pos():
    return lax.axis_index("x"), lax.axis_index("y"), lax.axis_index("c")


def _all_gather_small(name, v):
    m_per, n = v.shape

    def body(x_ref, out_ref, send_sems, recv_sems, local_sem):
        x, y, c = _my_pos()
        me, sibling = (x, y, c), (x, y, 1 - c)
        chips = [(1 - x, y), (x, 1 - y), (1 - x, 1 - y)]

        def rows(px, py, pc):
            return out_ref.at[pl.ds((4 * px + 2 * py + pc) * m_per, m_per), :]

        def copy(k, block, to, src=None):
            return pltpu.make_async_remote_copy(
                src_ref=rows(*block) if src is None else src, dst_ref=rows(*block),
                send_sem=send_sems.at[k], recv_sem=recv_sems.at[k], device_id=to, device_id_type=MESH)

        mine = pltpu.make_async_copy(x_ref, rows(*me), local_sem)
        mine.start()
        first = [copy(0, me, sibling, src=x_ref)]
        first += [copy(1 + j, me, (*chip, c), src=x_ref) for j, chip in enumerate(chips)]
        for cp in first:
            cp.start()
        passed = [copy(4 + j, (*chip, c), sibling) for j, chip in enumerate(chips)]
        for j, chip in enumerate(chips):
            copy(1 + j, (*chip, c), me).wait_recv()
            passed[j].start()
        copy(0, sibling, me).wait_recv()
        for j, chip in enumerate(chips):
            copy(4 + j, (*chip, 1 - c), me).wait_recv()
        for cp in first + passed:
            cp.wait_send()
        mine.wait()

    return pl.pallas_call(
        body, name=name, out_shape=jax.ShapeDtypeStruct((NDEV * m_per, n), v.dtype),
        in_specs=[pl.BlockSpec(memory_space=pltpu.VMEM)], out_specs=pl.BlockSpec(memory_space=pltpu.VMEM),
        scratch_shapes=[pltpu.SemaphoreType.DMA((7,)), pltpu.SemaphoreType.DMA((7,)), pltpu.SemaphoreType.DMA],
    )(v)


def _blk_cols(n):
    return lambda ref, b: ref.at[:, pl.ds(pl.multiple_of(b * n, 128), n)]


def _blk_rows(n):
    return lambda ref, b: ref.at[pl.ds(pl.multiple_of(b * n, 8), n), :]


def _blk_lead(ref, b):
    return ref.at[b]


def _blk_heads(ref, b):
    return ref.at[:, pl.ds(pl.multiple_of(b * (HB // NDEV), 8), HB // NDEV), :]


def _ag_phases(ins, outs, slicers, send_sems, recv_sems, local_sems):
    na = len(ins)
    x, y, c = _my_pos()
    me, sibling = (x, y, c), (x, y, 1 - c)
    chips = [(1 - x, y), (x, 1 - y), (1 - x, 1 - y)]

    def copy(a, k, block, to, from_shard=False):
        px, py, pc = block
        dst = slicers[a](outs[a], 4 * px + 2 * py + pc)
        return pltpu.make_async_remote_copy(
            src_ref=ins[a] if from_shard else dst, dst_ref=dst,
            send_sem=send_sems.at[a * 7 + k], recv_sem=recv_sems.at[a * 7 + k], device_id=to, device_id_type=MESH)

    def local(a):
        return pltpu.make_async_copy(ins[a], slicers[a](outs[a], 4 * x + 2 * y + c), local_sems.at[a])

    def firsts(a):
        return [copy(a, 0, me, sibling, True)] + [copy(a, 1 + j, me, (*chip, c), True) for j, chip in enumerate(chips)]

    def start():
        for a in range(na):
            local(a).start()
            for cp in firsts(a):
                cp.start()

    def forward():
        for a in range(na):
            for j, chip in enumerate(chips):
                copy(a, 1 + j, (*chip, c), me).wait_recv()
                copy(a, 4 + j, (*chip, c), sibling).start()

    def finish():
        for a in range(na):
            copy(a, 0, sibling, me).wait_recv()
            for j, chip in enumerate(chips):
                copy(a, 4 + j, (*chip, 1 - c), me).wait_recv()
        for a in range(na):
            for cp in firsts(a) + [copy(a, 4 + j, (*chip, c), sibling) for j, chip in enumerate(chips)]:
                cp.wait_send()
            local(a).wait()

    return start, forward, finish


def _ag_sems(na):
    return [pltpu.SemaphoreType.DMA((7 * na,)), pltpu.SemaphoreType.DMA((7 * na,)), pltpu.SemaphoreType.DMA((na,))]


def _all_gather_weights(shards, fulls, slicers):
    na = len(shards)

    def body(*refs):
        start, forward, finish = _ag_phases(refs[:na], refs[na:2 * na], slicers, *refs[2 * na:])
        start()
        forward()
        finish()

    return pl.pallas_call(
        body, name="gather_weights",
        out_shape=[jax.ShapeDtypeStruct(s, sh.dtype) for s, sh in zip(fulls, shards)],
        in_specs=[_ANY] * na, out_specs=[_ANY] * na, scratch_shapes=_ag_sems(na),
    )(*shards)


def _scatter_grads(grads, shard_shapes, slicers):
    na = len(grads)

    def body(*refs):
        ins, outs = refs[:na], refs[na:2 * na]
        send_sems, recv_sems, local_sems = refs[2 * na:]
        x, y, c = _my_pos()
        me = 4 * x + 2 * y + c
        mine, sent = [], []
        for a in range(na):
            cp = pltpu.make_async_copy(slicers[a](ins[a], me), outs[a].at[me], local_sems.at[a])
            cp.start()
            mine.append(cp)
        rel = [(k >> 2 & 1, k >> 1 & 1, k & 1) for k in range(1, NDEV)]
        for a in range(na):
            for k, (fx, fy, fc) in enumerate(rel):
                px, py, pc = x ^ fx, y ^ fy, c ^ fc
                cp = pltpu.make_async_remote_copy(
                    src_ref=slicers[a](ins[a], 4 * px + 2 * py + pc), dst_ref=outs[a].at[me],
                    send_sem=send_sems.at[a * 7 + k], recv_sem=recv_sems.at[a * 7 + k],
                    device_id=(px, py, pc), device_id_type=MESH)
                cp.start()
                sent.append(cp)
        for a in range(na):
            for k, (fx, fy, fc) in enumerate(rel):
                px, py, pc = x ^ fx, y ^ fy, c ^ fc
                src = 4 * px + 2 * py + pc
                pltpu.make_async_remote_copy(
                    src_ref=slicers[a](ins[a], me), dst_ref=outs[a].at[src],
                    send_sem=send_sems.at[a * 7 + k], recv_sem=recv_sems.at[a * 7 + k],
                    device_id=(px, py, pc), device_id_type=MESH).wait_recv()
        for cp in sent:
            cp.wait_send()
        for cp in mine:
            cp.wait()

    any_spec = pl.BlockSpec(memory_space=pl.ANY)
    return pl.pallas_call(
        body, name="scatter_grads",
        out_shape=[jax.ShapeDtypeStruct((NDEV,) + tuple(s), g.dtype) for s, g in zip(shard_shapes, grads)],
        in_specs=[any_spec] * na, out_specs=[any_spec] * na,
        scratch_shapes=[pltpu.SemaphoreType.DMA((7 * na,)), pltpu.SemaphoreType.DMA((7 * na,)),
                        pltpu.SemaphoreType.DMA((na,))],
    )(*grads)


def _local_step(x, target, mod, g_mix, g_ffn, g_fin, prm, w_in_shard, shards):
    fulls = [(HEADS, HB, HB), (HEADS, HB, HB), (D, D), (NDEV, FB, D), (DFF, D)]
    slicers = [_blk_heads, _blk_heads, _blk_rows(D // NDEV), _blk_lead, _blk_rows(DFF // NDEV)]
    my_chip = _my_index() >> 1
    own_first = (my_chip ^ jnp.arange(NCHIP, dtype=jnp.int32)).astype(jnp.int32)
    early, late = [0, 1, 2, 4], [3]
    pick = lambda lst, idx: [lst[i] for i in idx]
    proj, h, w_in, (wa, wx, w_out, w_down) = _in_proj(x, mod, g_mix, w_in_shard, own_first, pick(shards, early),
                                                      pick(fulls, early), pick(slicers, early))
    merged, hl, sv, (w_gu,) = _mixer_fwd(proj, prm, wa, wx, pick(shards, late), pick(fulls, late),
                                         pick(slicers, late))
    w_gu = w_gu.reshape(2, 4, FB, D)
    x1, h2 = _out_proj(merged, x, mod, g_ffn, w_out)
    gu, dx2, dx2b, loss, d_gfin = _ffn_fwd(h2, x1, target, mod, g_fin, w_gu, w_down)
    dgu, act, dx1, dx1b, dmg, sums2 = _ffn_bwd(dx2, gu, x1, mod, g_ffn, w_gu, w_down, w_out)
    chip_order = _xor_order(_my_index() >> 1, NCHIP)
    p_wgu = _gu_wgrad(h2, dgu, chip_order)
    p_wdown, d_gt2 = _scaled_wgrad("down_wgrad", act, dx2b, w_down, 5, mod, chip_order)
    p_wout, d_gt1 = _scaled_wgrad("out_wgrad", merged.reshape(1, *merged.shape), dx1b, w_out, 2, mod,
                                  jnp.zeros((1,), jnp.int32))
    dproj, msums, g_wa, g_wx = _mixer_bwd(proj, hl, sv, dmg, prm, wa, wx)
    p_win, p_wa, p_wx = _in_wgrad(h, dproj, g_wa, g_wx, chip_order)
    grad_x, sums1 = _in_proj_bwd(dproj, w_in, x, dx1, mod, g_mix)
    return dict(loss=loss, grad_x=grad_x, d_gfin=d_gfin, sums1=sums1, sums2=sums2, msums=msums,
                d_gt1=d_gt1[0:1], d_gt2=d_gt2[0:1], p_win=p_win, p_wa=p_wa, p_wx=p_wx, p_wout=p_wout, p_wgu=p_wgu,
                p_wdown=p_wdown)


def kernel(x, c, w_ada, b_ada, g_norm_mix, w_in, conv_a_w, conv_b_w, conv_b_bias, w_rg_a, b_rg_a, w_rg_x, b_rg_x, lru_lambda, w_out, g_norm_ffn, w_gate_up, w_down, g_norm_final, loss_target, m_w_ada, m_b_ada, m_g_norm_mix, m_w_in, m_conv_a_w, m_conv_b_w, m_conv_b_bias, m_w_rg_a, m_b_rg_a, m_w_rg_x, m_b_rg_x, m_lru_lambda, m_w_out, m_g_norm_ffn, m_w_gate_up, m_w_down, m_g_norm_final, v_w_ada, v_b_ada, v_g_norm_mix, v_w_in, v_conv_a_w, v_conv_b_w, v_conv_b_bias, v_w_rg_a, v_b_rg_a, v_w_rg_x, v_b_rg_x, v_lru_lambda, v_w_out, v_g_norm_ffn, v_w_gate_up, v_w_down, v_g_norm_final):
    me = 4 * lax.axis_index("x") + 2 * lax.axis_index("y") + lax.axis_index("c")
    ncol = w_ada.shape[2]
    cw = conv_a_w.shape[2]

    pack0 = jnp.concatenate([c, conv_a_w.reshape(1, 3 * cw), conv_b_w.reshape(1, 4 * cw)], axis=1)
    got0 = _all_gather_small("gather_c", jnp.broadcast_to(pack0, (8, pack0.shape[1])))
    got0 = got0.reshape(NDEV, 8, -1)[:, 0, :]
    c_all = got0[:, :D]
    conv_a = got0[:, D:D + 3 * cw].reshape(NDEV, 3, cw).transpose(1, 0, 2).reshape(3, D)
    conv_b = got0[:, D + 3 * cw:].reshape(NDEV, 4, cw).transpose(1, 0, 2).reshape(4, D)

    b_cols = lax.dynamic_slice_in_dim(b_ada, me * ncol, ncol, axis=1)
    c16 = jnp.concatenate([c_all, jnp.zeros((8, D), F32)], axis=0)
    mod_cols = _ada_fwd(c16, w_ada[0], b_cols)[:NDEV]
    got1 = _all_gather_small("gather_mod", mod_cols).reshape(NDEV, NDEV, ncol)
    mod6 = lax.dynamic_index_in_dim(got1, me, axis=1, keepdims=False).reshape(6, D)
    mod = jnp.concatenate([mod6, jnp.zeros((2, D), F32)], axis=0)

    tr = lambda a: jnp.swapaxes(a, 1, 2)
    shards = [w_rg_a[0].astype(BF16), w_rg_x[0].astype(BF16), w_out[0].astype(BF16), tr(w_gate_up)[0].astype(BF16),
              w_down[0].astype(BF16)]

    prm = jnp.concatenate([conv_a, conv_b, conv_b_bias, b_rg_a, b_rg_x, lru_lambda, jnp.zeros((5, D), F32)], axis=0)
    r = _local_step(x[0], loss_target[0], mod, g_norm_mix, g_norm_ffn, g_norm_final.reshape(1, D), prm,
                    w_in[0].astype(BF16), shards)

    parts = [r["p_win"], r["p_wa"], r["p_wx"], r["p_wout"], r["p_wgu"], r["p_wdown"]]
    big = {}
    for nm, p, w, m, v in (("w_in", parts[0], w_in, m_w_in, v_w_in), ("w_rg_a", parts[1], w_rg_a, m_w_rg_a, v_w_rg_a),
                           ("w_rg_x", parts[2], w_rg_x, m_w_rg_x, v_w_rg_x), ("w_out", parts[3], w_out, m_w_out, v_w_out),
                           ("w_gate_up", parts[4], tr(w_gate_up), tr(m_w_gate_up), tr(v_w_gate_up)),
                           ("w_down", parts[5], w_down, m_w_down, v_w_down)):
        two_d = (-1, w.shape[-1])
        outs = _adam("adam_" + nm, p.reshape((p.shape[0],) + w.reshape(two_d).shape), w.reshape(two_d), m.reshape(two_d),
                     v.reshape(two_d))
        big[nm] = [o.reshape(w.shape) for o in outs]
    big["w_gate_up"] = [tr(o) for o in big["w_gate_up"]]

    small = jnp.concatenate([
        r["sums1"][S_SH:S_SH + 1], r["sums1"][S_SC:S_SC + 1], r["d_gt1"],
        r["sums2"][S_SH:S_SH + 1], r["sums2"][S_SC:S_SC + 1], r["d_gt2"],
        r["sums1"][S_G:S_G + 1],
        r["msums"][M_CBIAS:M_CBIAS + 1], r["msums"][M_BA:M_BA + 1], r["msums"][M_BX:M_BX + 1],
        r["msums"][M_LS:M_LS + 1],
        r["sums2"][S_G:S_G + 1], r["d_gfin"],
        r["msums"][M_WA:M_WA + 3], r["msums"][M_WB:M_WB + 4],
        jnp.broadcast_to(r["loss"][0:1, 0:1], (1, D)),
        jnp.zeros((3, D), F32)], axis=0)
    got2 = _all_gather_small("gather_small", small).reshape(NDEV, 24, D)

    rep_w = jnp.concatenate([b_ada.reshape(6, D), g_norm_mix, conv_b_bias, b_rg_a, b_rg_x, lru_lambda, g_norm_ffn,
                             g_norm_final.reshape(1, D), jnp.zeros((3, D), F32)], axis=0)
    rep_m = jnp.concatenate([m_b_ada.reshape(6, D), m_g_norm_mix, m_conv_b_bias, m_b_rg_a, m_b_rg_x, m_lru_lambda,
                             m_g_norm_ffn, m_g_norm_final.reshape(1, D), jnp.zeros((3, D), F32)], axis=0)
    rep_v = jnp.concatenate([v_b_ada.reshape(6, D), v_g_norm_mix, v_conv_b_bias, v_b_rg_a, v_b_rg_x, v_lru_lambda,
                             v_g_norm_ffn, v_g_norm_final.reshape(1, D), jnp.ones((3, D), F32)], axis=0)
    rep = _adam("adam_rep", got2[:, :16, :], rep_w, rep_m, rep_v)

    conv_parts = lax.dynamic_slice_in_dim(got2[:, 13:21, :], me * cw, cw, axis=2)
    cv_w = jnp.concatenate([conv_a_w[0], conv_b_w[0], jnp.zeros((1, cw), F32)], axis=0)
    cv_m = jnp.concatenate([m_conv_a_w[0], m_conv_b_w[0], jnp.zeros((1, cw), F32)], axis=0)
    cv_v = jnp.concatenate([v_conv_a_w[0], v_conv_b_w[0], jnp.ones((1, cw), F32)], axis=0)
    cvo = _adam("adam_conv", conv_parts, cv_w, cv_m, cv_v)

    dmod_cols = lax.dynamic_slice_in_dim(got2[:, :6, :].reshape(NDEV, 6 * D), me * ncol, ncol, axis=1)
    dmod16 = jnp.concatenate([dmod_cols, jnp.zeros((8, ncol), F32)], axis=0)
    ada = _ada_bwd(c16, dmod16, w_ada[0], m_w_ada[0], v_w_ada[0])

    loss = jnp.sum(got2[:, 20, 0])

    def pick(q):
        one = lambda i: rep[q][i:i + 1]
        return [ada[q].reshape(w_ada.shape), rep[q][0:6].reshape(b_ada.shape), one(6), big["w_in"][q],
                cvo[q][0:3].reshape(conv_a_w.shape), cvo[q][3:7].reshape(conv_b_w.shape), one(7),
                big["w_rg_a"][q], one(8), big["w_rg_x"][q], one(9), one(10), big["w_out"][q], one(11),
                big["w_gate_up"][q], big["w_down"][q], rep[q][12]]

    return (loss, r["grad_x"].reshape(x.shape), *pick(0), *pick(1), *pick(2), *pick(3))
```

```python
import math

import jax
import jax.numpy as jnp
from jax import lax
from jax.experimental import pallas as pl
from jax.experimental.pallas import tpu as pltpu

F32 = jnp.float32
BF16 = jnp.bfloat16

D = 1024
DFF = 2816
NDEV = 8
HEADS = 4
HB = D // HEADS
FB = DFF // 4
EPS = 1e-6
LRU_C = 8.0
ADAM_LR, ADAM_B1, ADAM_B2, ADAM_EPS, ADAM_WD, ADAM_STEP = 0.001, 0.9, 0.999, 1e-08, 0.01, 10

VMEM_LIMIT = 56 * 1024 * 1024
TM = 512
TMI = 1024
TMF = 256
TK = 2048
TKI = 2048
SUB = 256
TT = 256
CG = 256
MESH = pl.DeviceIdType.MESH


def _cp(*sem):
    return pltpu.CompilerParams(dimension_semantics=sem, vmem_limit_bytes=VMEM_LIMIT)


def _sig(x):
    return 1.0 / (1.0 + jnp.exp(-x))


def _log_sigmoid(x):
    z = jnp.exp(-jnp.abs(x))
    u = 1.0 + z
    d = u - 1.0
    l1p = jnp.where(d == 0.0, z, jnp.log(u) * (z / jnp.where(d == 0.0, 1.0, d)))
    return -(jnp.maximum(-x, 0.0) + l1p)


def _neg_expm1(x):
    p = x * (1.0 + x * 0.5 * (1.0 + x * (1.0 / 3.0) * (1.0 + x * 0.25 * (1.0 + x * 0.2 * (1.0 + x * (1.0 / 6.0))))))
    return jnp.where(x > -0.25, -p, 1.0 - jnp.exp(x))


_GC = math.sqrt(2.0 / math.pi)


def _gelu(x):
    t = jnp.tanh(_GC * (x + 0.044715 * x * x * x))
    return 0.5 * x * (1.0 + t), t


def _dot(a, b):
    return jnp.dot(a, b, preferred_element_type=F32)


def _dot_nt(a, b):
    return lax.dot_general(a, b, (((1,), (1,)), ((), ())), preferred_element_type=F32)


def _dot_tn(a, b):
    return lax.dot_general(a, b, (((0,), (0,)), ((), ())), preferred_element_type=F32)


def _resident(shape):
    return pl.BlockSpec(shape, lambda *_: (0,) * len(shape), pipeline_mode=pl.Buffered(1))


def _sub_blocks(n_rows):
    step = min(SUB, n_rows)
    return [slice(r, r + step) for r in range(0, n_rows, step)]


def _fold8(v):
    return v[0:8] + v[8:16]


def _pj(ref, s, rows=slice(None), cols=slice(0, D)):
    return ref[rows, s * D + cols.start:s * D + cols.stop]


def _in_proj(x, mod, g_mix, w_shard, order, shards, fulls, slicers):
    t_len = x.shape[0]
    tm = min(TMI, t_len)
    ni = t_len // tm
    na = len(shards)
    cw = 7 * D // NDEV
    rc = 32

    def body(ord_ref, x_ref, mod_ref, g_ref, wsh_ref, *rest):
        ins, (proj_ref, h_ref, wfull_ref), outs = rest[:na], rest[na:na + 3], rest[na + 3:2 * na + 3]
        h_scr, w_scr, wsend, wrecv, wlocal, wout = rest[2 * na + 3:2 * na + 9]
        start, forward, finish = _ag_phases(ins, outs, slicers, *rest[2 * na + 9:])
        p, i = pl.program_id(0), pl.program_id(1)
        x_, y_, c = _my_pos()
        me, sibling = (x_, y_, c), (x_, y_, 1 - c)
        chip_at = [None, (x_, 1 - y_), (1 - x_, y_), (1 - x_, 1 - y_)]

        def cols(px, py, pc):
            return w_scr.at[:, pl.ds(pl.multiple_of((4 * px + 2 * py + pc) * cw, 128), cw)]

        def wcopy(k, block, to, from_shard=False):
            dst = cols(*block)
            return pltpu.make_async_remote_copy(src_ref=wsh_ref if from_shard else dst, dst_ref=dst,
                                                send_sem=wsend.at[k], recv_sem=wrecv.at[k], device_id=to,
                                                device_id_type=MESH)

        own_local = pltpu.make_async_copy(wsh_ref, cols(*me), wlocal)
        to_hbm = pltpu.make_async_copy(w_scr, wfull_ref, wout)

        @pl.when((p == 0) & (i == 0))
        def _():
            own_local.start()
            wcopy(0, me, sibling, True).start()
            for q in (1, 2):
                wcopy(q, me, (*chip_at[q], c), True).start()
            own_local.wait()
            wcopy(0, sibling, me).wait_recv()

        @pl.when((p == 0) & (i == ni // 2))
        def _():
            wcopy(3, me, (*chip_at[3], c), True).start()

        for q in (1, 2, 3):
            @pl.when((p == q - 1) & (i == ni - 1))
            def _():
                wcopy(q, (*chip_at[q], c), me).wait_recv()
                wcopy(3 + q, (*chip_at[q], c), sibling).start()

            @pl.when((p == q) & (i == 0))
            def _():
                wcopy(3 + q, (*chip_at[q], 1 - c), me).wait_recv()

        @pl.when((p == 1) & (i == 0))
        def _():
            start()

        @pl.when((p == NCHIP - 1) & (i == ni // 2))
        def _():
            forward()

        @pl.when((p == NCHIP - 1) & (i == 0))
        def _():
            to_hbm.start()

        gs = g_ref[...] * (1.0 + mod_ref[1:2, :])
        sh = mod_ref[0:1, :]

        wcols = pl.ds(pl.multiple_of(ord_ref[p] * (2 * cw), 128), 2 * cw)
        for sub in _sub_blocks(tm):
            for r0 in range(sub.start, sub.stop, rc):
                xv = x_ref[r0:r0 + rc, :]
                r = lax.rsqrt(jnp.mean(xv * xv, axis=-1, keepdims=True) + EPS)
                h_scr[r0:r0 + rc, :] = (xv * r * gs + sh).astype(BF16)
            proj_ref[sub, :] = _dot(h_scr[sub, :], w_scr[:, wcols]).astype(BF16)

        @pl.when(p == 0)
        def _():
            h_ref[...] = h_scr[...]

        @pl.when((p == NCHIP - 1) & (i == ni - 1))
        def _():
            wcopy(0, me, sibling, True).wait_send()
            for q in (1, 2, 3):
                wcopy(q, me, (*chip_at[q], c), True).wait_send()
                wcopy(3 + q, (*chip_at[q], c), sibling).wait_send()
            finish()
            to_hbm.wait()

    res = pl.pallas_call(
        body, name="in_proj",
        grid_spec=pltpu.PrefetchScalarGridSpec(
            num_scalar_prefetch=1, grid=(NCHIP, ni),
            in_specs=[pl.BlockSpec((tm, D), lambda p, i, o: (i, 0)),
                      pl.BlockSpec((8, D), lambda p, i, o: (0, 0)),
                      pl.BlockSpec((1, D), lambda p, i, o: (0, 0))] + [_ANY] * (1 + na),
            out_specs=[pl.BlockSpec((tm, 2 * cw), lambda p, i, o: (i, o[p])),
                       pl.BlockSpec((tm, D), lambda p, i, o: (jnp.where(p == 0, i, ni - 1), 0))]
            + [_ANY] * (1 + na),
            scratch_shapes=[pltpu.VMEM((tm, D), BF16), pltpu.VMEM((D, 7 * D), BF16),
                            pltpu.SemaphoreType.DMA((7,)), pltpu.SemaphoreType.DMA((7,)),
                            pltpu.SemaphoreType.DMA, pltpu.SemaphoreType.DMA] + _ag_sems(na)),
        out_shape=[jax.ShapeDtypeStruct((t_len, 7 * D), BF16), jax.ShapeDtypeStruct((t_len, D), BF16),
                   jax.ShapeDtypeStruct((D, 7 * D), BF16)]
        + [jax.ShapeDtypeStruct(f, sh.dtype) for f, sh in zip(fulls, shards)],
        compiler_params=_cp("arbitrary", "arbitrary"),
    )(order, x, mod, g_mix, w_shard, *shards)
    return res[0], res[1], res[2], res[3:]


P_WA, P_WB, P_CBIAS, P_BA, P_BX, P_LAM = 0, 3, 7, 8, 9, 10
SV_PLANES = SV_U, SV_YA, SV_R, SV_I, SV_A, SV_MULT = range(6)


def _lru_gates(rp, ip, ls, first_row):
    r = _sig(rp)
    ig = _sig(ip)
    la = LRU_C * r * ls
    a = jnp.exp(la)
    m2 = _neg_expm1(2.0 * la)
    mult = jnp.where(first_row, 1.0, jnp.sqrt(jnp.maximum(m2, 0.0)))
    return r, ig, la, a, m2, mult


def _shift_down(cur, prev, s, row):
    return jnp.where(row >= s, pltpu.roll(cur, s, 0), pltpu.roll(prev, s, 0))


def _shift_up(cur, nxt, s, row):
    return jnp.where(row < 8 - s, pltpu.roll(cur, 8 - s, 0), pltpu.roll(nxt, 8 - s, 0))


def _conv_fwd_rows(tt, proj_ref, prm_ref, xe, ve, u_s, ub_s, ya_s):
    row = lax.broadcasted_iota(jnp.int32, (8, CG), 0)
    w_b = [prm_ref[P_WB + k:P_WB + k + 1, :] for k in range(4)]
    w_a = [prm_ref[P_WA + k:P_WA + k + 1, :] for k in range(3)]
    bias = prm_ref[P_CBIAS:P_CBIAS + 1, :]

    def blk(ib, carry):
        r0 = pl.multiple_of(ib * 16, 16)
        rows = pl.ds(r0, 16)
        for g in range(D // CG):
            cs = slice(g * CG, (g + 1) * CG)
            x16 = _pj(proj_ref, 3, rows, cs).astype(F32)
            v16 = _pj(proj_ref, 1, rows, cs).astype(F32) * _pj(proj_ref, 2, rows, cs).astype(F32)
            xp = xe[pl.ds(r0, 8), cs]
            vp = ve[pl.ds(r0, 8), cs]
            xe[pl.ds(r0 + 8, 16), cs] = x16
            ve[pl.ds(r0 + 8, 16), cs] = v16
            us, yas = [], []
            for sb in range(2):
                xc, vc = x16[8 * sb:8 * sb + 8], v16[8 * sb:8 * sb + 8]
                u8 = bias[:, cs] + w_b[3][:, cs] * xc
                for s in (1, 2, 3):
                    u8 = u8 + w_b[3 - s][:, cs] * _shift_down(xc, xp, s, row)
                y8 = w_a[2][:, cs] * vc
                for s in (1, 2):
                    y8 = y8 + w_a[2 - s][:, cs] * _shift_down(vc, vp, s, row)
                us.append(u8)
                yas.append(y8)
                xp, vp = xc, vc
            u16 = jnp.concatenate(us, axis=0)
            u_s[rows, cs] = u16
            ub_s[rows, cs] = u16.astype(BF16)
            ya_s[rows, cs] = jnp.concatenate(yas, axis=0)
        return carry

    lax.fori_loop(0, tt // 16, blk, 0)


def _mixer_fwd(proj, prm, wa, wx, shards, fulls, slicers):
    t_len = proj.shape[0]
    tt = min(TT, t_len)
    nt = t_len // tt
    na = len(shards)

    def body(proj_ref, prm_ref, wa_ref, wx_ref, *rest):
        ins, (mg_ref, hl_ref, sv_hbm), outs = rest[:na], rest[na:na + 3], rest[na + 3:2 * na + 3]
        xe, ve, hc, rp_s, ip_s, ub_s, sv_st, sv_sems = rest[2 * na + 3:2 * na + 11]
        start, forward, finish = _ag_phases(ins, outs, slicers, *rest[2 * na + 11:])
        t = pl.program_id(0)

        slot = t % 2
        sv_ref = sv_st.at[slot]

        def sv_out(tile, sl):
            rows = pl.ds(pl.multiple_of(tile * tt, tt), tt)
            return pltpu.make_async_copy(sv_st.at[sl], sv_hbm.at[:, rows, :], sv_sems.at[sl])

        @pl.when(t >= 2)
        def _():
            sv_out(t - 2, slot).wait()

        @pl.when(t == 0)
        def _():
            start()
            xe[0:8, :] = jnp.zeros((8, D), F32)
            ve[0:8, :] = jnp.zeros((8, D), F32)
            hc[...] = jnp.zeros((8, D), F32)

        @pl.when(t == (3 * nt) // 4)
        def _():
            forward()

        _conv_fwd_rows(tt, proj_ref, prm_ref, xe, ve, sv_ref.at[SV_U], ub_s, sv_ref.at[SV_YA])
        xe[0:8, :] = xe[tt:tt + 8, :]
        ve[0:8, :] = ve[tt:tt + 8, :]

        ub = ub_s[...]
        for h in range(HEADS):
            cs = slice(h * HB, (h + 1) * HB)
            rp_s[:, cs] = _dot(ub[:, cs], wa_ref[h]) + prm_ref[P_BA:P_BA + 1, cs]
            ip_s[:, cs] = _dot(ub[:, cs], wx_ref[h]) + prm_ref[P_BX:P_BX + 1, cs]

        ls_all = _log_sigmoid(prm_ref[P_LAM:P_LAM + 1, :])
        row = lax.broadcasted_iota(jnp.int32, (8, CG), 0)

        def blk(i, carry):
            r0 = pl.multiple_of(i * 16, 16)
            for g in range(D // CG):
                cs = slice(g * CG, (g + 1) * CG)
                ls = ls_all[:, cs]
                hprev = hc[:, cs]
                hs = []
                for sb in range(2):
                    rr = r0 + 8 * sb
                    first = (row + (t * tt + rr)) == 0
                    r8 = pl.ds(rr, 8)
                    r, ig, _, a, _, mult = _lru_gates(rp_s[r8, cs], ip_s[r8, cs], ls, first)
                    for plane, val in ((SV_R, r), (SV_I, ig), (SV_A, a), (SV_MULT, mult)):
                        sv_ref[plane, r8, cs] = val
                    b = mult * (ig * sv_ref[SV_U, r8, cs])
                    for s in (1, 2, 4):
                        a_sh = jnp.where(row >= s, pltpu.roll(a, s, 0), 1.0)
                        b_sh = jnp.where(row >= s, pltpu.roll(b, s, 0), 0.0)
                        b = a * b_sh + b
                        a = a * a_sh
                    hv = a * hprev + b
                    hprev = jnp.broadcast_to(hv[7:8, :], hv.shape)
                    hs.append(hv)
                hc[:, cs] = hprev
                h16 = jnp.concatenate(hs, axis=0)
                rows = pl.ds(r0, 16)
                gl, _ = _gelu(_pj(proj_ref, 4, rows, cs).astype(F32))
                y_b = h16 * gl
                y_a = _pj(proj_ref, 0, rows, cs).astype(F32) * sv_ref[SV_YA, rows, cs]
                mg = (_sig(_pj(proj_ref, 5, rows, cs).astype(F32)) * y_a
                      + _sig(_pj(proj_ref, 6, rows, cs).astype(F32)) * y_b)
                mg_ref[rows, cs] = mg.astype(BF16)
                hl_ref[rows, cs] = h16.astype(BF16)
            return carry

        lax.fori_loop(0, tt // 16, blk, 0)
        sv_out(t, slot).start()

        @pl.when(t == nt - 1)
        def _():
            finish()
            if nt >= 2:
                sv_out(t - 1, 1 - slot).wait()
            sv_out(t, slot).wait()

    res = pl.pallas_call(
        body, name="mixer_fwd", grid=(nt,),
        in_specs=[pl.BlockSpec((tt, 7 * D), lambda t: (t, 0)),
                  pl.BlockSpec((16, D), lambda t: (0, 0)),
                  pl.BlockSpec((HEADS, HB, HB), lambda t: (0, 0, 0)),
                  pl.BlockSpec((HEADS, HB, HB), lambda t: (0, 0, 0))] + [_ANY] * na,
        out_specs=[pl.BlockSpec((tt, D), lambda t: (t, 0)), pl.BlockSpec((tt, D), lambda t: (t, 0)), _ANY]
        + [_ANY] * na,
        out_shape=[jax.ShapeDtypeStruct((t_len, D), BF16), jax.ShapeDtypeStruct((t_len, D), BF16),
                   jax.ShapeDtypeStruct((len(SV_PLANES), t_len, D), F32)]
        + [jax.ShapeDtypeStruct(f, sh.dtype) for f, sh in zip(fulls, shards)],
        scratch_shapes=[pltpu.VMEM((tt + 8, D), F32), pltpu.VMEM((tt + 8, D), F32), pltpu.VMEM((8, D), F32),
                        pltpu.VMEM((tt, D), F32), pltpu.VMEM((tt, D), F32), pltpu.VMEM((tt, D), BF16),
                        pltpu.VMEM((2, len(SV_PLANES), tt, D), F32), pltpu.SemaphoreType.DMA((2,))]
        + _ag_sems(na),
        compiler_params=_cp("arbitrary"),
    )(proj, prm, wa, wx, *shards)
    return res[0], res[1], res[2], res[3:]


def _out_proj(merged, x, mod, g_ffn, w_out):
    t_len = x.shape[0]
    tm = min(TM, t_len)

    def body(mg_ref, x_ref, mod_ref, g_ref, w_ref, x1_ref, h2_ref):
        gt1 = mod_ref[2:3, :]
        gs = g_ref[...] * (1.0 + mod_ref[4:5, :])
        sh = mod_ref[3:4, :]
        for sub in _sub_blocks(tm):
            x1_ref[sub, :] = x_ref[sub, :] + gt1 * _dot(mg_ref[sub, :], w_ref[...])
            for r0 in range(sub.start, sub.stop, 16):
                x1 = x1_ref[r0:r0 + 16, :]
                r = lax.rsqrt(jnp.mean(x1 * x1, axis=-1, keepdims=True) + EPS)
                h2_ref[r0:r0 + 16, :] = (x1 * r * gs + sh).astype(BF16)

    return pl.pallas_call(
        body, name="out_proj", grid=(t_len // tm,),
        in_specs=[pl.BlockSpec((tm, D), lambda i: (i, 0)), pl.BlockSpec((tm, D), lambda i: (i, 0)),
                  pl.BlockSpec((8, D), lambda i: (0, 0)), pl.BlockSpec((1, D), lambda i: (0, 0)),
                  pl.BlockSpec((D, D), lambda i: (0, 0))],
        out_specs=[pl.BlockSpec((tm, D), lambda i: (i, 0)), pl.BlockSpec((tm, D), lambda i: (i, 0))],
        out_shape=[jax.ShapeDtypeStruct((t_len, D), F32), jax.ShapeDtypeStruct((t_len, D), BF16)],
        compiler_params=_cp("parallel"),
    )(merged, x, mod, g_ffn, w_out)


def _ffn_fwd(h2, x1, target, mod, g_fin, w_gu, w_down):
    t_len = x1.shape[0]
    tm = min(TMF, t_len)

    def body(h2_ref, x1_ref, tg_ref, mod_ref, g_ref, wgu_ref, wd_ref, gu_ref, dx2_ref, dx2b_ref, loss_ref, dg_ref, acc):
        @pl.when(pl.program_id(0) == 0)
        def _():
            loss_ref[...] = jnp.zeros_like(loss_ref)
            dg_ref[...] = jnp.zeros_like(dg_ref)

        hb = h2_ref[...]
        ffn = None
        nxt = (_dot_nt(hb, wgu_ref[0, 0]), _dot_nt(hb, wgu_ref[1, 0]))
        for j in range(4):
            gate, up = nxt
            if j < 3:
                nxt = (_dot_nt(hb, wgu_ref[0, j + 1]), _dot_nt(hb, wgu_ref[1, j + 1]))
            gu_ref[0, j] = gate.astype(BF16)
            gu_ref[1, j] = up.astype(BF16)
            act = (gate * _sig(gate) * up).astype(BF16)
            part = _dot(act, wd_ref[j * FB:(j + 1) * FB, :])
            ffn = part if ffn is None else ffn + part
        acc[...] = ffn

        gt2 = mod_ref[5:6, :]
        gf = g_ref[...]

        s_loss = s_dg = jnp.zeros((8, D), F32)
        for r0 in range(0, tm, 16):
            rows = slice(r0, r0 + 16)
            x2 = x1_ref[rows, :] + gt2 * acc[rows, :]
            r = lax.rsqrt(jnp.mean(x2 * x2, axis=-1, keepdims=True) + EPS)
            xn = x2 * r
            diff = xn * gf - tg_ref[rows, :]
            dy = diff * (1.0 / D)
            dxn = dy * gf
            dx2 = r * (dxn - xn * jnp.mean(dxn * xn, axis=-1, keepdims=True))
            dx2_ref[rows, :] = dx2
            dx2b_ref[rows, :] = dx2.astype(BF16)
            s_loss, s_dg = s_loss + _fold8(diff * diff), s_dg + _fold8(dy * xn)
        loss_ref[...] += jnp.sum(s_loss) * (0.5 / D)
        dg_ref[...] += jnp.sum(s_dg, axis=0, keepdims=True)

    row = pl.BlockSpec((tm, D), lambda i: (i, 0))
    return pl.pallas_call(
        body, name="ffn_fwd", grid=(t_len // tm,),
        in_specs=[row, row, row, pl.BlockSpec((8, D), lambda i: (0, 0)), pl.BlockSpec((1, D), lambda i: (0, 0)),
                  _resident((2, 4, FB, D)), _resident((DFF, D))],
        out_specs=[pl.BlockSpec((2, 4, tm, FB), lambda i: (0, 0, i, 0)), row, row,
                   pl.BlockSpec((1, 128), lambda i: (0, 0)), pl.BlockSpec((1, D), lambda i: (0, 0))],
        out_shape=[jax.ShapeDtypeStruct((2, 4, t_len, FB), BF16), jax.ShapeDtypeStruct((t_len, D), F32),
                   jax.ShapeDtypeStruct((t_len, D), BF16),
                   jax.ShapeDtypeStruct((1, 128), F32), jax.ShapeDtypeStruct((1, D), F32)],
        scratch_shapes=[pltpu.VMEM((tm, D), F32)],
        compiler_params=_cp("arbitrary"),
    )(h2, x1, target, mod, g_fin, w_gu, w_down)


S_SH, S_SC, S_G = 0, 1, 2


def _norm_bwd_rows(span, sums, dh_ref, x_ref, dres_ref, scale, gain, write):
    gs = 1.0 + scale
    s_sh, s_sc, s_g = sums
    for r0 in range(span.start, span.stop, 16):
        rows = slice(r0, r0 + 16)
        dh = dh_ref[rows, :]
        xv = x_ref[rows, :]
        r = lax.rsqrt(jnp.mean(xv * xv, axis=-1, keepdims=True) + EPS)
        xn = xv * r
        dhn = dh * gs
        dxn = dhn * gain
        write(rows, dres_ref[rows, :] + r * (dxn - xn * jnp.mean(dxn * xn, axis=-1, keepdims=True)))
        s_sh, s_sc, s_g = s_sh + _fold8(dh), s_sc + _fold8(dh * (xn * gain)), s_g + _fold8(dhn * xn)
    return s_sh, s_sc, s_g


def _add_norm_sums(sums_ref, sums):
    for dst, s in zip((S_SH, S_SC, S_G), sums):
        sums_ref[dst:dst + 1, :] += jnp.sum(s, axis=0, keepdims=True)


def _ffn_bwd(dx2, gu, x1, mod, g_ffn, w_gu, w_down, w_out):
    t_len = x1.shape[0]
    tm = min(TMF, t_len)

    def body(dx2_ref, gu_ref, x1_ref, mod_ref, g_ref, wgu_ref, wd_ref, wo_ref,
             dgu_ref, act_ref, dx1_ref, dx1b_ref, dmg_ref, sums_ref, acc, dmo, dact_s):
        @pl.when(pl.program_id(0) == 0)
        def _():
            sums_ref[...] = jnp.zeros_like(sums_ref)

        dffn = (dx2_ref[...] * mod_ref[5:6, :]).astype(BF16)
        dact_s[0] = _dot_nt(dffn, wd_ref[0:FB, :])
        for j in range(4):
            if j < 3:
                dact_s[(j + 1) % 2] = _dot_nt(dffn, wd_ref[(j + 1) * FB:(j + 2) * FB, :])
            for r0 in range(0, tm, 16):
                rows = slice(r0, r0 + 16)
                dact = dact_s[j % 2, rows, :]
                gate = gu_ref[0, j, rows, :].astype(F32)
                up = gu_ref[1, j, rows, :].astype(F32)
                sg = _sig(gate)
                silu = gate * sg
                act_ref[j, rows, :] = (silu * up).astype(BF16)
                dgu_ref[0, j, rows, :] = (dact * up * (sg * (1.0 + gate * (1.0 - sg)))).astype(BF16)
                dgu_ref[1, j, rows, :] = (dact * silu).astype(BF16)
            part = _dot(dgu_ref[0, j], wgu_ref[0, j]) + _dot(dgu_ref[1, j], wgu_ref[1, j])
            if j == 0:
                acc[...] = part
            else:
                acc[...] += part

        gt1 = mod_ref[2:3, :]

        def write(rows, dx1):
            dx1_ref[rows, :] = dx1
            dx1b_ref[rows, :] = dx1.astype(BF16)
            dmo[rows, :] = (dx1 * gt1).astype(BF16)

        zero = jnp.zeros((8, D), F32)
        sums = (zero, zero, zero)
        for sub in (slice(0, tm // 2), slice(tm // 2, tm)):
            sums = _norm_bwd_rows(sub, sums, acc, x1_ref, dx2_ref, mod_ref[4:5, :], g_ref[...], write)
            dmg_ref[sub, :] = _dot_nt(dmo[sub, :], wo_ref[...]).astype(BF16)
        _add_norm_sums(sums_ref, sums)

    row = pl.BlockSpec((tm, D), lambda i: (i, 0))
    return pl.pallas_call(
        body, name="ffn_bwd", grid=(t_len // tm,),
        in_specs=[row, pl.BlockSpec((2, 4, tm, FB), lambda i: (0, 0, i, 0)), row,
                  pl.BlockSpec((8, D), lambda i: (0, 0)), pl.BlockSpec((1, D), lambda i: (0, 0)),
                  _resident((2, 4, FB, D)), _resident((DFF, D)), _resident((D, D))],
        out_specs=[pl.BlockSpec((2, 4, tm, FB), lambda i: (0, 0, i, 0)),
                   pl.BlockSpec((4, tm, FB), lambda i: (0, i, 0)), row, row, row,
                   pl.BlockSpec((8, D), lambda i: (0, 0))],
        out_shape=[jax.ShapeDtypeStruct((2, 4, t_len, FB), BF16), jax.ShapeDtypeStruct((4, t_len, FB), BF16),
                   jax.ShapeDtypeStruct((t_len, D), F32), jax.ShapeDtypeStruct((t_len, D), BF16),
                   jax.ShapeDtypeStruct((t_len, D), BF16), jax.ShapeDtypeStruct((8, D), F32)],
        scratch_shapes=[pltpu.VMEM((tm, D), F32), pltpu.VMEM((tm, D), BF16), pltpu.VMEM((2, tm, FB), F32)],
        compiler_params=_cp("arbitrary"),
    )(dx2, gu, x1, mod, g_ffn, w_gu, w_down, w_out)


def _my_pos():
    return lax.axis_index("x"), lax.axis_index("y"), lax.axis_index("c")


def _my_index():
    x, y, c = _my_pos()
    return 4 * x + 2 * y + c


def _device_of(b):
    return (b >> 2) & 1, (b >> 1) & 1, b & 1


def _rs_send(src, parts_ref, b, send_sems, recv_sems, local_sem):
    me = _my_index()
    dst = parts_ref.at[me]

    @pl.when(b == me)
    def _():
        pltpu.make_async_copy(src, dst, local_sem).start()

    @pl.when(b != me)
    def _():
        pltpu.make_async_remote_copy(src_ref=src, dst_ref=dst, send_sem=send_sems.at[b], recv_sem=recv_sems.at[me],
                                     device_id=_device_of(b), device_id_type=MESH).start()


def _rs_finish(src_of, parts_ref, send_sems, recv_sems, local_sem):
    me = _my_index()
    for s in range(NDEV):
        @pl.when(s != me)
        def _():
            cp = pltpu.make_async_remote_copy(src_ref=src_of(s), dst_ref=parts_ref.at[s], send_sem=send_sems.at[s],
                                              recv_sem=recv_sems.at[s], device_id=_device_of(s), device_id_type=MESH)
            cp.wait_send()
            cp.wait_recv()

        @pl.when(s == me)
        def _():
            pltpu.make_async_copy(src_of(s), parts_ref.at[s], local_sem).wait()


_RS_SEMS = [pltpu.SemaphoreType.DMA((NDEV,)), pltpu.SemaphoreType.DMA((NDEV,)), pltpu.SemaphoreType.DMA]
_ANY = pl.BlockSpec(memory_space=pl.ANY)


def _xor_order(me, n):
    return (me ^ (n - 1 - jnp.arange(n, dtype=jnp.int32))).astype(jnp.int32)


NCHIP = NDEV // 2


def _rs2_scratch(half_shape):
    blocks = lambda *lead: pltpu.VMEM(lead + tuple(half_shape), BF16)
    return [blocks(NCHIP, 2), blocks(NCHIP)] + [pltpu.SemaphoreType.DMA((NCHIP,))] * 4 + [pltpu.SemaphoreType.DMA]


def _rs2_to_sibling(q, rs):
    stage, from_sib, d_send, d_recv = rs[:4]
    x, y, c = _my_pos()
    pltpu.make_async_remote_copy(src_ref=stage.at[q, 1 - c], dst_ref=from_sib.at[q], send_sem=d_send.at[q],
                                 recv_sem=d_recv.at[q], device_id=(x, y, 1 - c), device_id_type=MESH).start()


def _rs2_forward(q, parts_ref, rs):
    stage, chip_sum, d_send, d_recv, i_send, i_recv, local_sem = rs
    x, y, c = _my_pos()
    my_chip = 2 * x + y
    pltpu.make_async_remote_copy(src_ref=stage.at[q, c], dst_ref=chip_sum.at[q], send_sem=d_send.at[q],
                                 recv_sem=d_recv.at[q], device_id=(x, y, 1 - c), device_id_type=MESH).wait_recv()
    chip_sum[q] = (stage[q, c].astype(F32) + chip_sum[q].astype(F32)).astype(BF16)

    @pl.when(q == my_chip)
    def _():
        pltpu.make_async_copy(chip_sum.at[q], parts_ref.at[my_chip], local_sem).start()

    @pl.when(q != my_chip)
    def _():
        pltpu.make_async_remote_copy(src_ref=chip_sum.at[q], dst_ref=parts_ref.at[my_chip], send_sem=i_send.at[q],
                                     recv_sem=i_recv.at[my_chip], device_id=((q >> 1) & 1, q & 1, c),
                                     device_id_type=MESH).start()


def _rs2_finish(parts_ref, rs):
    stage, chip_sum, d_send, d_recv, i_send, i_recv, local_sem = rs
    x, y, c = _my_pos()
    my_chip = 2 * x + y
    for q in range(NCHIP):
        pltpu.make_async_remote_copy(src_ref=stage.at[q, 1 - c], dst_ref=chip_sum.at[q], send_sem=d_send.at[q],
                                     recv_sem=d_recv.at[q], device_id=(x, y, 1 - c), device_id_type=MESH).wait_send()

        @pl.when(q != my_chip)
        def _():
            cp = pltpu.make_async_remote_copy(src_ref=chip_sum.at[q], dst_ref=parts_ref.at[q], send_sem=i_send.at[q],
                                              recv_sem=i_recv.at[q], device_id=((q >> 1) & 1, q & 1, c),
                                              device_id_type=MESH)
            cp.wait_send()
            cp.wait_recv()

        @pl.when(q == my_chip)
        def _():
            pltpu.make_async_copy(chip_sum.at[q], parts_ref.at[q], local_sem).wait()


def _gu_wgrad(h2, dgu, order):
    t_len = h2.shape[0]
    tk = min(TK, t_len)
    nk = t_len // tk

    def body(ord_ref, h_ref, d_ref, parts_ref, acc, *rs):
        p, k = pl.program_id(0), pl.program_id(1)

        @pl.when(k == 0)
        def _():
            acc[...] = jnp.zeros_like(acc)

        hb = h_ref[...]
        for half in range(2):
            acc[half] += _dot_tn(d_ref[0, half], hb)

        @pl.when(k == nk - 1)
        def _():
            q = ord_ref[p]
            rs[0][q] = acc[...].astype(BF16)
            _rs2_to_sibling(q, rs)

        @pl.when((k == nk - 1) & (p > 0))
        def _():
            _rs2_forward(ord_ref[p - 1], parts_ref, rs)

        @pl.when((p == NCHIP - 1) & (k == nk - 1))
        def _():
            _rs2_forward(ord_ref[p], parts_ref, rs)
            _rs2_finish(parts_ref, rs)

    return pl.pallas_call(
        body, name="gu_wgrad",
        grid_spec=pltpu.PrefetchScalarGridSpec(
            num_scalar_prefetch=1, grid=(NCHIP, nk),
            in_specs=[pl.BlockSpec((tk, D), lambda p, k, o: (k, 0)),
                      pl.BlockSpec((1, 2, tk, FB), lambda p, k, o: (o[p], 0, k, 0))],
            out_specs=_ANY,
            scratch_shapes=[pltpu.VMEM((2, FB, D), F32)] + _rs2_scratch((FB, D))),
        out_shape=jax.ShapeDtypeStruct((NCHIP, FB, D), BF16),
        compiler_params=_cp("arbitrary", "arbitrary"),
    )(order, h2, dgu.reshape(NCHIP, 2, t_len, FB))


def _scaled_wgrad(name, a, dx, w, gate_row, mod, order):
    nb, t_len, kb = a.shape
    tk = min(TK, t_len)
    nk = t_len // tk
    cpb = NCHIP // nb
    rows = kb // (2 * cpb)

    def body(ord_ref, a_ref, dx_ref, w_ref, mod_ref, parts_ref, dg_ref, acc, *rs):
        p, k = pl.program_id(0), pl.program_id(1)
        j = ord_ref[p]

        @pl.when((p == 0) & (k == 0))
        def _():
            dg_ref[...] = jnp.zeros_like(dg_ref)

        @pl.when(k == 0)
        def _():
            acc[...] = jnp.zeros_like(acc)

        acc[...] += _dot_tn(a_ref[0], dx_ref[...])

        @pl.when(k == nk - 1)
        def _():
            z = acc[...]
            zg = (z * mod_ref[gate_row:gate_row + 1, :]).astype(BF16)
            dg_ref[0:1, :] += jnp.sum(z * w_ref[...].astype(F32), axis=0, keepdims=True)
            for i in range(cpb):
                q = j * cpb + i
                for half in range(2):
                    rs[0][q, half] = zg[(2 * i + half) * rows:(2 * i + half + 1) * rows]
                _rs2_to_sibling(q, rs)

        if cpb == 1:
            @pl.when((k == nk - 1) & (p > 0))
            def _():
                _rs2_forward(ord_ref[p - 1], parts_ref, rs)

        @pl.when((p == nb - 1) & (k == nk - 1))
        def _():
            for i in range(cpb):
                _rs2_forward(j * cpb + i, parts_ref, rs)
            _rs2_finish(parts_ref, rs)

    return pl.pallas_call(
        body, name=name,
        grid_spec=pltpu.PrefetchScalarGridSpec(
            num_scalar_prefetch=1, grid=(nb, nk),
            in_specs=[pl.BlockSpec((1, tk, kb), lambda p, k, o: (o[p], k, 0)),
                      pl.BlockSpec((tk, D), lambda p, k, o: (k, 0)),
                      pl.BlockSpec((kb, D), lambda p, k, o: (o[p], 0)),
                      pl.BlockSpec((8, D), lambda p, k, o: (0, 0))],
            out_specs=[_ANY, pl.BlockSpec((8, D), lambda p, k, o: (0, 0))],
            scratch_shapes=[pltpu.VMEM((kb, D), F32)] + _rs2_scratch((rows, D))),
        out_shape=[jax.ShapeDtypeStruct((NCHIP, rows, D), BF16), jax.ShapeDtypeStruct((8, D), F32)],
        compiler_params=_cp("arbitrary", "arbitrary"),
    )(order, a, dx, w, mod)


M_WA, M_WB, M_CBIAS, M_BA, M_BX, M_LS = 0, 3, 7, 8, 9, 10


def _conv_bwd_rows(tt, proj_ref, prm_ref, xe, ve, due, dye, dp_ref, acc8):
    row = lax.broadcasted_iota(jnp.int32, (8, CG), 0)
    w_b = [prm_ref[P_WB + k:P_WB + k + 1, :] for k in range(4)]
    w_a = [prm_ref[P_WA + k:P_WA + k + 1, :] for k in range(3)]

    def blk(ib, carry):
        r0 = pl.multiple_of(ib * 16, 16)
        rows = pl.ds(r0, 16)
        for g in range(D // CG):
            cs = slice(g * CG, (g + 1) * CG)
            du16, du_after = due[rows, cs], due[pl.ds(r0 + 16, 8), cs]
            dy16, dy_after = dye[rows, cs], dye[pl.ds(r0 + 16, 8), cs]
            cc16 = _pj(proj_ref, 1, rows, cs).astype(F32)
            cx16 = _pj(proj_ref, 2, rows, cs).astype(F32)
            x16 = _pj(proj_ref, 3, rows, cs).astype(F32)
            v16 = cc16 * cx16
            xp, vp = xe[pl.ds(r0, 8), cs], ve[pl.ds(r0, 8), cs]
            xe[pl.ds(r0 + 16, 8), cs] = x16[8:16]
            ve[pl.ds(r0 + 16, 8), cs] = v16[8:16]
            acc = [acc8[8 * k:8 * k + 8, cs] for k in range(8)]
            drx, dv = [], []
            for sb in range(2):
                lo = slice(8 * sb, 8 * sb + 8)
                duc, dyc, xc, vc = du16[lo], dy16[lo], x16[lo], v16[lo]
                du_n = du16[8:16] if sb == 0 else du_after
                dy_n = dy16[8:16] if sb == 0 else dy_after
                acc[0] = acc[0] + duc
                acc[4] = acc[4] + duc * xc
                d8 = w_b[3][:, cs] * duc
                for s in (1, 2, 3):
                    acc[4 - s] = acc[4 - s] + duc * _shift_down(xc, xp, s, row)
                    d8 = d8 + w_b[3 - s][:, cs] * _shift_up(duc, du_n, s, row)
                acc[7] = acc[7] + dyc * vc
                e8 = w_a[2][:, cs] * dyc
                for s in (1, 2):
                    acc[7 - s] = acc[7 - s] + dyc * _shift_down(vc, vp, s, row)
                    e8 = e8 + w_a[2 - s][:, cs] * _shift_up(dyc, dy_n, s, row)
                drx.append(d8)
                dv.append(e8)
                xp, vp = xc, vc
            for k in range(8):
                acc8[8 * k:8 * k + 8, cs] = acc[k]
            dv16 = jnp.concatenate(dv, axis=0)
            col = lambda s: slice(s * D + g * CG, s * D + (g + 1) * CG)
            dp_ref[rows, col(3)] = jnp.concatenate(drx, axis=0).astype(BF16)
            dp_ref[rows, col(1)] = (dv16 * cx16).astype(BF16)
            dp_ref[rows, col(2)] = (dv16 * cc16).astype(BF16)
        return carry

    lax.fori_loop(0, tt // 16, blk, 0)


def _mixer_bwd(proj, hl, sv, dmg, prm, wa, wx):
    t_len = proj.shape[0]
    tt = min(TT, t_len)
    nt = t_len // tt
    hb8 = tt // 8

    def rev(i):
        return nt - 1 - i

    def halo(i):
        return jnp.maximum(rev(i) * hb8 - 1, 0)

    def body(proj_ref, ph_ref, hl_ref, hh_ref, sv_ref, dmg_ref, prm_ref, wa_ref, wx_ref,
             dp_ref, sums_ref, gwa_ref, gwx_ref,
             xe, ve, he, due, dye, drp_s, dip_s, an, gn, acc8):
        i = pl.program_id(0)
        t = rev(i)

        @pl.when(i == 0)
        def _():
            sums_ref[...] = jnp.zeros_like(sums_ref)
            gwa_ref[...] = jnp.zeros_like(gwa_ref)
            gwx_ref[...] = jnp.zeros_like(gwx_ref)
            due[tt:tt + 8, :] = jnp.zeros((8, D), F32)
            dye[tt:tt + 8, :] = jnp.zeros((8, D), F32)
            an[...] = jnp.zeros((8, D), F32)
            gn[...] = jnp.zeros((8, D), F32)

        live = (t > 0).astype(F32)
        xe[0:8, :] = _pj(ph_ref, 3).astype(F32) * live
        ve[0:8, :] = _pj(ph_ref, 1).astype(F32) * _pj(ph_ref, 2).astype(F32) * live
        he[0:8, :] = hh_ref[...].astype(F32) * live
        he[8:8 + tt, :] = hl_ref[...].astype(F32)

        ls_all = _log_sigmoid(prm_ref[P_LAM:P_LAM + 1, :])
        row = lax.broadcasted_iota(jnp.int32, (8, CG), 0)
        nblk = tt // 16

        def blk(ib, carry):
            r0 = pl.multiple_of((nblk - 1 - ib) * 16, 16)
            rows = pl.ds(r0, 16)
            for g in range(D // CG):
                cs = slice(g * CG, (g + 1) * CG)
                ls = ls_all[:, cs]
                dm = dmg_ref[rows, cs].astype(F32)
                cb = _pj(proj_ref, 0, rows, cs).astype(F32)
                rg = _pj(proj_ref, 4, rows, cs).astype(F32)
                sga = _sig(_pj(proj_ref, 5, rows, cs).astype(F32))
                sgb = _sig(_pj(proj_ref, 6, rows, cs).astype(F32))
                ya0 = sv_ref[SV_YA, rows, cs]
                h16 = he[pl.ds(r0 + 8, 16), cs]
                gl, th = _gelu(rg)
                dgl = 0.5 * (1.0 + th) + 0.5 * rg * (1.0 - th * th) * (_GC * (1.0 + 3.0 * 0.044715 * rg * rg))
                y_a = cb * ya0
                y_b = h16 * gl
                dy_a = dm * sga
                dy_b = dm * sgb
                col = lambda s: slice(s * D + g * CG, s * D + (g + 1) * CG)
                dp_ref[rows, col(5)] = (dm * y_a * sga * (1.0 - sga)).astype(BF16)
                dp_ref[rows, col(6)] = (dm * y_b * sgb * (1.0 - sgb)).astype(BF16)
                dp_ref[rows, col(4)] = (dy_b * h16 * dgl).astype(BF16)
                dp_ref[rows, col(0)] = (dy_a * ya0).astype(BF16)
                dye[rows, cs] = dy_a * cb
                dh16 = dy_b * gl

                a_next = an[:, cs]
                g_next = gn[:, cs]
                s_ba = jnp.zeros((8, CG), F32)
                s_bx = jnp.zeros((8, CG), F32)
                s_ls = jnp.zeros((8, CG), F32)
                for sb in (1, 0):
                    rr = r0 + 8 * sb
                    first = (row + (t * tt + rr)) == 0
                    r8 = pl.ds(rr, 8)
                    uu, r, ig, a, mult = (sv_ref[pln, r8, cs] for pln in (SV_U, SV_R, SV_I, SV_A, SV_MULT))
                    ca = jnp.where(row < 7, pltpu.roll(a, 7, 0), a_next)
                    cb_ = dh16[8 * sb:8 * sb + 8, :]
                    for s in (1, 2, 4):
                        a_sh = jnp.where(row < 8 - s, pltpu.roll(ca, 8 - s, 0), 1.0)
                        b_sh = jnp.where(row < 8 - s, pltpu.roll(cb_, 8 - s, 0), 0.0)
                        cb_ = ca * b_sh + cb_
                        ca = ca * a_sh
                    gv = ca * g_next + cb_
                    g_next = jnp.broadcast_to(gv[0:1, :], gv.shape)
                    a_next = jnp.broadcast_to(a[0:1, :], a.shape)
                    hprev = jnp.where(row >= 1, pltpu.roll(he[pl.ds(rr + 8, 8), cs], 1, 0),
                                      pltpu.roll(he[pl.ds(rr, 8), cs], 1, 0))
                    da = gv * hprev
                    dmult = jnp.where(first, 0.0, gv * ig * uu)
                    dla = da * a + jnp.where(mult > 0.0, dmult * (-(a * a) / mult), 0.0)
                    drp = dla * (LRU_C * ls) * r * (1.0 - r)
                    dip = gv * mult * uu * ig * (1.0 - ig)
                    s_ls = s_ls + dla * (LRU_C * r)
                    s_ba = s_ba + drp
                    s_bx = s_bx + dip
                    drp_s[pl.ds(rr, 8), cs] = drp
                    dip_s[pl.ds(rr, 8), cs] = dip
                    due[pl.ds(rr, 8), cs] = gv * mult * ig
                an[:, cs] = a_next
                gn[:, cs] = g_next
                sums_ref[M_BA:M_BA + 1, cs] += jnp.sum(s_ba, axis=0, keepdims=True)
                sums_ref[M_BX:M_BX + 1, cs] += jnp.sum(s_bx, axis=0, keepdims=True)
                sums_ref[M_LS:M_LS + 1, cs] += jnp.sum(s_ls, axis=0, keepdims=True)
            return carry

        lax.fori_loop(0, nblk, blk, 0)

        drp_b = drp_s[...].astype(BF16)
        dip_b = dip_s[...].astype(BF16)
        ub = sv_ref[SV_U].astype(BF16)
        for h in range(HEADS):
            cs = slice(h * HB, (h + 1) * HB)
            due[0:tt, cs] += _dot_nt(drp_b[:, cs], wa_ref[h]) + _dot_nt(dip_b[:, cs], wx_ref[h])
            gwa_ref[h] += _dot_tn(ub[:, cs], drp_b[:, cs])
            gwx_ref[h] += _dot_tn(ub[:, cs], dip_b[:, cs])

        acc8[...] = jnp.zeros_like(acc8)
        _conv_bwd_rows(tt, proj_ref, prm_ref, xe, ve, due, dye, dp_ref, acc8)
        for k, dst in enumerate([M_CBIAS] + [M_WB + k for k in range(4)] + [M_WA + k for k in range(3)]):
            sums_ref[dst:dst + 1, :] += jnp.sum(acc8[8 * k:8 * k + 8, :], axis=0, keepdims=True)
        due[tt:tt + 8, :] = due[0:8, :]
        dye[tt:tt + 8, :] = dye[0:8, :]

        @pl.when(i == nt - 1)
        def _():
            sums_ref[M_LS:M_LS + 1, :] = sums_ref[M_LS:M_LS + 1, :] * _sig(-prm_ref[P_LAM:P_LAM + 1, :])

    big = lambda: pltpu.VMEM((tt + 8, D), F32)
    tile = lambda: pltpu.VMEM((tt, D), F32)
    return pl.pallas_call(
        body, name="mixer_bwd", grid=(nt,),
        in_specs=[pl.BlockSpec((tt, 7 * D), lambda i: (rev(i), 0)),
                  pl.BlockSpec((8, 7 * D), lambda i: (halo(i), 0)),
                  pl.BlockSpec((tt, D), lambda i: (rev(i), 0)),
                  pl.BlockSpec((8, D), lambda i: (halo(i), 0)),
                  pl.BlockSpec((len(SV_PLANES), tt, D), lambda i: (0, rev(i), 0)),
                  pl.BlockSpec((tt, D), lambda i: (rev(i), 0)),
                  pl.BlockSpec((16, D), lambda i: (0, 0)),
                  pl.BlockSpec((HEADS, HB, HB), lambda i: (0, 0, 0)),
                  pl.BlockSpec((HEADS, HB, HB), lambda i: (0, 0, 0))],
        out_specs=[pl.BlockSpec((tt, 7 * D), lambda i: (rev(i), 0)),
                   pl.BlockSpec((16, D), lambda i: (0, 0)),
                   pl.BlockSpec((HEADS, HB, HB), lambda i: (0, 0, 0)),
                   pl.BlockSpec((HEADS, HB, HB), lambda i: (0, 0, 0))],
        out_shape=[jax.ShapeDtypeStruct((t_len, 7 * D), BF16), jax.ShapeDtypeStruct((16, D), F32),
                   jax.ShapeDtypeStruct((HEADS, HB, HB), F32), jax.ShapeDtypeStruct((HEADS, HB, HB), F32)],
        scratch_shapes=[big(), big(), big(), big(), big(), tile(), tile(),
                        pltpu.VMEM((8, D), F32), pltpu.VMEM((8, D), F32), pltpu.VMEM((64, D), F32)],
        compiler_params=_cp("arbitrary"),
    )(proj, proj, hl, hl, sv, dmg, prm, wa, wx)


def _in_proj_bwd(dproj, w_in, x, dx1, mod, g_mix):
    t_len = x.shape[0]
    tm = min(TM, t_len)

    def body(dp_ref, w_ref, x_ref, dx1_ref, mod_ref, g_ref, gx_ref, sums_ref, acc):
        @pl.when(pl.program_id(0) == 0)
        def _():
            sums_ref[...] = jnp.zeros_like(sums_ref)

        def write(rows, dx):
            gx_ref[rows, :] = dx

        zero = jnp.zeros((8, D), F32)
        sums = (zero, zero, zero)
        for sub in _sub_blocks(tm):
            acc[sub, :] = _dot_nt(dp_ref[sub, :], w_ref[...])
            sums = _norm_bwd_rows(sub, sums, acc, x_ref, dx1_ref, mod_ref[1:2, :], g_ref[...], write)
        _add_norm_sums(sums_ref, sums)

    return pl.pallas_call(
        body, name="in_proj_bwd", grid=(t_len // tm,),
        in_specs=[pl.BlockSpec((tm, 7 * D), lambda i: (i, 0)),
                  _resident((D, 7 * D)),
                  pl.BlockSpec((tm, D), lambda i: (i, 0)), pl.BlockSpec((tm, D), lambda i: (i, 0)),
                  pl.BlockSpec((8, D), lambda i: (0, 0)), pl.BlockSpec((1, D), lambda i: (0, 0))],
        out_specs=[pl.BlockSpec((tm, D), lambda i: (i, 0)), pl.BlockSpec((8, D), lambda i: (0, 0))],
        out_shape=[jax.ShapeDtypeStruct((t_len, D), F32), jax.ShapeDtypeStruct((8, D), F32)],
        scratch_shapes=[pltpu.VMEM((tm, D), F32)],
        compiler_params=_cp("arbitrary"),
    )(dproj, w_in, x, dx1, mod, g_mix)


def _in_wgrad(h, dproj, g_wa, g_wx, order):
    t_len = h.shape[0]
    tk = min(TKI, t_len)
    nk = t_len // tk
    cw = 7 * D // NDEV
    hr = HB // NDEV

    def body(ord_ref, h_ref, d_ref, ga_ref, gx_ref, parts_ref, pa_ref, px_ref, acc, *scr):
        rs, sems = scr[:-6], scr[-6:]
        p, k = pl.program_id(0), pl.program_id(1)

        def head_rows(ref):
            return lambda s: ref.at[:, pl.ds(s * hr, hr), :]

        @pl.when((p == 0) & (k == 0))
        def _():
            for s in range(NDEV):
                _rs_send(head_rows(ga_ref)(s), pa_ref, s, *sems[0:3])
                _rs_send(head_rows(gx_ref)(s), px_ref, s, *sems[3:6])

        @pl.when(k == 0)
        def _():
            acc[...] = jnp.zeros_like(acc)

        acc[...] += _dot_tn(h_ref[...], d_ref[...])

        @pl.when(k == nk - 1)
        def _():
            q = ord_ref[p]
            for half in range(2):
                rs[0][q, half] = acc[:, half * cw:(half + 1) * cw].astype(BF16)
            _rs2_to_sibling(q, rs)

        @pl.when((k == nk - 1) & (p > 0))
        def _():
            _rs2_forward(ord_ref[p - 1], parts_ref, rs)

        @pl.when((p == NCHIP - 1) & (k == nk - 1))
        def _():
            _rs2_forward(ord_ref[p], parts_ref, rs)
            _rs2_finish(parts_ref, rs)
            _rs_finish(head_rows(ga_ref), pa_ref, *sems[0:3])
            _rs_finish(head_rows(gx_ref), px_ref, *sems[3:6])

    return pl.pallas_call(
        body, name="in_wgrad",
        grid_spec=pltpu.PrefetchScalarGridSpec(
            num_scalar_prefetch=1, grid=(NCHIP, nk),
            in_specs=[pl.BlockSpec((tk, D), lambda p, k, o: (k, 0)),
                      pl.BlockSpec((tk, 2 * cw), lambda p, k, o: (k, o[p])), _ANY, _ANY],
            out_specs=[_ANY, _ANY, _ANY],
            scratch_shapes=[pltpu.VMEM((D, 2 * cw), F32)] + _rs2_scratch((D, cw)) + _RS_SEMS * 2),
        out_shape=[jax.ShapeDtypeStruct((NCHIP, D, cw), BF16), jax.ShapeDtypeStruct((NDEV, HEADS, hr, HB), F32),
                   jax.ShapeDtypeStruct((NDEV, HEADS, hr, HB), F32)],
        compiler_params=_cp("arbitrary", "arbitrary"),
    )(order, h, dproj, g_wa, g_wx)


def _adam_math(w, g, m, v):
    m = ADAM_B1 * m + (1.0 - ADAM_B1) * g
    v = ADAM_B2 * v + (1.0 - ADAM_B2) * (g * g)
    m_hat = m / (1.0 - ADAM_B1 ** ADAM_STEP)
    v_hat = v / (1.0 - ADAM_B2 ** ADAM_STEP)
    delta = -ADAM_LR * (m_hat / (jnp.sqrt(v_hat) + ADAM_EPS) + ADAM_WD * w)
    return delta, m, v


def _ada_bwd(c_all, dmod_cols, w, m, v):
    rb = 256
    n = w.shape[1]
    nrow = c_all.shape[0]

    def body(c_ref, d_ref, w_ref, m_ref, v_ref, g_ref, dl_ref, nm_ref, nv_ref):
        cv = c_ref[...]
        g = _dot_tn((cv * _sig(cv)).astype(BF16), d_ref[...].astype(BF16))
        g_ref[...] = g
        dl_ref[...], nm_ref[...], nv_ref[...] = _adam_math(w_ref[...], g, m_ref[...], v_ref[...])

    blk = pl.BlockSpec((rb, n), lambda i: (i, 0))
    sds = jax.ShapeDtypeStruct(w.shape, F32)
    return pl.pallas_call(
        body, name="ada_bwd", grid=(D // rb,),
        in_specs=[pl.BlockSpec((nrow, rb), lambda i: (0, i)), pl.BlockSpec((nrow, n), lambda i: (0, 0)), blk, blk, blk],
        out_specs=[blk, blk, blk, blk], out_shape=[sds, sds, sds, sds],
        compiler_params=_cp("parallel"),
    )(c_all, dmod_cols, w, m, v)


def _adam(name, parts, w, m, v):
    p, r, c = parts.shape
    rb = r
    for cand in (256, 128, 64, 32, 16, 8):
        if r % cand == 0 and r >= cand:
            rb = cand
            break

    def body(p_ref, w_ref, m_ref, v_ref, g_ref, dl_ref, nm_ref, nv_ref):
        g = p_ref[0].astype(F32)
        for q in range(1, p):
            g = g + p_ref[q].astype(F32)
        g_ref[...] = g
        dl_ref[...], nm_ref[...], nv_ref[...] = _adam_math(w_ref[...], g, m_ref[...], v_ref[...])

    blk = pl.BlockSpec((rb, c), lambda i: (i, 0))
    sds = jax.ShapeDtypeStruct((r, c), F32)
    return pl.pallas_call(
        body, name=name, grid=(r // rb,),
        in_specs=[pl.BlockSpec((p, rb, c), lambda i: (0, i, 0)), blk, blk, blk],
        out_specs=[blk, blk, blk, blk], out_shape=[sds, sds, sds, sds],
        compiler_params=_cp("parallel"),
    )(parts, w, m, v)


_SMALL_SEMS = [pltpu.SemaphoreType.DMA((7,)), pltpu.SemaphoreType.DMA((7,)), pltpu.SemaphoreType.DMA]
_VMEM = pl.BlockSpec(memory_space=pltpu.VMEM)


def _exchange_small(x_ref, out_ref, send_sems, recv_sems, local_sem):
    m_per = x_ref.shape[0]
    x, y, c = _my_pos()
    me, sibling = (x, y, c), (x, y, 1 - c)
    chips = [(1 - x, y), (x, 1 - y), (1 - x, 1 - y)]

    def rows(px, py, pc):
        return out_ref.at[pl.ds((4 * px + 2 * py + pc) * m_per, m_per), :]

    def copy(k, block, to, src=None):
        return pltpu.make_async_remote_copy(
            src_ref=rows(*block) if src is None else src, dst_ref=rows(*block),
            send_sem=send_sems.at[k], recv_sem=recv_sems.at[k], device_id=to, device_id_type=MESH)

    mine = pltpu.make_async_copy(x_ref, rows(*me), local_sem)
    mine.start()
    first = [copy(0, me, sibling, src=x_ref)]
    first += [copy(1 + j, me, (*chip, c), src=x_ref) for j, chip in enumerate(chips)]
    for cp in first:
        cp.start()
    passed = [copy(4 + j, (*chip, c), sibling) for j, chip in enumerate(chips)]
    for j, chip in enumerate(chips):
        copy(1 + j, (*chip, c), me).wait_recv()
        passed[j].start()
    copy(0, sibling, me).wait_recv()
    for j, chip in enumerate(chips):
        copy(4 + j, (*chip, 1 - c), me).wait_recv()
    for cp in first + passed:
        cp.wait_send()
    mine.wait()


def _all_gather_small(name, v):
    def body(x_ref, out_ref, send_sems, recv_sems, local_sem):
        _exchange_small(x_ref, out_ref, send_sems, recv_sems, local_sem)

    return pl.pallas_call(
        body, name=name, out_shape=jax.ShapeDtypeStruct((NDEV * v.shape[0], v.shape[1]), v.dtype),
        in_specs=[_VMEM], out_specs=_VMEM, scratch_shapes=_SMALL_SEMS,
    )(v)


def _ada_mod(pack, w_ada, b_cols):
    ncol = w_ada.shape[1]

    def body(p_ref, w_ref, b_ref, all_ref, mod_ref, cols, *sems):
        _exchange_small(p_ref, all_ref, *sems[0:3])
        c_all = jnp.concatenate([all_ref[8 * d:8 * d + 1, 0:D] for d in range(NDEV)], axis=0)
        c16 = jnp.concatenate([c_all, jnp.zeros_like(c_all)], axis=0)
        mod16 = _dot((c16 * _sig(c16)).astype(BF16), w_ref[...].astype(BF16)) + b_ref[...]
        cols[...] = mod16[0:NDEV]
        _exchange_small(cols, mod_ref, *sems[3:6])

    return pl.pallas_call(
        body, name="ada_mod",
        out_shape=[jax.ShapeDtypeStruct((NDEV * 8, pack.shape[1]), F32), jax.ShapeDtypeStruct((NDEV * 8, ncol), F32)],
        in_specs=[_VMEM, _VMEM, _VMEM], out_specs=[_VMEM, _VMEM],
        scratch_shapes=[pltpu.VMEM((NDEV, ncol), F32)] + _SMALL_SEMS * 2,
        compiler_params=_cp(),
    )(pack, w_ada, b_cols)


def _blk_rows(n):
    return lambda ref, b: ref.at[pl.ds(pl.multiple_of(b * n, 8), n), :]


def _blk_lead(ref, b):
    return ref.at[b]


def _blk_heads(ref, b):
    return ref.at[:, pl.ds(pl.multiple_of(b * (HB // NDEV), 8), HB // NDEV), :]


def _ag_phases(ins, outs, slicers, send_sems, recv_sems, local_sems):
    na = len(ins)
    x, y, c = _my_pos()
    me, sibling = (x, y, c), (x, y, 1 - c)
    chips = [(1 - x, y), (x, 1 - y), (1 - x, 1 - y)]

    def copy(a, k, block, to, from_shard=False):
        px, py, pc = block
        dst = slicers[a](outs[a], 4 * px + 2 * py + pc)
        return pltpu.make_async_remote_copy(
            src_ref=ins[a] if from_shard else dst, dst_ref=dst,
            send_sem=send_sems.at[a * 7 + k], recv_sem=recv_sems.at[a * 7 + k], device_id=to, device_id_type=MESH)

    def local(a):
        return pltpu.make_async_copy(ins[a], slicers[a](outs[a], 4 * x + 2 * y + c), local_sems.at[a])

    def firsts(a):
        return [copy(a, 0, me, sibling, True)] + [copy(a, 1 + j, me, (*chip, c), True) for j, chip in enumerate(chips)]

    def start():
        for a in range(na):
            local(a).start()
            for cp in firsts(a):
                cp.start()

    def forward():
        for a in range(na):
            for j, chip in enumerate(chips):
                copy(a, 1 + j, (*chip, c), me).wait_recv()
                copy(a, 4 + j, (*chip, c), sibling).start()

    def finish():
        for a in range(na):
            copy(a, 0, sibling, me).wait_recv()
            for j, chip in enumerate(chips):
                copy(a, 4 + j, (*chip, 1 - c), me).wait_recv()
        for a in range(na):
            for cp in firsts(a) + [copy(a, 4 + j, (*chip, c), sibling) for j, chip in enumerate(chips)]:
                cp.wait_send()
            local(a).wait()

    return start, forward, finish


def _ag_sems(na):
    return [pltpu.SemaphoreType.DMA((7 * na,)), pltpu.SemaphoreType.DMA((7 * na,)), pltpu.SemaphoreType.DMA((na,))]


def _local_step(x, target, mod, g_mix, g_ffn, g_fin, prm, w_in_shard, shards):
    fulls = [(HEADS, HB, HB), (HEADS, HB, HB), (D, D), (NDEV, FB, D), (DFF, D)]
    slicers = [_blk_heads, _blk_heads, _blk_rows(D // NDEV), _blk_lead, _blk_rows(DFF // NDEV)]
    my_chip = _my_index() >> 1
    own_first = (my_chip ^ jnp.arange(NCHIP, dtype=jnp.int32)).astype(jnp.int32)
    early, late = [0, 1, 2, 4], [3]
    pick = lambda lst, idx: [lst[i] for i in idx]
    proj, h, w_in, (wa, wx, w_out, w_down) = _in_proj(x, mod, g_mix, w_in_shard, own_first, pick(shards, early),
                                                      pick(fulls, early), pick(slicers, early))
    merged, hl, sv, (w_gu,) = _mixer_fwd(proj, prm, wa, wx, pick(shards, late), pick(fulls, late),
                                         pick(slicers, late))
    w_gu = w_gu.reshape(2, 4, FB, D)
    x1, h2 = _out_proj(merged, x, mod, g_ffn, w_out)
    gu, dx2, dx2b, loss, d_gfin = _ffn_fwd(h2, x1, target, mod, g_fin, w_gu, w_down)
    dgu, act, dx1, dx1b, dmg, sums2 = _ffn_bwd(dx2, gu, x1, mod, g_ffn, w_gu, w_down, w_out)
    chip_order = _xor_order(_my_index() >> 1, NCHIP)
    p_wgu = _gu_wgrad(h2, dgu, chip_order)
    p_wdown, d_gt2 = _scaled_wgrad("down_wgrad", act, dx2b, w_down, 5, mod, chip_order)
    p_wout, d_gt1 = _scaled_wgrad("out_wgrad", merged.reshape(1, *merged.shape), dx1b, w_out, 2, mod,
                                  jnp.zeros((1,), jnp.int32))
    dproj, msums, g_wa, g_wx = _mixer_bwd(proj, hl, sv, dmg, prm, wa, wx)
    p_win, p_wa, p_wx = _in_wgrad(h, dproj, g_wa, g_wx, chip_order)
    grad_x, sums1 = _in_proj_bwd(dproj, w_in, x, dx1, mod, g_mix)
    return dict(loss=loss, grad_x=grad_x, d_gfin=d_gfin, sums1=sums1, sums2=sums2, msums=msums,
                d_gt1=d_gt1[0:1], d_gt2=d_gt2[0:1], p_win=p_win, p_wa=p_wa, p_wx=p_wx, p_wout=p_wout, p_wgu=p_wgu,
                p_wdown=p_wdown)


def kernel(x, c, w_ada, b_ada, g_norm_mix, w_in, conv_a_w, conv_b_w, conv_b_bias, w_rg_a, b_rg_a, w_rg_x, b_rg_x, lru_lambda, w_out, g_norm_ffn, w_gate_up, w_down, g_norm_final, loss_target, m_w_ada, m_b_ada, m_g_norm_mix, m_w_in, m_conv_a_w, m_conv_b_w, m_conv_b_bias, m_w_rg_a, m_b_rg_a, m_w_rg_x, m_b_rg_x, m_lru_lambda, m_w_out, m_g_norm_ffn, m_w_gate_up, m_w_down, m_g_norm_final, v_w_ada, v_b_ada, v_g_norm_mix, v_w_in, v_conv_a_w, v_conv_b_w, v_conv_b_bias, v_w_rg_a, v_b_rg_a, v_w_rg_x, v_b_rg_x, v_lru_lambda, v_w_out, v_g_norm_ffn, v_w_gate_up, v_w_down, v_g_norm_final):
    me = 4 * lax.axis_index("x") + 2 * lax.axis_index("y") + lax.axis_index("c")
    ncol = w_ada.shape[2]
    cw = conv_a_w.shape[2]

    pack0 = jnp.concatenate([c, conv_a_w.reshape(1, 3 * cw), conv_b_w.reshape(1, 4 * cw)], axis=1)
    b_cols = lax.dynamic_slice_in_dim(b_ada, me * ncol, ncol, axis=1)
    got0, got1 = _ada_mod(jnp.broadcast_to(pack0, (8, pack0.shape[1])), w_ada[0], b_cols)
    got0 = got0.reshape(NDEV, 8, -1)[:, 0, :]
    c_all = got0[:, :D]
    conv_a = got0[:, D:D + 3 * cw].reshape(NDEV, 3, cw).transpose(1, 0, 2).reshape(3, D)
    conv_b = got0[:, D + 3 * cw:].reshape(NDEV, 4, cw).transpose(1, 0, 2).reshape(4, D)
    c16 = jnp.concatenate([c_all, jnp.zeros((8, D), F32)], axis=0)
    mod6 = lax.dynamic_index_in_dim(got1.reshape(NDEV, NDEV, ncol), me, axis=1, keepdims=False).reshape(6, D)
    mod = jnp.concatenate([mod6, jnp.zeros((2, D), F32)], axis=0)

    tr = lambda a: jnp.swapaxes(a, 1, 2)
    shards = [w_rg_a[0].astype(BF16), w_rg_x[0].astype(BF16), w_out[0].astype(BF16), tr(w_gate_up)[0].astype(BF16),
              w_down[0].astype(BF16)]

    prm = jnp.concatenate([conv_a, conv_b, conv_b_bias, b_rg_a, b_rg_x, lru_lambda, jnp.zeros((5, D), F32)], axis=0)
    r = _local_step(x[0], loss_target[0], mod, g_norm_mix, g_norm_ffn, g_norm_final.reshape(1, D), prm,
                    w_in[0].astype(BF16), shards)

    parts = [r["p_win"], r["p_wa"], r["p_wx"], r["p_wout"], r["p_wgu"], r["p_wdown"]]
    big = {}
    for nm, p, w, m, v in (("w_in", parts[0], w_in, m_w_in, v_w_in), ("w_rg_a", parts[1], w_rg_a, m_w_rg_a, v_w_rg_a),
                           ("w_rg_x", parts[2], w_rg_x, m_w_rg_x, v_w_rg_x), ("w_out", parts[3], w_out, m_w_out, v_w_out),
                           ("w_gate_up", parts[4], tr(w_gate_up), tr(m_w_gate_up), tr(v_w_gate_up)),
                           ("w_down", parts[5], w_down, m_w_down, v_w_down)):
        two_d = (-1, w.shape[-1])
        outs = _adam("adam_" + nm, p.reshape((p.shape[0],) + w.reshape(two_d).shape), w.reshape(two_d), m.reshape(two_d),
                     v.reshape(two_d))
        big[nm] = [o.reshape(w.shape) for o in outs]
    big["w_gate_up"] = [tr(o) for o in big["w_gate_up"]]

    small = jnp.concatenate([
        r["sums1"][S_SH:S_SH + 1], r["sums1"][S_SC:S_SC + 1], r["d_gt1"],
        r["sums2"][S_SH:S_SH + 1], r["sums2"][S_SC:S_SC + 1], r["d_gt2"],
        r["sums1"][S_G:S_G + 1],
        r["msums"][M_CBIAS:M_CBIAS + 1], r["msums"][M_BA:M_BA + 1], r["msums"][M_BX:M_BX + 1],
        r["msums"][M_LS:M_LS + 1],
        r["sums2"][S_G:S_G + 1], r["d_gfin"],
        r["msums"][M_WA:M_WA + 3], r["msums"][M_WB:M_WB + 4],
        jnp.broadcast_to(r["loss"][0:1, 0:1], (1, D)),
        jnp.zeros((3, D), F32)], axis=0)
    got2 = _all_gather_small("gather_small", small).reshape(NDEV, 24, D)

    rep_w = jnp.concatenate([b_ada.reshape(6, D), g_norm_mix, conv_b_bias, b_rg_a, b_rg_x, lru_lambda, g_norm_ffn,
                             g_norm_final.reshape(1, D), jnp.zeros((3, D), F32)], axis=0)
    rep_m = jnp.concatenate([m_b_ada.reshape(6, D), m_g_norm_mix, m_conv_b_bias, m_b_rg_a, m_b_rg_x, m_lru_lambda,
                             m_g_norm_ffn, m_g_norm_final.reshape(1, D), jnp.zeros((3, D), F32)], axis=0)
    rep_v = jnp.concatenate([v_b_ada.reshape(6, D), v_g_norm_mix, v_conv_b_bias, v_b_rg_a, v_b_rg_x, v_lru_lambda,
                             v_g_norm_ffn, v_g_norm_final.reshape(1, D), jnp.ones((3, D), F32)], axis=0)
    rep = _adam("adam_rep", got2[:, :16, :], rep_w, rep_m, rep_v)

    conv_parts = lax.dynamic_slice_in_dim(got2[:, 13:21, :], me * cw, cw, axis=2)
    cv_w = jnp.concatenate([conv_a_w[0], conv_b_w[0], jnp.zeros((1, cw), F32)], axis=0)
    cv_m = jnp.concatenate([m_conv_a_w[0], m_conv_b_w[0], jnp.zeros((1, cw), F32)], axis=0)
    cv_v = jnp.concatenate([v_conv_a_w[0], v_conv_b_w[0], jnp.ones((1, cw), F32)], axis=0)
    cvo = _adam("adam_conv", conv_parts, cv_w, cv_m, cv_v)

    dmod_cols = lax.dynamic_slice_in_dim(got2[:, :6, :].reshape(NDEV, 6 * D), me * ncol, ncol, axis=1)
    dmod16 = jnp.concatenate([dmod_cols, jnp.zeros((8, ncol), F32)], axis=0)
    ada = _ada_bwd(c16, dmod16, w_ada[0], m_w_ada[0], v_w_ada[0])

    loss = jnp.sum(got2[:, 20, 0])

    def pick(q):
        one = lambda i: rep[q][i:i + 1]
        return [ada[q].reshape(w_ada.shape), rep[q][0:6].reshape(b_ada.shape), one(6), big["w_in"][q],
                cvo[q][0:3].reshape(conv_a_w.shape), cvo[q][3:7].reshape(conv_b_w.shape), one(7),
                big["w_rg_a"][q], one(8), big["w_rg_x"][q], one(9), one(10), big["w_out"][q], one(11),
                big["w_gate_up"][q], big["w_down"][q], rep[q][12]]

    return (loss, r["grad_x"].reshape(x.shape), *pick(0), *pick(1), *pick(2), *pick(3))
```

```python
import math

import jax
import jax.numpy as jnp
from jax import lax
from jax.experimental import pallas as pl
from jax.experimental.pallas import tpu as pltpu

F32 = jnp.float32
BF16 = jnp.bfloat16

D = 1024
DFF = 2816
NDEV = 8
HEADS = 4
HB = D // HEADS
FB = DFF // 4
EPS = 1e-6
LRU_C = 8.0
ADAM_LR, ADAM_B1, ADAM_B2, ADAM_EPS, ADAM_WD, ADAM_STEP = 0.001, 0.9, 0.999, 1e-08, 0.01, 10

VMEM_LIMIT = 56 * 1024 * 1024
TM = 512
TMI = 1024
TMF = 256
TK = 2048
TKI = 2048
SUB = 256
TT = 256
CG = 256
MESH = pl.DeviceIdType.MESH


def _cp(*sem):
    return pltpu.CompilerParams(dimension_semantics=sem, vmem_limit_bytes=VMEM_LIMIT)


def _sig(x):
    return 1.0 / (1.0 + jnp.exp(-x))


def _log_sigmoid(x):
    z = jnp.exp(-jnp.abs(x))
    u = 1.0 + z
    d = u - 1.0
    l1p = jnp.where(d == 0.0, z, jnp.log(u) * (z / jnp.where(d == 0.0, 1.0, d)))
    return -(jnp.maximum(-x, 0.0) + l1p)


def _neg_expm1(x):
    p = x * (1.0 + x * 0.5 * (1.0 + x * (1.0 / 3.0) * (1.0 + x * 0.25 * (1.0 + x * 0.2 * (1.0 + x * (1.0 / 6.0))))))
    return jnp.where(x > -0.25, -p, 1.0 - jnp.exp(x))


_GC = math.sqrt(2.0 / math.pi)


def _gelu(x):
    t = jnp.tanh(_GC * (x + 0.044715 * x * x * x))
    return 0.5 * x * (1.0 + t), t


def _dot(a, b):
    return jnp.dot(a, b, preferred_element_type=F32)


def _dot_nt(a, b):
    return lax.dot_general(a, b, (((1,), (1,)), ((), ())), preferred_element_type=F32)


def _dot_tn(a, b):
    return lax.dot_general(a, b, (((0,), (0,)), ((), ())), preferred_element_type=F32)


def _resident(shape):
    return pl.BlockSpec(shape, lambda *_: (0,) * len(shape), pipeline_mode=pl.Buffered(1))


def _sub_blocks(n_rows):
    step = min(SUB, n_rows)
    return [slice(r, r + step) for r in range(0, n_rows, step)]


def _fold8(v):
    return v[0:8] + v[8:16]


def _pj(ref, s, rows=slice(None), cols=slice(0, D)):
    return ref[rows, s * D + cols.start:s * D + cols.stop]


def _in_proj(x, mod, g_mix, w_shard, order, shards, fulls, slicers):
    t_len = x.shape[0]
    tm = min(TMI, t_len)
    ni = t_len // tm
    na = len(shards)
    cw = 7 * D // NDEV
    rc = 32

    def body(ord_ref, x_ref, mod_ref, g_ref, wsh_ref, *rest):
        ins, (proj_ref, h_ref, wfull_ref), outs = rest[:na], rest[na:na + 3], rest[na + 3:2 * na + 3]
        h_scr, w_scr, wsend, wrecv, wlocal, wout = rest[2 * na + 3:2 * na + 9]
        start, forward, finish = _ag_phases(ins, outs, slicers, *rest[2 * na + 9:])
        p, i = pl.program_id(0), pl.program_id(1)
        x_, y_, c = _my_pos()
        me, sibling = (x_, y_, c), (x_, y_, 1 - c)
        chip_at = [None, (x_, 1 - y_), (1 - x_, y_), (1 - x_, 1 - y_)]

        def cols(px, py, pc):
            return w_scr.at[:, pl.ds(pl.multiple_of((4 * px + 2 * py + pc) * cw, 128), cw)]

        def wcopy(k, block, to, from_shard=False):
            dst = cols(*block)
            return pltpu.make_async_remote_copy(src_ref=wsh_ref if from_shard else dst, dst_ref=dst,
                                                send_sem=wsend.at[k], recv_sem=wrecv.at[k], device_id=to,
                                                device_id_type=MESH)

        own_local = pltpu.make_async_copy(wsh_ref, cols(*me), wlocal)
        to_hbm = pltpu.make_async_copy(w_scr, wfull_ref, wout)

        @pl.when((p == 0) & (i == 0))
        def _():
            own_local.start()
            wcopy(0, me, sibling, True).start()
            for q in (1, 2):
                wcopy(q, me, (*chip_at[q], c), True).start()
            own_local.wait()
            wcopy(0, sibling, me).wait_recv()

        @pl.when((p == 0) & (i == ni // 2))
        def _():
            wcopy(3, me, (*chip_at[3], c), True).start()

        for q in (1, 2, 3):
            @pl.when((p == q - 1) & (i == ni - 1))
            def _():
                wcopy(q, (*chip_at[q], c), me).wait_recv()
                wcopy(3 + q, (*chip_at[q], c), sibling).start()

            @pl.when((p == q) & (i == 0))
            def _():
                wcopy(3 + q, (*chip_at[q], 1 - c), me).wait_recv()

        @pl.when((p == 1) & (i == 0))
        def _():
            start()

        @pl.when((p == NCHIP - 1) & (i == ni // 2))
        def _():
            forward()

        @pl.when((p == NCHIP - 1) & (i == 0))
        def _():
            to_hbm.start()

        gs = g_ref[...] * (1.0 + mod_ref[1:2, :])
        sh = mod_ref[0:1, :]

        wcols = pl.ds(pl.multiple_of(ord_ref[p] * (2 * cw), 128), 2 * cw)
        for sub in _sub_blocks(tm):
            for r0 in range(sub.start, sub.stop, rc):
                xv = x_ref[r0:r0 + rc, :]
                r = lax.rsqrt(jnp.mean(xv * xv, axis=-1, keepdims=True) + EPS)
                h_scr[r0:r0 + rc, :] = (xv * r * gs + sh).astype(BF16)
            proj_ref[sub, :] = _dot(h_scr[sub, :], w_scr[:, wcols]).astype(BF16)

        @pl.when(p == 0)
        def _():
            h_ref[...] = h_scr[...]

        @pl.when((p == NCHIP - 1) & (i == ni - 1))
        def _():
            wcopy(0, me, sibling, True).wait_send()
            for q in (1, 2, 3):
                wcopy(q, me, (*chip_at[q], c), True).wait_send()
                wcopy(3 + q, (*chip_at[q], c), sibling).wait_send()
            finish()
            to_hbm.wait()

    res = pl.pallas_call(
        body, name="in_proj",
        grid_spec=pltpu.PrefetchScalarGridSpec(
            num_scalar_prefetch=1, grid=(NCHIP, ni),
            in_specs=[pl.BlockSpec((tm, D), lambda p, i, o: (i, 0)),
                      pl.BlockSpec((8, D), lambda p, i, o: (0, 0)),
                      pl.BlockSpec((1, D), lambda p, i, o: (0, 0))] + [_ANY] * (1 + na),
            out_specs=[pl.BlockSpec((tm, 2 * cw), lambda p, i, o: (i, o[p])),
                       pl.BlockSpec((tm, D), lambda p, i, o: (jnp.where(p == 0, i, ni - 1), 0))]
            + [_ANY] * (1 + na),
            scratch_shapes=[pltpu.VMEM((tm, D), BF16), pltpu.VMEM((D, 7 * D), BF16),
                            pltpu.SemaphoreType.DMA((7,)), pltpu.SemaphoreType.DMA((7,)),
                            pltpu.SemaphoreType.DMA, pltpu.SemaphoreType.DMA] + _ag_sems(na)),
        out_shape=[jax.ShapeDtypeStruct((t_len, 7 * D), BF16), jax.ShapeDtypeStruct((t_len, D), BF16),
                   jax.ShapeDtypeStruct((D, 7 * D), BF16)]
        + [jax.ShapeDtypeStruct(f, sh.dtype) for f, sh in zip(fulls, shards)],
        compiler_params=_cp("arbitrary", "arbitrary"),
    )(order, x, mod, g_mix, w_shard, *shards)
    return res[0], res[1], res[2], res[3:]


P_WA, P_WB, P_CBIAS, P_BA, P_BX, P_LAM = 0, 3, 7, 8, 9, 10
SV_PLANES = SV_U, SV_YA, SV_R, SV_I, SV_A, SV_MULT = range(6)


def _lru_gates(rp, ip, ls, first_row):
    r = _sig(rp)
    ig = _sig(ip)
    la = LRU_C * r * ls
    a = jnp.exp(la)
    m2 = _neg_expm1(2.0 * la)
    mult = jnp.where(first_row, 1.0, jnp.sqrt(jnp.maximum(m2, 0.0)))
    return r, ig, la, a, m2, mult


def _shift_down(cur, prev, s, row):
    return jnp.where(row >= s, pltpu.roll(cur, s, 0), pltpu.roll(prev, s, 0))


def _shift_up(cur, nxt, s, row):
    return jnp.where(row < 8 - s, pltpu.roll(cur, 8 - s, 0), pltpu.roll(nxt, 8 - s, 0))


def _conv_fwd_rows(tt, proj_ref, prm_ref, xe, ve, u_s, ub_s, ya_s):
    row = lax.broadcasted_iota(jnp.int32, (8, CG), 0)
    w_b = [prm_ref[P_WB + k:P_WB + k + 1, :] for k in range(4)]
    w_a = [prm_ref[P_WA + k:P_WA + k + 1, :] for k in range(3)]
    bias = prm_ref[P_CBIAS:P_CBIAS + 1, :]

    def blk(ib, carry):
        r0 = pl.multiple_of(ib * 16, 16)
        rows = pl.ds(r0, 16)
        for g in range(D // CG):
            cs = slice(g * CG, (g + 1) * CG)
            x16 = _pj(proj_ref, 3, rows, cs).astype(F32)
            v16 = _pj(proj_ref, 1, rows, cs).astype(F32) * _pj(proj_ref, 2, rows, cs).astype(F32)
            xp = xe[pl.ds(r0, 8), cs]
            vp = ve[pl.ds(r0, 8), cs]
            xe[pl.ds(r0 + 8, 16), cs] = x16
            ve[pl.ds(r0 + 8, 16), cs] = v16
            us, yas = [], []
            for sb in range(2):
                xc, vc = x16[8 * sb:8 * sb + 8], v16[8 * sb:8 * sb + 8]
                u8 = bias[:, cs] + w_b[3][:, cs] * xc
                for s in (1, 2, 3):
                    u8 = u8 + w_b[3 - s][:, cs] * _shift_down(xc, xp, s, row)
                y8 = w_a[2][:, cs] * vc
                for s in (1, 2):
                    y8 = y8 + w_a[2 - s][:, cs] * _shift_down(vc, vp, s, row)
                us.append(u8)
                yas.append(y8)
                xp, vp = xc, vc
            u16 = jnp.concatenate(us, axis=0)
            u_s[rows, cs] = u16
            ub_s[rows, cs] = u16.astype(BF16)
            ya_s[rows, cs] = jnp.concatenate(yas, axis=0)
        return carry

    lax.fori_loop(0, tt // 16, blk, 0)


def _mixer_fwd(proj, prm, wa, wx, shards, fulls, slicers):
    t_len = proj.shape[0]
    tt = min(TT, t_len)
    nt = t_len // tt
    na = len(shards)

    def body(proj_ref, prm_ref, wa_ref, wx_ref, *rest):
        ins, (mg_ref, hl_ref, sv_hbm), outs = rest[:na], rest[na:na + 3], rest[na + 3:2 * na + 3]
        xe, ve, hc, rp_s, ip_s, ub_s, sv_st, sv_sems = rest[2 * na + 3:2 * na + 11]
        start, forward, finish = _ag_phases(ins, outs, slicers, *rest[2 * na + 11:])
        t = pl.program_id(0)

        slot = t % 2
        sv_ref = sv_st.at[slot]

        def sv_out(tile, sl):
            rows = pl.ds(pl.multiple_of(tile * tt, tt), tt)
            return pltpu.make_async_copy(sv_st.at[sl], sv_hbm.at[:, rows, :], sv_sems.at[sl])

        @pl.when(t >= 2)
        def _():
            sv_out(t - 2, slot).wait()

        @pl.when(t == 0)
        def _():
            start()
            xe[0:8, :] = jnp.zeros((8, D), F32)
            ve[0:8, :] = jnp.zeros((8, D), F32)
            hc[...] = jnp.zeros((8, D), F32)

        @pl.when(t == (3 * nt) // 4)
        def _():
            forward()

        _conv_fwd_rows(tt, proj_ref, prm_ref, xe, ve, sv_ref.at[SV_U], ub_s, sv_ref.at[SV_YA])
        xe[0:8, :] = xe[tt:tt + 8, :]
        ve[0:8, :] = ve[tt:tt + 8, :]

        ub = ub_s[...]
        for h in range(HEADS):
            cs = slice(h * HB, (h + 1) * HB)
            rp_s[:, cs] = _dot(ub[:, cs], wa_ref[h]) + prm_ref[P_BA:P_BA + 1, cs]
            ip_s[:, cs] = _dot(ub[:, cs], wx_ref[h]) + prm_ref[P_BX:P_BX + 1, cs]

        ls_all = _log_sigmoid(prm_ref[P_LAM:P_LAM + 1, :])
        row = lax.broadcasted_iota(jnp.int32, (8, CG), 0)

        def blk(i, carry):
            r0 = pl.multiple_of(i * 16, 16)
            for g in range(D // CG):
                cs = slice(g * CG, (g + 1) * CG)
                ls = ls_all[:, cs]
                hprev = hc[:, cs]
                hs = []
                for sb in range(2):
                    rr = r0 + 8 * sb
                    first = (row + (t * tt + rr)) == 0
                    r8 = pl.ds(rr, 8)
                    r, ig, _, a, _, mult = _lru_gates(rp_s[r8, cs], ip_s[r8, cs], ls, first)
                    for plane, val in ((SV_R, r), (SV_I, ig), (SV_A, a), (SV_MULT, mult)):
                        sv_ref[plane, r8, cs] = val
                    b = mult * (ig * sv_ref[SV_U, r8, cs])
                    for s in (1, 2, 4):
                        a_sh = jnp.where(row >= s, pltpu.roll(a, s, 0), 1.0)
                        b_sh = jnp.where(row >= s, pltpu.roll(b, s, 0), 0.0)
                        b = a * b_sh + b
                        a = a * a_sh
                    hv = a * hprev + b
                    hprev = jnp.broadcast_to(hv[7:8, :], hv.shape)
                    hs.append(hv)
                hc[:, cs] = hprev
                h16 = jnp.concatenate(hs, axis=0)
                rows = pl.ds(r0, 16)
                gl, _ = _gelu(_pj(proj_ref, 4, rows, cs).astype(F32))
                y_b = h16 * gl
                y_a = _pj(proj_ref, 0, rows, cs).astype(F32) * sv_ref[SV_YA, rows, cs]
                mg = (_sig(_pj(proj_ref, 5, rows, cs).astype(F32)) * y_a
                      + _sig(_pj(proj_ref, 6, rows, cs).astype(F32)) * y_b)
                mg_ref[rows, cs] = mg.astype(BF16)
                hl_ref[rows, cs] = h16.astype(BF16)
            return carry

        lax.fori_loop(0, tt // 16, blk, 0)
        sv_out(t, slot).start()

        @pl.when(t == nt - 1)
        def _():
            finish()
            if nt >= 2:
                sv_out(t - 1, 1 - slot).wait()
            sv_out(t, slot).wait()

    res = pl.pallas_call(
        body, name="mixer_fwd", grid=(nt,),
        in_specs=[pl.BlockSpec((tt, 7 * D), lambda t: (t, 0)),
                  pl.BlockSpec((16, D), lambda t: (0, 0)),
                  pl.BlockSpec((HEADS, HB, HB), lambda t: (0, 0, 0)),
                  pl.BlockSpec((HEADS, HB, HB), lambda t: (0, 0, 0))] + [_ANY] * na,
        out_specs=[pl.BlockSpec((tt, D), lambda t: (t, 0)), pl.BlockSpec((tt, D), lambda t: (t, 0)), _ANY]
        + [_ANY] * na,
        out_shape=[jax.ShapeDtypeStruct((t_len, D), BF16), jax.ShapeDtypeStruct((t_len, D), BF16),
                   jax.ShapeDtypeStruct((len(SV_PLANES), t_len, D), F32)]
        + [jax.ShapeDtypeStruct(f, sh.dtype) for f, sh in zip(fulls, shards)],
        scratch_shapes=[pltpu.VMEM((tt + 8, D), F32), pltpu.VMEM((tt + 8, D), F32), pltpu.VMEM((8, D), F32),
                        pltpu.VMEM((tt, D), F32), pltpu.VMEM((tt, D), F32), pltpu.VMEM((tt, D), BF16),
                        pltpu.VMEM((2, len(SV_PLANES), tt, D), F32), pltpu.SemaphoreType.DMA((2,))]
        + _ag_sems(na),
        compiler_params=_cp("arbitrary"),
    )(proj, prm, wa, wx, *shards)
    return res[0], res[1], res[2], res[3:]


def _out_proj(merged, x, mod, g_ffn, w_out):
    t_len = x.shape[0]
    tm = min(TM, t_len)

    def body(mg_ref, x_ref, mod_ref, g_ref, w_ref, x1_ref, h2_ref):
        gt1 = mod_ref[2:3, :]
        gs = g_ref[...] * (1.0 + mod_ref[4:5, :])
        sh = mod_ref[3:4, :]
        for sub in _sub_blocks(tm):
            x1_ref[sub, :] = x_ref[sub, :] + gt1 * _dot(mg_ref[sub, :], w_ref[...])
            for r0 in range(sub.start, sub.stop, 16):
                x1 = x1_ref[r0:r0 + 16, :]
                r = lax.rsqrt(jnp.mean(x1 * x1, axis=-1, keepdims=True) + EPS)
                h2_ref[r0:r0 + 16, :] = (x1 * r * gs + sh).astype(BF16)

    return pl.pallas_call(
        body, name="out_proj", grid=(t_len // tm,),
        in_specs=[pl.BlockSpec((tm, D), lambda i: (i, 0)), pl.BlockSpec((tm, D), lambda i: (i, 0)),
                  pl.BlockSpec((8, D), lambda i: (0, 0)), pl.BlockSpec((1, D), lambda i: (0, 0)),
                  pl.BlockSpec((D, D), lambda i: (0, 0))],
        out_specs=[pl.BlockSpec((tm, D), lambda i: (i, 0)), pl.BlockSpec((tm, D), lambda i: (i, 0))],
        out_shape=[jax.ShapeDtypeStruct((t_len, D), F32), jax.ShapeDtypeStruct((t_len, D), BF16)],
        compiler_params=_cp("parallel"),
    )(merged, x, mod, g_ffn, w_out)


def _ffn_fwd(h2, x1, target, mod, g_fin, w_gu, w_down):
    t_len = x1.shape[0]
    tm = min(TMF, t_len)

    def body(h2_ref, x1_ref, tg_ref, mod_ref, g_ref, wgu_ref, wd_ref, gu_ref, dx2_ref, dx2b_ref, loss_ref, dg_ref, acc):
        @pl.when(pl.program_id(0) == 0)
        def _():
            loss_ref[...] = jnp.zeros_like(loss_ref)
            dg_ref[...] = jnp.zeros_like(dg_ref)

        hb = h2_ref[...]
        ffn = None
        nxt = (_dot_nt(hb, wgu_ref[0, 0]), _dot_nt(hb, wgu_ref[1, 0]))
        for j in range(4):
            gate, up = nxt
            if j < 3:
                nxt = (_dot_nt(hb, wgu_ref[0, j + 1]), _dot_nt(hb, wgu_ref[1, j + 1]))
            gu_ref[0, j] = gate.astype(BF16)
            gu_ref[1, j] = up.astype(BF16)
            act = (gate * _sig(gate) * up).astype(BF16)
            part = _dot(act, wd_ref[j * FB:(j + 1) * FB, :])
            ffn = part if ffn is None else ffn + part
        acc[...] = ffn

        gt2 = mod_ref[5:6, :]
        gf = g_ref[...]

        s_loss = s_dg = jnp.zeros((8, D), F32)
        for r0 in range(0, tm, 16):
            rows = slice(r0, r0 + 16)
            x2 = x1_ref[rows, :] + gt2 * acc[rows, :]
            r = lax.rsqrt(jnp.mean(x2 * x2, axis=-1, keepdims=True) + EPS)
            xn = x2 * r
            diff = xn * gf - tg_ref[rows, :]
            dy = diff * (1.0 / D)
            dxn = dy * gf
            dx2 = r * (dxn - xn * jnp.mean(dxn * xn, axis=-1, keepdims=True))
            dx2_ref[rows, :] = dx2
            dx2b_ref[rows, :] = dx2.astype(BF16)
            s_loss, s_dg = s_loss + _fold8(diff * diff), s_dg + _fold8(dy * xn)
        loss_ref[...] += jnp.sum(s_loss) * (0.5 / D)
        dg_ref[...] += jnp.sum(s_dg, axis=0, keepdims=True)

    row = pl.BlockSpec((tm, D), lambda i: (i, 0))
    return pl.pallas_call(
        body, name="ffn_fwd", grid=(t_len // tm,),
        in_specs=[row, row, row, pl.BlockSpec((8, D), lambda i: (0, 0)), pl.BlockSpec((1, D), lambda i: (0, 0)),
                  _resident((2, 4, FB, D)), _resident((DFF, D))],
        out_specs=[pl.BlockSpec((2, 4, tm, FB), lambda i: (0, 0, i, 0)), row, row,
                   pl.BlockSpec((1, 128), lambda i: (0, 0)), pl.BlockSpec((1, D), lambda i: (0, 0))],
        out_shape=[jax.ShapeDtypeStruct((2, 4, t_len, FB), BF16), jax.ShapeDtypeStruct((t_len, D), F32),
                   jax.ShapeDtypeStruct((t_len, D), BF16),
                   jax.ShapeDtypeStruct((1, 128), F32), jax.ShapeDtypeStruct((1, D), F32)],
        scratch_shapes=[pltpu.VMEM((tm, D), F32)],
        compiler_params=_cp("arbitrary"),
    )(h2, x1, target, mod, g_fin, w_gu, w_down)


S_SH, S_SC, S_G = 0, 1, 2


def _norm_bwd_rows(span, sums, dh_ref, x_ref, dres_ref, scale, gain, write):
    gs = 1.0 + scale
    s_sh, s_sc, s_g = sums
    for r0 in range(span.start, span.stop, 16):
        rows = slice(r0, r0 + 16)
        dh = dh_ref[rows, :]
        xv = x_ref[rows, :]
        r = lax.rsqrt(jnp.mean(xv * xv, axis=-1, keepdims=True) + EPS)
        xn = xv * r
        dhn = dh * gs
        dxn = dhn * gain
        write(rows, dres_ref[rows, :] + r * (dxn - xn * jnp.mean(dxn * xn, axis=-1, keepdims=True)))
        s_sh, s_sc, s_g = s_sh + _fold8(dh), s_sc + _fold8(dh * (xn * gain)), s_g + _fold8(dhn * xn)
    return s_sh, s_sc, s_g


def _add_norm_sums(sums_ref, sums):
    for dst, s in zip((S_SH, S_SC, S_G), sums):
        sums_ref[dst:dst + 1, :] += jnp.sum(s, axis=0, keepdims=True)


def _ffn_bwd(dx2, gu, x1, mod, g_ffn, w_gu, w_down, w_out):
    t_len = x1.shape[0]
    tm = min(TMF, t_len)

    def body(dx2_ref, gu_ref, x1_ref, mod_ref, g_ref, wgu_ref, wd_ref, wo_ref,
             dgu_ref, act_ref, dx1_ref, dx1b_ref, dmg_ref, sums_ref, acc, dmo, dact_s):
        @pl.when(pl.program_id(0) == 0)
        def _():
            sums_ref[...] = jnp.zeros_like(sums_ref)

        dffn = (dx2_ref[...] * mod_ref[5:6, :]).astype(BF16)
        dact_s[0] = _dot_nt(dffn, wd_ref[0:FB, :])
        for j in range(4):
            if j < 3:
                dact_s[(j + 1) % 2] = _dot_nt(dffn, wd_ref[(j + 1) * FB:(j + 2) * FB, :])
            for r0 in range(0, tm, 16):
                rows = slice(r0, r0 + 16)
                dact = dact_s[j % 2, rows, :]
                gate = gu_ref[0, j, rows, :].astype(F32)
                up = gu_ref[1, j, rows, :].astype(F32)
                sg = _sig(gate)
                silu = gate * sg
                act_ref[j, rows, :] = (silu * up).astype(BF16)
                dgu_ref[0, j, rows, :] = (dact * up * (sg * (1.0 + gate * (1.0 - sg)))).astype(BF16)
                dgu_ref[1, j, rows, :] = (dact * silu).astype(BF16)
            part = _dot(dgu_ref[0, j], wgu_ref[0, j]) + _dot(dgu_ref[1, j], wgu_ref[1, j])
            if j == 0:
                acc[...] = part
            else:
                acc[...] += part

        gt1 = mod_ref[2:3, :]

        def write(rows, dx1):
            dx1_ref[rows, :] = dx1
            dx1b_ref[rows, :] = dx1.astype(BF16)
            dmo[rows, :] = (dx1 * gt1).astype(BF16)

        zero = jnp.zeros((8, D), F32)
        sums = (zero, zero, zero)
        for sub in (slice(0, tm // 2), slice(tm // 2, tm)):
            sums = _norm_bwd_rows(sub, sums, acc, x1_ref, dx2_ref, mod_ref[4:5, :], g_ref[...], write)
            dmg_ref[sub, :] = _dot_nt(dmo[sub, :], wo_ref[...]).astype(BF16)
        _add_norm_sums(sums_ref, sums)

    row = pl.BlockSpec((tm, D), lambda i: (i, 0))
    return pl.pallas_call(
        body, name="ffn_bwd", grid=(t_len // tm,),
        in_specs=[row, pl.BlockSpec((2, 4, tm, FB), lambda i: (0, 0, i, 0)), row,
                  pl.BlockSpec((8, D), lambda i: (0, 0)), pl.BlockSpec((1, D), lambda i: (0, 0)),
                  _resident((2, 4, FB, D)), _resident((DFF, D)), _resident((D, D))],
        out_specs=[pl.BlockSpec((2, 4, tm, FB), lambda i: (0, 0, i, 0)),
                   pl.BlockSpec((4, tm, FB), lambda i: (0, i, 0)), row, row, row,
                   pl.BlockSpec((8, D), lambda i: (0, 0))],
        out_shape=[jax.ShapeDtypeStruct((2, 4, t_len, FB), BF16), jax.ShapeDtypeStruct((4, t_len, FB), BF16),
                   jax.ShapeDtypeStruct((t_len, D), F32), jax.ShapeDtypeStruct((t_len, D), BF16),
                   jax.ShapeDtypeStruct((t_len, D), BF16), jax.ShapeDtypeStruct((8, D), F32)],
        scratch_shapes=[pltpu.VMEM((tm, D), F32), pltpu.VMEM((tm, D), BF16), pltpu.VMEM((2, tm, FB), F32)],
        compiler_params=_cp("arbitrary"),
    )(dx2, gu, x1, mod, g_ffn, w_gu, w_down, w_out)


def _my_pos():
    return lax.axis_index("x"), lax.axis_index("y"), lax.axis_index("c")


def _my_index():
    x, y, c = _my_pos()
    return 4 * x + 2 * y + c


def _device_of(b):
    return (b >> 2) & 1, (b >> 1) & 1, b & 1


def _rs_send(src, parts_ref, b, send_sems, recv_sems, local_sem):
    me = _my_index()
    dst = parts_ref.at[me]

    @pl.when(b == me)
    def _():
        pltpu.make_async_copy(src, dst, local_sem).start()

    @pl.when(b != me)
    def _():
        pltpu.make_async_remote_copy(src_ref=src, dst_ref=dst, send_sem=send_sems.at[b], recv_sem=recv_sems.at[me],
                                     device_id=_device_of(b), device_id_type=MESH).start()


def _rs_finish(src_of, parts_ref, send_sems, recv_sems, local_sem):
    me = _my_index()
    for s in range(NDEV):
        @pl.when(s != me)
        def _():
            cp = pltpu.make_async_remote_copy(src_ref=src_of(s), dst_ref=parts_ref.at[s], send_sem=send_sems.at[s],
                                              recv_sem=recv_sems.at[s], device_id=_device_of(s), device_id_type=MESH)
            cp.wait_send()
            cp.wait_recv()

        @pl.when(s == me)
        def _():
            pltpu.make_async_copy(src_of(s), parts_ref.at[s], local_sem).wait()


_RS_SEMS = [pltpu.SemaphoreType.DMA((NDEV,)), pltpu.SemaphoreType.DMA((NDEV,)), pltpu.SemaphoreType.DMA]
_ANY = pl.BlockSpec(memory_space=pl.ANY)


def _xor_order(me, n):
    return (me ^ (n - 1 - jnp.arange(n, dtype=jnp.int32))).astype(jnp.int32)


NCHIP = NDEV // 2


def _rs2_scratch(half_shape):
    blocks = lambda *lead: pltpu.VMEM(lead + tuple(half_shape), BF16)
    return [blocks(NCHIP, 2), blocks(NCHIP)] + [pltpu.SemaphoreType.DMA((NCHIP,))] * 4 + [pltpu.SemaphoreType.DMA]


def _rs2_to_sibling(q, rs):
    stage, from_sib, d_send, d_recv = rs[:4]
    x, y, c = _my_pos()
    pltpu.make_async_remote_copy(src_ref=stage.at[q, 1 - c], dst_ref=from_sib.at[q], send_sem=d_send.at[q],
                                 recv_sem=d_recv.at[q], device_id=(x, y, 1 - c), device_id_type=MESH).start()


def _rs2_forward(q, parts_ref, rs):
    stage, chip_sum, d_send, d_recv, i_send, i_recv, local_sem = rs
    x, y, c = _my_pos()
    my_chip = 2 * x + y
    pltpu.make_async_remote_copy(src_ref=stage.at[q, c], dst_ref=chip_sum.at[q], send_sem=d_send.at[q],
                                 recv_sem=d_recv.at[q], device_id=(x, y, 1 - c), device_id_type=MESH).wait_recv()
    chip_sum[q] = (stage[q, c].astype(F32) + chip_sum[q].astype(F32)).astype(BF16)

    @pl.when(q == my_chip)
    def _():
        pltpu.make_async_copy(chip_sum.at[q], parts_ref.at[my_chip], local_sem).start()

    @pl.when(q != my_chip)
    def _():
        pltpu.make_async_remote_copy(src_ref=chip_sum.at[q], dst_ref=parts_ref.at[my_chip], send_sem=i_send.at[q],
                                     recv_sem=i_recv.at[my_chip], device_id=((q >> 1) & 1, q & 1, c),
                                     device_id_type=MESH).start()


def _rs2_finish(parts_ref, rs):
    stage, chip_sum, d_send, d_recv, i_send, i_recv, local_sem = rs
    x, y, c = _my_pos()
    my_chip = 2 * x + y
    for q in range(NCHIP):
        pltpu.make_async_remote_copy(src_ref=stage.at[q, 1 - c], dst_ref=chip_sum.at[q], send_sem=d_send.at[q],
                                     recv_sem=d_recv.at[q], device_id=(x, y, 1 - c), device_id_type=MESH).wait_send()

        @pl.when(q != my_chip)
        def _():
            cp = pltpu.make_async_remote_copy(src_ref=chip_sum.at[q], dst_ref=parts_ref.at[q], send_sem=i_send.at[q],
                                              recv_sem=i_recv.at[q], device_id=((q >> 1) & 1, q & 1, c),
                                              device_id_type=MESH)
            cp.wait_send()
            cp.wait_recv()

        @pl.when(q == my_chip)
        def _():
            pltpu.make_async_copy(chip_sum.at[q], parts_ref.at[q], local_sem).wait()


def _gu_wgrad(h2, dgu, order):
    t_len = h2.shape[0]
    tk = min(TK, t_len)
    nk = t_len // tk

    def body(ord_ref, h_ref, d_ref, parts_ref, acc, *rs):
        p, k = pl.program_id(0), pl.program_id(1)

        @pl.when(k == 0)
        def _():
            acc[...] = jnp.zeros_like(acc)

        hb = h_ref[...]
        for half in range(2):
            acc[half] += _dot_tn(d_ref[0, half], hb)

        @pl.when(k == nk - 1)
        def _():
            q = ord_ref[p]
            rs[0][q] = acc[...].astype(BF16)
            _rs2_to_sibling(q, rs)

        @pl.when((k == min(1, nk - 1)) & (p > 0))
        def _():
            _rs2_forward(ord_ref[p - 1], parts_ref, rs)

        @pl.when((p == NCHIP - 1) & (k == nk - 1))
        def _():
            _rs2_forward(ord_ref[p], parts_ref, rs)
            _rs2_finish(parts_ref, rs)

    return pl.pallas_call(
        body, name="gu_wgrad",
        grid_spec=pltpu.PrefetchScalarGridSpec(
            num_scalar_prefetch=1, grid=(NCHIP, nk),
            in_specs=[pl.BlockSpec((tk, D), lambda p, k, o: (k, 0)),
                      pl.BlockSpec((1, 2, tk, FB), lambda p, k, o: (o[p], 0, k, 0))],
            out_specs=_ANY,
            scratch_shapes=[pltpu.VMEM((2, FB, D), F32)] + _rs2_scratch((FB, D))),
        out_shape=jax.ShapeDtypeStruct((NCHIP, FB, D), BF16),
        compiler_params=_cp("arbitrary", "arbitrary"),
    )(order, h2, dgu.reshape(NCHIP, 2, t_len, FB))


def _scaled_wgrad(name, a, dx, w, gate_row, mod, order):
    t_len = dx.shape[0]
    kb = w.shape[0] // NCHIP
    tk = min(TK, t_len)
    nk = t_len // tk
    rows = kb // 2
    if a.ndim == 3:
        a_spec = pl.BlockSpec((None, tk, kb), lambda p, k, o: (o[p], k, 0))
    else:
        a_spec = pl.BlockSpec((tk, kb), lambda p, k, o: (k, o[p]))

    def body(ord_ref, a_ref, dx_ref, w_ref, mod_ref, parts_ref, dg_ref, acc, *rs):
        p, k = pl.program_id(0), pl.program_id(1)

        @pl.when((p == 0) & (k == 0))
        def _():
            dg_ref[...] = jnp.zeros_like(dg_ref)

        @pl.when(k == 0)
        def _():
            acc[...] = jnp.zeros_like(acc)

        acc[...] += _dot_tn(a_ref[...], dx_ref[...])

        @pl.when(k == nk - 1)
        def _():
            q = ord_ref[p]
            z = acc[...]
            zg = (z * mod_ref[gate_row:gate_row + 1, :]).astype(BF16)
            dg_ref[0:1, :] += jnp.sum(z * w_ref[...].astype(F32), axis=0, keepdims=True)
            for half in range(2):
                rs[0][q, half] = zg[half * rows:(half + 1) * rows]
            _rs2_to_sibling(q, rs)

        @pl.when((k == min(1, nk - 1)) & (p > 0))
        def _():
            _rs2_forward(ord_ref[p - 1], parts_ref, rs)

        @pl.when((p == NCHIP - 1) & (k == nk - 1))
        def _():
            _rs2_forward(ord_ref[p], parts_ref, rs)
            _rs2_finish(parts_ref, rs)

    return pl.pallas_call(
        body, name=name,
        grid_spec=pltpu.PrefetchScalarGridSpec(
            num_scalar_prefetch=1, grid=(NCHIP, nk),
            in_specs=[a_spec,
                      pl.BlockSpec((tk, D), lambda p, k, o: (k, 0)),
                      pl.BlockSpec((kb, D), lambda p, k, o: (o[p], 0)),
                      pl.BlockSpec((8, D), lambda p, k, o: (0, 0))],
            out_specs=[_ANY, pl.BlockSpec((8, D), lambda p, k, o: (0, 0))],
            scratch_shapes=[pltpu.VMEM((kb, D), F32)] + _rs2_scratch((rows, D))),
        out_shape=[jax.ShapeDtypeStruct((NCHIP, rows, D), BF16), jax.ShapeDtypeStruct((8, D), F32)],
        compiler_params=_cp("arbitrary", "arbitrary"),
    )(order, a, dx, w, mod)


M_WA, M_WB, M_CBIAS, M_BA, M_BX, M_LS = 0, 3, 7, 8, 9, 10


def _conv_bwd_rows(tt, proj_ref, prm_ref, xe, ve, due, dye, dp_ref, acc8):
    row = lax.broadcasted_iota(jnp.int32, (8, CG), 0)
    w_b = [prm_ref[P_WB + k:P_WB + k + 1, :] for k in range(4)]
    w_a = [prm_ref[P_WA + k:P_WA + k + 1, :] for k in range(3)]

    def blk(ib, carry):
        r0 = pl.multiple_of(ib * 16, 16)
        rows = pl.ds(r0, 16)
        for g in range(D // CG):
            cs = slice(g * CG, (g + 1) * CG)
            du16, du_after = due[rows, cs], due[pl.ds(r0 + 16, 8), cs]
            dy16, dy_after = dye[rows, cs], dye[pl.ds(r0 + 16, 8), cs]
            cc16 = _pj(proj_ref, 1, rows, cs).astype(F32)
            cx16 = _pj(proj_ref, 2, rows, cs).astype(F32)
            x16 = _pj(proj_ref, 3, rows, cs).astype(F32)
            v16 = cc16 * cx16
            xp, vp = xe[pl.ds(r0, 8), cs], ve[pl.ds(r0, 8), cs]
            xe[pl.ds(r0 + 16, 8), cs] = x16[8:16]
            ve[pl.ds(r0 + 16, 8), cs] = v16[8:16]
            acc = [acc8[8 * k:8 * k + 8, cs] for k in range(8)]
            drx, dv = [], []
            for sb in range(2):
                lo = slice(8 * sb, 8 * sb + 8)
                duc, dyc, xc, vc = du16[lo], dy16[lo], x16[lo], v16[lo]
                du_n = du16[8:16] if sb == 0 else du_after
                dy_n = dy16[8:16] if sb == 0 else dy_after
                acc[0] = acc[0] + duc
                acc[4] = acc[4] + duc * xc
                d8 = w_b[3][:, cs] * duc
                for s in (1, 2, 3):
                    acc[4 - s] = acc[4 - s] + duc * _shift_down(xc, xp, s, row)
                    d8 = d8 + w_b[3 - s][:, cs] * _shift_up(duc, du_n, s, row)
                acc[7] = acc[7] + dyc * vc
                e8 = w_a[2][:, cs] * dyc
                for s in (1, 2):
                    acc[7 - s] = acc[7 - s] + dyc * _shift_down(vc, vp, s, row)
                    e8 = e8 + w_a[2 - s][:, cs] * _shift_up(dyc, dy_n, s, row)
                drx.append(d8)
                dv.append(e8)
                xp, vp = xc, vc
            for k in range(8):
                acc8[8 * k:8 * k + 8, cs] = acc[k]
            dv16 = jnp.concatenate(dv, axis=0)
            col = lambda s: slice(s * D + g * CG, s * D + (g + 1) * CG)
            dp_ref[rows, col(3)] = jnp.concatenate(drx, axis=0).astype(BF16)
            dp_ref[rows, col(1)] = (dv16 * cx16).astype(BF16)
            dp_ref[rows, col(2)] = (dv16 * cc16).astype(BF16)
        return carry

    lax.fori_loop(0, tt // 16, blk, 0)


def _mixer_bwd(proj, hl, sv, dmg, prm, wa, wx):
    t_len = proj.shape[0]
    tt = min(TT, t_len)
    nt = t_len // tt
    hb8 = tt // 8

    def rev(i):
        return nt - 1 - i

    def halo(i):
        return jnp.maximum(rev(i) * hb8 - 1, 0)

    def body(proj_ref, ph_ref, hl_ref, hh_ref, sv_ref, dmg_ref, prm_ref, wa_ref, wx_ref,
             dp_ref, sums_ref, gwa_ref, gwx_ref,
             xe, ve, he, due, dye, drp_s, dip_s, an, gn, acc8):
        i = pl.program_id(0)
        t = rev(i)

        @pl.when(i == 0)
        def _():
            sums_ref[...] = jnp.zeros_like(sums_ref)
            gwa_ref[...] = jnp.zeros_like(gwa_ref)
            gwx_ref[...] = jnp.zeros_like(gwx_ref)
            due[tt:tt + 8, :] = jnp.zeros((8, D), F32)
            dye[tt:tt + 8, :] = jnp.zeros((8, D), F32)
            an[...] = jnp.zeros((8, D), F32)
            gn[...] = jnp.zeros((8, D), F32)

        live = (t > 0).astype(F32)
        xe[0:8, :] = _pj(ph_ref, 3).astype(F32) * live
        ve[0:8, :] = _pj(ph_ref, 1).astype(F32) * _pj(ph_ref, 2).astype(F32) * live
        he[0:8, :] = hh_ref[...].astype(F32) * live
        he[8:8 + tt, :] = hl_ref[...].astype(F32)

        ls_all = _log_sigmoid(prm_ref[P_LAM:P_LAM + 1, :])
        row = lax.broadcasted_iota(jnp.int32, (8, CG), 0)
        nblk = tt // 16

        def blk(ib, carry):
            r0 = pl.multiple_of((nblk - 1 - ib) * 16, 16)
            rows = pl.ds(r0, 16)
            for g in range(D // CG):
                cs = slice(g * CG, (g + 1) * CG)
                ls = ls_all[:, cs]
                dm = dmg_ref[rows, cs].astype(F32)
                cb = _pj(proj_ref, 0, rows, cs).astype(F32)
                rg = _pj(proj_ref, 4, rows, cs).astype(F32)
                sga = _sig(_pj(proj_ref, 5, rows, cs).astype(F32))
                sgb = _sig(_pj(proj_ref, 6, rows, cs).astype(F32))
                ya0 = sv_ref[SV_YA, rows, cs]
                h16 = he[pl.ds(r0 + 8, 16), cs]
                gl, th = _gelu(rg)
                dgl = 0.5 * (1.0 + th) + 0.5 * rg * (1.0 - th * th) * (_GC * (1.0 + 3.0 * 0.044715 * rg * rg))
                y_a = cb * ya0
                y_b = h16 * gl
                dy_a = dm * sga
                dy_b = dm * sgb
                col = lambda s: slice(s * D + g * CG, s * D + (g + 1) * CG)
                dp_ref[rows, col(5)] = (dm * y_a * sga * (1.0 - sga)).astype(BF16)
                dp_ref[rows, col(6)] = (dm * y_b * sgb * (1.0 - sgb)).astype(BF16)
                dp_ref[rows, col(4)] = (dy_b * h16 * dgl).astype(BF16)
                dp_ref[rows, col(0)] = (dy_a * ya0).astype(BF16)
                dye[rows, cs] = dy_a * cb
                dh16 = dy_b * gl

                a_next = an[:, cs]
                g_next = gn[:, cs]
                s_ba = jnp.zeros((8, CG), F32)
                s_bx = jnp.zeros((8, CG), F32)
                s_ls = jnp.zeros((8, CG), F32)
                for sb in (1, 0):
                    rr = r0 + 8 * sb
                    first = (row + (t * tt + rr)) == 0
                    r8 = pl.ds(rr, 8)
                    uu, r, ig, a, mult = (sv_ref[pln, r8, cs] for pln in (SV_U, SV_R, SV_I, SV_A, SV_MULT))
                    ca = jnp.where(row < 7, pltpu.roll(a, 7, 0), a_next)
                    cb_ = dh16[8 * sb:8 * sb + 8, :]
                    for s in (1, 2, 4):
                        a_sh = jnp.where(row < 8 - s, pltpu.roll(ca, 8 - s, 0), 1.0)
                        b_sh = jnp.where(row < 8 - s, pltpu.roll(cb_, 8 - s, 0), 0.0)
                        cb_ = ca * b_sh + cb_
                        ca = ca * a_sh
                    gv = ca * g_next + cb_
                    g_next = jnp.broadcast_to(gv[0:1, :], gv.shape)
                    a_next = jnp.broadcast_to(a[0:1, :], a.shape)
                    hprev = jnp.where(row >= 1, pltpu.roll(he[pl.ds(rr + 8, 8), cs], 1, 0),
                                      pltpu.roll(he[pl.ds(rr, 8), cs], 1, 0))
                    da = gv * hprev
                    dmult = jnp.where(first, 0.0, gv * ig * uu)
                    dla = da * a + jnp.where(mult > 0.0, dmult * (-(a * a) / mult), 0.0)
                    drp = dla * (LRU_C * ls) * r * (1.0 - r)
                    dip = gv * mult * uu * ig * (1.0 - ig)
                    s_ls = s_ls + dla * (LRU_C * r)
                    s_ba = s_ba + drp
                    s_bx = s_bx + dip
                    drp_s[pl.ds(rr, 8), cs] = drp
                    dip_s[pl.ds(rr, 8), cs] = dip
                    due[pl.ds(rr, 8), cs] = gv * mult * ig
                an[:, cs] = a_next
                gn[:, cs] = g_next
                sums_ref[M_BA:M_BA + 1, cs] += jnp.sum(s_ba, axis=0, keepdims=True)
                sums_ref[M_BX:M_BX + 1, cs] += jnp.sum(s_bx, axis=0, keepdims=True)
                sums_ref[M_LS:M_LS + 1, cs] += jnp.sum(s_ls, axis=0, keepdims=True)
            return carry

        lax.fori_loop(0, nblk, blk, 0)

        drp_b = drp_s[...].astype(BF16)
        dip_b = dip_s[...].astype(BF16)
        ub = sv_ref[SV_U].astype(BF16)
        for h in range(HEADS):
            cs = slice(h * HB, (h + 1) * HB)
            due[0:tt, cs] += _dot_nt(drp_b[:, cs], wa_ref[h]) + _dot_nt(dip_b[:, cs], wx_ref[h])
            gwa_ref[h] += _dot_tn(ub[:, cs], drp_b[:, cs])
            gwx_ref[h] += _dot_tn(ub[:, cs], dip_b[:, cs])

        acc8[...] = jnp.zeros_like(acc8)
        _conv_bwd_rows(tt, proj_ref, prm_ref, xe, ve, due, dye, dp_ref, acc8)
        for k, dst in enumerate([M_CBIAS] + [M_WB + k for k in range(4)] + [M_WA + k for k in range(3)]):
            sums_ref[dst:dst + 1, :] += jnp.sum(acc8[8 * k:8 * k + 8, :], axis=0, keepdims=True)
        due[tt:tt + 8, :] = due[0:8, :]
        dye[tt:tt + 8, :] = dye[0:8, :]

        @pl.when(i == nt - 1)
        def _():
            sums_ref[M_LS:M_LS + 1, :] = sums_ref[M_LS:M_LS + 1, :] * _sig(-prm_ref[P_LAM:P_LAM + 1, :])

    big = lambda: pltpu.VMEM((tt + 8, D), F32)
    tile = lambda: pltpu.VMEM((tt, D), F32)
    return pl.pallas_call(
        body, name="mixer_bwd", grid=(nt,),
        in_specs=[pl.BlockSpec((tt, 7 * D), lambda i: (rev(i), 0)),
                  pl.BlockSpec((8, 7 * D), lambda i: (halo(i), 0)),
                  pl.BlockSpec((tt, D), lambda i: (rev(i), 0)),
                  pl.BlockSpec((8, D), lambda i: (halo(i), 0)),
                  pl.BlockSpec((len(SV_PLANES), tt, D), lambda i: (0, rev(i), 0)),
                  pl.BlockSpec((tt, D), lambda i: (rev(i), 0)),
                  pl.BlockSpec((16, D), lambda i: (0, 0)),
                  pl.BlockSpec((HEADS, HB, HB), lambda i: (0, 0, 0)),
                  pl.BlockSpec((HEADS, HB, HB), lambda i: (0, 0, 0))],
        out_specs=[pl.BlockSpec((tt, 7 * D), lambda i: (rev(i), 0)),
                   pl.BlockSpec((16, D), lambda i: (0, 0)),
                   pl.BlockSpec((HEADS, HB, HB), lambda i: (0, 0, 0)),
                   pl.BlockSpec((HEADS, HB, HB), lambda i: (0, 0, 0))],
        out_shape=[jax.ShapeDtypeStruct((t_len, 7 * D), BF16), jax.ShapeDtypeStruct((16, D), F32),
                   jax.ShapeDtypeStruct((HEADS, HB, HB), F32), jax.ShapeDtypeStruct((HEADS, HB, HB), F32)],
        scratch_shapes=[big(), big(), big(), big(), big(), tile(), tile(),
                        pltpu.VMEM((8, D), F32), pltpu.VMEM((8, D), F32), pltpu.VMEM((64, D), F32)],
        compiler_params=_cp("arbitrary"),
    )(proj, proj, hl, hl, sv, dmg, prm, wa, wx)


def _in_proj_bwd(dproj, w_in, x, dx1, mod, g_mix):
    t_len = x.shape[0]
    tm = min(TM, t_len)

    def body(dp_ref, w_ref, x_ref, dx1_ref, mod_ref, g_ref, gx_ref, sums_ref, acc):
        @pl.when(pl.program_id(0) == 0)
        def _():
            sums_ref[...] = jnp.zeros_like(sums_ref)

        def write(rows, dx):
            gx_ref[rows, :] = dx

        zero = jnp.zeros((8, D), F32)
        sums = (zero, zero, zero)
        for sub in _sub_blocks(tm):
            acc[sub, :] = _dot_nt(dp_ref[sub, :], w_ref[...])
            sums = _norm_bwd_rows(sub, sums, acc, x_ref, dx1_ref, mod_ref[1:2, :], g_ref[...], write)
        _add_norm_sums(sums_ref, sums)

    return pl.pallas_call(
        body, name="in_proj_bwd", grid=(t_len // tm,),
        in_specs=[pl.BlockSpec((tm, 7 * D), lambda i: (i, 0)),
                  _resident((D, 7 * D)),
                  pl.BlockSpec((tm, D), lambda i: (i, 0)), pl.BlockSpec((tm, D), lambda i: (i, 0)),
                  pl.BlockSpec((8, D), lambda i: (0, 0)), pl.BlockSpec((1, D), lambda i: (0, 0))],
        out_specs=[pl.BlockSpec((tm, D), lambda i: (i, 0)), pl.BlockSpec((8, D), lambda i: (0, 0))],
        out_shape=[jax.ShapeDtypeStruct((t_len, D), F32), jax.ShapeDtypeStruct((8, D), F32)],
        scratch_shapes=[pltpu.VMEM((tm, D), F32)],
        compiler_params=_cp("arbitrary"),
    )(dproj, w_in, x, dx1, mod, g_mix)


def _in_wgrad(h, dproj, g_wa, g_wx, order):
    t_len = h.shape[0]
    tk = min(TKI, t_len)
    nk = t_len // tk
    cw = 7 * D // NDEV
    hr = HB // NDEV

    def body(ord_ref, h_ref, d_ref, ga_ref, gx_ref, parts_ref, pa_ref, px_ref, acc, *scr):
        rs, sems = scr[:-6], scr[-6:]
        p, k = pl.program_id(0), pl.program_id(1)

        def head_rows(ref):
            return lambda s: ref.at[:, pl.ds(s * hr, hr), :]

        @pl.when((p == 0) & (k == 0))
        def _():
            for s in range(NDEV):
                _rs_send(head_rows(ga_ref)(s), pa_ref, s, *sems[0:3])
                _rs_send(head_rows(gx_ref)(s), px_ref, s, *sems[3:6])

        @pl.when(k == 0)
        def _():
            acc[...] = jnp.zeros_like(acc)

        acc[...] += _dot_tn(h_ref[...], d_ref[...])

        @pl.when(k == nk - 1)
        def _():
            q = ord_ref[p]
            for half in range(2):
                rs[0][q, half] = acc[:, half * cw:(half + 1) * cw].astype(BF16)
            _rs2_to_sibling(q, rs)

        @pl.when((k == min(1, nk - 1)) & (p > 0))
        def _():
            _rs2_forward(ord_ref[p - 1], parts_ref, rs)

        @pl.when((p == NCHIP - 1) & (k == nk - 1))
        def _():
            _rs2_forward(ord_ref[p], parts_ref, rs)
            _rs2_finish(parts_ref, rs)
            _rs_finish(head_rows(ga_ref), pa_ref, *sems[0:3])
            _rs_finish(head_rows(gx_ref), px_ref, *sems[3:6])

    return pl.pallas_call(
        body, name="in_wgrad",
        grid_spec=pltpu.PrefetchScalarGridSpec(
            num_scalar_prefetch=1, grid=(NCHIP, nk),
            in_specs=[pl.BlockSpec((tk, D), lambda p, k, o: (k, 0)),
                      pl.BlockSpec((tk, 2 * cw), lambda p, k, o: (k, o[p])), _ANY, _ANY],
            out_specs=[_ANY, _ANY, _ANY],
            scratch_shapes=[pltpu.VMEM((D, 2 * cw), F32)] + _rs2_scratch((D, cw)) + _RS_SEMS * 2),
        out_shape=[jax.ShapeDtypeStruct((NCHIP, D, cw), BF16), jax.ShapeDtypeStruct((NDEV, HEADS, hr, HB), F32),
                   jax.ShapeDtypeStruct((NDEV, HEADS, hr, HB), F32)],
        compiler_params=_cp("arbitrary", "arbitrary"),
    )(order, h, dproj, g_wa, g_wx)


def _adam_math(w, g, m, v):
    m = ADAM_B1 * m + (1.0 - ADAM_B1) * g
    v = ADAM_B2 * v + (1.0 - ADAM_B2) * (g * g)
    m_hat = m / (1.0 - ADAM_B1 ** ADAM_STEP)
    v_hat = v / (1.0 - ADAM_B2 ** ADAM_STEP)
    delta = -ADAM_LR * (m_hat / (jnp.sqrt(v_hat) + ADAM_EPS) + ADAM_WD * w)
    return delta, m, v


def _ada_bwd(c_all, dmod_cols, w, m, v):
    rb = 256
    n = w.shape[1]
    nrow = c_all.shape[0]

    def body(c_ref, d_ref, w_ref, m_ref, v_ref, g_ref, dl_ref, nm_ref, nv_ref):
        cv = c_ref[...]
        g = _dot_tn((cv * _sig(cv)).astype(BF16), d_ref[...].astype(BF16))
        g_ref[...] = g
        dl_ref[...], nm_ref[...], nv_ref[...] = _adam_math(w_ref[...], g, m_ref[...], v_ref[...])

    blk = pl.BlockSpec((rb, n), lambda i: (i, 0))
    sds = jax.ShapeDtypeStruct(w.shape, F32)
    return pl.pallas_call(
        body, name="ada_bwd", grid=(D // rb,),
        in_specs=[pl.BlockSpec((nrow, rb), lambda i: (0, i)), pl.BlockSpec((nrow, n), lambda i: (0, 0)), blk, blk, blk],
        out_specs=[blk, blk, blk, blk], out_shape=[sds, sds, sds, sds],
        compiler_params=_cp("parallel"),
    )(c_all, dmod_cols, w, m, v)


def _adam(name, parts, w, m, v):
    p, r, c = parts.shape
    rb = r
    for cand in (256, 128, 64, 32, 16, 8):
        if r % cand == 0 and r >= cand:
            rb = cand
            break

    def body(p_ref, w_ref, m_ref, v_ref, g_ref, dl_ref, nm_ref, nv_ref):
        g = p_ref[0].astype(F32)
        for q in range(1, p):
            g = g + p_ref[q].astype(F32)
        g_ref[...] = g
        dl_ref[...], nm_ref[...], nv_ref[...] = _adam_math(w_ref[...], g, m_ref[...], v_ref[...])

    blk = pl.BlockSpec((rb, c), lambda i: (i, 0))
    sds = jax.ShapeDtypeStruct((r, c), F32)
    return pl.pallas_call(
        body, name=name, grid=(r // rb,),
        in_specs=[pl.BlockSpec((p, rb, c), lambda i: (0, i, 0)), blk, blk, blk],
        out_specs=[blk, blk, blk, blk], out_shape=[sds, sds, sds, sds],
        compiler_params=_cp("parallel"),
    )(parts, w, m, v)


_SMALL_SEMS = [pltpu.SemaphoreType.DMA((7,)), pltpu.SemaphoreType.DMA((7,)), pltpu.SemaphoreType.DMA]
_VMEM = pl.BlockSpec(memory_space=pltpu.VMEM)


def _exchange_small(x_ref, out_ref, send_sems, recv_sems, local_sem):
    m_per = x_ref.shape[0]
    x, y, c = _my_pos()
    me, sibling = (x, y, c), (x, y, 1 - c)
    chips = [(1 - x, y), (x, 1 - y), (1 - x, 1 - y)]

    def rows(px, py, pc):
        return out_ref.at[pl.ds((4 * px + 2 * py + pc) * m_per, m_per), :]

    def copy(k, block, to, src=None):
        return pltpu.make_async_remote_copy(
            src_ref=rows(*block) if src is None else src, dst_ref=rows(*block),
            send_sem=send_sems.at[k], recv_sem=recv_sems.at[k], device_id=to, device_id_type=MESH)

    mine = pltpu.make_async_copy(x_ref, rows(*me), local_sem)
    mine.start()
    first = [copy(0, me, sibling, src=x_ref)]
    first += [copy(1 + j, me, (*chip, c), src=x_ref) for j, chip in enumerate(chips)]
    for cp in first:
        cp.start()
    passed = [copy(4 + j, (*chip, c), sibling) for j, chip in enumerate(chips)]
    for j, chip in enumerate(chips):
        copy(1 + j, (*chip, c), me).wait_recv()
        passed[j].start()
    copy(0, sibling, me).wait_recv()
    for j, chip in enumerate(chips):
        copy(4 + j, (*chip, 1 - c), me).wait_recv()
    for cp in first + passed:
        cp.wait_send()
    mine.wait()


def _all_gather_small(name, v):
    def body(x_ref, out_ref, send_sems, recv_sems, local_sem):
        _exchange_small(x_ref, out_ref, send_sems, recv_sems, local_sem)

    return pl.pallas_call(
        body, name=name, out_shape=jax.ShapeDtypeStruct((NDEV * v.shape[0], v.shape[1]), v.dtype),
        in_specs=[_VMEM], out_specs=_VMEM, scratch_shapes=_SMALL_SEMS,
    )(v)


def _ada_mod(pack, w_ada, b_cols):
    ncol = w_ada.shape[1]

    def body(p_ref, w_ref, b_ref, all_ref, mod_ref, cols, *sems):
        _exchange_small(p_ref, all_ref, *sems[0:3])
        c_all = jnp.concatenate([all_ref[8 * d:8 * d + 1, 0:D] for d in range(NDEV)], axis=0)
        c16 = jnp.concatenate([c_all, jnp.zeros_like(c_all)], axis=0)
        mod16 = _dot((c16 * _sig(c16)).astype(BF16), w_ref[...].astype(BF16)) + b_ref[...]
        cols[...] = mod16[0:NDEV]
        _exchange_small(cols, mod_ref, *sems[3:6])

    return pl.pallas_call(
        body, name="ada_mod",
        out_shape=[jax.ShapeDtypeStruct((NDEV * 8, pack.shape[1]), F32), jax.ShapeDtypeStruct((NDEV * 8, ncol), F32)],
        in_specs=[_VMEM, _VMEM, _VMEM], out_specs=[_VMEM, _VMEM],
        scratch_shapes=[pltpu.VMEM((NDEV, ncol), F32)] + _SMALL_SEMS * 2,
        compiler_params=_cp(),
    )(pack, w_ada, b_cols)


def _blk_rows(n):
    return lambda ref, b: ref.at[pl.ds(pl.multiple_of(b * n, 8), n), :]


def _blk_lead(ref, b):
    return ref.at[b]


def _blk_heads(ref, b):
    return ref.at[:, pl.ds(pl.multiple_of(b * (HB // NDEV), 8), HB // NDEV), :]


def _ag_phases(ins, outs, slicers, send_sems, recv_sems, local_sems):
    na = len(ins)
    x, y, c = _my_pos()
    me, sibling = (x, y, c), (x, y, 1 - c)
    chips = [(1 - x, y), (x, 1 - y), (1 - x, 1 - y)]

    def copy(a, k, block, to, from_shard=False):
        px, py, pc = block
        dst = slicers[a](outs[a], 4 * px + 2 * py + pc)
        return pltpu.make_async_remote_copy(
            src_ref=ins[a] if from_shard else dst, dst_ref=dst,
            send_sem=send_sems.at[a * 7 + k], recv_sem=recv_sems.at[a * 7 + k], device_id=to, device_id_type=MESH)

    def local(a):
        return pltpu.make_async_copy(ins[a], slicers[a](outs[a], 4 * x + 2 * y + c), local_sems.at[a])

    def firsts(a):
        return [copy(a, 0, me, sibling, True)] + [copy(a, 1 + j, me, (*chip, c), True) for j, chip in enumerate(chips)]

    def start():
        for a in range(na):
            local(a).start()
            for cp in firsts(a):
                cp.start()

    def forward():
        for a in range(na):
            for j, chip in enumerate(chips):
                copy(a, 1 + j, (*chip, c), me).wait_recv()
                copy(a, 4 + j, (*chip, c), sibling).start()

    def finish():
        for a in range(na):
            copy(a, 0, sibling, me).wait_recv()
            for j, chip in enumerate(chips):
                copy(a, 4 + j, (*chip, 1 - c), me).wait_recv()
        for a in range(na):
            for cp in firsts(a) + [copy(a, 4 + j, (*chip, c), sibling) for j, chip in enumerate(chips)]:
                cp.wait_send()
            local(a).wait()

    return start, forward, finish


def _ag_sems(na):
    return [pltpu.SemaphoreType.DMA((7 * na,)), pltpu.SemaphoreType.DMA((7 * na,)), pltpu.SemaphoreType.DMA((na,))]


def _local_step(x, target, mod, g_mix, g_ffn, g_fin, prm, w_in_shard, shards):
    fulls = [(HEADS, HB, HB), (HEADS, HB, HB), (D, D), (NDEV, FB, D), (DFF, D)]
    slicers = [_blk_heads, _blk_heads, _blk_rows(D // NDEV), _blk_lead, _blk_rows(DFF // NDEV)]
    my_chip = _my_index() >> 1
    own_first = (my_chip ^ jnp.arange(NCHIP, dtype=jnp.int32)).astype(jnp.int32)
    early, late = [0, 1, 2, 4], [3]
    pick = lambda lst, idx: [lst[i] for i in idx]
    proj, h, w_in, (wa, wx, w_out, w_down) = _in_proj(x, mod, g_mix, w_in_shard, own_first, pick(shards, early),
                                                      pick(fulls, early), pick(slicers, early))
    merged, hl, sv, (w_gu,) = _mixer_fwd(proj, prm, wa, wx, pick(shards, late), pick(fulls, late),
                                         pick(slicers, late))
    w_gu = w_gu.reshape(2, 4, FB, D)
    x1, h2 = _out_proj(merged, x, mod, g_ffn, w_out)
    gu, dx2, dx2b, loss, d_gfin = _ffn_fwd(h2, x1, target, mod, g_fin, w_gu, w_down)
    dgu, act, dx1, dx1b, dmg, sums2 = _ffn_bwd(dx2, gu, x1, mod, g_ffn, w_gu, w_down, w_out)
    chip_order = _xor_order(_my_index() >> 1, NCHIP)
    p_wgu = _gu_wgrad(h2, dgu, chip_order)
    p_wdown, d_gt2 = _scaled_wgrad("down_wgrad", act, dx2b, w_down, 5, mod, chip_order)
    p_wout, d_gt1 = _scaled_wgrad("out_wgrad", merged, dx1b, w_out, 2, mod, chip_order)
    dproj, msums, g_wa, g_wx = _mixer_bwd(proj, hl, sv, dmg, prm, wa, wx)
    p_win, p_wa, p_wx = _in_wgrad(h, dproj, g_wa, g_wx, chip_order)
    grad_x, sums1 = _in_proj_bwd(dproj, w_in, x, dx1, mod, g_mix)
    return dict(loss=loss, grad_x=grad_x, d_gfin=d_gfin, sums1=sums1, sums2=sums2, msums=msums,
                d_gt1=d_gt1[0:1], d_gt2=d_gt2[0:1], p_win=p_win, p_wa=p_wa, p_wx=p_wx, p_wout=p_wout, p_wgu=p_wgu,
                p_wdown=p_wdown)


def kernel(x, c, w_ada, b_ada, g_norm_mix, w_in, conv_a_w, conv_b_w, conv_b_bias, w_rg_a, b_rg_a, w_rg_x, b_rg_x, lru_lambda, w_out, g_norm_ffn, w_gate_up, w_down, g_norm_final, loss_target, m_w_ada, m_b_ada, m_g_norm_mix, m_w_in, m_conv_a_w, m_conv_b_w, m_conv_b_bias, m_w_rg_a, m_b_rg_a, m_w_rg_x, m_b_rg_x, m_lru_lambda, m_w_out, m_g_norm_ffn, m_w_gate_up, m_w_down, m_g_norm_final, v_w_ada, v_b_ada, v_g_norm_mix, v_w_in, v_conv_a_w, v_conv_b_w, v_conv_b_bias, v_w_rg_a, v_b_rg_a, v_w_rg_x, v_b_rg_x, v_lru_lambda, v_w_out, v_g_norm_ffn, v_w_gate_up, v_w_down, v_g_norm_final):
    me = 4 * lax.axis_index("x") + 2 * lax.axis_index("y") + lax.axis_index("c")
    ncol = w_ada.shape[2]
    cw = conv_a_w.shape[2]

    pack0 = jnp.concatenate([c, conv_a_w.reshape(1, 3 * cw), conv_b_w.reshape(1, 4 * cw)], axis=1)
    b_cols = lax.dynamic_slice_in_dim(b_ada, me * ncol, ncol, axis=1)
    got0, got1 = _ada_mod(jnp.broadcast_to(pack0, (8, pack0.shape[1])), w_ada[0], b_cols)
    got0 = got0.reshape(NDEV, 8, -1)[:, 0, :]
    c_all = got0[:, :D]
    conv_a = got0[:, D:D + 3 * cw].reshape(NDEV, 3, cw).transpose(1, 0, 2).reshape(3, D)
    conv_b = got0[:, D + 3 * cw:].reshape(NDEV, 4, cw).transpose(1, 0, 2).reshape(4, D)
    c16 = jnp.concatenate([c_all, jnp.zeros((8, D), F32)], axis=0)
    mod6 = lax.dynamic_index_in_dim(got1.reshape(NDEV, NDEV, ncol), me, axis=1, keepdims=False).reshape(6, D)
    mod = jnp.concatenate([mod6, jnp.zeros((2, D), F32)], axis=0)

    tr = lambda a: jnp.swapaxes(a, 1, 2)
    shards = [w_rg_a[0].astype(BF16), w_rg_x[0].astype(BF16), w_out[0].astype(BF16), tr(w_gate_up)[0].astype(BF16),
              w_down[0].astype(BF16)]

    prm = jnp.concatenate([conv_a, conv_b, conv_b_bias, b_rg_a, b_rg_x, lru_lambda, jnp.zeros((5, D), F32)], axis=0)
    r = _local_step(x[0], loss_target[0], mod, g_norm_mix, g_norm_ffn, g_norm_final.reshape(1, D), prm,
                    w_in[0].astype(BF16), shards)

    parts = [r["p_win"], r["p_wa"], r["p_wx"], r["p_wout"], r["p_wgu"], r["p_wdown"]]
    big = {}
    for nm, p, w, m, v in (("w_in", parts[0], w_in, m_w_in, v_w_in), ("w_rg_a", parts[1], w_rg_a, m_w_rg_a, v_w_rg_a),
                           ("w_rg_x", parts[2], w_rg_x, m_w_rg_x, v_w_rg_x), ("w_out", parts[3], w_out, m_w_out, v_w_out),
                           ("w_gate_up", parts[4], tr(w_gate_up), tr(m_w_gate_up), tr(v_w_gate_up)),
                           ("w_down", parts[5], w_down, m_w_down, v_w_down)):
        two_d = (-1, w.shape[-1])
        outs = _adam("adam_" + nm, p.reshape((p.shape[0],) + w.reshape(two_d).shape), w.reshape(two_d), m.reshape(two_d),
                     v.reshape(two_d))
        big[nm] = [o.reshape(w.shape) for o in outs]
    big["w_gate_up"] = [tr(o) for o in big["w_gate_up"]]

    small = jnp.concatenate([
        r["sums1"][S_SH:S_SH + 1], r["sums1"][S_SC:S_SC + 1], r["d_gt1"],
        r["sums2"][S_SH:S_SH + 1], r["sums2"][S_SC:S_SC + 1], r["d_gt2"],
        r["sums1"][S_G:S_G + 1],
        r["msums"][M_CBIAS:M_CBIAS + 1], r["msums"][M_BA:M_BA + 1], r["msums"][M_BX:M_BX + 1],
        r["msums"][M_LS:M_LS + 1],
        r["sums2"][S_G:S_G + 1], r["d_gfin"],
        r["msums"][M_WA:M_WA + 3], r["msums"][M_WB:M_WB + 4],
        jnp.broadcast_to(r["loss"][0:1, 0:1], (1, D)),
        jnp.zeros((3, D), F32)], axis=0)
    got2 = _all_gather_small("gather_small", small).reshape(NDEV, 24, D)

    rep_w = jnp.concatenate([b_ada.reshape(6, D), g_norm_mix, conv_b_bias, b_rg_a, b_rg_x, lru_lambda, g_norm_ffn,
                             g_norm_final.reshape(1, D), jnp.zeros((3, D), F32)], axis=0)
    rep_m = jnp.concatenate([m_b_ada.reshape(6, D), m_g_norm_mix, m_conv_b_bias, m_b_rg_a, m_b_rg_x, m_lru_lambda,
                             m_g_norm_ffn, m_g_norm_final.reshape(1, D), jnp.zeros((3, D), F32)], axis=0)
    rep_v = jnp.concatenate([v_b_ada.reshape(6, D), v_g_norm_mix, v_conv_b_bias, v_b_rg_a, v_b_rg_x, v_lru_lambda,
                             v_g_norm_ffn, v_g_norm_final.reshape(1, D), jnp.ones((3, D), F32)], axis=0)
    rep = _adam("adam_rep", got2[:, :16, :], rep_w, rep_m, rep_v)

    conv_parts = lax.dynamic_slice_in_dim(got2[:, 13:21, :], me * cw, cw, axis=2)
    cv_w = jnp.concatenate([conv_a_w[0], conv_b_w[0], jnp.zeros((1, cw), F32)], axis=0)
    cv_m = jnp.concatenate([m_conv_a_w[0], m_conv_b_w[0], jnp.zeros((1, cw), F32)], axis=0)
    cv_v = jnp.concatenate([v_conv_a_w[0], v_conv_b_w[0], jnp.ones((1, cw), F32)], axis=0)
    cvo = _adam("adam_conv", conv_parts, cv_w, cv_m, cv_v)

    dmod_cols = lax.dynamic_slice_in_dim(got2[:, :6, :].reshape(NDEV, 6 * D), me * ncol, ncol, axis=1)
    dmod16 = jnp.concatenate([dmod_cols, jnp.zeros((8, ncol), F32)], axis=0)
    ada = _ada_bwd(c16, dmod16, w_ada[0], m_w_ada[0], v_w_ada[0])

    loss = jnp.sum(got2[:, 20, 0])

    def pick(q):
        one = lambda i: rep[q][i:i + 1]
        return [ada[q].reshape(w_ada.shape), rep[q][0:6].reshape(b_ada.shape), one(6), big["w_in"][q],
                cvo[q][0:3].reshape(conv_a_w.shape), cvo[q][3:7].reshape(conv_b_w.shape), one(7),
                big["w_rg_a"][q], one(8), big["w_rg_x"][q], one(9), one(10), big["w_out"][q], one(11),
                big["w_gate_up"][q], big["w_down"][q], rep[q][12]]

    return (loss, r["grad_x"].reshape(x.shape), *pick(0), *pick(1), *pick(2), *pick(3))
```

```python
import math

import jax
import jax.numpy as jnp
from jax import lax
from jax.experimental import pallas as pl
from jax.experimental.pallas import tpu as pltpu

F32 = jnp.float32
BF16 = jnp.bfloat16

D = 1024
DFF = 2816
NDEV = 8
HEADS = 4
HB = D // HEADS
FB = DFF // 4
EPS = 1e-6
LRU_C = 8.0
ADAM_LR, ADAM_B1, ADAM_B2, ADAM_EPS, ADAM_WD, ADAM_STEP = 0.001, 0.9, 0.999, 1e-08, 0.01, 10

VMEM_LIMIT = 56 * 1024 * 1024
TM = 512
TMI = 1024
TMF = 256
TK = 2048
TKI = 2048
SUB = 256
TT = 256
TTF = 512
CG = 256
MESH = pl.DeviceIdType.MESH


def _cp(*sem):
    return pltpu.CompilerParams(dimension_semantics=sem, vmem_limit_bytes=VMEM_LIMIT)


def _sig(x):
    return 1.0 / (1.0 + jnp.exp(-x))


def _log_sigmoid(x):
    z = jnp.exp(-jnp.abs(x))
    u = 1.0 + z
    d = u - 1.0
    l1p = jnp.where(d == 0.0, z, jnp.log(u) * (z / jnp.where(d == 0.0, 1.0, d)))
    return -(jnp.maximum(-x, 0.0) + l1p)


def _neg_expm1(x):
    p = x * (1.0 + x * 0.5 * (1.0 + x * (1.0 / 3.0) * (1.0 + x * 0.25 * (1.0 + x * 0.2 * (1.0 + x * (1.0 / 6.0))))))
    return jnp.where(x > -0.25, -p, 1.0 - jnp.exp(x))


_GC = math.sqrt(2.0 / math.pi)


def _gelu(x):
    t = jnp.tanh(_GC * (x + 0.044715 * x * x * x))
    return 0.5 * x * (1.0 + t), t


def _dot(a, b):
    return jnp.dot(a, b, preferred_element_type=F32)


def _dot_nt(a, b):
    return lax.dot_general(a, b, (((1,), (1,)), ((), ())), preferred_element_type=F32)


def _dot_tn(a, b):
    return lax.dot_general(a, b, (((0,), (0,)), ((), ())), preferred_element_type=F32)


def _resident(shape):
    return pl.BlockSpec(shape, lambda *_: (0,) * len(shape), pipeline_mode=pl.Buffered(1))


def _sub_blocks(n_rows):
    step = min(SUB, n_rows)
    return [slice(r, r + step) for r in range(0, n_rows, step)]


def _fold8(v):
    return v[0:8] + v[8:16]


def _pj(ref, s, rows=slice(None), cols=slice(0, D)):
    return ref[rows, s * D + cols.start:s * D + cols.stop]


def _in_proj(x, mod, g_mix, w_shard, order, shards, fulls, slicers):
    t_len = x.shape[0]
    tm = min(TMI, t_len)
    ni = t_len // tm
    na = len(shards)
    cw = 7 * D // NDEV
    rc = 32

    def body(ord_ref, x_ref, mod_ref, g_ref, wsh_ref, *rest):
        ins, (proj_ref, h_ref, wfull_ref), outs = rest[:na], rest[na:na + 3], rest[na + 3:2 * na + 3]
        h_scr, w_scr, wsend, wrecv, wlocal, wout = rest[2 * na + 3:2 * na + 9]
        start, forward, finish = _ag_phases(ins, outs, slicers, *rest[2 * na + 9:])
        p, i = pl.program_id(0), pl.program_id(1)
        x_, y_, c = _my_pos()
        me, sibling = (x_, y_, c), (x_, y_, 1 - c)
        chip_at = [None, (x_, 1 - y_), (1 - x_, y_), (1 - x_, 1 - y_)]

        def cols(px, py, pc):
            return w_scr.at[:, pl.ds(pl.multiple_of((4 * px + 2 * py + pc) * cw, 128), cw)]

        def wcopy(k, block, to, from_shard=False):
            dst = cols(*block)
            return pltpu.make_async_remote_copy(src_ref=wsh_ref if from_shard else dst, dst_ref=dst,
                                                send_sem=wsend.at[k], recv_sem=wrecv.at[k], device_id=to,
                                                device_id_type=MESH)

        own_local = pltpu.make_async_copy(wsh_ref, cols(*me), wlocal)
        to_hbm = pltpu.make_async_copy(w_scr, wfull_ref, wout)

        @pl.when((p == 0) & (i == 0))
        def _():
            own_local.start()
            wcopy(0, me, sibling, True).start()
            for q in (1, 2):
                wcopy(q, me, (*chip_at[q], c), True).start()
            own_local.wait()
            wcopy(0, sibling, me).wait_recv()

        @pl.when((p == 0) & (i == ni // 2))
        def _():
            wcopy(3, me, (*chip_at[3], c), True).start()

        for q in (1, 2, 3):
            @pl.when((p == q - 1) & (i == ni - 1))
            def _():
                wcopy(q, (*chip_at[q], c), me).wait_recv()
                wcopy(3 + q, (*chip_at[q], c), sibling).start()

            @pl.when((p == q) & (i == 0))
            def _():
                wcopy(3 + q, (*chip_at[q], 1 - c), me).wait_recv()

        @pl.when((p == 1) & (i == 0))
        def _():
            start()

        @pl.when((p == NCHIP - 1) & (i == ni // 2))
        def _():
            forward()

        @pl.when((p == NCHIP - 1) & (i == 0))
        def _():
            to_hbm.start()

        gs = g_ref[...] * (1.0 + mod_ref[1:2, :])
        sh = mod_ref[0:1, :]

        wcols = pl.ds(pl.multiple_of(ord_ref[p] * (2 * cw), 128), 2 * cw)
        for sub in _sub_blocks(tm):
            for r0 in range(sub.start, sub.stop, rc):
                xv = x_ref[r0:r0 + rc, :]
                r = lax.rsqrt(jnp.mean(xv * xv, axis=-1, keepdims=True) + EPS)
                h_scr[r0:r0 + rc, :] = (xv * r * gs + sh).astype(BF16)
            proj_ref[sub, :] = _dot(h_scr[sub, :], w_scr[:, wcols]).astype(BF16)

        @pl.when(p == 0)
        def _():
            h_ref[...] = h_scr[...]

        @pl.when((p == NCHIP - 1) & (i == ni - 1))
        def _():
            wcopy(0, me, sibling, True).wait_send()
            for q in (1, 2, 3):
                wcopy(q, me, (*chip_at[q], c), True).wait_send()
                wcopy(3 + q, (*chip_at[q], c), sibling).wait_send()
            finish()
            to_hbm.wait()

    res = pl.pallas_call(
        body, name="in_proj",
        grid_spec=pltpu.PrefetchScalarGridSpec(
            num_scalar_prefetch=1, grid=(NCHIP, ni),
            in_specs=[pl.BlockSpec((tm, D), lambda p, i, o: (i, 0)),
                      pl.BlockSpec((8, D), lambda p, i, o: (0, 0)),
                      pl.BlockSpec((1, D), lambda p, i, o: (0, 0))] + [_ANY] * (1 + na),
            out_specs=[pl.BlockSpec((tm, 2 * cw), lambda p, i, o: (i, o[p])),
                       pl.BlockSpec((tm, D), lambda p, i, o: (jnp.where(p == 0, i, ni - 1), 0))]
            + [_ANY] * (1 + na),
            scratch_shapes=[pltpu.VMEM((tm, D), BF16), pltpu.VMEM((D, 7 * D), BF16),
                            pltpu.SemaphoreType.DMA((7,)), pltpu.SemaphoreType.DMA((7,)),
                            pltpu.SemaphoreType.DMA, pltpu.SemaphoreType.DMA] + _ag_sems(na)),
        out_shape=[jax.ShapeDtypeStruct((t_len, 7 * D), BF16), jax.ShapeDtypeStruct((t_len, D), BF16),
                   jax.ShapeDtypeStruct((D, 7 * D), BF16)]
        + [jax.ShapeDtypeStruct(f, sh.dtype) for f, sh in zip(fulls, shards)],
        compiler_params=_cp("arbitrary", "arbitrary"),
    )(order, x, mod, g_mix, w_shard, *shards)
    return res[0], res[1], res[2], res[3:]


P_WA, P_WB, P_CBIAS, P_BA, P_BX, P_LAM = 0, 3, 7, 8, 9, 10
SV_PLANES = SV_U, SV_YA, SV_R, SV_I, SV_A, SV_MULT = range(6)


def _lru_gates(rp, ip, ls, first_row):
    r = _sig(rp)
    ig = _sig(ip)
    la = LRU_C * r * ls
    a = jnp.exp(la)
    m2 = _neg_expm1(2.0 * la)
    mult = jnp.where(first_row, 1.0, jnp.sqrt(jnp.maximum(m2, 0.0)))
    return r, ig, la, a, m2, mult


def _shift_down(cur, prev, s, row):
    return jnp.where(row >= s, pltpu.roll(cur, s, 0), pltpu.roll(prev, s, 0))


def _shift_up(cur, nxt, s, row):
    return jnp.where(row < 8 - s, pltpu.roll(cur, 8 - s, 0), pltpu.roll(nxt, 8 - s, 0))


def _conv_fwd_rows(tt, proj_ref, prm_ref, xe, ve, u_s, ub_s, ya_s):
    row = lax.broadcasted_iota(jnp.int32, (8, CG), 0)
    w_b = [prm_ref[P_WB + k:P_WB + k + 1, :] for k in range(4)]
    w_a = [prm_ref[P_WA + k:P_WA + k + 1, :] for k in range(3)]
    bias = prm_ref[P_CBIAS:P_CBIAS + 1, :]

    def blk(ib, carry):
        r0 = pl.multiple_of(ib * 16, 16)
        rows = pl.ds(r0, 16)
        for g in range(D // CG):
            cs = slice(g * CG, (g + 1) * CG)
            x16 = _pj(proj_ref, 3, rows, cs).astype(F32)
            v16 = _pj(proj_ref, 1, rows, cs).astype(F32) * _pj(proj_ref, 2, rows, cs).astype(F32)
            xp = xe[pl.ds(r0, 8), cs]
            vp = ve[pl.ds(r0, 8), cs]
            xe[pl.ds(r0 + 8, 16), cs] = x16
            ve[pl.ds(r0 + 8, 16), cs] = v16
            us, yas = [], []
            for sb in range(2):
                xc, vc = x16[8 * sb:8 * sb + 8], v16[8 * sb:8 * sb + 8]
                u8 = bias[:, cs] + w_b[3][:, cs] * xc
                for s in (1, 2, 3):
                    u8 = u8 + w_b[3 - s][:, cs] * _shift_down(xc, xp, s, row)
                y8 = w_a[2][:, cs] * vc
                for s in (1, 2):
                    y8 = y8 + w_a[2 - s][:, cs] * _shift_down(vc, vp, s, row)
                us.append(u8)
                yas.append(y8)
                xp, vp = xc, vc
            u16 = jnp.concatenate(us, axis=0)
            u_s[rows, cs] = u16
            ub_s[rows, cs] = u16.astype(BF16)
            ya_s[rows, cs] = jnp.concatenate(yas, axis=0)
        return carry

    lax.fori_loop(0, tt // 16, blk, 0)


def _mixer_fwd(proj, prm, wa, wx, shards, fulls, slicers):
    t_len = proj.shape[0]
    tt = min(TTF, t_len)
    nt = t_len // tt
    na = len(shards)

    def body(proj_ref, prm_ref, wa_ref, wx_ref, *rest):
        ins, (mg_ref, hl_ref, sv_hbm), outs = rest[:na], rest[na:na + 3], rest[na + 3:2 * na + 3]
        xe, ve, hc, rp_s, ip_s, ub_s, sv_st, sv_sems = rest[2 * na + 3:2 * na + 11]
        start, forward, finish = _ag_phases(ins, outs, slicers, *rest[2 * na + 11:])
        t = pl.program_id(0)

        slot = t % 2
        sv_ref = sv_st.at[slot]

        def sv_out(tile, sl):
            rows = pl.ds(pl.multiple_of(tile * tt, tt), tt)
            return pltpu.make_async_copy(sv_st.at[sl], sv_hbm.at[:, rows, :], sv_sems.at[sl])

        @pl.when(t >= 2)
        def _():
            sv_out(t - 2, slot).wait()

        @pl.when(t == 0)
        def _():
            start()
            xe[0:8, :] = jnp.zeros((8, D), F32)
            ve[0:8, :] = jnp.zeros((8, D), F32)
            hc[...] = jnp.zeros((8, D), F32)

        @pl.when(t == (3 * nt) // 4)
        def _():
            forward()

        _conv_fwd_rows(tt, proj_ref, prm_ref, xe, ve, sv_ref.at[SV_U], ub_s, sv_ref.at[SV_YA])
        xe[0:8, :] = xe[tt:tt + 8, :]
        ve[0:8, :] = ve[tt:tt + 8, :]

        ub = ub_s[...]
        for h in range(HEADS):
            cs = slice(h * HB, (h + 1) * HB)
            rp_s[:, cs] = _dot(ub[:, cs], wa_ref[h]) + prm_ref[P_BA:P_BA + 1, cs]
            ip_s[:, cs] = _dot(ub[:, cs], wx_ref[h]) + prm_ref[P_BX:P_BX + 1, cs]

        ls_all = _log_sigmoid(prm_ref[P_LAM:P_LAM + 1, :])
        row = lax.broadcasted_iota(jnp.int32, (8, CG), 0)

        def blk(i, carry):
            r0 = pl.multiple_of(i * 16, 16)
            for g in range(D // CG):
                cs = slice(g * CG, (g + 1) * CG)
                ls = ls_all[:, cs]
                hprev = hc[:, cs]
                hs = []
                for sb in range(2):
                    rr = r0 + 8 * sb
                    first = (row + (t * tt + rr)) == 0
                    r8 = pl.ds(rr, 8)
                    r, ig, _, a, _, mult = _lru_gates(rp_s[r8, cs], ip_s[r8, cs], ls, first)
                    for plane, val in ((SV_R, r), (SV_I, ig), (SV_A, a), (SV_MULT, mult)):
                        sv_ref[plane, r8, cs] = val
                    b = mult * (ig * sv_ref[SV_U, r8, cs])
                    for s in (1, 2, 4):
                        a_sh = jnp.where(row >= s, pltpu.roll(a, s, 0), 1.0)
                        b_sh = jnp.where(row >= s, pltpu.roll(b, s, 0), 0.0)
                        b = a * b_sh + b
                        a = a * a_sh
                    hv = a * hprev + b
                    hprev = jnp.broadcast_to(hv[7:8, :], hv.shape)
                    hs.append(hv)
                hc[:, cs] = hprev
                h16 = jnp.concatenate(hs, axis=0)
                rows = pl.ds(r0, 16)
                gl, _ = _gelu(_pj(proj_ref, 4, rows, cs).astype(F32))
                y_b = h16 * gl
                y_a = _pj(proj_ref, 0, rows, cs).astype(F32) * sv_ref[SV_YA, rows, cs]
                mg = (_sig(_pj(proj_ref, 5, rows, cs).astype(F32)) * y_a
                      + _sig(_pj(proj_ref, 6, rows, cs).astype(F32)) * y_b)
                mg_ref[rows, cs] = mg.astype(BF16)
                hl_ref[rows, cs] = h16.astype(BF16)
            return carry

        lax.fori_loop(0, tt // 16, blk, 0)
        sv_out(t, slot).start()

        @pl.when(t == nt - 1)
        def _():
            finish()
            if nt >= 2:
                sv_out(t - 1, 1 - slot).wait()
            sv_out(t, slot).wait()

    res = pl.pallas_call(
        body, name="mixer_fwd", grid=(nt,),
        in_specs=[pl.BlockSpec((tt, 7 * D), lambda t: (t, 0)),
                  pl.BlockSpec((16, D), lambda t: (0, 0)),
                  pl.BlockSpec((HEADS, HB, HB), lambda t: (0, 0, 0)),
                  pl.BlockSpec((HEADS, HB, HB), lambda t: (0, 0, 0))] + [_ANY] * na,
        out_specs=[pl.BlockSpec((tt, D), lambda t: (t, 0)), pl.BlockSpec((tt, D), lambda t: (t, 0)), _ANY]
        + [_ANY] * na,
        out_shape=[jax.ShapeDtypeStruct((t_len, D), BF16), jax.ShapeDtypeStruct((t_len, D), BF16),
                   jax.ShapeDtypeStruct((len(SV_PLANES), t_len, D), F32)]
        + [jax.ShapeDtypeStruct(f, sh.dtype) for f, sh in zip(fulls, shards)],
        scratch_shapes=[pltpu.VMEM((tt + 8, D), F32), pltpu.VMEM((tt + 8, D), F32), pltpu.VMEM((8, D), F32),
                        pltpu.VMEM((tt, D), F32), pltpu.VMEM((tt, D), F32), pltpu.VMEM((tt, D), BF16),
                        pltpu.VMEM((2, len(SV_PLANES), tt, D), F32), pltpu.SemaphoreType.DMA((2,))]
        + _ag_sems(na),
        compiler_params=_cp("arbitrary"),
    )(proj, prm, wa, wx, *shards)
    return res[0], res[1], res[2], res[3:]


def _out_proj(merged, x, mod, g_ffn, w_out):
    t_len = x.shape[0]
    tm = min(TM, t_len)

    def body(mg_ref, x_ref, mod_ref, g_ref, w_ref, x1_ref, h2_ref):
        gt1 = mod_ref[2:3, :]
        gs = g_ref[...] * (1.0 + mod_ref[4:5, :])
        sh = mod_ref[3:4, :]
        for sub in _sub_blocks(tm):
            x1_ref[sub, :] = x_ref[sub, :] + gt1 * _dot(mg_ref[sub, :], w_ref[...])
            for r0 in range(sub.start, sub.stop, 16):
                x1 = x1_ref[r0:r0 + 16, :]
                r = lax.rsqrt(jnp.mean(x1 * x1, axis=-1, keepdims=True) + EPS)
                h2_ref[r0:r0 + 16, :] = (x1 * r * gs + sh).astype(BF16)

    return pl.pallas_call(
        body, name="out_proj", grid=(t_len // tm,),
        in_specs=[pl.BlockSpec((tm, D), lambda i: (i, 0)), pl.BlockSpec((tm, D), lambda i: (i, 0)),
                  pl.BlockSpec((8, D), lambda i: (0, 0)), pl.BlockSpec((1, D), lambda i: (0, 0)),
                  pl.BlockSpec((D, D), lambda i: (0, 0))],
        out_specs=[pl.BlockSpec((tm, D), lambda i: (i, 0)), pl.BlockSpec((tm, D), lambda i: (i, 0))],
        out_shape=[jax.ShapeDtypeStruct((t_len, D), F32), jax.ShapeDtypeStruct((t_len, D), BF16)],
        compiler_params=_cp("parallel"),
    )(merged, x, mod, g_ffn, w_out)


def _ffn_fwd(h2, x1, target, mod, g_fin, w_gu, w_down):
    t_len = x1.shape[0]
    tm = min(TMF, t_len)

    def body(h2_ref, x1_ref, tg_ref, mod_ref, g_ref, wgu_ref, wd_ref, gu_ref, dx2_ref, dx2b_ref, loss_ref, dg_ref, acc):
        @pl.when(pl.program_id(0) == 0)
        def _():
            loss_ref[...] = jnp.zeros_like(loss_ref)
            dg_ref[...] = jnp.zeros_like(dg_ref)

        hb = h2_ref[...]
        ffn = None
        nxt = (_dot_nt(hb, wgu_ref[0, 0]), _dot_nt(hb, wgu_ref[1, 0]))
        for j in range(4):
            gate, up = nxt
            if j < 3:
                nxt = (_dot_nt(hb, wgu_ref[0, j + 1]), _dot_nt(hb, wgu_ref[1, j + 1]))
            gu_ref[0, j] = gate.astype(BF16)
            gu_ref[1, j] = up.astype(BF16)
            act = (gate * _sig(gate) * up).astype(BF16)
            part = _dot(act, wd_ref[j * FB:(j + 1) * FB, :])
            ffn = part if ffn is None else ffn + part
        acc[...] = ffn

        gt2 = mod_ref[5:6, :]
        gf = g_ref[...]

        s_loss = s_dg = jnp.zeros((8, D), F32)
        for r0 in range(0, tm, 16):
            rows = slice(r0, r0 + 16)
            x2 = x1_ref[rows, :] + gt2 * acc[rows, :]
            r = lax.rsqrt(jnp.mean(x2 * x2, axis=-1, keepdims=True) + EPS)
            xn = x2 * r
            diff = xn * gf - tg_ref[rows, :]
            dy = diff * (1.0 / D)
            dxn = dy * gf
            dx2 = r * (dxn - xn * jnp.mean(dxn * xn, axis=-1, keepdims=True))
            dx2_ref[rows, :] = dx2
            dx2b_ref[rows, :] = dx2.astype(BF16)
            s_loss, s_dg = s_loss + _fold8(diff * diff), s_dg + _fold8(dy * xn)
        loss_ref[...] += jnp.sum(s_loss) * (0.5 / D)
        dg_ref[...] += jnp.sum(s_dg, axis=0, keepdims=True)

    row = pl.BlockSpec((tm, D), lambda i: (i, 0))
    return pl.pallas_call(
        body, name="ffn_fwd", grid=(t_len // tm,),
        in_specs=[row, row, row, pl.BlockSpec((8, D), lambda i: (0, 0)), pl.BlockSpec((1, D), lambda i: (0, 0)),
                  _resident((2, 4, FB, D)), _resident((DFF, D))],
        out_specs=[pl.BlockSpec((2, 4, tm, FB), lambda i: (0, 0, i, 0)), row, row,
                   pl.BlockSpec((1, 128), lambda i: (0, 0)), pl.BlockSpec((1, D), lambda i: (0, 0))],
        out_shape=[jax.ShapeDtypeStruct((2, 4, t_len, FB), BF16), jax.ShapeDtypeStruct((t_len, D), F32),
                   jax.ShapeDtypeStruct((t_len, D), BF16),
                   jax.ShapeDtypeStruct((1, 128), F32), jax.ShapeDtypeStruct((1, D), F32)],
        scratch_shapes=[pltpu.VMEM((tm, D), F32)],
        compiler_params=_cp("arbitrary"),
    )(h2, x1, target, mod, g_fin, w_gu, w_down)


S_SH, S_SC, S_G = 0, 1, 2


def _norm_bwd_rows(span, sums, dh_ref, x_ref, dres_ref, scale, gain, write):
    gs = 1.0 + scale
    s_sh, s_sc, s_g = sums
    for r0 in range(span.start, span.stop, 16):
        rows = slice(r0, r0 + 16)
        dh = dh_ref[rows, :]
        xv = x_ref[rows, :]
        r = lax.rsqrt(jnp.mean(xv * xv, axis=-1, keepdims=True) + EPS)
        xn = xv * r
        dhn = dh * gs
        dxn = dhn * gain
        write(rows, dres_ref[rows, :] + r * (dxn - xn * jnp.mean(dxn * xn, axis=-1, keepdims=True)))
        s_sh, s_sc, s_g = s_sh + _fold8(dh), s_sc + _fold8(dh * (xn * gain)), s_g + _fold8(dhn * xn)
    return s_sh, s_sc, s_g


def _add_norm_sums(sums_ref, sums):
    for dst, s in zip((S_SH, S_SC, S_G), sums):
        sums_ref[dst:dst + 1, :] += jnp.sum(s, axis=0, keepdims=True)


def _ffn_bwd(dx2, gu, x1, mod, g_ffn, w_gu, w_down, w_out):
    t_len = x1.shape[0]
    tm = min(TMF, t_len)

    def body(dx2_ref, gu_ref, x1_ref, mod_ref, g_ref, wgu_ref, wd_ref, wo_ref,
             dgu_ref, act_ref, dx1_ref, dx1b_ref, dmg_ref, sums_ref, acc, dmo, dact_s):
        @pl.when(pl.program_id(0) == 0)
        def _():
            sums_ref[...] = jnp.zeros_like(sums_ref)

        dffn = (dx2_ref[...] * mod_ref[5:6, :]).astype(BF16)
        dact_s[0] = _dot_nt(dffn, wd_ref[0:FB, :])
        for j in range(4):
            if j < 3:
                dact_s[(j + 1) % 2] = _dot_nt(dffn, wd_ref[(j + 1) * FB:(j + 2) * FB, :])
            for r0 in range(0, tm, 16):
                rows = slice(r0, r0 + 16)
                dact = dact_s[j % 2, rows, :]
                gate = gu_ref[0, j, rows, :].astype(F32)
                up = gu_ref[1, j, rows, :].astype(F32)
                sg = _sig(gate)
                silu = gate * sg
                act_ref[j, rows, :] = (silu * up).astype(BF16)
                dgu_ref[0, j, rows, :] = (dact * up * (sg * (1.0 + gate * (1.0 - sg)))).astype(BF16)
                dgu_ref[1, j, rows, :] = (dact * silu).astype(BF16)
            part = _dot(dgu_ref[0, j], wgu_ref[0, j]) + _dot(dgu_ref[1, j], wgu_ref[1, j])
            if j == 0:
                acc[...] = part
            else:
                acc[...] += part

        gt1 = mod_ref[2:3, :]

        def write(rows, dx1):
            dx1_ref[rows, :] = dx1
            dx1b_ref[rows, :] = dx1.astype(BF16)
            dmo[rows, :] = (dx1 * gt1).astype(BF16)

        zero = jnp.zeros((8, D), F32)
        sums = (zero, zero, zero)
        for sub in (slice(0, tm // 2), slice(tm // 2, tm)):
            sums = _norm_bwd_rows(sub, sums, acc, x1_ref, dx2_ref, mod_ref[4:5, :], g_ref[...], write)
            dmg_ref[sub, :] = _dot_nt(dmo[sub, :], wo_ref[...]).astype(BF16)
        _add_norm_sums(sums_ref, sums)

    row = pl.BlockSpec((tm, D), lambda i: (i, 0))
    return pl.pallas_call(
        body, name="ffn_bwd", grid=(t_len // tm,),
        in_specs=[row, pl.BlockSpec((2, 4, tm, FB), lambda i: (0, 0, i, 0)), row,
                  pl.BlockSpec((8, D), lambda i: (0, 0)), pl.BlockSpec((1, D), lambda i: (0, 0)),
                  _resident((2, 4, FB, D)), _resident((DFF, D)), _resident((D, D))],
        out_specs=[pl.BlockSpec((2, 4, tm, FB), lambda i: (0, 0, i, 0)),
                   pl.BlockSpec((4, tm, FB), lambda i: (0, i, 0)), row, row, row,
                   pl.BlockSpec((8, D), lambda i: (0, 0))],
        out_shape=[jax.ShapeDtypeStruct((2, 4, t_len, FB), BF16), jax.ShapeDtypeStruct((4, t_len, FB), BF16),
                   jax.ShapeDtypeStruct((t_len, D), F32), jax.ShapeDtypeStruct((t_len, D), BF16),
                   jax.ShapeDtypeStruct((t_len, D), BF16), jax.ShapeDtypeStruct((8, D), F32)],
        scratch_shapes=[pltpu.VMEM((tm, D), F32), pltpu.VMEM((tm, D), BF16), pltpu.VMEM((2, tm, FB), F32)],
        compiler_params=_cp("arbitrary"),
    )(dx2, gu, x1, mod, g_ffn, w_gu, w_down, w_out)


def _my_pos():
    return lax.axis_index("x"), lax.axis_index("y"), lax.axis_index("c")


def _my_index():
    x, y, c = _my_pos()
    return 4 * x + 2 * y + c


def _device_of(b):
    return (b >> 2) & 1, (b >> 1) & 1, b & 1


def _rs_send(src, parts_ref, b, send_sems, recv_sems, local_sem):
    me = _my_index()
    dst = parts_ref.at[me]

    @pl.when(b == me)
    def _():
        pltpu.make_async_copy(src, dst, local_sem).start()

    @pl.when(b != me)
    def _():
        pltpu.make_async_remote_copy(src_ref=src, dst_ref=dst, send_sem=send_sems.at[b], recv_sem=recv_sems.at[me],
                                     device_id=_device_of(b), device_id_type=MESH).start()


def _rs_finish(src_of, parts_ref, send_sems, recv_sems, local_sem):
    me = _my_index()
    for s in range(NDEV):
        @pl.when(s != me)
        def _():
            cp = pltpu.make_async_remote_copy(src_ref=src_of(s), dst_ref=parts_ref.at[s], send_sem=send_sems.at[s],
                                              recv_sem=recv_sems.at[s], device_id=_device_of(s), device_id_type=MESH)
            cp.wait_send()
            cp.wait_recv()

        @pl.when(s == me)
        def _():
            pltpu.make_async_copy(src_of(s), parts_ref.at[s], local_sem).wait()


_RS_SEMS = [pltpu.SemaphoreType.DMA((NDEV,)), pltpu.SemaphoreType.DMA((NDEV,)), pltpu.SemaphoreType.DMA]
_ANY = pl.BlockSpec(memory_space=pl.ANY)


def _xor_order(me, n):
    return (me ^ (n - 1 - jnp.arange(n, dtype=jnp.int32))).astype(jnp.int32)


NCHIP = NDEV // 2


def _rs2_scratch(half_shape):
    blocks = lambda *lead: pltpu.VMEM(lead + tuple(half_shape), BF16)
    return [blocks(NCHIP, 2), blocks(NCHIP)] + [pltpu.SemaphoreType.DMA((NCHIP,))] * 4 + [pltpu.SemaphoreType.DMA]


def _rs2_to_sibling(q, rs):
    stage, from_sib, d_send, d_recv = rs[:4]
    x, y, c = _my_pos()
    pltpu.make_async_remote_copy(src_ref=stage.at[q, 1 - c], dst_ref=from_sib.at[q], send_sem=d_send.at[q],
                                 recv_sem=d_recv.at[q], device_id=(x, y, 1 - c), device_id_type=MESH).start()


def _rs2_forward(q, parts_ref, rs):
    stage, chip_sum, d_send, d_recv, i_send, i_recv, local_sem = rs
    x, y, c = _my_pos()
    my_chip = 2 * x + y
    pltpu.make_async_remote_copy(src_ref=stage.at[q, c], dst_ref=chip_sum.at[q], send_sem=d_send.at[q],
                                 recv_sem=d_recv.at[q], device_id=(x, y, 1 - c), device_id_type=MESH).wait_recv()
    chip_sum[q] = (stage[q, c].astype(F32) + chip_sum[q].astype(F32)).astype(BF16)

    @pl.when(q == my_chip)
    def _():
        pltpu.make_async_copy(chip_sum.at[q], parts_ref.at[my_chip], local_sem).start()

    @pl.when(q != my_chip)
    def _():
        pltpu.make_async_remote_copy(src_ref=chip_sum.at[q], dst_ref=parts_ref.at[my_chip], send_sem=i_send.at[q],
                                     recv_sem=i_recv.at[my_chip], device_id=((q >> 1) & 1, q & 1, c),
                                     device_id_type=MESH).start()


def _rs2_finish(parts_ref, rs):
    stage, chip_sum, d_send, d_recv, i_send, i_recv, local_sem = rs
    x, y, c = _my_pos()
    my_chip = 2 * x + y
    for q in range(NCHIP):
        pltpu.make_async_remote_copy(src_ref=stage.at[q, 1 - c], dst_ref=chip_sum.at[q], send_sem=d_send.at[q],
                                     recv_sem=d_recv.at[q], device_id=(x, y, 1 - c), device_id_type=MESH).wait_send()

        @pl.when(q != my_chip)
        def _():
            cp = pltpu.make_async_remote_copy(src_ref=chip_sum.at[q], dst_ref=parts_ref.at[q], send_sem=i_send.at[q],
                                              recv_sem=i_recv.at[q], device_id=((q >> 1) & 1, q & 1, c),
                                              device_id_type=MESH)
            cp.wait_send()
            cp.wait_recv()

        @pl.when(q == my_chip)
        def _():
            pltpu.make_async_copy(chip_sum.at[q], parts_ref.at[q], local_sem).wait()


def _gu_wgrad(h2, dgu, order):
    t_len = h2.shape[0]
    tk = min(TK, t_len)
    nk = t_len // tk

    def body(ord_ref, h_ref, d_ref, parts_ref, acc, *rs):
        p, k = pl.program_id(0), pl.program_id(1)

        @pl.when(k == 0)
        def _():
            acc[...] = jnp.zeros_like(acc)

        hb = h_ref[...]
        for half in range(2):
            acc[half] += _dot_tn(d_ref[0, half], hb)

        @pl.when(k == nk - 1)
        def _():
            q = ord_ref[p]
            rs[0][q] = acc[...].astype(BF16)
            _rs2_to_sibling(q, rs)

        @pl.when((k == min(1, nk - 1)) & (p > 0))
        def _():
            _rs2_forward(ord_ref[p - 1], parts_ref, rs)

        @pl.when((p == NCHIP - 1) & (k == nk - 1))
        def _():
            _rs2_forward(ord_ref[p], parts_ref, rs)
            _rs2_finish(parts_ref, rs)

    return pl.pallas_call(
        body, name="gu_wgrad",
        grid_spec=pltpu.PrefetchScalarGridSpec(
            num_scalar_prefetch=1, grid=(NCHIP, nk),
            in_specs=[pl.BlockSpec((tk, D), lambda p, k, o: (k, 0)),
                      pl.BlockSpec((1, 2, tk, FB), lambda p, k, o: (o[p], 0, k, 0))],
            out_specs=_ANY,
            scratch_shapes=[pltpu.VMEM((2, FB, D), F32)] + _rs2_scratch((FB, D))),
        out_shape=jax.ShapeDtypeStruct((NCHIP, FB, D), BF16),
        compiler_params=_cp("arbitrary", "arbitrary"),
    )(order, h2, dgu.reshape(NCHIP, 2, t_len, FB))


def _scaled_wgrad(name, a, dx, w, gate_row, mod, order):
    t_len = dx.shape[0]
    kb = w.shape[0] // NCHIP
    tk = min(TK, t_len)
    nk = t_len // tk
    rows = kb // 2
    if a.ndim == 3:
        a_spec = pl.BlockSpec((None, tk, kb), lambda p, k, o: (o[p], k, 0))
    else:
        a_spec = pl.BlockSpec((tk, kb), lambda p, k, o: (k, o[p]))

    def body(ord_ref, a_ref, dx_ref, w_ref, mod_ref, parts_ref, dg_ref, acc, *rs):
        p, k = pl.program_id(0), pl.program_id(1)

        @pl.when((p == 0) & (k == 0))
        def _():
            dg_ref[...] = jnp.zeros_like(dg_ref)

        @pl.when(k == 0)
        def _():
            acc[...] = jnp.zeros_like(acc)

        acc[...] += _dot_tn(a_ref[...], dx_ref[...])

        @pl.when(k == nk - 1)
        def _():
            q = ord_ref[p]
            z = acc[...]
            zg = (z * mod_ref[gate_row:gate_row + 1, :]).astype(BF16)
            dg_ref[0:1, :] += jnp.sum(z * w_ref[...].astype(F32), axis=0, keepdims=True)
            for half in range(2):
                rs[0][q, half] = zg[half * rows:(half + 1) * rows]
            _rs2_to_sibling(q, rs)

        @pl.when((k == min(1, nk - 1)) & (p > 0))
        def _():
            _rs2_forward(ord_ref[p - 1], parts_ref, rs)

        @pl.when((p == NCHIP - 1) & (k == nk - 1))
        def _():
            _rs2_forward(ord_ref[p], parts_ref, rs)
            _rs2_finish(parts_ref, rs)

    return pl.pallas_call(
        body, name=name,
        grid_spec=pltpu.PrefetchScalarGridSpec(
            num_scalar_prefetch=1, grid=(NCHIP, nk),
            in_specs=[a_spec,
                      pl.BlockSpec((tk, D), lambda p, k, o: (k, 0)),
                      pl.BlockSpec((kb, D), lambda p, k, o: (o[p], 0)),
                      pl.BlockSpec((8, D), lambda p, k, o: (0, 0))],
            out_specs=[_ANY, pl.BlockSpec((8, D), lambda p, k, o: (0, 0))],
            scratch_shapes=[pltpu.VMEM((kb, D), F32)] + _rs2_scratch((rows, D))),
        out_shape=[jax.ShapeDtypeStruct((NCHIP, rows, D), BF16), jax.ShapeDtypeStruct((8, D), F32)],
        compiler_params=_cp("arbitrary", "arbitrary"),
    )(order, a, dx, w, mod)


M_WA, M_WB, M_CBIAS, M_BA, M_BX, M_LS = 0, 3, 7, 8, 9, 10


def _conv_bwd_rows(tt, proj_ref, prm_ref, xe, ve, due, dye, dp_ref, acc8):
    row = lax.broadcasted_iota(jnp.int32, (8, CG), 0)
    w_b = [prm_ref[P_WB + k:P_WB + k + 1, :] for k in range(4)]
    w_a = [prm_ref[P_WA + k:P_WA + k + 1, :] for k in range(3)]

    def blk(ib, carry):
        r0 = pl.multiple_of(ib * 16, 16)
        rows = pl.ds(r0, 16)
        for g in range(D // CG):
            cs = slice(g * CG, (g + 1) * CG)
            du16, du_after = due[rows, cs], due[pl.ds(r0 + 16, 8), cs]
            dy16, dy_after = dye[rows, cs], dye[pl.ds(r0 + 16, 8), cs]
            cc16 = _pj(proj_ref, 1, rows, cs).astype(F32)
            cx16 = _pj(proj_ref, 2, rows, cs).astype(F32)
            x16 = _pj(proj_ref, 3, rows, cs).astype(F32)
            v16 = cc16 * cx16
            xp, vp = xe[pl.ds(r0, 8), cs], ve[pl.ds(r0, 8), cs]
            xe[pl.ds(r0 + 16, 8), cs] = x16[8:16]
            ve[pl.ds(r0 + 16, 8), cs] = v16[8:16]
            acc = [acc8[8 * k:8 * k + 8, cs] for k in range(8)]
            drx, dv = [], []
            for sb in range(2):
                lo = slice(8 * sb, 8 * sb + 8)
                duc, dyc, xc, vc = du16[lo], dy16[lo], x16[lo], v16[lo]
                du_n = du16[8:16] if sb == 0 else du_after
                dy_n = dy16[8:16] if sb == 0 else dy_after
                acc[0] = acc[0] + duc
                acc[4] = acc[4] + duc * xc
                d8 = w_b[3][:, cs] * duc
                for s in (1, 2, 3):
                    acc[4 - s] = acc[4 - s] + duc * _shift_down(xc, xp, s, row)
                    d8 = d8 + w_b[3 - s][:, cs] * _shift_up(duc, du_n, s, row)
                acc[7] = acc[7] + dyc * vc
                e8 = w_a[2][:, cs] * dyc
                for s in (1, 2):
                    acc[7 - s] = acc[7 - s] + dyc * _shift_down(vc, vp, s, row)
                    e8 = e8 + w_a[2 - s][:, cs] * _shift_up(dyc, dy_n, s, row)
                drx.append(d8)
                dv.append(e8)
                xp, vp = xc, vc
            for k in range(8):
                acc8[8 * k:8 * k + 8, cs] = acc[k]
            dv16 = jnp.concatenate(dv, axis=0)
            col = lambda s: slice(s * D + g * CG, s * D + (g + 1) * CG)
            dp_ref[rows, col(3)] = jnp.concatenate(drx, axis=0).astype(BF16)
            dp_ref[rows, col(1)] = (dv16 * cx16).astype(BF16)
            dp_ref[rows, col(2)] = (dv16 * cc16).astype(BF16)
        return carry

    lax.fori_loop(0, tt // 16, blk, 0)


def _mixer_bwd(proj, hl, sv, dmg, prm, wa, wx):
    t_len = proj.shape[0]
    tt = min(TT, t_len)
    nt = t_len // tt
    hb8 = tt // 8

    def rev(i):
        return nt - 1 - i

    def halo(i):
        return jnp.maximum(rev(i) * hb8 - 1, 0)

    def body(proj_ref, ph_ref, hl_ref, hh_ref, sv_ref, dmg_ref, prm_ref, wa_ref, wx_ref,
             dp_ref, sums_ref, gwa_ref, gwx_ref,
             xe, ve, he, due, dye, drp_s, dip_s, an, gn, acc8):
        i = pl.program_id(0)
        t = rev(i)

        @pl.when(i == 0)
        def _():
            sums_ref[...] = jnp.zeros_like(sums_ref)
            gwa_ref[...] = jnp.zeros_like(gwa_ref)
            gwx_ref[...] = jnp.zeros_like(gwx_ref)
            due[tt:tt + 8, :] = jnp.zeros((8, D), F32)
            dye[tt:tt + 8, :] = jnp.zeros((8, D), F32)
            an[...] = jnp.zeros((8, D), F32)
            gn[...] = jnp.zeros((8, D), F32)

        live = (t > 0).astype(F32)
        xe[0:8, :] = _pj(ph_ref, 3).astype(F32) * live
        ve[0:8, :] = _pj(ph_ref, 1).astype(F32) * _pj(ph_ref, 2).astype(F32) * live
        he[0:8, :] = hh_ref[...].astype(F32) * live
        he[8:8 + tt, :] = hl_ref[...].astype(F32)

        ls_all = _log_sigmoid(prm_ref[P_LAM:P_LAM + 1, :])
        row = lax.broadcasted_iota(jnp.int32, (8, CG), 0)
        nblk = tt // 16

        def blk(ib, carry):
            r0 = pl.multiple_of((nblk - 1 - ib) * 16, 16)
            rows = pl.ds(r0, 16)
            for g in range(D // CG):
                cs = slice(g * CG, (g + 1) * CG)
                ls = ls_all[:, cs]
                dm = dmg_ref[rows, cs].astype(F32)
                cb = _pj(proj_ref, 0, rows, cs).astype(F32)
                rg = _pj(proj_ref, 4, rows, cs).astype(F32)
                sga = _sig(_pj(proj_ref, 5, rows, cs).astype(F32))
                sgb = _sig(_pj(proj_ref, 6, rows, cs).astype(F32))
                ya0 = sv_ref[SV_YA, rows, cs]
                h16 = he[pl.ds(r0 + 8, 16), cs]
                gl, th = _gelu(rg)
                dgl = 0.5 * (1.0 + th) + 0.5 * rg * (1.0 - th * th) * (_GC * (1.0 + 3.0 * 0.044715 * rg * rg))
                y_a = cb * ya0
                y_b = h16 * gl
                dy_a = dm * sga
                dy_b = dm * sgb
                col = lambda s: slice(s * D + g * CG, s * D + (g + 1) * CG)
                dp_ref[rows, col(5)] = (dm * y_a * sga * (1.0 - sga)).astype(BF16)
                dp_ref[rows, col(6)] = (dm * y_b * sgb * (1.0 - sgb)).astype(BF16)
                dp_ref[rows, col(4)] = (dy_b * h16 * dgl).astype(BF16)
                dp_ref[rows, col(0)] = (dy_a * ya0).astype(BF16)
                dye[rows, cs] = dy_a * cb
                dh16 = dy_b * gl

                a_next = an[:, cs]
                g_next = gn[:, cs]
                s_ba = jnp.zeros((8, CG), F32)
                s_bx = jnp.zeros((8, CG), F32)
                s_ls = jnp.zeros((8, CG), F32)
                for sb in (1, 0):
                    rr = r0 + 8 * sb
                    first = (row + (t * tt + rr)) == 0
                    r8 = pl.ds(rr, 8)
                    uu, r, ig, a, mult = (sv_ref[pln, r8, cs] for pln in (SV_U, SV_R, SV_I, SV_A, SV_MULT))
                    ca = jnp.where(row < 7, pltpu.roll(a, 7, 0), a_next)
                    cb_ = dh16[8 * sb:8 * sb + 8, :]
                    for s in (1, 2, 4):
                        a_sh = jnp.where(row < 8 - s, pltpu.roll(ca, 8 - s, 0), 1.0)
                        b_sh = jnp.where(row < 8 - s, pltpu.roll(cb_, 8 - s, 0), 0.0)
                        cb_ = ca * b_sh + cb_
                        ca = ca * a_sh
                    gv = ca * g_next + cb_
                    g_next = jnp.broadcast_to(gv[0:1, :], gv.shape)
                    a_next = jnp.broadcast_to(a[0:1, :], a.shape)
                    hprev = jnp.where(row >= 1, pltpu.roll(he[pl.ds(rr + 8, 8), cs], 1, 0),
                                      pltpu.roll(he[pl.ds(rr, 8), cs], 1, 0))
                    da = gv * hprev
                    dmult = jnp.where(first, 0.0, gv * ig * uu)
                    dla = da * a + jnp.where(mult > 0.0, dmult * (-(a * a) / mult), 0.0)
                    drp = dla * (LRU_C * ls) * r * (1.0 - r)
                    dip = gv * mult * uu * ig * (1.0 - ig)
                    s_ls = s_ls + dla * (LRU_C * r)
                    s_ba = s_ba + drp
                    s_bx = s_bx + dip
                    drp_s[pl.ds(rr, 8), cs] = drp
                    dip_s[pl.ds(rr, 8), cs] = dip
                    due[pl.ds(rr, 8), cs] = gv * mult * ig
                an[:, cs] = a_next
                gn[:, cs] = g_next
                sums_ref[M_BA:M_BA + 1, cs] += jnp.sum(s_ba, axis=0, keepdims=True)
                sums_ref[M_BX:M_BX + 1, cs] += jnp.sum(s_bx, axis=0, keepdims=True)
                sums_ref[M_LS:M_LS + 1, cs] += jnp.sum(s_ls, axis=0, keepdims=True)
            return carry

        lax.fori_loop(0, nblk, blk, 0)

        drp_b = drp_s[...].astype(BF16)
        dip_b = dip_s[...].astype(BF16)
        ub = sv_ref[SV_U].astype(BF16)
        for h in range(HEADS):
            cs = slice(h * HB, (h + 1) * HB)
            due[0:tt, cs] += _dot_nt(drp_b[:, cs], wa_ref[h]) + _dot_nt(dip_b[:, cs], wx_ref[h])
            gwa_ref[h] += _dot_tn(ub[:, cs], drp_b[:, cs])
            gwx_ref[h] += _dot_tn(ub[:, cs], dip_b[:, cs])

        acc8[...] = jnp.zeros_like(acc8)
        _conv_bwd_rows(tt, proj_ref, prm_ref, xe, ve, due, dye, dp_ref, acc8)
        for k, dst in enumerate([M_CBIAS] + [M_WB + k for k in range(4)] + [M_WA + k for k in range(3)]):
            sums_ref[dst:dst + 1, :] += jnp.sum(acc8[8 * k:8 * k + 8, :], axis=0, keepdims=True)
        due[tt:tt + 8, :] = due[0:8, :]
        dye[tt:tt + 8, :] = dye[0:8, :]

        @pl.when(i == nt - 1)
        def _():
            sums_ref[M_LS:M_LS + 1, :] = sums_ref[M_LS:M_LS + 1, :] * _sig(-prm_ref[P_LAM:P_LAM + 1, :])

    big = lambda: pltpu.VMEM((tt + 8, D), F32)
    tile = lambda: pltpu.VMEM((tt, D), F32)
    return pl.pallas_call(
        body, name="mixer_bwd", grid=(nt,),
        in_specs=[pl.BlockSpec((tt, 7 * D), lambda i: (rev(i), 0)),
                  pl.BlockSpec((8, 7 * D), lambda i: (halo(i), 0)),
                  pl.BlockSpec((tt, D), lambda i: (rev(i), 0)),
                  pl.BlockSpec((8, D), lambda i: (halo(i), 0)),
                  pl.BlockSpec((len(SV_PLANES), tt, D), lambda i: (0, rev(i), 0)),
                  pl.BlockSpec((tt, D), lambda i: (rev(i), 0)),
                  pl.BlockSpec((16, D), lambda i: (0, 0)),
                  pl.BlockSpec((HEADS, HB, HB), lambda i: (0, 0, 0)),
                  pl.BlockSpec((HEADS, HB, HB), lambda i: (0, 0, 0))],
        out_specs=[pl.BlockSpec((tt, 7 * D), lambda i: (rev(i), 0)),
                   pl.BlockSpec((16, D), lambda i: (0, 0)),
                   pl.BlockSpec((HEADS, HB, HB), lambda i: (0, 0, 0)),
                   pl.BlockSpec((HEADS, HB, HB), lambda i: (0, 0, 0))],
        out_shape=[jax.ShapeDtypeStruct((t_len, 7 * D), BF16), jax.ShapeDtypeStruct((16, D), F32),
                   jax.ShapeDtypeStruct((HEADS, HB, HB), F32), jax.ShapeDtypeStruct((HEADS, HB, HB), F32)],
        scratch_shapes=[big(), big(), big(), big(), big(), tile(), tile(),
                        pltpu.VMEM((8, D), F32), pltpu.VMEM((8, D), F32), pltpu.VMEM((64, D), F32)],
        compiler_params=_cp("arbitrary"),
    )(proj, proj, hl, hl, sv, dmg, prm, wa, wx)


def _in_proj_bwd(dproj, w_in, x, dx1, mod, g_mix):
    t_len = x.shape[0]
    tm = min(TM, t_len)

    def body(dp_ref, w_ref, x_ref, dx1_ref, mod_ref, g_ref, gx_ref, sums_ref, acc):
        @pl.when(pl.program_id(0) == 0)
        def _():
            sums_ref[...] = jnp.zeros_like(sums_ref)

        def write(rows, dx):
            gx_ref[rows, :] = dx

        zero = jnp.zeros((8, D), F32)
        sums = (zero, zero, zero)
        for sub in _sub_blocks(tm):
            acc[sub, :] = _dot_nt(dp_ref[sub, :], w_ref[...])
            sums = _norm_bwd_rows(sub, sums, acc, x_ref, dx1_ref, mod_ref[1:2, :], g_ref[...], write)
        _add_norm_sums(sums_ref, sums)

    return pl.pallas_call(
        body, name="in_proj_bwd", grid=(t_len // tm,),
        in_specs=[pl.BlockSpec((tm, 7 * D), lambda i: (i, 0)),
                  _resident((D, 7 * D)),
                  pl.BlockSpec((tm, D), lambda i: (i, 0)), pl.BlockSpec((tm, D), lambda i: (i, 0)),
                  pl.BlockSpec((8, D), lambda i: (0, 0)), pl.BlockSpec((1, D), lambda i: (0, 0))],
        out_specs=[pl.BlockSpec((tm, D), lambda i: (i, 0)), pl.BlockSpec((8, D), lambda i: (0, 0))],
        out_shape=[jax.ShapeDtypeStruct((t_len, D), F32), jax.ShapeDtypeStruct((8, D), F32)],
        scratch_shapes=[pltpu.VMEM((tm, D), F32)],
        compiler_params=_cp("arbitrary"),
    )(dproj, w_in, x, dx1, mod, g_mix)


def _in_wgrad(h, dproj, g_wa, g_wx, order):
    t_len = h.shape[0]
    tk = min(TKI, t_len)
    nk = t_len // tk
    cw = 7 * D // NDEV
    hr = HB // NDEV

    def body(ord_ref, h_ref, d_ref, ga_ref, gx_ref, parts_ref, pa_ref, px_ref, acc, *scr):
        rs, sems = scr[:-6], scr[-6:]
        p, k = pl.program_id(0), pl.program_id(1)

        def head_rows(ref):
            return lambda s: ref.at[:, pl.ds(s * hr, hr), :]

        @pl.when((p == 0) & (k == 0))
        def _():
            for s in range(NDEV):
                _rs_send(head_rows(ga_ref)(s), pa_ref, s, *sems[0:3])
                _rs_send(head_rows(gx_ref)(s), px_ref, s, *sems[3:6])

        @pl.when(k == 0)
        def _():
            acc[...] = jnp.zeros_like(acc)

        acc[...] += _dot_tn(h_ref[...], d_ref[...])

        @pl.when(k == nk - 1)
        def _():
            q = ord_ref[p]
            for half in range(2):
                rs[0][q, half] = acc[:, half * cw:(half + 1) * cw].astype(BF16)
            _rs2_to_sibling(q, rs)

        @pl.when((k == min(1, nk - 1)) & (p > 0))
        def _():
            _rs2_forward(ord_ref[p - 1], parts_ref, rs)

        @pl.when((p == NCHIP - 1) & (k == nk - 1))
        def _():
            _rs2_forward(ord_ref[p], parts_ref, rs)
            _rs2_finish(parts_ref, rs)
            _rs_finish(head_rows(ga_ref), pa_ref, *sems[0:3])
            _rs_finish(head_rows(gx_ref), px_ref, *sems[3:6])

    return pl.pallas_call(
        body, name="in_wgrad",
        grid_spec=pltpu.PrefetchScalarGridSpec(
            num_scalar_prefetch=1, grid=(NCHIP, nk),
            in_specs=[pl.BlockSpec((tk, D), lambda p, k, o: (k, 0)),
                      pl.BlockSpec((tk, 2 * cw), lambda p, k, o: (k, o[p])), _ANY, _ANY],
            out_specs=[_ANY, _ANY, _ANY],
            scratch_shapes=[pltpu.VMEM((D, 2 * cw), F32)] + _rs2_scratch((D, cw)) + _RS_SEMS * 2),
        out_shape=[jax.ShapeDtypeStruct((NCHIP, D, cw), BF16), jax.ShapeDtypeStruct((NDEV, HEADS, hr, HB), F32),
                   jax.ShapeDtypeStruct((NDEV, HEADS, hr, HB), F32)],
        compiler_params=_cp("arbitrary", "arbitrary"),
    )(order, h, dproj, g_wa, g_wx)


def _adam_math(w, g, m, v):
    m = ADAM_B1 * m + (1.0 - ADAM_B1) * g
    v = ADAM_B2 * v + (1.0 - ADAM_B2) * (g * g)
    m_hat = m / (1.0 - ADAM_B1 ** ADAM_STEP)
    v_hat = v / (1.0 - ADAM_B2 ** ADAM_STEP)
    delta = -ADAM_LR * (m_hat / (jnp.sqrt(v_hat) + ADAM_EPS) + ADAM_WD * w)
    return delta, m, v


def _ada_bwd(c_all, dmod_cols, w, m, v):
    rb = 256
    n = w.shape[1]
    nrow = c_all.shape[0]

    def body(c_ref, d_ref, w_ref, m_ref, v_ref, g_ref, dl_ref, nm_ref, nv_ref):
        cv = c_ref[...]
        g = _dot_tn((cv * _sig(cv)).astype(BF16), d_ref[...].astype(BF16))
        g_ref[...] = g
        dl_ref[...], nm_ref[...], nv_ref[...] = _adam_math(w_ref[...], g, m_ref[...], v_ref[...])

    blk = pl.BlockSpec((rb, n), lambda i: (i, 0))
    sds = jax.ShapeDtypeStruct(w.shape, F32)
    return pl.pallas_call(
        body, name="ada_bwd", grid=(D // rb,),
        in_specs=[pl.BlockSpec((nrow, rb), lambda i: (0, i)), pl.BlockSpec((nrow, n), lambda i: (0, 0)), blk, blk, blk],
        out_specs=[blk, blk, blk, blk], out_shape=[sds, sds, sds, sds],
        compiler_params=_cp("parallel"),
    )(c_all, dmod_cols, w, m, v)


def _adam(name, parts, w, m, v):
    p, r, c = parts.shape
    rb = r
    for cand in (256, 128, 64, 32, 16, 8):
        if r % cand == 0 and r >= cand:
            rb = cand
            break

    def body(p_ref, w_ref, m_ref, v_ref, g_ref, dl_ref, nm_ref, nv_ref):
        g = p_ref[0].astype(F32)
        for q in range(1, p):
            g = g + p_ref[q].astype(F32)
        g_ref[...] = g
        dl_ref[...], nm_ref[...], nv_ref[...] = _adam_math(w_ref[...], g, m_ref[...], v_ref[...])

    blk = pl.BlockSpec((rb, c), lambda i: (i, 0))
    sds = jax.ShapeDtypeStruct((r, c), F32)
    return pl.pallas_call(
        body, name=name, grid=(r // rb,),
        in_specs=[pl.BlockSpec((p, rb, c), lambda i: (0, i, 0)), blk, blk, blk],
        out_specs=[blk, blk, blk, blk], out_shape=[sds, sds, sds, sds],
        compiler_params=_cp("parallel"),
    )(parts, w, m, v)


_SMALL_SEMS = [pltpu.SemaphoreType.DMA((7,)), pltpu.SemaphoreType.DMA((7,)), pltpu.SemaphoreType.DMA]
_VMEM = pl.BlockSpec(memory_space=pltpu.VMEM)


def _exchange_small(x_ref, out_ref, send_sems, recv_sems, local_sem):
    m_per = x_ref.shape[0]
    x, y, c = _my_pos()
    me, sibling = (x, y, c), (x, y, 1 - c)
    chips = [(1 - x, y), (x, 1 - y), (1 - x, 1 - y)]

    def rows(px, py, pc):
        return out_ref.at[pl.ds((4 * px + 2 * py + pc) * m_per, m_per), :]

    def copy(k, block, to, src=None):
        return pltpu.make_async_remote_copy(
            src_ref=rows(*block) if src is None else src, dst_ref=rows(*block),
            send_sem=send_sems.at[k], recv_sem=recv_sems.at[k], device_id=to, device_id_type=MESH)

    mine = pltpu.make_async_copy(x_ref, rows(*me), local_sem)
    mine.start()
    first = [copy(0, me, sibling, src=x_ref)]
    first += [copy(1 + j, me, (*chip, c), src=x_ref) for j, chip in enumerate(chips)]
    for cp in first:
        cp.start()
    passed = [copy(4 + j, (*chip, c), sibling) for j, chip in enumerate(chips)]
    for j, chip in enumerate(chips):
        copy(1 + j, (*chip, c), me).wait_recv()
        passed[j].start()
    copy(0, sibling, me).wait_recv()
    for j, chip in enumerate(chips):
        copy(4 + j, (*chip, 1 - c), me).wait_recv()
    for cp in first + passed:
        cp.wait_send()
    mine.wait()


def _all_gather_small(name, v):
    def body(x_ref, out_ref, send_sems, recv_sems, local_sem):
        _exchange_small(x_ref, out_ref, send_sems, recv_sems, local_sem)

    return pl.pallas_call(
        body, name=name, out_shape=jax.ShapeDtypeStruct((NDEV * v.shape[0], v.shape[1]), v.dtype),
        in_specs=[_VMEM], out_specs=_VMEM, scratch_shapes=_SMALL_SEMS,
    )(v)


def _ada_mod(pack, w_ada, b_cols):
    ncol = w_ada.shape[1]

    def body(p_ref, w_ref, b_ref, all_ref, mod_ref, cols, *sems):
        _exchange_small(p_ref, all_ref, *sems[0:3])
        c_all = jnp.concatenate([all_ref[8 * d:8 * d + 1, 0:D] for d in range(NDEV)], axis=0)
        c16 = jnp.concatenate([c_all, jnp.zeros_like(c_all)], axis=0)
        mod16 = _dot((c16 * _sig(c16)).astype(BF16), w_ref[...].astype(BF16)) + b_ref[...]
        cols[...] = mod16[0:NDEV]
        _exchange_small(cols, mod_ref, *sems[3:6])

    return pl.pallas_call(
        body, name="ada_mod",
        out_shape=[jax.ShapeDtypeStruct((NDEV * 8, pack.shape[1]), F32), jax.ShapeDtypeStruct((NDEV * 8, ncol), F32)],
        in_specs=[_VMEM, _VMEM, _VMEM], out_specs=[_VMEM, _VMEM],
        scratch_shapes=[pltpu.VMEM((NDEV, ncol), F32)] + _SMALL_SEMS * 2,
        compiler_params=_cp(),
    )(pack, w_ada, b_cols)


def _blk_rows(n):
    return lambda ref, b: ref.at[pl.ds(pl.multiple_of(b * n, 8), n), :]


def _blk_lead(ref, b):
    return ref.at[b]


def _blk_heads(ref, b):
    return ref.at[:, pl.ds(pl.multiple_of(b * (HB // NDEV), 8), HB // NDEV), :]


def _ag_phases(ins, outs, slicers, send_sems, recv_sems, local_sems):
    na = len(ins)
    x, y, c = _my_pos()
    me, sibling = (x, y, c), (x, y, 1 - c)
    chips = [(1 - x, y), (x, 1 - y), (1 - x, 1 - y)]

    def copy(a, k, block, to, from_shard=False):
        px, py, pc = block
        dst = slicers[a](outs[a], 4 * px + 2 * py + pc)
        return pltpu.make_async_remote_copy(
            src_ref=ins[a] if from_shard else dst, dst_ref=dst,
            send_sem=send_sems.at[a * 7 + k], recv_sem=recv_sems.at[a * 7 + k], device_id=to, device_id_type=MESH)

    def local(a):
        return pltpu.make_async_copy(ins[a], slicers[a](outs[a], 4 * x + 2 * y + c), local_sems.at[a])

    def firsts(a):
        return [copy(a, 0, me, sibling, True)] + [copy(a, 1 + j, me, (*chip, c), True) for j, chip in enumerate(chips)]

    def start():
        for a in range(na):
            local(a).start()
            for cp in firsts(a):
                cp.start()

    def forward():
        for a in range(na):
            for j, chip in enumerate(chips):
                copy(a, 1 + j, (*chip, c), me).wait_recv()
                copy(a, 4 + j, (*chip, c), sibling).start()

    def finish():
        for a in range(na):
            copy(a, 0, sibling, me).wait_recv()
            for j, chip in enumerate(chips):
                copy(a, 4 + j, (*chip, 1 - c), me).wait_recv()
        for a in range(na):
            for cp in firsts(a) + [copy(a, 4 + j, (*chip, c), sibling) for j, chip in enumerate(chips)]:
                cp.wait_send()
            local(a).wait()

    return start, forward, finish


def _ag_sems(na):
    return [pltpu.SemaphoreType.DMA((7 * na,)), pltpu.SemaphoreType.DMA((7 * na,)), pltpu.SemaphoreType.DMA((na,))]


def _local_step(x, target, mod, g_mix, g_ffn, g_fin, prm, w_in_shard, shards):
    fulls = [(HEADS, HB, HB), (HEADS, HB, HB), (D, D), (NDEV, FB, D), (DFF, D)]
    slicers = [_blk_heads, _blk_heads, _blk_rows(D // NDEV), _blk_lead, _blk_rows(DFF // NDEV)]
    my_chip = _my_index() >> 1
    own_first = (my_chip ^ jnp.arange(NCHIP, dtype=jnp.int32)).astype(jnp.int32)
    early, late = [0, 1, 2, 4], [3]
    pick = lambda lst, idx: [lst[i] for i in idx]
    proj, h, w_in, (wa, wx, w_out, w_down) = _in_proj(x, mod, g_mix, w_in_shard, own_first, pick(shards, early),
                                                      pick(fulls, early), pick(slicers, early))
    merged, hl, sv, (w_gu,) = _mixer_fwd(proj, prm, wa, wx, pick(shards, late), pick(fulls, late),
                                         pick(slicers, late))
    w_gu = w_gu.reshape(2, 4, FB, D)
    x1, h2 = _out_proj(merged, x, mod, g_ffn, w_out)
    gu, dx2, dx2b, loss, d_gfin = _ffn_fwd(h2, x1, target, mod, g_fin, w_gu, w_down)
    dgu, act, dx1, dx1b, dmg, sums2 = _ffn_bwd(dx2, gu, x1, mod, g_ffn, w_gu, w_down, w_out)
    chip_order = _xor_order(_my_index() >> 1, NCHIP)
    p_wgu = _gu_wgrad(h2, dgu, chip_order)
    p_wdown, d_gt2 = _scaled_wgrad("down_wgrad", act, dx2b, w_down, 5, mod, chip_order)
    p_wout, d_gt1 = _scaled_wgrad("out_wgrad", merged, dx1b, w_out, 2, mod, chip_order)
    dproj, msums, g_wa, g_wx = _mixer_bwd(proj, hl, sv, dmg, prm, wa, wx)
    p_win, p_wa, p_wx = _in_wgrad(h, dproj, g_wa, g_wx, chip_order)
    grad_x, sums1 = _in_proj_bwd(dproj, w_in, x, dx1, mod, g_mix)
    return dict(loss=loss, grad_x=grad_x, d_gfin=d_gfin, sums1=sums1, sums2=sums2, msums=msums,
                d_gt1=d_gt1[0:1], d_gt2=d_gt2[0:1], p_win=p_win, p_wa=p_wa, p_wx=p_wx, p_wout=p_wout, p_wgu=p_wgu,
                p_wdown=p_wdown)


def kernel(x, c, w_ada, b_ada, g_norm_mix, w_in, conv_a_w, conv_b_w, conv_b_bias, w_rg_a, b_rg_a, w_rg_x, b_rg_x, lru_lambda, w_out, g_norm_ffn, w_gate_up, w_down, g_norm_final, loss_target, m_w_ada, m_b_ada, m_g_norm_mix, m_w_in, m_conv_a_w, m_conv_b_w, m_conv_b_bias, m_w_rg_a, m_b_rg_a, m_w_rg_x, m_b_rg_x, m_lru_lambda, m_w_out, m_g_norm_ffn, m_w_gate_up, m_w_down, m_g_norm_final, v_w_ada, v_b_ada, v_g_norm_mix, v_w_in, v_conv_a_w, v_conv_b_w, v_conv_b_bias, v_w_rg_a, v_b_rg_a, v_w_rg_x, v_b_rg_x, v_lru_lambda, v_w_out, v_g_norm_ffn, v_w_gate_up, v_w_down, v_g_norm_final):
    me = 4 * lax.axis_index("x") + 2 * lax.axis_index("y") + lax.axis_index("c")
    ncol = w_ada.shape[2]
    cw = conv_a_w.shape[2]

    pack0 = jnp.concatenate([c, conv_a_w.reshape(1, 3 * cw), conv_b_w.reshape(1, 4 * cw)], axis=1)
    b_cols = lax.dynamic_slice_in_dim(b_ada, me * ncol, ncol, axis=1)
    got0, got1 = _ada_mod(jnp.broadcast_to(pack0, (8, pack0.shape[1])), w_ada[0], b_cols)
    got0 = got0.reshape(NDEV, 8, -1)[:, 0, :]
    c_all = got0[:, :D]
    conv_a = got0[:, D:D + 3 * cw].reshape(NDEV, 3, cw).transpose(1, 0, 2).reshape(3, D)
    conv_b = got0[:, D + 3 * cw:].reshape(NDEV, 4, cw).transpose(1, 0, 2).reshape(4, D)
    c16 = jnp.concatenate([c_all, jnp.zeros((8, D), F32)], axis=0)
    mod6 = lax.dynamic_index_in_dim(got1.reshape(NDEV, NDEV, ncol), me, axis=1, keepdims=False).reshape(6, D)
    mod = jnp.concatenate([mod6, jnp.zeros((2, D), F32)], axis=0)

    tr = lambda a: jnp.swapaxes(a, 1, 2)
    shards = [w_rg_a[0].astype(BF16), w_rg_x[0].astype(BF16), w_out[0].astype(BF16), tr(w_gate_up)[0].astype(BF16),
              w_down[0].astype(BF16)]

    prm = jnp.concatenate([conv_a, conv_b, conv_b_bias, b_rg_a, b_rg_x, lru_lambda, jnp.zeros((5, D), F32)], axis=0)
    r = _local_step(x[0], loss_target[0], mod, g_norm_mix, g_norm_ffn, g_norm_final.reshape(1, D), prm,
                    w_in[0].astype(BF16), shards)

    parts = [r["p_win"], r["p_wa"], r["p_wx"], r["p_wout"], r["p_wgu"], r["p_wdown"]]
    big = {}
    for nm, p, w, m, v in (("w_in", parts[0], w_in, m_w_in, v_w_in), ("w_rg_a", parts[1], w_rg_a, m_w_rg_a, v_w_rg_a),
                           ("w_rg_x", parts[2], w_rg_x, m_w_rg_x, v_w_rg_x), ("w_out", parts[3], w_out, m_w_out, v_w_out),
                           ("w_gate_up", parts[4], tr(w_gate_up), tr(m_w_gate_up), tr(v_w_gate_up)),
                           ("w_down", parts[5], w_down, m_w_down, v_w_down)):
        two_d = (-1, w.shape[-1])
        outs = _adam("adam_" + nm, p.reshape((p.shape[0],) + w.reshape(two_d).shape), w.reshape(two_d), m.reshape(two_d),
                     v.reshape(two_d))
        big[nm] = [o.reshape(w.shape) for o in outs]
    big["w_gate_up"] = [tr(o) for o in big["w_gate_up"]]

    small = jnp.concatenate([
        r["sums1"][S_SH:S_SH + 1], r["sums1"][S_SC:S_SC + 1], r["d_gt1"],
        r["sums2"][S_SH:S_SH + 1], r["sums2"][S_SC:S_SC + 1], r["d_gt2"],
        r["sums1"][S_G:S_G + 1],
        r["msums"][M_CBIAS:M_CBIAS + 1], r["msums"][M_BA:M_BA + 1], r["msums"][M_BX:M_BX + 1],
        r["msums"][M_LS:M_LS + 1],
        r["sums2"][S_G:S_G + 1], r["d_gfin"],
        r["msums"][M_WA:M_WA + 3], r["msums"][M_WB:M_WB + 4],
        jnp.broadcast_to(r["loss"][0:1, 0:1], (1, D)),
        jnp.zeros((3, D), F32)], axis=0)
    got2 = _all_gather_small("gather_small", small).reshape(NDEV, 24, D)

    rep_w = jnp.concatenate([b_ada.reshape(6, D), g_norm_mix, conv_b_bias, b_rg_a, b_rg_x, lru_lambda, g_norm_ffn,
                             g_norm_final.reshape(1, D), jnp.zeros((3, D), F32)], axis=0)
    rep_m = jnp.concatenate([m_b_ada.reshape(6, D), m_g_norm_mix, m_conv_b_bias, m_b_rg_a, m_b_rg_x, m_lru_lambda,
                             m_g_norm_ffn, m_g_norm_final.reshape(1, D), jnp.zeros((3, D), F32)], axis=0)
    rep_v = jnp.concatenate([v_b_ada.reshape(6, D), v_g_norm_mix, v_conv_b_bias, v_b_rg_a, v_b_rg_x, v_lru_lambda,
                             v_g_norm_ffn, v_g_norm_final.reshape(1, D), jnp.ones((3, D), F32)], axis=0)
    rep = _adam("adam_rep", got2[:, :16, :], rep_w, rep_m, rep_v)

    conv_parts = lax.dynamic_slice_in_dim(got2[:, 13:21, :], me * cw, cw, axis=2)
    cv_w = jnp.concatenate([conv_a_w[0], conv_b_w[0], jnp.zeros((1, cw), F32)], axis=0)
    cv_m = jnp.concatenate([m_conv_a_w[0], m_conv_b_w[0], jnp.zeros((1, cw), F32)], axis=0)
    cv_v = jnp.concatenate([v_conv_a_w[0], v_conv_b_w[0], jnp.ones((1, cw), F32)], axis=0)
    cvo = _adam("adam_conv", conv_parts, cv_w, cv_m, cv_v)

    dmod_cols = lax.dynamic_slice_in_dim(got2[:, :6, :].reshape(NDEV, 6 * D), me * ncol, ncol, axis=1)
    dmod16 = jnp.concatenate([dmod_cols, jnp.zeros((8, ncol), F32)], axis=0)
    ada = _ada_bwd(c16, dmod16, w_ada[0], m_w_ada[0], v_w_ada[0])

    loss = jnp.sum(got2[:, 20, 0])

    def pick(q):
        one = lambda i: rep[q][i:i + 1]
        return [ada[q].reshape(w_ada.shape), rep[q][0:6].reshape(b_ada.shape), one(6), big["w_in"][q],
                cvo[q][0:3].reshape(conv_a_w.shape), cvo[q][3:7].reshape(conv_b_w.shape), one(7),
                big["w_rg_a"][q], one(8), big["w_rg_x"][q], one(9), one(10), big["w_out"][q], one(11),
                big["w_gate_up"][q], big["w_down"][q], rep[q][12]]

    return (loss, r["grad_x"].reshape(x.shape), *pick(0), *pick(1), *pick(2), *pick(3))
```

```python
import math

import jax
import jax.numpy as jnp
from jax import lax
from jax.experimental import pallas as pl
from jax.experimental.pallas import tpu as pltpu

F32 = jnp.float32
BF16 = jnp.bfloat16

D = 1024
DFF = 2816
NDEV = 8
HEADS = 4
HB = D // HEADS
FB = DFF // 4
EPS = 1e-6
LRU_C = 8.0
ADAM_LR, ADAM_B1, ADAM_B2, ADAM_EPS, ADAM_WD, ADAM_STEP = 0.001, 0.9, 0.999, 1e-08, 0.01, 10

VMEM_LIMIT = 56 * 1024 * 1024
TM = 512
TMI = 1024
TMF = 256
TK = 2048
TKI = 2048
SUB = 256
TT = 256
CG = 256
MESH = pl.DeviceIdType.MESH


def _cp(*sem):
    return pltpu.CompilerParams(dimension_semantics=sem, vmem_limit_bytes=VMEM_LIMIT)


def _sig(x):
    return 1.0 / (1.0 + jnp.exp(-x))


def _log_sigmoid(x):
    z = jnp.exp(-jnp.abs(x))
    u = 1.0 + z
    d = u - 1.0
    l1p = jnp.where(d == 0.0, z, jnp.log(u) * (z / jnp.where(d == 0.0, 1.0, d)))
    return -(jnp.maximum(-x, 0.0) + l1p)


def _neg_expm1(x):
    p = x * (1.0 + x * 0.5 * (1.0 + x * (1.0 / 3.0) * (1.0 + x * 0.25 * (1.0 + x * 0.2 * (1.0 + x * (1.0 / 6.0))))))
    return jnp.where(x > -0.25, -p, 1.0 - jnp.exp(x))


_GC = math.sqrt(2.0 / math.pi)


def _gelu(x):
    t = jnp.tanh(_GC * (x + 0.044715 * x * x * x))
    return 0.5 * x * (1.0 + t), t


def _dot(a, b):
    return jnp.dot(a, b, preferred_element_type=F32)


def _dot_nt(a, b):
    return lax.dot_general(a, b, (((1,), (1,)), ((), ())), preferred_element_type=F32)


def _dot_tn(a, b):
    return lax.dot_general(a, b, (((0,), (0,)), ((), ())), preferred_element_type=F32)


def _resident(shape):
    return pl.BlockSpec(shape, lambda *_: (0,) * len(shape), pipeline_mode=pl.Buffered(1))


def _sub_blocks(n_rows):
    step = min(SUB, n_rows)
    return [slice(r, r + step) for r in range(0, n_rows, step)]


def _fold8(v):
    return v[0:8] + v[8:16]


def _pj(ref, s, rows=slice(None), cols=slice(0, D)):
    return ref[rows, s * D + cols.start:s * D + cols.stop]


def _in_proj(x, mod, g_mix, w_shard, order, shards, fulls, slicers):
    t_len = x.shape[0]
    tm = min(TMI, t_len)
    ni = t_len // tm
    na = len(shards)
    cw = 7 * D // NDEV
    rc = 32

    def body(ord_ref, x_ref, mod_ref, g_ref, wsh_ref, *rest):
        ins, (proj_ref, h_ref, wfull_ref), outs = rest[:na], rest[na:na + 3], rest[na + 3:2 * na + 3]
        h_scr, w_scr, wsend, wrecv, wlocal, wout = rest[2 * na + 3:2 * na + 9]
        start, forward, finish = _ag_phases(ins, outs, slicers, *rest[2 * na + 9:])
        p, i = pl.program_id(0), pl.program_id(1)
        x_, y_, c = _my_pos()
        me, sibling = (x_, y_, c), (x_, y_, 1 - c)
        chip_at = [None, (x_, 1 - y_), (1 - x_, y_), (1 - x_, 1 - y_)]

        def cols(px, py, pc):
            return w_scr.at[:, pl.ds(pl.multiple_of((4 * px + 2 * py + pc) * cw, 128), cw)]

        def wcopy(k, block, to, from_shard=False):
            dst = cols(*block)
            return pltpu.make_async_remote_copy(src_ref=wsh_ref if from_shard else dst, dst_ref=dst,
                                                send_sem=wsend.at[k], recv_sem=wrecv.at[k], device_id=to,
                                                device_id_type=MESH)

        own_local = pltpu.make_async_copy(wsh_ref, cols(*me), wlocal)
        to_hbm = pltpu.make_async_copy(w_scr, wfull_ref, wout)

        @pl.when((p == 0) & (i == 0))
        def _():
            own_local.start()
            wcopy(0, me, sibling, True).start()
            for q in (1, 2):
                wcopy(q, me, (*chip_at[q], c), True).start()
            own_local.wait()
            wcopy(0, sibling, me).wait_recv()

        @pl.when((p == 0) & (i == ni // 2))
        def _():
            wcopy(3, me, (*chip_at[3], c), True).start()

        for q in (1, 2, 3):
            @pl.when((p == q - 1) & (i == ni - 1))
            def _():
                wcopy(q, (*chip_at[q], c), me).wait_recv()
                wcopy(3 + q, (*chip_at[q], c), sibling).start()

            @pl.when((p == q) & (i == 0))
            def _():
                wcopy(3 + q, (*chip_at[q], 1 - c), me).wait_recv()

        @pl.when((p == 1) & (i == 0))
        def _():
            start()

        @pl.when((p == NCHIP - 1) & (i == ni // 2))
        def _():
            forward()

        @pl.when((p == NCHIP - 1) & (i == 0))
        def _():
            to_hbm.start()

        gs = g_ref[...] * (1.0 + mod_ref[1:2, :])
        sh = mod_ref[0:1, :]

        wcols = pl.ds(pl.multiple_of(ord_ref[p] * (2 * cw), 128), 2 * cw)
        for sub in _sub_blocks(tm):
            for r0 in range(sub.start, sub.stop, rc):
                xv = x_ref[r0:r0 + rc, :]
                r = lax.rsqrt(jnp.mean(xv * xv, axis=-1, keepdims=True) + EPS)
                h_scr[r0:r0 + rc, :] = (xv * r * gs + sh).astype(BF16)
            proj_ref[sub, :] = _dot(h_scr[sub, :], w_scr[:, wcols]).astype(BF16)

        @pl.when(p == 0)
        def _():
            h_ref[...] = h_scr[...]

        @pl.when((p == NCHIP - 1) & (i == ni - 1))
        def _():
            wcopy(0, me, sibling, True).wait_send()
            for q in (1, 2, 3):
                wcopy(q, me, (*chip_at[q], c), True).wait_send()
                wcopy(3 + q, (*chip_at[q], c), sibling).wait_send()
            finish()
            to_hbm.wait()

    res = pl.pallas_call(
        body, name="in_proj",
        grid_spec=pltpu.PrefetchScalarGridSpec(
            num_scalar_prefetch=1, grid=(NCHIP, ni),
            in_specs=[pl.BlockSpec((tm, D), lambda p, i, o: (i, 0)),
                      pl.BlockSpec((8, D), lambda p, i, o: (0, 0)),
                      pl.BlockSpec((1, D), lambda p, i, o: (0, 0))] + [_ANY] * (1 + na),
            out_specs=[pl.BlockSpec((tm, 2 * cw), lambda p, i, o: (i, o[p])),
                       pl.BlockSpec((tm, D), lambda p, i, o: (jnp.where(p == 0, i, ni - 1), 0))]
            + [_ANY] * (1 + na),
            scratch_shapes=[pltpu.VMEM((tm, D), BF16), pltpu.VMEM((D, 7 * D), BF16),
                            pltpu.SemaphoreType.DMA((7,)), pltpu.SemaphoreType.DMA((7,)),
                            pltpu.SemaphoreType.DMA, pltpu.SemaphoreType.DMA] + _ag_sems(na)),
        out_shape=[jax.ShapeDtypeStruct((t_len, 7 * D), BF16), jax.ShapeDtypeStruct((t_len, D), BF16),
                   jax.ShapeDtypeStruct((D, 7 * D), BF16)]
        + [jax.ShapeDtypeStruct(f, sh.dtype) for f, sh in zip(fulls, shards)],
        compiler_params=_cp("arbitrary", "arbitrary"),
    )(order, x, mod, g_mix, w_shard, *shards)
    return res[0], res[1], res[2], res[3:]


P_WA, P_WB, P_CBIAS, P_BA, P_BX, P_LAM = 0, 3, 7, 8, 9, 10
SV_PLANES = SV_U, SV_YA, SV_R, SV_I, SV_A, SV_MULT = range(6)


def _lru_gates(rp, ip, ls, first_row):
    r = _sig(rp)
    ig = _sig(ip)
    la = LRU_C * r * ls
    a = jnp.exp(la)
    m2 = _neg_expm1(2.0 * la)
    mult = jnp.where(first_row, 1.0, jnp.sqrt(jnp.maximum(m2, 0.0)))
    return r, ig, la, a, m2, mult


def _shift_down(cur, prev, s, row):
    return jnp.where(row >= s, pltpu.roll(cur, s, 0), pltpu.roll(prev, s, 0))


def _shift_up(cur, nxt, s, row):
    return jnp.where(row < 8 - s, pltpu.roll(cur, 8 - s, 0), pltpu.roll(nxt, 8 - s, 0))


def _conv_fwd_rows(tt, proj_ref, prm_ref, xe, ve, u_s, ub_s, ya_s):
    row = lax.broadcasted_iota(jnp.int32, (8, CG), 0)
    w_b = [prm_ref[P_WB + k:P_WB + k + 1, :] for k in range(4)]
    w_a = [prm_ref[P_WA + k:P_WA + k + 1, :] for k in range(3)]
    bias = prm_ref[P_CBIAS:P_CBIAS + 1, :]

    def blk(ib, carry):
        r0 = pl.multiple_of(ib * 16, 16)
        rows = pl.ds(r0, 16)
        for g in range(D // CG):
            cs = slice(g * CG, (g + 1) * CG)
            x16 = _pj(proj_ref, 3, rows, cs).astype(F32)
            v16 = _pj(proj_ref, 1, rows, cs).astype(F32) * _pj(proj_ref, 2, rows, cs).astype(F32)
            xp = xe[pl.ds(r0, 8), cs]
            vp = ve[pl.ds(r0, 8), cs]
            xe[pl.ds(r0 + 8, 16), cs] = x16
            ve[pl.ds(r0 + 8, 16), cs] = v16
            us, yas = [], []
            for sb in range(2):
                xc, vc = x16[8 * sb:8 * sb + 8], v16[8 * sb:8 * sb + 8]
                u8 = bias[:, cs] + w_b[3][:, cs] * xc
                for s in (1, 2, 3):
                    u8 = u8 + w_b[3 - s][:, cs] * _shift_down(xc, xp, s, row)
                y8 = w_a[2][:, cs] * vc
                for s in (1, 2):
                    y8 = y8 + w_a[2 - s][:, cs] * _shift_down(vc, vp, s, row)
                us.append(u8)
                yas.append(y8)
                xp, vp = xc, vc
            u16 = jnp.concatenate(us, axis=0)
            u_s[rows, cs] = u16
            ub_s[rows, cs] = u16.astype(BF16)
            ya_s[rows, cs] = jnp.concatenate(yas, axis=0)
        return carry

    lax.fori_loop(0, tt // 16, blk, 0)


def _mixer_fwd(proj, prm, wa, wx, shards, fulls, slicers):
    t_len = proj.shape[0]
    tt = min(TT, t_len)
    nt = t_len // tt
    na = len(shards)

    def body(proj_ref, prm_ref, wa_ref, wx_ref, *rest):
        ins, (mg_ref, hl_ref, sv_hbm), outs = rest[:na], rest[na:na + 3], rest[na + 3:2 * na + 3]
        xe, ve, hc, rp_s, ip_s, ub_s, u_s, ya_s, sv_st, sv_sems = rest[2 * na + 3:2 * na + 13]
        start, forward, finish = _ag_phases(ins, outs, slicers, *rest[2 * na + 13:])
        t = pl.program_id(0)

        slot = t % 2
        sv_ref = sv_st.at[slot]

        def sv_out(tile, sl):
            rows = pl.ds(pl.multiple_of(tile * tt, tt), tt)
            return pltpu.make_async_copy(sv_st.at[sl], sv_hbm.at[:, rows, :], sv_sems.at[sl])

        @pl.when(t >= 2)
        def _():
            sv_out(t - 2, slot).wait()

        @pl.when(t == 0)
        def _():
            start()
            xe[0:8, :] = jnp.zeros((8, D), F32)
            ve[0:8, :] = jnp.zeros((8, D), F32)
            hc[...] = jnp.zeros((8, D), F32)

        @pl.when(t == (3 * nt) // 4)
        def _():
            forward()

        _conv_fwd_rows(tt, proj_ref, prm_ref, xe, ve, u_s, ub_s, ya_s)
        sv_ref[SV_U] = u_s[...]
        sv_ref[SV_YA] = ya_s[...]
        xe[0:8, :] = xe[tt:tt + 8, :]
        ve[0:8, :] = ve[tt:tt + 8, :]

        ub = ub_s[...]
        for h in range(HEADS):
            cs = slice(h * HB, (h + 1) * HB)
            rp_s[:, cs] = _dot(ub[:, cs], wa_ref[h]) + prm_ref[P_BA:P_BA + 1, cs]
            ip_s[:, cs] = _dot(ub[:, cs], wx_ref[h]) + prm_ref[P_BX:P_BX + 1, cs]

        ls_all = _log_sigmoid(prm_ref[P_LAM:P_LAM + 1, :])
        row = lax.broadcasted_iota(jnp.int32, (8, CG), 0)

        def blk(i, carry):
            r0 = pl.multiple_of(i * 16, 16)
            for g in range(D // CG):
                cs = slice(g * CG, (g + 1) * CG)
                ls = ls_all[:, cs]
                hprev = hc[:, cs]
                hs = []
                for sb in range(2):
                    rr = r0 + 8 * sb
                    first = (row + (t * tt + rr)) == 0
                    r8 = pl.ds(rr, 8)
                    r, ig, _, a, _, mult = _lru_gates(rp_s[r8, cs], ip_s[r8, cs], ls, first)
                    for plane, val in ((SV_R, r), (SV_I, ig), (SV_A, a), (SV_MULT, mult)):
                        sv_ref[plane, r8, cs] = val
                    b = mult * (ig * u_s[r8, cs])
                    for s in (1, 2, 4):
                        a_sh = jnp.where(row >= s, pltpu.roll(a, s, 0), 1.0)
                        b_sh = jnp.where(row >= s, pltpu.roll(b, s, 0), 0.0)
                        b = a * b_sh + b
                        a = a * a_sh
                    hv = a * hprev + b
                    hprev = jnp.broadcast_to(hv[7:8, :], hv.shape)
                    hs.append(hv)
                hc[:, cs] = hprev
                h16 = jnp.concatenate(hs, axis=0)
                rows = pl.ds(r0, 16)
                gl, _ = _gelu(_pj(proj_ref, 4, rows, cs).astype(F32))
                y_b = h16 * gl
                y_a = _pj(proj_ref, 0, rows, cs).astype(F32) * ya_s[rows, cs]
                mg = (_sig(_pj(proj_ref, 5, rows, cs).astype(F32)) * y_a
                      + _sig(_pj(proj_ref, 6, rows, cs).astype(F32)) * y_b)
                mg_ref[rows, cs] = mg.astype(BF16)
                hl_ref[rows, cs] = h16.astype(BF16)
            return carry

        lax.fori_loop(0, tt // 16, blk, 0)
        sv_out(t, slot).start()

        @pl.when(t == nt - 1)
        def _():
            finish()
            if nt >= 2:
                sv_out(t - 1, 1 - slot).wait()
            sv_out(t, slot).wait()

    res = pl.pallas_call(
        body, name="mixer_fwd", grid=(nt,),
        in_specs=[pl.BlockSpec((tt, 7 * D), lambda t: (t, 0)),
                  pl.BlockSpec((16, D), lambda t: (0, 0)),
                  pl.BlockSpec((HEADS, HB, HB), lambda t: (0, 0, 0)),
                  pl.BlockSpec((HEADS, HB, HB), lambda t: (0, 0, 0))] + [_ANY] * na,
        out_specs=[pl.BlockSpec((tt, D), lambda t: (t, 0)), pl.BlockSpec((tt, D), lambda t: (t, 0)), _ANY]
        + [_ANY] * na,
        out_shape=[jax.ShapeDtypeStruct((t_len, D), BF16), jax.ShapeDtypeStruct((t_len, D), BF16),
                   jax.ShapeDtypeStruct((len(SV_PLANES), t_len, D), F32)]
        + [jax.ShapeDtypeStruct(f, sh.dtype) for f, sh in zip(fulls, shards)],
        scratch_shapes=[pltpu.VMEM((tt + 8, D), F32), pltpu.VMEM((tt + 8, D), F32), pltpu.VMEM((8, D), F32),
                        pltpu.VMEM((tt, D), F32), pltpu.VMEM((tt, D), F32), pltpu.VMEM((tt, D), BF16),
                        pltpu.VMEM((tt, D), F32), pltpu.VMEM((tt, D), F32),
                        pltpu.VMEM((2, len(SV_PLANES), tt, D), F32), pltpu.SemaphoreType.DMA((2,))]
        + _ag_sems(na),
        compiler_params=_cp("arbitrary"),
    )(proj, prm, wa, wx, *shards)
    return res[0], res[1], res[2], res[3:]


def _out_proj(merged, x, mod, g_ffn, w_out):
    t_len = x.shape[0]
    tm = min(TM, t_len)

    def body(mg_ref, x_ref, mod_ref, g_ref, w_ref, x1_ref, h2_ref):
        gt1 = mod_ref[2:3, :]
        gs = g_ref[...] * (1.0 + mod_ref[4:5, :])
        sh = mod_ref[3:4, :]
        for sub in _sub_blocks(tm):
            x1_ref[sub, :] = x_ref[sub, :] + gt1 * _dot(mg_ref[sub, :], w_ref[...])
            for r0 in range(sub.start, sub.stop, 16):
                x1 = x1_ref[r0:r0 + 16, :]
                r = lax.rsqrt(jnp.mean(x1 * x1, axis=-1, keepdims=True) + EPS)
                h2_ref[r0:r0 + 16, :] = (x1 * r * gs + sh).astype(BF16)

    return pl.pallas_call(
        body, name="out_proj", grid=(t_len // tm,),
        in_specs=[pl.BlockSpec((tm, D), lambda i: (i, 0)), pl.BlockSpec((tm, D), lambda i: (i, 0)),
                  pl.BlockSpec((8, D), lambda i: (0, 0)), pl.BlockSpec((1, D), lambda i: (0, 0)),
                  pl.BlockSpec((D, D), lambda i: (0, 0))],
        out_specs=[pl.BlockSpec((tm, D), lambda i: (i, 0)), pl.BlockSpec((tm, D), lambda i: (i, 0))],
        out_shape=[jax.ShapeDtypeStruct((t_len, D), F32), jax.ShapeDtypeStruct((t_len, D), BF16)],
        compiler_params=_cp("parallel"),
    )(merged, x, mod, g_ffn, w_out)


def _ffn_fwd(h2, x1, target, mod, g_fin, w_gu, w_down):
    t_len = x1.shape[0]
    tm = min(TMF, t_len)

    def body(h2_ref, x1_ref, tg_ref, mod_ref, g_ref, wgu_ref, wd_ref, gu_ref, dx2_ref, dx2b_ref, loss_ref, dg_ref, acc):
        @pl.when(pl.program_id(0) == 0)
        def _():
            loss_ref[...] = jnp.zeros_like(loss_ref)
            dg_ref[...] = jnp.zeros_like(dg_ref)

        hb = h2_ref[...]
        ffn = None
        nxt = (_dot_nt(hb, wgu_ref[0, 0]), _dot_nt(hb, wgu_ref[1, 0]))
        for j in range(4):
            gate, up = nxt
            if j < 3:
                nxt = (_dot_nt(hb, wgu_ref[0, j + 1]), _dot_nt(hb, wgu_ref[1, j + 1]))
            gu_ref[0, j] = gate.astype(BF16)
            gu_ref[1, j] = up.astype(BF16)
            act = (gate * _sig(gate) * up).astype(BF16)
            part = _dot(act, wd_ref[j * FB:(j + 1) * FB, :])
            ffn = part if ffn is None else ffn + part
        acc[...] = ffn

        gt2 = mod_ref[5:6, :]
        gf = g_ref[...]

        s_loss = s_dg = jnp.zeros((8, D), F32)
        for r0 in range(0, tm, 16):
            rows = slice(r0, r0 + 16)
            x2 = x1_ref[rows, :] + gt2 * acc[rows, :]
            r = lax.rsqrt(jnp.mean(x2 * x2, axis=-1, keepdims=True) + EPS)
            xn = x2 * r
            diff = xn * gf - tg_ref[rows, :]
            dy = diff * (1.0 / D)
            dxn = dy * gf
            dx2 = r * (dxn - xn * jnp.mean(dxn * xn, axis=-1, keepdims=True))
            dx2_ref[rows, :] = dx2
            dx2b_ref[rows, :] = dx2.astype(BF16)
            s_loss, s_dg = s_loss + _fold8(diff * diff), s_dg + _fold8(dy * xn)
        loss_ref[...] += jnp.sum(s_loss) * (0.5 / D)
        dg_ref[...] += jnp.sum(s_dg, axis=0, keepdims=True)

    row = pl.BlockSpec((tm, D), lambda i: (i, 0))
    return pl.pallas_call(
        body, name="ffn_fwd", grid=(t_len // tm,),
        in_specs=[row, row, row, pl.BlockSpec((8, D), lambda i: (0, 0)), pl.BlockSpec((1, D), lambda i: (0, 0)),
                  _resident((2, 4, FB, D)), _resident((DFF, D))],
        out_specs=[pl.BlockSpec((2, 4, tm, FB), lambda i: (0, 0, i, 0)), row, row,
                   pl.BlockSpec((1, 128), lambda i: (0, 0)), pl.BlockSpec((1, D), lambda i: (0, 0))],
        out_shape=[jax.ShapeDtypeStruct((2, 4, t_len, FB), BF16), jax.ShapeDtypeStruct((t_len, D), F32),
                   jax.ShapeDtypeStruct((t_len, D), BF16),
                   jax.ShapeDtypeStruct((1, 128), F32), jax.ShapeDtypeStruct((1, D), F32)],
        scratch_shapes=[pltpu.VMEM((tm, D), F32)],
        compiler_params=_cp("arbitrary"),
    )(h2, x1, target, mod, g_fin, w_gu, w_down)


S_SH, S_SC, S_G = 0, 1, 2


def _norm_bwd_rows(span, sums, dh_ref, x_ref, dres_ref, scale, gain, write):
    gs = 1.0 + scale
    s_sh, s_sc, s_g = sums
    for r0 in range(span.start, span.stop, 16):
        rows = slice(r0, r0 + 16)
        dh = dh_ref[rows, :]
        xv = x_ref[rows, :]
        r = lax.rsqrt(jnp.mean(xv * xv, axis=-1, keepdims=True) + EPS)
        xn = xv * r
        dhn = dh * gs
        dxn = dhn * gain
        write(rows, dres_ref[rows, :] + r * (dxn - xn * jnp.mean(dxn * xn, axis=-1, keepdims=True)))
        s_sh, s_sc, s_g = s_sh + _fold8(dh), s_sc + _fold8(dh * (xn * gain)), s_g + _fold8(dhn * xn)
    return s_sh, s_sc, s_g


def _add_norm_sums(sums_ref, sums):
    for dst, s in zip((S_SH, S_SC, S_G), sums):
        sums_ref[dst:dst + 1, :] += jnp.sum(s, axis=0, keepdims=True)


def _ffn_bwd(dx2, gu, x1, mod, g_ffn, w_gu, w_down, w_out):
    t_len = x1.shape[0]
    tm = min(TMF, t_len)

    def body(dx2_ref, gu_ref, x1_ref, mod_ref, g_ref, wgu_ref, wd_ref, wo_ref,
             dgu_ref, act_ref, dx1_ref, dx1b_ref, dmg_ref, sums_ref, acc, dmo, dact_s):
        @pl.when(pl.program_id(0) == 0)
        def _():
            sums_ref[...] = jnp.zeros_like(sums_ref)

        dffn = (dx2_ref[...] * mod_ref[5:6, :]).astype(BF16)
        dact_s[0] = _dot_nt(dffn, wd_ref[0:FB, :])
        for j in range(4):
            if j < 3:
                dact_s[(j + 1) % 2] = _dot_nt(dffn, wd_ref[(j + 1) * FB:(j + 2) * FB, :])
            for r0 in range(0, tm, 16):
                rows = slice(r0, r0 + 16)
                dact = dact_s[j % 2, rows, :]
                gate = gu_ref[0, j, rows, :].astype(F32)
                up = gu_ref[1, j, rows, :].astype(F32)
                sg = _sig(gate)
                silu = gate * sg
                act_ref[j, rows, :] = (silu * up).astype(BF16)
                dgu_ref[0, j, rows, :] = (dact * up * (sg * (1.0 + gate * (1.0 - sg)))).astype(BF16)
                dgu_ref[1, j, rows, :] = (dact * silu).astype(BF16)
            part = _dot(dgu_ref[0, j], wgu_ref[0, j]) + _dot(dgu_ref[1, j], wgu_ref[1, j])
            if j == 0:
                acc[...] = part
            else:
                acc[...] += part

        gt1 = mod_ref[2:3, :]

        def write(rows, dx1):
            dx1_ref[rows, :] = dx1
            dx1b_ref[rows, :] = dx1.astype(BF16)
            dmo[rows, :] = (dx1 * gt1).astype(BF16)

        zero = jnp.zeros((8, D), F32)
        sums = (zero, zero, zero)
        for sub in (slice(0, tm // 2), slice(tm // 2, tm)):
            sums = _norm_bwd_rows(sub, sums, acc, x1_ref, dx2_ref, mod_ref[4:5, :], g_ref[...], write)
            dmg_ref[sub, :] = _dot_nt(dmo[sub, :], wo_ref[...]).astype(BF16)
        _add_norm_sums(sums_ref, sums)

    row = pl.BlockSpec((tm, D), lambda i: (i, 0))
    return pl.pallas_call(
        body, name="ffn_bwd", grid=(t_len // tm,),
        in_specs=[row, pl.BlockSpec((2, 4, tm, FB), lambda i: (0, 0, i, 0)), row,
                  pl.BlockSpec((8, D), lambda i: (0, 0)), pl.BlockSpec((1, D), lambda i: (0, 0)),
                  _resident((2, 4, FB, D)), _resident((DFF, D)), _resident((D, D))],
        out_specs=[pl.BlockSpec((2, 4, tm, FB), lambda i: (0, 0, i, 0)),
                   pl.BlockSpec((4, tm, FB), lambda i: (0, i, 0)), row, row, row,
                   pl.BlockSpec((8, D), lambda i: (0, 0))],
        out_shape=[jax.ShapeDtypeStruct((2, 4, t_len, FB), BF16), jax.ShapeDtypeStruct((4, t_len, FB), BF16),
                   jax.ShapeDtypeStruct((t_len, D), F32), jax.ShapeDtypeStruct((t_len, D), BF16),
                   jax.ShapeDtypeStruct((t_len, D), BF16), jax.ShapeDtypeStruct((8, D), F32)],
        scratch_shapes=[pltpu.VMEM((tm, D), F32), pltpu.VMEM((tm, D), BF16), pltpu.VMEM((2, tm, FB), F32)],
        compiler_params=_cp("arbitrary"),
    )(dx2, gu, x1, mod, g_ffn, w_gu, w_down, w_out)


def _my_pos():
    return lax.axis_index("x"), lax.axis_index("y"), lax.axis_index("c")


def _my_index():
    x, y, c = _my_pos()
    return 4 * x + 2 * y + c


def _device_of(b):
    return (b >> 2) & 1, (b >> 1) & 1, b & 1


def _rs_send(src, parts_ref, b, send_sems, recv_sems, local_sem):
    me = _my_index()
    dst = parts_ref.at[me]

    @pl.when(b == me)
    def _():
        pltpu.make_async_copy(src, dst, local_sem).start()

    @pl.when(b != me)
    def _():
        pltpu.make_async_remote_copy(src_ref=src, dst_ref=dst, send_sem=send_sems.at[b], recv_sem=recv_sems.at[me],
                                     device_id=_device_of(b), device_id_type=MESH).start()


def _rs_finish(src_of, parts_ref, send_sems, recv_sems, local_sem):
    me = _my_index()
    for s in range(NDEV):
        @pl.when(s != me)
        def _():
            cp = pltpu.make_async_remote_copy(src_ref=src_of(s), dst_ref=parts_ref.at[s], send_sem=send_sems.at[s],
                                              recv_sem=recv_sems.at[s], device_id=_device_of(s), device_id_type=MESH)
            cp.wait_send()
            cp.wait_recv()

        @pl.when(s == me)
        def _():
            pltpu.make_async_copy(src_of(s), parts_ref.at[s], local_sem).wait()


_RS_SEMS = [pltpu.SemaphoreType.DMA((NDEV,)), pltpu.SemaphoreType.DMA((NDEV,)), pltpu.SemaphoreType.DMA]
_ANY = pl.BlockSpec(memory_space=pl.ANY)


def _xor_order(me, n):
    return (me ^ (n - 1 - jnp.arange(n, dtype=jnp.int32))).astype(jnp.int32)


NCHIP = NDEV // 2


def _rs2_scratch(half_shape):
    blocks = lambda *lead: pltpu.VMEM(lead + tuple(half_shape), BF16)
    return [blocks(NCHIP, 2), blocks(NCHIP)] + [pltpu.SemaphoreType.DMA((NCHIP,))] * 4 + [pltpu.SemaphoreType.DMA]


def _rs2_to_sibling(q, rs):
    stage, from_sib, d_send, d_recv = rs[:4]
    x, y, c = _my_pos()
    pltpu.make_async_remote_copy(src_ref=stage.at[q, 1 - c], dst_ref=from_sib.at[q], send_sem=d_send.at[q],
                                 recv_sem=d_recv.at[q], device_id=(x, y, 1 - c), device_id_type=MESH).start()


def _rs2_forward(q, parts_ref, rs):
    stage, chip_sum, d_send, d_recv, i_send, i_recv, local_sem = rs
    x, y, c = _my_pos()
    my_chip = 2 * x + y
    pltpu.make_async_remote_copy(src_ref=stage.at[q, c], dst_ref=chip_sum.at[q], send_sem=d_send.at[q],
                                 recv_sem=d_recv.at[q], device_id=(x, y, 1 - c), device_id_type=MESH).wait_recv()
    chip_sum[q] = (stage[q, c].astype(F32) + chip_sum[q].astype(F32)).astype(BF16)

    @pl.when(q == my_chip)
    def _():
        pltpu.make_async_copy(chip_sum.at[q], parts_ref.at[my_chip], local_sem).start()

    @pl.when(q != my_chip)
    def _():
        pltpu.make_async_remote_copy(src_ref=chip_sum.at[q], dst_ref=parts_ref.at[my_chip], send_sem=i_send.at[q],
                                     recv_sem=i_recv.at[my_chip], device_id=((q >> 1) & 1, q & 1, c),
                                     device_id_type=MESH).start()


def _rs2_finish(parts_ref, rs):
    stage, chip_sum, d_send, d_recv, i_send, i_recv, local_sem = rs
    x, y, c = _my_pos()
    my_chip = 2 * x + y
    for q in range(NCHIP):
        pltpu.make_async_remote_copy(src_ref=stage.at[q, 1 - c], dst_ref=chip_sum.at[q], send_sem=d_send.at[q],
                                     recv_sem=d_recv.at[q], device_id=(x, y, 1 - c), device_id_type=MESH).wait_send()

        @pl.when(q != my_chip)
        def _():
            cp = pltpu.make_async_remote_copy(src_ref=chip_sum.at[q], dst_ref=parts_ref.at[q], send_sem=i_send.at[q],
                                              recv_sem=i_recv.at[q], device_id=((q >> 1) & 1, q & 1, c),
                                              device_id_type=MESH)
            cp.wait_send()
            cp.wait_recv()

        @pl.when(q == my_chip)
        def _():
            pltpu.make_async_copy(chip_sum.at[q], parts_ref.at[q], local_sem).wait()


def _gu_wgrad(h2, dgu, order):
    t_len = h2.shape[0]
    tk = min(TK, t_len)
    nk = t_len // tk

    def body(ord_ref, h_ref, d_ref, parts_ref, acc, *rs):
        p, k = pl.program_id(0), pl.program_id(1)

        @pl.when(k == 0)
        def _():
            acc[...] = jnp.zeros_like(acc)

        hb = h_ref[...]
        for half in range(2):
            acc[half] += _dot_tn(d_ref[0, half], hb)

        @pl.when(k == nk - 1)
        def _():
            q = ord_ref[p]
            rs[0][q] = acc[...].astype(BF16)
            _rs2_to_sibling(q, rs)

        @pl.when((k == min(1, nk - 1)) & (p > 0))
        def _():
            _rs2_forward(ord_ref[p - 1], parts_ref, rs)

        @pl.when((p == NCHIP - 1) & (k == nk - 1))
        def _():
            _rs2_forward(ord_ref[p], parts_ref, rs)
            _rs2_finish(parts_ref, rs)

    return pl.pallas_call(
        body, name="gu_wgrad",
        grid_spec=pltpu.PrefetchScalarGridSpec(
            num_scalar_prefetch=1, grid=(NCHIP, nk),
            in_specs=[pl.BlockSpec((tk, D), lambda p, k, o: (k, 0)),
                      pl.BlockSpec((1, 2, tk, FB), lambda p, k, o: (o[p], 0, k, 0))],
            out_specs=_ANY,
            scratch_shapes=[pltpu.VMEM((2, FB, D), F32)] + _rs2_scratch((FB, D))),
        out_shape=jax.ShapeDtypeStruct((NCHIP, FB, D), BF16),
        compiler_params=_cp("arbitrary", "arbitrary"),
    )(order, h2, dgu.reshape(NCHIP, 2, t_len, FB))


def _scaled_wgrad(name, a, dx, w, gate_row, mod, order):
    t_len = dx.shape[0]
    kb = w.shape[0] // NCHIP
    tk = min(TK, t_len)
    nk = t_len // tk
    rows = kb // 2
    if a.ndim == 3:
        a_spec = pl.BlockSpec((None, tk, kb), lambda p, k, o: (o[p], k, 0))
    else:
        a_spec = pl.BlockSpec((tk, kb), lambda p, k, o: (k, o[p]))

    def body(ord_ref, a_ref, dx_ref, w_ref, mod_ref, parts_ref, dg_ref, acc, *rs):
        p, k = pl.program_id(0), pl.program_id(1)

        @pl.when((p == 0) & (k == 0))
        def _():
            dg_ref[...] = jnp.zeros_like(dg_ref)

        @pl.when(k == 0)
        def _():
            acc[...] = jnp.zeros_like(acc)

        acc[...] += _dot_tn(a_ref[...], dx_ref[...])

        @pl.when(k == nk - 1)
        def _():
            q = ord_ref[p]
            z = acc[...]
            zg = (z * mod_ref[gate_row:gate_row + 1, :]).astype(BF16)
            dg_ref[0:1, :] += jnp.sum(z * w_ref[...].astype(F32), axis=0, keepdims=True)
            for half in range(2):
                rs[0][q, half] = zg[half * rows:(half + 1) * rows]
            _rs2_to_sibling(q, rs)

        @pl.when((k == min(1, nk - 1)) & (p > 0))
        def _():
            _rs2_forward(ord_ref[p - 1], parts_ref, rs)

        @pl.when((p == NCHIP - 1) & (k == nk - 1))
        def _():
            _rs2_forward(ord_ref[p], parts_ref, rs)
            _rs2_finish(parts_ref, rs)

    return pl.pallas_call(
        body, name=name,
        grid_spec=pltpu.PrefetchScalarGridSpec(
            num_scalar_prefetch=1, grid=(NCHIP, nk),
            in_specs=[a_spec,
                      pl.BlockSpec((tk, D), lambda p, k, o: (k, 0)),
                      pl.BlockSpec((kb, D), lambda p, k, o: (o[p], 0)),
                      pl.BlockSpec((8, D), lambda p, k, o: (0, 0))],
            out_specs=[_ANY, pl.BlockSpec((8, D), lambda p, k, o: (0, 0))],
            scratch_shapes=[pltpu.VMEM((kb, D), F32)] + _rs2_scratch((rows, D))),
        out_shape=[jax.ShapeDtypeStruct((NCHIP, rows, D), BF16), jax.ShapeDtypeStruct((8, D), F32)],
        compiler_params=_cp("arbitrary", "arbitrary"),
    )(order, a, dx, w, mod)


M_WA, M_WB, M_CBIAS, M_BA, M_BX, M_LS = 0, 3, 7, 8, 9, 10


def _conv_bwd_rows(tt, proj_ref, prm_ref, xe, ve, due, dye, dp_ref, acc8):
    row = lax.broadcasted_iota(jnp.int32, (8, CG), 0)
    w_b = [prm_ref[P_WB + k:P_WB + k + 1, :] for k in range(4)]
    w_a = [prm_ref[P_WA + k:P_WA + k + 1, :] for k in range(3)]

    def blk(ib, carry):
        r0 = pl.multiple_of(ib * 16, 16)
        rows = pl.ds(r0, 16)
        for g in range(D // CG):
            cs = slice(g * CG, (g + 1) * CG)
            du16, du_after = due[rows, cs], due[pl.ds(r0 + 16, 8), cs]
            dy16, dy_after = dye[rows, cs], dye[pl.ds(r0 + 16, 8), cs]
            cc16 = _pj(proj_ref, 1, rows, cs).astype(F32)
            cx16 = _pj(proj_ref, 2, rows, cs).astype(F32)
            x16 = _pj(proj_ref, 3, rows, cs).astype(F32)
            v16 = cc16 * cx16
            xp, vp = xe[pl.ds(r0, 8), cs], ve[pl.ds(r0, 8), cs]
            xe[pl.ds(r0 + 16, 8), cs] = x16[8:16]
            ve[pl.ds(r0 + 16, 8), cs] = v16[8:16]
            acc = [acc8[8 * k:8 * k + 8, cs] for k in range(8)]
            drx, dv = [], []
            for sb in range(2):
                lo = slice(8 * sb, 8 * sb + 8)
                duc, dyc, xc, vc = du16[lo], dy16[lo], x16[lo], v16[lo]
                du_n = du16[8:16] if sb == 0 else du_after
                dy_n = dy16[8:16] if sb == 0 else dy_after
                acc[0] = acc[0] + duc
                acc[4] = acc[4] + duc * xc
                d8 = w_b[3][:, cs] * duc
                for s in (1, 2, 3):
                    acc[4 - s] = acc[4 - s] + duc * _shift_down(xc, xp, s, row)
                    d8 = d8 + w_b[3 - s][:, cs] * _shift_up(duc, du_n, s, row)
                acc[7] = acc[7] + dyc * vc
                e8 = w_a[2][:, cs] * dyc
                for s in (1, 2):
                    acc[7 - s] = acc[7 - s] + dyc * _shift_down(vc, vp, s, row)
                    e8 = e8 + w_a[2 - s][:, cs] * _shift_up(dyc, dy_n, s, row)
                drx.append(d8)
                dv.append(e8)
                xp, vp = xc, vc
            for k in range(8):
                acc8[8 * k:8 * k + 8, cs] = acc[k]
            dv16 = jnp.concatenate(dv, axis=0)
            col = lambda s: slice(s * D + g * CG, s * D + (g + 1) * CG)
            dp_ref[rows, col(3)] = jnp.concatenate(drx, axis=0).astype(BF16)
            dp_ref[rows, col(1)] = (dv16 * cx16).astype(BF16)
            dp_ref[rows, col(2)] = (dv16 * cc16).astype(BF16)
        return carry

    lax.fori_loop(0, tt // 16, blk, 0)


def _mixer_bwd(proj, hl, sv, dmg, prm, wa, wx):
    t_len = proj.shape[0]
    tt = min(TT, t_len)
    nt = t_len // tt
    hb8 = tt // 8

    def rev(i):
        return nt - 1 - i

    def halo(i):
        return jnp.maximum(rev(i) * hb8 - 1, 0)

    def body(proj_ref, ph_ref, hl_ref, hh_ref, sv_ref, dmg_ref, prm_ref, wa_ref, wx_ref,
             dp_ref, sums_ref, gwa_ref, gwx_ref,
             xe, ve, he, due, dye, drp_s, dip_s, an, gn, acc8):
        i = pl.program_id(0)
        t = rev(i)

        @pl.when(i == 0)
        def _():
            sums_ref[...] = jnp.zeros_like(sums_ref)
            gwa_ref[...] = jnp.zeros_like(gwa_ref)
            gwx_ref[...] = jnp.zeros_like(gwx_ref)
            due[tt:tt + 8, :] = jnp.zeros((8, D), F32)
            dye[tt:tt + 8, :] = jnp.zeros((8, D), F32)
            an[...] = jnp.zeros((8, D), F32)
            gn[...] = jnp.zeros((8, D), F32)

        live = (t > 0).astype(F32)
        xe[0:8, :] = _pj(ph_ref, 3).astype(F32) * live
        ve[0:8, :] = _pj(ph_ref, 1).astype(F32) * _pj(ph_ref, 2).astype(F32) * live
        he[0:8, :] = hh_ref[...].astype(F32) * live
        he[8:8 + tt, :] = hl_ref[...].astype(F32)

        ls_all = _log_sigmoid(prm_ref[P_LAM:P_LAM + 1, :])
        row = lax.broadcasted_iota(jnp.int32, (8, CG), 0)
        nblk = tt // 16

        def blk(ib, carry):
            r0 = pl.multiple_of((nblk - 1 - ib) * 16, 16)
            rows = pl.ds(r0, 16)
            for g in range(D // CG):
                cs = slice(g * CG, (g + 1) * CG)
                ls = ls_all[:, cs]
                dm = dmg_ref[rows, cs].astype(F32)
                cb = _pj(proj_ref, 0, rows, cs).astype(F32)
                rg = _pj(proj_ref, 4, rows, cs).astype(F32)
                sga = _sig(_pj(proj_ref, 5, rows, cs).astype(F32))
                sgb = _sig(_pj(proj_ref, 6, rows, cs).astype(F32))
                ya0 = sv_ref[SV_YA, rows, cs]
                h16 = he[pl.ds(r0 + 8, 16), cs]
                gl, th = _gelu(rg)
                dgl = 0.5 * (1.0 + th) + 0.5 * rg * (1.0 - th * th) * (_GC * (1.0 + 3.0 * 0.044715 * rg * rg))
                y_a = cb * ya0
                y_b = h16 * gl
                dy_a = dm * sga
                dy_b = dm * sgb
                col = lambda s: slice(s * D + g * CG, s * D + (g + 1) * CG)
                dp_ref[rows, col(5)] = (dm * y_a * sga * (1.0 - sga)).astype(BF16)
                dp_ref[rows, col(6)] = (dm * y_b * sgb * (1.0 - sgb)).astype(BF16)
                dp_ref[rows, col(4)] = (dy_b * h16 * dgl).astype(BF16)
                dp_ref[rows, col(0)] = (dy_a * ya0).astype(BF16)
                dye[rows, cs] = dy_a * cb
                dh16 = dy_b * gl

                a_next = an[:, cs]
                g_next = gn[:, cs]
                s_ba = jnp.zeros((8, CG), F32)
                s_bx = jnp.zeros((8, CG), F32)
                s_ls = jnp.zeros((8, CG), F32)
                for sb in (1, 0):
                    rr = r0 + 8 * sb
                    first = (row + (t * tt + rr)) == 0
                    r8 = pl.ds(rr, 8)
                    uu, r, ig, a, mult = (sv_ref[pln, r8, cs] for pln in (SV_U, SV_R, SV_I, SV_A, SV_MULT))
                    ca = jnp.where(row < 7, pltpu.roll(a, 7, 0), a_next)
                    cb_ = dh16[8 * sb:8 * sb + 8, :]
                    for s in (1, 2, 4):
                        a_sh = jnp.where(row < 8 - s, pltpu.roll(ca, 8 - s, 0), 1.0)
                        b_sh = jnp.where(row < 8 - s, pltpu.roll(cb_, 8 - s, 0), 0.0)
                        cb_ = ca * b_sh + cb_
                        ca = ca * a_sh
                    gv = ca * g_next + cb_
                    g_next = jnp.broadcast_to(gv[0:1, :], gv.shape)
                    a_next = jnp.broadcast_to(a[0:1, :], a.shape)
                    hprev = jnp.where(row >= 1, pltpu.roll(he[pl.ds(rr + 8, 8), cs], 1, 0),
                                      pltpu.roll(he[pl.ds(rr, 8), cs], 1, 0))
                    da = gv * hprev
                    dmult = jnp.where(first, 0.0, gv * ig * uu)
                    dla = da * a + jnp.where(mult > 0.0, dmult * (-(a * a) / mult), 0.0)
                    drp = dla * (LRU_C * ls) * r * (1.0 - r)
                    dip = gv * mult * uu * ig * (1.0 - ig)
                    s_ls = s_ls + dla * (LRU_C * r)
                    s_ba = s_ba + drp
                    s_bx = s_bx + dip
                    drp_s[pl.ds(rr, 8), cs] = drp
                    dip_s[pl.ds(rr, 8), cs] = dip
                    due[pl.ds(rr, 8), cs] = gv * mult * ig
                an[:, cs] = a_next
                gn[:, cs] = g_next
                sums_ref[M_BA:M_BA + 1, cs] += jnp.sum(s_ba, axis=0, keepdims=True)
                sums_ref[M_BX:M_BX + 1, cs] += jnp.sum(s_bx, axis=0, keepdims=True)
                sums_ref[M_LS:M_LS + 1, cs] += jnp.sum(s_ls, axis=0, keepdims=True)
            return carry

        lax.fori_loop(0, nblk, blk, 0)

        drp_b = drp_s[...].astype(BF16)
        dip_b = dip_s[...].astype(BF16)
        ub = sv_ref[SV_U].astype(BF16)
        for h in range(HEADS):
            cs = slice(h * HB, (h + 1) * HB)
            due[0:tt, cs] += _dot_nt(drp_b[:, cs], wa_ref[h]) + _dot_nt(dip_b[:, cs], wx_ref[h])
            gwa_ref[h] += _dot_tn(ub[:, cs], drp_b[:, cs])
            gwx_ref[h] += _dot_tn(ub[:, cs], dip_b[:, cs])

        acc8[...] = jnp.zeros_like(acc8)
        _conv_bwd_rows(tt, proj_ref, prm_ref, xe, ve, due, dye, dp_ref, acc8)
        for k, dst in enumerate([M_CBIAS] + [M_WB + k for k in range(4)] + [M_WA + k for k in range(3)]):
            sums_ref[dst:dst + 1, :] += jnp.sum(acc8[8 * k:8 * k + 8, :], axis=0, keepdims=True)
        due[tt:tt + 8, :] = due[0:8, :]
        dye[tt:tt + 8, :] = dye[0:8, :]

        @pl.when(i == nt - 1)
        def _():
            sums_ref[M_LS:M_LS + 1, :] = sums_ref[M_LS:M_LS + 1, :] * _sig(-prm_ref[P_LAM:P_LAM + 1, :])

    big = lambda: pltpu.VMEM((tt + 8, D), F32)
    tile = lambda: pltpu.VMEM((tt, D), F32)
    return pl.pallas_call(
        body, name="mixer_bwd", grid=(nt,),
        in_specs=[pl.BlockSpec((tt, 7 * D), lambda i: (rev(i), 0)),
                  pl.BlockSpec((8, 7 * D), lambda i: (halo(i), 0)),
                  pl.BlockSpec((tt, D), lambda i: (rev(i), 0)),
                  pl.BlockSpec((8, D), lambda i: (halo(i), 0)),
                  pl.BlockSpec((len(SV_PLANES), tt, D), lambda i: (0, rev(i), 0)),
                  pl.BlockSpec((tt, D), lambda i: (rev(i), 0)),
                  pl.BlockSpec((16, D), lambda i: (0, 0)),
                  pl.BlockSpec((HEADS, HB, HB), lambda i: (0, 0, 0)),
                  pl.BlockSpec((HEADS, HB, HB), lambda i: (0, 0, 0))],
        out_specs=[pl.BlockSpec((tt, 7 * D), lambda i: (rev(i), 0)),
                   pl.BlockSpec((16, D), lambda i: (0, 0)),
                   pl.BlockSpec((HEADS, HB, HB), lambda i: (0, 0, 0)),
                   pl.BlockSpec((HEADS, HB, HB), lambda i: (0, 0, 0))],
        out_shape=[jax.ShapeDtypeStruct((t_len, 7 * D), BF16), jax.ShapeDtypeStruct((16, D), F32),
                   jax.ShapeDtypeStruct((HEADS, HB, HB), F32), jax.ShapeDtypeStruct((HEADS, HB, HB), F32)],
        scratch_shapes=[big(), big(), big(), big(), big(), tile(), tile(),
                        pltpu.VMEM((8, D), F32), pltpu.VMEM((8, D), F32), pltpu.VMEM((64, D), F32)],
        compiler_params=_cp("arbitrary"),
    )(proj, proj, hl, hl, sv, dmg, prm, wa, wx)


def _in_proj_bwd(dproj, w_in, x, dx1, mod, g_mix):
    t_len = x.shape[0]
    tm = min(TM, t_len)

    def body(dp_ref, w_ref, x_ref, dx1_ref, mod_ref, g_ref, gx_ref, sums_ref, acc):
        @pl.when(pl.program_id(0) == 0)
        def _():
            sums_ref[...] = jnp.zeros_like(sums_ref)

        def write(rows, dx):
            gx_ref[rows, :] = dx

        zero = jnp.zeros((8, D), F32)
        sums = (zero, zero, zero)
        for sub in _sub_blocks(tm):
            acc[sub, :] = _dot_nt(dp_ref[sub, :], w_ref[...])
            sums = _norm_bwd_rows(sub, sums, acc, x_ref, dx1_ref, mod_ref[1:2, :], g_ref[...], write)
        _add_norm_sums(sums_ref, sums)

    return pl.pallas_call(
        body, name="in_proj_bwd", grid=(t_len // tm,),
        in_specs=[pl.BlockSpec((tm, 7 * D), lambda i: (i, 0)),
                  _resident((D, 7 * D)),
                  pl.BlockSpec((tm, D), lambda i: (i, 0)), pl.BlockSpec((tm, D), lambda i: (i, 0)),
                  pl.BlockSpec((8, D), lambda i: (0, 0)), pl.BlockSpec((1, D), lambda i: (0, 0))],
        out_specs=[pl.BlockSpec((tm, D), lambda i: (i, 0)), pl.BlockSpec((8, D), lambda i: (0, 0))],
        out_shape=[jax.ShapeDtypeStruct((t_len, D), F32), jax.ShapeDtypeStruct((8, D), F32)],
        scratch_shapes=[pltpu.VMEM((tm, D), F32)],
        compiler_params=_cp("arbitrary"),
    )(dproj, w_in, x, dx1, mod, g_mix)


def _in_wgrad(h, dproj, g_wa, g_wx, order):
    t_len = h.shape[0]
    tk = min(TKI, t_len)
    nk = t_len // tk
    cw = 7 * D // NDEV
    hr = HB // NDEV

    def body(ord_ref, h_ref, d_ref, ga_ref, gx_ref, parts_ref, pa_ref, px_ref, acc, *scr):
        rs, sems = scr[:-6], scr[-6:]
        p, k = pl.program_id(0), pl.program_id(1)

        def head_rows(ref):
            return lambda s: ref.at[:, pl.ds(s * hr, hr), :]

        @pl.when((p == 0) & (k == 0))
        def _():
            for s in range(NDEV):
                _rs_send(head_rows(ga_ref)(s), pa_ref, s, *sems[0:3])
                _rs_send(head_rows(gx_ref)(s), px_ref, s, *sems[3:6])

        @pl.when(k == 0)
        def _():
            acc[...] = jnp.zeros_like(acc)

        acc[...] += _dot_tn(h_ref[...], d_ref[...])

        @pl.when(k == nk - 1)
        def _():
            q = ord_ref[p]
            for half in range(2):
                rs[0][q, half] = acc[:, half * cw:(half + 1) * cw].astype(BF16)
            _rs2_to_sibling(q, rs)

        @pl.when((k == min(1, nk - 1)) & (p > 0))
        def _():
            _rs2_forward(ord_ref[p - 1], parts_ref, rs)

        @pl.when((p == NCHIP - 1) & (k == nk - 1))
        def _():
            _rs2_forward(ord_ref[p], parts_ref, rs)
            _rs2_finish(parts_ref, rs)
            _rs_finish(head_rows(ga_ref), pa_ref, *sems[0:3])
            _rs_finish(head_rows(gx_ref), px_ref, *sems[3:6])

    return pl.pallas_call(
        body, name="in_wgrad",
        grid_spec=pltpu.PrefetchScalarGridSpec(
            num_scalar_prefetch=1, grid=(NCHIP, nk),
            in_specs=[pl.BlockSpec((tk, D), lambda p, k, o: (k, 0)),
                      pl.BlockSpec((tk, 2 * cw), lambda p, k, o: (k, o[p])), _ANY, _ANY],
            out_specs=[_ANY, _ANY, _ANY],
            scratch_shapes=[pltpu.VMEM((D, 2 * cw), F32)] + _rs2_scratch((D, cw)) + _RS_SEMS * 2),
        out_shape=[jax.ShapeDtypeStruct((NCHIP, D, cw), BF16), jax.ShapeDtypeStruct((NDEV, HEADS, hr, HB), F32),
                   jax.ShapeDtypeStruct((NDEV, HEADS, hr, HB), F32)],
        compiler_params=_cp("arbitrary", "arbitrary"),
    )(order, h, dproj, g_wa, g_wx)


def _adam_math(w, g, m, v):
    m = ADAM_B1 * m + (1.0 - ADAM_B1) * g
    v = ADAM_B2 * v + (1.0 - ADAM_B2) * (g * g)
    m_hat = m / (1.0 - ADAM_B1 ** ADAM_STEP)
    v_hat = v / (1.0 - ADAM_B2 ** ADAM_STEP)
    delta = -ADAM_LR * (m_hat / (jnp.sqrt(v_hat) + ADAM_EPS) + ADAM_WD * w)
    return delta, m, v


def _ada_bwd(c_all, dmod_cols, w, m, v):
    rb = 256
    n = w.shape[1]
    nrow = c_all.shape[0]

    def body(c_ref, d_ref, w_ref, m_ref, v_ref, g_ref, dl_ref, nm_ref, nv_ref):
        cv = c_ref[...]
        g = _dot_tn((cv * _sig(cv)).astype(BF16), d_ref[...].astype(BF16))
        g_ref[...] = g
        dl_ref[...], nm_ref[...], nv_ref[...] = _adam_math(w_ref[...], g, m_ref[...], v_ref[...])

    blk = pl.BlockSpec((rb, n), lambda i: (i, 0))
    sds = jax.ShapeDtypeStruct(w.shape, F32)
    return pl.pallas_call(
        body, name="ada_bwd", grid=(D // rb,),
        in_specs=[pl.BlockSpec((nrow, rb), lambda i: (0, i)), pl.BlockSpec((nrow, n), lambda i: (0, 0)), blk, blk, blk],
        out_specs=[blk, blk, blk, blk], out_shape=[sds, sds, sds, sds],
        compiler_params=_cp("parallel"),
    )(c_all, dmod_cols, w, m, v)


def _adam(name, parts, w, m, v):
    p, r, c = parts.shape
    rb = r
    for cand in (256, 128, 64, 32, 16, 8):
        if r % cand == 0 and r >= cand:
            rb = cand
            break

    def body(p_ref, w_ref, m_ref, v_ref, g_ref, dl_ref, nm_ref, nv_ref):
        g = p_ref[0].astype(F32)
        for q in range(1, p):
            g = g + p_ref[q].astype(F32)
        g_ref[...] = g
        dl_ref[...], nm_ref[...], nv_ref[...] = _adam_math(w_ref[...], g, m_ref[...], v_ref[...])

    blk = pl.BlockSpec((rb, c), lambda i: (i, 0))
    sds = jax.ShapeDtypeStruct((r, c), F32)
    return pl.pallas_call(
        body, name=name, grid=(r // rb,),
        in_specs=[pl.BlockSpec((p, rb, c), lambda i: (0, i, 0)), blk, blk, blk],
        out_specs=[blk, blk, blk, blk], out_shape=[sds, sds, sds, sds],
        compiler_params=_cp("parallel"),
    )(parts, w, m, v)


_SMALL_SEMS = [pltpu.SemaphoreType.DMA((7,)), pltpu.SemaphoreType.DMA((7,)), pltpu.SemaphoreType.DMA]
_VMEM = pl.BlockSpec(memory_space=pltpu.VMEM)


def _exchange_small(x_ref, out_ref, send_sems, recv_sems, local_sem):
    m_per = x_ref.shape[0]
    x, y, c = _my_pos()
    me, sibling = (x, y, c), (x, y, 1 - c)
    chips = [(1 - x, y), (x, 1 - y), (1 - x, 1 - y)]

    def rows(px, py, pc):
        return out_ref.at[pl.ds((4 * px + 2 * py + pc) * m_per, m_per), :]

    def copy(k, block, to, src=None):
        return pltpu.make_async_remote_copy(
            src_ref=rows(*block) if src is None else src, dst_ref=rows(*block),
            send_sem=send_sems.at[k], recv_sem=recv_sems.at[k], device_id=to, device_id_type=MESH)

    mine = pltpu.make_async_copy(x_ref, rows(*me), local_sem)
    mine.start()
    first = [copy(0, me, sibling, src=x_ref)]
    first += [copy(1 + j, me, (*chip, c), src=x_ref) for j, chip in enumerate(chips)]
    for cp in first:
        cp.start()
    passed = [copy(4 + j, (*chip, c), sibling) for j, chip in enumerate(chips)]
    for j, chip in enumerate(chips):
        copy(1 + j, (*chip, c), me).wait_recv()
        passed[j].start()
    copy(0, sibling, me).wait_recv()
    for j, chip in enumerate(chips):
        copy(4 + j, (*chip, 1 - c), me).wait_recv()
    for cp in first + passed:
        cp.wait_send()
    mine.wait()


def _all_gather_small(name, v):
    def body(x_ref, out_ref, send_sems, recv_sems, local_sem):
        _exchange_small(x_ref, out_ref, send_sems, recv_sems, local_sem)

    return pl.pallas_call(
        body, name=name, out_shape=jax.ShapeDtypeStruct((NDEV * v.shape[0], v.shape[1]), v.dtype),
        in_specs=[_VMEM], out_specs=_VMEM, scratch_shapes=_SMALL_SEMS,
    )(v)


def _ada_mod(pack, w_ada, b_cols):
    ncol = w_ada.shape[1]

    def body(p_ref, w_ref, b_ref, all_ref, mod_ref, cols, *sems):
        _exchange_small(p_ref, all_ref, *sems[0:3])
        c_all = jnp.concatenate([all_ref[8 * d:8 * d + 1, 0:D] for d in range(NDEV)], axis=0)
        c16 = jnp.concatenate([c_all, jnp.zeros_like(c_all)], axis=0)
        mod16 = _dot((c16 * _sig(c16)).astype(BF16), w_ref[...].astype(BF16)) + b_ref[...]
        cols[...] = mod16[0:NDEV]
        _exchange_small(cols, mod_ref, *sems[3:6])

    return pl.pallas_call(
        body, name="ada_mod",
        out_shape=[jax.ShapeDtypeStruct((NDEV * 8, pack.shape[1]), F32), jax.ShapeDtypeStruct((NDEV * 8, ncol), F32)],
        in_specs=[_VMEM, _VMEM, _VMEM], out_specs=[_VMEM, _VMEM],
        scratch_shapes=[pltpu.VMEM((NDEV, ncol), F32)] + _SMALL_SEMS * 2,
        compiler_params=_cp(),
    )(pack, w_ada, b_cols)


def _blk_rows(n):
    return lambda ref, b: ref.at[pl.ds(pl.multiple_of(b * n, 8), n), :]


def _blk_lead(ref, b):
    return ref.at[b]


def _blk_heads(ref, b):
    return ref.at[:, pl.ds(pl.multiple_of(b * (HB // NDEV), 8), HB // NDEV), :]


def _ag_phases(ins, outs, slicers, send_sems, recv_sems, local_sems):
    na = len(ins)
    x, y, c = _my_pos()
    me, sibling = (x, y, c), (x, y, 1 - c)
    chips = [(1 - x, y), (x, 1 - y), (1 - x, 1 - y)]

    def copy(a, k, block, to, from_shard=False):
        px, py, pc = block
        dst = slicers[a](outs[a], 4 * px + 2 * py + pc)
        return pltpu.make_async_remote_copy(
            src_ref=ins[a] if from_shard else dst, dst_ref=dst,
            send_sem=send_sems.at[a * 7 + k], recv_sem=recv_sems.at[a * 7 + k], device_id=to, device_id_type=MESH)

    def local(a):
        return pltpu.make_async_copy(ins[a], slicers[a](outs[a], 4 * x + 2 * y + c), local_sems.at[a])

    def firsts(a):
        return [copy(a, 0, me, sibling, True)] + [copy(a, 1 + j, me, (*chip, c), True) for j, chip in enumerate(chips)]

    def start():
        for a in range(na):
            local(a).start()
            for cp in firsts(a):
                cp.start()

    def forward():
        for a in range(na):
            for j, chip in enumerate(chips):
                copy(a, 1 + j, (*chip, c), me).wait_recv()
                copy(a, 4 + j, (*chip, c), sibling).start()

    def finish():
        for a in range(na):
            copy(a, 0, sibling, me).wait_recv()
            for j, chip in enumerate(chips):
                copy(a, 4 + j, (*chip, 1 - c), me).wait_recv()
        for a in range(na):
            for cp in firsts(a) + [copy(a, 4 + j, (*chip, c), sibling) for j, chip in enumerate(chips)]:
                cp.wait_send()
            local(a).wait()

    return start, forward, finish


def _ag_sems(na):
    return [pltpu.SemaphoreType.DMA((7 * na,)), pltpu.SemaphoreType.DMA((7 * na,)), pltpu.SemaphoreType.DMA((na,))]


def _local_step(x, target, mod, g_mix, g_ffn, g_fin, prm, w_in_shard, shards):
    fulls = [(HEADS, HB, HB), (HEADS, HB, HB), (D, D), (NDEV, FB, D), (DFF, D)]
    slicers = [_blk_heads, _blk_heads, _blk_rows(D // NDEV), _blk_lead, _blk_rows(DFF // NDEV)]
    my_chip = _my_index() >> 1
    own_first = (my_chip ^ jnp.arange(NCHIP, dtype=jnp.int32)).astype(jnp.int32)
    early, late = [0, 1, 2, 4], [3]
    pick = lambda lst, idx: [lst[i] for i in idx]
    proj, h, w_in, (wa, wx, w_out, w_down) = _in_proj(x, mod, g_mix, w_in_shard, own_first, pick(shards, early),
                                                      pick(fulls, early), pick(slicers, early))
    merged, hl, sv, (w_gu,) = _mixer_fwd(proj, prm, wa, wx, pick(shards, late), pick(fulls, late),
                                         pick(slicers, late))
    w_gu = w_gu.reshape(2, 4, FB, D)
    x1, h2 = _out_proj(merged, x, mod, g_ffn, w_out)
    gu, dx2, dx2b, loss, d_gfin = _ffn_fwd(h2, x1, target, mod, g_fin, w_gu, w_down)
    dgu, act, dx1, dx1b, dmg, sums2 = _ffn_bwd(dx2, gu, x1, mod, g_ffn, w_gu, w_down, w_out)
    chip_order = _xor_order(_my_index() >> 1, NCHIP)
    p_wgu = _gu_wgrad(h2, dgu, chip_order)
    p_wdown, d_gt2 = _scaled_wgrad("down_wgrad", act, dx2b, w_down, 5, mod, chip_order)
    p_wout, d_gt1 = _scaled_wgrad("out_wgrad", merged, dx1b, w_out, 2, mod, chip_order)
    dproj, msums, g_wa, g_wx = _mixer_bwd(proj, hl, sv, dmg, prm, wa, wx)
    p_win, p_wa, p_wx = _in_wgrad(h, dproj, g_wa, g_wx, chip_order)
    grad_x, sums1 = _in_proj_bwd(dproj, w_in, x, dx1, mod, g_mix)
    return dict(loss=loss, grad_x=grad_x, d_gfin=d_gfin, sums1=sums1, sums2=sums2, msums=msums,
                d_gt1=d_gt1[0:1], d_gt2=d_gt2[0:1], p_win=p_win, p_wa=p_wa, p_wx=p_wx, p_wout=p_wout, p_wgu=p_wgu,
                p_wdown=p_wdown)


def kernel(x, c, w_ada, b_ada, g_norm_mix, w_in, conv_a_w, conv_b_w, conv_b_bias, w_rg_a, b_rg_a, w_rg_x, b_rg_x, lru_lambda, w_out, g_norm_ffn, w_gate_up, w_down, g_norm_final, loss_target, m_w_ada, m_b_ada, m_g_norm_mix, m_w_in, m_conv_a_w, m_conv_b_w, m_conv_b_bias, m_w_rg_a, m_b_rg_a, m_w_rg_x, m_b_rg_x, m_lru_lambda, m_w_out, m_g_norm_ffn, m_w_gate_up, m_w_down, m_g_norm_final, v_w_ada, v_b_ada, v_g_norm_mix, v_w_in, v_conv_a_w, v_conv_b_w, v_conv_b_bias, v_w_rg_a, v_b_rg_a, v_w_rg_x, v_b_rg_x, v_lru_lambda, v_w_out, v_g_norm_ffn, v_w_gate_up, v_w_down, v_g_norm_final):
    me = 4 * lax.axis_index("x") + 2 * lax.axis_index("y") + lax.axis_index("c")
    ncol = w_ada.shape[2]
    cw = conv_a_w.shape[2]

    pack0 = jnp.concatenate([c, conv_a_w.reshape(1, 3 * cw), conv_b_w.reshape(1, 4 * cw)], axis=1)
    b_cols = lax.dynamic_slice_in_dim(b_ada, me * ncol, ncol, axis=1)
    got0, got1 = _ada_mod(jnp.broadcast_to(pack0, (8, pack0.shape[1])), w_ada[0], b_cols)
    got0 = got0.reshape(NDEV, 8, -1)[:, 0, :]
    c_all = got0[:, :D]
    conv_a = got0[:, D:D + 3 * cw].reshape(NDEV, 3, cw).transpose(1, 0, 2).reshape(3, D)
    conv_b = got0[:, D + 3 * cw:].reshape(NDEV, 4, cw).transpose(1, 0, 2).reshape(4, D)
    c16 = jnp.concatenate([c_all, jnp.zeros((8, D), F32)], axis=0)
    mod6 = lax.dynamic_index_in_dim(got1.reshape(NDEV, NDEV, ncol), me, axis=1, keepdims=False).reshape(6, D)
    mod = jnp.concatenate([mod6, jnp.zeros((2, D), F32)], axis=0)

    tr = lambda a: jnp.swapaxes(a, 1, 2)
    shards = [w_rg_a[0].astype(BF16), w_rg_x[0].astype(BF16), w_out[0].astype(BF16), tr(w_gate_up)[0].astype(BF16),
              w_down[0].astype(BF16)]

    prm = jnp.concatenate([conv_a, conv_b, conv_b_bias, b_rg_a, b_rg_x, lru_lambda, jnp.zeros((5, D), F32)], axis=0)
    r = _local_step(x[0], loss_target[0], mod, g_norm_mix, g_norm_ffn, g_norm_final.reshape(1, D), prm,
                    w_in[0].astype(BF16), shards)

    parts = [r["p_win"], r["p_wa"], r["p_wx"], r["p_wout"], r["p_wgu"], r["p_wdown"]]
    big = {}
    for nm, p, w, m, v in (("w_in", parts[0], w_in, m_w_in, v_w_in), ("w_rg_a", parts[1], w_rg_a, m_w_rg_a, v_w_rg_a),
                           ("w_rg_x", parts[2], w_rg_x, m_w_rg_x, v_w_rg_x), ("w_out", parts[3], w_out, m_w_out, v_w_out),
                           ("w_gate_up", parts[4], tr(w_gate_up), tr(m_w_gate_up), tr(v_w_gate_up)),
                           ("w_down", parts[5], w_down, m_w_down, v_w_down)):
        two_d = (-1, w.shape[-1])
        outs = _adam("adam_" + nm, p.reshape((p.shape[0],) + w.reshape(two_d).shape), w.reshape(two_d), m.reshape(two_d),
                     v.reshape(two_d))
        big[nm] = [o.reshape(w.shape) for o in outs]
    big["w_gate_up"] = [tr(o) for o in big["w_gate_up"]]

    small = jnp.concatenate([
        r["sums1"][S_SH:S_SH + 1], r["sums1"][S_SC:S_SC + 1], r["d_gt1"],
        r["sums2"][S_SH:S_SH + 1], r["sums2"][S_SC:S_SC + 1], r["d_gt2"],
        r["sums1"][S_G:S_G + 1],
        r["msums"][M_CBIAS:M_CBIAS + 1], r["msums"][M_BA:M_BA + 1], r["msums"][M_BX:M_BX + 1],
        r["msums"][M_LS:M_LS + 1],
        r["sums2"][S_G:S_G + 1], r["d_gfin"],
        r["msums"][M_WA:M_WA + 3], r["msums"][M_WB:M_WB + 4],
        jnp.broadcast_to(r["loss"][0:1, 0:1], (1, D)),
        jnp.zeros((3, D), F32)], axis=0)
    got2 = _all_gather_small("gather_small", small).reshape(NDEV, 24, D)

    rep_w = jnp.concatenate([b_ada.reshape(6, D), g_norm_mix, conv_b_bias, b_rg_a, b_rg_x, lru_lambda, g_norm_ffn,
                             g_norm_final.reshape(1, D), jnp.zeros((3, D), F32)], axis=0)
    rep_m = jnp.concatenate([m_b_ada.reshape(6, D), m_g_norm_mix, m_conv_b_bias, m_b_rg_a, m_b_rg_x, m_lru_lambda,
                             m_g_norm_ffn, m_g_norm_final.reshape(1, D), jnp.zeros((3, D), F32)], axis=0)
    rep_v = jnp.concatenate([v_b_ada.reshape(6, D), v_g_norm_mix, v_conv_b_bias, v_b_rg_a, v_b_rg_x, v_lru_lambda,
                             v_g_norm_ffn, v_g_norm_final.reshape(1, D), jnp.ones((3, D), F32)], axis=0)
    rep = _adam("adam_rep", got2[:, :16, :], rep_w, rep_m, rep_v)

    conv_parts = lax.dynamic_slice_in_dim(got2[:, 13:21, :], me * cw, cw, axis=2)
    cv_w = jnp.concatenate([conv_a_w[0], conv_b_w[0], jnp.zeros((1, cw), F32)], axis=0)
    cv_m = jnp.concatenate([m_conv_a_w[0], m_conv_b_w[0], jnp.zeros((1, cw), F32)], axis=0)
    cv_v = jnp.concatenate([v_conv_a_w[0], v_conv_b_w[0], jnp.ones((1, cw), F32)], axis=0)
    cvo = _adam("adam_conv", conv_parts, cv_w, cv_m, cv_v)

    dmod_cols = lax.dynamic_slice_in_dim(got2[:, :6, :].reshape(NDEV, 6 * D), me * ncol, ncol, axis=1)
    dmod16 = jnp.concatenate([dmod_cols, jnp.zeros((8, ncol), F32)], axis=0)
    ada = _ada_bwd(c16, dmod16, w_ada[0], m_w_ada[0], v_w_ada[0])

    loss = jnp.sum(got2[:, 20, 0])

    def pick(q):
        one = lambda i: rep[q][i:i + 1]
        return [ada[q].reshape(w_ada.shape), rep[q][0:6].reshape(b_ada.shape), one(6), big["w_in"][q],
                cvo[q][0:3].reshape(conv_a_w.shape), cvo[q][3:7].reshape(conv_b_w.shape), one(7),
                big["w_rg_a"][q], one(8), big["w_rg_x"][q], one(9), one(10), big["w_out"][q], one(11),
                big["w_gate_up"][q], big["w_down"][q], rep[q][12]]

    return (loss, r["grad_x"].reshape(x.shape), *pick(0), *pick(1), *pick(2), *pick(3))
```

```python
import math

import jax
import jax.numpy as jnp
from jax import lax
from jax.experimental import pallas as pl
from jax.experimental.pallas import tpu as pltpu

F32 = jnp.float32
BF16 = jnp.bfloat16

D = 1024
DFF = 2816
NDEV = 8
HEADS = 4
HB = D // HEADS
FB = DFF // 4
EPS = 1e-6
LRU_C = 8.0
ADAM_LR, ADAM_B1, ADAM_B2, ADAM_EPS, ADAM_WD, ADAM_STEP = 0.001, 0.9, 0.999, 1e-08, 0.01, 10

VMEM_LIMIT = 56 * 1024 * 1024
TM = 512
TMI = 1024
TMF = 256
TK = 2048
TKI = 2048
SUB = 256
TT = 256
CG = 256
MESH = pl.DeviceIdType.MESH


def _cp(*sem):
    return pltpu.CompilerParams(dimension_semantics=sem, vmem_limit_bytes=VMEM_LIMIT)


def _sig(x):
    return 1.0 / (1.0 + jnp.exp(-x))


def _log_sigmoid(x):
    z = jnp.exp(-jnp.abs(x))
    u = 1.0 + z
    d = u - 1.0
    l1p = jnp.where(d == 0.0, z, jnp.log(u) * (z / jnp.where(d == 0.0, 1.0, d)))
    return -(jnp.maximum(-x, 0.0) + l1p)


def _neg_expm1(x):
    p = x * (1.0 + x * 0.5 * (1.0 + x * (1.0 / 3.0) * (1.0 + x * 0.25 * (1.0 + x * 0.2 * (1.0 + x * (1.0 / 6.0))))))
    return jnp.where(x > -0.25, -p, 1.0 - jnp.exp(x))


_GC = math.sqrt(2.0 / math.pi)


def _gelu(x):
    t = jnp.tanh(_GC * (x + 0.044715 * x * x * x))
    return 0.5 * x * (1.0 + t), t


def _dot(a, b):
    return jnp.dot(a, b, preferred_element_type=F32)


def _dot_nt(a, b):
    return lax.dot_general(a, b, (((1,), (1,)), ((), ())), preferred_element_type=F32)


def _dot_tn(a, b):
    return lax.dot_general(a, b, (((0,), (0,)), ((), ())), preferred_element_type=F32)


def _resident(shape):
    return pl.BlockSpec(shape, lambda *_: (0,) * len(shape), pipeline_mode=pl.Buffered(1))


def _sub_blocks(n_rows):
    step = min(SUB, n_rows)
    return [slice(r, r + step) for r in range(0, n_rows, step)]


def _fold8(v):
    return v[0:8] + v[8:16]


def _pj(ref, s, rows=slice(None), cols=slice(0, D)):
    return ref[rows, s * D + cols.start:s * D + cols.stop]


def _in_proj(x, mod, g_mix, w_shard, order, shards, fulls, slicers):
    t_len = x.shape[0]
    tm = min(TMI, t_len)
    ni = t_len // tm
    na = len(shards)
    cw = 7 * D // NDEV
    rc = 32

    def body(ord_ref, x_ref, mod_ref, g_ref, wsh_ref, *rest):
        ins, (proj_ref, h_ref, wfull_ref), outs = rest[:na], rest[na:na + 3], rest[na + 3:2 * na + 3]
        h_scr, w_scr, wsend, wrecv, wlocal, wout = rest[2 * na + 3:2 * na + 9]
        start, forward, finish = _ag_phases(ins, outs, slicers, *rest[2 * na + 9:])
        p, i = pl.program_id(0), pl.program_id(1)
        x_, y_, c = _my_pos()
        me, sibling = (x_, y_, c), (x_, y_, 1 - c)
        chip_at = [None, (x_, 1 - y_), (1 - x_, y_), (1 - x_, 1 - y_)]

        def cols(px, py, pc):
            return w_scr.at[:, pl.ds(pl.multiple_of((4 * px + 2 * py + pc) * cw, 128), cw)]

        def wcopy(k, block, to, from_shard=False):
            dst = cols(*block)
            return pltpu.make_async_remote_copy(src_ref=wsh_ref if from_shard else dst, dst_ref=dst,
                                                send_sem=wsend.at[k], recv_sem=wrecv.at[k], device_id=to,
                                                device_id_type=MESH)

        own_local = pltpu.make_async_copy(wsh_ref, cols(*me), wlocal)
        to_hbm = pltpu.make_async_copy(w_scr, wfull_ref, wout)

        @pl.when((p == 0) & (i == 0))
        def _():
            own_local.start()
            wcopy(0, me, sibling, True).start()
            for q in (1, 2):
                wcopy(q, me, (*chip_at[q], c), True).start()
            own_local.wait()
            wcopy(0, sibling, me).wait_recv()

        @pl.when((p == 0) & (i == ni // 2))
        def _():
            wcopy(3, me, (*chip_at[3], c), True).start()

        for q in (1, 2, 3):
            @pl.when((p == q - 1) & (i == ni - 1))
            def _():
                wcopy(q, (*chip_at[q], c), me).wait_recv()
                wcopy(3 + q, (*chip_at[q], c), sibling).start()

            @pl.when((p == q) & (i == 0))
            def _():
                wcopy(3 + q, (*chip_at[q], 1 - c), me).wait_recv()

        @pl.when((p == 1) & (i == 0))
        def _():
            start()

        @pl.when((p == NCHIP - 1) & (i == ni // 2))
        def _():
            forward()

        @pl.when((p == NCHIP - 1) & (i == 0))
        def _():
            to_hbm.start()

        gs = g_ref[...] * (1.0 + mod_ref[1:2, :])
        sh = mod_ref[0:1, :]

        wcols = pl.ds(pl.multiple_of(ord_ref[p] * (2 * cw), 128), 2 * cw)
        for sub in _sub_blocks(tm):
            for r0 in range(sub.start, sub.stop, rc):
                xv = x_ref[r0:r0 + rc, :]
                r = lax.rsqrt(jnp.mean(xv * xv, axis=-1, keepdims=True) + EPS)
                h_scr[r0:r0 + rc, :] = (xv * r * gs + sh).astype(BF16)
            proj_ref[sub, :] = _dot(h_scr[sub, :], w_scr[:, wcols]).astype(BF16)

        @pl.when(p == 0)
        def _():
            h_ref[...] = h_scr[...]

        @pl.when((p == NCHIP - 1) & (i == ni - 1))
        def _():
            wcopy(0, me, sibling, True).wait_send()
            for q in (1, 2, 3):
                wcopy(q, me, (*chip_at[q], c), True).wait_send()
                wcopy(3 + q, (*chip_at[q], c), sibling).wait_send()
            finish()
            to_hbm.wait()

    res = pl.pallas_call(
        body, name="in_proj",
        grid_spec=pltpu.PrefetchScalarGridSpec(
            num_scalar_prefetch=1, grid=(NCHIP, ni),
            in_specs=[pl.BlockSpec((tm, D), lambda p, i, o: (i, 0)),
                      pl.BlockSpec((8, D), lambda p, i, o: (0, 0)),
                      pl.BlockSpec((1, D), lambda p, i, o: (0, 0))] + [_ANY] * (1 + na),
            out_specs=[pl.BlockSpec((tm, 2 * cw), lambda p, i, o: (i, o[p])),
                       pl.BlockSpec((tm, D), lambda p, i, o: (jnp.where(p == 0, i, ni - 1), 0))]
            + [_ANY] * (1 + na),
            scratch_shapes=[pltpu.VMEM((tm, D), BF16), pltpu.VMEM((D, 7 * D), BF16),
                            pltpu.SemaphoreType.DMA((7,)), pltpu.SemaphoreType.DMA((7,)),
                            pltpu.SemaphoreType.DMA, pltpu.SemaphoreType.DMA] + _ag_sems(na)),
        out_shape=[jax.ShapeDtypeStruct((t_len, 7 * D), BF16), jax.ShapeDtypeStruct((t_len, D), BF16),
                   jax.ShapeDtypeStruct((D, 7 * D), BF16)]
        + [jax.ShapeDtypeStruct(f, sh.dtype) for f, sh in zip(fulls, shards)],
        compiler_params=_cp("arbitrary", "arbitrary"),
    )(order, x, mod, g_mix, w_shard, *shards)
    return res[0], res[1], res[2], res[3:]


P_WA, P_WB, P_CBIAS, P_BA, P_BX, P_LAM = 0, 3, 7, 8, 9, 10
SV_PLANES = SV_U, SV_YA, SV_R, SV_I, SV_A, SV_MULT = range(6)
SV_PAD = 8


def _lru_gates(rp, ip, ls, first_row):
    r = _sig(rp)
    ig = _sig(ip)
    la = LRU_C * r * ls
    a = jnp.exp(la)
    m2 = _neg_expm1(2.0 * la)
    mult = jnp.where(first_row, 1.0, jnp.sqrt(jnp.maximum(m2, 0.0)))
    return r, ig, la, a, m2, mult


def _shift_down(cur, prev, s, row):
    return jnp.where(row >= s, pltpu.roll(cur, s, 0), pltpu.roll(prev, s, 0))


def _shift_up(cur, nxt, s, row):
    return jnp.where(row < 8 - s, pltpu.roll(cur, 8 - s, 0), pltpu.roll(nxt, 8 - s, 0))


def _conv_fwd_rows(tt, proj_ref, prm_ref, xe, ve, u_s, ub_s, ya_s):
    row = lax.broadcasted_iota(jnp.int32, (8, CG), 0)
    w_b = [prm_ref[P_WB + k:P_WB + k + 1, :] for k in range(4)]
    w_a = [prm_ref[P_WA + k:P_WA + k + 1, :] for k in range(3)]
    bias = prm_ref[P_CBIAS:P_CBIAS + 1, :]

    def blk(ib, carry):
        r0 = pl.multiple_of(ib * 16, 16)
        rows = pl.ds(r0, 16)
        for g in range(D // CG):
            cs = slice(g * CG, (g + 1) * CG)
            x16 = _pj(proj_ref, 3, rows, cs).astype(F32)
            v16 = _pj(proj_ref, 1, rows, cs).astype(F32) * _pj(proj_ref, 2, rows, cs).astype(F32)
            xp = xe[pl.ds(r0, 8), cs]
            vp = ve[pl.ds(r0, 8), cs]
            xe[pl.ds(r0 + 8, 16), cs] = x16
            ve[pl.ds(r0 + 8, 16), cs] = v16
            us, yas = [], []
            for sb in range(2):
                xc, vc = x16[8 * sb:8 * sb + 8], v16[8 * sb:8 * sb + 8]
                u8 = bias[:, cs] + w_b[3][:, cs] * xc
                for s in (1, 2, 3):
                    u8 = u8 + w_b[3 - s][:, cs] * _shift_down(xc, xp, s, row)
                y8 = w_a[2][:, cs] * vc
                for s in (1, 2):
                    y8 = y8 + w_a[2 - s][:, cs] * _shift_down(vc, vp, s, row)
                us.append(u8)
                yas.append(y8)
                xp, vp = xc, vc
            u16 = jnp.concatenate(us, axis=0)
            u_s[rows, cs] = u16
            ub_s[rows, cs] = u16.astype(BF16)
            ya_s[rows, cs] = jnp.concatenate(yas, axis=0)
        return carry

    lax.fori_loop(0, tt // 16, blk, 0)


def _mixer_fwd(proj, prm, wa, wx, shards, fulls, slicers):
    t_len = proj.shape[0]
    tt = min(TT, t_len)
    nt = t_len // tt
    na = len(shards)

    def body(proj_ref, prm_ref, wa_ref, wx_ref, *rest):
        ins, (mg_ref, hl_ref, sv_hbm), outs = rest[:na], rest[na:na + 3], rest[na + 3:2 * na + 3]
        xe, ve, hc, rp_s, ip_s, ub_s, sv_st, sv_sems = rest[2 * na + 3:2 * na + 11]
        start, forward, finish = _ag_phases(ins, outs, slicers, *rest[2 * na + 11:])
        t = pl.program_id(0)

        slot = t % 2
        sv_ref = sv_st.at[slot]

        def sv_out(tile, sl):
            rows = pl.ds(pl.multiple_of(tile * tt, tt), tt)
            return pltpu.make_async_copy(sv_st.at[sl, :, pl.ds(0, tt), :], sv_hbm.at[:, rows, :], sv_sems.at[sl])

        @pl.when(t >= 2)
        def _():
            sv_out(t - 2, slot).wait()

        @pl.when(t == 0)
        def _():
            start()
            xe[0:8, :] = jnp.zeros((8, D), F32)
            ve[0:8, :] = jnp.zeros((8, D), F32)
            hc[...] = jnp.zeros((8, D), F32)

        @pl.when(t == (3 * nt) // 4)
        def _():
            forward()

        _conv_fwd_rows(tt, proj_ref, prm_ref, xe, ve, sv_ref.at[SV_U], ub_s, sv_ref.at[SV_YA])
        xe[0:8, :] = xe[tt:tt + 8, :]
        ve[0:8, :] = ve[tt:tt + 8, :]

        ub = ub_s[...]
        for h in range(HEADS):
            cs = slice(h * HB, (h + 1) * HB)
            rp_s[:, cs] = _dot(ub[:, cs], wa_ref[h]) + prm_ref[P_BA:P_BA + 1, cs]
            ip_s[:, cs] = _dot(ub[:, cs], wx_ref[h]) + prm_ref[P_BX:P_BX + 1, cs]

        ls_all = _log_sigmoid(prm_ref[P_LAM:P_LAM + 1, :])
        row = lax.broadcasted_iota(jnp.int32, (8, CG), 0)

        def blk(i, carry):
            r0 = pl.multiple_of(i * 16, 16)
            for g in range(D // CG):
                cs = slice(g * CG, (g + 1) * CG)
                ls = ls_all[:, cs]
                hprev = hc[:, cs]
                hs = []
                for sb in range(2):
                    rr = r0 + 8 * sb
                    first = (row + (t * tt + rr)) == 0
                    r8 = pl.ds(rr, 8)
                    r, ig, _, a, _, mult = _lru_gates(rp_s[r8, cs], ip_s[r8, cs], ls, first)
                    for plane, val in ((SV_R, r), (SV_I, ig), (SV_A, a), (SV_MULT, mult)):
                        sv_ref[plane, r8, cs] = val
                    b = mult * (ig * sv_ref[SV_U, r8, cs])
                    for s in (1, 2, 4):
                        a_sh = jnp.where(row >= s, pltpu.roll(a, s, 0), 1.0)
                        b_sh = jnp.where(row >= s, pltpu.roll(b, s, 0), 0.0)
                        b = a * b_sh + b
                        a = a * a_sh
                    hv = a * hprev + b
                    hprev = jnp.broadcast_to(hv[7:8, :], hv.shape)
                    hs.append(hv)
                hc[:, cs] = hprev
                h16 = jnp.concatenate(hs, axis=0)
                rows = pl.ds(r0, 16)
                gl, _ = _gelu(_pj(proj_ref, 4, rows, cs).astype(F32))
                y_b = h16 * gl
                y_a = _pj(proj_ref, 0, rows, cs).astype(F32) * sv_ref[SV_YA, rows, cs]
                mg = (_sig(_pj(proj_ref, 5, rows, cs).astype(F32)) * y_a
                      + _sig(_pj(proj_ref, 6, rows, cs).astype(F32)) * y_b)
                mg_ref[rows, cs] = mg.astype(BF16)
                hl_ref[rows, cs] = h16.astype(BF16)
            return carry

        lax.fori_loop(0, tt // 16, blk, 0)
        sv_out(t, slot).start()

        @pl.when(t == nt - 1)
        def _():
            finish()
            if nt >= 2:
                sv_out(t - 1, 1 - slot).wait()
            sv_out(t, slot).wait()

    res = pl.pallas_call(
        body, name="mixer_fwd", grid=(nt,),
        in_specs=[pl.BlockSpec((tt, 7 * D), lambda t: (t, 0)),
                  pl.BlockSpec((16, D), lambda t: (0, 0)),
                  pl.BlockSpec((HEADS, HB, HB), lambda t: (0, 0, 0)),
                  pl.BlockSpec((HEADS, HB, HB), lambda t: (0, 0, 0))] + [_ANY] * na,
        out_specs=[pl.BlockSpec((tt, D), lambda t: (t, 0)), pl.BlockSpec((tt, D), lambda t: (t, 0)), _ANY]
        + [_ANY] * na,
        out_shape=[jax.ShapeDtypeStruct((t_len, D), BF16), jax.ShapeDtypeStruct((t_len, D), BF16),
                   jax.ShapeDtypeStruct((len(SV_PLANES), t_len, D), F32)]
        + [jax.ShapeDtypeStruct(f, sh.dtype) for f, sh in zip(fulls, shards)],
        scratch_shapes=[pltpu.VMEM((tt + 8, D), F32), pltpu.VMEM((tt + 8, D), F32), pltpu.VMEM((8, D), F32),
                        pltpu.VMEM((tt, D), F32), pltpu.VMEM((tt, D), F32), pltpu.VMEM((tt, D), BF16),
                        pltpu.VMEM((2, len(SV_PLANES), tt + SV_PAD, D), F32), pltpu.SemaphoreType.DMA((2,))]
        + _ag_sems(na),
        compiler_params=_cp("arbitrary"),
    )(proj, prm, wa, wx, *shards)
    return res[0], res[1], res[2], res[3:]


def _out_proj(merged, x, mod, g_ffn, w_out):
    t_len = x.shape[0]
    tm = min(TM, t_len)

    def body(mg_ref, x_ref, mod_ref, g_ref, w_ref, x1_ref, h2_ref):
        gt1 = mod_ref[2:3, :]
        gs = g_ref[...] * (1.0 + mod_ref[4:5, :])
        sh = mod_ref[3:4, :]
        for sub in _sub_blocks(tm):
            x1_ref[sub, :] = x_ref[sub, :] + gt1 * _dot(mg_ref[sub, :], w_ref[...])
            for r0 in range(sub.start, sub.stop, 16):
                x1 = x1_ref[r0:r0 + 16, :]
                r = lax.rsqrt(jnp.mean(x1 * x1, axis=-1, keepdims=True) + EPS)
                h2_ref[r0:r0 + 16, :] = (x1 * r * gs + sh).astype(BF16)

    return pl.pallas_call(
        body, name="out_proj", grid=(t_len // tm,),
        in_specs=[pl.BlockSpec((tm, D), lambda i: (i, 0)), pl.BlockSpec((tm, D), lambda i: (i, 0)),
                  pl.BlockSpec((8, D), lambda i: (0, 0)), pl.BlockSpec((1, D), lambda i: (0, 0)),
                  pl.BlockSpec((D, D), lambda i: (0, 0))],
        out_specs=[pl.BlockSpec((tm, D), lambda i: (i, 0)), pl.BlockSpec((tm, D), lambda i: (i, 0))],
        out_shape=[jax.ShapeDtypeStruct((t_len, D), F32), jax.ShapeDtypeStruct((t_len, D), BF16)],
        compiler_params=_cp("parallel"),
    )(merged, x, mod, g_ffn, w_out)


def _ffn_fwd(h2, x1, target, mod, g_fin, w_gu, w_down):
    t_len = x1.shape[0]
    tm = min(TMF, t_len)

    def body(h2_ref, x1_ref, tg_ref, mod_ref, g_ref, wgu_ref, wd_ref, gu_ref, dx2_ref, dx2b_ref, loss_ref, dg_ref, acc):
        @pl.when(pl.program_id(0) == 0)
        def _():
            loss_ref[...] = jnp.zeros_like(loss_ref)
            dg_ref[...] = jnp.zeros_like(dg_ref)

        hb = h2_ref[...]
        ffn = None
        nxt = (_dot_nt(hb, wgu_ref[0, 0]), _dot_nt(hb, wgu_ref[1, 0]))
        for j in range(4):
            gate, up = nxt
            if j < 3:
                nxt = (_dot_nt(hb, wgu_ref[0, j + 1]), _dot_nt(hb, wgu_ref[1, j + 1]))
            gu_ref[0, j] = gate.astype(BF16)
            gu_ref[1, j] = up.astype(BF16)
            act = (gate * _sig(gate) * up).astype(BF16)
            part = _dot(act, wd_ref[j * FB:(j + 1) * FB, :])
            ffn = part if ffn is None else ffn + part
        acc[...] = ffn

        gt2 = mod_ref[5:6, :]
        gf = g_ref[...]

        s_loss = s_dg = jnp.zeros((8, D), F32)
        for r0 in range(0, tm, 16):
            rows = slice(r0, r0 + 16)
            x2 = x1_ref[rows, :] + gt2 * acc[rows, :]
            r = lax.rsqrt(jnp.mean(x2 * x2, axis=-1, keepdims=True) + EPS)
            xn = x2 * r
            diff = xn * gf - tg_ref[rows, :]
            dy = diff * (1.0 / D)
            dxn = dy * gf
            dx2 = r * (dxn - xn * jnp.mean(dxn * xn, axis=-1, keepdims=True))
            dx2_ref[rows, :] = dx2
            dx2b_ref[rows, :] = dx2.astype(BF16)
            s_loss, s_dg = s_loss + _fold8(diff * diff), s_dg + _fold8(dy * xn)
        loss_ref[...] += jnp.sum(s_loss) * (0.5 / D)
        dg_ref[...] += jnp.sum(s_dg, axis=0, keepdims=True)

    row = pl.BlockSpec((tm, D), lambda i: (i, 0))
    return pl.pallas_call(
        body, name="ffn_fwd", grid=(t_len // tm,),
        in_specs=[row, row, row, pl.BlockSpec((8, D), lambda i: (0, 0)), pl.BlockSpec((1, D), lambda i: (0, 0)),
                  _resident((2, 4, FB, D)), _resident((DFF, D))],
        out_specs=[pl.BlockSpec((2, 4, tm, FB), lambda i: (0, 0, i, 0)), row, row,
                   pl.BlockSpec((1, 128), lambda i: (0, 0)), pl.BlockSpec((1, D), lambda i: (0, 0))],
        out_shape=[jax.ShapeDtypeStruct((2, 4, t_len, FB), BF16), jax.ShapeDtypeStruct((t_len, D), F32),
                   jax.ShapeDtypeStruct((t_len, D), BF16),
                   jax.ShapeDtypeStruct((1, 128), F32), jax.ShapeDtypeStruct((1, D), F32)],
        scratch_shapes=[pltpu.VMEM((tm, D), F32)],
        compiler_params=_cp("arbitrary"),
    )(h2, x1, target, mod, g_fin, w_gu, w_down)


S_SH, S_SC, S_G = 0, 1, 2


def _norm_bwd_rows(span, sums, dh_ref, x_ref, dres_ref, scale, gain, write):
    gs = 1.0 + scale
    s_sh, s_sc, s_g = sums
    for r0 in range(span.start, span.stop, 16):
        rows = slice(r0, r0 + 16)
        dh = dh_ref[rows, :]
        xv = x_ref[rows, :]
        r = lax.rsqrt(jnp.mean(xv * xv, axis=-1, keepdims=True) + EPS)
        xn = xv * r
        dhn = dh * gs
        dxn = dhn * gain
        write(rows, dres_ref[rows, :] + r * (dxn - xn * jnp.mean(dxn * xn, axis=-1, keepdims=True)))
        s_sh, s_sc, s_g = s_sh + _fold8(dh), s_sc + _fold8(dh * (xn * gain)), s_g + _fold8(dhn * xn)
    return s_sh, s_sc, s_g


def _add_norm_sums(sums_ref, sums):
    for dst, s in zip((S_SH, S_SC, S_G), sums):
        sums_ref[dst:dst + 1, :] += jnp.sum(s, axis=0, keepdims=True)


def _ffn_bwd(dx2, gu, x1, mod, g_ffn, w_gu, w_down, w_out):
    t_len = x1.shape[0]
    tm = min(TMF, t_len)

    def body(dx2_ref, gu_ref, x1_ref, mod_ref, g_ref, wgu_ref, wd_ref, wo_ref,
             dgu_ref, act_ref, dx1_ref, dx1b_ref, dmg_ref, sums_ref, acc, dmo, dact_s):
        @pl.when(pl.program_id(0) == 0)
        def _():
            sums_ref[...] = jnp.zeros_like(sums_ref)

        dffn = (dx2_ref[...] * mod_ref[5:6, :]).astype(BF16)
        dact_s[0] = _dot_nt(dffn, wd_ref[0:FB, :])
        for j in range(4):
            if j < 3:
                dact_s[(j + 1) % 2] = _dot_nt(dffn, wd_ref[(j + 1) * FB:(j + 2) * FB, :])
            for r0 in range(0, tm, 16):
                rows = slice(r0, r0 + 16)
                dact = dact_s[j % 2, rows, :]
                gate = gu_ref[0, j, rows, :].astype(F32)
                up = gu_ref[1, j, rows, :].astype(F32)
                sg = _sig(gate)
                silu = gate * sg
                act_ref[j, rows, :] = (silu * up).astype(BF16)
                dgu_ref[0, j, rows, :] = (dact * up * (sg * (1.0 + gate * (1.0 - sg)))).astype(BF16)
                dgu_ref[1, j, rows, :] = (dact * silu).astype(BF16)
            part = _dot(dgu_ref[0, j], wgu_ref[0, j]) + _dot(dgu_ref[1, j], wgu_ref[1, j])
            if j == 0:
                acc[...] = part
            else:
                acc[...] += part

        gt1 = mod_ref[2:3, :]

        def write(rows, dx1):
            dx1_ref[rows, :] = dx1
            dx1b_ref[rows, :] = dx1.astype(BF16)
            dmo[rows, :] = (dx1 * gt1).astype(BF16)

        zero = jnp.zeros((8, D), F32)
        sums = (zero, zero, zero)
        for sub in (slice(0, tm // 2), slice(tm // 2, tm)):
            sums = _norm_bwd_rows(sub, sums, acc, x1_ref, dx2_ref, mod_ref[4:5, :], g_ref[...], write)
            dmg_ref[sub, :] = _dot_nt(dmo[sub, :], wo_ref[...]).astype(BF16)
        _add_norm_sums(sums_ref, sums)

    row = pl.BlockSpec((tm, D), lambda i: (i, 0))
    return pl.pallas_call(
        body, name="ffn_bwd", grid=(t_len // tm,),
        in_specs=[row, pl.BlockSpec((2, 4, tm, FB), lambda i: (0, 0, i, 0)), row,
                  pl.BlockSpec((8, D), lambda i: (0, 0)), pl.BlockSpec((1, D), lambda i: (0, 0)),
                  _resident((2, 4, FB, D)), _resident((DFF, D)), _resident((D, D))],
        out_specs=[pl.BlockSpec((2, 4, tm, FB), lambda i: (0, 0, i, 0)),
                   pl.BlockSpec((4, tm, FB), lambda i: (0, i, 0)), row, row, row,
                   pl.BlockSpec((8, D), lambda i: (0, 0))],
        out_shape=[jax.ShapeDtypeStruct((2, 4, t_len, FB), BF16), jax.ShapeDtypeStruct((4, t_len, FB), BF16),
                   jax.ShapeDtypeStruct((t_len, D), F32), jax.ShapeDtypeStruct((t_len, D), BF16),
                   jax.ShapeDtypeStruct((t_len, D), BF16), jax.ShapeDtypeStruct((8, D), F32)],
        scratch_shapes=[pltpu.VMEM((tm, D), F32), pltpu.VMEM((tm, D), BF16), pltpu.VMEM((2, tm, FB), F32)],
        compiler_params=_cp("arbitrary"),
    )(dx2, gu, x1, mod, g_ffn, w_gu, w_down, w_out)


def _my_pos():
    return lax.axis_index("x"), lax.axis_index("y"), lax.axis_index("c")


def _my_index():
    x, y, c = _my_pos()
    return 4 * x + 2 * y + c


def _device_of(b):
    return (b >> 2) & 1, (b >> 1) & 1, b & 1


def _rs_send(src, parts_ref, b, send_sems, recv_sems, local_sem):
    me = _my_index()
    dst = parts_ref.at[me]

    @pl.when(b == me)
    def _():
        pltpu.make_async_copy(src, dst, local_sem).start()

    @pl.when(b != me)
    def _():
        pltpu.make_async_remote_copy(src_ref=src, dst_ref=dst, send_sem=send_sems.at[b], recv_sem=recv_sems.at[me],
                                     device_id=_device_of(b), device_id_type=MESH).start()


def _rs_finish(src_of, parts_ref, send_sems, recv_sems, local_sem):
    me = _my_index()
    for s in range(NDEV):
        @pl.when(s != me)
        def _():
            cp = pltpu.make_async_remote_copy(src_ref=src_of(s), dst_ref=parts_ref.at[s], send_sem=send_sems.at[s],
                                              recv_sem=recv_sems.at[s], device_id=_device_of(s), device_id_type=MESH)
            cp.wait_send()
            cp.wait_recv()

        @pl.when(s == me)
        def _():
            pltpu.make_async_copy(src_of(s), parts_ref.at[s], local_sem).wait()


_RS_SEMS = [pltpu.SemaphoreType.DMA((NDEV,)), pltpu.SemaphoreType.DMA((NDEV,)), pltpu.SemaphoreType.DMA]
_ANY = pl.BlockSpec(memory_space=pl.ANY)


def _xor_order(me, n):
    return (me ^ (n - 1 - jnp.arange(n, dtype=jnp.int32))).astype(jnp.int32)


NCHIP = NDEV // 2


def _rs2_scratch(half_shape):
    blocks = lambda *lead: pltpu.VMEM(lead + tuple(half_shape), BF16)
    return [blocks(NCHIP, 2), blocks(NCHIP)] + [pltpu.SemaphoreType.DMA((NCHIP,))] * 4 + [pltpu.SemaphoreType.DMA]


def _rs2_to_sibling(q, rs):
    stage, from_sib, d_send, d_recv = rs[:4]
    x, y, c = _my_pos()
    pltpu.make_async_remote_copy(src_ref=stage.at[q, 1 - c], dst_ref=from_sib.at[q], send_sem=d_send.at[q],
                                 recv_sem=d_recv.at[q], device_id=(x, y, 1 - c), device_id_type=MESH).start()


def _rs2_forward(q, parts_ref, rs):
    stage, chip_sum, d_send, d_recv, i_send, i_recv, local_sem = rs
    x, y, c = _my_pos()
    my_chip = 2 * x + y
    pltpu.make_async_remote_copy(src_ref=stage.at[q, c], dst_ref=chip_sum.at[q], send_sem=d_send.at[q],
                                 recv_sem=d_recv.at[q], device_id=(x, y, 1 - c), device_id_type=MESH).wait_recv()
    chip_sum[q] = (stage[q, c].astype(F32) + chip_sum[q].astype(F32)).astype(BF16)

    @pl.when(q == my_chip)
    def _():
        pltpu.make_async_copy(chip_sum.at[q], parts_ref.at[my_chip], local_sem).start()

    @pl.when(q != my_chip)
    def _():
        pltpu.make_async_remote_copy(src_ref=chip_sum.at[q], dst_ref=parts_ref.at[my_chip], send_sem=i_send.at[q],
                                     recv_sem=i_recv.at[my_chip], device_id=((q >> 1) & 1, q & 1, c),
                                     device_id_type=MESH).start()


def _rs2_finish(parts_ref, rs):
    stage, chip_sum, d_send, d_recv, i_send, i_recv, local_sem = rs
    x, y, c = _my_pos()
    my_chip = 2 * x + y
    for q in range(NCHIP):
        pltpu.make_async_remote_copy(src_ref=stage.at[q, 1 - c], dst_ref=chip_sum.at[q], send_sem=d_send.at[q],
                                     recv_sem=d_recv.at[q], device_id=(x, y, 1 - c), device_id_type=MESH).wait_send()

        @pl.when(q != my_chip)
        def _():
            cp = pltpu.make_async_remote_copy(src_ref=chip_sum.at[q], dst_ref=parts_ref.at[q], send_sem=i_send.at[q],
                                              recv_sem=i_recv.at[q], device_id=((q >> 1) & 1, q & 1, c),
                                              device_id_type=MESH)
            cp.wait_send()
            cp.wait_recv()

        @pl.when(q == my_chip)
        def _():
            pltpu.make_async_copy(chip_sum.at[q], parts_ref.at[q], local_sem).wait()


def _gu_wgrad(h2, dgu, order):
    t_len = h2.shape[0]
    tk = min(TK, t_len)
    nk = t_len // tk

    def body(ord_ref, h_ref, d_ref, parts_ref, acc, *rs):
        p, k = pl.program_id(0), pl.program_id(1)

        @pl.when(k == 0)
        def _():
            acc[...] = jnp.zeros_like(acc)

        hb = h_ref[...]
        for half in range(2):
            acc[half] += _dot_tn(d_ref[0, half], hb)

        @pl.when(k == nk - 1)
        def _():
            q = ord_ref[p]
            rs[0][q] = acc[...].astype(BF16)
            _rs2_to_sibling(q, rs)

        @pl.when((k == min(1, nk - 1)) & (p > 0))
        def _():
            _rs2_forward(ord_ref[p - 1], parts_ref, rs)

        @pl.when((p == NCHIP - 1) & (k == nk - 1))
        def _():
            _rs2_forward(ord_ref[p], parts_ref, rs)
            _rs2_finish(parts_ref, rs)

    return pl.pallas_call(
        body, name="gu_wgrad",
        grid_spec=pltpu.PrefetchScalarGridSpec(
            num_scalar_prefetch=1, grid=(NCHIP, nk),
            in_specs=[pl.BlockSpec((tk, D), lambda p, k, o: (k, 0)),
                      pl.BlockSpec((1, 2, tk, FB), lambda p, k, o: (o[p], 0, k, 0))],
            out_specs=_ANY,
            scratch_shapes=[pltpu.VMEM((2, FB, D), F32)] + _rs2_scratch((FB, D))),
        out_shape=jax.ShapeDtypeStruct((NCHIP, FB, D), BF16),
        compiler_params=_cp("arbitrary", "arbitrary"),
    )(order, h2, dgu.reshape(NCHIP, 2, t_len, FB))


def _scaled_wgrad(name, a, dx, w, gate_row, mod, order):
    t_len = dx.shape[0]
    kb = w.shape[0] // NCHIP
    tk = min(TK, t_len)
    nk = t_len // tk
    rows = kb // 2
    if a.ndim == 3:
        a_spec = pl.BlockSpec((None, tk, kb), lambda p, k, o: (o[p], k, 0))
    else:
        a_spec = pl.BlockSpec((tk, kb), lambda p, k, o: (k, o[p]))

    def body(ord_ref, a_ref, dx_ref, w_ref, mod_ref, parts_ref, dg_ref, acc, *rs):
        p, k = pl.program_id(0), pl.program_id(1)

        @pl.when((p == 0) & (k == 0))
        def _():
            dg_ref[...] = jnp.zeros_like(dg_ref)

        @pl.when(k == 0)
        def _():
            acc[...] = jnp.zeros_like(acc)

        acc[...] += _dot_tn(a_ref[...], dx_ref[...])

        @pl.when(k == nk - 1)
        def _():
            q = ord_ref[p]
            z = acc[...]
            zg = (z * mod_ref[gate_row:gate_row + 1, :]).astype(BF16)
            dg_ref[0:1, :] += jnp.sum(z * w_ref[...].astype(F32), axis=0, keepdims=True)
            for half in range(2):
                rs[0][q, half] = zg[half * rows:(half + 1) * rows]
            _rs2_to_sibling(q, rs)

        @pl.when((k == min(1, nk - 1)) & (p > 0))
        def _():
            _rs2_forward(ord_ref[p - 1], parts_ref, rs)

        @pl.when((p == NCHIP - 1) & (k == nk - 1))
        def _():
            _rs2_forward(ord_ref[p], parts_ref, rs)
            _rs2_finish(parts_ref, rs)

    return pl.pallas_call(
        body, name=name,
        grid_spec=pltpu.PrefetchScalarGridSpec(
            num_scalar_prefetch=1, grid=(NCHIP, nk),
            in_specs=[a_spec,
                      pl.BlockSpec((tk, D), lambda p, k, o: (k, 0)),
                      pl.BlockSpec((kb, D), lambda p, k, o: (o[p], 0)),
                      pl.BlockSpec((8, D), lambda p, k, o: (0, 0))],
            out_specs=[_ANY, pl.BlockSpec((8, D), lambda p, k, o: (0, 0))],
            scratch_shapes=[pltpu.VMEM((kb, D), F32)] + _rs2_scratch((rows, D))),
        out_shape=[jax.ShapeDtypeStruct((NCHIP, rows, D), BF16), jax.ShapeDtypeStruct((8, D), F32)],
        compiler_params=_cp("arbitrary", "arbitrary"),
    )(order, a, dx, w, mod)


M_WA, M_WB, M_CBIAS, M_BA, M_BX, M_LS = 0, 3, 7, 8, 9, 10


def _conv_bwd_rows(tt, proj_ref, prm_ref, xe, ve, due, dye, dp_ref, acc8):
    row = lax.broadcasted_iota(jnp.int32, (8, CG), 0)
    w_b = [prm_ref[P_WB + k:P_WB + k + 1, :] for k in range(4)]
    w_a = [prm_ref[P_WA + k:P_WA + k + 1, :] for k in range(3)]

    def blk(ib, carry):
        r0 = pl.multiple_of(ib * 16, 16)
        rows = pl.ds(r0, 16)
        for g in range(D // CG):
            cs = slice(g * CG, (g + 1) * CG)
            du16, du_after = due[rows, cs], due[pl.ds(r0 + 16, 8), cs]
            dy16, dy_after = dye[rows, cs], dye[pl.ds(r0 + 16, 8), cs]
            cc16 = _pj(proj_ref, 1, rows, cs).astype(F32)
            cx16 = _pj(proj_ref, 2, rows, cs).astype(F32)
            x16 = _pj(proj_ref, 3, rows, cs).astype(F32)
            v16 = cc16 * cx16
            xp, vp = xe[pl.ds(r0, 8), cs], ve[pl.ds(r0, 8), cs]
            xe[pl.ds(r0 + 16, 8), cs] = x16[8:16]
            ve[pl.ds(r0 + 16, 8), cs] = v16[8:16]
            acc = [acc8[8 * k:8 * k + 8, cs] for k in range(8)]
            drx, dv = [], []
            for sb in range(2):
                lo = slice(8 * sb, 8 * sb + 8)
                duc, dyc, xc, vc = du16[lo], dy16[lo], x16[lo], v16[lo]
                du_n = du16[8:16] if sb == 0 else du_after
                dy_n = dy16[8:16] if sb == 0 else dy_after
                acc[0] = acc[0] + duc
                acc[4] = acc[4] + duc * xc
                d8 = w_b[3][:, cs] * duc
                for s in (1, 2, 3):
                    acc[4 - s] = acc[4 - s] + duc * _shift_down(xc, xp, s, row)
                    d8 = d8 + w_b[3 - s][:, cs] * _shift_up(duc, du_n, s, row)
                acc[7] = acc[7] + dyc * vc
                e8 = w_a[2][:, cs] * dyc
                for s in (1, 2):
                    acc[7 - s] = acc[7 - s] + dyc * _shift_down(vc, vp, s, row)
                    e8 = e8 + w_a[2 - s][:, cs] * _shift_up(dyc, dy_n, s, row)
                drx.append(d8)
                dv.append(e8)
                xp, vp = xc, vc
            for k in range(8):
                acc8[8 * k:8 * k + 8, cs] = acc[k]
            dv16 = jnp.concatenate(dv, axis=0)
            col = lambda s: slice(s * D + g * CG, s * D + (g + 1) * CG)
            dp_ref[rows, col(3)] = jnp.concatenate(drx, axis=0).astype(BF16)
            dp_ref[rows, col(1)] = (dv16 * cx16).astype(BF16)
            dp_ref[rows, col(2)] = (dv16 * cc16).astype(BF16)
        return carry

    lax.fori_loop(0, tt // 16, blk, 0)


def _mixer_bwd(proj, hl, sv, dmg, prm, wa, wx):
    t_len = proj.shape[0]
    tt = min(TT, t_len)
    nt = t_len // tt
    hb8 = tt // 8

    def rev(i):
        return nt - 1 - i

    def halo(i):
        return jnp.maximum(rev(i) * hb8 - 1, 0)

    def body(proj_ref, ph_ref, hl_ref, hh_ref, sv_ref, dmg_ref, prm_ref, wa_ref, wx_ref,
             dp_ref, sums_ref, gwa_ref, gwx_ref,
             xe, ve, he, due, dye, drp_s, dip_s, an, gn, acc8):
        i = pl.program_id(0)
        t = rev(i)

        @pl.when(i == 0)
        def _():
            sums_ref[...] = jnp.zeros_like(sums_ref)
            gwa_ref[...] = jnp.zeros_like(gwa_ref)
            gwx_ref[...] = jnp.zeros_like(gwx_ref)
            due[tt:tt + 8, :] = jnp.zeros((8, D), F32)
            dye[tt:tt + 8, :] = jnp.zeros((8, D), F32)
            an[...] = jnp.zeros((8, D), F32)
            gn[...] = jnp.zeros((8, D), F32)

        live = (t > 0).astype(F32)
        xe[0:8, :] = _pj(ph_ref, 3).astype(F32) * live
        ve[0:8, :] = _pj(ph_ref, 1).astype(F32) * _pj(ph_ref, 2).astype(F32) * live
        he[0:8, :] = hh_ref[...].astype(F32) * live
        he[8:8 + tt, :] = hl_ref[...].astype(F32)

        ls_all = _log_sigmoid(prm_ref[P_LAM:P_LAM + 1, :])
        row = lax.broadcasted_iota(jnp.int32, (8, CG), 0)
        nblk = tt // 16

        def blk(ib, carry):
            r0 = pl.multiple_of((nblk - 1 - ib) * 16, 16)
            rows = pl.ds(r0, 16)
            for g in range(D // CG):
                cs = slice(g * CG, (g + 1) * CG)
                ls = ls_all[:, cs]
                dm = dmg_ref[rows, cs].astype(F32)
                cb = _pj(proj_ref, 0, rows, cs).astype(F32)
                rg = _pj(proj_ref, 4, rows, cs).astype(F32)
                sga = _sig(_pj(proj_ref, 5, rows, cs).astype(F32))
                sgb = _sig(_pj(proj_ref, 6, rows, cs).astype(F32))
                ya0 = sv_ref[SV_YA, rows, cs]
                h16 = he[pl.ds(r0 + 8, 16), cs]
                gl, th = _gelu(rg)
                dgl = 0.5 * (1.0 + th) + 0.5 * rg * (1.0 - th * th) * (_GC * (1.0 + 3.0 * 0.044715 * rg * rg))
                y_a = cb * ya0
                y_b = h16 * gl
                dy_a = dm * sga
                dy_b = dm * sgb
                col = lambda s: slice(s * D + g * CG, s * D + (g + 1) * CG)
                dp_ref[rows, col(5)] = (dm * y_a * sga * (1.0 - sga)).astype(BF16)
                dp_ref[rows, col(6)] = (dm * y_b * sgb * (1.0 - sgb)).astype(BF16)
                dp_ref[rows, col(4)] = (dy_b * h16 * dgl).astype(BF16)
                dp_ref[rows, col(0)] = (dy_a * ya0).astype(BF16)
                dye[rows, cs] = dy_a * cb
                dh16 = dy_b * gl

                a_next = an[:, cs]
                g_next = gn[:, cs]
                s_ba = jnp.zeros((8, CG), F32)
                s_bx = jnp.zeros((8, CG), F32)
                s_ls = jnp.zeros((8, CG), F32)
                for sb in (1, 0):
                    rr = r0 + 8 * sb
                    first = (row + (t * tt + rr)) == 0
                    r8 = pl.ds(rr, 8)
                    uu, r, ig, a, mult = (sv_ref[pln, r8, cs] for pln in (SV_U, SV_R, SV_I, SV_A, SV_MULT))
                    ca = jnp.where(row < 7, pltpu.roll(a, 7, 0), a_next)
                    cb_ = dh16[8 * sb:8 * sb + 8, :]
                    for s in (1, 2, 4):
                        a_sh = jnp.where(row < 8 - s, pltpu.roll(ca, 8 - s, 0), 1.0)
                        b_sh = jnp.where(row < 8 - s, pltpu.roll(cb_, 8 - s, 0), 0.0)
                        cb_ = ca * b_sh + cb_
                        ca = ca * a_sh
                    gv = ca * g_next + cb_
                    g_next = jnp.broadcast_to(gv[0:1, :], gv.shape)
                    a_next = jnp.broadcast_to(a[0:1, :], a.shape)
                    hprev = jnp.where(row >= 1, pltpu.roll(he[pl.ds(rr + 8, 8), cs], 1, 0),
                                      pltpu.roll(he[pl.ds(rr, 8), cs], 1, 0))
                    da = gv * hprev
                    dmult = jnp.where(first, 0.0, gv * ig * uu)
                    dla = da * a + jnp.where(mult > 0.0, dmult * (-(a * a) / mult), 0.0)
                    drp = dla * (LRU_C * ls) * r * (1.0 - r)
                    dip = gv * mult * uu * ig * (1.0 - ig)
                    s_ls = s_ls + dla * (LRU_C * r)
                    s_ba = s_ba + drp
                    s_bx = s_bx + dip
                    drp_s[pl.ds(rr, 8), cs] = drp
                    dip_s[pl.ds(rr, 8), cs] = dip
                    due[pl.ds(rr, 8), cs] = gv * mult * ig
                an[:, cs] = a_next
                gn[:, cs] = g_next
                sums_ref[M_BA:M_BA + 1, cs] += jnp.sum(s_ba, axis=0, keepdims=True)
                sums_ref[M_BX:M_BX + 1, cs] += jnp.sum(s_bx, axis=0, keepdims=True)
                sums_ref[M_LS:M_LS + 1, cs] += jnp.sum(s_ls, axis=0, keepdims=True)
            return carry

        lax.fori_loop(0, nblk, blk, 0)

        drp_b = drp_s[...].astype(BF16)
        dip_b = dip_s[...].astype(BF16)
        ub = sv_ref[SV_U].astype(BF16)
        for h in range(HEADS):
            cs = slice(h * HB, (h + 1) * HB)
            due[0:tt, cs] += _dot_nt(drp_b[:, cs], wa_ref[h]) + _dot_nt(dip_b[:, cs], wx_ref[h])
            gwa_ref[h] += _dot_tn(ub[:, cs], drp_b[:, cs])
            gwx_ref[h] += _dot_tn(ub[:, cs], dip_b[:, cs])

        acc8[...] = jnp.zeros_like(acc8)
        _conv_bwd_rows(tt, proj_ref, prm_ref, xe, ve, due, dye, dp_ref, acc8)
        for k, dst in enumerate([M_CBIAS] + [M_WB + k for k in range(4)] + [M_WA + k for k in range(3)]):
            sums_ref[dst:dst + 1, :] += jnp.sum(acc8[8 * k:8 * k + 8, :], axis=0, keepdims=True)
        due[tt:tt + 8, :] = due[0:8, :]
        dye[tt:tt + 8, :] = dye[0:8, :]

        @pl.when(i == nt - 1)
        def _():
            sums_ref[M_LS:M_LS + 1, :] = sums_ref[M_LS:M_LS + 1, :] * _sig(-prm_ref[P_LAM:P_LAM + 1, :])

    big = lambda: pltpu.VMEM((tt + 8, D), F32)
    tile = lambda: pltpu.VMEM((tt, D), F32)
    return pl.pallas_call(
        body, name="mixer_bwd", grid=(nt,),
        in_specs=[pl.BlockSpec((tt, 7 * D), lambda i: (rev(i), 0)),
                  pl.BlockSpec((8, 7 * D), lambda i: (halo(i), 0)),
                  pl.BlockSpec((tt, D), lambda i: (rev(i), 0)),
                  pl.BlockSpec((8, D), lambda i: (halo(i), 0)),
                  pl.BlockSpec((len(SV_PLANES), tt, D), lambda i: (0, rev(i), 0)),
                  pl.BlockSpec((tt, D), lambda i: (rev(i), 0)),
                  pl.BlockSpec((16, D), lambda i: (0, 0)),
                  pl.BlockSpec((HEADS, HB, HB), lambda i: (0, 0, 0)),
                  pl.BlockSpec((HEADS, HB, HB), lambda i: (0, 0, 0))],
        out_specs=[pl.BlockSpec((tt, 7 * D), lambda i: (rev(i), 0)),
                   pl.BlockSpec((16, D), lambda i: (0, 0)),
                   pl.BlockSpec((HEADS, HB, HB), lambda i: (0, 0, 0)),
                   pl.BlockSpec((HEADS, HB, HB), lambda i: (0, 0, 0))],
        out_shape=[jax.ShapeDtypeStruct((t_len, 7 * D), BF16), jax.ShapeDtypeStruct((16, D), F32),
                   jax.ShapeDtypeStruct((HEADS, HB, HB), F32), jax.ShapeDtypeStruct((HEADS, HB, HB), F32)],
        scratch_shapes=[big(), big(), big(), big(), big(), tile(), tile(),
                        pltpu.VMEM((8, D), F32), pltpu.VMEM((8, D), F32), pltpu.VMEM((64, D), F32)],
        compiler_params=_cp("arbitrary"),
    )(proj, proj, hl, hl, sv, dmg, prm, wa, wx)


def _in_proj_bwd(dproj, w_in, x, dx1, mod, g_mix):
    t_len = x.shape[0]
    tm = min(TM, t_len)

    def body(dp_ref, w_ref, x_ref, dx1_ref, mod_ref, g_ref, gx_ref, sums_ref, acc):
        @pl.when(pl.program_id(0) == 0)
        def _():
            sums_ref[...] = jnp.zeros_like(sums_ref)

        def write(rows, dx):
            gx_ref[rows, :] = dx

        zero = jnp.zeros((8, D), F32)
        sums = (zero, zero, zero)
        for sub in _sub_blocks(tm):
            acc[sub, :] = _dot_nt(dp_ref[sub, :], w_ref[...])
            sums = _norm_bwd_rows(sub, sums, acc, x_ref, dx1_ref, mod_ref[1:2, :], g_ref[...], write)
        _add_norm_sums(sums_ref, sums)

    return pl.pallas_call(
        body, name="in_proj_bwd", grid=(t_len // tm,),
        in_specs=[pl.BlockSpec((tm, 7 * D), lambda i: (i, 0)),
                  _resident((D, 7 * D)),
                  pl.BlockSpec((tm, D), lambda i: (i, 0)), pl.BlockSpec((tm, D), lambda i: (i, 0)),
                  pl.BlockSpec((8, D), lambda i: (0, 0)), pl.BlockSpec((1, D), lambda i: (0, 0))],
        out_specs=[pl.BlockSpec((tm, D), lambda i: (i, 0)), pl.BlockSpec((8, D), lambda i: (0, 0))],
        out_shape=[jax.ShapeDtypeStruct((t_len, D), F32), jax.ShapeDtypeStruct((8, D), F32)],
        scratch_shapes=[pltpu.VMEM((tm, D), F32)],
        compiler_params=_cp("arbitrary"),
    )(dproj, w_in, x, dx1, mod, g_mix)


def _in_wgrad(h, dproj, g_wa, g_wx, order):
    t_len = h.shape[0]
    tk = min(TKI, t_len)
    nk = t_len // tk
    cw = 7 * D // NDEV
    hr = HB // NDEV

    def body(ord_ref, h_ref, d_ref, ga_ref, gx_ref, parts_ref, pa_ref, px_ref, acc, *scr):
        rs, sems = scr[:-6], scr[-6:]
        p, k = pl.program_id(0), pl.program_id(1)

        def head_rows(ref):
            return lambda s: ref.at[:, pl.ds(s * hr, hr), :]

        @pl.when((p == 0) & (k == 0))
        def _():
            for s in range(NDEV):
                _rs_send(head_rows(ga_ref)(s), pa_ref, s, *sems[0:3])
                _rs_send(head_rows(gx_ref)(s), px_ref, s, *sems[3:6])

        @pl.when(k == 0)
        def _():
            acc[...] = jnp.zeros_like(acc)

        acc[...] += _dot_tn(h_ref[...], d_ref[...])

        @pl.when(k == nk - 1)
        def _():
            q = ord_ref[p]
            for half in range(2):
                rs[0][q, half] = acc[:, half * cw:(half + 1) * cw].astype(BF16)
            _rs2_to_sibling(q, rs)

        @pl.when((k == min(1, nk - 1)) & (p > 0))
        def _():
            _rs2_forward(ord_ref[p - 1], parts_ref, rs)

        @pl.when((p == NCHIP - 1) & (k == nk - 1))
        def _():
            _rs2_forward(ord_ref[p], parts_ref, rs)
            _rs2_finish(parts_ref, rs)
            _rs_finish(head_rows(ga_ref), pa_ref, *sems[0:3])
            _rs_finish(head_rows(gx_ref), px_ref, *sems[3:6])

    return pl.pallas_call(
        body, name="in_wgrad",
        grid_spec=pltpu.PrefetchScalarGridSpec(
            num_scalar_prefetch=1, grid=(NCHIP, nk),
            in_specs=[pl.BlockSpec((tk, D), lambda p, k, o: (k, 0)),
                      pl.BlockSpec((tk, 2 * cw), lambda p, k, o: (k, o[p])), _ANY, _ANY],
            out_specs=[_ANY, _ANY, _ANY],
            scratch_shapes=[pltpu.VMEM((D, 2 * cw), F32)] + _rs2_scratch((D, cw)) + _RS_SEMS * 2),
        out_shape=[jax.ShapeDtypeStruct((NCHIP, D, cw), BF16), jax.ShapeDtypeStruct((NDEV, HEADS, hr, HB), F32),
                   jax.ShapeDtypeStruct((NDEV, HEADS, hr, HB), F32)],
        compiler_params=_cp("arbitrary", "arbitrary"),
    )(order, h, dproj, g_wa, g_wx)


def _adam_math(w, g, m, v):
    m = ADAM_B1 * m + (1.0 - ADAM_B1) * g
    v = ADAM_B2 * v + (1.0 - ADAM_B2) * (g * g)
    m_hat = m / (1.0 - ADAM_B1 ** ADAM_STEP)
    v_hat = v / (1.0 - ADAM_B2 ** ADAM_STEP)
    delta = -ADAM_LR * (m_hat / (jnp.sqrt(v_hat) + ADAM_EPS) + ADAM_WD * w)
    return delta, m, v


def _ada_bwd(c_all, dmod_cols, w, m, v):
    rb = 256
    n = w.shape[1]
    nrow = c_all.shape[0]

    def body(c_ref, d_ref, w_ref, m_ref, v_ref, g_ref, dl_ref, nm_ref, nv_ref):
        cv = c_ref[...]
        g = _dot_tn((cv * _sig(cv)).astype(BF16), d_ref[...].astype(BF16))
        g_ref[...] = g
        dl_ref[...], nm_ref[...], nv_ref[...] = _adam_math(w_ref[...], g, m_ref[...], v_ref[...])

    blk = pl.BlockSpec((rb, n), lambda i: (i, 0))
    sds = jax.ShapeDtypeStruct(w.shape, F32)
    return pl.pallas_call(
        body, name="ada_bwd", grid=(D // rb,),
        in_specs=[pl.BlockSpec((nrow, rb), lambda i: (0, i)), pl.BlockSpec((nrow, n), lambda i: (0, 0)), blk, blk, blk],
        out_specs=[blk, blk, blk, blk], out_shape=[sds, sds, sds, sds],
        compiler_params=_cp("parallel"),
    )(c_all, dmod_cols, w, m, v)


def _adam(name, parts, w, m, v):
    p, r, c = parts.shape
    rb = r
    for cand in (256, 128, 64, 32, 16, 8):
        if r % cand == 0 and r >= cand:
            rb = cand
            break

    def body(p_ref, w_ref, m_ref, v_ref, g_ref, dl_ref, nm_ref, nv_ref):
        g = p_ref[0].astype(F32)
        for q in range(1, p):
            g = g + p_ref[q].astype(F32)
        g_ref[...] = g
        dl_ref[...], nm_ref[...], nv_ref[...] = _adam_math(w_ref[...], g, m_ref[...], v_ref[...])

    blk = pl.BlockSpec((rb, c), lambda i: (i, 0))
    sds = jax.ShapeDtypeStruct((r, c), F32)
    return pl.pallas_call(
        body, name=name, grid=(r // rb,),
        in_specs=[pl.BlockSpec((p, rb, c), lambda i: (0, i, 0)), blk, blk, blk],
        out_specs=[blk, blk, blk, blk], out_shape=[sds, sds, sds, sds],
        compiler_params=_cp("parallel"),
    )(parts, w, m, v)


_SMALL_SEMS = [pltpu.SemaphoreType.DMA((7,)), pltpu.SemaphoreType.DMA((7,)), pltpu.SemaphoreType.DMA]
_VMEM = pl.BlockSpec(memory_space=pltpu.VMEM)


def _exchange_small(x_ref, out_ref, send_sems, recv_sems, local_sem):
    m_per = x_ref.shape[0]
    x, y, c = _my_pos()
    me, sibling = (x, y, c), (x, y, 1 - c)
    chips = [(1 - x, y), (x, 1 - y), (1 - x, 1 - y)]

    def rows(px, py, pc):
        return out_ref.at[pl.ds((4 * px + 2 * py + pc) * m_per, m_per), :]

    def copy(k, block, to, src=None):
        return pltpu.make_async_remote_copy(
            src_ref=rows(*block) if src is None else src, dst_ref=rows(*block),
            send_sem=send_sems.at[k], recv_sem=recv_sems.at[k], device_id=to, device_id_type=MESH)

    mine = pltpu.make_async_copy(x_ref, rows(*me), local_sem)
    mine.start()
    first = [copy(0, me, sibling, src=x_ref)]
    first += [copy(1 + j, me, (*chip, c), src=x_ref) for j, chip in enumerate(chips)]
    for cp in first:
        cp.start()
    passed = [copy(4 + j, (*chip, c), sibling) for j, chip in enumerate(chips)]
    for j, chip in enumerate(chips):
        copy(1 + j, (*chip, c), me).wait_recv()
        passed[j].start()
    copy(0, sibling, me).wait_recv()
    for j, chip in enumerate(chips):
        copy(4 + j, (*chip, 1 - c), me).wait_recv()
    for cp in first + passed:
        cp.wait_send()
    mine.wait()


def _all_gather_small(name, v):
    def body(x_ref, out_ref, send_sems, recv_sems, local_sem):
        _exchange_small(x_ref, out_ref, send_sems, recv_sems, local_sem)

    return pl.pallas_call(
        body, name=name, out_shape=jax.ShapeDtypeStruct((NDEV * v.shape[0], v.shape[1]), v.dtype),
        in_specs=[_VMEM], out_specs=_VMEM, scratch_shapes=_SMALL_SEMS,
    )(v)


def _ada_mod(pack, w_ada, b_cols):
    ncol = w_ada.shape[1]

    def body(p_ref, w_ref, b_ref, all_ref, mod_ref, cols, *sems):
        _exchange_small(p_ref, all_ref, *sems[0:3])
        c_all = jnp.concatenate([all_ref[8 * d:8 * d + 1, 0:D] for d in range(NDEV)], axis=0)
        c16 = jnp.concatenate([c_all, jnp.zeros_like(c_all)], axis=0)
        mod16 = _dot((c16 * _sig(c16)).astype(BF16), w_ref[...].astype(BF16)) + b_ref[...]
        cols[...] = mod16[0:NDEV]
        _exchange_small(cols, mod_ref, *sems[3:6])

    return pl.pallas_call(
        body, name="ada_mod",
        out_shape=[jax.ShapeDtypeStruct((NDEV * 8, pack.shape[1]), F32), jax.ShapeDtypeStruct((NDEV * 8, ncol), F32)],
        in_specs=[_VMEM, _VMEM, _VMEM], out_specs=[_VMEM, _VMEM],
        scratch_shapes=[pltpu.VMEM((NDEV, ncol), F32)] + _SMALL_SEMS * 2,
        compiler_params=_cp(),
    )(pack, w_ada, b_cols)


def _blk_rows(n):
    return lambda ref, b: ref.at[pl.ds(pl.multiple_of(b * n, 8), n), :]


def _blk_lead(ref, b):
    return ref.at[b]


def _blk_heads(ref, b):
    return ref.at[:, pl.ds(pl.multiple_of(b * (HB // NDEV), 8), HB // NDEV), :]


def _ag_phases(ins, outs, slicers, send_sems, recv_sems, local_sems):
    na = len(ins)
    x, y, c = _my_pos()
    me, sibling = (x, y, c), (x, y, 1 - c)
    chips = [(1 - x, y), (x, 1 - y), (1 - x, 1 - y)]

    def copy(a, k, block, to, from_shard=False):
        px, py, pc = block
        dst = slicers[a](outs[a], 4 * px + 2 * py + pc)
        return pltpu.make_async_remote_copy(
            src_ref=ins[a] if from_shard else dst, dst_ref=dst,
            send_sem=send_sems.at[a * 7 + k], recv_sem=recv_sems.at[a * 7 + k], device_id=to, device_id_type=MESH)

    def local(a):
        return pltpu.make_async_copy(ins[a], slicers[a](outs[a], 4 * x + 2 * y + c), local_sems.at[a])

    def firsts(a):
        return [copy(a, 0, me, sibling, True)] + [copy(a, 1 + j, me, (*chip, c), True) for j, chip in enumerate(chips)]

    def start():
        for a in range(na):
            local(a).start()
            for cp in firsts(a):
                cp.start()

    def forward():
        for a in range(na):
            for j, chip in enumerate(chips):
                copy(a, 1 + j, (*chip, c), me).wait_recv()
                copy(a, 4 + j, (*chip, c), sibling).start()

    def finish():
        for a in range(na):
            copy(a, 0, sibling, me).wait_recv()
            for j, chip in enumerate(chips):
                copy(a, 4 + j, (*chip, 1 - c), me).wait_recv()
        for a in range(na):
            for cp in firsts(a) + [copy(a, 4 + j, (*chip, c), sibling) for j, chip in enumerate(chips)]:
                cp.wait_send()
            local(a).wait()

    return start, forward, finish


def _ag_sems(na):
    return [pltpu.SemaphoreType.DMA((7 * na,)), pltpu.SemaphoreType.DMA((7 * na,)), pltpu.SemaphoreType.DMA((na,))]


def _local_step(x, target, mod, g_mix, g_ffn, g_fin, prm, w_in_shard, shards):
    fulls = [(HEADS, HB, HB), (HEADS, HB, HB), (D, D), (NDEV, FB, D), (DFF, D)]
    slicers = [_blk_heads, _blk_heads, _blk_rows(D // NDEV), _blk_lead, _blk_rows(DFF // NDEV)]
    my_chip = _my_index() >> 1
    own_first = (my_chip ^ jnp.arange(NCHIP, dtype=jnp.int32)).astype(jnp.int32)
    early, late = [0, 1, 2, 4], [3]
    pick = lambda lst, idx: [lst[i] for i in idx]
    proj, h, w_in, (wa, wx, w_out, w_down) = _in_proj(x, mod, g_mix, w_in_shard, own_first, pick(shards, early),
                                                      pick(fulls, early), pick(slicers, early))
    merged, hl, sv, (w_gu,) = _mixer_fwd(proj, prm, wa, wx, pick(shards, late), pick(fulls, late),
                                         pick(slicers, late))
    w_gu = w_gu.reshape(2, 4, FB, D)
    x1, h2 = _out_proj(merged, x, mod, g_ffn, w_out)
    gu, dx2, dx2b, loss, d_gfin = _ffn_fwd(h2, x1, target, mod, g_fin, w_gu, w_down)
    dgu, act, dx1, dx1b, dmg, sums2 = _ffn_bwd(dx2, gu, x1, mod, g_ffn, w_gu, w_down, w_out)
    chip_order = _xor_order(_my_index() >> 1, NCHIP)
    p_wgu = _gu_wgrad(h2, dgu, chip_order)
    p_wdown, d_gt2 = _scaled_wgrad("down_wgrad", act, dx2b, w_down, 5, mod, chip_order)
    p_wout, d_gt1 = _scaled_wgrad("out_wgrad", merged, dx1b, w_out, 2, mod, chip_order)
    dproj, msums, g_wa, g_wx = _mixer_bwd(proj, hl, sv, dmg, prm, wa, wx)
    p_win, p_wa, p_wx = _in_wgrad(h, dproj, g_wa, g_wx, chip_order)
    grad_x, sums1 = _in_proj_bwd(dproj, w_in, x, dx1, mod, g_mix)
    return dict(loss=loss, grad_x=grad_x, d_gfin=d_gfin, sums1=sums1, sums2=sums2, msums=msums,
                d_gt1=d_gt1[0:1], d_gt2=d_gt2[0:1], p_win=p_win, p_wa=p_wa, p_wx=p_wx, p_wout=p_wout, p_wgu=p_wgu,
                p_wdown=p_wdown)


def kernel(x, c, w_ada, b_ada, g_norm_mix, w_in, conv_a_w, conv_b_w, conv_b_bias, w_rg_a, b_rg_a, w_rg_x, b_rg_x, lru_lambda, w_out, g_norm_ffn, w_gate_up, w_down, g_norm_final, loss_target, m_w_ada, m_b_ada, m_g_norm_mix, m_w_in, m_conv_a_w, m_conv_b_w, m_conv_b_bias, m_w_rg_a, m_b_rg_a, m_w_rg_x, m_b_rg_x, m_lru_lambda, m_w_out, m_g_norm_ffn, m_w_gate_up, m_w_down, m_g_norm_final, v_w_ada, v_b_ada, v_g_norm_mix, v_w_in, v_conv_a_w, v_conv_b_w, v_conv_b_bias, v_w_rg_a, v_b_rg_a, v_w_rg_x, v_b_rg_x, v_lru_lambda, v_w_out, v_g_norm_ffn, v_w_gate_up, v_w_down, v_g_norm_final):
    me = 4 * lax.axis_index("x") + 2 * lax.axis_index("y") + lax.axis_index("c")
    ncol = w_ada.shape[2]
    cw = conv_a_w.shape[2]

    pack0 = jnp.concatenate([c, conv_a_w.reshape(1, 3 * cw), conv_b_w.reshape(1, 4 * cw)], axis=1)
    b_cols = lax.dynamic_slice_in_dim(b_ada, me * ncol, ncol, axis=1)
    got0, got1 = _ada_mod(jnp.broadcast_to(pack0, (8, pack0.shape[1])), w_ada[0], b_cols)
    got0 = got0.reshape(NDEV, 8, -1)[:, 0, :]
    c_all = got0[:, :D]
    conv_a = got0[:, D:D + 3 * cw].reshape(NDEV, 3, cw).transpose(1, 0, 2).reshape(3, D)
    conv_b = got0[:, D + 3 * cw:].reshape(NDEV, 4, cw).transpose(1, 0, 2).reshape(4, D)
    c16 = jnp.concatenate([c_all, jnp.zeros((8, D), F32)], axis=0)
    mod6 = lax.dynamic_index_in_dim(got1.reshape(NDEV, NDEV, ncol), me, axis=1, keepdims=False).reshape(6, D)
    mod = jnp.concatenate([mod6, jnp.zeros((2, D), F32)], axis=0)

    tr = lambda a: jnp.swapaxes(a, 1, 2)
    shards = [w_rg_a[0].astype(BF16), w_rg_x[0].astype(BF16), w_out[0].astype(BF16), tr(w_gate_up)[0].astype(BF16),
              w_down[0].astype(BF16)]

    prm = jnp.concatenate([conv_a, conv_b, conv_b_bias, b_rg_a, b_rg_x, lru_lambda, jnp.zeros((5, D), F32)], axis=0)
    r = _local_step(x[0], loss_target[0], mod, g_norm_mix, g_norm_ffn, g_norm_final.reshape(1, D), prm,
                    w_in[0].astype(BF16), shards)

    parts = [r["p_win"], r["p_wa"], r["p_wx"], r["p_wout"], r["p_wgu"], r["p_wdown"]]
    big = {}
    for nm, p, w, m, v in (("w_in", parts[0], w_in, m_w_in, v_w_in), ("w_rg_a", parts[1], w_rg_a, m_w_rg_a, v_w_rg_a),
                           ("w_rg_x", parts[2], w_rg_x, m_w_rg_x, v_w_rg_x), ("w_out", parts[3], w_out, m_w_out, v_w_out),
                           ("w_gate_up", parts[4], tr(w_gate_up), tr(m_w_gate_up), tr(v_w_gate_up)),
                           ("w_down", parts[5], w_down, m_w_down, v_w_down)):
        two_d = (-1, w.shape[-1])
        outs = _adam("adam_" + nm, p.reshape((p.shape[0],) + w.reshape(two_d).shape), w.reshape(two_d), m.reshape(two_d),
                     v.reshape(two_d))
        big[nm] = [o.reshape(w.shape) for o in outs]
    big["w_gate_up"] = [tr(o) for o in big["w_gate_up"]]

    small = jnp.concatenate([
        r["sums1"][S_SH:S_SH + 1], r["sums1"][S_SC:S_SC + 1], r["d_gt1"],
        r["sums2"][S_SH:S_SH + 1], r["sums2"][S_SC:S_SC + 1], r["d_gt2"],
        r["sums1"][S_G:S_G + 1],
        r["msums"][M_CBIAS:M_CBIAS + 1], r["msums"][M_BA:M_BA + 1], r["msums"][M_BX:M_BX + 1],
        r["msums"][M_LS:M_LS + 1],
        r["sums2"][S_G:S_G + 1], r["d_gfin"],
        r["msums"][M_WA:M_WA + 3], r["msums"][M_WB:M_WB + 4],
        jnp.broadcast_to(r["loss"][0:1, 0:1], (1, D)),
        jnp.zeros((3, D), F32)], axis=0)
    got2 = _all_gather_small("gather_small", small).reshape(NDEV, 24, D)

    rep_w = jnp.concatenate([b_ada.reshape(6, D), g_norm_mix, conv_b_bias, b_rg_a, b_rg_x, lru_lambda, g_norm_ffn,
                             g_norm_final.reshape(1, D), jnp.zeros((3, D), F32)], axis=0)
    rep_m = jnp.concatenate([m_b_ada.reshape(6, D), m_g_norm_mix, m_conv_b_bias, m_b_rg_a, m_b_rg_x, m_lru_lambda,
                             m_g_norm_ffn, m_g_norm_final.reshape(1, D), jnp.zeros((3, D), F32)], axis=0)
    rep_v = jnp.concatenate([v_b_ada.reshape(6, D), v_g_norm_mix, v_conv_b_bias, v_b_rg_a, v_b_rg_x, v_lru_lambda,
                             v_g_norm_ffn, v_g_norm_final.reshape(1, D), jnp.ones((3, D), F32)], axis=0)
    rep = _adam("adam_rep", got2[:, :16, :], rep_w, rep_m, rep_v)

    conv_parts = lax.dynamic_slice_in_dim(got2[:, 13:21, :], me * cw, cw, axis=2)
    cv_w = jnp.concatenate([conv_a_w[0], conv_b_w[0], jnp.zeros((1, cw), F32)], axis=0)
    cv_m = jnp.concatenate([m_conv_a_w[0], m_conv_b_w[0], jnp.zeros((1, cw), F32)], axis=0)
    cv_v = jnp.concatenate([v_conv_a_w[0], v_conv_b_w[0], jnp.ones((1, cw), F32)], axis=0)
    cvo = _adam("adam_conv", conv_parts, cv_w, cv_m, cv_v)

    dmod_cols = lax.dynamic_slice_in_dim(got2[:, :6, :].reshape(NDEV, 6 * D), me * ncol, ncol, axis=1)
    dmod16 = jnp.concatenate([dmod_cols, jnp.zeros((8, ncol), F32)], axis=0)
    ada = _ada_bwd(c16, dmod16, w_ada[0], m_w_ada[0], v_w_ada[0])

    loss = jnp.sum(got2[:, 20, 0])

    def pick(q):
        one = lambda i: rep[q][i:i + 1]
        return [ada[q].reshape(w_ada.shape), rep[q][0:6].reshape(b_ada.shape), one(6), big["w_in"][q],
                cvo[q][0:3].reshape(conv_a_w.shape), cvo[q][3:7].reshape(conv_b_w.shape), one(7),
                big["w_rg_a"][q], one(8), big["w_rg_x"][q], one(9), one(10), big["w_out"][q], one(11),
                big["w_gate_up"][q], big["w_down"][q], rep[q][12]]

    return (loss, r["grad_x"].reshape(x.shape), *pick(0), *pick(1), *pick(2), *pick(3))
```

```python
import math

import jax
import jax.numpy as jnp
from jax import lax
from jax.experimental import pallas as pl
from jax.experimental.pallas import tpu as pltpu

F32 = jnp.float32
BF16 = jnp.bfloat16

D = 1024
DFF = 2816
NDEV = 8
HEADS = 4
HB = D // HEADS
FB = DFF // 4
EPS = 1e-6
LRU_C = 8.0
ADAM_LR, ADAM_B1, ADAM_B2, ADAM_EPS, ADAM_WD, ADAM_STEP = 0.001, 0.9, 0.999, 1e-08, 0.01, 10

VMEM_LIMIT = 56 * 1024 * 1024
TM = 512
TMI = 1024
TMF = 256
TK = 2048
TKI = 2048
SUB = 256
TT = 256
CG = 256
MESH = pl.DeviceIdType.MESH


def _cp(*sem):
    return pltpu.CompilerParams(dimension_semantics=sem, vmem_limit_bytes=VMEM_LIMIT)


def _sig(x):
    return 1.0 / (1.0 + jnp.exp(-x))


def _log_sigmoid(x):
    z = jnp.exp(-jnp.abs(x))
    u = 1.0 + z
    d = u - 1.0
    l1p = jnp.where(d == 0.0, z, jnp.log(u) * (z / jnp.where(d == 0.0, 1.0, d)))
    return -(jnp.maximum(-x, 0.0) + l1p)


def _neg_expm1(x):
    p = x * (1.0 + x * 0.5 * (1.0 + x * (1.0 / 3.0) * (1.0 + x * 0.25 * (1.0 + x * 0.2 * (1.0 + x * (1.0 / 6.0))))))
    return jnp.where(x > -0.25, -p, 1.0 - jnp.exp(x))


_GC = math.sqrt(2.0 / math.pi)


def _gelu(x):
    t = jnp.tanh(_GC * (x + 0.044715 * x * x * x))
    return 0.5 * x * (1.0 + t), t


def _dot(a, b):
    return jnp.dot(a, b, preferred_element_type=F32)


def _dot_nt(a, b):
    return lax.dot_general(a, b, (((1,), (1,)), ((), ())), preferred_element_type=F32)


def _dot_tn(a, b):
    return lax.dot_general(a, b, (((0,), (0,)), ((), ())), preferred_element_type=F32)


def _resident(shape):
    return pl.BlockSpec(shape, lambda *_: (0,) * len(shape), pipeline_mode=pl.Buffered(1))


def _sub_blocks(n_rows):
    step = min(SUB, n_rows)
    return [slice(r, r + step) for r in range(0, n_rows, step)]


def _fold8(v):
    return v[0:8] + v[8:16]


def _pj(ref, s, rows=slice(None), cols=slice(0, D)):
    return ref[rows, s * D + cols.start:s * D + cols.stop]


def _in_proj(x, mod, g_mix, w_shard, order, shards, fulls, slicers):
    t_len = x.shape[0]
    tm = min(TMI, t_len)
    ni = t_len // tm
    na = len(shards)
    cw = 7 * D // NDEV
    rc = 32

    def body(ord_ref, x_ref, mod_ref, g_ref, wsh_ref, *rest):
        ins, (proj_ref, h_ref, wfull_ref), outs = rest[:na], rest[na:na + 3], rest[na + 3:2 * na + 3]
        h_scr, w_scr, wsend, wrecv, wlocal, wout = rest[2 * na + 3:2 * na + 9]
        start, forward, finish = _ag_phases(ins, outs, slicers, *rest[2 * na + 9:])
        p, i = pl.program_id(0), pl.program_id(1)
        x_, y_, c = _my_pos()
        me, sibling = (x_, y_, c), (x_, y_, 1 - c)
        chip_at = [None, (x_, 1 - y_), (1 - x_, y_), (1 - x_, 1 - y_)]

        def cols(px, py, pc):
            return w_scr.at[:, pl.ds(pl.multiple_of((4 * px + 2 * py + pc) * cw, 128), cw)]

        def wcopy(k, block, to, from_shard=False):
            dst = cols(*block)
            return pltpu.make_async_remote_copy(src_ref=wsh_ref if from_shard else dst, dst_ref=dst,
                                                send_sem=wsend.at[k], recv_sem=wrecv.at[k], device_id=to,
                                                device_id_type=MESH)

        own_local = pltpu.make_async_copy(wsh_ref, cols(*me), wlocal)
        to_hbm = pltpu.make_async_copy(w_scr, wfull_ref, wout)

        @pl.when((p == 0) & (i == 0))
        def _():
            own_local.start()
            wcopy(0, me, sibling, True).start()
            for q in (1, 2):
                wcopy(q, me, (*chip_at[q], c), True).start()
            own_local.wait()
            wcopy(0, sibling, me).wait_recv()

        @pl.when((p == 0) & (i == ni // 2))
        def _():
            wcopy(3, me, (*chip_at[3], c), True).start()

        for q in (1, 2, 3):
            @pl.when((p == q - 1) & (i == ni - 1))
            def _():
                wcopy(q, (*chip_at[q], c), me).wait_recv()
                wcopy(3 + q, (*chip_at[q], c), sibling).start()

            @pl.when((p == q) & (i == 0))
            def _():
                wcopy(3 + q, (*chip_at[q], 1 - c), me).wait_recv()

        @pl.when((p == 1) & (i == 0))
        def _():
            start()

        @pl.when((p == NCHIP - 1) & (i == ni // 2))
        def _():
            forward()

        @pl.when((p == NCHIP - 1) & (i == 0))
        def _():
            to_hbm.start()

        gs = g_ref[...] * (1.0 + mod_ref[1:2, :])
        sh = mod_ref[0:1, :]

        wcols = pl.ds(pl.multiple_of(ord_ref[p] * (2 * cw), 128), 2 * cw)
        for sub in _sub_blocks(tm):
            for r0 in range(sub.start, sub.stop, rc):
                xv = x_ref[r0:r0 + rc, :]
                r = lax.rsqrt(jnp.mean(xv * xv, axis=-1, keepdims=True) + EPS)
                h_scr[r0:r0 + rc, :] = (xv * r * gs + sh).astype(BF16)
            proj_ref[sub, :] = _dot(h_scr[sub, :], w_scr[:, wcols]).astype(BF16)

        @pl.when(p == 0)
        def _():
            h_ref[...] = h_scr[...]

        @pl.when((p == NCHIP - 1) & (i == ni - 1))
        def _():
            wcopy(0, me, sibling, True).wait_send()
            for q in (1, 2, 3):
                wcopy(q, me, (*chip_at[q], c), True).wait_send()
                wcopy(3 + q, (*chip_at[q], c), sibling).wait_send()
            finish()
            to_hbm.wait()

    res = pl.pallas_call(
        body, name="in_proj",
        grid_spec=pltpu.PrefetchScalarGridSpec(
            num_scalar_prefetch=1, grid=(NCHIP, ni),
            in_specs=[pl.BlockSpec((tm, D), lambda p, i, o: (i, 0)),
                      pl.BlockSpec((8, D), lambda p, i, o: (0, 0)),
                      pl.BlockSpec((1, D), lambda p, i, o: (0, 0))] + [_ANY] * (1 + na),
            out_specs=[pl.BlockSpec((tm, 2 * cw), lambda p, i, o: (i, o[p])),
                       pl.BlockSpec((tm, D), lambda p, i, o: (jnp.where(p == 0, i, ni - 1), 0))]
            + [_ANY] * (1 + na),
            scratch_shapes=[pltpu.VMEM((tm, D), BF16), pltpu.VMEM((D, 7 * D), BF16),
                            pltpu.SemaphoreType.DMA((7,)), pltpu.SemaphoreType.DMA((7,)),
                            pltpu.SemaphoreType.DMA, pltpu.SemaphoreType.DMA] + _ag_sems(na)),
        out_shape=[jax.ShapeDtypeStruct((t_len, 7 * D), BF16), jax.ShapeDtypeStruct((t_len, D), BF16),
                   jax.ShapeDtypeStruct((D, 7 * D), BF16)]
        + [jax.ShapeDtypeStruct(f, sh.dtype) for f, sh in zip(fulls, shards)],
        compiler_params=_cp("arbitrary", "arbitrary"),
    )(order, x, mod, g_mix, w_shard, *shards)
    return res[0], res[1], res[2], res[3:]


P_WA, P_WB, P_CBIAS, P_BA, P_BX, P_LAM = 0, 3, 7, 8, 9, 10
SV_PLANES = SV_U, SV_YA, SV_R, SV_I, SV_A, SV_MULT = range(6)


def _lru_gates(rp, ip, ls, first_row):
    r = _sig(rp)
    ig = _sig(ip)
    la = LRU_C * r * ls
    a = jnp.exp(la)
    m2 = _neg_expm1(2.0 * la)
    mult = jnp.where(first_row, 1.0, jnp.sqrt(jnp.maximum(m2, 0.0)))
    return r, ig, la, a, m2, mult


def _shift_down(cur, prev, s, row):
    return jnp.where(row >= s, pltpu.roll(cur, s, 0), pltpu.roll(prev, s, 0))


def _shift_up(cur, nxt, s, row):
    return jnp.where(row < 8 - s, pltpu.roll(cur, 8 - s, 0), pltpu.roll(nxt, 8 - s, 0))


def _conv_fwd_rows(tt, proj_ref, prm_ref, xe, ve, u_s, ub_s, ya_s):
    row = lax.broadcasted_iota(jnp.int32, (8, CG), 0)
    w_b = [prm_ref[P_WB + k:P_WB + k + 1, :] for k in range(4)]
    w_a = [prm_ref[P_WA + k:P_WA + k + 1, :] for k in range(3)]
    bias = prm_ref[P_CBIAS:P_CBIAS + 1, :]

    def blk(ib, carry):
        r0 = pl.multiple_of(ib * 16, 16)
        rows = pl.ds(r0, 16)
        for g in range(D // CG):
            cs = slice(g * CG, (g + 1) * CG)
            x16 = _pj(proj_ref, 3, rows, cs).astype(F32)
            v16 = _pj(proj_ref, 1, rows, cs).astype(F32) * _pj(proj_ref, 2, rows, cs).astype(F32)
            xp = xe[pl.ds(r0, 8), cs]
            vp = ve[pl.ds(r0, 8), cs]
            xe[pl.ds(r0 + 8, 16), cs] = x16
            ve[pl.ds(r0 + 8, 16), cs] = v16
            us, yas = [], []
            for sb in range(2):
                xc, vc = x16[8 * sb:8 * sb + 8], v16[8 * sb:8 * sb + 8]
                u8 = bias[:, cs] + w_b[3][:, cs] * xc
                for s in (1, 2, 3):
                    u8 = u8 + w_b[3 - s][:, cs] * _shift_down(xc, xp, s, row)
                y8 = w_a[2][:, cs] * vc
                for s in (1, 2):
                    y8 = y8 + w_a[2 - s][:, cs] * _shift_down(vc, vp, s, row)
                us.append(u8)
                yas.append(y8)
                xp, vp = xc, vc
            u16 = jnp.concatenate(us, axis=0)
            u_s[rows, cs] = u16
            ub_s[rows, cs] = u16.astype(BF16)
            ya_s[rows, cs] = jnp.concatenate(yas, axis=0)
        return carry

    lax.fori_loop(0, tt // 16, blk, 0)


def _mixer_fwd(proj, prm, wa, wx, shards, fulls, slicers):
    t_len = proj.shape[0]
    tt = min(TT, t_len)
    nt = t_len // tt
    na = len(shards)

    def body(proj_ref, prm_ref, wa_ref, wx_ref, *rest):
        ins, (mg_ref, hl_ref, sv_ref), outs = rest[:na], rest[na:na + 3], rest[na + 3:2 * na + 3]
        xe, ve, hc, rp_s, ip_s, ub_s = rest[2 * na + 3:2 * na + 9]
        start, forward, finish = _ag_phases(ins, outs, slicers, *rest[2 * na + 9:])
        t = pl.program_id(0)

        @pl.when(t == 0)
        def _():
            start()
            xe[0:8, :] = jnp.zeros((8, D), F32)
            ve[0:8, :] = jnp.zeros((8, D), F32)
            hc[...] = jnp.zeros((8, D), F32)

        @pl.when(t == (3 * nt) // 4)
        def _():
            forward()

        _conv_fwd_rows(tt, proj_ref, prm_ref, xe, ve, sv_ref.at[SV_U], ub_s, sv_ref.at[SV_YA])
        xe[0:8, :] = xe[tt:tt + 8, :]
        ve[0:8, :] = ve[tt:tt + 8, :]

        ub = ub_s[...]
        for h in range(HEADS):
            cs = slice(h * HB, (h + 1) * HB)
            rp_s[:, cs] = _dot(ub[:, cs], wa_ref[h]) + prm_ref[P_BA:P_BA + 1, cs]
            ip_s[:, cs] = _dot(ub[:, cs], wx_ref[h]) + prm_ref[P_BX:P_BX + 1, cs]

        ls_all = _log_sigmoid(prm_ref[P_LAM:P_LAM + 1, :])
        row = lax.broadcasted_iota(jnp.int32, (8, CG), 0)

        def blk(i, carry):
            r0 = pl.multiple_of(i * 16, 16)
            for g in range(D // CG):
                cs = slice(g * CG, (g + 1) * CG)
                ls = ls_all[:, cs]
                hprev = hc[:, cs]
                hs = []
                for sb in range(2):
                    rr = r0 + 8 * sb
                    first = (row + (t * tt + rr)) == 0
                    r8 = pl.ds(rr, 8)
                    r, ig, _, a, _, mult = _lru_gates(rp_s[r8, cs], ip_s[r8, cs], ls, first)
                    for plane, val in ((SV_R, r), (SV_I, ig), (SV_A, a), (SV_MULT, mult)):
                        sv_ref[plane, r8, cs] = val
                    b = mult * (ig * sv_ref[SV_U, r8, cs])
                    for s in (1, 2, 4):
                        a_sh = jnp.where(row >= s, pltpu.roll(a, s, 0), 1.0)
                        b_sh = jnp.where(row >= s, pltpu.roll(b, s, 0), 0.0)
                        b = a * b_sh + b
                        a = a * a_sh
                    hv = a * hprev + b
                    hprev = jnp.broadcast_to(hv[7:8, :], hv.shape)
                    hs.append(hv)
                hc[:, cs] = hprev
                h16 = jnp.concatenate(hs, axis=0)
                rows = pl.ds(r0, 16)
                gl, _ = _gelu(_pj(proj_ref, 4, rows, cs).astype(F32))
                y_b = h16 * gl
                y_a = _pj(proj_ref, 0, rows, cs).astype(F32) * sv_ref[SV_YA, rows, cs]
                mg = (_sig(_pj(proj_ref, 5, rows, cs).astype(F32)) * y_a
                      + _sig(_pj(proj_ref, 6, rows, cs).astype(F32)) * y_b)
                mg_ref[rows, cs] = mg.astype(BF16)
                hl_ref[rows, cs] = h16.astype(BF16)
            return carry

        lax.fori_loop(0, tt // 16, blk, 0)

        @pl.when(t == nt - 1)
        def _():
            finish()

    res = pl.pallas_call(
        body, name="mixer_fwd", grid=(nt,),
        in_specs=[pl.BlockSpec((tt, 7 * D), lambda t: (t, 0)),
                  pl.BlockSpec((16, D), lambda t: (0, 0)),
                  pl.BlockSpec((HEADS, HB, HB), lambda t: (0, 0, 0)),
                  pl.BlockSpec((HEADS, HB, HB), lambda t: (0, 0, 0))] + [_ANY] * na,
        out_specs=[pl.BlockSpec((tt, D), lambda t: (t, 0)), pl.BlockSpec((tt, D), lambda t: (t, 0)),
                   pl.BlockSpec((len(SV_PLANES), tt, D), lambda t: (0, t, 0))] + [_ANY] * na,
        out_shape=[jax.ShapeDtypeStruct((t_len, D), BF16), jax.ShapeDtypeStruct((t_len, D), BF16),
                   jax.ShapeDtypeStruct((len(SV_PLANES), t_len, D), F32)]
        + [jax.ShapeDtypeStruct(f, sh.dtype) for f, sh in zip(fulls, shards)],
        scratch_shapes=[pltpu.VMEM((tt + 8, D), F32), pltpu.VMEM((tt + 8, D), F32), pltpu.VMEM((8, D), F32),
                        pltpu.VMEM((tt, D), F32), pltpu.VMEM((tt, D), F32), pltpu.VMEM((tt, D), BF16)]
        + _ag_sems(na),
        compiler_params=_cp("arbitrary"),
    )(proj, prm, wa, wx, *shards)
    return res[0], res[1], res[2], res[3:]


def _out_proj(merged, x, mod, g_ffn, w_out):
    t_len = x.shape[0]
    tm = min(TM, t_len)

    def body(mg_ref, x_ref, mod_ref, g_ref, w_ref, x1_ref, h2_ref):
        gt1 = mod_ref[2:3, :]
        gs = g_ref[...] * (1.0 + mod_ref[4:5, :])
        sh = mod_ref[3:4, :]
        for sub in _sub_blocks(tm):
            x1_ref[sub, :] = x_ref[sub, :] + gt1 * _dot(mg_ref[sub, :], w_ref[...])
            for r0 in range(sub.start, sub.stop, 16):
                x1 = x1_ref[r0:r0 + 16, :]
                r = lax.rsqrt(jnp.mean(x1 * x1, axis=-1, keepdims=True) + EPS)
                h2_ref[r0:r0 + 16, :] = (x1 * r * gs + sh).astype(BF16)

    return pl.pallas_call(
        body, name="out_proj", grid=(t_len // tm,),
        in_specs=[pl.BlockSpec((tm, D), lambda i: (i, 0)), pl.BlockSpec((tm, D), lambda i: (i, 0)),
                  pl.BlockSpec((8, D), lambda i: (0, 0)), pl.BlockSpec((1, D), lambda i: (0, 0)),
                  pl.BlockSpec((D, D), lambda i: (0, 0))],
        out_specs=[pl.BlockSpec((tm, D), lambda i: (i, 0)), pl.BlockSpec((tm, D), lambda i: (i, 0))],
        out_shape=[jax.ShapeDtypeStruct((t_len, D), F32), jax.ShapeDtypeStruct((t_len, D), BF16)],
        compiler_params=_cp("parallel"),
    )(merged, x, mod, g_ffn, w_out)


def _ffn_fwd(h2, x1, target, mod, g_fin, w_gu, w_down):
    t_len = x1.shape[0]
    tm = min(TMF, t_len)

    def body(h2_ref, x1_ref, tg_ref, mod_ref, g_ref, wgu_ref, wd_ref, gu_ref, dx2_ref, dx2b_ref, loss_ref, dg_ref, acc):
        @pl.when(pl.program_id(0) == 0)
        def _():
            loss_ref[...] = jnp.zeros_like(loss_ref)
            dg_ref[...] = jnp.zeros_like(dg_ref)

        hb = h2_ref[...]
        ffn = None
        nxt = (_dot_nt(hb, wgu_ref[0, 0]), _dot_nt(hb, wgu_ref[1, 0]))
        for j in range(4):
            gate, up = nxt
            if j < 3:
                nxt = (_dot_nt(hb, wgu_ref[0, j + 1]), _dot_nt(hb, wgu_ref[1, j + 1]))
            gu_ref[0, j] = gate.astype(BF16)
            gu_ref[1, j] = up.astype(BF16)
            act = (gate * _sig(gate) * up).astype(BF16)
            part = _dot(act, wd_ref[j * FB:(j + 1) * FB, :])
            ffn = part if ffn is None else ffn + part
        acc[...] = ffn

        gt2 = mod_ref[5:6, :]
        gf = g_ref[...]

        s_loss = s_dg = jnp.zeros((8, D), F32)
        for r0 in range(0, tm, 16):
            rows = slice(r0, r0 + 16)
            x2 = x1_ref[rows, :] + gt2 * acc[rows, :]
            r = lax.rsqrt(jnp.mean(x2 * x2, axis=-1, keepdims=True) + EPS)
            xn = x2 * r
            diff = xn * gf - tg_ref[rows, :]
            dy = diff * (1.0 / D)
            dxn = dy * gf
            dx2 = r * (dxn - xn * jnp.mean(dxn * xn, axis=-1, keepdims=True))
            dx2_ref[rows, :] = dx2
            dx2b_ref[rows, :] = dx2.astype(BF16)
            s_loss, s_dg = s_loss + _fold8(diff * diff), s_dg + _fold8(dy * xn)
        loss_ref[...] += jnp.sum(s_loss) * (0.5 / D)
        dg_ref[...] += jnp.sum(s_dg, axis=0, keepdims=True)

    row = pl.BlockSpec((tm, D), lambda i: (i, 0))
    return pl.pallas_call(
        body, name="ffn_fwd", grid=(t_len // tm,),
        in_specs=[row, row, row, pl.BlockSpec((8, D), lambda i: (0, 0)), pl.BlockSpec((1, D), lambda i: (0, 0)),
                  _resident((2, 4, FB, D)), _resident((DFF, D))],
        out_specs=[pl.BlockSpec((2, 4, tm, FB), lambda i: (0, 0, i, 0)), row, row,
                   pl.BlockSpec((1, 128), lambda i: (0, 0)), pl.BlockSpec((1, D), lambda i: (0, 0))],
        out_shape=[jax.ShapeDtypeStruct((2, 4, t_len, FB), BF16), jax.ShapeDtypeStruct((t_len, D), F32),
                   jax.ShapeDtypeStruct((t_len, D), BF16),
                   jax.ShapeDtypeStruct((1, 128), F32), jax.ShapeDtypeStruct((1, D), F32)],
        scratch_shapes=[pltpu.VMEM((tm, D), F32)],
        compiler_params=_cp("arbitrary"),
    )(h2, x1, target, mod, g_fin, w_gu, w_down)


S_SH, S_SC, S_G = 0, 1, 2


def _norm_bwd_rows(span, sums, dh_ref, x_ref, dres_ref, scale, gain, write):
    gs = 1.0 + scale
    s_sh, s_sc, s_g = sums
    for r0 in range(span.start, span.stop, 16):
        rows = slice(r0, r0 + 16)
        dh = dh_ref[rows, :]
        xv = x_ref[rows, :]
        r = lax.rsqrt(jnp.mean(xv * xv, axis=-1, keepdims=True) + EPS)
        xn = xv * r
        dhn = dh * gs
        dxn = dhn * gain
        write(rows, dres_ref[rows, :] + r * (dxn - xn * jnp.mean(dxn * xn, axis=-1, keepdims=True)))
        s_sh, s_sc, s_g = s_sh + _fold8(dh), s_sc + _fold8(dh * (xn * gain)), s_g + _fold8(dhn * xn)
    return s_sh, s_sc, s_g


def _add_norm_sums(sums_ref, sums):
    for dst, s in zip((S_SH, S_SC, S_G), sums):
        sums_ref[dst:dst + 1, :] += jnp.sum(s, axis=0, keepdims=True)


def _ffn_bwd(dx2, gu, x1, mod, g_ffn, w_gu, w_down, w_out):
    t_len = x1.shape[0]
    tm = min(TMF, t_len)

    def body(dx2_ref, gu_ref, x1_ref, mod_ref, g_ref, wgu_ref, wd_ref, wo_ref,
             dgu_ref, act_ref, dx1_ref, dx1b_ref, dmg_ref, sums_ref, acc, dmo, dact_s):
        @pl.when(pl.program_id(0) == 0)
        def _():
            sums_ref[...] = jnp.zeros_like(sums_ref)

        dffn = (dx2_ref[...] * mod_ref[5:6, :]).astype(BF16)
        dact_s[0] = _dot_nt(dffn, wd_ref[0:FB, :])
        for j in range(4):
            if j < 3:
                dact_s[(j + 1) % 2] = _dot_nt(dffn, wd_ref[(j + 1) * FB:(j + 2) * FB, :])
            for r0 in range(0, tm, 16):
                rows = slice(r0, r0 + 16)
                dact = dact_s[j % 2, rows, :]
                gate = gu_ref[0, j, rows, :].astype(F32)
                up = gu_ref[1, j, rows, :].astype(F32)
                sg = _sig(gate)
                silu = gate * sg
                act_ref[j, rows, :] = (silu * up).astype(BF16)
                dgu_ref[0, j, rows, :] = (dact * up * (sg * (1.0 + gate * (1.0 - sg)))).astype(BF16)
                dgu_ref[1, j, rows, :] = (dact * silu).astype(BF16)
            part = _dot(dgu_ref[0, j], wgu_ref[0, j]) + _dot(dgu_ref[1, j], wgu_ref[1, j])
            if j == 0:
                acc[...] = part
            else:
                acc[...] += part

        gt1 = mod_ref[2:3, :]

        def write(rows, dx1):
            dx1_ref[rows, :] = dx1
            dx1b_ref[rows, :] = dx1.astype(BF16)
            dmo[rows, :] = (dx1 * gt1).astype(BF16)

        zero = jnp.zeros((8, D), F32)
        sums = (zero, zero, zero)
        for sub in (slice(0, tm // 2), slice(tm // 2, tm)):
            sums = _norm_bwd_rows(sub, sums, acc, x1_ref, dx2_ref, mod_ref[4:5, :], g_ref[...], write)
            dmg_ref[sub, :] = _dot_nt(dmo[sub, :], wo_ref[...]).astype(BF16)
        _add_norm_sums(sums_ref, sums)

    row = pl.BlockSpec((tm, D), lambda i: (i, 0))
    return pl.pallas_call(
        body, name="ffn_bwd", grid=(t_len // tm,),
        in_specs=[row, pl.BlockSpec((2, 4, tm, FB), lambda i: (0, 0, i, 0)), row,
                  pl.BlockSpec((8, D), lambda i: (0, 0)), pl.BlockSpec((1, D), lambda i: (0, 0)),
                  _resident((2, 4, FB, D)), _resident((DFF, D)), _resident((D, D))],
        out_specs=[pl.BlockSpec((2, 4, tm, FB), lambda i: (0, 0, i, 0)),
                   pl.BlockSpec((4, tm, FB), lambda i: (0, i, 0)), row, row, row,
                   pl.BlockSpec((8, D), lambda i: (0, 0))],
        out_shape=[jax.ShapeDtypeStruct((2, 4, t_len, FB), BF16), jax.ShapeDtypeStruct((4, t_len, FB), BF16),
                   jax.ShapeDtypeStruct((t_len, D), F32), jax.ShapeDtypeStruct((t_len, D), BF16),
                   jax.ShapeDtypeStruct((t_len, D), BF16), jax.ShapeDtypeStruct((8, D), F32)],
        scratch_shapes=[pltpu.VMEM((tm, D), F32), pltpu.VMEM((tm, D), BF16), pltpu.VMEM((2, tm, FB), F32)],
        compiler_params=_cp("arbitrary"),
    )(dx2, gu, x1, mod, g_ffn, w_gu, w_down, w_out)


def _my_pos():
    return lax.axis_index("x"), lax.axis_index("y"), lax.axis_index("c")


def _my_index():
    x, y, c = _my_pos()
    return 4 * x + 2 * y + c


def _device_of(b):
    return (b >> 2) & 1, (b >> 1) & 1, b & 1


def _rs_send(src, parts_ref, b, send_sems, recv_sems, local_sem):
    me = _my_index()
    dst = parts_ref.at[me]

    @pl.when(b == me)
    def _():
        pltpu.make_async_copy(src, dst, local_sem).start()

    @pl.when(b != me)
    def _():
        pltpu.make_async_remote_copy(src_ref=src, dst_ref=dst, send_sem=send_sems.at[b], recv_sem=recv_sems.at[me],
                                     device_id=_device_of(b), device_id_type=MESH).start()


def _rs_finish(src_of, parts_ref, send_sems, recv_sems, local_sem):
    me = _my_index()
    for s in range(NDEV):
        @pl.when(s != me)
        def _():
            cp = pltpu.make_async_remote_copy(src_ref=src_of(s), dst_ref=parts_ref.at[s], send_sem=send_sems.at[s],
                                              recv_sem=recv_sems.at[s], device_id=_device_of(s), device_id_type=MESH)
            cp.wait_send()
            cp.wait_recv()

        @pl.when(s == me)
        def _():
            pltpu.make_async_copy(src_of(s), parts_ref.at[s], local_sem).wait()


_RS_SEMS = [pltpu.SemaphoreType.DMA((NDEV,)), pltpu.SemaphoreType.DMA((NDEV,)), pltpu.SemaphoreType.DMA]
_ANY = pl.BlockSpec(memory_space=pl.ANY)


def _xor_order(me, n):
    return (me ^ (n - 1 - jnp.arange(n, dtype=jnp.int32))).astype(jnp.int32)


NCHIP = NDEV // 2


def _rs2_scratch(half_shape):
    blocks = lambda *lead: pltpu.VMEM(lead + tuple(half_shape), BF16)
    return [blocks(NCHIP, 2), blocks(NCHIP)] + [pltpu.SemaphoreType.DMA((NCHIP,))] * 4 + [pltpu.SemaphoreType.DMA]


def _rs2_to_sibling(q, rs):
    stage, from_sib, d_send, d_recv = rs[:4]
    x, y, c = _my_pos()
    pltpu.make_async_remote_copy(src_ref=stage.at[q, 1 - c], dst_ref=from_sib.at[q], send_sem=d_send.at[q],
                                 recv_sem=d_recv.at[q], device_id=(x, y, 1 - c), device_id_type=MESH).start()


def _rs2_forward(q, parts_ref, rs):
    stage, chip_sum, d_send, d_recv, i_send, i_recv, local_sem = rs
    x, y, c = _my_pos()
    my_chip = 2 * x + y
    pltpu.make_async_remote_copy(src_ref=stage.at[q, c], dst_ref=chip_sum.at[q], send_sem=d_send.at[q],
                                 recv_sem=d_recv.at[q], device_id=(x, y, 1 - c), device_id_type=MESH).wait_recv()
    chip_sum[q] = (stage[q, c].astype(F32) + chip_sum[q].astype(F32)).astype(BF16)

    @pl.when(q == my_chip)
    def _():
        pltpu.make_async_copy(chip_sum.at[q], parts_ref.at[my_chip], local_sem).start()

    @pl.when(q != my_chip)
    def _():
        pltpu.make_async_remote_copy(src_ref=chip_sum.at[q], dst_ref=parts_ref.at[my_chip], send_sem=i_send.at[q],
                                     recv_sem=i_recv.at[my_chip], device_id=((q >> 1) & 1, q & 1, c),
                                     device_id_type=MESH).start()


def _rs2_finish(parts_ref, rs):
    stage, chip_sum, d_send, d_recv, i_send, i_recv, local_sem = rs
    x, y, c = _my_pos()
    my_chip = 2 * x + y
    for q in range(NCHIP):
        pltpu.make_async_remote_copy(src_ref=stage.at[q, 1 - c], dst_ref=chip_sum.at[q], send_sem=d_send.at[q],
                                     recv_sem=d_recv.at[q], device_id=(x, y, 1 - c), device_id_type=MESH).wait_send()

        @pl.when(q != my_chip)
        def _():
            cp = pltpu.make_async_remote_copy(src_ref=chip_sum.at[q], dst_ref=parts_ref.at[q], send_sem=i_send.at[q],
                                              recv_sem=i_recv.at[q], device_id=((q >> 1) & 1, q & 1, c),
                                              device_id_type=MESH)
            cp.wait_send()
            cp.wait_recv()

        @pl.when(q == my_chip)
        def _():
            pltpu.make_async_copy(chip_sum.at[q], parts_ref.at[q], local_sem).wait()


def _gu_wgrad(h2, dgu, order):
    t_len = h2.shape[0]
    tk = min(TK, t_len)
    nk = t_len // tk

    def body(ord_ref, h_ref, d_ref, parts_ref, acc, *rs):
        p, k = pl.program_id(0), pl.program_id(1)

        @pl.when(k == 0)
        def _():
            acc[...] = jnp.zeros_like(acc)

        hb = h_ref[...]
        for half in range(2):
            acc[half] += _dot_tn(d_ref[0, half], hb)

        @pl.when(k == nk - 1)
        def _():
            q = ord_ref[p]
            rs[0][q] = acc[...].astype(BF16)
            _rs2_to_sibling(q, rs)

        @pl.when((k == min(1, nk - 1)) & (p > 0))
        def _():
            _rs2_forward(ord_ref[p - 1], parts_ref, rs)

        @pl.when((p == NCHIP - 1) & (k == nk - 1))
        def _():
            _rs2_forward(ord_ref[p], parts_ref, rs)
            _rs2_finish(parts_ref, rs)

    return pl.pallas_call(
        body, name="gu_wgrad",
        grid_spec=pltpu.PrefetchScalarGridSpec(
            num_scalar_prefetch=1, grid=(NCHIP, nk),
            in_specs=[pl.BlockSpec((tk, D), lambda p, k, o: (k, 0)),
                      pl.BlockSpec((1, 2, tk, FB), lambda p, k, o: (o[p], 0, k, 0))],
            out_specs=_ANY,
            scratch_shapes=[pltpu.VMEM((2, FB, D), F32)] + _rs2_scratch((FB, D))),
        out_shape=jax.ShapeDtypeStruct((NCHIP, FB, D), BF16),
        compiler_params=_cp("arbitrary", "arbitrary"),
    )(order, h2, dgu.reshape(NCHIP, 2, t_len, FB))


def _scaled_wgrad(name, a, dx, w, gate_row, mod, order):
    t_len = dx.shape[0]
    kb = w.shape[0] // NCHIP
    tk = min(TK, t_len)
    nk = t_len // tk
    rows = kb // 2
    if a.ndim == 3:
        a_spec = pl.BlockSpec((None, tk, kb), lambda p, k, o: (o[p], k, 0))
    else:
        a_spec = pl.BlockSpec((tk, kb), lambda p, k, o: (k, o[p]))

    def body(ord_ref, a_ref, dx_ref, w_ref, mod_ref, parts_ref, dg_ref, acc, *rs):
        p, k = pl.program_id(0), pl.program_id(1)

        @pl.when((p == 0) & (k == 0))
        def _():
            dg_ref[...] = jnp.zeros_like(dg_ref)

        @pl.when(k == 0)
        def _():
            acc[...] = jnp.zeros_like(acc)

        acc[...] += _dot_tn(a_ref[...], dx_ref[...])

        @pl.when(k == nk - 1)
        def _():
            q = ord_ref[p]
            z = acc[...]
            zg = (z * mod_ref[gate_row:gate_row + 1, :]).astype(BF16)
            dg_ref[0:1, :] += jnp.sum(z * w_ref[...].astype(F32), axis=0, keepdims=True)
            for half in range(2):
                rs[0][q, half] = zg[half * rows:(half + 1) * rows]
            _rs2_to_sibling(q, rs)

        @pl.when((k == min(1, nk - 1)) & (p > 0))
        def _():
            _rs2_forward(ord_ref[p - 1], parts_ref, rs)

        @pl.when((p == NCHIP - 1) & (k == nk - 1))
        def _():
            _rs2_forward(ord_ref[p], parts_ref, rs)
            _rs2_finish(parts_ref, rs)

    return pl.pallas_call(
        body, name=name,
        grid_spec=pltpu.PrefetchScalarGridSpec(
            num_scalar_prefetch=1, grid=(NCHIP, nk),
            in_specs=[a_spec,
                      pl.BlockSpec((tk, D), lambda p, k, o: (k, 0)),
                      pl.BlockSpec((kb, D), lambda p, k, o: (o[p], 0)),
                      pl.BlockSpec((8, D), lambda p, k, o: (0, 0))],
            out_specs=[_ANY, pl.BlockSpec((8, D), lambda p, k, o: (0, 0))],
            scratch_shapes=[pltpu.VMEM((kb, D), F32)] + _rs2_scratch((rows, D))),
        out_shape=[jax.ShapeDtypeStruct((NCHIP, rows, D), BF16), jax.ShapeDtypeStruct((8, D), F32)],
        compiler_params=_cp("arbitrary", "arbitrary"),
    )(order, a, dx, w, mod)


M_WA, M_WB, M_CBIAS, M_BA, M_BX, M_LS = 0, 3, 7, 8, 9, 10


def _conv_bwd_rows(tt, proj_ref, prm_ref, xe, ve, due, dye, dp_ref, acc8):
    row = lax.broadcasted_iota(jnp.int32, (8, CG), 0)
    w_b = [prm_ref[P_WB + k:P_WB + k + 1, :] for k in range(4)]
    w_a = [prm_ref[P_WA + k:P_WA + k + 1, :] for k in range(3)]

    def blk(ib, carry):
        r0 = pl.multiple_of(ib * 16, 16)
        rows = pl.ds(r0, 16)
        for g in range(D // CG):
            cs = slice(g * CG, (g + 1) * CG)
            du16, du_after = due[rows, cs], due[pl.ds(r0 + 16, 8), cs]
            dy16, dy_after = dye[rows, cs], dye[pl.ds(r0 + 16, 8), cs]
            cc16 = _pj(proj_ref, 1, rows, cs).astype(F32)
            cx16 = _pj(proj_ref, 2, rows, cs).astype(F32)
            x16 = _pj(proj_ref, 3, rows, cs).astype(F32)
            v16 = cc16 * cx16
            xp, vp = xe[pl.ds(r0, 8), cs], ve[pl.ds(r0, 8), cs]
            xe[pl.ds(r0 + 16, 8), cs] = x16[8:16]
            ve[pl.ds(r0 + 16, 8), cs] = v16[8:16]
            acc = [acc8[8 * k:8 * k + 8, cs] for k in range(8)]
            drx, dv = [], []
            for sb in range(2):
                lo = slice(8 * sb, 8 * sb + 8)
                duc, dyc, xc, vc = du16[lo], dy16[lo], x16[lo], v16[lo]
                du_n = du16[8:16] if sb == 0 else du_after
                dy_n = dy16[8:16] if sb == 0 else dy_after
                acc[0] = acc[0] + duc
                acc[4] = acc[4] + duc * xc
                d8 = w_b[3][:, cs] * duc
                for s in (1, 2, 3):
                    acc[4 - s] = acc[4 - s] + duc * _shift_down(xc, xp, s, row)
                    d8 = d8 + w_b[3 - s][:, cs] * _shift_up(duc, du_n, s, row)
                acc[7] = acc[7] + dyc * vc
                e8 = w_a[2][:, cs] * dyc
                for s in (1, 2):
                    acc[7 - s] = acc[7 - s] + dyc * _shift_down(vc, vp, s, row)
                    e8 = e8 + w_a[2 - s][:, cs] * _shift_up(dyc, dy_n, s, row)
                drx.append(d8)
                dv.append(e8)
                xp, vp = xc, vc
            for k in range(8):
                acc8[8 * k:8 * k + 8, cs] = acc[k]
            dv16 = jnp.concatenate(dv, axis=0)
            col = lambda s: slice(s * D + g * CG, s * D + (g + 1) * CG)
            dp_ref[rows, col(3)] = jnp.concatenate(drx, axis=0).astype(BF16)
            dp_ref[rows, col(1)] = (dv16 * cx16).astype(BF16)
            dp_ref[rows, col(2)] = (dv16 * cc16).astype(BF16)
        return carry

    lax.fori_loop(0, tt // 16, blk, 0)


def _mixer_bwd(proj, hl, sv, dmg, prm, wa, wx):
    t_len = proj.shape[0]
    tt = min(TT, t_len)
    nt = t_len // tt
    hb8 = tt // 8

    def rev(i):
        return nt - 1 - i

    def halo(i):
        return jnp.maximum(rev(i) * hb8 - 1, 0)

    def body(proj_ref, ph_ref, hl_ref, hh_ref, sv_ref, dmg_ref, prm_ref, wa_ref, wx_ref,
             dp_ref, sums_ref, gwa_ref, gwx_ref,
             xe, ve, he, due, dye, drp_s, dip_s, an, gn, acc8):
        i = pl.program_id(0)
        t = rev(i)

        @pl.when(i == 0)
        def _():
            sums_ref[...] = jnp.zeros_like(sums_ref)
            gwa_ref[...] = jnp.zeros_like(gwa_ref)
            gwx_ref[...] = jnp.zeros_like(gwx_ref)
            due[tt:tt + 8, :] = jnp.zeros((8, D), F32)
            dye[tt:tt + 8, :] = jnp.zeros((8, D), F32)
            an[...] = jnp.zeros((8, D), F32)
            gn[...] = jnp.zeros((8, D), F32)

        live = (t > 0).astype(F32)
        xe[0:8, :] = _pj(ph_ref, 3).astype(F32) * live
        ve[0:8, :] = _pj(ph_ref, 1).astype(F32) * _pj(ph_ref, 2).astype(F32) * live
        he[0:8, :] = hh_ref[...].astype(F32) * live
        he[8:8 + tt, :] = hl_ref[...].astype(F32)

        ls_all = _log_sigmoid(prm_ref[P_LAM:P_LAM + 1, :])
        row = lax.broadcasted_iota(jnp.int32, (8, CG), 0)
        nblk = tt // 16

        def blk(ib, carry):
            r0 = pl.multiple_of((nblk - 1 - ib) * 16, 16)
            rows = pl.ds(r0, 16)
            for g in range(D // CG):
                cs = slice(g * CG, (g + 1) * CG)
                ls = ls_all[:, cs]
                dm = dmg_ref[rows, cs].astype(F32)
                cb = _pj(proj_ref, 0, rows, cs).astype(F32)
                rg = _pj(proj_ref, 4, rows, cs).astype(F32)
                sga = _sig(_pj(proj_ref, 5, rows, cs).astype(F32))
                sgb = _sig(_pj(proj_ref, 6, rows, cs).astype(F32))
                ya0 = sv_ref[SV_YA, rows, cs]
                h16 = he[pl.ds(r0 + 8, 16), cs]
                gl, th = _gelu(rg)
                dgl = 0.5 * (1.0 + th) + 0.5 * rg * (1.0 - th * th) * (_GC * (1.0 + 3.0 * 0.044715 * rg * rg))
                y_a = cb * ya0
                y_b = h16 * gl
                dy_a = dm * sga
                dy_b = dm * sgb
                col = lambda s: slice(s * D + g * CG, s * D + (g + 1) * CG)
                dp_ref[rows, col(5)] = (dm * y_a * sga * (1.0 - sga)).astype(BF16)
                dp_ref[rows, col(6)] = (dm * y_b * sgb * (1.0 - sgb)).astype(BF16)
                dp_ref[rows, col(4)] = (dy_b * h16 * dgl).astype(BF16)
                dp_ref[rows, col(0)] = (dy_a * ya0).astype(BF16)
                dye[rows, cs] = dy_a * cb
                dh16 = dy_b * gl

                a_next = an[:, cs]
                g_next = gn[:, cs]
                s_ba = jnp.zeros((8, CG), F32)
                s_bx = jnp.zeros((8, CG), F32)
                s_ls = jnp.zeros((8, CG), F32)
                for sb in (1, 0):
                    rr = r0 + 8 * sb
                    first = (row + (t * tt + rr)) == 0
                    r8 = pl.ds(rr, 8)
                    uu, r, ig, a, mult = (sv_ref[pln, r8, cs] for pln in (SV_U, SV_R, SV_I, SV_A, SV_MULT))
                    ca = jnp.where(row < 7, pltpu.roll(a, 7, 0), a_next)
                    cb_ = dh16[8 * sb:8 * sb + 8, :]
                    for s in (1, 2, 4):
                        a_sh = jnp.where(row < 8 - s, pltpu.roll(ca, 8 - s, 0), 1.0)
                        b_sh = jnp.where(row < 8 - s, pltpu.roll(cb_, 8 - s, 0), 0.0)
                        cb_ = ca * b_sh + cb_
                        ca = ca * a_sh
                    gv = ca * g_next + cb_
                    g_next = jnp.broadcast_to(gv[0:1, :], gv.shape)
                    a_next = jnp.broadcast_to(a[0:1, :], a.shape)
                    hprev = jnp.where(row >= 1, pltpu.roll(he[pl.ds(rr + 8, 8), cs], 1, 0),
                                      pltpu.roll(he[pl.ds(rr, 8), cs], 1, 0))
                    da = gv * hprev
                    dmult = jnp.where(first, 0.0, gv * ig * uu)
                    dla = da * a + jnp.where(mult > 0.0, dmult * (-(a * a) / mult), 0.0)
                    drp = dla * (LRU_C * ls) * r * (1.0 - r)
                    dip = gv * mult * uu * ig * (1.0 - ig)
                    s_ls = s_ls + dla * (LRU_C * r)
                    s_ba = s_ba + drp
                    s_bx = s_bx + dip
                    drp_s[pl.ds(rr, 8), cs] = drp
                    dip_s[pl.ds(rr, 8), cs] = dip
                    due[pl.ds(rr, 8), cs] = gv * mult * ig
                an[:, cs] = a_next
                gn[:, cs] = g_next
                sums_ref[M_BA:M_BA + 1, cs] += jnp.sum(s_ba, axis=0, keepdims=True)
                sums_ref[M_BX:M_BX + 1, cs] += jnp.sum(s_bx, axis=0, keepdims=True)
                sums_ref[M_LS:M_LS + 1, cs] += jnp.sum(s_ls, axis=0, keepdims=True)
            return carry

        lax.fori_loop(0, nblk, blk, 0)

        drp_b = drp_s[...].astype(BF16)
        dip_b = dip_s[...].astype(BF16)
        ub = sv_ref[SV_U].astype(BF16)
        for h in range(HEADS):
            cs = slice(h * HB, (h + 1) * HB)
            due[0:tt, cs] += _dot_nt(drp_b[:, cs], wa_ref[h]) + _dot_nt(dip_b[:, cs], wx_ref[h])
            gwa_ref[h] += _dot_tn(ub[:, cs], drp_b[:, cs])
            gwx_ref[h] += _dot_tn(ub[:, cs], dip_b[:, cs])

        acc8[...] = jnp.zeros_like(acc8)
        _conv_bwd_rows(tt, proj_ref, prm_ref, xe, ve, due, dye, dp_ref, acc8)
        for k, dst in enumerate([M_CBIAS] + [M_WB + k for k in range(4)] + [M_WA + k for k in range(3)]):
            sums_ref[dst:dst + 1, :] += jnp.sum(acc8[8 * k:8 * k + 8, :], axis=0, keepdims=True)
        due[tt:tt + 8, :] = due[0:8, :]
        dye[tt:tt + 8, :] = dye[0:8, :]

        @pl.when(i == nt - 1)
        def _():
            sums_ref[M_LS:M_LS + 1, :] = sums_ref[M_LS:M_LS + 1, :] * _sig(-prm_ref[P_LAM:P_LAM + 1, :])

    big = lambda: pltpu.VMEM((tt + 8, D), F32)
    tile = lambda: pltpu.VMEM((tt, D), F32)
    return pl.pallas_call(
        body, name="mixer_bwd", grid=(nt,),
        in_specs=[pl.BlockSpec((tt, 7 * D), lambda i: (rev(i), 0)),
                  pl.BlockSpec((8, 7 * D), lambda i: (halo(i), 0)),
                  pl.BlockSpec((tt, D), lambda i: (rev(i), 0)),
                  pl.BlockSpec((8, D), lambda i: (halo(i), 0)),
                  pl.BlockSpec((len(SV_PLANES), tt, D), lambda i: (0, rev(i), 0)),
                  pl.BlockSpec((tt, D), lambda i: (rev(i), 0)),
                  pl.BlockSpec((16, D), lambda i: (0, 0)),
                  pl.BlockSpec((HEADS, HB, HB), lambda i: (0, 0, 0)),
                  pl.BlockSpec((HEADS, HB, HB), lambda i: (0, 0, 0))],
        out_specs=[pl.BlockSpec((tt, 7 * D), lambda i: (rev(i), 0)),
                   pl.BlockSpec((16, D), lambda i: (0, 0)),
                   pl.BlockSpec((HEADS, HB, HB), lambda i: (0, 0, 0)),
                   pl.BlockSpec((HEADS, HB, HB), lambda i: (0, 0, 0))],
        out_shape=[jax.ShapeDtypeStruct((t_len, 7 * D), BF16), jax.ShapeDtypeStruct((16, D), F32),
                   jax.ShapeDtypeStruct((HEADS, HB, HB), F32), jax.ShapeDtypeStruct((HEADS, HB, HB), F32)],
        scratch_shapes=[big(), big(), big(), big(), big(), tile(), tile(),
                        pltpu.VMEM((8, D), F32), pltpu.VMEM((8, D), F32), pltpu.VMEM((64, D), F32)],
        compiler_params=_cp("arbitrary"),
    )(proj, proj, hl, hl, sv, dmg, prm, wa, wx)


def _in_proj_bwd(dproj, w_in, x, dx1, mod, g_mix):
    t_len = x.shape[0]
    tm = min(TM, t_len)

    def body(dp_ref, w_ref, x_ref, dx1_ref, mod_ref, g_ref, gx_ref, sums_ref, acc):
        @pl.when(pl.program_id(0) == 0)
        def _():
            sums_ref[...] = jnp.zeros_like(sums_ref)

        def write(rows, dx):
            gx_ref[rows, :] = dx

        zero = jnp.zeros((8, D), F32)
        sums = (zero, zero, zero)
        for sub in _sub_blocks(tm):
            acc[sub, :] = _dot_nt(dp_ref[sub, :], w_ref[...])
            sums = _norm_bwd_rows(sub, sums, acc, x_ref, dx1_ref, mod_ref[1:2, :], g_ref[...], write)
        _add_norm_sums(sums_ref, sums)

    return pl.pallas_call(
        body, name="in_proj_bwd", grid=(t_len // tm,),
        in_specs=[pl.BlockSpec((tm, 7 * D), lambda i: (i, 0)),
                  _resident((D, 7 * D)),
                  pl.BlockSpec((tm, D), lambda i: (i, 0)), pl.BlockSpec((tm, D), lambda i: (i, 0)),
                  pl.BlockSpec((8, D), lambda i: (0, 0)), pl.BlockSpec((1, D), lambda i: (0, 0))],
        out_specs=[pl.BlockSpec((tm, D), lambda i: (i, 0)), pl.BlockSpec((8, D), lambda i: (0, 0))],
        out_shape=[jax.ShapeDtypeStruct((t_len, D), F32), jax.ShapeDtypeStruct((8, D), F32)],
        scratch_shapes=[pltpu.VMEM((tm, D), F32)],
        compiler_params=_cp("arbitrary"),
    )(dproj, w_in, x, dx1, mod, g_mix)


def _in_wgrad(h, dproj, g_wa, g_wx, order):
    t_len = h.shape[0]
    tk = min(TKI, t_len)
    nk = t_len // tk
    cw = 7 * D // NDEV
    hr = HB // NDEV

    def body(ord_ref, h_ref, d_ref, ga_ref, gx_ref, parts_ref, pa_ref, px_ref, acc, *scr):
        rs, sems = scr[:-6], scr[-6:]
        p, k = pl.program_id(0), pl.program_id(1)

        def head_rows(ref):
            return lambda s: ref.at[:, pl.ds(s * hr, hr), :]

        @pl.when((p == 0) & (k == 0))
        def _():
            for s in range(NDEV):
                _rs_send(head_rows(ga_ref)(s), pa_ref, s, *sems[0:3])
                _rs_send(head_rows(gx_ref)(s), px_ref, s, *sems[3:6])

        @pl.when(k == 0)
        def _():
            acc[...] = jnp.zeros_like(acc)

        acc[...] += _dot_tn(h_ref[...], d_ref[...])

        @pl.when(k == nk - 1)
        def _():
            q = ord_ref[p]
            for half in range(2):
                rs[0][q, half] = acc[:, half * cw:(half + 1) * cw].astype(BF16)
            _rs2_to_sibling(q, rs)

        @pl.when((k == min(1, nk - 1)) & (p > 0))
        def _():
            _rs2_forward(ord_ref[p - 1], parts_ref, rs)

        @pl.when((p == NCHIP - 1) & (k == nk - 1))
        def _():
            _rs2_forward(ord_ref[p], parts_ref, rs)
            _rs2_finish(parts_ref, rs)
            _rs_finish(head_rows(ga_ref), pa_ref, *sems[0:3])
            _rs_finish(head_rows(gx_ref), px_ref, *sems[3:6])

    return pl.pallas_call(
        body, name="in_wgrad",
        grid_spec=pltpu.PrefetchScalarGridSpec(
            num_scalar_prefetch=1, grid=(NCHIP, nk),
            in_specs=[pl.BlockSpec((tk, D), lambda p, k, o: (k, 0)),
                      pl.BlockSpec((tk, 2 * cw), lambda p, k, o: (k, o[p])), _ANY, _ANY],
            out_specs=[_ANY, _ANY, _ANY],
            scratch_shapes=[pltpu.VMEM((D, 2 * cw), F32)] + _rs2_scratch((D, cw)) + _RS_SEMS * 2),
        out_shape=[jax.ShapeDtypeStruct((NCHIP, D, cw), BF16), jax.ShapeDtypeStruct((NDEV, HEADS, hr, HB), F32),
                   jax.ShapeDtypeStruct((NDEV, HEADS, hr, HB), F32)],
        compiler_params=_cp("arbitrary", "arbitrary"),
    )(order, h, dproj, g_wa, g_wx)


def _adam_math(w, g, m, v):
    m = ADAM_B1 * m + (1.0 - ADAM_B1) * g
    v = ADAM_B2 * v + (1.0 - ADAM_B2) * (g * g)
    m_hat = m / (1.0 - ADAM_B1 ** ADAM_STEP)
    v_hat = v / (1.0 - ADAM_B2 ** ADAM_STEP)
    delta = -ADAM_LR * (m_hat / (jnp.sqrt(v_hat) + ADAM_EPS) + ADAM_WD * w)
    return delta, m, v


def _ada_bwd(c_all, dmod_cols, w, m, v):
    rb = 256
    n = w.shape[1]
    nrow = c_all.shape[0]

    def body(c_ref, d_ref, w_ref, m_ref, v_ref, g_ref, dl_ref, nm_ref, nv_ref):
        cv = c_ref[...]
        g = _dot_tn((cv * _sig(cv)).astype(BF16), d_ref[...].astype(BF16))
        g_ref[...] = g
        dl_ref[...], nm_ref[...], nv_ref[...] = _adam_math(w_ref[...], g, m_ref[...], v_ref[...])

    blk = pl.BlockSpec((rb, n), lambda i: (i, 0))
    sds = jax.ShapeDtypeStruct(w.shape, F32)
    return pl.pallas_call(
        body, name="ada_bwd", grid=(D // rb,),
        in_specs=[pl.BlockSpec((nrow, rb), lambda i: (0, i)), pl.BlockSpec((nrow, n), lambda i: (0, 0)), blk, blk, blk],
        out_specs=[blk, blk, blk, blk], out_shape=[sds, sds, sds, sds],
        compiler_params=_cp("parallel"),
    )(c_all, dmod_cols, w, m, v)


def _adam(name, parts, w, m, v):
    p, r, c = parts.shape
    rb = r
    for cand in (256, 128, 64, 32, 16, 8):
        if r % cand == 0 and r >= cand:
            rb = cand
            break

    def body(p_ref, w_ref, m_ref, v_ref, g_ref, dl_ref, nm_ref, nv_ref):
        g = p_ref[0].astype(F32)
        for q in range(1, p):
            g = g + p_ref[q].astype(F32)
        g_ref[...] = g
        dl_ref[...], nm_ref[...], nv_ref[...] = _adam_math(w_ref[...], g, m_ref[...], v_ref[...])

    blk = pl.BlockSpec((rb, c), lambda i: (i, 0))
    sds = jax.ShapeDtypeStruct((r, c), F32)
    return pl.pallas_call(
        body, name=name, grid=(r // rb,),
        in_specs=[pl.BlockSpec((p, rb, c), lambda i: (0, i, 0)), blk, blk, blk],
        out_specs=[blk, blk, blk, blk], out_shape=[sds, sds, sds, sds],
        compiler_params=_cp("parallel"),
    )(parts, w, m, v)


_SMALL_SEMS = [pltpu.SemaphoreType.DMA((7,)), pltpu.SemaphoreType.DMA((7,)), pltpu.SemaphoreType.DMA]
_VMEM = pl.BlockSpec(memory_space=pltpu.VMEM)


def _exchange_small(x_ref, out_ref, send_sems, recv_sems, local_sem):
    m_per = x_ref.shape[0]
    x, y, c = _my_pos()
    me, sibling = (x, y, c), (x, y, 1 - c)
    chips = [(1 - x, y), (x, 1 - y), (1 - x, 1 - y)]

    def rows(px, py, pc):
        return out_ref.at[pl.ds((4 * px + 2 * py + pc) * m_per, m_per), :]

    def copy(k, block, to, src=None):
        return pltpu.make_async_remote_copy(
            src_ref=rows(*block) if src is None else src, dst_ref=rows(*block),
            send_sem=send_sems.at[k], recv_sem=recv_sems.at[k], device_id=to, device_id_type=MESH)

    mine = pltpu.make_async_copy(x_ref, rows(*me), local_sem)
    mine.start()
    first = [copy(0, me, sibling, src=x_ref)]
    first += [copy(1 + j, me, (*chip, c), src=x_ref) for j, chip in enumerate(chips)]
    for cp in first:
        cp.start()
    passed = [copy(4 + j, (*chip, c), sibling) for j, chip in enumerate(chips)]
    for j, chip in enumerate(chips):
        copy(1 + j, (*chip, c), me).wait_recv()
        passed[j].start()
    copy(0, sibling, me).wait_recv()
    for j, chip in enumerate(chips):
        copy(4 + j, (*chip, 1 - c), me).wait_recv()
    for cp in first + passed:
        cp.wait_send()
    mine.wait()


def _gather_small_grads(sums1, sums2, msums, d_gt1, d_gt2, d_gfin, loss):
    def body(s1, s2, ms, g1, g2, gf, ls, out_ref, pack, *sems):
        rows = [s1[S_SH:S_SH + 1, :], s1[S_SC:S_SC + 1, :], g1[0:1, :],
                s2[S_SH:S_SH + 1, :], s2[S_SC:S_SC + 1, :], g2[0:1, :],
                s1[S_G:S_G + 1, :], ms[M_CBIAS:M_CBIAS + 1, :], ms[M_BA:M_BA + 1, :], ms[M_BX:M_BX + 1, :],
                ms[M_LS:M_LS + 1, :], s2[S_G:S_G + 1, :], gf[...]]
        rows += [ms[M_WA + k:M_WA + k + 1, :] for k in range(3)] + [ms[M_WB + k:M_WB + k + 1, :] for k in range(4)]
        rows += [jnp.broadcast_to(ls[0:1, 0:1], (1, D))]
        for i, v in enumerate(rows):
            pack[i:i + 1, :] = v
        pack[len(rows):24, :] = jnp.zeros((24 - len(rows), D), F32)
        _exchange_small(pack, out_ref, *sems)

    return pl.pallas_call(
        body, name="gather_small", out_shape=jax.ShapeDtypeStruct((NDEV * 24, D), F32),
        in_specs=[_VMEM] * 7, out_specs=_VMEM, scratch_shapes=[pltpu.VMEM((24, D), F32)] + _SMALL_SEMS,
    )(sums1, sums2, msums, d_gt1, d_gt2, d_gfin, loss)


def _ada_mod(pack, w_ada, b_cols):
    ncol = w_ada.shape[1]

    def body(p_ref, w_ref, b_ref, all_ref, mod_ref, cols, *sems):
        _exchange_small(p_ref, all_ref, *sems[0:3])
        c_all = jnp.concatenate([all_ref[8 * d:8 * d + 1, 0:D] for d in range(NDEV)], axis=0)
        c16 = jnp.concatenate([c_all, jnp.zeros_like(c_all)], axis=0)
        mod16 = _dot((c16 * _sig(c16)).astype(BF16), w_ref[...].astype(BF16)) + b_ref[...]
        cols[...] = mod16[0:NDEV]
        _exchange_small(cols, mod_ref, *sems[3:6])

    return pl.pallas_call(
        body, name="ada_mod",
        out_shape=[jax.ShapeDtypeStruct((NDEV * 8, pack.shape[1]), F32), jax.ShapeDtypeStruct((NDEV * 8, ncol), F32)],
        in_specs=[_VMEM, _VMEM, _VMEM], out_specs=[_VMEM, _VMEM],
        scratch_shapes=[pltpu.VMEM((NDEV, ncol), F32)] + _SMALL_SEMS * 2,
        compiler_params=_cp(),
    )(pack, w_ada, b_cols)


def _blk_rows(n):
    return lambda ref, b: ref.at[pl.ds(pl.multiple_of(b * n, 8), n), :]


def _blk_lead(ref, b):
    return ref.at[b]


def _blk_heads(ref, b):
    return ref.at[:, pl.ds(pl.multiple_of(b * (HB // NDEV), 8), HB // NDEV), :]


def _ag_phases(ins, outs, slicers, send_sems, recv_sems, local_sems):
    na = len(ins)
    x, y, c = _my_pos()
    me, sibling = (x, y, c), (x, y, 1 - c)
    chips = [(1 - x, y), (x, 1 - y), (1 - x, 1 - y)]

    def copy(a, k, block, to, from_shard=False):
        px, py, pc = block
        dst = slicers[a](outs[a], 4 * px + 2 * py + pc)
        return pltpu.make_async_remote_copy(
            src_ref=ins[a] if from_shard else dst, dst_ref=dst,
            send_sem=send_sems.at[a * 7 + k], recv_sem=recv_sems.at[a * 7 + k], device_id=to, device_id_type=MESH)

    def local(a):
        return pltpu.make_async_copy(ins[a], slicers[a](outs[a], 4 * x + 2 * y + c), local_sems.at[a])

    def firsts(a):
        return [copy(a, 0, me, sibling, True)] + [copy(a, 1 + j, me, (*chip, c), True) for j, chip in enumerate(chips)]

    def start():
        for a in range(na):
            local(a).start()
            for cp in firsts(a):
                cp.start()

    def forward():
        for a in range(na):
            for j, chip in enumerate(chips):
                copy(a, 1 + j, (*chip, c), me).wait_recv()
                copy(a, 4 + j, (*chip, c), sibling).start()

    def finish():
        for a in range(na):
            copy(a, 0, sibling, me).wait_recv()
            for j, chip in enumerate(chips):
                copy(a, 4 + j, (*chip, 1 - c), me).wait_recv()
        for a in range(na):
            for cp in firsts(a) + [copy(a, 4 + j, (*chip, c), sibling) for j, chip in enumerate(chips)]:
                cp.wait_send()
            local(a).wait()

    return start, forward, finish


def _ag_sems(na):
    return [pltpu.SemaphoreType.DMA((7 * na,)), pltpu.SemaphoreType.DMA((7 * na,)), pltpu.SemaphoreType.DMA((na,))]


def _local_step(x, target, mod, g_mix, g_ffn, g_fin, prm, w_in_shard, shards):
    fulls = [(HEADS, HB, HB), (HEADS, HB, HB), (D, D), (NDEV, FB, D), (DFF, D)]
    slicers = [_blk_heads, _blk_heads, _blk_rows(D // NDEV), _blk_lead, _blk_rows(DFF // NDEV)]
    my_chip = _my_index() >> 1
    own_first = (my_chip ^ jnp.arange(NCHIP, dtype=jnp.int32)).astype(jnp.int32)
    early, late = [0, 1, 2, 4], [3]
    pick = lambda lst, idx: [lst[i] for i in idx]
    proj, h, w_in, (wa, wx, w_out, w_down) = _in_proj(x, mod, g_mix, w_in_shard, own_first, pick(shards, early),
                                                      pick(fulls, early), pick(slicers, early))
    merged, hl, sv, (w_gu,) = _mixer_fwd(proj, prm, wa, wx, pick(shards, late), pick(fulls, late),
                                         pick(slicers, late))
    w_gu = w_gu.reshape(2, 4, FB, D)
    x1, h2 = _out_proj(merged, x, mod, g_ffn, w_out)
    gu, dx2, dx2b, loss, d_gfin = _ffn_fwd(h2, x1, target, mod, g_fin, w_gu, w_down)
    dgu, act, dx1, dx1b, dmg, sums2 = _ffn_bwd(dx2, gu, x1, mod, g_ffn, w_gu, w_down, w_out)
    chip_order = _xor_order(_my_index() >> 1, NCHIP)
    p_wgu = _gu_wgrad(h2, dgu, chip_order)
    p_wdown, d_gt2 = _scaled_wgrad("down_wgrad", act, dx2b, w_down, 5, mod, chip_order)
    p_wout, d_gt1 = _scaled_wgrad("out_wgrad", merged, dx1b, w_out, 2, mod, chip_order)
    dproj, msums, g_wa, g_wx = _mixer_bwd(proj, hl, sv, dmg, prm, wa, wx)
    p_win, p_wa, p_wx = _in_wgrad(h, dproj, g_wa, g_wx, chip_order)
    grad_x, sums1 = _in_proj_bwd(dproj, w_in, x, dx1, mod, g_mix)
    return dict(loss=loss, grad_x=grad_x, d_gfin=d_gfin, sums1=sums1, sums2=sums2, msums=msums,
                d_gt1=d_gt1, d_gt2=d_gt2, p_win=p_win, p_wa=p_wa, p_wx=p_wx, p_wout=p_wout, p_wgu=p_wgu,
                p_wdown=p_wdown)


def kernel(x, c, w_ada, b_ada, g_norm_mix, w_in, conv_a_w, conv_b_w, conv_b_bias, w_rg_a, b_rg_a, w_rg_x, b_rg_x, lru_lambda, w_out, g_norm_ffn, w_gate_up, w_down, g_norm_final, loss_target, m_w_ada, m_b_ada, m_g_norm_mix, m_w_in, m_conv_a_w, m_conv_b_w, m_conv_b_bias, m_w_rg_a, m_b_rg_a, m_w_rg_x, m_b_rg_x, m_lru_lambda, m_w_out, m_g_norm_ffn, m_w_gate_up, m_w_down, m_g_norm_final, v_w_ada, v_b_ada, v_g_norm_mix, v_w_in, v_conv_a_w, v_conv_b_w, v_conv_b_bias, v_w_rg_a, v_b_rg_a, v_w_rg_x, v_b_rg_x, v_lru_lambda, v_w_out, v_g_norm_ffn, v_w_gate_up, v_w_down, v_g_norm_final):
    me = 4 * lax.axis_index("x") + 2 * lax.axis_index("y") + lax.axis_index("c")
    ncol = w_ada.shape[2]
    cw = conv_a_w.shape[2]

    pack0 = jnp.concatenate([c, conv_a_w.reshape(1, 3 * cw), conv_b_w.reshape(1, 4 * cw)], axis=1)
    b_cols = lax.dynamic_slice_in_dim(b_ada, me * ncol, ncol, axis=1)
    got0, got1 = _ada_mod(jnp.broadcast_to(pack0, (8, pack0.shape[1])), w_ada[0], b_cols)
    got0 = got0.reshape(NDEV, 8, -1)[:, 0, :]
    c_all = got0[:, :D]
    conv_a = got0[:, D:D + 3 * cw].reshape(NDEV, 3, cw).transpose(1, 0, 2).reshape(3, D)
    conv_b = got0[:, D + 3 * cw:].reshape(NDEV, 4, cw).transpose(1, 0, 2).reshape(4, D)
    c16 = jnp.concatenate([c_all, jnp.zeros((8, D), F32)], axis=0)
    mod6 = lax.dynamic_index_in_dim(got1.reshape(NDEV, NDEV, ncol), me, axis=1, keepdims=False).reshape(6, D)
    mod = jnp.concatenate([mod6, jnp.zeros((2, D), F32)], axis=0)

    tr = lambda a: jnp.swapaxes(a, 1, 2)
    shards = [w_rg_a[0].astype(BF16), w_rg_x[0].astype(BF16), w_out[0].astype(BF16), tr(w_gate_up)[0].astype(BF16),
              w_down[0].astype(BF16)]

    prm = jnp.concatenate([conv_a, conv_b, conv_b_bias, b_rg_a, b_rg_x, lru_lambda, jnp.zeros((5, D), F32)], axis=0)
    r = _local_step(x[0], loss_target[0], mod, g_norm_mix, g_norm_ffn, g_norm_final.reshape(1, D), prm,
                    w_in[0].astype(BF16), shards)

    parts = [r["p_win"], r["p_wa"], r["p_wx"], r["p_wout"], r["p_wgu"], r["p_wdown"]]
    big = {}
    for nm, p, w, m, v in (("w_in", parts[0], w_in, m_w_in, v_w_in), ("w_rg_a", parts[1], w_rg_a, m_w_rg_a, v_w_rg_a),
                           ("w_rg_x", parts[2], w_rg_x, m_w_rg_x, v_w_rg_x), ("w_out", parts[3], w_out, m_w_out, v_w_out),
                           ("w_gate_up", parts[4], tr(w_gate_up), tr(m_w_gate_up), tr(v_w_gate_up)),
                           ("w_down", parts[5], w_down, m_w_down, v_w_down)):
        two_d = (-1, w.shape[-1])
        outs = _adam("adam_" + nm, p.reshape((p.shape[0],) + w.reshape(two_d).shape), w.reshape(two_d), m.reshape(two_d),
                     v.reshape(two_d))
        big[nm] = [o.reshape(w.shape) for o in outs]
    big["w_gate_up"] = [tr(o) for o in big["w_gate_up"]]

    got2 = _gather_small_grads(r["sums1"], r["sums2"], r["msums"], r["d_gt1"], r["d_gt2"], r["d_gfin"],
                               r["loss"]).reshape(NDEV, 24, D)

    rep_w = jnp.concatenate([b_ada.reshape(6, D), g_norm_mix, conv_b_bias, b_rg_a, b_rg_x, lru_lambda, g_norm_ffn,
                             g_norm_final.reshape(1, D), jnp.zeros((3, D), F32)], axis=0)
    rep_m = jnp.concatenate([m_b_ada.reshape(6, D), m_g_norm_mix, m_conv_b_bias, m_b_rg_a, m_b_rg_x, m_lru_lambda,
                             m_g_norm_ffn, m_g_norm_final.reshape(1, D), jnp.zeros((3, D), F32)], axis=0)
    rep_v = jnp.concatenate([v_b_ada.reshape(6, D), v_g_norm_mix, v_conv_b_bias, v_b_rg_a, v_b_rg_x, v_lru_lambda,
                             v_g_norm_ffn, v_g_norm_final.reshape(1, D), jnp.ones((3, D), F32)], axis=0)
    rep = _adam("adam_rep", got2[:, :16, :], rep_w, rep_m, rep_v)

    conv_parts = lax.dynamic_slice_in_dim(got2[:, 13:21, :], me * cw, cw, axis=2)
    cv_w = jnp.concatenate([conv_a_w[0], conv_b_w[0], jnp.zeros((1, cw), F32)], axis=0)
    cv_m = jnp.concatenate([m_conv_a_w[0], m_conv_b_w[0], jnp.zeros((1, cw), F32)], axis=0)
    cv_v = jnp.concatenate([v_conv_a_w[0], v_conv_b_w[0], jnp.ones((1, cw), F32)], axis=0)
    cvo = _adam("adam_conv", conv_parts, cv_w, cv_m, cv_v)

    dmod_cols = lax.dynamic_slice_in_dim(got2[:, :6, :].reshape(NDEV, 6 * D), me * ncol, ncol, axis=1)
    dmod16 = jnp.concatenate([dmod_cols, jnp.zeros((8, ncol), F32)], axis=0)
    ada = _ada_bwd(c16, dmod16, w_ada[0], m_w_ada[0], v_w_ada[0])

    loss = jnp.sum(got2[:, 20, 0])

    def pick(q):
        one = lambda i: rep[q][i:i + 1]
        return [ada[q].reshape(w_ada.shape), rep[q][0:6].reshape(b_ada.shape), one(6), big["w_in"][q],
                cvo[q][0:3].reshape(conv_a_w.shape), cvo[q][3:7].reshape(conv_b_w.shape), one(7),
                big["w_rg_a"][q], one(8), big["w_rg_x"][q], one(9), one(10), big["w_out"][q], one(11),
                big["w_gate_up"][q], big["w_down"][q], rep[q][12]]

    return (loss, r["grad_x"].reshape(x.shape), *pick(0), *pick(1), *pick(2), *pick(3))
```

```python
import math

import jax
import jax.numpy as jnp
from jax import lax
from jax.experimental import pallas as pl
from jax.experimental.pallas import tpu as pltpu

F32 = jnp.float32
BF16 = jnp.bfloat16

D = 1024
DFF = 2816
NDEV = 8
HEADS = 4
HB = D // HEADS
FB = DFF // 4
EPS = 1e-6
LRU_C = 8.0
ADAM_LR, ADAM_B1, ADAM_B2, ADAM_EPS, ADAM_WD, ADAM_STEP = 0.001, 0.9, 0.999, 1e-08, 0.01, 10

VMEM_LIMIT = 56 * 1024 * 1024
TM = 512
TMI = 1024
TMF = 256
TK = 2048
TKI = 2048
SUB = 256
TT = 256
CG = 256
MESH = pl.DeviceIdType.MESH


def _cp(*sem):
    return pltpu.CompilerParams(dimension_semantics=sem, vmem_limit_bytes=VMEM_LIMIT)


def _sig(x):
    return 1.0 / (1.0 + jnp.exp(-x))


def _log_sigmoid(x):
    z = jnp.exp(-jnp.abs(x))
    u = 1.0 + z
    d = u - 1.0
    l1p = jnp.where(d == 0.0, z, jnp.log(u) * (z / jnp.where(d == 0.0, 1.0, d)))
    return -(jnp.maximum(-x, 0.0) + l1p)


def _neg_expm1(x):
    p = x * (1.0 + x * 0.5 * (1.0 + x * (1.0 / 3.0) * (1.0 + x * 0.25 * (1.0 + x * 0.2 * (1.0 + x * (1.0 / 6.0))))))
    return jnp.where(x > -0.25, -p, 1.0 - jnp.exp(x))


_GC = math.sqrt(2.0 / math.pi)


def _gelu(x):
    t = jnp.tanh(_GC * (x + 0.044715 * x * x * x))
    return 0.5 * x * (1.0 + t), t


def _dot(a, b):
    return jnp.dot(a, b, preferred_element_type=F32)


def _dot_nt(a, b):
    return lax.dot_general(a, b, (((1,), (1,)), ((), ())), preferred_element_type=F32)


def _dot_tn(a, b):
    return lax.dot_general(a, b, (((0,), (0,)), ((), ())), preferred_element_type=F32)


def _resident(shape):
    return pl.BlockSpec(shape, lambda *_: (0,) * len(shape), pipeline_mode=pl.Buffered(1))


def _sub_blocks(n_rows):
    step = min(SUB, n_rows)
    return [slice(r, r + step) for r in range(0, n_rows, step)]


def _fold8(v):
    return v[0:8] + v[8:16]


def _pj(ref, s, rows=slice(None), cols=slice(0, D)):
    return ref[rows, s * D + cols.start:s * D + cols.stop]


def _in_proj(x, mod, g_mix, w_shard, order, shards, fulls, slicers):
    t_len = x.shape[0]
    tm = min(TMI, t_len)
    ni = t_len // tm
    na = len(shards)
    cw = 7 * D // NDEV
    rc = 32

    def body(ord_ref, x_ref, mod_ref, g_ref, wsh_ref, *rest):
        ins, (proj_ref, h_ref, wfull_ref), outs = rest[:na], rest[na:na + 3], rest[na + 3:2 * na + 3]
        h_scr, w_scr, wsend, wrecv, wlocal, wout = rest[2 * na + 3:2 * na + 9]
        start, forward, finish = _ag_phases(ins, outs, slicers, *rest[2 * na + 9:])
        p, i = pl.program_id(0), pl.program_id(1)
        x_, y_, c = _my_pos()
        me, sibling = (x_, y_, c), (x_, y_, 1 - c)
        chip_at = [None, (x_, 1 - y_), (1 - x_, y_), (1 - x_, 1 - y_)]

        def cols(px, py, pc):
            return w_scr.at[:, pl.ds(pl.multiple_of((4 * px + 2 * py + pc) * cw, 128), cw)]

        def wcopy(k, block, to, from_shard=False):
            dst = cols(*block)
            return pltpu.make_async_remote_copy(src_ref=wsh_ref if from_shard else dst, dst_ref=dst,
                                                send_sem=wsend.at[k], recv_sem=wrecv.at[k], device_id=to,
                                                device_id_type=MESH)

        own_local = pltpu.make_async_copy(wsh_ref, cols(*me), wlocal)
        to_hbm = pltpu.make_async_copy(w_scr, wfull_ref, wout)

        @pl.when((p == 0) & (i == 0))
        def _():
            own_local.start()
            wcopy(0, me, sibling, True).start()
            for q in (1, 2):
                wcopy(q, me, (*chip_at[q], c), True).start()
            own_local.wait()
            wcopy(0, sibling, me).wait_recv()

        @pl.when((p == 0) & (i == ni // 2))
        def _():
            wcopy(3, me, (*chip_at[3], c), True).start()

        for q in (1, 2, 3):
            @pl.when((p == q - 1) & (i == ni - 1))
            def _():
                wcopy(q, (*chip_at[q], c), me).wait_recv()
                wcopy(3 + q, (*chip_at[q], c), sibling).start()

            @pl.when((p == q) & (i == 0))
            def _():
                wcopy(3 + q, (*chip_at[q], 1 - c), me).wait_recv()

        @pl.when((p == 1) & (i == 0))
        def _():
            start()

        @pl.when((p == NCHIP - 1) & (i == ni // 2))
        def _():
            forward()

        @pl.when((p == NCHIP - 1) & (i == 0))
        def _():
            to_hbm.start()

        gs = g_ref[...] * (1.0 + mod_ref[1:2, :])
        sh = mod_ref[0:1, :]

        wcols = pl.ds(pl.multiple_of(ord_ref[p] * (2 * cw), 128), 2 * cw)
        for sub in _sub_blocks(tm):
            for r0 in range(sub.start, sub.stop, rc):
                xv = x_ref[r0:r0 + rc, :]
                r = lax.rsqrt(jnp.mean(xv * xv, axis=-1, keepdims=True) + EPS)
                h_scr[r0:r0 + rc, :] = (xv * r * gs + sh).astype(BF16)
            proj_ref[sub, :] = _dot(h_scr[sub, :], w_scr[:, wcols]).astype(BF16)

        @pl.when(p == 0)
        def _():
            h_ref[...] = h_scr[...]

        @pl.when((p == NCHIP - 1) & (i == ni - 1))
        def _():
            wcopy(0, me, sibling, True).wait_send()
            for q in (1, 2, 3):
                wcopy(q, me, (*chip_at[q], c), True).wait_send()
                wcopy(3 + q, (*chip_at[q], c), sibling).wait_send()
            finish()
            to_hbm.wait()

    res = pl.pallas_call(
        body, name="in_proj",
        grid_spec=pltpu.PrefetchScalarGridSpec(
            num_scalar_prefetch=1, grid=(NCHIP, ni),
            in_specs=[pl.BlockSpec((tm, D), lambda p, i, o: (i, 0)),
                      pl.BlockSpec((8, D), lambda p, i, o: (0, 0)),
                      pl.BlockSpec((1, D), lambda p, i, o: (0, 0))] + [_ANY] * (1 + na),
            out_specs=[pl.BlockSpec((tm, 2 * cw), lambda p, i, o: (i, o[p])),
                       pl.BlockSpec((tm, D), lambda p, i, o: (jnp.where(p == 0, i, ni - 1), 0))]
            + [_ANY] * (1 + na),
            scratch_shapes=[pltpu.VMEM((tm, D), BF16), pltpu.VMEM((D, 7 * D), BF16),
                            pltpu.SemaphoreType.DMA((7,)), pltpu.SemaphoreType.DMA((7,)),
                            pltpu.SemaphoreType.DMA, pltpu.SemaphoreType.DMA] + _ag_sems(na)),
        out_shape=[jax.ShapeDtypeStruct((t_len, 7 * D), BF16), jax.ShapeDtypeStruct((t_len, D), BF16),
                   jax.ShapeDtypeStruct((D, 7 * D), BF16)]
        + [jax.ShapeDtypeStruct(f, sh.dtype) for f, sh in zip(fulls, shards)],
        compiler_params=_cp("arbitrary", "arbitrary"),
    )(order, x, mod, g_mix, w_shard, *shards)
    return res[0], res[1], res[2], res[3:]


P_WA, P_WB, P_CBIAS, P_BA, P_BX, P_LAM = 0, 3, 7, 8, 9, 10
SV_PLANES = SV_U, SV_YA, SV_R, SV_I, SV_A, SV_MULT = range(6)


def _lru_gates(rp, ip, ls, first_row):
    r = _sig(rp)
    ig = _sig(ip)
    la = LRU_C * r * ls
    a = jnp.exp(la)
    m2 = _neg_expm1(2.0 * la)
    mult = jnp.where(first_row, 1.0, jnp.sqrt(jnp.maximum(m2, 0.0)))
    return r, ig, la, a, m2, mult


def _shift_down(cur, prev, s, row):
    return jnp.where(row >= s, pltpu.roll(cur, s, 0), pltpu.roll(prev, s, 0))


def _shift_up(cur, nxt, s, row):
    return jnp.where(row < 8 - s, pltpu.roll(cur, 8 - s, 0), pltpu.roll(nxt, 8 - s, 0))


def _conv_fwd_rows(tt, proj_ref, prm_ref, xe, ve, u_s, ub_s, ya_s):
    row = lax.broadcasted_iota(jnp.int32, (8, CG), 0)
    w_b = [prm_ref[P_WB + k:P_WB + k + 1, :] for k in range(4)]
    w_a = [prm_ref[P_WA + k:P_WA + k + 1, :] for k in range(3)]
    bias = prm_ref[P_CBIAS:P_CBIAS + 1, :]

    def blk(ib, carry):
        r0 = pl.multiple_of(ib * 16, 16)
        rows = pl.ds(r0, 16)
        for g in range(D // CG):
            cs = slice(g * CG, (g + 1) * CG)
            x16 = _pj(proj_ref, 3, rows, cs).astype(F32)
            v16 = _pj(proj_ref, 1, rows, cs).astype(F32) * _pj(proj_ref, 2, rows, cs).astype(F32)
            xp = xe[pl.ds(r0, 8), cs]
            vp = ve[pl.ds(r0, 8), cs]
            xe[pl.ds(r0 + 8, 16), cs] = x16
            ve[pl.ds(r0 + 8, 16), cs] = v16
            us, yas = [], []
            for sb in range(2):
                xc, vc = x16[8 * sb:8 * sb + 8], v16[8 * sb:8 * sb + 8]
                u8 = bias[:, cs] + w_b[3][:, cs] * xc
                for s in (1, 2, 3):
                    u8 = u8 + w_b[3 - s][:, cs] * _shift_down(xc, xp, s, row)
                y8 = w_a[2][:, cs] * vc
                for s in (1, 2):
                    y8 = y8 + w_a[2 - s][:, cs] * _shift_down(vc, vp, s, row)
                us.append(u8)
                yas.append(y8)
                xp, vp = xc, vc
            u16 = jnp.concatenate(us, axis=0)
            u_s[rows, cs] = u16
            ub_s[rows, cs] = u16.astype(BF16)
            ya_s[rows, cs] = jnp.concatenate(yas, axis=0)
        return carry

    lax.fori_loop(0, tt // 16, blk, 0)


def _mixer_fwd(proj, prm, wa, wx, shards, fulls, slicers):
    t_len = proj.shape[0]
    tt = min(TT, t_len)
    nt = t_len // tt
    na = len(shards)

    def body(proj_ref, prm_ref, wa_ref, wx_ref, *rest):
        ins, (mg_ref, hl_ref, sv_ref), outs = rest[:na], rest[na:na + 3], rest[na + 3:2 * na + 3]
        xe, ve, hc, rp_s, ip_s, ub_s = rest[2 * na + 3:2 * na + 9]
        start, forward, finish = _ag_phases(ins, outs, slicers, *rest[2 * na + 9:])
        t = pl.program_id(0)

        @pl.when(t == 0)
        def _():
            start()
            xe[0:8, :] = jnp.zeros((8, D), F32)
            ve[0:8, :] = jnp.zeros((8, D), F32)
            hc[...] = jnp.zeros((8, D), F32)

        @pl.when(t == (3 * nt) // 4)
        def _():
            forward()

        _conv_fwd_rows(tt, proj_ref, prm_ref, xe, ve, sv_ref.at[SV_U], ub_s, sv_ref.at[SV_YA])
        xe[0:8, :] = xe[tt:tt + 8, :]
        ve[0:8, :] = ve[tt:tt + 8, :]

        ub = ub_s[...]
        for h in range(HEADS):
            cs = slice(h * HB, (h + 1) * HB)
            rp_s[:, cs] = _dot(ub[:, cs], wa_ref[h]) + prm_ref[P_BA:P_BA + 1, cs]
            ip_s[:, cs] = _dot(ub[:, cs], wx_ref[h]) + prm_ref[P_BX:P_BX + 1, cs]

        ls_all = _log_sigmoid(prm_ref[P_LAM:P_LAM + 1, :])
        row = lax.broadcasted_iota(jnp.int32, (8, CG), 0)

        def blk(i, carry):
            r0 = pl.multiple_of(i * 16, 16)
            for g in range(D // CG):
                cs = slice(g * CG, (g + 1) * CG)
                ls = ls_all[:, cs]
                hprev = hc[:, cs]
                hs = []
                for sb in range(2):
                    rr = r0 + 8 * sb
                    first = (row + (t * tt + rr)) == 0
                    r8 = pl.ds(rr, 8)
                    r, ig, _, a, _, mult = _lru_gates(rp_s[r8, cs], ip_s[r8, cs], ls, first)
                    for plane, val in ((SV_R, r), (SV_I, ig), (SV_A, a), (SV_MULT, mult)):
                        sv_ref[plane, r8, cs] = val
                    b = mult * (ig * sv_ref[SV_U, r8, cs])
                    for s in (1, 2, 4):
                        a_sh = jnp.where(row >= s, pltpu.roll(a, s, 0), 1.0)
                        b_sh = jnp.where(row >= s, pltpu.roll(b, s, 0), 0.0)
                        b = a * b_sh + b
                        a = a * a_sh
                    hv = a * hprev + b
                    hprev = jnp.broadcast_to(hv[7:8, :], hv.shape)
                    hs.append(hv)
                hc[:, cs] = hprev
                h16 = jnp.concatenate(hs, axis=0)
                rows = pl.ds(r0, 16)
                gl, _ = _gelu(_pj(proj_ref, 4, rows, cs).astype(F32))
                y_b = h16 * gl
                y_a = _pj(proj_ref, 0, rows, cs).astype(F32) * sv_ref[SV_YA, rows, cs]
                mg = (_sig(_pj(proj_ref, 5, rows, cs).astype(F32)) * y_a
                      + _sig(_pj(proj_ref, 6, rows, cs).astype(F32)) * y_b)
                mg_ref[rows, cs] = mg.astype(BF16)
                hl_ref[rows, cs] = h16.astype(BF16)
            return carry

        lax.fori_loop(0, tt // 16, blk, 0)

        @pl.when(t == nt - 1)
        def _():
            finish()

    res = pl.pallas_call(
        body, name="mixer_fwd", grid=(nt,),
        in_specs=[pl.BlockSpec((tt, 7 * D), lambda t: (t, 0)),
                  pl.BlockSpec((16, D), lambda t: (0, 0)),
                  pl.BlockSpec((HEADS, HB, HB), lambda t: (0, 0, 0)),
                  pl.BlockSpec((HEADS, HB, HB), lambda t: (0, 0, 0))] + [_ANY] * na,
        out_specs=[pl.BlockSpec((tt, D), lambda t: (t, 0)), pl.BlockSpec((tt, D), lambda t: (t, 0)),
                   pl.BlockSpec((len(SV_PLANES), tt, D), lambda t: (0, t, 0))] + [_ANY] * na,
        out_shape=[jax.ShapeDtypeStruct((t_len, D), BF16), jax.ShapeDtypeStruct((t_len, D), BF16),
                   jax.ShapeDtypeStruct((len(SV_PLANES), t_len, D), F32)]
        + [jax.ShapeDtypeStruct(f, sh.dtype) for f, sh in zip(fulls, shards)],
        scratch_shapes=[pltpu.VMEM((tt + 8, D), F32), pltpu.VMEM((tt + 8, D), F32), pltpu.VMEM((8, D), F32),
                        pltpu.VMEM((tt, D), F32), pltpu.VMEM((tt, D), F32), pltpu.VMEM((tt, D), BF16)]
        + _ag_sems(na),
        compiler_params=_cp("arbitrary"),
    )(proj, prm, wa, wx, *shards)
    return res[0], res[1], res[2], res[3:]


def _out_proj(merged, x, mod, g_ffn, w_out):
    t_len = x.shape[0]
    tm = min(TM, t_len)

    def body(mg_ref, x_ref, mod_ref, g_ref, w_ref, x1_ref, h2_ref):
        gt1 = mod_ref[2:3, :]
        gs = g_ref[...] * (1.0 + mod_ref[4:5, :])
        sh = mod_ref[3:4, :]
        for sub in _sub_blocks(tm):
            x1_ref[sub, :] = x_ref[sub, :] + gt1 * _dot(mg_ref[sub, :], w_ref[...])
            for r0 in range(sub.start, sub.stop, 16):
                x1 = x1_ref[r0:r0 + 16, :]
                r = lax.rsqrt(jnp.mean(x1 * x1, axis=-1, keepdims=True) + EPS)
                h2_ref[r0:r0 + 16, :] = (x1 * r * gs + sh).astype(BF16)

    return pl.pallas_call(
        body, name="out_proj", grid=(t_len // tm,),
        in_specs=[pl.BlockSpec((tm, D), lambda i: (i, 0)), pl.BlockSpec((tm, D), lambda i: (i, 0)),
                  pl.BlockSpec((8, D), lambda i: (0, 0)), pl.BlockSpec((1, D), lambda i: (0, 0)),
                  pl.BlockSpec((D, D), lambda i: (0, 0))],
        out_specs=[pl.BlockSpec((tm, D), lambda i: (i, 0)), pl.BlockSpec((tm, D), lambda i: (i, 0))],
        out_shape=[jax.ShapeDtypeStruct((t_len, D), F32), jax.ShapeDtypeStruct((t_len, D), BF16)],
        compiler_params=_cp("parallel"),
    )(merged, x, mod, g_ffn, w_out)


def _ffn_fwd(h2, x1, target, mod, g_fin, w_gu, w_down):
    t_len = x1.shape[0]
    tm = min(TMF, t_len)

    def body(h2_ref, x1_ref, tg_ref, mod_ref, g_ref, wgu_ref, wd_ref, gu_ref, dx2_ref, dx2b_ref, loss_ref, dg_ref, acc):
        @pl.when(pl.program_id(0) == 0)
        def _():
            loss_ref[...] = jnp.zeros_like(loss_ref)
            dg_ref[...] = jnp.zeros_like(dg_ref)

        hb = h2_ref[...]
        ffn = None
        nxt = (_dot_nt(hb, wgu_ref[0, 0]), _dot_nt(hb, wgu_ref[1, 0]))
        for j in range(4):
            gate, up = nxt
            if j < 3:
                nxt = (_dot_nt(hb, wgu_ref[0, j + 1]), _dot_nt(hb, wgu_ref[1, j + 1]))
            gu_ref[0, j] = gate.astype(BF16)
            gu_ref[1, j] = up.astype(BF16)
            act = (gate * _sig(gate) * up).astype(BF16)
            part = _dot(act, wd_ref[j * FB:(j + 1) * FB, :])
            ffn = part if ffn is None else ffn + part
        acc[...] = ffn

        gt2 = mod_ref[5:6, :]
        gf = g_ref[...]

        s_loss = s_dg = jnp.zeros((8, D), F32)
        for r0 in range(0, tm, 16):
            rows = slice(r0, r0 + 16)
            x2 = x1_ref[rows, :] + gt2 * acc[rows, :]
            r = lax.rsqrt(jnp.mean(x2 * x2, axis=-1, keepdims=True) + EPS)
            xn = x2 * r
            diff = xn * gf - tg_ref[rows, :]
            dy = diff * (1.0 / D)
            dxn = dy * gf
            dx2 = r * (dxn - xn * jnp.mean(dxn * xn, axis=-1, keepdims=True))
            dx2_ref[rows, :] = dx2
            dx2b_ref[rows, :] = dx2.astype(BF16)
            s_loss, s_dg = s_loss + _fold8(diff * diff), s_dg + _fold8(dy * xn)
        loss_ref[...] += jnp.sum(s_loss) * (0.5 / D)
        dg_ref[...] += jnp.sum(s_dg, axis=0, keepdims=True)

    row = pl.BlockSpec((tm, D), lambda i: (i, 0))
    return pl.pallas_call(
        body, name="ffn_fwd", grid=(t_len // tm,),
        in_specs=[row, row, row, pl.BlockSpec((8, D), lambda i: (0, 0)), pl.BlockSpec((1, D), lambda i: (0, 0)),
                  _resident((2, 4, FB, D)), _resident((DFF, D))],
        out_specs=[pl.BlockSpec((2, 4, tm, FB), lambda i: (0, 0, i, 0)), row, row,
                   pl.BlockSpec((1, 128), lambda i: (0, 0)), pl.BlockSpec((1, D), lambda i: (0, 0))],
        out_shape=[jax.ShapeDtypeStruct((2, 4, t_len, FB), BF16), jax.ShapeDtypeStruct((t_len, D), F32),
                   jax.ShapeDtypeStruct((t_len, D), BF16),
                   jax.ShapeDtypeStruct((1, 128), F32), jax.ShapeDtypeStruct((1, D), F32)],
        scratch_shapes=[pltpu.VMEM((tm, D), F32)],
        compiler_params=_cp("arbitrary"),
    )(h2, x1, target, mod, g_fin, w_gu, w_down)


S_SH, S_SC, S_G = 0, 1, 2


def _norm_bwd_rows(span, sums, dh_ref, x_ref, dres_ref, scale, gain, write):
    gs = 1.0 + scale
    s_sh, s_sc, s_g = sums
    for r0 in range(span.start, span.stop, 16):
        rows = slice(r0, r0 + 16)
        dh = dh_ref[rows, :]
        xv = x_ref[rows, :]
        r = lax.rsqrt(jnp.mean(xv * xv, axis=-1, keepdims=True) + EPS)
        xn = xv * r
        dhn = dh * gs
        dxn = dhn * gain
        write(rows, dres_ref[rows, :] + r * (dxn - xn * jnp.mean(dxn * xn, axis=-1, keepdims=True)))
        s_sh, s_sc, s_g = s_sh + _fold8(dh), s_sc + _fold8(dh * (xn * gain)), s_g + _fold8(dhn * xn)
    return s_sh, s_sc, s_g


def _add_norm_sums(sums_ref, sums):
    for dst, s in zip((S_SH, S_SC, S_G), sums):
        sums_ref[dst:dst + 1, :] += jnp.sum(s, axis=0, keepdims=True)


def _ffn_bwd(dx2, gu, x1, mod, g_ffn, w_gu, w_down, w_out):
    t_len = x1.shape[0]
    tm = min(TMF, t_len)

    def body(dx2_ref, gu_ref, x1_ref, mod_ref, g_ref, wgu_ref, wd_ref, wo_ref,
             dgu_ref, act_ref, dx1_ref, dx1b_ref, dmg_ref, sums_ref, acc, dmo, dact_s):
        @pl.when(pl.program_id(0) == 0)
        def _():
            sums_ref[...] = jnp.zeros_like(sums_ref)

        dffn = (dx2_ref[...] * mod_ref[5:6, :]).astype(BF16)
        dact_s[0] = _dot_nt(dffn, wd_ref[0:FB, :])
        for j in range(4):
            if j < 3:
                dact_s[(j + 1) % 2] = _dot_nt(dffn, wd_ref[(j + 1) * FB:(j + 2) * FB, :])
            for r0 in range(0, tm, 16):
                rows = slice(r0, r0 + 16)
                dact = dact_s[j % 2, rows, :]
                gate = gu_ref[0, j, rows, :].astype(F32)
                up = gu_ref[1, j, rows, :].astype(F32)
                sg = _sig(gate)
                silu = gate * sg
                act_ref[j, rows, :] = (silu * up).astype(BF16)
                dgu_ref[0, j, rows, :] = (dact * up * (sg * (1.0 + gate * (1.0 - sg)))).astype(BF16)
                dgu_ref[1, j, rows, :] = (dact * silu).astype(BF16)
            part = _dot(dgu_ref[0, j], wgu_ref[0, j]) + _dot(dgu_ref[1, j], wgu_ref[1, j])
            if j == 0:
                acc[...] = part
            else:
                acc[...] += part

        gt1 = mod_ref[2:3, :]

        def write(rows, dx1):
            dx1_ref[rows, :] = dx1
            dx1b_ref[rows, :] = dx1.astype(BF16)
            dmo[rows, :] = (dx1 * gt1).astype(BF16)

        zero = jnp.zeros((8, D), F32)
        sums = (zero, zero, zero)
        for sub in (slice(0, tm // 2), slice(tm // 2, tm)):
            sums = _norm_bwd_rows(sub, sums, acc, x1_ref, dx2_ref, mod_ref[4:5, :], g_ref[...], write)
            dmg_ref[sub, :] = _dot_nt(dmo[sub, :], wo_ref[...]).astype(BF16)
        _add_norm_sums(sums_ref, sums)

    row = pl.BlockSpec((tm, D), lambda i: (i, 0))
    return pl.pallas_call(
        body, name="ffn_bwd", grid=(t_len // tm,),
        in_specs=[row, pl.BlockSpec((2, 4, tm, FB), lambda i: (0, 0, i, 0)), row,
                  pl.BlockSpec((8, D), lambda i: (0, 0)), pl.BlockSpec((1, D), lambda i: (0, 0)),
                  _resident((2, 4, FB, D)), _resident((DFF, D)), _resident((D, D))],
        out_specs=[pl.BlockSpec((2, 4, tm, FB), lambda i: (0, 0, i, 0)),
                   pl.BlockSpec((4, tm, FB), lambda i: (0, i, 0)), row, row, row,
                   pl.BlockSpec((8, D), lambda i: (0, 0))],
        out_shape=[jax.ShapeDtypeStruct((2, 4, t_len, FB), BF16), jax.ShapeDtypeStruct((4, t_len, FB), BF16),
                   jax.ShapeDtypeStruct((t_len, D), F32), jax.ShapeDtypeStruct((t_len, D), BF16),
                   jax.ShapeDtypeStruct((t_len, D), BF16), jax.ShapeDtypeStruct((8, D), F32)],
        scratch_shapes=[pltpu.VMEM((tm, D), F32), pltpu.VMEM((tm, D), BF16), pltpu.VMEM((2, tm, FB), F32)],
        compiler_params=_cp("arbitrary"),
    )(dx2, gu, x1, mod, g_ffn, w_gu, w_down, w_out)


def _my_pos():
    return lax.axis_index("x"), lax.axis_index("y"), lax.axis_index("c")


def _my_index():
    x, y, c = _my_pos()
    return 4 * x + 2 * y + c


def _device_of(b):
    return (b >> 2) & 1, (b >> 1) & 1, b & 1


def _rs_send(src, parts_ref, b, send_sems, recv_sems, local_sem):
    me = _my_index()
    dst = parts_ref.at[me]

    @pl.when(b == me)
    def _():
        pltpu.make_async_copy(src, dst, local_sem).start()

    @pl.when(b != me)
    def _():
        pltpu.make_async_remote_copy(src_ref=src, dst_ref=dst, send_sem=send_sems.at[b], recv_sem=recv_sems.at[me],
                                     device_id=_device_of(b), device_id_type=MESH).start()


def _rs_finish(src_of, parts_ref, send_sems, recv_sems, local_sem):
    me = _my_index()
    for s in range(NDEV):
        @pl.when(s != me)
        def _():
            cp = pltpu.make_async_remote_copy(src_ref=src_of(s), dst_ref=parts_ref.at[s], send_sem=send_sems.at[s],
                                              recv_sem=recv_sems.at[s], device_id=_device_of(s), device_id_type=MESH)
            cp.wait_send()
            cp.wait_recv()

        @pl.when(s == me)
        def _():
            pltpu.make_async_copy(src_of(s), parts_ref.at[s], local_sem).wait()


_RS_SEMS = [pltpu.SemaphoreType.DMA((NDEV,)), pltpu.SemaphoreType.DMA((NDEV,)), pltpu.SemaphoreType.DMA]
_ANY = pl.BlockSpec(memory_space=pl.ANY)


def _xor_order(me, n):
    return (me ^ (n - 1 - jnp.arange(n, dtype=jnp.int32))).astype(jnp.int32)


NCHIP = NDEV // 2


def _rs2_scratch(half_shape):
    blocks = lambda *lead: pltpu.VMEM(lead + tuple(half_shape), BF16)
    return [blocks(NCHIP, 2), blocks(NCHIP)] + [pltpu.SemaphoreType.DMA((NCHIP,))] * 4 + [pltpu.SemaphoreType.DMA]


def _rs2_to_sibling(q, rs):
    stage, from_sib, d_send, d_recv = rs[:4]
    x, y, c = _my_pos()
    pltpu.make_async_remote_copy(src_ref=stage.at[q, 1 - c], dst_ref=from_sib.at[q], send_sem=d_send.at[q],
                                 recv_sem=d_recv.at[q], device_id=(x, y, 1 - c), device_id_type=MESH).start()


def _rs2_forward(q, parts_ref, rs):
    stage, chip_sum, d_send, d_recv, i_send, i_recv, local_sem = rs
    x, y, c = _my_pos()
    my_chip = 2 * x + y
    pltpu.make_async_remote_copy(src_ref=stage.at[q, c], dst_ref=chip_sum.at[q], send_sem=d_send.at[q],
                                 recv_sem=d_recv.at[q], device_id=(x, y, 1 - c), device_id_type=MESH).wait_recv()
    chip_sum[q] = (stage[q, c].astype(F32) + chip_sum[q].astype(F32)).astype(BF16)

    @pl.when(q == my_chip)
    def _():
        pltpu.make_async_copy(chip_sum.at[q], parts_ref.at[my_chip], local_sem).start()

    @pl.when(q != my_chip)
    def _():
        pltpu.make_async_remote_copy(src_ref=chip_sum.at[q], dst_ref=parts_ref.at[my_chip], send_sem=i_send.at[q],
                                     recv_sem=i_recv.at[my_chip], device_id=((q >> 1) & 1, q & 1, c),
                                     device_id_type=MESH).start()


def _rs2_finish(parts_ref, rs):
    stage, chip_sum, d_send, d_recv, i_send, i_recv, local_sem = rs
    x, y, c = _my_pos()
    my_chip = 2 * x + y
    for q in range(NCHIP):
        pltpu.make_async_remote_copy(src_ref=stage.at[q, 1 - c], dst_ref=chip_sum.at[q], send_sem=d_send.at[q],
                                     recv_sem=d_recv.at[q], device_id=(x, y, 1 - c), device_id_type=MESH).wait_send()

        @pl.when(q != my_chip)
        def _():
            cp = pltpu.make_async_remote_copy(src_ref=chip_sum.at[q], dst_ref=parts_ref.at[q], send_sem=i_send.at[q],
                                              recv_sem=i_recv.at[q], device_id=((q >> 1) & 1, q & 1, c),
                                              device_id_type=MESH)
            cp.wait_send()
            cp.wait_recv()

        @pl.when(q == my_chip)
        def _():
            pltpu.make_async_copy(chip_sum.at[q], parts_ref.at[q], local_sem).wait()


def _gu_wgrad(h2, dgu, order):
    t_len = h2.shape[0]
    tk = min(TK, t_len)
    nk = t_len // tk

    def body(ord_ref, h_ref, d_ref, parts_ref, acc, *rs):
        p, k = pl.program_id(0), pl.program_id(1)

        @pl.when(k == 0)
        def _():
            acc[...] = jnp.zeros_like(acc)

        hb = h_ref[...]
        for half in range(2):
            acc[half] += _dot_tn(d_ref[0, half], hb)

        @pl.when(k == nk - 1)
        def _():
            q = ord_ref[p]
            rs[0][q] = acc[...].astype(BF16)
            _rs2_to_sibling(q, rs)

        @pl.when((k == min(1, nk - 1)) & (p > 0))
        def _():
            _rs2_forward(ord_ref[p - 1], parts_ref, rs)

        @pl.when((p == NCHIP - 1) & (k == nk - 1))
        def _():
            _rs2_forward(ord_ref[p], parts_ref, rs)
            _rs2_finish(parts_ref, rs)

    return pl.pallas_call(
        body, name="gu_wgrad",
        grid_spec=pltpu.PrefetchScalarGridSpec(
            num_scalar_prefetch=1, grid=(NCHIP, nk),
            in_specs=[pl.BlockSpec((tk, D), lambda p, k, o: (k, 0)),
                      pl.BlockSpec((1, 2, tk, FB), lambda p, k, o: (o[p], 0, k, 0))],
            out_specs=_ANY,
            scratch_shapes=[pltpu.VMEM((2, FB, D), F32)] + _rs2_scratch((FB, D))),
        out_shape=jax.ShapeDtypeStruct((NCHIP, FB, D), BF16),
        compiler_params=_cp("arbitrary", "arbitrary"),
    )(order, h2, dgu.reshape(NCHIP, 2, t_len, FB))


def _scaled_wgrad(name, a, dx, w, gate_row, mod, order):
    t_len = dx.shape[0]
    kb = w.shape[0] // NCHIP
    tk = min(TK, t_len)
    nk = t_len // tk
    rows = kb // 2
    if a.ndim == 3:
        a_spec = pl.BlockSpec((None, tk, kb), lambda p, k, o: (o[p], k, 0))
    else:
        a_spec = pl.BlockSpec((tk, kb), lambda p, k, o: (k, o[p]))

    def body(ord_ref, a_ref, dx_ref, w_ref, mod_ref, parts_ref, dg_ref, acc, *rs):
        p, k = pl.program_id(0), pl.program_id(1)

        @pl.when((p == 0) & (k == 0))
        def _():
            dg_ref[...] = jnp.zeros_like(dg_ref)

        @pl.when(k == 0)
        def _():
            acc[...] = jnp.zeros_like(acc)

        acc[...] += _dot_tn(a_ref[...], dx_ref[...])

        @pl.when(k == nk - 1)
        def _():
            q = ord_ref[p]
            z = acc[...]
            zg = (z * mod_ref[gate_row:gate_row + 1, :]).astype(BF16)
            dg_ref[0:1, :] += jnp.sum(z * w_ref[...].astype(F32), axis=0, keepdims=True)
            for half in range(2):
                rs[0][q, half] = zg[half * rows:(half + 1) * rows]
            _rs2_to_sibling(q, rs)

        @pl.when((k == min(1, nk - 1)) & (p > 0))
        def _():
            _rs2_forward(ord_ref[p - 1], parts_ref, rs)

        @pl.when((p == NCHIP - 1) & (k == nk - 1))
        def _():
            _rs2_forward(ord_ref[p], parts_ref, rs)
            _rs2_finish(parts_ref, rs)

    return pl.pallas_call(
        body, name=name,
        grid_spec=pltpu.PrefetchScalarGridSpec(
            num_scalar_prefetch=1, grid=(NCHIP, nk),
            in_specs=[a_spec,
                      pl.BlockSpec((tk, D), lambda p, k, o: (k, 0)),
                      pl.BlockSpec((kb, D), lambda p, k, o: (o[p], 0)),
                      pl.BlockSpec((8, D), lambda p, k, o: (0, 0))],
            out_specs=[_ANY, pl.BlockSpec((8, D), lambda p, k, o: (0, 0))],
            scratch_shapes=[pltpu.VMEM((kb, D), F32)] + _rs2_scratch((rows, D))),
        out_shape=[jax.ShapeDtypeStruct((NCHIP, rows, D), BF16), jax.ShapeDtypeStruct((8, D), F32)],
        compiler_params=_cp("arbitrary", "arbitrary"),
    )(order, a, dx, w, mod)


M_WA, M_WB, M_CBIAS, M_BA, M_BX, M_LS = 0, 3, 7, 8, 9, 10


def _conv_bwd_rows(tt, proj_ref, prm_ref, due, dye, dp_ref, acc8):
    row = lax.broadcasted_iota(jnp.int32, (8, CG), 0)
    w_b = [prm_ref[P_WB + k:P_WB + k + 1, :] for k in range(4)]
    w_a = [prm_ref[P_WA + k:P_WA + k + 1, :] for k in range(3)]

    def blk(ib, carry):
        r0 = pl.multiple_of(ib * 16, 16)
        rows = pl.ds(r0, 16)
        for g in range(D // CG):
            cs = slice(g * CG, (g + 1) * CG)
            du16, du_after = due[rows, cs], due[pl.ds(r0 + 16, 8), cs]
            dy16, dy_after = dye[rows, cs], dye[pl.ds(r0 + 16, 8), cs]
            cc16 = _pj(proj_ref, 1, rows, cs).astype(F32)
            cx16 = _pj(proj_ref, 2, rows, cs).astype(F32)
            x16 = _pj(proj_ref, 3, rows, cs).astype(F32)
            v16 = cc16 * cx16
            acc = [acc8[8 * k:8 * k + 8, cs] for k in range(8)]
            drx, dv = [], []
            for sb in range(2):
                lo = slice(8 * sb, 8 * sb + 8)
                duc, dyc, xc, vc = du16[lo], dy16[lo], x16[lo], v16[lo]
                du_n = du16[8:16] if sb == 0 else du_after
                dy_n = dy16[8:16] if sb == 0 else dy_after
                acc[0] = acc[0] + duc
                acc[4] = acc[4] + duc * xc
                d8 = w_b[3][:, cs] * duc
                for s in (1, 2, 3):
                    du_s = _shift_up(duc, du_n, s, row)
                    acc[4 - s] = acc[4 - s] + du_s * xc
                    d8 = d8 + w_b[3 - s][:, cs] * du_s
                acc[7] = acc[7] + dyc * vc
                e8 = w_a[2][:, cs] * dyc
                for s in (1, 2):
                    dy_s = _shift_up(dyc, dy_n, s, row)
                    acc[7 - s] = acc[7 - s] + dy_s * vc
                    e8 = e8 + w_a[2 - s][:, cs] * dy_s
                drx.append(d8)
                dv.append(e8)
            for k in range(8):
                acc8[8 * k:8 * k + 8, cs] = acc[k]
            dv16 = jnp.concatenate(dv, axis=0)
            col = lambda s: slice(s * D + g * CG, s * D + (g + 1) * CG)
            dp_ref[rows, col(3)] = jnp.concatenate(drx, axis=0).astype(BF16)
            dp_ref[rows, col(1)] = (dv16 * cx16).astype(BF16)
            dp_ref[rows, col(2)] = (dv16 * cc16).astype(BF16)
        return carry

    lax.fori_loop(0, tt // 16, blk, 0)


def _mixer_bwd(proj, hl, sv, dmg, prm, wa, wx):
    t_len = proj.shape[0]
    tt = min(TT, t_len)
    nt = t_len // tt
    hb8 = tt // 8

    def rev(i):
        return nt - 1 - i

    def halo(i):
        return jnp.maximum(rev(i) * hb8 - 1, 0)

    def body(proj_ref, hl_ref, hh_ref, sv_ref, dmg_ref, prm_ref, wa_ref, wx_ref,
             dp_ref, sums_ref, gwa_ref, gwx_ref,
             he, due, dye, drp_s, dip_s, an, gn, acc8):
        i = pl.program_id(0)
        t = rev(i)

        @pl.when(i == 0)
        def _():
            sums_ref[...] = jnp.zeros_like(sums_ref)
            gwa_ref[...] = jnp.zeros_like(gwa_ref)
            gwx_ref[...] = jnp.zeros_like(gwx_ref)
            due[tt:tt + 8, :] = jnp.zeros((8, D), F32)
            dye[tt:tt + 8, :] = jnp.zeros((8, D), F32)
            an[...] = jnp.zeros((8, D), F32)
            gn[...] = jnp.zeros((8, D), F32)

        live = (t > 0).astype(F32)
        he[0:8, :] = hh_ref[...].astype(F32) * live
        he[8:8 + tt, :] = hl_ref[...].astype(F32)

        ls_all = _log_sigmoid(prm_ref[P_LAM:P_LAM + 1, :])
        row = lax.broadcasted_iota(jnp.int32, (8, CG), 0)
        nblk = tt // 16

        def blk(ib, carry):
            r0 = pl.multiple_of((nblk - 1 - ib) * 16, 16)
            rows = pl.ds(r0, 16)
            for g in range(D // CG):
                cs = slice(g * CG, (g + 1) * CG)
                ls = ls_all[:, cs]
                dm = dmg_ref[rows, cs].astype(F32)
                cb = _pj(proj_ref, 0, rows, cs).astype(F32)
                rg = _pj(proj_ref, 4, rows, cs).astype(F32)
                sga = _sig(_pj(proj_ref, 5, rows, cs).astype(F32))
                sgb = _sig(_pj(proj_ref, 6, rows, cs).astype(F32))
                ya0 = sv_ref[SV_YA, rows, cs]
                h16 = he[pl.ds(r0 + 8, 16), cs]
                gl, th = _gelu(rg)
                dgl = 0.5 * (1.0 + th) + 0.5 * rg * (1.0 - th * th) * (_GC * (1.0 + 3.0 * 0.044715 * rg * rg))
                y_a = cb * ya0
                y_b = h16 * gl
                dy_a = dm * sga
                dy_b = dm * sgb
                col = lambda s: slice(s * D + g * CG, s * D + (g + 1) * CG)
                dp_ref[rows, col(5)] = (dm * y_a * sga * (1.0 - sga)).astype(BF16)
                dp_ref[rows, col(6)] = (dm * y_b * sgb * (1.0 - sgb)).astype(BF16)
                dp_ref[rows, col(4)] = (dy_b * h16 * dgl).astype(BF16)
                dp_ref[rows, col(0)] = (dy_a * ya0).astype(BF16)
                dye[rows, cs] = dy_a * cb
                dh16 = dy_b * gl

                a_next = an[:, cs]
                g_next = gn[:, cs]
                s_ba = jnp.zeros((8, CG), F32)
                s_bx = jnp.zeros((8, CG), F32)
                s_ls = jnp.zeros((8, CG), F32)
                for sb in (1, 0):
                    rr = r0 + 8 * sb
                    first = (row + (t * tt + rr)) == 0
                    r8 = pl.ds(rr, 8)
                    uu, r, ig, a, mult = (sv_ref[pln, r8, cs] for pln in (SV_U, SV_R, SV_I, SV_A, SV_MULT))
                    ca = jnp.where(row < 7, pltpu.roll(a, 7, 0), a_next)
                    cb_ = dh16[8 * sb:8 * sb + 8, :]
                    for s in (1, 2, 4):
                        a_sh = jnp.where(row < 8 - s, pltpu.roll(ca, 8 - s, 0), 1.0)
                        b_sh = jnp.where(row < 8 - s, pltpu.roll(cb_, 8 - s, 0), 0.0)
                        cb_ = ca * b_sh + cb_
                        ca = ca * a_sh
                    gv = ca * g_next + cb_
                    g_next = jnp.broadcast_to(gv[0:1, :], gv.shape)
                    a_next = jnp.broadcast_to(a[0:1, :], a.shape)
                    hprev = jnp.where(row >= 1, pltpu.roll(he[pl.ds(rr + 8, 8), cs], 1, 0),
                                      pltpu.roll(he[pl.ds(rr, 8), cs], 1, 0))
                    da = gv * hprev
                    dmult = jnp.where(first, 0.0, gv * ig * uu)
                    dla = da * a + jnp.where(mult > 0.0, dmult * (-(a * a) / mult), 0.0)
                    drp = dla * (LRU_C * ls) * r * (1.0 - r)
                    dip = gv * mult * uu * ig * (1.0 - ig)
                    s_ls = s_ls + dla * (LRU_C * r)
                    s_ba = s_ba + drp
                    s_bx = s_bx + dip
                    drp_s[pl.ds(rr, 8), cs] = drp
                    dip_s[pl.ds(rr, 8), cs] = dip
                    due[pl.ds(rr, 8), cs] = gv * mult * ig
                an[:, cs] = a_next
                gn[:, cs] = g_next
                sums_ref[M_BA:M_BA + 1, cs] += jnp.sum(s_ba, axis=0, keepdims=True)
                sums_ref[M_BX:M_BX + 1, cs] += jnp.sum(s_bx, axis=0, keepdims=True)
                sums_ref[M_LS:M_LS + 1, cs] += jnp.sum(s_ls, axis=0, keepdims=True)
            return carry

        lax.fori_loop(0, nblk, blk, 0)

        drp_b = drp_s[...].astype(BF16)
        dip_b = dip_s[...].astype(BF16)
        ub = sv_ref[SV_U].astype(BF16)
        for h in range(HEADS):
            cs = slice(h * HB, (h + 1) * HB)
            due[0:tt, cs] += _dot_nt(drp_b[:, cs], wa_ref[h]) + _dot_nt(dip_b[:, cs], wx_ref[h])
            gwa_ref[h] += _dot_tn(ub[:, cs], drp_b[:, cs])
            gwx_ref[h] += _dot_tn(ub[:, cs], dip_b[:, cs])

        acc8[...] = jnp.zeros_like(acc8)
        _conv_bwd_rows(tt, proj_ref, prm_ref, due, dye, dp_ref, acc8)
        for k, dst in enumerate([M_CBIAS] + [M_WB + k for k in range(4)] + [M_WA + k for k in range(3)]):
            sums_ref[dst:dst + 1, :] += jnp.sum(acc8[8 * k:8 * k + 8, :], axis=0, keepdims=True)
        due[tt:tt + 8, :] = due[0:8, :]
        dye[tt:tt + 8, :] = dye[0:8, :]

        @pl.when(i == nt - 1)
        def _():
            sums_ref[M_LS:M_LS + 1, :] = sums_ref[M_LS:M_LS + 1, :] * _sig(-prm_ref[P_LAM:P_LAM + 1, :])

    big = lambda: pltpu.VMEM((tt + 8, D), F32)
    tile = lambda: pltpu.VMEM((tt, D), F32)
    return pl.pallas_call(
        body, name="mixer_bwd", grid=(nt,),
        in_specs=[pl.BlockSpec((tt, 7 * D), lambda i: (rev(i), 0)),
                  pl.BlockSpec((tt, D), lambda i: (rev(i), 0)),
                  pl.BlockSpec((8, D), lambda i: (halo(i), 0)),
                  pl.BlockSpec((len(SV_PLANES), tt, D), lambda i: (0, rev(i), 0)),
                  pl.BlockSpec((tt, D), lambda i: (rev(i), 0)),
                  pl.BlockSpec((16, D), lambda i: (0, 0)),
                  pl.BlockSpec((HEADS, HB, HB), lambda i: (0, 0, 0)),
                  pl.BlockSpec((HEADS, HB, HB), lambda i: (0, 0, 0))],
        out_specs=[pl.BlockSpec((tt, 7 * D), lambda i: (rev(i), 0)),
                   pl.BlockSpec((16, D), lambda i: (0, 0)),
                   pl.BlockSpec((HEADS, HB, HB), lambda i: (0, 0, 0)),
                   pl.BlockSpec((HEADS, HB, HB), lambda i: (0, 0, 0))],
        out_shape=[jax.ShapeDtypeStruct((t_len, 7 * D), BF16), jax.ShapeDtypeStruct((16, D), F32),
                   jax.ShapeDtypeStruct((HEADS, HB, HB), F32), jax.ShapeDtypeStruct((HEADS, HB, HB), F32)],
        scratch_shapes=[big(), big(), big(), tile(), tile(),
                        pltpu.VMEM((8, D), F32), pltpu.VMEM((8, D), F32), pltpu.VMEM((64, D), F32)],
        compiler_params=_cp("arbitrary"),
    )(proj, hl, hl, sv, dmg, prm, wa, wx)


def _in_proj_bwd(dproj, w_in, x, dx1, mod, g_mix):
    t_len = x.shape[0]
    tm = min(TM, t_len)

    def body(dp_ref, w_ref, x_ref, dx1_ref, mod_ref, g_ref, gx_ref, sums_ref, acc):
        @pl.when(pl.program_id(0) == 0)
        def _():
            sums_ref[...] = jnp.zeros_like(sums_ref)

        def write(rows, dx):
            gx_ref[rows, :] = dx

        zero = jnp.zeros((8, D), F32)
        sums = (zero, zero, zero)
        for sub in _sub_blocks(tm):
            acc[sub, :] = _dot_nt(dp_ref[sub, :], w_ref[...])
            sums = _norm_bwd_rows(sub, sums, acc, x_ref, dx1_ref, mod_ref[1:2, :], g_ref[...], write)
        _add_norm_sums(sums_ref, sums)

    return pl.pallas_call(
        body, name="in_proj_bwd", grid=(t_len // tm,),
        in_specs=[pl.BlockSpec((tm, 7 * D), lambda i: (i, 0)),
                  _resident((D, 7 * D)),
                  pl.BlockSpec((tm, D), lambda i: (i, 0)), pl.BlockSpec((tm, D), lambda i: (i, 0)),
                  pl.BlockSpec((8, D), lambda i: (0, 0)), pl.BlockSpec((1, D), lambda i: (0, 0))],
        out_specs=[pl.BlockSpec((tm, D), lambda i: (i, 0)), pl.BlockSpec((8, D), lambda i: (0, 0))],
        out_shape=[jax.ShapeDtypeStruct((t_len, D), F32), jax.ShapeDtypeStruct((8, D), F32)],
        scratch_shapes=[pltpu.VMEM((tm, D), F32)],
        compiler_params=_cp("arbitrary"),
    )(dproj, w_in, x, dx1, mod, g_mix)


def _in_wgrad(h, dproj, g_wa, g_wx, order):
    t_len = h.shape[0]
    tk = min(TKI, t_len)
    nk = t_len // tk
    cw = 7 * D // NDEV
    hr = HB // NDEV

    def body(ord_ref, h_ref, d_ref, ga_ref, gx_ref, parts_ref, pa_ref, px_ref, acc, *scr):
        rs, sems = scr[:-6], scr[-6:]
        p, k = pl.program_id(0), pl.program_id(1)

        def head_rows(ref):
            return lambda s: ref.at[:, pl.ds(s * hr, hr), :]

        @pl.when((p == 0) & (k == 0))
        def _():
            for s in range(NDEV):
                _rs_send(head_rows(ga_ref)(s), pa_ref, s, *sems[0:3])
                _rs_send(head_rows(gx_ref)(s), px_ref, s, *sems[3:6])

        @pl.when(k == 0)
        def _():
            acc[...] = jnp.zeros_like(acc)

        acc[...] += _dot_tn(h_ref[...], d_ref[...])

        @pl.when(k == nk - 1)
        def _():
            q = ord_ref[p]
            for half in range(2):
                rs[0][q, half] = acc[:, half * cw:(half + 1) * cw].astype(BF16)
            _rs2_to_sibling(q, rs)

        @pl.when((k == min(1, nk - 1)) & (p > 0))
        def _():
            _rs2_forward(ord_ref[p - 1], parts_ref, rs)

        @pl.when((p == NCHIP - 1) & (k == nk - 1))
        def _():
            _rs2_forward(ord_ref[p], parts_ref, rs)
            _rs2_finish(parts_ref, rs)
            _rs_finish(head_rows(ga_ref), pa_ref, *sems[0:3])
            _rs_finish(head_rows(gx_ref), px_ref, *sems[3:6])

    return pl.pallas_call(
        body, name="in_wgrad",
        grid_spec=pltpu.PrefetchScalarGridSpec(
            num_scalar_prefetch=1, grid=(NCHIP, nk),
            in_specs=[pl.BlockSpec((tk, D), lambda p, k, o: (k, 0)),
                      pl.BlockSpec((tk, 2 * cw), lambda p, k, o: (k, o[p])), _ANY, _ANY],
            out_specs=[_ANY, _ANY, _ANY],
            scratch_shapes=[pltpu.VMEM((D, 2 * cw), F32)] + _rs2_scratch((D, cw)) + _RS_SEMS * 2),
        out_shape=[jax.ShapeDtypeStruct((NCHIP, D, cw), BF16), jax.ShapeDtypeStruct((NDEV, HEADS, hr, HB), F32),
                   jax.ShapeDtypeStruct((NDEV, HEADS, hr, HB), F32)],
        compiler_params=_cp("arbitrary", "arbitrary"),
    )(order, h, dproj, g_wa, g_wx)


def _adam_math(w, g, m, v):
    m = ADAM_B1 * m + (1.0 - ADAM_B1) * g
    v = ADAM_B2 * v + (1.0 - ADAM_B2) * (g * g)
    m_hat = m / (1.0 - ADAM_B1 ** ADAM_STEP)
    v_hat = v / (1.0 - ADAM_B2 ** ADAM_STEP)
    delta = -ADAM_LR * (m_hat / (jnp.sqrt(v_hat) + ADAM_EPS) + ADAM_WD * w)
    return delta, m, v


def _ada_bwd(c_all, dmod_cols, w, m, v):
    rb = 256
    n = w.shape[1]
    nrow = c_all.shape[0]

    def body(c_ref, d_ref, w_ref, m_ref, v_ref, g_ref, dl_ref, nm_ref, nv_ref):
        cv = c_ref[...]
        g = _dot_tn((cv * _sig(cv)).astype(BF16), d_ref[...].astype(BF16))
        g_ref[...] = g
        dl_ref[...], nm_ref[...], nv_ref[...] = _adam_math(w_ref[...], g, m_ref[...], v_ref[...])

    blk = pl.BlockSpec((rb, n), lambda i: (i, 0))
    sds = jax.ShapeDtypeStruct(w.shape, F32)
    return pl.pallas_call(
        body, name="ada_bwd", grid=(D // rb,),
        in_specs=[pl.BlockSpec((nrow, rb), lambda i: (0, i)), pl.BlockSpec((nrow, n), lambda i: (0, 0)), blk, blk, blk],
        out_specs=[blk, blk, blk, blk], out_shape=[sds, sds, sds, sds],
        compiler_params=_cp("parallel"),
    )(c_all, dmod_cols, w, m, v)


def _adam(name, parts, w, m, v):
    p, r, c = parts.shape
    rb = r
    for cand in (256, 128, 64, 32, 16, 8):
        if r % cand == 0 and r >= cand:
            rb = cand
            break

    def body(p_ref, w_ref, m_ref, v_ref, g_ref, dl_ref, nm_ref, nv_ref):
        g = p_ref[0].astype(F32)
        for q in range(1, p):
            g = g + p_ref[q].astype(F32)
        g_ref[...] = g
        dl_ref[...], nm_ref[...], nv_ref[...] = _adam_math(w_ref[...], g, m_ref[...], v_ref[...])

    blk = pl.BlockSpec((rb, c), lambda i: (i, 0))
    sds = jax.ShapeDtypeStruct((r, c), F32)
    return pl.pallas_call(
        body, name=name, grid=(r // rb,),
        in_specs=[pl.BlockSpec((p, rb, c), lambda i: (0, i, 0)), blk, blk, blk],
        out_specs=[blk, blk, blk, blk], out_shape=[sds, sds, sds, sds],
        compiler_params=_cp("parallel"),
    )(parts, w, m, v)


_SMALL_SEMS = [pltpu.SemaphoreType.DMA((7,)), pltpu.SemaphoreType.DMA((7,)), pltpu.SemaphoreType.DMA]
_VMEM = pl.BlockSpec(memory_space=pltpu.VMEM)


def _exchange_small(x_ref, out_ref, send_sems, recv_sems, local_sem):
    m_per = x_ref.shape[0]
    x, y, c = _my_pos()
    me, sibling = (x, y, c), (x, y, 1 - c)
    chips = [(1 - x, y), (x, 1 - y), (1 - x, 1 - y)]

    def rows(px, py, pc):
        return out_ref.at[pl.ds((4 * px + 2 * py + pc) * m_per, m_per), :]

    def copy(k, block, to, src=None):
        return pltpu.make_async_remote_copy(
            src_ref=rows(*block) if src is None else src, dst_ref=rows(*block),
            send_sem=send_sems.at[k], recv_sem=recv_sems.at[k], device_id=to, device_id_type=MESH)

    mine = pltpu.make_async_copy(x_ref, rows(*me), local_sem)
    mine.start()
    first = [copy(0, me, sibling, src=x_ref)]
    first += [copy(1 + j, me, (*chip, c), src=x_ref) for j, chip in enumerate(chips)]
    for cp in first:
        cp.start()
    passed = [copy(4 + j, (*chip, c), sibling) for j, chip in enumerate(chips)]
    for j, chip in enumerate(chips):
        copy(1 + j, (*chip, c), me).wait_recv()
        passed[j].start()
    copy(0, sibling, me).wait_recv()
    for j, chip in enumerate(chips):
        copy(4 + j, (*chip, 1 - c), me).wait_recv()
    for cp in first + passed:
        cp.wait_send()
    mine.wait()


def _gather_small_grads(sums1, sums2, msums, d_gt1, d_gt2, d_gfin, loss):
    def body(s1, s2, ms, g1, g2, gf, ls, out_ref, pack, *sems):
        rows = [s1[S_SH:S_SH + 1, :], s1[S_SC:S_SC + 1, :], g1[0:1, :],
                s2[S_SH:S_SH + 1, :], s2[S_SC:S_SC + 1, :], g2[0:1, :],
                s1[S_G:S_G + 1, :], ms[M_CBIAS:M_CBIAS + 1, :], ms[M_BA:M_BA + 1, :], ms[M_BX:M_BX + 1, :],
                ms[M_LS:M_LS + 1, :], s2[S_G:S_G + 1, :], gf[...]]
        rows += [ms[M_WA + k:M_WA + k + 1, :] for k in range(3)] + [ms[M_WB + k:M_WB + k + 1, :] for k in range(4)]
        rows += [jnp.broadcast_to(ls[0:1, 0:1], (1, D))]
        for i, v in enumerate(rows):
            pack[i:i + 1, :] = v
        pack[len(rows):24, :] = jnp.zeros((24 - len(rows), D), F32)
        _exchange_small(pack, out_ref, *sems)

    return pl.pallas_call(
        body, name="gather_small", out_shape=jax.ShapeDtypeStruct((NDEV * 24, D), F32),
        in_specs=[_VMEM] * 7, out_specs=_VMEM, scratch_shapes=[pltpu.VMEM((24, D), F32)] + _SMALL_SEMS,
    )(sums1, sums2, msums, d_gt1, d_gt2, d_gfin, loss)


def _ada_mod(pack, w_ada, b_cols):
    ncol = w_ada.shape[1]

    def body(p_ref, w_ref, b_ref, all_ref, mod_ref, cols, *sems):
        _exchange_small(p_ref, all_ref, *sems[0:3])
        c_all = jnp.concatenate([all_ref[8 * d:8 * d + 1, 0:D] for d in range(NDEV)], axis=0)
        c16 = jnp.concatenate([c_all, jnp.zeros_like(c_all)], axis=0)
        mod16 = _dot((c16 * _sig(c16)).astype(BF16), w_ref[...].astype(BF16)) + b_ref[...]
        cols[...] = mod16[0:NDEV]
        _exchange_small(cols, mod_ref, *sems[3:6])

    return pl.pallas_call(
        body, name="ada_mod",
        out_shape=[jax.ShapeDtypeStruct((NDEV * 8, pack.shape[1]), F32), jax.ShapeDtypeStruct((NDEV * 8, ncol), F32)],
        in_specs=[_VMEM, _VMEM, _VMEM], out_specs=[_VMEM, _VMEM],
        scratch_shapes=[pltpu.VMEM((NDEV, ncol), F32)] + _SMALL_SEMS * 2,
        compiler_params=_cp(),
    )(pack, w_ada, b_cols)


def _blk_rows(n):
    return lambda ref, b: ref.at[pl.ds(pl.multiple_of(b * n, 8), n), :]


def _blk_lead(ref, b):
    return ref.at[b]


def _blk_heads(ref, b):
    return ref.at[:, pl.ds(pl.multiple_of(b * (HB // NDEV), 8), HB // NDEV), :]


def _ag_phases(ins, outs, slicers, send_sems, recv_sems, local_sems):
    na = len(ins)
    x, y, c = _my_pos()
    me, sibling = (x, y, c), (x, y, 1 - c)
    chips = [(1 - x, y), (x, 1 - y), (1 - x, 1 - y)]

    def copy(a, k, block, to, from_shard=False):
        px, py, pc = block
        dst = slicers[a](outs[a], 4 * px + 2 * py + pc)
        return pltpu.make_async_remote_copy(
            src_ref=ins[a] if from_shard else dst, dst_ref=dst,
            send_sem=send_sems.at[a * 7 + k], recv_sem=recv_sems.at[a * 7 + k], device_id=to, device_id_type=MESH)

    def local(a):
        return pltpu.make_async_copy(ins[a], slicers[a](outs[a], 4 * x + 2 * y + c), local_sems.at[a])

    def firsts(a):
        return [copy(a, 0, me, sibling, True)] + [copy(a, 1 + j, me, (*chip, c), True) for j, chip in enumerate(chips)]

    def start():
        for a in range(na):
            local(a).start()
            for cp in firsts(a):
                cp.start()

    def forward():
        for a in range(na):
            for j, chip in enumerate(chips):
                copy(a, 1 + j, (*chip, c), me).wait_recv()
                copy(a, 4 + j, (*chip, c), sibling).start()

    def finish():
        for a in range(na):
            copy(a, 0, sibling, me).wait_recv()
            for j, chip in enumerate(chips):
                copy(a, 4 + j, (*chip, 1 - c), me).wait_recv()
        for a in range(na):
            for cp in firsts(a) + [copy(a, 4 + j, (*chip, c), sibling) for j, chip in enumerate(chips)]:
                cp.wait_send()
            local(a).wait()

    return start, forward, finish


def _ag_sems(na):
    return [pltpu.SemaphoreType.DMA((7 * na,)), pltpu.SemaphoreType.DMA((7 * na,)), pltpu.SemaphoreType.DMA((na,))]


def _local_step(x, target, mod, g_mix, g_ffn, g_fin, prm, w_in_shard, shards):
    fulls = [(HEADS, HB, HB), (HEADS, HB, HB), (D, D), (NDEV, FB, D), (DFF, D)]
    slicers = [_blk_heads, _blk_heads, _blk_rows(D // NDEV), _blk_lead, _blk_rows(DFF // NDEV)]
    my_chip = _my_index() >> 1
    own_first = (my_chip ^ jnp.arange(NCHIP, dtype=jnp.int32)).astype(jnp.int32)
    early, late = [0, 1, 2, 4], [3]
    pick = lambda lst, idx: [lst[i] for i in idx]
    proj, h, w_in, (wa, wx, w_out, w_down) = _in_proj(x, mod, g_mix, w_in_shard, own_first, pick(shards, early),
                                                      pick(fulls, early), pick(slicers, early))
    merged, hl, sv, (w_gu,) = _mixer_fwd(proj, prm, wa, wx, pick(shards, late), pick(fulls, late),
                                         pick(slicers, late))
    w_gu = w_gu.reshape(2, 4, FB, D)
    x1, h2 = _out_proj(merged, x, mod, g_ffn, w_out)
    gu, dx2, dx2b, loss, d_gfin = _ffn_fwd(h2, x1, target, mod, g_fin, w_gu, w_down)
    dgu, act, dx1, dx1b, dmg, sums2 = _ffn_bwd(dx2, gu, x1, mod, g_ffn, w_gu, w_down, w_out)
    chip_order = _xor_order(_my_index() >> 1, NCHIP)
    p_wgu = _gu_wgrad(h2, dgu, chip_order)
    p_wdown, d_gt2 = _scaled_wgrad("down_wgrad", act, dx2b, w_down, 5, mod, chip_order)
    p_wout, d_gt1 = _scaled_wgrad("out_wgrad", merged, dx1b, w_out, 2, mod, chip_order)
    dproj, msums, g_wa, g_wx = _mixer_bwd(proj, hl, sv, dmg, prm, wa, wx)
    p_win, p_wa, p_wx = _in_wgrad(h, dproj, g_wa, g_wx, chip_order)
    grad_x, sums1 = _in_proj_bwd(dproj, w_in, x, dx1, mod, g_mix)
    return dict(loss=loss, grad_x=grad_x, d_gfin=d_gfin, sums1=sums1, sums2=sums2, msums=msums,
                d_gt1=d_gt1, d_gt2=d_gt2, p_win=p_win, p_wa=p_wa, p_wx=p_wx, p_wout=p_wout, p_wgu=p_wgu,
                p_wdown=p_wdown)


def kernel(x, c, w_ada, b_ada, g_norm_mix, w_in, conv_a_w, conv_b_w, conv_b_bias, w_rg_a, b_rg_a, w_rg_x, b_rg_x, lru_lambda, w_out, g_norm_ffn, w_gate_up, w_down, g_norm_final, loss_target, m_w_ada, m_b_ada, m_g_norm_mix, m_w_in, m_conv_a_w, m_conv_b_w, m_conv_b_bias, m_w_rg_a, m_b_rg_a, m_w_rg_x, m_b_rg_x, m_lru_lambda, m_w_out, m_g_norm_ffn, m_w_gate_up, m_w_down, m_g_norm_final, v_w_ada, v_b_ada, v_g_norm_mix, v_w_in, v_conv_a_w, v_conv_b_w, v_conv_b_bias, v_w_rg_a, v_b_rg_a, v_w_rg_x, v_b_rg_x, v_lru_lambda, v_w_out, v_g_norm_ffn, v_w_gate_up, v_w_down, v_g_norm_final):
    me = 4 * lax.axis_index("x") + 2 * lax.axis_index("y") + lax.axis_index("c")
    ncol = w_ada.shape[2]
    cw = conv_a_w.shape[2]

    pack0 = jnp.concatenate([c, conv_a_w.reshape(1, 3 * cw), conv_b_w.reshape(1, 4 * cw)], axis=1)
    b_cols = lax.dynamic_slice_in_dim(b_ada, me * ncol, ncol, axis=1)
    got0, got1 = _ada_mod(jnp.broadcast_to(pack0, (8, pack0.shape[1])), w_ada[0], b_cols)
    got0 = got0.reshape(NDEV, 8, -1)[:, 0, :]
    c_all = got0[:, :D]
    conv_a = got0[:, D:D + 3 * cw].reshape(NDEV, 3, cw).transpose(1, 0, 2).reshape(3, D)
    conv_b = got0[:, D + 3 * cw:].reshape(NDEV, 4, cw).transpose(1, 0, 2).reshape(4, D)
    c16 = jnp.concatenate([c_all, jnp.zeros((8, D), F32)], axis=0)
    mod6 = lax.dynamic_index_in_dim(got1.reshape(NDEV, NDEV, ncol), me, axis=1, keepdims=False).reshape(6, D)
    mod = jnp.concatenate([mod6, jnp.zeros((2, D), F32)], axis=0)

    tr = lambda a: jnp.swapaxes(a, 1, 2)
    shards = [w_rg_a[0].astype(BF16), w_rg_x[0].astype(BF16), w_out[0].astype(BF16), tr(w_gate_up)[0].astype(BF16),
              w_down[0].astype(BF16)]

    prm = jnp.concatenate([conv_a, conv_b, conv_b_bias, b_rg_a, b_rg_x, lru_lambda, jnp.zeros((5, D), F32)], axis=0)
    r = _local_step(x[0], loss_target[0], mod, g_norm_mix, g_norm_ffn, g_norm_final.reshape(1, D), prm,
                    w_in[0].astype(BF16), shards)

    parts = [r["p_win"], r["p_wa"], r["p_wx"], r["p_wout"], r["p_wgu"], r["p_wdown"]]
    big = {}
    for nm, p, w, m, v in (("w_in", parts[0], w_in, m_w_in, v_w_in), ("w_rg_a", parts[1], w_rg_a, m_w_rg_a, v_w_rg_a),
                           ("w_rg_x", parts[2], w_rg_x, m_w_rg_x, v_w_rg_x), ("w_out", parts[3], w_out, m_w_out, v_w_out),
                           ("w_gate_up", parts[4], tr(w_gate_up), tr(m_w_gate_up), tr(v_w_gate_up)),
                           ("w_down", parts[5], w_down, m_w_down, v_w_down)):
        two_d = (-1, w.shape[-1])
        outs = _adam("adam_" + nm, p.reshape((p.shape[0],) + w.reshape(two_d).shape), w.reshape(two_d), m.reshape(two_d),
                     v.reshape(two_d))
        big[nm] = [o.reshape(w.shape) for o in outs]
    big["w_gate_up"] = [tr(o) for o in big["w_gate_up"]]

    got2 = _gather_small_grads(r["sums1"], r["sums2"], r["msums"], r["d_gt1"], r["d_gt2"], r["d_gfin"],
                               r["loss"]).reshape(NDEV, 24, D)

    rep_w = jnp.concatenate([b_ada.reshape(6, D), g_norm_mix, conv_b_bias, b_rg_a, b_rg_x, lru_lambda, g_norm_ffn,
                             g_norm_final.reshape(1, D), jnp.zeros((3, D), F32)], axis=0)
    rep_m = jnp.concatenate([m_b_ada.reshape(6, D), m_g_norm_mix, m_conv_b_bias, m_b_rg_a, m_b_rg_x, m_lru_lambda,
                             m_g_norm_ffn, m_g_norm_final.reshape(1, D), jnp.zeros((3, D), F32)], axis=0)
    rep_v = jnp.concatenate([v_b_ada.reshape(6, D), v_g_norm_mix, v_conv_b_bias, v_b_rg_a, v_b_rg_x, v_lru_lambda,
                             v_g_norm_ffn, v_g_norm_final.reshape(1, D), jnp.ones((3, D), F32)], axis=0)
    rep = _adam("adam_rep", got2[:, :16, :], rep_w, rep_m, rep_v)

    conv_parts = lax.dynamic_slice_in_dim(got2[:, 13:21, :], me * cw, cw, axis=2)
    cv_w = jnp.concatenate([conv_a_w[0], conv_b_w[0], jnp.zeros((1, cw), F32)], axis=0)
    cv_m = jnp.concatenate([m_conv_a_w[0], m_conv_b_w[0], jnp.zeros((1, cw), F32)], axis=0)
    cv_v = jnp.concatenate([v_conv_a_w[0], v_conv_b_w[0], jnp.ones((1, cw), F32)], axis=0)
    cvo = _adam("adam_conv", conv_parts, cv_w, cv_m, cv_v)

    dmod_cols = lax.dynamic_slice_in_dim(got2[:, :6, :].reshape(NDEV, 6 * D), me * ncol, ncol, axis=1)
    dmod16 = jnp.concatenate([dmod_cols, jnp.zeros((8, ncol), F32)], axis=0)
    ada = _ada_bwd(c16, dmod16, w_ada[0], m_w_ada[0], v_w_ada[0])

    loss = jnp.sum(got2[:, 20, 0])

    def pick(q):
        one = lambda i: rep[q][i:i + 1]
        return [ada[q].reshape(w_ada.shape), rep[q][0:6].reshape(b_ada.shape), one(6), big["w_in"][q],
                cvo[q][0:3].reshape(conv_a_w.shape), cvo[q][3:7].reshape(conv_b_w.shape), one(7),
                big["w_rg_a"][q], one(8), big["w_rg_x"][q], one(9), one(10), big["w_out"][q], one(11),
                big["w_gate_up"][q], big["w_down"][q], rep[q][12]]

    return (loss, r["grad_x"].reshape(x.shape), *pick(0), *pick(1), *pick(2), *pick(3))
```

```python
import math

import jax
import jax.numpy as jnp
from jax import lax
from jax.experimental import pallas as pl
from jax.experimental.pallas import tpu as pltpu

F32 = jnp.float32
BF16 = jnp.bfloat16

D = 1024
DFF = 2816
NDEV = 8
HEADS = 4
HB = D // HEADS
FB = DFF // 4
EPS = 1e-6
LRU_C = 8.0
ADAM_LR, ADAM_B1, ADAM_B2, ADAM_EPS, ADAM_WD, ADAM_STEP = 0.001, 0.9, 0.999, 1e-08, 0.01, 10

VMEM_LIMIT = 56 * 1024 * 1024
TM = 512
TMI = 1024
TMF = 256
TK = 2048
TKI = 2048
SUB = 256
TT = 256
CG = 256
MESH = pl.DeviceIdType.MESH


def _cp(*sem):
    return pltpu.CompilerParams(dimension_semantics=sem, vmem_limit_bytes=VMEM_LIMIT)


def _sig(x):
    return 1.0 / (1.0 + jnp.exp(-x))


def _log_sigmoid(x):
    z = jnp.exp(-jnp.abs(x))
    u = 1.0 + z
    d = u - 1.0
    l1p = jnp.where(d == 0.0, z, jnp.log(u) * (z / jnp.where(d == 0.0, 1.0, d)))
    return -(jnp.maximum(-x, 0.0) + l1p)


def _neg_expm1(x):
    p = x * (1.0 + x * 0.5 * (1.0 + x * (1.0 / 3.0) * (1.0 + x * 0.25 * (1.0 + x * 0.2 * (1.0 + x * (1.0 / 6.0))))))
    return jnp.where(x > -0.25, -p, 1.0 - jnp.exp(x))


_GC = math.sqrt(2.0 / math.pi)


def _gelu(x):
    t = jnp.tanh(_GC * (x + 0.044715 * x * x * x))
    return 0.5 * x * (1.0 + t), t


def _dot(a, b):
    return jnp.dot(a, b, preferred_element_type=F32)


def _dot_nt(a, b):
    return lax.dot_general(a, b, (((1,), (1,)), ((), ())), preferred_element_type=F32)


def _dot_tn(a, b):
    return lax.dot_general(a, b, (((0,), (0,)), ((), ())), preferred_element_type=F32)


def _resident(shape):
    return pl.BlockSpec(shape, lambda *_: (0,) * len(shape), pipeline_mode=pl.Buffered(1))


def _sub_blocks(n_rows):
    step = min(SUB, n_rows)
    return [slice(r, r + step) for r in range(0, n_rows, step)]


def _fold8(v):
    return v[0:8] + v[8:16]


def _pj(ref, s, rows=slice(None), cols=slice(0, D)):
    return ref[rows, s * D + cols.start:s * D + cols.stop]


def _in_proj(x, mod, g_mix, w_shard, order, shards, fulls, slicers):
    t_len = x.shape[0]
    tm = min(TMI, t_len)
    ni = t_len // tm
    na = len(shards)
    cw = 7 * D // NDEV
    rc = 32

    def body(ord_ref, x_ref, mod_ref, g_ref, wsh_ref, *rest):
        ins, (proj_ref, h_ref, wfull_ref), outs = rest[:na], rest[na:na + 3], rest[na + 3:2 * na + 3]
        h_scr, w_scr, wsend, wrecv, wlocal, wout = rest[2 * na + 3:2 * na + 9]
        start, forward, finish = _ag_phases(ins, outs, slicers, *rest[2 * na + 9:])
        p, i = pl.program_id(0), pl.program_id(1)
        x_, y_, c = _my_pos()
        me, sibling = (x_, y_, c), (x_, y_, 1 - c)
        chip_at = [None, (x_, 1 - y_), (1 - x_, y_), (1 - x_, 1 - y_)]

        def cols(px, py, pc):
            return w_scr.at[:, pl.ds(pl.multiple_of((4 * px + 2 * py + pc) * cw, 128), cw)]

        def wcopy(k, block, to, from_shard=False):
            dst = cols(*block)
            return pltpu.make_async_remote_copy(src_ref=wsh_ref if from_shard else dst, dst_ref=dst,
                                                send_sem=wsend.at[k], recv_sem=wrecv.at[k], device_id=to,
                                                device_id_type=MESH)

        own_local = pltpu.make_async_copy(wsh_ref, cols(*me), wlocal)
        to_hbm = pltpu.make_async_copy(w_scr, wfull_ref, wout)

        @pl.when((p == 0) & (i == 0))
        def _():
            own_local.start()
            wcopy(0, me, sibling, True).start()
            for q in (1, 2):
                wcopy(q, me, (*chip_at[q], c), True).start()
            own_local.wait()
            wcopy(0, sibling, me).wait_recv()

        @pl.when((p == 0) & (i == ni // 2))
        def _():
            wcopy(3, me, (*chip_at[3], c), True).start()

        for q in (1, 2, 3):
            @pl.when((p == q - 1) & (i == ni - 1))
            def _():
                wcopy(q, (*chip_at[q], c), me).wait_recv()
                wcopy(3 + q, (*chip_at[q], c), sibling).start()

            @pl.when((p == q) & (i == 0))
            def _():
                wcopy(3 + q, (*chip_at[q], 1 - c), me).wait_recv()

        @pl.when((p == 1) & (i == 0))
        def _():
            start()

        @pl.when((p == NCHIP - 1) & (i == ni // 2))
        def _():
            forward()

        @pl.when((p == NCHIP - 1) & (i == 0))
        def _():
            to_hbm.start()

        gs = g_ref[...] * (1.0 + mod_ref[1:2, :])
        sh = mod_ref[0:1, :]

        wcols = pl.ds(pl.multiple_of(ord_ref[p] * (2 * cw), 128), 2 * cw)
        for sub in _sub_blocks(tm):
            for r0 in range(sub.start, sub.stop, rc):
                xv = x_ref[r0:r0 + rc, :]
                r = lax.rsqrt(jnp.mean(xv * xv, axis=-1, keepdims=True) + EPS)
                h_scr[r0:r0 + rc, :] = (xv * r * gs + sh).astype(BF16)
            proj_ref[sub, :] = _dot(h_scr[sub, :], w_scr[:, wcols]).astype(BF16)

        @pl.when(p == 0)
        def _():
            h_ref[...] = h_scr[...]

        @pl.when((p == NCHIP - 1) & (i == ni - 1))
        def _():
            wcopy(0, me, sibling, True).wait_send()
            for q in (1, 2, 3):
                wcopy(q, me, (*chip_at[q], c), True).wait_send()
                wcopy(3 + q, (*chip_at[q], c), sibling).wait_send()
            finish()
            to_hbm.wait()

    res = pl.pallas_call(
        body, name="in_proj",
        grid_spec=pltpu.PrefetchScalarGridSpec(
            num_scalar_prefetch=1, grid=(NCHIP, ni),
            in_specs=[pl.BlockSpec((tm, D), lambda p, i, o: (i, 0)),
                      pl.BlockSpec((8, D), lambda p, i, o: (0, 0)),
                      pl.BlockSpec((1, D), lambda p, i, o: (0, 0))] + [_ANY] * (1 + na),
            out_specs=[pl.BlockSpec((tm, 2 * cw), lambda p, i, o: (i, o[p])),
                       pl.BlockSpec((tm, D), lambda p, i, o: (jnp.where(p == 0, i, ni - 1), 0))]
            + [_ANY] * (1 + na),
            scratch_shapes=[pltpu.VMEM((tm, D), BF16), pltpu.VMEM((D, 7 * D), BF16),
                            pltpu.SemaphoreType.DMA((7,)), pltpu.SemaphoreType.DMA((7,)),
                            pltpu.SemaphoreType.DMA, pltpu.SemaphoreType.DMA] + _ag_sems(na)),
        out_shape=[jax.ShapeDtypeStruct((t_len, 7 * D), BF16), jax.ShapeDtypeStruct((t_len, D), BF16),
                   jax.ShapeDtypeStruct((D, 7 * D), BF16)]
        + [jax.ShapeDtypeStruct(f, sh.dtype) for f, sh in zip(fulls, shards)],
        compiler_params=_cp("arbitrary", "arbitrary"),
    )(order, x, mod, g_mix, w_shard, *shards)
    return res[0], res[1], res[2], res[3:]


P_WA, P_WB, P_CBIAS, P_BA, P_BX, P_LAM = 0, 3, 7, 8, 9, 10
SVH_PLANES = H_U, H_YA, H_R, H_I = range(4)
SVF_PLANES = F_A, F_MULT = range(2)


def _lru_gates(rp, ip, ls, first_row):
    r = _sig(rp)
    ig = _sig(ip)
    la = LRU_C * r * ls
    a = jnp.exp(la)
    m2 = _neg_expm1(2.0 * la)
    mult = jnp.where(first_row, 1.0, jnp.sqrt(jnp.maximum(m2, 0.0)))
    return r, ig, la, a, m2, mult


def _shift_down(cur, prev, s, row):
    return jnp.where(row >= s, pltpu.roll(cur, s, 0), pltpu.roll(prev, s, 0))


def _shift_up(cur, nxt, s, row):
    return jnp.where(row < 8 - s, pltpu.roll(cur, 8 - s, 0), pltpu.roll(nxt, 8 - s, 0))


def _conv_fwd_rows(tt, proj_ref, prm_ref, xe, ve, u_s, ub_s, ya_s, yab_s):
    row = lax.broadcasted_iota(jnp.int32, (8, CG), 0)
    w_b = [prm_ref[P_WB + k:P_WB + k + 1, :] for k in range(4)]
    w_a = [prm_ref[P_WA + k:P_WA + k + 1, :] for k in range(3)]
    bias = prm_ref[P_CBIAS:P_CBIAS + 1, :]

    def blk(ib, carry):
        r0 = pl.multiple_of(ib * 16, 16)
        rows = pl.ds(r0, 16)
        for g in range(D // CG):
            cs = slice(g * CG, (g + 1) * CG)
            x16 = _pj(proj_ref, 3, rows, cs).astype(F32)
            v16 = _pj(proj_ref, 1, rows, cs).astype(F32) * _pj(proj_ref, 2, rows, cs).astype(F32)
            xp = xe[pl.ds(r0, 8), cs]
            vp = ve[pl.ds(r0, 8), cs]
            xe[pl.ds(r0 + 8, 16), cs] = x16
            ve[pl.ds(r0 + 8, 16), cs] = v16
            us, yas = [], []
            for sb in range(2):
                xc, vc = x16[8 * sb:8 * sb + 8], v16[8 * sb:8 * sb + 8]
                u8 = bias[:, cs] + w_b[3][:, cs] * xc
                for s in (1, 2, 3):
                    u8 = u8 + w_b[3 - s][:, cs] * _shift_down(xc, xp, s, row)
                y8 = w_a[2][:, cs] * vc
                for s in (1, 2):
                    y8 = y8 + w_a[2 - s][:, cs] * _shift_down(vc, vp, s, row)
                us.append(u8)
                yas.append(y8)
                xp, vp = xc, vc
            u16 = jnp.concatenate(us, axis=0)
            ya16 = jnp.concatenate(yas, axis=0)
            u_s[rows, cs] = u16
            ub_s[rows, cs] = u16.astype(BF16)
            ya_s[rows, cs] = ya16
            yab_s[rows, cs] = ya16.astype(BF16)
        return carry

    lax.fori_loop(0, tt // 16, blk, 0)


def _mixer_fwd(proj, prm, wa, wx, shards, fulls, slicers):
    t_len = proj.shape[0]
    tt = min(TT, t_len)
    nt = t_len // tt
    na = len(shards)

    def body(proj_ref, prm_ref, wa_ref, wx_ref, *rest):
        ins, (mg_ref, hl_ref, svh_ref, svf_ref), outs = rest[:na], rest[na:na + 4], rest[na + 4:2 * na + 4]
        xe, ve, hc, rp_s, ip_s, u_s, ya_s = rest[2 * na + 4:2 * na + 11]
        start, forward, finish = _ag_phases(ins, outs, slicers, *rest[2 * na + 11:])
        t = pl.program_id(0)

        @pl.when(t == 0)
        def _():
            start()
            xe[0:8, :] = jnp.zeros((8, D), F32)
            ve[0:8, :] = jnp.zeros((8, D), F32)
            hc[...] = jnp.zeros((8, D), F32)

        @pl.when(t == (3 * nt) // 4)
        def _():
            forward()

        _conv_fwd_rows(tt, proj_ref, prm_ref, xe, ve, u_s, svh_ref.at[H_U], ya_s, svh_ref.at[H_YA])
        xe[0:8, :] = xe[tt:tt + 8, :]
        ve[0:8, :] = ve[tt:tt + 8, :]

        ub = svh_ref[H_U]
        for h in range(HEADS):
            cs = slice(h * HB, (h + 1) * HB)
            rp_s[:, cs] = _dot(ub[:, cs], wa_ref[h]) + prm_ref[P_BA:P_BA + 1, cs]
            ip_s[:, cs] = _dot(ub[:, cs], wx_ref[h]) + prm_ref[P_BX:P_BX + 1, cs]

        ls_all = _log_sigmoid(prm_ref[P_LAM:P_LAM + 1, :])
        row = lax.broadcasted_iota(jnp.int32, (8, CG), 0)

        def blk(i, carry):
            r0 = pl.multiple_of(i * 16, 16)
            for g in range(D // CG):
                cs = slice(g * CG, (g + 1) * CG)
                ls = ls_all[:, cs]
                hprev = hc[:, cs]
                hs, rs, igs = [], [], []
                for sb in range(2):
                    rr = r0 + 8 * sb
                    first = (row + (t * tt + rr)) == 0
                    r8 = pl.ds(rr, 8)
                    r, ig, _, a, _, mult = _lru_gates(rp_s[r8, cs], ip_s[r8, cs], ls, first)
                    rs.append(r)
                    igs.append(ig)
                    svf_ref[F_A, r8, cs] = a
                    svf_ref[F_MULT, r8, cs] = mult
                    b = mult * (ig * u_s[r8, cs])
                    for s in (1, 2, 4):
                        a_sh = jnp.where(row >= s, pltpu.roll(a, s, 0), 1.0)
                        b_sh = jnp.where(row >= s, pltpu.roll(b, s, 0), 0.0)
                        b = a * b_sh + b
                        a = a * a_sh
                    hv = a * hprev + b
                    hprev = jnp.broadcast_to(hv[7:8, :], hv.shape)
                    hs.append(hv)
                hc[:, cs] = hprev
                h16 = jnp.concatenate(hs, axis=0)
                rows = pl.ds(r0, 16)
                svh_ref[H_R, rows, cs] = jnp.concatenate(rs, axis=0).astype(BF16)
                svh_ref[H_I, rows, cs] = jnp.concatenate(igs, axis=0).astype(BF16)
                gl, _ = _gelu(_pj(proj_ref, 4, rows, cs).astype(F32))
                y_b = h16 * gl
                y_a = _pj(proj_ref, 0, rows, cs).astype(F32) * ya_s[rows, cs]
                mg = (_sig(_pj(proj_ref, 5, rows, cs).astype(F32)) * y_a
                      + _sig(_pj(proj_ref, 6, rows, cs).astype(F32)) * y_b)
                mg_ref[rows, cs] = mg.astype(BF16)
                hl_ref[rows, cs] = h16.astype(BF16)
            return carry

        lax.fori_loop(0, tt // 16, blk, 0)

        @pl.when(t == nt - 1)
        def _():
            finish()

    res = pl.pallas_call(
        body, name="mixer_fwd", grid=(nt,),
        in_specs=[pl.BlockSpec((tt, 7 * D), lambda t: (t, 0)),
                  pl.BlockSpec((16, D), lambda t: (0, 0)),
                  pl.BlockSpec((HEADS, HB, HB), lambda t: (0, 0, 0)),
                  pl.BlockSpec((HEADS, HB, HB), lambda t: (0, 0, 0))] + [_ANY] * na,
        out_specs=[pl.BlockSpec((tt, D), lambda t: (t, 0)), pl.BlockSpec((tt, D), lambda t: (t, 0)),
                   pl.BlockSpec((len(SVH_PLANES), tt, D), lambda t: (0, t, 0)),
                   pl.BlockSpec((len(SVF_PLANES), tt, D), lambda t: (0, t, 0))] + [_ANY] * na,
        out_shape=[jax.ShapeDtypeStruct((t_len, D), BF16), jax.ShapeDtypeStruct((t_len, D), BF16),
                   jax.ShapeDtypeStruct((len(SVH_PLANES), t_len, D), BF16),
                   jax.ShapeDtypeStruct((len(SVF_PLANES), t_len, D), F32)]
        + [jax.ShapeDtypeStruct(f, sh.dtype) for f, sh in zip(fulls, shards)],
        scratch_shapes=[pltpu.VMEM((tt + 8, D), F32), pltpu.VMEM((tt + 8, D), F32), pltpu.VMEM((8, D), F32),
                        pltpu.VMEM((tt, D), F32), pltpu.VMEM((tt, D), F32), pltpu.VMEM((tt, D), F32),
                        pltpu.VMEM((tt, D), F32)]
        + _ag_sems(na),
        compiler_params=_cp("arbitrary"),
    )(proj, prm, wa, wx, *shards)
    return res[0], res[1], (res[2], res[3]), res[4:]


def _out_proj(merged, x, mod, g_ffn, w_out):
    t_len = x.shape[0]
    tm = min(TM, t_len)

    def body(mg_ref, x_ref, mod_ref, g_ref, w_ref, x1_ref, h2_ref):
        gt1 = mod_ref[2:3, :]
        gs = g_ref[...] * (1.0 + mod_ref[4:5, :])
        sh = mod_ref[3:4, :]
        for sub in _sub_blocks(tm):
            x1_ref[sub, :] = x_ref[sub, :] + gt1 * _dot(mg_ref[sub, :], w_ref[...])
            for r0 in range(sub.start, sub.stop, 16):
                x1 = x1_ref[r0:r0 + 16, :]
                r = lax.rsqrt(jnp.mean(x1 * x1, axis=-1, keepdims=True) + EPS)
                h2_ref[r0:r0 + 16, :] = (x1 * r * gs + sh).astype(BF16)

    return pl.pallas_call(
        body, name="out_proj", grid=(t_len // tm,),
        in_specs=[pl.BlockSpec((tm, D), lambda i: (i, 0)), pl.BlockSpec((tm, D), lambda i: (i, 0)),
                  pl.BlockSpec((8, D), lambda i: (0, 0)), pl.BlockSpec((1, D), lambda i: (0, 0)),
                  pl.BlockSpec((D, D), lambda i: (0, 0))],
        out_specs=[pl.BlockSpec((tm, D), lambda i: (i, 0)), pl.BlockSpec((tm, D), lambda i: (i, 0))],
        out_shape=[jax.ShapeDtypeStruct((t_len, D), F32), jax.ShapeDtypeStruct((t_len, D), BF16)],
        compiler_params=_cp("parallel"),
    )(merged, x, mod, g_ffn, w_out)


def _ffn_fwd(h2, x1, target, mod, g_fin, w_gu, w_down):
    t_len = x1.shape[0]
    tm = min(TMF, t_len)

    def body(h2_ref, x1_ref, tg_ref, mod_ref, g_ref, wgu_ref, wd_ref, gu_ref, dx2_ref, dx2b_ref, loss_ref, dg_ref, acc):
        @pl.when(pl.program_id(0) == 0)
        def _():
            loss_ref[...] = jnp.zeros_like(loss_ref)
            dg_ref[...] = jnp.zeros_like(dg_ref)

        hb = h2_ref[...]
        ffn = None
        nxt = (_dot_nt(hb, wgu_ref[0, 0]), _dot_nt(hb, wgu_ref[1, 0]))
        for j in range(4):
            gate, up = nxt
            if j < 3:
                nxt = (_dot_nt(hb, wgu_ref[0, j + 1]), _dot_nt(hb, wgu_ref[1, j + 1]))
            gu_ref[0, j] = gate.astype(BF16)
            gu_ref[1, j] = up.astype(BF16)
            act = (gate * _sig(gate) * up).astype(BF16)
            part = _dot(act, wd_ref[j * FB:(j + 1) * FB, :])
            ffn = part if ffn is None else ffn + part
        acc[...] = ffn

        gt2 = mod_ref[5:6, :]
        gf = g_ref[...]

        s_loss = s_dg = jnp.zeros((8, D), F32)
        for r0 in range(0, tm, 16):
            rows = slice(r0, r0 + 16)
            x2 = x1_ref[rows, :] + gt2 * acc[rows, :]
            r = lax.rsqrt(jnp.mean(x2 * x2, axis=-1, keepdims=True) + EPS)
            xn = x2 * r
            diff = xn * gf - tg_ref[rows, :]
            dy = diff * (1.0 / D)
            dxn = dy * gf
            dx2 = r * (dxn - xn * jnp.mean(dxn * xn, axis=-1, keepdims=True))
            dx2_ref[rows, :] = dx2
            dx2b_ref[rows, :] = dx2.astype(BF16)
            s_loss, s_dg = s_loss + _fold8(diff * diff), s_dg + _fold8(dy * xn)
        loss_ref[...] += jnp.sum(s_loss) * (0.5 / D)
        dg_ref[...] += jnp.sum(s_dg, axis=0, keepdims=True)

    row = pl.BlockSpec((tm, D), lambda i: (i, 0))
    return pl.pallas_call(
        body, name="ffn_fwd", grid=(t_len // tm,),
        in_specs=[row, row, row, pl.BlockSpec((8, D), lambda i: (0, 0)), pl.BlockSpec((1, D), lambda i: (0, 0)),
                  _resident((2, 4, FB, D)), _resident((DFF, D))],
        out_specs=[pl.BlockSpec((2, 4, tm, FB), lambda i: (0, 0, i, 0)), row, row,
                   pl.BlockSpec((1, 128), lambda i: (0, 0)), pl.BlockSpec((1, D), lambda i: (0, 0))],
        out_shape=[jax.ShapeDtypeStruct((2, 4, t_len, FB), BF16), jax.ShapeDtypeStruct((t_len, D), F32),
                   jax.ShapeDtypeStruct((t_len, D), BF16),
                   jax.ShapeDtypeStruct((1, 128), F32), jax.ShapeDtypeStruct((1, D), F32)],
        scratch_shapes=[pltpu.VMEM((tm, D), F32)],
        compiler_params=_cp("arbitrary"),
    )(h2, x1, target, mod, g_fin, w_gu, w_down)


S_SH, S_SC, S_G = 0, 1, 2


def _norm_bwd_rows(span, sums, dh_ref, x_ref, dres_ref, scale, gain, write):
    gs = 1.0 + scale
    s_sh, s_sc, s_g = sums
    for r0 in range(span.start, span.stop, 16):
        rows = slice(r0, r0 + 16)
        dh = dh_ref[rows, :]
        xv = x_ref[rows, :]
        r = lax.rsqrt(jnp.mean(xv * xv, axis=-1, keepdims=True) + EPS)
        xn = xv * r
        dhn = dh * gs
        dxn = dhn * gain
        write(rows, dres_ref[rows, :] + r * (dxn - xn * jnp.mean(dxn * xn, axis=-1, keepdims=True)))
        s_sh, s_sc, s_g = s_sh + _fold8(dh), s_sc + _fold8(dh * (xn * gain)), s_g + _fold8(dhn * xn)
    return s_sh, s_sc, s_g


def _add_norm_sums(sums_ref, sums):
    for dst, s in zip((S_SH, S_SC, S_G), sums):
        sums_ref[dst:dst + 1, :] += jnp.sum(s, axis=0, keepdims=True)


def _ffn_bwd(dx2, gu, x1, mod, g_ffn, w_gu, w_down, w_out):
    t_len = x1.shape[0]
    tm = min(TMF, t_len)

    def body(dx2_ref, gu_ref, x1_ref, mod_ref, g_ref, wgu_ref, wd_ref, wo_ref,
             dgu_ref, act_ref, dx1_ref, dx1b_ref, dmg_ref, sums_ref, acc, dmo, dact_s):
        @pl.when(pl.program_id(0) == 0)
        def _():
            sums_ref[...] = jnp.zeros_like(sums_ref)

        dffn = (dx2_ref[...] * mod_ref[5:6, :]).astype(BF16)
        dact_s[0] = _dot_nt(dffn, wd_ref[0:FB, :])
        for j in range(4):
            if j < 3:
                dact_s[(j + 1) % 2] = _dot_nt(dffn, wd_ref[(j + 1) * FB:(j + 2) * FB, :])
            for r0 in range(0, tm, 16):
                rows = slice(r0, r0 + 16)
                dact = dact_s[j % 2, rows, :]
                gate = gu_ref[0, j, rows, :].astype(F32)
                up = gu_ref[1, j, rows, :].astype(F32)
                sg = _sig(gate)
                silu = gate * sg
                act_ref[j, rows, :] = (silu * up).astype(BF16)
                dgu_ref[0, j, rows, :] = (dact * up * (sg * (1.0 + gate * (1.0 - sg)))).astype(BF16)
                dgu_ref[1, j, rows, :] = (dact * silu).astype(BF16)
            part = _dot(dgu_ref[0, j], wgu_ref[0, j]) + _dot(dgu_ref[1, j], wgu_ref[1, j])
            if j == 0:
                acc[...] = part
            else:
                acc[...] += part

        gt1 = mod_ref[2:3, :]

        def write(rows, dx1):
            dx1_ref[rows, :] = dx1
            dx1b_ref[rows, :] = dx1.astype(BF16)
            dmo[rows, :] = (dx1 * gt1).astype(BF16)

        zero = jnp.zeros((8, D), F32)
        sums = (zero, zero, zero)
        for sub in (slice(0, tm // 2), slice(tm // 2, tm)):
            sums = _norm_bwd_rows(sub, sums, acc, x1_ref, dx2_ref, mod_ref[4:5, :], g_ref[...], write)
            dmg_ref[sub, :] = _dot_nt(dmo[sub, :], wo_ref[...]).astype(BF16)
        _add_norm_sums(sums_ref, sums)

    row = pl.BlockSpec((tm, D), lambda i: (i, 0))
    return pl.pallas_call(
        body, name="ffn_bwd", grid=(t_len // tm,),
        in_specs=[row, pl.BlockSpec((2, 4, tm, FB), lambda i: (0, 0, i, 0)), row,
                  pl.BlockSpec((8, D), lambda i: (0, 0)), pl.BlockSpec((1, D), lambda i: (0, 0)),
                  _resident((2, 4, FB, D)), _resident((DFF, D)), _resident((D, D))],
        out_specs=[pl.BlockSpec((2, 4, tm, FB), lambda i: (0, 0, i, 0)),
                   pl.BlockSpec((4, tm, FB), lambda i: (0, i, 0)), row, row, row,
                   pl.BlockSpec((8, D), lambda i: (0, 0))],
        out_shape=[jax.ShapeDtypeStruct((2, 4, t_len, FB), BF16), jax.ShapeDtypeStruct((4, t_len, FB), BF16),
                   jax.ShapeDtypeStruct((t_len, D), F32), jax.ShapeDtypeStruct((t_len, D), BF16),
                   jax.ShapeDtypeStruct((t_len, D), BF16), jax.ShapeDtypeStruct((8, D), F32)],
        scratch_shapes=[pltpu.VMEM((tm, D), F32), pltpu.VMEM((tm, D), BF16), pltpu.VMEM((2, tm, FB), F32)],
        compiler_params=_cp("arbitrary"),
    )(dx2, gu, x1, mod, g_ffn, w_gu, w_down, w_out)


def _my_pos():
    return lax.axis_index("x"), lax.axis_index("y"), lax.axis_index("c")


def _my_index():
    x, y, c = _my_pos()
    return 4 * x + 2 * y + c


def _device_of(b):
    return (b >> 2) & 1, (b >> 1) & 1, b & 1


def _rs_send(src, parts_ref, b, send_sems, recv_sems, local_sem):
    me = _my_index()
    dst = parts_ref.at[me]

    @pl.when(b == me)
    def _():
        pltpu.make_async_copy(src, dst, local_sem).start()

    @pl.when(b != me)
    def _():
        pltpu.make_async_remote_copy(src_ref=src, dst_ref=dst, send_sem=send_sems.at[b], recv_sem=recv_sems.at[me],
                                     device_id=_device_of(b), device_id_type=MESH).start()


def _rs_finish(src_of, parts_ref, send_sems, recv_sems, local_sem):
    me = _my_index()
    for s in range(NDEV):
        @pl.when(s != me)
        def _():
            cp = pltpu.make_async_remote_copy(src_ref=src_of(s), dst_ref=parts_ref.at[s], send_sem=send_sems.at[s],
                                              recv_sem=recv_sems.at[s], device_id=_device_of(s), device_id_type=MESH)
            cp.wait_send()
            cp.wait_recv()

        @pl.when(s == me)
        def _():
            pltpu.make_async_copy(src_of(s), parts_ref.at[s], local_sem).wait()


_RS_SEMS = [pltpu.SemaphoreType.DMA((NDEV,)), pltpu.SemaphoreType.DMA((NDEV,)), pltpu.SemaphoreType.DMA]
_ANY = pl.BlockSpec(memory_space=pl.ANY)


def _xor_order(me, n):
    return (me ^ (n - 1 - jnp.arange(n, dtype=jnp.int32))).astype(jnp.int32)


NCHIP = NDEV // 2


def _rs2_scratch(half_shape):
    blocks = lambda *lead: pltpu.VMEM(lead + tuple(half_shape), BF16)
    return [blocks(NCHIP, 2), blocks(NCHIP)] + [pltpu.SemaphoreType.DMA((NCHIP,))] * 4 + [pltpu.SemaphoreType.DMA]


def _rs2_to_sibling(q, rs):
    stage, from_sib, d_send, d_recv = rs[:4]
    x, y, c = _my_pos()
    pltpu.make_async_remote_copy(src_ref=stage.at[q, 1 - c], dst_ref=from_sib.at[q], send_sem=d_send.at[q],
                                 recv_sem=d_recv.at[q], device_id=(x, y, 1 - c), device_id_type=MESH).start()


def _rs2_forward(q, parts_ref, rs):
    stage, chip_sum, d_send, d_recv, i_send, i_recv, local_sem = rs
    x, y, c = _my_pos()
    my_chip = 2 * x + y
    pltpu.make_async_remote_copy(src_ref=stage.at[q, c], dst_ref=chip_sum.at[q], send_sem=d_send.at[q],
                                 recv_sem=d_recv.at[q], device_id=(x, y, 1 - c), device_id_type=MESH).wait_recv()
    chip_sum[q] = (stage[q, c].astype(F32) + chip_sum[q].astype(F32)).astype(BF16)

    @pl.when(q == my_chip)
    def _():
        pltpu.make_async_copy(chip_sum.at[q], parts_ref.at[my_chip], local_sem).start()

    @pl.when(q != my_chip)
    def _():
        pltpu.make_async_remote_copy(src_ref=chip_sum.at[q], dst_ref=parts_ref.at[my_chip], send_sem=i_send.at[q],
                                     recv_sem=i_recv.at[my_chip], device_id=((q >> 1) & 1, q & 1, c),
                                     device_id_type=MESH).start()


def _rs2_finish(parts_ref, rs):
    stage, chip_sum, d_send, d_recv, i_send, i_recv, local_sem = rs
    x, y, c = _my_pos()
    my_chip = 2 * x + y
    for q in range(NCHIP):
        pltpu.make_async_remote_copy(src_ref=stage.at[q, 1 - c], dst_ref=chip_sum.at[q], send_sem=d_send.at[q],
                                     recv_sem=d_recv.at[q], device_id=(x, y, 1 - c), device_id_type=MESH).wait_send()

        @pl.when(q != my_chip)
        def _():
            cp = pltpu.make_async_remote_copy(src_ref=chip_sum.at[q], dst_ref=parts_ref.at[q], send_sem=i_send.at[q],
                                              recv_sem=i_recv.at[q], device_id=((q >> 1) & 1, q & 1, c),
                                              device_id_type=MESH)
            cp.wait_send()
            cp.wait_recv()

        @pl.when(q == my_chip)
        def _():
            pltpu.make_async_copy(chip_sum.at[q], parts_ref.at[q], local_sem).wait()


def _gu_wgrad(h2, dgu, order):
    t_len = h2.shape[0]
    tk = min(TK, t_len)
    nk = t_len // tk

    def body(ord_ref, h_ref, d_ref, parts_ref, acc, *rs):
        p, k = pl.program_id(0), pl.program_id(1)

        @pl.when(k == 0)
        def _():
            acc[...] = jnp.zeros_like(acc)

        hb = h_ref[...]
        for half in range(2):
            acc[half] += _dot_tn(d_ref[0, half], hb)

        @pl.when(k == nk - 1)
        def _():
            q = ord_ref[p]
            rs[0][q] = acc[...].astype(BF16)
            _rs2_to_sibling(q, rs)

        @pl.when((k == min(1, nk - 1)) & (p > 0))
        def _():
            _rs2_forward(ord_ref[p - 1], parts_ref, rs)

        @pl.when((p == NCHIP - 1) & (k == nk - 1))
        def _():
            _rs2_forward(ord_ref[p], parts_ref, rs)
            _rs2_finish(parts_ref, rs)

    return pl.pallas_call(
        body, name="gu_wgrad",
        grid_spec=pltpu.PrefetchScalarGridSpec(
            num_scalar_prefetch=1, grid=(NCHIP, nk),
            in_specs=[pl.BlockSpec((tk, D), lambda p, k, o: (k, 0)),
                      pl.BlockSpec((1, 2, tk, FB), lambda p, k, o: (o[p], 0, k, 0))],
            out_specs=_ANY,
            scratch_shapes=[pltpu.VMEM((2, FB, D), F32)] + _rs2_scratch((FB, D))),
        out_shape=jax.ShapeDtypeStruct((NCHIP, FB, D), BF16),
        compiler_params=_cp("arbitrary", "arbitrary"),
    )(order, h2, dgu.reshape(NCHIP, 2, t_len, FB))


def _scaled_wgrad(name, a, dx, w, gate_row, mod, order):
    t_len = dx.shape[0]
    kb = w.shape[0] // NCHIP
    tk = min(TK, t_len)
    nk = t_len // tk
    rows = kb // 2
    if a.ndim == 3:
        a_spec = pl.BlockSpec((None, tk, kb), lambda p, k, o: (o[p], k, 0))
    else:
        a_spec = pl.BlockSpec((tk, kb), lambda p, k, o: (k, o[p]))

    def body(ord_ref, a_ref, dx_ref, w_ref, mod_ref, parts_ref, dg_ref, acc, *rs):
        p, k = pl.program_id(0), pl.program_id(1)

        @pl.when((p == 0) & (k == 0))
        def _():
            dg_ref[...] = jnp.zeros_like(dg_ref)

        @pl.when(k == 0)
        def _():
            acc[...] = jnp.zeros_like(acc)

        acc[...] += _dot_tn(a_ref[...], dx_ref[...])

        @pl.when(k == nk - 1)
        def _():
            q = ord_ref[p]
            z = acc[...]
            zg = (z * mod_ref[gate_row:gate_row + 1, :]).astype(BF16)
            dg_ref[0:1, :] += jnp.sum(z * w_ref[...].astype(F32), axis=0, keepdims=True)
            for half in range(2):
                rs[0][q, half] = zg[half * rows:(half + 1) * rows]
            _rs2_to_sibling(q, rs)

        @pl.when((k == min(1, nk - 1)) & (p > 0))
        def _():
            _rs2_forward(ord_ref[p - 1], parts_ref, rs)

        @pl.when((p == NCHIP - 1) & (k == nk - 1))
        def _():
            _rs2_forward(ord_ref[p], parts_ref, rs)
            _rs2_finish(parts_ref, rs)

    return pl.pallas_call(
        body, name=name,
        grid_spec=pltpu.PrefetchScalarGridSpec(
            num_scalar_prefetch=1, grid=(NCHIP, nk),
            in_specs=[a_spec,
                      pl.BlockSpec((tk, D), lambda p, k, o: (k, 0)),
                      pl.BlockSpec((kb, D), lambda p, k, o: (o[p], 0)),
                      pl.BlockSpec((8, D), lambda p, k, o: (0, 0))],
            out_specs=[_ANY, pl.BlockSpec((8, D), lambda p, k, o: (0, 0))],
            scratch_shapes=[pltpu.VMEM((kb, D), F32)] + _rs2_scratch((rows, D))),
        out_shape=[jax.ShapeDtypeStruct((NCHIP, rows, D), BF16), jax.ShapeDtypeStruct((8, D), F32)],
        compiler_params=_cp("arbitrary", "arbitrary"),
    )(order, a, dx, w, mod)


M_WA, M_WB, M_CBIAS, M_BA, M_BX, M_LS = 0, 3, 7, 8, 9, 10


def _conv_bwd_rows(tt, proj_ref, prm_ref, due, dye, dp_ref, acc8):
    row = lax.broadcasted_iota(jnp.int32, (8, CG), 0)
    w_b = [prm_ref[P_WB + k:P_WB + k + 1, :] for k in range(4)]
    w_a = [prm_ref[P_WA + k:P_WA + k + 1, :] for k in range(3)]

    def blk(ib, carry):
        r0 = pl.multiple_of(ib * 16, 16)
        rows = pl.ds(r0, 16)
        for g in range(D // CG):
            cs = slice(g * CG, (g + 1) * CG)
            du16, du_after = due[rows, cs], due[pl.ds(r0 + 16, 8), cs]
            dy16, dy_after = dye[rows, cs], dye[pl.ds(r0 + 16, 8), cs]
            cc16 = _pj(proj_ref, 1, rows, cs).astype(F32)
            cx16 = _pj(proj_ref, 2, rows, cs).astype(F32)
            x16 = _pj(proj_ref, 3, rows, cs).astype(F32)
            v16 = cc16 * cx16
            acc = [acc8[8 * k:8 * k + 8, cs] for k in range(8)]
            drx, dv = [], []
            for sb in range(2):
                lo = slice(8 * sb, 8 * sb + 8)
                duc, dyc, xc, vc = du16[lo], dy16[lo], x16[lo], v16[lo]
                du_n = du16[8:16] if sb == 0 else du_after
                dy_n = dy16[8:16] if sb == 0 else dy_after
                acc[0] = acc[0] + duc
                acc[4] = acc[4] + duc * xc
                d8 = w_b[3][:, cs] * duc
                for s in (1, 2, 3):
                    du_s = _shift_up(duc, du_n, s, row)
                    acc[4 - s] = acc[4 - s] + du_s * xc
                    d8 = d8 + w_b[3 - s][:, cs] * du_s
                acc[7] = acc[7] + dyc * vc
                e8 = w_a[2][:, cs] * dyc
                for s in (1, 2):
                    dy_s = _shift_up(dyc, dy_n, s, row)
                    acc[7 - s] = acc[7 - s] + dy_s * vc
                    e8 = e8 + w_a[2 - s][:, cs] * dy_s
                drx.append(d8)
                dv.append(e8)
            for k in range(8):
                acc8[8 * k:8 * k + 8, cs] = acc[k]
            dv16 = jnp.concatenate(dv, axis=0)
            col = lambda s: slice(s * D + g * CG, s * D + (g + 1) * CG)
            dp_ref[rows, col(3)] = jnp.concatenate(drx, axis=0).astype(BF16)
            dp_ref[rows, col(1)] = (dv16 * cx16).astype(BF16)
            dp_ref[rows, col(2)] = (dv16 * cc16).astype(BF16)
        return carry

    lax.fori_loop(0, tt // 16, blk, 0)


def _mixer_bwd(proj, hl, sv, dmg, prm, wa, wx):
    t_len = proj.shape[0]
    tt = min(TT, t_len)
    nt = t_len // tt
    hb8 = tt // 8

    def rev(i):
        return nt - 1 - i

    def halo(i):
        return jnp.maximum(rev(i) * hb8 - 1, 0)

    def body(proj_ref, hl_ref, hh_ref, svh_ref, svf_ref, dmg_ref, prm_ref, wa_ref, wx_ref,
             dp_ref, sums_ref, gwa_ref, gwx_ref,
             he, due, dye, drp_s, dip_s, an, gn, acc8):
        i = pl.program_id(0)
        t = rev(i)

        @pl.when(i == 0)
        def _():
            sums_ref[...] = jnp.zeros_like(sums_ref)
            gwa_ref[...] = jnp.zeros_like(gwa_ref)
            gwx_ref[...] = jnp.zeros_like(gwx_ref)
            due[tt:tt + 8, :] = jnp.zeros((8, D), F32)
            dye[tt:tt + 8, :] = jnp.zeros((8, D), F32)
            an[...] = jnp.zeros((8, D), F32)
            gn[...] = jnp.zeros((8, D), F32)

        live = (t > 0).astype(F32)
        he[0:8, :] = hh_ref[...].astype(F32) * live
        he[8:8 + tt, :] = hl_ref[...].astype(F32)

        ls_all = _log_sigmoid(prm_ref[P_LAM:P_LAM + 1, :])
        row = lax.broadcasted_iota(jnp.int32, (8, CG), 0)
        nblk = tt // 16

        def blk(ib, carry):
            r0 = pl.multiple_of((nblk - 1 - ib) * 16, 16)
            rows = pl.ds(r0, 16)
            for g in range(D // CG):
                cs = slice(g * CG, (g + 1) * CG)
                ls = ls_all[:, cs]
                dm = dmg_ref[rows, cs].astype(F32)
                cb = _pj(proj_ref, 0, rows, cs).astype(F32)
                rg = _pj(proj_ref, 4, rows, cs).astype(F32)
                sga = _sig(_pj(proj_ref, 5, rows, cs).astype(F32))
                sgb = _sig(_pj(proj_ref, 6, rows, cs).astype(F32))
                ya0 = svh_ref[H_YA, rows, cs].astype(F32)
                h16 = he[pl.ds(r0 + 8, 16), cs]
                gl, th = _gelu(rg)
                dgl = 0.5 * (1.0 + th) + 0.5 * rg * (1.0 - th * th) * (_GC * (1.0 + 3.0 * 0.044715 * rg * rg))
                y_a = cb * ya0
                y_b = h16 * gl
                dy_a = dm * sga
                dy_b = dm * sgb
                col = lambda s: slice(s * D + g * CG, s * D + (g + 1) * CG)
                dp_ref[rows, col(5)] = (dm * y_a * sga * (1.0 - sga)).astype(BF16)
                dp_ref[rows, col(6)] = (dm * y_b * sgb * (1.0 - sgb)).astype(BF16)
                dp_ref[rows, col(4)] = (dy_b * h16 * dgl).astype(BF16)
                dp_ref[rows, col(0)] = (dy_a * ya0).astype(BF16)
                dye[rows, cs] = dy_a * cb
                dh16 = dy_b * gl

                a_next = an[:, cs]
                g_next = gn[:, cs]
                s_ba = jnp.zeros((8, CG), F32)
                s_bx = jnp.zeros((8, CG), F32)
                s_ls = jnp.zeros((8, CG), F32)
                u16, r16, i16 = (svh_ref[pln, rows, cs].astype(F32) for pln in (H_U, H_R, H_I))
                for sb in (1, 0):
                    rr = r0 + 8 * sb
                    first = (row + (t * tt + rr)) == 0
                    r8 = pl.ds(rr, 8)
                    lo = slice(8 * sb, 8 * sb + 8)
                    uu, r, ig = u16[lo], r16[lo], i16[lo]
                    a, mult = svf_ref[F_A, r8, cs], svf_ref[F_MULT, r8, cs]
                    ca = jnp.where(row < 7, pltpu.roll(a, 7, 0), a_next)
                    cb_ = dh16[8 * sb:8 * sb + 8, :]
                    for s in (1, 2, 4):
                        a_sh = jnp.where(row < 8 - s, pltpu.roll(ca, 8 - s, 0), 1.0)
                        b_sh = jnp.where(row < 8 - s, pltpu.roll(cb_, 8 - s, 0), 0.0)
                        cb_ = ca * b_sh + cb_
                        ca = ca * a_sh
                    gv = ca * g_next + cb_
                    g_next = jnp.broadcast_to(gv[0:1, :], gv.shape)
                    a_next = jnp.broadcast_to(a[0:1, :], a.shape)
                    hprev = jnp.where(row >= 1, pltpu.roll(he[pl.ds(rr + 8, 8), cs], 1, 0),
                                      pltpu.roll(he[pl.ds(rr, 8), cs], 1, 0))
                    da = gv * hprev
                    dmult = jnp.where(first, 0.0, gv * ig * uu)
                    dla = da * a + jnp.where(mult > 0.0, dmult * (-(a * a) / mult), 0.0)
                    drp = dla * (LRU_C * ls) * r * (1.0 - r)
                    dip = gv * mult * uu * ig * (1.0 - ig)
                    s_ls = s_ls + dla * (LRU_C * r)
                    s_ba = s_ba + drp
                    s_bx = s_bx + dip
                    drp_s[pl.ds(rr, 8), cs] = drp
                    dip_s[pl.ds(rr, 8), cs] = dip
                    due[pl.ds(rr, 8), cs] = gv * mult * ig
                an[:, cs] = a_next
                gn[:, cs] = g_next
                sums_ref[M_BA:M_BA + 1, cs] += jnp.sum(s_ba, axis=0, keepdims=True)
                sums_ref[M_BX:M_BX + 1, cs] += jnp.sum(s_bx, axis=0, keepdims=True)
                sums_ref[M_LS:M_LS + 1, cs] += jnp.sum(s_ls, axis=0, keepdims=True)
            return carry

        lax.fori_loop(0, nblk, blk, 0)

        drp_b = drp_s[...].astype(BF16)
        dip_b = dip_s[...].astype(BF16)
        ub = svh_ref[H_U]
        for h in range(HEADS):
            cs = slice(h * HB, (h + 1) * HB)
            due[0:tt, cs] += _dot_nt(drp_b[:, cs], wa_ref[h]) + _dot_nt(dip_b[:, cs], wx_ref[h])
            gwa_ref[h] += _dot_tn(ub[:, cs], drp_b[:, cs])
            gwx_ref[h] += _dot_tn(ub[:, cs], dip_b[:, cs])

        acc8[...] = jnp.zeros_like(acc8)
        _conv_bwd_rows(tt, proj_ref, prm_ref, due, dye, dp_ref, acc8)
        for k, dst in enumerate([M_CBIAS] + [M_WB + k for k in range(4)] + [M_WA + k for k in range(3)]):
            sums_ref[dst:dst + 1, :] += jnp.sum(acc8[8 * k:8 * k + 8, :], axis=0, keepdims=True)
        due[tt:tt + 8, :] = due[0:8, :]
        dye[tt:tt + 8, :] = dye[0:8, :]

        @pl.when(i == nt - 1)
        def _():
            sums_ref[M_LS:M_LS + 1, :] = sums_ref[M_LS:M_LS + 1, :] * _sig(-prm_ref[P_LAM:P_LAM + 1, :])

    big = lambda: pltpu.VMEM((tt + 8, D), F32)
    tile = lambda: pltpu.VMEM((tt, D), F32)
    return pl.pallas_call(
        body, name="mixer_bwd", grid=(nt,),
        in_specs=[pl.BlockSpec((tt, 7 * D), lambda i: (rev(i), 0)),
                  pl.BlockSpec((tt, D), lambda i: (rev(i), 0)),
                  pl.BlockSpec((8, D), lambda i: (halo(i), 0)),
                  pl.BlockSpec((len(SVH_PLANES), tt, D), lambda i: (0, rev(i), 0)),
                  pl.BlockSpec((len(SVF_PLANES), tt, D), lambda i: (0, rev(i), 0)),
                  pl.BlockSpec((tt, D), lambda i: (rev(i), 0)),
                  pl.BlockSpec((16, D), lambda i: (0, 0)),
                  pl.BlockSpec((HEADS, HB, HB), lambda i: (0, 0, 0)),
                  pl.BlockSpec((HEADS, HB, HB), lambda i: (0, 0, 0))],
        out_specs=[pl.BlockSpec((tt, 7 * D), lambda i: (rev(i), 0)),
                   pl.BlockSpec((16, D), lambda i: (0, 0)),
                   pl.BlockSpec((HEADS, HB, HB), lambda i: (0, 0, 0)),
                   pl.BlockSpec((HEADS, HB, HB), lambda i: (0, 0, 0))],
        out_shape=[jax.ShapeDtypeStruct((t_len, 7 * D), BF16), jax.ShapeDtypeStruct((16, D), F32),
                   jax.ShapeDtypeStruct((HEADS, HB, HB), F32), jax.ShapeDtypeStruct((HEADS, HB, HB), F32)],
        scratch_shapes=[big(), big(), big(), tile(), tile(),
                        pltpu.VMEM((8, D), F32), pltpu.VMEM((8, D), F32), pltpu.VMEM((64, D), F32)],
        compiler_params=_cp("arbitrary"),
    )(proj, hl, hl, sv[0], sv[1], dmg, prm, wa, wx)


def _in_proj_bwd(dproj, w_in, x, dx1, mod, g_mix):
    t_len = x.shape[0]
    tm = min(TM, t_len)

    def body(dp_ref, w_ref, x_ref, dx1_ref, mod_ref, g_ref, gx_ref, sums_ref, acc):
        @pl.when(pl.program_id(0) == 0)
        def _():
            sums_ref[...] = jnp.zeros_like(sums_ref)

        def write(rows, dx):
            gx_ref[rows, :] = dx

        zero = jnp.zeros((8, D), F32)
        sums = (zero, zero, zero)
        for sub in _sub_blocks(tm):
            acc[sub, :] = _dot_nt(dp_ref[sub, :], w_ref[...])
            sums = _norm_bwd_rows(sub, sums, acc, x_ref, dx1_ref, mod_ref[1:2, :], g_ref[...], write)
        _add_norm_sums(sums_ref, sums)

    return pl.pallas_call(
        body, name="in_proj_bwd", grid=(t_len // tm,),
        in_specs=[pl.BlockSpec((tm, 7 * D), lambda i: (i, 0)),
                  _resident((D, 7 * D)),
                  pl.BlockSpec((tm, D), lambda i: (i, 0)), pl.BlockSpec((tm, D), lambda i: (i, 0)),
                  pl.BlockSpec((8, D), lambda i: (0, 0)), pl.BlockSpec((1, D), lambda i: (0, 0))],
        out_specs=[pl.BlockSpec((tm, D), lambda i: (i, 0)), pl.BlockSpec((8, D), lambda i: (0, 0))],
        out_shape=[jax.ShapeDtypeStruct((t_len, D), F32), jax.ShapeDtypeStruct((8, D), F32)],
        scratch_shapes=[pltpu.VMEM((tm, D), F32)],
        compiler_params=_cp("arbitrary"),
    )(dproj, w_in, x, dx1, mod, g_mix)


def _in_wgrad(h, dproj, g_wa, g_wx, order):
    t_len = h.shape[0]
    tk = min(TKI, t_len)
    nk = t_len // tk
    cw = 7 * D // NDEV
    hr = HB // NDEV

    def body(ord_ref, h_ref, d_ref, ga_ref, gx_ref, parts_ref, pa_ref, px_ref, acc, *scr):
        rs, sems = scr[:-6], scr[-6:]
        p, k = pl.program_id(0), pl.program_id(1)

        def head_rows(ref):
            return lambda s: ref.at[:, pl.ds(s * hr, hr), :]

        @pl.when((p == 0) & (k == 0))
        def _():
            for s in range(NDEV):
                _rs_send(head_rows(ga_ref)(s), pa_ref, s, *sems[0:3])
                _rs_send(head_rows(gx_ref)(s), px_ref, s, *sems[3:6])

        @pl.when(k == 0)
        def _():
            acc[...] = jnp.zeros_like(acc)

        acc[...] += _dot_tn(h_ref[...], d_ref[...])

        @pl.when(k == nk - 1)
        def _():
            q = ord_ref[p]
            for half in range(2):
                rs[0][q, half] = acc[:, half * cw:(half + 1) * cw].astype(BF16)
            _rs2_to_sibling(q, rs)

        @pl.when((k == min(1, nk - 1)) & (p > 0))
        def _():
            _rs2_forward(ord_ref[p - 1], parts_ref, rs)

        @pl.when((p == NCHIP - 1) & (k == nk - 1))
        def _():
            _rs2_forward(ord_ref[p], parts_ref, rs)
            _rs2_finish(parts_ref, rs)
            _rs_finish(head_rows(ga_ref), pa_ref, *sems[0:3])
            _rs_finish(head_rows(gx_ref), px_ref, *sems[3:6])

    return pl.pallas_call(
        body, name="in_wgrad",
        grid_spec=pltpu.PrefetchScalarGridSpec(
            num_scalar_prefetch=1, grid=(NCHIP, nk),
            in_specs=[pl.BlockSpec((tk, D), lambda p, k, o: (k, 0)),
                      pl.BlockSpec((tk, 2 * cw), lambda p, k, o: (k, o[p])), _ANY, _ANY],
            out_specs=[_ANY, _ANY, _ANY],
            scratch_shapes=[pltpu.VMEM((D, 2 * cw), F32)] + _rs2_scratch((D, cw)) + _RS_SEMS * 2),
        out_shape=[jax.ShapeDtypeStruct((NCHIP, D, cw), BF16), jax.ShapeDtypeStruct((NDEV, HEADS, hr, HB), F32),
                   jax.ShapeDtypeStruct((NDEV, HEADS, hr, HB), F32)],
        compiler_params=_cp("arbitrary", "arbitrary"),
    )(order, h, dproj, g_wa, g_wx)


def _adam_math(w, g, m, v):
    m = ADAM_B1 * m + (1.0 - ADAM_B1) * g
    v = ADAM_B2 * v + (1.0 - ADAM_B2) * (g * g)
    m_hat = m / (1.0 - ADAM_B1 ** ADAM_STEP)
    v_hat = v / (1.0 - ADAM_B2 ** ADAM_STEP)
    delta = -ADAM_LR * (m_hat / (jnp.sqrt(v_hat) + ADAM_EPS) + ADAM_WD * w)
    return delta, m, v


def _ada_bwd(c_all, dmod_cols, w, m, v):
    rb = 256
    n = w.shape[1]
    nrow = c_all.shape[0]

    def body(c_ref, d_ref, w_ref, m_ref, v_ref, g_ref, dl_ref, nm_ref, nv_ref):
        cv = c_ref[...]
        g = _dot_tn((cv * _sig(cv)).astype(BF16), d_ref[...].astype(BF16))
        g_ref[...] = g
        dl_ref[...], nm_ref[...], nv_ref[...] = _adam_math(w_ref[...], g, m_ref[...], v_ref[...])

    blk = pl.BlockSpec((rb, n), lambda i: (i, 0))
    sds = jax.ShapeDtypeStruct(w.shape, F32)
    return pl.pallas_call(
        body, name="ada_bwd", grid=(D // rb,),
        in_specs=[pl.BlockSpec((nrow, rb), lambda i: (0, i)), pl.BlockSpec((nrow, n), lambda i: (0, 0)), blk, blk, blk],
        out_specs=[blk, blk, blk, blk], out_shape=[sds, sds, sds, sds],
        compiler_params=_cp("parallel"),
    )(c_all, dmod_cols, w, m, v)


def _adam(name, parts, w, m, v):
    p, r, c = parts.shape
    rb = r
    for cand in (256, 128, 64, 32, 16, 8):
        if r % cand == 0 and r >= cand:
            rb = cand
            break

    def body(p_ref, w_ref, m_ref, v_ref, g_ref, dl_ref, nm_ref, nv_ref):
        g = p_ref[0].astype(F32)
        for q in range(1, p):
            g = g + p_ref[q].astype(F32)
        g_ref[...] = g
        dl_ref[...], nm_ref[...], nv_ref[...] = _adam_math(w_ref[...], g, m_ref[...], v_ref[...])

    blk = pl.BlockSpec((rb, c), lambda i: (i, 0))
    sds = jax.ShapeDtypeStruct((r, c), F32)
    return pl.pallas_call(
        body, name=name, grid=(r // rb,),
        in_specs=[pl.BlockSpec((p, rb, c), lambda i: (0, i, 0)), blk, blk, blk],
        out_specs=[blk, blk, blk, blk], out_shape=[sds, sds, sds, sds],
        compiler_params=_cp("parallel"),
    )(parts, w, m, v)


_SMALL_SEMS = [pltpu.SemaphoreType.DMA((7,)), pltpu.SemaphoreType.DMA((7,)), pltpu.SemaphoreType.DMA]
_VMEM = pl.BlockSpec(memory_space=pltpu.VMEM)


def _exchange_small(x_ref, out_ref, send_sems, recv_sems, local_sem):
    m_per = x_ref.shape[0]
    x, y, c = _my_pos()
    me, sibling = (x, y, c), (x, y, 1 - c)
    chips = [(1 - x, y), (x, 1 - y), (1 - x, 1 - y)]

    def rows(px, py, pc):
        return out_ref.at[pl.ds((4 * px + 2 * py + pc) * m_per, m_per), :]

    def copy(k, block, to, src=None):
        return pltpu.make_async_remote_copy(
            src_ref=rows(*block) if src is None else src, dst_ref=rows(*block),
            send_sem=send_sems.at[k], recv_sem=recv_sems.at[k], device_id=to, device_id_type=MESH)

    mine = pltpu.make_async_copy(x_ref, rows(*me), local_sem)
    mine.start()
    first = [copy(0, me, sibling, src=x_ref)]
    first += [copy(1 + j, me, (*chip, c), src=x_ref) for j, chip in enumerate(chips)]
    for cp in first:
        cp.start()
    passed = [copy(4 + j, (*chip, c), sibling) for j, chip in enumerate(chips)]
    for j, chip in enumerate(chips):
        copy(1 + j, (*chip, c), me).wait_recv()
        passed[j].start()
    copy(0, sibling, me).wait_recv()
    for j, chip in enumerate(chips):
        copy(4 + j, (*chip, 1 - c), me).wait_recv()
    for cp in first + passed:
        cp.wait_send()
    mine.wait()


def _gather_small_grads(sums1, sums2, msums, d_gt1, d_gt2, d_gfin, loss):
    def body(s1, s2, ms, g1, g2, gf, ls, out_ref, pack, *sems):
        rows = [s1[S_SH:S_SH + 1, :], s1[S_SC:S_SC + 1, :], g1[0:1, :],
                s2[S_SH:S_SH + 1, :], s2[S_SC:S_SC + 1, :], g2[0:1, :],
                s1[S_G:S_G + 1, :], ms[M_CBIAS:M_CBIAS + 1, :], ms[M_BA:M_BA + 1, :], ms[M_BX:M_BX + 1, :],
                ms[M_LS:M_LS + 1, :], s2[S_G:S_G + 1, :], gf[...]]
        rows += [ms[M_WA + k:M_WA + k + 1, :] for k in range(3)] + [ms[M_WB + k:M_WB + k + 1, :] for k in range(4)]
        rows += [jnp.broadcast_to(ls[0:1, 0:1], (1, D))]
        for i, v in enumerate(rows):
            pack[i:i + 1, :] = v
        pack[len(rows):24, :] = jnp.zeros((24 - len(rows), D), F32)
        _exchange_small(pack, out_ref, *sems)

    return pl.pallas_call(
        body, name="gather_small", out_shape=jax.ShapeDtypeStruct((NDEV * 24, D), F32),
        in_specs=[_VMEM] * 7, out_specs=_VMEM, scratch_shapes=[pltpu.VMEM((24, D), F32)] + _SMALL_SEMS,
    )(sums1, sums2, msums, d_gt1, d_gt2, d_gfin, loss)


def _ada_mod(pack, w_ada, b_cols):
    ncol = w_ada.shape[1]

    def body(p_ref, w_ref, b_ref, all_ref, mod_ref, cols, *sems):
        _exchange_small(p_ref, all_ref, *sems[0:3])
        c_all = jnp.concatenate([all_ref[8 * d:8 * d + 1, 0:D] for d in range(NDEV)], axis=0)
        c16 = jnp.concatenate([c_all, jnp.zeros_like(c_all)], axis=0)
        mod16 = _dot((c16 * _sig(c16)).astype(BF16), w_ref[...].astype(BF16)) + b_ref[...]
        cols[...] = mod16[0:NDEV]
        _exchange_small(cols, mod_ref, *sems[3:6])

    return pl.pallas_call(
        body, name="ada_mod",
        out_shape=[jax.ShapeDtypeStruct((NDEV * 8, pack.shape[1]), F32), jax.ShapeDtypeStruct((NDEV * 8, ncol), F32)],
        in_specs=[_VMEM, _VMEM, _VMEM], out_specs=[_VMEM, _VMEM],
        scratch_shapes=[pltpu.VMEM((NDEV, ncol), F32)] + _SMALL_SEMS * 2,
        compiler_params=_cp(),
    )(pack, w_ada, b_cols)


def _blk_rows(n):
    return lambda ref, b: ref.at[pl.ds(pl.multiple_of(b * n, 8), n), :]


def _blk_lead(ref, b):
    return ref.at[b]


def _blk_heads(ref, b):
    return ref.at[:, pl.ds(pl.multiple_of(b * (HB // NDEV), 8), HB // NDEV), :]


def _ag_phases(ins, outs, slicers, send_sems, recv_sems, local_sems):
    na = len(ins)
    x, y, c = _my_pos()
    me, sibling = (x, y, c), (x, y, 1 - c)
    chips = [(1 - x, y), (x, 1 - y), (1 - x, 1 - y)]

    def copy(a, k, block, to, from_shard=False):
        px, py, pc = block
        dst = slicers[a](outs[a], 4 * px + 2 * py + pc)
        return pltpu.make_async_remote_copy(
            src_ref=ins[a] if from_shard else dst, dst_ref=dst,
            send_sem=send_sems.at[a * 7 + k], recv_sem=recv_sems.at[a * 7 + k], device_id=to, device_id_type=MESH)

    def local(a):
        return pltpu.make_async_copy(ins[a], slicers[a](outs[a], 4 * x + 2 * y + c), local_sems.at[a])

    def firsts(a):
        return [copy(a, 0, me, sibling, True)] + [copy(a, 1 + j, me, (*chip, c), True) for j, chip in enumerate(chips)]

    def start():
        for a in range(na):
            local(a).start()
            for cp in firsts(a):
                cp.start()

    def forward():
        for a in range(na):
            for j, chip in enumerate(chips):
                copy(a, 1 + j, (*chip, c), me).wait_recv()
                copy(a, 4 + j, (*chip, c), sibling).start()

    def finish():
        for a in range(na):
            copy(a, 0, sibling, me).wait_recv()
            for j, chip in enumerate(chips):
                copy(a, 4 + j, (*chip, 1 - c), me).wait_recv()
        for a in range(na):
            for cp in firsts(a) + [copy(a, 4 + j, (*chip, c), sibling) for j, chip in enumerate(chips)]:
                cp.wait_send()
            local(a).wait()

    return start, forward, finish


def _ag_sems(na):
    return [pltpu.SemaphoreType.DMA((7 * na,)), pltpu.SemaphoreType.DMA((7 * na,)), pltpu.SemaphoreType.DMA((na,))]


def _local_step(x, target, mod, g_mix, g_ffn, g_fin, prm, w_in_shard, shards):
    fulls = [(HEADS, HB, HB), (HEADS, HB, HB), (D, D), (NDEV, FB, D), (DFF, D)]
    slicers = [_blk_heads, _blk_heads, _blk_rows(D // NDEV), _blk_lead, _blk_rows(DFF // NDEV)]
    my_chip = _my_index() >> 1
    own_first = (my_chip ^ jnp.arange(NCHIP, dtype=jnp.int32)).astype(jnp.int32)
    early, late = [0, 1, 2, 4], [3]
    pick = lambda lst, idx: [lst[i] for i in idx]
    proj, h, w_in, (wa, wx, w_out, w_down) = _in_proj(x, mod, g_mix, w_in_shard, own_first, pick(shards, early),
                                                      pick(fulls, early), pick(slicers, early))
    merged, hl, sv, (w_gu,) = _mixer_fwd(proj, prm, wa, wx, pick(shards, late), pick(fulls, late),
                                         pick(slicers, late))
    w_gu = w_gu.reshape(2, 4, FB, D)
    x1, h2 = _out_proj(merged, x, mod, g_ffn, w_out)
    gu, dx2, dx2b, loss, d_gfin = _ffn_fwd(h2, x1, target, mod, g_fin, w_gu, w_down)
    dgu, act, dx1, dx1b, dmg, sums2 = _ffn_bwd(dx2, gu, x1, mod, g_ffn, w_gu, w_down, w_out)
    chip_order = _xor_order(_my_index() >> 1, NCHIP)
    p_wgu = _gu_wgrad(h2, dgu, chip_order)
    p_wdown, d_gt2 = _scaled_wgrad("down_wgrad", act, dx2b, w_down, 5, mod, chip_order)
    p_wout, d_gt1 = _scaled_wgrad("out_wgrad", merged, dx1b, w_out, 2, mod, chip_order)
    dproj, msums, g_wa, g_wx = _mixer_bwd(proj, hl, sv, dmg, prm, wa, wx)
    p_win, p_wa, p_wx = _in_wgrad(h, dproj, g_wa, g_wx, chip_order)
    grad_x, sums1 = _in_proj_bwd(dproj, w_in, x, dx1, mod, g_mix)
    return dict(loss=loss, grad_x=grad_x, d_gfin=d_gfin, sums1=sums1, sums2=sums2, msums=msums,
                d_gt1=d_gt1, d_gt2=d_gt2, p_win=p_win, p_wa=p_wa, p_wx=p_wx, p_wout=p_wout, p_wgu=p_wgu,
                p_wdown=p_wdown)


def kernel(x, c, w_ada, b_ada, g_norm_mix, w_in, conv_a_w, conv_b_w, conv_b_bias, w_rg_a, b_rg_a, w_rg_x, b_rg_x, lru_lambda, w_out, g_norm_ffn, w_gate_up, w_down, g_norm_final, loss_target, m_w_ada, m_b_ada, m_g_norm_mix, m_w_in, m_conv_a_w, m_conv_b_w, m_conv_b_bias, m_w_rg_a, m_b_rg_a, m_w_rg_x, m_b_rg_x, m_lru_lambda, m_w_out, m_g_norm_ffn, m_w_gate_up, m_w_down, m_g_norm_final, v_w_ada, v_b_ada, v_g_norm_mix, v_w_in, v_conv_a_w, v_conv_b_w, v_conv_b_bias, v_w_rg_a, v_b_rg_a, v_w_rg_x, v_b_rg_x, v_lru_lambda, v_w_out, v_g_norm_ffn, v_w_gate_up, v_w_down, v_g_norm_final):
    me = 4 * lax.axis_index("x") + 2 * lax.axis_index("y") + lax.axis_index("c")
    ncol = w_ada.shape[2]
    cw = conv_a_w.shape[2]

    pack0 = jnp.concatenate([c, conv_a_w.reshape(1, 3 * cw), conv_b_w.reshape(1, 4 * cw)], axis=1)
    b_cols = lax.dynamic_slice_in_dim(b_ada, me * ncol, ncol, axis=1)
    got0, got1 = _ada_mod(jnp.broadcast_to(pack0, (8, pack0.shape[1])), w_ada[0], b_cols)
    got0 = got0.reshape(NDEV, 8, -1)[:, 0, :]
    c_all = got0[:, :D]
    conv_a = got0[:, D:D + 3 * cw].reshape(NDEV, 3, cw).transpose(1, 0, 2).reshape(3, D)
    conv_b = got0[:, D + 3 * cw:].reshape(NDEV, 4, cw).transpose(1, 0, 2).reshape(4, D)
    c16 = jnp.concatenate([c_all, jnp.zeros((8, D), F32)], axis=0)
    mod6 = lax.dynamic_index_in_dim(got1.reshape(NDEV, NDEV, ncol), me, axis=1, keepdims=False).reshape(6, D)
    mod = jnp.concatenate([mod6, jnp.zeros((2, D), F32)], axis=0)

    tr = lambda a: jnp.swapaxes(a, 1, 2)
    shards = [w_rg_a[0].astype(BF16), w_rg_x[0].astype(BF16), w_out[0].astype(BF16), tr(w_gate_up)[0].astype(BF16),
              w_down[0].astype(BF16)]

    prm = jnp.concatenate([conv_a, conv_b, conv_b_bias, b_rg_a, b_rg_x, lru_lambda, jnp.zeros((5, D), F32)], axis=0)
    r = _local_step(x[0], loss_target[0], mod, g_norm_mix, g_norm_ffn, g_norm_final.reshape(1, D), prm,
                    w_in[0].astype(BF16), shards)

    parts = [r["p_win"], r["p_wa"], r["p_wx"], r["p_wout"], r["p_wgu"], r["p_wdown"]]
    big = {}
    for nm, p, w, m, v in (("w_in", parts[0], w_in, m_w_in, v_w_in), ("w_rg_a", parts[1], w_rg_a, m_w_rg_a, v_w_rg_a),
                           ("w_rg_x", parts[2], w_rg_x, m_w_rg_x, v_w_rg_x), ("w_out", parts[3], w_out, m_w_out, v_w_out),
                           ("w_gate_up", parts[4], tr(w_gate_up), tr(m_w_gate_up), tr(v_w_gate_up)),
                           ("w_down", parts[5], w_down, m_w_down, v_w_down)):
        two_d = (-1, w.shape[-1])
        outs = _adam("adam_" + nm, p.reshape((p.shape[0],) + w.reshape(two_d).shape), w.reshape(two_d), m.reshape(two_d),
                     v.reshape(two_d))
        big[nm] = [o.reshape(w.shape) for o in outs]
    big["w_gate_up"] = [tr(o) for o in big["w_gate_up"]]

    got2 = _gather_small_grads(r["sums1"], r["sums2"], r["msums"], r["d_gt1"], r["d_gt2"], r["d_gfin"],
                               r["loss"]).reshape(NDEV, 24, D)

    rep_w = jnp.concatenate([b_ada.reshape(6, D), g_norm_mix, conv_b_bias, b_rg_a, b_rg_x, lru_lambda, g_norm_ffn,
                             g_norm_final.reshape(1, D), jnp.zeros((3, D), F32)], axis=0)
    rep_m = jnp.concatenate([m_b_ada.reshape(6, D), m_g_norm_mix, m_conv_b_bias, m_b_rg_a, m_b_rg_x, m_lru_lambda,
                             m_g_norm_ffn, m_g_norm_final.reshape(1, D), jnp.zeros((3, D), F32)], axis=0)
    rep_v = jnp.concatenate([v_b_ada.reshape(6, D), v_g_norm_mix, v_conv_b_bias, v_b_rg_a, v_b_rg_x, v_lru_lambda,
                             v_g_norm_ffn, v_g_norm_final.reshape(1, D), jnp.ones((3, D), F32)], axis=0)
    rep = _adam("adam_rep", got2[:, :16, :], rep_w, rep_m, rep_v)

    conv_parts = lax.dynamic_slice_in_dim(got2[:, 13:21, :], me * cw, cw, axis=2)
    cv_w = jnp.concatenate([conv_a_w[0], conv_b_w[0], jnp.zeros((1, cw), F32)], axis=0)
    cv_m = jnp.concatenate([m_conv_a_w[0], m_conv_b_w[0], jnp.zeros((1, cw), F32)], axis=0)
    cv_v = jnp.concatenate([v_conv_a_w[0], v_conv_b_w[0], jnp.ones((1, cw), F32)], axis=0)
    cvo = _adam("adam_conv", conv_parts, cv_w, cv_m, cv_v)

    dmod_cols = lax.dynamic_slice_in_dim(got2[:, :6, :].reshape(NDEV, 6 * D), me * ncol, ncol, axis=1)
    dmod16 = jnp.concatenate([dmod_cols, jnp.zeros((8, ncol), F32)], axis=0)
    ada = _ada_bwd(c16, dmod16, w_ada[0], m_w_ada[0], v_w_ada[0])

    loss = jnp.sum(got2[:, 20, 0])

    def pick(q):
        one = lambda i: rep[q][i:i + 1]
        return [ada[q].reshape(w_ada.shape), rep[q][0:6].reshape(b_ada.shape), one(6), big["w_in"][q],
                cvo[q][0:3].reshape(conv_a_w.shape), cvo[q][3:7].reshape(conv_b_w.shape), one(7),
                big["w_rg_a"][q], one(8), big["w_rg_x"][q], one(9), one(10), big["w_out"][q], one(11),
                big["w_gate_up"][q], big["w_down"][q], rep[q][12]]

    return (loss, r["grad_x"].reshape(x.shape), *pick(0), *pick(1), *pick(2), *pick(3))
```

```python
import math

import jax
import jax.numpy as jnp
from jax import lax
from jax.experimental import pallas as pl
from jax.experimental.pallas import tpu as pltpu

F32 = jnp.float32
BF16 = jnp.bfloat16

D = 1024
DFF = 2816
NDEV = 8
HEADS = 4
HB = D // HEADS
FB = DFF // 4
EPS = 1e-6
LRU_C = 8.0
ADAM_LR, ADAM_B1, ADAM_B2, ADAM_EPS, ADAM_WD, ADAM_STEP = 0.001, 0.9, 0.999, 1e-08, 0.01, 10

VMEM_LIMIT = 56 * 1024 * 1024
TM = 512
TMI = 1024
TMF = 256
TK = 2048
TKI = 2048
SUB = 256
TT = 256
CG = 256
MESH = pl.DeviceIdType.MESH


def _cp(*sem):
    return pltpu.CompilerParams(dimension_semantics=sem, vmem_limit_bytes=VMEM_LIMIT)


def _sig(x):
    return 1.0 / (1.0 + jnp.exp(-x))


def _log_sigmoid(x):
    z = jnp.exp(-jnp.abs(x))
    u = 1.0 + z
    d = u - 1.0
    l1p = jnp.where(d == 0.0, z, jnp.log(u) * (z / jnp.where(d == 0.0, 1.0, d)))
    return -(jnp.maximum(-x, 0.0) + l1p)


def _neg_expm1(x):
    p = x * (1.0 + x * 0.5 * (1.0 + x * (1.0 / 3.0) * (1.0 + x * 0.25 * (1.0 + x * 0.2 * (1.0 + x * (1.0 / 6.0))))))
    return jnp.where(x > -0.25, -p, 1.0 - jnp.exp(x))


_GC = math.sqrt(2.0 / math.pi)


def _gelu(x):
    t = jnp.tanh(_GC * (x + 0.044715 * x * x * x))
    return 0.5 * x * (1.0 + t), t


def _dot(a, b):
    return jnp.dot(a, b, preferred_element_type=F32)


def _dot_nt(a, b):
    return lax.dot_general(a, b, (((1,), (1,)), ((), ())), preferred_element_type=F32)


def _dot_tn(a, b):
    return lax.dot_general(a, b, (((0,), (0,)), ((), ())), preferred_element_type=F32)


def _resident(shape):
    return pl.BlockSpec(shape, lambda *_: (0,) * len(shape), pipeline_mode=pl.Buffered(1))


def _sub_blocks(n_rows):
    step = min(SUB, n_rows)
    return [slice(r, r + step) for r in range(0, n_rows, step)]


def _fold8(v):
    return v[0:8] + v[8:16]


def _pj(ref, s, rows=slice(None), cols=slice(0, D)):
    return ref[rows, s * D + cols.start:s * D + cols.stop]


def _in_proj(x, mod, g_mix, w_shard, order, shards, fulls, slicers):
    t_len = x.shape[0]
    tm = min(TMI, t_len)
    ni = t_len // tm
    na = len(shards)
    cw = 7 * D // NDEV
    rc = 32

    def body(ord_ref, x_ref, mod_ref, g_ref, wsh_ref, *rest):
        ins, (proj_ref, h_ref, wfull_ref), outs = rest[:na], rest[na:na + 3], rest[na + 3:2 * na + 3]
        h_scr, w_scr, wsend, wrecv, wlocal, wout = rest[2 * na + 3:2 * na + 9]
        start, forward, finish = _ag_phases(ins, outs, slicers, *rest[2 * na + 9:])
        p, i = pl.program_id(0), pl.program_id(1)
        x_, y_, c = _my_pos()
        me, sibling = (x_, y_, c), (x_, y_, 1 - c)
        chip_at = [None, (x_, 1 - y_), (1 - x_, y_), (1 - x_, 1 - y_)]

        def cols(px, py, pc):
            return w_scr.at[:, pl.ds(pl.multiple_of((4 * px + 2 * py + pc) * cw, 128), cw)]

        def wcopy(k, block, to, from_shard=False):
            dst = cols(*block)
            return pltpu.make_async_remote_copy(src_ref=wsh_ref if from_shard else dst, dst_ref=dst,
                                                send_sem=wsend.at[k], recv_sem=wrecv.at[k], device_id=to,
                                                device_id_type=MESH)

        own_local = pltpu.make_async_copy(wsh_ref, cols(*me), wlocal)
        to_hbm = pltpu.make_async_copy(w_scr, wfull_ref, wout)

        @pl.when((p == 0) & (i == 0))
        def _():
            own_local.start()
            wcopy(0, me, sibling, True).start()
            for q in (1, 2):
                wcopy(q, me, (*chip_at[q], c), True).start()
            own_local.wait()
            wcopy(0, sibling, me).wait_recv()

        @pl.when((p == 0) & (i == ni // 2))
        def _():
            wcopy(3, me, (*chip_at[3], c), True).start()

        for q in (1, 2, 3):
            @pl.when((p == q - 1) & (i == ni - 1))
            def _():
                wcopy(q, (*chip_at[q], c), me).wait_recv()
                wcopy(3 + q, (*chip_at[q], c), sibling).start()

            @pl.when((p == q) & (i == 0))
            def _():
                wcopy(3 + q, (*chip_at[q], 1 - c), me).wait_recv()

        @pl.when((p == 1) & (i == 0))
        def _():
            start()

        @pl.when((p == NCHIP - 1) & (i == ni // 2))
        def _():
            forward()

        @pl.when((p == NCHIP - 1) & (i == 0))
        def _():
            to_hbm.start()

        gs = g_ref[...] * (1.0 + mod_ref[1:2, :])
        sh = mod_ref[0:1, :]

        wcols = pl.ds(pl.multiple_of(ord_ref[p] * (2 * cw), 128), 2 * cw)
        for sub in _sub_blocks(tm):
            for r0 in range(sub.start, sub.stop, rc):
                xv = x_ref[r0:r0 + rc, :]
                r = lax.rsqrt(jnp.mean(xv * xv, axis=-1, keepdims=True) + EPS)
                h_scr[r0:r0 + rc, :] = (xv * r * gs + sh).astype(BF16)
            proj_ref[sub, :] = _dot(h_scr[sub, :], w_scr[:, wcols]).astype(BF16)

        @pl.when(p == 0)
        def _():
            h_ref[...] = h_scr[...]

        @pl.when((p == NCHIP - 1) & (i == ni - 1))
        def _():
            wcopy(0, me, sibling, True).wait_send()
            for q in (1, 2, 3):
                wcopy(q, me, (*chip_at[q], c), True).wait_send()
                wcopy(3 + q, (*chip_at[q], c), sibling).wait_send()
            finish()
            to_hbm.wait()

    res = pl.pallas_call(
        body, name="in_proj",
        grid_spec=pltpu.PrefetchScalarGridSpec(
            num_scalar_prefetch=1, grid=(NCHIP, ni),
            in_specs=[pl.BlockSpec((tm, D), lambda p, i, o: (i, 0)),
                      pl.BlockSpec((8, D), lambda p, i, o: (0, 0)),
                      pl.BlockSpec((1, D), lambda p, i, o: (0, 0))] + [_ANY] * (1 + na),
            out_specs=[pl.BlockSpec((tm, 2 * cw), lambda p, i, o: (i, o[p])),
                       pl.BlockSpec((tm, D), lambda p, i, o: (jnp.where(p == 0, i, ni - 1), 0))]
            + [_ANY] * (1 + na),
            scratch_shapes=[pltpu.VMEM((tm, D), BF16), pltpu.VMEM((D, 7 * D), BF16),
                            pltpu.SemaphoreType.DMA((7,)), pltpu.SemaphoreType.DMA((7,)),
                            pltpu.SemaphoreType.DMA, pltpu.SemaphoreType.DMA] + _ag_sems(na)),
        out_shape=[jax.ShapeDtypeStruct((t_len, 7 * D), BF16), jax.ShapeDtypeStruct((t_len, D), BF16),
                   jax.ShapeDtypeStruct((D, 7 * D), BF16)]
        + [jax.ShapeDtypeStruct(f, sh.dtype) for f, sh in zip(fulls, shards)],
        compiler_params=_cp("arbitrary", "arbitrary"),
    )(order, x, mod, g_mix, w_shard, *shards)
    return res[0], res[1], res[2], res[3:]


P_WA, P_WB, P_CBIAS, P_BA, P_BX, P_LAM = 0, 3, 7, 8, 9, 10
SV_PLANES = SV_U, SV_YA, SV_R, SV_I, SV_A, SV_MULT = range(6)


def _lru_gates(rp, ip, ls, first_row):
    r = _sig(rp)
    ig = _sig(ip)
    la = LRU_C * r * ls
    a = jnp.exp(la)
    m2 = _neg_expm1(2.0 * la)
    mult = jnp.where(first_row, 1.0, jnp.sqrt(jnp.maximum(m2, 0.0)))
    return r, ig, la, a, m2, mult


def _shift_down(cur, prev, s, row):
    return jnp.where(row >= s, pltpu.roll(cur, s, 0), pltpu.roll(prev, s, 0))


def _shift_up(cur, nxt, s, row):
    return jnp.where(row < 8 - s, pltpu.roll(cur, 8 - s, 0), pltpu.roll(nxt, 8 - s, 0))


def _conv_fwd_rows(tt, proj_ref, prm_ref, xe, ve, u_s, ub_s, ya_s):
    row = lax.broadcasted_iota(jnp.int32, (8, CG), 0)
    w_b = [prm_ref[P_WB + k:P_WB + k + 1, :] for k in range(4)]
    w_a = [prm_ref[P_WA + k:P_WA + k + 1, :] for k in range(3)]
    bias = prm_ref[P_CBIAS:P_CBIAS + 1, :]

    def blk(ib, carry):
        r0 = pl.multiple_of(ib * 16, 16)
        rows = pl.ds(r0, 16)
        for g in range(D // CG):
            cs = slice(g * CG, (g + 1) * CG)
            x16 = _pj(proj_ref, 3, rows, cs).astype(F32)
            v16 = _pj(proj_ref, 1, rows, cs).astype(F32) * _pj(proj_ref, 2, rows, cs).astype(F32)
            xp = xe[pl.ds(r0, 8), cs]
            vp = ve[pl.ds(r0, 8), cs]
            xe[pl.ds(r0 + 8, 16), cs] = x16
            ve[pl.ds(r0 + 8, 16), cs] = v16
            us, yas = [], []
            for sb in range(2):
                xc, vc = x16[8 * sb:8 * sb + 8], v16[8 * sb:8 * sb + 8]
                u8 = bias[:, cs] + w_b[3][:, cs] * xc
                for s in (1, 2, 3):
                    u8 = u8 + w_b[3 - s][:, cs] * _shift_down(xc, xp, s, row)
                y8 = w_a[2][:, cs] * vc
                for s in (1, 2):
                    y8 = y8 + w_a[2 - s][:, cs] * _shift_down(vc, vp, s, row)
                us.append(u8)
                yas.append(y8)
                xp, vp = xc, vc
            u16 = jnp.concatenate(us, axis=0)
            u_s[rows, cs] = u16
            ub_s[rows, cs] = u16.astype(BF16)
            ya_s[rows, cs] = jnp.concatenate(yas, axis=0)
        return carry

    lax.fori_loop(0, tt // 16, blk, 0)


def _mixer_fwd(proj, prm, wa, wx, shards, fulls, slicers):
    t_len = proj.shape[0]
    tt = min(TT, t_len)
    nt = t_len // tt
    na = len(shards)

    def body(proj_ref, prm_ref, wa_ref, wx_ref, *rest):
        ins, (mg_ref, hl_ref, sv_ref), outs = rest[:na], rest[na:na + 3], rest[na + 3:2 * na + 3]
        xe, ve, hc, rp_s, ip_s, ub_s = rest[2 * na + 3:2 * na + 9]
        start, forward, finish = _ag_phases(ins, outs, slicers, *rest[2 * na + 9:])
        t = pl.program_id(0)

        @pl.when(t == 0)
        def _():
            start()
            xe[0:8, :] = jnp.zeros((8, D), F32)
            ve[0:8, :] = jnp.zeros((8, D), F32)
            hc[...] = jnp.zeros((8, D), F32)

        @pl.when(t == (3 * nt) // 4)
        def _():
            forward()

        _conv_fwd_rows(tt, proj_ref, prm_ref, xe, ve, sv_ref.at[SV_U], ub_s, sv_ref.at[SV_YA])
        xe[0:8, :] = xe[tt:tt + 8, :]
        ve[0:8, :] = ve[tt:tt + 8, :]

        ub = ub_s[...]
        for h in range(HEADS):
            cs = slice(h * HB, (h + 1) * HB)
            rp_s[:, cs] = _dot(ub[:, cs], wa_ref[h]) + prm_ref[P_BA:P_BA + 1, cs]
            ip_s[:, cs] = _dot(ub[:, cs], wx_ref[h]) + prm_ref[P_BX:P_BX + 1, cs]

        ls_all = _log_sigmoid(prm_ref[P_LAM:P_LAM + 1, :])
        row = lax.broadcasted_iota(jnp.int32, (8, CG), 0)

        def blk(i, carry):
            r0 = pl.multiple_of(i * 16, 16)
            for g in range(D // CG):
                cs = slice(g * CG, (g + 1) * CG)
                ls = ls_all[:, cs]
                hprev = hc[:, cs]
                hs = []
                for sb in range(2):
                    rr = r0 + 8 * sb
                    first = (row + (t * tt + rr)) == 0
                    r8 = pl.ds(rr, 8)
                    r, ig, _, a, _, mult = _lru_gates(rp_s[r8, cs], ip_s[r8, cs], ls, first)
                    for plane, val in ((SV_R, r), (SV_I, ig), (SV_A, a), (SV_MULT, mult)):
                        sv_ref[plane, r8, cs] = val
                    b = mult * (ig * sv_ref[SV_U, r8, cs])
                    for s in (1, 2, 4):
                        a_sh = jnp.where(row >= s, pltpu.roll(a, s, 0), 1.0)
                        b_sh = jnp.where(row >= s, pltpu.roll(b, s, 0), 0.0)
                        b = a * b_sh + b
                        a = a * a_sh
                    hv = a * hprev + b
                    hprev = jnp.broadcast_to(hv[7:8, :], hv.shape)
                    hs.append(hv)
                hc[:, cs] = hprev
                h16 = jnp.concatenate(hs, axis=0)
                rows = pl.ds(r0, 16)
                gl, _ = _gelu(_pj(proj_ref, 4, rows, cs).astype(F32))
                y_b = h16 * gl
                y_a = _pj(proj_ref, 0, rows, cs).astype(F32) * sv_ref[SV_YA, rows, cs]
                mg = (_sig(_pj(proj_ref, 5, rows, cs).astype(F32)) * y_a
                      + _sig(_pj(proj_ref, 6, rows, cs).astype(F32)) * y_b)
                mg_ref[rows, cs] = mg.astype(BF16)
                hl_ref[rows, cs] = h16.astype(BF16)
            return carry

        lax.fori_loop(0, tt // 16, blk, 0)

        @pl.when(t == nt - 1)
        def _():
            finish()

    res = pl.pallas_call(
        body, name="mixer_fwd", grid=(nt,),
        in_specs=[pl.BlockSpec((tt, 7 * D), lambda t: (t, 0)),
                  pl.BlockSpec((16, D), lambda t: (0, 0)),
                  pl.BlockSpec((HEADS, HB, HB), lambda t: (0, 0, 0)),
                  pl.BlockSpec((HEADS, HB, HB), lambda t: (0, 0, 0))] + [_ANY] * na,
        out_specs=[pl.BlockSpec((tt, D), lambda t: (t, 0)), pl.BlockSpec((tt, D), lambda t: (t, 0)),
                   pl.BlockSpec((len(SV_PLANES), tt, D), lambda t: (0, t, 0))] + [_ANY] * na,
        out_shape=[jax.ShapeDtypeStruct((t_len, D), BF16), jax.ShapeDtypeStruct((t_len, D), BF16),
                   jax.ShapeDtypeStruct((len(SV_PLANES), t_len, D), F32)]
        + [jax.ShapeDtypeStruct(f, sh.dtype) for f, sh in zip(fulls, shards)],
        scratch_shapes=[pltpu.VMEM((tt + 8, D), F32), pltpu.VMEM((tt + 8, D), F32), pltpu.VMEM((8, D), F32),
                        pltpu.VMEM((tt, D), F32), pltpu.VMEM((tt, D), F32), pltpu.VMEM((tt, D), BF16)]
        + _ag_sems(na),
        compiler_params=_cp("arbitrary"),
    )(proj, prm, wa, wx, *shards)
    return res[0], res[1], res[2], res[3:]


def _out_proj(merged, x, mod, g_ffn, w_out):
    t_len = x.shape[0]
    tm = min(TM, t_len)

    def body(mg_ref, x_ref, mod_ref, g_ref, w_ref, x1_ref, h2_ref):
        gt1 = mod_ref[2:3, :]
        gs = g_ref[...] * (1.0 + mod_ref[4:5, :])
        sh = mod_ref[3:4, :]
        for sub in _sub_blocks(tm):
            x1_ref[sub, :] = x_ref[sub, :] + gt1 * _dot(mg_ref[sub, :], w_ref[...])
            for r0 in range(sub.start, sub.stop, 16):
                x1 = x1_ref[r0:r0 + 16, :]
                r = lax.rsqrt(jnp.mean(x1 * x1, axis=-1, keepdims=True) + EPS)
                h2_ref[r0:r0 + 16, :] = (x1 * r * gs + sh).astype(BF16)

    return pl.pallas_call(
        body, name="out_proj", grid=(t_len // tm,),
        in_specs=[pl.BlockSpec((tm, D), lambda i: (i, 0)), pl.BlockSpec((tm, D), lambda i: (i, 0)),
                  pl.BlockSpec((8, D), lambda i: (0, 0)), pl.BlockSpec((1, D), lambda i: (0, 0)),
                  pl.BlockSpec((D, D), lambda i: (0, 0))],
        out_specs=[pl.BlockSpec((tm, D), lambda i: (i, 0)), pl.BlockSpec((tm, D), lambda i: (i, 0))],
        out_shape=[jax.ShapeDtypeStruct((t_len, D), F32), jax.ShapeDtypeStruct((t_len, D), BF16)],
        compiler_params=_cp("parallel"),
    )(merged, x, mod, g_ffn, w_out)


def _ffn_fwd(h2, x1, target, mod, g_fin, w_gu, w_down):
    t_len = x1.shape[0]
    tm = min(TMF, t_len)

    def body(h2_ref, x1_ref, tg_ref, mod_ref, g_ref, wgu_ref, wd_ref, gu_ref, dx2_ref, dx2b_ref, loss_ref, dg_ref, acc):
        @pl.when(pl.program_id(0) == 0)
        def _():
            loss_ref[...] = jnp.zeros_like(loss_ref)
            dg_ref[...] = jnp.zeros_like(dg_ref)

        hb = h2_ref[...]
        ffn = None
        nxt = (_dot_nt(hb, wgu_ref[0, 0]), _dot_nt(hb, wgu_ref[1, 0]))
        for j in range(4):
            gate, up = nxt
            if j < 3:
                nxt = (_dot_nt(hb, wgu_ref[0, j + 1]), _dot_nt(hb, wgu_ref[1, j + 1]))
            gu_ref[0, j] = gate.astype(BF16)
            gu_ref[1, j] = up.astype(BF16)
            act = (gate * _sig(gate) * up).astype(BF16)
            part = _dot(act, wd_ref[j * FB:(j + 1) * FB, :])
            ffn = part if ffn is None else ffn + part
        acc[...] = ffn

        gt2 = mod_ref[5:6, :]
        gf = g_ref[...]

        s_loss = s_dg = jnp.zeros((8, D), F32)
        for r0 in range(0, tm, 16):
            rows = slice(r0, r0 + 16)
            x2 = x1_ref[rows, :] + gt2 * acc[rows, :]
            r = lax.rsqrt(jnp.mean(x2 * x2, axis=-1, keepdims=True) + EPS)
            xn = x2 * r
            diff = xn * gf - tg_ref[rows, :]
            dy = diff * (1.0 / D)
            dxn = dy * gf
            dx2 = r * (dxn - xn * jnp.mean(dxn * xn, axis=-1, keepdims=True))
            dx2_ref[rows, :] = dx2
            dx2b_ref[rows, :] = dx2.astype(BF16)
            s_loss, s_dg = s_loss + _fold8(diff * diff), s_dg + _fold8(dy * xn)
        loss_ref[...] += jnp.sum(s_loss) * (0.5 / D)
        dg_ref[...] += jnp.sum(s_dg, axis=0, keepdims=True)

    row = pl.BlockSpec((tm, D), lambda i: (i, 0))
    return pl.pallas_call(
        body, name="ffn_fwd", grid=(t_len // tm,),
        in_specs=[row, row, row, pl.BlockSpec((8, D), lambda i: (0, 0)), pl.BlockSpec((1, D), lambda i: (0, 0)),
                  _resident((2, 4, FB, D)), _resident((DFF, D))],
        out_specs=[pl.BlockSpec((2, 4, tm, FB), lambda i: (0, 0, i, 0)), row, row,
                   pl.BlockSpec((1, 128), lambda i: (0, 0)), pl.BlockSpec((1, D), lambda i: (0, 0))],
        out_shape=[jax.ShapeDtypeStruct((2, 4, t_len, FB), BF16), jax.ShapeDtypeStruct((t_len, D), F32),
                   jax.ShapeDtypeStruct((t_len, D), BF16),
                   jax.ShapeDtypeStruct((1, 128), F32), jax.ShapeDtypeStruct((1, D), F32)],
        scratch_shapes=[pltpu.VMEM((tm, D), F32)],
        compiler_params=_cp("arbitrary"),
    )(h2, x1, target, mod, g_fin, w_gu, w_down)


S_SH, S_SC, S_G = 0, 1, 2


def _norm_bwd_rows(span, sums, dh_ref, x_ref, dres_ref, scale, gain, write):
    gs = 1.0 + scale
    s_sh, s_sc, s_g = sums
    for r0 in range(span.start, span.stop, 16):
        rows = slice(r0, r0 + 16)
        dh = dh_ref[rows, :]
        xv = x_ref[rows, :]
        r = lax.rsqrt(jnp.mean(xv * xv, axis=-1, keepdims=True) + EPS)
        xn = xv * r
        dhn = dh * gs
        dxn = dhn * gain
        write(rows, dres_ref[rows, :] + r * (dxn - xn * jnp.mean(dxn * xn, axis=-1, keepdims=True)))
        s_sh, s_sc, s_g = s_sh + _fold8(dh), s_sc + _fold8(dh * (xn * gain)), s_g + _fold8(dhn * xn)
    return s_sh, s_sc, s_g


def _add_norm_sums(sums_ref, sums):
    for dst, s in zip((S_SH, S_SC, S_G), sums):
        sums_ref[dst:dst + 1, :] += jnp.sum(s, axis=0, keepdims=True)


def _ffn_bwd(dx2, gu, x1, mod, g_ffn, w_gu, w_down, w_out):
    t_len = x1.shape[0]
    tm = min(TMF, t_len)

    def body(dx2_ref, gu_ref, x1_ref, mod_ref, g_ref, wgu_ref, wd_ref, wo_ref,
             dgu_ref, act_ref, dx1_ref, dx1b_ref, dmg_ref, sums_ref, acc, dmo, dact_s):
        @pl.when(pl.program_id(0) == 0)
        def _():
            sums_ref[...] = jnp.zeros_like(sums_ref)

        dffn = (dx2_ref[...] * mod_ref[5:6, :]).astype(BF16)
        dact_s[0] = _dot_nt(dffn, wd_ref[0:FB, :])
        for j in range(4):
            if j < 3:
                dact_s[(j + 1) % 2] = _dot_nt(dffn, wd_ref[(j + 1) * FB:(j + 2) * FB, :])
            for r0 in range(0, tm, 16):
                rows = slice(r0, r0 + 16)
                dact = dact_s[j % 2, rows, :]
                gate = gu_ref[0, j, rows, :].astype(F32)
                up = gu_ref[1, j, rows, :].astype(F32)
                sg = _sig(gate)
                silu = gate * sg
                act_ref[j, rows, :] = (silu * up).astype(BF16)
                dgu_ref[0, j, rows, :] = (dact * up * (sg * (1.0 + gate * (1.0 - sg)))).astype(BF16)
                dgu_ref[1, j, rows, :] = (dact * silu).astype(BF16)
            part = _dot(dgu_ref[0, j], wgu_ref[0, j]) + _dot(dgu_ref[1, j], wgu_ref[1, j])
            if j == 0:
                acc[...] = part
            else:
                acc[...] += part

        gt1 = mod_ref[2:3, :]

        def write(rows, dx1):
            dx1_ref[rows, :] = dx1
            dx1b_ref[rows, :] = dx1.astype(BF16)
            dmo[rows, :] = (dx1 * gt1).astype(BF16)

        zero = jnp.zeros((8, D), F32)
        sums = (zero, zero, zero)
        for sub in (slice(0, tm // 2), slice(tm // 2, tm)):
            sums = _norm_bwd_rows(sub, sums, acc, x1_ref, dx2_ref, mod_ref[4:5, :], g_ref[...], write)
            dmg_ref[sub, :] = _dot_nt(dmo[sub, :], wo_ref[...]).astype(BF16)
        _add_norm_sums(sums_ref, sums)

    row = pl.BlockSpec((tm, D), lambda i: (i, 0))
    return pl.pallas_call(
        body, name="ffn_bwd", grid=(t_len // tm,),
        in_specs=[row, pl.BlockSpec((2, 4, tm, FB), lambda i: (0, 0, i, 0)), row,
                  pl.BlockSpec((8, D), lambda i: (0, 0)), pl.BlockSpec((1, D), lambda i: (0, 0)),
                  _resident((2, 4, FB, D)), _resident((DFF, D)), _resident((D, D))],
        out_specs=[pl.BlockSpec((2, 4, tm, FB), lambda i: (0, 0, i, 0)),
                   pl.BlockSpec((4, tm, FB), lambda i: (0, i, 0)), row, row, row,
                   pl.BlockSpec((8, D), lambda i: (0, 0))],
        out_shape=[jax.ShapeDtypeStruct((2, 4, t_len, FB), BF16), jax.ShapeDtypeStruct((4, t_len, FB), BF16),
                   jax.ShapeDtypeStruct((t_len, D), F32), jax.ShapeDtypeStruct((t_len, D), BF16),
                   jax.ShapeDtypeStruct((t_len, D), BF16), jax.ShapeDtypeStruct((8, D), F32)],
        scratch_shapes=[pltpu.VMEM((tm, D), F32), pltpu.VMEM((tm, D), BF16), pltpu.VMEM((2, tm, FB), F32)],
        compiler_params=_cp("arbitrary"),
    )(dx2, gu, x1, mod, g_ffn, w_gu, w_down, w_out)


def _my_pos():
    return lax.axis_index("x"), lax.axis_index("y"), lax.axis_index("c")


def _my_index():
    x, y, c = _my_pos()
    return 4 * x + 2 * y + c


def _device_of(b):
    return (b >> 2) & 1, (b >> 1) & 1, b & 1


def _rs_send(src, parts_ref, b, send_sems, recv_sems, local_sem):
    me = _my_index()
    dst = parts_ref.at[me]

    @pl.when(b == me)
    def _():
        pltpu.make_async_copy(src, dst, local_sem).start()

    @pl.when(b != me)
    def _():
        pltpu.make_async_remote_copy(src_ref=src, dst_ref=dst, send_sem=send_sems.at[b], recv_sem=recv_sems.at[me],
                                     device_id=_device_of(b), device_id_type=MESH).start()


def _rs_finish(src_of, parts_ref, send_sems, recv_sems, local_sem):
    me = _my_index()
    for s in range(NDEV):
        @pl.when(s != me)
        def _():
            cp = pltpu.make_async_remote_copy(src_ref=src_of(s), dst_ref=parts_ref.at[s], send_sem=send_sems.at[s],
                                              recv_sem=recv_sems.at[s], device_id=_device_of(s), device_id_type=MESH)
            cp.wait_send()
            cp.wait_recv()

        @pl.when(s == me)
        def _():
            pltpu.make_async_copy(src_of(s), parts_ref.at[s], local_sem).wait()


_RS_SEMS = [pltpu.SemaphoreType.DMA((NDEV,)), pltpu.SemaphoreType.DMA((NDEV,)), pltpu.SemaphoreType.DMA]
_ANY = pl.BlockSpec(memory_space=pl.ANY)


def _xor_order(me, n):
    return (me ^ (n - 1 - jnp.arange(n, dtype=jnp.int32))).astype(jnp.int32)


NCHIP = NDEV // 2


def _rs2_scratch(half_shape):
    blocks = lambda *lead: pltpu.VMEM(lead + tuple(half_shape), BF16)
    return [blocks(NCHIP, 2), blocks(NCHIP)] + [pltpu.SemaphoreType.DMA((NCHIP,))] * 4 + [pltpu.SemaphoreType.DMA]


def _rs2_to_sibling(q, rs):
    stage, from_sib, d_send, d_recv = rs[:4]
    x, y, c = _my_pos()
    pltpu.make_async_remote_copy(src_ref=stage.at[q, 1 - c], dst_ref=from_sib.at[q], send_sem=d_send.at[q],
                                 recv_sem=d_recv.at[q], device_id=(x, y, 1 - c), device_id_type=MESH).start()


def _rs2_forward(q, parts_ref, rs):
    stage, chip_sum, d_send, d_recv, i_send, i_recv, local_sem = rs
    x, y, c = _my_pos()
    my_chip = 2 * x + y
    pltpu.make_async_remote_copy(src_ref=stage.at[q, c], dst_ref=chip_sum.at[q], send_sem=d_send.at[q],
                                 recv_sem=d_recv.at[q], device_id=(x, y, 1 - c), device_id_type=MESH).wait_recv()
    chip_sum[q] = (stage[q, c].astype(F32) + chip_sum[q].astype(F32)).astype(BF16)

    @pl.when(q == my_chip)
    def _():
        pltpu.make_async_copy(chip_sum.at[q], parts_ref.at[my_chip], local_sem).start()

    @pl.when(q != my_chip)
    def _():
        pltpu.make_async_remote_copy(src_ref=chip_sum.at[q], dst_ref=parts_ref.at[my_chip], send_sem=i_send.at[q],
                                     recv_sem=i_recv.at[my_chip], device_id=((q >> 1) & 1, q & 1, c),
                                     device_id_type=MESH).start()


def _rs2_finish(parts_ref, rs):
    stage, chip_sum, d_send, d_recv, i_send, i_recv, local_sem = rs
    x, y, c = _my_pos()
    my_chip = 2 * x + y
    for q in range(NCHIP):
        pltpu.make_async_remote_copy(src_ref=stage.at[q, 1 - c], dst_ref=chip_sum.at[q], send_sem=d_send.at[q],
                                     recv_sem=d_recv.at[q], device_id=(x, y, 1 - c), device_id_type=MESH).wait_send()

        @pl.when(q != my_chip)
        def _():
            cp = pltpu.make_async_remote_copy(src_ref=chip_sum.at[q], dst_ref=parts_ref.at[q], send_sem=i_send.at[q],
                                              recv_sem=i_recv.at[q], device_id=((q >> 1) & 1, q & 1, c),
                                              device_id_type=MESH)
            cp.wait_send()
            cp.wait_recv()

        @pl.when(q == my_chip)
        def _():
            pltpu.make_async_copy(chip_sum.at[q], parts_ref.at[q], local_sem).wait()


def _gu_wgrad(h2, dgu, order):
    t_len = h2.shape[0]
    tk = min(TK, t_len)
    nk = t_len // tk

    def body(ord_ref, h_ref, d_ref, parts_ref, acc, *rs):
        p, k = pl.program_id(0), pl.program_id(1)

        @pl.when(k == 0)
        def _():
            acc[...] = jnp.zeros_like(acc)

        hb = h_ref[...]
        for half in range(2):
            acc[half] += _dot_tn(d_ref[0, half], hb)

        @pl.when(k == nk - 1)
        def _():
            q = ord_ref[p]
            rs[0][q] = acc[...].astype(BF16)
            _rs2_to_sibling(q, rs)

        @pl.when((k == min(1, nk - 1)) & (p > 0))
        def _():
            _rs2_forward(ord_ref[p - 1], parts_ref, rs)

        @pl.when((p == NCHIP - 1) & (k == nk - 1))
        def _():
            _rs2_forward(ord_ref[p], parts_ref, rs)
            _rs2_finish(parts_ref, rs)

    return pl.pallas_call(
        body, name="gu_wgrad",
        grid_spec=pltpu.PrefetchScalarGridSpec(
            num_scalar_prefetch=1, grid=(NCHIP, nk),
            in_specs=[pl.BlockSpec((tk, D), lambda p, k, o: (k, 0)),
                      pl.BlockSpec((1, 2, tk, FB), lambda p, k, o: (o[p], 0, k, 0))],
            out_specs=_ANY,
            scratch_shapes=[pltpu.VMEM((2, FB, D), F32)] + _rs2_scratch((FB, D))),
        out_shape=jax.ShapeDtypeStruct((NCHIP, FB, D), BF16),
        compiler_params=_cp("arbitrary", "arbitrary"),
    )(order, h2, dgu.reshape(NCHIP, 2, t_len, FB))


def _scaled_wgrad(name, a, dx, w, gate_row, mod, order):
    t_len = dx.shape[0]
    kb = w.shape[0] // NCHIP
    tk = min(TK, t_len)
    nk = t_len // tk
    rows = kb // 2
    if a.ndim == 3:
        a_spec = pl.BlockSpec((None, tk, kb), lambda p, k, o: (o[p], k, 0))
    else:
        a_spec = pl.BlockSpec((tk, kb), lambda p, k, o: (k, o[p]))

    def body(ord_ref, a_ref, dx_ref, w_ref, mod_ref, parts_ref, dg_ref, acc, *rs):
        p, k = pl.program_id(0), pl.program_id(1)

        @pl.when((p == 0) & (k == 0))
        def _():
            dg_ref[...] = jnp.zeros_like(dg_ref)

        @pl.when(k == 0)
        def _():
            acc[...] = jnp.zeros_like(acc)

        acc[...] += _dot_tn(a_ref[...], dx_ref[...])

        @pl.when(k == nk - 1)
        def _():
            q = ord_ref[p]
            z = acc[...]
            zg = (z * mod_ref[gate_row:gate_row + 1, :]).astype(BF16)
            dg_ref[0:1, :] += jnp.sum(z * w_ref[...].astype(F32), axis=0, keepdims=True)
            for half in range(2):
                rs[0][q, half] = zg[half * rows:(half + 1) * rows]
            _rs2_to_sibling(q, rs)

        @pl.when((k == min(1, nk - 1)) & (p > 0))
        def _():
            _rs2_forward(ord_ref[p - 1], parts_ref, rs)

        @pl.when((p == NCHIP - 1) & (k == nk - 1))
        def _():
            _rs2_forward(ord_ref[p], parts_ref, rs)
            _rs2_finish(parts_ref, rs)

    return pl.pallas_call(
        body, name=name,
        grid_spec=pltpu.PrefetchScalarGridSpec(
            num_scalar_prefetch=1, grid=(NCHIP, nk),
            in_specs=[a_spec,
                      pl.BlockSpec((tk, D), lambda p, k, o: (k, 0)),
                      pl.BlockSpec((kb, D), lambda p, k, o: (o[p], 0)),
                      pl.BlockSpec((8, D), lambda p, k, o: (0, 0))],
            out_specs=[_ANY, pl.BlockSpec((8, D), lambda p, k, o: (0, 0))],
            scratch_shapes=[pltpu.VMEM((kb, D), F32)] + _rs2_scratch((rows, D))),
        out_shape=[jax.ShapeDtypeStruct((NCHIP, rows, D), BF16), jax.ShapeDtypeStruct((8, D), F32)],
        compiler_params=_cp("arbitrary", "arbitrary"),
    )(order, a, dx, w, mod)


M_WA, M_WB, M_CBIAS, M_BA, M_BX, M_LS = 0, 3, 7, 8, 9, 10


def _conv_bwd_rows(tt, proj_ref, prm_ref, due, dye, dp_ref, acc8):
    row = lax.broadcasted_iota(jnp.int32, (8, CG), 0)
    w_b = [prm_ref[P_WB + k:P_WB + k + 1, :] for k in range(4)]
    w_a = [prm_ref[P_WA + k:P_WA + k + 1, :] for k in range(3)]

    def blk(ib, carry):
        r0 = pl.multiple_of(ib * 16, 16)
        rows = pl.ds(r0, 16)
        for g in range(D // CG):
            cs = slice(g * CG, (g + 1) * CG)
            du16, du_after = due[rows, cs], due[pl.ds(r0 + 16, 8), cs]
            dy16, dy_after = dye[rows, cs], dye[pl.ds(r0 + 16, 8), cs]
            cc16 = _pj(proj_ref, 1, rows, cs).astype(F32)
            cx16 = _pj(proj_ref, 2, rows, cs).astype(F32)
            x16 = _pj(proj_ref, 3, rows, cs).astype(F32)
            v16 = cc16 * cx16
            acc = [acc8[8 * k:8 * k + 8, cs] for k in range(8)]
            drx, dv = [], []
            for sb in range(2):
                lo = slice(8 * sb, 8 * sb + 8)
                duc, dyc, xc, vc = du16[lo], dy16[lo], x16[lo], v16[lo]
                du_n = du16[8:16] if sb == 0 else du_after
                dy_n = dy16[8:16] if sb == 0 else dy_after
                acc[0] = acc[0] + duc
                acc[4] = acc[4] + duc * xc
                d8 = w_b[3][:, cs] * duc
                for s in (1, 2, 3):
                    du_s = _shift_up(duc, du_n, s, row)
                    acc[4 - s] = acc[4 - s] + du_s * xc
                    d8 = d8 + w_b[3 - s][:, cs] * du_s
                acc[7] = acc[7] + dyc * vc
                e8 = w_a[2][:, cs] * dyc
                for s in (1, 2):
                    dy_s = _shift_up(dyc, dy_n, s, row)
                    acc[7 - s] = acc[7 - s] + dy_s * vc
                    e8 = e8 + w_a[2 - s][:, cs] * dy_s
                drx.append(d8)
                dv.append(e8)
            for k in range(8):
                acc8[8 * k:8 * k + 8, cs] = acc[k]
            dv16 = jnp.concatenate(dv, axis=0)
            col = lambda s: slice(s * D + g * CG, s * D + (g + 1) * CG)
            dp_ref[rows, col(3)] = jnp.concatenate(drx, axis=0).astype(BF16)
            dp_ref[rows, col(1)] = (dv16 * cx16).astype(BF16)
            dp_ref[rows, col(2)] = (dv16 * cc16).astype(BF16)
        return carry

    lax.fori_loop(0, tt // 16, blk, 0)


def _mixer_bwd(proj, hl, sv, dmg, prm, wa, wx):
    t_len = proj.shape[0]
    tt = min(TT, t_len)
    nt = t_len // tt
    hb8 = tt // 8

    def rev(i):
        return nt - 1 - i

    def halo(i):
        return jnp.maximum(rev(i) * hb8 - 1, 0)

    def body(proj_ref, hl_ref, hh_ref, sv_ref, dmg_ref, prm_ref, wa_ref, wx_ref,
             dp_ref, sums_ref, gwa_ref, gwx_ref,
             he, due, dye, drp_s, dip_s, an, gn, acc8):
        i = pl.program_id(0)
        t = rev(i)

        @pl.when(i == 0)
        def _():
            sums_ref[...] = jnp.zeros_like(sums_ref)
            gwa_ref[...] = jnp.zeros_like(gwa_ref)
            gwx_ref[...] = jnp.zeros_like(gwx_ref)
            due[tt:tt + 8, :] = jnp.zeros((8, D), F32)
            dye[tt:tt + 8, :] = jnp.zeros((8, D), F32)
            an[...] = jnp.zeros((8, D), F32)
            gn[...] = jnp.zeros((8, D), F32)

        live = (t > 0).astype(F32)
        he[0:8, :] = hh_ref[...].astype(F32) * live
        he[8:8 + tt, :] = hl_ref[...].astype(F32)

        ls_all = _log_sigmoid(prm_ref[P_LAM:P_LAM + 1, :])
        row = lax.broadcasted_iota(jnp.int32, (8, CG), 0)
        nblk = tt // 16

        def blk(ib, carry):
            r0 = pl.multiple_of((nblk - 1 - ib) * 16, 16)
            rows = pl.ds(r0, 16)
            for g in range(D // CG):
                cs = slice(g * CG, (g + 1) * CG)
                ls = ls_all[:, cs]
                dm = dmg_ref[rows, cs].astype(F32)
                cb = _pj(proj_ref, 0, rows, cs).astype(F32)
                rg = _pj(proj_ref, 4, rows, cs).astype(F32)
                sga = _sig(_pj(proj_ref, 5, rows, cs).astype(F32))
                sgb = _sig(_pj(proj_ref, 6, rows, cs).astype(F32))
                ya0 = sv_ref[SV_YA, rows, cs]
                h16 = he[pl.ds(r0 + 8, 16), cs]
                gl, th = _gelu(rg)
                dgl = 0.5 * (1.0 + th) + 0.5 * rg * (1.0 - th * th) * (_GC * (1.0 + 3.0 * 0.044715 * rg * rg))
                y_a = cb * ya0
                y_b = h16 * gl
                dy_a = dm * sga
                dy_b = dm * sgb
                col = lambda s: slice(s * D + g * CG, s * D + (g + 1) * CG)
                dp_ref[rows, col(5)] = (dm * y_a * sga * (1.0 - sga)).astype(BF16)
                dp_ref[rows, col(6)] = (dm * y_b * sgb * (1.0 - sgb)).astype(BF16)
                dp_ref[rows, col(4)] = (dy_b * h16 * dgl).astype(BF16)
                dp_ref[rows, col(0)] = (dy_a * ya0).astype(BF16)
                dye[rows, cs] = dy_a * cb
                dh16 = dy_b * gl

                a_next = an[:, cs]
                g_next = gn[:, cs]
                s_ba = jnp.zeros((8, CG), F32)
                s_bx = jnp.zeros((8, CG), F32)
                s_ls = jnp.zeros((8, CG), F32)
                for sb in (1, 0):
                    rr = r0 + 8 * sb
                    first = (row + (t * tt + rr)) == 0
                    r8 = pl.ds(rr, 8)
                    uu, r, ig, a, mult = (sv_ref[pln, r8, cs] for pln in (SV_U, SV_R, SV_I, SV_A, SV_MULT))
                    ca = jnp.where(row < 7, pltpu.roll(a, 7, 0), a_next)
                    cb_ = dh16[8 * sb:8 * sb + 8, :]
                    for s in (1, 2, 4):
                        a_sh = jnp.where(row < 8 - s, pltpu.roll(ca, 8 - s, 0), 1.0)
                        b_sh = jnp.where(row < 8 - s, pltpu.roll(cb_, 8 - s, 0), 0.0)
                        cb_ = ca * b_sh + cb_
                        ca = ca * a_sh
                    gv = ca * g_next + cb_
                    g_next = jnp.broadcast_to(gv[0:1, :], gv.shape)
                    a_next = jnp.broadcast_to(a[0:1, :], a.shape)
                    hprev = jnp.where(row >= 1, pltpu.roll(he[pl.ds(rr + 8, 8), cs], 1, 0),
                                      pltpu.roll(he[pl.ds(rr, 8), cs], 1, 0))
                    da = gv * hprev
                    dmult = jnp.where(first, 0.0, gv * ig * uu)
                    dla = da * a + jnp.where(mult > 0.0, dmult * (-(a * a) / mult), 0.0)
                    drp = dla * (LRU_C * ls) * r * (1.0 - r)
                    dip = gv * mult * uu * ig * (1.0 - ig)
                    s_ls = s_ls + dla * (LRU_C * r)
                    s_ba = s_ba + drp
                    s_bx = s_bx + dip
                    drp_s[pl.ds(rr, 8), cs] = drp
                    dip_s[pl.ds(rr, 8), cs] = dip
                    due[pl.ds(rr, 8), cs] = gv * mult * ig
                an[:, cs] = a_next
                gn[:, cs] = g_next
                sums_ref[M_BA:M_BA + 1, cs] += jnp.sum(s_ba, axis=0, keepdims=True)
                sums_ref[M_BX:M_BX + 1, cs] += jnp.sum(s_bx, axis=0, keepdims=True)
                sums_ref[M_LS:M_LS + 1, cs] += jnp.sum(s_ls, axis=0, keepdims=True)
            return carry

        lax.fori_loop(0, nblk, blk, 0)

        drp_b = drp_s[...].astype(BF16)
        dip_b = dip_s[...].astype(BF16)
        ub = sv_ref[SV_U].astype(BF16)
        for h in range(HEADS):
            cs = slice(h * HB, (h + 1) * HB)
            due[0:tt, cs] += _dot_nt(drp_b[:, cs], wa_ref[h]) + _dot_nt(dip_b[:, cs], wx_ref[h])
            gwa_ref[h] += _dot_tn(ub[:, cs], drp_b[:, cs])
            gwx_ref[h] += _dot_tn(ub[:, cs], dip_b[:, cs])

        acc8[...] = jnp.zeros_like(acc8)
        _conv_bwd_rows(tt, proj_ref, prm_ref, due, dye, dp_ref, acc8)
        for k, dst in enumerate([M_CBIAS] + [M_WB + k for k in range(4)] + [M_WA + k for k in range(3)]):
            sums_ref[dst:dst + 1, :] += jnp.sum(acc8[8 * k:8 * k + 8, :], axis=0, keepdims=True)
        due[tt:tt + 8, :] = due[0:8, :]
        dye[tt:tt + 8, :] = dye[0:8, :]

        @pl.when(i == nt - 1)
        def _():
            sums_ref[M_LS:M_LS + 1, :] = sums_ref[M_LS:M_LS + 1, :] * _sig(-prm_ref[P_LAM:P_LAM + 1, :])

    big = lambda: pltpu.VMEM((tt + 8, D), F32)
    tile = lambda: pltpu.VMEM((tt, D), F32)
    return pl.pallas_call(
        body, name="mixer_bwd", grid=(nt,),
        in_specs=[pl.BlockSpec((tt, 7 * D), lambda i: (rev(i), 0)),
                  pl.BlockSpec((tt, D), lambda i: (rev(i), 0)),
                  pl.BlockSpec((8, D), lambda i: (halo(i), 0)),
                  pl.BlockSpec((len(SV_PLANES), tt, D), lambda i: (0, rev(i), 0)),
                  pl.BlockSpec((tt, D), lambda i: (rev(i), 0)),
                  pl.BlockSpec((16, D), lambda i: (0, 0)),
                  pl.BlockSpec((HEADS, HB, HB), lambda i: (0, 0, 0)),
                  pl.BlockSpec((HEADS, HB, HB), lambda i: (0, 0, 0))],
        out_specs=[pl.BlockSpec((tt, 7 * D), lambda i: (rev(i), 0)),
                   pl.BlockSpec((16, D), lambda i: (0, 0)),
                   pl.BlockSpec((HEADS, HB, HB), lambda i: (0, 0, 0)),
                   pl.BlockSpec((HEADS, HB, HB), lambda i: (0, 0, 0))],
        out_shape=[jax.ShapeDtypeStruct((t_len, 7 * D), BF16), jax.ShapeDtypeStruct((16, D), F32),
                   jax.ShapeDtypeStruct((HEADS, HB, HB), F32), jax.ShapeDtypeStruct((HEADS, HB, HB), F32)],
        scratch_shapes=[big(), big(), big(), tile(), tile(),
                        pltpu.VMEM((8, D), F32), pltpu.VMEM((8, D), F32), pltpu.VMEM((64, D), F32)],
        compiler_params=_cp("arbitrary"),
    )(proj, hl, hl, sv, dmg, prm, wa, wx)


def _in_proj_bwd(dproj, w_in, x, dx1, mod, g_mix):
    t_len = x.shape[0]
    tm = min(TM, t_len)

    def body(dp_ref, w_ref, x_ref, dx1_ref, mod_ref, g_ref, gx_ref, sums_ref, acc):
        @pl.when(pl.program_id(0) == 0)
        def _():
            sums_ref[...] = jnp.zeros_like(sums_ref)

        def write(rows, dx):
            gx_ref[rows, :] = dx

        zero = jnp.zeros((8, D), F32)
        sums = (zero, zero, zero)
        for sub in _sub_blocks(tm):
            acc[sub, :] = _dot_nt(dp_ref[sub, :], w_ref[...])
            sums = _norm_bwd_rows(sub, sums, acc, x_ref, dx1_ref, mod_ref[1:2, :], g_ref[...], write)
        _add_norm_sums(sums_ref, sums)

    return pl.pallas_call(
        body, name="in_proj_bwd", grid=(t_len // tm,),
        in_specs=[pl.BlockSpec((tm, 7 * D), lambda i: (i, 0)),
                  _resident((D, 7 * D)),
                  pl.BlockSpec((tm, D), lambda i: (i, 0)), pl.BlockSpec((tm, D), lambda i: (i, 0)),
                  pl.BlockSpec((8, D), lambda i: (0, 0)), pl.BlockSpec((1, D), lambda i: (0, 0))],
        out_specs=[pl.BlockSpec((tm, D), lambda i: (i, 0)), pl.BlockSpec((8, D), lambda i: (0, 0))],
        out_shape=[jax.ShapeDtypeStruct((t_len, D), F32), jax.ShapeDtypeStruct((8, D), F32)],
        scratch_shapes=[pltpu.VMEM((tm, D), F32)],
        compiler_params=_cp("arbitrary"),
    )(dproj, w_in, x, dx1, mod, g_mix)


def _in_wgrad(h, dproj, g_wa, g_wx, order):
    t_len = h.shape[0]
    tk = min(TKI, t_len)
    nk = t_len // tk
    cw = 7 * D // NDEV
    hr = HB // NDEV

    def body(ord_ref, h_ref, d_ref, ga_ref, gx_ref, parts_ref, pa_ref, px_ref, acc, *scr):
        rs, sems = scr[:-6], scr[-6:]
        p, k = pl.program_id(0), pl.program_id(1)

        def head_rows(ref):
            return lambda s: ref.at[:, pl.ds(s * hr, hr), :]

        @pl.when((p == 0) & (k == 0))
        def _():
            for s in range(NDEV):
                _rs_send(head_rows(ga_ref)(s), pa_ref, s, *sems[0:3])
                _rs_send(head_rows(gx_ref)(s), px_ref, s, *sems[3:6])

        @pl.when(k == 0)
        def _():
            acc[...] = jnp.zeros_like(acc)

        acc[...] += _dot_tn(h_ref[...], d_ref[...])

        @pl.when(k == nk - 1)
        def _():
            q = ord_ref[p]
            for half in range(2):
                rs[0][q, half] = acc[:, half * cw:(half + 1) * cw].astype(BF16)
            _rs2_to_sibling(q, rs)

        @pl.when((k == min(1, nk - 1)) & (p > 0))
        def _():
            _rs2_forward(ord_ref[p - 1], parts_ref, rs)

        @pl.when((p == NCHIP - 1) & (k == nk - 1))
        def _():
            _rs2_forward(ord_ref[p], parts_ref, rs)
            _rs2_finish(parts_ref, rs)
            _rs_finish(head_rows(ga_ref), pa_ref, *sems[0:3])
            _rs_finish(head_rows(gx_ref), px_ref, *sems[3:6])

    return pl.pallas_call(
        body, name="in_wgrad",
        grid_spec=pltpu.PrefetchScalarGridSpec(
            num_scalar_prefetch=1, grid=(NCHIP, nk),
            in_specs=[pl.BlockSpec((tk, D), lambda p, k, o: (k, 0)),
                      pl.BlockSpec((tk, 2 * cw), lambda p, k, o: (k, o[p])), _ANY, _ANY],
            out_specs=[_ANY, _ANY, _ANY],
            scratch_shapes=[pltpu.VMEM((D, 2 * cw), F32)] + _rs2_scratch((D, cw)) + _RS_SEMS * 2),
        out_shape=[jax.ShapeDtypeStruct((NCHIP, D, cw), BF16), jax.ShapeDtypeStruct((NDEV, HEADS, hr, HB), F32),
                   jax.ShapeDtypeStruct((NDEV, HEADS, hr, HB), F32)],
        compiler_params=_cp("arbitrary", "arbitrary"),
    )(order, h, dproj, g_wa, g_wx)


def _adam_math(w, g, m, v):
    m = ADAM_B1 * m + (1.0 - ADAM_B1) * g
    v = ADAM_B2 * v + (1.0 - ADAM_B2) * (g * g)
    m_hat = m / (1.0 - ADAM_B1 ** ADAM_STEP)
    v_hat = v / (1.0 - ADAM_B2 ** ADAM_STEP)
    delta = -ADAM_LR * (m_hat / (jnp.sqrt(v_hat) + ADAM_EPS) + ADAM_WD * w)
    return delta, m, v


def _ada_bwd(c_all, dmod_cols, w, m, v):
    rb = 256
    n = w.shape[1]
    nrow = c_all.shape[0]

    def body(c_ref, d_ref, w_ref, m_ref, v_ref, g_ref, dl_ref, nm_ref, nv_ref):
        cv = c_ref[...]
        g = _dot_tn((cv * _sig(cv)).astype(BF16), d_ref[...].astype(BF16))
        g_ref[...] = g
        dl_ref[...], nm_ref[...], nv_ref[...] = _adam_math(w_ref[...], g, m_ref[...], v_ref[...])

    blk = pl.BlockSpec((rb, n), lambda i: (i, 0))
    sds = jax.ShapeDtypeStruct(w.shape, F32)
    return pl.pallas_call(
        body, name="ada_bwd", grid=(D // rb,),
        in_specs=[pl.BlockSpec((nrow, rb), lambda i: (0, i)), pl.BlockSpec((nrow, n), lambda i: (0, 0)), blk, blk, blk],
        out_specs=[blk, blk, blk, blk], out_shape=[sds, sds, sds, sds],
        compiler_params=_cp("parallel"),
    )(c_all, dmod_cols, w, m, v)


def _adam(name, parts, w, m, v):
    p, r, c = parts.shape
    rb = max([cand for cand in range(8, min(r, 256) + 1, 8) if r % cand == 0], default=r)

    def body(p_ref, w_ref, m_ref, v_ref, g_ref, dl_ref, nm_ref, nv_ref):
        g = p_ref[0].astype(F32)
        for q in range(1, p):
            g = g + p_ref[q].astype(F32)
        g_ref[...] = g
        dl_ref[...], nm_ref[...], nv_ref[...] = _adam_math(w_ref[...], g, m_ref[...], v_ref[...])

    blk = pl.BlockSpec((rb, c), lambda i: (i, 0))
    sds = jax.ShapeDtypeStruct((r, c), F32)
    return pl.pallas_call(
        body, name=name, grid=(r // rb,),
        in_specs=[pl.BlockSpec((p, rb, c), lambda i: (0, i, 0)), blk, blk, blk],
        out_specs=[blk, blk, blk, blk], out_shape=[sds, sds, sds, sds],
        compiler_params=_cp("parallel"),
    )(parts, w, m, v)


_SMALL_SEMS = [pltpu.SemaphoreType.DMA((7,)), pltpu.SemaphoreType.DMA((7,)), pltpu.SemaphoreType.DMA]
_VMEM = pl.BlockSpec(memory_space=pltpu.VMEM)


def _exchange_small(x_ref, out_ref, send_sems, recv_sems, local_sem):
    m_per = x_ref.shape[0]
    x, y, c = _my_pos()
    me, sibling = (x, y, c), (x, y, 1 - c)
    chips = [(1 - x, y), (x, 1 - y), (1 - x, 1 - y)]

    def rows(px, py, pc):
        return out_ref.at[pl.ds((4 * px + 2 * py + pc) * m_per, m_per), :]

    def copy(k, block, to, src=None):
        return pltpu.make_async_remote_copy(
            src_ref=rows(*block) if src is None else src, dst_ref=rows(*block),
            send_sem=send_sems.at[k], recv_sem=recv_sems.at[k], device_id=to, device_id_type=MESH)

    mine = pltpu.make_async_copy(x_ref, rows(*me), local_sem)
    mine.start()
    first = [copy(0, me, sibling, src=x_ref)]
    first += [copy(1 + j, me, (*chip, c), src=x_ref) for j, chip in enumerate(chips)]
    for cp in first:
        cp.start()
    passed = [copy(4 + j, (*chip, c), sibling) for j, chip in enumerate(chips)]
    for j, chip in enumerate(chips):
        copy(1 + j, (*chip, c), me).wait_recv()
        passed[j].start()
    copy(0, sibling, me).wait_recv()
    for j, chip in enumerate(chips):
        copy(4 + j, (*chip, 1 - c), me).wait_recv()
    for cp in first + passed:
        cp.wait_send()
    mine.wait()


def _gather_small_grads(sums1, sums2, msums, d_gt1, d_gt2, d_gfin, loss):
    def body(s1, s2, ms, g1, g2, gf, ls, out_ref, pack, *sems):
        rows = [s1[S_SH:S_SH + 1, :], s1[S_SC:S_SC + 1, :], g1[0:1, :],
                s2[S_SH:S_SH + 1, :], s2[S_SC:S_SC + 1, :], g2[0:1, :],
                s1[S_G:S_G + 1, :], ms[M_CBIAS:M_CBIAS + 1, :], ms[M_BA:M_BA + 1, :], ms[M_BX:M_BX + 1, :],
                ms[M_LS:M_LS + 1, :], s2[S_G:S_G + 1, :], gf[...]]
        rows += [ms[M_WA + k:M_WA + k + 1, :] for k in range(3)] + [ms[M_WB + k:M_WB + k + 1, :] for k in range(4)]
        rows += [jnp.broadcast_to(ls[0:1, 0:1], (1, D))]
        for i, v in enumerate(rows):
            pack[i:i + 1, :] = v
        pack[len(rows):24, :] = jnp.zeros((24 - len(rows), D), F32)
        _exchange_small(pack, out_ref, *sems)

    return pl.pallas_call(
        body, name="gather_small", out_shape=jax.ShapeDtypeStruct((NDEV * 24, D), F32),
        in_specs=[_VMEM] * 7, out_specs=_VMEM, scratch_shapes=[pltpu.VMEM((24, D), F32)] + _SMALL_SEMS,
    )(sums1, sums2, msums, d_gt1, d_gt2, d_gfin, loss)


def _ada_mod(pack, w_ada, b_cols):
    ncol = w_ada.shape[1]

    def body(p_ref, w_ref, b_ref, all_ref, mod_ref, cols, *sems):
        _exchange_small(p_ref, all_ref, *sems[0:3])
        c_all = jnp.concatenate([all_ref[8 * d:8 * d + 1, 0:D] for d in range(NDEV)], axis=0)
        c16 = jnp.concatenate([c_all, jnp.zeros_like(c_all)], axis=0)
        mod16 = _dot((c16 * _sig(c16)).astype(BF16), w_ref[...].astype(BF16)) + b_ref[...]
        cols[...] = mod16[0:NDEV]
        _exchange_small(cols, mod_ref, *sems[3:6])

    return pl.pallas_call(
        body, name="ada_mod",
        out_shape=[jax.ShapeDtypeStruct((NDEV * 8, pack.shape[1]), F32), jax.ShapeDtypeStruct((NDEV * 8, ncol), F32)],
        in_specs=[_VMEM, _VMEM, _VMEM], out_specs=[_VMEM, _VMEM],
        scratch_shapes=[pltpu.VMEM((NDEV, ncol), F32)] + _SMALL_SEMS * 2,
        compiler_params=_cp(),
    )(pack, w_ada, b_cols)


def _blk_rows(n):
    return lambda ref, b: ref.at[pl.ds(pl.multiple_of(b * n, 8), n), :]


def _blk_lead(ref, b):
    return ref.at[b]


def _blk_heads(ref, b):
    return ref.at[:, pl.ds(pl.multiple_of(b * (HB // NDEV), 8), HB // NDEV), :]


def _ag_phases(ins, outs, slicers, send_sems, recv_sems, local_sems):
    na = len(ins)
    x, y, c = _my_pos()
    me, sibling = (x, y, c), (x, y, 1 - c)
    chips = [(1 - x, y), (x, 1 - y), (1 - x, 1 - y)]

    def copy(a, k, block, to, from_shard=False):
        px, py, pc = block
        dst = slicers[a](outs[a], 4 * px + 2 * py + pc)
        return pltpu.make_async_remote_copy(
            src_ref=ins[a] if from_shard else dst, dst_ref=dst,
            send_sem=send_sems.at[a * 7 + k], recv_sem=recv_sems.at[a * 7 + k], device_id=to, device_id_type=MESH)

    def local(a):
        return pltpu.make_async_copy(ins[a], slicers[a](outs[a], 4 * x + 2 * y + c), local_sems.at[a])

    def firsts(a):
        return [copy(a, 0, me, sibling, True)] + [copy(a, 1 + j, me, (*chip, c), True) for j, chip in enumerate(chips)]

    def start():
        for a in range(na):
            local(a).start()
            for cp in firsts(a):
                cp.start()

    def forward():
        for a in range(na):
            for j, chip in enumerate(chips):
                copy(a, 1 + j, (*chip, c), me).wait_recv()
                copy(a, 4 + j, (*chip, c), sibling).start()

    def finish():
        for a in range(na):
            copy(a, 0, sibling, me).wait_recv()
            for j, chip in enumerate(chips):
                copy(a, 4 + j, (*chip, 1 - c), me).wait_recv()
        for a in range(na):
            for cp in firsts(a) + [copy(a, 4 + j, (*chip, c), sibling) for j, chip in enumerate(chips)]:
                cp.wait_send()
            local(a).wait()

    return start, forward, finish


def _ag_sems(na):
    return [pltpu.SemaphoreType.DMA((7 * na,)), pltpu.SemaphoreType.DMA((7 * na,)), pltpu.SemaphoreType.DMA((na,))]


def _local_step(x, target, mod, g_mix, g_ffn, g_fin, prm, w_in_shard, shards):
    fulls = [(HEADS, HB, HB), (HEADS, HB, HB), (D, D), (NDEV, FB, D), (DFF, D)]
    slicers = [_blk_heads, _blk_heads, _blk_rows(D // NDEV), _blk_lead, _blk_rows(DFF // NDEV)]
    my_chip = _my_index() >> 1
    own_first = (my_chip ^ jnp.arange(NCHIP, dtype=jnp.int32)).astype(jnp.int32)
    early, late = [0, 1, 2, 4], [3]
    pick = lambda lst, idx: [lst[i] for i in idx]
    proj, h, w_in, (wa, wx, w_out, w_down) = _in_proj(x, mod, g_mix, w_in_shard, own_first, pick(shards, early),
                                                      pick(fulls, early), pick(slicers, early))
    merged, hl, sv, (w_gu,) = _mixer_fwd(proj, prm, wa, wx, pick(shards, late), pick(fulls, late),
                                         pick(slicers, late))
    w_gu = w_gu.reshape(2, 4, FB, D)
    x1, h2 = _out_proj(merged, x, mod, g_ffn, w_out)
    gu, dx2, dx2b, loss, d_gfin = _ffn_fwd(h2, x1, target, mod, g_fin, w_gu, w_down)
    dgu, act, dx1, dx1b, dmg, sums2 = _ffn_bwd(dx2, gu, x1, mod, g_ffn, w_gu, w_down, w_out)
    chip_order = _xor_order(_my_index() >> 1, NCHIP)
    p_wgu = _gu_wgrad(h2, dgu, chip_order)
    p_wdown, d_gt2 = _scaled_wgrad("down_wgrad", act, dx2b, w_down, 5, mod, chip_order)
    p_wout, d_gt1 = _scaled_wgrad("out_wgrad", merged, dx1b, w_out, 2, mod, chip_order)
    dproj, msums, g_wa, g_wx = _mixer_bwd(proj, hl, sv, dmg, prm, wa, wx)
    p_win, p_wa, p_wx = _in_wgrad(h, dproj, g_wa, g_wx, chip_order)
    grad_x, sums1 = _in_proj_bwd(dproj, w_in, x, dx1, mod, g_mix)
    return dict(loss=loss, grad_x=grad_x, d_gfin=d_gfin, sums1=sums1, sums2=sums2, msums=msums,
                d_gt1=d_gt1, d_gt2=d_gt2, p_win=p_win, p_wa=p_wa, p_wx=p_wx, p_wout=p_wout, p_wgu=p_wgu,
                p_wdown=p_wdown)


def kernel(x, c, w_ada, b_ada, g_norm_mix, w_in, conv_a_w, conv_b_w, conv_b_bias, w_rg_a, b_rg_a, w_rg_x, b_rg_x, lru_lambda, w_out, g_norm_ffn, w_gate_up, w_down, g_norm_final, loss_target, m_w_ada, m_b_ada, m_g_norm_mix, m_w_in, m_conv_a_w, m_conv_b_w, m_conv_b_bias, m_w_rg_a, m_b_rg_a, m_w_rg_x, m_b_rg_x, m_lru_lambda, m_w_out, m_g_norm_ffn, m_w_gate_up, m_w_down, m_g_norm_final, v_w_ada, v_b_ada, v_g_norm_mix, v_w_in, v_conv_a_w, v_conv_b_w, v_conv_b_bias, v_w_rg_a, v_b_rg_a, v_w_rg_x, v_b_rg_x, v_lru_lambda, v_w_out, v_g_norm_ffn, v_w_gate_up, v_w_down, v_g_norm_final):
    me = 4 * lax.axis_index("x") + 2 * lax.axis_index("y") + lax.axis_index("c")
    ncol = w_ada.shape[2]
    cw = conv_a_w.shape[2]

    pack0 = jnp.concatenate([c, conv_a_w.reshape(1, 3 * cw), conv_b_w.reshape(1, 4 * cw)], axis=1)
    b_cols = lax.dynamic_slice_in_dim(b_ada, me * ncol, ncol, axis=1)
    got0, got1 = _ada_mod(jnp.broadcast_to(pack0, (8, pack0.shape[1])), w_ada[0], b_cols)
    got0 = got0.reshape(NDEV, 8, -1)[:, 0, :]
    c_all = got0[:, :D]
    conv_a = got0[:, D:D + 3 * cw].reshape(NDEV, 3, cw).transpose(1, 0, 2).reshape(3, D)
    conv_b = got0[:, D + 3 * cw:].reshape(NDEV, 4, cw).transpose(1, 0, 2).reshape(4, D)
    c16 = jnp.concatenate([c_all, jnp.zeros((8, D), F32)], axis=0)
    mod6 = lax.dynamic_index_in_dim(got1.reshape(NDEV, NDEV, ncol), me, axis=1, keepdims=False).reshape(6, D)
    mod = jnp.concatenate([mod6, jnp.zeros((2, D), F32)], axis=0)

    tr = lambda a: jnp.swapaxes(a, 1, 2)
    shards = [w_rg_a[0].astype(BF16), w_rg_x[0].astype(BF16), w_out[0].astype(BF16), tr(w_gate_up)[0].astype(BF16),
              w_down[0].astype(BF16)]

    prm = jnp.concatenate([conv_a, conv_b, conv_b_bias, b_rg_a, b_rg_x, lru_lambda, jnp.zeros((5, D), F32)], axis=0)
    r = _local_step(x[0], loss_target[0], mod, g_norm_mix, g_norm_ffn, g_norm_final.reshape(1, D), prm,
                    w_in[0].astype(BF16), shards)

    parts = [r["p_win"], r["p_wa"], r["p_wx"], r["p_wout"], r["p_wgu"], r["p_wdown"]]
    big = {}
    for nm, p, w, m, v in (("w_in", parts[0], w_in, m_w_in, v_w_in), ("w_rg_a", parts[1], w_rg_a, m_w_rg_a, v_w_rg_a),
                           ("w_rg_x", parts[2], w_rg_x, m_w_rg_x, v_w_rg_x), ("w_out", parts[3], w_out, m_w_out, v_w_out),
                           ("w_gate_up", parts[4], tr(w_gate_up), tr(m_w_gate_up), tr(v_w_gate_up)),
                           ("w_down", parts[5], w_down, m_w_down, v_w_down)):
        two_d = (-1, w.shape[-1])
        outs = _adam("adam_" + nm, p.reshape((p.shape[0],) + w.reshape(two_d).shape), w.reshape(two_d), m.reshape(two_d),
                     v.reshape(two_d))
        big[nm] = [o.reshape(w.shape) for o in outs]
    big["w_gate_up"] = [tr(o) for o in big["w_gate_up"]]

    got2 = _gather_small_grads(r["sums1"], r["sums2"], r["msums"], r["d_gt1"], r["d_gt2"], r["d_gfin"],
                               r["loss"]).reshape(NDEV, 24, D)

    rep_w = jnp.concatenate([b_ada.reshape(6, D), g_norm_mix, conv_b_bias, b_rg_a, b_rg_x, lru_lambda, g_norm_ffn,
                             g_norm_final.reshape(1, D), jnp.zeros((3, D), F32)], axis=0)
    rep_m = jnp.concatenate([m_b_ada.reshape(6, D), m_g_norm_mix, m_conv_b_bias, m_b_rg_a, m_b_rg_x, m_lru_lambda,
                             m_g_norm_ffn, m_g_norm_final.reshape(1, D), jnp.zeros((3, D), F32)], axis=0)
    rep_v = jnp.concatenate([v_b_ada.reshape(6, D), v_g_norm_mix, v_conv_b_bias, v_b_rg_a, v_b_rg_x, v_lru_lambda,
                             v_g_norm_ffn, v_g_norm_final.reshape(1, D), jnp.ones((3, D), F32)], axis=0)
    rep = _adam("adam_rep", got2[:, :16, :], rep_w, rep_m, rep_v)

    conv_parts = lax.dynamic_slice_in_dim(got2[:, 13:21, :], me * cw, cw, axis=2)
    cv_w = jnp.concatenate([conv_a_w[0], conv_b_w[0], jnp.zeros((1, cw), F32)], axis=0)
    cv_m = jnp.concatenate([m_conv_a_w[0], m_conv_b_w[0], jnp.zeros((1, cw), F32)], axis=0)
    cv_v = jnp.concatenate([v_conv_a_w[0], v_conv_b_w[0], jnp.ones((1, cw), F32)], axis=0)
    cvo = _adam("adam_conv", conv_parts, cv_w, cv_m, cv_v)

    dmod_cols = lax.dynamic_slice_in_dim(got2[:, :6, :].reshape(NDEV, 6 * D), me * ncol, ncol, axis=1)
    dmod16 = jnp.concatenate([dmod_cols, jnp.zeros((8, ncol), F32)], axis=0)
    ada = _ada_bwd(c16, dmod16, w_ada[0], m_w_ada[0], v_w_ada[0])

    loss = jnp.sum(got2[:, 20, 0])

    def pick(q):
        one = lambda i: rep[q][i:i + 1]
        return [ada[q].reshape(w_ada.shape), rep[q][0:6].reshape(b_ada.shape), one(6), big["w_in"][q],
                cvo[q][0:3].reshape(conv_a_w.shape), cvo[q][3:7].reshape(conv_b_w.shape), one(7),
                big["w_rg_a"][q], one(8), big["w_rg_x"][q], one(9), one(10), big["w_out"][q], one(11),
                big["w_gate_up"][q], big["w_down"][q], rep[q][12]]

    return (loss, r["grad_x"].reshape(x.shape), *pick(0), *pick(1), *pick(2), *pick(3))
```

```python
import math

import jax
import jax.numpy as jnp
from jax import lax
from jax.experimental import pallas as pl
from jax.experimental.pallas import tpu as pltpu

F32 = jnp.float32
BF16 = jnp.bfloat16

D = 1024
DFF = 2816
NDEV = 8
HEADS = 4
HB = D // HEADS
FB = DFF // 4
EPS = 1e-6
LRU_C = 8.0
ADAM_LR, ADAM_B1, ADAM_B2, ADAM_EPS, ADAM_WD, ADAM_STEP = 0.001, 0.9, 0.999, 1e-08, 0.01, 10

VMEM_LIMIT = 56 * 1024 * 1024
TM = 512
TMI = 1024
TMF = 256
TK = 2048
TKI = 2048
SUB = 256
TT = 256
CG = 256
MESH = pl.DeviceIdType.MESH


def _cp(*sem):
    return pltpu.CompilerParams(dimension_semantics=sem, vmem_limit_bytes=VMEM_LIMIT)


def _sig(x):
    return 1.0 / (1.0 + jnp.exp(-x))


def _log_sigmoid(x):
    z = jnp.exp(-jnp.abs(x))
    u = 1.0 + z
    d = u - 1.0
    l1p = jnp.where(d == 0.0, z, jnp.log(u) * (z / jnp.where(d == 0.0, 1.0, d)))
    return -(jnp.maximum(-x, 0.0) + l1p)


def _neg_expm1(x):
    p = x * (1.0 + x * 0.5 * (1.0 + x * (1.0 / 3.0) * (1.0 + x * 0.25 * (1.0 + x * 0.2 * (1.0 + x * (1.0 / 6.0))))))
    return jnp.where(x > -0.25, -p, 1.0 - jnp.exp(x))


_GC = math.sqrt(2.0 / math.pi)


def _gelu(x):
    t = jnp.tanh(_GC * (x + 0.044715 * x * x * x))
    return 0.5 * x * (1.0 + t), t


def _dot(a, b):
    return jnp.dot(a, b, preferred_element_type=F32)


def _dot_nt(a, b):
    return lax.dot_general(a, b, (((1,), (1,)), ((), ())), preferred_element_type=F32)


def _dot_tn(a, b):
    return lax.dot_general(a, b, (((0,), (0,)), ((), ())), preferred_element_type=F32)


def _resident(shape):
    return pl.BlockSpec(shape, lambda *_: (0,) * len(shape), pipeline_mode=pl.Buffered(1))


def _sub_blocks(n_rows):
    step = min(SUB, n_rows)
    return [slice(r, r + step) for r in range(0, n_rows, step)]


def _fold8(v):
    return v[0:8] + v[8:16]


def _pj(ref, s, rows=slice(None), cols=slice(0, D)):
    return ref[rows, s * D + cols.start:s * D + cols.stop]


def _in_proj(x, mod, g_mix, w_shard, order, shards, fulls, slicers):
    t_len = x.shape[0]
    tm = min(TMI, t_len)
    ni = t_len // tm
    na = len(shards)
    cw = 7 * D // NDEV
    rc = 32

    def body(ord_ref, x_ref, mod_ref, g_ref, wsh_ref, *rest):
        ins, (proj_ref, h_ref, wfull_ref), outs = rest[:na], rest[na:na + 3], rest[na + 3:2 * na + 3]
        h_scr, w_scr, wsend, wrecv, wlocal, wout = rest[2 * na + 3:2 * na + 9]
        start, forward, finish = _ag_phases(ins, outs, slicers, *rest[2 * na + 9:])
        p, i = pl.program_id(0), pl.program_id(1)
        x_, y_, c = _my_pos()
        me, sibling = (x_, y_, c), (x_, y_, 1 - c)
        chip_at = [None, (x_, 1 - y_), (1 - x_, y_), (1 - x_, 1 - y_)]

        def cols(px, py, pc):
            return w_scr.at[:, pl.ds(pl.multiple_of((4 * px + 2 * py + pc) * cw, 128), cw)]

        def wcopy(k, block, to, from_shard=False):
            dst = cols(*block)
            return pltpu.make_async_remote_copy(src_ref=wsh_ref if from_shard else dst, dst_ref=dst,
                                                send_sem=wsend.at[k], recv_sem=wrecv.at[k], device_id=to,
                                                device_id_type=MESH)

        own_local = pltpu.make_async_copy(wsh_ref, cols(*me), wlocal)
        to_hbm = pltpu.make_async_copy(w_scr, wfull_ref, wout)

        @pl.when((p == 0) & (i == 0))
        def _():
            own_local.start()
            wcopy(0, me, sibling, True).start()
            for q in (1, 2):
                wcopy(q, me, (*chip_at[q], c), True).start()
            own_local.wait()
            wcopy(0, sibling, me).wait_recv()

        @pl.when((p == 0) & (i == ni // 2))
        def _():
            wcopy(3, me, (*chip_at[3], c), True).start()

        for q in (1, 2, 3):
            @pl.when((p == q - 1) & (i == ni - 1))
            def _():
                wcopy(q, (*chip_at[q], c), me).wait_recv()
                wcopy(3 + q, (*chip_at[q], c), sibling).start()

            @pl.when((p == q) & (i == 0))
            def _():
                wcopy(3 + q, (*chip_at[q], 1 - c), me).wait_recv()

        @pl.when((p == 1) & (i == 0))
        def _():
            start()

        @pl.when((p == NCHIP - 1) & (i == ni // 2))
        def _():
            forward()

        @pl.when((p == NCHIP - 1) & (i == 0))
        def _():
            to_hbm.start()

        gs = g_ref[...] * (1.0 + mod_ref[1:2, :])
        sh = mod_ref[0:1, :]

        wcols = pl.ds(pl.multiple_of(ord_ref[p] * (2 * cw), 128), 2 * cw)
        for sub in _sub_blocks(tm):
            for r0 in range(sub.start, sub.stop, rc):
                xv = x_ref[r0:r0 + rc, :]
                r = lax.rsqrt(jnp.mean(xv * xv, axis=-1, keepdims=True) + EPS)
                h_scr[r0:r0 + rc, :] = (xv * r * gs + sh).astype(BF16)
            proj_ref[sub, :] = _dot(h_scr[sub, :], w_scr[:, wcols]).astype(BF16)

        @pl.when(p == 0)
        def _():
            h_ref[...] = h_scr[...]

        @pl.when((p == NCHIP - 1) & (i == ni - 1))
        def _():
            wcopy(0, me, sibling, True).wait_send()
            for q in (1, 2, 3):
                wcopy(q, me, (*chip_at[q], c), True).wait_send()
                wcopy(3 + q, (*chip_at[q], c), sibling).wait_send()
            finish()
            to_hbm.wait()

    res = pl.pallas_call(
        body, name="in_proj",
        grid_spec=pltpu.PrefetchScalarGridSpec(
            num_scalar_prefetch=1, grid=(NCHIP, ni),
            in_specs=[pl.BlockSpec((tm, D), lambda p, i, o: (i, 0)),
                      pl.BlockSpec((8, D), lambda p, i, o: (0, 0)),
                      pl.BlockSpec((1, D), lambda p, i, o: (0, 0))] + [_ANY] * (1 + na),
            out_specs=[pl.BlockSpec((tm, 2 * cw), lambda p, i, o: (i, o[p])),
                       pl.BlockSpec((tm, D), lambda p, i, o: (jnp.where(p == 0, i, ni - 1), 0))]
            + [_ANY] * (1 + na),
            scratch_shapes=[pltpu.VMEM((tm, D), BF16), pltpu.VMEM((D, 7 * D), BF16),
                            pltpu.SemaphoreType.DMA((7,)), pltpu.SemaphoreType.DMA((7,)),
                            pltpu.SemaphoreType.DMA, pltpu.SemaphoreType.DMA] + _ag_sems(na)),
        out_shape=[jax.ShapeDtypeStruct((t_len, 7 * D), BF16), jax.ShapeDtypeStruct((t_len, D), BF16),
                   jax.ShapeDtypeStruct((D, 7 * D), BF16)]
        + [jax.ShapeDtypeStruct(f, sh.dtype) for f, sh in zip(fulls, shards)],
        compiler_params=_cp("arbitrary", "arbitrary"),
    )(order, x, mod, g_mix, w_shard, *shards)
    return res[0], res[1], res[2], res[3:]


P_WA, P_WB, P_CBIAS, P_BA, P_BX, P_LAM = 0, 3, 7, 8, 9, 10
SV_PLANES = SV_U, SV_YA, SV_R, SV_I, SV_A, SV_MULT = range(6)


def _lru_gates(rp, ip, ls, first_row):
    r = _sig(rp)
    ig = _sig(ip)
    la = LRU_C * r * ls
    a = jnp.exp(la)
    m2 = _neg_expm1(2.0 * la)
    mult = jnp.where(first_row, 1.0, jnp.sqrt(jnp.maximum(m2, 0.0)))
    return r, ig, la, a, m2, mult


def _shift_down(cur, prev, s, row):
    return jnp.where(row >= s, pltpu.roll(cur, s, 0), pltpu.roll(prev, s, 0))


def _shift_up(cur, nxt, s, row):
    return jnp.where(row < 8 - s, pltpu.roll(cur, 8 - s, 0), pltpu.roll(nxt, 8 - s, 0))


def _conv_fwd_rows(tt, proj_ref, prm_ref, xe, ve, u_s, ub_s, ya_s):
    row = lax.broadcasted_iota(jnp.int32, (8, CG), 0)
    w_b = [prm_ref[P_WB + k:P_WB + k + 1, :] for k in range(4)]
    w_a = [prm_ref[P_WA + k:P_WA + k + 1, :] for k in range(3)]
    bias = prm_ref[P_CBIAS:P_CBIAS + 1, :]

    def blk(ib, carry):
        r0 = pl.multiple_of(ib * 16, 16)
        rows = pl.ds(r0, 16)
        for g in range(D // CG):
            cs = slice(g * CG, (g + 1) * CG)
            x16 = _pj(proj_ref, 3, rows, cs).astype(F32)
            v16 = _pj(proj_ref, 1, rows, cs).astype(F32) * _pj(proj_ref, 2, rows, cs).astype(F32)
            xp = xe[pl.ds(r0, 8), cs]
            vp = ve[pl.ds(r0, 8), cs]
            xe[pl.ds(r0 + 8, 16), cs] = x16
            ve[pl.ds(r0 + 8, 16), cs] = v16
            us, yas = [], []
            for sb in range(2):
                xc, vc = x16[8 * sb:8 * sb + 8], v16[8 * sb:8 * sb + 8]
                u8 = bias[:, cs] + w_b[3][:, cs] * xc
                for s in (1, 2, 3):
                    u8 = u8 + w_b[3 - s][:, cs] * _shift_down(xc, xp, s, row)
                y8 = w_a[2][:, cs] * vc
                for s in (1, 2):
                    y8 = y8 + w_a[2 - s][:, cs] * _shift_down(vc, vp, s, row)
                us.append(u8)
                yas.append(y8)
                xp, vp = xc, vc
            u16 = jnp.concatenate(us, axis=0)
            u_s[rows, cs] = u16
            ub_s[rows, cs] = u16.astype(BF16)
            ya_s[rows, cs] = jnp.concatenate(yas, axis=0)
        return carry

    lax.fori_loop(0, tt // 16, blk, 0)


def _mixer_fwd(proj, prm, wa, wx, shards, fulls, slicers):
    t_len = proj.shape[0]
    tt = min(TT, t_len)
    nt = t_len // tt
    na = len(shards)

    def body(proj_ref, prm_ref, wa_ref, wx_ref, *rest):
        ins, (mg_ref, hl_ref, sv_ref), outs = rest[:na], rest[na:na + 3], rest[na + 3:2 * na + 3]
        xe, ve, hc, rp_s, ip_s, ub_s = rest[2 * na + 3:2 * na + 9]
        start, forward, finish = _ag_phases(ins, outs, slicers, *rest[2 * na + 9:])
        t = pl.program_id(0)

        @pl.when(t == 0)
        def _():
            start()
            xe[0:8, :] = jnp.zeros((8, D), F32)
            ve[0:8, :] = jnp.zeros((8, D), F32)
            hc[...] = jnp.zeros((8, D), F32)

        @pl.when(t == (3 * nt) // 4)
        def _():
            forward()

        _conv_fwd_rows(tt, proj_ref, prm_ref, xe, ve, sv_ref.at[SV_U], ub_s, sv_ref.at[SV_YA])
        xe[0:8, :] = xe[tt:tt + 8, :]
        ve[0:8, :] = ve[tt:tt + 8, :]

        ub = ub_s[...]
        for h in range(HEADS):
            cs = slice(h * HB, (h + 1) * HB)
            rp_s[:, cs] = _dot(ub[:, cs], wa_ref[h]) + prm_ref[P_BA:P_BA + 1, cs]
            ip_s[:, cs] = _dot(ub[:, cs], wx_ref[h]) + prm_ref[P_BX:P_BX + 1, cs]

        ls_all = _log_sigmoid(prm_ref[P_LAM:P_LAM + 1, :])
        row = lax.broadcasted_iota(jnp.int32, (8, CG), 0)

        def blk(i, carry):
            r0 = pl.multiple_of(i * 16, 16)
            for g in range(D // CG):
                cs = slice(g * CG, (g + 1) * CG)
                ls = ls_all[:, cs]
                hprev = hc[:, cs]
                hs = []
                for sb in range(2):
                    rr = r0 + 8 * sb
                    first = (row + (t * tt + rr)) == 0
                    r8 = pl.ds(rr, 8)
                    r, ig, _, a, _, mult = _lru_gates(rp_s[r8, cs], ip_s[r8, cs], ls, first)
                    for plane, val in ((SV_R, r), (SV_I, ig), (SV_A, a), (SV_MULT, mult)):
                        sv_ref[plane, r8, cs] = val
                    b = mult * (ig * sv_ref[SV_U, r8, cs])
                    for s in (1, 2, 4):
                        a_sh = jnp.where(row >= s, pltpu.roll(a, s, 0), 1.0)
                        b_sh = jnp.where(row >= s, pltpu.roll(b, s, 0), 0.0)
                        b = a * b_sh + b
                        a = a * a_sh
                    hv = a * hprev + b
                    hprev = jnp.broadcast_to(hv[7:8, :], hv.shape)
                    hs.append(hv)
                hc[:, cs] = hprev
                h16 = jnp.concatenate(hs, axis=0)
                rows = pl.ds(r0, 16)
                gl, _ = _gelu(_pj(proj_ref, 4, rows, cs).astype(F32))
                y_b = h16 * gl
                y_a = _pj(proj_ref, 0, rows, cs).astype(F32) * sv_ref[SV_YA, rows, cs]
                mg = (_sig(_pj(proj_ref, 5, rows, cs).astype(F32)) * y_a
                      + _sig(_pj(proj_ref, 6, rows, cs).astype(F32)) * y_b)
                mg_ref[rows, cs] = mg.astype(BF16)
                hl_ref[rows, cs] = h16.astype(BF16)
            return carry

        lax.fori_loop(0, tt // 16, blk, 0)

        @pl.when(t == nt - 1)
        def _():
            finish()

    res = pl.pallas_call(
        body, name="mixer_fwd", grid=(nt,),
        in_specs=[pl.BlockSpec((tt, 7 * D), lambda t: (t, 0)),
                  pl.BlockSpec((16, D), lambda t: (0, 0)),
                  pl.BlockSpec((HEADS, HB, HB), lambda t: (0, 0, 0)),
                  pl.BlockSpec((HEADS, HB, HB), lambda t: (0, 0, 0))] + [_ANY] * na,
        out_specs=[pl.BlockSpec((tt, D), lambda t: (t, 0)), pl.BlockSpec((tt, D), lambda t: (t, 0)),
                   pl.BlockSpec((len(SV_PLANES), tt, D), lambda t: (0, t, 0))] + [_ANY] * na,
        out_shape=[jax.ShapeDtypeStruct((t_len, D), BF16), jax.ShapeDtypeStruct((t_len, D), BF16),
                   jax.ShapeDtypeStruct((len(SV_PLANES), t_len, D), F32)]
        + [jax.ShapeDtypeStruct(f, sh.dtype) for f, sh in zip(fulls, shards)],
        scratch_shapes=[pltpu.VMEM((tt + 8, D), F32), pltpu.VMEM((tt + 8, D), F32), pltpu.VMEM((8, D), F32),
                        pltpu.VMEM((tt, D), F32), pltpu.VMEM((tt, D), F32), pltpu.VMEM((tt, D), BF16)]
        + _ag_sems(na),
        compiler_params=_cp("arbitrary"),
    )(proj, prm, wa, wx, *shards)
    return res[0], res[1], res[2], res[3:]


def _out_proj(merged, x, mod, g_ffn, w_out):
    t_len = x.shape[0]
    tm = min(TMI, t_len)

    def body(mg_ref, x_ref, mod_ref, g_ref, w_ref, x1_ref, h2_ref):
        gt1 = mod_ref[2:3, :]
        gs = g_ref[...] * (1.0 + mod_ref[4:5, :])
        sh = mod_ref[3:4, :]
        for sub in _sub_blocks(tm):
            x1_ref[sub, :] = x_ref[sub, :] + gt1 * _dot(mg_ref[sub, :], w_ref[...])
            for r0 in range(sub.start, sub.stop, 16):
                x1 = x1_ref[r0:r0 + 16, :]
                r = lax.rsqrt(jnp.mean(x1 * x1, axis=-1, keepdims=True) + EPS)
                h2_ref[r0:r0 + 16, :] = (x1 * r * gs + sh).astype(BF16)

    return pl.pallas_call(
        body, name="out_proj", grid=(t_len // tm,),
        in_specs=[pl.BlockSpec((tm, D), lambda i: (i, 0)), pl.BlockSpec((tm, D), lambda i: (i, 0)),
                  pl.BlockSpec((8, D), lambda i: (0, 0)), pl.BlockSpec((1, D), lambda i: (0, 0)),
                  pl.BlockSpec((D, D), lambda i: (0, 0))],
        out_specs=[pl.BlockSpec((tm, D), lambda i: (i, 0)), pl.BlockSpec((tm, D), lambda i: (i, 0))],
        out_shape=[jax.ShapeDtypeStruct((t_len, D), F32), jax.ShapeDtypeStruct((t_len, D), BF16)],
        compiler_params=_cp("parallel"),
    )(merged, x, mod, g_ffn, w_out)


def _ffn_fwd(h2, x1, target, mod, g_fin, w_gu, w_down):
    t_len = x1.shape[0]
    tm = min(TMF, t_len)

    def body(h2_ref, x1_ref, tg_ref, mod_ref, g_ref, wgu_ref, wd_ref, gu_ref, dx2_ref, dx2b_ref, loss_ref, dg_ref, acc):
        @pl.when(pl.program_id(0) == 0)
        def _():
            loss_ref[...] = jnp.zeros_like(loss_ref)
            dg_ref[...] = jnp.zeros_like(dg_ref)

        hb = h2_ref[...]
        ffn = None
        nxt = (_dot_nt(hb, wgu_ref[0, 0]), _dot_nt(hb, wgu_ref[1, 0]))
        for j in range(4):
            gate, up = nxt
            if j < 3:
                nxt = (_dot_nt(hb, wgu_ref[0, j + 1]), _dot_nt(hb, wgu_ref[1, j + 1]))
            gu_ref[0, j] = gate.astype(BF16)
            gu_ref[1, j] = up.astype(BF16)
            act = (gate * _sig(gate) * up).astype(BF16)
            part = _dot(act, wd_ref[j * FB:(j + 1) * FB, :])
            ffn = part if ffn is None else ffn + part
        acc[...] = ffn

        gt2 = mod_ref[5:6, :]
        gf = g_ref[...]

        s_loss = s_dg = jnp.zeros((8, D), F32)
        for r0 in range(0, tm, 16):
            rows = slice(r0, r0 + 16)
            x2 = x1_ref[rows, :] + gt2 * acc[rows, :]
            r = lax.rsqrt(jnp.mean(x2 * x2, axis=-1, keepdims=True) + EPS)
            xn = x2 * r
            diff = xn * gf - tg_ref[rows, :]
            dy = diff * (1.0 / D)
            dxn = dy * gf
            dx2 = r * (dxn - xn * jnp.mean(dxn * xn, axis=-1, keepdims=True))
            dx2_ref[rows, :] = dx2
            dx2b_ref[rows, :] = dx2.astype(BF16)
            s_loss, s_dg = s_loss + _fold8(diff * diff), s_dg + _fold8(dy * xn)
        loss_ref[...] += jnp.sum(s_loss) * (0.5 / D)
        dg_ref[...] += jnp.sum(s_dg, axis=0, keepdims=True)

    row = pl.BlockSpec((tm, D), lambda i: (i, 0))
    return pl.pallas_call(
        body, name="ffn_fwd", grid=(t_len // tm,),
        in_specs=[row, row, row, pl.BlockSpec((8, D), lambda i: (0, 0)), pl.BlockSpec((1, D), lambda i: (0, 0)),
                  _resident((2, 4, FB, D)), _resident((DFF, D))],
        out_specs=[pl.BlockSpec((2, 4, tm, FB), lambda i: (0, 0, i, 0)), row, row,
                   pl.BlockSpec((1, 128), lambda i: (0, 0)), pl.BlockSpec((1, D), lambda i: (0, 0))],
        out_shape=[jax.ShapeDtypeStruct((2, 4, t_len, FB), BF16), jax.ShapeDtypeStruct((t_len, D), F32),
                   jax.ShapeDtypeStruct((t_len, D), BF16),
                   jax.ShapeDtypeStruct((1, 128), F32), jax.ShapeDtypeStruct((1, D), F32)],
        scratch_shapes=[pltpu.VMEM((tm, D), F32)],
        compiler_params=_cp("arbitrary"),
    )(h2, x1, target, mod, g_fin, w_gu, w_down)


S_SH, S_SC, S_G = 0, 1, 2


def _norm_bwd_rows(span, sums, dh_ref, x_ref, dres_ref, scale, gain, write):
    gs = 1.0 + scale
    s_sh, s_sc, s_g = sums
    for r0 in range(span.start, span.stop, 16):
        rows = slice(r0, r0 + 16)
        dh = dh_ref[rows, :]
        xv = x_ref[rows, :]
        r = lax.rsqrt(jnp.mean(xv * xv, axis=-1, keepdims=True) + EPS)
        xn = xv * r
        dhn = dh * gs
        dxn = dhn * gain
        write(rows, dres_ref[rows, :] + r * (dxn - xn * jnp.mean(dxn * xn, axis=-1, keepdims=True)))
        s_sh, s_sc, s_g = s_sh + _fold8(dh), s_sc + _fold8(dh * (xn * gain)), s_g + _fold8(dhn * xn)
    return s_sh, s_sc, s_g


def _add_norm_sums(sums_ref, sums):
    for dst, s in zip((S_SH, S_SC, S_G), sums):
        sums_ref[dst:dst + 1, :] += jnp.sum(s, axis=0, keepdims=True)


def _ffn_bwd(dx2, gu, x1, mod, g_ffn, w_gu, w_down, w_out):
    t_len = x1.shape[0]
    tm = min(TMF, t_len)

    def body(dx2_ref, gu_ref, x1_ref, mod_ref, g_ref, wgu_ref, wd_ref, wo_ref,
             dgu_ref, act_ref, dx1_ref, dx1b_ref, dmg_ref, sums_ref, acc, dmo, dact_s):
        @pl.when(pl.program_id(0) == 0)
        def _():
            sums_ref[...] = jnp.zeros_like(sums_ref)

        dffn = (dx2_ref[...] * mod_ref[5:6, :]).astype(BF16)
        dact_s[0] = _dot_nt(dffn, wd_ref[0:FB, :])
        for j in range(4):
            if j < 3:
                dact_s[(j + 1) % 2] = _dot_nt(dffn, wd_ref[(j + 1) * FB:(j + 2) * FB, :])
            for r0 in range(0, tm, 16):
                rows = slice(r0, r0 + 16)
                dact = dact_s[j % 2, rows, :]
                gate = gu_ref[0, j, rows, :].astype(F32)
                up = gu_ref[1, j, rows, :].astype(F32)
                sg = _sig(gate)
                silu = gate * sg
                act_ref[j, rows, :] = (silu * up).astype(BF16)
                dgu_ref[0, j, rows, :] = (dact * up * (sg * (1.0 + gate * (1.0 - sg)))).astype(BF16)
                dgu_ref[1, j, rows, :] = (dact * silu).astype(BF16)
            part = _dot(dgu_ref[0, j], wgu_ref[0, j]) + _dot(dgu_ref[1, j], wgu_ref[1, j])
            if j == 0:
                acc[...] = part
            else:
                acc[...] += part

        gt1 = mod_ref[2:3, :]

        def write(rows, dx1):
            dx1_ref[rows, :] = dx1
            dx1b_ref[rows, :] = dx1.astype(BF16)
            dmo[rows, :] = (dx1 * gt1).astype(BF16)

        zero = jnp.zeros((8, D), F32)
        sums = (zero, zero, zero)
        for sub in (slice(0, tm // 2), slice(tm // 2, tm)):
            sums = _norm_bwd_rows(sub, sums, acc, x1_ref, dx2_ref, mod_ref[4:5, :], g_ref[...], write)
            dmg_ref[sub, :] = _dot_nt(dmo[sub, :], wo_ref[...]).astype(BF16)
        _add_norm_sums(sums_ref, sums)

    row = pl.BlockSpec((tm, D), lambda i: (i, 0))
    return pl.pallas_call(
        body, name="ffn_bwd", grid=(t_len // tm,),
        in_specs=[row, pl.BlockSpec((2, 4, tm, FB), lambda i: (0, 0, i, 0)), row,
                  pl.BlockSpec((8, D), lambda i: (0, 0)), pl.BlockSpec((1, D), lambda i: (0, 0)),
                  _resident((2, 4, FB, D)), _resident((DFF, D)), _resident((D, D))],
        out_specs=[pl.BlockSpec((2, 4, tm, FB), lambda i: (0, 0, i, 0)),
                   pl.BlockSpec((4, tm, FB), lambda i: (0, i, 0)), row, row, row,
                   pl.BlockSpec((8, D), lambda i: (0, 0))],
        out_shape=[jax.ShapeDtypeStruct((2, 4, t_len, FB), BF16), jax.ShapeDtypeStruct((4, t_len, FB), BF16),
                   jax.ShapeDtypeStruct((t_len, D), F32), jax.ShapeDtypeStruct((t_len, D), BF16),
                   jax.ShapeDtypeStruct((t_len, D), BF16), jax.ShapeDtypeStruct((8, D), F32)],
        scratch_shapes=[pltpu.VMEM((tm, D), F32), pltpu.VMEM((tm, D), BF16), pltpu.VMEM((2, tm, FB), F32)],
        compiler_params=_cp("arbitrary"),
    )(dx2, gu, x1, mod, g_ffn, w_gu, w_down, w_out)


def _my_pos():
    return lax.axis_index("x"), lax.axis_index("y"), lax.axis_index("c")


def _my_index():
    x, y, c = _my_pos()
    return 4 * x + 2 * y + c


def _device_of(b):
    return (b >> 2) & 1, (b >> 1) & 1, b & 1


def _rs_send(src, parts_ref, b, send_sems, recv_sems, local_sem):
    me = _my_index()
    dst = parts_ref.at[me]

    @pl.when(b == me)
    def _():
        pltpu.make_async_copy(src, dst, local_sem).start()

    @pl.when(b != me)
    def _():
        pltpu.make_async_remote_copy(src_ref=src, dst_ref=dst, send_sem=send_sems.at[b], recv_sem=recv_sems.at[me],
                                     device_id=_device_of(b), device_id_type=MESH).start()


def _rs_finish(src_of, parts_ref, send_sems, recv_sems, local_sem):
    me = _my_index()
    for s in range(NDEV):
        @pl.when(s != me)
        def _():
            cp = pltpu.make_async_remote_copy(src_ref=src_of(s), dst_ref=parts_ref.at[s], send_sem=send_sems.at[s],
                                              recv_sem=recv_sems.at[s], device_id=_device_of(s), device_id_type=MESH)
            cp.wait_send()
            cp.wait_recv()

        @pl.when(s == me)
        def _():
            pltpu.make_async_copy(src_of(s), parts_ref.at[s], local_sem).wait()


_RS_SEMS = [pltpu.SemaphoreType.DMA((NDEV,)), pltpu.SemaphoreType.DMA((NDEV,)), pltpu.SemaphoreType.DMA]
_ANY = pl.BlockSpec(memory_space=pl.ANY)


def _xor_order(me, n):
    return (me ^ (n - 1 - jnp.arange(n, dtype=jnp.int32))).astype(jnp.int32)


NCHIP = NDEV // 2


def _rs2_scratch(half_shape):
    blocks = lambda *lead: pltpu.VMEM(lead + tuple(half_shape), BF16)
    return [blocks(NCHIP, 2), blocks(NCHIP)] + [pltpu.SemaphoreType.DMA((NCHIP,))] * 4 + [pltpu.SemaphoreType.DMA]


def _rs2_to_sibling(q, rs):
    stage, from_sib, d_send, d_recv = rs[:4]
    x, y, c = _my_pos()
    pltpu.make_async_remote_copy(src_ref=stage.at[q, 1 - c], dst_ref=from_sib.at[q], send_sem=d_send.at[q],
                                 recv_sem=d_recv.at[q], device_id=(x, y, 1 - c), device_id_type=MESH).start()


def _rs2_forward(q, parts_ref, rs):
    stage, chip_sum, d_send, d_recv, i_send, i_recv, local_sem = rs
    x, y, c = _my_pos()
    my_chip = 2 * x + y
    pltpu.make_async_remote_copy(src_ref=stage.at[q, c], dst_ref=chip_sum.at[q], send_sem=d_send.at[q],
                                 recv_sem=d_recv.at[q], device_id=(x, y, 1 - c), device_id_type=MESH).wait_recv()
    chip_sum[q] = (stage[q, c].astype(F32) + chip_sum[q].astype(F32)).astype(BF16)

    @pl.when(q == my_chip)
    def _():
        pltpu.make_async_copy(chip_sum.at[q], parts_ref.at[my_chip], local_sem).start()

    @pl.when(q != my_chip)
    def _():
        pltpu.make_async_remote_copy(src_ref=chip_sum.at[q], dst_ref=parts_ref.at[my_chip], send_sem=i_send.at[q],
                                     recv_sem=i_recv.at[my_chip], device_id=((q >> 1) & 1, q & 1, c),
                                     device_id_type=MESH).start()


def _rs2_finish(parts_ref, rs):
    stage, chip_sum, d_send, d_recv, i_send, i_recv, local_sem = rs
    x, y, c = _my_pos()
    my_chip = 2 * x + y
    for q in range(NCHIP):
        pltpu.make_async_remote_copy(src_ref=stage.at[q, 1 - c], dst_ref=chip_sum.at[q], send_sem=d_send.at[q],
                                     recv_sem=d_recv.at[q], device_id=(x, y, 1 - c), device_id_type=MESH).wait_send()

        @pl.when(q != my_chip)
        def _():
            cp = pltpu.make_async_remote_copy(src_ref=chip_sum.at[q], dst_ref=parts_ref.at[q], send_sem=i_send.at[q],
                                              recv_sem=i_recv.at[q], device_id=((q >> 1) & 1, q & 1, c),
                                              device_id_type=MESH)
            cp.wait_send()
            cp.wait_recv()

        @pl.when(q == my_chip)
        def _():
            pltpu.make_async_copy(chip_sum.at[q], parts_ref.at[q], local_sem).wait()


def _gu_wgrad(h2, dgu, order):
    t_len = h2.shape[0]
    tk = min(TK, t_len)
    nk = t_len // tk

    def body(ord_ref, h_ref, d_ref, parts_ref, acc, *rs):
        p, k = pl.program_id(0), pl.program_id(1)

        @pl.when(k == 0)
        def _():
            acc[...] = jnp.zeros_like(acc)

        hb = h_ref[...]
        for half in range(2):
            acc[half] += _dot_tn(d_ref[0, half], hb)

        @pl.when(k == nk - 1)
        def _():
            q = ord_ref[p]
            rs[0][q] = acc[...].astype(BF16)
            _rs2_to_sibling(q, rs)

        @pl.when((k == min(1, nk - 1)) & (p > 0))
        def _():
            _rs2_forward(ord_ref[p - 1], parts_ref, rs)

        @pl.when((p == NCHIP - 1) & (k == nk - 1))
        def _():
            _rs2_forward(ord_ref[p], parts_ref, rs)
            _rs2_finish(parts_ref, rs)

    return pl.pallas_call(
        body, name="gu_wgrad",
        grid_spec=pltpu.PrefetchScalarGridSpec(
            num_scalar_prefetch=1, grid=(NCHIP, nk),
            in_specs=[pl.BlockSpec((tk, D), lambda p, k, o: (k, 0)),
                      pl.BlockSpec((1, 2, tk, FB), lambda p, k, o: (o[p], 0, k, 0))],
            out_specs=_ANY,
            scratch_shapes=[pltpu.VMEM((2, FB, D), F32)] + _rs2_scratch((FB, D))),
        out_shape=jax.ShapeDtypeStruct((NCHIP, FB, D), BF16),
        compiler_params=_cp("arbitrary", "arbitrary"),
    )(order, h2, dgu.reshape(NCHIP, 2, t_len, FB))


def _scaled_wgrad(name, a, dx, w, gate_row, mod, order):
    t_len = dx.shape[0]
    kb = w.shape[0] // NCHIP
    tk = min(TK, t_len)
    nk = t_len // tk
    rows = kb // 2
    if a.ndim == 3:
        a_spec = pl.BlockSpec((None, tk, kb), lambda p, k, o: (o[p], k, 0))
    else:
        a_spec = pl.BlockSpec((tk, kb), lambda p, k, o: (k, o[p]))

    def body(ord_ref, a_ref, dx_ref, w_ref, mod_ref, parts_ref, dg_ref, acc, *rs):
        p, k = pl.program_id(0), pl.program_id(1)

        @pl.when((p == 0) & (k == 0))
        def _():
            dg_ref[...] = jnp.zeros_like(dg_ref)

        @pl.when(k == 0)
        def _():
            acc[...] = jnp.zeros_like(acc)

        acc[...] += _dot_tn(a_ref[...], dx_ref[...])

        @pl.when(k == nk - 1)
        def _():
            q = ord_ref[p]
            z = acc[...]
            zg = (z * mod_ref[gate_row:gate_row + 1, :]).astype(BF16)
            dg_ref[0:1, :] += jnp.sum(z * w_ref[...].astype(F32), axis=0, keepdims=True)
            for half in range(2):
                rs[0][q, half] = zg[half * rows:(half + 1) * rows]
            _rs2_to_sibling(q, rs)

        @pl.when((k == min(1, nk - 1)) & (p > 0))
        def _():
            _rs2_forward(ord_ref[p - 1], parts_ref, rs)

        @pl.when((p == NCHIP - 1) & (k == nk - 1))
        def _():
            _rs2_forward(ord_ref[p], parts_ref, rs)
            _rs2_finish(parts_ref, rs)

    return pl.pallas_call(
        body, name=name,
        grid_spec=pltpu.PrefetchScalarGridSpec(
            num_scalar_prefetch=1, grid=(NCHIP, nk),
            in_specs=[a_spec,
                      pl.BlockSpec((tk, D), lambda p, k, o: (k, 0)),
                      pl.BlockSpec((kb, D), lambda p, k, o: (o[p], 0)),
                      pl.BlockSpec((8, D), lambda p, k, o: (0, 0))],
            out_specs=[_ANY, pl.BlockSpec((8, D), lambda p, k, o: (0, 0))],
            scratch_shapes=[pltpu.VMEM((kb, D), F32)] + _rs2_scratch((rows, D))),
        out_shape=[jax.ShapeDtypeStruct((NCHIP, rows, D), BF16), jax.ShapeDtypeStruct((8, D), F32)],
        compiler_params=_cp("arbitrary", "arbitrary"),
    )(order, a, dx, w, mod)


M_WA, M_WB, M_CBIAS, M_BA, M_BX, M_LS = 0, 3, 7, 8, 9, 10


def _conv_bwd_rows(tt, proj_ref, prm_ref, due, dye, dp_ref, acc8):
    row = lax.broadcasted_iota(jnp.int32, (8, CG), 0)
    w_b = [prm_ref[P_WB + k:P_WB + k + 1, :] for k in range(4)]
    w_a = [prm_ref[P_WA + k:P_WA + k + 1, :] for k in range(3)]

    def blk(ib, carry):
        r0 = pl.multiple_of(ib * 16, 16)
        rows = pl.ds(r0, 16)
        for g in range(D // CG):
            cs = slice(g * CG, (g + 1) * CG)
            du16, du_after = due[rows, cs], due[pl.ds(r0 + 16, 8), cs]
            dy16, dy_after = dye[rows, cs], dye[pl.ds(r0 + 16, 8), cs]
            cc16 = _pj(proj_ref, 1, rows, cs).astype(F32)
            cx16 = _pj(proj_ref, 2, rows, cs).astype(F32)
            x16 = _pj(proj_ref, 3, rows, cs).astype(F32)
            v16 = cc16 * cx16
            acc = [acc8[8 * k:8 * k + 8, cs] for k in range(8)]
            drx, dv = [], []
            for sb in range(2):
                lo = slice(8 * sb, 8 * sb + 8)
                duc, dyc, xc, vc = du16[lo], dy16[lo], x16[lo], v16[lo]
                du_n = du16[8:16] if sb == 0 else du_after
                dy_n = dy16[8:16] if sb == 0 else dy_after
                acc[0] = acc[0] + duc
                acc[4] = acc[4] + duc * xc
                d8 = w_b[3][:, cs] * duc
                for s in (1, 2, 3):
                    du_s = _shift_up(duc, du_n, s, row)
                    acc[4 - s] = acc[4 - s] + du_s * xc
                    d8 = d8 + w_b[3 - s][:, cs] * du_s
                acc[7] = acc[7] + dyc * vc
                e8 = w_a[2][:, cs] * dyc
                for s in (1, 2):
                    dy_s = _shift_up(dyc, dy_n, s, row)
                    acc[7 - s] = acc[7 - s] + dy_s * vc
                    e8 = e8 + w_a[2 - s][:, cs] * dy_s
                drx.append(d8)
                dv.append(e8)
            for k in range(8):
                acc8[8 * k:8 * k + 8, cs] = acc[k]
            dv16 = jnp.concatenate(dv, axis=0)
            col = lambda s: slice(s * D + g * CG, s * D + (g + 1) * CG)
            dp_ref[rows, col(3)] = jnp.concatenate(drx, axis=0).astype(BF16)
            dp_ref[rows, col(1)] = (dv16 * cx16).astype(BF16)
            dp_ref[rows, col(2)] = (dv16 * cc16).astype(BF16)
        return carry

    lax.fori_loop(0, tt // 16, blk, 0)


def _mixer_bwd(proj, hl, sv, dmg, prm, wa, wx):
    t_len = proj.shape[0]
    tt = min(TT, t_len)
    nt = t_len // tt
    hb8 = tt // 8

    def rev(i):
        return nt - 1 - i

    def halo(i):
        return jnp.maximum(rev(i) * hb8 - 1, 0)

    def body(proj_ref, hl_ref, hh_ref, sv_ref, dmg_ref, prm_ref, wa_ref, wx_ref,
             dp_ref, sums_ref, gwa_ref, gwx_ref,
             he, due, dye, drp_s, dip_s, an, gn, acc8):
        i = pl.program_id(0)
        t = rev(i)

        @pl.when(i == 0)
        def _():
            sums_ref[...] = jnp.zeros_like(sums_ref)
            gwa_ref[...] = jnp.zeros_like(gwa_ref)
            gwx_ref[...] = jnp.zeros_like(gwx_ref)
            due[tt:tt + 8, :] = jnp.zeros((8, D), F32)
            dye[tt:tt + 8, :] = jnp.zeros((8, D), F32)
            an[...] = jnp.zeros((8, D), F32)
            gn[...] = jnp.zeros((8, D), F32)

        live = (t > 0).astype(F32)
        he[0:8, :] = hh_ref[...].astype(F32) * live
        he[8:8 + tt, :] = hl_ref[...].astype(F32)

        ls_all = _log_sigmoid(prm_ref[P_LAM:P_LAM + 1, :])
        row = lax.broadcasted_iota(jnp.int32, (8, CG), 0)
        nblk = tt // 16

        def blk(ib, carry):
            r0 = pl.multiple_of((nblk - 1 - ib) * 16, 16)
            rows = pl.ds(r0, 16)
            for g in range(D // CG):
                cs = slice(g * CG, (g + 1) * CG)
                ls = ls_all[:, cs]
                dm = dmg_ref[rows, cs].astype(F32)
                cb = _pj(proj_ref, 0, rows, cs).astype(F32)
                rg = _pj(proj_ref, 4, rows, cs).astype(F32)
                sga = _sig(_pj(proj_ref, 5, rows, cs).astype(F32))
                sgb = _sig(_pj(proj_ref, 6, rows, cs).astype(F32))
                ya0 = sv_ref[SV_YA, rows, cs]
                h16 = he[pl.ds(r0 + 8, 16), cs]
                gl, th = _gelu(rg)
                dgl = 0.5 * (1.0 + th) + 0.5 * rg * (1.0 - th * th) * (_GC * (1.0 + 3.0 * 0.044715 * rg * rg))
                y_a = cb * ya0
                y_b = h16 * gl
                dy_a = dm * sga
                dy_b = dm * sgb
                col = lambda s: slice(s * D + g * CG, s * D + (g + 1) * CG)
                dp_ref[rows, col(5)] = (dm * y_a * sga * (1.0 - sga)).astype(BF16)
                dp_ref[rows, col(6)] = (dm * y_b * sgb * (1.0 - sgb)).astype(BF16)
                dp_ref[rows, col(4)] = (dy_b * h16 * dgl).astype(BF16)
                dp_ref[rows, col(0)] = (dy_a * ya0).astype(BF16)
                dye[rows, cs] = dy_a * cb
                dh16 = dy_b * gl

                a_next = an[:, cs]
                g_next = gn[:, cs]
                s_ba = jnp.zeros((8, CG), F32)
                s_bx = jnp.zeros((8, CG), F32)
                s_ls = jnp.zeros((8, CG), F32)
                for sb in (1, 0):
                    rr = r0 + 8 * sb
                    first = (row + (t * tt + rr)) == 0
                    r8 = pl.ds(rr, 8)
                    uu, r, ig, a, mult = (sv_ref[pln, r8, cs] for pln in (SV_U, SV_R, SV_I, SV_A, SV_MULT))
                    ca = jnp.where(row < 7, pltpu.roll(a, 7, 0), a_next)
                    cb_ = dh16[8 * sb:8 * sb + 8, :]
                    for s in (1, 2, 4):
                        a_sh = jnp.where(row < 8 - s, pltpu.roll(ca, 8 - s, 0), 1.0)
                        b_sh = jnp.where(row < 8 - s, pltpu.roll(cb_, 8 - s, 0), 0.0)
                        cb_ = ca * b_sh + cb_
                        ca = ca * a_sh
                    gv = ca * g_next + cb_
                    g_next = jnp.broadcast_to(gv[0:1, :], gv.shape)
                    a_next = jnp.broadcast_to(a[0:1, :], a.shape)
                    hprev = jnp.where(row >= 1, pltpu.roll(he[pl.ds(rr + 8, 8), cs], 1, 0),
                                      pltpu.roll(he[pl.ds(rr, 8), cs], 1, 0))
                    da = gv * hprev
                    dmult = jnp.where(first, 0.0, gv * ig * uu)
                    dla = da * a + jnp.where(mult > 0.0, dmult * (-(a * a) / mult), 0.0)
                    drp = dla * (LRU_C * ls) * r * (1.0 - r)
                    dip = gv * mult * uu * ig * (1.0 - ig)
                    s_ls = s_ls + dla * (LRU_C * r)
                    s_ba = s_ba + drp
                    s_bx = s_bx + dip
                    drp_s[pl.ds(rr, 8), cs] = drp
                    dip_s[pl.ds(rr, 8), cs] = dip
                    due[pl.ds(rr, 8), cs] = gv * mult * ig
                an[:, cs] = a_next
                gn[:, cs] = g_next
                sums_ref[M_BA:M_BA + 1, cs] += jnp.sum(s_ba, axis=0, keepdims=True)
                sums_ref[M_BX:M_BX + 1, cs] += jnp.sum(s_bx, axis=0, keepdims=True)
                sums_ref[M_LS:M_LS + 1, cs] += jnp.sum(s_ls, axis=0, keepdims=True)
            return carry

        lax.fori_loop(0, nblk, blk, 0)

        drp_b = drp_s[...].astype(BF16)
        dip_b = dip_s[...].astype(BF16)
        ub = sv_ref[SV_U].astype(BF16)
        for h in range(HEADS):
            cs = slice(h * HB, (h + 1) * HB)
            due[0:tt, cs] += _dot_nt(drp_b[:, cs], wa_ref[h]) + _dot_nt(dip_b[:, cs], wx_ref[h])
            gwa_ref[h] += _dot_tn(ub[:, cs], drp_b[:, cs])
            gwx_ref[h] += _dot_tn(ub[:, cs], dip_b[:, cs])

        acc8[...] = jnp.zeros_like(acc8)
        _conv_bwd_rows(tt, proj_ref, prm_ref, due, dye, dp_ref, acc8)
        for k, dst in enumerate([M_CBIAS] + [M_WB + k for k in range(4)] + [M_WA + k for k in range(3)]):
            sums_ref[dst:dst + 1, :] += jnp.sum(acc8[8 * k:8 * k + 8, :], axis=0, keepdims=True)
        due[tt:tt + 8, :] = due[0:8, :]
        dye[tt:tt + 8, :] = dye[0:8, :]

        @pl.when(i == nt - 1)
        def _():
            sums_ref[M_LS:M_LS + 1, :] = sums_ref[M_LS:M_LS + 1, :] * _sig(-prm_ref[P_LAM:P_LAM + 1, :])

    big = lambda: pltpu.VMEM((tt + 8, D), F32)
    tile = lambda: pltpu.VMEM((tt, D), F32)
    return pl.pallas_call(
        body, name="mixer_bwd", grid=(nt,),
        in_specs=[pl.BlockSpec((tt, 7 * D), lambda i: (rev(i), 0)),
                  pl.BlockSpec((tt, D), lambda i: (rev(i), 0)),
                  pl.BlockSpec((8, D), lambda i: (halo(i), 0)),
                  pl.BlockSpec((len(SV_PLANES), tt, D), lambda i: (0, rev(i), 0)),
                  pl.BlockSpec((tt, D), lambda i: (rev(i), 0)),
                  pl.BlockSpec((16, D), lambda i: (0, 0)),
                  pl.BlockSpec((HEADS, HB, HB), lambda i: (0, 0, 0)),
                  pl.BlockSpec((HEADS, HB, HB), lambda i: (0, 0, 0))],
        out_specs=[pl.BlockSpec((tt, 7 * D), lambda i: (rev(i), 0)),
                   pl.BlockSpec((16, D), lambda i: (0, 0)),
                   pl.BlockSpec((HEADS, HB, HB), lambda i: (0, 0, 0)),
                   pl.BlockSpec((HEADS, HB, HB), lambda i: (0, 0, 0))],
        out_shape=[jax.ShapeDtypeStruct((t_len, 7 * D), BF16), jax.ShapeDtypeStruct((16, D), F32),
                   jax.ShapeDtypeStruct((HEADS, HB, HB), F32), jax.ShapeDtypeStruct((HEADS, HB, HB), F32)],
        scratch_shapes=[big(), big(), big(), tile(), tile(),
                        pltpu.VMEM((8, D), F32), pltpu.VMEM((8, D), F32), pltpu.VMEM((64, D), F32)],
        compiler_params=_cp("arbitrary"),
    )(proj, hl, hl, sv, dmg, prm, wa, wx)


def _in_proj_bwd(dproj, w_in, x, dx1, mod, g_mix):
    t_len = x.shape[0]
    tm = min(TM, t_len)

    def body(dp_ref, w_ref, x_ref, dx1_ref, mod_ref, g_ref, gx_ref, sums_ref, acc):
        @pl.when(pl.program_id(0) == 0)
        def _():
            sums_ref[...] = jnp.zeros_like(sums_ref)

        def write(rows, dx):
            gx_ref[rows, :] = dx

        zero = jnp.zeros((8, D), F32)
        sums = (zero, zero, zero)
        for sub in _sub_blocks(tm):
            acc[sub, :] = _dot_nt(dp_ref[sub, :], w_ref[...])
            sums = _norm_bwd_rows(sub, sums, acc, x_ref, dx1_ref, mod_ref[1:2, :], g_ref[...], write)
        _add_norm_sums(sums_ref, sums)

    return pl.pallas_call(
        body, name="in_proj_bwd", grid=(t_len // tm,),
        in_specs=[pl.BlockSpec((tm, 7 * D), lambda i: (i, 0)),
                  _resident((D, 7 * D)),
                  pl.BlockSpec((tm, D), lambda i: (i, 0)), pl.BlockSpec((tm, D), lambda i: (i, 0)),
                  pl.BlockSpec((8, D), lambda i: (0, 0)), pl.BlockSpec((1, D), lambda i: (0, 0))],
        out_specs=[pl.BlockSpec((tm, D), lambda i: (i, 0)), pl.BlockSpec((8, D), lambda i: (0, 0))],
        out_shape=[jax.ShapeDtypeStruct((t_len, D), F32), jax.ShapeDtypeStruct((8, D), F32)],
        scratch_shapes=[pltpu.VMEM((tm, D), F32)],
        compiler_params=_cp("arbitrary"),
    )(dproj, w_in, x, dx1, mod, g_mix)


def _in_wgrad(h, dproj, g_wa, g_wx, order):
    t_len = h.shape[0]
    tk = min(TKI, t_len)
    nk = t_len // tk
    cw = 7 * D // NDEV
    hr = HB // NDEV

    def body(ord_ref, h_ref, d_ref, ga_ref, gx_ref, parts_ref, pa_ref, px_ref, acc, *scr):
        rs, sems = scr[:-6], scr[-6:]
        p, k = pl.program_id(0), pl.program_id(1)

        def head_rows(ref):
            return lambda s: ref.at[:, pl.ds(s * hr, hr), :]

        @pl.when((p == 0) & (k == 0))
        def _():
            for s in range(NDEV):
                _rs_send(head_rows(ga_ref)(s), pa_ref, s, *sems[0:3])
                _rs_send(head_rows(gx_ref)(s), px_ref, s, *sems[3:6])

        @pl.when(k == 0)
        def _():
            acc[...] = jnp.zeros_like(acc)

        acc[...] += _dot_tn(h_ref[...], d_ref[...])

        @pl.when(k == nk - 1)
        def _():
            q = ord_ref[p]
            for half in range(2):
                rs[0][q, half] = acc[:, half * cw:(half + 1) * cw].astype(BF16)
            _rs2_to_sibling(q, rs)

        @pl.when((k == min(1, nk - 1)) & (p > 0))
        def _():
            _rs2_forward(ord_ref[p - 1], parts_ref, rs)

        @pl.when((p == NCHIP - 1) & (k == nk - 1))
        def _():
            _rs2_forward(ord_ref[p], parts_ref, rs)
            _rs2_finish(parts_ref, rs)
            _rs_finish(head_rows(ga_ref), pa_ref, *sems[0:3])
            _rs_finish(head_rows(gx_ref), px_ref, *sems[3:6])

    return pl.pallas_call(
        body, name="in_wgrad",
        grid_spec=pltpu.PrefetchScalarGridSpec(
            num_scalar_prefetch=1, grid=(NCHIP, nk),
            in_specs=[pl.BlockSpec((tk, D), lambda p, k, o: (k, 0)),
                      pl.BlockSpec((tk, 2 * cw), lambda p, k, o: (k, o[p])), _ANY, _ANY],
            out_specs=[_ANY, _ANY, _ANY],
            scratch_shapes=[pltpu.VMEM((D, 2 * cw), F32)] + _rs2_scratch((D, cw)) + _RS_SEMS * 2),
        out_shape=[jax.ShapeDtypeStruct((NCHIP, D, cw), BF16), jax.ShapeDtypeStruct((NDEV, HEADS, hr, HB), F32),
                   jax.ShapeDtypeStruct((NDEV, HEADS, hr, HB), F32)],
        compiler_params=_cp("arbitrary", "arbitrary"),
    )(order, h, dproj, g_wa, g_wx)


def _adam_math(w, g, m, v):
    m = ADAM_B1 * m + (1.0 - ADAM_B1) * g
    v = ADAM_B2 * v + (1.0 - ADAM_B2) * (g * g)
    m_hat = m / (1.0 - ADAM_B1 ** ADAM_STEP)
    v_hat = v / (1.0 - ADAM_B2 ** ADAM_STEP)
    delta = -ADAM_LR * (m_hat / (jnp.sqrt(v_hat) + ADAM_EPS) + ADAM_WD * w)
    return delta, m, v


def _ada_bwd(c_all, dmod_cols, w, m, v):
    rb = 256
    n = w.shape[1]
    nrow = c_all.shape[0]

    def body(c_ref, d_ref, w_ref, m_ref, v_ref, g_ref, dl_ref, nm_ref, nv_ref):
        cv = c_ref[...]
        g = _dot_tn((cv * _sig(cv)).astype(BF16), d_ref[...].astype(BF16))
        g_ref[...] = g
        dl_ref[...], nm_ref[...], nv_ref[...] = _adam_math(w_ref[...], g, m_ref[...], v_ref[...])

    blk = pl.BlockSpec((rb, n), lambda i: (i, 0))
    sds = jax.ShapeDtypeStruct(w.shape, F32)
    return pl.pallas_call(
        body, name="ada_bwd", grid=(D // rb,),
        in_specs=[pl.BlockSpec((nrow, rb), lambda i: (0, i)), pl.BlockSpec((nrow, n), lambda i: (0, 0)), blk, blk, blk],
        out_specs=[blk, blk, blk, blk], out_shape=[sds, sds, sds, sds],
        compiler_params=_cp("parallel"),
    )(c_all, dmod_cols, w, m, v)


def _adam(name, parts, w, m, v):
    p, r, c = parts.shape
    rb = max([cand for cand in range(8, min(r, 256) + 1, 8) if r % cand == 0], default=r)

    def body(p_ref, w_ref, m_ref, v_ref, g_ref, dl_ref, nm_ref, nv_ref):
        g = p_ref[0].astype(F32)
        for q in range(1, p):
            g = g + p_ref[q].astype(F32)
        g_ref[...] = g
        dl_ref[...], nm_ref[...], nv_ref[...] = _adam_math(w_ref[...], g, m_ref[...], v_ref[...])

    blk = pl.BlockSpec((rb, c), lambda i: (i, 0))
    sds = jax.ShapeDtypeStruct((r, c), F32)
    return pl.pallas_call(
        body, name=name, grid=(r // rb,),
        in_specs=[pl.BlockSpec((p, rb, c), lambda i: (0, i, 0)), blk, blk, blk],
        out_specs=[blk, blk, blk, blk], out_shape=[sds, sds, sds, sds],
        compiler_params=_cp("parallel"),
    )(parts, w, m, v)


_SMALL_SEMS = [pltpu.SemaphoreType.DMA((7,)), pltpu.SemaphoreType.DMA((7,)), pltpu.SemaphoreType.DMA]
_VMEM = pl.BlockSpec(memory_space=pltpu.VMEM)


def _exchange_small(x_ref, out_ref, send_sems, recv_sems, local_sem):
    m_per = x_ref.shape[0]
    x, y, c = _my_pos()
    me, sibling = (x, y, c), (x, y, 1 - c)
    chips = [(1 - x, y), (x, 1 - y), (1 - x, 1 - y)]

    def rows(px, py, pc):
        return out_ref.at[pl.ds((4 * px + 2 * py + pc) * m_per, m_per), :]

    def copy(k, block, to, src=None):
        return pltpu.make_async_remote_copy(
            src_ref=rows(*block) if src is None else src, dst_ref=rows(*block),
            send_sem=send_sems.at[k], recv_sem=recv_sems.at[k], device_id=to, device_id_type=MESH)

    mine = pltpu.make_async_copy(x_ref, rows(*me), local_sem)
    mine.start()
    first = [copy(0, me, sibling, src=x_ref)]
    first += [copy(1 + j, me, (*chip, c), src=x_ref) for j, chip in enumerate(chips)]
    for cp in first:
        cp.start()
    passed = [copy(4 + j, (*chip, c), sibling) for j, chip in enumerate(chips)]
    for j, chip in enumerate(chips):
        copy(1 + j, (*chip, c), me).wait_recv()
        passed[j].start()
    copy(0, sibling, me).wait_recv()
    for j, chip in enumerate(chips):
        copy(4 + j, (*chip, 1 - c), me).wait_recv()
    for cp in first + passed:
        cp.wait_send()
    mine.wait()


def _gather_small_grads(sums1, sums2, msums, d_gt1, d_gt2, d_gfin, loss):
    def body(s1, s2, ms, g1, g2, gf, ls, out_ref, pack, *sems):
        rows = [s1[S_SH:S_SH + 1, :], s1[S_SC:S_SC + 1, :], g1[0:1, :],
                s2[S_SH:S_SH + 1, :], s2[S_SC:S_SC + 1, :], g2[0:1, :],
                s1[S_G:S_G + 1, :], ms[M_CBIAS:M_CBIAS + 1, :], ms[M_BA:M_BA + 1, :], ms[M_BX:M_BX + 1, :],
                ms[M_LS:M_LS + 1, :], s2[S_G:S_G + 1, :], gf[...]]
        rows += [ms[M_WA + k:M_WA + k + 1, :] for k in range(3)] + [ms[M_WB + k:M_WB + k + 1, :] for k in range(4)]
        rows += [jnp.broadcast_to(ls[0:1, 0:1], (1, D))]
        for i, v in enumerate(rows):
            pack[i:i + 1, :] = v
        pack[len(rows):24, :] = jnp.zeros((24 - len(rows), D), F32)
        _exchange_small(pack, out_ref, *sems)

    return pl.pallas_call(
        body, name="gather_small", out_shape=jax.ShapeDtypeStruct((NDEV * 24, D), F32),
        in_specs=[_VMEM] * 7, out_specs=_VMEM, scratch_shapes=[pltpu.VMEM((24, D), F32)] + _SMALL_SEMS,
    )(sums1, sums2, msums, d_gt1, d_gt2, d_gfin, loss)


def _ada_mod(pack, w_ada, b_cols):
    ncol = w_ada.shape[1]

    def body(p_ref, w_ref, b_ref, all_ref, mod_ref, cols, *sems):
        _exchange_small(p_ref, all_ref, *sems[0:3])
        c_all = jnp.concatenate([all_ref[8 * d:8 * d + 1, 0:D] for d in range(NDEV)], axis=0)
        c16 = jnp.concatenate([c_all, jnp.zeros_like(c_all)], axis=0)
        mod16 = _dot((c16 * _sig(c16)).astype(BF16), w_ref[...].astype(BF16)) + b_ref[...]
        cols[...] = mod16[0:NDEV]
        _exchange_small(cols, mod_ref, *sems[3:6])

    return pl.pallas_call(
        body, name="ada_mod",
        out_shape=[jax.ShapeDtypeStruct((NDEV * 8, pack.shape[1]), F32), jax.ShapeDtypeStruct((NDEV * 8, ncol), F32)],
        in_specs=[_VMEM, _VMEM, _VMEM], out_specs=[_VMEM, _VMEM],
        scratch_shapes=[pltpu.VMEM((NDEV, ncol), F32)] + _SMALL_SEMS * 2,
        compiler_params=_cp(),
    )(pack, w_ada, b_cols)


def _blk_rows(n):
    return lambda ref, b: ref.at[pl.ds(pl.multiple_of(b * n, 8), n), :]


def _blk_lead(ref, b):
    return ref.at[b]


def _blk_heads(ref, b):
    return ref.at[:, pl.ds(pl.multiple_of(b * (HB // NDEV), 8), HB // NDEV), :]


def _ag_phases(ins, outs, slicers, send_sems, recv_sems, local_sems):
    na = len(ins)
    x, y, c = _my_pos()
    me, sibling = (x, y, c), (x, y, 1 - c)
    chips = [(1 - x, y), (x, 1 - y), (1 - x, 1 - y)]

    def copy(a, k, block, to, from_shard=False):
        px, py, pc = block
        dst = slicers[a](outs[a], 4 * px + 2 * py + pc)
        return pltpu.make_async_remote_copy(
            src_ref=ins[a] if from_shard else dst, dst_ref=dst,
            send_sem=send_sems.at[a * 7 + k], recv_sem=recv_sems.at[a * 7 + k], device_id=to, device_id_type=MESH)

    def local(a):
        return pltpu.make_async_copy(ins[a], slicers[a](outs[a], 4 * x + 2 * y + c), local_sems.at[a])

    def firsts(a):
        return [copy(a, 0, me, sibling, True)] + [copy(a, 1 + j, me, (*chip, c), True) for j, chip in enumerate(chips)]

    def start():
        for a in range(na):
            local(a).start()
            for cp in firsts(a):
                cp.start()

    def forward():
        for a in range(na):
            for j, chip in enumerate(chips):
                copy(a, 1 + j, (*chip, c), me).wait_recv()
                copy(a, 4 + j, (*chip, c), sibling).start()

    def finish():
        for a in range(na):
            copy(a, 0, sibling, me).wait_recv()
            for j, chip in enumerate(chips):
                copy(a, 4 + j, (*chip, 1 - c), me).wait_recv()
        for a in range(na):
            for cp in firsts(a) + [copy(a, 4 + j, (*chip, c), sibling) for j, chip in enumerate(chips)]:
                cp.wait_send()
            local(a).wait()

    return start, forward, finish


def _ag_sems(na):
    return [pltpu.SemaphoreType.DMA((7 * na,)), pltpu.SemaphoreType.DMA((7 * na,)), pltpu.SemaphoreType.DMA((na,))]


def _local_step(x, target, mod, g_mix, g_ffn, g_fin, prm, w_in_shard, shards):
    fulls = [(HEADS, HB, HB), (HEADS, HB, HB), (D, D), (NDEV, FB, D), (DFF, D)]
    slicers = [_blk_heads, _blk_heads, _blk_rows(D // NDEV), _blk_lead, _blk_rows(DFF // NDEV)]
    my_chip = _my_index() >> 1
    own_first = (my_chip ^ jnp.arange(NCHIP, dtype=jnp.int32)).astype(jnp.int32)
    early, late = [0, 1, 2, 4], [3]
    pick = lambda lst, idx: [lst[i] for i in idx]
    proj, h, w_in, (wa, wx, w_out, w_down) = _in_proj(x, mod, g_mix, w_in_shard, own_first, pick(shards, early),
                                                      pick(fulls, early), pick(slicers, early))
    merged, hl, sv, (w_gu,) = _mixer_fwd(proj, prm, wa, wx, pick(shards, late), pick(fulls, late),
                                         pick(slicers, late))
    w_gu = w_gu.reshape(2, 4, FB, D)
    x1, h2 = _out_proj(merged, x, mod, g_ffn, w_out)
    gu, dx2, dx2b, loss, d_gfin = _ffn_fwd(h2, x1, target, mod, g_fin, w_gu, w_down)
    dgu, act, dx1, dx1b, dmg, sums2 = _ffn_bwd(dx2, gu, x1, mod, g_ffn, w_gu, w_down, w_out)
    chip_order = _xor_order(_my_index() >> 1, NCHIP)
    p_wgu = _gu_wgrad(h2, dgu, chip_order)
    p_wdown, d_gt2 = _scaled_wgrad("down_wgrad", act, dx2b, w_down, 5, mod, chip_order)
    p_wout, d_gt1 = _scaled_wgrad("out_wgrad", merged, dx1b, w_out, 2, mod, chip_order)
    dproj, msums, g_wa, g_wx = _mixer_bwd(proj, hl, sv, dmg, prm, wa, wx)
    p_win, p_wa, p_wx = _in_wgrad(h, dproj, g_wa, g_wx, chip_order)
    grad_x, sums1 = _in_proj_bwd(dproj, w_in, x, dx1, mod, g_mix)
    return dict(loss=loss, grad_x=grad_x, d_gfin=d_gfin, sums1=sums1, sums2=sums2, msums=msums,
                d_gt1=d_gt1, d_gt2=d_gt2, p_win=p_win, p_wa=p_wa, p_wx=p_wx, p_wout=p_wout, p_wgu=p_wgu,
                p_wdown=p_wdown)


def kernel(x, c, w_ada, b_ada, g_norm_mix, w_in, conv_a_w, conv_b_w, conv_b_bias, w_rg_a, b_rg_a, w_rg_x, b_rg_x, lru_lambda, w_out, g_norm_ffn, w_gate_up, w_down, g_norm_final, loss_target, m_w_ada, m_b_ada, m_g_norm_mix, m_w_in, m_conv_a_w, m_conv_b_w, m_conv_b_bias, m_w_rg_a, m_b_rg_a, m_w_rg_x, m_b_rg_x, m_lru_lambda, m_w_out, m_g_norm_ffn, m_w_gate_up, m_w_down, m_g_norm_final, v_w_ada, v_b_ada, v_g_norm_mix, v_w_in, v_conv_a_w, v_conv_b_w, v_conv_b_bias, v_w_rg_a, v_b_rg_a, v_w_rg_x, v_b_rg_x, v_lru_lambda, v_w_out, v_g_norm_ffn, v_w_gate_up, v_w_down, v_g_norm_final):
    me = 4 * lax.axis_index("x") + 2 * lax.axis_index("y") + lax.axis_index("c")
    ncol = w_ada.shape[2]
    cw = conv_a_w.shape[2]

    pack0 = jnp.concatenate([c, conv_a_w.reshape(1, 3 * cw), conv_b_w.reshape(1, 4 * cw)], axis=1)
    b_cols = lax.dynamic_slice_in_dim(b_ada, me * ncol, ncol, axis=1)
    got0, got1 = _ada_mod(jnp.broadcast_to(pack0, (8, pack0.shape[1])), w_ada[0], b_cols)
    got0 = got0.reshape(NDEV, 8, -1)[:, 0, :]
    c_all = got0[:, :D]
    conv_a = got0[:, D:D + 3 * cw].reshape(NDEV, 3, cw).transpose(1, 0, 2).reshape(3, D)
    conv_b = got0[:, D + 3 * cw:].reshape(NDEV, 4, cw).transpose(1, 0, 2).reshape(4, D)
    c16 = jnp.concatenate([c_all, jnp.zeros((8, D), F32)], axis=0)
    mod6 = lax.dynamic_index_in_dim(got1.reshape(NDEV, NDEV, ncol), me, axis=1, keepdims=False).reshape(6, D)
    mod = jnp.concatenate([mod6, jnp.zeros((2, D), F32)], axis=0)

    tr = lambda a: jnp.swapaxes(a, 1, 2)
    shards = [w_rg_a[0].astype(BF16), w_rg_x[0].astype(BF16), w_out[0].astype(BF16), tr(w_gate_up)[0].astype(BF16),
              w_down[0].astype(BF16)]

    prm = jnp.concatenate([conv_a, conv_b, conv_b_bias, b_rg_a, b_rg_x, lru_lambda, jnp.zeros((5, D), F32)], axis=0)
    r = _local_step(x[0], loss_target[0], mod, g_norm_mix, g_norm_ffn, g_norm_final.reshape(1, D), prm,
                    w_in[0].astype(BF16), shards)

    parts = [r["p_win"], r["p_wa"], r["p_wx"], r["p_wout"], r["p_wgu"], r["p_wdown"]]
    big = {}
    for nm, p, w, m, v in (("w_in", parts[0], w_in, m_w_in, v_w_in), ("w_rg_a", parts[1], w_rg_a, m_w_rg_a, v_w_rg_a),
                           ("w_rg_x", parts[2], w_rg_x, m_w_rg_x, v_w_rg_x), ("w_out", parts[3], w_out, m_w_out, v_w_out),
                           ("w_gate_up", parts[4], tr(w_gate_up), tr(m_w_gate_up), tr(v_w_gate_up)),
                           ("w_down", parts[5], w_down, m_w_down, v_w_down)):
        two_d = (-1, w.shape[-1])
        outs = _adam("adam_" + nm, p.reshape((p.shape[0],) + w.reshape(two_d).shape), w.reshape(two_d), m.reshape(two_d),
                     v.reshape(two_d))
        big[nm] = [o.reshape(w.shape) for o in outs]
    big["w_gate_up"] = [tr(o) for o in big["w_gate_up"]]

    got2 = _gather_small_grads(r["sums1"], r["sums2"], r["msums"], r["d_gt1"], r["d_gt2"], r["d_gfin"],
                               r["loss"]).reshape(NDEV, 24, D)

    rep_w = jnp.concatenate([b_ada.reshape(6, D), g_norm_mix, conv_b_bias, b_rg_a, b_rg_x, lru_lambda, g_norm_ffn,
                             g_norm_final.reshape(1, D), jnp.zeros((3, D), F32)], axis=0)
    rep_m = jnp.concatenate([m_b_ada.reshape(6, D), m_g_norm_mix, m_conv_b_bias, m_b_rg_a, m_b_rg_x, m_lru_lambda,
                             m_g_norm_ffn, m_g_norm_final.reshape(1, D), jnp.zeros((3, D), F32)], axis=0)
    rep_v = jnp.concatenate([v_b_ada.reshape(6, D), v_g_norm_mix, v_conv_b_bias, v_b_rg_a, v_b_rg_x, v_lru_lambda,
                             v_g_norm_ffn, v_g_norm_final.reshape(1, D), jnp.ones((3, D), F32)], axis=0)
    rep = _adam("adam_rep", got2[:, :16, :], rep_w, rep_m, rep_v)

    conv_parts = lax.dynamic_slice_in_dim(got2[:, 13:21, :], me * cw, cw, axis=2)
    cv_w = jnp.concatenate([conv_a_w[0], conv_b_w[0], jnp.zeros((1, cw), F32)], axis=0)
    cv_m = jnp.concatenate([m_conv_a_w[0], m_conv_b_w[0], jnp.zeros((1, cw), F32)], axis=0)
    cv_v = jnp.concatenate([v_conv_a_w[0], v_conv_b_w[0], jnp.ones((1, cw), F32)], axis=0)
    cvo = _adam("adam_conv", conv_parts, cv_w, cv_m, cv_v)

    dmod_cols = lax.dynamic_slice_in_dim(got2[:, :6, :].reshape(NDEV, 6 * D), me * ncol, ncol, axis=1)
    dmod16 = jnp.concatenate([dmod_cols, jnp.zeros((8, ncol), F32)], axis=0)
    ada = _ada_bwd(c16, dmod16, w_ada[0], m_w_ada[0], v_w_ada[0])

    loss = jnp.sum(got2[:, 20, 0])

    def pick(q):
        one = lambda i: rep[q][i:i + 1]
        return [ada[q].reshape(w_ada.shape), rep[q][0:6].reshape(b_ada.shape), one(6), big["w_in"][q],
                cvo[q][0:3].reshape(conv_a_w.shape), cvo[q][3:7].reshape(conv_b_w.shape), one(7),
                big["w_rg_a"][q], one(8), big["w_rg_x"][q], one(9), one(10), big["w_out"][q], one(11),
                big["w_gate_up"][q], big["w_down"][q], rep[q][12]]

    return (loss, r["grad_x"].reshape(x.shape), *pick(0), *pick(1), *pick(2), *pick(3))
```

```python
import math

import jax
import jax.numpy as jnp
from jax import lax
from jax.experimental import pallas as pl
from jax.experimental.pallas import tpu as pltpu

F32 = jnp.float32
BF16 = jnp.bfloat16

D = 1024
DFF = 2816
NDEV = 8
HEADS = 4
HB = D // HEADS
FB = DFF // 4
EPS = 1e-6
LRU_C = 8.0
ADAM_LR, ADAM_B1, ADAM_B2, ADAM_EPS, ADAM_WD, ADAM_STEP = 0.001, 0.9, 0.999, 1e-08, 0.01, 10

VMEM_LIMIT = 56 * 1024 * 1024
TM = 512
TMI = 1024
TMF = 256
TK = 2048
TKI = 2048
SUB = 256
TT = 256
CG = 256
MESH = pl.DeviceIdType.MESH


def _cp(*sem):
    return pltpu.CompilerParams(dimension_semantics=sem, vmem_limit_bytes=VMEM_LIMIT)


def _sig(x):
    return 1.0 / (1.0 + jnp.exp(-x))


def _log_sigmoid(x):
    z = jnp.exp(-jnp.abs(x))
    u = 1.0 + z
    d = u - 1.0
    l1p = jnp.where(d == 0.0, z, jnp.log(u) * (z / jnp.where(d == 0.0, 1.0, d)))
    return -(jnp.maximum(-x, 0.0) + l1p)


def _neg_expm1(x):
    p = x * (1.0 + x * 0.5 * (1.0 + x * (1.0 / 3.0) * (1.0 + x * 0.25 * (1.0 + x * 0.2 * (1.0 + x * (1.0 / 6.0))))))
    return jnp.where(x > -0.25, -p, 1.0 - jnp.exp(x))


_GC = math.sqrt(2.0 / math.pi)


def _gelu(x):
    t = jnp.tanh(_GC * (x + 0.044715 * x * x * x))
    return 0.5 * x * (1.0 + t), t


def _dot(a, b):
    return jnp.dot(a, b, preferred_element_type=F32)


def _dot_nt(a, b):
    return lax.dot_general(a, b, (((1,), (1,)), ((), ())), preferred_element_type=F32)


def _dot_tn(a, b):
    return lax.dot_general(a, b, (((0,), (0,)), ((), ())), preferred_element_type=F32)


def _resident(shape):
    return pl.BlockSpec(shape, lambda *_: (0,) * len(shape), pipeline_mode=pl.Buffered(1))


def _sub_blocks(n_rows):
    step = min(SUB, n_rows)
    return [slice(r, r + step) for r in range(0, n_rows, step)]


def _fold8(v):
    return v[0:8] + v[8:16]


def _pj(ref, s, rows=slice(None), cols=slice(0, D)):
    return ref[rows, s * D + cols.start:s * D + cols.stop]


def _in_proj(x, mod, g_mix, w_shard, order, shards, fulls, slicers):
    t_len = x.shape[0]
    tm = min(TMI, t_len)
    ni = t_len // tm
    na = len(shards)
    cw = 7 * D // NDEV
    rc = 32

    def body(ord_ref, x_ref, mod_ref, g_ref, wsh_ref, *rest):
        ins, (proj_ref, h_ref, wfull_ref), outs = rest[:na], rest[na:na + 3], rest[na + 3:2 * na + 3]
        h_scr, w_scr, wsend, wrecv, wlocal, wout = rest[2 * na + 3:2 * na + 9]
        start, forward, finish = _ag_phases(ins, outs, slicers, *rest[2 * na + 9:])
        p, i = pl.program_id(0), pl.program_id(1)
        x_, y_, c = _my_pos()
        me, sibling = (x_, y_, c), (x_, y_, 1 - c)
        chip_at = [None, (x_, 1 - y_), (1 - x_, y_), (1 - x_, 1 - y_)]

        def cols(px, py, pc):
            return w_scr.at[:, pl.ds(pl.multiple_of((4 * px + 2 * py + pc) * cw, 128), cw)]

        def wcopy(k, block, to, from_shard=False):
            dst = cols(*block)
            return pltpu.make_async_remote_copy(src_ref=wsh_ref if from_shard else dst, dst_ref=dst,
                                                send_sem=wsend.at[k], recv_sem=wrecv.at[k], device_id=to,
                                                device_id_type=MESH)

        own_local = pltpu.make_async_copy(wsh_ref, cols(*me), wlocal)
        to_hbm = pltpu.make_async_copy(w_scr, wfull_ref, wout)

        @pl.when((p == 0) & (i == 0))
        def _():
            own_local.start()
            wcopy(0, me, sibling, True).start()
            for q in (1, 2):
                wcopy(q, me, (*chip_at[q], c), True).start()
            own_local.wait()
            wcopy(0, sibling, me).wait_recv()

        @pl.when((p == 0) & (i == ni // 2))
        def _():
            wcopy(3, me, (*chip_at[3], c), True).start()

        for q in (1, 2, 3):
            @pl.when((p == q - 1) & (i == ni - 1))
            def _():
                wcopy(q, (*chip_at[q], c), me).wait_recv()
                wcopy(3 + q, (*chip_at[q], c), sibling).start()

            @pl.when((p == q) & (i == 0))
            def _():
                wcopy(3 + q, (*chip_at[q], 1 - c), me).wait_recv()

        @pl.when((p == 1) & (i == 0))
        def _():
            start()

        @pl.when((p == NCHIP - 1) & (i == ni // 2))
        def _():
            forward()

        @pl.when((p == NCHIP - 1) & (i == 0))
        def _():
            to_hbm.start()

        gs = g_ref[...] * (1.0 + mod_ref[1:2, :])
        sh = mod_ref[0:1, :]

        wcols = pl.ds(pl.multiple_of(ord_ref[p] * (2 * cw), 128), 2 * cw)
        for sub in _sub_blocks(tm):
            for r0 in range(sub.start, sub.stop, rc):
                xv = x_ref[r0:r0 + rc, :]
                r = lax.rsqrt(jnp.mean(xv * xv, axis=-1, keepdims=True) + EPS)
                h_scr[r0:r0 + rc, :] = (xv * r * gs + sh).astype(BF16)
            proj_ref[sub, :] = _dot(h_scr[sub, :], w_scr[:, wcols]).astype(BF16)

        @pl.when(p == 0)
        def _():
            h_ref[...] = h_scr[...]

        @pl.when((p == NCHIP - 1) & (i == ni - 1))
        def _():
            wcopy(0, me, sibling, True).wait_send()
            for q in (1, 2, 3):
                wcopy(q, me, (*chip_at[q], c), True).wait_send()
                wcopy(3 + q, (*chip_at[q], c), sibling).wait_send()
            finish()
            to_hbm.wait()

    res = pl.pallas_call(
        body, name="in_proj",
        grid_spec=pltpu.PrefetchScalarGridSpec(
            num_scalar_prefetch=1, grid=(NCHIP, ni),
            in_specs=[pl.BlockSpec((tm, D), lambda p, i, o: (i, 0)),
                      pl.BlockSpec((8, D), lambda p, i, o: (0, 0)),
                      pl.BlockSpec((1, D), lambda p, i, o: (0, 0))] + [_ANY] * (1 + na),
            out_specs=[pl.BlockSpec((tm, 2 * cw), lambda p, i, o: (i, o[p])),
                       pl.BlockSpec((tm, D), lambda p, i, o: (jnp.where(p == 0, i, ni - 1), 0))]
            + [_ANY] * (1 + na),
            scratch_shapes=[pltpu.VMEM((tm, D), BF16), pltpu.VMEM((D, 7 * D), BF16),
                            pltpu.SemaphoreType.DMA((7,)), pltpu.SemaphoreType.DMA((7,)),
                            pltpu.SemaphoreType.DMA, pltpu.SemaphoreType.DMA] + _ag_sems(na)),
        out_shape=[jax.ShapeDtypeStruct((t_len, 7 * D), BF16), jax.ShapeDtypeStruct((t_len, D), BF16),
                   jax.ShapeDtypeStruct((D, 7 * D), BF16)]
        + [jax.ShapeDtypeStruct(f, sh.dtype) for f, sh in zip(fulls, shards)],
        compiler_params=_cp("arbitrary", "arbitrary"),
    )(order, x, mod, g_mix, w_shard, *shards)
    return res[0], res[1], res[2], res[3:]


P_WA, P_WB, P_CBIAS, P_BA, P_BX, P_LAM = 0, 3, 7, 8, 9, 10
SV_PLANES = SV_U, SV_YA, SV_R, SV_I, SV_A, SV_MULT = range(6)


def _lru_gates(rp, ip, ls, first_row):
    r = _sig(rp)
    ig = _sig(ip)
    la = LRU_C * r * ls
    a = jnp.exp(la)
    m2 = _neg_expm1(2.0 * la)
    mult = jnp.where(first_row, 1.0, jnp.sqrt(jnp.maximum(m2, 0.0)))
    return r, ig, la, a, m2, mult


def _shift_down(cur, prev, s, row):
    return jnp.where(row >= s, pltpu.roll(cur, s, 0), pltpu.roll(prev, s, 0))


def _shift_up(cur, nxt, s, row):
    return jnp.where(row < 8 - s, pltpu.roll(cur, 8 - s, 0), pltpu.roll(nxt, 8 - s, 0))


def _conv_fwd_rows(tt, proj_ref, prm_ref, xe, ve, u_s, ub_s, ya_s):
    row = lax.broadcasted_iota(jnp.int32, (8, CG), 0)
    w_b = [prm_ref[P_WB + k:P_WB + k + 1, :] for k in range(4)]
    w_a = [prm_ref[P_WA + k:P_WA + k + 1, :] for k in range(3)]
    bias = prm_ref[P_CBIAS:P_CBIAS + 1, :]

    def blk(ib, carry):
        r0 = pl.multiple_of(ib * 16, 16)
        rows = pl.ds(r0, 16)
        for g in range(D // CG):
            cs = slice(g * CG, (g + 1) * CG)
            x16 = _pj(proj_ref, 3, rows, cs).astype(F32)
            v16 = _pj(proj_ref, 1, rows, cs).astype(F32) * _pj(proj_ref, 2, rows, cs).astype(F32)
            xp = xe[pl.ds(r0, 8), cs]
            vp = ve[pl.ds(r0, 8), cs]
            xe[pl.ds(r0 + 8, 16), cs] = x16
            ve[pl.ds(r0 + 8, 16), cs] = v16
            us, yas = [], []
            for sb in range(2):
                xc, vc = x16[8 * sb:8 * sb + 8], v16[8 * sb:8 * sb + 8]
                u8 = bias[:, cs] + w_b[3][:, cs] * xc
                for s in (1, 2, 3):
                    u8 = u8 + w_b[3 - s][:, cs] * _shift_down(xc, xp, s, row)
                y8 = w_a[2][:, cs] * vc
                for s in (1, 2):
                    y8 = y8 + w_a[2 - s][:, cs] * _shift_down(vc, vp, s, row)
                us.append(u8)
                yas.append(y8)
                xp, vp = xc, vc
            u16 = jnp.concatenate(us, axis=0)
            u_s[rows, cs] = u16
            ub_s[rows, cs] = u16.astype(BF16)
            ya_s[rows, cs] = jnp.concatenate(yas, axis=0)
        return carry

    lax.fori_loop(0, tt // 16, blk, 0)


def _mixer_fwd(proj, prm, wa, wx, shards, fulls, slicers):
    t_len = proj.shape[0]
    tt = min(TT, t_len)
    nt = t_len // tt
    na = len(shards)

    def body(proj_hbm, prm_ref, wa_ref, wx_ref, *rest):
        ins, (mg_ref, hl_ref, sv_ref), outs = rest[:na], rest[na:na + 3], rest[na + 3:2 * na + 3]
        xe, ve, hc, rp_s, ip_s, ub_s, ring, ring_sem = rest[2 * na + 3:2 * na + 11]
        start, forward, finish = _ag_phases(ins, outs, slicers, *rest[2 * na + 11:])
        t = pl.program_id(0)

        def fetch(s):
            row0 = s * tt if isinstance(s, int) else pl.multiple_of(s * tt, tt)
            return pltpu.make_async_copy(proj_hbm.at[pl.ds(row0, tt), :], ring.at[s % 3], ring_sem.at[s % 3])

        @pl.when(t == 0)
        def _():
            for s in range(min(2, nt)):
                fetch(s).start()

        @pl.when(t + 2 < nt)
        def _():
            fetch(t + 2).start()

        fetch(t).wait()
        proj_ref = ring.at[t % 3]

        @pl.when(t == 0)
        def _():
            start()
            xe[0:8, :] = jnp.zeros((8, D), F32)
            ve[0:8, :] = jnp.zeros((8, D), F32)
            hc[...] = jnp.zeros((8, D), F32)

        @pl.when(t == (3 * nt) // 4)
        def _():
            forward()

        _conv_fwd_rows(tt, proj_ref, prm_ref, xe, ve, sv_ref.at[SV_U], ub_s, sv_ref.at[SV_YA])
        xe[0:8, :] = xe[tt:tt + 8, :]
        ve[0:8, :] = ve[tt:tt + 8, :]

        ub = ub_s[...]
        for h in range(HEADS):
            cs = slice(h * HB, (h + 1) * HB)
            rp_s[:, cs] = _dot(ub[:, cs], wa_ref[h]) + prm_ref[P_BA:P_BA + 1, cs]
            ip_s[:, cs] = _dot(ub[:, cs], wx_ref[h]) + prm_ref[P_BX:P_BX + 1, cs]

        ls_all = _log_sigmoid(prm_ref[P_LAM:P_LAM + 1, :])
        row = lax.broadcasted_iota(jnp.int32, (8, CG), 0)

        def blk(i, carry):
            r0 = pl.multiple_of(i * 16, 16)
            for g in range(D // CG):
                cs = slice(g * CG, (g + 1) * CG)
                ls = ls_all[:, cs]
                hprev = hc[:, cs]
                hs = []
                for sb in range(2):
                    rr = r0 + 8 * sb
                    first = (row + (t * tt + rr)) == 0
                    r8 = pl.ds(rr, 8)
                    r, ig, _, a, _, mult = _lru_gates(rp_s[r8, cs], ip_s[r8, cs], ls, first)
                    for plane, val in ((SV_R, r), (SV_I, ig), (SV_A, a), (SV_MULT, mult)):
                        sv_ref[plane, r8, cs] = val
                    b = mult * (ig * sv_ref[SV_U, r8, cs])
                    for s in (1, 2, 4):
                        a_sh = jnp.where(row >= s, pltpu.roll(a, s, 0), 1.0)
                        b_sh = jnp.where(row >= s, pltpu.roll(b, s, 0), 0.0)
                        b = a * b_sh + b
                        a = a * a_sh
                    hv = a * hprev + b
                    hprev = jnp.broadcast_to(hv[7:8, :], hv.shape)
                    hs.append(hv)
                hc[:, cs] = hprev
                h16 = jnp.concatenate(hs, axis=0)
                rows = pl.ds(r0, 16)
                gl, _ = _gelu(_pj(proj_ref, 4, rows, cs).astype(F32))
                y_b = h16 * gl
                y_a = _pj(proj_ref, 0, rows, cs).astype(F32) * sv_ref[SV_YA, rows, cs]
                mg = (_sig(_pj(proj_ref, 5, rows, cs).astype(F32)) * y_a
                      + _sig(_pj(proj_ref, 6, rows, cs).astype(F32)) * y_b)
                mg_ref[rows, cs] = mg.astype(BF16)
                hl_ref[rows, cs] = h16.astype(BF16)
            return carry

        lax.fori_loop(0, tt // 16, blk, 0)

        @pl.when(t == nt - 1)
        def _():
            finish()

    res = pl.pallas_call(
        body, name="mixer_fwd", grid=(nt,),
        in_specs=[_ANY,
                  pl.BlockSpec((16, D), lambda t: (0, 0)),
                  pl.BlockSpec((HEADS, HB, HB), lambda t: (0, 0, 0)),
                  pl.BlockSpec((HEADS, HB, HB), lambda t: (0, 0, 0))] + [_ANY] * na,
        out_specs=[pl.BlockSpec((tt, D), lambda t: (t, 0)), pl.BlockSpec((tt, D), lambda t: (t, 0)),
                   pl.BlockSpec((len(SV_PLANES), tt, D), lambda t: (0, t, 0))] + [_ANY] * na,
        out_shape=[jax.ShapeDtypeStruct((t_len, D), BF16), jax.ShapeDtypeStruct((t_len, D), BF16),
                   jax.ShapeDtypeStruct((len(SV_PLANES), t_len, D), F32)]
        + [jax.ShapeDtypeStruct(f, sh.dtype) for f, sh in zip(fulls, shards)],
        scratch_shapes=[pltpu.VMEM((tt + 8, D), F32), pltpu.VMEM((tt + 8, D), F32), pltpu.VMEM((8, D), F32),
                        pltpu.VMEM((tt, D), F32), pltpu.VMEM((tt, D), F32), pltpu.VMEM((tt, D), BF16),
                        pltpu.VMEM((3, tt, 7 * D), BF16), pltpu.SemaphoreType.DMA((3,))]
        + _ag_sems(na),
        compiler_params=_cp("arbitrary"),
    )(proj, prm, wa, wx, *shards)
    return res[0], res[1], res[2], res[3:]


def _out_proj(merged, x, mod, g_ffn, w_out):
    t_len = x.shape[0]
    tm = min(TMI, t_len)

    def body(mg_ref, x_ref, mod_ref, g_ref, w_ref, x1_ref, h2_ref):
        gt1 = mod_ref[2:3, :]
        gs = g_ref[...] * (1.0 + mod_ref[4:5, :])
        sh = mod_ref[3:4, :]
        for sub in _sub_blocks(tm):
            x1_ref[sub, :] = x_ref[sub, :] + gt1 * _dot(mg_ref[sub, :], w_ref[...])
            for r0 in range(sub.start, sub.stop, 16):
                x1 = x1_ref[r0:r0 + 16, :]
                r = lax.rsqrt(jnp.mean(x1 * x1, axis=-1, keepdims=True) + EPS)
                h2_ref[r0:r0 + 16, :] = (x1 * r * gs + sh).astype(BF16)

    return pl.pallas_call(
        body, name="out_proj", grid=(t_len // tm,),
        in_specs=[pl.BlockSpec((tm, D), lambda i: (i, 0)), pl.BlockSpec((tm, D), lambda i: (i, 0)),
                  pl.BlockSpec((8, D), lambda i: (0, 0)), pl.BlockSpec((1, D), lambda i: (0, 0)),
                  pl.BlockSpec((D, D), lambda i: (0, 0))],
        out_specs=[pl.BlockSpec((tm, D), lambda i: (i, 0)), pl.BlockSpec((tm, D), lambda i: (i, 0))],
        out_shape=[jax.ShapeDtypeStruct((t_len, D), F32), jax.ShapeDtypeStruct((t_len, D), BF16)],
        compiler_params=_cp("parallel"),
    )(merged, x, mod, g_ffn, w_out)


def _ffn_fwd(h2, x1, target, mod, g_fin, w_gu, w_down):
    t_len = x1.shape[0]
    tm = min(TMF, t_len)

    def body(h2_ref, x1_ref, tg_ref, mod_ref, g_ref, wgu_ref, wd_ref, gu_ref, dx2_ref, dx2b_ref, loss_ref, dg_ref, acc):
        @pl.when(pl.program_id(0) == 0)
        def _():
            loss_ref[...] = jnp.zeros_like(loss_ref)
            dg_ref[...] = jnp.zeros_like(dg_ref)

        hb = h2_ref[...]
        ffn = None
        nxt = (_dot_nt(hb, wgu_ref[0, 0]), _dot_nt(hb, wgu_ref[1, 0]))
        for j in range(4):
            gate, up = nxt
            if j < 3:
                nxt = (_dot_nt(hb, wgu_ref[0, j + 1]), _dot_nt(hb, wgu_ref[1, j + 1]))
            gu_ref[0, j] = gate.astype(BF16)
            gu_ref[1, j] = up.astype(BF16)
            act = (gate * _sig(gate) * up).astype(BF16)
            part = _dot(act, wd_ref[j * FB:(j + 1) * FB, :])
            ffn = part if ffn is None else ffn + part
        acc[...] = ffn

        gt2 = mod_ref[5:6, :]
        gf = g_ref[...]

        s_loss = s_dg = jnp.zeros((8, D), F32)
        for r0 in range(0, tm, 16):
            rows = slice(r0, r0 + 16)
            x2 = x1_ref[rows, :] + gt2 * acc[rows, :]
            r = lax.rsqrt(jnp.mean(x2 * x2, axis=-1, keepdims=True) + EPS)
            xn = x2 * r
            diff = xn * gf - tg_ref[rows, :]
            dy = diff * (1.0 / D)
            dxn = dy * gf
            dx2 = r * (dxn - xn * jnp.mean(dxn * xn, axis=-1, keepdims=True))
            dx2_ref[rows, :] = dx2
            dx2b_ref[rows, :] = dx2.astype(BF16)
            s_loss, s_dg = s_loss + _fold8(diff * diff), s_dg + _fold8(dy * xn)
        loss_ref[...] += jnp.sum(s_loss) * (0.5 / D)
        dg_ref[...] += jnp.sum(s_dg, axis=0, keepdims=True)

    row = pl.BlockSpec((tm, D), lambda i: (i, 0))
    return pl.pallas_call(
        body, name="ffn_fwd", grid=(t_len // tm,),
        in_specs=[row, row, row, pl.BlockSpec((8, D), lambda i: (0, 0)), pl.BlockSpec((1, D), lambda i: (0, 0)),
                  _resident((2, 4, FB, D)), _resident((DFF, D))],
        out_specs=[pl.BlockSpec((2, 4, tm, FB), lambda i: (0, 0, i, 0)), row, row,
                   pl.BlockSpec((1, 128), lambda i: (0, 0)), pl.BlockSpec((1, D), lambda i: (0, 0))],
        out_shape=[jax.ShapeDtypeStruct((2, 4, t_len, FB), BF16), jax.ShapeDtypeStruct((t_len, D), F32),
                   jax.ShapeDtypeStruct((t_len, D), BF16),
                   jax.ShapeDtypeStruct((1, 128), F32), jax.ShapeDtypeStruct((1, D), F32)],
        scratch_shapes=[pltpu.VMEM((tm, D), F32)],
        compiler_params=_cp("arbitrary"),
    )(h2, x1, target, mod, g_fin, w_gu, w_down)


S_SH, S_SC, S_G = 0, 1, 2


def _norm_bwd_rows(span, sums, dh_ref, x_ref, dres_ref, scale, gain, write):
    gs = 1.0 + scale
    s_sh, s_sc, s_g = sums
    for r0 in range(span.start, span.stop, 16):
        rows = slice(r0, r0 + 16)
        dh = dh_ref[rows, :]
        xv = x_ref[rows, :]
        r = lax.rsqrt(jnp.mean(xv * xv, axis=-1, keepdims=True) + EPS)
        xn = xv * r
        dhn = dh * gs
        dxn = dhn * gain
        write(rows, dres_ref[rows, :] + r * (dxn - xn * jnp.mean(dxn * xn, axis=-1, keepdims=True)))
        s_sh, s_sc, s_g = s_sh + _fold8(dh), s_sc + _fold8(dh * (xn * gain)), s_g + _fold8(dhn * xn)
    return s_sh, s_sc, s_g


def _add_norm_sums(sums_ref, sums):
    for dst, s in zip((S_SH, S_SC, S_G), sums):
        sums_ref[dst:dst + 1, :] += jnp.sum(s, axis=0, keepdims=True)


def _ffn_bwd(dx2, gu, x1, mod, g_ffn, w_gu, w_down, w_out):
    t_len = x1.shape[0]
    tm = min(TMF, t_len)

    def body(dx2_ref, gu_ref, x1_ref, mod_ref, g_ref, wgu_ref, wd_ref, wo_ref,
             dgu_ref, act_ref, dx1_ref, dx1b_ref, dmg_ref, sums_ref, acc, dmo, dact_s):
        @pl.when(pl.program_id(0) == 0)
        def _():
            sums_ref[...] = jnp.zeros_like(sums_ref)

        dffn = (dx2_ref[...] * mod_ref[5:6, :]).astype(BF16)
        dact_s[0] = _dot_nt(dffn, wd_ref[0:FB, :])
        for j in range(4):
            if j < 3:
                dact_s[(j + 1) % 2] = _dot_nt(dffn, wd_ref[(j + 1) * FB:(j + 2) * FB, :])
            for r0 in range(0, tm, 16):
                rows = slice(r0, r0 + 16)
                dact = dact_s[j % 2, rows, :]
                gate = gu_ref[0, j, rows, :].astype(F32)
                up = gu_ref[1, j, rows, :].astype(F32)
                sg = _sig(gate)
                silu = gate * sg
                act_ref[j, rows, :] = (silu * up).astype(BF16)
                dgu_ref[0, j, rows, :] = (dact * up * (sg * (1.0 + gate * (1.0 - sg)))).astype(BF16)
                dgu_ref[1, j, rows, :] = (dact * silu).astype(BF16)
            part = _dot(dgu_ref[0, j], wgu_ref[0, j]) + _dot(dgu_ref[1, j], wgu_ref[1, j])
            if j == 0:
                acc[...] = part
            else:
                acc[...] += part

        gt1 = mod_ref[2:3, :]

        def write(rows, dx1):
            dx1_ref[rows, :] = dx1
            dx1b_ref[rows, :] = dx1.astype(BF16)
            dmo[rows, :] = (dx1 * gt1).astype(BF16)

        zero = jnp.zeros((8, D), F32)
        sums = (zero, zero, zero)
        for sub in (slice(0, tm // 2), slice(tm // 2, tm)):
            sums = _norm_bwd_rows(sub, sums, acc, x1_ref, dx2_ref, mod_ref[4:5, :], g_ref[...], write)
            dmg_ref[sub, :] = _dot_nt(dmo[sub, :], wo_ref[...]).astype(BF16)
        _add_norm_sums(sums_ref, sums)

    row = pl.BlockSpec((tm, D), lambda i: (i, 0))
    return pl.pallas_call(
        body, name="ffn_bwd", grid=(t_len // tm,),
        in_specs=[row, pl.BlockSpec((2, 4, tm, FB), lambda i: (0, 0, i, 0)), row,
                  pl.BlockSpec((8, D), lambda i: (0, 0)), pl.BlockSpec((1, D), lambda i: (0, 0)),
                  _resident((2, 4, FB, D)), _resident((DFF, D)), _resident((D, D))],
        out_specs=[pl.BlockSpec((2, 4, tm, FB), lambda i: (0, 0, i, 0)),
                   pl.BlockSpec((4, tm, FB), lambda i: (0, i, 0)), row, row, row,
                   pl.BlockSpec((8, D), lambda i: (0, 0))],
        out_shape=[jax.ShapeDtypeStruct((2, 4, t_len, FB), BF16), jax.ShapeDtypeStruct((4, t_len, FB), BF16),
                   jax.ShapeDtypeStruct((t_len, D), F32), jax.ShapeDtypeStruct((t_len, D), BF16),
                   jax.ShapeDtypeStruct((t_len, D), BF16), jax.ShapeDtypeStruct((8, D), F32)],
        scratch_shapes=[pltpu.VMEM((tm, D), F32), pltpu.VMEM((tm, D), BF16), pltpu.VMEM((2, tm, FB), F32)],
        compiler_params=_cp("arbitrary"),
    )(dx2, gu, x1, mod, g_ffn, w_gu, w_down, w_out)


def _my_pos():
    return lax.axis_index("x"), lax.axis_index("y"), lax.axis_index("c")


def _my_index():
    x, y, c = _my_pos()
    return 4 * x + 2 * y + c


def _device_of(b):
    return (b >> 2) & 1, (b >> 1) & 1, b & 1


def _rs_send(src, parts_ref, b, send_sems, recv_sems, local_sem):
    me = _my_index()
    dst = parts_ref.at[me]

    @pl.when(b == me)
    def _():
        pltpu.make_async_copy(src, dst, local_sem).start()

    @pl.when(b != me)
    def _():
        pltpu.make_async_remote_copy(src_ref=src, dst_ref=dst, send_sem=send_sems.at[b], recv_sem=recv_sems.at[me],
                                     device_id=_device_of(b), device_id_type=MESH).start()


def _rs_finish(src_of, parts_ref, send_sems, recv_sems, local_sem):
    me = _my_index()
    for s in range(NDEV):
        @pl.when(s != me)
        def _():
            cp = pltpu.make_async_remote_copy(src_ref=src_of(s), dst_ref=parts_ref.at[s], send_sem=send_sems.at[s],
                                              recv_sem=recv_sems.at[s], device_id=_device_of(s), device_id_type=MESH)
            cp.wait_send()
            cp.wait_recv()

        @pl.when(s == me)
        def _():
            pltpu.make_async_copy(src_of(s), parts_ref.at[s], local_sem).wait()


_RS_SEMS = [pltpu.SemaphoreType.DMA((NDEV,)), pltpu.SemaphoreType.DMA((NDEV,)), pltpu.SemaphoreType.DMA]
_ANY = pl.BlockSpec(memory_space=pl.ANY)


def _xor_order(me, n):
    return (me ^ (n - 1 - jnp.arange(n, dtype=jnp.int32))).astype(jnp.int32)


NCHIP = NDEV // 2


def _rs2_scratch(half_shape):
    blocks = lambda *lead: pltpu.VMEM(lead + tuple(half_shape), BF16)
    return [blocks(NCHIP, 2), blocks(NCHIP)] + [pltpu.SemaphoreType.DMA((NCHIP,))] * 4 + [pltpu.SemaphoreType.DMA]


def _rs2_to_sibling(q, rs):
    stage, from_sib, d_send, d_recv = rs[:4]
    x, y, c = _my_pos()
    pltpu.make_async_remote_copy(src_ref=stage.at[q, 1 - c], dst_ref=from_sib.at[q], send_sem=d_send.at[q],
                                 recv_sem=d_recv.at[q], device_id=(x, y, 1 - c), device_id_type=MESH).start()


def _rs2_forward(q, parts_ref, rs):
    stage, chip_sum, d_send, d_recv, i_send, i_recv, local_sem = rs
    x, y, c = _my_pos()
    my_chip = 2 * x + y
    pltpu.make_async_remote_copy(src_ref=stage.at[q, c], dst_ref=chip_sum.at[q], send_sem=d_send.at[q],
                                 recv_sem=d_recv.at[q], device_id=(x, y, 1 - c), device_id_type=MESH).wait_recv()
    chip_sum[q] = (stage[q, c].astype(F32) + chip_sum[q].astype(F32)).astype(BF16)

    @pl.when(q == my_chip)
    def _():
        pltpu.make_async_copy(chip_sum.at[q], parts_ref.at[my_chip], local_sem).start()

    @pl.when(q != my_chip)
    def _():
        pltpu.make_async_remote_copy(src_ref=chip_sum.at[q], dst_ref=parts_ref.at[my_chip], send_sem=i_send.at[q],
                                     recv_sem=i_recv.at[my_chip], device_id=((q >> 1) & 1, q & 1, c),
                                     device_id_type=MESH).start()


def _rs2_finish(parts_ref, rs):
    stage, chip_sum, d_send, d_recv, i_send, i_recv, local_sem = rs
    x, y, c = _my_pos()
    my_chip = 2 * x + y
    for q in range(NCHIP):
        pltpu.make_async_remote_copy(src_ref=stage.at[q, 1 - c], dst_ref=chip_sum.at[q], send_sem=d_send.at[q],
                                     recv_sem=d_recv.at[q], device_id=(x, y, 1 - c), device_id_type=MESH).wait_send()

        @pl.when(q != my_chip)
        def _():
            cp = pltpu.make_async_remote_copy(src_ref=chip_sum.at[q], dst_ref=parts_ref.at[q], send_sem=i_send.at[q],
                                              recv_sem=i_recv.at[q], device_id=((q >> 1) & 1, q & 1, c),
                                              device_id_type=MESH)
            cp.wait_send()
            cp.wait_recv()

        @pl.when(q == my_chip)
        def _():
            pltpu.make_async_copy(chip_sum.at[q], parts_ref.at[q], local_sem).wait()


def _gu_wgrad(h2, dgu, order):
    t_len = h2.shape[0]
    tk = min(TK, t_len)
    nk = t_len // tk

    def body(ord_ref, h_ref, d_ref, parts_ref, acc, *rs):
        p, k = pl.program_id(0), pl.program_id(1)

        @pl.when(k == 0)
        def _():
            acc[...] = jnp.zeros_like(acc)

        hb = h_ref[...]
        for half in range(2):
            acc[half] += _dot_tn(d_ref[0, half], hb)

        @pl.when(k == nk - 1)
        def _():
            q = ord_ref[p]
            rs[0][q] = acc[...].astype(BF16)
            _rs2_to_sibling(q, rs)

        @pl.when((k == min(1, nk - 1)) & (p > 0))
        def _():
            _rs2_forward(ord_ref[p - 1], parts_ref, rs)

        @pl.when((p == NCHIP - 1) & (k == nk - 1))
        def _():
            _rs2_forward(ord_ref[p], parts_ref, rs)
            _rs2_finish(parts_ref, rs)

    return pl.pallas_call(
        body, name="gu_wgrad",
        grid_spec=pltpu.PrefetchScalarGridSpec(
            num_scalar_prefetch=1, grid=(NCHIP, nk),
            in_specs=[pl.BlockSpec((tk, D), lambda p, k, o: (k, 0)),
                      pl.BlockSpec((1, 2, tk, FB), lambda p, k, o: (o[p], 0, k, 0))],
            out_specs=_ANY,
            scratch_shapes=[pltpu.VMEM((2, FB, D), F32)] + _rs2_scratch((FB, D))),
        out_shape=jax.ShapeDtypeStruct((NCHIP, FB, D), BF16),
        compiler_params=_cp("arbitrary", "arbitrary"),
    )(order, h2, dgu.reshape(NCHIP, 2, t_len, FB))


def _scaled_wgrad(name, a, dx, w, gate_row, mod, order):
    t_len = dx.shape[0]
    kb = w.shape[0] // NCHIP
    tk = min(TK, t_len)
    nk = t_len // tk
    rows = kb // 2
    if a.ndim == 3:
        a_spec = pl.BlockSpec((None, tk, kb), lambda p, k, o: (o[p], k, 0))
    else:
        a_spec = pl.BlockSpec((tk, kb), lambda p, k, o: (k, o[p]))

    def body(ord_ref, a_ref, dx_ref, w_ref, mod_ref, parts_ref, dg_ref, acc, *rs):
        p, k = pl.program_id(0), pl.program_id(1)

        @pl.when((p == 0) & (k == 0))
        def _():
            dg_ref[...] = jnp.zeros_like(dg_ref)

        @pl.when(k == 0)
        def _():
            acc[...] = jnp.zeros_like(acc)

        acc[...] += _dot_tn(a_ref[...], dx_ref[...])

        @pl.when(k == nk - 1)
        def _():
            q = ord_ref[p]
            z = acc[...]
            zg = (z * mod_ref[gate_row:gate_row + 1, :]).astype(BF16)
            dg_ref[0:1, :] += jnp.sum(z * w_ref[...].astype(F32), axis=0, keepdims=True)
            for half in range(2):
                rs[0][q, half] = zg[half * rows:(half + 1) * rows]
            _rs2_to_sibling(q, rs)

        @pl.when((k == min(1, nk - 1)) & (p > 0))
        def _():
            _rs2_forward(ord_ref[p - 1], parts_ref, rs)

        @pl.when((p == NCHIP - 1) & (k == nk - 1))
        def _():
            _rs2_forward(ord_ref[p], parts_ref, rs)
            _rs2_finish(parts_ref, rs)

    return pl.pallas_call(
        body, name=name,
        grid_spec=pltpu.PrefetchScalarGridSpec(
            num_scalar_prefetch=1, grid=(NCHIP, nk),
            in_specs=[a_spec,
                      pl.BlockSpec((tk, D), lambda p, k, o: (k, 0)),
                      pl.BlockSpec((kb, D), lambda p, k, o: (o[p], 0)),
                      pl.BlockSpec((8, D), lambda p, k, o: (0, 0))],
            out_specs=[_ANY, pl.BlockSpec((8, D), lambda p, k, o: (0, 0))],
            scratch_shapes=[pltpu.VMEM((kb, D), F32)] + _rs2_scratch((rows, D))),
        out_shape=[jax.ShapeDtypeStruct((NCHIP, rows, D), BF16), jax.ShapeDtypeStruct((8, D), F32)],
        compiler_params=_cp("arbitrary", "arbitrary"),
    )(order, a, dx, w, mod)


M_WA, M_WB, M_CBIAS, M_BA, M_BX, M_LS = 0, 3, 7, 8, 9, 10


def _conv_bwd_rows(tt, proj_ref, prm_ref, due, dye, dp_ref, acc8):
    row = lax.broadcasted_iota(jnp.int32, (8, CG), 0)
    w_b = [prm_ref[P_WB + k:P_WB + k + 1, :] for k in range(4)]
    w_a = [prm_ref[P_WA + k:P_WA + k + 1, :] for k in range(3)]

    def blk(ib, carry):
        r0 = pl.multiple_of(ib * 16, 16)
        rows = pl.ds(r0, 16)
        for g in range(D // CG):
            cs = slice(g * CG, (g + 1) * CG)
            du16, du_after = due[rows, cs], due[pl.ds(r0 + 16, 8), cs]
            dy16, dy_after = dye[rows, cs], dye[pl.ds(r0 + 16, 8), cs]
            cc16 = _pj(proj_ref, 1, rows, cs).astype(F32)
            cx16 = _pj(proj_ref, 2, rows, cs).astype(F32)
            x16 = _pj(proj_ref, 3, rows, cs).astype(F32)
            v16 = cc16 * cx16
            acc = [acc8[8 * k:8 * k + 8, cs] for k in range(8)]
            drx, dv = [], []
            for sb in range(2):
                lo = slice(8 * sb, 8 * sb + 8)
                duc, dyc, xc, vc = du16[lo], dy16[lo], x16[lo], v16[lo]
                du_n = du16[8:16] if sb == 0 else du_after
                dy_n = dy16[8:16] if sb == 0 else dy_after
                acc[0] = acc[0] + duc
                acc[4] = acc[4] + duc * xc
                d8 = w_b[3][:, cs] * duc
                for s in (1, 2, 3):
                    du_s = _shift_up(duc, du_n, s, row)
                    acc[4 - s] = acc[4 - s] + du_s * xc
                    d8 = d8 + w_b[3 - s][:, cs] * du_s
                acc[7] = acc[7] + dyc * vc
                e8 = w_a[2][:, cs] * dyc
                for s in (1, 2):
                    dy_s = _shift_up(dyc, dy_n, s, row)
                    acc[7 - s] = acc[7 - s] + dy_s * vc
                    e8 = e8 + w_a[2 - s][:, cs] * dy_s
                drx.append(d8)
                dv.append(e8)
            for k in range(8):
                acc8[8 * k:8 * k + 8, cs] = acc[k]
            dv16 = jnp.concatenate(dv, axis=0)
            col = lambda s: slice(s * D + g * CG, s * D + (g + 1) * CG)
            dp_ref[rows, col(3)] = jnp.concatenate(drx, axis=0).astype(BF16)
            dp_ref[rows, col(1)] = (dv16 * cx16).astype(BF16)
            dp_ref[rows, col(2)] = (dv16 * cc16).astype(BF16)
        return carry

    lax.fori_loop(0, tt // 16, blk, 0)


def _mixer_bwd(proj, hl, sv, dmg, prm, wa, wx):
    t_len = proj.shape[0]
    tt = min(TT, t_len)
    nt = t_len // tt
    hb8 = tt // 8

    def rev(i):
        return nt - 1 - i

    def halo(i):
        return jnp.maximum(rev(i) * hb8 - 1, 0)

    def body(proj_ref, hl_ref, hh_ref, sv_ref, dmg_ref, prm_ref, wa_ref, wx_ref,
             dp_ref, sums_ref, gwa_ref, gwx_ref,
             he, due, dye, drp_s, dip_s, an, gn, acc8):
        i = pl.program_id(0)
        t = rev(i)

        @pl.when(i == 0)
        def _():
            sums_ref[...] = jnp.zeros_like(sums_ref)
            gwa_ref[...] = jnp.zeros_like(gwa_ref)
            gwx_ref[...] = jnp.zeros_like(gwx_ref)
            due[tt:tt + 8, :] = jnp.zeros((8, D), F32)
            dye[tt:tt + 8, :] = jnp.zeros((8, D), F32)
            an[...] = jnp.zeros((8, D), F32)
            gn[...] = jnp.zeros((8, D), F32)

        live = (t > 0).astype(F32)
        he[0:8, :] = hh_ref[...].astype(F32) * live
        he[8:8 + tt, :] = hl_ref[...].astype(F32)

        ls_all = _log_sigmoid(prm_ref[P_LAM:P_LAM + 1, :])
        row = lax.broadcasted_iota(jnp.int32, (8, CG), 0)
        nblk = tt // 16

        def blk(ib, carry):
            r0 = pl.multiple_of((nblk - 1 - ib) * 16, 16)
            rows = pl.ds(r0, 16)
            for g in range(D // CG):
                cs = slice(g * CG, (g + 1) * CG)
                ls = ls_all[:, cs]
                dm = dmg_ref[rows, cs].astype(F32)
                cb = _pj(proj_ref, 0, rows, cs).astype(F32)
                rg = _pj(proj_ref, 4, rows, cs).astype(F32)
                sga = _sig(_pj(proj_ref, 5, rows, cs).astype(F32))
                sgb = _sig(_pj(proj_ref, 6, rows, cs).astype(F32))
                ya0 = sv_ref[SV_YA, rows, cs]
                h16 = he[pl.ds(r0 + 8, 16), cs]
                gl, th = _gelu(rg)
                dgl = 0.5 * (1.0 + th) + 0.5 * rg * (1.0 - th * th) * (_GC * (1.0 + 3.0 * 0.044715 * rg * rg))
                y_a = cb * ya0
                y_b = h16 * gl
                dy_a = dm * sga
                dy_b = dm * sgb
                col = lambda s: slice(s * D + g * CG, s * D + (g + 1) * CG)
                dp_ref[rows, col(5)] = (dm * y_a * sga * (1.0 - sga)).astype(BF16)
                dp_ref[rows, col(6)] = (dm * y_b * sgb * (1.0 - sgb)).astype(BF16)
                dp_ref[rows, col(4)] = (dy_b * h16 * dgl).astype(BF16)
                dp_ref[rows, col(0)] = (dy_a * ya0).astype(BF16)
                dye[rows, cs] = dy_a * cb
                dh16 = dy_b * gl

                a_next = an[:, cs]
                g_next = gn[:, cs]
                s_ba = jnp.zeros((8, CG), F32)
                s_bx = jnp.zeros((8, CG), F32)
                s_ls = jnp.zeros((8, CG), F32)
                for sb in (1, 0):
                    rr = r0 + 8 * sb
                    first = (row + (t * tt + rr)) == 0
                    r8 = pl.ds(rr, 8)
                    uu, r, ig, a, mult = (sv_ref[pln, r8, cs] for pln in (SV_U, SV_R, SV_I, SV_A, SV_MULT))
                    ca = jnp.where(row < 7, pltpu.roll(a, 7, 0), a_next)
                    cb_ = dh16[8 * sb:8 * sb + 8, :]
                    for s in (1, 2, 4):
                        a_sh = jnp.where(row < 8 - s, pltpu.roll(ca, 8 - s, 0), 1.0)
                        b_sh = jnp.where(row < 8 - s, pltpu.roll(cb_, 8 - s, 0), 0.0)
                        cb_ = ca * b_sh + cb_
                        ca = ca * a_sh
                    gv = ca * g_next + cb_
                    g_next = jnp.broadcast_to(gv[0:1, :], gv.shape)
                    a_next = jnp.broadcast_to(a[0:1, :], a.shape)
                    hprev = jnp.where(row >= 1, pltpu.roll(he[pl.ds(rr + 8, 8), cs], 1, 0),
                                      pltpu.roll(he[pl.ds(rr, 8), cs], 1, 0))
                    da = gv * hprev
                    dmult = jnp.where(first, 0.0, gv * ig * uu)
                    dla = da * a + jnp.where(mult > 0.0, dmult * (-(a * a) / mult), 0.0)
                    drp = dla * (LRU_C * ls) * r * (1.0 - r)
                    dip = gv * mult * uu * ig * (1.0 - ig)
                    s_ls = s_ls + dla * (LRU_C * r)
                    s_ba = s_ba + drp
                    s_bx = s_bx + dip
                    drp_s[pl.ds(rr, 8), cs] = drp
                    dip_s[pl.ds(rr, 8), cs] = dip
                    due[pl.ds(rr, 8), cs] = gv * mult * ig
                an[:, cs] = a_next
                gn[:, cs] = g_next
                sums_ref[M_BA:M_BA + 1, cs] += jnp.sum(s_ba, axis=0, keepdims=True)
                sums_ref[M_BX:M_BX + 1, cs] += jnp.sum(s_bx, axis=0, keepdims=True)
                sums_ref[M_LS:M_LS + 1, cs] += jnp.sum(s_ls, axis=0, keepdims=True)
            return carry

        lax.fori_loop(0, nblk, blk, 0)

        drp_b = drp_s[...].astype(BF16)
        dip_b = dip_s[...].astype(BF16)
        ub = sv_ref[SV_U].astype(BF16)
        for h in range(HEADS):
            cs = slice(h * HB, (h + 1) * HB)
            due[0:tt, cs] += _dot_nt(drp_b[:, cs], wa_ref[h]) + _dot_nt(dip_b[:, cs], wx_ref[h])
            gwa_ref[h] += _dot_tn(ub[:, cs], drp_b[:, cs])
            gwx_ref[h] += _dot_tn(ub[:, cs], dip_b[:, cs])

        acc8[...] = jnp.zeros_like(acc8)
        _conv_bwd_rows(tt, proj_ref, prm_ref, due, dye, dp_ref, acc8)
        for k, dst in enumerate([M_CBIAS] + [M_WB + k for k in range(4)] + [M_WA + k for k in range(3)]):
            sums_ref[dst:dst + 1, :] += jnp.sum(acc8[8 * k:8 * k + 8, :], axis=0, keepdims=True)
        due[tt:tt + 8, :] = due[0:8, :]
        dye[tt:tt + 8, :] = dye[0:8, :]

        @pl.when(i == nt - 1)
        def _():
            sums_ref[M_LS:M_LS + 1, :] = sums_ref[M_LS:M_LS + 1, :] * _sig(-prm_ref[P_LAM:P_LAM + 1, :])

    big = lambda: pltpu.VMEM((tt + 8, D), F32)
    tile = lambda: pltpu.VMEM((tt, D), F32)
    return pl.pallas_call(
        body, name="mixer_bwd", grid=(nt,),
        in_specs=[pl.BlockSpec((tt, 7 * D), lambda i: (rev(i), 0)),
                  pl.BlockSpec((tt, D), lambda i: (rev(i), 0)),
                  pl.BlockSpec((8, D), lambda i: (halo(i), 0)),
                  pl.BlockSpec((len(SV_PLANES), tt, D), lambda i: (0, rev(i), 0)),
                  pl.BlockSpec((tt, D), lambda i: (rev(i), 0)),
                  pl.BlockSpec((16, D), lambda i: (0, 0)),
                  pl.BlockSpec((HEADS, HB, HB), lambda i: (0, 0, 0)),
                  pl.BlockSpec((HEADS, HB, HB), lambda i: (0, 0, 0))],
        out_specs=[pl.BlockSpec((tt, 7 * D), lambda i: (rev(i), 0)),
                   pl.BlockSpec((16, D), lambda i: (0, 0)),
                   pl.BlockSpec((HEADS, HB, HB), lambda i: (0, 0, 0)),
                   pl.BlockSpec((HEADS, HB, HB), lambda i: (0, 0, 0))],
        out_shape=[jax.ShapeDtypeStruct((t_len, 7 * D), BF16), jax.ShapeDtypeStruct((16, D), F32),
                   jax.ShapeDtypeStruct((HEADS, HB, HB), F32), jax.ShapeDtypeStruct((HEADS, HB, HB), F32)],
        scratch_shapes=[big(), big(), big(), tile(), tile(),
                        pltpu.VMEM((8, D), F32), pltpu.VMEM((8, D), F32), pltpu.VMEM((64, D), F32)],
        compiler_params=_cp("arbitrary"),
    )(proj, hl, hl, sv, dmg, prm, wa, wx)


def _in_proj_bwd(dproj, w_in, x, dx1, mod, g_mix):
    t_len = x.shape[0]
    tm = min(TM, t_len)

    def body(dp_ref, w_ref, x_ref, dx1_ref, mod_ref, g_ref, gx_ref, sums_ref, acc):
        @pl.when(pl.program_id(0) == 0)
        def _():
            sums_ref[...] = jnp.zeros_like(sums_ref)

        def write(rows, dx):
            gx_ref[rows, :] = dx

        zero = jnp.zeros((8, D), F32)
        sums = (zero, zero, zero)
        for sub in _sub_blocks(tm):
            acc[sub, :] = _dot_nt(dp_ref[sub, :], w_ref[...])
            sums = _norm_bwd_rows(sub, sums, acc, x_ref, dx1_ref, mod_ref[1:2, :], g_ref[...], write)
        _add_norm_sums(sums_ref, sums)

    return pl.pallas_call(
        body, name="in_proj_bwd", grid=(t_len // tm,),
        in_specs=[pl.BlockSpec((tm, 7 * D), lambda i: (i, 0)),
                  _resident((D, 7 * D)),
                  pl.BlockSpec((tm, D), lambda i: (i, 0)), pl.BlockSpec((tm, D), lambda i: (i, 0)),
                  pl.BlockSpec((8, D), lambda i: (0, 0)), pl.BlockSpec((1, D), lambda i: (0, 0))],
        out_specs=[pl.BlockSpec((tm, D), lambda i: (i, 0)), pl.BlockSpec((8, D), lambda i: (0, 0))],
        out_shape=[jax.ShapeDtypeStruct((t_len, D), F32), jax.ShapeDtypeStruct((8, D), F32)],
        scratch_shapes=[pltpu.VMEM((tm, D), F32)],
        compiler_params=_cp("arbitrary"),
    )(dproj, w_in, x, dx1, mod, g_mix)


def _in_wgrad(h, dproj, g_wa, g_wx, order):
    t_len = h.shape[0]
    tk = min(TKI, t_len)
    nk = t_len // tk
    cw = 7 * D // NDEV
    hr = HB // NDEV

    def body(ord_ref, h_ref, d_ref, ga_ref, gx_ref, parts_ref, pa_ref, px_ref, acc, *scr):
        rs, sems = scr[:-6], scr[-6:]
        p, k = pl.program_id(0), pl.program_id(1)

        def head_rows(ref):
            return lambda s: ref.at[:, pl.ds(s * hr, hr), :]

        @pl.when((p == 0) & (k == 0))
        def _():
            for s in range(NDEV):
                _rs_send(head_rows(ga_ref)(s), pa_ref, s, *sems[0:3])
                _rs_send(head_rows(gx_ref)(s), px_ref, s, *sems[3:6])

        @pl.when(k == 0)
        def _():
            acc[...] = jnp.zeros_like(acc)

        acc[...] += _dot_tn(h_ref[...], d_ref[...])

        @pl.when(k == nk - 1)
        def _():
            q = ord_ref[p]
            for half in range(2):
                rs[0][q, half] = acc[:, half * cw:(half + 1) * cw].astype(BF16)
            _rs2_to_sibling(q, rs)

        @pl.when((k == min(1, nk - 1)) & (p > 0))
        def _():
            _rs2_forward(ord_ref[p - 1], parts_ref, rs)

        @pl.when((p == NCHIP - 1) & (k == nk - 1))
        def _():
            _rs2_forward(ord_ref[p], parts_ref, rs)
            _rs2_finish(parts_ref, rs)
            _rs_finish(head_rows(ga_ref), pa_ref, *sems[0:3])
            _rs_finish(head_rows(gx_ref), px_ref, *sems[3:6])

    return pl.pallas_call(
        body, name="in_wgrad",
        grid_spec=pltpu.PrefetchScalarGridSpec(
            num_scalar_prefetch=1, grid=(NCHIP, nk),
            in_specs=[pl.BlockSpec((tk, D), lambda p, k, o: (k, 0)),
                      pl.BlockSpec((tk, 2 * cw), lambda p, k, o: (k, o[p])), _ANY, _ANY],
            out_specs=[_ANY, _ANY, _ANY],
            scratch_shapes=[pltpu.VMEM((D, 2 * cw), F32)] + _rs2_scratch((D, cw)) + _RS_SEMS * 2),
        out_shape=[jax.ShapeDtypeStruct((NCHIP, D, cw), BF16), jax.ShapeDtypeStruct((NDEV, HEADS, hr, HB), F32),
                   jax.ShapeDtypeStruct((NDEV, HEADS, hr, HB), F32)],
        compiler_params=_cp("arbitrary", "arbitrary"),
    )(order, h, dproj, g_wa, g_wx)


def _adam_math(w, g, m, v):
    m = ADAM_B1 * m + (1.0 - ADAM_B1) * g
    v = ADAM_B2 * v + (1.0 - ADAM_B2) * (g * g)
    m_hat = m / (1.0 - ADAM_B1 ** ADAM_STEP)
    v_hat = v / (1.0 - ADAM_B2 ** ADAM_STEP)
    delta = -ADAM_LR * (m_hat / (jnp.sqrt(v_hat) + ADAM_EPS) + ADAM_WD * w)
    return delta, m, v


def _ada_bwd(c_all, dmod_cols, w, m, v):
    rb = 256
    n = w.shape[1]
    nrow = c_all.shape[0]

    def body(c_ref, d_ref, w_ref, m_ref, v_ref, g_ref, dl_ref, nm_ref, nv_ref):
        cv = c_ref[...]
        g = _dot_tn((cv * _sig(cv)).astype(BF16), d_ref[...].astype(BF16))
        g_ref[...] = g
        dl_ref[...], nm_ref[...], nv_ref[...] = _adam_math(w_ref[...], g, m_ref[...], v_ref[...])

    blk = pl.BlockSpec((rb, n), lambda i: (i, 0))
    sds = jax.ShapeDtypeStruct(w.shape, F32)
    return pl.pallas_call(
        body, name="ada_bwd", grid=(D // rb,),
        in_specs=[pl.BlockSpec((nrow, rb), lambda i: (0, i)), pl.BlockSpec((nrow, n), lambda i: (0, 0)), blk, blk, blk],
        out_specs=[blk, blk, blk, blk], out_shape=[sds, sds, sds, sds],
        compiler_params=_cp("parallel"),
    )(c_all, dmod_cols, w, m, v)


def _adam(name, parts, w, m, v):
    p, r, c = parts.shape
    rb = max([cand for cand in range(8, min(r, 256) + 1, 8) if r % cand == 0], default=r)

    def body(p_ref, w_ref, m_ref, v_ref, g_ref, dl_ref, nm_ref, nv_ref):
        g = p_ref[0].astype(F32)
        for q in range(1, p):
            g = g + p_ref[q].astype(F32)
        g_ref[...] = g
        dl_ref[...], nm_ref[...], nv_ref[...] = _adam_math(w_ref[...], g, m_ref[...], v_ref[...])

    blk = pl.BlockSpec((rb, c), lambda i: (i, 0))
    sds = jax.ShapeDtypeStruct((r, c), F32)
    return pl.pallas_call(
        body, name=name, grid=(r // rb,),
        in_specs=[pl.BlockSpec((p, rb, c), lambda i: (0, i, 0)), blk, blk, blk],
        out_specs=[blk, blk, blk, blk], out_shape=[sds, sds, sds, sds],
        compiler_params=_cp("parallel"),
    )(parts, w, m, v)


_SMALL_SEMS = [pltpu.SemaphoreType.DMA((7,)), pltpu.SemaphoreType.DMA((7,)), pltpu.SemaphoreType.DMA]
_VMEM = pl.BlockSpec(memory_space=pltpu.VMEM)


def _exchange_small(x_ref, out_ref, send_sems, recv_sems, local_sem):
    m_per = x_ref.shape[0]
    x, y, c = _my_pos()
    me, sibling = (x, y, c), (x, y, 1 - c)
    chips = [(1 - x, y), (x, 1 - y), (1 - x, 1 - y)]

    def rows(px, py, pc):
        return out_ref.at[pl.ds((4 * px + 2 * py + pc) * m_per, m_per), :]

    def copy(k, block, to, src=None):
        return pltpu.make_async_remote_copy(
            src_ref=rows(*block) if src is None else src, dst_ref=rows(*block),
            send_sem=send_sems.at[k], recv_sem=recv_sems.at[k], device_id=to, device_id_type=MESH)

    mine = pltpu.make_async_copy(x_ref, rows(*me), local_sem)
    mine.start()
    first = [copy(0, me, sibling, src=x_ref)]
    first += [copy(1 + j, me, (*chip, c), src=x_ref) for j, chip in enumerate(chips)]
    for cp in first:
        cp.start()
    passed = [copy(4 + j, (*chip, c), sibling) for j, chip in enumerate(chips)]
    for j, chip in enumerate(chips):
        copy(1 + j, (*chip, c), me).wait_recv()
        passed[j].start()
    copy(0, sibling, me).wait_recv()
    for j, chip in enumerate(chips):
        copy(4 + j, (*chip, 1 - c), me).wait_recv()
    for cp in first + passed:
        cp.wait_send()
    mine.wait()


def _gather_small_grads(sums1, sums2, msums, d_gt1, d_gt2, d_gfin, loss):
    def body(s1, s2, ms, g1, g2, gf, ls, out_ref, pack, *sems):
        rows = [s1[S_SH:S_SH + 1, :], s1[S_SC:S_SC + 1, :], g1[0:1, :],
                s2[S_SH:S_SH + 1, :], s2[S_SC:S_SC + 1, :], g2[0:1, :],
                s1[S_G:S_G + 1, :], ms[M_CBIAS:M_CBIAS + 1, :], ms[M_BA:M_BA + 1, :], ms[M_BX:M_BX + 1, :],
                ms[M_LS:M_LS + 1, :], s2[S_G:S_G + 1, :], gf[...]]
        rows += [ms[M_WA + k:M_WA + k + 1, :] for k in range(3)] + [ms[M_WB + k:M_WB + k + 1, :] for k in range(4)]
        rows += [jnp.broadcast_to(ls[0:1, 0:1], (1, D))]
        for i, v in enumerate(rows):
            pack[i:i + 1, :] = v
        pack[len(rows):24, :] = jnp.zeros((24 - len(rows), D), F32)
        _exchange_small(pack, out_ref, *sems)

    return pl.pallas_call(
        body, name="gather_small", out_shape=jax.ShapeDtypeStruct((NDEV * 24, D), F32),
        in_specs=[_VMEM] * 7, out_specs=_VMEM, scratch_shapes=[pltpu.VMEM((24, D), F32)] + _SMALL_SEMS,
    )(sums1, sums2, msums, d_gt1, d_gt2, d_gfin, loss)


def _ada_mod(pack, w_ada, b_cols):
    ncol = w_ada.shape[1]

    def body(p_ref, w_ref, b_ref, all_ref, mod_ref, cols, *sems):
        _exchange_small(p_ref, all_ref, *sems[0:3])
        c_all = jnp.concatenate([all_ref[8 * d:8 * d + 1, 0:D] for d in range(NDEV)], axis=0)
        c16 = jnp.concatenate([c_all, jnp.zeros_like(c_all)], axis=0)
        mod16 = _dot((c16 * _sig(c16)).astype(BF16), w_ref[...].astype(BF16)) + b_ref[...]
        cols[...] = mod16[0:NDEV]
        _exchange_small(cols, mod_ref, *sems[3:6])

    return pl.pallas_call(
        body, name="ada_mod",
        out_shape=[jax.ShapeDtypeStruct((NDEV * 8, pack.shape[1]), F32), jax.ShapeDtypeStruct((NDEV * 8, ncol), F32)],
        in_specs=[_VMEM, _VMEM, _VMEM], out_specs=[_VMEM, _VMEM],
        scratch_shapes=[pltpu.VMEM((NDEV, ncol), F32)] + _SMALL_SEMS * 2,
        compiler_params=_cp(),
    )(pack, w_ada, b_cols)


def _blk_rows(n):
    return lambda ref, b: ref.at[pl.ds(pl.multiple_of(b * n, 8), n), :]


def _blk_lead(ref, b):
    return ref.at[b]


def _blk_heads(ref, b):
    return ref.at[:, pl.ds(pl.multiple_of(b * (HB // NDEV), 8), HB // NDEV), :]


def _ag_phases(ins, outs, slicers, send_sems, recv_sems, local_sems):
    na = len(ins)
    x, y, c = _my_pos()
    me, sibling = (x, y, c), (x, y, 1 - c)
    chips = [(1 - x, y), (x, 1 - y), (1 - x, 1 - y)]

    def copy(a, k, block, to, from_shard=False):
        px, py, pc = block
        dst = slicers[a](outs[a], 4 * px + 2 * py + pc)
        return pltpu.make_async_remote_copy(
            src_ref=ins[a] if from_shard else dst, dst_ref=dst,
            send_sem=send_sems.at[a * 7 + k], recv_sem=recv_sems.at[a * 7 + k], device_id=to, device_id_type=MESH)

    def local(a):
        return pltpu.make_async_copy(ins[a], slicers[a](outs[a], 4 * x + 2 * y + c), local_sems.at[a])

    def firsts(a):
        return [copy(a, 0, me, sibling, True)] + [copy(a, 1 + j, me, (*chip, c), True) for j, chip in enumerate(chips)]

    def start():
        for a in range(na):
            local(a).start()
            for cp in firsts(a):
                cp.start()

    def forward():
        for a in range(na):
            for j, chip in enumerate(chips):
                copy(a, 1 + j, (*chip, c), me).wait_recv()
                copy(a, 4 + j, (*chip, c), sibling).start()

    def finish():
        for a in range(na):
            copy(a, 0, sibling, me).wait_recv()
            for j, chip in enumerate(chips):
                copy(a, 4 + j, (*chip, 1 - c), me).wait_recv()
        for a in range(na):
            for cp in firsts(a) + [copy(a, 4 + j, (*chip, c), sibling) for j, chip in enumerate(chips)]:
                cp.wait_send()
            local(a).wait()

    return start, forward, finish


def _ag_sems(na):
    return [pltpu.SemaphoreType.DMA((7 * na,)), pltpu.SemaphoreType.DMA((7 * na,)), pltpu.SemaphoreType.DMA((na,))]


def _local_step(x, target, mod, g_mix, g_ffn, g_fin, prm, w_in_shard, shards):
    fulls = [(HEADS, HB, HB), (HEADS, HB, HB), (D, D), (NDEV, FB, D), (DFF, D)]
    slicers = [_blk_heads, _blk_heads, _blk_rows(D // NDEV), _blk_lead, _blk_rows(DFF // NDEV)]
    my_chip = _my_index() >> 1
    own_first = (my_chip ^ jnp.arange(NCHIP, dtype=jnp.int32)).astype(jnp.int32)
    early, late = [0, 1, 2, 4], [3]
    pick = lambda lst, idx: [lst[i] for i in idx]
    proj, h, w_in, (wa, wx, w_out, w_down) = _in_proj(x, mod, g_mix, w_in_shard, own_first, pick(shards, early),
                                                      pick(fulls, early), pick(slicers, early))
    merged, hl, sv, (w_gu,) = _mixer_fwd(proj, prm, wa, wx, pick(shards, late), pick(fulls, late),
                                         pick(slicers, late))
    w_gu = w_gu.reshape(2, 4, FB, D)
    x1, h2 = _out_proj(merged, x, mod, g_ffn, w_out)
    gu, dx2, dx2b, loss, d_gfin = _ffn_fwd(h2, x1, target, mod, g_fin, w_gu, w_down)
    dgu, act, dx1, dx1b, dmg, sums2 = _ffn_bwd(dx2, gu, x1, mod, g_ffn, w_gu, w_down, w_out)
    chip_order = _xor_order(_my_index() >> 1, NCHIP)
    p_wgu = _gu_wgrad(h2, dgu, chip_order)
    p_wdown, d_gt2 = _scaled_wgrad("down_wgrad", act, dx2b, w_down, 5, mod, chip_order)
    p_wout, d_gt1 = _scaled_wgrad("out_wgrad", merged, dx1b, w_out, 2, mod, chip_order)
    dproj, msums, g_wa, g_wx = _mixer_bwd(proj, hl, sv, dmg, prm, wa, wx)
    p_win, p_wa, p_wx = _in_wgrad(h, dproj, g_wa, g_wx, chip_order)
    grad_x, sums1 = _in_proj_bwd(dproj, w_in, x, dx1, mod, g_mix)
    return dict(loss=loss, grad_x=grad_x, d_gfin=d_gfin, sums1=sums1, sums2=sums2, msums=msums,
                d_gt1=d_gt1, d_gt2=d_gt2, p_win=p_win, p_wa=p_wa, p_wx=p_wx, p_wout=p_wout, p_wgu=p_wgu,
                p_wdown=p_wdown)


def kernel(x, c, w_ada, b_ada, g_norm_mix, w_in, conv_a_w, conv_b_w, conv_b_bias, w_rg_a, b_rg_a, w_rg_x, b_rg_x, lru_lambda, w_out, g_norm_ffn, w_gate_up, w_down, g_norm_final, loss_target, m_w_ada, m_b_ada, m_g_norm_mix, m_w_in, m_conv_a_w, m_conv_b_w, m_conv_b_bias, m_w_rg_a, m_b_rg_a, m_w_rg_x, m_b_rg_x, m_lru_lambda, m_w_out, m_g_norm_ffn, m_w_gate_up, m_w_down, m_g_norm_final, v_w_ada, v_b_ada, v_g_norm_mix, v_w_in, v_conv_a_w, v_conv_b_w, v_conv_b_bias, v_w_rg_a, v_b_rg_a, v_w_rg_x, v_b_rg_x, v_lru_lambda, v_w_out, v_g_norm_ffn, v_w_gate_up, v_w_down, v_g_norm_final):
    me = 4 * lax.axis_index("x") + 2 * lax.axis_index("y") + lax.axis_index("c")
    ncol = w_ada.shape[2]
    cw = conv_a_w.shape[2]

    pack0 = jnp.concatenate([c, conv_a_w.reshape(1, 3 * cw), conv_b_w.reshape(1, 4 * cw)], axis=1)
    b_cols = lax.dynamic_slice_in_dim(b_ada, me * ncol, ncol, axis=1)
    got0, got1 = _ada_mod(jnp.broadcast_to(pack0, (8, pack0.shape[1])), w_ada[0], b_cols)
    got0 = got0.reshape(NDEV, 8, -1)[:, 0, :]
    c_all = got0[:, :D]
    conv_a = got0[:, D:D + 3 * cw].reshape(NDEV, 3, cw).transpose(1, 0, 2).reshape(3, D)
    conv_b = got0[:, D + 3 * cw:].reshape(NDEV, 4, cw).transpose(1, 0, 2).reshape(4, D)
    c16 = jnp.concatenate([c_all, jnp.zeros((8, D), F32)], axis=0)
    mod6 = lax.dynamic_index_in_dim(got1.reshape(NDEV, NDEV, ncol), me, axis=1, keepdims=False).reshape(6, D)
    mod = jnp.concatenate([mod6, jnp.zeros((2, D), F32)], axis=0)

    tr = lambda a: jnp.swapaxes(a, 1, 2)
    shards = [w_rg_a[0].astype(BF16), w_rg_x[0].astype(BF16), w_out[0].astype(BF16), tr(w_gate_up)[0].astype(BF16),
              w_down[0].astype(BF16)]

    prm = jnp.concatenate([conv_a, conv_b, conv_b_bias, b_rg_a, b_rg_x, lru_lambda, jnp.zeros((5, D), F32)], axis=0)
    r = _local_step(x[0], loss_target[0], mod, g_norm_mix, g_norm_ffn, g_norm_final.reshape(1, D), prm,
                    w_in[0].astype(BF16), shards)

    parts = [r["p_win"], r["p_wa"], r["p_wx"], r["p_wout"], r["p_wgu"], r["p_wdown"]]
    big = {}
    for nm, p, w, m, v in (("w_in", parts[0], w_in, m_w_in, v_w_in), ("w_rg_a", parts[1], w_rg_a, m_w_rg_a, v_w_rg_a),
                           ("w_rg_x", parts[2], w_rg_x, m_w_rg_x, v_w_rg_x), ("w_out", parts[3], w_out, m_w_out, v_w_out),
                           ("w_gate_up", parts[4], tr(w_gate_up), tr(m_w_gate_up), tr(v_w_gate_up)),
                           ("w_down", parts[5], w_down, m_w_down, v_w_down)):
        two_d = (-1, w.shape[-1])
        outs = _adam("adam_" + nm, p.reshape((p.shape[0],) + w.reshape(two_d).shape), w.reshape(two_d), m.reshape(two_d),
                     v.reshape(two_d))
        big[nm] = [o.reshape(w.shape) for o in outs]
    big["w_gate_up"] = [tr(o) for o in big["w_gate_up"]]

    got2 = _gather_small_grads(r["sums1"], r["sums2"], r["msums"], r["d_gt1"], r["d_gt2"], r["d_gfin"],
                               r["loss"]).reshape(NDEV, 24, D)

    rep_w = jnp.concatenate([b_ada.reshape(6, D), g_norm_mix, conv_b_bias, b_rg_a, b_rg_x, lru_lambda, g_norm_ffn,
                             g_norm_final.reshape(1, D), jnp.zeros((3, D), F32)], axis=0)
    rep_m = jnp.concatenate([m_b_ada.reshape(6, D), m_g_norm_mix, m_conv_b_bias, m_b_rg_a, m_b_rg_x, m_lru_lambda,
                             m_g_norm_ffn, m_g_norm_final.reshape(1, D), jnp.zeros((3, D), F32)], axis=0)
    rep_v = jnp.concatenate([v_b_ada.reshape(6, D), v_g_norm_mix, v_conv_b_bias, v_b_rg_a, v_b_rg_x, v_lru_lambda,
                             v_g_norm_ffn, v_g_norm_final.reshape(1, D), jnp.ones((3, D), F32)], axis=0)
    rep = _adam("adam_rep", got2[:, :16, :], rep_w, rep_m, rep_v)

    conv_parts = lax.dynamic_slice_in_dim(got2[:, 13:21, :], me * cw, cw, axis=2)
    cv_w = jnp.concatenate([conv_a_w[0], conv_b_w[0], jnp.zeros((1, cw), F32)], axis=0)
    cv_m = jnp.concatenate([m_conv_a_w[0], m_conv_b_w[0], jnp.zeros((1, cw), F32)], axis=0)
    cv_v = jnp.concatenate([v_conv_a_w[0], v_conv_b_w[0], jnp.ones((1, cw), F32)], axis=0)
    cvo = _adam("adam_conv", conv_parts, cv_w, cv_m, cv_v)

    dmod_cols = lax.dynamic_slice_in_dim(got2[:, :6, :].reshape(NDEV, 6 * D), me * ncol, ncol, axis=1)
    dmod16 = jnp.concatenate([dmod_cols, jnp.zeros((8, ncol), F32)], axis=0)
    ada = _ada_bwd(c16, dmod16, w_ada[0], m_w_ada[0], v_w_ada[0])

    loss = jnp.sum(got2[:, 20, 0])

    def pick(q):
        one = lambda i: rep[q][i:i + 1]
        return [ada[q].reshape(w_ada.shape), rep[q][0:6].reshape(b_ada.shape), one(6), big["w_in"][q],
                cvo[q][0:3].reshape(conv_a_w.shape), cvo[q][3:7].reshape(conv_b_w.shape), one(7),
                big["w_rg_a"][q], one(8), big["w_rg_x"][q], one(9), one(10), big["w_out"][q], one(11),
                big["w_gate_up"][q], big["w_down"][q], rep[q][12]]

    return (loss, r["grad_x"].reshape(x.shape), *pick(0), *pick(1), *pick(2), *pick(3))
```

```python
import math

import jax
import jax.numpy as jnp
from jax import lax
from jax.experimental import pallas as pl
from jax.experimental.pallas import tpu as pltpu

F32 = jnp.float32
BF16 = jnp.bfloat16

D = 1024
DFF = 2816
NDEV = 8
HEADS = 4
HB = D // HEADS
FB = DFF // 4
EPS = 1e-6
LRU_C = 8.0
ADAM_LR, ADAM_B1, ADAM_B2, ADAM_EPS, ADAM_WD, ADAM_STEP = 0.001, 0.9, 0.999, 1e-08, 0.01, 10

VMEM_LIMIT = 56 * 1024 * 1024
TM = 512
TMI = 1024
TMF = 256
TK = 2048
TKI = 2048
SUB = 256
TT = 256
CG = 256
MESH = pl.DeviceIdType.MESH


def _cp(*sem):
    return pltpu.CompilerParams(dimension_semantics=sem, vmem_limit_bytes=VMEM_LIMIT)


def _sig(x):
    return 1.0 / (1.0 + jnp.exp(-x))


def _log_sigmoid(x):
    z = jnp.exp(-jnp.abs(x))
    u = 1.0 + z
    d = u - 1.0
    l1p = jnp.where(d == 0.0, z, jnp.log(u) * (z / jnp.where(d == 0.0, 1.0, d)))
    return -(jnp.maximum(-x, 0.0) + l1p)


def _neg_expm1(x):
    p = x * (1.0 + x * 0.5 * (1.0 + x * (1.0 / 3.0) * (1.0 + x * 0.25 * (1.0 + x * 0.2 * (1.0 + x * (1.0 / 6.0))))))
    return jnp.where(x > -0.25, -p, 1.0 - jnp.exp(x))


_GC = math.sqrt(2.0 / math.pi)


def _gelu(x):
    t = jnp.tanh(_GC * (x + 0.044715 * x * x * x))
    return 0.5 * x * (1.0 + t), t


def _dot(a, b):
    return jnp.dot(a, b, preferred_element_type=F32)


def _dot_nt(a, b):
    return lax.dot_general(a, b, (((1,), (1,)), ((), ())), preferred_element_type=F32)


def _dot_tn(a, b):
    return lax.dot_general(a, b, (((0,), (0,)), ((), ())), preferred_element_type=F32)


def _resident(shape):
    return pl.BlockSpec(shape, lambda *_: (0,) * len(shape), pipeline_mode=pl.Buffered(1))


def _sub_blocks(n_rows):
    step = min(SUB, n_rows)
    return [slice(r, r + step) for r in range(0, n_rows, step)]


def _fold8(v):
    return v[0:8] + v[8:16]


def _pj(ref, s, rows=slice(None), cols=slice(0, D)):
    return ref[rows, s * D + cols.start:s * D + cols.stop]


def _in_proj(x, mod, g_mix, w_shard, order, shards, fulls, slicers):
    t_len = x.shape[0]
    tm = min(TMI, t_len)
    ni = t_len // tm
    na = len(shards)
    cw = 7 * D // NDEV
    rc = 32

    def body(ord_ref, x_ref, mod_ref, g_ref, wsh_ref, *rest):
        ins, (proj_ref, h_ref, wfull_ref), outs = rest[:na], rest[na:na + 3], rest[na + 3:2 * na + 3]
        h_scr, w_scr, wsend, wrecv, wlocal, wout = rest[2 * na + 3:2 * na + 9]
        start, forward, finish = _ag_phases(ins, outs, slicers, *rest[2 * na + 9:])
        p, i = pl.program_id(0), pl.program_id(1)
        x_, y_, c = _my_pos()
        me, sibling = (x_, y_, c), (x_, y_, 1 - c)
        chip_at = [None, (x_, 1 - y_), (1 - x_, y_), (1 - x_, 1 - y_)]

        def cols(px, py, pc):
            return w_scr.at[:, pl.ds(pl.multiple_of((4 * px + 2 * py + pc) * cw, 128), cw)]

        def wcopy(k, block, to, from_shard=False):
            dst = cols(*block)
            return pltpu.make_async_remote_copy(src_ref=wsh_ref if from_shard else dst, dst_ref=dst,
                                                send_sem=wsend.at[k], recv_sem=wrecv.at[k], device_id=to,
                                                device_id_type=MESH)

        own_local = pltpu.make_async_copy(wsh_ref, cols(*me), wlocal)
        to_hbm = pltpu.make_async_copy(w_scr, wfull_ref, wout)

        @pl.when((p == 0) & (i == 0))
        def _():
            own_local.start()
            wcopy(0, me, sibling, True).start()
            for q in (1, 2):
                wcopy(q, me, (*chip_at[q], c), True).start()
            own_local.wait()
            wcopy(0, sibling, me).wait_recv()

        @pl.when((p == 0) & (i == ni // 2))
        def _():
            wcopy(3, me, (*chip_at[3], c), True).start()

        for q in (1, 2, 3):
            @pl.when((p == q - 1) & (i == ni - 1))
            def _():
                wcopy(q, (*chip_at[q], c), me).wait_recv()
                wcopy(3 + q, (*chip_at[q], c), sibling).start()

            @pl.when((p == q) & (i == 0))
            def _():
                wcopy(3 + q, (*chip_at[q], 1 - c), me).wait_recv()

        @pl.when((p == 1) & (i == 0))
        def _():
            start()

        @pl.when((p == NCHIP - 1) & (i == ni // 2))
        def _():
            forward()

        @pl.when((p == NCHIP - 1) & (i == 0))
        def _():
            to_hbm.start()

        gs = g_ref[...] * (1.0 + mod_ref[1:2, :])
        sh = mod_ref[0:1, :]

        wcols = pl.ds(pl.multiple_of(ord_ref[p] * (2 * cw), 128), 2 * cw)
        for sub in _sub_blocks(tm):
            for r0 in range(sub.start, sub.stop, rc):
                xv = x_ref[r0:r0 + rc, :]
                r = lax.rsqrt(jnp.mean(xv * xv, axis=-1, keepdims=True) + EPS)
                h_scr[r0:r0 + rc, :] = (xv * r * gs + sh).astype(BF16)
            proj_ref[sub, :] = _dot(h_scr[sub, :], w_scr[:, wcols]).astype(BF16)

        @pl.when(p == 0)
        def _():
            h_ref[...] = h_scr[...]

        @pl.when((p == NCHIP - 1) & (i == ni - 1))
        def _():
            wcopy(0, me, sibling, True).wait_send()
            for q in (1, 2, 3):
                wcopy(q, me, (*chip_at[q], c), True).wait_send()
                wcopy(3 + q, (*chip_at[q], c), sibling).wait_send()
            finish()
            to_hbm.wait()

    res = pl.pallas_call(
        body, name="in_proj",
        grid_spec=pltpu.PrefetchScalarGridSpec(
            num_scalar_prefetch=1, grid=(NCHIP, ni),
            in_specs=[pl.BlockSpec((tm, D), lambda p, i, o: (i, 0)),
                      pl.BlockSpec((8, D), lambda p, i, o: (0, 0)),
                      pl.BlockSpec((1, D), lambda p, i, o: (0, 0))] + [_ANY] * (1 + na),
            out_specs=[pl.BlockSpec((tm, 2 * cw), lambda p, i, o: (i, o[p])),
                       pl.BlockSpec((tm, D), lambda p, i, o: (jnp.where(p == 0, i, ni - 1), 0))]
            + [_ANY] * (1 + na),
            scratch_shapes=[pltpu.VMEM((tm, D), BF16), pltpu.VMEM((D, 7 * D), BF16),
                            pltpu.SemaphoreType.DMA((7,)), pltpu.SemaphoreType.DMA((7,)),
                            pltpu.SemaphoreType.DMA, pltpu.SemaphoreType.DMA] + _ag_sems(na)),
        out_shape=[jax.ShapeDtypeStruct((t_len, 7 * D), BF16), jax.ShapeDtypeStruct((t_len, D), BF16),
                   jax.ShapeDtypeStruct((D, 7 * D), BF16)]
        + [jax.ShapeDtypeStruct(f, sh.dtype) for f, sh in zip(fulls, shards)],
        compiler_params=_cp("arbitrary", "arbitrary"),
    )(order, x, mod, g_mix, w_shard, *shards)
    return res[0], res[1], res[2], res[3:]


P_WA, P_WB, P_CBIAS, P_BA, P_BX, P_LAM = 0, 3, 7, 8, 9, 10
SV_PLANES = SV_U, SV_YA, SV_R, SV_I, SV_A, SV_MULT = range(6)


def _lru_gates(rp, ip, ls, first_row):
    r = _sig(rp)
    ig = _sig(ip)
    la = LRU_C * r * ls
    a = jnp.exp(la)
    m2 = _neg_expm1(2.0 * la)
    mult = jnp.where(first_row, 1.0, jnp.sqrt(jnp.maximum(m2, 0.0)))
    return r, ig, la, a, m2, mult


def _shift_down(cur, prev, s, row):
    return jnp.where(row >= s, pltpu.roll(cur, s, 0), pltpu.roll(prev, s, 0))


def _shift_up(cur, nxt, s, row):
    return jnp.where(row < 8 - s, pltpu.roll(cur, 8 - s, 0), pltpu.roll(nxt, 8 - s, 0))


def _conv_fwd_rows(tt, proj_ref, prm_ref, xe, ve, u_s, ub_s, ya_s):
    row = lax.broadcasted_iota(jnp.int32, (8, CG), 0)
    w_b = [prm_ref[P_WB + k:P_WB + k + 1, :] for k in range(4)]
    w_a = [prm_ref[P_WA + k:P_WA + k + 1, :] for k in range(3)]
    bias = prm_ref[P_CBIAS:P_CBIAS + 1, :]

    def blk(ib, carry):
        r0 = pl.multiple_of(ib * 16, 16)
        rows = pl.ds(r0, 16)
        for g in range(D // CG):
            cs = slice(g * CG, (g + 1) * CG)
            x16 = _pj(proj_ref, 3, rows, cs).astype(F32)
            v16 = _pj(proj_ref, 1, rows, cs).astype(F32) * _pj(proj_ref, 2, rows, cs).astype(F32)
            xp = xe[pl.ds(r0, 8), cs]
            vp = ve[pl.ds(r0, 8), cs]
            xe[pl.ds(r0 + 8, 16), cs] = x16
            ve[pl.ds(r0 + 8, 16), cs] = v16
            us, yas = [], []
            for sb in range(2):
                xc, vc = x16[8 * sb:8 * sb + 8], v16[8 * sb:8 * sb + 8]
                u8 = bias[:, cs] + w_b[3][:, cs] * xc
                for s in (1, 2, 3):
                    u8 = u8 + w_b[3 - s][:, cs] * _shift_down(xc, xp, s, row)
                y8 = w_a[2][:, cs] * vc
                for s in (1, 2):
                    y8 = y8 + w_a[2 - s][:, cs] * _shift_down(vc, vp, s, row)
                us.append(u8)
                yas.append(y8)
                xp, vp = xc, vc
            u16 = jnp.concatenate(us, axis=0)
            u_s[rows, cs] = u16
            ub_s[rows, cs] = u16.astype(BF16)
            ya_s[rows, cs] = jnp.concatenate(yas, axis=0)
        return carry

    lax.fori_loop(0, tt // 16, blk, 0)


def _mixer_fwd(proj, prm, wa, wx, shards, fulls, slicers):
    t_len = proj.shape[0]
    tt = min(TT, t_len)
    nt = t_len // tt
    na = len(shards)

    def body(proj_hbm, prm_ref, wa_ref, wx_ref, *rest):
        ins, (mg_ref, hl_ref, sv_ref), outs = rest[:na], rest[na:na + 3], rest[na + 3:2 * na + 3]
        xe, ve, hc, rp_s, ip_s, ub_s, ring, ring_sem = rest[2 * na + 3:2 * na + 11]
        start, forward, finish = _ag_phases(ins, outs, slicers, *rest[2 * na + 11:])
        t = pl.program_id(0)

        def fetch(s):
            row0 = s * tt if isinstance(s, int) else pl.multiple_of(s * tt, tt)
            return pltpu.make_async_copy(proj_hbm.at[pl.ds(row0, tt), :], ring.at[s % 3], ring_sem.at[s % 3])

        @pl.when(t == 0)
        def _():
            for s in range(min(2, nt)):
                fetch(s).start()

        @pl.when(t + 2 < nt)
        def _():
            fetch(t + 2).start()

        fetch(t).wait()
        proj_ref = ring.at[t % 3]

        @pl.when(t == 0)
        def _():
            start()
            xe[0:8, :] = jnp.zeros((8, D), F32)
            ve[0:8, :] = jnp.zeros((8, D), F32)
            hc[...] = jnp.zeros((8, D), F32)

        @pl.when(t == (3 * nt) // 4)
        def _():
            forward()

        _conv_fwd_rows(tt, proj_ref, prm_ref, xe, ve, sv_ref.at[SV_U], ub_s, sv_ref.at[SV_YA])
        xe[0:8, :] = xe[tt:tt + 8, :]
        ve[0:8, :] = ve[tt:tt + 8, :]

        ub = ub_s[...]
        for h in range(HEADS):
            cs = slice(h * HB, (h + 1) * HB)
            rp_s[:, cs] = _dot(ub[:, cs], wa_ref[h]) + prm_ref[P_BA:P_BA + 1, cs]
            ip_s[:, cs] = _dot(ub[:, cs], wx_ref[h]) + prm_ref[P_BX:P_BX + 1, cs]

        ls_all = _log_sigmoid(prm_ref[P_LAM:P_LAM + 1, :])
        row = lax.broadcasted_iota(jnp.int32, (8, CG), 0)

        def blk(i, carry):
            r0 = pl.multiple_of(i * 16, 16)
            for g in range(D // CG):
                cs = slice(g * CG, (g + 1) * CG)
                ls = ls_all[:, cs]
                hprev = hc[:, cs]
                hs = []
                for sb in range(2):
                    rr = r0 + 8 * sb
                    first = (row + (t * tt + rr)) == 0
                    r8 = pl.ds(rr, 8)
                    r, ig, _, a, _, mult = _lru_gates(rp_s[r8, cs], ip_s[r8, cs], ls, first)
                    for plane, val in ((SV_R, r), (SV_I, ig), (SV_A, a), (SV_MULT, mult)):
                        sv_ref[plane, r8, cs] = val
                    b = mult * (ig * sv_ref[SV_U, r8, cs])
                    for s in (1, 2, 4):
                        a_sh = jnp.where(row >= s, pltpu.roll(a, s, 0), 1.0)
                        b_sh = jnp.where(row >= s, pltpu.roll(b, s, 0), 0.0)
                        b = a * b_sh + b
                        a = a * a_sh
                    hv = a * hprev + b
                    hprev = jnp.broadcast_to(hv[7:8, :], hv.shape)
                    hs.append(hv)
                hc[:, cs] = hprev
                h16 = jnp.concatenate(hs, axis=0)
                rows = pl.ds(r0, 16)
                gl, _ = _gelu(_pj(proj_ref, 4, rows, cs).astype(F32))
                y_b = h16 * gl
                y_a = _pj(proj_ref, 0, rows, cs).astype(F32) * sv_ref[SV_YA, rows, cs]
                mg = (_sig(_pj(proj_ref, 5, rows, cs).astype(F32)) * y_a
                      + _sig(_pj(proj_ref, 6, rows, cs).astype(F32)) * y_b)
                mg_ref[rows, cs] = mg.astype(BF16)
                hl_ref[rows, cs] = h16.astype(BF16)
            return carry

        lax.fori_loop(0, tt // 16, blk, 0)

        @pl.when(t == nt - 1)
        def _():
            finish()

    res = pl.pallas_call(
        body, name="mixer_fwd", grid=(nt,),
        in_specs=[_ANY,
                  pl.BlockSpec((16, D), lambda t: (0, 0)),
                  pl.BlockSpec((HEADS, HB, HB), lambda t: (0, 0, 0)),
                  pl.BlockSpec((HEADS, HB, HB), lambda t: (0, 0, 0))] + [_ANY] * na,
        out_specs=[pl.BlockSpec((tt, D), lambda t: (t, 0)), pl.BlockSpec((tt, D), lambda t: (t, 0)),
                   pl.BlockSpec((len(SV_PLANES), tt, D), lambda t: (0, t, 0))] + [_ANY] * na,
        out_shape=[jax.ShapeDtypeStruct((t_len, D), BF16), jax.ShapeDtypeStruct((t_len, D), BF16),
                   jax.ShapeDtypeStruct((len(SV_PLANES), t_len, D), F32)]
        + [jax.ShapeDtypeStruct(f, sh.dtype) for f, sh in zip(fulls, shards)],
        scratch_shapes=[pltpu.VMEM((tt + 8, D), F32), pltpu.VMEM((tt + 8, D), F32), pltpu.VMEM((8, D), F32),
                        pltpu.VMEM((tt, D), F32), pltpu.VMEM((tt, D), F32), pltpu.VMEM((tt, D), BF16),
                        pltpu.VMEM((3, tt, 7 * D), BF16), pltpu.SemaphoreType.DMA((3,))]
        + _ag_sems(na),
        compiler_params=_cp("arbitrary"),
    )(proj, prm, wa, wx, *shards)
    return res[0], res[1], res[2], res[3:]


def _out_proj(merged, x, mod, g_ffn, w_out):
    t_len = x.shape[0]
    tm = min(TMI, t_len)

    def body(mg_ref, x_ref, mod_ref, g_ref, w_ref, x1_ref, h2_ref):
        gt1 = mod_ref[2:3, :]
        gs = g_ref[...] * (1.0 + mod_ref[4:5, :])
        sh = mod_ref[3:4, :]
        for sub in _sub_blocks(tm):
            x1_ref[sub, :] = x_ref[sub, :] + gt1 * _dot(mg_ref[sub, :], w_ref[...])
            for r0 in range(sub.start, sub.stop, 16):
                x1 = x1_ref[r0:r0 + 16, :]
                r = lax.rsqrt(jnp.mean(x1 * x1, axis=-1, keepdims=True) + EPS)
                h2_ref[r0:r0 + 16, :] = (x1 * r * gs + sh).astype(BF16)

    return pl.pallas_call(
        body, name="out_proj", grid=(t_len // tm,),
        in_specs=[pl.BlockSpec((tm, D), lambda i: (i, 0)), pl.BlockSpec((tm, D), lambda i: (i, 0)),
                  pl.BlockSpec((8, D), lambda i: (0, 0)), pl.BlockSpec((1, D), lambda i: (0, 0)),
                  pl.BlockSpec((D, D), lambda i: (0, 0))],
        out_specs=[pl.BlockSpec((tm, D), lambda i: (i, 0)), pl.BlockSpec((tm, D), lambda i: (i, 0))],
        out_shape=[jax.ShapeDtypeStruct((t_len, D), F32), jax.ShapeDtypeStruct((t_len, D), BF16)],
        compiler_params=_cp("parallel"),
    )(merged, x, mod, g_ffn, w_out)


def _ffn_fwd(h2, x1, target, mod, g_fin, w_gu, w_down):
    t_len = x1.shape[0]
    tm = min(TMF, t_len)

    def body(h2_ref, x1_ref, tg_ref, mod_ref, g_ref, wgu_ref, wd_ref, gu_ref, dx2_ref, dx2b_ref, loss_ref, dg_ref, acc):
        @pl.when(pl.program_id(0) == 0)
        def _():
            loss_ref[...] = jnp.zeros_like(loss_ref)
            dg_ref[...] = jnp.zeros_like(dg_ref)

        hb = h2_ref[...]
        ffn = None
        nxt = (_dot_nt(hb, wgu_ref[0, 0]), _dot_nt(hb, wgu_ref[1, 0]))
        for j in range(4):
            gate, up = nxt
            if j < 3:
                nxt = (_dot_nt(hb, wgu_ref[0, j + 1]), _dot_nt(hb, wgu_ref[1, j + 1]))
            gu_ref[0, j] = gate.astype(BF16)
            gu_ref[1, j] = up.astype(BF16)
            act = (gate * _sig(gate) * up).astype(BF16)
            part = _dot(act, wd_ref[j * FB:(j + 1) * FB, :])
            ffn = part if ffn is None else ffn + part
        acc[...] = ffn

        gt2 = mod_ref[5:6, :]
        gf = g_ref[...]

        s_loss = s_dg = jnp.zeros((8, D), F32)
        for r0 in range(0, tm, 16):
            rows = slice(r0, r0 + 16)
            x2 = x1_ref[rows, :] + gt2 * acc[rows, :]
            r = lax.rsqrt(jnp.mean(x2 * x2, axis=-1, keepdims=True) + EPS)
            xn = x2 * r
            diff = xn * gf - tg_ref[rows, :]
            dy = diff * (1.0 / D)
            dxn = dy * gf
            dx2 = r * (dxn - xn * jnp.mean(dxn * xn, axis=-1, keepdims=True))
            dx2_ref[rows, :] = dx2
            dx2b_ref[rows, :] = dx2.astype(BF16)
            s_loss, s_dg = s_loss + _fold8(diff * diff), s_dg + _fold8(dy * xn)
        loss_ref[...] += jnp.sum(s_loss) * (0.5 / D)
        dg_ref[...] += jnp.sum(s_dg, axis=0, keepdims=True)

    row = pl.BlockSpec((tm, D), lambda i: (i, 0))
    return pl.pallas_call(
        body, name="ffn_fwd", grid=(t_len // tm,),
        in_specs=[row, row, row, pl.BlockSpec((8, D), lambda i: (0, 0)), pl.BlockSpec((1, D), lambda i: (0, 0)),
                  _resident((2, 4, FB, D)), _resident((DFF, D))],
        out_specs=[pl.BlockSpec((2, 4, tm, FB), lambda i: (0, 0, i, 0)), row, row,
                   pl.BlockSpec((1, 128), lambda i: (0, 0)), pl.BlockSpec((1, D), lambda i: (0, 0))],
        out_shape=[jax.ShapeDtypeStruct((2, 4, t_len, FB), BF16), jax.ShapeDtypeStruct((t_len, D), F32),
                   jax.ShapeDtypeStruct((t_len, D), BF16),
                   jax.ShapeDtypeStruct((1, 128), F32), jax.ShapeDtypeStruct((1, D), F32)],
        scratch_shapes=[pltpu.VMEM((tm, D), F32)],
        compiler_params=_cp("arbitrary"),
    )(h2, x1, target, mod, g_fin, w_gu, w_down)


S_SH, S_SC, S_G = 0, 1, 2


def _norm_bwd_rows(span, sums, dh_ref, x_ref, dres_ref, scale, gain, write):
    gs = 1.0 + scale
    s_sh, s_sc, s_g = sums
    for r0 in range(span.start, span.stop, 16):
        rows = slice(r0, r0 + 16)
        dh = dh_ref[rows, :]
        xv = x_ref[rows, :]
        r = lax.rsqrt(jnp.mean(xv * xv, axis=-1, keepdims=True) + EPS)
        xn = xv * r
        dhn = dh * gs
        dxn = dhn * gain
        write(rows, dres_ref[rows, :] + r * (dxn - xn * jnp.mean(dxn * xn, axis=-1, keepdims=True)))
        s_sh, s_sc, s_g = s_sh + _fold8(dh), s_sc + _fold8(dh * (xn * gain)), s_g + _fold8(dhn * xn)
    return s_sh, s_sc, s_g


def _add_norm_sums(sums_ref, sums):
    for dst, s in zip((S_SH, S_SC, S_G), sums):
        sums_ref[dst:dst + 1, :] += jnp.sum(s, axis=0, keepdims=True)


def _ffn_bwd(dx2, gu, x1, mod, g_ffn, w_gu, w_down, w_out):
    t_len = x1.shape[0]
    tm = min(TMF, t_len)

    def body(dx2_ref, gu_ref, x1_ref, mod_ref, g_ref, wgu_ref, wd_ref, wo_ref,
             dgu_ref, act_ref, dx1_ref, dx1b_ref, dmg_ref, sums_ref, acc, dmo, dact_s):
        @pl.when(pl.program_id(0) == 0)
        def _():
            sums_ref[...] = jnp.zeros_like(sums_ref)

        dffn = (dx2_ref[...] * mod_ref[5:6, :]).astype(BF16)
        dact_s[0] = _dot_nt(dffn, wd_ref[0:FB, :])
        for j in range(4):
            if j < 3:
                dact_s[(j + 1) % 2] = _dot_nt(dffn, wd_ref[(j + 1) * FB:(j + 2) * FB, :])
            for r0 in range(0, tm, 16):
                rows = slice(r0, r0 + 16)
                dact = dact_s[j % 2, rows, :]
                gate = gu_ref[0, j, rows, :].astype(F32)
                up = gu_ref[1, j, rows, :].astype(F32)
                sg = _sig(gate)
                silu = gate * sg
                act_ref[j, rows, :] = (silu * up).astype(BF16)
                dgu_ref[0, j, rows, :] = (dact * up * (sg * (1.0 + gate * (1.0 - sg)))).astype(BF16)
                dgu_ref[1, j, rows, :] = (dact * silu).astype(BF16)
            part = _dot(dgu_ref[0, j], wgu_ref[0, j]) + _dot(dgu_ref[1, j], wgu_ref[1, j])
            if j == 0:
                acc[...] = part
            else:
                acc[...] += part

        gt1 = mod_ref[2:3, :]

        def write(rows, dx1):
            dx1_ref[rows, :] = dx1
            dx1b_ref[rows, :] = dx1.astype(BF16)
            dmo[rows, :] = (dx1 * gt1).astype(BF16)

        zero = jnp.zeros((8, D), F32)
        sums = (zero, zero, zero)
        for sub in (slice(0, tm // 2), slice(tm // 2, tm)):
            sums = _norm_bwd_rows(sub, sums, acc, x1_ref, dx2_ref, mod_ref[4:5, :], g_ref[...], write)
            dmg_ref[sub, :] = _dot_nt(dmo[sub, :], wo_ref[...]).astype(BF16)
        _add_norm_sums(sums_ref, sums)

    row = pl.BlockSpec((tm, D), lambda i: (i, 0))
    return pl.pallas_call(
        body, name="ffn_bwd", grid=(t_len // tm,),
        in_specs=[row, pl.BlockSpec((2, 4, tm, FB), lambda i: (0, 0, i, 0)), row,
                  pl.BlockSpec((8, D), lambda i: (0, 0)), pl.BlockSpec((1, D), lambda i: (0, 0)),
                  _resident((2, 4, FB, D)), _resident((DFF, D)), _resident((D, D))],
        out_specs=[pl.BlockSpec((2, 4, tm, FB), lambda i: (0, 0, i, 0)),
                   pl.BlockSpec((4, tm, FB), lambda i: (0, i, 0)), row, row, row,
                   pl.BlockSpec((8, D), lambda i: (0, 0))],
        out_shape=[jax.ShapeDtypeStruct((2, 4, t_len, FB), BF16), jax.ShapeDtypeStruct((4, t_len, FB), BF16),
                   jax.ShapeDtypeStruct((t_len, D), F32), jax.ShapeDtypeStruct((t_len, D), BF16),
                   jax.ShapeDtypeStruct((t_len, D), BF16), jax.ShapeDtypeStruct((8, D), F32)],
        scratch_shapes=[pltpu.VMEM((tm, D), F32), pltpu.VMEM((tm, D), BF16), pltpu.VMEM((2, tm, FB), F32)],
        compiler_params=_cp("arbitrary"),
    )(dx2, gu, x1, mod, g_ffn, w_gu, w_down, w_out)


def _my_pos():
    return lax.axis_index("x"), lax.axis_index("y"), lax.axis_index("c")


def _my_index():
    x, y, c = _my_pos()
    return 4 * x + 2 * y + c


def _device_of(b):
    return (b >> 2) & 1, (b >> 1) & 1, b & 1


def _rs_send(src, parts_ref, b, send_sems, recv_sems, local_sem):
    me = _my_index()
    dst = parts_ref.at[me]

    @pl.when(b == me)
    def _():
        pltpu.make_async_copy(src, dst, local_sem).start()

    @pl.when(b != me)
    def _():
        pltpu.make_async_remote_copy(src_ref=src, dst_ref=dst, send_sem=send_sems.at[b], recv_sem=recv_sems.at[me],
                                     device_id=_device_of(b), device_id_type=MESH).start()


def _rs_finish(src_of, parts_ref, send_sems, recv_sems, local_sem):
    me = _my_index()
    for s in range(NDEV):
        @pl.when(s != me)
        def _():
            cp = pltpu.make_async_remote_copy(src_ref=src_of(s), dst_ref=parts_ref.at[s], send_sem=send_sems.at[s],
                                              recv_sem=recv_sems.at[s], device_id=_device_of(s), device_id_type=MESH)
            cp.wait_send()
            cp.wait_recv()

        @pl.when(s == me)
        def _():
            pltpu.make_async_copy(src_of(s), parts_ref.at[s], local_sem).wait()


_RS_SEMS = [pltpu.SemaphoreType.DMA((NDEV,)), pltpu.SemaphoreType.DMA((NDEV,)), pltpu.SemaphoreType.DMA]
_ANY = pl.BlockSpec(memory_space=pl.ANY)


def _xor_order(me, n):
    return (me ^ (n - 1 - jnp.arange(n, dtype=jnp.int32))).astype(jnp.int32)


NCHIP = NDEV // 2


def _rs2_scratch(half_shape):
    blocks = lambda *lead: pltpu.VMEM(lead + tuple(half_shape), BF16)
    return [blocks(NCHIP, 2), blocks(NCHIP)] + [pltpu.SemaphoreType.DMA((NCHIP,))] * 4 + [pltpu.SemaphoreType.DMA]


def _rs2_to_sibling(q, rs):
    stage, from_sib, d_send, d_recv = rs[:4]
    x, y, c = _my_pos()
    pltpu.make_async_remote_copy(src_ref=stage.at[q, 1 - c], dst_ref=from_sib.at[q], send_sem=d_send.at[q],
                                 recv_sem=d_recv.at[q], device_id=(x, y, 1 - c), device_id_type=MESH).start()


def _rs2_forward(q, parts_ref, rs):
    stage, chip_sum, d_send, d_recv, i_send, i_recv, local_sem = rs
    x, y, c = _my_pos()
    my_chip = 2 * x + y
    pltpu.make_async_remote_copy(src_ref=stage.at[q, c], dst_ref=chip_sum.at[q], send_sem=d_send.at[q],
                                 recv_sem=d_recv.at[q], device_id=(x, y, 1 - c), device_id_type=MESH).wait_recv()
    chip_sum[q] = (stage[q, c].astype(F32) + chip_sum[q].astype(F32)).astype(BF16)

    @pl.when(q == my_chip)
    def _():
        pltpu.make_async_copy(chip_sum.at[q], parts_ref.at[my_chip], local_sem).start()

    @pl.when(q != my_chip)
    def _():
        pltpu.make_async_remote_copy(src_ref=chip_sum.at[q], dst_ref=parts_ref.at[my_chip], send_sem=i_send.at[q],
                                     recv_sem=i_recv.at[my_chip], device_id=((q >> 1) & 1, q & 1, c),
                                     device_id_type=MESH).start()


def _rs2_finish(parts_ref, rs):
    stage, chip_sum, d_send, d_recv, i_send, i_recv, local_sem = rs
    x, y, c = _my_pos()
    my_chip = 2 * x + y
    for q in range(NCHIP):
        pltpu.make_async_remote_copy(src_ref=stage.at[q, 1 - c], dst_ref=chip_sum.at[q], send_sem=d_send.at[q],
                                     recv_sem=d_recv.at[q], device_id=(x, y, 1 - c), device_id_type=MESH).wait_send()

        @pl.when(q != my_chip)
        def _():
            cp = pltpu.make_async_remote_copy(src_ref=chip_sum.at[q], dst_ref=parts_ref.at[q], send_sem=i_send.at[q],
                                              recv_sem=i_recv.at[q], device_id=((q >> 1) & 1, q & 1, c),
                                              device_id_type=MESH)
            cp.wait_send()
            cp.wait_recv()

        @pl.when(q == my_chip)
        def _():
            pltpu.make_async_copy(chip_sum.at[q], parts_ref.at[q], local_sem).wait()


def _gu_wgrad(h2, dgu, order):
    t_len = h2.shape[0]
    tk = min(TK, t_len)
    nk = t_len // tk

    def body(ord_ref, h_ref, d_ref, parts_ref, acc, *rs):
        p, k = pl.program_id(0), pl.program_id(1)

        @pl.when(k == 0)
        def _():
            acc[...] = jnp.zeros_like(acc)

        hb = h_ref[...]
        for half in range(2):
            acc[half] += _dot_tn(d_ref[0, half], hb)

        @pl.when(k == nk - 1)
        def _():
            q = ord_ref[p]
            rs[0][q] = acc[...].astype(BF16)
            _rs2_to_sibling(q, rs)

        @pl.when((k == min(1, nk - 1)) & (p > 0))
        def _():
            _rs2_forward(ord_ref[p - 1], parts_ref, rs)

        @pl.when((p == NCHIP - 1) & (k == nk - 1))
        def _():
            _rs2_forward(ord_ref[p], parts_ref, rs)
            _rs2_finish(parts_ref, rs)

    return pl.pallas_call(
        body, name="gu_wgrad",
        grid_spec=pltpu.PrefetchScalarGridSpec(
            num_scalar_prefetch=1, grid=(NCHIP, nk),
            in_specs=[pl.BlockSpec((tk, D), lambda p, k, o: (k, 0)),
                      pl.BlockSpec((1, 2, tk, FB), lambda p, k, o: (o[p], 0, k, 0))],
            out_specs=_ANY,
            scratch_shapes=[pltpu.VMEM((2, FB, D), F32)] + _rs2_scratch((FB, D))),
        out_shape=jax.ShapeDtypeStruct((NCHIP, FB, D), BF16),
        compiler_params=_cp("arbitrary", "arbitrary"),
    )(order, h2, dgu.reshape(NCHIP, 2, t_len, FB))


def _scaled_wgrad(name, a, dx, w, gate_row, mod, order):
    t_len = dx.shape[0]
    kb = w.shape[0] // NCHIP
    tk = min(TK, t_len)
    nk = t_len // tk
    rows = kb // 2
    if a.ndim == 3:
        a_spec = pl.BlockSpec((None, tk, kb), lambda p, k, o: (o[p], k, 0))
    else:
        a_spec = pl.BlockSpec((tk, kb), lambda p, k, o: (k, o[p]))

    def body(ord_ref, a_ref, dx_ref, w_ref, mod_ref, parts_ref, dg_ref, acc, *rs):
        p, k = pl.program_id(0), pl.program_id(1)

        @pl.when((p == 0) & (k == 0))
        def _():
            dg_ref[...] = jnp.zeros_like(dg_ref)

        @pl.when(k == 0)
        def _():
            acc[...] = jnp.zeros_like(acc)

        acc[...] += _dot_tn(a_ref[...], dx_ref[...])

        @pl.when(k == nk - 1)
        def _():
            q = ord_ref[p]
            z = acc[...]
            zg = (z * mod_ref[gate_row:gate_row + 1, :]).astype(BF16)
            dg_ref[0:1, :] += jnp.sum(z * w_ref[...].astype(F32), axis=0, keepdims=True)
            for half in range(2):
                rs[0][q, half] = zg[half * rows:(half + 1) * rows]
            _rs2_to_sibling(q, rs)

        @pl.when((k == min(1, nk - 1)) & (p > 0))
        def _():
            _rs2_forward(ord_ref[p - 1], parts_ref, rs)

        @pl.when((p == NCHIP - 1) & (k == nk - 1))
        def _():
            _rs2_forward(ord_ref[p], parts_ref, rs)
            _rs2_finish(parts_ref, rs)

    return pl.pallas_call(
        body, name=name,
        grid_spec=pltpu.PrefetchScalarGridSpec(
            num_scalar_prefetch=1, grid=(NCHIP, nk),
            in_specs=[a_spec,
                      pl.BlockSpec((tk, D), lambda p, k, o: (k, 0)),
                      pl.BlockSpec((kb, D), lambda p, k, o: (o[p], 0)),
                      pl.BlockSpec((8, D), lambda p, k, o: (0, 0))],
            out_specs=[_ANY, pl.BlockSpec((8, D), lambda p, k, o: (0, 0))],
            scratch_shapes=[pltpu.VMEM((kb, D), F32)] + _rs2_scratch((rows, D))),
        out_shape=[jax.ShapeDtypeStruct((NCHIP, rows, D), BF16), jax.ShapeDtypeStruct((8, D), F32)],
        compiler_params=_cp("arbitrary", "arbitrary"),
    )(order, a, dx, w, mod)


M_WA, M_WB, M_CBIAS, M_BA, M_BX, M_LS = 0, 3, 7, 8, 9, 10


def _conv_bwd_rows(tt, proj_ref, prm_ref, due, dye, dp_ref, acc8):
    row = lax.broadcasted_iota(jnp.int32, (8, CG), 0)
    w_b = [prm_ref[P_WB + k:P_WB + k + 1, :] for k in range(4)]
    w_a = [prm_ref[P_WA + k:P_WA + k + 1, :] for k in range(3)]

    def blk(ib, carry):
        r0 = pl.multiple_of(ib * 16, 16)
        rows = pl.ds(r0, 16)
        for g in range(D // CG):
            cs = slice(g * CG, (g + 1) * CG)
            du16, du_after = due[rows, cs], due[pl.ds(r0 + 16, 8), cs]
            dy16, dy_after = dye[rows, cs], dye[pl.ds(r0 + 16, 8), cs]
            cc16 = _pj(proj_ref, 1, rows, cs).astype(F32)
            cx16 = _pj(proj_ref, 2, rows, cs).astype(F32)
            x16 = _pj(proj_ref, 3, rows, cs).astype(F32)
            v16 = cc16 * cx16
            acc = [acc8[8 * k:8 * k + 8, cs] for k in range(8)]
            drx, dv = [], []
            for sb in range(2):
                lo = slice(8 * sb, 8 * sb + 8)
                duc, dyc, xc, vc = du16[lo], dy16[lo], x16[lo], v16[lo]
                du_n = du16[8:16] if sb == 0 else du_after
                dy_n = dy16[8:16] if sb == 0 else dy_after
                acc[0] = acc[0] + duc
                acc[4] = acc[4] + duc * xc
                d8 = w_b[3][:, cs] * duc
                for s in (1, 2, 3):
                    du_s = _shift_up(duc, du_n, s, row)
                    acc[4 - s] = acc[4 - s] + du_s * xc
                    d8 = d8 + w_b[3 - s][:, cs] * du_s
                acc[7] = acc[7] + dyc * vc
                e8 = w_a[2][:, cs] * dyc
                for s in (1, 2):
                    dy_s = _shift_up(dyc, dy_n, s, row)
                    acc[7 - s] = acc[7 - s] + dy_s * vc
                    e8 = e8 + w_a[2 - s][:, cs] * dy_s
                drx.append(d8)
                dv.append(e8)
            for k in range(8):
                acc8[8 * k:8 * k + 8, cs] = acc[k]
            dv16 = jnp.concatenate(dv, axis=0)
            col = lambda s: slice(s * D + g * CG, s * D + (g + 1) * CG)
            dp_ref[rows, col(3)] = jnp.concatenate(drx, axis=0).astype(BF16)
            dp_ref[rows, col(1)] = (dv16 * cx16).astype(BF16)
            dp_ref[rows, col(2)] = (dv16 * cc16).astype(BF16)
        return carry

    lax.fori_loop(0, tt // 16, blk, 0)


def _mixer_bwd(proj, hl, sv, dmg, prm, wa, wx):
    t_len = proj.shape[0]
    tt = min(TT, t_len)
    nt = t_len // tt
    hb8 = tt // 8

    def rev(i):
        return nt - 1 - i

    def halo(i):
        return jnp.maximum(rev(i) * hb8 - 1, 0)

    def body(proj_ref, hl_ref, hh_ref, sv_hbm, dmg_ref, prm_ref, wa_ref, wx_ref,
             dp_ref, sums_ref, gwa_ref, gwx_ref,
             he, due, dye, drp_s, dip_s, an, gn, acc8, ring, ring_sem):
        i = pl.program_id(0)
        t = rev(i)

        def fetch(j):
            row0 = rev(j) * tt if isinstance(j, int) else pl.multiple_of(rev(j) * tt, tt)
            return pltpu.make_async_copy(sv_hbm.at[:, pl.ds(row0, tt), :], ring.at[j % 3], ring_sem.at[j % 3])

        @pl.when(i == 0)
        def _():
            for j in range(min(2, nt)):
                fetch(j).start()

        @pl.when(i + 2 < nt)
        def _():
            fetch(i + 2).start()

        fetch(i).wait()
        sv_ref = ring.at[i % 3]

        @pl.when(i == 0)
        def _():
            sums_ref[...] = jnp.zeros_like(sums_ref)
            gwa_ref[...] = jnp.zeros_like(gwa_ref)
            gwx_ref[...] = jnp.zeros_like(gwx_ref)
            due[tt:tt + 8, :] = jnp.zeros((8, D), F32)
            dye[tt:tt + 8, :] = jnp.zeros((8, D), F32)
            an[...] = jnp.zeros((8, D), F32)
            gn[...] = jnp.zeros((8, D), F32)

        live = (t > 0).astype(F32)
        he[0:8, :] = hh_ref[...].astype(F32) * live
        he[8:8 + tt, :] = hl_ref[...].astype(F32)

        ls_all = _log_sigmoid(prm_ref[P_LAM:P_LAM + 1, :])
        row = lax.broadcasted_iota(jnp.int32, (8, CG), 0)
        nblk = tt // 16

        def blk(ib, carry):
            r0 = pl.multiple_of((nblk - 1 - ib) * 16, 16)
            rows = pl.ds(r0, 16)
            for g in range(D // CG):
                cs = slice(g * CG, (g + 1) * CG)
                ls = ls_all[:, cs]
                dm = dmg_ref[rows, cs].astype(F32)
                cb = _pj(proj_ref, 0, rows, cs).astype(F32)
                rg = _pj(proj_ref, 4, rows, cs).astype(F32)
                sga = _sig(_pj(proj_ref, 5, rows, cs).astype(F32))
                sgb = _sig(_pj(proj_ref, 6, rows, cs).astype(F32))
                ya0 = sv_ref[SV_YA, rows, cs]
                h16 = he[pl.ds(r0 + 8, 16), cs]
                gl, th = _gelu(rg)
                dgl = 0.5 * (1.0 + th) + 0.5 * rg * (1.0 - th * th) * (_GC * (1.0 + 3.0 * 0.044715 * rg * rg))
                y_a = cb * ya0
                y_b = h16 * gl
                dy_a = dm * sga
                dy_b = dm * sgb
                col = lambda s: slice(s * D + g * CG, s * D + (g + 1) * CG)
                dp_ref[rows, col(5)] = (dm * y_a * sga * (1.0 - sga)).astype(BF16)
                dp_ref[rows, col(6)] = (dm * y_b * sgb * (1.0 - sgb)).astype(BF16)
                dp_ref[rows, col(4)] = (dy_b * h16 * dgl).astype(BF16)
                dp_ref[rows, col(0)] = (dy_a * ya0).astype(BF16)
                dye[rows, cs] = dy_a * cb
                dh16 = dy_b * gl

                a_next = an[:, cs]
                g_next = gn[:, cs]
                s_ba = jnp.zeros((8, CG), F32)
                s_bx = jnp.zeros((8, CG), F32)
                s_ls = jnp.zeros((8, CG), F32)
                for sb in (1, 0):
                    rr = r0 + 8 * sb
                    first = (row + (t * tt + rr)) == 0
                    r8 = pl.ds(rr, 8)
                    uu, r, ig, a, mult = (sv_ref[pln, r8, cs] for pln in (SV_U, SV_R, SV_I, SV_A, SV_MULT))
                    ca = jnp.where(row < 7, pltpu.roll(a, 7, 0), a_next)
                    cb_ = dh16[8 * sb:8 * sb + 8, :]
                    for s in (1, 2, 4):
                        a_sh = jnp.where(row < 8 - s, pltpu.roll(ca, 8 - s, 0), 1.0)
                        b_sh = jnp.where(row < 8 - s, pltpu.roll(cb_, 8 - s, 0), 0.0)
                        cb_ = ca * b_sh + cb_
                        ca = ca * a_sh
                    gv = ca * g_next + cb_
                    g_next = jnp.broadcast_to(gv[0:1, :], gv.shape)
                    a_next = jnp.broadcast_to(a[0:1, :], a.shape)
                    hprev = jnp.where(row >= 1, pltpu.roll(he[pl.ds(rr + 8, 8), cs], 1, 0),
                                      pltpu.roll(he[pl.ds(rr, 8), cs], 1, 0))
                    da = gv * hprev
                    dmult = jnp.where(first, 0.0, gv * ig * uu)
                    dla = da * a + jnp.where(mult > 0.0, dmult * (-(a * a) / mult), 0.0)
                    drp = dla * (LRU_C * ls) * r * (1.0 - r)
                    dip = gv * mult * uu * ig * (1.0 - ig)
                    s_ls = s_ls + dla * (LRU_C * r)
                    s_ba = s_ba + drp
                    s_bx = s_bx + dip
                    drp_s[pl.ds(rr, 8), cs] = drp
                    dip_s[pl.ds(rr, 8), cs] = dip
                    due[pl.ds(rr, 8), cs] = gv * mult * ig
                an[:, cs] = a_next
                gn[:, cs] = g_next
                sums_ref[M_BA:M_BA + 1, cs] += jnp.sum(s_ba, axis=0, keepdims=True)
                sums_ref[M_BX:M_BX + 1, cs] += jnp.sum(s_bx, axis=0, keepdims=True)
                sums_ref[M_LS:M_LS + 1, cs] += jnp.sum(s_ls, axis=0, keepdims=True)
            return carry

        lax.fori_loop(0, nblk, blk, 0)

        drp_b = drp_s[...].astype(BF16)
        dip_b = dip_s[...].astype(BF16)
        ub = sv_ref[SV_U].astype(BF16)
        for h in range(HEADS):
            cs = slice(h * HB, (h + 1) * HB)
            due[0:tt, cs] += _dot_nt(drp_b[:, cs], wa_ref[h]) + _dot_nt(dip_b[:, cs], wx_ref[h])
            gwa_ref[h] += _dot_tn(ub[:, cs], drp_b[:, cs])
            gwx_ref[h] += _dot_tn(ub[:, cs], dip_b[:, cs])

        acc8[...] = jnp.zeros_like(acc8)
        _conv_bwd_rows(tt, proj_ref, prm_ref, due, dye, dp_ref, acc8)
        for k, dst in enumerate([M_CBIAS] + [M_WB + k for k in range(4)] + [M_WA + k for k in range(3)]):
            sums_ref[dst:dst + 1, :] += jnp.sum(acc8[8 * k:8 * k + 8, :], axis=0, keepdims=True)
        due[tt:tt + 8, :] = due[0:8, :]
        dye[tt:tt + 8, :] = dye[0:8, :]

        @pl.when(i == nt - 1)
        def _():
            sums_ref[M_LS:M_LS + 1, :] = sums_ref[M_LS:M_LS + 1, :] * _sig(-prm_ref[P_LAM:P_LAM + 1, :])

    big = lambda: pltpu.VMEM((tt + 8, D), F32)
    tile = lambda: pltpu.VMEM((tt, D), F32)
    return pl.pallas_call(
        body, name="mixer_bwd", grid=(nt,),
        in_specs=[pl.BlockSpec((tt, 7 * D), lambda i: (rev(i), 0)),
                  pl.BlockSpec((tt, D), lambda i: (rev(i), 0)),
                  pl.BlockSpec((8, D), lambda i: (halo(i), 0)),
                  _ANY,
                  pl.BlockSpec((tt, D), lambda i: (rev(i), 0)),
                  pl.BlockSpec((16, D), lambda i: (0, 0)),
                  pl.BlockSpec((HEADS, HB, HB), lambda i: (0, 0, 0)),
                  pl.BlockSpec((HEADS, HB, HB), lambda i: (0, 0, 0))],
        out_specs=[pl.BlockSpec((tt, 7 * D), lambda i: (rev(i), 0)),
                   pl.BlockSpec((16, D), lambda i: (0, 0)),
                   pl.BlockSpec((HEADS, HB, HB), lambda i: (0, 0, 0)),
                   pl.BlockSpec((HEADS, HB, HB), lambda i: (0, 0, 0))],
        out_shape=[jax.ShapeDtypeStruct((t_len, 7 * D), BF16), jax.ShapeDtypeStruct((16, D), F32),
                   jax.ShapeDtypeStruct((HEADS, HB, HB), F32), jax.ShapeDtypeStruct((HEADS, HB, HB), F32)],
        scratch_shapes=[big(), big(), big(), tile(), tile(),
                        pltpu.VMEM((8, D), F32), pltpu.VMEM((8, D), F32), pltpu.VMEM((64, D), F32),
                        pltpu.VMEM((3, len(SV_PLANES), tt, D), F32), pltpu.SemaphoreType.DMA((3,))],
        compiler_params=_cp("arbitrary"),
    )(proj, hl, hl, sv, dmg, prm, wa, wx)


def _in_proj_bwd(dproj, w_in, x, dx1, mod, g_mix):
    t_len = x.shape[0]
    tm = min(TM, t_len)

    def body(dp_ref, w_ref, x_ref, dx1_ref, mod_ref, g_ref, gx_ref, sums_ref, acc):
        @pl.when(pl.program_id(0) == 0)
        def _():
            sums_ref[...] = jnp.zeros_like(sums_ref)

        def write(rows, dx):
            gx_ref[rows, :] = dx

        zero = jnp.zeros((8, D), F32)
        sums = (zero, zero, zero)
        for sub in _sub_blocks(tm):
            acc[sub, :] = _dot_nt(dp_ref[sub, :], w_ref[...])
            sums = _norm_bwd_rows(sub, sums, acc, x_ref, dx1_ref, mod_ref[1:2, :], g_ref[...], write)
        _add_norm_sums(sums_ref, sums)

    return pl.pallas_call(
        body, name="in_proj_bwd", grid=(t_len // tm,),
        in_specs=[pl.BlockSpec((tm, 7 * D), lambda i: (i, 0)),
                  _resident((D, 7 * D)),
                  pl.BlockSpec((tm, D), lambda i: (i, 0)), pl.BlockSpec((tm, D), lambda i: (i, 0)),
                  pl.BlockSpec((8, D), lambda i: (0, 0)), pl.BlockSpec((1, D), lambda i: (0, 0))],
        out_specs=[pl.BlockSpec((tm, D), lambda i: (i, 0)), pl.BlockSpec((8, D), lambda i: (0, 0))],
        out_shape=[jax.ShapeDtypeStruct((t_len, D), F32), jax.ShapeDtypeStruct((8, D), F32)],
        scratch_shapes=[pltpu.VMEM((tm, D), F32)],
        compiler_params=_cp("arbitrary"),
    )(dproj, w_in, x, dx1, mod, g_mix)


def _in_wgrad(h, dproj, g_wa, g_wx, order):
    t_len = h.shape[0]
    tk = min(TKI, t_len)
    nk = t_len // tk
    cw = 7 * D // NDEV
    hr = HB // NDEV

    def body(ord_ref, h_ref, d_ref, ga_ref, gx_ref, parts_ref, pa_ref, px_ref, acc, *scr):
        rs, sems = scr[:-6], scr[-6:]
        p, k = pl.program_id(0), pl.program_id(1)

        def head_rows(ref):
            return lambda s: ref.at[:, pl.ds(s * hr, hr), :]

        @pl.when((p == 0) & (k == 0))
        def _():
            for s in range(NDEV):
                _rs_send(head_rows(ga_ref)(s), pa_ref, s, *sems[0:3])
                _rs_send(head_rows(gx_ref)(s), px_ref, s, *sems[3:6])

        @pl.when(k == 0)
        def _():
            acc[...] = jnp.zeros_like(acc)

        acc[...] += _dot_tn(h_ref[...], d_ref[...])

        @pl.when(k == nk - 1)
        def _():
            q = ord_ref[p]
            for half in range(2):
                rs[0][q, half] = acc[:, half * cw:(half + 1) * cw].astype(BF16)
            _rs2_to_sibling(q, rs)

        @pl.when((k == min(1, nk - 1)) & (p > 0))
        def _():
            _rs2_forward(ord_ref[p - 1], parts_ref, rs)

        @pl.when((p == NCHIP - 1) & (k == nk - 1))
        def _():
            _rs2_forward(ord_ref[p], parts_ref, rs)
            _rs2_finish(parts_ref, rs)
            _rs_finish(head_rows(ga_ref), pa_ref, *sems[0:3])
            _rs_finish(head_rows(gx_ref), px_ref, *sems[3:6])

    return pl.pallas_call(
        body, name="in_wgrad",
        grid_spec=pltpu.PrefetchScalarGridSpec(
            num_scalar_prefetch=1, grid=(NCHIP, nk),
            in_specs=[pl.BlockSpec((tk, D), lambda p, k, o: (k, 0)),
                      pl.BlockSpec((tk, 2 * cw), lambda p, k, o: (k, o[p])), _ANY, _ANY],
            out_specs=[_ANY, _ANY, _ANY],
            scratch_shapes=[pltpu.VMEM((D, 2 * cw), F32)] + _rs2_scratch((D, cw)) + _RS_SEMS * 2),
        out_shape=[jax.ShapeDtypeStruct((NCHIP, D, cw), BF16), jax.ShapeDtypeStruct((NDEV, HEADS, hr, HB), F32),
                   jax.ShapeDtypeStruct((NDEV, HEADS, hr, HB), F32)],
        compiler_params=_cp("arbitrary", "arbitrary"),
    )(order, h, dproj, g_wa, g_wx)


def _adam_math(w, g, m, v):
    m = ADAM_B1 * m + (1.0 - ADAM_B1) * g
    v = ADAM_B2 * v + (1.0 - ADAM_B2) * (g * g)
    m_hat = m / (1.0 - ADAM_B1 ** ADAM_STEP)
    v_hat = v / (1.0 - ADAM_B2 ** ADAM_STEP)
    delta = -ADAM_LR * (m_hat / (jnp.sqrt(v_hat) + ADAM_EPS) + ADAM_WD * w)
    return delta, m, v


def _ada_bwd(c_all, dmod_cols, w, m, v):
    rb = 256
    n = w.shape[1]
    nrow = c_all.shape[0]

    def body(c_ref, d_ref, w_ref, m_ref, v_ref, g_ref, dl_ref, nm_ref, nv_ref):
        cv = c_ref[...]
        g = _dot_tn((cv * _sig(cv)).astype(BF16), d_ref[...].astype(BF16))
        g_ref[...] = g
        dl_ref[...], nm_ref[...], nv_ref[...] = _adam_math(w_ref[...], g, m_ref[...], v_ref[...])

    blk = pl.BlockSpec((rb, n), lambda i: (i, 0))
    sds = jax.ShapeDtypeStruct(w.shape, F32)
    return pl.pallas_call(
        body, name="ada_bwd", grid=(D // rb,),
        in_specs=[pl.BlockSpec((nrow, rb), lambda i: (0, i)), pl.BlockSpec((nrow, n), lambda i: (0, 0)), blk, blk, blk],
        out_specs=[blk, blk, blk, blk], out_shape=[sds, sds, sds, sds],
        compiler_params=_cp("parallel"),
    )(c_all, dmod_cols, w, m, v)


def _adam(name, parts, w, m, v):
    p, r, c = parts.shape
    rb = max([cand for cand in range(8, min(r, 256) + 1, 8) if r % cand == 0], default=r)

    def body(p_ref, w_ref, m_ref, v_ref, g_ref, dl_ref, nm_ref, nv_ref):
        g = p_ref[0].astype(F32)
        for q in range(1, p):
            g = g + p_ref[q].astype(F32)
        g_ref[...] = g
        dl_ref[...], nm_ref[...], nv_ref[...] = _adam_math(w_ref[...], g, m_ref[...], v_ref[...])

    blk = pl.BlockSpec((rb, c), lambda i: (i, 0))
    sds = jax.ShapeDtypeStruct((r, c), F32)
    return pl.pallas_call(
        body, name=name, grid=(r // rb,),
        in_specs=[pl.BlockSpec((p, rb, c), lambda i: (0, i, 0)), blk, blk, blk],
        out_specs=[blk, blk, blk, blk], out_shape=[sds, sds, sds, sds],
        compiler_params=_cp("parallel"),
    )(parts, w, m, v)


_SMALL_SEMS = [pltpu.SemaphoreType.DMA((7,)), pltpu.SemaphoreType.DMA((7,)), pltpu.SemaphoreType.DMA]
_VMEM = pl.BlockSpec(memory_space=pltpu.VMEM)


def _exchange_small(x_ref, out_ref, send_sems, recv_sems, local_sem):
    m_per = x_ref.shape[0]
    x, y, c = _my_pos()
    me, sibling = (x, y, c), (x, y, 1 - c)
    chips = [(1 - x, y), (x, 1 - y), (1 - x, 1 - y)]

    def rows(px, py, pc):
        return out_ref.at[pl.ds((4 * px + 2 * py + pc) * m_per, m_per), :]

    def copy(k, block, to, src=None):
        return pltpu.make_async_remote_copy(
            src_ref=rows(*block) if src is None else src, dst_ref=rows(*block),
            send_sem=send_sems.at[k], recv_sem=recv_sems.at[k], device_id=to, device_id_type=MESH)

    mine = pltpu.make_async_copy(x_ref, rows(*me), local_sem)
    mine.start()
    first = [copy(0, me, sibling, src=x_ref)]
    first += [copy(1 + j, me, (*chip, c), src=x_ref) for j, chip in enumerate(chips)]
    for cp in first:
        cp.start()
    passed = [copy(4 + j, (*chip, c), sibling) for j, chip in enumerate(chips)]
    for j, chip in enumerate(chips):
        copy(1 + j, (*chip, c), me).wait_recv()
        passed[j].start()
    copy(0, sibling, me).wait_recv()
    for j, chip in enumerate(chips):
        copy(4 + j, (*chip, 1 - c), me).wait_recv()
    for cp in first + passed:
        cp.wait_send()
    mine.wait()


def _gather_small_grads(sums1, sums2, msums, d_gt1, d_gt2, d_gfin, loss):
    def body(s1, s2, ms, g1, g2, gf, ls, out_ref, pack, *sems):
        rows = [s1[S_SH:S_SH + 1, :], s1[S_SC:S_SC + 1, :], g1[0:1, :],
                s2[S_SH:S_SH + 1, :], s2[S_SC:S_SC + 1, :], g2[0:1, :],
                s1[S_G:S_G + 1, :], ms[M_CBIAS:M_CBIAS + 1, :], ms[M_BA:M_BA + 1, :], ms[M_BX:M_BX + 1, :],
                ms[M_LS:M_LS + 1, :], s2[S_G:S_G + 1, :], gf[...]]
        rows += [ms[M_WA + k:M_WA + k + 1, :] for k in range(3)] + [ms[M_WB + k:M_WB + k + 1, :] for k in range(4)]
        rows += [jnp.broadcast_to(ls[0:1, 0:1], (1, D))]
        for i, v in enumerate(rows):
            pack[i:i + 1, :] = v
        pack[len(rows):24, :] = jnp.zeros((24 - len(rows), D), F32)
        _exchange_small(pack, out_ref, *sems)

    return pl.pallas_call(
        body, name="gather_small", out_shape=jax.ShapeDtypeStruct((NDEV * 24, D), F32),
        in_specs=[_VMEM] * 7, out_specs=_VMEM, scratch_shapes=[pltpu.VMEM((24, D), F32)] + _SMALL_SEMS,
    )(sums1, sums2, msums, d_gt1, d_gt2, d_gfin, loss)


def _ada_mod(pack, w_ada, b_cols):
    ncol = w_ada.shape[1]

    def body(p_ref, w_ref, b_ref, all_ref, mod_ref, cols, *sems):
        _exchange_small(p_ref, all_ref, *sems[0:3])
        c_all = jnp.concatenate([all_ref[8 * d:8 * d + 1, 0:D] for d in range(NDEV)], axis=0)
        c16 = jnp.concatenate([c_all, jnp.zeros_like(c_all)], axis=0)
        mod16 = _dot((c16 * _sig(c16)).astype(BF16), w_ref[...].astype(BF16)) + b_ref[...]
        cols[...] = mod16[0:NDEV]
        _exchange_small(cols, mod_ref, *sems[3:6])

    return pl.pallas_call(
        body, name="ada_mod",
        out_shape=[jax.ShapeDtypeStruct((NDEV * 8, pack.shape[1]), F32), jax.ShapeDtypeStruct((NDEV * 8, ncol), F32)],
        in_specs=[_VMEM, _VMEM, _VMEM], out_specs=[_VMEM, _VMEM],
        scratch_shapes=[pltpu.VMEM((NDEV, ncol), F32)] + _SMALL_SEMS * 2,
        compiler_params=_cp(),
    )(pack, w_ada, b_cols)


def _blk_rows(n):
    return lambda ref, b: ref.at[pl.ds(pl.multiple_of(b * n, 8), n), :]


def _blk_lead(ref, b):
    return ref.at[b]


def _blk_heads(ref, b):
    return ref.at[:, pl.ds(pl.multiple_of(b * (HB // NDEV), 8), HB // NDEV), :]


def _ag_phases(ins, outs, slicers, send_sems, recv_sems, local_sems):
    na = len(ins)
    x, y, c = _my_pos()
    me, sibling = (x, y, c), (x, y, 1 - c)
    chips = [(1 - x, y), (x, 1 - y), (1 - x, 1 - y)]

    def copy(a, k, block, to, from_shard=False):
        px, py, pc = block
        dst = slicers[a](outs[a], 4 * px + 2 * py + pc)
        return pltpu.make_async_remote_copy(
            src_ref=ins[a] if from_shard else dst, dst_ref=dst,
            send_sem=send_sems.at[a * 7 + k], recv_sem=recv_sems.at[a * 7 + k], device_id=to, device_id_type=MESH)

    def local(a):
        return pltpu.make_async_copy(ins[a], slicers[a](outs[a], 4 * x + 2 * y + c), local_sems.at[a])

    def firsts(a):
        return [copy(a, 0, me, sibling, True)] + [copy(a, 1 + j, me, (*chip, c), True) for j, chip in enumerate(chips)]

    def start():
        for a in range(na):
            local(a).start()
            for cp in firsts(a):
                cp.start()

    def forward():
        for a in range(na):
            for j, chip in enumerate(chips):
                copy(a, 1 + j, (*chip, c), me).wait_recv()
                copy(a, 4 + j, (*chip, c), sibling).start()

    def finish():
        for a in range(na):
            copy(a, 0, sibling, me).wait_recv()
            for j, chip in enumerate(chips):
                copy(a, 4 + j, (*chip, 1 - c), me).wait_recv()
        for a in range(na):
            for cp in firsts(a) + [copy(a, 4 + j, (*chip, c), sibling) for j, chip in enumerate(chips)]:
                cp.wait_send()
            local(a).wait()

    return start, forward, finish


def _ag_sems(na):
    return [pltpu.SemaphoreType.DMA((7 * na,)), pltpu.SemaphoreType.DMA((7 * na,)), pltpu.SemaphoreType.DMA((na,))]


def _local_step(x, target, mod, g_mix, g_ffn, g_fin, prm, w_in_shard, shards):
    fulls = [(HEADS, HB, HB), (HEADS, HB, HB), (D, D), (NDEV, FB, D), (DFF, D)]
    slicers = [_blk_heads, _blk_heads, _blk_rows(D // NDEV), _blk_lead, _blk_rows(DFF // NDEV)]
    my_chip = _my_index() >> 1
    own_first = (my_chip ^ jnp.arange(NCHIP, dtype=jnp.int32)).astype(jnp.int32)
    early, late = [0, 1, 2, 4], [3]
    pick = lambda lst, idx: [lst[i] for i in idx]
    proj, h, w_in, (wa, wx, w_out, w_down) = _in_proj(x, mod, g_mix, w_in_shard, own_first, pick(shards, early),
                                                      pick(fulls, early), pick(slicers, early))
    merged, hl, sv, (w_gu,) = _mixer_fwd(proj, prm, wa, wx, pick(shards, late), pick(fulls, late),
                                         pick(slicers, late))
    w_gu = w_gu.reshape(2, 4, FB, D)
    x1, h2 = _out_proj(merged, x, mod, g_ffn, w_out)
    gu, dx2, dx2b, loss, d_gfin = _ffn_fwd(h2, x1, target, mod, g_fin, w_gu, w_down)
    dgu, act, dx1, dx1b, dmg, sums2 = _ffn_bwd(dx2, gu, x1, mod, g_ffn, w_gu, w_down, w_out)
    chip_order = _xor_order(_my_index() >> 1, NCHIP)
    p_wgu = _gu_wgrad(h2, dgu, chip_order)
    p_wdown, d_gt2 = _scaled_wgrad("down_wgrad", act, dx2b, w_down, 5, mod, chip_order)
    p_wout, d_gt1 = _scaled_wgrad("out_wgrad", merged, dx1b, w_out, 2, mod, chip_order)
    dproj, msums, g_wa, g_wx = _mixer_bwd(proj, hl, sv, dmg, prm, wa, wx)
    p_win, p_wa, p_wx = _in_wgrad(h, dproj, g_wa, g_wx, chip_order)
    grad_x, sums1 = _in_proj_bwd(dproj, w_in, x, dx1, mod, g_mix)
    return dict(loss=loss, grad_x=grad_x, d_gfin=d_gfin, sums1=sums1, sums2=sums2, msums=msums,
                d_gt1=d_gt1, d_gt2=d_gt2, p_win=p_win, p_wa=p_wa, p_wx=p_wx, p_wout=p_wout, p_wgu=p_wgu,
                p_wdown=p_wdown)


def kernel(x, c, w_ada, b_ada, g_norm_mix, w_in, conv_a_w, conv_b_w, conv_b_bias, w_rg_a, b_rg_a, w_rg_x, b_rg_x, lru_lambda, w_out, g_norm_ffn, w_gate_up, w_down, g_norm_final, loss_target, m_w_ada, m_b_ada, m_g_norm_mix, m_w_in, m_conv_a_w, m_conv_b_w, m_conv_b_bias, m_w_rg_a, m_b_rg_a, m_w_rg_x, m_b_rg_x, m_lru_lambda, m_w_out, m_g_norm_ffn, m_w_gate_up, m_w_down, m_g_norm_final, v_w_ada, v_b_ada, v_g_norm_mix, v_w_in, v_conv_a_w, v_conv_b_w, v_conv_b_bias, v_w_rg_a, v_b_rg_a, v_w_rg_x, v_b_rg_x, v_lru_lambda, v_w_out, v_g_norm_ffn, v_w_gate_up, v_w_down, v_g_norm_final):
    me = 4 * lax.axis_index("x") + 2 * lax.axis_index("y") + lax.axis_index("c")
    ncol = w_ada.shape[2]
    cw = conv_a_w.shape[2]

    pack0 = jnp.concatenate([c, conv_a_w.reshape(1, 3 * cw), conv_b_w.reshape(1, 4 * cw)], axis=1)
    b_cols = lax.dynamic_slice_in_dim(b_ada, me * ncol, ncol, axis=1)
    got0, got1 = _ada_mod(jnp.broadcast_to(pack0, (8, pack0.shape[1])), w_ada[0], b_cols)
    got0 = got0.reshape(NDEV, 8, -1)[:, 0, :]
    c_all = got0[:, :D]
    conv_a = got0[:, D:D + 3 * cw].reshape(NDEV, 3, cw).transpose(1, 0, 2).reshape(3, D)
    conv_b = got0[:, D + 3 * cw:].reshape(NDEV, 4, cw).transpose(1, 0, 2).reshape(4, D)
    c16 = jnp.concatenate([c_all, jnp.zeros((8, D), F32)], axis=0)
    mod6 = lax.dynamic_index_in_dim(got1.reshape(NDEV, NDEV, ncol), me, axis=1, keepdims=False).reshape(6, D)
    mod = jnp.concatenate([mod6, jnp.zeros((2, D), F32)], axis=0)

    tr = lambda a: jnp.swapaxes(a, 1, 2)
    shards = [w_rg_a[0].astype(BF16), w_rg_x[0].astype(BF16), w_out[0].astype(BF16), tr(w_gate_up)[0].astype(BF16),
              w_down[0].astype(BF16)]

    prm = jnp.concatenate([conv_a, conv_b, conv_b_bias, b_rg_a, b_rg_x, lru_lambda, jnp.zeros((5, D), F32)], axis=0)
    r = _local_step(x[0], loss_target[0], mod, g_norm_mix, g_norm_ffn, g_norm_final.reshape(1, D), prm,
                    w_in[0].astype(BF16), shards)

    parts = [r["p_win"], r["p_wa"], r["p_wx"], r["p_wout"], r["p_wgu"], r["p_wdown"]]
    big = {}
    for nm, p, w, m, v in (("w_in", parts[0], w_in, m_w_in, v_w_in), ("w_rg_a", parts[1], w_rg_a, m_w_rg_a, v_w_rg_a),
                           ("w_rg_x", parts[2], w_rg_x, m_w_rg_x, v_w_rg_x), ("w_out", parts[3], w_out, m_w_out, v_w_out),
                           ("w_gate_up", parts[4], tr(w_gate_up), tr(m_w_gate_up), tr(v_w_gate_up)),
                           ("w_down", parts[5], w_down, m_w_down, v_w_down)):
        two_d = (-1, w.shape[-1])
        outs = _adam("adam_" + nm, p.reshape((p.shape[0],) + w.reshape(two_d).shape), w.reshape(two_d), m.reshape(two_d),
                     v.reshape(two_d))
        big[nm] = [o.reshape(w.shape) for o in outs]
    big["w_gate_up"] = [tr(o) for o in big["w_gate_up"]]

    got2 = _gather_small_grads(r["sums1"], r["sums2"], r["msums"], r["d_gt1"], r["d_gt2"], r["d_gfin"],
                               r["loss"]).reshape(NDEV, 24, D)

    rep_w = jnp.concatenate([b_ada.reshape(6, D), g_norm_mix, conv_b_bias, b_rg_a, b_rg_x, lru_lambda, g_norm_ffn,
                             g_norm_final.reshape(1, D), jnp.zeros((3, D), F32)], axis=0)
    rep_m = jnp.concatenate([m_b_ada.reshape(6, D), m_g_norm_mix, m_conv_b_bias, m_b_rg_a, m_b_rg_x, m_lru_lambda,
                             m_g_norm_ffn, m_g_norm_final.reshape(1, D), jnp.zeros((3, D), F32)], axis=0)
    rep_v = jnp.concatenate([v_b_ada.reshape(6, D), v_g_norm_mix, v_conv_b_bias, v_b_rg_a, v_b_rg_x, v_lru_lambda,
                             v_g_norm_ffn, v_g_norm_final.reshape(1, D), jnp.ones((3, D), F32)], axis=0)
    rep = _adam("adam_rep", got2[:, :16, :], rep_w, rep_m, rep_v)

    conv_parts = lax.dynamic_slice_in_dim(got2[:, 13:21, :], me * cw, cw, axis=2)
    cv_w = jnp.concatenate([conv_a_w[0], conv_b_w[0], jnp.zeros((1, cw), F32)], axis=0)
    cv_m = jnp.concatenate([m_conv_a_w[0], m_conv_b_w[0], jnp.zeros((1, cw), F32)], axis=0)
    cv_v = jnp.concatenate([v_conv_a_w[0], v_conv_b_w[0], jnp.ones((1, cw), F32)], axis=0)
    cvo = _adam("adam_conv", conv_parts, cv_w, cv_m, cv_v)

    dmod_cols = lax.dynamic_slice_in_dim(got2[:, :6, :].reshape(NDEV, 6 * D), me * ncol, ncol, axis=1)
    dmod16 = jnp.concatenate([dmod_cols, jnp.zeros((8, ncol), F32)], axis=0)
    ada = _ada_bwd(c16, dmod16, w_ada[0], m_w_ada[0], v_w_ada[0])

    loss = jnp.sum(got2[:, 20, 0])

    def pick(q):
        one = lambda i: rep[q][i:i + 1]
        return [ada[q].reshape(w_ada.shape), rep[q][0:6].reshape(b_ada.shape), one(6), big["w_in"][q],
                cvo[q][0:3].reshape(conv_a_w.shape), cvo[q][3:7].reshape(conv_b_w.shape), one(7),
                big["w_rg_a"][q], one(8), big["w_rg_x"][q], one(9), one(10), big["w_out"][q], one(11),
                big["w_gate_up"][q], big["w_down"][q], rep[q][12]]

    return (loss, r["grad_x"].reshape(x.shape), *pick(0), *pick(1), *pick(2), *pick(3))
```

```python
import math

import jax
import jax.numpy as jnp
from jax import lax
from jax.experimental import pallas as pl
from jax.experimental.pallas import tpu as pltpu

F32 = jnp.float32
BF16 = jnp.bfloat16

D = 1024
DFF = 2816
NDEV = 8
HEADS = 4
HB = D // HEADS
FB = DFF // 4
EPS = 1e-6
LRU_C = 8.0
ADAM_LR, ADAM_B1, ADAM_B2, ADAM_EPS, ADAM_WD, ADAM_STEP = 0.001, 0.9, 0.999, 1e-08, 0.01, 10

VMEM_LIMIT = 56 * 1024 * 1024
TM = 512
TMI = 1024
TMF = 256
TK = 2048
TKI = 2048
SUB = 256
TT = 256
CG = 256
MESH = pl.DeviceIdType.MESH


def _cp(*sem):
    return pltpu.CompilerParams(dimension_semantics=sem, vmem_limit_bytes=VMEM_LIMIT)


def _sig(x):
    return 1.0 / (1.0 + jnp.exp(-x))


def _log_sigmoid(x):
    z = jnp.exp(-jnp.abs(x))
    u = 1.0 + z
    d = u - 1.0
    l1p = jnp.where(d == 0.0, z, jnp.log(u) * (z / jnp.where(d == 0.0, 1.0, d)))
    return -(jnp.maximum(-x, 0.0) + l1p)


def _neg_expm1(x):
    p = x * (1.0 + x * 0.5 * (1.0 + x * (1.0 / 3.0) * (1.0 + x * 0.25 * (1.0 + x * 0.2 * (1.0 + x * (1.0 / 6.0))))))
    return jnp.where(x > -0.25, -p, 1.0 - jnp.exp(x))


_GC = math.sqrt(2.0 / math.pi)


def _gelu(x):
    t = jnp.tanh(_GC * (x + 0.044715 * x * x * x))
    return 0.5 * x * (1.0 + t), t


def _dot(a, b):
    return jnp.dot(a, b, preferred_element_type=F32)


def _dot_nt(a, b):
    return lax.dot_general(a, b, (((1,), (1,)), ((), ())), preferred_element_type=F32)


def _dot_tn(a, b):
    return lax.dot_general(a, b, (((0,), (0,)), ((), ())), preferred_element_type=F32)


def _resident(shape):
    return pl.BlockSpec(shape, lambda *_: (0,) * len(shape), pipeline_mode=pl.Buffered(1))


def _sub_blocks(n_rows):
    step = min(SUB, n_rows)
    return [slice(r, r + step) for r in range(0, n_rows, step)]


def _fold8(v):
    return v[0:8] + v[8:16]


def _pj(ref, s, rows=slice(None), cols=slice(0, D)):
    return ref[rows, s * D + cols.start:s * D + cols.stop]


def _in_proj(x, mod, g_mix, w_shard, order, shards, fulls, slicers):
    t_len = x.shape[0]
    tm = min(TMI, t_len)
    ni = t_len // tm
    na = len(shards)
    cw = 7 * D // NDEV
    rc = 32

    def body(ord_ref, x_ref, mod_ref, g_ref, wsh_ref, *rest):
        ins, (proj_ref, h_ref, wfull_ref), outs = rest[:na], rest[na:na + 3], rest[na + 3:2 * na + 3]
        h_scr, w_scr, wsend, wrecv, wlocal, wout = rest[2 * na + 3:2 * na + 9]
        start, forward, finish = _ag_phases(ins, outs, slicers, *rest[2 * na + 9:])
        p, i = pl.program_id(0), pl.program_id(1)
        x_, y_, c = _my_pos()
        me, sibling = (x_, y_, c), (x_, y_, 1 - c)
        chip_at = [None, (x_, 1 - y_), (1 - x_, y_), (1 - x_, 1 - y_)]

        def cols(px, py, pc):
            return w_scr.at[:, pl.ds(pl.multiple_of((4 * px + 2 * py + pc) * cw, 128), cw)]

        def wcopy(k, block, to, from_shard=False):
            dst = cols(*block)
            return pltpu.make_async_remote_copy(src_ref=wsh_ref if from_shard else dst, dst_ref=dst,
                                                send_sem=wsend.at[k], recv_sem=wrecv.at[k], device_id=to,
                                                device_id_type=MESH)

        own_local = pltpu.make_async_copy(wsh_ref, cols(*me), wlocal)
        to_hbm = pltpu.make_async_copy(w_scr, wfull_ref, wout)

        @pl.when((p == 0) & (i == 0))
        def _():
            own_local.start()
            wcopy(0, me, sibling, True).start()
            for q in (1, 2):
                wcopy(q, me, (*chip_at[q], c), True).start()
            own_local.wait()
            wcopy(0, sibling, me).wait_recv()

        @pl.when((p == 0) & (i == ni // 2))
        def _():
            wcopy(3, me, (*chip_at[3], c), True).start()

        for q in (1, 2, 3):
            @pl.when((p == q - 1) & (i == ni - 1))
            def _():
                wcopy(q, (*chip_at[q], c), me).wait_recv()
                wcopy(3 + q, (*chip_at[q], c), sibling).start()

            @pl.when((p == q) & (i == 0))
            def _():
                wcopy(3 + q, (*chip_at[q], 1 - c), me).wait_recv()

        @pl.when((p == 1) & (i == 0))
        def _():
            start()

        @pl.when((p == NCHIP - 1) & (i == ni // 2))
        def _():
            forward()

        @pl.when((p == NCHIP - 1) & (i == 0))
        def _():
            to_hbm.start()

        gs = g_ref[...] * (1.0 + mod_ref[1:2, :])
        sh = mod_ref[0:1, :]

        wcols = pl.ds(pl.multiple_of(ord_ref[p] * (2 * cw), 128), 2 * cw)
        for sub in _sub_blocks(tm):
            for r0 in range(sub.start, sub.stop, rc):
                xv = x_ref[r0:r0 + rc, :]
                r = lax.rsqrt(jnp.mean(xv * xv, axis=-1, keepdims=True) + EPS)
                h_scr[r0:r0 + rc, :] = (xv * r * gs + sh).astype(BF16)
            proj_ref[sub, :] = _dot(h_scr[sub, :], w_scr[:, wcols]).astype(BF16)

        @pl.when(p == 0)
        def _():
            h_ref[...] = h_scr[...]

        @pl.when((p == NCHIP - 1) & (i == ni - 1))
        def _():
            wcopy(0, me, sibling, True).wait_send()
            for q in (1, 2, 3):
                wcopy(q, me, (*chip_at[q], c), True).wait_send()
                wcopy(3 + q, (*chip_at[q], c), sibling).wait_send()
            finish()
            to_hbm.wait()

    res = pl.pallas_call(
        body, name="in_proj",
        grid_spec=pltpu.PrefetchScalarGridSpec(
            num_scalar_prefetch=1, grid=(NCHIP, ni),
            in_specs=[pl.BlockSpec((tm, D), lambda p, i, o: (i, 0)),
                      pl.BlockSpec((8, D), lambda p, i, o: (0, 0)),
                      pl.BlockSpec((1, D), lambda p, i, o: (0, 0))] + [_ANY] * (1 + na),
            out_specs=[pl.BlockSpec((tm, 2 * cw), lambda p, i, o: (i, o[p])),
                       pl.BlockSpec((tm, D), lambda p, i, o: (jnp.where(p == 0, i, ni - 1), 0))]
            + [_ANY] * (1 + na),
            scratch_shapes=[pltpu.VMEM((tm, D), BF16), pltpu.VMEM((D, 7 * D), BF16),
                            pltpu.SemaphoreType.DMA((7,)), pltpu.SemaphoreType.DMA((7,)),
                            pltpu.SemaphoreType.DMA, pltpu.SemaphoreType.DMA] + _ag_sems(na)),
        out_shape=[jax.ShapeDtypeStruct((t_len, 7 * D), BF16), jax.ShapeDtypeStruct((t_len, D), BF16),
                   jax.ShapeDtypeStruct((D, 7 * D), BF16)]
        + [jax.ShapeDtypeStruct(f, sh.dtype) for f, sh in zip(fulls, shards)],
        compiler_params=_cp("arbitrary", "arbitrary"),
    )(order, x, mod, g_mix, w_shard, *shards)
    return res[0], res[1], res[2], res[3:]


P_WA, P_WB, P_CBIAS, P_BA, P_BX, P_LAM = 0, 3, 7, 8, 9, 10
SV_PLANES = SV_U, SV_YA, SV_R, SV_I, SV_A, SV_MULT = range(6)


def _lru_gates(rp, ip, ls, first_row):
    r = _sig(rp)
    ig = _sig(ip)
    la = LRU_C * r * ls
    a = jnp.exp(la)
    m2 = _neg_expm1(2.0 * la)
    mult = jnp.where(first_row, 1.0, jnp.sqrt(jnp.maximum(m2, 0.0)))
    return r, ig, la, a, m2, mult


def _shift_down(cur, prev, s, row):
    return jnp.where(row >= s, pltpu.roll(cur, s, 0), pltpu.roll(prev, s, 0))


def _shift_up(cur, nxt, s, row):
    return jnp.where(row < 8 - s, pltpu.roll(cur, 8 - s, 0), pltpu.roll(nxt, 8 - s, 0))


def _conv_fwd_rows(tt, proj_ref, prm_ref, xe, ve, u_s, ub_s, ya_s):
    row = lax.broadcasted_iota(jnp.int32, (8, CG), 0)
    w_b = [prm_ref[P_WB + k:P_WB + k + 1, :] for k in range(4)]
    w_a = [prm_ref[P_WA + k:P_WA + k + 1, :] for k in range(3)]
    bias = prm_ref[P_CBIAS:P_CBIAS + 1, :]

    def blk(ib, carry):
        r0 = pl.multiple_of(ib * 16, 16)
        rows = pl.ds(r0, 16)
        for g in range(D // CG):
            cs = slice(g * CG, (g + 1) * CG)
            x16 = _pj(proj_ref, 3, rows, cs).astype(F32)
            v16 = _pj(proj_ref, 1, rows, cs).astype(F32) * _pj(proj_ref, 2, rows, cs).astype(F32)
            xp = xe[pl.ds(r0, 8), cs]
            vp = ve[pl.ds(r0, 8), cs]
            xe[pl.ds(r0 + 8, 16), cs] = x16
            ve[pl.ds(r0 + 8, 16), cs] = v16
            us, yas = [], []
            for sb in range(2):
                xc, vc = x16[8 * sb:8 * sb + 8], v16[8 * sb:8 * sb + 8]
                u8 = bias[:, cs] + w_b[3][:, cs] * xc
                for s in (1, 2, 3):
                    u8 = u8 + w_b[3 - s][:, cs] * _shift_down(xc, xp, s, row)
                y8 = w_a[2][:, cs] * vc
                for s in (1, 2):
                    y8 = y8 + w_a[2 - s][:, cs] * _shift_down(vc, vp, s, row)
                us.append(u8)
                yas.append(y8)
                xp, vp = xc, vc
            u16 = jnp.concatenate(us, axis=0)
            u_s[rows, cs] = u16
            ub_s[rows, cs] = u16.astype(BF16)
            ya_s[rows, cs] = jnp.concatenate(yas, axis=0)
        return carry

    lax.fori_loop(0, tt // 16, blk, 0)


def _mixer_fwd(proj, prm, wa, wx, shards, fulls, slicers):
    t_len = proj.shape[0]
    tt = min(TT, t_len)
    nt = t_len // tt
    na = len(shards)

    def body(proj_hbm, prm_ref, wa_ref, wx_ref, *rest):
        ins, (mg_ref, hl_ref, sv_ref), outs = rest[:na], rest[na:na + 3], rest[na + 3:2 * na + 3]
        xe, ve, hc, rp_s, ip_s, ub_s, ring, ring_sem = rest[2 * na + 3:2 * na + 11]
        start, forward, finish = _ag_phases(ins, outs, slicers, *rest[2 * na + 11:])
        t = pl.program_id(0)

        def fetch(s):
            row0 = s * tt if isinstance(s, int) else pl.multiple_of(s * tt, tt)
            return pltpu.make_async_copy(proj_hbm.at[pl.ds(row0, tt), :], ring.at[s % 3], ring_sem.at[s % 3])

        @pl.when(t == 0)
        def _():
            for s in range(min(2, nt)):
                fetch(s).start(priority=1)

        @pl.when(t + 2 < nt)
        def _():
            fetch(t + 2).start(priority=1)

        fetch(t).wait()
        proj_ref = ring.at[t % 3]

        @pl.when(t == 0)
        def _():
            start()
            xe[0:8, :] = jnp.zeros((8, D), F32)
            ve[0:8, :] = jnp.zeros((8, D), F32)
            hc[...] = jnp.zeros((8, D), F32)

        @pl.when(t == (3 * nt) // 4)
        def _():
            forward()

        _conv_fwd_rows(tt, proj_ref, prm_ref, xe, ve, sv_ref.at[SV_U], ub_s, sv_ref.at[SV_YA])
        xe[0:8, :] = xe[tt:tt + 8, :]
        ve[0:8, :] = ve[tt:tt + 8, :]

        ub = ub_s[...]
        for h in range(HEADS):
            cs = slice(h * HB, (h + 1) * HB)
            rp_s[:, cs] = _dot(ub[:, cs], wa_ref[h]) + prm_ref[P_BA:P_BA + 1, cs]
            ip_s[:, cs] = _dot(ub[:, cs], wx_ref[h]) + prm_ref[P_BX:P_BX + 1, cs]

        ls_all = _log_sigmoid(prm_ref[P_LAM:P_LAM + 1, :])
        row = lax.broadcasted_iota(jnp.int32, (8, CG), 0)

        def blk(i, carry):
            r0 = pl.multiple_of(i * 16, 16)
            for g in range(D // CG):
                cs = slice(g * CG, (g + 1) * CG)
                ls = ls_all[:, cs]
                hprev = hc[:, cs]
                hs = []
                for sb in range(2):
                    rr = r0 + 8 * sb
                    first = (row + (t * tt + rr)) == 0
                    r8 = pl.ds(rr, 8)
                    r, ig, _, a, _, mult = _lru_gates(rp_s[r8, cs], ip_s[r8, cs], ls, first)
                    for plane, val in ((SV_R, r), (SV_I, ig), (SV_A, a), (SV_MULT, mult)):
                        sv_ref[plane, r8, cs] = val
                    b = mult * (ig * sv_ref[SV_U, r8, cs])
                    for s in (1, 2, 4):
                        a_sh = jnp.where(row >= s, pltpu.roll(a, s, 0), 1.0)
                        b_sh = jnp.where(row >= s, pltpu.roll(b, s, 0), 0.0)
                        b = a * b_sh + b
                        a = a * a_sh
                    hv = a * hprev + b
                    hprev = jnp.broadcast_to(hv[7:8, :], hv.shape)
                    hs.append(hv)
                hc[:, cs] = hprev
                h16 = jnp.concatenate(hs, axis=0)
                rows = pl.ds(r0, 16)
                gl, _ = _gelu(_pj(proj_ref, 4, rows, cs).astype(F32))
                y_b = h16 * gl
                y_a = _pj(proj_ref, 0, rows, cs).astype(F32) * sv_ref[SV_YA, rows, cs]
                mg = (_sig(_pj(proj_ref, 5, rows, cs).astype(F32)) * y_a
                      + _sig(_pj(proj_ref, 6, rows, cs).astype(F32)) * y_b)
                mg_ref[rows, cs] = mg.astype(BF16)
                hl_ref[rows, cs] = h16.astype(BF16)
            return carry

        lax.fori_loop(0, tt // 16, blk, 0)

        @pl.when(t == nt - 1)
        def _():
            finish()

    res = pl.pallas_call(
        body, name="mixer_fwd", grid=(nt,),
        in_specs=[_ANY,
                  pl.BlockSpec((16, D), lambda t: (0, 0)),
                  pl.BlockSpec((HEADS, HB, HB), lambda t: (0, 0, 0)),
                  pl.BlockSpec((HEADS, HB, HB), lambda t: (0, 0, 0))] + [_ANY] * na,
        out_specs=[pl.BlockSpec((tt, D), lambda t: (t, 0)), pl.BlockSpec((tt, D), lambda t: (t, 0)),
                   pl.BlockSpec((len(SV_PLANES), tt, D), lambda t: (0, t, 0))] + [_ANY] * na,
        out_shape=[jax.ShapeDtypeStruct((t_len, D), BF16), jax.ShapeDtypeStruct((t_len, D), BF16),
                   jax.ShapeDtypeStruct((len(SV_PLANES), t_len, D), F32)]
        + [jax.ShapeDtypeStruct(f, sh.dtype) for f, sh in zip(fulls, shards)],
        scratch_shapes=[pltpu.VMEM((tt + 8, D), F32), pltpu.VMEM((tt + 8, D), F32), pltpu.VMEM((8, D), F32),
                        pltpu.VMEM((tt, D), F32), pltpu.VMEM((tt, D), F32), pltpu.VMEM((tt, D), BF16),
                        pltpu.VMEM((3, tt, 7 * D), BF16), pltpu.SemaphoreType.DMA((3,))]
        + _ag_sems(na),
        compiler_params=_cp("arbitrary"),
    )(proj, prm, wa, wx, *shards)
    return res[0], res[1], res[2], res[3:]


def _out_proj(merged, x, mod, g_ffn, w_out):
    t_len = x.shape[0]
    tm = min(TMI, t_len)

    def body(mg_ref, x_ref, mod_ref, g_ref, w_ref, x1_ref, h2_ref):
        gt1 = mod_ref[2:3, :]
        gs = g_ref[...] * (1.0 + mod_ref[4:5, :])
        sh = mod_ref[3:4, :]
        for sub in _sub_blocks(tm):
            x1_ref[sub, :] = x_ref[sub, :] + gt1 * _dot(mg_ref[sub, :], w_ref[...])
            for r0 in range(sub.start, sub.stop, 16):
                x1 = x1_ref[r0:r0 + 16, :]
                r = lax.rsqrt(jnp.mean(x1 * x1, axis=-1, keepdims=True) + EPS)
                h2_ref[r0:r0 + 16, :] = (x1 * r * gs + sh).astype(BF16)

    return pl.pallas_call(
        body, name="out_proj", grid=(t_len // tm,),
        in_specs=[pl.BlockSpec((tm, D), lambda i: (i, 0)), pl.BlockSpec((tm, D), lambda i: (i, 0)),
                  pl.BlockSpec((8, D), lambda i: (0, 0)), pl.BlockSpec((1, D), lambda i: (0, 0)),
                  pl.BlockSpec((D, D), lambda i: (0, 0))],
        out_specs=[pl.BlockSpec((tm, D), lambda i: (i, 0)), pl.BlockSpec((tm, D), lambda i: (i, 0))],
        out_shape=[jax.ShapeDtypeStruct((t_len, D), F32), jax.ShapeDtypeStruct((t_len, D), BF16)],
        compiler_params=_cp("parallel"),
    )(merged, x, mod, g_ffn, w_out)


def _ffn_fwd(h2, x1, target, mod, g_fin, w_gu, w_down):
    t_len = x1.shape[0]
    tm = min(TMF, t_len)

    def body(h2_ref, x1_ref, tg_ref, mod_ref, g_ref, wgu_ref, wd_ref, gu_ref, dx2_ref, dx2b_ref, loss_ref, dg_ref, acc):
        @pl.when(pl.program_id(0) == 0)
        def _():
            loss_ref[...] = jnp.zeros_like(loss_ref)
            dg_ref[...] = jnp.zeros_like(dg_ref)

        hb = h2_ref[...]
        ffn = None
        nxt = (_dot_nt(hb, wgu_ref[0, 0]), _dot_nt(hb, wgu_ref[1, 0]))
        for j in range(4):
            gate, up = nxt
            if j < 3:
                nxt = (_dot_nt(hb, wgu_ref[0, j + 1]), _dot_nt(hb, wgu_ref[1, j + 1]))
            gu_ref[0, j] = gate.astype(BF16)
            gu_ref[1, j] = up.astype(BF16)
            act = (gate * _sig(gate) * up).astype(BF16)
            part = _dot(act, wd_ref[j * FB:(j + 1) * FB, :])
            ffn = part if ffn is None else ffn + part
        acc[...] = ffn

        gt2 = mod_ref[5:6, :]
        gf = g_ref[...]

        s_loss = s_dg = jnp.zeros((8, D), F32)
        for r0 in range(0, tm, 16):
            rows = slice(r0, r0 + 16)
            x2 = x1_ref[rows, :] + gt2 * acc[rows, :]
            r = lax.rsqrt(jnp.mean(x2 * x2, axis=-1, keepdims=True) + EPS)
            xn = x2 * r
            diff = xn * gf - tg_ref[rows, :]
            dy = diff * (1.0 / D)
            dxn = dy * gf
            dx2 = r * (dxn - xn * jnp.mean(dxn * xn, axis=-1, keepdims=True))
            dx2_ref[rows, :] = dx2
            dx2b_ref[rows, :] = dx2.astype(BF16)
            s_loss, s_dg = s_loss + _fold8(diff * diff), s_dg + _fold8(dy * xn)
        loss_ref[...] += jnp.sum(s_loss) * (0.5 / D)
        dg_ref[...] += jnp.sum(s_dg, axis=0, keepdims=True)

    row = pl.BlockSpec((tm, D), lambda i: (i, 0))
    return pl.pallas_call(
        body, name="ffn_fwd", grid=(t_len // tm,),
        in_specs=[row, row, row, pl.BlockSpec((8, D), lambda i: (0, 0)), pl.BlockSpec((1, D), lambda i: (0, 0)),
                  _resident((2, 4, FB, D)), _resident((DFF, D))],
        out_specs=[pl.BlockSpec((2, 4, tm, FB), lambda i: (0, 0, i, 0)), row, row,
                   pl.BlockSpec((1, 128), lambda i: (0, 0)), pl.BlockSpec((1, D), lambda i: (0, 0))],
        out_shape=[jax.ShapeDtypeStruct((2, 4, t_len, FB), BF16), jax.ShapeDtypeStruct((t_len, D), F32),
                   jax.ShapeDtypeStruct((t_len, D), BF16),
                   jax.ShapeDtypeStruct((1, 128), F32), jax.ShapeDtypeStruct((1, D), F32)],
        scratch_shapes=[pltpu.VMEM((tm, D), F32)],
        compiler_params=_cp("arbitrary"),
    )(h2, x1, target, mod, g_fin, w_gu, w_down)


S_SH, S_SC, S_G = 0, 1, 2


def _norm_bwd_rows(span, sums, dh_ref, x_ref, dres_ref, scale, gain, write):
    gs = 1.0 + scale
    s_sh, s_sc, s_g = sums
    for r0 in range(span.start, span.stop, 16):
        rows = slice(r0, r0 + 16)
        dh = dh_ref[rows, :]
        xv = x_ref[rows, :]
        r = lax.rsqrt(jnp.mean(xv * xv, axis=-1, keepdims=True) + EPS)
        xn = xv * r
        dhn = dh * gs
        dxn = dhn * gain
        write(rows, dres_ref[rows, :] + r * (dxn - xn * jnp.mean(dxn * xn, axis=-1, keepdims=True)))
        s_sh, s_sc, s_g = s_sh + _fold8(dh), s_sc + _fold8(dh * (xn * gain)), s_g + _fold8(dhn * xn)
    return s_sh, s_sc, s_g


def _add_norm_sums(sums_ref, sums):
    for dst, s in zip((S_SH, S_SC, S_G), sums):
        sums_ref[dst:dst + 1, :] += jnp.sum(s, axis=0, keepdims=True)


def _ffn_bwd(dx2, gu, x1, mod, g_ffn, w_gu, w_down, w_out):
    t_len = x1.shape[0]
    tm = min(TMF, t_len)

    def body(dx2_ref, gu_ref, x1_ref, mod_ref, g_ref, wgu_ref, wd_ref, wo_ref,
             dgu_ref, act_ref, dx1_ref, dx1b_ref, dmg_ref, sums_ref, acc, dmo, dact_s):
        @pl.when(pl.program_id(0) == 0)
        def _():
            sums_ref[...] = jnp.zeros_like(sums_ref)

        dffn = (dx2_ref[...] * mod_ref[5:6, :]).astype(BF16)
        dact_s[0] = _dot_nt(dffn, wd_ref[0:FB, :])
        for j in range(4):
            if j < 3:
                dact_s[(j + 1) % 2] = _dot_nt(dffn, wd_ref[(j + 1) * FB:(j + 2) * FB, :])
            for r0 in range(0, tm, 16):
                rows = slice(r0, r0 + 16)
                dact = dact_s[j % 2, rows, :]
                gate = gu_ref[0, j, rows, :].astype(F32)
                up = gu_ref[1, j, rows, :].astype(F32)
                sg = _sig(gate)
                silu = gate * sg
                act_ref[j, rows, :] = (silu * up).astype(BF16)
                dgu_ref[0, j, rows, :] = (dact * up * (sg * (1.0 + gate * (1.0 - sg)))).astype(BF16)
                dgu_ref[1, j, rows, :] = (dact * silu).astype(BF16)
            part = _dot(dgu_ref[0, j], wgu_ref[0, j]) + _dot(dgu_ref[1, j], wgu_ref[1, j])
            if j == 0:
                acc[...] = part
            else:
                acc[...] += part

        gt1 = mod_ref[2:3, :]

        def write(rows, dx1):
            dx1_ref[rows, :] = dx1
            dx1b_ref[rows, :] = dx1.astype(BF16)
            dmo[rows, :] = (dx1 * gt1).astype(BF16)

        zero = jnp.zeros((8, D), F32)
        sums = (zero, zero, zero)
        for sub in (slice(0, tm // 2), slice(tm // 2, tm)):
            sums = _norm_bwd_rows(sub, sums, acc, x1_ref, dx2_ref, mod_ref[4:5, :], g_ref[...], write)
            dmg_ref[sub, :] = _dot_nt(dmo[sub, :], wo_ref[...]).astype(BF16)
        _add_norm_sums(sums_ref, sums)

    row = pl.BlockSpec((tm, D), lambda i: (i, 0))
    return pl.pallas_call(
        body, name="ffn_bwd", grid=(t_len // tm,),
        in_specs=[row, pl.BlockSpec((2, 4, tm, FB), lambda i: (0, 0, i, 0)), row,
                  pl.BlockSpec((8, D), lambda i: (0, 0)), pl.BlockSpec((1, D), lambda i: (0, 0)),
                  _resident((2, 4, FB, D)), _resident((DFF, D)), _resident((D, D))],
        out_specs=[pl.BlockSpec((2, 4, tm, FB), lambda i: (0, 0, i, 0)),
                   pl.BlockSpec((4, tm, FB), lambda i: (0, i, 0)), row, row, row,
                   pl.BlockSpec((8, D), lambda i: (0, 0))],
        out_shape=[jax.ShapeDtypeStruct((2, 4, t_len, FB), BF16), jax.ShapeDtypeStruct((4, t_len, FB), BF16),
                   jax.ShapeDtypeStruct((t_len, D), F32), jax.ShapeDtypeStruct((t_len, D), BF16),
                   jax.ShapeDtypeStruct((t_len, D), BF16), jax.ShapeDtypeStruct((8, D), F32)],
        scratch_shapes=[pltpu.VMEM((tm, D), F32), pltpu.VMEM((tm, D), BF16), pltpu.VMEM((2, tm, FB), F32)],
        compiler_params=_cp("arbitrary"),
    )(dx2, gu, x1, mod, g_ffn, w_gu, w_down, w_out)


def _my_pos():
    return lax.axis_index("x"), lax.axis_index("y"), lax.axis_index("c")


def _my_index():
    x, y, c = _my_pos()
    return 4 * x + 2 * y + c


def _device_of(b):
    return (b >> 2) & 1, (b >> 1) & 1, b & 1


def _rs_send(src, parts_ref, b, send_sems, recv_sems, local_sem):
    me = _my_index()
    dst = parts_ref.at[me]

    @pl.when(b == me)
    def _():
        pltpu.make_async_copy(src, dst, local_sem).start()

    @pl.when(b != me)
    def _():
        pltpu.make_async_remote_copy(src_ref=src, dst_ref=dst, send_sem=send_sems.at[b], recv_sem=recv_sems.at[me],
                                     device_id=_device_of(b), device_id_type=MESH).start()


def _rs_finish(src_of, parts_ref, send_sems, recv_sems, local_sem):
    me = _my_index()
    for s in range(NDEV):
        @pl.when(s != me)
        def _():
            cp = pltpu.make_async_remote_copy(src_ref=src_of(s), dst_ref=parts_ref.at[s], send_sem=send_sems.at[s],
                                              recv_sem=recv_sems.at[s], device_id=_device_of(s), device_id_type=MESH)
            cp.wait_send()
            cp.wait_recv()

        @pl.when(s == me)
        def _():
            pltpu.make_async_copy(src_of(s), parts_ref.at[s], local_sem).wait()


_RS_SEMS = [pltpu.SemaphoreType.DMA((NDEV,)), pltpu.SemaphoreType.DMA((NDEV,)), pltpu.SemaphoreType.DMA]
_ANY = pl.BlockSpec(memory_space=pl.ANY)


def _xor_order(me, n):
    return (me ^ (n - 1 - jnp.arange(n, dtype=jnp.int32))).astype(jnp.int32)


NCHIP = NDEV // 2


def _rs2_scratch(half_shape):
    blocks = lambda *lead: pltpu.VMEM(lead + tuple(half_shape), BF16)
    return [blocks(NCHIP, 2), blocks(NCHIP)] + [pltpu.SemaphoreType.DMA((NCHIP,))] * 4 + [pltpu.SemaphoreType.DMA]


def _rs2_to_sibling(q, rs):
    stage, from_sib, d_send, d_recv = rs[:4]
    x, y, c = _my_pos()
    pltpu.make_async_remote_copy(src_ref=stage.at[q, 1 - c], dst_ref=from_sib.at[q], send_sem=d_send.at[q],
                                 recv_sem=d_recv.at[q], device_id=(x, y, 1 - c), device_id_type=MESH).start()


def _rs2_forward(q, parts_ref, rs):
    stage, chip_sum, d_send, d_recv, i_send, i_recv, local_sem = rs
    x, y, c = _my_pos()
    my_chip = 2 * x + y
    pltpu.make_async_remote_copy(src_ref=stage.at[q, c], dst_ref=chip_sum.at[q], send_sem=d_send.at[q],
                                 recv_sem=d_recv.at[q], device_id=(x, y, 1 - c), device_id_type=MESH).wait_recv()
    chip_sum[q] = (stage[q, c].astype(F32) + chip_sum[q].astype(F32)).astype(BF16)

    @pl.when(q == my_chip)
    def _():
        pltpu.make_async_copy(chip_sum.at[q], parts_ref.at[my_chip], local_sem).start()

    @pl.when(q != my_chip)
    def _():
        pltpu.make_async_remote_copy(src_ref=chip_sum.at[q], dst_ref=parts_ref.at[my_chip], send_sem=i_send.at[q],
                                     recv_sem=i_recv.at[my_chip], device_id=((q >> 1) & 1, q & 1, c),
                                     device_id_type=MESH).start()


def _rs2_finish(parts_ref, rs):
    stage, chip_sum, d_send, d_recv, i_send, i_recv, local_sem = rs
    x, y, c = _my_pos()
    my_chip = 2 * x + y
    for q in range(NCHIP):
        pltpu.make_async_remote_copy(src_ref=stage.at[q, 1 - c], dst_ref=chip_sum.at[q], send_sem=d_send.at[q],
                                     recv_sem=d_recv.at[q], device_id=(x, y, 1 - c), device_id_type=MESH).wait_send()

        @pl.when(q != my_chip)
        def _():
            cp = pltpu.make_async_remote_copy(src_ref=chip_sum.at[q], dst_ref=parts_ref.at[q], send_sem=i_send.at[q],
                                              recv_sem=i_recv.at[q], device_id=((q >> 1) & 1, q & 1, c),
                                              device_id_type=MESH)
            cp.wait_send()
            cp.wait_recv()

        @pl.when(q == my_chip)
        def _():
            pltpu.make_async_copy(chip_sum.at[q], parts_ref.at[q], local_sem).wait()


def _gu_wgrad(h2, dgu, order):
    t_len = h2.shape[0]
    tk = min(TK, t_len)
    nk = t_len // tk

    def body(ord_ref, h_ref, d_ref, parts_ref, acc, *rs):
        p, k = pl.program_id(0), pl.program_id(1)

        @pl.when(k == 0)
        def _():
            acc[...] = jnp.zeros_like(acc)

        hb = h_ref[...]
        for half in range(2):
            acc[half] += _dot_tn(d_ref[0, half], hb)

        @pl.when(k == nk - 1)
        def _():
            q = ord_ref[p]
            rs[0][q] = acc[...].astype(BF16)
            _rs2_to_sibling(q, rs)

        @pl.when((k == min(1, nk - 1)) & (p > 0))
        def _():
            _rs2_forward(ord_ref[p - 1], parts_ref, rs)

        @pl.when((p == NCHIP - 1) & (k == nk - 1))
        def _():
            _rs2_forward(ord_ref[p], parts_ref, rs)
            _rs2_finish(parts_ref, rs)

    return pl.pallas_call(
        body, name="gu_wgrad",
        grid_spec=pltpu.PrefetchScalarGridSpec(
            num_scalar_prefetch=1, grid=(NCHIP, nk),
            in_specs=[pl.BlockSpec((tk, D), lambda p, k, o: (k, 0)),
                      pl.BlockSpec((1, 2, tk, FB), lambda p, k, o: (o[p], 0, k, 0))],
            out_specs=_ANY,
            scratch_shapes=[pltpu.VMEM((2, FB, D), F32)] + _rs2_scratch((FB, D))),
        out_shape=jax.ShapeDtypeStruct((NCHIP, FB, D), BF16),
        compiler_params=_cp("arbitrary", "arbitrary"),
    )(order, h2, dgu.reshape(NCHIP, 2, t_len, FB))


def _scaled_wgrad(name, a, dx, w, gate_row, mod, order):
    t_len = dx.shape[0]
    kb = w.shape[0] // NCHIP
    tk = min(TK, t_len)
    nk = t_len // tk
    rows = kb // 2
    if a.ndim == 3:
        a_spec = pl.BlockSpec((None, tk, kb), lambda p, k, o: (o[p], k, 0))
    else:
        a_spec = pl.BlockSpec((tk, kb), lambda p, k, o: (k, o[p]))

    def body(ord_ref, a_ref, dx_ref, w_ref, mod_ref, parts_ref, dg_ref, acc, *rs):
        p, k = pl.program_id(0), pl.program_id(1)

        @pl.when((p == 0) & (k == 0))
        def _():
            dg_ref[...] = jnp.zeros_like(dg_ref)

        @pl.when(k == 0)
        def _():
            acc[...] = jnp.zeros_like(acc)

        acc[...] += _dot_tn(a_ref[...], dx_ref[...])

        @pl.when(k == nk - 1)
        def _():
            q = ord_ref[p]
            z = acc[...]
            zg = (z * mod_ref[gate_row:gate_row + 1, :]).astype(BF16)
            dg_ref[0:1, :] += jnp.sum(z * w_ref[...].astype(F32), axis=0, keepdims=True)
            for half in range(2):
                rs[0][q, half] = zg[half * rows:(half + 1) * rows]
            _rs2_to_sibling(q, rs)

        @pl.when((k == min(1, nk - 1)) & (p > 0))
        def _():
            _rs2_forward(ord_ref[p - 1], parts_ref, rs)

        @pl.when((p == NCHIP - 1) & (k == nk - 1))
        def _():
            _rs2_forward(ord_ref[p], parts_ref, rs)
            _rs2_finish(parts_ref, rs)

    return pl.pallas_call(
        body, name=name,
        grid_spec=pltpu.PrefetchScalarGridSpec(
            num_scalar_prefetch=1, grid=(NCHIP, nk),
            in_specs=[a_spec,
                      pl.BlockSpec((tk, D), lambda p, k, o: (k, 0)),
                      pl.BlockSpec((kb, D), lambda p, k, o: (o[p], 0)),
                      pl.BlockSpec((8, D), lambda p, k, o: (0, 0))],
            out_specs=[_ANY, pl.BlockSpec((8, D), lambda p, k, o: (0, 0))],
            scratch_shapes=[pltpu.VMEM((kb, D), F32)] + _rs2_scratch((rows, D))),
        out_shape=[jax.ShapeDtypeStruct((NCHIP, rows, D), BF16), jax.ShapeDtypeStruct((8, D), F32)],
        compiler_params=_cp("arbitrary", "arbitrary"),
    )(order, a, dx, w, mod)


M_WA, M_WB, M_CBIAS, M_BA, M_BX, M_LS = 0, 3, 7, 8, 9, 10


def _conv_bwd_rows(tt, proj_ref, prm_ref, due, dye, dp_ref, acc8):
    row = lax.broadcasted_iota(jnp.int32, (8, CG), 0)
    w_b = [prm_ref[P_WB + k:P_WB + k + 1, :] for k in range(4)]
    w_a = [prm_ref[P_WA + k:P_WA + k + 1, :] for k in range(3)]

    def blk(ib, carry):
        r0 = pl.multiple_of(ib * 16, 16)
        rows = pl.ds(r0, 16)
        for g in range(D // CG):
            cs = slice(g * CG, (g + 1) * CG)
            du16, du_after = due[rows, cs], due[pl.ds(r0 + 16, 8), cs]
            dy16, dy_after = dye[rows, cs], dye[pl.ds(r0 + 16, 8), cs]
            cc16 = _pj(proj_ref, 1, rows, cs).astype(F32)
            cx16 = _pj(proj_ref, 2, rows, cs).astype(F32)
            x16 = _pj(proj_ref, 3, rows, cs).astype(F32)
            v16 = cc16 * cx16
            acc = [acc8[8 * k:8 * k + 8, cs] for k in range(8)]
            drx, dv = [], []
            for sb in range(2):
                lo = slice(8 * sb, 8 * sb + 8)
                duc, dyc, xc, vc = du16[lo], dy16[lo], x16[lo], v16[lo]
                du_n = du16[8:16] if sb == 0 else du_after
                dy_n = dy16[8:16] if sb == 0 else dy_after
                acc[0] = acc[0] + duc
                acc[4] = acc[4] + duc * xc
                d8 = w_b[3][:, cs] * duc
                for s in (1, 2, 3):
                    du_s = _shift_up(duc, du_n, s, row)
                    acc[4 - s] = acc[4 - s] + du_s * xc
                    d8 = d8 + w_b[3 - s][:, cs] * du_s
                acc[7] = acc[7] + dyc * vc
                e8 = w_a[2][:, cs] * dyc
                for s in (1, 2):
                    dy_s = _shift_up(dyc, dy_n, s, row)
                    acc[7 - s] = acc[7 - s] + dy_s * vc
                    e8 = e8 + w_a[2 - s][:, cs] * dy_s
                drx.append(d8)
                dv.append(e8)
            for k in range(8):
                acc8[8 * k:8 * k + 8, cs] = acc[k]
            dv16 = jnp.concatenate(dv, axis=0)
            col = lambda s: slice(s * D + g * CG, s * D + (g + 1) * CG)
            dp_ref[rows, col(3)] = jnp.concatenate(drx, axis=0).astype(BF16)
            dp_ref[rows, col(1)] = (dv16 * cx16).astype(BF16)
            dp_ref[rows, col(2)] = (dv16 * cc16).astype(BF16)
        return carry

    lax.fori_loop(0, tt // 16, blk, 0)


def _mixer_bwd(proj, hl, sv, dmg, prm, wa, wx):
    t_len = proj.shape[0]
    tt = min(TT, t_len)
    nt = t_len // tt
    hb8 = tt // 8

    def rev(i):
        return nt - 1 - i

    def halo(i):
        return jnp.maximum(rev(i) * hb8 - 1, 0)

    def body(proj_ref, hl_ref, hh_ref, sv_ref, dmg_ref, prm_ref, wa_ref, wx_ref,
             dp_ref, sums_ref, gwa_ref, gwx_ref,
             he, due, dye, drp_s, dip_s, an, gn, acc8):
        i = pl.program_id(0)
        t = rev(i)

        @pl.when(i == 0)
        def _():
            sums_ref[...] = jnp.zeros_like(sums_ref)
            gwa_ref[...] = jnp.zeros_like(gwa_ref)
            gwx_ref[...] = jnp.zeros_like(gwx_ref)
            due[tt:tt + 8, :] = jnp.zeros((8, D), F32)
            dye[tt:tt + 8, :] = jnp.zeros((8, D), F32)
            an[...] = jnp.zeros((8, D), F32)
            gn[...] = jnp.zeros((8, D), F32)

        live = (t > 0).astype(F32)
        he[0:8, :] = hh_ref[...].astype(F32) * live
        he[8:8 + tt, :] = hl_ref[...].astype(F32)

        ls_all = _log_sigmoid(prm_ref[P_LAM:P_LAM + 1, :])
        row = lax.broadcasted_iota(jnp.int32, (8, CG), 0)
        nblk = tt // 16

        def blk(ib, carry):
            r0 = pl.multiple_of((nblk - 1 - ib) * 16, 16)
            rows = pl.ds(r0, 16)
            for g in range(D // CG):
                cs = slice(g * CG, (g + 1) * CG)
                ls = ls_all[:, cs]
                dm = dmg_ref[rows, cs].astype(F32)
                cb = _pj(proj_ref, 0, rows, cs).astype(F32)
                rg = _pj(proj_ref, 4, rows, cs).astype(F32)
                sga = _sig(_pj(proj_ref, 5, rows, cs).astype(F32))
                sgb = _sig(_pj(proj_ref, 6, rows, cs).astype(F32))
                ya0 = sv_ref[SV_YA, rows, cs]
                h16 = he[pl.ds(r0 + 8, 16), cs]
                gl, th = _gelu(rg)
                dgl = 0.5 * (1.0 + th) + 0.5 * rg * (1.0 - th * th) * (_GC * (1.0 + 3.0 * 0.044715 * rg * rg))
                y_a = cb * ya0
                y_b = h16 * gl
                dy_a = dm * sga
                dy_b = dm * sgb
                col = lambda s: slice(s * D + g * CG, s * D + (g + 1) * CG)
                dp_ref[rows, col(5)] = (dm * y_a * sga * (1.0 - sga)).astype(BF16)
                dp_ref[rows, col(6)] = (dm * y_b * sgb * (1.0 - sgb)).astype(BF16)
                dp_ref[rows, col(4)] = (dy_b * h16 * dgl).astype(BF16)
                dp_ref[rows, col(0)] = (dy_a * ya0).astype(BF16)
                dye[rows, cs] = dy_a * cb
                dh16 = dy_b * gl

                a_next = an[:, cs]
                g_next = gn[:, cs]
                s_ba = jnp.zeros((8, CG), F32)
                s_bx = jnp.zeros((8, CG), F32)
                s_ls = jnp.zeros((8, CG), F32)
                for sb in (1, 0):
                    rr = r0 + 8 * sb
                    first = (row + (t * tt + rr)) == 0
                    r8 = pl.ds(rr, 8)
                    uu, r, ig, a, mult = (sv_ref[pln, r8, cs] for pln in (SV_U, SV_R, SV_I, SV_A, SV_MULT))
                    ca = jnp.where(row < 7, pltpu.roll(a, 7, 0), a_next)
                    cb_ = dh16[8 * sb:8 * sb + 8, :]
                    for s in (1, 2, 4):
                        a_sh = jnp.where(row < 8 - s, pltpu.roll(ca, 8 - s, 0), 1.0)
                        b_sh = jnp.where(row < 8 - s, pltpu.roll(cb_, 8 - s, 0), 0.0)
                        cb_ = ca * b_sh + cb_
                        ca = ca * a_sh
                    gv = ca * g_next + cb_
                    g_next = jnp.broadcast_to(gv[0:1, :], gv.shape)
                    a_next = jnp.broadcast_to(a[0:1, :], a.shape)
                    hprev = jnp.where(row >= 1, pltpu.roll(he[pl.ds(rr + 8, 8), cs], 1, 0),
                                      pltpu.roll(he[pl.ds(rr, 8), cs], 1, 0))
                    da = gv * hprev
                    dmult = jnp.where(first, 0.0, gv * ig * uu)
                    dla = da * a + jnp.where(mult > 0.0, dmult * (-(a * a) / mult), 0.0)
                    drp = dla * (LRU_C * ls) * r * (1.0 - r)
                    dip = gv * mult * uu * ig * (1.0 - ig)
                    s_ls = s_ls + dla * (LRU_C * r)
                    s_ba = s_ba + drp
                    s_bx = s_bx + dip
                    drp_s[pl.ds(rr, 8), cs] = drp
                    dip_s[pl.ds(rr, 8), cs] = dip
                    due[pl.ds(rr, 8), cs] = gv * mult * ig
                an[:, cs] = a_next
                gn[:, cs] = g_next
                sums_ref[M_BA:M_BA + 1, cs] += jnp.sum(s_ba, axis=0, keepdims=True)
                sums_ref[M_BX:M_BX + 1, cs] += jnp.sum(s_bx, axis=0, keepdims=True)
                sums_ref[M_LS:M_LS + 1, cs] += jnp.sum(s_ls, axis=0, keepdims=True)
            return carry

        lax.fori_loop(0, nblk, blk, 0)

        drp_b = drp_s[...].astype(BF16)
        dip_b = dip_s[...].astype(BF16)
        ub = sv_ref[SV_U].astype(BF16)
        for h in range(HEADS):
            cs = slice(h * HB, (h + 1) * HB)
            due[0:tt, cs] += _dot_nt(drp_b[:, cs], wa_ref[h]) + _dot_nt(dip_b[:, cs], wx_ref[h])
            gwa_ref[h] += _dot_tn(ub[:, cs], drp_b[:, cs])
            gwx_ref[h] += _dot_tn(ub[:, cs], dip_b[:, cs])

        acc8[...] = jnp.zeros_like(acc8)
        _conv_bwd_rows(tt, proj_ref, prm_ref, due, dye, dp_ref, acc8)
        for k, dst in enumerate([M_CBIAS] + [M_WB + k for k in range(4)] + [M_WA + k for k in range(3)]):
            sums_ref[dst:dst + 1, :] += jnp.sum(acc8[8 * k:8 * k + 8, :], axis=0, keepdims=True)
        due[tt:tt + 8, :] = due[0:8, :]
        dye[tt:tt + 8, :] = dye[0:8, :]

        @pl.when(i == nt - 1)
        def _():
            sums_ref[M_LS:M_LS + 1, :] = sums_ref[M_LS:M_LS + 1, :] * _sig(-prm_ref[P_LAM:P_LAM + 1, :])

    big = lambda: pltpu.VMEM((tt + 8, D), F32)
    tile = lambda: pltpu.VMEM((tt, D), F32)
    return pl.pallas_call(
        body, name="mixer_bwd", grid=(nt,),
        in_specs=[pl.BlockSpec((tt, 7 * D), lambda i: (rev(i), 0)),
                  pl.BlockSpec((tt, D), lambda i: (rev(i), 0)),
                  pl.BlockSpec((8, D), lambda i: (halo(i), 0)),
                  pl.BlockSpec((len(SV_PLANES), tt, D), lambda i: (0, rev(i), 0)),
                  pl.BlockSpec((tt, D), lambda i: (rev(i), 0)),
                  pl.BlockSpec((16, D), lambda i: (0, 0)),
                  pl.BlockSpec((HEADS, HB, HB), lambda i: (0, 0, 0)),
                  pl.BlockSpec((HEADS, HB, HB), lambda i: (0, 0, 0))],
        out_specs=[pl.BlockSpec((tt, 7 * D), lambda i: (rev(i), 0)),
                   pl.BlockSpec((16, D), lambda i: (0, 0)),
                   pl.BlockSpec((HEADS, HB, HB), lambda i: (0, 0, 0)),
                   pl.BlockSpec((HEADS, HB, HB), lambda i: (0, 0, 0))],
        out_shape=[jax.ShapeDtypeStruct((t_len, 7 * D), BF16), jax.ShapeDtypeStruct((16, D), F32),
                   jax.ShapeDtypeStruct((HEADS, HB, HB), F32), jax.ShapeDtypeStruct((HEADS, HB, HB), F32)],
        scratch_shapes=[big(), big(), big(), tile(), tile(),
                        pltpu.VMEM((8, D), F32), pltpu.VMEM((8, D), F32), pltpu.VMEM((64, D), F32)],
        compiler_params=_cp("arbitrary"),
    )(proj, hl, hl, sv, dmg, prm, wa, wx)


def _in_proj_bwd(dproj, w_in, x, dx1, mod, g_mix):
    t_len = x.shape[0]
    tm = min(TM, t_len)

    def body(dp_ref, w_ref, x_ref, dx1_ref, mod_ref, g_ref, gx_ref, sums_ref, acc):
        @pl.when(pl.program_id(0) == 0)
        def _():
            sums_ref[...] = jnp.zeros_like(sums_ref)

        def write(rows, dx):
            gx_ref[rows, :] = dx

        zero = jnp.zeros((8, D), F32)
        sums = (zero, zero, zero)
        for sub in _sub_blocks(tm):
            acc[sub, :] = _dot_nt(dp_ref[sub, :], w_ref[...])
            sums = _norm_bwd_rows(sub, sums, acc, x_ref, dx1_ref, mod_ref[1:2, :], g_ref[...], write)
        _add_norm_sums(sums_ref, sums)

    return pl.pallas_call(
        body, name="in_proj_bwd", grid=(t_len // tm,),
        in_specs=[pl.BlockSpec((tm, 7 * D), lambda i: (i, 0)),
                  _resident((D, 7 * D)),
                  pl.BlockSpec((tm, D), lambda i: (i, 0)), pl.BlockSpec((tm, D), lambda i: (i, 0)),
                  pl.BlockSpec((8, D), lambda i: (0, 0)), pl.BlockSpec((1, D), lambda i: (0, 0))],
        out_specs=[pl.BlockSpec((tm, D), lambda i: (i, 0)), pl.BlockSpec((8, D), lambda i: (0, 0))],
        out_shape=[jax.ShapeDtypeStruct((t_len, D), F32), jax.ShapeDtypeStruct((8, D), F32)],
        scratch_shapes=[pltpu.VMEM((tm, D), F32)],
        compiler_params=_cp("arbitrary"),
    )(dproj, w_in, x, dx1, mod, g_mix)


def _in_wgrad(h, dproj, g_wa, g_wx, order):
    t_len = h.shape[0]
    tk = min(TKI, t_len)
    nk = t_len // tk
    cw = 7 * D // NDEV
    hr = HB // NDEV

    def body(ord_ref, h_ref, d_ref, ga_ref, gx_ref, parts_ref, pa_ref, px_ref, acc, *scr):
        rs, sems = scr[:-6], scr[-6:]
        p, k = pl.program_id(0), pl.program_id(1)

        def head_rows(ref):
            return lambda s: ref.at[:, pl.ds(s * hr, hr), :]

        @pl.when((p == 0) & (k == 0))
        def _():
            for s in range(NDEV):
                _rs_send(head_rows(ga_ref)(s), pa_ref, s, *sems[0:3])
                _rs_send(head_rows(gx_ref)(s), px_ref, s, *sems[3:6])

        @pl.when(k == 0)
        def _():
            acc[...] = jnp.zeros_like(acc)

        acc[...] += _dot_tn(h_ref[...], d_ref[...])

        @pl.when(k == nk - 1)
        def _():
            q = ord_ref[p]
            for half in range(2):
                rs[0][q, half] = acc[:, half * cw:(half + 1) * cw].astype(BF16)
            _rs2_to_sibling(q, rs)

        @pl.when((k == min(1, nk - 1)) & (p > 0))
        def _():
            _rs2_forward(ord_ref[p - 1], parts_ref, rs)

        @pl.when((p == NCHIP - 1) & (k == nk - 1))
        def _():
            _rs2_forward(ord_ref[p], parts_ref, rs)
            _rs2_finish(parts_ref, rs)
            _rs_finish(head_rows(ga_ref), pa_ref, *sems[0:3])
            _rs_finish(head_rows(gx_ref), px_ref, *sems[3:6])

    return pl.pallas_call(
        body, name="in_wgrad",
        grid_spec=pltpu.PrefetchScalarGridSpec(
            num_scalar_prefetch=1, grid=(NCHIP, nk),
            in_specs=[pl.BlockSpec((tk, D), lambda p, k, o: (k, 0)),
                      pl.BlockSpec((tk, 2 * cw), lambda p, k, o: (k, o[p])), _ANY, _ANY],
            out_specs=[_ANY, _ANY, _ANY],
            scratch_shapes=[pltpu.VMEM((D, 2 * cw), F32)] + _rs2_scratch((D, cw)) + _RS_SEMS * 2),
        out_shape=[jax.ShapeDtypeStruct((NCHIP, D, cw), BF16), jax.ShapeDtypeStruct((NDEV, HEADS, hr, HB), F32),
                   jax.ShapeDtypeStruct((NDEV, HEADS, hr, HB), F32)],
        compiler_params=_cp("arbitrary", "arbitrary"),
    )(order, h, dproj, g_wa, g_wx)


def _adam_math(w, g, m, v):
    m = ADAM_B1 * m + (1.0 - ADAM_B1) * g
    v = ADAM_B2 * v + (1.0 - ADAM_B2) * (g * g)
    m_hat = m / (1.0 - ADAM_B1 ** ADAM_STEP)
    v_hat = v / (1.0 - ADAM_B2 ** ADAM_STEP)
    delta = -ADAM_LR * (m_hat / (jnp.sqrt(v_hat) + ADAM_EPS) + ADAM_WD * w)
    return delta, m, v


def _ada_bwd(c_all, dmod_cols, w, m, v):
    rb = 256
    n = w.shape[1]
    nrow = c_all.shape[0]

    def body(c_ref, d_ref, w_ref, m_ref, v_ref, g_ref, dl_ref, nm_ref, nv_ref):
        cv = c_ref[...]
        g = _dot_tn((cv * _sig(cv)).astype(BF16), d_ref[...].astype(BF16))
        g_ref[...] = g
        dl_ref[...], nm_ref[...], nv_ref[...] = _adam_math(w_ref[...], g, m_ref[...], v_ref[...])

    blk = pl.BlockSpec((rb, n), lambda i: (i, 0))
    sds = jax.ShapeDtypeStruct(w.shape, F32)
    return pl.pallas_call(
        body, name="ada_bwd", grid=(D // rb,),
        in_specs=[pl.BlockSpec((nrow, rb), lambda i: (0, i)), pl.BlockSpec((nrow, n), lambda i: (0, 0)), blk, blk, blk],
        out_specs=[blk, blk, blk, blk], out_shape=[sds, sds, sds, sds],
        compiler_params=_cp("parallel"),
    )(c_all, dmod_cols, w, m, v)


def _adam(name, parts, w, m, v):
    p, r, c = parts.shape
    rb = max([cand for cand in range(8, min(r, 256) + 1, 8) if r % cand == 0], default=r)

    def body(p_ref, w_ref, m_ref, v_ref, g_ref, dl_ref, nm_ref, nv_ref):
        g = p_ref[0].astype(F32)
        for q in range(1, p):
            g = g + p_ref[q].astype(F32)
        g_ref[...] = g
        dl_ref[...], nm_ref[...], nv_ref[...] = _adam_math(w_ref[...], g, m_ref[...], v_ref[...])

    blk = pl.BlockSpec((rb, c), lambda i: (i, 0))
    sds = jax.ShapeDtypeStruct((r, c), F32)
    return pl.pallas_call(
        body, name=name, grid=(r // rb,),
        in_specs=[pl.BlockSpec((p, rb, c), lambda i: (0, i, 0)), blk, blk, blk],
        out_specs=[blk, blk, blk, blk], out_shape=[sds, sds, sds, sds],
        compiler_params=_cp("parallel"),
    )(parts, w, m, v)


_SMALL_SEMS = [pltpu.SemaphoreType.DMA((7,)), pltpu.SemaphoreType.DMA((7,)), pltpu.SemaphoreType.DMA]
_VMEM = pl.BlockSpec(memory_space=pltpu.VMEM)


def _exchange_small(x_ref, out_ref, send_sems, recv_sems, local_sem):
    m_per = x_ref.shape[0]
    x, y, c = _my_pos()
    me, sibling = (x, y, c), (x, y, 1 - c)
    chips = [(1 - x, y), (x, 1 - y), (1 - x, 1 - y)]

    def rows(px, py, pc):
        return out_ref.at[pl.ds((4 * px + 2 * py + pc) * m_per, m_per), :]

    def copy(k, block, to, src=None):
        return pltpu.make_async_remote_copy(
            src_ref=rows(*block) if src is None else src, dst_ref=rows(*block),
            send_sem=send_sems.at[k], recv_sem=recv_sems.at[k], device_id=to, device_id_type=MESH)

    mine = pltpu.make_async_copy(x_ref, rows(*me), local_sem)
    mine.start()
    first = [copy(0, me, sibling, src=x_ref)]
    first += [copy(1 + j, me, (*chip, c), src=x_ref) for j, chip in enumerate(chips)]
    for cp in first:
        cp.start()
    passed = [copy(4 + j, (*chip, c), sibling) for j, chip in enumerate(chips)]
    for j, chip in enumerate(chips):
        copy(1 + j, (*chip, c), me).wait_recv()
        passed[j].start()
    copy(0, sibling, me).wait_recv()
    for j, chip in enumerate(chips):
        copy(4 + j, (*chip, 1 - c), me).wait_recv()
    for cp in first + passed:
        cp.wait_send()
    mine.wait()


def _gather_small_grads(sums1, sums2, msums, d_gt1, d_gt2, d_gfin, loss):
    def body(s1, s2, ms, g1, g2, gf, ls, out_ref, pack, *sems):
        rows = [s1[S_SH:S_SH + 1, :], s1[S_SC:S_SC + 1, :], g1[0:1, :],
                s2[S_SH:S_SH + 1, :], s2[S_SC:S_SC + 1, :], g2[0:1, :],
                s1[S_G:S_G + 1, :], ms[M_CBIAS:M_CBIAS + 1, :], ms[M_BA:M_BA + 1, :], ms[M_BX:M_BX + 1, :],
                ms[M_LS:M_LS + 1, :], s2[S_G:S_G + 1, :], gf[...]]
        rows += [ms[M_WA + k:M_WA + k + 1, :] for k in range(3)] + [ms[M_WB + k:M_WB + k + 1, :] for k in range(4)]
        rows += [jnp.broadcast_to(ls[0:1, 0:1], (1, D))]
        for i, v in enumerate(rows):
            pack[i:i + 1, :] = v
        pack[len(rows):24, :] = jnp.zeros((24 - len(rows), D), F32)
        _exchange_small(pack, out_ref, *sems)

    return pl.pallas_call(
        body, name="gather_small", out_shape=jax.ShapeDtypeStruct((NDEV * 24, D), F32),
        in_specs=[_VMEM] * 7, out_specs=_VMEM, scratch_shapes=[pltpu.VMEM((24, D), F32)] + _SMALL_SEMS,
    )(sums1, sums2, msums, d_gt1, d_gt2, d_gfin, loss)


def _ada_mod(pack, w_ada, b_cols):
    ncol = w_ada.shape[1]

    def body(p_ref, w_ref, b_ref, all_ref, mod_ref, cols, *sems):
        _exchange_small(p_ref, all_ref, *sems[0:3])
        c_all = jnp.concatenate([all_ref[8 * d:8 * d + 1, 0:D] for d in range(NDEV)], axis=0)
        c16 = jnp.concatenate([c_all, jnp.zeros_like(c_all)], axis=0)
        mod16 = _dot((c16 * _sig(c16)).astype(BF16), w_ref[...].astype(BF16)) + b_ref[...]
        cols[...] = mod16[0:NDEV]
        _exchange_small(cols, mod_ref, *sems[3:6])

    return pl.pallas_call(
        body, name="ada_mod",
        out_shape=[jax.ShapeDtypeStruct((NDEV * 8, pack.shape[1]), F32), jax.ShapeDtypeStruct((NDEV * 8, ncol), F32)],
        in_specs=[_VMEM, _VMEM, _VMEM], out_specs=[_VMEM, _VMEM],
        scratch_shapes=[pltpu.VMEM((NDEV, ncol), F32)] + _SMALL_SEMS * 2,
        compiler_params=_cp(),
    )(pack, w_ada, b_cols)


def _blk_rows(n):
    return lambda ref, b: ref.at[pl.ds(pl.multiple_of(b * n, 8), n), :]


def _blk_lead(ref, b):
    return ref.at[b]


def _blk_heads(ref, b):
    return ref.at[:, pl.ds(pl.multiple_of(b * (HB // NDEV), 8), HB // NDEV), :]


def _ag_phases(ins, outs, slicers, send_sems, recv_sems, local_sems):
    na = len(ins)
    x, y, c = _my_pos()
    me, sibling = (x, y, c), (x, y, 1 - c)
    chips = [(1 - x, y), (x, 1 - y), (1 - x, 1 - y)]

    def copy(a, k, block, to, from_shard=False):
        px, py, pc = block
        dst = slicers[a](outs[a], 4 * px + 2 * py + pc)
        return pltpu.make_async_remote_copy(
            src_ref=ins[a] if from_shard else dst, dst_ref=dst,
            send_sem=send_sems.at[a * 7 + k], recv_sem=recv_sems.at[a * 7 + k], device_id=to, device_id_type=MESH)

    def local(a):
        return pltpu.make_async_copy(ins[a], slicers[a](outs[a], 4 * x + 2 * y + c), local_sems.at[a])

    def firsts(a):
        return [copy(a, 0, me, sibling, True)] + [copy(a, 1 + j, me, (*chip, c), True) for j, chip in enumerate(chips)]

    def start():
        for a in range(na):
            local(a).start()
            for cp in firsts(a):
                cp.start()

    def forward():
        for a in range(na):
            for j, chip in enumerate(chips):
                copy(a, 1 + j, (*chip, c), me).wait_recv()
                copy(a, 4 + j, (*chip, c), sibling).start()

    def finish():
        for a in range(na):
            copy(a, 0, sibling, me).wait_recv()
            for j, chip in enumerate(chips):
                copy(a, 4 + j, (*chip, 1 - c), me).wait_recv()
        for a in range(na):
            for cp in firsts(a) + [copy(a, 4 + j, (*chip, c), sibling) for j, chip in enumerate(chips)]:
                cp.wait_send()
            local(a).wait()

    return start, forward, finish


def _ag_sems(na):
    return [pltpu.SemaphoreType.DMA((7 * na,)), pltpu.SemaphoreType.DMA((7 * na,)), pltpu.SemaphoreType.DMA((na,))]


def _local_step(x, target, mod, g_mix, g_ffn, g_fin, prm, w_in_shard, shards):
    fulls = [(HEADS, HB, HB), (HEADS, HB, HB), (D, D), (NDEV, FB, D), (DFF, D)]
    slicers = [_blk_heads, _blk_heads, _blk_rows(D // NDEV), _blk_lead, _blk_rows(DFF // NDEV)]
    my_chip = _my_index() >> 1
    own_first = (my_chip ^ jnp.arange(NCHIP, dtype=jnp.int32)).astype(jnp.int32)
    early, late = [0, 1, 2, 4], [3]
    pick = lambda lst, idx: [lst[i] for i in idx]
    proj, h, w_in, (wa, wx, w_out, w_down) = _in_proj(x, mod, g_mix, w_in_shard, own_first, pick(shards, early),
                                                      pick(fulls, early), pick(slicers, early))
    merged, hl, sv, (w_gu,) = _mixer_fwd(proj, prm, wa, wx, pick(shards, late), pick(fulls, late),
                                         pick(slicers, late))
    w_gu = w_gu.reshape(2, 4, FB, D)
    x1, h2 = _out_proj(merged, x, mod, g_ffn, w_out)
    gu, dx2, dx2b, loss, d_gfin = _ffn_fwd(h2, x1, target, mod, g_fin, w_gu, w_down)
    dgu, act, dx1, dx1b, dmg, sums2 = _ffn_bwd(dx2, gu, x1, mod, g_ffn, w_gu, w_down, w_out)
    chip_order = _xor_order(_my_index() >> 1, NCHIP)
    p_wgu = _gu_wgrad(h2, dgu, chip_order)
    p_wdown, d_gt2 = _scaled_wgrad("down_wgrad", act, dx2b, w_down, 5, mod, chip_order)
    p_wout, d_gt1 = _scaled_wgrad("out_wgrad", merged, dx1b, w_out, 2, mod, chip_order)
    dproj, msums, g_wa, g_wx = _mixer_bwd(proj, hl, sv, dmg, prm, wa, wx)
    p_win, p_wa, p_wx = _in_wgrad(h, dproj, g_wa, g_wx, chip_order)
    grad_x, sums1 = _in_proj_bwd(dproj, w_in, x, dx1, mod, g_mix)
    return dict(loss=loss, grad_x=grad_x, d_gfin=d_gfin, sums1=sums1, sums2=sums2, msums=msums,
                d_gt1=d_gt1, d_gt2=d_gt2, p_win=p_win, p_wa=p_wa, p_wx=p_wx, p_wout=p_wout, p_wgu=p_wgu,
                p_wdown=p_wdown)


def kernel(x, c, w_ada, b_ada, g_norm_mix, w_in, conv_a_w, conv_b_w, conv_b_bias, w_rg_a, b_rg_a, w_rg_x, b_rg_x, lru_lambda, w_out, g_norm_ffn, w_gate_up, w_down, g_norm_final, loss_target, m_w_ada, m_b_ada, m_g_norm_mix, m_w_in, m_conv_a_w, m_conv_b_w, m_conv_b_bias, m_w_rg_a, m_b_rg_a, m_w_rg_x, m_b_rg_x, m_lru_lambda, m_w_out, m_g_norm_ffn, m_w_gate_up, m_w_down, m_g_norm_final, v_w_ada, v_b_ada, v_g_norm_mix, v_w_in, v_conv_a_w, v_conv_b_w, v_conv_b_bias, v_w_rg_a, v_b_rg_a, v_w_rg_x, v_b_rg_x, v_lru_lambda, v_w_out, v_g_norm_ffn, v_w_gate_up, v_w_down, v_g_norm_final):
    me = 4 * lax.axis_index("x") + 2 * lax.axis_index("y") + lax.axis_index("c")
    ncol = w_ada.shape[2]
    cw = conv_a_w.shape[2]

    pack0 = jnp.concatenate([c, conv_a_w.reshape(1, 3 * cw), conv_b_w.reshape(1, 4 * cw)], axis=1)
    b_cols = lax.dynamic_slice_in_dim(b_ada, me * ncol, ncol, axis=1)
    got0, got1 = _ada_mod(jnp.broadcast_to(pack0, (8, pack0.shape[1])), w_ada[0], b_cols)
    got0 = got0.reshape(NDEV, 8, -1)[:, 0, :]
    c_all = got0[:, :D]
    conv_a = got0[:, D:D + 3 * cw].reshape(NDEV, 3, cw).transpose(1, 0, 2).reshape(3, D)
    conv_b = got0[:, D + 3 * cw:].reshape(NDEV, 4, cw).transpose(1, 0, 2).reshape(4, D)
    c16 = jnp.concatenate([c_all, jnp.zeros((8, D), F32)], axis=0)
    mod6 = lax.dynamic_index_in_dim(got1.reshape(NDEV, NDEV, ncol), me, axis=1, keepdims=False).reshape(6, D)
    mod = jnp.concatenate([mod6, jnp.zeros((2, D), F32)], axis=0)

    tr = lambda a: jnp.swapaxes(a, 1, 2)
    shards = [w_rg_a[0].astype(BF16), w_rg_x[0].astype(BF16), w_out[0].astype(BF16), tr(w_gate_up)[0].astype(BF16),
              w_down[0].astype(BF16)]

    prm = jnp.concatenate([conv_a, conv_b, conv_b_bias, b_rg_a, b_rg_x, lru_lambda, jnp.zeros((5, D), F32)], axis=0)
    r = _local_step(x[0], loss_target[0], mod, g_norm_mix, g_norm_ffn, g_norm_final.reshape(1, D), prm,
                    w_in[0].astype(BF16), shards)

    parts = [r["p_win"], r["p_wa"], r["p_wx"], r["p_wout"], r["p_wgu"], r["p_wdown"]]
    big = {}
    for nm, p, w, m, v in (("w_in", parts[0], w_in, m_w_in, v_w_in), ("w_rg_a", parts[1], w_rg_a, m_w_rg_a, v_w_rg_a),
                           ("w_rg_x", parts[2], w_rg_x, m_w_rg_x, v_w_rg_x), ("w_out", parts[3], w_out, m_w_out, v_w_out),
                           ("w_gate_up", parts[4], tr(w_gate_up), tr(m_w_gate_up), tr(v_w_gate_up)),
                           ("w_down", parts[5], w_down, m_w_down, v_w_down)):
        two_d = (-1, w.shape[-1])
        outs = _adam("adam_" + nm, p.reshape((p.shape[0],) + w.reshape(two_d).shape), w.reshape(two_d), m.reshape(two_d),
                     v.reshape(two_d))
        big[nm] = [o.reshape(w.shape) for o in outs]
    big["w_gate_up"] = [tr(o) for o in big["w_gate_up"]]

    got2 = _gather_small_grads(r["sums1"], r["sums2"], r["msums"], r["d_gt1"], r["d_gt2"], r["d_gfin"],
                               r["loss"]).reshape(NDEV, 24, D)

    rep_w = jnp.concatenate([b_ada.reshape(6, D), g_norm_mix, conv_b_bias, b_rg_a, b_rg_x, lru_lambda, g_norm_ffn,
                             g_norm_final.reshape(1, D), jnp.zeros((3, D), F32)], axis=0)
    rep_m = jnp.concatenate([m_b_ada.reshape(6, D), m_g_norm_mix, m_conv_b_bias, m_b_rg_a, m_b_rg_x, m_lru_lambda,
                             m_g_norm_ffn, m_g_norm_final.reshape(1, D), jnp.zeros((3, D), F32)], axis=0)
    rep_v = jnp.concatenate([v_b_ada.reshape(6, D), v_g_norm_mix, v_conv_b_bias, v_b_rg_a, v_b_rg_x, v_lru_lambda,
                             v_g_norm_ffn, v_g_norm_final.reshape(1, D), jnp.ones((3, D), F32)], axis=0)
    rep = _adam("adam_rep", got2[:, :16, :], rep_w, rep_m, rep_v)

    conv_parts = lax.dynamic_slice_in_dim(got2[:, 13:21, :], me * cw, cw, axis=2)
    cv_w = jnp.concatenate([conv_a_w[0], conv_b_w[0], jnp.zeros((1, cw), F32)], axis=0)
    cv_m = jnp.concatenate([m_conv_a_w[0], m_conv_b_w[0], jnp.zeros((1, cw), F32)], axis=0)
    cv_v = jnp.concatenate([v_conv_a_w[0], v_conv_b_w[0], jnp.ones((1, cw), F32)], axis=0)
    cvo = _adam("adam_conv", conv_parts, cv_w, cv_m, cv_v)

    dmod_cols = lax.dynamic_slice_in_dim(got2[:, :6, :].reshape(NDEV, 6 * D), me * ncol, ncol, axis=1)
    dmod16 = jnp.concatenate([dmod_cols, jnp.zeros((8, ncol), F32)], axis=0)
    ada = _ada_bwd(c16, dmod16, w_ada[0], m_w_ada[0], v_w_ada[0])

    loss = jnp.sum(got2[:, 20, 0])

    def pick(q):
        one = lambda i: rep[q][i:i + 1]
        return [ada[q].reshape(w_ada.shape), rep[q][0:6].reshape(b_ada.shape), one(6), big["w_in"][q],
                cvo[q][0:3].reshape(conv_a_w.shape), cvo[q][3:7].reshape(conv_b_w.shape), one(7),
                big["w_rg_a"][q], one(8), big["w_rg_x"][q], one(9), one(10), big["w_out"][q], one(11),
                big["w_gate_up"][q], big["w_down"][q], rep[q][12]]

    return (loss, r["grad_x"].reshape(x.shape), *pick(0), *pick(1), *pick(2), *pick(3))
```

```python
import math

import jax
import jax.numpy as jnp
from jax import lax
from jax.experimental import pallas as pl
from jax.experimental.pallas import tpu as pltpu

F32 = jnp.float32
BF16 = jnp.bfloat16

D = 1024
DFF = 2816
NDEV = 8
HEADS = 4
HB = D // HEADS
FB = DFF // 4
EPS = 1e-6
LRU_C = 8.0
ADAM_LR, ADAM_B1, ADAM_B2, ADAM_EPS, ADAM_WD, ADAM_STEP = 0.001, 0.9, 0.999, 1e-08, 0.01, 10

VMEM_LIMIT = 56 * 1024 * 1024
TM = 512
TMI = 1024
TMF = 256
TK = 2048
TKI = 2048
SUB = 256
TT = 256
CG = 256
MESH = pl.DeviceIdType.MESH


def _cp(*sem):
    return pltpu.CompilerParams(dimension_semantics=sem, vmem_limit_bytes=VMEM_LIMIT)


def _sig(x):
    return 1.0 / (1.0 + jnp.exp(-x))


def _log_sigmoid(x):
    z = jnp.exp(-jnp.abs(x))
    u = 1.0 + z
    d = u - 1.0
    l1p = jnp.where(d == 0.0, z, jnp.log(u) * (z / jnp.where(d == 0.0, 1.0, d)))
    return -(jnp.maximum(-x, 0.0) + l1p)


def _neg_expm1(x):
    p = x * (1.0 + x * 0.5 * (1.0 + x * (1.0 / 3.0) * (1.0 + x * 0.25 * (1.0 + x * 0.2 * (1.0 + x * (1.0 / 6.0))))))
    return jnp.where(x > -0.25, -p, 1.0 - jnp.exp(x))


_GC = math.sqrt(2.0 / math.pi)


def _gelu(x):
    t = jnp.tanh(_GC * (x + 0.044715 * x * x * x))
    return 0.5 * x * (1.0 + t), t


def _dot(a, b):
    return jnp.dot(a, b, preferred_element_type=F32)


def _dot_nt(a, b):
    return lax.dot_general(a, b, (((1,), (1,)), ((), ())), preferred_element_type=F32)


def _dot_tn(a, b):
    return lax.dot_general(a, b, (((0,), (0,)), ((), ())), preferred_element_type=F32)


def _resident(shape):
    return pl.BlockSpec(shape, lambda *_: (0,) * len(shape), pipeline_mode=pl.Buffered(1))


def _sub_blocks(n_rows):
    step = min(SUB, n_rows)
    return [slice(r, r + step) for r in range(0, n_rows, step)]


def _fold8(v):
    return v[0:8] + v[8:16]


def _pj(ref, s, rows=slice(None), cols=slice(0, D)):
    return ref[rows, s * D + cols.start:s * D + cols.stop]


def _in_proj(x, mod, g_mix, w_shard, order, shards, fulls, slicers):
    t_len = x.shape[0]
    tm = min(TMI, t_len)
    ni = t_len // tm
    na = len(shards)
    cw = 7 * D // NDEV
    rc = 32

    def body(ord_ref, x_hbm, mod_ref, g_ref, wsh_ref, *rest):
        ins, (proj_ref, h_ref, wfull_ref), outs = rest[:na], rest[na:na + 3], rest[na + 3:2 * na + 3]
        h_scr, w_scr, wsend, wrecv, wlocal, wout, ring, ring_sem = rest[2 * na + 3:2 * na + 11]
        start, forward, finish = _ag_phases(ins, outs, slicers, *rest[2 * na + 11:])
        p, i = pl.program_id(0), pl.program_id(1)
        step, nstep = p * ni + i, NCHIP * ni

        def fetch(s):
            row0 = (s % ni) * tm if isinstance(s, int) else pl.multiple_of((s % ni) * tm, tm)
            return pltpu.make_async_copy(x_hbm.at[pl.ds(row0, tm), :], ring.at[s % 3], ring_sem.at[s % 3])

        @pl.when(step == 0)
        def _():
            for s in range(min(2, nstep)):
                fetch(s).start()

        @pl.when(step + 2 < nstep)
        def _():
            fetch(step + 2).start()

        fetch(step).wait()
        x_ref = ring.at[step % 3]
        x_, y_, c = _my_pos()
        me, sibling = (x_, y_, c), (x_, y_, 1 - c)
        chip_at = [None, (x_, 1 - y_), (1 - x_, y_), (1 - x_, 1 - y_)]

        def cols(px, py, pc):
            return w_scr.at[:, pl.ds(pl.multiple_of((4 * px + 2 * py + pc) * cw, 128), cw)]

        def wcopy(k, block, to, from_shard=False):
            dst = cols(*block)
            return pltpu.make_async_remote_copy(src_ref=wsh_ref if from_shard else dst, dst_ref=dst,
                                                send_sem=wsend.at[k], recv_sem=wrecv.at[k], device_id=to,
                                                device_id_type=MESH)

        own_local = pltpu.make_async_copy(wsh_ref, cols(*me), wlocal)
        to_hbm = pltpu.make_async_copy(w_scr, wfull_ref, wout)

        @pl.when((p == 0) & (i == 0))
        def _():
            own_local.start()
            wcopy(0, me, sibling, True).start()
            for q in (1, 2):
                wcopy(q, me, (*chip_at[q], c), True).start()
            own_local.wait()
            wcopy(0, sibling, me).wait_recv()

        @pl.when((p == 0) & (i == ni // 2))
        def _():
            wcopy(3, me, (*chip_at[3], c), True).start()

        for q in (1, 2, 3):
            @pl.when((p == q - 1) & (i == ni - 1))
            def _():
                wcopy(q, (*chip_at[q], c), me).wait_recv()
                wcopy(3 + q, (*chip_at[q], c), sibling).start()

            @pl.when((p == q) & (i == 0))
            def _():
                wcopy(3 + q, (*chip_at[q], 1 - c), me).wait_recv()

        @pl.when((p == 1) & (i == 0))
        def _():
            start()

        @pl.when((p == NCHIP - 1) & (i == ni // 2))
        def _():
            forward()

        @pl.when((p == NCHIP - 1) & (i == 0))
        def _():
            to_hbm.start()

        gs = g_ref[...] * (1.0 + mod_ref[1:2, :])
        sh = mod_ref[0:1, :]

        wcols = pl.ds(pl.multiple_of(ord_ref[p] * (2 * cw), 128), 2 * cw)
        for sub in _sub_blocks(tm):
            for r0 in range(sub.start, sub.stop, rc):
                xv = x_ref[r0:r0 + rc, :]
                r = lax.rsqrt(jnp.mean(xv * xv, axis=-1, keepdims=True) + EPS)
                h_scr[r0:r0 + rc, :] = (xv * r * gs + sh).astype(BF16)
            proj_ref[sub, :] = _dot(h_scr[sub, :], w_scr[:, wcols]).astype(BF16)

        @pl.when(p == 0)
        def _():
            h_ref[...] = h_scr[...]

        @pl.when((p == NCHIP - 1) & (i == ni - 1))
        def _():
            wcopy(0, me, sibling, True).wait_send()
            for q in (1, 2, 3):
                wcopy(q, me, (*chip_at[q], c), True).wait_send()
                wcopy(3 + q, (*chip_at[q], c), sibling).wait_send()
            finish()
            to_hbm.wait()

    res = pl.pallas_call(
        body, name="in_proj",
        grid_spec=pltpu.PrefetchScalarGridSpec(
            num_scalar_prefetch=1, grid=(NCHIP, ni),
            in_specs=[_ANY,
                      pl.BlockSpec((8, D), lambda p, i, o: (0, 0)),
                      pl.BlockSpec((1, D), lambda p, i, o: (0, 0))] + [_ANY] * (1 + na),
            out_specs=[pl.BlockSpec((tm, 2 * cw), lambda p, i, o: (i, o[p])),
                       pl.BlockSpec((tm, D), lambda p, i, o: (jnp.where(p == 0, i, ni - 1), 0))]
            + [_ANY] * (1 + na),
            scratch_shapes=[pltpu.VMEM((tm, D), BF16), pltpu.VMEM((D, 7 * D), BF16),
                            pltpu.SemaphoreType.DMA((7,)), pltpu.SemaphoreType.DMA((7,)),
                            pltpu.SemaphoreType.DMA, pltpu.SemaphoreType.DMA,
                            pltpu.VMEM((3, tm, D), F32), pltpu.SemaphoreType.DMA((3,))] + _ag_sems(na)),
        out_shape=[jax.ShapeDtypeStruct((t_len, 7 * D), BF16), jax.ShapeDtypeStruct((t_len, D), BF16),
                   jax.ShapeDtypeStruct((D, 7 * D), BF16)]
        + [jax.ShapeDtypeStruct(f, sh.dtype) for f, sh in zip(fulls, shards)],
        compiler_params=_cp("arbitrary", "arbitrary"),
    )(order, x, mod, g_mix, w_shard, *shards)
    return res[0], res[1], res[2], res[3:]


P_WA, P_WB, P_CBIAS, P_BA, P_BX, P_LAM = 0, 3, 7, 8, 9, 10
SV_PLANES = SV_U, SV_YA, SV_R, SV_I, SV_A, SV_MULT = range(6)


def _lru_gates(rp, ip, ls, first_row):
    r = _sig(rp)
    ig = _sig(ip)
    la = LRU_C * r * ls
    a = jnp.exp(la)
    m2 = _neg_expm1(2.0 * la)
    mult = jnp.where(first_row, 1.0, jnp.sqrt(jnp.maximum(m2, 0.0)))
    return r, ig, la, a, m2, mult


def _shift_down(cur, prev, s, row):
    return jnp.where(row >= s, pltpu.roll(cur, s, 0), pltpu.roll(prev, s, 0))


def _shift_up(cur, nxt, s, row):
    return jnp.where(row < 8 - s, pltpu.roll(cur, 8 - s, 0), pltpu.roll(nxt, 8 - s, 0))


def _conv_fwd_rows(tt, proj_ref, prm_ref, xe, ve, u_s, ub_s, ya_s):
    row = lax.broadcasted_iota(jnp.int32, (8, CG), 0)
    w_b = [prm_ref[P_WB + k:P_WB + k + 1, :] for k in range(4)]
    w_a = [prm_ref[P_WA + k:P_WA + k + 1, :] for k in range(3)]
    bias = prm_ref[P_CBIAS:P_CBIAS + 1, :]

    def blk(ib, carry):
        r0 = pl.multiple_of(ib * 16, 16)
        rows = pl.ds(r0, 16)
        for g in range(D // CG):
            cs = slice(g * CG, (g + 1) * CG)
            x16 = _pj(proj_ref, 3, rows, cs).astype(F32)
            v16 = _pj(proj_ref, 1, rows, cs).astype(F32) * _pj(proj_ref, 2, rows, cs).astype(F32)
            xp = xe[pl.ds(r0, 8), cs]
            vp = ve[pl.ds(r0, 8), cs]
            xe[pl.ds(r0 + 8, 16), cs] = x16
            ve[pl.ds(r0 + 8, 16), cs] = v16
            us, yas = [], []
            for sb in range(2):
                xc, vc = x16[8 * sb:8 * sb + 8], v16[8 * sb:8 * sb + 8]
                u8 = bias[:, cs] + w_b[3][:, cs] * xc
                for s in (1, 2, 3):
                    u8 = u8 + w_b[3 - s][:, cs] * _shift_down(xc, xp, s, row)
                y8 = w_a[2][:, cs] * vc
                for s in (1, 2):
                    y8 = y8 + w_a[2 - s][:, cs] * _shift_down(vc, vp, s, row)
                us.append(u8)
                yas.append(y8)
                xp, vp = xc, vc
            u16 = jnp.concatenate(us, axis=0)
            u_s[rows, cs] = u16
            ub_s[rows, cs] = u16.astype(BF16)
            ya_s[rows, cs] = jnp.concatenate(yas, axis=0)
        return carry

    lax.fori_loop(0, tt // 16, blk, 0)


def _mixer_fwd(proj, prm, wa, wx, shards, fulls, slicers):
    t_len = proj.shape[0]
    tt = min(TT, t_len)
    nt = t_len // tt
    na = len(shards)

    def body(proj_hbm, prm_ref, wa_ref, wx_ref, *rest):
        ins, (mg_ref, hl_ref, sv_ref), outs = rest[:na], rest[na:na + 3], rest[na + 3:2 * na + 3]
        xe, ve, hc, rp_s, ip_s, ub_s, ring, ring_sem = rest[2 * na + 3:2 * na + 11]
        start, forward, finish = _ag_phases(ins, outs, slicers, *rest[2 * na + 11:])
        t = pl.program_id(0)

        def fetch(s):
            row0 = s * tt if isinstance(s, int) else pl.multiple_of(s * tt, tt)
            return pltpu.make_async_copy(proj_hbm.at[pl.ds(row0, tt), :], ring.at[s % 3], ring_sem.at[s % 3])

        @pl.when(t == 0)
        def _():
            for s in range(min(2, nt)):
                fetch(s).start()

        @pl.when(t + 2 < nt)
        def _():
            fetch(t + 2).start()

        fetch(t).wait()
        proj_ref = ring.at[t % 3]

        @pl.when(t == 0)
        def _():
            start()
            xe[0:8, :] = jnp.zeros((8, D), F32)
            ve[0:8, :] = jnp.zeros((8, D), F32)
            hc[...] = jnp.zeros((8, D), F32)

        @pl.when(t == (3 * nt) // 4)
        def _():
            forward()

        _conv_fwd_rows(tt, proj_ref, prm_ref, xe, ve, sv_ref.at[SV_U], ub_s, sv_ref.at[SV_YA])
        xe[0:8, :] = xe[tt:tt + 8, :]
        ve[0:8, :] = ve[tt:tt + 8, :]

        ub = ub_s[...]
        for h in range(HEADS):
            cs = slice(h * HB, (h + 1) * HB)
            rp_s[:, cs] = _dot(ub[:, cs], wa_ref[h]) + prm_ref[P_BA:P_BA + 1, cs]
            ip_s[:, cs] = _dot(ub[:, cs], wx_ref[h]) + prm_ref[P_BX:P_BX + 1, cs]

        ls_all = _log_sigmoid(prm_ref[P_LAM:P_LAM + 1, :])
        row = lax.broadcasted_iota(jnp.int32, (8, CG), 0)

        def blk(i, carry):
            r0 = pl.multiple_of(i * 16, 16)
            for g in range(D // CG):
                cs = slice(g * CG, (g + 1) * CG)
                ls = ls_all[:, cs]
                hprev = hc[:, cs]
                hs = []
                for sb in range(2):
                    rr = r0 + 8 * sb
                    first = (row + (t * tt + rr)) == 0
                    r8 = pl.ds(rr, 8)
                    r, ig, _, a, _, mult = _lru_gates(rp_s[r8, cs], ip_s[r8, cs], ls, first)
                    for plane, val in ((SV_R, r), (SV_I, ig), (SV_A, a), (SV_MULT, mult)):
                        sv_ref[plane, r8, cs] = val
                    b = mult * (ig * sv_ref[SV_U, r8, cs])
                    for s in (1, 2, 4):
                        a_sh = jnp.where(row >= s, pltpu.roll(a, s, 0), 1.0)
                        b_sh = jnp.where(row >= s, pltpu.roll(b, s, 0), 0.0)
                        b = a * b_sh + b
                        a = a * a_sh
                    hv = a * hprev + b
                    hprev = jnp.broadcast_to(hv[7:8, :], hv.shape)
                    hs.append(hv)
                hc[:, cs] = hprev
                h16 = jnp.concatenate(hs, axis=0)
                rows = pl.ds(r0, 16)
                gl, _ = _gelu(_pj(proj_ref, 4, rows, cs).astype(F32))
                y_b = h16 * gl
                y_a = _pj(proj_ref, 0, rows, cs).astype(F32) * sv_ref[SV_YA, rows, cs]
                mg = (_sig(_pj(proj_ref, 5, rows, cs).astype(F32)) * y_a
                      + _sig(_pj(proj_ref, 6, rows, cs).astype(F32)) * y_b)
                mg_ref[rows, cs] = mg.astype(BF16)
                hl_ref[rows, cs] = h16.astype(BF16)
            return carry

        lax.fori_loop(0, tt // 16, blk, 0)

        @pl.when(t == nt - 1)
        def _():
            finish()

    res = pl.pallas_call(
        body, name="mixer_fwd", grid=(nt,),
        in_specs=[_ANY,
                  pl.BlockSpec((16, D), lambda t: (0, 0)),
                  pl.BlockSpec((HEADS, HB, HB), lambda t: (0, 0, 0)),
                  pl.BlockSpec((HEADS, HB, HB), lambda t: (0, 0, 0))] + [_ANY] * na,
        out_specs=[pl.BlockSpec((tt, D), lambda t: (t, 0)), pl.BlockSpec((tt, D), lambda t: (t, 0)),
                   pl.BlockSpec((len(SV_PLANES), tt, D), lambda t: (0, t, 0))] + [_ANY] * na,
        out_shape=[jax.ShapeDtypeStruct((t_len, D), BF16), jax.ShapeDtypeStruct((t_len, D), BF16),
                   jax.ShapeDtypeStruct((len(SV_PLANES), t_len, D), F32)]
        + [jax.ShapeDtypeStruct(f, sh.dtype) for f, sh in zip(fulls, shards)],
        scratch_shapes=[pltpu.VMEM((tt + 8, D), F32), pltpu.VMEM((tt + 8, D), F32), pltpu.VMEM((8, D), F32),
                        pltpu.VMEM((tt, D), F32), pltpu.VMEM((tt, D), F32), pltpu.VMEM((tt, D), BF16),
                        pltpu.VMEM((3, tt, 7 * D), BF16), pltpu.SemaphoreType.DMA((3,))]
        + _ag_sems(na),
        compiler_params=_cp("arbitrary"),
    )(proj, prm, wa, wx, *shards)
    return res[0], res[1], res[2], res[3:]


def _out_proj(merged, x, mod, g_ffn, w_out):
    t_len = x.shape[0]
    tm = min(TMI, t_len)

    def body(mg_ref, x_ref, mod_ref, g_ref, w_ref, x1_ref, h2_ref):
        gt1 = mod_ref[2:3, :]
        gs = g_ref[...] * (1.0 + mod_ref[4:5, :])
        sh = mod_ref[3:4, :]
        for sub in _sub_blocks(tm):
            x1_ref[sub, :] = x_ref[sub, :] + gt1 * _dot(mg_ref[sub, :], w_ref[...])
            for r0 in range(sub.start, sub.stop, 16):
                x1 = x1_ref[r0:r0 + 16, :]
                r = lax.rsqrt(jnp.mean(x1 * x1, axis=-1, keepdims=True) + EPS)
                h2_ref[r0:r0 + 16, :] = (x1 * r * gs + sh).astype(BF16)

    return pl.pallas_call(
        body, name="out_proj", grid=(t_len // tm,),
        in_specs=[pl.BlockSpec((tm, D), lambda i: (i, 0)), pl.BlockSpec((tm, D), lambda i: (i, 0)),
                  pl.BlockSpec((8, D), lambda i: (0, 0)), pl.BlockSpec((1, D), lambda i: (0, 0)),
                  pl.BlockSpec((D, D), lambda i: (0, 0))],
        out_specs=[pl.BlockSpec((tm, D), lambda i: (i, 0)), pl.BlockSpec((tm, D), lambda i: (i, 0))],
        out_shape=[jax.ShapeDtypeStruct((t_len, D), F32), jax.ShapeDtypeStruct((t_len, D), BF16)],
        compiler_params=_cp("parallel"),
    )(merged, x, mod, g_ffn, w_out)


def _ffn_fwd(h2, x1, target, mod, g_fin, w_gu, w_down):
    t_len = x1.shape[0]
    tm = min(TMF, t_len)

    def body(h2_ref, x1_ref, tg_ref, mod_ref, g_ref, wgu_ref, wd_ref, gu_ref, dx2_ref, dx2b_ref, loss_ref, dg_ref, acc):
        @pl.when(pl.program_id(0) == 0)
        def _():
            loss_ref[...] = jnp.zeros_like(loss_ref)
            dg_ref[...] = jnp.zeros_like(dg_ref)

        hb = h2_ref[...]
        ffn = None
        nxt = (_dot_nt(hb, wgu_ref[0, 0]), _dot_nt(hb, wgu_ref[1, 0]))
        for j in range(4):
            gate, up = nxt
            if j < 3:
                nxt = (_dot_nt(hb, wgu_ref[0, j + 1]), _dot_nt(hb, wgu_ref[1, j + 1]))
            gu_ref[0, j] = gate.astype(BF16)
            gu_ref[1, j] = up.astype(BF16)
            act = (gate * _sig(gate) * up).astype(BF16)
            part = _dot(act, wd_ref[j * FB:(j + 1) * FB, :])
            ffn = part if ffn is None else ffn + part
        acc[...] = ffn

        gt2 = mod_ref[5:6, :]
        gf = g_ref[...]

        s_loss = s_dg = jnp.zeros((8, D), F32)
        for r0 in range(0, tm, 16):
            rows = slice(r0, r0 + 16)
            x2 = x1_ref[rows, :] + gt2 * acc[rows, :]
            r = lax.rsqrt(jnp.mean(x2 * x2, axis=-1, keepdims=True) + EPS)
            xn = x2 * r
            diff = xn * gf - tg_ref[rows, :]
            dy = diff * (1.0 / D)
            dxn = dy * gf
            dx2 = r * (dxn - xn * jnp.mean(dxn * xn, axis=-1, keepdims=True))
            dx2_ref[rows, :] = dx2
            dx2b_ref[rows, :] = dx2.astype(BF16)
            s_loss, s_dg = s_loss + _fold8(diff * diff), s_dg + _fold8(dy * xn)
        loss_ref[...] += jnp.sum(s_loss) * (0.5 / D)
        dg_ref[...] += jnp.sum(s_dg, axis=0, keepdims=True)

    row = pl.BlockSpec((tm, D), lambda i: (i, 0))
    return pl.pallas_call(
        body, name="ffn_fwd", grid=(t_len // tm,),
        in_specs=[row, row, row, pl.BlockSpec((8, D), lambda i: (0, 0)), pl.BlockSpec((1, D), lambda i: (0, 0)),
                  _resident((2, 4, FB, D)), _resident((DFF, D))],
        out_specs=[pl.BlockSpec((2, 4, tm, FB), lambda i: (0, 0, i, 0)), row, row,
                   pl.BlockSpec((1, 128), lambda i: (0, 0)), pl.BlockSpec((1, D), lambda i: (0, 0))],
        out_shape=[jax.ShapeDtypeStruct((2, 4, t_len, FB), BF16), jax.ShapeDtypeStruct((t_len, D), F32),
                   jax.ShapeDtypeStruct((t_len, D), BF16),
                   jax.ShapeDtypeStruct((1, 128), F32), jax.ShapeDtypeStruct((1, D), F32)],
        scratch_shapes=[pltpu.VMEM((tm, D), F32)],
        compiler_params=_cp("arbitrary"),
    )(h2, x1, target, mod, g_fin, w_gu, w_down)


S_SH, S_SC, S_G = 0, 1, 2


def _norm_bwd_rows(span, sums, dh_ref, x_ref, dres_ref, scale, gain, write):
    gs = 1.0 + scale
    s_sh, s_sc, s_g = sums
    for r0 in range(span.start, span.stop, 16):
        rows = slice(r0, r0 + 16)
        dh = dh_ref[rows, :]
        xv = x_ref[rows, :]
        r = lax.rsqrt(jnp.mean(xv * xv, axis=-1, keepdims=True) + EPS)
        xn = xv * r
        dhn = dh * gs
        dxn = dhn * gain
        write(rows, dres_ref[rows, :] + r * (dxn - xn * jnp.mean(dxn * xn, axis=-1, keepdims=True)))
        s_sh, s_sc, s_g = s_sh + _fold8(dh), s_sc + _fold8(dh * (xn * gain)), s_g + _fold8(dhn * xn)
    return s_sh, s_sc, s_g


def _add_norm_sums(sums_ref, sums):
    for dst, s in zip((S_SH, S_SC, S_G), sums):
        sums_ref[dst:dst + 1, :] += jnp.sum(s, axis=0, keepdims=True)


def _ffn_bwd(dx2, gu, x1, mod, g_ffn, w_gu, w_down, w_out):
    t_len = x1.shape[0]
    tm = min(TMF, t_len)

    def body(dx2_ref, gu_ref, x1_ref, mod_ref, g_ref, wgu_ref, wd_ref, wo_ref,
             dgu_ref, act_ref, dx1_ref, dx1b_ref, dmg_ref, sums_ref, acc, dmo, dact_s):
        @pl.when(pl.program_id(0) == 0)
        def _():
            sums_ref[...] = jnp.zeros_like(sums_ref)

        dffn = (dx2_ref[...] * mod_ref[5:6, :]).astype(BF16)
        dact_s[0] = _dot_nt(dffn, wd_ref[0:FB, :])
        for j in range(4):
            if j < 3:
                dact_s[(j + 1) % 2] = _dot_nt(dffn, wd_ref[(j + 1) * FB:(j + 2) * FB, :])
            for r0 in range(0, tm, 16):
                rows = slice(r0, r0 + 16)
                dact = dact_s[j % 2, rows, :]
                gate = gu_ref[0, j, rows, :].astype(F32)
                up = gu_ref[1, j, rows, :].astype(F32)
                sg = _sig(gate)
                silu = gate * sg
                act_ref[j, rows, :] = (silu * up).astype(BF16)
                dgu_ref[0, j, rows, :] = (dact * up * (sg * (1.0 + gate * (1.0 - sg)))).astype(BF16)
                dgu_ref[1, j, rows, :] = (dact * silu).astype(BF16)
            part = _dot(dgu_ref[0, j], wgu_ref[0, j]) + _dot(dgu_ref[1, j], wgu_ref[1, j])
            if j == 0:
                acc[...] = part
            else:
                acc[...] += part

        gt1 = mod_ref[2:3, :]

        def write(rows, dx1):
            dx1_ref[rows, :] = dx1
            dx1b_ref[rows, :] = dx1.astype(BF16)
            dmo[rows, :] = (dx1 * gt1).astype(BF16)

        zero = jnp.zeros((8, D), F32)
        sums = (zero, zero, zero)
        for sub in (slice(0, tm // 2), slice(tm // 2, tm)):
            sums = _norm_bwd_rows(sub, sums, acc, x1_ref, dx2_ref, mod_ref[4:5, :], g_ref[...], write)
            dmg_ref[sub, :] = _dot_nt(dmo[sub, :], wo_ref[...]).astype(BF16)
        _add_norm_sums(sums_ref, sums)

    row = pl.BlockSpec((tm, D), lambda i: (i, 0))
    return pl.pallas_call(
        body, name="ffn_bwd", grid=(t_len // tm,),
        in_specs=[row, pl.BlockSpec((2, 4, tm, FB), lambda i: (0, 0, i, 0)), row,
                  pl.BlockSpec((8, D), lambda i: (0, 0)), pl.BlockSpec((1, D), lambda i: (0, 0)),
                  _resident((2, 4, FB, D)), _resident((DFF, D)), _resident((D, D))],
        out_specs=[pl.BlockSpec((2, 4, tm, FB), lambda i: (0, 0, i, 0)),
                   pl.BlockSpec((4, tm, FB), lambda i: (0, i, 0)), row, row, row,
                   pl.BlockSpec((8, D), lambda i: (0, 0))],
        out_shape=[jax.ShapeDtypeStruct((2, 4, t_len, FB), BF16), jax.ShapeDtypeStruct((4, t_len, FB), BF16),
                   jax.ShapeDtypeStruct((t_len, D), F32), jax.ShapeDtypeStruct((t_len, D), BF16),
                   jax.ShapeDtypeStruct((t_len, D), BF16), jax.ShapeDtypeStruct((8, D), F32)],
        scratch_shapes=[pltpu.VMEM((tm, D), F32), pltpu.VMEM((tm, D), BF16), pltpu.VMEM((2, tm, FB), F32)],
        compiler_params=_cp("arbitrary"),
    )(dx2, gu, x1, mod, g_ffn, w_gu, w_down, w_out)


def _my_pos():
    return lax.axis_index("x"), lax.axis_index("y"), lax.axis_index("c")


def _my_index():
    x, y, c = _my_pos()
    return 4 * x + 2 * y + c


def _device_of(b):
    return (b >> 2) & 1, (b >> 1) & 1, b & 1


def _rs_send(src, parts_ref, b, send_sems, recv_sems, local_sem):
    me = _my_index()
    dst = parts_ref.at[me]

    @pl.when(b == me)
    def _():
        pltpu.make_async_copy(src, dst, local_sem).start()

    @pl.when(b != me)
    def _():
        pltpu.make_async_remote_copy(src_ref=src, dst_ref=dst, send_sem=send_sems.at[b], recv_sem=recv_sems.at[me],
                                     device_id=_device_of(b), device_id_type=MESH).start()


def _rs_finish(src_of, parts_ref, send_sems, recv_sems, local_sem):
    me = _my_index()
    for s in range(NDEV):
        @pl.when(s != me)
        def _():
            cp = pltpu.make_async_remote_copy(src_ref=src_of(s), dst_ref=parts_ref.at[s], send_sem=send_sems.at[s],
                                              recv_sem=recv_sems.at[s], device_id=_device_of(s), device_id_type=MESH)
            cp.wait_send()
            cp.wait_recv()

        @pl.when(s == me)
        def _():
            pltpu.make_async_copy(src_of(s), parts_ref.at[s], local_sem).wait()


_RS_SEMS = [pltpu.SemaphoreType.DMA((NDEV,)), pltpu.SemaphoreType.DMA((NDEV,)), pltpu.SemaphoreType.DMA]
_ANY = pl.BlockSpec(memory_space=pl.ANY)


def _xor_order(me, n):
    return (me ^ (n - 1 - jnp.arange(n, dtype=jnp.int32))).astype(jnp.int32)


NCHIP = NDEV // 2


def _rs2_scratch(half_shape):
    blocks = lambda *lead: pltpu.VMEM(lead + tuple(half_shape), BF16)
    return [blocks(NCHIP, 2), blocks(NCHIP)] + [pltpu.SemaphoreType.DMA((NCHIP,))] * 4 + [pltpu.SemaphoreType.DMA]


def _rs2_to_sibling(q, rs):
    stage, from_sib, d_send, d_recv = rs[:4]
    x, y, c = _my_pos()
    pltpu.make_async_remote_copy(src_ref=stage.at[q, 1 - c], dst_ref=from_sib.at[q], send_sem=d_send.at[q],
                                 recv_sem=d_recv.at[q], device_id=(x, y, 1 - c), device_id_type=MESH).start()


def _rs2_forward(q, parts_ref, rs):
    stage, chip_sum, d_send, d_recv, i_send, i_recv, local_sem = rs
    x, y, c = _my_pos()
    my_chip = 2 * x + y
    pltpu.make_async_remote_copy(src_ref=stage.at[q, c], dst_ref=chip_sum.at[q], send_sem=d_send.at[q],
                                 recv_sem=d_recv.at[q], device_id=(x, y, 1 - c), device_id_type=MESH).wait_recv()
    chip_sum[q] = (stage[q, c].astype(F32) + chip_sum[q].astype(F32)).astype(BF16)

    @pl.when(q == my_chip)
    def _():
        pltpu.make_async_copy(chip_sum.at[q], parts_ref.at[my_chip], local_sem).start()

    @pl.when(q != my_chip)
    def _():
        pltpu.make_async_remote_copy(src_ref=chip_sum.at[q], dst_ref=parts_ref.at[my_chip], send_sem=i_send.at[q],
                                     recv_sem=i_recv.at[my_chip], device_id=((q >> 1) & 1, q & 1, c),
                                     device_id_type=MESH).start()


def _rs2_finish(parts_ref, rs):
    stage, chip_sum, d_send, d_recv, i_send, i_recv, local_sem = rs
    x, y, c = _my_pos()
    my_chip = 2 * x + y
    for q in range(NCHIP):
        pltpu.make_async_remote_copy(src_ref=stage.at[q, 1 - c], dst_ref=chip_sum.at[q], send_sem=d_send.at[q],
                                     recv_sem=d_recv.at[q], device_id=(x, y, 1 - c), device_id_type=MESH).wait_send()

        @pl.when(q != my_chip)
        def _():
            cp = pltpu.make_async_remote_copy(src_ref=chip_sum.at[q], dst_ref=parts_ref.at[q], send_sem=i_send.at[q],
                                              recv_sem=i_recv.at[q], device_id=((q >> 1) & 1, q & 1, c),
                                              device_id_type=MESH)
            cp.wait_send()
            cp.wait_recv()

        @pl.when(q == my_chip)
        def _():
            pltpu.make_async_copy(chip_sum.at[q], parts_ref.at[q], local_sem).wait()


def _gu_wgrad(h2, dgu, order):
    t_len = h2.shape[0]
    tk = min(TK, t_len)
    nk = t_len // tk

    def body(ord_ref, h_ref, d_ref, parts_ref, acc, *rs):
        p, k = pl.program_id(0), pl.program_id(1)

        @pl.when(k == 0)
        def _():
            acc[...] = jnp.zeros_like(acc)

        hb = h_ref[...]
        for half in range(2):
            acc[half] += _dot_tn(d_ref[0, half], hb)

        @pl.when(k == nk - 1)
        def _():
            q = ord_ref[p]
            rs[0][q] = acc[...].astype(BF16)
            _rs2_to_sibling(q, rs)

        @pl.when((k == min(1, nk - 1)) & (p > 0))
        def _():
            _rs2_forward(ord_ref[p - 1], parts_ref, rs)

        @pl.when((p == NCHIP - 1) & (k == nk - 1))
        def _():
            _rs2_forward(ord_ref[p], parts_ref, rs)
            _rs2_finish(parts_ref, rs)

    return pl.pallas_call(
        body, name="gu_wgrad",
        grid_spec=pltpu.PrefetchScalarGridSpec(
            num_scalar_prefetch=1, grid=(NCHIP, nk),
            in_specs=[pl.BlockSpec((tk, D), lambda p, k, o: (k, 0)),
                      pl.BlockSpec((1, 2, tk, FB), lambda p, k, o: (o[p], 0, k, 0))],
            out_specs=_ANY,
            scratch_shapes=[pltpu.VMEM((2, FB, D), F32)] + _rs2_scratch((FB, D))),
        out_shape=jax.ShapeDtypeStruct((NCHIP, FB, D), BF16),
        compiler_params=_cp("arbitrary", "arbitrary"),
    )(order, h2, dgu.reshape(NCHIP, 2, t_len, FB))


def _scaled_wgrad(name, a, dx, w, gate_row, mod, order):
    t_len = dx.shape[0]
    kb = w.shape[0] // NCHIP
    tk = min(TK, t_len)
    nk = t_len // tk
    rows = kb // 2
    if a.ndim == 3:
        a_spec = pl.BlockSpec((None, tk, kb), lambda p, k, o: (o[p], k, 0))
    else:
        a_spec = pl.BlockSpec((tk, kb), lambda p, k, o: (k, o[p]))

    def body(ord_ref, a_ref, dx_ref, w_ref, mod_ref, parts_ref, dg_ref, acc, *rs):
        p, k = pl.program_id(0), pl.program_id(1)

        @pl.when((p == 0) & (k == 0))
        def _():
            dg_ref[...] = jnp.zeros_like(dg_ref)

        @pl.when(k == 0)
        def _():
            acc[...] = jnp.zeros_like(acc)

        acc[...] += _dot_tn(a_ref[...], dx_ref[...])

        @pl.when(k == nk - 1)
        def _():
            q = ord_ref[p]
            z = acc[...]
            zg = (z * mod_ref[gate_row:gate_row + 1, :]).astype(BF16)
            dg_ref[0:1, :] += jnp.sum(z * w_ref[...].astype(F32), axis=0, keepdims=True)
            for half in range(2):
                rs[0][q, half] = zg[half * rows:(half + 1) * rows]
            _rs2_to_sibling(q, rs)

        @pl.when((k == min(1, nk - 1)) & (p > 0))
        def _():
            _rs2_forward(ord_ref[p - 1], parts_ref, rs)

        @pl.when((p == NCHIP - 1) & (k == nk - 1))
        def _():
            _rs2_forward(ord_ref[p], parts_ref, rs)
            _rs2_finish(parts_ref, rs)

    return pl.pallas_call(
        body, name=name,
        grid_spec=pltpu.PrefetchScalarGridSpec(
            num_scalar_prefetch=1, grid=(NCHIP, nk),
            in_specs=[a_spec,
                      pl.BlockSpec((tk, D), lambda p, k, o: (k, 0)),
                      pl.BlockSpec((kb, D), lambda p, k, o: (o[p], 0)),
                      pl.BlockSpec((8, D), lambda p, k, o: (0, 0))],
            out_specs=[_ANY, pl.BlockSpec((8, D), lambda p, k, o: (0, 0))],
            scratch_shapes=[pltpu.VMEM((kb, D), F32)] + _rs2_scratch((rows, D))),
        out_shape=[jax.ShapeDtypeStruct((NCHIP, rows, D), BF16), jax.ShapeDtypeStruct((8, D), F32)],
        compiler_params=_cp("arbitrary", "arbitrary"),
    )(order, a, dx, w, mod)


M_WA, M_WB, M_CBIAS, M_BA, M_BX, M_LS = 0, 3, 7, 8, 9, 10


def _conv_bwd_rows(tt, proj_ref, prm_ref, due, dye, dp_ref, acc8):
    row = lax.broadcasted_iota(jnp.int32, (8, CG), 0)
    w_b = [prm_ref[P_WB + k:P_WB + k + 1, :] for k in range(4)]
    w_a = [prm_ref[P_WA + k:P_WA + k + 1, :] for k in range(3)]

    def blk(ib, carry):
        r0 = pl.multiple_of(ib * 16, 16)
        rows = pl.ds(r0, 16)
        for g in range(D // CG):
            cs = slice(g * CG, (g + 1) * CG)
            du16, du_after = due[rows, cs], due[pl.ds(r0 + 16, 8), cs]
            dy16, dy_after = dye[rows, cs], dye[pl.ds(r0 + 16, 8), cs]
            cc16 = _pj(proj_ref, 1, rows, cs).astype(F32)
            cx16 = _pj(proj_ref, 2, rows, cs).astype(F32)
            x16 = _pj(proj_ref, 3, rows, cs).astype(F32)
            v16 = cc16 * cx16
            acc = [acc8[8 * k:8 * k + 8, cs] for k in range(8)]
            drx, dv = [], []
            for sb in range(2):
                lo = slice(8 * sb, 8 * sb + 8)
                duc, dyc, xc, vc = du16[lo], dy16[lo], x16[lo], v16[lo]
                du_n = du16[8:16] if sb == 0 else du_after
                dy_n = dy16[8:16] if sb == 0 else dy_after
                acc[0] = acc[0] + duc
                acc[4] = acc[4] + duc * xc
                d8 = w_b[3][:, cs] * duc
                for s in (1, 2, 3):
                    du_s = _shift_up(duc, du_n, s, row)
                    acc[4 - s] = acc[4 - s] + du_s * xc
                    d8 = d8 + w_b[3 - s][:, cs] * du_s
                acc[7] = acc[7] + dyc * vc
                e8 = w_a[2][:, cs] * dyc
                for s in (1, 2):
                    dy_s = _shift_up(dyc, dy_n, s, row)
                    acc[7 - s] = acc[7 - s] + dy_s * vc
                    e8 = e8 + w_a[2 - s][:, cs] * dy_s
                drx.append(d8)
                dv.append(e8)
            for k in range(8):
                acc8[8 * k:8 * k + 8, cs] = acc[k]
            dv16 = jnp.concatenate(dv, axis=0)
            col = lambda s: slice(s * D + g * CG, s * D + (g + 1) * CG)
            dp_ref[rows, col(3)] = jnp.concatenate(drx, axis=0).astype(BF16)
            dp_ref[rows, col(1)] = (dv16 * cx16).astype(BF16)
            dp_ref[rows, col(2)] = (dv16 * cc16).astype(BF16)
        return carry

    lax.fori_loop(0, tt // 16, blk, 0)


def _mixer_bwd(proj, hl, sv, dmg, prm, wa, wx):
    t_len = proj.shape[0]
    tt = min(TT, t_len)
    nt = t_len // tt
    hb8 = tt // 8

    def rev(i):
        return nt - 1 - i

    def halo(i):
        return jnp.maximum(rev(i) * hb8 - 1, 0)

    def body(proj_ref, hl_ref, hh_ref, sv_ref, dmg_ref, prm_ref, wa_ref, wx_ref,
             dp_ref, sums_ref, gwa_ref, gwx_ref,
             he, due, dye, drp_s, dip_s, an, gn, acc8):
        i = pl.program_id(0)
        t = rev(i)

        @pl.when(i == 0)
        def _():
            sums_ref[...] = jnp.zeros_like(sums_ref)
            gwa_ref[...] = jnp.zeros_like(gwa_ref)
            gwx_ref[...] = jnp.zeros_like(gwx_ref)
            due[tt:tt + 8, :] = jnp.zeros((8, D), F32)
            dye[tt:tt + 8, :] = jnp.zeros((8, D), F32)
            an[...] = jnp.zeros((8, D), F32)
            gn[...] = jnp.zeros((8, D), F32)

        live = (t > 0).astype(F32)
        he[0:8, :] = hh_ref[...].astype(F32) * live
        he[8:8 + tt, :] = hl_ref[...].astype(F32)

        ls_all = _log_sigmoid(prm_ref[P_LAM:P_LAM + 1, :])
        row = lax.broadcasted_iota(jnp.int32, (8, CG), 0)
        nblk = tt // 16

        def blk(ib, carry):
            r0 = pl.multiple_of((nblk - 1 - ib) * 16, 16)
            rows = pl.ds(r0, 16)
            for g in range(D // CG):
                cs = slice(g * CG, (g + 1) * CG)
                ls = ls_all[:, cs]
                dm = dmg_ref[rows, cs].astype(F32)
                cb = _pj(proj_ref, 0, rows, cs).astype(F32)
                rg = _pj(proj_ref, 4, rows, cs).astype(F32)
                sga = _sig(_pj(proj_ref, 5, rows, cs).astype(F32))
                sgb = _sig(_pj(proj_ref, 6, rows, cs).astype(F32))
                ya0 = sv_ref[SV_YA, rows, cs]
                h16 = he[pl.ds(r0 + 8, 16), cs]
                gl, th = _gelu(rg)
                dgl = 0.5 * (1.0 + th) + 0.5 * rg * (1.0 - th * th) * (_GC * (1.0 + 3.0 * 0.044715 * rg * rg))
                y_a = cb * ya0
                y_b = h16 * gl
                dy_a = dm * sga
                dy_b = dm * sgb
                col = lambda s: slice(s * D + g * CG, s * D + (g + 1) * CG)
                dp_ref[rows, col(5)] = (dm * y_a * sga * (1.0 - sga)).astype(BF16)
                dp_ref[rows, col(6)] = (dm * y_b * sgb * (1.0 - sgb)).astype(BF16)
                dp_ref[rows, col(4)] = (dy_b * h16 * dgl).astype(BF16)
                dp_ref[rows, col(0)] = (dy_a * ya0).astype(BF16)
                dye[rows, cs] = dy_a * cb
                dh16 = dy_b * gl

                a_next = an[:, cs]
                g_next = gn[:, cs]
                s_ba = jnp.zeros((8, CG), F32)
                s_bx = jnp.zeros((8, CG), F32)
                s_ls = jnp.zeros((8, CG), F32)
                for sb in (1, 0):
                    rr = r0 + 8 * sb
                    first = (row + (t * tt + rr)) == 0
                    r8 = pl.ds(rr, 8)
                    uu, r, ig, a, mult = (sv_ref[pln, r8, cs] for pln in (SV_U, SV_R, SV_I, SV_A, SV_MULT))
                    ca = jnp.where(row < 7, pltpu.roll(a, 7, 0), a_next)
                    cb_ = dh16[8 * sb:8 * sb + 8, :]
                    for s in (1, 2, 4):
                        a_sh = jnp.where(row < 8 - s, pltpu.roll(ca, 8 - s, 0), 1.0)
                        b_sh = jnp.where(row < 8 - s, pltpu.roll(cb_, 8 - s, 0), 0.0)
                        cb_ = ca * b_sh + cb_
                        ca = ca * a_sh
                    gv = ca * g_next + cb_
                    g_next = jnp.broadcast_to(gv[0:1, :], gv.shape)
                    a_next = jnp.broadcast_to(a[0:1, :], a.shape)
                    hprev = jnp.where(row >= 1, pltpu.roll(he[pl.ds(rr + 8, 8), cs], 1, 0),
                                      pltpu.roll(he[pl.ds(rr, 8), cs], 1, 0))
                    da = gv * hprev
                    dmult = jnp.where(first, 0.0, gv * ig * uu)
                    dla = da * a + jnp.where(mult > 0.0, dmult * (-(a * a) / mult), 0.0)
                    drp = dla * (LRU_C * ls) * r * (1.0 - r)
                    dip = gv * mult * uu * ig * (1.0 - ig)
                    s_ls = s_ls + dla * (LRU_C * r)
                    s_ba = s_ba + drp
                    s_bx = s_bx + dip
                    drp_s[pl.ds(rr, 8), cs] = drp
                    dip_s[pl.ds(rr, 8), cs] = dip
                    due[pl.ds(rr, 8), cs] = gv * mult * ig
                an[:, cs] = a_next
                gn[:, cs] = g_next
                sums_ref[M_BA:M_BA + 1, cs] += jnp.sum(s_ba, axis=0, keepdims=True)
                sums_ref[M_BX:M_BX + 1, cs] += jnp.sum(s_bx, axis=0, keepdims=True)
                sums_ref[M_LS:M_LS + 1, cs] += jnp.sum(s_ls, axis=0, keepdims=True)
            return carry

        lax.fori_loop(0, nblk, blk, 0)

        drp_b = drp_s[...].astype(BF16)
        dip_b = dip_s[...].astype(BF16)
        ub = sv_ref[SV_U].astype(BF16)
        for h in range(HEADS):
            cs = slice(h * HB, (h + 1) * HB)
            due[0:tt, cs] += _dot_nt(drp_b[:, cs], wa_ref[h]) + _dot_nt(dip_b[:, cs], wx_ref[h])
            gwa_ref[h] += _dot_tn(ub[:, cs], drp_b[:, cs])
            gwx_ref[h] += _dot_tn(ub[:, cs], dip_b[:, cs])

        acc8[...] = jnp.zeros_like(acc8)
        _conv_bwd_rows(tt, proj_ref, prm_ref, due, dye, dp_ref, acc8)
        for k, dst in enumerate([M_CBIAS] + [M_WB + k for k in range(4)] + [M_WA + k for k in range(3)]):
            sums_ref[dst:dst + 1, :] += jnp.sum(acc8[8 * k:8 * k + 8, :], axis=0, keepdims=True)
        due[tt:tt + 8, :] = due[0:8, :]
        dye[tt:tt + 8, :] = dye[0:8, :]

        @pl.when(i == nt - 1)
        def _():
            sums_ref[M_LS:M_LS + 1, :] = sums_ref[M_LS:M_LS + 1, :] * _sig(-prm_ref[P_LAM:P_LAM + 1, :])

    big = lambda: pltpu.VMEM((tt + 8, D), F32)
    tile = lambda: pltpu.VMEM((tt, D), F32)
    return pl.pallas_call(
        body, name="mixer_bwd", grid=(nt,),
        in_specs=[pl.BlockSpec((tt, 7 * D), lambda i: (rev(i), 0)),
                  pl.BlockSpec((tt, D), lambda i: (rev(i), 0)),
                  pl.BlockSpec((8, D), lambda i: (halo(i), 0)),
                  pl.BlockSpec((len(SV_PLANES), tt, D), lambda i: (0, rev(i), 0)),
                  pl.BlockSpec((tt, D), lambda i: (rev(i), 0)),
                  pl.BlockSpec((16, D), lambda i: (0, 0)),
                  pl.BlockSpec((HEADS, HB, HB), lambda i: (0, 0, 0)),
                  pl.BlockSpec((HEADS, HB, HB), lambda i: (0, 0, 0))],
        out_specs=[pl.BlockSpec((tt, 7 * D), lambda i: (rev(i), 0)),
                   pl.BlockSpec((16, D), lambda i: (0, 0)),
                   pl.BlockSpec((HEADS, HB, HB), lambda i: (0, 0, 0)),
                   pl.BlockSpec((HEADS, HB, HB), lambda i: (0, 0, 0))],
        out_shape=[jax.ShapeDtypeStruct((t_len, 7 * D), BF16), jax.ShapeDtypeStruct((16, D), F32),
                   jax.ShapeDtypeStruct((HEADS, HB, HB), F32), jax.ShapeDtypeStruct((HEADS, HB, HB), F32)],
        scratch_shapes=[big(), big(), big(), tile(), tile(),
                        pltpu.VMEM((8, D), F32), pltpu.VMEM((8, D), F32), pltpu.VMEM((64, D), F32)],
        compiler_params=_cp("arbitrary"),
    )(proj, hl, hl, sv, dmg, prm, wa, wx)


def _in_proj_bwd(dproj, w_in, x, dx1, mod, g_mix):
    t_len = x.shape[0]
    tm = min(TM, t_len)

    def body(dp_ref, w_ref, x_ref, dx1_ref, mod_ref, g_ref, gx_ref, sums_ref, acc):
        @pl.when(pl.program_id(0) == 0)
        def _():
            sums_ref[...] = jnp.zeros_like(sums_ref)

        def write(rows, dx):
            gx_ref[rows, :] = dx

        zero = jnp.zeros((8, D), F32)
        sums = (zero, zero, zero)
        for sub in _sub_blocks(tm):
            acc[sub, :] = _dot_nt(dp_ref[sub, :], w_ref[...])
            sums = _norm_bwd_rows(sub, sums, acc, x_ref, dx1_ref, mod_ref[1:2, :], g_ref[...], write)
        _add_norm_sums(sums_ref, sums)

    return pl.pallas_call(
        body, name="in_proj_bwd", grid=(t_len // tm,),
        in_specs=[pl.BlockSpec((tm, 7 * D), lambda i: (i, 0)),
                  _resident((D, 7 * D)),
                  pl.BlockSpec((tm, D), lambda i: (i, 0)), pl.BlockSpec((tm, D), lambda i: (i, 0)),
                  pl.BlockSpec((8, D), lambda i: (0, 0)), pl.BlockSpec((1, D), lambda i: (0, 0))],
        out_specs=[pl.BlockSpec((tm, D), lambda i: (i, 0)), pl.BlockSpec((8, D), lambda i: (0, 0))],
        out_shape=[jax.ShapeDtypeStruct((t_len, D), F32), jax.ShapeDtypeStruct((8, D), F32)],
        scratch_shapes=[pltpu.VMEM((tm, D), F32)],
        compiler_params=_cp("arbitrary"),
    )(dproj, w_in, x, dx1, mod, g_mix)


def _in_wgrad(h, dproj, g_wa, g_wx, order):
    t_len = h.shape[0]
    tk = min(TKI, t_len)
    nk = t_len // tk
    cw = 7 * D // NDEV
    hr = HB // NDEV

    def body(ord_ref, h_ref, d_ref, ga_ref, gx_ref, parts_ref, pa_ref, px_ref, acc, *scr):
        rs, sems = scr[:-6], scr[-6:]
        p, k = pl.program_id(0), pl.program_id(1)

        def head_rows(ref):
            return lambda s: ref.at[:, pl.ds(s * hr, hr), :]

        @pl.when((p == 0) & (k == 0))
        def _():
            for s in range(NDEV):
                _rs_send(head_rows(ga_ref)(s), pa_ref, s, *sems[0:3])
                _rs_send(head_rows(gx_ref)(s), px_ref, s, *sems[3:6])

        @pl.when(k == 0)
        def _():
            acc[...] = jnp.zeros_like(acc)

        acc[...] += _dot_tn(h_ref[...], d_ref[...])

        @pl.when(k == nk - 1)
        def _():
            q = ord_ref[p]
            for half in range(2):
                rs[0][q, half] = acc[:, half * cw:(half + 1) * cw].astype(BF16)
            _rs2_to_sibling(q, rs)

        @pl.when((k == min(1, nk - 1)) & (p > 0))
        def _():
            _rs2_forward(ord_ref[p - 1], parts_ref, rs)

        @pl.when((p == NCHIP - 1) & (k == nk - 1))
        def _():
            _rs2_forward(ord_ref[p], parts_ref, rs)
            _rs2_finish(parts_ref, rs)
            _rs_finish(head_rows(ga_ref), pa_ref, *sems[0:3])
            _rs_finish(head_rows(gx_ref), px_ref, *sems[3:6])

    return pl.pallas_call(
        body, name="in_wgrad",
        grid_spec=pltpu.PrefetchScalarGridSpec(
            num_scalar_prefetch=1, grid=(NCHIP, nk),
            in_specs=[pl.BlockSpec((tk, D), lambda p, k, o: (k, 0)),
                      pl.BlockSpec((tk, 2 * cw), lambda p, k, o: (k, o[p])), _ANY, _ANY],
            out_specs=[_ANY, _ANY, _ANY],
            scratch_shapes=[pltpu.VMEM((D, 2 * cw), F32)] + _rs2_scratch((D, cw)) + _RS_SEMS * 2),
        out_shape=[jax.ShapeDtypeStruct((NCHIP, D, cw), BF16), jax.ShapeDtypeStruct((NDEV, HEADS, hr, HB), F32),
                   jax.ShapeDtypeStruct((NDEV, HEADS, hr, HB), F32)],
        compiler_params=_cp("arbitrary", "arbitrary"),
    )(order, h, dproj, g_wa, g_wx)


def _adam_math(w, g, m, v):
    m = ADAM_B1 * m + (1.0 - ADAM_B1) * g
    v = ADAM_B2 * v + (1.0 - ADAM_B2) * (g * g)
    m_hat = m / (1.0 - ADAM_B1 ** ADAM_STEP)
    v_hat = v / (1.0 - ADAM_B2 ** ADAM_STEP)
    delta = -ADAM_LR * (m_hat / (jnp.sqrt(v_hat) + ADAM_EPS) + ADAM_WD * w)
    return delta, m, v


def _ada_bwd(c_all, dmod_cols, w, m, v):
    rb = 256
    n = w.shape[1]
    nrow = c_all.shape[0]

    def body(c_ref, d_ref, w_ref, m_ref, v_ref, g_ref, dl_ref, nm_ref, nv_ref):
        cv = c_ref[...]
        g = _dot_tn((cv * _sig(cv)).astype(BF16), d_ref[...].astype(BF16))
        g_ref[...] = g
        dl_ref[...], nm_ref[...], nv_ref[...] = _adam_math(w_ref[...], g, m_ref[...], v_ref[...])

    blk = pl.BlockSpec((rb, n), lambda i: (i, 0))
    sds = jax.ShapeDtypeStruct(w.shape, F32)
    return pl.pallas_call(
        body, name="ada_bwd", grid=(D // rb,),
        in_specs=[pl.BlockSpec((nrow, rb), lambda i: (0, i)), pl.BlockSpec((nrow, n), lambda i: (0, 0)), blk, blk, blk],
        out_specs=[blk, blk, blk, blk], out_shape=[sds, sds, sds, sds],
        compiler_params=_cp("parallel"),
    )(c_all, dmod_cols, w, m, v)


def _adam(name, parts, w, m, v):
    p, r, c = parts.shape
    rb = max([cand for cand in range(8, min(r, 256) + 1, 8) if r % cand == 0], default=r)

    def body(p_ref, w_ref, m_ref, v_ref, g_ref, dl_ref, nm_ref, nv_ref):
        g = p_ref[0].astype(F32)
        for q in range(1, p):
            g = g + p_ref[q].astype(F32)
        g_ref[...] = g
        dl_ref[...], nm_ref[...], nv_ref[...] = _adam_math(w_ref[...], g, m_ref[...], v_ref[...])

    blk = pl.BlockSpec((rb, c), lambda i: (i, 0))
    sds = jax.ShapeDtypeStruct((r, c), F32)
    return pl.pallas_call(
        body, name=name, grid=(r // rb,),
        in_specs=[pl.BlockSpec((p, rb, c), lambda i: (0, i, 0)), blk, blk, blk],
        out_specs=[blk, blk, blk, blk], out_shape=[sds, sds, sds, sds],
        compiler_params=_cp("parallel"),
    )(parts, w, m, v)


_SMALL_SEMS = [pltpu.SemaphoreType.DMA((7,)), pltpu.SemaphoreType.DMA((7,)), pltpu.SemaphoreType.DMA]
_VMEM = pl.BlockSpec(memory_space=pltpu.VMEM)


def _exchange_small(x_ref, out_ref, send_sems, recv_sems, local_sem):
    m_per = x_ref.shape[0]
    x, y, c = _my_pos()
    me, sibling = (x, y, c), (x, y, 1 - c)
    chips = [(1 - x, y), (x, 1 - y), (1 - x, 1 - y)]

    def rows(px, py, pc):
        return out_ref.at[pl.ds((4 * px + 2 * py + pc) * m_per, m_per), :]

    def copy(k, block, to, src=None):
        return pltpu.make_async_remote_copy(
            src_ref=rows(*block) if src is None else src, dst_ref=rows(*block),
            send_sem=send_sems.at[k], recv_sem=recv_sems.at[k], device_id=to, device_id_type=MESH)

    mine = pltpu.make_async_copy(x_ref, rows(*me), local_sem)
    mine.start()
    first = [copy(0, me, sibling, src=x_ref)]
    first += [copy(1 + j, me, (*chip, c), src=x_ref) for j, chip in enumerate(chips)]
    for cp in first:
        cp.start()
    passed = [copy(4 + j, (*chip, c), sibling) for j, chip in enumerate(chips)]
    for j, chip in enumerate(chips):
        copy(1 + j, (*chip, c), me).wait_recv()
        passed[j].start()
    copy(0, sibling, me).wait_recv()
    for j, chip in enumerate(chips):
        copy(4 + j, (*chip, 1 - c), me).wait_recv()
    for cp in first + passed:
        cp.wait_send()
    mine.wait()


def _gather_small_grads(sums1, sums2, msums, d_gt1, d_gt2, d_gfin, loss):
    def body(s1, s2, ms, g1, g2, gf, ls, out_ref, pack, *sems):
        rows = [s1[S_SH:S_SH + 1, :], s1[S_SC:S_SC + 1, :], g1[0:1, :],
                s2[S_SH:S_SH + 1, :], s2[S_SC:S_SC + 1, :], g2[0:1, :],
                s1[S_G:S_G + 1, :], ms[M_CBIAS:M_CBIAS + 1, :], ms[M_BA:M_BA + 1, :], ms[M_BX:M_BX + 1, :],
                ms[M_LS:M_LS + 1, :], s2[S_G:S_G + 1, :], gf[...]]
        rows += [ms[M_WA + k:M_WA + k + 1, :] for k in range(3)] + [ms[M_WB + k:M_WB + k + 1, :] for k in range(4)]
        rows += [jnp.broadcast_to(ls[0:1, 0:1], (1, D))]
        for i, v in enumerate(rows):
            pack[i:i + 1, :] = v
        pack[len(rows):24, :] = jnp.zeros((24 - len(rows), D), F32)
        _exchange_small(pack, out_ref, *sems)

    return pl.pallas_call(
        body, name="gather_small", out_shape=jax.ShapeDtypeStruct((NDEV * 24, D), F32),
        in_specs=[_VMEM] * 7, out_specs=_VMEM, scratch_shapes=[pltpu.VMEM((24, D), F32)] + _SMALL_SEMS,
    )(sums1, sums2, msums, d_gt1, d_gt2, d_gfin, loss)


def _ada_mod(pack, w_ada, b_cols):
    ncol = w_ada.shape[1]

    def body(p_ref, w_ref, b_ref, all_ref, mod_ref, cols, *sems):
        _exchange_small(p_ref, all_ref, *sems[0:3])
        c_all = jnp.concatenate([all_ref[8 * d:8 * d + 1, 0:D] for d in range(NDEV)], axis=0)
        c16 = jnp.concatenate([c_all, jnp.zeros_like(c_all)], axis=0)
        mod16 = _dot((c16 * _sig(c16)).astype(BF16), w_ref[...].astype(BF16)) + b_ref[...]
        cols[...] = mod16[0:NDEV]
        _exchange_small(cols, mod_ref, *sems[3:6])

    return pl.pallas_call(
        body, name="ada_mod",
        out_shape=[jax.ShapeDtypeStruct((NDEV * 8, pack.shape[1]), F32), jax.ShapeDtypeStruct((NDEV * 8, ncol), F32)],
        in_specs=[_VMEM, _VMEM, _VMEM], out_specs=[_VMEM, _VMEM],
        scratch_shapes=[pltpu.VMEM((NDEV, ncol), F32)] + _SMALL_SEMS * 2,
        compiler_params=_cp(),
    )(pack, w_ada, b_cols)


def _blk_rows(n):
    return lambda ref, b: ref.at[pl.ds(pl.multiple_of(b * n, 8), n), :]


def _blk_lead(ref, b):
    return ref.at[b]


def _blk_heads(ref, b):
    return ref.at[:, pl.ds(pl.multiple_of(b * (HB // NDEV), 8), HB // NDEV), :]


def _ag_phases(ins, outs, slicers, send_sems, recv_sems, local_sems):
    na = len(ins)
    x, y, c = _my_pos()
    me, sibling = (x, y, c), (x, y, 1 - c)
    chips = [(1 - x, y), (x, 1 - y), (1 - x, 1 - y)]

    def copy(a, k, block, to, from_shard=False):
        px, py, pc = block
        dst = slicers[a](outs[a], 4 * px + 2 * py + pc)
        return pltpu.make_async_remote_copy(
            src_ref=ins[a] if from_shard else dst, dst_ref=dst,
            send_sem=send_sems.at[a * 7 + k], recv_sem=recv_sems.at[a * 7 + k], device_id=to, device_id_type=MESH)

    def local(a):
        return pltpu.make_async_copy(ins[a], slicers[a](outs[a], 4 * x + 2 * y + c), local_sems.at[a])

    def firsts(a):
        return [copy(a, 0, me, sibling, True)] + [copy(a, 1 + j, me, (*chip, c), True) for j, chip in enumerate(chips)]

    def start():
        for a in range(na):
            local(a).start()
            for cp in firsts(a):
                cp.start()

    def forward():
        for a in range(na):
            for j, chip in enumerate(chips):
                copy(a, 1 + j, (*chip, c), me).wait_recv()
                copy(a, 4 + j, (*chip, c), sibling).start()

    def finish():
        for a in range(na):
            copy(a, 0, sibling, me).wait_recv()
            for j, chip in enumerate(chips):
                copy(a, 4 + j, (*chip, 1 - c), me).wait_recv()
        for a in range(na):
            for cp in firsts(a) + [copy(a, 4 + j, (*chip, c), sibling) for j, chip in enumerate(chips)]:
                cp.wait_send()
            local(a).wait()

    return start, forward, finish


def _ag_sems(na):
    return [pltpu.SemaphoreType.DMA((7 * na,)), pltpu.SemaphoreType.DMA((7 * na,)), pltpu.SemaphoreType.DMA((na,))]


def _local_step(x, target, mod, g_mix, g_ffn, g_fin, prm, w_in_shard, shards):
    fulls = [(HEADS, HB, HB), (HEADS, HB, HB), (D, D), (NDEV, FB, D), (DFF, D)]
    slicers = [_blk_heads, _blk_heads, _blk_rows(D // NDEV), _blk_lead, _blk_rows(DFF // NDEV)]
    my_chip = _my_index() >> 1
    own_first = (my_chip ^ jnp.arange(NCHIP, dtype=jnp.int32)).astype(jnp.int32)
    early, late = [0, 1, 2, 4], [3]
    pick = lambda lst, idx: [lst[i] for i in idx]
    proj, h, w_in, (wa, wx, w_out, w_down) = _in_proj(x, mod, g_mix, w_in_shard, own_first, pick(shards, early),
                                                      pick(fulls, early), pick(slicers, early))
    merged, hl, sv, (w_gu,) = _mixer_fwd(proj, prm, wa, wx, pick(shards, late), pick(fulls, late),
                                         pick(slicers, late))
    w_gu = w_gu.reshape(2, 4, FB, D)
    x1, h2 = _out_proj(merged, x, mod, g_ffn, w_out)
    gu, dx2, dx2b, loss, d_gfin = _ffn_fwd(h2, x1, target, mod, g_fin, w_gu, w_down)
    dgu, act, dx1, dx1b, dmg, sums2 = _ffn_bwd(dx2, gu, x1, mod, g_ffn, w_gu, w_down, w_out)
    chip_order = _xor_order(_my_index() >> 1, NCHIP)
    p_wgu = _gu_wgrad(h2, dgu, chip_order)
    p_wdown, d_gt2 = _scaled_wgrad("down_wgrad", act, dx2b, w_down, 5, mod, chip_order)
    p_wout, d_gt1 = _scaled_wgrad("out_wgrad", merged, dx1b, w_out, 2, mod, chip_order)
    dproj, msums, g_wa, g_wx = _mixer_bwd(proj, hl, sv, dmg, prm, wa, wx)
    p_win, p_wa, p_wx = _in_wgrad(h, dproj, g_wa, g_wx, chip_order)
    grad_x, sums1 = _in_proj_bwd(dproj, w_in, x, dx1, mod, g_mix)
    return dict(loss=loss, grad_x=grad_x, d_gfin=d_gfin, sums1=sums1, sums2=sums2, msums=msums,
                d_gt1=d_gt1, d_gt2=d_gt2, p_win=p_win, p_wa=p_wa, p_wx=p_wx, p_wout=p_wout, p_wgu=p_wgu,
                p_wdown=p_wdown)


def kernel(x, c, w_ada, b_ada, g_norm_mix, w_in, conv_a_w, conv_b_w, conv_b_bias, w_rg_a, b_rg_a, w_rg_x, b_rg_x, lru_lambda, w_out, g_norm_ffn, w_gate_up, w_down, g_norm_final, loss_target, m_w_ada, m_b_ada, m_g_norm_mix, m_w_in, m_conv_a_w, m_conv_b_w, m_conv_b_bias, m_w_rg_a, m_b_rg_a, m_w_rg_x, m_b_rg_x, m_lru_lambda, m_w_out, m_g_norm_ffn, m_w_gate_up, m_w_down, m_g_norm_final, v_w_ada, v_b_ada, v_g_norm_mix, v_w_in, v_conv_a_w, v_conv_b_w, v_conv_b_bias, v_w_rg_a, v_b_rg_a, v_w_rg_x, v_b_rg_x, v_lru_lambda, v_w_out, v_g_norm_ffn, v_w_gate_up, v_w_down, v_g_norm_final):
    me = 4 * lax.axis_index("x") + 2 * lax.axis_index("y") + lax.axis_index("c")
    ncol = w_ada.shape[2]
    cw = conv_a_w.shape[2]

    pack0 = jnp.concatenate([c, conv_a_w.reshape(1, 3 * cw), conv_b_w.reshape(1, 4 * cw)], axis=1)
    b_cols = lax.dynamic_slice_in_dim(b_ada, me * ncol, ncol, axis=1)
    got0, got1 = _ada_mod(jnp.broadcast_to(pack0, (8, pack0.shape[1])), w_ada[0], b_cols)
    got0 = got0.reshape(NDEV, 8, -1)[:, 0, :]
    c_all = got0[:, :D]
    conv_a = got0[:, D:D + 3 * cw].reshape(NDEV, 3, cw).transpose(1, 0, 2).reshape(3, D)
    conv_b = got0[:, D + 3 * cw:].reshape(NDEV, 4, cw).transpose(1, 0, 2).reshape(4, D)
    c16 = jnp.concatenate([c_all, jnp.zeros((8, D), F32)], axis=0)
    mod6 = lax.dynamic_index_in_dim(got1.reshape(NDEV, NDEV, ncol), me, axis=1, keepdims=False).reshape(6, D)
    mod = jnp.concatenate([mod6, jnp.zeros((2, D), F32)], axis=0)

    tr = lambda a: jnp.swapaxes(a, 1, 2)
    shards = [w_rg_a[0].astype(BF16), w_rg_x[0].astype(BF16), w_out[0].astype(BF16), tr(w_gate_up)[0].astype(BF16),
              w_down[0].astype(BF16)]

    prm = jnp.concatenate([conv_a, conv_b, conv_b_bias, b_rg_a, b_rg_x, lru_lambda, jnp.zeros((5, D), F32)], axis=0)
    r = _local_step(x[0], loss_target[0], mod, g_norm_mix, g_norm_ffn, g_norm_final.reshape(1, D), prm,
                    w_in[0].astype(BF16), shards)

    parts = [r["p_win"], r["p_wa"], r["p_wx"], r["p_wout"], r["p_wgu"], r["p_wdown"]]
    big = {}
    for nm, p, w, m, v in (("w_in", parts[0], w_in, m_w_in, v_w_in), ("w_rg_a", parts[1], w_rg_a, m_w_rg_a, v_w_rg_a),
                           ("w_rg_x", parts[2], w_rg_x, m_w_rg_x, v_w_rg_x), ("w_out", parts[3], w_out, m_w_out, v_w_out),
                           ("w_gate_up", parts[4], tr(w_gate_up), tr(m_w_gate_up), tr(v_w_gate_up)),
                           ("w_down", parts[5], w_down, m_w_down, v_w_down)):
        two_d = (-1, w.shape[-1])
        outs = _adam("adam_" + nm, p.reshape((p.shape[0],) + w.reshape(two_d).shape), w.reshape(two_d), m.reshape(two_d),
                     v.reshape(two_d))
        big[nm] = [o.reshape(w.shape) for o in outs]
    big["w_gate_up"] = [tr(o) for o in big["w_gate_up"]]

    got2 = _gather_small_grads(r["sums1"], r["sums2"], r["msums"], r["d_gt1"], r["d_gt2"], r["d_gfin"],
                               r["loss"]).reshape(NDEV, 24, D)

    rep_w = jnp.concatenate([b_ada.reshape(6, D), g_norm_mix, conv_b_bias, b_rg_a, b_rg_x, lru_lambda, g_norm_ffn,
                             g_norm_final.reshape(1, D), jnp.zeros((3, D), F32)], axis=0)
    rep_m = jnp.concatenate([m_b_ada.reshape(6, D), m_g_norm_mix, m_conv_b_bias, m_b_rg_a, m_b_rg_x, m_lru_lambda,
                             m_g_norm_ffn, m_g_norm_final.reshape(1, D), jnp.zeros((3, D), F32)], axis=0)
    rep_v = jnp.concatenate([v_b_ada.reshape(6, D), v_g_norm_mix, v_conv_b_bias, v_b_rg_a, v_b_rg_x, v_lru_lambda,
                             v_g_norm_ffn, v_g_norm_final.reshape(1, D), jnp.ones((3, D), F32)], axis=0)
    rep = _adam("adam_rep", got2[:, :16, :], rep_w, rep_m, rep_v)

    conv_parts = lax.dynamic_slice_in_dim(got2[:, 13:21, :], me * cw, cw, axis=2)
    cv_w = jnp.concatenate([conv_a_w[0], conv_b_w[0], jnp.zeros((1, cw), F32)], axis=0)
    cv_m = jnp.concatenate([m_conv_a_w[0], m_conv_b_w[0], jnp.zeros((1, cw), F32)], axis=0)
    cv_v = jnp.concatenate([v_conv_a_w[0], v_conv_b_w[0], jnp.ones((1, cw), F32)], axis=0)
    cvo = _adam("adam_conv", conv_parts, cv_w, cv_m, cv_v)

    dmod_cols = lax.dynamic_slice_in_dim(got2[:, :6, :].reshape(NDEV, 6 * D), me * ncol, ncol, axis=1)
    dmod16 = jnp.concatenate([dmod_cols, jnp.zeros((8, ncol), F32)], axis=0)
    ada = _ada_bwd(c16, dmod16, w_ada[0], m_w_ada[0], v_w_ada[0])

    loss = jnp.sum(got2[:, 20, 0])

    def pick(q):
        one = lambda i: rep[q][i:i + 1]
        return [ada[q].reshape(w_ada.shape), rep[q][0:6].reshape(b_ada.shape), one(6), big["w_in"][q],
                cvo[q][0:3].reshape(conv_a_w.shape), cvo[q][3:7].reshape(conv_b_w.shape), one(7),
                big["w_rg_a"][q], one(8), big["w_rg_x"][q], one(9), one(10), big["w_out"][q], one(11),
                big["w_gate_up"][q], big["w_down"][q], rep[q][12]]

    return (loss, r["grad_x"].reshape(x.shape), *pick(0), *pick(1), *pick(2), *pick(3))
```
